```python
import math
import jax, jax.numpy as jnp
from jax import lax
import numpy as np

D_MODEL = 2048
BATCH = 8
SEQ = 8192
DEPTH = 1

N_META = 16
CONV_K = 4
GDN_HEADS = 8
GDN_DK = 128
GDN_DV = 128
GDN_CHUNK = 64
GLA_HEADS = 4
GLA_DK = 128
GLA_DV = 256
GLA_CHUNK = 16
GLA_GATE_RANK = 16
GLA_GATE_NORMALIZER = 16.0
GDN_QK = GDN_HEADS * GDN_DK
GDN_V = GDN_HEADS * GDN_DV
GLA_QK = GLA_HEADS * GLA_DK
GLA_V = GLA_HEADS * GLA_DV
MIX_WIDTH = GDN_V + GLA_V
D_FF = -(-8 * D_MODEL // (3 * 256)) * 256
IN_SPLITS = (2 * GDN_QK + GDN_V, GDN_V, GDN_HEADS, GDN_HEADS, GLA_QK, GLA_QK, GLA_V, GLA_V, GLA_GATE_RANK)
D_IN = sum(IN_SPLITS)
IN_OFFSETS = tuple(int(i) for i in np.cumsum(IN_SPLITS)[:-1])
NORM_EPS = 1e-6

kernel_name = "hybrid_gdn_gla_meta_block"


def rms_norm(x, w):
    xf = x.astype(jnp.float32)
    y = xf * lax.rsqrt(jnp.mean(xf * xf, axis=-1, keepdims=True) + NORM_EPS)
    return (y * w.astype(jnp.float32)).astype(x.dtype)


def l2_normalize(x):
    xf = x.astype(jnp.float32)
    return (xf * lax.rsqrt(jnp.sum(xf * xf, axis=-1, keepdims=True) + NORM_EPS)).astype(x.dtype)


def causal_short_conv(x, w):
    L = x.shape[1]
    xp = jnp.pad(x, ((0, 0), (CONV_K - 1, 0), (0, 0)))
    y = xp[:, 0:L] * w[0]
    for i in range(1, CONV_K):
        y = y + xp[:, i:i + L] * w[i]
    return jax.nn.silu(y)


def to_chunks(t, chunk, pad):
    t = jnp.pad(t, ((0, 0), (pad, 0), (0, 0), (0, 0)))
    b, lp, h, d = t.shape
    return t.reshape(b, lp // chunk, chunk, h, d).transpose(0, 3, 1, 2, 4)


def from_chunks(o, pad):
    b, h, n, c, d = o.shape
    return o.transpose(0, 2, 3, 1, 4).reshape(b, n * c, h, d)[:, pad:]


def gated_delta_rule(q, k, v, beta, g):
    out_dtype = v.dtype
    f32 = jnp.float32
    C = GDN_CHUNK
    pad = (-N_META) % C
    q, k, v = (to_chunks(t.astype(f32), C, pad) for t in (q, k, v))
    beta, g = (to_chunks(t.astype(f32)[..., None], C, pad)[..., 0] for t in (beta, g))
    gc = jnp.cumsum(g, axis=-1)
    causal = jnp.tril(jnp.ones((C, C), bool))
    strict = jnp.tril(jnp.ones((C, C), bool), -1)
    decay = jnp.exp(jnp.where(causal, gc[..., :, None] - gc[..., None, :], -jnp.inf))
    kb = k * beta[..., None]
    a_low = jnp.where(strict, jnp.einsum('bhncd,bhnsd->bhncs', kb, k) * decay, 0.0)
    t_mat = a_low + jnp.eye(C, dtype=f32)
    u = lax.linalg.triangular_solve(t_mat, v * beta[..., None], left_side=True, lower=True, unit_diagonal=True)
    w = lax.linalg.triangular_solve(t_mat, kb * jnp.exp(gc)[..., None], left_side=True, lower=True, unit_diagonal=True)
    qk = jnp.einsum('bhncd,bhnsd->bhncs', q, k) * decay
    q_dec = q * jnp.exp(gc)[..., None]
    k_dec = k * jnp.exp(gc[..., -1:] - gc)[..., None]
    g_last = jnp.exp(gc[..., -1])

    def step(S, inp):
        qd, kd, u_c, w_c, qk_c, gl = inp
        v_new = u_c - jnp.einsum('bhcd,bhde->bhce', w_c, S)
        o = jnp.einsum('bhcd,bhde->bhce', qd, S) + jnp.einsum('bhcs,bhse->bhce', qk_c, v_new)
        S = S * gl[..., None, None] + jnp.einsum('bhcd,bhce->bhde', kd, v_new)
        return S, o

    xs = tuple(jnp.moveaxis(t, 2, 0) for t in (q_dec, k_dec, u, w, qk, g_last))
    b, h = q.shape[0], q.shape[1]
    S0 = jnp.zeros((b, h, GDN_DK, GDN_DV), f32)
    _, o = lax.scan(step, S0, xs)
    return from_chunks(jnp.moveaxis(o, 0, 2), pad).astype(out_dtype)


def gla_chunked(q, k, v, log_a):
    out_dtype = v.dtype
    f32 = jnp.float32
    C = GLA_CHUNK
    pad = (-N_META) % C
    q, k, v, log_a = (to_chunks(t.astype(f32), C, pad) for t in (q, k, v, log_a))
    bcum = jnp.cumsum(log_a, axis=-2)
    causal = jnp.tril(jnp.ones((C, C), bool))

    def step(S, inp):
        q_c, k_c, v_c, b_c = inp
        diff = jnp.where(causal[..., None], b_c[..., :, None, :] - b_c[..., None, :, :], -jnp.inf)
        scores = jnp.einsum('bhid,bhjd,bhijd->bhij', q_c, k_c, jnp.exp(diff))
        o = jnp.einsum('bhid,bhde->bhie', q_c * jnp.exp(b_c), S) + jnp.einsum('bhij,bhje->bhie', scores, v_c)
        b_last = b_c[..., -1, :]
        S = S * jnp.exp(b_last)[..., None] + jnp.einsum(
            'bhjd,bhje->bhde', k_c * jnp.exp(b_last[..., None, :] - b_c), v_c)
        return S, o

    xs = tuple(jnp.moveaxis(t, 2, 0) for t in (q, k, v, bcum))
    b, h = q.shape[0], q.shape[1]
    S0 = jnp.zeros((b, h, GLA_DK, GLA_DV), f32)
    _, o = lax.scan(step, S0, xs)
    return from_chunks(jnp.moveaxis(o, 0, 2), pad).astype(out_dtype)


def _fwd_setup_inputs(seed: int = 0) -> dict:
    key = jax.random.key(seed)
    ks = jax.random.split(key, 20)
    f32 = jnp.float32

    def nrm(k, shape, scale):
        return jax.random.normal(k, shape, f32) * scale

    def gain(k, shape):
        return 1.0 + 0.01 * jax.random.normal(k, shape, f32)

    dt = jnp.exp(jax.random.uniform(ks[7], (DEPTH, GDN_HEADS), f32, math.log(1e-3), math.log(1e-1)))
    return {
        "x": nrm(ks[0], (BATCH, SEQ, D_MODEL), 1.0),
        "meta_tokens": nrm(ks[1], (N_META, D_MODEL), 1.0),
        "attn_norm_w": gain(ks[2], (DEPTH, D_MODEL)),
        "w_in": nrm(ks[3], (DEPTH, D_MODEL, D_IN), D_MODEL ** -0.5),
        "gdn_conv_w": nrm(ks[4], (DEPTH, CONV_K, 2 * GDN_QK + GDN_V), CONV_K ** -0.5),
        "gdn_a_log": jnp.log(jax.random.uniform(ks[5], (DEPTH, GDN_HEADS), f32, 1.0, 16.0)),
        "gdn_dt_bias": dt + jnp.log(-jnp.expm1(-dt)),
        "gdn_norm_w": gain(ks[6], (DEPTH, GDN_DV)),
        "gla_gate_w2": nrm(ks[8], (DEPTH, GLA_GATE_RANK, GLA_QK), GLA_GATE_RANK ** -0.5),
        "gla_gate_b": nrm(ks[9], (DEPTH, GLA_QK), 0.01),
        "gla_norm_w": gain(ks[10], (DEPTH, GLA_DV)),
        "w_out": nrm(ks[11], (DEPTH, MIX_WIDTH, D_MODEL), MIX_WIDTH ** -0.5),
        "ffn_norm_w": gain(ks[12], (DEPTH, D_MODEL)),
        "w_gate": nrm(ks[13], (DEPTH, D_MODEL, D_FF), D_MODEL ** -0.5),
        "w_up": nrm(ks[14], (DEPTH, D_MODEL, D_FF), D_MODEL ** -0.5),
        "w_down": nrm(ks[15], (DEPTH, D_FF, D_MODEL), D_FF ** -0.5),
        "final_norm_w": gain(ks[16], (D_MODEL,)),
    }


def _fwd_reference(x, meta_tokens, attn_norm_w, w_in, gdn_conv_w, gdn_a_log, gdn_dt_bias, gdn_norm_w,
              gla_gate_w2, gla_gate_b, gla_norm_w, w_out, ffn_norm_w, w_gate, w_up, w_down, final_norm_w):
    f32 = jnp.float32
    bsz = x.shape[0]
    meta = jnp.broadcast_to(meta_tokens.astype(x.dtype)[None], (bsz, N_META, D_MODEL))
    h = jnp.concatenate([meta, x], axis=1)
    L = h.shape[1]
    for layer in range(DEPTH):
        n = rms_norm(h, attn_norm_w[layer])
        proj = n @ w_in[layer]
        (gdn_qkv, gdn_z, gdn_a, gdn_b, gla_q, gla_k, gla_v, gla_r, gla_lr) = jnp.split(proj, IN_OFFSETS, axis=-1)

        qkv = causal_short_conv(gdn_qkv, gdn_conv_w[layer])
        q, k, v = jnp.split(qkv, (GDN_QK, 2 * GDN_QK), axis=-1)
        q = l2_normalize(q.reshape(bsz, L, GDN_HEADS, GDN_DK)) * (GDN_DK ** -0.5)
        k = l2_normalize(k.reshape(bsz, L, GDN_HEADS, GDN_DK))
        v = v.reshape(bsz, L, GDN_HEADS, GDN_DV)
        beta = jax.nn.sigmoid(gdn_b.astype(f32))
        g = -jnp.exp(gdn_a_log[layer].astype(f32)) * jax.nn.softplus(
            gdn_a.astype(f32) + gdn_dt_bias[layer].astype(f32))
        o_gdn = gated_delta_rule(q, k, v, beta, g)
        o_gdn = rms_norm(o_gdn, gdn_norm_w[layer]) * jax.nn.silu(gdn_z.reshape(bsz, L, GDN_HEADS, GDN_DV))

        gq = gla_q.reshape(bsz, L, GLA_HEADS, GLA_DK) * (GLA_DK ** -0.5)
        gk = gla_k.reshape(bsz, L, GLA_HEADS, GLA_DK)
        gv = gla_v.reshape(bsz, L, GLA_HEADS, GLA_DV)
        log_a = jax.nn.log_sigmoid((gla_lr @ gla_gate_w2[layer] + gla_gate_b[layer]).astype(f32)) / GLA_GATE_NORMALIZER
        o_gla = gla_chunked(gq, gk, gv, log_a.reshape(bsz, L, GLA_HEADS, GLA_DK))
        o_gla = rms_norm(o_gla, gla_norm_w[layer]) * jax.nn.silu(gla_r.reshape(bsz, L, GLA_HEADS, GLA_DV))

        mixed = jnp.concatenate([o_gdn.reshape(bsz, L, GDN_V), o_gla.reshape(bsz, L, GLA_V)], axis=-1)
        h = h + mixed @ w_out[layer]

        n = rms_norm(h, ffn_norm_w[layer])
        h = h + (jax.nn.silu(n @ w_gate[layer]) * (n @ w_up[layer])) @ w_down[layer]
    return rms_norm(h[:, N_META:], final_norm_w)


import jax as _jax
import jax.numpy as _jnp

TWIN_FORMAT = 'train_step'
FWD_PARAMS = ['x', 'meta_tokens', 'attn_norm_w', 'w_in', 'gdn_conv_w', 'gdn_a_log', 'gdn_dt_bias', 'gdn_norm_w', 'gla_gate_w2', 'gla_gate_b', 'gla_norm_w', 'w_out', 'ffn_norm_w', 'w_gate', 'w_up', 'w_down', 'final_norm_w']
TWIN_WEIGHTS = ['meta_tokens', 'attn_norm_w', 'w_in', 'gdn_conv_w', 'gdn_a_log', 'gdn_dt_bias', 'gdn_norm_w', 'gla_gate_w2', 'gla_gate_b', 'gla_norm_w', 'w_out', 'ffn_norm_w', 'w_gate', 'w_up', 'w_down', 'final_norm_w']
TWIN_DIFF_INPUT = 'x'
TWIN_INPUTS = ['x', 'meta_tokens', 'attn_norm_w', 'w_in', 'gdn_conv_w', 'gdn_a_log', 'gdn_dt_bias', 'gdn_norm_w', 'gla_gate_w2', 'gla_gate_b', 'gla_norm_w', 'w_out', 'ffn_norm_w', 'w_gate', 'w_up', 'w_down', 'final_norm_w', 'loss_target', 'm_meta_tokens', 'm_attn_norm_w', 'm_w_in', 'm_gdn_conv_w', 'm_gdn_a_log', 'm_gdn_dt_bias', 'm_gdn_norm_w', 'm_gla_gate_w2', 'm_gla_gate_b', 'm_gla_norm_w', 'm_w_out', 'm_ffn_norm_w', 'm_w_gate', 'm_w_up', 'm_w_down', 'm_final_norm_w', 'v_meta_tokens', 'v_attn_norm_w', 'v_w_in', 'v_gdn_conv_w', 'v_gdn_a_log', 'v_gdn_dt_bias', 'v_gdn_norm_w', 'v_gla_gate_w2', 'v_gla_gate_b', 'v_gla_norm_w', 'v_w_out', 'v_ffn_norm_w', 'v_w_gate', 'v_w_up', 'v_w_down', 'v_final_norm_w']
TWIN_OUTPUTS = ['loss', 'grad_x', 'grad_meta_tokens', 'grad_attn_norm_w', 'grad_w_in', 'grad_gdn_conv_w', 'grad_gdn_a_log', 'grad_gdn_dt_bias', 'grad_gdn_norm_w', 'grad_gla_gate_w2', 'grad_gla_gate_b', 'grad_gla_norm_w', 'grad_w_out', 'grad_ffn_norm_w', 'grad_w_gate', 'grad_w_up', 'grad_w_down', 'grad_final_norm_w', 'delta_meta_tokens', 'delta_attn_norm_w', 'delta_w_in', 'delta_gdn_conv_w', 'delta_gdn_a_log', 'delta_gdn_dt_bias', 'delta_gdn_norm_w', 'delta_gla_gate_w2', 'delta_gla_gate_b', 'delta_gla_norm_w', 'delta_w_out', 'delta_ffn_norm_w', 'delta_w_gate', 'delta_w_up', 'delta_w_down', 'delta_final_norm_w', 'new_m_meta_tokens', 'new_m_attn_norm_w', 'new_m_w_in', 'new_m_gdn_conv_w', 'new_m_gdn_a_log', 'new_m_gdn_dt_bias', 'new_m_gdn_norm_w', 'new_m_gla_gate_w2', 'new_m_gla_gate_b', 'new_m_gla_norm_w', 'new_m_w_out', 'new_m_ffn_norm_w', 'new_m_w_gate', 'new_m_w_up', 'new_m_w_down', 'new_m_final_norm_w', 'new_v_meta_tokens', 'new_v_attn_norm_w', 'new_v_w_in', 'new_v_gdn_conv_w', 'new_v_gdn_a_log', 'new_v_gdn_dt_bias', 'new_v_gdn_norm_w', 'new_v_gla_gate_w2', 'new_v_gla_gate_b', 'new_v_gla_norm_w', 'new_v_w_out', 'new_v_ffn_norm_w', 'new_v_w_gate', 'new_v_w_up', 'new_v_w_down', 'new_v_final_norm_w']
TWIN_LEAF_KINDS = {'loss': 'loss', 'grad_x': 'grad_x', 'grad_meta_tokens': 'grad_w', 'grad_attn_norm_w': 'grad_w', 'grad_w_in': 'grad_w', 'grad_gdn_conv_w': 'grad_w', 'grad_gdn_a_log': 'grad_w', 'grad_gdn_dt_bias': 'grad_w', 'grad_gdn_norm_w': 'grad_w', 'grad_gla_gate_w2': 'grad_w', 'grad_gla_gate_b': 'grad_w', 'grad_gla_norm_w': 'grad_w', 'grad_w_out': 'grad_w', 'grad_ffn_norm_w': 'grad_w', 'grad_w_gate': 'grad_w', 'grad_w_up': 'grad_w', 'grad_w_down': 'grad_w', 'grad_final_norm_w': 'grad_w', 'delta_meta_tokens': 'delta_w', 'delta_attn_norm_w': 'delta_w', 'delta_w_in': 'delta_w', 'delta_gdn_conv_w': 'delta_w', 'delta_gdn_a_log': 'delta_w', 'delta_gdn_dt_bias': 'delta_w', 'delta_gdn_norm_w': 'delta_w', 'delta_gla_gate_w2': 'delta_w', 'delta_gla_gate_b': 'delta_w', 'delta_gla_norm_w': 'delta_w', 'delta_w_out': 'delta_w', 'delta_ffn_norm_w': 'delta_w', 'delta_w_gate': 'delta_w', 'delta_w_up': 'delta_w', 'delta_w_down': 'delta_w', 'delta_final_norm_w': 'delta_w', 'new_m_meta_tokens': 'new_m', 'new_m_attn_norm_w': 'new_m', 'new_m_w_in': 'new_m', 'new_m_gdn_conv_w': 'new_m', 'new_m_gdn_a_log': 'new_m', 'new_m_gdn_dt_bias': 'new_m', 'new_m_gdn_norm_w': 'new_m', 'new_m_gla_gate_w2': 'new_m', 'new_m_gla_gate_b': 'new_m', 'new_m_gla_norm_w': 'new_m', 'new_m_w_out': 'new_m', 'new_m_ffn_norm_w': 'new_m', 'new_m_w_gate': 'new_m', 'new_m_w_up': 'new_m', 'new_m_w_down': 'new_m', 'new_m_final_norm_w': 'new_m', 'new_v_meta_tokens': 'new_v', 'new_v_attn_norm_w': 'new_v', 'new_v_w_in': 'new_v', 'new_v_gdn_conv_w': 'new_v', 'new_v_gdn_a_log': 'new_v', 'new_v_gdn_dt_bias': 'new_v', 'new_v_gdn_norm_w': 'new_v', 'new_v_gla_gate_w2': 'new_v', 'new_v_gla_gate_b': 'new_v', 'new_v_gla_norm_w': 'new_v', 'new_v_w_out': 'new_v', 'new_v_ffn_norm_w': 'new_v', 'new_v_w_gate': 'new_v', 'new_v_w_up': 'new_v', 'new_v_w_down': 'new_v', 'new_v_final_norm_w': 'new_v'}


def _forward(args):
    return _fwd_reference(*[args[k] for k in FWD_PARAMS])


def _output_shape():
    def fwd():
        inp = _fwd_setup_inputs(0)
        return _fwd_reference(*[inp[k] for k in FWD_PARAMS])
    out = _jax.eval_shape(fwd)
    return out.shape, out.dtype

N_MICROBATCH = 1
ADAM_LR = 0.001
ADAM_B1 = 0.9
ADAM_B2 = 0.999
ADAM_EPS = 1e-08
ADAM_WD = 0.01
ADAM_STEP = 10
PER_EXAMPLE_BATCH_AXIS = {'x': 0, 'loss_target': 0}
SHARED_INPUTS = []
_WEIGHT_DTYPES = {'meta_tokens': _jnp.float32, 'attn_norm_w': _jnp.float32, 'w_in': _jnp.float32, 'gdn_conv_w': _jnp.float32, 'gdn_a_log': _jnp.float32, 'gdn_dt_bias': _jnp.float32, 'gdn_norm_w': _jnp.float32, 'gla_gate_w2': _jnp.float32, 'gla_gate_b': _jnp.float32, 'gla_norm_w': _jnp.float32, 'w_out': _jnp.float32, 'ffn_norm_w': _jnp.float32, 'w_gate': _jnp.float32, 'w_up': _jnp.float32, 'w_down': _jnp.float32, 'final_norm_w': _jnp.float32}
MOMENT_SCALE = {'meta_tokens': 4.300333e-03, 'attn_norm_w': 1.326937e-01, 'w_in': 7.076064e-02, 'gdn_conv_w': 5.392363e-02, 'gdn_a_log': 4.100739e-01, 'gdn_dt_bias': 4.006313e-01, 'gdn_norm_w': 1.997979e-01, 'gla_gate_w2': 1.207218e-02, 'gla_gate_b': 5.058844e-02, 'gla_norm_w': 1.462475e-01, 'w_out': 7.054160e-02, 'ffn_norm_w': 8.245636e-02, 'w_gate': 3.564386e-02, 'w_up': 3.450447e-02, 'w_down': 5.725913e-02, 'final_norm_w': 3.192518e+01}


def _to_microbatches(a, axis):
    t = _jnp.moveaxis(a, axis, 0)
    t = t.reshape((N_MICROBATCH, t.shape[0] // N_MICROBATCH) + t.shape[1:])
    return _jnp.moveaxis(t, 1, axis + 1)


def setup_inputs(seed: int = 0) -> dict:
    inp = _fwd_setup_inputs(seed)
    key = _jax.random.fold_in(_jax.random.key(seed), 7919)
    shape, _ = _output_shape()
    out = dict(inp)
    out["loss_target"] = _jax.random.normal(_jax.random.fold_in(key, 0), shape, _jnp.float32)
    for i, name in enumerate(TWIN_WEIGHTS):
        w = inp[name].astype(_jnp.float32)
        if MOMENT_SCALE is None:
            s = _jnp.sqrt(_jnp.mean(_jnp.square(w)) + 1e-30)
        else:
            s = MOMENT_SCALE[name]
        km, kv = _jax.random.split(_jax.random.fold_in(key, i + 1))
        out[name] = w
        out["m_" + name] = s * _jax.random.normal(km, w.shape, _jnp.float32)
        out["v_" + name] = (s * s) * _jax.random.uniform(kv, w.shape, _jnp.float32, 0.5, 1.5)
    if N_MICROBATCH > 1:
        for name, axis in PER_EXAMPLE_BATCH_AXIS.items():
            out[name] = _to_microbatches(out[name], axis)
    return {'x': out['x'], 'meta_tokens': out['meta_tokens'], 'attn_norm_w': out['attn_norm_w'], 'w_in': out['w_in'], 'gdn_conv_w': out['gdn_conv_w'], 'gdn_a_log': out['gdn_a_log'], 'gdn_dt_bias': out['gdn_dt_bias'], 'gdn_norm_w': out['gdn_norm_w'], 'gla_gate_w2': out['gla_gate_w2'], 'gla_gate_b': out['gla_gate_b'], 'gla_norm_w': out['gla_norm_w'], 'w_out': out['w_out'], 'ffn_norm_w': out['ffn_norm_w'], 'w_gate': out['w_gate'], 'w_up': out['w_up'], 'w_down': out['w_down'], 'final_norm_w': out['final_norm_w'], 'loss_target': out['loss_target'], 'm_meta_tokens': out['m_meta_tokens'], 'm_attn_norm_w': out['m_attn_norm_w'], 'm_w_in': out['m_w_in'], 'm_gdn_conv_w': out['m_gdn_conv_w'], 'm_gdn_a_log': out['m_gdn_a_log'], 'm_gdn_dt_bias': out['m_gdn_dt_bias'], 'm_gdn_norm_w': out['m_gdn_norm_w'], 'm_gla_gate_w2': out['m_gla_gate_w2'], 'm_gla_gate_b': out['m_gla_gate_b'], 'm_gla_norm_w': out['m_gla_norm_w'], 'm_w_out': out['m_w_out'], 'm_ffn_norm_w': out['m_ffn_norm_w'], 'm_w_gate': out['m_w_gate'], 'm_w_up': out['m_w_up'], 'm_w_down': out['m_w_down'], 'm_final_norm_w': out['m_final_norm_w'], 'v_meta_tokens': out['v_meta_tokens'], 'v_attn_norm_w': out['v_attn_norm_w'], 'v_w_in': out['v_w_in'], 'v_gdn_conv_w': out['v_gdn_conv_w'], 'v_gdn_a_log': out['v_gdn_a_log'], 'v_gdn_dt_bias': out['v_gdn_dt_bias'], 'v_gdn_norm_w': out['v_gdn_norm_w'], 'v_gla_gate_w2': out['v_gla_gate_w2'], 'v_gla_gate_b': out['v_gla_gate_b'], 'v_gla_norm_w': out['v_gla_norm_w'], 'v_w_out': out['v_w_out'], 'v_ffn_norm_w': out['v_ffn_norm_w'], 'v_w_gate': out['v_w_gate'], 'v_w_up': out['v_w_up'], 'v_w_down': out['v_w_down'], 'v_final_norm_w': out['v_final_norm_w']}


def _loss(weights, diff, rest, loss_target):
    with _jax.named_scope("forward"):
        args = {**rest, TWIN_DIFF_INPUT: diff, **{k: w.astype(_WEIGHT_DTYPES[k]) for k, w in weights.items()}}
        y = _forward(args)
    with _jax.named_scope("loss_head"):
        err = _jnp.square(y.astype(_jnp.float32) - loss_target)
        return 0.5 * _jnp.sum(_jnp.mean(err, axis=-1)) if err.ndim else 0.5 * err


def _adamw(w, g, m, v):
    m = ADAM_B1 * m + (1.0 - ADAM_B1) * g
    v = ADAM_B2 * v + (1.0 - ADAM_B2) * _jnp.square(g)
    m_hat = m / (1.0 - ADAM_B1 ** ADAM_STEP)
    v_hat = v / (1.0 - ADAM_B2 ** ADAM_STEP)
    delta = -ADAM_LR * (m_hat / (_jnp.sqrt(v_hat) + ADAM_EPS) + ADAM_WD * w)
    return delta, m, v


def reference(x, meta_tokens, attn_norm_w, w_in, gdn_conv_w, gdn_a_log, gdn_dt_bias, gdn_norm_w, gla_gate_w2, gla_gate_b, gla_norm_w, w_out, ffn_norm_w, w_gate, w_up, w_down, final_norm_w, loss_target, m_meta_tokens, m_attn_norm_w, m_w_in, m_gdn_conv_w, m_gdn_a_log, m_gdn_dt_bias, m_gdn_norm_w, m_gla_gate_w2, m_gla_gate_b, m_gla_norm_w, m_w_out, m_ffn_norm_w, m_w_gate, m_w_up, m_w_down, m_final_norm_w, v_meta_tokens, v_attn_norm_w, v_w_in, v_gdn_conv_w, v_gdn_a_log, v_gdn_dt_bias, v_gdn_norm_w, v_gla_gate_w2, v_gla_gate_b, v_gla_norm_w, v_w_out, v_ffn_norm_w, v_w_gate, v_w_up, v_w_down, v_final_norm_w):
    given = dict(x=x, meta_tokens=meta_tokens, attn_norm_w=attn_norm_w, w_in=w_in, gdn_conv_w=gdn_conv_w, gdn_a_log=gdn_a_log, gdn_dt_bias=gdn_dt_bias, gdn_norm_w=gdn_norm_w, gla_gate_w2=gla_gate_w2, gla_gate_b=gla_gate_b, gla_norm_w=gla_norm_w, w_out=w_out, ffn_norm_w=ffn_norm_w, w_gate=w_gate, w_up=w_up, w_down=w_down, final_norm_w=final_norm_w, loss_target=loss_target, m_meta_tokens=m_meta_tokens, m_attn_norm_w=m_attn_norm_w, m_w_in=m_w_in, m_gdn_conv_w=m_gdn_conv_w, m_gdn_a_log=m_gdn_a_log, m_gdn_dt_bias=m_gdn_dt_bias, m_gdn_norm_w=m_gdn_norm_w, m_gla_gate_w2=m_gla_gate_w2, m_gla_gate_b=m_gla_gate_b, m_gla_norm_w=m_gla_norm_w, m_w_out=m_w_out, m_ffn_norm_w=m_ffn_norm_w, m_w_gate=m_w_gate, m_w_up=m_w_up, m_w_down=m_w_down, m_final_norm_w=m_final_norm_w, v_meta_tokens=v_meta_tokens, v_attn_norm_w=v_attn_norm_w, v_w_in=v_w_in, v_gdn_conv_w=v_gdn_conv_w, v_gdn_a_log=v_gdn_a_log, v_gdn_dt_bias=v_gdn_dt_bias, v_gdn_norm_w=v_gdn_norm_w, v_gla_gate_w2=v_gla_gate_w2, v_gla_gate_b=v_gla_gate_b, v_gla_norm_w=v_gla_norm_w, v_w_out=v_w_out, v_ffn_norm_w=v_ffn_norm_w, v_w_gate=v_w_gate, v_w_up=v_w_up, v_w_down=v_w_down, v_final_norm_w=v_final_norm_w)
    weights = {n: given[n] for n in TWIN_WEIGHTS}
    shared = {n: given[n] for n in SHARED_INPUTS}
    per_example = {n: given[n] for n in ['x']}
    grad_fn = _jax.value_and_grad(_loss, argnums=(0, 1))

    def one_microbatch(ex, loss_target):
        ex = dict(ex)
        diff = ex.pop(TWIN_DIFF_INPUT)
        return grad_fn(weights, diff, {**shared, **ex}, loss_target)

    if N_MICROBATCH == 1:
        loss, (grad_w, grad_x) = one_microbatch(per_example, given["loss_target"])
    else:
        def body(carry, xs):
            loss_sum, grad_sum = carry
            l_k, (gw_k, gx_k) = one_microbatch(xs[0], xs[1])
            with _jax.named_scope("update"):
                return (loss_sum + l_k, _jax.tree.map(_jnp.add, grad_sum, gw_k)), gx_k

        init = (_jnp.zeros((), _jnp.float32), _jax.tree.map(_jnp.zeros_like, weights))
        (loss, grad_w), grad_x = _jax.lax.scan(body, init, (per_example, given["loss_target"]))
    with _jax.named_scope("update"):
        delta_w, new_m, new_v = {}, {}, {}
        for n in TWIN_WEIGHTS:
            delta_w[n], new_m[n], new_v[n] = _adamw(weights[n], grad_w[n], given["m_" + n], given["v_" + n])
    return (loss, grad_x, *[grad_w[n] for n in TWIN_WEIGHTS], *[delta_w[n] for n in TWIN_WEIGHTS],
            *[new_m[n] for n in TWIN_WEIGHTS], *[new_v[n] for n in TWIN_WEIGHTS])
```

```python
import functools

import jax
import jax.numpy as jnp
from jax import lax
from jax.experimental import pallas as pl
from jax.experimental.pallas import tpu as pltpu

F32 = jnp.float32
BF16 = jnp.bfloat16
_MXU_DTYPE = jnp.bfloat16

D_MODEL = 2048
N_META = 16
ROW_PAD = 48
HEAD_ROWS = ROW_PAD + N_META
CONV_K = 4
GDN_HEADS, GDN_DK, GDN_DV, GDN_CHUNK = 8, 128, 128, 64
GLA_HEADS, GLA_DK, GLA_DV, GLA_CHUNK = 4, 128, 256, 16
GLA_RANK = 16
GLA_GATE_NORMALIZER = 16.0
GDN_QK = GDN_HEADS * GDN_DK
GDN_V = GDN_HEADS * GDN_DV
GLA_QK = GLA_HEADS * GLA_DK
GLA_V = GLA_HEADS * GLA_DV
D_FF = 5632
D_IN = 7200
NORM_EPS = 1e-6
C_QKV, C_Z, C_GQ, C_GK, C_GV, C_GR, C_SM = 0, 3072, 4096, 4608, 5120, 6144, 7168
SM_W = 128
D_PROJ = 7680
R_Z, R_A, R_B, R_GQ, R_GK, R_GV, R_GR, R_LR = 3072, 4096, 4104, 4112, 4624, 5136, 6160, 7184

ADAM_LR, ADAM_B1, ADAM_B2, ADAM_EPS, ADAM_WD, ADAM_STEP = 0.001, 0.9, 0.999, 1e-08, 0.01, 10

N_DEV = 8
VMEM_LIMIT = 56 * 1024 * 1024

NN = (((1,), (0,)), ((), ()))
NT = (((1,), (1,)), ((), ()))
TN = (((0,), (0,)), ((), ()))


def _dot(a, b, dims=NN):
    return lax.dot_general(a.astype(_MXU_DTYPE), b.astype(_MXU_DTYPE), dims, preferred_element_type=F32)


def _dotx(a, b, dims=NN):
    return lax.dot_general(a, b, dims, precision=lax.Precision.HIGHEST, preferred_element_type=F32)


def _dot3(a, b):
    ah = a.astype(BF16)
    al = (a - ah.astype(F32)).astype(BF16)
    bh = b.astype(BF16)
    bl = (b - bh.astype(F32)).astype(BF16)
    d = functools.partial(lax.dot_general, dimension_numbers=NN, preferred_element_type=F32)
    return d(ah, bh) + (d(ah, bl) + d(al, bh))


def _tile(n, target, mult=8):
    best = None
    for t in range(mult, min(n, target) + 1, mult):
        if n % t == 0:
            best = t
    return best if best is not None else n


def _params(*sem):
    return pltpu.CompilerParams(dimension_semantics=sem, vmem_limit_bytes=VMEM_LIMIT)


def _sigmoid(x):
    return 1.0 / (1.0 + jnp.exp(-x))


def _softplus(x):
    return jnp.maximum(x, 0.0) + jnp.log1p(jnp.exp(-jnp.abs(x)))


def _silu_and_grad(c):
    s = _sigmoid(c)
    return c * s, s * (1.0 + c * (1.0 - s))


def _matmul(a, b, *, mode, name, out_dtype=F32, add=None, tm=1376, tn=512, tk=2064):
    if mode == "tn":
        K, M = a.shape
        N = b.shape[1]
    else:
        M, K = a.shape
        N = b.shape[0] if mode == "nt" else b.shape[1]
    tm = _tile(M, tm, 128 if mode == "tn" else 16)
    tn = _tile(N, tn, 128)
    tk = _tile(K, tk, 16 if mode == "tn" else 128)
    gm, gn, gk = M // tm, N // tn, K // tk
    dims = {"nn": NN, "nt": NT, "tn": TN}[mode]

    def body(*refs):
        if add is None:
            a_ref, b_ref, o_ref = refs[:3]
            add_ref = None
        else:
            a_ref, b_ref, add_ref, o_ref = refs[:4]
        p = _dot(a_ref[...], b_ref[...], dims)

        def finish(r):
            if add_ref is not None:
                r = r + add_ref[...]
            o_ref[...] = r.astype(out_dtype)

        if gk == 1:
            finish(p)
        else:
            acc_ref = refs[-1]
            k = pl.program_id(2)

            @pl.when(k == 0)
            def _():
                acc_ref[...] = p

            @pl.when(k > 0)
            def _():
                acc_ref[...] += p

            @pl.when(k == gk - 1)
            def _():
                finish(acc_ref[...])

    if mode == "tn":
        a_spec = pl.BlockSpec((tk, tm), lambda i, j, k: (k, i))
    else:
        a_spec = pl.BlockSpec((tm, tk), lambda i, j, k: (i, k))
    if mode == "nt":
        b_spec = pl.BlockSpec((tn, tk), lambda i, j, k: (j, k))
    else:
        b_spec = pl.BlockSpec((tk, tn), lambda i, j, k: (k, j))
    o_spec = pl.BlockSpec((tm, tn), lambda i, j, k: (i, j))
    in_specs = [a_spec, b_spec] + ([o_spec] if add is not None else [])
    args = (a, b) + ((add,) if add is not None else ())
    return pl.pallas_call(
        body, name=name, grid=(gm, gn, gk), in_specs=in_specs, out_specs=o_spec,
        out_shape=jax.ShapeDtypeStruct((M, N), out_dtype),
        scratch_shapes=[pltpu.VMEM((tm, tn), F32)] if gk > 1 else [],
        compiler_params=_params("parallel", "parallel", "arbitrary"),
    )(*args)


def _rmsnorm_fwd(h, w, *, name):
    M, D = h.shape
    tm = _tile(M, 688, 16)

    def body(h_ref, w_ref, n_ref):
        x = h_ref[...]
        r = lax.rsqrt(jnp.mean(x * x, axis=-1, keepdims=True) + NORM_EPS)
        n_ref[...] = (x * r * w_ref[...]).astype(n_ref.dtype)

    return pl.pallas_call(
        body, name=name, grid=(M // tm,),
        in_specs=[pl.BlockSpec((tm, D), lambda i: (i, 0)), pl.BlockSpec((1, D), lambda i: (0, 0))],
        out_specs=pl.BlockSpec((tm, D), lambda i: (i, 0)),
        out_shape=jax.ShapeDtypeStruct((M, D), BF16),
        compiler_params=_params("parallel"),
    )(h, w)


def _rmsnorm_bwd(h, w, dn, dres, *, name):
    M, D = h.shape
    tm = _tile(M, 344, 8)
    g = M // tm

    def body(h_ref, w_ref, dn_ref, dres_ref, dh_ref, dw_ref, acc_ref):
        i = pl.program_id(0)
        x = h_ref[...]
        r = lax.rsqrt(jnp.mean(x * x, axis=-1, keepdims=True) + NORM_EPS)
        xhat = x * r
        dn_ = dn_ref[...]
        dxhat = dn_ * w_ref[...]
        dh_ref[...] = dres_ref[...] + r * (dxhat - xhat * jnp.mean(dxhat * xhat, axis=-1, keepdims=True))
        part = jnp.sum((dn_ * xhat).reshape(tm // 8, 8, D), axis=0)

        @pl.when(i == 0)
        def _():
            acc_ref[...] = part

        @pl.when(i > 0)
        def _():
            acc_ref[...] += part

        @pl.when(i == g - 1)
        def _():
            dw_ref[...] = jnp.sum(acc_ref[...], axis=0, keepdims=True)

    row = pl.BlockSpec((tm, D), lambda i: (i, 0))
    vec = pl.BlockSpec((1, D), lambda i: (0, 0))
    return pl.pallas_call(
        body, name=name, grid=(g,), in_specs=[row, vec, row, row], out_specs=[row, vec],
        out_shape=[jax.ShapeDtypeStruct((M, D), F32), jax.ShapeDtypeStruct((1, D), F32)],
        scratch_shapes=[pltpu.VMEM((8, D), F32)],
        compiler_params=_params("arbitrary"),
    )(h, w, dn, dres)


def _loss_head(h, w, target_p, *, name):
    M, D = h.shape
    tm = _tile(M, 344, 8)
    g = M // tm

    def body(h_ref, w_ref, t_ref, dh_ref, dw_ref, loss_ref, acc_ref, lacc_ref):
        i = pl.program_id(0)
        x = h_ref[...]
        row = i * tm + lax.broadcasted_iota(jnp.int32, (tm, 1), 0)
        live = row >= HEAD_ROWS
        r = lax.rsqrt(jnp.mean(x * x, axis=-1, keepdims=True) + NORM_EPS)
        xhat = x * r
        err = jnp.where(live, xhat * w_ref[...] - t_ref[...], 0.0)
        dy = err * (1.0 / D)
        dxhat = dy * w_ref[...]
        dh_ref[...] = r * (dxhat - xhat * jnp.mean(dxhat * xhat, axis=-1, keepdims=True))
        part = jnp.sum((dy * xhat).reshape(tm // 8, 8, D), axis=0)
        lpart = jnp.sum((err * err).reshape(tm // 8, 8, D), axis=0)

        @pl.when(i == 0)
        def _():
            acc_ref[...] = part
            lacc_ref[...] = lpart

        @pl.when(i > 0)
        def _():
            acc_ref[...] += part
            lacc_ref[...] += lpart

        @pl.when(i == g - 1)
        def _():
            dw_ref[...] = jnp.sum(acc_ref[...], axis=0, keepdims=True)
            tot = jnp.sum(jnp.sum(lacc_ref[...], axis=0, keepdims=True), axis=1, keepdims=True)
            loss_ref[...] = jnp.broadcast_to(tot * (0.5 / D), (1, 128))

    row = pl.BlockSpec((tm, D), lambda i: (i, 0))
    vec = pl.BlockSpec((1, D), lambda i: (0, 0))
    return pl.pallas_call(
        body, name=name, grid=(g,), in_specs=[row, vec, row],
        out_specs=[row, vec, pl.BlockSpec((1, 128), lambda i: (0, 0))],
        out_shape=[jax.ShapeDtypeStruct((M, D), F32), jax.ShapeDtypeStruct((1, D), F32),
                   jax.ShapeDtypeStruct((1, 128), F32)],
        scratch_shapes=[pltpu.VMEM((8, D), F32), pltpu.VMEM((8, D), F32)],
        compiler_params=_params("arbitrary"),
    )(h, w, target_p)


def _gate_terms(sm, w2p, b2, alog_p, dt_p, row0):
    tm = sm.shape[0]
    lane = lax.broadcasted_iota(jnp.int32, (tm, SM_W), 1)
    live = (row0 + lax.broadcasted_iota(jnp.int32, (tm, 1), 0)) >= ROW_PAD
    pre = sm + dt_p
    neg_a = -jnp.exp(alog_p)
    g = neg_a * _softplus(pre)
    beta = _sigmoid(sm)
    z = _dot(sm, w2p) + b2
    return lane, live, pre, neg_a, g, beta, z


def _gates_fwd(proj, w2p, b2, alog_p, dt_p, *, name):
    M = proj.shape[0]
    tm = _tile(M, 688, 8)

    def body(sm_ref, w2_ref, b2_ref, al_ref, dt_ref, gb_ref, la_ref):
        row0 = pl.program_id(0) * tm
        lane, live, _, _, g, beta, z = _gate_terms(sm_ref[...], w2_ref[...], b2_ref[...], al_ref[...], dt_ref[...], row0)
        gb = jnp.where(lane < GDN_HEADS, g, jnp.where(lane < 2 * GDN_HEADS, beta, 0.0))
        gb_ref[...] = jnp.where(live, gb, 0.0)
        la = (jnp.minimum(z, 0.0) - jnp.log1p(jnp.exp(-jnp.abs(z)))) * (1.0 / GLA_GATE_NORMALIZER)
        la_ref[...] = jnp.where(live, la, 0.0)

    full = lambda s: pl.BlockSpec(s, lambda i: (0, 0))
    return pl.pallas_call(
        body, name=name, grid=(M // tm,),
        in_specs=[pl.BlockSpec((tm, SM_W), lambda i: (i, C_SM // SM_W)), full((SM_W, GLA_QK)), full((1, GLA_QK)),
                  full((1, SM_W)), full((1, SM_W))],
        out_specs=[pl.BlockSpec((tm, SM_W), lambda i: (i, 0)), pl.BlockSpec((tm, GLA_QK), lambda i: (i, 0))],
        out_shape=[jax.ShapeDtypeStruct((M, SM_W), F32), jax.ShapeDtypeStruct((M, GLA_QK), F32)],
        compiler_params=_params("parallel"),
    )(proj, w2p, b2, alog_p, dt_p)


def _gates_bwd(proj, w2p, b2, alog_p, dt_p, dgb_heads, dla, *, name):
    M = proj.shape[0]
    tm = _tile(M, 688, 8)
    g_ = M // tm

    def body(sm_ref, w2_ref, b2_ref, al_ref, dt_ref, dgb_ref, dla_ref,
             dsm_ref, dw2_ref, db2_ref, dal_ref, ddt_ref):
        i = pl.program_id(0)
        sm = sm_ref[...]
        lane, live, pre, neg_a, g, beta, z = _gate_terms(sm, w2_ref[...], b2_ref[...], al_ref[...], dt_ref[...], i * tm)
        dz = jnp.where(live, dla_ref[...] * (_sigmoid(-z) * (1.0 / GLA_GATE_NORMALIZER)), 0.0)
        dsm_lr = _dot(dz, w2_ref[...], NT)
        dgb = dgb_ref[0]
        for hh in range(1, GDN_HEADS):
            dgb = dgb + dgb_ref[hh]
        dgb = jnp.where(live, dgb, 0.0)
        da = dgb * neg_a * _sigmoid(pre)
        db = dgb * beta * (1.0 - beta)
        dsm = jnp.where(lane < GDN_HEADS, da, jnp.where(lane < 2 * GDN_HEADS, db, dsm_lr))
        dsm_ref[...] = dsm.astype(dsm_ref.dtype)
        is_a = lane < GDN_HEADS
        dal = jnp.sum(jnp.where(is_a, dgb * g, 0.0), axis=0, keepdims=True)
        ddt = jnp.sum(jnp.where(is_a, da, 0.0), axis=0, keepdims=True)
        dw2 = _dot(sm, dz, TN)
        db2 = jnp.sum(dz, axis=0, keepdims=True)

        @pl.when(i == 0)
        def _():
            dw2_ref[...] = dw2
            db2_ref[...] = db2
            dal_ref[...] = dal
            ddt_ref[...] = ddt

        @pl.when(i > 0)
        def _():
            dw2_ref[...] += dw2
            db2_ref[...] += db2
            dal_ref[...] += dal
            ddt_ref[...] += ddt

    full = lambda s: pl.BlockSpec(s, lambda i: (0, 0))
    return pl.pallas_call(
        body, name=name, grid=(g_,),
        in_specs=[pl.BlockSpec((tm, SM_W), lambda i: (i, C_SM // SM_W)), full((SM_W, GLA_QK)), full((1, GLA_QK)),
                  full((1, SM_W)), full((1, SM_W)),
                  pl.BlockSpec((GDN_HEADS, tm, SM_W), lambda i: (0, i, 0)),
                  pl.BlockSpec((tm, GLA_QK), lambda i: (i, 0))],
        out_specs=[pl.BlockSpec((tm, SM_W), lambda i: (i, 0)), full((SM_W, GLA_QK)), full((1, GLA_QK)),
                   full((1, SM_W)), full((1, SM_W))],
        out_shape=[jax.ShapeDtypeStruct((M, SM_W), BF16), jax.ShapeDtypeStruct((SM_W, GLA_QK), F32),
                   jax.ShapeDtypeStruct((1, GLA_QK), F32), jax.ShapeDtypeStruct((1, SM_W), F32),
                   jax.ShapeDtypeStruct((1, SM_W), F32)],
        compiler_params=_params("arbitrary"),
    )(proj, w2p, b2, alog_p, dt_p, dgb_heads, dla)


N_QKV_BLOCKS = (2 * GDN_QK + GDN_V) // 128
HALO = 8


def _conv_terms(x_ref, halo_ref, cw_ref, xs_ref, i, tm):
    xs_ref[HALO:HALO + tm, :] = x_ref[...]
    xs_ref[0:HALO, :] = jnp.where(i > 0, halo_ref[...], 0.0)
    cw = cw_ref[...]
    taps = [xs_ref[HALO - (CONV_K - 1) + t:HALO - (CONV_K - 1) + t + tm, :] for t in range(CONV_K)]
    c = taps[0] * cw[0:1, :]
    for t in range(1, CONV_K):
        c = c + taps[t] * cw[t:t + 1, :]
    return c, taps


def _prep_fwd(proj, conv_w8, *, name):
    M = proj.shape[0]
    tm = _tile(M, 688, 8)

    def body(x_ref, halo_ref, cw_ref, o_ref, xs_ref):
        j, i = pl.program_id(0), pl.program_id(1)
        c, _ = _conv_terms(x_ref, halo_ref, cw_ref, xs_ref, i, tm)
        s, _ = _silu_and_grad(c)
        r = lax.rsqrt(jnp.sum(s * s, axis=-1, keepdims=True) + NORM_EPS)
        scale = jnp.where(j < GDN_HEADS, GDN_DK ** -0.5, 1.0)
        o_ref[...] = jnp.where(j < 2 * GDN_HEADS, s * (r * scale), s)

    hb = tm // HALO
    return pl.pallas_call(
        body, name=name, grid=(N_QKV_BLOCKS, M // tm),
        in_specs=[pl.BlockSpec((tm, 128), lambda j, i: (i, j)),
                  pl.BlockSpec((HALO, 128), lambda j, i: (jnp.maximum(i * hb - 1, 0), j)),
                  pl.BlockSpec((8, 128), lambda j, i: (0, j))],
        out_specs=pl.BlockSpec((tm, 128), lambda j, i: (i, j)),
        out_shape=jax.ShapeDtypeStruct((M, N_QKV_BLOCKS * 128), F32),
        scratch_shapes=[pltpu.VMEM((tm + HALO, 128), F32)],
        compiler_params=_params("parallel", "arbitrary"),
    )(proj, proj, conv_w8)


def _prep_bwd_a(proj, conv_w8, dact, *, name):
    M = proj.shape[0]
    tm = _tile(M, 688, 8)
    g_ = M // tm

    def body(x_ref, halo_ref, cw_ref, da_ref, dc_ref, dcw_ref, xs_ref):
        j, i = pl.program_id(0), pl.program_id(1)
        c, taps = _conv_terms(x_ref, halo_ref, cw_ref, xs_ref, i, tm)
        s, ds_dc = _silu_and_grad(c)
        r = lax.rsqrt(jnp.sum(s * s, axis=-1, keepdims=True) + NORM_EPS)
        scale = jnp.where(j < GDN_HEADS, GDN_DK ** -0.5, 1.0)
        da = da_ref[...]
        y = s * r
        dy = da * scale
        ds_norm = r * (dy - y * jnp.sum(dy * y, axis=-1, keepdims=True))
        ds = jnp.where(j < 2 * GDN_HEADS, ds_norm, da)
        dc = ds * ds_dc
        dc_ref[...] = dc
        r8 = lax.broadcasted_iota(jnp.int32, (8, 128), 0)
        part = jnp.zeros((8, 128), F32)
        for t in range(CONV_K):
            part = jnp.where(r8 == t, jnp.sum(dc * taps[t], axis=0, keepdims=True), part)

        @pl.when(i == 0)
        def _():
            dcw_ref[...] = part

        @pl.when(i > 0)
        def _():
            dcw_ref[...] += part

    hb = tm // HALO
    blk = pl.BlockSpec((tm, 128), lambda j, i: (i, j))
    return pl.pallas_call(
        body, name=name, grid=(N_QKV_BLOCKS, g_),
        in_specs=[blk, pl.BlockSpec((HALO, 128), lambda j, i: (jnp.maximum(i * hb - 1, 0), j)),
                  pl.BlockSpec((8, 128), lambda j, i: (0, j)), blk],
        out_specs=[blk, pl.BlockSpec((8, 128), lambda j, i: (0, j))],
        out_shape=[jax.ShapeDtypeStruct((M, N_QKV_BLOCKS * 128), F32),
                   jax.ShapeDtypeStruct((8, N_QKV_BLOCKS * 128), F32)],
        scratch_shapes=[pltpu.VMEM((tm + HALO, 128), F32)],
        compiler_params=_params("parallel", "arbitrary"),
    )(proj, proj, conv_w8, dact)


def _prep_bwd_b(dc, conv_w8, *, name):
    M = dc.shape[0]
    tm = _tile(M, 688, 8)
    g_ = M // tm

    def body(d_ref, halo_ref, cw_ref, o_ref, ds_ref):
        i = pl.program_id(1)
        ds_ref[0:tm, :] = d_ref[...]
        ds_ref[tm:tm + HALO, :] = jnp.where(i < g_ - 1, halo_ref[...], 0.0)
        cw = cw_ref[...]
        acc = ds_ref[CONV_K - 1:CONV_K - 1 + tm, :] * cw[0:1, :]
        for t in range(1, CONV_K):
            acc = acc + ds_ref[CONV_K - 1 - t:CONV_K - 1 - t + tm, :] * cw[t:t + 1, :]
        o_ref[...] = acc.astype(o_ref.dtype)

    hb = tm // HALO
    last = M // HALO - 1
    blk = pl.BlockSpec((tm, 128), lambda j, i: (i, j))
    return pl.pallas_call(
        body, name=name, grid=(N_QKV_BLOCKS, g_),
        in_specs=[blk, pl.BlockSpec((HALO, 128), lambda j, i: (jnp.minimum((i + 1) * hb, last), j)),
                  pl.BlockSpec((8, 128), lambda j, i: (0, j))],
        out_specs=blk,
        out_shape=jax.ShapeDtypeStruct((M, N_QKV_BLOCKS * 128), BF16),
        scratch_shapes=[pltpu.VMEM((tm + HALO, 128), F32)],
        compiler_params=_params("parallel", "arbitrary"),
    )(dc, dc, conv_w8)


def _unit_lower_inverse(a_low, eye):
    b = -a_low
    x = eye + b
    pw = b
    for _ in range(5):
        pw = _dot3(pw, pw)
        x = x + _dot3(x, pw)
    return x


class _GdnChunk:
    def __init__(self, q, k, v, gb, h):
        C = GDN_CHUNK
        lane = lax.broadcasted_iota(jnp.int32, (C, SM_W), 1)
        g = jnp.sum(jnp.where(lane == h, gb, 0.0), axis=1, keepdims=True)
        self.beta = jnp.sum(jnp.where(lane == h + GDN_HEADS, gb, 0.0), axis=1, keepdims=True)
        ri = lax.broadcasted_iota(jnp.int32, (C, C), 0)
        ci = lax.broadcasted_iota(jnp.int32, (C, C), 1)
        self.causal = ri >= ci
        self.strict = ri > ci
        self.eye = (ri == ci).astype(F32)
        gcb = _dotx(self.causal.astype(F32), jnp.broadcast_to(g, (C, SM_W)))
        self.gcol = gcb[:, 0:1]
        grow = gcb.T[0:1, 0:C]
        self.decay = jnp.exp(jnp.where(self.causal, self.gcol - grow, -1e30))
        self.egc = jnp.exp(self.gcol)
        glast = gcb[C - 1:C, 0:1]
        self.elast = jnp.exp(glast - self.gcol)
        self.gl = jnp.exp(glast)
        self.q, self.k, self.v = q, k, v
        self.kb = k * self.beta
        self.a_low = jnp.where(self.strict, _dot(self.kb, k, NT) * self.decay, 0.0)
        self.p = _dot(q, k, NT) * self.decay
        self.qd = q * self.egc
        self.kd = k * self.elast
        self.bu = v * self.beta
        self.bw = self.kb * self.egc


def _gdn_specs(n_of):
    C = GDN_CHUNK
    q_spec = pl.BlockSpec((C, 128), lambda h, n: (n_of(n), h))
    k_spec = pl.BlockSpec((C, 128), lambda h, n: (n_of(n), h + GDN_HEADS))
    v_spec = pl.BlockSpec((C, 128), lambda h, n: (n_of(n), h + 2 * GDN_HEADS))
    gb_spec = pl.BlockSpec((C, SM_W), lambda h, n: (n_of(n), 0))
    o_spec = pl.BlockSpec((C, 128), lambda h, n: (n_of(n), h))
    s_spec = pl.BlockSpec((None, None, GDN_DK, GDN_DV), lambda h, n: (h, n_of(n), 0, 0))
    t_spec = pl.BlockSpec((None, None, C, C), lambda h, n: (h, n_of(n), 0, 0))
    return q_spec, k_spec, v_spec, gb_spec, o_spec, s_spec, t_spec


def _gdn_fwd(act, gb, *, name):
    M = act.shape[0]
    N = M // GDN_CHUNK

    def body(q_ref, k_ref, v_ref, gb_ref, o_ref, s_ref, t_ref, state):
        h, n = pl.program_id(0), pl.program_id(1)

        @pl.when(n == 0)
        def _():
            state[...] = jnp.zeros_like(state)

        c = _GdnChunk(q_ref[...], k_ref[...], v_ref[...], gb_ref[...], h)
        tinv = _unit_lower_inverse(c.a_low, c.eye)
        s = state[...]
        s_ref[...] = s
        t_ref[...] = tinv
        u = _dot(tinv, c.bu)
        w = _dot(tinv, c.bw)
        vn = u - _dot(w, s)
        o_ref[...] = _dot(c.qd, s) + _dot(c.p, vn)
        state[...] = c.gl * s + _dot(c.kd, vn, TN)

    q_spec, k_spec, v_spec, gb_spec, o_spec, s_spec, t_spec = _gdn_specs(lambda n: n)
    return pl.pallas_call(
        body, name=name, grid=(GDN_HEADS, N),
        in_specs=[q_spec, k_spec, v_spec, gb_spec], out_specs=[o_spec, s_spec, t_spec],
        out_shape=[jax.ShapeDtypeStruct((M, GDN_V), F32),
                   jax.ShapeDtypeStruct((GDN_HEADS, N, GDN_DK, GDN_DV), F32),
                   jax.ShapeDtypeStruct((GDN_HEADS, N, GDN_CHUNK, GDN_CHUNK), F32)],
        scratch_shapes=[pltpu.VMEM((GDN_DK, GDN_DV), F32)],
        compiler_params=_params("parallel", "arbitrary"),
    )(act, act, act, gb)


def _gdn_bwd(act, gb, do, s_all, t_all, *, name):
    M = act.shape[0]
    N = M // GDN_CHUNK
    C = GDN_CHUNK

    def body(q_ref, k_ref, v_ref, gb_ref, do_ref, s_ref, t_ref, dq_ref, dk_ref, dv_ref, dgb_ref, dstate):
        h, n = pl.program_id(0), pl.program_id(1)

        @pl.when(n == 0)
        def _():
            dstate[...] = jnp.zeros_like(dstate)

        c = _GdnChunk(q_ref[...], k_ref[...], v_ref[...], gb_ref[...], h)
        tinv = t_ref[...]
        s = s_ref[...]
        do_ = do_ref[...]
        ds1 = dstate[...]
        u = _dot(tinv, c.bu)
        w = _dot(tinv, c.bw)
        vn = u - _dot(w, s)
        dvn = _dot(c.p, do_, TN) + _dot(c.kd, ds1)
        dqd = _dot(do_, s, NT)
        dp = jnp.where(c.causal, _dot(do_, vn, NT), 0.0)
        dstate[...] = _dot(c.qd, do_, TN) + c.gl * ds1 - _dot(w, dvn, TN)
        dkd = _dot(vn, ds1, NT)
        dgl = jnp.sum(jnp.sum(s * ds1, axis=1, keepdims=True), axis=0, keepdims=True)
        dw = -_dot(dvn, s, NT)
        dbu = _dot(tinv, dvn, TN)
        dbw = _dot(tinv, dw, TN)
        da = jnp.where(c.strict, -(_dot(dbu, u, NT) + _dot(dbw, w, NT)), 0.0)
        dm = da * c.decay
        dn_ = dp * c.decay
        e = da * c.a_low + dp * c.p
        dkb = _dot(dm, c.k) + dbw * c.egc
        dk_ref[...] = _dot(dm, c.kb, TN) + _dot(dn_, c.q, TN) + dkb * c.beta + dkd * c.elast
        dq_ref[...] = _dot(dn_, c.k) + dqd * c.egc
        dv_ref[...] = dbu * c.beta
        dbeta = jnp.sum(dbu * c.v, axis=1, keepdims=True) + jnp.sum(dkb * c.k, axis=1, keepdims=True)
        t_kd = jnp.sum(dkd * c.kd, axis=1, keepdims=True)
        dgc = (jnp.sum(e, axis=1, keepdims=True) - jnp.sum(e.T, axis=1, keepdims=True)
               + jnp.sum(dbw * c.bw, axis=1, keepdims=True) + jnp.sum(dqd * c.qd, axis=1, keepdims=True) - t_kd)
        last = lax.broadcasted_iota(jnp.int32, (C, 1), 0) == C - 1
        dgc = dgc + jnp.where(last, jnp.sum(t_kd, axis=0, keepdims=True) + dgl * c.gl, 0.0)
        upper = (lax.broadcasted_iota(jnp.int32, (C, C), 0) <= lax.broadcasted_iota(jnp.int32, (C, C), 1)).astype(F32)
        dg = _dotx(upper, jnp.broadcast_to(dgc, (C, SM_W)))
        lane = lax.broadcasted_iota(jnp.int32, (C, SM_W), 1)
        dgb_ref[...] = jnp.where(lane == h, dg, jnp.where(lane == h + GDN_HEADS, dbeta, 0.0))

    rev = lambda n: N - 1 - n
    q_spec, k_spec, v_spec, gb_spec, o_spec, s_spec, t_spec = _gdn_specs(rev)
    dgb_spec = pl.BlockSpec((None, C, SM_W), lambda h, n: (h, rev(n), 0))
    return pl.pallas_call(
        body, name=name, grid=(GDN_HEADS, N),
        in_specs=[q_spec, k_spec, v_spec, gb_spec, o_spec, s_spec, t_spec],
        out_specs=[o_spec, o_spec, o_spec, dgb_spec],
        out_shape=[jax.ShapeDtypeStruct((M, GDN_QK), F32), jax.ShapeDtypeStruct((M, GDN_QK), F32),
                   jax.ShapeDtypeStruct((M, GDN_V), F32), jax.ShapeDtypeStruct((GDN_HEADS, M, SM_W), F32)],
        scratch_shapes=[pltpu.VMEM((GDN_DK, GDN_DV), F32)],
        compiler_params=_params("parallel", "arbitrary"),
    )(act, act, act, gb, do, s_all, t_all)


GLA_STEP_ROWS = 64
GLA_SUB = GLA_STEP_ROWS // GLA_CHUNK


def _gla_cumsum(la):
    C = GLA_CHUNK
    ltri = (lax.broadcasted_iota(jnp.int32, (C, C), 0) >= lax.broadcasted_iota(jnp.int32, (C, C), 1)).astype(F32)
    return _dotx(ltri, la)


def _gla_decay_rows(b, i):
    rj = lax.broadcasted_iota(jnp.int32, (GLA_CHUNK, GLA_DK), 0)
    return jnp.where(rj <= i, jnp.exp(jnp.minimum(b[i:i + 1, :] - b, 0.0)), 0.0)


def _gla_scores_t(q, k, b):
    C = GLA_CHUNK
    lane = lax.broadcasted_iota(jnp.int32, (C, C), 1)
    st = jnp.zeros((C, C), F32)
    for i in range(C):
        si = jnp.sum(q[i:i + 1, :] * k * _gla_decay_rows(b, i), axis=1, keepdims=True)
        st = jnp.where(lane == i, si, st)
    return st


def _gla_specs(n_of):
    R = GLA_STEP_ROWS
    q_spec = pl.BlockSpec((R, GLA_DK), lambda h, n: (n_of(n), C_GQ // GLA_DK + h))
    k_spec = pl.BlockSpec((R, GLA_DK), lambda h, n: (n_of(n), C_GK // GLA_DK + h))
    v_spec = pl.BlockSpec((R, GLA_DV), lambda h, n: (n_of(n), C_GV // GLA_DV + h))
    la_spec = pl.BlockSpec((R, GLA_DK), lambda h, n: (n_of(n), h))
    o_spec = pl.BlockSpec((R, GLA_DV), lambda h, n: (n_of(n), h))
    s_spec = pl.BlockSpec((None, None, GLA_SUB, GLA_DV, GLA_DK), lambda h, n: (h, n_of(n), 0, 0, 0))
    return q_spec, k_spec, v_spec, la_spec, o_spec, s_spec


def _gla_fwd(proj, la, *, name):
    M = proj.shape[0]
    N = M // GLA_STEP_ROWS
    C = GLA_CHUNK

    def body(q_ref, k_ref, v_ref, la_ref, o_ref, s_ref, state):
        n = pl.program_id(1)

        @pl.when(n == 0)
        def _():
            state[...] = jnp.zeros_like(state)

        for c in range(GLA_SUB):
            rows = slice(c * C, (c + 1) * C)
            q = q_ref[rows, :] * (GLA_DK ** -0.5)
            k = k_ref[rows, :]
            v = v_ref[rows, :]
            b = _gla_cumsum(la_ref[rows, :])
            st = state[...]
            s_ref[c] = st
            blast = b[C - 1:C, :]
            o_ref[rows, :] = _dot(q * jnp.exp(b), st, NT) + _dot(_gla_scores_t(q, k, b), v, TN)
            state[...] = st * jnp.exp(blast) + _dot(v, k * jnp.exp(blast - b), TN)

    q_spec, k_spec, v_spec, la_spec, o_spec, s_spec = _gla_specs(lambda n: n)
    return pl.pallas_call(
        body, name=name, grid=(GLA_HEADS, N),
        in_specs=[q_spec, k_spec, v_spec, la_spec], out_specs=[o_spec, s_spec],
        out_shape=[jax.ShapeDtypeStruct((M, GLA_V), F32),
                   jax.ShapeDtypeStruct((GLA_HEADS, N, GLA_SUB, GLA_DV, GLA_DK), F32)],
        scratch_shapes=[pltpu.VMEM((GLA_DV, GLA_DK), F32)],
        compiler_params=_params("parallel", "arbitrary"),
    )(proj, proj, proj, la)


def _gla_bwd(proj, la, do, s_all, *, name):
    M = proj.shape[0]
    N = M // GLA_STEP_ROWS
    C = GLA_CHUNK

    def body(q_ref, k_ref, v_ref, la_ref, do_ref, s_ref, dq_ref, dk_ref, dv_ref, dla_ref, dstate):
        n = pl.program_id(1)

        @pl.when(n == 0)
        def _():
            dstate[...] = jnp.zeros_like(dstate)

        lane = lax.broadcasted_iota(jnp.int32, (C, C), 1)
        ri = lax.broadcasted_iota(jnp.int32, (C, GLA_DK), 0)
        upper = (lax.broadcasted_iota(jnp.int32, (C, C), 0) <= lane).astype(F32)
        for c in reversed(range(GLA_SUB)):
            rows = slice(c * C, (c + 1) * C)
            q = q_ref[rows, :] * (GLA_DK ** -0.5)
            k = k_ref[rows, :]
            v = v_ref[rows, :]
            b = _gla_cumsum(la_ref[rows, :])
            do_ = do_ref[rows, :]
            st = s_ref[c]
            ds1 = dstate[...]
            blast = b[C - 1:C, :]
            eb = jnp.exp(b)
            elast = jnp.exp(blast - b)
            eblast = jnp.exp(blast)
            qe = q * eb
            ke = k * elast
            dqe = _dot(do_, st)
            dsc_t = _dot(v, do_, NT)
            dke = _dot(v, ds1)
            dstate[...] = ds1 * eblast + _dot(do_, qe, TN)
            deblast = jnp.sum(st * ds1, axis=0, keepdims=True)
            sc_t = jnp.zeros((C, C), F32)
            dq_sc = jnp.zeros((C, GLA_DK), F32)
            dk_sc = jnp.zeros((C, GLA_DK), F32)
            for i in range(C):
                f = _gla_decay_rows(b, i)
                kf = k * f
                si = jnp.sum(q[i:i + 1, :] * kf, axis=1, keepdims=True)
                sc_t = jnp.where(lane == i, si, sc_t)
                dsi = jnp.sum(jnp.where(lane == i, dsc_t, 0.0), axis=1, keepdims=True)
                dq_sc = jnp.where(ri == i, jnp.sum(dsi * kf, axis=0, keepdims=True), dq_sc)
                dk_sc = dk_sc + (dsi * f) * q[i:i + 1, :]
            dv_ref[rows, :] = (_dot(sc_t, do_) + _dot(ke, ds1, NT)).astype(dv_ref.dtype)
            dq_ref[rows, :] = ((dq_sc + dqe * eb) * (GLA_DK ** -0.5)).astype(dq_ref.dtype)
            dk_ref[rows, :] = (dk_sc + dke * elast).astype(dk_ref.dtype)
            t_ke = dke * ke
            db = q * dq_sc - k * dk_sc + dqe * qe - t_ke
            db = db + jnp.where(ri == C - 1, jnp.sum(t_ke, axis=0, keepdims=True) + deblast * eblast, 0.0)
            dla_ref[rows, :] = _dotx(upper, db)

    rev = lambda n: N - 1 - n
    q_spec, k_spec, v_spec, la_spec, o_spec, s_spec = _gla_specs(rev)
    return pl.pallas_call(
        body, name=name, grid=(GLA_HEADS, N),
        in_specs=[q_spec, k_spec, v_spec, la_spec, o_spec, s_spec],
        out_specs=[la_spec, la_spec, o_spec, la_spec],
        out_shape=[jax.ShapeDtypeStruct((M, GLA_QK), BF16), jax.ShapeDtypeStruct((M, GLA_QK), BF16),
                   jax.ShapeDtypeStruct((M, GLA_V), BF16), jax.ShapeDtypeStruct((M, GLA_QK), F32)],
        scratch_shapes=[pltpu.VMEM((GLA_DV, GLA_DK), F32)],
        compiler_params=_params("parallel", "arbitrary"),
    )(proj, proj, proj, la, do, s_all)


def _head_norm(o, wn):
    r = lax.rsqrt(jnp.mean(o * o, axis=-1, keepdims=True) + NORM_EPS)
    return o * r, r


def _mix_heads():
    heads = [(0, GDN_DV, hh * GDN_DV, hh * GDN_DV) for hh in range(GDN_HEADS)]
    heads += [(1, GLA_DV, GDN_V + hh * GLA_DV, hh * GLA_DV) for hh in range(GLA_HEADS)]
    return heads


def _mix_fwd(o_gdn, o_gla, proj, wn_gdn, wn_gla, *, name):
    M = proj.shape[0]
    tm = _tile(M, 344, 16)

    def body(og_ref, ol_ref, z_ref, r_ref, wg_ref, wl_ref, m_ref):
        srcs = ((og_ref, z_ref, wg_ref), (ol_ref, r_ref, wl_ref))
        for grp, width, mcol, col in _mix_heads():
            o_ref, gate_ref, w_ref = srcs[grp]
            xhat, _ = _head_norm(o_ref[:, col:col + width], None)
            gate, _ = _silu_and_grad(gate_ref[:, col:col + width])
            m_ref[:, mcol:mcol + width] = (xhat * w_ref[...] * gate).astype(m_ref.dtype)

    full = lambda s: pl.BlockSpec(s, lambda i: (0, 0))
    return pl.pallas_call(
        body, name=name, grid=(M // tm,),
        in_specs=[pl.BlockSpec((tm, GDN_V), lambda i: (i, 0)), pl.BlockSpec((tm, GLA_V), lambda i: (i, 0)),
                  pl.BlockSpec((tm, GDN_V), lambda i: (i, C_Z // GDN_V)),
                  pl.BlockSpec((tm, GLA_V), lambda i: (i, C_GR // GLA_V)),
                  full((1, GDN_DV)), full((1, GLA_DV))],
        out_specs=pl.BlockSpec((tm, D_MODEL), lambda i: (i, 0)),
        out_shape=jax.ShapeDtypeStruct((M, D_MODEL), BF16),
        compiler_params=_params("parallel"),
    )(o_gdn, o_gla, proj, proj, wn_gdn, wn_gla)


def _mix_bwd(o_gdn, o_gla, proj, wn_gdn, wn_gla, dmixed, *, name):
    M = proj.shape[0]
    tm = _tile(M, 344, 16)
    g_ = M // tm

    def body(og_ref, ol_ref, z_ref, r_ref, wg_ref, wl_ref, dm_ref,
             dog_ref, dol_ref, dz_ref, dr_ref, dwg_ref, dwl_ref):
        i = pl.program_id(0)
        srcs = ((og_ref, z_ref, wg_ref, dog_ref, dz_ref), (ol_ref, r_ref, wl_ref, dol_ref, dr_ref))
        dws = [jnp.zeros((1, GDN_DV), F32), jnp.zeros((1, GLA_DV), F32)]
        for grp, width, mcol, col in _mix_heads():
            o_ref, gate_ref, w_ref, do_ref, dgate_ref = srcs[grp]
            cols = slice(col, col + width)
            xhat, r = _head_norm(o_ref[:, cols], None)
            gate, dgate_dc = _silu_and_grad(gate_ref[:, cols])
            dm = dm_ref[:, mcol:mcol + width]
            dgate_ref[:, cols] = (dm * xhat * w_ref[...] * dgate_dc).astype(dgate_ref.dtype)
            dnorm = dm * gate
            dws[grp] = dws[grp] + jnp.sum(dnorm * xhat, axis=0, keepdims=True)
            dxhat = dnorm * w_ref[...]
            do_ref[:, cols] = r * (dxhat - xhat * jnp.mean(dxhat * xhat, axis=-1, keepdims=True))

        @pl.when(i == 0)
        def _():
            dwg_ref[...] = dws[0]
            dwl_ref[...] = dws[1]

        @pl.when(i > 0)
        def _():
            dwg_ref[...] += dws[0]
            dwl_ref[...] += dws[1]

    full = lambda s: pl.BlockSpec(s, lambda i: (0, 0))
    half = pl.BlockSpec((tm, GDN_V), lambda i: (i, 0))
    return pl.pallas_call(
        body, name=name, grid=(g_,),
        in_specs=[half, half, pl.BlockSpec((tm, GDN_V), lambda i: (i, C_Z // GDN_V)),
                  pl.BlockSpec((tm, GLA_V), lambda i: (i, C_GR // GLA_V)),
                  full((1, GDN_DV)), full((1, GLA_DV)), pl.BlockSpec((tm, D_MODEL), lambda i: (i, 0))],
        out_specs=[half, half, half, half, full((1, GDN_DV)), full((1, GLA_DV))],
        out_shape=[jax.ShapeDtypeStruct((M, GDN_V), F32), jax.ShapeDtypeStruct((M, GLA_V), F32),
                   jax.ShapeDtypeStruct((M, GDN_V), BF16), jax.ShapeDtypeStruct((M, GLA_V), BF16),
                   jax.ShapeDtypeStruct((1, GDN_DV), F32), jax.ShapeDtypeStruct((1, GLA_DV), F32)],
        compiler_params=_params("arbitrary"),
    )(o_gdn, o_gla, proj, proj, wn_gdn, wn_gla, dmixed)


def _swiglu_fwd(gate, up, *, name):
    M, F = gate.shape
    tm, tf = _tile(M, 688, 16), _tile(F, 1408, 128)

    def body(g_ref, u_ref, a_ref):
        s, _ = _silu_and_grad(g_ref[...])
        a_ref[...] = (s * u_ref[...]).astype(a_ref.dtype)

    blk = pl.BlockSpec((tm, tf), lambda i, j: (i, j))
    return pl.pallas_call(
        body, name=name, grid=(M // tm, F // tf), in_specs=[blk, blk], out_specs=blk,
        out_shape=jax.ShapeDtypeStruct((M, F), BF16), compiler_params=_params("parallel", "parallel"),
    )(gate, up)


def _swiglu_bwd(gate, up, da, *, name):
    M, F = gate.shape
    tm, tf = _tile(M, 688, 16), _tile(F, 1408, 128)

    def body(g_ref, u_ref, da_ref, dg_ref, du_ref):
        s, ds = _silu_and_grad(g_ref[...])
        da_ = da_ref[...]
        dg_ref[...] = (da_ * u_ref[...] * ds).astype(dg_ref.dtype)
        du_ref[...] = (da_ * s).astype(du_ref.dtype)

    blk = pl.BlockSpec((tm, tf), lambda i, j: (i, j))
    return pl.pallas_call(
        body, name=name, grid=(M // tm, F // tf), in_specs=[blk, blk, blk], out_specs=[blk, blk],
        out_shape=[jax.ShapeDtypeStruct((M, F), BF16), jax.ShapeDtypeStruct((M, F), BF16)],
        compiler_params=_params("parallel", "parallel"),
    )(gate, up, da)


def _adamw(w, g, m, v, *, name):
    shape = w.shape
    cols = shape[-1]
    rows = w.size // cols
    w2, g2, m2, v2 = (t.reshape(rows, cols) for t in (w, g, m, v))
    tr = _tile(rows, 256, 8) if rows % 8 == 0 else rows

    def body(w_ref, g_ref, m_ref, v_ref, d_ref, nm_ref, nv_ref):
        g_ = g_ref[...]
        nm = ADAM_B1 * m_ref[...] + (1.0 - ADAM_B1) * g_
        nv = ADAM_B2 * v_ref[...] + (1.0 - ADAM_B2) * (g_ * g_)
        m_hat = nm / (1.0 - ADAM_B1 ** ADAM_STEP)
        v_hat = nv / (1.0 - ADAM_B2 ** ADAM_STEP)
        d_ref[...] = -ADAM_LR * (m_hat / (jnp.sqrt(v_hat) + ADAM_EPS) + ADAM_WD * w_ref[...])
        nm_ref[...] = nm
        nv_ref[...] = nv

    blk = pl.BlockSpec((tr, cols), lambda i: (i, 0))
    outs = pl.pallas_call(
        body, name=name, grid=(rows // tr,), in_specs=[blk] * 4, out_specs=[blk] * 3,
        out_shape=[jax.ShapeDtypeStruct((rows, cols), F32)] * 3, compiler_params=_params("parallel"),
    )(w2, g2, m2, v2)
    return tuple(t.reshape(shape) for t in outs)


def _sum_slabs(x, *, name):
    _, R, C = x.shape
    tr = _tile(R, 128, 8) if R % 8 == 0 else R

    def body(x_ref, o_ref):
        acc = x_ref[0]
        for s in range(1, N_DEV):
            acc = acc + x_ref[s]
        o_ref[...] = acc

    return pl.pallas_call(
        body, name=name, grid=(R // tr,),
        in_specs=[pl.BlockSpec((N_DEV, tr, C), lambda i: (0, i, 0))], out_specs=pl.BlockSpec((tr, C), lambda i: (i, 0)),
        out_shape=jax.ShapeDtypeStruct((R, C), x.dtype), compiler_params=_params("parallel"),
    )(x)


def _peers():
    x, y, c = lax.axis_index("x"), lax.axis_index("y"), lax.axis_index("c")
    me = 4 * x + 2 * y + c
    peers = []
    for k in range(1, N_DEV):
        px = 1 - x if k & 4 else x
        py = 1 - y if k & 2 else y
        pc = 1 - c if k & 1 else c
        peers.append(((px, py, pc), 4 * px + 2 * py + pc))
    return me, peers


def _exchange(x, *, gather, name):
    slab = x.shape if gather else x.shape[1:]

    def body(x_ref, o_ref, send_sems, recv_sems, own_sem):
        me, peers = _peers()
        own = pltpu.make_async_copy(x_ref if gather else x_ref.at[me], o_ref.at[me], own_sem)
        own.start()
        sends, recvs = [], []
        for k, (pos, idx) in enumerate(peers):
            sends.append(pltpu.make_async_remote_copy(
                src_ref=x_ref if gather else x_ref.at[idx], dst_ref=o_ref.at[me],
                send_sem=send_sems.at[k], recv_sem=recv_sems.at[k],
                device_id=pos, device_id_type=pl.DeviceIdType.MESH))
            recvs.append(pltpu.make_async_remote_copy(
                src_ref=x_ref if gather else x_ref.at[idx], dst_ref=o_ref.at[idx],
                send_sem=send_sems.at[k], recv_sem=recv_sems.at[k],
                device_id=pos, device_id_type=pl.DeviceIdType.MESH))
        for cp in sends:
            cp.start()
        for cp in recvs:
            cp.wait_recv()
        for cp in sends:
            cp.wait_send()
        own.wait()

    hbm = pl.BlockSpec(memory_space=pltpu.HBM)
    return pl.pallas_call(
        body, name=name, in_specs=[hbm], out_specs=hbm,
        out_shape=jax.ShapeDtypeStruct((N_DEV,) + tuple(slab), x.dtype),
        scratch_shapes=[pltpu.SemaphoreType.DMA((N_DEV - 1,)), pltpu.SemaphoreType.DMA((N_DEV - 1,)),
                        pltpu.SemaphoreType.DMA],
    )(x)


def _to_proj_rows(t):
    z = jnp.zeros((SM_W - 2 * GDN_HEADS - GLA_RANK + D_PROJ - C_SM - SM_W,) + t.shape[1:], t.dtype)
    return jnp.concatenate([t[:R_A], t[R_GQ:R_LR], t[R_A:R_GQ], t[R_LR:], z], axis=0)


def _from_proj_rows(t):
    return jnp.concatenate([t[:C_GQ], t[C_SM:C_SM + 2 * GDN_HEADS], t[C_GQ:C_SM],
                            t[C_SM + 2 * GDN_HEADS:C_SM + 2 * GDN_HEADS + GLA_RANK]], axis=0)


def _local_step(x, target, meta, attn_nw, w_in_t, conv_w, a_log, dt_bias, gdn_nw, w2, b2, gla_nw,
                w_out, ffn_nw, w_gate_t, w_up_t, w_down, final_nw):
    S = x.shape[0]
    h0 = jnp.concatenate([jnp.zeros((ROW_PAD, D_MODEL), F32), meta, x], axis=0)
    target_p = jnp.concatenate([jnp.zeros((HEAD_ROWS, D_MODEL), F32), target], axis=0)
    conv_w8 = jnp.concatenate([conv_w, jnp.zeros((8 - CONV_K, conv_w.shape[1]), F32)], axis=0)
    w2p = jnp.zeros((SM_W, GLA_QK), F32).at[2 * GDN_HEADS:2 * GDN_HEADS + GLA_RANK].set(w2)
    alog_p = jnp.zeros((1, SM_W), F32).at[:, :GDN_HEADS].set(a_log)
    dt_p = jnp.zeros((1, SM_W), F32).at[:, :GDN_HEADS].set(dt_bias)

    n1 = _rmsnorm_fwd(h0, attn_nw, name="attn_norm")
    proj = _matmul(n1, w_in_t, mode="nt", name="in_proj")
    gb, la = _gates_fwd(proj, w2p, b2, alog_p, dt_p, name="gates")
    act = _prep_fwd(proj, conv_w8, name="gdn_prep")
    o_gdn, s_gdn, t_gdn = _gdn_fwd(act, gb, name="gdn_fwd")
    o_gla, s_gla = _gla_fwd(proj, la, name="gla_fwd")
    mixed = _mix_fwd(o_gdn, o_gla, proj, gdn_nw, gla_nw, name="mix")
    h1 = _matmul(mixed, w_out, mode="nn", add=h0, name="out_proj")
    n2 = _rmsnorm_fwd(h1, ffn_nw, name="ffn_norm")
    gate = _matmul(n2, w_gate_t, mode="nt", name="ffn_gate")
    up = _matmul(n2, w_up_t, mode="nt", name="ffn_up")
    hid = _swiglu_fwd(gate, up, name="swiglu")
    h2 = _matmul(hid, w_down, mode="nn", add=h1, name="ffn_down")
    dh2, d_final_nw, loss = _loss_head(h2, final_nw, target_p, name="loss_head")

    dh2_b = dh2.astype(BF16)
    d_hid = _matmul(dh2_b, w_down, mode="nt", name="d_hid")
    d_w_down = _matmul(hid, dh2_b, mode="tn", name="d_w_down", tm=1408, tn=1024, tk=1376)
    d_gate, d_up = _swiglu_bwd(gate, up, d_hid, name="d_swiglu")
    d_w_gate_t = _matmul(d_gate, n2, mode="tn", name="d_w_gate", tm=1408, tn=1024, tk=1376)
    d_w_up_t = _matmul(d_up, n2, mode="tn", name="d_w_up", tm=1408, tn=1024, tk=1376)
    d_n2 = _matmul(d_gate, w_gate_t, mode="nn", name="d_n2_gate", tk=1408)
    d_n2 = _matmul(d_up, w_up_t, mode="nn", add=d_n2, name="d_n2_up", tk=1408)
    dh1, d_ffn_nw = _rmsnorm_bwd(h1, ffn_nw, d_n2, dh2, name="d_ffn_norm")

    dh1_b = dh1.astype(BF16)
    d_mixed = _matmul(dh1_b, w_out, mode="nt", name="d_mixed")
    d_w_out = _matmul(mixed, dh1_b, mode="tn", name="d_w_out", tm=1024, tn=1024, tk=1376)
    do_gdn, do_gla, dz, dr, d_gdn_nw, d_gla_nw = _mix_bwd(o_gdn, o_gla, proj, gdn_nw, gla_nw, d_mixed, name="d_mix")
    d_gq, d_gk, d_gv, d_la = _gla_bwd(proj, la, do_gla, s_gla, name="gla_bwd")
    dq, dk, dv, dgb_heads = _gdn_bwd(act, gb, do_gdn, s_gdn, t_gdn, name="gdn_bwd")
    dsm, d_w2p, d_b2, d_alog, d_dt = _gates_bwd(proj, w2p, b2, alog_p, dt_p, dgb_heads, d_la, name="d_gates")
    dc, d_conv_w8 = _prep_bwd_a(proj, conv_w8, jnp.concatenate([dq, dk, dv], axis=1), name="d_gdn_prep")
    d_qkv = _prep_bwd_b(dc, conv_w8, name="d_conv")
    d_proj = jnp.concatenate([d_qkv, dz, d_gq, d_gk, d_gv, dr, dsm,
                              jnp.zeros((S + HEAD_ROWS, D_PROJ - C_SM - SM_W), BF16)], axis=1)
    d_w_in_t = _matmul(d_proj, n1, mode="tn", name="d_w_in", tm=1536, tn=1024, tk=1376)
    d_n1 = _matmul(d_proj, w_in_t, mode="nn", name="d_n1", tk=1536)
    dh0, d_attn_nw = _rmsnorm_bwd(h0, attn_nw, d_n1, dh1, name="d_attn_norm")

    return dict(
        loss=loss[0, 0], grad_x=dh0[HEAD_ROWS:], meta=dh0[ROW_PAD:HEAD_ROWS], attn_nw=d_attn_nw, w_in_t=d_w_in_t,
        conv_w=d_conv_w8[:CONV_K], a_log=d_alog[:, :GDN_HEADS], dt_bias=d_dt[:, :GDN_HEADS], gdn_nw=d_gdn_nw,
        w2=d_w2p[2 * GDN_HEADS:2 * GDN_HEADS + GLA_RANK], b2=d_b2, gla_nw=d_gla_nw, w_out=d_w_out, ffn_nw=d_ffn_nw,
        w_gate_t=d_w_gate_t, w_up_t=d_w_up_t, w_down=d_w_down, final_nw=d_final_nw)


SMALL_ROWS = 32


def kernel(x, meta_tokens, attn_norm_w, w_in, gdn_conv_w, gdn_a_log, gdn_dt_bias, gdn_norm_w, gla_gate_w2, gla_gate_b, gla_norm_w, w_out, ffn_norm_w, w_gate, w_up, w_down, final_norm_w, loss_target, m_meta_tokens, m_attn_norm_w, m_w_in, m_gdn_conv_w, m_gdn_a_log, m_gdn_dt_bias, m_gdn_norm_w, m_gla_gate_w2, m_gla_gate_b, m_gla_norm_w, m_w_out, m_ffn_norm_w, m_w_gate, m_w_up, m_w_down, m_final_norm_w, v_meta_tokens, v_attn_norm_w, v_w_in, v_gdn_conv_w, v_gdn_a_log, v_gdn_dt_bias, v_gdn_norm_w, v_gla_gate_w2, v_gla_gate_b, v_gla_norm_w, v_w_out, v_ffn_norm_w, v_w_gate, v_w_up, v_w_down, v_final_norm_w):
    me = 4 * lax.axis_index("x") + 2 * lax.axis_index("y") + lax.axis_index("c")
    n_in, n_ff, n_out = D_IN // N_DEV, D_FF // N_DEV, D_MODEL // N_DEV

    o1, o2, o3, o4 = n_in, n_in + n_ff, n_in + 2 * n_ff, n_in + 2 * n_ff + n_out
    o5 = o4 + n_ff
    slab_rows = -(-o5 // 128) * 128
    big = jnp.concatenate([w_in[0].T, w_gate[0].T, w_up[0].T, w_out[0], w_down[0],
                           jnp.zeros((slab_rows - o5, D_MODEL), F32)], axis=0).astype(BF16)
    big_all = _exchange(big, gather=True, name="gather_weights")
    w_in_t = _to_proj_rows(big_all[:, :o1].reshape(D_IN, D_MODEL))
    w_gate_t = big_all[:, o1:o2].reshape(D_FF, D_MODEL)
    w_up_t = big_all[:, o2:o3].reshape(D_FF, D_MODEL)
    w_out_f = big_all[:, o3:o4].reshape(D_MODEL, D_MODEL)
    w_down_f = big_all[:, o4:o5].reshape(D_FF, D_MODEL)

    n_conv = gdn_conv_w.shape[2]
    n_w2 = gla_gate_w2.shape[2]
    n_meta = meta_tokens.shape[1]
    small = jnp.zeros((40, n_conv), F32)
    small = small.at[0:N_META, :n_meta].set(meta_tokens)
    small = small.at[N_META:N_META + CONV_K, :].set(gdn_conv_w[0])
    small = small.at[24:24 + GLA_RANK, :n_w2].set(gla_gate_w2[0])
    small_all = _exchange(small, gather=True, name="gather_small")
    meta_f = small_all[:, 0:N_META, :n_meta].transpose(1, 0, 2).reshape(N_META, D_MODEL)
    conv_f = small_all[:, N_META:N_META + CONV_K, :].transpose(1, 0, 2).reshape(CONV_K, N_DEV * n_conv)
    w2_f = small_all[:, 24:24 + GLA_RANK, :n_w2].transpose(1, 0, 2).reshape(GLA_RANK, N_DEV * n_w2)

    g = _local_step(x[0], loss_target[0], meta_f, attn_norm_w, w_in_t, conv_f, gdn_a_log, gdn_dt_bias,
                    gdn_norm_w, w2_f, gla_gate_b, gla_norm_w, w_out_f, ffn_norm_w, w_gate_t, w_up_t, w_down_f,
                    final_norm_w.reshape(1, D_MODEL))

    parts = jnp.concatenate([
        _from_proj_rows(g["w_in_t"]).reshape(N_DEV, n_in, D_MODEL), g["w_gate_t"].reshape(N_DEV, n_ff, D_MODEL),
        g["w_up_t"].reshape(N_DEV, n_ff, D_MODEL), g["w_out"].reshape(N_DEV, n_out, D_MODEL),
        g["w_down"].reshape(N_DEV, n_ff, D_MODEL), jnp.zeros((N_DEV, slab_rows - o5, D_MODEL), F32)], axis=1)
    big_grad = _sum_slabs(_exchange(parts, gather=False, name="scatter_grads"), name="sum_grads")
    grad_w_in = big_grad[:o1].T[None]
    grad_w_gate = big_grad[o1:o2].T[None]
    grad_w_up = big_grad[o2:o3].T[None]
    grad_w_out = big_grad[o3:o4][None]
    grad_w_down = big_grad[o4:o5][None]

    misc = jnp.concatenate([g["a_log"], g["dt_bias"], g["gdn_nw"], g["gla_nw"], g["b2"], g["loss"].reshape(1, 1)], axis=1)
    n_misc = misc.shape[1]
    misc = jnp.pad(misc, ((0, 0), (0, D_MODEL - n_misc)))
    rows = jnp.concatenate([g["attn_nw"], g["ffn_nw"], g["final_nw"], misc, g["meta"],
                            g["conv_w"].reshape(-1, D_MODEL), g["w2"].reshape(-1, D_MODEL)], axis=0)
    rows = jnp.pad(rows, ((0, SMALL_ROWS - rows.shape[0]), (0, 0)))
    tot = _sum_slabs(_exchange(rows, gather=True, name="gather_small_grads"), name="sum_small_grads")
    grad_attn_nw, grad_ffn_nw, grad_final_nw = tot[0:1], tot[1:2], tot[2]
    grad_a_log = tot[3:4, 0:8]
    grad_dt = tot[3:4, 8:16]
    grad_gdn_nw = tot[3:4, 16:16 + GDN_DV]
    grad_gla_nw = tot[3:4, 144:144 + GLA_DV]
    grad_b2 = tot[3:4, 400:400 + GLA_QK]
    loss = tot[3, n_misc - 1]
    r0 = 4 + N_META
    grad_meta = lax.dynamic_slice(tot[4:r0], (0, me * n_meta), (N_META, n_meta))
    r1 = r0 + CONV_K * N_DEV * n_conv // D_MODEL
    grad_conv = lax.dynamic_slice(tot[r0:r1].reshape(CONV_K, N_DEV * n_conv), (0, me * n_conv), (CONV_K, n_conv))[None]
    r2 = r1 + GLA_RANK * N_DEV * n_w2 // D_MODEL
    grad_w2 = lax.dynamic_slice(tot[r1:r2].reshape(GLA_RANK, N_DEV * n_w2), (0, me * n_w2), (GLA_RANK, n_w2))[None]

    weights = [meta_tokens, attn_norm_w, w_in, gdn_conv_w, gdn_a_log, gdn_dt_bias, gdn_norm_w, gla_gate_w2,
               gla_gate_b, gla_norm_w, w_out, ffn_norm_w, w_gate, w_up, w_down, final_norm_w]
    grads = [grad_meta, grad_attn_nw, grad_w_in, grad_conv, grad_a_log, grad_dt, grad_gdn_nw, grad_w2,
             grad_b2, grad_gla_nw, grad_w_out, grad_ffn_nw, grad_w_gate, grad_w_up, grad_w_down, grad_final_nw]
    ms = [m_meta_tokens, m_attn_norm_w, m_w_in, m_gdn_conv_w, m_gdn_a_log, m_gdn_dt_bias, m_gdn_norm_w,
          m_gla_gate_w2, m_gla_gate_b, m_gla_norm_w, m_w_out, m_ffn_norm_w, m_w_gate, m_w_up, m_w_down, m_final_norm_w]
    vs = [v_meta_tokens, v_attn_norm_w, v_w_in, v_gdn_conv_w, v_gdn_a_log, v_gdn_dt_bias, v_gdn_norm_w,
          v_gla_gate_w2, v_gla_gate_b, v_gla_norm_w, v_w_out, v_ffn_norm_w, v_w_gate, v_w_up, v_w_down, v_final_norm_w]
    grads = [gr.reshape(w.shape) for gr, w in zip(grads, weights)]
    deltas, new_ms, new_vs = [], [], []
    for idx, (w, gr, m, v) in enumerate(zip(weights, grads, ms, vs)):
        d, nm, nv = _adamw(w, gr, m, v, name=f"adamw_{idx}")
        deltas.append(d)
        new_ms.append(nm)
        new_vs.append(nv)
    return (loss, g["grad_x"][None], *grads, *deltas, *new_ms, *new_vs)
```

```python
import functools

import jax
import jax.numpy as jnp
from jax import lax
from jax.experimental import pallas as pl
from jax.experimental.pallas import tpu as pltpu

F32 = jnp.float32
BF16 = jnp.bfloat16
_MXU_DTYPE = jnp.bfloat16

D_MODEL = 2048
N_META = 16
ROW_PAD = 48
HEAD_ROWS = ROW_PAD + N_META
CONV_K = 4
GDN_HEADS, GDN_DK, GDN_DV, GDN_CHUNK = 8, 128, 128, 64
GLA_HEADS, GLA_DK, GLA_DV, GLA_CHUNK = 4, 128, 256, 16
GLA_RANK = 16
GLA_GATE_NORMALIZER = 16.0
GDN_QK = GDN_HEADS * GDN_DK
GDN_V = GDN_HEADS * GDN_DV
GLA_QK = GLA_HEADS * GLA_DK
GLA_V = GLA_HEADS * GLA_DV
D_FF = 5632
D_IN = 7200
NORM_EPS = 1e-6
C_QKV, C_Z, C_GQ, C_GK, C_GV, C_GR, C_SM = 0, 3072, 4096, 4608, 5120, 6144, 7168
SM_W = 128
D_PROJ = 7680
R_Z, R_A, R_B, R_GQ, R_GK, R_GV, R_GR, R_LR = 3072, 4096, 4104, 4112, 4624, 5136, 6160, 7184

ADAM_LR, ADAM_B1, ADAM_B2, ADAM_EPS, ADAM_WD, ADAM_STEP = 0.001, 0.9, 0.999, 1e-08, 0.01, 10

N_DEV = 8
VMEM_LIMIT = 56 * 1024 * 1024

NN = (((1,), (0,)), ((), ()))
NT = (((1,), (1,)), ((), ()))
TN = (((0,), (0,)), ((), ()))


def _dot(a, b, dims=NN):
    return lax.dot_general(a.astype(_MXU_DTYPE), b.astype(_MXU_DTYPE), dims, preferred_element_type=F32)


def _dotx(a, b, dims=NN):
    return lax.dot_general(a, b, dims, precision=lax.Precision.HIGHEST, preferred_element_type=F32)


def _dot3(a, b):
    ah = a.astype(BF16)
    al = (a - ah.astype(F32)).astype(BF16)
    bh = b.astype(BF16)
    bl = (b - bh.astype(F32)).astype(BF16)
    d = functools.partial(lax.dot_general, dimension_numbers=NN, preferred_element_type=F32)
    return d(ah, bh) + (d(ah, bl) + d(al, bh))


def _tile(n, target, mult=8):
    best = None
    for t in range(mult, min(n, target) + 1, mult):
        if n % t == 0:
            best = t
    return best if best is not None else n


def _params(*sem):
    return pltpu.CompilerParams(dimension_semantics=sem, vmem_limit_bytes=VMEM_LIMIT)


def _sigmoid(x):
    return 1.0 / (1.0 + jnp.exp(-x))


def _softplus(x):
    return jnp.maximum(x, 0.0) + jnp.log1p(jnp.exp(-jnp.abs(x)))


def _silu_and_grad(c):
    s = _sigmoid(c)
    return c * s, s * (1.0 + c * (1.0 - s))


def _matmul(a, b, *, mode, name, out_dtype=F32, add=None, tm=1376, tn=512, tk=2064):
    if mode == "tn":
        K, M = a.shape
        N = b.shape[1]
    else:
        M, K = a.shape
        N = b.shape[0] if mode == "nt" else b.shape[1]
    tm = _tile(M, tm, 128 if mode == "tn" else 16)
    tn = _tile(N, tn, 128)
    tk = _tile(K, tk, 16 if mode == "tn" else 128)
    gm, gn, gk = M // tm, N // tn, K // tk
    dims = {"nn": NN, "nt": NT, "tn": TN}[mode]

    def body(*refs):
        if add is None:
            a_ref, b_ref, o_ref = refs[:3]
            add_ref = None
        else:
            a_ref, b_ref, add_ref, o_ref = refs[:4]
        p = _dot(a_ref[...], b_ref[...], dims)

        def finish(r):
            if add_ref is not None:
                r = r + add_ref[...]
            o_ref[...] = r.astype(out_dtype)

        if gk == 1:
            finish(p)
        else:
            acc_ref = refs[-1]
            k = pl.program_id(2)

            @pl.when(k == 0)
            def _():
                acc_ref[...] = p

            @pl.when(k > 0)
            def _():
                acc_ref[...] += p

            @pl.when(k == gk - 1)
            def _():
                finish(acc_ref[...])

    if mode == "tn":
        a_spec = pl.BlockSpec((tk, tm), lambda i, j, k: (k, i))
    else:
        a_spec = pl.BlockSpec((tm, tk), lambda i, j, k: (i, k))
    if mode == "nt":
        b_spec = pl.BlockSpec((tn, tk), lambda i, j, k: (j, k))
    else:
        b_spec = pl.BlockSpec((tk, tn), lambda i, j, k: (k, j))
    o_spec = pl.BlockSpec((tm, tn), lambda i, j, k: (i, j))
    in_specs = [a_spec, b_spec] + ([o_spec] if add is not None else [])
    args = (a, b) + ((add,) if add is not None else ())
    return pl.pallas_call(
        body, name=name, grid=(gm, gn, gk), in_specs=in_specs, out_specs=o_spec,
        out_shape=jax.ShapeDtypeStruct((M, N), out_dtype),
        scratch_shapes=[pltpu.VMEM((tm, tn), F32)] if gk > 1 else [],
        compiler_params=_params("parallel", "parallel", "arbitrary"),
    )(*args)


def _rmsnorm_fwd(h, w, *, name):
    M, D = h.shape
    tm = _tile(M, 688, 16)

    def body(h_ref, w_ref, n_ref):
        x = h_ref[...]
        r = lax.rsqrt(jnp.mean(x * x, axis=-1, keepdims=True) + NORM_EPS)
        n_ref[...] = (x * r * w_ref[...]).astype(n_ref.dtype)

    return pl.pallas_call(
        body, name=name, grid=(M // tm,),
        in_specs=[pl.BlockSpec((tm, D), lambda i: (i, 0)), pl.BlockSpec((1, D), lambda i: (0, 0))],
        out_specs=pl.BlockSpec((tm, D), lambda i: (i, 0)),
        out_shape=jax.ShapeDtypeStruct((M, D), BF16),
        compiler_params=_params("parallel"),
    )(h, w)


def _rmsnorm_bwd(h, w, dn, dres, *, name):
    M, D = h.shape
    tm = _tile(M, 344, 8)
    g = M // tm

    def body(h_ref, w_ref, dn_ref, dres_ref, dh_ref, dw_ref, acc_ref):
        i = pl.program_id(0)
        x = h_ref[...]
        r = lax.rsqrt(jnp.mean(x * x, axis=-1, keepdims=True) + NORM_EPS)
        xhat = x * r
        dn_ = dn_ref[...]
        dxhat = dn_ * w_ref[...]
        dh_ref[...] = dres_ref[...] + r * (dxhat - xhat * jnp.mean(dxhat * xhat, axis=-1, keepdims=True))
        part = jnp.sum((dn_ * xhat).reshape(tm // 8, 8, D), axis=0)

        @pl.when(i == 0)
        def _():
            acc_ref[...] = part

        @pl.when(i > 0)
        def _():
            acc_ref[...] += part

        @pl.when(i == g - 1)
        def _():
            dw_ref[...] = jnp.sum(acc_ref[...], axis=0, keepdims=True)

    row = pl.BlockSpec((tm, D), lambda i: (i, 0))
    vec = pl.BlockSpec((1, D), lambda i: (0, 0))
    return pl.pallas_call(
        body, name=name, grid=(g,), in_specs=[row, vec, row, row], out_specs=[row, vec],
        out_shape=[jax.ShapeDtypeStruct((M, D), F32), jax.ShapeDtypeStruct((1, D), F32)],
        scratch_shapes=[pltpu.VMEM((8, D), F32)],
        compiler_params=_params("arbitrary"),
    )(h, w, dn, dres)


def _loss_head(h, w, target_p, *, name):
    M, D = h.shape
    tm = _tile(M, 344, 8)
    g = M // tm

    def body(h_ref, w_ref, t_ref, dh_ref, dw_ref, loss_ref, acc_ref, lacc_ref):
        i = pl.program_id(0)
        x = h_ref[...]
        row = i * tm + lax.broadcasted_iota(jnp.int32, (tm, 1), 0)
        live = row >= HEAD_ROWS
        r = lax.rsqrt(jnp.mean(x * x, axis=-1, keepdims=True) + NORM_EPS)
        xhat = x * r
        err = jnp.where(live, xhat * w_ref[...] - t_ref[...], 0.0)
        dy = err * (1.0 / D)
        dxhat = dy * w_ref[...]
        dh_ref[...] = r * (dxhat - xhat * jnp.mean(dxhat * xhat, axis=-1, keepdims=True))
        part = jnp.sum((dy * xhat).reshape(tm // 8, 8, D), axis=0)
        lpart = jnp.sum((err * err).reshape(tm // 8, 8, D), axis=0)

        @pl.when(i == 0)
        def _():
            acc_ref[...] = part
            lacc_ref[...] = lpart

        @pl.when(i > 0)
        def _():
            acc_ref[...] += part
            lacc_ref[...] += lpart

        @pl.when(i == g - 1)
        def _():
            dw_ref[...] = jnp.sum(acc_ref[...], axis=0, keepdims=True)
            tot = jnp.sum(jnp.sum(lacc_ref[...], axis=0, keepdims=True), axis=1, keepdims=True)
            loss_ref[...] = jnp.broadcast_to(tot * (0.5 / D), (1, 128))

    row = pl.BlockSpec((tm, D), lambda i: (i, 0))
    vec = pl.BlockSpec((1, D), lambda i: (0, 0))
    return pl.pallas_call(
        body, name=name, grid=(g,), in_specs=[row, vec, row],
        out_specs=[row, vec, pl.BlockSpec((1, 128), lambda i: (0, 0))],
        out_shape=[jax.ShapeDtypeStruct((M, D), F32), jax.ShapeDtypeStruct((1, D), F32),
                   jax.ShapeDtypeStruct((1, 128), F32)],
        scratch_shapes=[pltpu.VMEM((8, D), F32), pltpu.VMEM((8, D), F32)],
        compiler_params=_params("arbitrary"),
    )(h, w, target_p)


def _gate_terms(sm, w2p, b2, alog_p, dt_p, row0):
    tm = sm.shape[0]
    lane = lax.broadcasted_iota(jnp.int32, (tm, SM_W), 1)
    live = (row0 + lax.broadcasted_iota(jnp.int32, (tm, 1), 0)) >= ROW_PAD
    pre = sm + dt_p
    neg_a = -jnp.exp(alog_p)
    g = neg_a * _softplus(pre)
    beta = _sigmoid(sm)
    z = _dot(sm, w2p) + b2
    return lane, live, pre, neg_a, g, beta, z


def _gates_fwd(proj, w2p, b2, alog_p, dt_p, *, name):
    M = proj.shape[0]
    tm = _tile(M, 688, 8)

    def body(sm_ref, w2_ref, b2_ref, al_ref, dt_ref, gb_ref, la_ref):
        row0 = pl.program_id(0) * tm
        lane, live, _, _, g, beta, z = _gate_terms(sm_ref[...], w2_ref[...], b2_ref[...], al_ref[...], dt_ref[...], row0)
        gb = jnp.where(lane < GDN_HEADS, g, jnp.where(lane < 2 * GDN_HEADS, beta, 0.0))
        gb_ref[...] = jnp.where(live, gb, 0.0)
        la = (jnp.minimum(z, 0.0) - jnp.log1p(jnp.exp(-jnp.abs(z)))) * (1.0 / GLA_GATE_NORMALIZER)
        la_ref[...] = jnp.where(live, la, 0.0)

    full = lambda s: pl.BlockSpec(s, lambda i: (0, 0))
    return pl.pallas_call(
        body, name=name, grid=(M // tm,),
        in_specs=[pl.BlockSpec((tm, SM_W), lambda i: (i, C_SM // SM_W)), full((SM_W, GLA_QK)), full((1, GLA_QK)),
                  full((1, SM_W)), full((1, SM_W))],
        out_specs=[pl.BlockSpec((tm, SM_W), lambda i: (i, 0)), pl.BlockSpec((tm, GLA_QK), lambda i: (i, 0))],
        out_shape=[jax.ShapeDtypeStruct((M, SM_W), F32), jax.ShapeDtypeStruct((M, GLA_QK), F32)],
        compiler_params=_params("parallel"),
    )(proj, w2p, b2, alog_p, dt_p)


def _gates_bwd(proj, w2p, b2, alog_p, dt_p, dgb_heads, dla, *, name):
    M = proj.shape[0]
    tm = _tile(M, 688, 8)
    g_ = M // tm

    def body(sm_ref, w2_ref, b2_ref, al_ref, dt_ref, dgb_ref, dla_ref,
             dsm_ref, dw2_ref, db2_ref, dal_ref, ddt_ref):
        i = pl.program_id(0)
        sm = sm_ref[...]
        lane, live, pre, neg_a, g, beta, z = _gate_terms(sm, w2_ref[...], b2_ref[...], al_ref[...], dt_ref[...], i * tm)
        dz = jnp.where(live, dla_ref[...] * (_sigmoid(-z) * (1.0 / GLA_GATE_NORMALIZER)), 0.0)
        dsm_lr = _dot(dz, w2_ref[...], NT)
        dgb = dgb_ref[0]
        for hh in range(1, GDN_HEADS):
            dgb = dgb + dgb_ref[hh]
        dgb = jnp.where(live, dgb, 0.0)
        da = dgb * neg_a * _sigmoid(pre)
        db = dgb * beta * (1.0 - beta)
        dsm = jnp.where(lane < GDN_HEADS, da, jnp.where(lane < 2 * GDN_HEADS, db, dsm_lr))
        dsm_ref[...] = dsm.astype(dsm_ref.dtype)
        is_a = lane < GDN_HEADS
        dal = jnp.sum(jnp.where(is_a, dgb * g, 0.0), axis=0, keepdims=True)
        ddt = jnp.sum(jnp.where(is_a, da, 0.0), axis=0, keepdims=True)
        dw2 = _dot(sm, dz, TN)
        db2 = jnp.sum(dz, axis=0, keepdims=True)

        @pl.when(i == 0)
        def _():
            dw2_ref[...] = dw2
            db2_ref[...] = db2
            dal_ref[...] = dal
            ddt_ref[...] = ddt

        @pl.when(i > 0)
        def _():
            dw2_ref[...] += dw2
            db2_ref[...] += db2
            dal_ref[...] += dal
            ddt_ref[...] += ddt

    full = lambda s: pl.BlockSpec(s, lambda i: (0, 0))
    return pl.pallas_call(
        body, name=name, grid=(g_,),
        in_specs=[pl.BlockSpec((tm, SM_W), lambda i: (i, C_SM // SM_W)), full((SM_W, GLA_QK)), full((1, GLA_QK)),
                  full((1, SM_W)), full((1, SM_W)),
                  pl.BlockSpec((GDN_HEADS, tm, SM_W), lambda i: (0, i, 0)),
                  pl.BlockSpec((tm, GLA_QK), lambda i: (i, 0))],
        out_specs=[pl.BlockSpec((tm, SM_W), lambda i: (i, 0)), full((SM_W, GLA_QK)), full((1, GLA_QK)),
                   full((1, SM_W)), full((1, SM_W))],
        out_shape=[jax.ShapeDtypeStruct((M, SM_W), BF16), jax.ShapeDtypeStruct((SM_W, GLA_QK), F32),
                   jax.ShapeDtypeStruct((1, GLA_QK), F32), jax.ShapeDtypeStruct((1, SM_W), F32),
                   jax.ShapeDtypeStruct((1, SM_W), F32)],
        compiler_params=_params("arbitrary"),
    )(proj, w2p, b2, alog_p, dt_p, dgb_heads, dla)


N_QKV_BLOCKS = (2 * GDN_QK + GDN_V) // 128
HALO = 8


def _conv_terms(x_ref, halo_ref, cw_ref, xs_ref, i, tm):
    xs_ref[HALO:HALO + tm, :] = x_ref[...]
    xs_ref[0:HALO, :] = jnp.where(i > 0, halo_ref[...], 0.0)
    cw = cw_ref[...]
    taps = [xs_ref[HALO - (CONV_K - 1) + t:HALO - (CONV_K - 1) + t + tm, :] for t in range(CONV_K)]
    c = taps[0] * cw[0:1, :]
    for t in range(1, CONV_K):
        c = c + taps[t] * cw[t:t + 1, :]
    return c, taps


def _prep_fwd(proj, conv_w8, *, name):
    M = proj.shape[0]
    tm = _tile(M, 688, 8)

    def body(x_ref, halo_ref, cw_ref, o_ref, xs_ref):
        j, i = pl.program_id(0), pl.program_id(1)
        c, _ = _conv_terms(x_ref, halo_ref, cw_ref, xs_ref, i, tm)
        s, _ = _silu_and_grad(c)
        r = lax.rsqrt(jnp.sum(s * s, axis=-1, keepdims=True) + NORM_EPS)
        scale = jnp.where(j < GDN_HEADS, GDN_DK ** -0.5, 1.0)
        o_ref[...] = jnp.where(j < 2 * GDN_HEADS, s * (r * scale), s)

    hb = tm // HALO
    return pl.pallas_call(
        body, name=name, grid=(N_QKV_BLOCKS, M // tm),
        in_specs=[pl.BlockSpec((tm, 128), lambda j, i: (i, j)),
                  pl.BlockSpec((HALO, 128), lambda j, i: (jnp.maximum(i * hb - 1, 0), j)),
                  pl.BlockSpec((8, 128), lambda j, i: (0, j))],
        out_specs=pl.BlockSpec((tm, 128), lambda j, i: (i, j)),
        out_shape=jax.ShapeDtypeStruct((M, N_QKV_BLOCKS * 128), F32),
        scratch_shapes=[pltpu.VMEM((tm + HALO, 128), F32)],
        compiler_params=_params("parallel", "arbitrary"),
    )(proj, proj, conv_w8)


def _prep_bwd_a(proj, conv_w8, dact, *, name):
    M = proj.shape[0]
    tm = _tile(M, 688, 8)
    g_ = M // tm

    def body(x_ref, halo_ref, cw_ref, da_ref, dc_ref, dcw_ref, xs_ref):
        j, i = pl.program_id(0), pl.program_id(1)
        c, taps = _conv_terms(x_ref, halo_ref, cw_ref, xs_ref, i, tm)
        s, ds_dc = _silu_and_grad(c)
        r = lax.rsqrt(jnp.sum(s * s, axis=-1, keepdims=True) + NORM_EPS)
        scale = jnp.where(j < GDN_HEADS, GDN_DK ** -0.5, 1.0)
        da = da_ref[...]
        y = s * r
        dy = da * scale
        ds_norm = r * (dy - y * jnp.sum(dy * y, axis=-1, keepdims=True))
        ds = jnp.where(j < 2 * GDN_HEADS, ds_norm, da)
        dc = ds * ds_dc
        dc_ref[...] = dc
        r8 = lax.broadcasted_iota(jnp.int32, (8, 128), 0)
        part = jnp.zeros((8, 128), F32)
        for t in range(CONV_K):
            part = jnp.where(r8 == t, jnp.sum(dc * taps[t], axis=0, keepdims=True), part)

        @pl.when(i == 0)
        def _():
            dcw_ref[...] = part

        @pl.when(i > 0)
        def _():
            dcw_ref[...] += part

    hb = tm // HALO
    blk = pl.BlockSpec((tm, 128), lambda j, i: (i, j))
    return pl.pallas_call(
        body, name=name, grid=(N_QKV_BLOCKS, g_),
        in_specs=[blk, pl.BlockSpec((HALO, 128), lambda j, i: (jnp.maximum(i * hb - 1, 0), j)),
                  pl.BlockSpec((8, 128), lambda j, i: (0, j)), blk],
        out_specs=[blk, pl.BlockSpec((8, 128), lambda j, i: (0, j))],
        out_shape=[jax.ShapeDtypeStruct((M, N_QKV_BLOCKS * 128), F32),
                   jax.ShapeDtypeStruct((8, N_QKV_BLOCKS * 128), F32)],
        scratch_shapes=[pltpu.VMEM((tm + HALO, 128), F32)],
        compiler_params=_params("parallel", "arbitrary"),
    )(proj, proj, conv_w8, dact)


def _prep_bwd_b(dc, conv_w8, *, name):
    M = dc.shape[0]
    tm = _tile(M, 688, 8)
    g_ = M // tm

    def body(d_ref, halo_ref, cw_ref, o_ref, ds_ref):
        i = pl.program_id(1)
        ds_ref[0:tm, :] = d_ref[...]
        ds_ref[tm:tm + HALO, :] = jnp.where(i < g_ - 1, halo_ref[...], 0.0)
        cw = cw_ref[...]
        acc = ds_ref[CONV_K - 1:CONV_K - 1 + tm, :] * cw[0:1, :]
        for t in range(1, CONV_K):
            acc = acc + ds_ref[CONV_K - 1 - t:CONV_K - 1 - t + tm, :] * cw[t:t + 1, :]
        o_ref[...] = acc.astype(o_ref.dtype)

    hb = tm // HALO
    last = M // HALO - 1
    blk = pl.BlockSpec((tm, 128), lambda j, i: (i, j))
    return pl.pallas_call(
        body, name=name, grid=(N_QKV_BLOCKS, g_),
        in_specs=[blk, pl.BlockSpec((HALO, 128), lambda j, i: (jnp.minimum((i + 1) * hb, last), j)),
                  pl.BlockSpec((8, 128), lambda j, i: (0, j))],
        out_specs=blk,
        out_shape=jax.ShapeDtypeStruct((M, N_QKV_BLOCKS * 128), BF16),
        scratch_shapes=[pltpu.VMEM((tm + HALO, 128), F32)],
        compiler_params=_params("parallel", "arbitrary"),
    )(dc, dc, conv_w8)


def _round_robin(gens):
    gens = list(gens)
    while gens:
        alive = []
        for gen in gens:
            try:
                next(gen)
                alive.append(gen)
            except StopIteration:
                pass
        gens = alive


def _unit_lower_inverse(a_low, eye):
    b = -a_low
    x = eye + b
    pw = b
    for _ in range(5):
        pw = _dot3(pw, pw)
        yield
        x = x + _dot3(x, pw)
        yield
    return x


class _GdnChunk:
    def build(self, q, k, v, gb, h):
        C = GDN_CHUNK
        lane = lax.broadcasted_iota(jnp.int32, (C, SM_W), 1)
        g = jnp.sum(jnp.where(lane == h, gb, 0.0), axis=1, keepdims=True)
        self.beta = jnp.sum(jnp.where(lane == h + GDN_HEADS, gb, 0.0), axis=1, keepdims=True)
        ri = lax.broadcasted_iota(jnp.int32, (C, C), 0)
        ci = lax.broadcasted_iota(jnp.int32, (C, C), 1)
        self.causal = ri >= ci
        self.strict = ri > ci
        self.eye = (ri == ci).astype(F32)
        gcb = _dotx(self.causal.astype(F32), jnp.broadcast_to(g, (C, SM_W)))
        yield
        self.gcol = gcb[:, 0:1]
        grow = gcb.T[0:1, 0:C]
        self.decay = jnp.exp(jnp.where(self.causal, self.gcol - grow, -1e30))
        self.egc = jnp.exp(self.gcol)
        glast = gcb[C - 1:C, 0:1]
        self.elast = jnp.exp(glast - self.gcol)
        self.gl = jnp.exp(glast)
        self.q, self.k, self.v = q, k, v
        self.kb = k * self.beta
        m = _dot(self.kb, k, NT)
        n_ = _dot(q, k, NT)
        yield
        self.a_low = jnp.where(self.strict, m * self.decay, 0.0)
        self.p = n_ * self.decay
        self.qd = q * self.egc
        self.kd = k * self.elast
        self.bu = v * self.beta
        self.bw = self.kb * self.egc


GDN_HB = 8
GDN_HG = GDN_HEADS // GDN_HB


def _gdn_specs(n_of):
    C, W = GDN_CHUNK, 128 * GDN_HB
    q_spec = pl.BlockSpec((C, W), lambda g, n: (n_of(n), g))
    k_spec = pl.BlockSpec((C, W), lambda g, n: (n_of(n), g + GDN_HG))
    v_spec = pl.BlockSpec((C, W), lambda g, n: (n_of(n), g + 2 * GDN_HG))
    gb_spec = pl.BlockSpec((C, SM_W), lambda g, n: (n_of(n), 0))
    o_spec = pl.BlockSpec((C, W), lambda g, n: (n_of(n), g))
    s_spec = pl.BlockSpec((GDN_HB, None, GDN_DK, GDN_DV), lambda g, n: (g, n_of(n), 0, 0))
    t_spec = pl.BlockSpec((GDN_HB, None, C, C), lambda g, n: (g, n_of(n), 0, 0))
    return q_spec, k_spec, v_spec, gb_spec, o_spec, s_spec, t_spec


def _gdn_fwd(act, gb, *, name):
    M = act.shape[0]
    N = M // GDN_CHUNK

    def body(q_ref, k_ref, v_ref, gb_ref, o_ref, s_ref, t_ref, state):
        g, n = pl.program_id(0), pl.program_id(1)

        @pl.when(n == 0)
        def _():
            state[...] = jnp.zeros_like(state)

        gb_ = gb_ref[...]

        def head(hh):
            cols = slice(hh * 128, (hh + 1) * 128)
            c = _GdnChunk()
            yield from c.build(q_ref[:, cols], k_ref[:, cols], v_ref[:, cols], gb_, g * GDN_HB + hh)
            tinv = yield from _unit_lower_inverse(c.a_low, c.eye)
            s = state[hh]
            s_ref[hh] = s
            t_ref[hh] = tinv
            u = _dot(tinv, c.bu)
            w = _dot(tinv, c.bw)
            yield
            vn = u - _dot(w, s)
            o1 = _dot(c.qd, s)
            yield
            o_ref[:, cols] = o1 + _dot(c.p, vn)
            state[hh] = c.gl * s + _dot(c.kd, vn, TN)

        _round_robin(head(hh) for hh in range(GDN_HB))

    q_spec, k_spec, v_spec, gb_spec, o_spec, s_spec, t_spec = _gdn_specs(lambda n: n)
    return pl.pallas_call(
        body, name=name, grid=(GDN_HG, N),
        in_specs=[q_spec, k_spec, v_spec, gb_spec], out_specs=[o_spec, s_spec, t_spec],
        out_shape=[jax.ShapeDtypeStruct((M, GDN_V), F32),
                   jax.ShapeDtypeStruct((GDN_HEADS, N, GDN_DK, GDN_DV), F32),
                   jax.ShapeDtypeStruct((GDN_HEADS, N, GDN_CHUNK, GDN_CHUNK), F32)],
        scratch_shapes=[pltpu.VMEM((GDN_HB, GDN_DK, GDN_DV), F32)],
        compiler_params=_params("parallel", "arbitrary"),
    )(act, act, act, gb)


def _gdn_bwd(act, gb, do, s_all, t_all, *, name):
    M = act.shape[0]
    N = M // GDN_CHUNK
    C = GDN_CHUNK

    def body(q_ref, k_ref, v_ref, gb_ref, do_ref, s_ref, t_ref, dq_ref, dk_ref, dv_ref, dgb_ref, dstate):
        g, n = pl.program_id(0), pl.program_id(1)

        @pl.when(n == 0)
        def _():
            dstate[...] = jnp.zeros_like(dstate)

        gb_ = gb_ref[...]
        last = lax.broadcasted_iota(jnp.int32, (C, 1), 0) == C - 1
        upper = (lax.broadcasted_iota(jnp.int32, (C, C), 0) <= lax.broadcasted_iota(jnp.int32, (C, C), 1)).astype(F32)
        lane = lax.broadcasted_iota(jnp.int32, (C, SM_W), 1)
        def head(hh):
            cols = slice(hh * 128, (hh + 1) * 128)
            h = g * GDN_HB + hh
            c = _GdnChunk()
            yield from c.build(q_ref[:, cols], k_ref[:, cols], v_ref[:, cols], gb_, h)
            tinv = t_ref[hh]
            s = s_ref[hh]
            do_ = do_ref[:, cols]
            ds1 = dstate[hh]
            u = _dot(tinv, c.bu)
            w = _dot(tinv, c.bw)
            dqd = _dot(do_, s, NT)
            dvn0 = _dot(c.p, do_, TN) + _dot(c.kd, ds1)
            dst0 = _dot(c.qd, do_, TN) + c.gl * ds1
            yield
            vn = u - _dot(w, s)
            dvn = dvn0
            yield
            dp = jnp.where(c.causal, _dot(do_, vn, NT), 0.0)
            dstate[hh] = dst0 - _dot(w, dvn, TN)
            dkd = _dot(vn, ds1, NT)
            dw = -_dot(dvn, s, NT)
            dbu = _dot(tinv, dvn, TN)
            dgl = jnp.sum(jnp.sum(s * ds1, axis=1, keepdims=True), axis=0, keepdims=True)
            yield
            dbw = _dot(tinv, dw, TN)
            t1 = _dot(dbu, u, NT)
            yield
            da = jnp.where(c.strict, -(t1 + _dot(dbw, w, NT)), 0.0)
            dn_ = dp * c.decay
            dq0 = _dot(dn_, c.k)
            dk0 = _dot(dn_, c.q, TN)
            yield
            dm = da * c.decay
            e = da * c.a_low + dp * c.p
            dkb = _dot(dm, c.k) + dbw * c.egc
            dk_ref[:, cols] = _dot(dm, c.kb, TN) + dk0 + dkb * c.beta + dkd * c.elast
            dq_ref[:, cols] = dq0 + dqd * c.egc
            dv_ref[:, cols] = dbu * c.beta
            dbeta = jnp.sum(dbu * c.v, axis=1, keepdims=True) + jnp.sum(dkb * c.k, axis=1, keepdims=True)
            t_kd = jnp.sum(dkd * c.kd, axis=1, keepdims=True)
            dgc = (jnp.sum(e, axis=1, keepdims=True) - jnp.sum(e.T, axis=1, keepdims=True)
                   + jnp.sum(dbw * c.bw, axis=1, keepdims=True) + jnp.sum(dqd * c.qd, axis=1, keepdims=True) - t_kd)
            dgc = dgc + jnp.where(last, jnp.sum(t_kd, axis=0, keepdims=True) + dgl * c.gl, 0.0)
            yield
            dg = _dotx(upper, jnp.broadcast_to(dgc, (C, SM_W)))
            dgb_ref[hh] = jnp.where(lane == h, dg, jnp.where(lane == h + GDN_HEADS, dbeta, 0.0))

        _round_robin(head(hh) for hh in range(GDN_HB))

    rev = lambda n: N - 1 - n
    q_spec, k_spec, v_spec, gb_spec, o_spec, s_spec, t_spec = _gdn_specs(rev)
    dgb_spec = pl.BlockSpec((GDN_HB, C, SM_W), lambda g, n: (g, rev(n), 0))
    return pl.pallas_call(
        body, name=name, grid=(GDN_HG, N),
        in_specs=[q_spec, k_spec, v_spec, gb_spec, o_spec, s_spec, t_spec],
        out_specs=[o_spec, o_spec, o_spec, dgb_spec],
        out_shape=[jax.ShapeDtypeStruct((M, GDN_QK), F32), jax.ShapeDtypeStruct((M, GDN_QK), F32),
                   jax.ShapeDtypeStruct((M, GDN_V), F32), jax.ShapeDtypeStruct((GDN_HEADS, M, SM_W), F32)],
        scratch_shapes=[pltpu.VMEM((GDN_HB, GDN_DK, GDN_DV), F32)],
        compiler_params=_params("parallel", "arbitrary"),
    )(act, act, act, gb, do, s_all, t_all)


GLA_STEP_ROWS = 64
GLA_SUB = GLA_STEP_ROWS // GLA_CHUNK


def _gla_cumsum(la):
    C = GLA_CHUNK
    ltri = (lax.broadcasted_iota(jnp.int32, (C, C), 0) >= lax.broadcasted_iota(jnp.int32, (C, C), 1)).astype(F32)
    return _dotx(ltri, la)


def _gla_decay_rows(b, i):
    rj = lax.broadcasted_iota(jnp.int32, (GLA_CHUNK, GLA_DK), 0)
    return jnp.where(rj <= i, jnp.exp(jnp.minimum(b[i:i + 1, :] - b, 0.0)), 0.0)


def _gla_scores_t(q, k, b):
    C = GLA_CHUNK
    lane = lax.broadcasted_iota(jnp.int32, (C, C), 1)
    st = jnp.zeros((C, C), F32)
    for i in range(C):
        si = jnp.sum(q[i:i + 1, :] * k * _gla_decay_rows(b, i), axis=1, keepdims=True)
        st = jnp.where(lane == i, si, st)
        if i % 4 == 3:
            yield
    return st


def _gla_specs(n_of):
    R = GLA_STEP_ROWS
    q_spec = pl.BlockSpec((R, GLA_QK), lambda n: (n_of(n), C_GQ // GLA_QK))
    k_spec = pl.BlockSpec((R, GLA_QK), lambda n: (n_of(n), C_GK // GLA_QK))
    v_spec = pl.BlockSpec((R, GLA_V), lambda n: (n_of(n), C_GV // GLA_V))
    la_spec = pl.BlockSpec((R, GLA_QK), lambda n: (n_of(n), 0))
    o_spec = pl.BlockSpec((R, GLA_V), lambda n: (n_of(n), 0))
    s_spec = pl.BlockSpec((GLA_HEADS, None, GLA_SUB, GLA_DV, GLA_DK), lambda n: (0, n_of(n), 0, 0, 0))
    return q_spec, k_spec, v_spec, la_spec, o_spec, s_spec


def _gla_fwd(proj, la, *, name):
    M = proj.shape[0]
    N = M // GLA_STEP_ROWS
    C = GLA_CHUNK

    def body(q_ref, k_ref, v_ref, la_ref, o_ref, s_ref, state):
        n = pl.program_id(0)

        @pl.when(n == 0)
        def _():
            state[...] = jnp.zeros_like(state)

        def head(hh):
            kc = slice(hh * GLA_DK, (hh + 1) * GLA_DK)
            vc = slice(hh * GLA_DV, (hh + 1) * GLA_DV)
            st = state[hh]
            for c in range(GLA_SUB):
                rows = slice(c * C, (c + 1) * C)
                q = q_ref[rows, kc] * (GLA_DK ** -0.5)
                k = k_ref[rows, kc]
                v = v_ref[rows, vc]
                b = _gla_cumsum(la_ref[rows, kc])
                yield
                s_ref[hh, c] = st
                blast = b[C - 1:C, :]
                sc_t = yield from _gla_scores_t(q, k, b)
                o1 = _dot(q * jnp.exp(b), st, NT)
                kv = _dot(v, k * jnp.exp(blast - b), TN)
                o2 = _dot(sc_t, v, TN)
                yield
                o_ref[rows, vc] = o1 + o2
                st = st * jnp.exp(blast) + kv
            state[hh] = st

        _round_robin(head(hh) for hh in range(GLA_HEADS))

    q_spec, k_spec, v_spec, la_spec, o_spec, s_spec = _gla_specs(lambda n: n)
    return pl.pallas_call(
        body, name=name, grid=(N,),
        in_specs=[q_spec, k_spec, v_spec, la_spec], out_specs=[o_spec, s_spec],
        out_shape=[jax.ShapeDtypeStruct((M, GLA_V), F32),
                   jax.ShapeDtypeStruct((GLA_HEADS, N, GLA_SUB, GLA_DV, GLA_DK), F32)],
        scratch_shapes=[pltpu.VMEM((GLA_HEADS, GLA_DV, GLA_DK), F32)],
        compiler_params=_params("arbitrary"),
    )(proj, proj, proj, la)


def _gla_bwd(proj, la, do, s_all, *, name):
    M = proj.shape[0]
    N = M // GLA_STEP_ROWS
    C = GLA_CHUNK

    def body(q_ref, k_ref, v_ref, la_ref, do_ref, s_ref, dq_ref, dk_ref, dv_ref, dla_ref, dstate):
        n = pl.program_id(0)

        @pl.when(n == 0)
        def _():
            dstate[...] = jnp.zeros_like(dstate)

        lane = lax.broadcasted_iota(jnp.int32, (C, C), 1)
        ri = lax.broadcasted_iota(jnp.int32, (C, GLA_DK), 0)
        upper = (lax.broadcasted_iota(jnp.int32, (C, C), 0) <= lane).astype(F32)
        def head(hh):
            kc = slice(hh * GLA_DK, (hh + 1) * GLA_DK)
            vc = slice(hh * GLA_DV, (hh + 1) * GLA_DV)
            ds1 = dstate[hh]
            for c in reversed(range(GLA_SUB)):
                rows = slice(c * C, (c + 1) * C)
                q = q_ref[rows, kc] * (GLA_DK ** -0.5)
                k = k_ref[rows, kc]
                v = v_ref[rows, vc]
                b = _gla_cumsum(la_ref[rows, kc])
                do_ = do_ref[rows, vc]
                st = s_ref[hh, c]
                dsc_t = _dot(v, do_, NT)
                dqe = _dot(do_, st)
                dke = _dot(v, ds1)
                yield
                blast = b[C - 1:C, :]
                eb = jnp.exp(b)
                elast = jnp.exp(blast - b)
                eblast = jnp.exp(blast)
                qe = q * eb
                ke = k * elast
                dv2 = _dot(ke, ds1, NT)
                ds_new = _dot(do_, qe, TN)
                deblast = jnp.sum(st * ds1, axis=0, keepdims=True)
                sc_t = jnp.zeros((C, C), F32)
                dq_sc = jnp.zeros((C, GLA_DK), F32)
                dk_sc = jnp.zeros((C, GLA_DK), F32)
                for i in range(C):
                    f = _gla_decay_rows(b, i)
                    kf = k * f
                    si = jnp.sum(q[i:i + 1, :] * kf, axis=1, keepdims=True)
                    sc_t = jnp.where(lane == i, si, sc_t)
                    dsi = jnp.sum(jnp.where(lane == i, dsc_t, 0.0), axis=1, keepdims=True)
                    dq_sc = jnp.where(ri == i, jnp.sum(dsi * kf, axis=0, keepdims=True), dq_sc)
                    dk_sc = dk_sc + (dsi * f) * q[i:i + 1, :]
                    if i % 4 == 3:
                        yield
                dv1 = _dot(sc_t, do_)
                dq_ref[rows, kc] = ((dq_sc + dqe * eb) * (GLA_DK ** -0.5)).astype(dq_ref.dtype)
                dk_ref[rows, kc] = (dk_sc + dke * elast).astype(dk_ref.dtype)
                t_ke = dke * ke
                db = q * dq_sc - k * dk_sc + dqe * qe - t_ke
                db = db + jnp.where(ri == C - 1, jnp.sum(t_ke, axis=0, keepdims=True) + deblast * eblast, 0.0)
                dla = _dotx(upper, db)
                yield
                dv_ref[rows, vc] = (dv1 + dv2).astype(dv_ref.dtype)
                dla_ref[rows, kc] = dla
                ds1 = ds1 * eblast + ds_new
            dstate[hh] = ds1

        _round_robin(head(hh) for hh in range(GLA_HEADS))

    rev = lambda n: N - 1 - n
    q_spec, k_spec, v_spec, la_spec, o_spec, s_spec = _gla_specs(rev)
    return pl.pallas_call(
        body, name=name, grid=(N,),
        in_specs=[q_spec, k_spec, v_spec, la_spec, o_spec, s_spec],
        out_specs=[la_spec, la_spec, o_spec, la_spec],
        out_shape=[jax.ShapeDtypeStruct((M, GLA_QK), BF16), jax.ShapeDtypeStruct((M, GLA_QK), BF16),
                   jax.ShapeDtypeStruct((M, GLA_V), BF16), jax.ShapeDtypeStruct((M, GLA_QK), F32)],
        scratch_shapes=[pltpu.VMEM((GLA_HEADS, GLA_DV, GLA_DK), F32)],
        compiler_params=_params("arbitrary"),
    )(proj, proj, proj, la, do, s_all)


def _head_norm(o, wn):
    r = lax.rsqrt(jnp.mean(o * o, axis=-1, keepdims=True) + NORM_EPS)
    return o * r, r


def _mix_heads():
    heads = [(0, GDN_DV, hh * GDN_DV, hh * GDN_DV) for hh in range(GDN_HEADS)]
    heads += [(1, GLA_DV, GDN_V + hh * GLA_DV, hh * GLA_DV) for hh in range(GLA_HEADS)]
    return heads


def _mix_fwd(o_gdn, o_gla, proj, wn_gdn, wn_gla, *, name):
    M = proj.shape[0]
    tm = _tile(M, 344, 16)

    def body(og_ref, ol_ref, z_ref, r_ref, wg_ref, wl_ref, m_ref):
        srcs = ((og_ref, z_ref, wg_ref), (ol_ref, r_ref, wl_ref))
        for grp, width, mcol, col in _mix_heads():
            o_ref, gate_ref, w_ref = srcs[grp]
            xhat, _ = _head_norm(o_ref[:, col:col + width], None)
            gate, _ = _silu_and_grad(gate_ref[:, col:col + width])
            m_ref[:, mcol:mcol + width] = (xhat * w_ref[...] * gate).astype(m_ref.dtype)

    full = lambda s: pl.BlockSpec(s, lambda i: (0, 0))
    return pl.pallas_call(
        body, name=name, grid=(M // tm,),
        in_specs=[pl.BlockSpec((tm, GDN_V), lambda i: (i, 0)), pl.BlockSpec((tm, GLA_V), lambda i: (i, 0)),
                  pl.BlockSpec((tm, GDN_V), lambda i: (i, C_Z // GDN_V)),
                  pl.BlockSpec((tm, GLA_V), lambda i: (i, C_GR // GLA_V)),
                  full((1, GDN_DV)), full((1, GLA_DV))],
        out_specs=pl.BlockSpec((tm, D_MODEL), lambda i: (i, 0)),
        out_shape=jax.ShapeDtypeStruct((M, D_MODEL), BF16),
        compiler_params=_params("parallel"),
    )(o_gdn, o_gla, proj, proj, wn_gdn, wn_gla)


def _mix_bwd(o_gdn, o_gla, proj, wn_gdn, wn_gla, dmixed, *, name):
    M = proj.shape[0]
    tm = _tile(M, 344, 16)
    g_ = M // tm

    def body(og_ref, ol_ref, z_ref, r_ref, wg_ref, wl_ref, dm_ref,
             dog_ref, dol_ref, dz_ref, dr_ref, dwg_ref, dwl_ref):
        i = pl.program_id(0)
        srcs = ((og_ref, z_ref, wg_ref, dog_ref, dz_ref), (ol_ref, r_ref, wl_ref, dol_ref, dr_ref))
        dws = [jnp.zeros((1, GDN_DV), F32), jnp.zeros((1, GLA_DV), F32)]
        for grp, width, mcol, col in _mix_heads():
            o_ref, gate_ref, w_ref, do_ref, dgate_ref = srcs[grp]
            cols = slice(col, col + width)
            xhat, r = _head_norm(o_ref[:, cols], None)
            gate, dgate_dc = _silu_and_grad(gate_ref[:, cols])
            dm = dm_ref[:, mcol:mcol + width]
            dgate_ref[:, cols] = (dm * xhat * w_ref[...] * dgate_dc).astype(dgate_ref.dtype)
            dnorm = dm * gate
            dws[grp] = dws[grp] + jnp.sum(dnorm * xhat, axis=0, keepdims=True)
            dxhat = dnorm * w_ref[...]
            do_ref[:, cols] = r * (dxhat - xhat * jnp.mean(dxhat * xhat, axis=-1, keepdims=True))

        @pl.when(i == 0)
        def _():
            dwg_ref[...] = dws[0]
            dwl_ref[...] = dws[1]

        @pl.when(i > 0)
        def _():
            dwg_ref[...] += dws[0]
            dwl_ref[...] += dws[1]

    full = lambda s: pl.BlockSpec(s, lambda i: (0, 0))
    half = pl.BlockSpec((tm, GDN_V), lambda i: (i, 0))
    return pl.pallas_call(
        body, name=name, grid=(g_,),
        in_specs=[half, half, pl.BlockSpec((tm, GDN_V), lambda i: (i, C_Z // GDN_V)),
                  pl.BlockSpec((tm, GLA_V), lambda i: (i, C_GR // GLA_V)),
                  full((1, GDN_DV)), full((1, GLA_DV)), pl.BlockSpec((tm, D_MODEL), lambda i: (i, 0))],
        out_specs=[half, half, half, half, full((1, GDN_DV)), full((1, GLA_DV))],
        out_shape=[jax.ShapeDtypeStruct((M, GDN_V), F32), jax.ShapeDtypeStruct((M, GLA_V), F32),
                   jax.ShapeDtypeStruct((M, GDN_V), BF16), jax.ShapeDtypeStruct((M, GLA_V), BF16),
                   jax.ShapeDtypeStruct((1, GDN_DV), F32), jax.ShapeDtypeStruct((1, GLA_DV), F32)],
        compiler_params=_params("arbitrary"),
    )(o_gdn, o_gla, proj, proj, wn_gdn, wn_gla, dmixed)


def _swiglu_fwd(gate, up, *, name):
    M, F = gate.shape
    tm, tf = _tile(M, 688, 16), _tile(F, 1408, 128)

    def body(g_ref, u_ref, a_ref):
        s, _ = _silu_and_grad(g_ref[...])
        a_ref[...] = (s * u_ref[...]).astype(a_ref.dtype)

    blk = pl.BlockSpec((tm, tf), lambda i, j: (i, j))
    return pl.pallas_call(
        body, name=name, grid=(M // tm, F // tf), in_specs=[blk, blk], out_specs=blk,
        out_shape=jax.ShapeDtypeStruct((M, F), BF16), compiler_params=_params("parallel", "parallel"),
    )(gate, up)


def _swiglu_bwd(gate, up, da, *, name):
    M, F = gate.shape
    tm, tf = _tile(M, 688, 16), _tile(F, 1408, 128)

    def body(g_ref, u_ref, da_ref, dg_ref, du_ref):
        s, ds = _silu_and_grad(g_ref[...])
        da_ = da_ref[...]
        dg_ref[...] = (da_ * u_ref[...] * ds).astype(dg_ref.dtype)
        du_ref[...] = (da_ * s).astype(du_ref.dtype)

    blk = pl.BlockSpec((tm, tf), lambda i, j: (i, j))
    return pl.pallas_call(
        body, name=name, grid=(M // tm, F // tf), in_specs=[blk, blk, blk], out_specs=[blk, blk],
        out_shape=[jax.ShapeDtypeStruct((M, F), BF16), jax.ShapeDtypeStruct((M, F), BF16)],
        compiler_params=_params("parallel", "parallel"),
    )(gate, up, da)


def _adamw(w, g, m, v, *, name):
    shape = w.shape
    cols = shape[-1]
    rows = w.size // cols
    w2, g2, m2, v2 = (t.reshape(rows, cols) for t in (w, g, m, v))
    tr = _tile(rows, 256, 8) if rows % 8 == 0 else rows

    def body(w_ref, g_ref, m_ref, v_ref, d_ref, nm_ref, nv_ref):
        g_ = g_ref[...]
        nm = ADAM_B1 * m_ref[...] + (1.0 - ADAM_B1) * g_
        nv = ADAM_B2 * v_ref[...] + (1.0 - ADAM_B2) * (g_ * g_)
        m_hat = nm / (1.0 - ADAM_B1 ** ADAM_STEP)
        v_hat = nv / (1.0 - ADAM_B2 ** ADAM_STEP)
        d_ref[...] = -ADAM_LR * (m_hat / (jnp.sqrt(v_hat) + ADAM_EPS) + ADAM_WD * w_ref[...])
        nm_ref[...] = nm
        nv_ref[...] = nv

    blk = pl.BlockSpec((tr, cols), lambda i: (i, 0))
    outs = pl.pallas_call(
        body, name=name, grid=(rows // tr,), in_specs=[blk] * 4, out_specs=[blk] * 3,
        out_shape=[jax.ShapeDtypeStruct((rows, cols), F32)] * 3, compiler_params=_params("parallel"),
    )(w2, g2, m2, v2)
    return tuple(t.reshape(shape) for t in outs)


def _sum_slabs(x, *, name):
    _, R, C = x.shape
    tr = _tile(R, 128, 8) if R % 8 == 0 else R

    def body(x_ref, o_ref):
        acc = x_ref[0]
        for s in range(1, N_DEV):
            acc = acc + x_ref[s]
        o_ref[...] = acc

    return pl.pallas_call(
        body, name=name, grid=(R // tr,),
        in_specs=[pl.BlockSpec((N_DEV, tr, C), lambda i: (0, i, 0))], out_specs=pl.BlockSpec((tr, C), lambda i: (i, 0)),
        out_shape=jax.ShapeDtypeStruct((R, C), x.dtype), compiler_params=_params("parallel"),
    )(x)


def _peers():
    x, y, c = lax.axis_index("x"), lax.axis_index("y"), lax.axis_index("c")
    me = 4 * x + 2 * y + c
    peers = []
    for k in range(1, N_DEV):
        px = 1 - x if k & 4 else x
        py = 1 - y if k & 2 else y
        pc = 1 - c if k & 1 else c
        peers.append(((px, py, pc), 4 * px + 2 * py + pc))
    return me, peers


def _exchange(x, *, gather, name):
    slab = x.shape if gather else x.shape[1:]

    def body(x_ref, o_ref, send_sems, recv_sems, own_sem):
        me, peers = _peers()
        own = pltpu.make_async_copy(x_ref if gather else x_ref.at[me], o_ref.at[me], own_sem)
        own.start()
        sends, recvs = [], []
        for k, (pos, idx) in enumerate(peers):
            sends.append(pltpu.make_async_remote_copy(
                src_ref=x_ref if gather else x_ref.at[idx], dst_ref=o_ref.at[me],
                send_sem=send_sems.at[k], recv_sem=recv_sems.at[k],
                device_id=pos, device_id_type=pl.DeviceIdType.MESH))
            recvs.append(pltpu.make_async_remote_copy(
                src_ref=x_ref if gather else x_ref.at[idx], dst_ref=o_ref.at[idx],
                send_sem=send_sems.at[k], recv_sem=recv_sems.at[k],
                device_id=pos, device_id_type=pl.DeviceIdType.MESH))
        for cp in sends:
            cp.start()
        for cp in recvs:
            cp.wait_recv()
        for cp in sends:
            cp.wait_send()
        own.wait()

    hbm = pl.BlockSpec(memory_space=pltpu.HBM)
    return pl.pallas_call(
        body, name=name, in_specs=[hbm], out_specs=hbm,
        out_shape=jax.ShapeDtypeStruct((N_DEV,) + tuple(slab), x.dtype),
        scratch_shapes=[pltpu.SemaphoreType.DMA((N_DEV - 1,)), pltpu.SemaphoreType.DMA((N_DEV - 1,)),
                        pltpu.SemaphoreType.DMA],
    )(x)


def _to_proj_rows(t):
    z = jnp.zeros((SM_W - 2 * GDN_HEADS - GLA_RANK + D_PROJ - C_SM - SM_W,) + t.shape[1:], t.dtype)
    return jnp.concatenate([t[:R_A], t[R_GQ:R_LR], t[R_A:R_GQ], t[R_LR:], z], axis=0)


def _from_proj_rows(t):
    return jnp.concatenate([t[:C_GQ], t[C_SM:C_SM + 2 * GDN_HEADS], t[C_GQ:C_SM],
                            t[C_SM + 2 * GDN_HEADS:C_SM + 2 * GDN_HEADS + GLA_RANK]], axis=0)


def _local_step(x, target, meta, attn_nw, w_in_t, conv_w, a_log, dt_bias, gdn_nw, w2, b2, gla_nw,
                w_out, ffn_nw, w_gate_t, w_up_t, w_down, final_nw):
    S = x.shape[0]
    h0 = jnp.concatenate([jnp.zeros((ROW_PAD, D_MODEL), F32), meta, x], axis=0)
    target_p = jnp.concatenate([jnp.zeros((HEAD_ROWS, D_MODEL), F32), target], axis=0)
    conv_w8 = jnp.concatenate([conv_w, jnp.zeros((8 - CONV_K, conv_w.shape[1]), F32)], axis=0)
    w2p = jnp.zeros((SM_W, GLA_QK), F32).at[2 * GDN_HEADS:2 * GDN_HEADS + GLA_RANK].set(w2)
    alog_p = jnp.zeros((1, SM_W), F32).at[:, :GDN_HEADS].set(a_log)
    dt_p = jnp.zeros((1, SM_W), F32).at[:, :GDN_HEADS].set(dt_bias)

    n1 = _rmsnorm_fwd(h0, attn_nw, name="attn_norm")
    proj = _matmul(n1, w_in_t, mode="nt", name="in_proj")
    gb, la = _gates_fwd(proj, w2p, b2, alog_p, dt_p, name="gates")
    act = _prep_fwd(proj, conv_w8, name="gdn_prep")
    o_gdn, s_gdn, t_gdn = _gdn_fwd(act, gb, name="gdn_fwd")
    o_gla, s_gla = _gla_fwd(proj, la, name="gla_fwd")
    mixed = _mix_fwd(o_gdn, o_gla, proj, gdn_nw, gla_nw, name="mix")
    h1 = _matmul(mixed, w_out, mode="nn", add=h0, name="out_proj")
    n2 = _rmsnorm_fwd(h1, ffn_nw, name="ffn_norm")
    gate = _matmul(n2, w_gate_t, mode="nt", name="ffn_gate")
    up = _matmul(n2, w_up_t, mode="nt", name="ffn_up")
    hid = _swiglu_fwd(gate, up, name="swiglu")
    h2 = _matmul(hid, w_down, mode="nn", add=h1, name="ffn_down")
    dh2, d_final_nw, loss = _loss_head(h2, final_nw, target_p, name="loss_head")

    dh2_b = dh2.astype(BF16)
    d_hid = _matmul(dh2_b, w_down, mode="nt", name="d_hid")
    d_w_down = _matmul(hid, dh2_b, mode="tn", name="d_w_down", tm=1408, tn=1024, tk=1376)
    d_gate, d_up = _swiglu_bwd(gate, up, d_hid, name="d_swiglu")
    d_w_gate_t = _matmul(d_gate, n2, mode="tn", name="d_w_gate", tm=1408, tn=1024, tk=1376)
    d_w_up_t = _matmul(d_up, n2, mode="tn", name="d_w_up", tm=1408, tn=1024, tk=1376)
    d_n2 = _matmul(d_gate, w_gate_t, mode="nn", name="d_n2_gate", tk=1408)
    d_n2 = _matmul(d_up, w_up_t, mode="nn", add=d_n2, name="d_n2_up", tk=1408)
    dh1, d_ffn_nw = _rmsnorm_bwd(h1, ffn_nw, d_n2, dh2, name="d_ffn_norm")

    dh1_b = dh1.astype(BF16)
    d_mixed = _matmul(dh1_b, w_out, mode="nt", name="d_mixed")
    d_w_out = _matmul(mixed, dh1_b, mode="tn", name="d_w_out", tm=1024, tn=1024, tk=1376)
    do_gdn, do_gla, dz, dr, d_gdn_nw, d_gla_nw = _mix_bwd(o_gdn, o_gla, proj, gdn_nw, gla_nw, d_mixed, name="d_mix")
    d_gq, d_gk, d_gv, d_la = _gla_bwd(proj, la, do_gla, s_gla, name="gla_bwd")
    dq, dk, dv, dgb_heads = _gdn_bwd(act, gb, do_gdn, s_gdn, t_gdn, name="gdn_bwd")
    dsm, d_w2p, d_b2, d_alog, d_dt = _gates_bwd(proj, w2p, b2, alog_p, dt_p, dgb_heads, d_la, name="d_gates")
    dc, d_conv_w8 = _prep_bwd_a(proj, conv_w8, jnp.concatenate([dq, dk, dv], axis=1), name="d_gdn_prep")
    d_qkv = _prep_bwd_b(dc, conv_w8, name="d_conv")
    d_proj = jnp.concatenate([d_qkv, dz, d_gq, d_gk, d_gv, dr, dsm,
                              jnp.zeros((S + HEAD_ROWS, D_PROJ - C_SM - SM_W), BF16)], axis=1)
    d_w_in_t = _matmul(d_proj, n1, mode="tn", name="d_w_in", tm=1536, tn=1024, tk=1376)
    d_n1 = _matmul(d_proj, w_in_t, mode="nn", name="d_n1", tk=1536)
    dh0, d_attn_nw = _rmsnorm_bwd(h0, attn_nw, d_n1, dh1, name="d_attn_norm")

    return dict(
        loss=loss[0, 0], grad_x=dh0[HEAD_ROWS:], meta=dh0[ROW_PAD:HEAD_ROWS], attn_nw=d_attn_nw, w_in_t=d_w_in_t,
        conv_w=d_conv_w8[:CONV_K], a_log=d_alog[:, :GDN_HEADS], dt_bias=d_dt[:, :GDN_HEADS], gdn_nw=d_gdn_nw,
        w2=d_w2p[2 * GDN_HEADS:2 * GDN_HEADS + GLA_RANK], b2=d_b2, gla_nw=d_gla_nw, w_out=d_w_out, ffn_nw=d_ffn_nw,
        w_gate_t=d_w_gate_t, w_up_t=d_w_up_t, w_down=d_w_down, final_nw=d_final_nw)


SMALL_ROWS = 32


def kernel(x, meta_tokens, attn_norm_w, w_in, gdn_conv_w, gdn_a_log, gdn_dt_bias, gdn_norm_w, gla_gate_w2, gla_gate_b, gla_norm_w, w_out, ffn_norm_w, w_gate, w_up, w_down, final_norm_w, loss_target, m_meta_tokens, m_attn_norm_w, m_w_in, m_gdn_conv_w, m_gdn_a_log, m_gdn_dt_bias, m_gdn_norm_w, m_gla_gate_w2, m_gla_gate_b, m_gla_norm_w, m_w_out, m_ffn_norm_w, m_w_gate, m_w_up, m_w_down, m_final_norm_w, v_meta_tokens, v_attn_norm_w, v_w_in, v_gdn_conv_w, v_gdn_a_log, v_gdn_dt_bias, v_gdn_norm_w, v_gla_gate_w2, v_gla_gate_b, v_gla_norm_w, v_w_out, v_ffn_norm_w, v_w_gate, v_w_up, v_w_down, v_final_norm_w):
    me = 4 * lax.axis_index("x") + 2 * lax.axis_index("y") + lax.axis_index("c")
    n_in, n_ff, n_out = D_IN // N_DEV, D_FF // N_DEV, D_MODEL // N_DEV

    o1, o2, o3, o4 = n_in, n_in + n_ff, n_in + 2 * n_ff, n_in + 2 * n_ff + n_out
    o5 = o4 + n_ff
    slab_rows = -(-o5 // 128) * 128
    big = jnp.concatenate([w_in[0].T, w_gate[0].T, w_up[0].T, w_out[0], w_down[0],
                           jnp.zeros((slab_rows - o5, D_MODEL), F32)], axis=0).astype(BF16)
    big_all = _exchange(big, gather=True, name="gather_weights")
    w_in_t = _to_proj_rows(big_all[:, :o1].reshape(D_IN, D_MODEL))
    w_gate_t = big_all[:, o1:o2].reshape(D_FF, D_MODEL)
    w_up_t = big_all[:, o2:o3].reshape(D_FF, D_MODEL)
    w_out_f = big_all[:, o3:o4].reshape(D_MODEL, D_MODEL)
    w_down_f = big_all[:, o4:o5].reshape(D_FF, D_MODEL)

    n_conv = gdn_conv_w.shape[2]
    n_w2 = gla_gate_w2.shape[2]
    n_meta = meta_tokens.shape[1]
    small = jnp.zeros((40, n_conv), F32)
    small = small.at[0:N_META, :n_meta].set(meta_tokens)
    small = small.at[N_META:N_META + CONV_K, :].set(gdn_conv_w[0])
    small = small.at[24:24 + GLA_RANK, :n_w2].set(gla_gate_w2[0])
    small_all = _exchange(small, gather=True, name="gather_small")
    meta_f = small_all[:, 0:N_META, :n_meta].transpose(1, 0, 2).reshape(N_META, D_MODEL)
    conv_f = small_all[:, N_META:N_META + CONV_K, :].transpose(1, 0, 2).reshape(CONV_K, N_DEV * n_conv)
    w2_f = small_all[:, 24:24 + GLA_RANK, :n_w2].transpose(1, 0, 2).reshape(GLA_RANK, N_DEV * n_w2)

    g = _local_step(x[0], loss_target[0], meta_f, attn_norm_w, w_in_t, conv_f, gdn_a_log, gdn_dt_bias,
                    gdn_norm_w, w2_f, gla_gate_b, gla_norm_w, w_out_f, ffn_norm_w, w_gate_t, w_up_t, w_down_f,
                    final_norm_w.reshape(1, D_MODEL))

    parts = jnp.concatenate([
        _from_proj_rows(g["w_in_t"]).reshape(N_DEV, n_in, D_MODEL), g["w_gate_t"].reshape(N_DEV, n_ff, D_MODEL),
        g["w_up_t"].reshape(N_DEV, n_ff, D_MODEL), g["w_out"].reshape(N_DEV, n_out, D_MODEL),
        g["w_down"].reshape(N_DEV, n_ff, D_MODEL), jnp.zeros((N_DEV, slab_rows - o5, D_MODEL), F32)], axis=1)
    big_grad = _sum_slabs(_exchange(parts, gather=False, name="scatter_grads"), name="sum_grads")
    grad_w_in = big_grad[:o1].T[None]
    grad_w_gate = big_grad[o1:o2].T[None]
    grad_w_up = big_grad[o2:o3].T[None]
    grad_w_out = big_grad[o3:o4][None]
    grad_w_down = big_grad[o4:o5][None]

    misc = jnp.concatenate([g["a_log"], g["dt_bias"], g["gdn_nw"], g["gla_nw"], g["b2"], g["loss"].reshape(1, 1)], axis=1)
    n_misc = misc.shape[1]
    misc = jnp.pad(misc, ((0, 0), (0, D_MODEL - n_misc)))
    rows = jnp.concatenate([g["attn_nw"], g["ffn_nw"], g["final_nw"], misc, g["meta"],
                            g["conv_w"].reshape(-1, D_MODEL), g["w2"].reshape(-1, D_MODEL)], axis=0)
    rows = jnp.pad(rows, ((0, SMALL_ROWS - rows.shape[0]), (0, 0)))
    tot = _sum_slabs(_exchange(rows, gather=True, name="gather_small_grads"), name="sum_small_grads")
    grad_attn_nw, grad_ffn_nw, grad_final_nw = tot[0:1], tot[1:2], tot[2]
    grad_a_log = tot[3:4, 0:8]
    grad_dt = tot[3:4, 8:16]
    grad_gdn_nw = tot[3:4, 16:16 + GDN_DV]
    grad_gla_nw = tot[3:4, 144:144 + GLA_DV]
    grad_b2 = tot[3:4, 400:400 + GLA_QK]
    loss = tot[3, n_misc - 1]
    r0 = 4 + N_META
    grad_meta = lax.dynamic_slice(tot[4:r0], (0, me * n_meta), (N_META, n_meta))
    r1 = r0 + CONV_K * N_DEV * n_conv // D_MODEL
    grad_conv = lax.dynamic_slice(tot[r0:r1].reshape(CONV_K, N_DEV * n_conv), (0, me * n_conv), (CONV_K, n_conv))[None]
    r2 = r1 + GLA_RANK * N_DEV * n_w2 // D_MODEL
    grad_w2 = lax.dynamic_slice(tot[r1:r2].reshape(GLA_RANK, N_DEV * n_w2), (0, me * n_w2), (GLA_RANK, n_w2))[None]

    weights = [meta_tokens, attn_norm_w, w_in, gdn_conv_w, gdn_a_log, gdn_dt_bias, gdn_norm_w, gla_gate_w2,
               gla_gate_b, gla_norm_w, w_out, ffn_norm_w, w_gate, w_up, w_down, final_norm_w]
    grads = [grad_meta, grad_attn_nw, grad_w_in, grad_conv, grad_a_log, grad_dt, grad_gdn_nw, grad_w2,
             grad_b2, grad_gla_nw, grad_w_out, grad_ffn_nw, grad_w_gate, grad_w_up, grad_w_down, grad_final_nw]
    ms = [m_meta_tokens, m_attn_norm_w, m_w_in, m_gdn_conv_w, m_gdn_a_log, m_gdn_dt_bias, m_gdn_norm_w,
          m_gla_gate_w2, m_gla_gate_b, m_gla_norm_w, m_w_out, m_ffn_norm_w, m_w_gate, m_w_up, m_w_down, m_final_norm_w]
    vs = [v_meta_tokens, v_attn_norm_w, v_w_in, v_gdn_conv_w, v_gdn_a_log, v_gdn_dt_bias, v_gdn_norm_w,
          v_gla_gate_w2, v_gla_gate_b, v_gla_norm_w, v_w_out, v_ffn_norm_w, v_w_gate, v_w_up, v_w_down, v_final_norm_w]
    grads = [gr.reshape(w.shape) for gr, w in zip(grads, weights)]
    deltas, new_ms, new_vs = [], [], []
    for idx, (w, gr, m, v) in enumerate(zip(weights, grads, ms, vs)):
        d, nm, nv = _adamw(w, gr, m, v, name=f"adamw_{idx}")
        deltas.append(d)
        new_ms.append(nm)
        new_vs.append(nv)
    return (loss, g["grad_x"][None], *grads, *deltas, *new_ms, *new_vs)
```

```python
import functools

import jax
import jax.numpy as jnp
from jax import lax
from jax.experimental import pallas as pl
from jax.experimental.pallas import tpu as pltpu

F32 = jnp.float32
BF16 = jnp.bfloat16
_MXU_DTYPE = jnp.bfloat16

D_MODEL = 2048
N_META = 16
ROW_PAD = 48
HEAD_ROWS = ROW_PAD + N_META
CONV_K = 4
GDN_HEADS, GDN_DK, GDN_DV, GDN_CHUNK = 8, 128, 128, 64
GLA_HEADS, GLA_DK, GLA_DV, GLA_CHUNK = 4, 128, 256, 16
GLA_RANK = 16
GLA_GATE_NORMALIZER = 16.0
GDN_QK = GDN_HEADS * GDN_DK
GDN_V = GDN_HEADS * GDN_DV
GLA_QK = GLA_HEADS * GLA_DK
GLA_V = GLA_HEADS * GLA_DV
D_FF = 5632
D_IN = 7200
NORM_EPS = 1e-6
C_QKV, C_Z, C_GQ, C_GK, C_GV, C_GR, C_SM = 0, 3072, 4096, 4608, 5120, 6144, 7168
SM_W = 128
D_PROJ = 7680
R_Z, R_A, R_B, R_GQ, R_GK, R_GV, R_GR, R_LR = 3072, 4096, 4104, 4112, 4624, 5136, 6160, 7184

ADAM_LR, ADAM_B1, ADAM_B2, ADAM_EPS, ADAM_WD, ADAM_STEP = 0.001, 0.9, 0.999, 1e-08, 0.01, 10

N_DEV = 8
VMEM_LIMIT = 56 * 1024 * 1024

NN = (((1,), (0,)), ((), ()))
NT = (((1,), (1,)), ((), ()))
TN = (((0,), (0,)), ((), ()))


def _dot(a, b, dims=NN):
    return lax.dot_general(a.astype(_MXU_DTYPE), b.astype(_MXU_DTYPE), dims, preferred_element_type=F32)


def _dotx(a, b, dims=NN):
    return lax.dot_general(a, b, dims, precision=lax.Precision.HIGHEST, preferred_element_type=F32)


def _dot3(a, b):
    ah = a.astype(BF16)
    al = (a - ah.astype(F32)).astype(BF16)
    bh = b.astype(BF16)
    bl = (b - bh.astype(F32)).astype(BF16)
    d = functools.partial(lax.dot_general, dimension_numbers=NN, preferred_element_type=F32)
    return d(ah, bh) + (d(ah, bl) + d(al, bh))


def _tile(n, target, mult=8):
    best = None
    for t in range(mult, min(n, target) + 1, mult):
        if n % t == 0:
            best = t
    return best if best is not None else n


def _params(*sem):
    return pltpu.CompilerParams(dimension_semantics=sem, vmem_limit_bytes=VMEM_LIMIT)


def _sigmoid(x):
    return 1.0 / (1.0 + jnp.exp(-x))


def _softplus(x):
    return jnp.maximum(x, 0.0) + jnp.log1p(jnp.exp(-jnp.abs(x)))


def _silu_and_grad(c):
    s = _sigmoid(c)
    return c * s, s * (1.0 + c * (1.0 - s))


_ANY = pl.BlockSpec(memory_space=pl.ANY)


def _matmul(a, b, *, mode, name, out_dtype=F32, add=None, after=None, tm=1376, tn=512, tk=2064):
    if mode == "tn":
        K, M = a.shape
        N = b.shape[1]
    else:
        M, K = a.shape
        N = b.shape[0] if mode == "nt" else b.shape[1]
    tm = _tile(M, tm, 128 if mode == "tn" else 16)
    tn = _tile(N, tn, 128)
    tk = _tile(K, tk, 16 if mode == "tn" else 128)
    gm, gn, gk = M // tm, N // tn, K // tk
    dims = {"nn": NN, "nt": NT, "tn": TN}[mode]

    n_after = 0 if after is None else 1

    def body(*refs):
        refs = refs[n_after:]
        if add is None:
            a_ref, b_ref, o_ref = refs[:3]
            add_ref = None
        else:
            a_ref, b_ref, add_ref, o_ref = refs[:4]
        p = _dot(a_ref[...], b_ref[...], dims)

        def finish(r):
            if add_ref is not None:
                r = r + add_ref[...]
            o_ref[...] = r.astype(out_dtype)

        if gk == 1:
            finish(p)
        else:
            acc_ref = refs[-1]
            k = pl.program_id(2)

            @pl.when(k == 0)
            def _():
                acc_ref[...] = p

            @pl.when(k > 0)
            def _():
                acc_ref[...] += p

            @pl.when(k == gk - 1)
            def _():
                finish(acc_ref[...])

    if mode == "tn":
        a_spec = pl.BlockSpec((tk, tm), lambda i, j, k: (k, i))
    else:
        a_spec = pl.BlockSpec((tm, tk), lambda i, j, k: (i, k))
    if mode == "nt":
        b_spec = pl.BlockSpec((tn, tk), lambda i, j, k: (j, k))
    else:
        b_spec = pl.BlockSpec((tk, tn), lambda i, j, k: (k, j))
    o_spec = pl.BlockSpec((tm, tn), lambda i, j, k: (i, j))
    in_specs = [_ANY] * n_after + [a_spec, b_spec] + ([o_spec] if add is not None else [])
    args = ((after,) if n_after else ()) + (a, b) + ((add,) if add is not None else ())
    return pl.pallas_call(
        body, name=name, grid=(gm, gn, gk), in_specs=in_specs, out_specs=o_spec,
        out_shape=jax.ShapeDtypeStruct((M, N), out_dtype),
        scratch_shapes=[pltpu.VMEM((tm, tn), F32)] if gk > 1 else [],
        compiler_params=_params("parallel", "parallel", "arbitrary"),
    )(*args)


def _rmsnorm_fwd(h, w, *, name, after=None):
    M, D = h.shape
    tm = _tile(M, 688, 16)
    n_after = 0 if after is None else 1

    def body(*refs):
        h_ref, w_ref, n_ref = refs[n_after:]
        x = h_ref[...]
        r = lax.rsqrt(jnp.mean(x * x, axis=-1, keepdims=True) + NORM_EPS)
        n_ref[...] = (x * r * w_ref[...]).astype(n_ref.dtype)

    return pl.pallas_call(
        body, name=name, grid=(M // tm,),
        in_specs=[_ANY] * n_after + [pl.BlockSpec((tm, D), lambda i: (i, 0)), pl.BlockSpec((1, D), lambda i: (0, 0))],
        out_specs=pl.BlockSpec((tm, D), lambda i: (i, 0)),
        out_shape=jax.ShapeDtypeStruct((M, D), BF16),
        compiler_params=_params("parallel"),
    )(*((after,) if n_after else ()), h, w)


def _rmsnorm_bwd(h, w, dn, dres, *, name):
    M, D = h.shape
    tm = _tile(M, 344, 8)
    g = M // tm

    def body(h_ref, w_ref, dn_ref, dres_ref, dh_ref, dw_ref, acc_ref):
        i = pl.program_id(0)
        x = h_ref[...]
        r = lax.rsqrt(jnp.mean(x * x, axis=-1, keepdims=True) + NORM_EPS)
        xhat = x * r
        dn_ = dn_ref[...]
        dxhat = dn_ * w_ref[...]
        dh_ref[...] = dres_ref[...] + r * (dxhat - xhat * jnp.mean(dxhat * xhat, axis=-1, keepdims=True))
        part = jnp.sum((dn_ * xhat).reshape(tm // 8, 8, D), axis=0)

        @pl.when(i == 0)
        def _():
            acc_ref[...] = part

        @pl.when(i > 0)
        def _():
            acc_ref[...] += part

        @pl.when(i == g - 1)
        def _():
            dw_ref[...] = jnp.sum(acc_ref[...], axis=0, keepdims=True)

    row = pl.BlockSpec((tm, D), lambda i: (i, 0))
    vec = pl.BlockSpec((1, D), lambda i: (0, 0))
    return pl.pallas_call(
        body, name=name, grid=(g,), in_specs=[row, vec, row, row], out_specs=[row, vec],
        out_shape=[jax.ShapeDtypeStruct((M, D), F32), jax.ShapeDtypeStruct((1, D), F32)],
        scratch_shapes=[pltpu.VMEM((8, D), F32)],
        compiler_params=_params("arbitrary"),
    )(h, w, dn, dres)


def _loss_head(h, w, target_p, *, name):
    M, D = h.shape
    tm = _tile(M, 344, 8)
    g = M // tm

    def body(h_ref, w_ref, t_ref, dh_ref, dw_ref, loss_ref, acc_ref, lacc_ref):
        i = pl.program_id(0)
        x = h_ref[...]
        row = i * tm + lax.broadcasted_iota(jnp.int32, (tm, 1), 0)
        live = row >= HEAD_ROWS
        r = lax.rsqrt(jnp.mean(x * x, axis=-1, keepdims=True) + NORM_EPS)
        xhat = x * r
        err = jnp.where(live, xhat * w_ref[...] - t_ref[...], 0.0)
        dy = err * (1.0 / D)
        dxhat = dy * w_ref[...]
        dh_ref[...] = r * (dxhat - xhat * jnp.mean(dxhat * xhat, axis=-1, keepdims=True))
        part = jnp.sum((dy * xhat).reshape(tm // 8, 8, D), axis=0)
        lpart = jnp.sum((err * err).reshape(tm // 8, 8, D), axis=0)

        @pl.when(i == 0)
        def _():
            acc_ref[...] = part
            lacc_ref[...] = lpart

        @pl.when(i > 0)
        def _():
            acc_ref[...] += part
            lacc_ref[...] += lpart

        @pl.when(i == g - 1)
        def _():
            dw_ref[...] = jnp.sum(acc_ref[...], axis=0, keepdims=True)
            tot = jnp.sum(jnp.sum(lacc_ref[...], axis=0, keepdims=True), axis=1, keepdims=True)
            loss_ref[...] = jnp.broadcast_to(tot * (0.5 / D), (1, 128))

    row = pl.BlockSpec((tm, D), lambda i: (i, 0))
    vec = pl.BlockSpec((1, D), lambda i: (0, 0))
    return pl.pallas_call(
        body, name=name, grid=(g,), in_specs=[row, vec, row],
        out_specs=[row, vec, pl.BlockSpec((1, 128), lambda i: (0, 0))],
        out_shape=[jax.ShapeDtypeStruct((M, D), F32), jax.ShapeDtypeStruct((1, D), F32),
                   jax.ShapeDtypeStruct((1, 128), F32)],
        scratch_shapes=[pltpu.VMEM((8, D), F32), pltpu.VMEM((8, D), F32)],
        compiler_params=_params("arbitrary"),
    )(h, w, target_p)


def _gate_terms(sm, w2p, b2, alog_p, dt_p, row0):
    tm = sm.shape[0]
    lane = lax.broadcasted_iota(jnp.int32, (tm, SM_W), 1)
    live = (row0 + lax.broadcasted_iota(jnp.int32, (tm, 1), 0)) >= ROW_PAD
    pre = sm + dt_p
    neg_a = -jnp.exp(alog_p)
    g = neg_a * _softplus(pre)
    beta = _sigmoid(sm)
    z = _dot(sm, w2p) + b2
    return lane, live, pre, neg_a, g, beta, z


def _gates_fwd(proj, w2p, b2, alog_p, dt_p, *, name):
    M = proj.shape[0]
    tm = _tile(M, 688, 8)

    def body(sm_ref, w2_ref, b2_ref, al_ref, dt_ref, gb_ref, la_ref):
        row0 = pl.program_id(0) * tm
        lane, live, _, _, g, beta, z = _gate_terms(sm_ref[...], w2_ref[...], b2_ref[...], al_ref[...], dt_ref[...], row0)
        gb = jnp.where(lane < GDN_HEADS, g, jnp.where(lane < 2 * GDN_HEADS, beta, 0.0))
        gb_ref[...] = jnp.where(live, gb, 0.0)
        la = (jnp.minimum(z, 0.0) - jnp.log1p(jnp.exp(-jnp.abs(z)))) * (1.0 / GLA_GATE_NORMALIZER)
        la_ref[...] = jnp.where(live, la, 0.0)

    full = lambda s: pl.BlockSpec(s, lambda i: (0, 0))
    return pl.pallas_call(
        body, name=name, grid=(M // tm,),
        in_specs=[pl.BlockSpec((tm, SM_W), lambda i: (i, C_SM // SM_W)), full((SM_W, GLA_QK)), full((1, GLA_QK)),
                  full((1, SM_W)), full((1, SM_W))],
        out_specs=[pl.BlockSpec((tm, SM_W), lambda i: (i, 0)), pl.BlockSpec((tm, GLA_QK), lambda i: (i, 0))],
        out_shape=[jax.ShapeDtypeStruct((M, SM_W), F32), jax.ShapeDtypeStruct((M, GLA_QK), F32)],
        compiler_params=_params("parallel"),
    )(proj, w2p, b2, alog_p, dt_p)


def _gates_bwd(proj, w2p, b2, alog_p, dt_p, dgb_heads, dla, *, name):
    M = proj.shape[0]
    tm = _tile(M, 688, 8)
    g_ = M // tm

    def body(sm_ref, w2_ref, b2_ref, al_ref, dt_ref, dgb_ref, dla_ref,
             dsm_ref, dw2_ref, db2_ref, dal_ref, ddt_ref):
        i = pl.program_id(0)
        sm = sm_ref[...]
        lane, live, pre, neg_a, g, beta, z = _gate_terms(sm, w2_ref[...], b2_ref[...], al_ref[...], dt_ref[...], i * tm)
        dz = jnp.where(live, dla_ref[...] * (_sigmoid(-z) * (1.0 / GLA_GATE_NORMALIZER)), 0.0)
        dsm_lr = _dot(dz, w2_ref[...], NT)
        dgb = dgb_ref[0]
        for hh in range(1, GDN_HEADS):
            dgb = dgb + dgb_ref[hh]
        dgb = jnp.where(live, dgb, 0.0)
        da = dgb * neg_a * _sigmoid(pre)
        db = dgb * beta * (1.0 - beta)
        dsm = jnp.where(lane < GDN_HEADS, da, jnp.where(lane < 2 * GDN_HEADS, db, dsm_lr))
        dsm_ref[...] = dsm.astype(dsm_ref.dtype)
        is_a = lane < GDN_HEADS
        dal = jnp.sum(jnp.where(is_a, dgb * g, 0.0), axis=0, keepdims=True)
        ddt = jnp.sum(jnp.where(is_a, da, 0.0), axis=0, keepdims=True)
        dw2 = _dot(sm, dz, TN)
        db2 = jnp.sum(dz, axis=0, keepdims=True)

        @pl.when(i == 0)
        def _():
            dw2_ref[...] = dw2
            db2_ref[...] = db2
            dal_ref[...] = dal
            ddt_ref[...] = ddt

        @pl.when(i > 0)
        def _():
            dw2_ref[...] += dw2
            db2_ref[...] += db2
            dal_ref[...] += dal
            ddt_ref[...] += ddt

    full = lambda s: pl.BlockSpec(s, lambda i: (0, 0))
    return pl.pallas_call(
        body, name=name, grid=(g_,),
        in_specs=[pl.BlockSpec((tm, SM_W), lambda i: (i, C_SM // SM_W)), full((SM_W, GLA_QK)), full((1, GLA_QK)),
                  full((1, SM_W)), full((1, SM_W)),
                  pl.BlockSpec((GDN_HEADS, tm, SM_W), lambda i: (0, i, 0)),
                  pl.BlockSpec((tm, GLA_QK), lambda i: (i, 0))],
        out_specs=[pl.BlockSpec((tm, SM_W), lambda i: (i, 0)), full((SM_W, GLA_QK)), full((1, GLA_QK)),
                   full((1, SM_W)), full((1, SM_W))],
        out_shape=[jax.ShapeDtypeStruct((M, SM_W), BF16), jax.ShapeDtypeStruct((SM_W, GLA_QK), F32),
                   jax.ShapeDtypeStruct((1, GLA_QK), F32), jax.ShapeDtypeStruct((1, SM_W), F32),
                   jax.ShapeDtypeStruct((1, SM_W), F32)],
        compiler_params=_params("arbitrary"),
    )(proj, w2p, b2, alog_p, dt_p, dgb_heads, dla)


N_QKV_BLOCKS = (2 * GDN_QK + GDN_V) // 128
HALO = 8


def _conv_terms(x_ref, halo_ref, cw_ref, xs_ref, i, tm):
    xs_ref[HALO:HALO + tm, :] = x_ref[...]
    xs_ref[0:HALO, :] = jnp.where(i > 0, halo_ref[...], 0.0)
    cw = cw_ref[...]
    taps = [xs_ref[HALO - (CONV_K - 1) + t:HALO - (CONV_K - 1) + t + tm, :] for t in range(CONV_K)]
    c = taps[0] * cw[0:1, :]
    for t in range(1, CONV_K):
        c = c + taps[t] * cw[t:t + 1, :]
    return c, taps


def _prep_fwd(proj, conv_w8, *, name):
    M = proj.shape[0]
    tm = _tile(M, 688, 8)

    def body(x_ref, halo_ref, cw_ref, o_ref, xs_ref):
        j, i = pl.program_id(0), pl.program_id(1)
        c, _ = _conv_terms(x_ref, halo_ref, cw_ref, xs_ref, i, tm)
        s, _ = _silu_and_grad(c)
        r = lax.rsqrt(jnp.sum(s * s, axis=-1, keepdims=True) + NORM_EPS)
        scale = jnp.where(j < GDN_HEADS, GDN_DK ** -0.5, 1.0)
        o_ref[...] = jnp.where(j < 2 * GDN_HEADS, s * (r * scale), s)

    hb = tm // HALO
    return pl.pallas_call(
        body, name=name, grid=(N_QKV_BLOCKS, M // tm),
        in_specs=[pl.BlockSpec((tm, 128), lambda j, i: (i, j)),
                  pl.BlockSpec((HALO, 128), lambda j, i: (jnp.maximum(i * hb - 1, 0), j)),
                  pl.BlockSpec((8, 128), lambda j, i: (0, j))],
        out_specs=pl.BlockSpec((tm, 128), lambda j, i: (i, j)),
        out_shape=jax.ShapeDtypeStruct((M, N_QKV_BLOCKS * 128), F32),
        scratch_shapes=[pltpu.VMEM((tm + HALO, 128), F32)],
        compiler_params=_params("parallel", "arbitrary"),
    )(proj, proj, conv_w8)


def _prep_bwd_a(proj, conv_w8, dact, *, name):
    M = proj.shape[0]
    tm = _tile(M, 688, 8)
    g_ = M // tm

    def body(x_ref, halo_ref, cw_ref, da_ref, dc_ref, dcw_ref, xs_ref):
        j, i = pl.program_id(0), pl.program_id(1)
        c, taps = _conv_terms(x_ref, halo_ref, cw_ref, xs_ref, i, tm)
        s, ds_dc = _silu_and_grad(c)
        r = lax.rsqrt(jnp.sum(s * s, axis=-1, keepdims=True) + NORM_EPS)
        scale = jnp.where(j < GDN_HEADS, GDN_DK ** -0.5, 1.0)
        da = da_ref[...]
        y = s * r
        dy = da * scale
        ds_norm = r * (dy - y * jnp.sum(dy * y, axis=-1, keepdims=True))
        ds = jnp.where(j < 2 * GDN_HEADS, ds_norm, da)
        dc = ds * ds_dc
        dc_ref[...] = dc
        r8 = lax.broadcasted_iota(jnp.int32, (8, 128), 0)
        part = jnp.zeros((8, 128), F32)
        for t in range(CONV_K):
            part = jnp.where(r8 == t, jnp.sum(dc * taps[t], axis=0, keepdims=True), part)

        @pl.when(i == 0)
        def _():
            dcw_ref[...] = part

        @pl.when(i > 0)
        def _():
            dcw_ref[...] += part

    hb = tm // HALO
    blk = pl.BlockSpec((tm, 128), lambda j, i: (i, j))
    return pl.pallas_call(
        body, name=name, grid=(N_QKV_BLOCKS, g_),
        in_specs=[blk, pl.BlockSpec((HALO, 128), lambda j, i: (jnp.maximum(i * hb - 1, 0), j)),
                  pl.BlockSpec((8, 128), lambda j, i: (0, j)), blk],
        out_specs=[blk, pl.BlockSpec((8, 128), lambda j, i: (0, j))],
        out_shape=[jax.ShapeDtypeStruct((M, N_QKV_BLOCKS * 128), F32),
                   jax.ShapeDtypeStruct((8, N_QKV_BLOCKS * 128), F32)],
        scratch_shapes=[pltpu.VMEM((tm + HALO, 128), F32)],
        compiler_params=_params("parallel", "arbitrary"),
    )(proj, proj, conv_w8, dact)


def _prep_bwd_b(dc, conv_w8, *, name):
    M = dc.shape[0]
    tm = _tile(M, 688, 8)
    g_ = M // tm

    def body(d_ref, halo_ref, cw_ref, o_ref, ds_ref):
        i = pl.program_id(1)
        ds_ref[0:tm, :] = d_ref[...]
        ds_ref[tm:tm + HALO, :] = jnp.where(i < g_ - 1, halo_ref[...], 0.0)
        cw = cw_ref[...]
        acc = ds_ref[CONV_K - 1:CONV_K - 1 + tm, :] * cw[0:1, :]
        for t in range(1, CONV_K):
            acc = acc + ds_ref[CONV_K - 1 - t:CONV_K - 1 - t + tm, :] * cw[t:t + 1, :]
        o_ref[...] = acc.astype(o_ref.dtype)

    hb = tm // HALO
    last = M // HALO - 1
    blk = pl.BlockSpec((tm, 128), lambda j, i: (i, j))
    return pl.pallas_call(
        body, name=name, grid=(N_QKV_BLOCKS, g_),
        in_specs=[blk, pl.BlockSpec((HALO, 128), lambda j, i: (jnp.minimum((i + 1) * hb, last), j)),
                  pl.BlockSpec((8, 128), lambda j, i: (0, j))],
        out_specs=blk,
        out_shape=jax.ShapeDtypeStruct((M, N_QKV_BLOCKS * 128), BF16),
        scratch_shapes=[pltpu.VMEM((tm + HALO, 128), F32)],
        compiler_params=_params("parallel", "arbitrary"),
    )(dc, dc, conv_w8)


def _round_robin(gens):
    gens = list(gens)
    while gens:
        alive = []
        for gen in gens:
            try:
                next(gen)
                alive.append(gen)
            except StopIteration:
                pass
        gens = alive


def _unit_lower_inverse(a_low, eye):
    b = -a_low
    x = eye + b
    pw = b
    for _ in range(5):
        pw = _dot3(pw, pw)
        yield
        x = x + _dot3(x, pw)
        yield
    return x


class _GdnChunk:
    def build(self, q, k, v, gb, h):
        C = GDN_CHUNK
        lane = lax.broadcasted_iota(jnp.int32, (C, SM_W), 1)
        g = jnp.sum(jnp.where(lane == h, gb, 0.0), axis=1, keepdims=True)
        self.beta = jnp.sum(jnp.where(lane == h + GDN_HEADS, gb, 0.0), axis=1, keepdims=True)
        ri = lax.broadcasted_iota(jnp.int32, (C, C), 0)
        ci = lax.broadcasted_iota(jnp.int32, (C, C), 1)
        self.causal = ri >= ci
        self.strict = ri > ci
        self.eye = (ri == ci).astype(F32)
        gcb = _dotx(self.causal.astype(F32), jnp.broadcast_to(g, (C, SM_W)))
        yield
        self.gcol = gcb[:, 0:1]
        grow = gcb.T[0:1, 0:C]
        self.decay = jnp.exp(jnp.where(self.causal, self.gcol - grow, -1e30))
        self.egc = jnp.exp(self.gcol)
        glast = gcb[C - 1:C, 0:1]
        self.elast = jnp.exp(glast - self.gcol)
        self.gl = jnp.exp(glast)
        self.q, self.k, self.v = q, k, v
        self.kb = k * self.beta
        m = _dot(self.kb, k, NT)
        n_ = _dot(q, k, NT)
        yield
        self.a_low = jnp.where(self.strict, m * self.decay, 0.0)
        self.p = n_ * self.decay
        self.qd = q * self.egc
        self.kd = k * self.elast
        self.bu = v * self.beta
        self.bw = self.kb * self.egc


GDN_HB = 8
GDN_HG = GDN_HEADS // GDN_HB


def _gdn_specs(n_of):
    C, W = GDN_CHUNK, 128 * GDN_HB
    q_spec = pl.BlockSpec((C, W), lambda g, n: (n_of(n), g))
    k_spec = pl.BlockSpec((C, W), lambda g, n: (n_of(n), g + GDN_HG))
    v_spec = pl.BlockSpec((C, W), lambda g, n: (n_of(n), g + 2 * GDN_HG))
    gb_spec = pl.BlockSpec((C, SM_W), lambda g, n: (n_of(n), 0))
    o_spec = pl.BlockSpec((C, W), lambda g, n: (n_of(n), g))
    s_spec = pl.BlockSpec((GDN_HB, None, GDN_DK, GDN_DV), lambda g, n: (g, n_of(n), 0, 0))
    t_spec = pl.BlockSpec((GDN_HB, None, C, C), lambda g, n: (g, n_of(n), 0, 0))
    return q_spec, k_spec, v_spec, gb_spec, o_spec, s_spec, t_spec


def _gdn_fwd(act, gb, *, name):
    M = act.shape[0]
    N = M // GDN_CHUNK

    def body(q_ref, k_ref, v_ref, gb_ref, o_ref, s_ref, t_ref, state):
        g, n = pl.program_id(0), pl.program_id(1)

        @pl.when(n == 0)
        def _():
            state[...] = jnp.zeros_like(state)

        gb_ = gb_ref[...]

        def head(hh):
            cols = slice(hh * 128, (hh + 1) * 128)
            c = _GdnChunk()
            yield from c.build(q_ref[:, cols], k_ref[:, cols], v_ref[:, cols], gb_, g * GDN_HB + hh)
            tinv = yield from _unit_lower_inverse(c.a_low, c.eye)
            s = state[hh]
            s_ref[hh] = s
            t_ref[hh] = tinv
            u = _dot(tinv, c.bu)
            w = _dot(tinv, c.bw)
            yield
            vn = u - _dot(w, s)
            o1 = _dot(c.qd, s)
            yield
            o_ref[:, cols] = o1 + _dot(c.p, vn)
            state[hh] = c.gl * s + _dot(c.kd, vn, TN)

        _round_robin(head(hh) for hh in range(GDN_HB))

    q_spec, k_spec, v_spec, gb_spec, o_spec, s_spec, t_spec = _gdn_specs(lambda n: n)
    return pl.pallas_call(
        body, name=name, grid=(GDN_HG, N),
        in_specs=[q_spec, k_spec, v_spec, gb_spec], out_specs=[o_spec, s_spec, t_spec],
        out_shape=[jax.ShapeDtypeStruct((M, GDN_V), F32),
                   jax.ShapeDtypeStruct((GDN_HEADS, N, GDN_DK, GDN_DV), F32),
                   jax.ShapeDtypeStruct((GDN_HEADS, N, GDN_CHUNK, GDN_CHUNK), F32)],
        scratch_shapes=[pltpu.VMEM((GDN_HB, GDN_DK, GDN_DV), F32)],
        compiler_params=_params("parallel", "arbitrary"),
    )(act, act, act, gb)


def _gdn_bwd(act, gb, do, s_all, t_all, *, name):
    M = act.shape[0]
    N = M // GDN_CHUNK
    C = GDN_CHUNK

    def body(q_ref, k_ref, v_ref, gb_ref, do_ref, s_ref, t_ref, dq_ref, dk_ref, dv_ref, dgb_ref, dstate):
        g, n = pl.program_id(0), pl.program_id(1)

        @pl.when(n == 0)
        def _():
            dstate[...] = jnp.zeros_like(dstate)

        gb_ = gb_ref[...]
        last = lax.broadcasted_iota(jnp.int32, (C, 1), 0) == C - 1
        upper = (lax.broadcasted_iota(jnp.int32, (C, C), 0) <= lax.broadcasted_iota(jnp.int32, (C, C), 1)).astype(F32)
        lane = lax.broadcasted_iota(jnp.int32, (C, SM_W), 1)
        def head(hh):
            cols = slice(hh * 128, (hh + 1) * 128)
            h = g * GDN_HB + hh
            c = _GdnChunk()
            yield from c.build(q_ref[:, cols], k_ref[:, cols], v_ref[:, cols], gb_, h)
            tinv = t_ref[hh]
            s = s_ref[hh]
            do_ = do_ref[:, cols]
            ds1 = dstate[hh]
            u = _dot(tinv, c.bu)
            w = _dot(tinv, c.bw)
            dqd = _dot(do_, s, NT)
            dvn0 = _dot(c.p, do_, TN) + _dot(c.kd, ds1)
            dst0 = _dot(c.qd, do_, TN) + c.gl * ds1
            yield
            vn = u - _dot(w, s)
            dvn = dvn0
            yield
            dp = jnp.where(c.causal, _dot(do_, vn, NT), 0.0)
            dstate[hh] = dst0 - _dot(w, dvn, TN)
            dkd = _dot(vn, ds1, NT)
            dw = -_dot(dvn, s, NT)
            dbu = _dot(tinv, dvn, TN)
            dgl = jnp.sum(jnp.sum(s * ds1, axis=1, keepdims=True), axis=0, keepdims=True)
            yield
            dbw = _dot(tinv, dw, TN)
            t1 = _dot(dbu, u, NT)
            yield
            da = jnp.where(c.strict, -(t1 + _dot(dbw, w, NT)), 0.0)
            dn_ = dp * c.decay
            dq0 = _dot(dn_, c.k)
            dk0 = _dot(dn_, c.q, TN)
            yield
            dm = da * c.decay
            e = da * c.a_low + dp * c.p
            dkb = _dot(dm, c.k) + dbw * c.egc
            dk_ref[:, cols] = _dot(dm, c.kb, TN) + dk0 + dkb * c.beta + dkd * c.elast
            dq_ref[:, cols] = dq0 + dqd * c.egc
            dv_ref[:, cols] = dbu * c.beta
            dbeta = jnp.sum(dbu * c.v, axis=1, keepdims=True) + jnp.sum(dkb * c.k, axis=1, keepdims=True)
            t_kd = jnp.sum(dkd * c.kd, axis=1, keepdims=True)
            dgc = (jnp.sum(e, axis=1, keepdims=True) - jnp.sum(e.T, axis=1, keepdims=True)
                   + jnp.sum(dbw * c.bw, axis=1, keepdims=True) + jnp.sum(dqd * c.qd, axis=1, keepdims=True) - t_kd)
            dgc = dgc + jnp.where(last, jnp.sum(t_kd, axis=0, keepdims=True) + dgl * c.gl, 0.0)
            yield
            dg = _dotx(upper, jnp.broadcast_to(dgc, (C, SM_W)))
            dgb_ref[hh] = jnp.where(lane == h, dg, jnp.where(lane == h + GDN_HEADS, dbeta, 0.0))

        _round_robin(head(hh) for hh in range(GDN_HB))

    rev = lambda n: N - 1 - n
    q_spec, k_spec, v_spec, gb_spec, o_spec, s_spec, t_spec = _gdn_specs(rev)
    dgb_spec = pl.BlockSpec((GDN_HB, C, SM_W), lambda g, n: (g, rev(n), 0))
    return pl.pallas_call(
        body, name=name, grid=(GDN_HG, N),
        in_specs=[q_spec, k_spec, v_spec, gb_spec, o_spec, s_spec, t_spec],
        out_specs=[o_spec, o_spec, o_spec, dgb_spec],
        out_shape=[jax.ShapeDtypeStruct((M, GDN_QK), F32), jax.ShapeDtypeStruct((M, GDN_QK), F32),
                   jax.ShapeDtypeStruct((M, GDN_V), F32), jax.ShapeDtypeStruct((GDN_HEADS, M, SM_W), F32)],
        scratch_shapes=[pltpu.VMEM((GDN_HB, GDN_DK, GDN_DV), F32)],
        compiler_params=_params("parallel", "arbitrary"),
    )(act, act, act, gb, do, s_all, t_all)


GLA_STEP_ROWS = 64
GLA_SUB = GLA_STEP_ROWS // GLA_CHUNK


def _gla_cumsum(la):
    C = GLA_CHUNK
    ltri = (lax.broadcasted_iota(jnp.int32, (C, C), 0) >= lax.broadcasted_iota(jnp.int32, (C, C), 1)).astype(F32)
    return _dotx(ltri, la)


def _gla_decay_rows(b, i):
    rj = lax.broadcasted_iota(jnp.int32, (GLA_CHUNK, GLA_DK), 0)
    return jnp.where(rj <= i, jnp.exp(jnp.minimum(b[i:i + 1, :] - b, 0.0)), 0.0)


def _gla_scores_t(q, k, b):
    C = GLA_CHUNK
    lane = lax.broadcasted_iota(jnp.int32, (C, C), 1)
    st = jnp.zeros((C, C), F32)
    for i in range(C):
        si = jnp.sum(q[i:i + 1, :] * k * _gla_decay_rows(b, i), axis=1, keepdims=True)
        st = jnp.where(lane == i, si, st)
        if i % 4 == 3:
            yield
    return st


def _gla_specs(n_of):
    R = GLA_STEP_ROWS
    q_spec = pl.BlockSpec((R, GLA_QK), lambda n: (n_of(n), C_GQ // GLA_QK))
    k_spec = pl.BlockSpec((R, GLA_QK), lambda n: (n_of(n), C_GK // GLA_QK))
    v_spec = pl.BlockSpec((R, GLA_V), lambda n: (n_of(n), C_GV // GLA_V))
    la_spec = pl.BlockSpec((R, GLA_QK), lambda n: (n_of(n), 0))
    o_spec = pl.BlockSpec((R, GLA_V), lambda n: (n_of(n), 0))
    s_spec = pl.BlockSpec((GLA_HEADS, None, GLA_SUB, GLA_DV, GLA_DK), lambda n: (0, n_of(n), 0, 0, 0))
    return q_spec, k_spec, v_spec, la_spec, o_spec, s_spec


def _gla_fwd(proj, la, *, name):
    M = proj.shape[0]
    N = M // GLA_STEP_ROWS
    C = GLA_CHUNK

    def body(q_ref, k_ref, v_ref, la_ref, o_ref, s_ref, state):
        n = pl.program_id(0)

        @pl.when(n == 0)
        def _():
            state[...] = jnp.zeros_like(state)

        def head(hh):
            kc = slice(hh * GLA_DK, (hh + 1) * GLA_DK)
            vc = slice(hh * GLA_DV, (hh + 1) * GLA_DV)
            st = state[hh]
            for c in range(GLA_SUB):
                rows = slice(c * C, (c + 1) * C)
                q = q_ref[rows, kc] * (GLA_DK ** -0.5)
                k = k_ref[rows, kc]
                v = v_ref[rows, vc]
                b = _gla_cumsum(la_ref[rows, kc])
                yield
                s_ref[hh, c] = st
                blast = b[C - 1:C, :]
                sc_t = yield from _gla_scores_t(q, k, b)
                o1 = _dot(q * jnp.exp(b), st, NT)
                kv = _dot(v, k * jnp.exp(blast - b), TN)
                o2 = _dot(sc_t, v, TN)
                yield
                o_ref[rows, vc] = o1 + o2
                st = st * jnp.exp(blast) + kv
            state[hh] = st

        _round_robin(head(hh) for hh in range(GLA_HEADS))

    q_spec, k_spec, v_spec, la_spec, o_spec, s_spec = _gla_specs(lambda n: n)
    return pl.pallas_call(
        body, name=name, grid=(N,),
        in_specs=[q_spec, k_spec, v_spec, la_spec], out_specs=[o_spec, s_spec],
        out_shape=[jax.ShapeDtypeStruct((M, GLA_V), F32),
                   jax.ShapeDtypeStruct((GLA_HEADS, N, GLA_SUB, GLA_DV, GLA_DK), F32)],
        scratch_shapes=[pltpu.VMEM((GLA_HEADS, GLA_DV, GLA_DK), F32)],
        compiler_params=_params("arbitrary"),
    )(proj, proj, proj, la)


def _gla_bwd(proj, la, do, s_all, *, name):
    M = proj.shape[0]
    N = M // GLA_STEP_ROWS
    C = GLA_CHUNK

    def body(q_ref, k_ref, v_ref, la_ref, do_ref, s_ref, dq_ref, dk_ref, dv_ref, dla_ref, dstate):
        n = pl.program_id(0)

        @pl.when(n == 0)
        def _():
            dstate[...] = jnp.zeros_like(dstate)

        lane = lax.broadcasted_iota(jnp.int32, (C, C), 1)
        ri = lax.broadcasted_iota(jnp.int32, (C, GLA_DK), 0)
        upper = (lax.broadcasted_iota(jnp.int32, (C, C), 0) <= lane).astype(F32)
        def head(hh):
            kc = slice(hh * GLA_DK, (hh + 1) * GLA_DK)
            vc = slice(hh * GLA_DV, (hh + 1) * GLA_DV)
            ds1 = dstate[hh]
            for c in reversed(range(GLA_SUB)):
                rows = slice(c * C, (c + 1) * C)
                q = q_ref[rows, kc] * (GLA_DK ** -0.5)
                k = k_ref[rows, kc]
                v = v_ref[rows, vc]
                b = _gla_cumsum(la_ref[rows, kc])
                do_ = do_ref[rows, vc]
                st = s_ref[hh, c]
                dsc_t = _dot(v, do_, NT)
                dqe = _dot(do_, st)
                dke = _dot(v, ds1)
                yield
                blast = b[C - 1:C, :]
                eb = jnp.exp(b)
                elast = jnp.exp(blast - b)
                eblast = jnp.exp(blast)
                qe = q * eb
                ke = k * elast
                dv2 = _dot(ke, ds1, NT)
                ds_new = _dot(do_, qe, TN)
                deblast = jnp.sum(st * ds1, axis=0, keepdims=True)
                sc_t = jnp.zeros((C, C), F32)
                dq_sc = jnp.zeros((C, GLA_DK), F32)
                dk_sc = jnp.zeros((C, GLA_DK), F32)
                for i in range(C):
                    f = _gla_decay_rows(b, i)
                    kf = k * f
                    si = jnp.sum(q[i:i + 1, :] * kf, axis=1, keepdims=True)
                    sc_t = jnp.where(lane == i, si, sc_t)
                    dsi = jnp.sum(jnp.where(lane == i, dsc_t, 0.0), axis=1, keepdims=True)
                    dq_sc = jnp.where(ri == i, jnp.sum(dsi * kf, axis=0, keepdims=True), dq_sc)
                    dk_sc = dk_sc + (dsi * f) * q[i:i + 1, :]
                    if i % 4 == 3:
                        yield
                dv1 = _dot(sc_t, do_)
                dq_ref[rows, kc] = ((dq_sc + dqe * eb) * (GLA_DK ** -0.5)).astype(dq_ref.dtype)
                dk_ref[rows, kc] = (dk_sc + dke * elast).astype(dk_ref.dtype)
                t_ke = dke * ke
                db = q * dq_sc - k * dk_sc + dqe * qe - t_ke
                db = db + jnp.where(ri == C - 1, jnp.sum(t_ke, axis=0, keepdims=True) + deblast * eblast, 0.0)
                dla = _dotx(upper, db)
                yield
                dv_ref[rows, vc] = (dv1 + dv2).astype(dv_ref.dtype)
                dla_ref[rows, kc] = dla
                ds1 = ds1 * eblast + ds_new
            dstate[hh] = ds1

        _round_robin(head(hh) for hh in range(GLA_HEADS))

    rev = lambda n: N - 1 - n
    q_spec, k_spec, v_spec, la_spec, o_spec, s_spec = _gla_specs(rev)
    return pl.pallas_call(
        body, name=name, grid=(N,),
        in_specs=[q_spec, k_spec, v_spec, la_spec, o_spec, s_spec],
        out_specs=[la_spec, la_spec, o_spec, la_spec],
        out_shape=[jax.ShapeDtypeStruct((M, GLA_QK), BF16), jax.ShapeDtypeStruct((M, GLA_QK), BF16),
                   jax.ShapeDtypeStruct((M, GLA_V), BF16), jax.ShapeDtypeStruct((M, GLA_QK), F32)],
        scratch_shapes=[pltpu.VMEM((GLA_HEADS, GLA_DV, GLA_DK), F32)],
        compiler_params=_params("arbitrary"),
    )(proj, proj, proj, la, do, s_all)


def _head_norm(o, wn):
    r = lax.rsqrt(jnp.mean(o * o, axis=-1, keepdims=True) + NORM_EPS)
    return o * r, r


def _mix_heads():
    heads = [(0, GDN_DV, hh * GDN_DV, hh * GDN_DV) for hh in range(GDN_HEADS)]
    heads += [(1, GLA_DV, GDN_V + hh * GLA_DV, hh * GLA_DV) for hh in range(GLA_HEADS)]
    return heads


def _mix_fwd(o_gdn, o_gla, proj, wn_gdn, wn_gla, *, name):
    M = proj.shape[0]
    tm = _tile(M, 344, 16)

    def body(og_ref, ol_ref, z_ref, r_ref, wg_ref, wl_ref, m_ref):
        srcs = ((og_ref, z_ref, wg_ref), (ol_ref, r_ref, wl_ref))
        for grp, width, mcol, col in _mix_heads():
            o_ref, gate_ref, w_ref = srcs[grp]
            xhat, _ = _head_norm(o_ref[:, col:col + width], None)
            gate, _ = _silu_and_grad(gate_ref[:, col:col + width])
            m_ref[:, mcol:mcol + width] = (xhat * w_ref[...] * gate).astype(m_ref.dtype)

    full = lambda s: pl.BlockSpec(s, lambda i: (0, 0))
    return pl.pallas_call(
        body, name=name, grid=(M // tm,),
        in_specs=[pl.BlockSpec((tm, GDN_V), lambda i: (i, 0)), pl.BlockSpec((tm, GLA_V), lambda i: (i, 0)),
                  pl.BlockSpec((tm, GDN_V), lambda i: (i, C_Z // GDN_V)),
                  pl.BlockSpec((tm, GLA_V), lambda i: (i, C_GR // GLA_V)),
                  full((1, GDN_DV)), full((1, GLA_DV))],
        out_specs=pl.BlockSpec((tm, D_MODEL), lambda i: (i, 0)),
        out_shape=jax.ShapeDtypeStruct((M, D_MODEL), BF16),
        compiler_params=_params("parallel"),
    )(o_gdn, o_gla, proj, proj, wn_gdn, wn_gla)


def _mix_bwd(o_gdn, o_gla, proj, wn_gdn, wn_gla, dmixed, *, name):
    M = proj.shape[0]
    tm = _tile(M, 344, 16)
    g_ = M // tm

    def body(og_ref, ol_ref, z_ref, r_ref, wg_ref, wl_ref, dm_ref,
             dog_ref, dol_ref, dz_ref, dr_ref, dwg_ref, dwl_ref):
        i = pl.program_id(0)
        srcs = ((og_ref, z_ref, wg_ref, dog_ref, dz_ref), (ol_ref, r_ref, wl_ref, dol_ref, dr_ref))
        dws = [jnp.zeros((1, GDN_DV), F32), jnp.zeros((1, GLA_DV), F32)]
        for grp, width, mcol, col in _mix_heads():
            o_ref, gate_ref, w_ref, do_ref, dgate_ref = srcs[grp]
            cols = slice(col, col + width)
            xhat, r = _head_norm(o_ref[:, cols], None)
            gate, dgate_dc = _silu_and_grad(gate_ref[:, cols])
            dm = dm_ref[:, mcol:mcol + width]
            dgate_ref[:, cols] = (dm * xhat * w_ref[...] * dgate_dc).astype(dgate_ref.dtype)
            dnorm = dm * gate
            dws[grp] = dws[grp] + jnp.sum(dnorm * xhat, axis=0, keepdims=True)
            dxhat = dnorm * w_ref[...]
            do_ref[:, cols] = r * (dxhat - xhat * jnp.mean(dxhat * xhat, axis=-1, keepdims=True))

        @pl.when(i == 0)
        def _():
            dwg_ref[...] = dws[0]
            dwl_ref[...] = dws[1]

        @pl.when(i > 0)
        def _():
            dwg_ref[...] += dws[0]
            dwl_ref[...] += dws[1]

    full = lambda s: pl.BlockSpec(s, lambda i: (0, 0))
    half = pl.BlockSpec((tm, GDN_V), lambda i: (i, 0))
    return pl.pallas_call(
        body, name=name, grid=(g_,),
        in_specs=[half, half, pl.BlockSpec((tm, GDN_V), lambda i: (i, C_Z // GDN_V)),
                  pl.BlockSpec((tm, GLA_V), lambda i: (i, C_GR // GLA_V)),
                  full((1, GDN_DV)), full((1, GLA_DV)), pl.BlockSpec((tm, D_MODEL), lambda i: (i, 0))],
        out_specs=[half, half, half, half, full((1, GDN_DV)), full((1, GLA_DV))],
        out_shape=[jax.ShapeDtypeStruct((M, GDN_V), F32), jax.ShapeDtypeStruct((M, GLA_V), F32),
                   jax.ShapeDtypeStruct((M, GDN_V), BF16), jax.ShapeDtypeStruct((M, GLA_V), BF16),
                   jax.ShapeDtypeStruct((1, GDN_DV), F32), jax.ShapeDtypeStruct((1, GLA_DV), F32)],
        compiler_params=_params("arbitrary"),
    )(o_gdn, o_gla, proj, proj, wn_gdn, wn_gla, dmixed)


def _swiglu_fwd(gate, up, *, name):
    M, F = gate.shape
    tm, tf = _tile(M, 688, 16), _tile(F, 1408, 128)

    def body(g_ref, u_ref, a_ref):
        s, _ = _silu_and_grad(g_ref[...])
        a_ref[...] = (s * u_ref[...]).astype(a_ref.dtype)

    blk = pl.BlockSpec((tm, tf), lambda i, j: (i, j))
    return pl.pallas_call(
        body, name=name, grid=(M // tm, F // tf), in_specs=[blk, blk], out_specs=blk,
        out_shape=jax.ShapeDtypeStruct((M, F), BF16), compiler_params=_params("parallel", "parallel"),
    )(gate, up)


def _swiglu_bwd(gate, up, da, *, name, after=None):
    M, F = gate.shape
    tm, tf = _tile(M, 688, 16), _tile(F, 1408, 128)
    n_after = 0 if after is None else 1

    def body(*refs):
        g_ref, u_ref, da_ref, dg_ref, du_ref = refs[n_after:]
        s, ds = _silu_and_grad(g_ref[...])
        da_ = da_ref[...]
        dg_ref[...] = (da_ * u_ref[...] * ds).astype(dg_ref.dtype)
        du_ref[...] = (da_ * s).astype(du_ref.dtype)

    blk = pl.BlockSpec((tm, tf), lambda i, j: (i, j))
    return pl.pallas_call(
        body, name=name, grid=(M // tm, F // tf), in_specs=[_ANY] * n_after + [blk, blk, blk], out_specs=[blk, blk],
        out_shape=[jax.ShapeDtypeStruct((M, F), BF16), jax.ShapeDtypeStruct((M, F), BF16)],
        compiler_params=_params("parallel", "parallel"),
    )(*((after,) if n_after else ()), gate, up, da)


def _adamw(w, g, m, v, *, name):
    shape = w.shape
    cols = shape[-1]
    rows = w.size // cols
    w2, g2, m2, v2 = (t.reshape(rows, cols) for t in (w, g, m, v))
    tr = _tile(rows, 256, 8) if rows % 8 == 0 else rows

    def body(w_ref, g_ref, m_ref, v_ref, d_ref, nm_ref, nv_ref):
        g_ = g_ref[...]
        nm = ADAM_B1 * m_ref[...] + (1.0 - ADAM_B1) * g_
        nv = ADAM_B2 * v_ref[...] + (1.0 - ADAM_B2) * (g_ * g_)
        m_hat = nm / (1.0 - ADAM_B1 ** ADAM_STEP)
        v_hat = nv / (1.0 - ADAM_B2 ** ADAM_STEP)
        d_ref[...] = -ADAM_LR * (m_hat / (jnp.sqrt(v_hat) + ADAM_EPS) + ADAM_WD * w_ref[...])
        nm_ref[...] = nm
        nv_ref[...] = nv

    blk = pl.BlockSpec((tr, cols), lambda i: (i, 0))
    outs = pl.pallas_call(
        body, name=name, grid=(rows // tr,), in_specs=[blk] * 4, out_specs=[blk] * 3,
        out_shape=[jax.ShapeDtypeStruct((rows, cols), F32)] * 3, compiler_params=_params("parallel"),
    )(w2, g2, m2, v2)
    return tuple(t.reshape(shape) for t in outs)


def _sum_slabs(x, *, name):
    _, R, C = x.shape
    sub = 16 if x.dtype == BF16 else 8
    if R % sub == 0:
        tr, tc = _tile(R, 128, sub), C
    else:
        tr, tc = R, _tile(C, 256, 128)

    def body(x_ref, o_ref):
        acc = x_ref[0].astype(F32)
        for s in range(1, N_DEV):
            acc = acc + x_ref[s].astype(F32)
        o_ref[...] = acc

    return pl.pallas_call(
        body, name=name, grid=(R // tr, C // tc),
        in_specs=[pl.BlockSpec((N_DEV, tr, tc), lambda i, j: (0, i, j))],
        out_specs=pl.BlockSpec((tr, tc), lambda i, j: (i, j)),
        out_shape=jax.ShapeDtypeStruct((R, C), F32), compiler_params=_params("parallel", "parallel"),
    )(x)


def _peers():
    x, y, c = lax.axis_index("x"), lax.axis_index("y"), lax.axis_index("c")
    me = 4 * x + 2 * y + c
    peers = []
    for k in range(1, N_DEV):
        px = 1 - x if k & 4 else x
        py = 1 - y if k & 2 else y
        pc = 1 - c if k & 1 else c
        peers.append(((px, py, pc), 4 * px + 2 * py + pc))
    return me, peers


def _exchange(x, *, gather, name):
    slab = x.shape if gather else x.shape[1:]

    def body(x_ref, o_ref, send_sems, recv_sems, own_sem):
        me, peers = _peers()
        own = pltpu.make_async_copy(x_ref if gather else x_ref.at[me], o_ref.at[me], own_sem)
        own.start()
        sends, recvs = [], []
        for k, (pos, idx) in enumerate(peers):
            sends.append(pltpu.make_async_remote_copy(
                src_ref=x_ref if gather else x_ref.at[idx], dst_ref=o_ref.at[me],
                send_sem=send_sems.at[k], recv_sem=recv_sems.at[k],
                device_id=pos, device_id_type=pl.DeviceIdType.MESH))
            recvs.append(pltpu.make_async_remote_copy(
                src_ref=x_ref if gather else x_ref.at[idx], dst_ref=o_ref.at[idx],
                send_sem=send_sems.at[k], recv_sem=recv_sems.at[k],
                device_id=pos, device_id_type=pl.DeviceIdType.MESH))
        for cp in sends:
            cp.start()
        for cp in recvs:
            cp.wait_recv()
        for cp in sends:
            cp.wait_send()
        own.wait()

    hbm = pl.BlockSpec(memory_space=pltpu.HBM)
    return pl.pallas_call(
        body, name=name, in_specs=[hbm], out_specs=hbm,
        out_shape=jax.ShapeDtypeStruct((N_DEV,) + tuple(slab), x.dtype),
        scratch_shapes=[pltpu.SemaphoreType.DMA((N_DEV - 1,)), pltpu.SemaphoreType.DMA((N_DEV - 1,)),
                        pltpu.SemaphoreType.DMA],
    )(x)


_HBM = pl.BlockSpec(memory_space=pltpu.HBM)
_SEM = pl.BlockSpec(memory_space=pltpu.SEMAPHORE)
_EFFECT = pltpu.SideEffectType.DATAFLOW_SIDE_EFFECTING


def _exchange_start(x, *, gather, name, after=None):
    slab = x.shape if gather else x.shape[1:]
    n_after = 0 if after is None else 1

    def body(*refs):
        x_ref, land_ref, send_sems, recv_sems, _, _, token = refs[n_after:]
        me, peers = _peers()
        for k, (pos, idx) in enumerate(peers):
            pltpu.make_async_remote_copy(
                src_ref=x_ref if gather else x_ref.at[idx], dst_ref=land_ref.at[me],
                send_sem=send_sems.at[k], recv_sem=recv_sems.at[k],
                device_id=pos, device_id_type=pl.DeviceIdType.MESH).start()
        token[...] = jnp.zeros_like(token)

    land = lax.empty((N_DEV,) + tuple(slab), x.dtype)
    return pl.pallas_call(
        body, name=name,
        out_shape=(pltpu.SemaphoreType.DMA((N_DEV - 1,)), pltpu.SemaphoreType.DMA((N_DEV - 1,)),
                   pltpu.HBM(x.shape, x.dtype), pltpu.HBM(land.shape, land.dtype), jax.ShapeDtypeStruct((8, 128), F32)),
        in_specs=[_ANY] * n_after + [_HBM, _HBM],
        out_specs=(_SEM, _SEM, _HBM, _HBM, pl.BlockSpec(memory_space=pltpu.VMEM)),
        input_output_aliases={n_after: 2, n_after + 1: 3},
        compiler_params=pltpu.CompilerParams(has_side_effects=_EFFECT),
    )(*((after,) if n_after else ()), pltpu.with_memory_space_constraint(x, pltpu.HBM),
      pltpu.with_memory_space_constraint(land, pltpu.HBM))


def _exchange_wait(handle, after, *, gather, name):
    send_sems, recv_sems, x_thru, land_thru, _ = handle

    def body(x_ref, land_ref, send_sems, recv_sems, after_ref, x_out, land_out):
        me, peers = _peers()
        for k, (pos, idx) in enumerate(peers):
            cp = pltpu.make_async_remote_copy(
                src_ref=x_ref if gather else x_ref.at[idx], dst_ref=land_ref.at[idx],
                send_sem=send_sems.at[k], recv_sem=recv_sems.at[k],
                device_id=pos, device_id_type=pl.DeviceIdType.MESH)
            cp.wait_send()
            cp.wait_recv()

    return pl.pallas_call(
        body, name=name,
        out_shape=(pltpu.HBM(x_thru.shape, x_thru.dtype), pltpu.HBM(land_thru.shape, land_thru.dtype)),
        in_specs=(_HBM, _HBM, _SEM, _SEM, _ANY), out_specs=(_HBM, _HBM), input_output_aliases={0: 0, 1: 1},
        compiler_params=pltpu.CompilerParams(has_side_effects=_EFFECT),
    )(x_thru, land_thru, send_sems, recv_sems, after)


def _to_proj_rows(t):
    z = jnp.zeros((SM_W - 2 * GDN_HEADS - GLA_RANK + D_PROJ - C_SM - SM_W,) + t.shape[1:], t.dtype)
    return jnp.concatenate([t[:R_A], t[R_GQ:R_LR], t[R_A:R_GQ], t[R_LR:], z], axis=0)


def _from_proj_rows(t):
    return jnp.concatenate([t[:C_GQ], t[C_SM:C_SM + 2 * GDN_HEADS], t[C_GQ:C_SM],
                            t[C_SM + 2 * GDN_HEADS:C_SM + 2 * GDN_HEADS + GLA_RANK]], axis=0)


def _local_step(x, target, meta, attn_nw, conv_w, a_log, dt_bias, gdn_nw, w2, b2, gla_nw, ffn_nw, final_nw,
                fetch, emit, start=None):
    S = x.shape[0]
    h0 = jnp.concatenate([jnp.zeros((ROW_PAD, D_MODEL), F32), meta, x], axis=0)
    target_p = jnp.concatenate([jnp.zeros((HEAD_ROWS, D_MODEL), F32), target], axis=0)
    conv_w8 = jnp.concatenate([conv_w, jnp.zeros((8 - CONV_K, conv_w.shape[1]), F32)], axis=0)
    w2p = jnp.zeros((SM_W, GLA_QK), F32).at[2 * GDN_HEADS:2 * GDN_HEADS + GLA_RANK].set(w2)
    alog_p = jnp.zeros((1, SM_W), F32).at[:, :GDN_HEADS].set(a_log)
    dt_p = jnp.zeros((1, SM_W), F32).at[:, :GDN_HEADS].set(dt_bias)

    n1 = _rmsnorm_fwd(h0, attn_nw, name="attn_norm", after=start)
    w_in_t = fetch("w_in_t", n1)
    proj = _matmul(n1, w_in_t, mode="nt", name="in_proj")
    gb, la = _gates_fwd(proj, w2p, b2, alog_p, dt_p, name="gates")
    act = _prep_fwd(proj, conv_w8, name="gdn_prep")
    o_gdn, s_gdn, t_gdn = _gdn_fwd(act, gb, name="gdn_fwd")
    o_gla, s_gla = _gla_fwd(proj, la, name="gla_fwd")
    mixed = _mix_fwd(o_gdn, o_gla, proj, gdn_nw, gla_nw, name="mix")
    w_gate_t, w_up_t, w_out, w_down = fetch("rest", mixed)
    h1 = _matmul(mixed, w_out, mode="nn", add=h0, name="out_proj")
    n2 = _rmsnorm_fwd(h1, ffn_nw, name="ffn_norm")
    gate = _matmul(n2, w_gate_t, mode="nt", name="ffn_gate")
    up = _matmul(n2, w_up_t, mode="nt", name="ffn_up")
    hid = _swiglu_fwd(gate, up, name="swiglu")
    h2 = _matmul(hid, w_down, mode="nn", add=h1, name="ffn_down")
    dh2, d_final_nw, loss = _loss_head(h2, final_nw, target_p, name="loss_head")

    dh2_b = dh2.astype(BF16)
    d_hid = _matmul(dh2_b, w_down, mode="nt", name="d_hid")
    wg = dict(mode="tn", out_dtype=BF16, tn=1024, tk=1376)
    tok = emit("w_down", _matmul(hid, dh2_b, name="d_w_down", tm=1408, **wg))
    d_gate, d_up = _swiglu_bwd(gate, up, d_hid, name="d_swiglu", after=tok)
    tok = emit("w_gate_t", _matmul(d_gate, n2, name="d_w_gate", tm=1408, **wg))
    tok = emit("w_up_t", _matmul(d_up, n2, name="d_w_up", tm=1408, after=tok, **wg))
    d_n2 = _matmul(d_gate, w_gate_t, mode="nn", name="d_n2_gate", tk=1408, after=tok)
    d_n2 = _matmul(d_up, w_up_t, mode="nn", add=d_n2, name="d_n2_up", tk=1408)
    dh1, d_ffn_nw = _rmsnorm_bwd(h1, ffn_nw, d_n2, dh2, name="d_ffn_norm")

    dh1_b = dh1.astype(BF16)
    tok = emit("w_out", _matmul(mixed, dh1_b, name="d_w_out", tm=1024, **wg))
    d_mixed = _matmul(dh1_b, w_out, mode="nt", name="d_mixed", after=tok)
    do_gdn, do_gla, dz, dr, d_gdn_nw, d_gla_nw = _mix_bwd(o_gdn, o_gla, proj, gdn_nw, gla_nw, d_mixed, name="d_mix")
    d_gq, d_gk, d_gv, d_la = _gla_bwd(proj, la, do_gla, s_gla, name="gla_bwd")
    dq, dk, dv, dgb_heads = _gdn_bwd(act, gb, do_gdn, s_gdn, t_gdn, name="gdn_bwd")
    dsm, d_w2p, d_b2, d_alog, d_dt = _gates_bwd(proj, w2p, b2, alog_p, dt_p, dgb_heads, d_la, name="d_gates")
    dc, d_conv_w8 = _prep_bwd_a(proj, conv_w8, jnp.concatenate([dq, dk, dv], axis=1), name="d_gdn_prep")
    d_qkv = _prep_bwd_b(dc, conv_w8, name="d_conv")
    d_proj = jnp.concatenate([d_qkv, dz, d_gq, d_gk, d_gv, dr, dsm,
                              jnp.zeros((S + HEAD_ROWS, D_PROJ - C_SM - SM_W), BF16)], axis=1)
    tok = emit("w_in_t", _matmul(d_proj, n1, name="d_w_in", tm=1536, **wg))
    d_n1 = _matmul(d_proj, w_in_t, mode="nn", name="d_n1", tk=1536, after=tok)
    dh0, d_attn_nw = _rmsnorm_bwd(h0, attn_nw, d_n1, dh1, name="d_attn_norm")

    return dict(
        loss=loss[0, 0], grad_x=dh0[HEAD_ROWS:], meta=dh0[ROW_PAD:HEAD_ROWS], attn_nw=d_attn_nw,
        conv_w=d_conv_w8[:CONV_K], a_log=d_alog[:, :GDN_HEADS], dt_bias=d_dt[:, :GDN_HEADS], gdn_nw=d_gdn_nw,
        w2=d_w2p[2 * GDN_HEADS:2 * GDN_HEADS + GLA_RANK], b2=d_b2, gla_nw=d_gla_nw, ffn_nw=d_ffn_nw,
        final_nw=d_final_nw)


SMALL_ROWS = 32


def kernel(x, meta_tokens, attn_norm_w, w_in, gdn_conv_w, gdn_a_log, gdn_dt_bias, gdn_norm_w, gla_gate_w2, gla_gate_b, gla_norm_w, w_out, ffn_norm_w, w_gate, w_up, w_down, final_norm_w, loss_target, m_meta_tokens, m_attn_norm_w, m_w_in, m_gdn_conv_w, m_gdn_a_log, m_gdn_dt_bias, m_gdn_norm_w, m_gla_gate_w2, m_gla_gate_b, m_gla_norm_w, m_w_out, m_ffn_norm_w, m_w_gate, m_w_up, m_w_down, m_final_norm_w, v_meta_tokens, v_attn_norm_w, v_w_in, v_gdn_conv_w, v_gdn_a_log, v_gdn_dt_bias, v_gdn_norm_w, v_gla_gate_w2, v_gla_gate_b, v_gla_norm_w, v_w_out, v_ffn_norm_w, v_w_gate, v_w_up, v_w_down, v_final_norm_w):
    me = 4 * lax.axis_index("x") + 2 * lax.axis_index("y") + lax.axis_index("c")
    n_in, n_ff, n_out = D_IN // N_DEV, D_FF // N_DEV, D_MODEL // N_DEV

    n_conv = gdn_conv_w.shape[2]
    n_w2 = gla_gate_w2.shape[2]
    n_meta = meta_tokens.shape[1]
    small = jnp.zeros((40, n_conv), F32)
    small = small.at[0:N_META, :n_meta].set(meta_tokens)
    small = small.at[N_META:N_META + CONV_K, :].set(gdn_conv_w[0])
    small = small.at[24:24 + GLA_RANK, :n_w2].set(gla_gate_w2[0])
    small_all = _exchange(small, gather=True, name="gather_small")
    meta_f = small_all[:, 0:N_META, :n_meta].transpose(1, 0, 2).reshape(N_META, D_MODEL)
    conv_f = small_all[:, N_META:N_META + CONV_K, :].transpose(1, 0, 2).reshape(CONV_K, N_DEV * n_conv)
    w2_f = small_all[:, 24:24 + GLA_RANK, :n_w2].transpose(1, 0, 2).reshape(GLA_RANK, N_DEV * n_w2)

    o1, o2, o3 = n_ff, 2 * n_ff, 2 * n_ff + n_out
    in_h = _exchange_start(w_in[0].T.astype(BF16), gather=True, name="gather_w_in_start")
    rest = jnp.concatenate([w_gate[0].T, w_up[0].T, w_out[0], w_down[0]], axis=0).astype(BF16)
    rest_h = _exchange_start(rest, gather=True, name="gather_rest_start", after=in_h[4])

    def fetch(name, after):
        handle = in_h if name == "w_in_t" else rest_h
        own, got = _exchange_wait(handle, after, gather=True, name="gather_" + name + "_wait")
        got = lax.dynamic_update_index_in_dim(got, own, me, 0)
        if name == "w_in_t":
            return _to_proj_rows(got.reshape(D_IN, D_MODEL))
        return (got[:, :o1].reshape(D_FF, D_MODEL), got[:, o1:o2].reshape(D_FF, D_MODEL),
                got[:, o2:o3].reshape(D_MODEL, D_MODEL), got[:, o3:].reshape(D_FF, D_MODEL))

    sent = {}

    def emit(name, grad):
        if name == "w_in_t":
            grad = _from_proj_rows(grad)
        parts = grad.reshape(N_DEV, grad.shape[0] // N_DEV, D_MODEL)
        sent[name] = _exchange_start(parts, gather=False, name="scatter_" + name + "_start")
        return sent[name][4]

    g = _local_step(x[0], loss_target[0], meta_f, attn_norm_w, conv_f, gdn_a_log, gdn_dt_bias, gdn_norm_w, w2_f,
                    gla_gate_b, gla_norm_w, ffn_norm_w, final_norm_w.reshape(1, D_MODEL), fetch, emit, start=rest_h[4])

    def total(name, after):
        handle = sent[name]
        own, got = _exchange_wait(handle, after, gather=False, name="scatter_" + name + "_wait")
        got = lax.dynamic_update_index_in_dim(got, lax.dynamic_index_in_dim(own, me, 0, keepdims=False), me, 0)
        return _sum_slabs(got, name="sum_" + name)

    grad_w_down = total("w_down", g["grad_x"])[None]
    grad_w_gate = total("w_gate_t", grad_w_down).T[None]
    grad_w_up = total("w_up_t", grad_w_gate).T[None]
    grad_w_out = total("w_out", grad_w_up)[None]
    grad_w_in = total("w_in_t", grad_w_out).T[None]

    misc = jnp.concatenate([g["a_log"], g["dt_bias"], g["gdn_nw"], g["gla_nw"], g["b2"], g["loss"].reshape(1, 1)], axis=1)
    n_misc = misc.shape[1]
    misc = jnp.pad(misc, ((0, 0), (0, D_MODEL - n_misc)))
    rows = jnp.concatenate([g["attn_nw"], g["ffn_nw"], g["final_nw"], misc, g["meta"],
                            g["conv_w"].reshape(-1, D_MODEL), g["w2"].reshape(-1, D_MODEL)], axis=0)
    rows = jnp.pad(rows, ((0, SMALL_ROWS - rows.shape[0]), (0, 0)))
    tot = _sum_slabs(_exchange(rows, gather=True, name="gather_small_grads"), name="sum_small_grads")
    grad_attn_nw, grad_ffn_nw, grad_final_nw = tot[0:1], tot[1:2], tot[2]
    grad_a_log = tot[3:4, 0:8]
    grad_dt = tot[3:4, 8:16]
    grad_gdn_nw = tot[3:4, 16:16 + GDN_DV]
    grad_gla_nw = tot[3:4, 144:144 + GLA_DV]
    grad_b2 = tot[3:4, 400:400 + GLA_QK]
    loss = tot[3, n_misc - 1]
    r0 = 4 + N_META
    grad_meta = lax.dynamic_slice(tot[4:r0], (0, me * n_meta), (N_META, n_meta))
    r1 = r0 + CONV_K * N_DEV * n_conv // D_MODEL
    grad_conv = lax.dynamic_slice(tot[r0:r1].reshape(CONV_K, N_DEV * n_conv), (0, me * n_conv), (CONV_K, n_conv))[None]
    r2 = r1 + GLA_RANK * N_DEV * n_w2 // D_MODEL
    grad_w2 = lax.dynamic_slice(tot[r1:r2].reshape(GLA_RANK, N_DEV * n_w2), (0, me * n_w2), (GLA_RANK, n_w2))[None]

    weights = [meta_tokens, attn_norm_w, w_in, gdn_conv_w, gdn_a_log, gdn_dt_bias, gdn_norm_w, gla_gate_w2,
               gla_gate_b, gla_norm_w, w_out, ffn_norm_w, w_gate, w_up, w_down, final_norm_w]
    grads = [grad_meta, grad_attn_nw, grad_w_in, grad_conv, grad_a_log, grad_dt, grad_gdn_nw, grad_w2,
             grad_b2, grad_gla_nw, grad_w_out, grad_ffn_nw, grad_w_gate, grad_w_up, grad_w_down, grad_final_nw]
    ms = [m_meta_tokens, m_attn_norm_w, m_w_in, m_gdn_conv_w, m_gdn_a_log, m_gdn_dt_bias, m_gdn_norm_w,
          m_gla_gate_w2, m_gla_gate_b, m_gla_norm_w, m_w_out, m_ffn_norm_w, m_w_gate, m_w_up, m_w_down, m_final_norm_w]
    vs = [v_meta_tokens, v_attn_norm_w, v_w_in, v_gdn_conv_w, v_gdn_a_log, v_gdn_dt_bias, v_gdn_norm_w,
          v_gla_gate_w2, v_gla_gate_b, v_gla_norm_w, v_w_out, v_ffn_norm_w, v_w_gate, v_w_up, v_w_down, v_final_norm_w]
    grads = [gr.reshape(w.shape) for gr, w in zip(grads, weights)]
    deltas, new_ms, new_vs = [], [], []
    for idx, (w, gr, m, v) in enumerate(zip(weights, grads, ms, vs)):
        d, nm, nv = _adamw(w, gr, m, v, name=f"adamw_{idx}")
        deltas.append(d)
        new_ms.append(nm)
        new_vs.append(nv)
    return (loss, g["grad_x"][None], *grads, *deltas, *new_ms, *new_vs)
```

```python
import functools

import jax
import jax.numpy as jnp
from jax import lax
from jax.experimental import pallas as pl
from jax.experimental.pallas import tpu as pltpu

F32 = jnp.float32
BF16 = jnp.bfloat16
_MXU_DTYPE = jnp.bfloat16

D_MODEL = 2048
N_META = 16
ROW_PAD = 48
HEAD_ROWS = ROW_PAD + N_META
CONV_K = 4
GDN_HEADS, GDN_DK, GDN_DV, GDN_CHUNK = 8, 128, 128, 64
GLA_HEADS, GLA_DK, GLA_DV, GLA_CHUNK = 4, 128, 256, 16
GLA_RANK = 16
GLA_GATE_NORMALIZER = 16.0
GDN_QK = GDN_HEADS * GDN_DK
GDN_V = GDN_HEADS * GDN_DV
GLA_QK = GLA_HEADS * GLA_DK
GLA_V = GLA_HEADS * GLA_DV
D_FF = 5632
D_IN = 7200
NORM_EPS = 1e-6
C_QKV, C_Z, C_GQ, C_GK, C_GV, C_GR, C_SM = 0, 3072, 4096, 4608, 5120, 6144, 7168
SM_W = 128
D_PROJ = 7680
R_Z, R_A, R_B, R_GQ, R_GK, R_GV, R_GR, R_LR = 3072, 4096, 4104, 4112, 4624, 5136, 6160, 7184

ADAM_LR, ADAM_B1, ADAM_B2, ADAM_EPS, ADAM_WD, ADAM_STEP = 0.001, 0.9, 0.999, 1e-08, 0.01, 10

N_DEV = 8
VMEM_LIMIT = 56 * 1024 * 1024

NN = (((1,), (0,)), ((), ()))
NT = (((1,), (1,)), ((), ()))
TN = (((0,), (0,)), ((), ()))


def _dot(a, b, dims=NN):
    return lax.dot_general(a.astype(_MXU_DTYPE), b.astype(_MXU_DTYPE), dims, preferred_element_type=F32)


def _dotx(a, b, dims=NN):
    return lax.dot_general(a, b, dims, precision=lax.Precision.HIGHEST, preferred_element_type=F32)


def _dot3(a, b):
    ah = a.astype(BF16)
    al = (a - ah.astype(F32)).astype(BF16)
    bh = b.astype(BF16)
    bl = (b - bh.astype(F32)).astype(BF16)
    d = functools.partial(lax.dot_general, dimension_numbers=NN, preferred_element_type=F32)
    return d(ah, bh) + (d(ah, bl) + d(al, bh))


def _tile(n, target, mult=8):
    best = None
    for t in range(mult, min(n, target) + 1, mult):
        if n % t == 0:
            best = t
    return best if best is not None else n


def _params(*sem):
    return pltpu.CompilerParams(dimension_semantics=sem, vmem_limit_bytes=VMEM_LIMIT)


def _sigmoid(x):
    return 0.5 * jnp.tanh(0.5 * x) + 0.5


def _softplus(x):
    return jnp.maximum(x, 0.0) + jnp.log1p(jnp.exp(-jnp.abs(x)))


def _silu_and_grad(c):
    s = _sigmoid(c)
    return c * s, s * (1.0 + c * (1.0 - s))


_ANY = pl.BlockSpec(memory_space=pl.ANY)


def _matmul(a, b, *, mode, name, out_dtype=F32, add=None, after=None, tm=1376, tn=512, tk=2064):
    if mode == "tn":
        K, M = a.shape
        N = b.shape[1]
    else:
        M, K = a.shape
        N = b.shape[0] if mode == "nt" else b.shape[1]
    tm = _tile(M, tm, 128 if mode == "tn" else 16)
    tn = _tile(N, tn, 128)
    tk = _tile(K, tk, 16 if mode == "tn" else 128)
    gm, gn, gk = M // tm, N // tn, K // tk
    dims = {"nn": NN, "nt": NT, "tn": TN}[mode]

    n_after = 0 if after is None else 1

    def body(*refs):
        refs = refs[n_after:]
        if add is None:
            a_ref, b_ref, o_ref = refs[:3]
            add_ref = None
        else:
            a_ref, b_ref, add_ref, o_ref = refs[:4]
        p = _dot(a_ref[...], b_ref[...], dims)

        def finish(r):
            if add_ref is not None:
                r = r + add_ref[...]
            o_ref[...] = r.astype(out_dtype)

        if gk == 1:
            finish(p)
        else:
            acc_ref = refs[-1]
            k = pl.program_id(2)

            @pl.when(k == 0)
            def _():
                acc_ref[...] = p

            @pl.when(k > 0)
            def _():
                acc_ref[...] += p

            @pl.when(k == gk - 1)
            def _():
                finish(acc_ref[...])

    if mode == "tn":
        a_spec = pl.BlockSpec((tk, tm), lambda i, j, k: (k, i))
    else:
        a_spec = pl.BlockSpec((tm, tk), lambda i, j, k: (i, k))
    if mode == "nt":
        b_spec = pl.BlockSpec((tn, tk), lambda i, j, k: (j, k))
    else:
        b_spec = pl.BlockSpec((tk, tn), lambda i, j, k: (k, j))
    o_spec = pl.BlockSpec((tm, tn), lambda i, j, k: (i, j))
    in_specs = [_ANY] * n_after + [a_spec, b_spec] + ([o_spec] if add is not None else [])
    args = ((after,) if n_after else ()) + (a, b) + ((add,) if add is not None else ())
    return pl.pallas_call(
        body, name=name, grid=(gm, gn, gk), in_specs=in_specs, out_specs=o_spec,
        out_shape=jax.ShapeDtypeStruct((M, N), out_dtype),
        scratch_shapes=[pltpu.VMEM((tm, tn), F32)] if gk > 1 else [],
        compiler_params=_params("parallel", "parallel", "arbitrary"),
    )(*args)


def _rmsnorm_fwd(h, w, *, name, after=None):
    M, D = h.shape
    tm = _tile(M, 688, 16)
    n_after = 0 if after is None else 1

    def body(*refs):
        h_ref, w_ref, n_ref = refs[n_after:]
        x = h_ref[...]
        r = lax.rsqrt(jnp.mean(x * x, axis=-1, keepdims=True) + NORM_EPS)
        n_ref[...] = (x * r * w_ref[...]).astype(n_ref.dtype)

    return pl.pallas_call(
        body, name=name, grid=(M // tm,),
        in_specs=[_ANY] * n_after + [pl.BlockSpec((tm, D), lambda i: (i, 0)), pl.BlockSpec((1, D), lambda i: (0, 0))],
        out_specs=pl.BlockSpec((tm, D), lambda i: (i, 0)),
        out_shape=jax.ShapeDtypeStruct((M, D), BF16),
        compiler_params=_params("parallel"),
    )(*((after,) if n_after else ()), h, w)


def _rmsnorm_bwd(h, w, dn, dres, *, name, also_bf16):
    M, D = h.shape
    tm = _tile(M, 344, 16)
    g = M // tm

    def body(h_ref, w_ref, dn_ref, dres_ref, dh_ref, *rest):
        dhb_ref = rest[0] if also_bf16 else None
        dw_ref, acc_ref = rest[-2:]
        i = pl.program_id(0)
        x = h_ref[...]
        r = lax.rsqrt(jnp.mean(x * x, axis=-1, keepdims=True) + NORM_EPS)
        xhat = x * r
        dn_ = dn_ref[...]
        dxhat = dn_ * w_ref[...]
        dh = dres_ref[...] + r * (dxhat - xhat * jnp.mean(dxhat * xhat, axis=-1, keepdims=True))
        dh_ref[...] = dh
        if also_bf16:
            dhb_ref[...] = dh.astype(dhb_ref.dtype)
        part = jnp.sum((dn_ * xhat).reshape(tm // 8, 8, D), axis=0)

        @pl.when(i == 0)
        def _():
            acc_ref[...] = part

        @pl.when(i > 0)
        def _():
            acc_ref[...] += part

        @pl.when(i == g - 1)
        def _():
            dw_ref[...] = jnp.sum(acc_ref[...], axis=0, keepdims=True)

    row = pl.BlockSpec((tm, D), lambda i: (i, 0))
    vec = pl.BlockSpec((1, D), lambda i: (0, 0))
    return pl.pallas_call(
        body, name=name, grid=(g,), in_specs=[row, vec, row, row],
        out_specs=[row] + ([row] if also_bf16 else []) + [vec],
        out_shape=[jax.ShapeDtypeStruct((M, D), F32)] + ([jax.ShapeDtypeStruct((M, D), BF16)] if also_bf16 else [])
        + [jax.ShapeDtypeStruct((1, D), F32)],
        scratch_shapes=[pltpu.VMEM((8, D), F32)],
        compiler_params=_params("arbitrary"),
    )(h, w, dn, dres)


def _loss_head(h, w, target_p, *, name):
    M, D = h.shape
    tm = _tile(M, 344, 16)
    g = M // tm

    def body(h_ref, w_ref, t_ref, dh_ref, dhb_ref, dw_ref, loss_ref, acc_ref, lacc_ref):
        i = pl.program_id(0)
        x = h_ref[...]
        row = i * tm + lax.broadcasted_iota(jnp.int32, (tm, 1), 0)
        live = row >= HEAD_ROWS
        r = lax.rsqrt(jnp.mean(x * x, axis=-1, keepdims=True) + NORM_EPS)
        xhat = x * r
        err = jnp.where(live, xhat * w_ref[...] - t_ref[...], 0.0)
        dy = err * (1.0 / D)
        dxhat = dy * w_ref[...]
        dh = r * (dxhat - xhat * jnp.mean(dxhat * xhat, axis=-1, keepdims=True))
        dh_ref[...] = dh
        dhb_ref[...] = dh.astype(dhb_ref.dtype)
        part = jnp.sum((dy * xhat).reshape(tm // 8, 8, D), axis=0)
        lpart = jnp.sum((err * err).reshape(tm // 8, 8, D), axis=0)

        @pl.when(i == 0)
        def _():
            acc_ref[...] = part
            lacc_ref[...] = lpart

        @pl.when(i > 0)
        def _():
            acc_ref[...] += part
            lacc_ref[...] += lpart

        @pl.when(i == g - 1)
        def _():
            dw_ref[...] = jnp.sum(acc_ref[...], axis=0, keepdims=True)
            tot = jnp.sum(jnp.sum(lacc_ref[...], axis=0, keepdims=True), axis=1, keepdims=True)
            loss_ref[...] = jnp.broadcast_to(tot * (0.5 / D), (1, 128))

    row = pl.BlockSpec((tm, D), lambda i: (i, 0))
    vec = pl.BlockSpec((1, D), lambda i: (0, 0))
    return pl.pallas_call(
        body, name=name, grid=(g,), in_specs=[row, vec, row],
        out_specs=[row, row, vec, pl.BlockSpec((1, 128), lambda i: (0, 0))],
        out_shape=[jax.ShapeDtypeStruct((M, D), F32), jax.ShapeDtypeStruct((M, D), BF16),
                   jax.ShapeDtypeStruct((1, D), F32), jax.ShapeDtypeStruct((1, 128), F32)],
        scratch_shapes=[pltpu.VMEM((8, D), F32), pltpu.VMEM((8, D), F32)],
        compiler_params=_params("arbitrary"),
    )(h, w, target_p)


def _gate_terms(sm, w2p, b2, alog_p, dt_p, row0):
    tm = sm.shape[0]
    lane = lax.broadcasted_iota(jnp.int32, (tm, SM_W), 1)
    live = (row0 + lax.broadcasted_iota(jnp.int32, (tm, 1), 0)) >= ROW_PAD
    pre = sm + dt_p
    neg_a = -jnp.exp(alog_p)
    g = neg_a * _softplus(pre)
    beta = _sigmoid(sm)
    z = _dot(sm, w2p) + b2
    return lane, live, pre, neg_a, g, beta, z


def _gates_fwd(proj, w2p, b2, alog_p, dt_p, *, name):
    M = proj.shape[0]
    tm = _tile(M, 688, 8)

    def body(sm_ref, w2_ref, b2_ref, al_ref, dt_ref, gb_ref, la_ref):
        row0 = pl.program_id(0) * tm
        lane, live, _, _, g, beta, z = _gate_terms(sm_ref[...], w2_ref[...], b2_ref[...], al_ref[...], dt_ref[...], row0)
        gb = jnp.where(lane < GDN_HEADS, g, jnp.where(lane < 2 * GDN_HEADS, beta, 0.0))
        gb_ref[...] = jnp.where(live, gb, 0.0)
        la = (jnp.minimum(z, 0.0) - jnp.log1p(jnp.exp(-jnp.abs(z)))) * (1.0 / GLA_GATE_NORMALIZER)
        la_ref[...] = jnp.where(live, la, 0.0)

    full = lambda s: pl.BlockSpec(s, lambda i: (0, 0))
    return pl.pallas_call(
        body, name=name, grid=(M // tm,),
        in_specs=[pl.BlockSpec((tm, SM_W), lambda i: (i, C_SM // SM_W)), full((SM_W, GLA_QK)), full((1, GLA_QK)),
                  full((1, SM_W)), full((1, SM_W))],
        out_specs=[pl.BlockSpec((tm, SM_W), lambda i: (i, 0)), pl.BlockSpec((tm, GLA_QK), lambda i: (i, 0))],
        out_shape=[jax.ShapeDtypeStruct((M, SM_W), F32), jax.ShapeDtypeStruct((M, GLA_QK), F32)],
        compiler_params=_params("parallel"),
    )(proj, w2p, b2, alog_p, dt_p)


def _gates_bwd(proj, w2p, b2, alog_p, dt_p, dgb_heads, dla, *, name):
    M = proj.shape[0]
    tm = _tile(M, 688, 8)
    g_ = M // tm

    def body(sm_ref, w2_ref, b2_ref, al_ref, dt_ref, dgb_ref, dla_ref,
             dsm_ref, dw2_ref, db2_ref, dal_ref, ddt_ref):
        i = pl.program_id(0)
        sm = sm_ref[...]
        lane, live, pre, neg_a, g, beta, z = _gate_terms(sm, w2_ref[...], b2_ref[...], al_ref[...], dt_ref[...], i * tm)
        dz = jnp.where(live, dla_ref[...] * (_sigmoid(-z) * (1.0 / GLA_GATE_NORMALIZER)), 0.0)
        dsm_lr = _dot(dz, w2_ref[...], NT)
        dgb = dgb_ref[0]
        for hh in range(1, GDN_HEADS):
            dgb = dgb + dgb_ref[hh]
        dgb = jnp.where(live, dgb, 0.0)
        da = dgb * neg_a * _sigmoid(pre)
        db = dgb * beta * (1.0 - beta)
        dsm = jnp.where(lane < GDN_HEADS, da, jnp.where(lane < 2 * GDN_HEADS, db, dsm_lr))
        dsm_ref[...] = dsm.astype(dsm_ref.dtype)
        is_a = lane < GDN_HEADS
        dal = jnp.sum(jnp.where(is_a, dgb * g, 0.0), axis=0, keepdims=True)
        ddt = jnp.sum(jnp.where(is_a, da, 0.0), axis=0, keepdims=True)
        dw2 = _dot(sm, dz, TN)
        db2 = jnp.sum(dz, axis=0, keepdims=True)

        @pl.when(i == 0)
        def _():
            dw2_ref[...] = dw2
            db2_ref[...] = db2
            dal_ref[...] = dal
            ddt_ref[...] = ddt

        @pl.when(i > 0)
        def _():
            dw2_ref[...] += dw2
            db2_ref[...] += db2
            dal_ref[...] += dal
            ddt_ref[...] += ddt

    full = lambda s: pl.BlockSpec(s, lambda i: (0, 0))
    return pl.pallas_call(
        body, name=name, grid=(g_,),
        in_specs=[pl.BlockSpec((tm, SM_W), lambda i: (i, C_SM // SM_W)), full((SM_W, GLA_QK)), full((1, GLA_QK)),
                  full((1, SM_W)), full((1, SM_W)),
                  pl.BlockSpec((GDN_HEADS, tm, SM_W), lambda i: (0, i, 0)),
                  pl.BlockSpec((tm, GLA_QK), lambda i: (i, 0))],
        out_specs=[pl.BlockSpec((tm, SM_W), lambda i: (i, 0)), full((SM_W, GLA_QK)), full((1, GLA_QK)),
                   full((1, SM_W)), full((1, SM_W))],
        out_shape=[jax.ShapeDtypeStruct((M, SM_W), BF16), jax.ShapeDtypeStruct((SM_W, GLA_QK), F32),
                   jax.ShapeDtypeStruct((1, GLA_QK), F32), jax.ShapeDtypeStruct((1, SM_W), F32),
                   jax.ShapeDtypeStruct((1, SM_W), F32)],
        compiler_params=_params("arbitrary"),
    )(proj, w2p, b2, alog_p, dt_p, dgb_heads, dla)


QKV_W = GDN_QK
N_QKV_GROUPS = 3
HALO = 8


def _conv_terms(x_ref, halo_ref, cw_ref, xs_ref, i, tm):
    xs_ref[HALO:HALO + tm, :] = x_ref[...]
    xs_ref[0:HALO, :] = jnp.where(i > 0, halo_ref[...], 0.0)
    cw = cw_ref[...]
    xs = xs_ref[...]
    taps = [(pltpu.roll(xs, CONV_K - 1 - t, 0) if t < CONV_K - 1 else xs)[HALO:HALO + tm, :] for t in range(CONV_K)]
    c = taps[0] * cw[0:1, :]
    for t in range(1, CONV_K):
        c = c + taps[t] * cw[t:t + 1, :]
    return c, taps


def _prep_fwd(proj, conv_w8, *, name):
    M = proj.shape[0]
    tm = _tile(M, 344, 8)

    def body(x_ref, halo_ref, cw_ref, o_ref, xs_ref):
        j, i = pl.program_id(0), pl.program_id(1)
        c, _ = _conv_terms(x_ref, halo_ref, cw_ref, xs_ref, i, tm)
        s, _ = _silu_and_grad(c)
        scale = jnp.where(j == 0, GDN_DK ** -0.5, 1.0)
        for hh in range(GDN_HEADS):
            cols = slice(hh * 128, (hh + 1) * 128)
            sh = s[:, cols]
            r = lax.rsqrt(jnp.sum(sh * sh, axis=-1, keepdims=True) + NORM_EPS)
            o_ref[:, cols] = jnp.where(j < 2, sh * (r * scale), sh)

    hb = tm // HALO
    return pl.pallas_call(
        body, name=name, grid=(N_QKV_GROUPS, M // tm),
        in_specs=[pl.BlockSpec((tm, QKV_W), lambda j, i: (i, j)),
                  pl.BlockSpec((HALO, QKV_W), lambda j, i: (jnp.maximum(i * hb - 1, 0), j)),
                  pl.BlockSpec((8, QKV_W), lambda j, i: (0, j))],
        out_specs=pl.BlockSpec((tm, QKV_W), lambda j, i: (i, j)),
        out_shape=jax.ShapeDtypeStruct((M, N_QKV_GROUPS * QKV_W), F32),
        scratch_shapes=[pltpu.VMEM((tm + HALO, QKV_W), F32)],
        compiler_params=_params("parallel", "arbitrary"),
    )(proj, proj, conv_w8)


def _prep_bwd_a(proj, conv_w8, dact, *, name):
    M = proj.shape[0]
    tm = _tile(M, 344, 8)
    g_ = M // tm

    def body(x_ref, halo_ref, cw_ref, da_ref, dc_ref, dcw_ref, xs_ref):
        j, i = pl.program_id(0), pl.program_id(1)
        c, taps = _conv_terms(x_ref, halo_ref, cw_ref, xs_ref, i, tm)
        s, ds_dc = _silu_and_grad(c)
        scale = jnp.where(j == 0, GDN_DK ** -0.5, 1.0)
        for hh in range(GDN_HEADS):
            cols = slice(hh * 128, (hh + 1) * 128)
            sh = s[:, cols]
            r = lax.rsqrt(jnp.sum(sh * sh, axis=-1, keepdims=True) + NORM_EPS)
            da = da_ref[:, cols]
            y = sh * r
            dy = da * scale
            ds_norm = r * (dy - y * jnp.sum(dy * y, axis=-1, keepdims=True))
            dc_ref[:, cols] = jnp.where(j < 2, ds_norm, da) * ds_dc[:, cols]
        dc = dc_ref[...]
        r8 = lax.broadcasted_iota(jnp.int32, (8, QKV_W), 0)
        part = jnp.zeros((8, QKV_W), F32)
        for t in range(CONV_K):
            part = jnp.where(r8 == t, jnp.sum(dc * taps[t], axis=0, keepdims=True), part)

        @pl.when(i == 0)
        def _():
            dcw_ref[...] = part

        @pl.when(i > 0)
        def _():
            dcw_ref[...] += part

    hb = tm // HALO
    blk = pl.BlockSpec((tm, QKV_W), lambda j, i: (i, j))
    return pl.pallas_call(
        body, name=name, grid=(N_QKV_GROUPS, g_),
        in_specs=[blk, pl.BlockSpec((HALO, QKV_W), lambda j, i: (jnp.maximum(i * hb - 1, 0), j)),
                  pl.BlockSpec((8, QKV_W), lambda j, i: (0, j)), blk],
        out_specs=[blk, pl.BlockSpec((8, QKV_W), lambda j, i: (0, j))],
        out_shape=[jax.ShapeDtypeStruct((M, N_QKV_GROUPS * QKV_W), F32),
                   jax.ShapeDtypeStruct((8, N_QKV_GROUPS * QKV_W), F32)],
        scratch_shapes=[pltpu.VMEM((tm + HALO, QKV_W), F32)],
        compiler_params=_params("parallel", "arbitrary"),
    )(proj, proj, conv_w8, dact)


def _prep_bwd_b(dc, conv_w8, *, name):
    M = dc.shape[0]
    tm = _tile(M, 344, 8)
    g_ = M // tm

    def body(d_ref, halo_ref, cw_ref, o_ref, ds_ref):
        i = pl.program_id(1)
        ds_ref[0:tm, :] = d_ref[...]
        ds_ref[tm:tm + HALO, :] = jnp.where(i < g_ - 1, halo_ref[...], 0.0)
        cw = cw_ref[...]
        ds = ds_ref[...]
        acc = ds[0:tm, :] * cw[CONV_K - 1:CONV_K, :]
        for t in range(CONV_K - 1):
            acc = acc + pltpu.roll(ds, tm + HALO - (CONV_K - 1 - t), 0)[0:tm, :] * cw[t:t + 1, :]
        o_ref[...] = acc.astype(o_ref.dtype)

    hb = tm // HALO
    last = M // HALO - 1
    blk = pl.BlockSpec((tm, QKV_W), lambda j, i: (i, j))
    return pl.pallas_call(
        body, name=name, grid=(N_QKV_GROUPS, g_),
        in_specs=[blk, pl.BlockSpec((HALO, QKV_W), lambda j, i: (jnp.minimum((i + 1) * hb, last), j)),
                  pl.BlockSpec((8, QKV_W), lambda j, i: (0, j))],
        out_specs=blk,
        out_shape=jax.ShapeDtypeStruct((M, N_QKV_GROUPS * QKV_W), BF16),
        scratch_shapes=[pltpu.VMEM((tm + HALO, QKV_W), F32)],
        compiler_params=_params("parallel", "arbitrary"),
    )(dc, dc, conv_w8)


def _round_robin(gens):
    gens = list(gens)
    while gens:
        alive = []
        for gen in gens:
            try:
                next(gen)
                alive.append(gen)
            except StopIteration:
                pass
        gens = alive


def _unit_lower_inverse(a_low, eye):
    b = -a_low
    x = eye + b
    pw = b
    for _ in range(5):
        pw = _dot3(pw, pw)
        yield
        x = x + _dot3(x, pw)
        yield
    return x


class _GdnChunk:
    def build(self, q, k, v, gb, h):
        C = GDN_CHUNK
        lane = lax.broadcasted_iota(jnp.int32, (C, SM_W), 1)
        g = jnp.sum(jnp.where(lane == h, gb, 0.0), axis=1, keepdims=True)
        self.beta = jnp.sum(jnp.where(lane == h + GDN_HEADS, gb, 0.0), axis=1, keepdims=True)
        ri = lax.broadcasted_iota(jnp.int32, (C, C), 0)
        ci = lax.broadcasted_iota(jnp.int32, (C, C), 1)
        self.causal = ri >= ci
        self.strict = ri > ci
        self.eye = (ri == ci).astype(F32)
        gcb = _dotx(self.causal.astype(F32), jnp.broadcast_to(g, (C, SM_W)))
        yield
        self.gcol = gcb[:, 0:1]
        grow = gcb.T[0:1, 0:C]
        self.decay = jnp.exp(jnp.where(self.causal, self.gcol - grow, -1e30))
        self.egc = jnp.exp(self.gcol)
        glast = gcb[C - 1:C, 0:1]
        self.elast = jnp.exp(glast - self.gcol)
        self.gl = jnp.exp(glast)
        self.q, self.k, self.v = q, k, v
        self.kb = k * self.beta
        m = _dot(self.kb, k, NT)
        n_ = _dot(q, k, NT)
        yield
        self.a_low = jnp.where(self.strict, m * self.decay, 0.0)
        self.p = n_ * self.decay
        self.qd = q * self.egc
        self.kd = k * self.elast
        self.bu = v * self.beta
        self.bw = self.kb * self.egc


GDN_HB = 8
GDN_HG = GDN_HEADS // GDN_HB


def _gdn_specs(n_of):
    C, W = GDN_CHUNK, 128 * GDN_HB
    q_spec = pl.BlockSpec((C, W), lambda g, n: (n_of(n), g))
    k_spec = pl.BlockSpec((C, W), lambda g, n: (n_of(n), g + GDN_HG))
    v_spec = pl.BlockSpec((C, W), lambda g, n: (n_of(n), g + 2 * GDN_HG))
    gb_spec = pl.BlockSpec((C, SM_W), lambda g, n: (n_of(n), 0))
    o_spec = pl.BlockSpec((C, W), lambda g, n: (n_of(n), g))
    s_spec = pl.BlockSpec((GDN_HB, None, GDN_DK, GDN_DV), lambda g, n: (g, n_of(n), 0, 0))
    t_spec = pl.BlockSpec((GDN_HB, None, C, C), lambda g, n: (g, n_of(n), 0, 0))
    return q_spec, k_spec, v_spec, gb_spec, o_spec, s_spec, t_spec


def _gdn_fwd(act, gb, *, name):
    M = act.shape[0]
    N = M // GDN_CHUNK

    def body(q_ref, k_ref, v_ref, gb_ref, o_ref, s_ref, t_ref, state):
        g, n = pl.program_id(0), pl.program_id(1)

        @pl.when(n == 0)
        def _():
            state[...] = jnp.zeros_like(state)

        gb_ = gb_ref[...]

        def head(hh):
            cols = slice(hh * 128, (hh + 1) * 128)
            c = _GdnChunk()
            yield from c.build(q_ref[:, cols], k_ref[:, cols], v_ref[:, cols], gb_, g * GDN_HB + hh)
            tinv = yield from _unit_lower_inverse(c.a_low, c.eye)
            s = state[hh]
            s_ref[hh] = s
            t_ref[hh] = tinv
            u = _dot(tinv, c.bu)
            w = _dot(tinv, c.bw)
            yield
            vn = u - _dot(w, s)
            o1 = _dot(c.qd, s)
            yield
            o_ref[:, cols] = o1 + _dot(c.p, vn)
            state[hh] = c.gl * s + _dot(c.kd, vn, TN)

        _round_robin(head(hh) for hh in range(GDN_HB))

    q_spec, k_spec, v_spec, gb_spec, o_spec, s_spec, t_spec = _gdn_specs(lambda n: n)
    return pl.pallas_call(
        body, name=name, grid=(GDN_HG, N),
        in_specs=[q_spec, k_spec, v_spec, gb_spec], out_specs=[o_spec, s_spec, t_spec],
        out_shape=[jax.ShapeDtypeStruct((M, GDN_V), F32),
                   jax.ShapeDtypeStruct((GDN_HEADS, N, GDN_DK, GDN_DV), F32),
                   jax.ShapeDtypeStruct((GDN_HEADS, N, GDN_CHUNK, GDN_CHUNK), F32)],
        scratch_shapes=[pltpu.VMEM((GDN_HB, GDN_DK, GDN_DV), F32)],
        compiler_params=_params("parallel", "arbitrary"),
    )(act, act, act, gb)


def _gdn_bwd(act, gb, do, s_all, t_all, *, name):
    M = act.shape[0]
    N = M // GDN_CHUNK
    C = GDN_CHUNK

    def body(q_ref, k_ref, v_ref, gb_ref, do_ref, s_ref, t_ref, dq_ref, dk_ref, dv_ref, dgb_ref, dstate):
        g, n = pl.program_id(0), pl.program_id(1)

        @pl.when(n == 0)
        def _():
            dstate[...] = jnp.zeros_like(dstate)

        gb_ = gb_ref[...]
        last = lax.broadcasted_iota(jnp.int32, (C, 1), 0) == C - 1
        upper = (lax.broadcasted_iota(jnp.int32, (C, C), 0) <= lax.broadcasted_iota(jnp.int32, (C, C), 1)).astype(F32)
        lane = lax.broadcasted_iota(jnp.int32, (C, SM_W), 1)
        def head(hh):
            cols = slice(hh * 128, (hh + 1) * 128)
            h = g * GDN_HB + hh
            c = _GdnChunk()
            yield from c.build(q_ref[:, cols], k_ref[:, cols], v_ref[:, cols], gb_, h)
            tinv = t_ref[hh]
            s = s_ref[hh]
            do_ = do_ref[:, cols]
            ds1 = dstate[hh]
            u = _dot(tinv, c.bu)
            w = _dot(tinv, c.bw)
            dqd = _dot(do_, s, NT)
            dvn0 = _dot(c.p, do_, TN) + _dot(c.kd, ds1)
            dst0 = _dot(c.qd, do_, TN) + c.gl * ds1
            yield
            vn = u - _dot(w, s)
            dvn = dvn0
            yield
            dp = jnp.where(c.causal, _dot(do_, vn, NT), 0.0)
            dstate[hh] = dst0 - _dot(w, dvn, TN)
            dkd = _dot(vn, ds1, NT)
            dw = -_dot(dvn, s, NT)
            dbu = _dot(tinv, dvn, TN)
            dgl = jnp.sum(jnp.sum(s * ds1, axis=1, keepdims=True), axis=0, keepdims=True)
            yield
            dbw = _dot(tinv, dw, TN)
            t1 = _dot(dbu, u, NT)
            yield
            da = jnp.where(c.strict, -(t1 + _dot(dbw, w, NT)), 0.0)
            dn_ = dp * c.decay
            dq0 = _dot(dn_, c.k)
            dk0 = _dot(dn_, c.q, TN)
            yield
            dm = da * c.decay
            e = da * c.a_low + dp * c.p
            dkb = _dot(dm, c.k) + dbw * c.egc
            dk_ref[:, cols] = _dot(dm, c.kb, TN) + dk0 + dkb * c.beta + dkd * c.elast
            dq_ref[:, cols] = dq0 + dqd * c.egc
            dv_ref[:, cols] = dbu * c.beta
            dbeta = jnp.sum(dbu * c.v, axis=1, keepdims=True) + jnp.sum(dkb * c.k, axis=1, keepdims=True)
            t_kd = jnp.sum(dkd * c.kd, axis=1, keepdims=True)
            dgc = (jnp.sum(e, axis=1, keepdims=True) - jnp.sum(e.T, axis=1, keepdims=True)
                   + jnp.sum(dbw * c.bw, axis=1, keepdims=True) + jnp.sum(dqd * c.qd, axis=1, keepdims=True) - t_kd)
            dgc = dgc + jnp.where(last, jnp.sum(t_kd, axis=0, keepdims=True) + dgl * c.gl, 0.0)
            yield
            dg = _dotx(upper, jnp.broadcast_to(dgc, (C, SM_W)))
            dgb_ref[hh] = jnp.where(lane == h, dg, jnp.where(lane == h + GDN_HEADS, dbeta, 0.0))

        _round_robin(head(hh) for hh in range(GDN_HB))

    rev = lambda n: N - 1 - n
    q_spec, k_spec, v_spec, gb_spec, o_spec, s_spec, t_spec = _gdn_specs(rev)
    dgb_spec = pl.BlockSpec((GDN_HB, C, SM_W), lambda g, n: (g, rev(n), 0))
    return pl.pallas_call(
        body, name=name, grid=(GDN_HG, N),
        in_specs=[q_spec, k_spec, v_spec, gb_spec, o_spec, s_spec, t_spec],
        out_specs=[o_spec, o_spec, o_spec, dgb_spec],
        out_shape=[jax.ShapeDtypeStruct((M, GDN_QK), F32), jax.ShapeDtypeStruct((M, GDN_QK), F32),
                   jax.ShapeDtypeStruct((M, GDN_V), F32), jax.ShapeDtypeStruct((GDN_HEADS, M, SM_W), F32)],
        scratch_shapes=[pltpu.VMEM((GDN_HB, GDN_DK, GDN_DV), F32)],
        compiler_params=_params("parallel", "arbitrary"),
    )(act, act, act, gb, do, s_all, t_all)


GLA_STEP_ROWS = 64
GLA_SUB = GLA_STEP_ROWS // GLA_CHUNK


def _gla_cumsum(la):
    C = GLA_CHUNK
    ltri = (lax.broadcasted_iota(jnp.int32, (C, C), 0) >= lax.broadcasted_iota(jnp.int32, (C, C), 1)).astype(F32)
    return _dotx(ltri, la)


def _gla_decay_rows(b, i):
    rj = lax.broadcasted_iota(jnp.int32, (GLA_CHUNK, GLA_DK), 0)
    return jnp.where(rj <= i, jnp.exp(jnp.minimum(b[i:i + 1, :] - b, 0.0)), 0.0)


def _gla_scores_t(q, k, b):
    C = GLA_CHUNK
    lane = lax.broadcasted_iota(jnp.int32, (C, C), 1)
    st = jnp.zeros((C, C), F32)
    for i in range(C):
        si = jnp.sum(q[i:i + 1, :] * k * _gla_decay_rows(b, i), axis=1, keepdims=True)
        st = jnp.where(lane == i, si, st)
        if i % 4 == 3:
            yield
    return st


def _gla_specs(n_of):
    R = GLA_STEP_ROWS
    q_spec = pl.BlockSpec((R, GLA_QK), lambda n: (n_of(n), C_GQ // GLA_QK))
    k_spec = pl.BlockSpec((R, GLA_QK), lambda n: (n_of(n), C_GK // GLA_QK))
    v_spec = pl.BlockSpec((R, GLA_V), lambda n: (n_of(n), C_GV // GLA_V))
    la_spec = pl.BlockSpec((R, GLA_QK), lambda n: (n_of(n), 0))
    o_spec = pl.BlockSpec((R, GLA_V), lambda n: (n_of(n), 0))
    s_spec = pl.BlockSpec((GLA_HEADS, None, GLA_SUB, GLA_DV, GLA_DK), lambda n: (0, n_of(n), 0, 0, 0))
    return q_spec, k_spec, v_spec, la_spec, o_spec, s_spec


def _gla_fwd(proj, la, *, name):
    M = proj.shape[0]
    N = M // GLA_STEP_ROWS
    C = GLA_CHUNK

    def body(q_ref, k_ref, v_ref, la_ref, o_ref, s_ref, state):
        n = pl.program_id(0)

        @pl.when(n == 0)
        def _():
            state[...] = jnp.zeros_like(state)

        def head(hh):
            kc = slice(hh * GLA_DK, (hh + 1) * GLA_DK)
            vc = slice(hh * GLA_DV, (hh + 1) * GLA_DV)
            st = state[hh]
            for c in range(GLA_SUB):
                rows = slice(c * C, (c + 1) * C)
                q = q_ref[rows, kc] * (GLA_DK ** -0.5)
                k = k_ref[rows, kc]
                v = v_ref[rows, vc]
                b = _gla_cumsum(la_ref[rows, kc])
                yield
                s_ref[hh, c] = st
                blast = b[C - 1:C, :]
                sc_t = yield from _gla_scores_t(q, k, b)
                o1 = _dot(q * jnp.exp(b), st, NT)
                kv = _dot(v, k * jnp.exp(blast - b), TN)
                o2 = _dot(sc_t, v, TN)
                yield
                o_ref[rows, vc] = o1 + o2
                st = st * jnp.exp(blast) + kv
            state[hh] = st

        _round_robin(head(hh) for hh in range(GLA_HEADS))

    q_spec, k_spec, v_spec, la_spec, o_spec, s_spec = _gla_specs(lambda n: n)
    return pl.pallas_call(
        body, name=name, grid=(N,),
        in_specs=[q_spec, k_spec, v_spec, la_spec], out_specs=[o_spec, s_spec],
        out_shape=[jax.ShapeDtypeStruct((M, GLA_V), F32),
                   jax.ShapeDtypeStruct((GLA_HEADS, N, GLA_SUB, GLA_DV, GLA_DK), F32)],
        scratch_shapes=[pltpu.VMEM((GLA_HEADS, GLA_DV, GLA_DK), F32)],
        compiler_params=_params("arbitrary"),
    )(proj, proj, proj, la)


def _gla_bwd(proj, la, do, s_all, *, name):
    M = proj.shape[0]
    N = M // GLA_STEP_ROWS
    C = GLA_CHUNK

    def body(q_ref, k_ref, v_ref, la_ref, do_ref, s_ref, dq_ref, dk_ref, dv_ref, dla_ref, dstate):
        n = pl.program_id(0)

        @pl.when(n == 0)
        def _():
            dstate[...] = jnp.zeros_like(dstate)

        lane = lax.broadcasted_iota(jnp.int32, (C, C), 1)
        ri = lax.broadcasted_iota(jnp.int32, (C, GLA_DK), 0)
        upper = (lax.broadcasted_iota(jnp.int32, (C, C), 0) <= lane).astype(F32)
        def head(hh):
            kc = slice(hh * GLA_DK, (hh + 1) * GLA_DK)
            vc = slice(hh * GLA_DV, (hh + 1) * GLA_DV)
            ds1 = dstate[hh]
            for c in reversed(range(GLA_SUB)):
                rows = slice(c * C, (c + 1) * C)
                q = q_ref[rows, kc] * (GLA_DK ** -0.5)
                k = k_ref[rows, kc]
                v = v_ref[rows, vc]
                b = _gla_cumsum(la_ref[rows, kc])
                do_ = do_ref[rows, vc]
                st = s_ref[hh, c]
                dsc_t = _dot(v, do_, NT)
                dqe = _dot(do_, st)
                dke = _dot(v, ds1)
                yield
                blast = b[C - 1:C, :]
                eb = jnp.exp(b)
                elast = jnp.exp(blast - b)
                eblast = jnp.exp(blast)
                qe = q * eb
                ke = k * elast
                dv2 = _dot(ke, ds1, NT)
                ds_new = _dot(do_, qe, TN)
                deblast = jnp.sum(st * ds1, axis=0, keepdims=True)
                sc_t = jnp.zeros((C, C), F32)
                dq_sc = jnp.zeros((C, GLA_DK), F32)
                dk_sc = jnp.zeros((C, GLA_DK), F32)
                for i in range(C):
                    f = _gla_decay_rows(b, i)
                    kf = k * f
                    si = jnp.sum(q[i:i + 1, :] * kf, axis=1, keepdims=True)
                    sc_t = jnp.where(lane == i, si, sc_t)
                    dsi = jnp.sum(jnp.where(lane == i, dsc_t, 0.0), axis=1, keepdims=True)
                    dq_sc = jnp.where(ri == i, jnp.sum(dsi * kf, axis=0, keepdims=True), dq_sc)
                    dk_sc = dk_sc + (dsi * f) * q[i:i + 1, :]
                    if i % 4 == 3:
                        yield
                dv1 = _dot(sc_t, do_)
                dq_ref[rows, kc] = ((dq_sc + dqe * eb) * (GLA_DK ** -0.5)).astype(dq_ref.dtype)
                dk_ref[rows, kc] = (dk_sc + dke * elast).astype(dk_ref.dtype)
                t_ke = dke * ke
                db = q * dq_sc - k * dk_sc + dqe * qe - t_ke
                db = db + jnp.where(ri == C - 1, jnp.sum(t_ke, axis=0, keepdims=True) + deblast * eblast, 0.0)
                dla = _dotx(upper, db)
                yield
                dv_ref[rows, vc] = (dv1 + dv2).astype(dv_ref.dtype)
                dla_ref[rows, kc] = dla
                ds1 = ds1 * eblast + ds_new
            dstate[hh] = ds1

        _round_robin(head(hh) for hh in range(GLA_HEADS))

    rev = lambda n: N - 1 - n
    q_spec, k_spec, v_spec, la_spec, o_spec, s_spec = _gla_specs(rev)
    return pl.pallas_call(
        body, name=name, grid=(N,),
        in_specs=[q_spec, k_spec, v_spec, la_spec, o_spec, s_spec],
        out_specs=[la_spec, la_spec, o_spec, la_spec],
        out_shape=[jax.ShapeDtypeStruct((M, GLA_QK), BF16), jax.ShapeDtypeStruct((M, GLA_QK), BF16),
                   jax.ShapeDtypeStruct((M, GLA_V), BF16), jax.ShapeDtypeStruct((M, GLA_QK), F32)],
        scratch_shapes=[pltpu.VMEM((GLA_HEADS, GLA_DV, GLA_DK), F32)],
        compiler_params=_params("arbitrary"),
    )(proj, proj, proj, la, do, s_all)


def _head_norm(o, wn):
    r = lax.rsqrt(jnp.mean(o * o, axis=-1, keepdims=True) + NORM_EPS)
    return o * r, r


def _mix_heads():
    heads = [(0, GDN_DV, hh * GDN_DV, hh * GDN_DV) for hh in range(GDN_HEADS)]
    heads += [(1, GLA_DV, GDN_V + hh * GLA_DV, hh * GLA_DV) for hh in range(GLA_HEADS)]
    return heads


def _mix_fwd(o_gdn, o_gla, proj, wn_gdn, wn_gla, *, name):
    M = proj.shape[0]
    tm = _tile(M, 344, 16)

    def body(og_ref, ol_ref, z_ref, r_ref, wg_ref, wl_ref, m_ref):
        srcs = ((og_ref, z_ref, wg_ref), (ol_ref, r_ref, wl_ref))
        for grp, width, mcol, col in _mix_heads():
            o_ref, gate_ref, w_ref = srcs[grp]
            xhat, _ = _head_norm(o_ref[:, col:col + width], None)
            gate, _ = _silu_and_grad(gate_ref[:, col:col + width])
            m_ref[:, mcol:mcol + width] = (xhat * w_ref[...] * gate).astype(m_ref.dtype)

    full = lambda s: pl.BlockSpec(s, lambda i: (0, 0))
    return pl.pallas_call(
        body, name=name, grid=(M // tm,),
        in_specs=[pl.BlockSpec((tm, GDN_V), lambda i: (i, 0)), pl.BlockSpec((tm, GLA_V), lambda i: (i, 0)),
                  pl.BlockSpec((tm, GDN_V), lambda i: (i, C_Z // GDN_V)),
                  pl.BlockSpec((tm, GLA_V), lambda i: (i, C_GR // GLA_V)),
                  full((1, GDN_DV)), full((1, GLA_DV))],
        out_specs=pl.BlockSpec((tm, D_MODEL), lambda i: (i, 0)),
        out_shape=jax.ShapeDtypeStruct((M, D_MODEL), BF16),
        compiler_params=_params("parallel"),
    )(o_gdn, o_gla, proj, proj, wn_gdn, wn_gla)


def _mix_bwd(o_gdn, o_gla, proj, wn_gdn, wn_gla, dmixed, *, name):
    M = proj.shape[0]
    tm = _tile(M, 344, 16)
    g_ = M // tm

    def body(og_ref, ol_ref, z_ref, r_ref, wg_ref, wl_ref, dm_ref,
             dog_ref, dol_ref, dz_ref, dr_ref, dwg_ref, dwl_ref):
        i = pl.program_id(0)
        srcs = ((og_ref, z_ref, wg_ref, dog_ref, dz_ref), (ol_ref, r_ref, wl_ref, dol_ref, dr_ref))
        dws = [jnp.zeros((1, GDN_DV), F32), jnp.zeros((1, GLA_DV), F32)]
        for grp, width, mcol, col in _mix_heads():
            o_ref, gate_ref, w_ref, do_ref, dgate_ref = srcs[grp]
            cols = slice(col, col + width)
            xhat, r = _head_norm(o_ref[:, cols], None)
            gate, dgate_dc = _silu_and_grad(gate_ref[:, cols])
            dm = dm_ref[:, mcol:mcol + width]
            dgate_ref[:, cols] = (dm * xhat * w_ref[...] * dgate_dc).astype(dgate_ref.dtype)
            dnorm = dm * gate
            dws[grp] = dws[grp] + jnp.sum(dnorm * xhat, axis=0, keepdims=True)
            dxhat = dnorm * w_ref[...]
            do_ref[:, cols] = r * (dxhat - xhat * jnp.mean(dxhat * xhat, axis=-1, keepdims=True))

        @pl.when(i == 0)
        def _():
            dwg_ref[...] = dws[0]
            dwl_ref[...] = dws[1]

        @pl.when(i > 0)
        def _():
            dwg_ref[...] += dws[0]
            dwl_ref[...] += dws[1]

    full = lambda s: pl.BlockSpec(s, lambda i: (0, 0))
    half = pl.BlockSpec((tm, GDN_V), lambda i: (i, 0))
    return pl.pallas_call(
        body, name=name, grid=(g_,),
        in_specs=[half, half, pl.BlockSpec((tm, GDN_V), lambda i: (i, C_Z // GDN_V)),
                  pl.BlockSpec((tm, GLA_V), lambda i: (i, C_GR // GLA_V)),
                  full((1, GDN_DV)), full((1, GLA_DV)), pl.BlockSpec((tm, D_MODEL), lambda i: (i, 0))],
        out_specs=[half, half, half, half, full((1, GDN_DV)), full((1, GLA_DV))],
        out_shape=[jax.ShapeDtypeStruct((M, GDN_V), F32), jax.ShapeDtypeStruct((M, GLA_V), F32),
                   jax.ShapeDtypeStruct((M, GDN_V), BF16), jax.ShapeDtypeStruct((M, GLA_V), BF16),
                   jax.ShapeDtypeStruct((1, GDN_DV), F32), jax.ShapeDtypeStruct((1, GLA_DV), F32)],
        compiler_params=_params("arbitrary"),
    )(o_gdn, o_gla, proj, proj, wn_gdn, wn_gla, dmixed)


def _swiglu_fwd(gate, up, *, name):
    M, F = gate.shape
    tm, tf = _tile(M, 688, 16), _tile(F, 1408, 128)

    def body(g_ref, u_ref, a_ref):
        s, _ = _silu_and_grad(g_ref[...].astype(F32))
        a_ref[...] = (s * u_ref[...].astype(F32)).astype(a_ref.dtype)

    blk = pl.BlockSpec((tm, tf), lambda i, j: (i, j))
    return pl.pallas_call(
        body, name=name, grid=(M // tm, F // tf), in_specs=[blk, blk], out_specs=blk,
        out_shape=jax.ShapeDtypeStruct((M, F), BF16), compiler_params=_params("parallel", "parallel"),
    )(gate, up)


def _swiglu_bwd(gate, up, da, *, name, after=None):
    M, F = gate.shape
    tm, tf = _tile(M, 688, 16), _tile(F, 1408, 128)
    n_after = 0 if after is None else 1

    def body(*refs):
        g_ref, u_ref, da_ref, dg_ref, du_ref = refs[n_after:]
        s, ds = _silu_and_grad(g_ref[...].astype(F32))
        da_ = da_ref[...].astype(F32)
        dg_ref[...] = (da_ * u_ref[...].astype(F32) * ds).astype(dg_ref.dtype)
        du_ref[...] = (da_ * s).astype(du_ref.dtype)

    blk = pl.BlockSpec((tm, tf), lambda i, j: (i, j))
    return pl.pallas_call(
        body, name=name, grid=(M // tm, F // tf), in_specs=[_ANY] * n_after + [blk, blk, blk], out_specs=[blk, blk],
        out_shape=[jax.ShapeDtypeStruct((M, F), BF16), jax.ShapeDtypeStruct((M, F), BF16)],
        compiler_params=_params("parallel", "parallel"),
    )(*((after,) if n_after else ()), gate, up, da)


def _adamw(w, g, m, v, *, name):
    shape = w.shape
    cols = shape[-1]
    rows = w.size // cols
    w2, g2, m2, v2 = (t.reshape(rows, cols) for t in (w, g, m, v))
    tr = _tile(rows, 256, 8) if rows % 8 == 0 else rows

    def body(w_ref, g_ref, m_ref, v_ref, d_ref, nm_ref, nv_ref):
        g_ = g_ref[...]
        nm = ADAM_B1 * m_ref[...] + (1.0 - ADAM_B1) * g_
        nv = ADAM_B2 * v_ref[...] + (1.0 - ADAM_B2) * (g_ * g_)
        m_hat = nm / (1.0 - ADAM_B1 ** ADAM_STEP)
        v_hat = nv / (1.0 - ADAM_B2 ** ADAM_STEP)
        d_ref[...] = -ADAM_LR * (m_hat / (jnp.sqrt(v_hat) + ADAM_EPS) + ADAM_WD * w_ref[...])
        nm_ref[...] = nm
        nv_ref[...] = nv

    blk = pl.BlockSpec((tr, cols), lambda i: (i, 0))
    outs = pl.pallas_call(
        body, name=name, grid=(rows // tr,), in_specs=[blk] * 4, out_specs=[blk] * 3,
        out_shape=[jax.ShapeDtypeStruct((rows, cols), F32)] * 3, compiler_params=_params("parallel"),
    )(w2, g2, m2, v2)
    return tuple(t.reshape(shape) for t in outs)


def _sum_slabs(x, *, name):
    _, R, C = x.shape
    sub = 16 if x.dtype == BF16 else 8
    if R % sub == 0:
        tr, tc = _tile(R, 128, sub), C
    else:
        tr, tc = R, _tile(C, 256, 128)

    def body(x_ref, o_ref):
        acc = x_ref[0].astype(F32)
        for s in range(1, N_DEV):
            acc = acc + x_ref[s].astype(F32)
        o_ref[...] = acc

    return pl.pallas_call(
        body, name=name, grid=(R // tr, C // tc),
        in_specs=[pl.BlockSpec((N_DEV, tr, tc), lambda i, j: (0, i, j))],
        out_specs=pl.BlockSpec((tr, tc), lambda i, j: (i, j)),
        out_shape=jax.ShapeDtypeStruct((R, C), F32), compiler_params=_params("parallel", "parallel"),
    )(x)


def _peers():
    x, y, c = lax.axis_index("x"), lax.axis_index("y"), lax.axis_index("c")
    me = 4 * x + 2 * y + c
    peers = []
    for k in range(1, N_DEV):
        px = 1 - x if k & 4 else x
        py = 1 - y if k & 2 else y
        pc = 1 - c if k & 1 else c
        peers.append(((px, py, pc), 4 * px + 2 * py + pc))
    return me, peers


def _exchange(x, *, gather, name):
    slab = x.shape if gather else x.shape[1:]

    def body(x_ref, o_ref, send_sems, recv_sems, own_sem):
        me, peers = _peers()
        own = pltpu.make_async_copy(x_ref if gather else x_ref.at[me], o_ref.at[me], own_sem)
        own.start()
        sends, recvs = [], []
        for k, (pos, idx) in enumerate(peers):
            sends.append(pltpu.make_async_remote_copy(
                src_ref=x_ref if gather else x_ref.at[idx], dst_ref=o_ref.at[me],
                send_sem=send_sems.at[k], recv_sem=recv_sems.at[k],
                device_id=pos, device_id_type=pl.DeviceIdType.MESH))
            recvs.append(pltpu.make_async_remote_copy(
                src_ref=x_ref if gather else x_ref.at[idx], dst_ref=o_ref.at[idx],
                send_sem=send_sems.at[k], recv_sem=recv_sems.at[k],
                device_id=pos, device_id_type=pl.DeviceIdType.MESH))
        for cp in sends:
            cp.start()
        for cp in recvs:
            cp.wait_recv()
        for cp in sends:
            cp.wait_send()
        own.wait()

    hbm = pl.BlockSpec(memory_space=pltpu.HBM)
    return pl.pallas_call(
        body, name=name, in_specs=[hbm], out_specs=hbm,
        out_shape=jax.ShapeDtypeStruct((N_DEV,) + tuple(slab), x.dtype),
        scratch_shapes=[pltpu.SemaphoreType.DMA((N_DEV - 1,)), pltpu.SemaphoreType.DMA((N_DEV - 1,)),
                        pltpu.SemaphoreType.DMA],
    )(x)


_HBM = pl.BlockSpec(memory_space=pltpu.HBM)
_SEM = pl.BlockSpec(memory_space=pltpu.SEMAPHORE)
_EFFECT = pltpu.SideEffectType.DATAFLOW_SIDE_EFFECTING


def _exchange_start(x, *, gather, name, after=None):
    slab = x.shape if gather else x.shape[1:]
    n_after = 0 if after is None else 1

    def body(*refs):
        x_ref, land_ref, send_sems, recv_sems, _, _, token = refs[n_after:]
        me, peers = _peers()
        for k, (pos, idx) in enumerate(peers):
            pltpu.make_async_remote_copy(
                src_ref=x_ref if gather else x_ref.at[idx], dst_ref=land_ref.at[me],
                send_sem=send_sems.at[k], recv_sem=recv_sems.at[k],
                device_id=pos, device_id_type=pl.DeviceIdType.MESH).start()
        token[...] = jnp.zeros_like(token)

    land = lax.empty((N_DEV,) + tuple(slab), x.dtype)
    return pl.pallas_call(
        body, name=name,
        out_shape=(pltpu.SemaphoreType.DMA((N_DEV - 1,)), pltpu.SemaphoreType.DMA((N_DEV - 1,)),
                   pltpu.HBM(x.shape, x.dtype), pltpu.HBM(land.shape, land.dtype), jax.ShapeDtypeStruct((8, 128), F32)),
        in_specs=[_ANY] * n_after + [_HBM, _HBM],
        out_specs=(_SEM, _SEM, _HBM, _HBM, pl.BlockSpec(memory_space=pltpu.VMEM)),
        input_output_aliases={n_after: 2, n_after + 1: 3},
        compiler_params=pltpu.CompilerParams(has_side_effects=_EFFECT),
    )(*((after,) if n_after else ()), pltpu.with_memory_space_constraint(x, pltpu.HBM),
      pltpu.with_memory_space_constraint(land, pltpu.HBM))


def _exchange_wait(handle, after, *, gather, name):
    send_sems, recv_sems, x_thru, land_thru, _ = handle

    def body(x_ref, land_ref, send_sems, recv_sems, after_ref, x_out, land_out):
        me, peers = _peers()
        for k, (pos, idx) in enumerate(peers):
            cp = pltpu.make_async_remote_copy(
                src_ref=x_ref if gather else x_ref.at[idx], dst_ref=land_ref.at[idx],
                send_sem=send_sems.at[k], recv_sem=recv_sems.at[k],
                device_id=pos, device_id_type=pl.DeviceIdType.MESH)
            cp.wait_send()
            cp.wait_recv()

    return pl.pallas_call(
        body, name=name,
        out_shape=(pltpu.HBM(x_thru.shape, x_thru.dtype), pltpu.HBM(land_thru.shape, land_thru.dtype)),
        in_specs=(_HBM, _HBM, _SEM, _SEM, _ANY), out_specs=(_HBM, _HBM), input_output_aliases={0: 0, 1: 1},
        compiler_params=pltpu.CompilerParams(has_side_effects=_EFFECT),
    )(x_thru, land_thru, send_sems, recv_sems, after)


def _to_proj_rows(t):
    z = jnp.zeros((SM_W - 2 * GDN_HEADS - GLA_RANK + D_PROJ - C_SM - SM_W,) + t.shape[1:], t.dtype)
    return jnp.concatenate([t[:R_A], t[R_GQ:R_LR], t[R_A:R_GQ], t[R_LR:], z], axis=0)


def _from_proj_rows(t):
    return jnp.concatenate([t[:C_GQ], t[C_SM:C_SM + 2 * GDN_HEADS], t[C_GQ:C_SM],
                            t[C_SM + 2 * GDN_HEADS:C_SM + 2 * GDN_HEADS + GLA_RANK]], axis=0)


def _local_step(x, target, meta, attn_nw, conv_w, a_log, dt_bias, gdn_nw, w2, b2, gla_nw, ffn_nw, final_nw,
                fetch, emit, start=None):
    S = x.shape[0]
    h0 = jnp.concatenate([jnp.zeros((ROW_PAD, D_MODEL), F32), meta, x], axis=0)
    target_p = jnp.concatenate([jnp.zeros((HEAD_ROWS, D_MODEL), F32), target], axis=0)
    conv_w8 = jnp.concatenate([conv_w, jnp.zeros((8 - CONV_K, conv_w.shape[1]), F32)], axis=0)
    w2p = jnp.zeros((SM_W, GLA_QK), F32).at[2 * GDN_HEADS:2 * GDN_HEADS + GLA_RANK].set(w2)
    alog_p = jnp.zeros((1, SM_W), F32).at[:, :GDN_HEADS].set(a_log)
    dt_p = jnp.zeros((1, SM_W), F32).at[:, :GDN_HEADS].set(dt_bias)

    n1 = _rmsnorm_fwd(h0, attn_nw, name="attn_norm", after=start)
    w_in_t = fetch("w_in_t", n1)
    proj = _matmul(n1, w_in_t, mode="nt", name="in_proj")
    gb, la = _gates_fwd(proj, w2p, b2, alog_p, dt_p, name="gates")
    act = _prep_fwd(proj, conv_w8, name="gdn_prep")
    o_gdn, s_gdn, t_gdn = _gdn_fwd(act, gb, name="gdn_fwd")
    o_gla, s_gla = _gla_fwd(proj, la, name="gla_fwd")
    mixed = _mix_fwd(o_gdn, o_gla, proj, gdn_nw, gla_nw, name="mix")
    w_gate_t, w_up_t, w_out, w_down = fetch("rest", mixed)
    h1 = _matmul(mixed, w_out, mode="nn", add=h0, name="out_proj")
    n2 = _rmsnorm_fwd(h1, ffn_nw, name="ffn_norm")
    gate = _matmul(n2, w_gate_t, mode="nt", name="ffn_gate", out_dtype=BF16)
    up = _matmul(n2, w_up_t, mode="nt", name="ffn_up", out_dtype=BF16)
    hid = _swiglu_fwd(gate, up, name="swiglu")
    h2 = _matmul(hid, w_down, mode="nn", add=h1, name="ffn_down")
    dh2, dh2_b, d_final_nw, loss = _loss_head(h2, final_nw, target_p, name="loss_head")

    d_hid = _matmul(dh2_b, w_down, mode="nt", name="d_hid", out_dtype=BF16)
    wg = dict(mode="tn", out_dtype=BF16, tn=1024, tk=1376)
    tok = emit("w_down", _matmul(hid, dh2_b, name="d_w_down", tm=1408, **wg))
    d_gate, d_up = _swiglu_bwd(gate, up, d_hid, name="d_swiglu", after=tok)
    tok = emit("w_gate_t", _matmul(d_gate, n2, name="d_w_gate", tm=1408, **wg))
    tok = emit("w_up_t", _matmul(d_up, n2, name="d_w_up", tm=1408, after=tok, **wg))
    d_n2 = _matmul(d_gate, w_gate_t, mode="nn", name="d_n2_gate", tk=1408, after=tok)
    d_n2 = _matmul(d_up, w_up_t, mode="nn", add=d_n2, name="d_n2_up", tk=1408)
    dh1, dh1_b, d_ffn_nw = _rmsnorm_bwd(h1, ffn_nw, d_n2, dh2, name="d_ffn_norm", also_bf16=True)

    tok = emit("w_out", _matmul(mixed, dh1_b, name="d_w_out", tm=1024, **wg))
    d_mixed = _matmul(dh1_b, w_out, mode="nt", name="d_mixed", after=tok)
    do_gdn, do_gla, dz, dr, d_gdn_nw, d_gla_nw = _mix_bwd(o_gdn, o_gla, proj, gdn_nw, gla_nw, d_mixed, name="d_mix")
    d_gq, d_gk, d_gv, d_la = _gla_bwd(proj, la, do_gla, s_gla, name="gla_bwd")
    dq, dk, dv, dgb_heads = _gdn_bwd(act, gb, do_gdn, s_gdn, t_gdn, name="gdn_bwd")
    dsm, d_w2p, d_b2, d_alog, d_dt = _gates_bwd(proj, w2p, b2, alog_p, dt_p, dgb_heads, d_la, name="d_gates")
    dc, d_conv_w8 = _prep_bwd_a(proj, conv_w8, jnp.concatenate([dq, dk, dv], axis=1), name="d_gdn_prep")
    d_qkv = _prep_bwd_b(dc, conv_w8, name="d_conv")
    d_proj = jnp.concatenate([d_qkv, dz, d_gq, d_gk, d_gv, dr, dsm,
                              jnp.zeros((S + HEAD_ROWS, D_PROJ - C_SM - SM_W), BF16)], axis=1)
    tok = emit("w_in_t", _matmul(d_proj, n1, name="d_w_in", tm=1536, **wg))
    d_n1 = _matmul(d_proj, w_in_t, mode="nn", name="d_n1", tk=1536, after=tok)
    dh0, d_attn_nw = _rmsnorm_bwd(h0, attn_nw, d_n1, dh1, name="d_attn_norm", also_bf16=False)

    return dict(
        loss=loss[0, 0], grad_x=dh0[HEAD_ROWS:], meta=dh0[ROW_PAD:HEAD_ROWS], attn_nw=d_attn_nw,
        conv_w=d_conv_w8[:CONV_K], a_log=d_alog[:, :GDN_HEADS], dt_bias=d_dt[:, :GDN_HEADS], gdn_nw=d_gdn_nw,
        w2=d_w2p[2 * GDN_HEADS:2 * GDN_HEADS + GLA_RANK], b2=d_b2, gla_nw=d_gla_nw, ffn_nw=d_ffn_nw,
        final_nw=d_final_nw)


SMALL_ROWS = 32


def kernel(x, meta_tokens, attn_norm_w, w_in, gdn_conv_w, gdn_a_log, gdn_dt_bias, gdn_norm_w, gla_gate_w2, gla_gate_b, gla_norm_w, w_out, ffn_norm_w, w_gate, w_up, w_down, final_norm_w, loss_target, m_meta_tokens, m_attn_norm_w, m_w_in, m_gdn_conv_w, m_gdn_a_log, m_gdn_dt_bias, m_gdn_norm_w, m_gla_gate_w2, m_gla_gate_b, m_gla_norm_w, m_w_out, m_ffn_norm_w, m_w_gate, m_w_up, m_w_down, m_final_norm_w, v_meta_tokens, v_attn_norm_w, v_w_in, v_gdn_conv_w, v_gdn_a_log, v_gdn_dt_bias, v_gdn_norm_w, v_gla_gate_w2, v_gla_gate_b, v_gla_norm_w, v_w_out, v_ffn_norm_w, v_w_gate, v_w_up, v_w_down, v_final_norm_w):
    me = 4 * lax.axis_index("x") + 2 * lax.axis_index("y") + lax.axis_index("c")
    n_in, n_ff, n_out = D_IN // N_DEV, D_FF // N_DEV, D_MODEL // N_DEV

    n_conv = gdn_conv_w.shape[2]
    n_w2 = gla_gate_w2.shape[2]
    n_meta = meta_tokens.shape[1]
    small = jnp.zeros((40, n_conv), F32)
    small = small.at[0:N_META, :n_meta].set(meta_tokens)
    small = small.at[N_META:N_META + CONV_K, :].set(gdn_conv_w[0])
    small = small.at[24:24 + GLA_RANK, :n_w2].set(gla_gate_w2[0])
    small_all = _exchange(small, gather=True, name="gather_small")
    meta_f = small_all[:, 0:N_META, :n_meta].transpose(1, 0, 2).reshape(N_META, D_MODEL)
    conv_f = small_all[:, N_META:N_META + CONV_K, :].transpose(1, 0, 2).reshape(CONV_K, N_DEV * n_conv)
    w2_f = small_all[:, 24:24 + GLA_RANK, :n_w2].transpose(1, 0, 2).reshape(GLA_RANK, N_DEV * n_w2)

    o1, o2, o3 = n_ff, 2 * n_ff, 2 * n_ff + n_out
    in_h = _exchange_start(w_in[0].T.astype(BF16), gather=True, name="gather_w_in_start")
    rest = jnp.concatenate([w_gate[0].T, w_up[0].T, w_out[0], w_down[0]], axis=0).astype(BF16)
    rest_h = _exchange_start(rest, gather=True, name="gather_rest_start", after=in_h[4])

    def fetch(name, after):
        handle = in_h if name == "w_in_t" else rest_h
        own, got = _exchange_wait(handle, after, gather=True, name="gather_" + name + "_wait")
        got = lax.dynamic_update_index_in_dim(got, own, me, 0)
        if name == "w_in_t":
            return _to_proj_rows(got.reshape(D_IN, D_MODEL))
        return (got[:, :o1].reshape(D_FF, D_MODEL), got[:, o1:o2].reshape(D_FF, D_MODEL),
                got[:, o2:o3].reshape(D_MODEL, D_MODEL), got[:, o3:].reshape(D_FF, D_MODEL))

    sent = {}

    def emit(name, grad):
        if name == "w_in_t":
            grad = _from_proj_rows(grad)
        parts = grad.reshape(N_DEV, grad.shape[0] // N_DEV, D_MODEL)
        sent[name] = _exchange_start(parts, gather=False, name="scatter_" + name + "_start")
        return sent[name][4]

    g = _local_step(x[0], loss_target[0], meta_f, attn_norm_w, conv_f, gdn_a_log, gdn_dt_bias, gdn_norm_w, w2_f,
                    gla_gate_b, gla_norm_w, ffn_norm_w, final_norm_w.reshape(1, D_MODEL), fetch, emit, start=rest_h[4])

    def total(name, after):
        handle = sent[name]
        own, got = _exchange_wait(handle, after, gather=False, name="scatter_" + name + "_wait")
        got = lax.dynamic_update_index_in_dim(got, lax.dynamic_index_in_dim(own, me, 0, keepdims=False), me, 0)
        return _sum_slabs(got, name="sum_" + name)

    grad_w_down = total("w_down", g["grad_x"])[None]
    grad_w_gate = total("w_gate_t", grad_w_down).T[None]
    grad_w_up = total("w_up_t", grad_w_gate).T[None]
    grad_w_out = total("w_out", grad_w_up)[None]
    grad_w_in = total("w_in_t", grad_w_out).T[None]

    misc = jnp.concatenate([g["a_log"], g["dt_bias"], g["gdn_nw"], g["gla_nw"], g["b2"], g["loss"].reshape(1, 1)], axis=1)
    n_misc = misc.shape[1]
    misc = jnp.pad(misc, ((0, 0), (0, D_MODEL - n_misc)))
    rows = jnp.concatenate([g["attn_nw"], g["ffn_nw"], g["final_nw"], misc, g["meta"],
                            g["conv_w"].reshape(-1, D_MODEL), g["w2"].reshape(-1, D_MODEL)], axis=0)
    rows = jnp.pad(rows, ((0, SMALL_ROWS - rows.shape[0]), (0, 0)))
    tot = _sum_slabs(_exchange(rows, gather=True, name="gather_small_grads"), name="sum_small_grads")
    grad_attn_nw, grad_ffn_nw, grad_final_nw = tot[0:1], tot[1:2], tot[2]
    grad_a_log = tot[3:4, 0:8]
    grad_dt = tot[3:4, 8:16]
    grad_gdn_nw = tot[3:4, 16:16 + GDN_DV]
    grad_gla_nw = tot[3:4, 144:144 + GLA_DV]
    grad_b2 = tot[3:4, 400:400 + GLA_QK]
    loss = tot[3, n_misc - 1]
    r0 = 4 + N_META
    grad_meta = lax.dynamic_slice(tot[4:r0], (0, me * n_meta), (N_META, n_meta))
    r1 = r0 + CONV_K * N_DEV * n_conv // D_MODEL
    grad_conv = lax.dynamic_slice(tot[r0:r1].reshape(CONV_K, N_DEV * n_conv), (0, me * n_conv), (CONV_K, n_conv))[None]
    r2 = r1 + GLA_RANK * N_DEV * n_w2 // D_MODEL
    grad_w2 = lax.dynamic_slice(tot[r1:r2].reshape(GLA_RANK, N_DEV * n_w2), (0, me * n_w2), (GLA_RANK, n_w2))[None]

    weights = [meta_tokens, attn_norm_w, w_in, gdn_conv_w, gdn_a_log, gdn_dt_bias, gdn_norm_w, gla_gate_w2,
               gla_gate_b, gla_norm_w, w_out, ffn_norm_w, w_gate, w_up, w_down, final_norm_w]
    grads = [grad_meta, grad_attn_nw, grad_w_in, grad_conv, grad_a_log, grad_dt, grad_gdn_nw, grad_w2,
             grad_b2, grad_gla_nw, grad_w_out, grad_ffn_nw, grad_w_gate, grad_w_up, grad_w_down, grad_final_nw]
    ms = [m_meta_tokens, m_attn_norm_w, m_w_in, m_gdn_conv_w, m_gdn_a_log, m_gdn_dt_bias, m_gdn_norm_w,
          m_gla_gate_w2, m_gla_gate_b, m_gla_norm_w, m_w_out, m_ffn_norm_w, m_w_gate, m_w_up, m_w_down, m_final_norm_w]
    vs = [v_meta_tokens, v_attn_norm_w, v_w_in, v_gdn_conv_w, v_gdn_a_log, v_gdn_dt_bias, v_gdn_norm_w,
          v_gla_gate_w2, v_gla_gate_b, v_gla_norm_w, v_w_out, v_ffn_norm_w, v_w_gate, v_w_up, v_w_down, v_final_norm_w]
    grads = [gr.reshape(w.shape) for gr, w in zip(grads, weights)]
    deltas, new_ms, new_vs = [], [], []
    for idx, (w, gr, m, v) in enumerate(zip(weights, grads, ms, vs)):
        d, nm, nv = _adamw(w, gr, m, v, name=f"adamw_{idx}")
        deltas.append(d)
        new_ms.append(nm)
        new_vs.append(nv)
    return (loss, g["grad_x"][None], *grads, *deltas, *new_ms, *new_vs)
```

```python
import functools

import jax
import jax.numpy as jnp
from jax import lax
from jax.experimental import pallas as pl
from jax.experimental.pallas import tpu as pltpu

F32 = jnp.float32
BF16 = jnp.bfloat16
_MXU_DTYPE = jnp.bfloat16

D_MODEL = 2048
N_META = 16
ROW_PAD = 48
HEAD_ROWS = ROW_PAD + N_META
CONV_K = 4
GDN_HEADS, GDN_DK, GDN_DV, GDN_CHUNK = 8, 128, 128, 64
GLA_HEADS, GLA_DK, GLA_DV, GLA_CHUNK = 4, 128, 256, 16
GLA_RANK = 16
GLA_GATE_NORMALIZER = 16.0
GDN_QK = GDN_HEADS * GDN_DK
GDN_V = GDN_HEADS * GDN_DV
GLA_QK = GLA_HEADS * GLA_DK
GLA_V = GLA_HEADS * GLA_DV
D_FF = 5632
D_IN = 7200
NORM_EPS = 1e-6
C_Z, C_GR, C_GQ, C_GK, C_GV, C_QKV, C_SM = 0, 1024, 2048, 2560, 3072, 4096, 7168
SM_W = 128
D_PROJ = 7680
R_Z, R_A, R_B, R_GQ, R_GK, R_GV, R_GR, R_LR = 3072, 4096, 4104, 4112, 4624, 5136, 6160, 7184

ADAM_LR, ADAM_B1, ADAM_B2, ADAM_EPS, ADAM_WD, ADAM_STEP = 0.001, 0.9, 0.999, 1e-08, 0.01, 10

N_DEV = 8
VMEM_LIMIT = 56 * 1024 * 1024

NN = (((1,), (0,)), ((), ()))
NT = (((1,), (1,)), ((), ()))
TN = (((0,), (0,)), ((), ()))


def _dot(a, b, dims=NN):
    return lax.dot_general(a.astype(_MXU_DTYPE), b.astype(_MXU_DTYPE), dims, preferred_element_type=F32)


def _dotx(a, b, dims=NN):
    return lax.dot_general(a, b, dims, precision=lax.Precision.HIGHEST, preferred_element_type=F32)


def _dot3(a, b):
    ah = a.astype(BF16)
    al = (a - ah.astype(F32)).astype(BF16)
    bh = b.astype(BF16)
    bl = (b - bh.astype(F32)).astype(BF16)
    d = functools.partial(lax.dot_general, dimension_numbers=NN, preferred_element_type=F32)
    return d(ah, bh) + (d(ah, bl) + d(al, bh))


def _tile(n, target, mult=8):
    best = None
    for t in range(mult, min(n, target) + 1, mult):
        if n % t == 0:
            best = t
    return best if best is not None else n


def _params(*sem):
    return pltpu.CompilerParams(dimension_semantics=sem, vmem_limit_bytes=VMEM_LIMIT)


def _sigmoid(x):
    return 0.5 * jnp.tanh(0.5 * x) + 0.5


def _softplus(x):
    return jnp.maximum(x, 0.0) + jnp.log1p(jnp.exp(-jnp.abs(x)))


def _silu_and_grad(c):
    s = _sigmoid(c)
    return c * s, s * (1.0 + c * (1.0 - s))


_ANY = pl.BlockSpec(memory_space=pl.ANY)


def _matmul(a, b, *, mode, name, out_dtype=F32, add=None, after=None, tm=1376, tn=512, tk=2064):
    if mode == "tn":
        K, M = a.shape
        N = b.shape[1]
    else:
        M, K = a.shape
        N = b.shape[0] if mode == "nt" else b.shape[1]
    tm = _tile(M, tm, 128 if mode == "tn" else 16)
    tn = _tile(N, tn, 128)
    tk = _tile(K, tk, 16 if mode == "tn" else 128)
    gm, gn, gk = M // tm, N // tn, K // tk
    dims = {"nn": NN, "nt": NT, "tn": TN}[mode]

    n_after = 0 if after is None else 1

    def body(*refs):
        refs = refs[n_after:]
        if add is None:
            a_ref, b_ref, o_ref = refs[:3]
            add_ref = None
        else:
            a_ref, b_ref, add_ref, o_ref = refs[:4]
        p = _dot(a_ref[...], b_ref[...], dims)

        def finish(r):
            if add_ref is not None:
                r = r + add_ref[...]
            o_ref[...] = r.astype(out_dtype)

        if gk == 1:
            finish(p)
        else:
            acc_ref = refs[-1]
            k = pl.program_id(2)

            @pl.when(k == 0)
            def _():
                acc_ref[...] = p

            @pl.when(k > 0)
            def _():
                acc_ref[...] += p

            @pl.when(k == gk - 1)
            def _():
                finish(acc_ref[...])

    if mode == "tn":
        a_spec = pl.BlockSpec((tk, tm), lambda i, j, k: (k, i))
    else:
        a_spec = pl.BlockSpec((tm, tk), lambda i, j, k: (i, k))
    if mode == "nt":
        b_spec = pl.BlockSpec((tn, tk), lambda i, j, k: (j, k))
    else:
        b_spec = pl.BlockSpec((tk, tn), lambda i, j, k: (k, j))
    o_spec = pl.BlockSpec((tm, tn), lambda i, j, k: (i, j))
    in_specs = [_ANY] * n_after + [a_spec, b_spec] + ([o_spec] if add is not None else [])
    args = ((after,) if n_after else ()) + (a, b) + ((add,) if add is not None else ())
    return pl.pallas_call(
        body, name=name, grid=(gm, gn, gk), in_specs=in_specs, out_specs=o_spec,
        out_shape=jax.ShapeDtypeStruct((M, N), out_dtype),
        scratch_shapes=[pltpu.VMEM((tm, tn), F32)] if gk > 1 else [],
        compiler_params=_params("parallel", "parallel", "arbitrary"),
    )(*args)


def _rmsnorm_fwd(h, w, *, name, after=None):
    M, D = h.shape
    tm = _tile(M, 688, 16)
    n_after = 0 if after is None else 1

    def body(*refs):
        h_ref, w_ref, n_ref = refs[n_after:]
        x = h_ref[...]
        r = lax.rsqrt(jnp.mean(x * x, axis=-1, keepdims=True) + NORM_EPS)
        n_ref[...] = (x * r * w_ref[...]).astype(n_ref.dtype)

    return pl.pallas_call(
        body, name=name, grid=(M // tm,),
        in_specs=[_ANY] * n_after + [pl.BlockSpec((tm, D), lambda i: (i, 0)), pl.BlockSpec((1, D), lambda i: (0, 0))],
        out_specs=pl.BlockSpec((tm, D), lambda i: (i, 0)),
        out_shape=jax.ShapeDtypeStruct((M, D), BF16),
        compiler_params=_params("parallel"),
    )(*((after,) if n_after else ()), h, w)


def _rmsnorm_bwd(h, w, dn, dres, *, name, also_bf16):
    M, D = h.shape
    tm = _tile(M, 344, 16)
    g = M // tm

    def body(h_ref, w_ref, dn_ref, dres_ref, dh_ref, *rest):
        dhb_ref = rest[0] if also_bf16 else None
        dw_ref, acc_ref = rest[-2:]
        i = pl.program_id(0)
        x = h_ref[...]
        r = lax.rsqrt(jnp.mean(x * x, axis=-1, keepdims=True) + NORM_EPS)
        xhat = x * r
        dn_ = dn_ref[...]
        dxhat = dn_ * w_ref[...]
        dh = dres_ref[...] + r * (dxhat - xhat * jnp.mean(dxhat * xhat, axis=-1, keepdims=True))
        dh_ref[...] = dh
        if also_bf16:
            dhb_ref[...] = dh.astype(dhb_ref.dtype)
        part = jnp.sum((dn_ * xhat).reshape(tm // 8, 8, D), axis=0)

        @pl.when(i == 0)
        def _():
            acc_ref[...] = part

        @pl.when(i > 0)
        def _():
            acc_ref[...] += part

        @pl.when(i == g - 1)
        def _():
            dw_ref[...] = jnp.sum(acc_ref[...], axis=0, keepdims=True)

    row = pl.BlockSpec((tm, D), lambda i: (i, 0))
    vec = pl.BlockSpec((1, D), lambda i: (0, 0))
    return pl.pallas_call(
        body, name=name, grid=(g,), in_specs=[row, vec, row, row],
        out_specs=[row] + ([row] if also_bf16 else []) + [vec],
        out_shape=[jax.ShapeDtypeStruct((M, D), F32)] + ([jax.ShapeDtypeStruct((M, D), BF16)] if also_bf16 else [])
        + [jax.ShapeDtypeStruct((1, D), F32)],
        scratch_shapes=[pltpu.VMEM((8, D), F32)],
        compiler_params=_params("arbitrary"),
    )(h, w, dn, dres)


def _loss_head(h, w, target_p, *, name):
    M, D = h.shape
    tm = _tile(M, 344, 16)
    g = M // tm

    def body(h_ref, w_ref, t_ref, dh_ref, dhb_ref, dw_ref, loss_ref, acc_ref, lacc_ref):
        i = pl.program_id(0)
        x = h_ref[...]
        row = i * tm + lax.broadcasted_iota(jnp.int32, (tm, 1), 0)
        live = row >= HEAD_ROWS
        r = lax.rsqrt(jnp.mean(x * x, axis=-1, keepdims=True) + NORM_EPS)
        xhat = x * r
        err = jnp.where(live, xhat * w_ref[...] - t_ref[...], 0.0)
        dy = err * (1.0 / D)
        dxhat = dy * w_ref[...]
        dh = r * (dxhat - xhat * jnp.mean(dxhat * xhat, axis=-1, keepdims=True))
        dh_ref[...] = dh
        dhb_ref[...] = dh.astype(dhb_ref.dtype)
        part = jnp.sum((dy * xhat).reshape(tm // 8, 8, D), axis=0)
        lpart = jnp.sum((err * err).reshape(tm // 8, 8, D), axis=0)

        @pl.when(i == 0)
        def _():
            acc_ref[...] = part
            lacc_ref[...] = lpart

        @pl.when(i > 0)
        def _():
            acc_ref[...] += part
            lacc_ref[...] += lpart

        @pl.when(i == g - 1)
        def _():
            dw_ref[...] = jnp.sum(acc_ref[...], axis=0, keepdims=True)
            tot = jnp.sum(jnp.sum(lacc_ref[...], axis=0, keepdims=True), axis=1, keepdims=True)
            loss_ref[...] = jnp.broadcast_to(tot * (0.5 / D), (1, 128))

    row = pl.BlockSpec((tm, D), lambda i: (i, 0))
    vec = pl.BlockSpec((1, D), lambda i: (0, 0))
    return pl.pallas_call(
        body, name=name, grid=(g,), in_specs=[row, vec, row],
        out_specs=[row, row, vec, pl.BlockSpec((1, 128), lambda i: (0, 0))],
        out_shape=[jax.ShapeDtypeStruct((M, D), F32), jax.ShapeDtypeStruct((M, D), BF16),
                   jax.ShapeDtypeStruct((1, D), F32), jax.ShapeDtypeStruct((1, 128), F32)],
        scratch_shapes=[pltpu.VMEM((8, D), F32), pltpu.VMEM((8, D), F32)],
        compiler_params=_params("arbitrary"),
    )(h, w, target_p)


def _gate_terms(sm, w2p, b2, alog_p, dt_p, row0):
    tm = sm.shape[0]
    lane = lax.broadcasted_iota(jnp.int32, (tm, SM_W), 1)
    live = (row0 + lax.broadcasted_iota(jnp.int32, (tm, 1), 0)) >= ROW_PAD
    pre = sm + dt_p
    neg_a = -jnp.exp(alog_p)
    g = neg_a * _softplus(pre)
    beta = _sigmoid(sm)
    z = _dot(sm, w2p) + b2
    return lane, live, pre, neg_a, g, beta, z


def _gates_fwd(proj, w2p, b2, alog_p, dt_p, *, name):
    M = proj.shape[0]
    tm = _tile(M, 688, 8)

    def body(sm_ref, w2_ref, b2_ref, al_ref, dt_ref, gb_ref, la_ref):
        row0 = pl.program_id(0) * tm
        lane, live, _, _, g, beta, z = _gate_terms(sm_ref[...], w2_ref[...], b2_ref[...], al_ref[...], dt_ref[...], row0)
        gb = jnp.where(lane < GDN_HEADS, g, jnp.where(lane < 2 * GDN_HEADS, beta, 0.0))
        gb_ref[...] = jnp.where(live, gb, 0.0)
        la = (jnp.minimum(z, 0.0) - jnp.log1p(jnp.exp(-jnp.abs(z)))) * (1.0 / GLA_GATE_NORMALIZER)
        la_ref[...] = jnp.where(live, la, 0.0)

    full = lambda s: pl.BlockSpec(s, lambda i: (0, 0))
    return pl.pallas_call(
        body, name=name, grid=(M // tm,),
        in_specs=[pl.BlockSpec((tm, SM_W), lambda i: (i, C_SM // SM_W)), full((SM_W, GLA_QK)), full((1, GLA_QK)),
                  full((1, SM_W)), full((1, SM_W))],
        out_specs=[pl.BlockSpec((tm, SM_W), lambda i: (i, 0)), pl.BlockSpec((tm, GLA_QK), lambda i: (i, 0))],
        out_shape=[jax.ShapeDtypeStruct((M, SM_W), F32), jax.ShapeDtypeStruct((M, GLA_QK), F32)],
        compiler_params=_params("parallel"),
    )(proj, w2p, b2, alog_p, dt_p)


def _gates_bwd(proj, w2p, b2, alog_p, dt_p, dgb_heads, dla, d_proj, *, name):
    M = proj.shape[0]
    tm = _tile(M, 688, 8)
    g_ = M // tm

    tail_w = D_PROJ - C_SM

    def body(sm_ref, w2_ref, b2_ref, al_ref, dt_ref, dgb_ref, dla_ref, _,
             dsm_ref, dw2_ref, db2_ref, dal_ref, ddt_ref):
        i = pl.program_id(0)
        sm = sm_ref[...]
        lane, live, pre, neg_a, g, beta, z = _gate_terms(sm, w2_ref[...], b2_ref[...], al_ref[...], dt_ref[...], i * tm)
        dz = jnp.where(live, dla_ref[...] * (_sigmoid(-z) * (1.0 / GLA_GATE_NORMALIZER)), 0.0)
        dsm_lr = _dot(dz, w2_ref[...], NT)
        dgb = dgb_ref[0]
        for hh in range(1, GDN_HEADS):
            dgb = dgb + dgb_ref[hh]
        dgb = jnp.where(live, dgb, 0.0)
        da = dgb * neg_a * _sigmoid(pre)
        db = dgb * beta * (1.0 - beta)
        dsm = jnp.where(lane < GDN_HEADS, da, jnp.where(lane < 2 * GDN_HEADS, db, dsm_lr))
        dsm_ref[:, 0:SM_W] = dsm.astype(dsm_ref.dtype)
        dsm_ref[:, SM_W:tail_w] = jnp.zeros((tm, tail_w - SM_W), dsm_ref.dtype)
        is_a = lane < GDN_HEADS
        dal = jnp.sum(jnp.where(is_a, dgb * g, 0.0), axis=0, keepdims=True)
        ddt = jnp.sum(jnp.where(is_a, da, 0.0), axis=0, keepdims=True)
        dw2 = _dot(sm, dz, TN)
        db2 = jnp.sum(dz, axis=0, keepdims=True)

        @pl.when(i == 0)
        def _():
            dw2_ref[...] = dw2
            db2_ref[...] = db2
            dal_ref[...] = dal
            ddt_ref[...] = ddt

        @pl.when(i > 0)
        def _():
            dw2_ref[...] += dw2
            db2_ref[...] += db2
            dal_ref[...] += dal
            ddt_ref[...] += ddt

    full = lambda s: pl.BlockSpec(s, lambda i: (0, 0))
    return pl.pallas_call(
        body, name=name, grid=(g_,),
        in_specs=[pl.BlockSpec((tm, SM_W), lambda i: (i, C_SM // SM_W)), full((SM_W, GLA_QK)), full((1, GLA_QK)),
                  full((1, SM_W)), full((1, SM_W)),
                  pl.BlockSpec((GDN_HEADS, tm, SM_W), lambda i: (0, i, 0)),
                  pl.BlockSpec((tm, GLA_QK), lambda i: (i, 0)), _ANY],
        out_specs=[pl.BlockSpec((tm, tail_w), lambda i: (i, C_SM // tail_w)), full((SM_W, GLA_QK)), full((1, GLA_QK)),
                   full((1, SM_W)), full((1, SM_W))],
        out_shape=[jax.ShapeDtypeStruct(d_proj.shape, d_proj.dtype), jax.ShapeDtypeStruct((SM_W, GLA_QK), F32),
                   jax.ShapeDtypeStruct((1, GLA_QK), F32), jax.ShapeDtypeStruct((1, SM_W), F32),
                   jax.ShapeDtypeStruct((1, SM_W), F32)],
        input_output_aliases={7: 0},
        compiler_params=_params("arbitrary"),
    )(proj, w2p, b2, alog_p, dt_p, dgb_heads, dla, d_proj)


QKV_W = GDN_QK
N_QKV_GROUPS = 3
QKV_B0 = C_QKV // QKV_W
HALO = 8


def _conv_terms(x_ref, halo_ref, cw_ref, xs_ref, i, tm):
    xs_ref[HALO:HALO + tm, :] = x_ref[...]
    xs_ref[0:HALO, :] = jnp.where(i > 0, halo_ref[...], 0.0)
    cw = cw_ref[...]
    xs = xs_ref[...]
    taps = [(pltpu.roll(xs, CONV_K - 1 - t, 0) if t < CONV_K - 1 else xs)[HALO:HALO + tm, :] for t in range(CONV_K)]
    c = taps[0] * cw[0:1, :]
    for t in range(1, CONV_K):
        c = c + taps[t] * cw[t:t + 1, :]
    return c, taps


def _prep_fwd(proj, conv_w8, *, name):
    M = proj.shape[0]
    tm = _tile(M, 344, 8)

    def body(x_ref, halo_ref, cw_ref, o_ref, xs_ref):
        j, i = pl.program_id(0), pl.program_id(1)
        c, _ = _conv_terms(x_ref, halo_ref, cw_ref, xs_ref, i, tm)
        s, _ = _silu_and_grad(c)
        scale = jnp.where(j == 0, GDN_DK ** -0.5, 1.0)
        for hh in range(GDN_HEADS):
            cols = slice(hh * 128, (hh + 1) * 128)
            sh = s[:, cols]
            r = lax.rsqrt(jnp.sum(sh * sh, axis=-1, keepdims=True) + NORM_EPS)
            o_ref[:, cols] = jnp.where(j < 2, sh * (r * scale), sh)

    hb = tm // HALO
    return pl.pallas_call(
        body, name=name, grid=(N_QKV_GROUPS, M // tm),
        in_specs=[pl.BlockSpec((tm, QKV_W), lambda j, i: (i, QKV_B0 + j)),
                  pl.BlockSpec((HALO, QKV_W), lambda j, i: (jnp.maximum(i * hb - 1, 0), QKV_B0 + j)),
                  pl.BlockSpec((8, QKV_W), lambda j, i: (0, j))],
        out_specs=pl.BlockSpec((tm, QKV_W), lambda j, i: (i, j)),
        out_shape=jax.ShapeDtypeStruct((M, N_QKV_GROUPS * QKV_W), F32),
        scratch_shapes=[pltpu.VMEM((tm + HALO, QKV_W), F32)],
        compiler_params=_params("parallel", "arbitrary"),
    )(proj, proj, conv_w8)


def _prep_bwd_a(proj, conv_w8, dact, *, name):
    M = proj.shape[0]
    tm = _tile(M, 344, 8)
    g_ = M // tm

    def body(x_ref, halo_ref, cw_ref, da_ref, dc_ref, dcw_ref, xs_ref):
        j, i = pl.program_id(0), pl.program_id(1)
        c, taps = _conv_terms(x_ref, halo_ref, cw_ref, xs_ref, i, tm)
        s, ds_dc = _silu_and_grad(c)
        scale = jnp.where(j == 0, GDN_DK ** -0.5, 1.0)
        for hh in range(GDN_HEADS):
            cols = slice(hh * 128, (hh + 1) * 128)
            sh = s[:, cols]
            r = lax.rsqrt(jnp.sum(sh * sh, axis=-1, keepdims=True) + NORM_EPS)
            da = da_ref[:, cols]
            y = sh * r
            dy = da * scale
            ds_norm = r * (dy - y * jnp.sum(dy * y, axis=-1, keepdims=True))
            dc_ref[:, cols] = jnp.where(j < 2, ds_norm, da) * ds_dc[:, cols]
        dc = dc_ref[...]
        r8 = lax.broadcasted_iota(jnp.int32, (8, QKV_W), 0)
        part = jnp.zeros((8, QKV_W), F32)
        for t in range(CONV_K):
            part = jnp.where(r8 == t, jnp.sum(dc * taps[t], axis=0, keepdims=True), part)

        @pl.when(i == 0)
        def _():
            dcw_ref[...] = part

        @pl.when(i > 0)
        def _():
            dcw_ref[...] += part

    hb = tm // HALO
    blk = pl.BlockSpec((tm, QKV_W), lambda j, i: (i, j))
    return pl.pallas_call(
        body, name=name, grid=(N_QKV_GROUPS, g_),
        in_specs=[pl.BlockSpec((tm, QKV_W), lambda j, i: (i, QKV_B0 + j)),
                  pl.BlockSpec((HALO, QKV_W), lambda j, i: (jnp.maximum(i * hb - 1, 0), QKV_B0 + j)),
                  pl.BlockSpec((8, QKV_W), lambda j, i: (0, j)), blk],
        out_specs=[blk, pl.BlockSpec((8, QKV_W), lambda j, i: (0, j))],
        out_shape=[jax.ShapeDtypeStruct((M, N_QKV_GROUPS * QKV_W), F32),
                   jax.ShapeDtypeStruct((8, N_QKV_GROUPS * QKV_W), F32)],
        scratch_shapes=[pltpu.VMEM((tm + HALO, QKV_W), F32)],
        compiler_params=_params("parallel", "arbitrary"),
    )(proj, proj, conv_w8, dact)


def _prep_bwd_b(dc, conv_w8, d_proj, *, name):
    M = dc.shape[0]
    tm = _tile(M, 344, 16)
    g_ = M // tm

    def body(d_ref, halo_ref, cw_ref, _, o_ref, ds_ref):
        i = pl.program_id(1)
        ds_ref[0:tm, :] = d_ref[...]
        ds_ref[tm:tm + HALO, :] = jnp.where(i < g_ - 1, halo_ref[...], 0.0)
        cw = cw_ref[...]
        ds = ds_ref[...]
        acc = ds[0:tm, :] * cw[CONV_K - 1:CONV_K, :]
        for t in range(CONV_K - 1):
            acc = acc + pltpu.roll(ds, tm + HALO - (CONV_K - 1 - t), 0)[0:tm, :] * cw[t:t + 1, :]
        o_ref[...] = acc.astype(o_ref.dtype)

    hb = tm // HALO
    last = M // HALO - 1
    blk = pl.BlockSpec((tm, QKV_W), lambda j, i: (i, j))
    return pl.pallas_call(
        body, name=name, grid=(N_QKV_GROUPS, g_),
        in_specs=[blk, pl.BlockSpec((HALO, QKV_W), lambda j, i: (jnp.minimum((i + 1) * hb, last), j)),
                  pl.BlockSpec((8, QKV_W), lambda j, i: (0, j)), _ANY],
        out_specs=pl.BlockSpec((tm, QKV_W), lambda j, i: (i, QKV_B0 + j)),
        out_shape=jax.ShapeDtypeStruct(d_proj.shape, d_proj.dtype), input_output_aliases={3: 0},
        scratch_shapes=[pltpu.VMEM((tm + HALO, QKV_W), F32)],
        compiler_params=_params("parallel", "arbitrary"),
    )(dc, dc, conv_w8, d_proj)


def _round_robin(gens):
    gens = list(gens)
    while gens:
        alive = []
        for gen in gens:
            try:
                next(gen)
                alive.append(gen)
            except StopIteration:
                pass
        gens = alive


def _unit_lower_inverse(a_low, eye):
    b = -a_low
    x = eye + b
    pw = b
    for _ in range(5):
        pw = _dot3(pw, pw)
        yield
        x = x + _dot3(x, pw)
        yield
    return x


class _GdnChunk:
    def build(self, q, k, v, gb, h):
        C = GDN_CHUNK
        lane = lax.broadcasted_iota(jnp.int32, (C, SM_W), 1)
        g = jnp.sum(jnp.where(lane == h, gb, 0.0), axis=1, keepdims=True)
        self.beta = jnp.sum(jnp.where(lane == h + GDN_HEADS, gb, 0.0), axis=1, keepdims=True)
        ri = lax.broadcasted_iota(jnp.int32, (C, C), 0)
        ci = lax.broadcasted_iota(jnp.int32, (C, C), 1)
        self.causal = ri >= ci
        self.strict = ri > ci
        self.eye = (ri == ci).astype(F32)
        gcb = _dotx(self.causal.astype(F32), jnp.broadcast_to(g, (C, SM_W)))
        yield
        self.gcol = gcb[:, 0:1]
        grow = gcb.T[0:1, 0:C]
        self.decay = jnp.exp(jnp.where(self.causal, self.gcol - grow, -1e30))
        self.egc = jnp.exp(self.gcol)
        glast = gcb[C - 1:C, 0:1]
        self.elast = jnp.exp(glast - self.gcol)
        self.gl = jnp.exp(glast)
        self.q, self.k, self.v = q, k, v
        self.kb = k * self.beta
        m = _dot(self.kb, k, NT)
        n_ = _dot(q, k, NT)
        yield
        self.a_low = jnp.where(self.strict, m * self.decay, 0.0)
        self.p = n_ * self.decay
        self.qd = q * self.egc
        self.kd = k * self.elast
        self.bu = v * self.beta
        self.bw = self.kb * self.egc


GDN_HB = 8
GDN_HG = GDN_HEADS // GDN_HB


def _gdn_specs(n_of):
    C, W = GDN_CHUNK, 128 * GDN_HB
    q_spec = pl.BlockSpec((C, W), lambda g, n: (n_of(n), g))
    k_spec = pl.BlockSpec((C, W), lambda g, n: (n_of(n), g + GDN_HG))
    v_spec = pl.BlockSpec((C, W), lambda g, n: (n_of(n), g + 2 * GDN_HG))
    gb_spec = pl.BlockSpec((C, SM_W), lambda g, n: (n_of(n), 0))
    o_spec = pl.BlockSpec((C, W), lambda g, n: (n_of(n), g))
    s_spec = pl.BlockSpec((GDN_HB, None, GDN_DK, GDN_DV), lambda g, n: (g, n_of(n), 0, 0))
    t_spec = pl.BlockSpec((GDN_HB, None, C, C), lambda g, n: (g, n_of(n), 0, 0))
    return q_spec, k_spec, v_spec, gb_spec, o_spec, s_spec, t_spec


def _gdn_fwd(act, gb, *, name):
    M = act.shape[0]
    N = M // GDN_CHUNK

    def body(q_ref, k_ref, v_ref, gb_ref, o_ref, s_ref, t_ref, state):
        g, n = pl.program_id(0), pl.program_id(1)

        @pl.when(n == 0)
        def _():
            state[...] = jnp.zeros_like(state)

        gb_ = gb_ref[...]

        def head(hh):
            cols = slice(hh * 128, (hh + 1) * 128)
            c = _GdnChunk()
            yield from c.build(q_ref[:, cols], k_ref[:, cols], v_ref[:, cols], gb_, g * GDN_HB + hh)
            tinv = yield from _unit_lower_inverse(c.a_low, c.eye)
            s = state[hh]
            s_ref[hh] = s
            t_ref[hh] = tinv
            u = _dot(tinv, c.bu)
            w = _dot(tinv, c.bw)
            yield
            vn = u - _dot(w, s)
            o1 = _dot(c.qd, s)
            yield
            o_ref[:, cols] = o1 + _dot(c.p, vn)
            state[hh] = c.gl * s + _dot(c.kd, vn, TN)

        _round_robin(head(hh) for hh in range(GDN_HB))

    q_spec, k_spec, v_spec, gb_spec, o_spec, s_spec, t_spec = _gdn_specs(lambda n: n)
    return pl.pallas_call(
        body, name=name, grid=(GDN_HG, N),
        in_specs=[q_spec, k_spec, v_spec, gb_spec], out_specs=[o_spec, s_spec, t_spec],
        out_shape=[jax.ShapeDtypeStruct((M, GDN_V), F32),
                   jax.ShapeDtypeStruct((GDN_HEADS, N, GDN_DK, GDN_DV), F32),
                   jax.ShapeDtypeStruct((GDN_HEADS, N, GDN_CHUNK, GDN_CHUNK), F32)],
        scratch_shapes=[pltpu.VMEM((GDN_HB, GDN_DK, GDN_DV), F32)],
        compiler_params=_params("parallel", "arbitrary"),
    )(act, act, act, gb)


def _gdn_bwd(act, gb, do, s_all, t_all, *, name):
    M = act.shape[0]
    N = M // GDN_CHUNK
    C = GDN_CHUNK
    assert GDN_HG == 1

    def body(q_ref, k_ref, v_ref, gb_ref, do_ref, s_ref, t_ref, dact_ref, dgb_ref, dstate):
        g, n = pl.program_id(0), pl.program_id(1)

        @pl.when(n == 0)
        def _():
            dstate[...] = jnp.zeros_like(dstate)

        gb_ = gb_ref[...]
        last = lax.broadcasted_iota(jnp.int32, (C, 1), 0) == C - 1
        upper = (lax.broadcasted_iota(jnp.int32, (C, C), 0) <= lax.broadcasted_iota(jnp.int32, (C, C), 1)).astype(F32)
        lane = lax.broadcasted_iota(jnp.int32, (C, SM_W), 1)
        def head(hh):
            cols = slice(hh * 128, (hh + 1) * 128)
            h = g * GDN_HB + hh
            c = _GdnChunk()
            yield from c.build(q_ref[:, cols], k_ref[:, cols], v_ref[:, cols], gb_, h)
            tinv = t_ref[hh]
            s = s_ref[hh]
            do_ = do_ref[:, cols]
            ds1 = dstate[hh]
            u = _dot(tinv, c.bu)
            w = _dot(tinv, c.bw)
            dqd = _dot(do_, s, NT)
            dvn0 = _dot(c.p, do_, TN) + _dot(c.kd, ds1)
            dst0 = _dot(c.qd, do_, TN) + c.gl * ds1
            yield
            vn = u - _dot(w, s)
            dvn = dvn0
            yield
            dp = jnp.where(c.causal, _dot(do_, vn, NT), 0.0)
            dstate[hh] = dst0 - _dot(w, dvn, TN)
            dkd = _dot(vn, ds1, NT)
            dw = -_dot(dvn, s, NT)
            dbu = _dot(tinv, dvn, TN)
            dgl = jnp.sum(jnp.sum(s * ds1, axis=1, keepdims=True), axis=0, keepdims=True)
            yield
            dbw = _dot(tinv, dw, TN)
            t1 = _dot(dbu, u, NT)
            yield
            da = jnp.where(c.strict, -(t1 + _dot(dbw, w, NT)), 0.0)
            dn_ = dp * c.decay
            dq0 = _dot(dn_, c.k)
            dk0 = _dot(dn_, c.q, TN)
            yield
            dm = da * c.decay
            e = da * c.a_low + dp * c.p
            dkb = _dot(dm, c.k) + dbw * c.egc
            dact_ref[:, GDN_QK + hh * 128:GDN_QK + (hh + 1) * 128] = (
                _dot(dm, c.kb, TN) + dk0 + dkb * c.beta + dkd * c.elast)
            dact_ref[:, cols] = dq0 + dqd * c.egc
            dact_ref[:, 2 * GDN_QK + hh * 128:2 * GDN_QK + (hh + 1) * 128] = dbu * c.beta
            dbeta = jnp.sum(dbu * c.v, axis=1, keepdims=True) + jnp.sum(dkb * c.k, axis=1, keepdims=True)
            t_kd = jnp.sum(dkd * c.kd, axis=1, keepdims=True)
            dgc = (jnp.sum(e, axis=1, keepdims=True) - jnp.sum(e.T, axis=1, keepdims=True)
                   + jnp.sum(dbw * c.bw, axis=1, keepdims=True) + jnp.sum(dqd * c.qd, axis=1, keepdims=True) - t_kd)
            dgc = dgc + jnp.where(last, jnp.sum(t_kd, axis=0, keepdims=True) + dgl * c.gl, 0.0)
            yield
            dg = _dotx(upper, jnp.broadcast_to(dgc, (C, SM_W)))
            dgb_ref[hh] = jnp.where(lane == h, dg, jnp.where(lane == h + GDN_HEADS, dbeta, 0.0))

        _round_robin(head(hh) for hh in range(GDN_HB))

    rev = lambda n: N - 1 - n
    q_spec, k_spec, v_spec, gb_spec, o_spec, s_spec, t_spec = _gdn_specs(rev)
    dgb_spec = pl.BlockSpec((GDN_HB, C, SM_W), lambda g, n: (g, rev(n), 0))
    return pl.pallas_call(
        body, name=name, grid=(GDN_HG, N),
        in_specs=[q_spec, k_spec, v_spec, gb_spec, o_spec, s_spec, t_spec],
        out_specs=[pl.BlockSpec((C, 2 * GDN_QK + GDN_V), lambda g, n: (rev(n), 0)), dgb_spec],
        out_shape=[jax.ShapeDtypeStruct((M, 2 * GDN_QK + GDN_V), F32),
                   jax.ShapeDtypeStruct((GDN_HEADS, M, SM_W), F32)],
        scratch_shapes=[pltpu.VMEM((GDN_HB, GDN_DK, GDN_DV), F32)],
        compiler_params=_params("parallel", "arbitrary"),
    )(act, act, act, gb, do, s_all, t_all)


GLA_STEP_ROWS = 64
GLA_SUB = GLA_STEP_ROWS // GLA_CHUNK


def _gla_cumsum(la):
    C = GLA_CHUNK
    ltri = (lax.broadcasted_iota(jnp.int32, (C, C), 0) >= lax.broadcasted_iota(jnp.int32, (C, C), 1)).astype(F32)
    return _dotx(ltri, la)


def _gla_decay_rows(b, i):
    rj = lax.broadcasted_iota(jnp.int32, (GLA_CHUNK, GLA_DK), 0)
    return jnp.where(rj <= i, jnp.exp(jnp.minimum(b[i:i + 1, :] - b, 0.0)), 0.0)


def _gla_scores_t(q, k, b):
    C = GLA_CHUNK
    lane = lax.broadcasted_iota(jnp.int32, (C, C), 1)
    st = jnp.zeros((C, C), F32)
    for i in range(C):
        si = jnp.sum(q[i:i + 1, :] * k * _gla_decay_rows(b, i), axis=1, keepdims=True)
        st = jnp.where(lane == i, si, st)
        if i % 4 == 3:
            yield
    return st


def _gla_specs(n_of):
    R = GLA_STEP_ROWS
    q_spec = pl.BlockSpec((R, GLA_QK), lambda n: (n_of(n), C_GQ // GLA_QK))
    k_spec = pl.BlockSpec((R, GLA_QK), lambda n: (n_of(n), C_GK // GLA_QK))
    v_spec = pl.BlockSpec((R, GLA_V), lambda n: (n_of(n), C_GV // GLA_V))
    la_spec = pl.BlockSpec((R, GLA_QK), lambda n: (n_of(n), 0))
    o_spec = pl.BlockSpec((R, GLA_V), lambda n: (n_of(n), 0))
    s_spec = pl.BlockSpec((GLA_HEADS, None, GLA_SUB, GLA_DV, GLA_DK), lambda n: (0, n_of(n), 0, 0, 0))
    return q_spec, k_spec, v_spec, la_spec, o_spec, s_spec


def _gla_fwd(proj, la, *, name):
    M = proj.shape[0]
    N = M // GLA_STEP_ROWS
    C = GLA_CHUNK

    def body(q_ref, k_ref, v_ref, la_ref, o_ref, s_ref, state):
        n = pl.program_id(0)

        @pl.when(n == 0)
        def _():
            state[...] = jnp.zeros_like(state)

        def head(hh):
            kc = slice(hh * GLA_DK, (hh + 1) * GLA_DK)
            vc = slice(hh * GLA_DV, (hh + 1) * GLA_DV)
            st = state[hh]
            for c in range(GLA_SUB):
                rows = slice(c * C, (c + 1) * C)
                q = q_ref[rows, kc] * (GLA_DK ** -0.5)
                k = k_ref[rows, kc]
                v = v_ref[rows, vc]
                b = _gla_cumsum(la_ref[rows, kc])
                yield
                s_ref[hh, c] = st
                blast = b[C - 1:C, :]
                sc_t = yield from _gla_scores_t(q, k, b)
                o1 = _dot(q * jnp.exp(b), st, NT)
                kv = _dot(v, k * jnp.exp(blast - b), TN)
                o2 = _dot(sc_t, v, TN)
                yield
                o_ref[rows, vc] = o1 + o2
                st = st * jnp.exp(blast) + kv
            state[hh] = st

        _round_robin(head(hh) for hh in range(GLA_HEADS))

    q_spec, k_spec, v_spec, la_spec, o_spec, s_spec = _gla_specs(lambda n: n)
    return pl.pallas_call(
        body, name=name, grid=(N,),
        in_specs=[q_spec, k_spec, v_spec, la_spec], out_specs=[o_spec, s_spec],
        out_shape=[jax.ShapeDtypeStruct((M, GLA_V), F32),
                   jax.ShapeDtypeStruct((GLA_HEADS, N, GLA_SUB, GLA_DV, GLA_DK), F32)],
        scratch_shapes=[pltpu.VMEM((GLA_HEADS, GLA_DV, GLA_DK), F32)],
        compiler_params=_params("arbitrary"),
    )(proj, proj, proj, la)


def _gla_bwd(proj, la, do, s_all, d_proj, *, name):
    M = proj.shape[0]
    N = M // GLA_STEP_ROWS
    C = GLA_CHUNK
    qkv_w = 2 * GLA_QK + GLA_V
    assert C_GK == C_GQ + GLA_QK and C_GV == C_GK + GLA_QK and C_GQ % qkv_w == 0

    def body(q_ref, k_ref, v_ref, la_ref, do_ref, s_ref, _, dp_ref, dla_ref, dstate):
        n = pl.program_id(0)

        @pl.when(n == 0)
        def _():
            dstate[...] = jnp.zeros_like(dstate)

        lane = lax.broadcasted_iota(jnp.int32, (C, C), 1)
        ri = lax.broadcasted_iota(jnp.int32, (C, GLA_DK), 0)
        upper = (lax.broadcasted_iota(jnp.int32, (C, C), 0) <= lane).astype(F32)
        def head(hh):
            kc = slice(hh * GLA_DK, (hh + 1) * GLA_DK)
            vc = slice(hh * GLA_DV, (hh + 1) * GLA_DV)
            ds1 = dstate[hh]
            for c in reversed(range(GLA_SUB)):
                rows = slice(c * C, (c + 1) * C)
                q = q_ref[rows, kc] * (GLA_DK ** -0.5)
                k = k_ref[rows, kc]
                v = v_ref[rows, vc]
                b = _gla_cumsum(la_ref[rows, kc])
                do_ = do_ref[rows, vc]
                st = s_ref[hh, c]
                dsc_t = _dot(v, do_, NT)
                dqe = _dot(do_, st)
                dke = _dot(v, ds1)
                yield
                blast = b[C - 1:C, :]
                eb = jnp.exp(b)
                elast = jnp.exp(blast - b)
                eblast = jnp.exp(blast)
                qe = q * eb
                ke = k * elast
                dv2 = _dot(ke, ds1, NT)
                ds_new = _dot(do_, qe, TN)
                deblast = jnp.sum(st * ds1, axis=0, keepdims=True)
                sc_t = jnp.zeros((C, C), F32)
                dq_sc = jnp.zeros((C, GLA_DK), F32)
                dk_sc = jnp.zeros((C, GLA_DK), F32)
                for i in range(C):
                    f = _gla_decay_rows(b, i)
                    kf = k * f
                    si = jnp.sum(q[i:i + 1, :] * kf, axis=1, keepdims=True)
                    sc_t = jnp.where(lane == i, si, sc_t)
                    dsi = jnp.sum(jnp.where(lane == i, dsc_t, 0.0), axis=1, keepdims=True)
                    dq_sc = jnp.where(ri == i, jnp.sum(dsi * kf, axis=0, keepdims=True), dq_sc)
                    dk_sc = dk_sc + (dsi * f) * q[i:i + 1, :]
                    if i % 4 == 3:
                        yield
                dv1 = _dot(sc_t, do_)
                dp_ref[rows, kc] = ((dq_sc + dqe * eb) * (GLA_DK ** -0.5)).astype(dp_ref.dtype)
                dp_ref[rows, GLA_QK + hh * GLA_DK:GLA_QK + (hh + 1) * GLA_DK] = (dk_sc + dke * elast).astype(dp_ref.dtype)
                t_ke = dke * ke
                db = q * dq_sc - k * dk_sc + dqe * qe - t_ke
                db = db + jnp.where(ri == C - 1, jnp.sum(t_ke, axis=0, keepdims=True) + deblast * eblast, 0.0)
                dla = _dotx(upper, db)
                yield
                dp_ref[rows, 2 * GLA_QK + hh * GLA_DV:2 * GLA_QK + (hh + 1) * GLA_DV] = (dv1 + dv2).astype(dp_ref.dtype)
                dla_ref[rows, kc] = dla
                ds1 = ds1 * eblast + ds_new
            dstate[hh] = ds1

        _round_robin(head(hh) for hh in range(GLA_HEADS))

    rev = lambda n: N - 1 - n
    q_spec, k_spec, v_spec, la_spec, o_spec, s_spec = _gla_specs(rev)
    return pl.pallas_call(
        body, name=name, grid=(N,),
        in_specs=[q_spec, k_spec, v_spec, la_spec, o_spec, s_spec, _ANY],
        out_specs=[pl.BlockSpec((GLA_STEP_ROWS, qkv_w), lambda n: (rev(n), C_GQ // qkv_w)), la_spec],
        out_shape=[jax.ShapeDtypeStruct(d_proj.shape, d_proj.dtype), jax.ShapeDtypeStruct((M, GLA_QK), F32)],
        input_output_aliases={6: 0},
        scratch_shapes=[pltpu.VMEM((GLA_HEADS, GLA_DV, GLA_DK), F32)],
        compiler_params=_params("arbitrary"),
    )(proj, proj, proj, la, do, s_all, d_proj)


def _head_norm(o, wn):
    r = lax.rsqrt(jnp.mean(o * o, axis=-1, keepdims=True) + NORM_EPS)
    return o * r, r


def _mix_heads():
    heads = [(0, GDN_DV, hh * GDN_DV, hh * GDN_DV) for hh in range(GDN_HEADS)]
    heads += [(1, GLA_DV, GDN_V + hh * GLA_DV, hh * GLA_DV) for hh in range(GLA_HEADS)]
    return heads


def _mix_fwd(o_gdn, o_gla, proj, wn_gdn, wn_gla, *, name):
    M = proj.shape[0]
    tm = _tile(M, 344, 16)

    def body(og_ref, ol_ref, z_ref, r_ref, wg_ref, wl_ref, m_ref):
        srcs = ((og_ref, z_ref, wg_ref), (ol_ref, r_ref, wl_ref))
        for grp, width, mcol, col in _mix_heads():
            o_ref, gate_ref, w_ref = srcs[grp]
            xhat, _ = _head_norm(o_ref[:, col:col + width], None)
            gate, _ = _silu_and_grad(gate_ref[:, col:col + width])
            m_ref[:, mcol:mcol + width] = (xhat * w_ref[...] * gate).astype(m_ref.dtype)

    full = lambda s: pl.BlockSpec(s, lambda i: (0, 0))
    return pl.pallas_call(
        body, name=name, grid=(M // tm,),
        in_specs=[pl.BlockSpec((tm, GDN_V), lambda i: (i, 0)), pl.BlockSpec((tm, GLA_V), lambda i: (i, 0)),
                  pl.BlockSpec((tm, GDN_V), lambda i: (i, C_Z // GDN_V)),
                  pl.BlockSpec((tm, GLA_V), lambda i: (i, C_GR // GLA_V)),
                  full((1, GDN_DV)), full((1, GLA_DV))],
        out_specs=pl.BlockSpec((tm, D_MODEL), lambda i: (i, 0)),
        out_shape=jax.ShapeDtypeStruct((M, D_MODEL), BF16),
        compiler_params=_params("parallel"),
    )(o_gdn, o_gla, proj, proj, wn_gdn, wn_gla)


def _mix_bwd(o_gdn, o_gla, proj, wn_gdn, wn_gla, dmixed, *, name):
    M = proj.shape[0]
    tm = _tile(M, 344, 16)
    g_ = M // tm
    assert C_Z == 0 and C_GR == GDN_V

    def body(og_ref, ol_ref, z_ref, r_ref, wg_ref, wl_ref, dm_ref,
             dog_ref, dol_ref, dzr_ref, dwg_ref, dwl_ref):
        i = pl.program_id(0)
        srcs = ((og_ref, z_ref, wg_ref, dog_ref), (ol_ref, r_ref, wl_ref, dol_ref))
        dws = [jnp.zeros((1, GDN_DV), F32), jnp.zeros((1, GLA_DV), F32)]
        for grp, width, mcol, col in _mix_heads():
            o_ref, gate_ref, w_ref, do_ref = srcs[grp]
            cols = slice(col, col + width)
            xhat, r = _head_norm(o_ref[:, cols], None)
            gate, dgate_dc = _silu_and_grad(gate_ref[:, cols])
            dm = dm_ref[:, mcol:mcol + width]
            dzr_ref[:, mcol:mcol + width] = (dm * xhat * w_ref[...] * dgate_dc).astype(dzr_ref.dtype)
            dnorm = dm * gate
            dws[grp] = dws[grp] + jnp.sum(dnorm * xhat, axis=0, keepdims=True)
            dxhat = dnorm * w_ref[...]
            do_ref[:, cols] = r * (dxhat - xhat * jnp.mean(dxhat * xhat, axis=-1, keepdims=True))

        @pl.when(i == 0)
        def _():
            dwg_ref[...] = dws[0]
            dwl_ref[...] = dws[1]

        @pl.when(i > 0)
        def _():
            dwg_ref[...] += dws[0]
            dwl_ref[...] += dws[1]

    full = lambda s: pl.BlockSpec(s, lambda i: (0, 0))
    half = pl.BlockSpec((tm, GDN_V), lambda i: (i, 0))
    return pl.pallas_call(
        body, name=name, grid=(g_,),
        in_specs=[half, half, pl.BlockSpec((tm, GDN_V), lambda i: (i, C_Z // GDN_V)),
                  pl.BlockSpec((tm, GLA_V), lambda i: (i, C_GR // GLA_V)),
                  full((1, GDN_DV)), full((1, GLA_DV)), pl.BlockSpec((tm, D_MODEL), lambda i: (i, 0))],
        out_specs=[half, half, pl.BlockSpec((tm, GDN_V + GLA_V), lambda i: (i, 0)),
                   full((1, GDN_DV)), full((1, GLA_DV))],
        out_shape=[jax.ShapeDtypeStruct((M, GDN_V), F32), jax.ShapeDtypeStruct((M, GLA_V), F32),
                   jax.ShapeDtypeStruct((M, D_PROJ), BF16),
                   jax.ShapeDtypeStruct((1, GDN_DV), F32), jax.ShapeDtypeStruct((1, GLA_DV), F32)],
        compiler_params=_params("arbitrary"),
    )(o_gdn, o_gla, proj, proj, wn_gdn, wn_gla, dmixed)


def _swiglu_fwd(gate, up, *, name):
    M, F = gate.shape
    tm, tf = _tile(M, 688, 16), _tile(F, 1408, 128)

    def body(g_ref, u_ref, a_ref):
        s, _ = _silu_and_grad(g_ref[...].astype(F32))
        a_ref[...] = (s * u_ref[...].astype(F32)).astype(a_ref.dtype)

    blk = pl.BlockSpec((tm, tf), lambda i, j: (i, j))
    return pl.pallas_call(
        body, name=name, grid=(M // tm, F // tf), in_specs=[blk, blk], out_specs=blk,
        out_shape=jax.ShapeDtypeStruct((M, F), BF16), compiler_params=_params("parallel", "parallel"),
    )(gate, up)


def _swiglu_bwd(gate, up, da, *, name, after=None):
    M, F = gate.shape
    tm, tf = _tile(M, 688, 16), _tile(F, 1408, 128)
    n_after = 0 if after is None else 1

    def body(*refs):
        g_ref, u_ref, da_ref, dg_ref, du_ref = refs[n_after:]
        s, ds = _silu_and_grad(g_ref[...].astype(F32))
        da_ = da_ref[...].astype(F32)
        dg_ref[...] = (da_ * u_ref[...].astype(F32) * ds).astype(dg_ref.dtype)
        du_ref[...] = (da_ * s).astype(du_ref.dtype)

    blk = pl.BlockSpec((tm, tf), lambda i, j: (i, j))
    return pl.pallas_call(
        body, name=name, grid=(M // tm, F // tf), in_specs=[_ANY] * n_after + [blk, blk, blk], out_specs=[blk, blk],
        out_shape=[jax.ShapeDtypeStruct((M, F), BF16), jax.ShapeDtypeStruct((M, F), BF16)],
        compiler_params=_params("parallel", "parallel"),
    )(*((after,) if n_after else ()), gate, up, da)


def _adamw(w, g, m, v, *, name):
    shape = w.shape
    cols = shape[-1]
    rows = w.size // cols
    w2, g2, m2, v2 = (t.reshape(rows, cols) for t in (w, g, m, v))
    tr = _tile(rows, 256, 8) if rows % 8 == 0 else rows

    def body(w_ref, g_ref, m_ref, v_ref, d_ref, nm_ref, nv_ref):
        g_ = g_ref[...]
        nm = ADAM_B1 * m_ref[...] + (1.0 - ADAM_B1) * g_
        nv = ADAM_B2 * v_ref[...] + (1.0 - ADAM_B2) * (g_ * g_)
        m_hat = nm / (1.0 - ADAM_B1 ** ADAM_STEP)
        v_hat = nv / (1.0 - ADAM_B2 ** ADAM_STEP)
        d_ref[...] = -ADAM_LR * (m_hat / (jnp.sqrt(v_hat) + ADAM_EPS) + ADAM_WD * w_ref[...])
        nm_ref[...] = nm
        nv_ref[...] = nv

    blk = pl.BlockSpec((tr, cols), lambda i: (i, 0))
    outs = pl.pallas_call(
        body, name=name, grid=(rows // tr,), in_specs=[blk] * 4, out_specs=[blk] * 3,
        out_shape=[jax.ShapeDtypeStruct((rows, cols), F32)] * 3, compiler_params=_params("parallel"),
    )(w2, g2, m2, v2)
    return tuple(t.reshape(shape) for t in outs)


def _sum_slabs(x, *, name):
    _, R, C = x.shape
    sub = 16 if x.dtype == BF16 else 8
    if R % sub == 0:
        tr, tc = _tile(R, 128, sub), C
    else:
        tr, tc = R, _tile(C, 256, 128)

    def body(x_ref, o_ref):
        acc = x_ref[0].astype(F32)
        for s in range(1, N_DEV):
            acc = acc + x_ref[s].astype(F32)
        o_ref[...] = acc

    return pl.pallas_call(
        body, name=name, grid=(R // tr, C // tc),
        in_specs=[pl.BlockSpec((N_DEV, tr, tc), lambda i, j: (0, i, j))],
        out_specs=pl.BlockSpec((tr, tc), lambda i, j: (i, j)),
        out_shape=jax.ShapeDtypeStruct((R, C), F32), compiler_params=_params("parallel", "parallel"),
    )(x)


def _peers():
    x, y, c = lax.axis_index("x"), lax.axis_index("y"), lax.axis_index("c")
    me = 4 * x + 2 * y + c
    peers = []
    for k in range(1, N_DEV):
        px = 1 - x if k & 4 else x
        py = 1 - y if k & 2 else y
        pc = 1 - c if k & 1 else c
        peers.append(((px, py, pc), 4 * px + 2 * py + pc))
    return me, peers


def _exchange(x, *, gather, name):
    slab = x.shape if gather else x.shape[1:]

    def body(x_ref, o_ref, send_sems, recv_sems, own_sem):
        me, peers = _peers()
        own = pltpu.make_async_copy(x_ref if gather else x_ref.at[me], o_ref.at[me], own_sem)
        own.start()
        sends, recvs = [], []
        for k, (pos, idx) in enumerate(peers):
            sends.append(pltpu.make_async_remote_copy(
                src_ref=x_ref if gather else x_ref.at[idx], dst_ref=o_ref.at[me],
                send_sem=send_sems.at[k], recv_sem=recv_sems.at[k],
                device_id=pos, device_id_type=pl.DeviceIdType.MESH))
            recvs.append(pltpu.make_async_remote_copy(
                src_ref=x_ref if gather else x_ref.at[idx], dst_ref=o_ref.at[idx],
                send_sem=send_sems.at[k], recv_sem=recv_sems.at[k],
                device_id=pos, device_id_type=pl.DeviceIdType.MESH))
        for cp in sends:
            cp.start()
        for cp in recvs:
            cp.wait_recv()
        for cp in sends:
            cp.wait_send()
        own.wait()

    hbm = pl.BlockSpec(memory_space=pltpu.HBM)
    return pl.pallas_call(
        body, name=name, in_specs=[hbm], out_specs=hbm,
        out_shape=jax.ShapeDtypeStruct((N_DEV,) + tuple(slab), x.dtype),
        scratch_shapes=[pltpu.SemaphoreType.DMA((N_DEV - 1,)), pltpu.SemaphoreType.DMA((N_DEV - 1,)),
                        pltpu.SemaphoreType.DMA],
    )(x)


_HBM = pl.BlockSpec(memory_space=pltpu.HBM)
_SEM = pl.BlockSpec(memory_space=pltpu.SEMAPHORE)
_EFFECT = pltpu.SideEffectType.DATAFLOW_SIDE_EFFECTING


def _exchange_start(x, *, gather, name, after=None):
    slab = x.shape if gather else x.shape[1:]
    n_after = 0 if after is None else 1

    def body(*refs):
        x_ref, land_ref, send_sems, recv_sems, _, _, token = refs[n_after:]
        me, peers = _peers()
        for k, (pos, idx) in enumerate(peers):
            pltpu.make_async_remote_copy(
                src_ref=x_ref if gather else x_ref.at[idx], dst_ref=land_ref.at[me],
                send_sem=send_sems.at[k], recv_sem=recv_sems.at[k],
                device_id=pos, device_id_type=pl.DeviceIdType.MESH).start()
        token[...] = jnp.zeros_like(token)

    land = lax.empty((N_DEV,) + tuple(slab), x.dtype)
    return pl.pallas_call(
        body, name=name,
        out_shape=(pltpu.SemaphoreType.DMA((N_DEV - 1,)), pltpu.SemaphoreType.DMA((N_DEV - 1,)),
                   pltpu.HBM(x.shape, x.dtype), pltpu.HBM(land.shape, land.dtype), jax.ShapeDtypeStruct((8, 128), F32)),
        in_specs=[_ANY] * n_after + [_HBM, _HBM],
        out_specs=(_SEM, _SEM, _HBM, _HBM, pl.BlockSpec(memory_space=pltpu.VMEM)),
        input_output_aliases={n_after: 2, n_after + 1: 3},
        compiler_params=pltpu.CompilerParams(has_side_effects=_EFFECT),
    )(*((after,) if n_after else ()), pltpu.with_memory_space_constraint(x, pltpu.HBM),
      pltpu.with_memory_space_constraint(land, pltpu.HBM))


def _exchange_wait(handle, after, *, gather, name):
    send_sems, recv_sems, x_thru, land_thru, _ = handle

    def body(x_ref, land_ref, send_sems, recv_sems, after_ref, x_out, land_out):
        me, peers = _peers()
        for k, (pos, idx) in enumerate(peers):
            cp = pltpu.make_async_remote_copy(
                src_ref=x_ref if gather else x_ref.at[idx], dst_ref=land_ref.at[idx],
                send_sem=send_sems.at[k], recv_sem=recv_sems.at[k],
                device_id=pos, device_id_type=pl.DeviceIdType.MESH)
            cp.wait_send()
            cp.wait_recv()

    return pl.pallas_call(
        body, name=name,
        out_shape=(pltpu.HBM(x_thru.shape, x_thru.dtype), pltpu.HBM(land_thru.shape, land_thru.dtype)),
        in_specs=(_HBM, _HBM, _SEM, _SEM, _ANY), out_specs=(_HBM, _HBM), input_output_aliases={0: 0, 1: 1},
        compiler_params=pltpu.CompilerParams(has_side_effects=_EFFECT),
    )(x_thru, land_thru, send_sems, recv_sems, after)


def _to_proj_rows(t):
    z = jnp.zeros((D_PROJ - C_SM - 2 * GDN_HEADS - GLA_RANK,) + t.shape[1:], t.dtype)
    return jnp.concatenate([t[R_Z:R_A], t[R_GR:R_LR], t[R_GQ:R_GR], t[:R_Z], t[R_A:R_GQ], t[R_LR:], z], axis=0)


def _from_proj_rows(t):
    ab = C_SM + 2 * GDN_HEADS
    return jnp.concatenate([t[C_QKV:C_SM], t[C_Z:C_GR], t[C_SM:ab], t[C_GQ:C_QKV], t[C_GR:C_GQ],
                            t[ab:ab + GLA_RANK]], axis=0)


def _local_step(x, target, meta, attn_nw, conv_w, a_log, dt_bias, gdn_nw, w2, b2, gla_nw, ffn_nw, final_nw,
                fetch, emit, start=None):
    S = x.shape[0]
    h0 = jnp.concatenate([jnp.zeros((ROW_PAD, D_MODEL), F32), meta, x], axis=0)
    target_p = jnp.concatenate([jnp.zeros((HEAD_ROWS, D_MODEL), F32), target], axis=0)
    conv_w8 = jnp.concatenate([conv_w, jnp.zeros((8 - CONV_K, conv_w.shape[1]), F32)], axis=0)
    w2p = jnp.zeros((SM_W, GLA_QK), F32).at[2 * GDN_HEADS:2 * GDN_HEADS + GLA_RANK].set(w2)
    alog_p = jnp.zeros((1, SM_W), F32).at[:, :GDN_HEADS].set(a_log)
    dt_p = jnp.zeros((1, SM_W), F32).at[:, :GDN_HEADS].set(dt_bias)

    n1 = _rmsnorm_fwd(h0, attn_nw, name="attn_norm", after=start)
    w_in_t = fetch("w_in_t", n1)
    proj = _matmul(n1, w_in_t, mode="nt", name="in_proj")
    gb, la = _gates_fwd(proj, w2p, b2, alog_p, dt_p, name="gates")
    act = _prep_fwd(proj, conv_w8, name="gdn_prep")
    o_gdn, s_gdn, t_gdn = _gdn_fwd(act, gb, name="gdn_fwd")
    o_gla, s_gla = _gla_fwd(proj, la, name="gla_fwd")
    mixed = _mix_fwd(o_gdn, o_gla, proj, gdn_nw, gla_nw, name="mix")
    w_gate_t, w_up_t, w_out, w_down = fetch("rest", mixed)
    h1 = _matmul(mixed, w_out, mode="nn", add=h0, name="out_proj")
    n2 = _rmsnorm_fwd(h1, ffn_nw, name="ffn_norm")
    gate = _matmul(n2, w_gate_t, mode="nt", name="ffn_gate", out_dtype=BF16)
    up = _matmul(n2, w_up_t, mode="nt", name="ffn_up", out_dtype=BF16)
    hid = _swiglu_fwd(gate, up, name="swiglu")
    h2 = _matmul(hid, w_down, mode="nn", add=h1, name="ffn_down", tm=688, tk=D_FF)
    dh2, dh2_b, d_final_nw, loss = _loss_head(h2, final_nw, target_p, name="loss_head")

    d_hid = _matmul(dh2_b, w_down, mode="nt", name="d_hid", out_dtype=BF16)
    wg = dict(mode="tn", out_dtype=BF16, tn=512, tk=S + HEAD_ROWS)
    tok = emit("w_down", _matmul(hid, dh2_b, name="d_w_down", tm=704, **wg))
    d_gate, d_up = _swiglu_bwd(gate, up, d_hid, name="d_swiglu", after=tok)
    tok = emit("w_gate_t", _matmul(d_gate, n2, name="d_w_gate", tm=704, **wg))
    tok = emit("w_up_t", _matmul(d_up, n2, name="d_w_up", tm=704, after=tok, **wg))
    d_n2 = _matmul(d_gate, w_gate_t, mode="nn", name="d_n2_gate", tm=688, tk=D_FF, after=tok)
    d_n2 = _matmul(d_up, w_up_t, mode="nn", add=d_n2, name="d_n2_up", tm=688, tk=D_FF)
    dh1, dh1_b, d_ffn_nw = _rmsnorm_bwd(h1, ffn_nw, d_n2, dh2, name="d_ffn_norm", also_bf16=True)

    tok = emit("w_out", _matmul(mixed, dh1_b, name="d_w_out", tm=512, **wg))
    d_mixed = _matmul(dh1_b, w_out, mode="nt", name="d_mixed", after=tok)
    do_gdn, do_gla, d_proj, d_gdn_nw, d_gla_nw = _mix_bwd(o_gdn, o_gla, proj, gdn_nw, gla_nw, d_mixed, name="d_mix")
    d_proj, d_la = _gla_bwd(proj, la, do_gla, s_gla, d_proj, name="gla_bwd")
    dact, dgb_heads = _gdn_bwd(act, gb, do_gdn, s_gdn, t_gdn, name="gdn_bwd")
    d_proj, d_w2p, d_b2, d_alog, d_dt = _gates_bwd(proj, w2p, b2, alog_p, dt_p, dgb_heads, d_la, d_proj, name="d_gates")
    dc, d_conv_w8 = _prep_bwd_a(proj, conv_w8, dact, name="d_gdn_prep")
    d_proj = _prep_bwd_b(dc, conv_w8, d_proj, name="d_conv")
    tok = emit("w_in_t", _matmul(d_proj, n1, name="d_w_in", tm=768, **wg))
    d_n1 = _matmul(d_proj, w_in_t, mode="nn", name="d_n1", tm=688, tk=D_PROJ, after=tok)
    dh0, d_attn_nw = _rmsnorm_bwd(h0, attn_nw, d_n1, dh1, name="d_attn_norm", also_bf16=False)

    return dict(
        loss=loss[0, 0], grad_x=dh0[HEAD_ROWS:], meta=dh0[ROW_PAD:HEAD_ROWS], attn_nw=d_attn_nw,
        conv_w=d_conv_w8[:CONV_K], a_log=d_alog[:, :GDN_HEADS], dt_bias=d_dt[:, :GDN_HEADS], gdn_nw=d_gdn_nw,
        w2=d_w2p[2 * GDN_HEADS:2 * GDN_HEADS + GLA_RANK], b2=d_b2, gla_nw=d_gla_nw, ffn_nw=d_ffn_nw,
        final_nw=d_final_nw)


SMALL_ROWS = 32


def kernel(x, meta_tokens, attn_norm_w, w_in, gdn_conv_w, gdn_a_log, gdn_dt_bias, gdn_norm_w, gla_gate_w2, gla_gate_b, gla_norm_w, w_out, ffn_norm_w, w_gate, w_up, w_down, final_norm_w, loss_target, m_meta_tokens, m_attn_norm_w, m_w_in, m_gdn_conv_w, m_gdn_a_log, m_gdn_dt_bias, m_gdn_norm_w, m_gla_gate_w2, m_gla_gate_b, m_gla_norm_w, m_w_out, m_ffn_norm_w, m_w_gate, m_w_up, m_w_down, m_final_norm_w, v_meta_tokens, v_attn_norm_w, v_w_in, v_gdn_conv_w, v_gdn_a_log, v_gdn_dt_bias, v_gdn_norm_w, v_gla_gate_w2, v_gla_gate_b, v_gla_norm_w, v_w_out, v_ffn_norm_w, v_w_gate, v_w_up, v_w_down, v_final_norm_w):
    me = 4 * lax.axis_index("x") + 2 * lax.axis_index("y") + lax.axis_index("c")
    n_in, n_ff, n_out = D_IN // N_DEV, D_FF // N_DEV, D_MODEL // N_DEV

    n_conv = gdn_conv_w.shape[2]
    n_w2 = gla_gate_w2.shape[2]
    n_meta = meta_tokens.shape[1]
    small = jnp.zeros((40, n_conv), F32)
    small = small.at[0:N_META, :n_meta].set(meta_tokens)
    small = small.at[N_META:N_META + CONV_K, :].set(gdn_conv_w[0])
    small = small.at[24:24 + GLA_RANK, :n_w2].set(gla_gate_w2[0])
    small_all = _exchange(small, gather=True, name="gather_small")
    meta_f = small_all[:, 0:N_META, :n_meta].transpose(1, 0, 2).reshape(N_META, D_MODEL)
    conv_f = small_all[:, N_META:N_META + CONV_K, :].transpose(1, 0, 2).reshape(CONV_K, N_DEV * n_conv)
    w2_f = small_all[:, 24:24 + GLA_RANK, :n_w2].transpose(1, 0, 2).reshape(GLA_RANK, N_DEV * n_w2)

    o1, o2, o3 = n_ff, 2 * n_ff, 2 * n_ff + n_out
    in_h = _exchange_start(w_in[0].T.astype(BF16), gather=True, name="gather_w_in_start")
    rest = jnp.concatenate([w_gate[0].T, w_up[0].T, w_out[0], w_down[0]], axis=0).astype(BF16)
    rest_h = _exchange_start(rest, gather=True, name="gather_rest_start", after=in_h[4])

    def fetch(name, after):
        handle = in_h if name == "w_in_t" else rest_h
        own, got = _exchange_wait(handle, after, gather=True, name="gather_" + name + "_wait")
        got = lax.dynamic_update_index_in_dim(got, own, me, 0)
        if name == "w_in_t":
            return _to_proj_rows(got.reshape(D_IN, D_MODEL))
        return (got[:, :o1].reshape(D_FF, D_MODEL), got[:, o1:o2].reshape(D_FF, D_MODEL),
                got[:, o2:o3].reshape(D_MODEL, D_MODEL), got[:, o3:].reshape(D_FF, D_MODEL))

    sent = {}

    def emit(name, grad):
        if name == "w_in_t":
            grad = _from_proj_rows(grad)
        parts = grad.reshape(N_DEV, grad.shape[0] // N_DEV, D_MODEL)
        sent[name] = _exchange_start(parts, gather=False, name="scatter_" + name + "_start")
        return sent[name][4]

    g = _local_step(x[0], loss_target[0], meta_f, attn_norm_w, conv_f, gdn_a_log, gdn_dt_bias, gdn_norm_w, w2_f,
                    gla_gate_b, gla_norm_w, ffn_norm_w, final_norm_w.reshape(1, D_MODEL), fetch, emit, start=rest_h[4])

    def total(name, after):
        handle = sent[name]
        own, got = _exchange_wait(handle, after, gather=False, name="scatter_" + name + "_wait")
        got = lax.dynamic_update_index_in_dim(got, lax.dynamic_index_in_dim(own, me, 0, keepdims=False), me, 0)
        return _sum_slabs(got, name="sum_" + name)

    grad_w_down = total("w_down", g["grad_x"])[None]
    grad_w_gate = total("w_gate_t", grad_w_down).T[None]
    grad_w_up = total("w_up_t", grad_w_gate).T[None]
    grad_w_out = total("w_out", grad_w_up)[None]
    grad_w_in = total("w_in_t", grad_w_out).T[None]

    misc = jnp.concatenate([g["a_log"], g["dt_bias"], g["gdn_nw"], g["gla_nw"], g["b2"], g["loss"].reshape(1, 1)], axis=1)
    n_misc = misc.shape[1]
    misc = jnp.pad(misc, ((0, 0), (0, D_MODEL - n_misc)))
    rows = jnp.concatenate([g["attn_nw"], g["ffn_nw"], g["final_nw"], misc, g["meta"],
                            g["conv_w"].reshape(-1, D_MODEL), g["w2"].reshape(-1, D_MODEL)], axis=0)
    rows = jnp.pad(rows, ((0, SMALL_ROWS - rows.shape[0]), (0, 0)))
    tot = _sum_slabs(_exchange(rows, gather=True, name="gather_small_grads"), name="sum_small_grads")
    grad_attn_nw, grad_ffn_nw, grad_final_nw = tot[0:1], tot[1:2], tot[2]
    grad_a_log = tot[3:4, 0:8]
    grad_dt = tot[3:4, 8:16]
    grad_gdn_nw = tot[3:4, 16:16 + GDN_DV]
    grad_gla_nw = tot[3:4, 144:144 + GLA_DV]
    grad_b2 = tot[3:4, 400:400 + GLA_QK]
    loss = tot[3, n_misc - 1]
    r0 = 4 + N_META
    grad_meta = lax.dynamic_slice(tot[4:r0], (0, me * n_meta), (N_META, n_meta))
    r1 = r0 + CONV_K * N_DEV * n_conv // D_MODEL
    grad_conv = lax.dynamic_slice(tot[r0:r1].reshape(CONV_K, N_DEV * n_conv), (0, me * n_conv), (CONV_K, n_conv))[None]
    r2 = r1 + GLA_RANK * N_DEV * n_w2 // D_MODEL
    grad_w2 = lax.dynamic_slice(tot[r1:r2].reshape(GLA_RANK, N_DEV * n_w2), (0, me * n_w2), (GLA_RANK, n_w2))[None]

    weights = [meta_tokens, attn_norm_w, w_in, gdn_conv_w, gdn_a_log, gdn_dt_bias, gdn_norm_w, gla_gate_w2,
               gla_gate_b, gla_norm_w, w_out, ffn_norm_w, w_gate, w_up, w_down, final_norm_w]
    grads = [grad_meta, grad_attn_nw, grad_w_in, grad_conv, grad_a_log, grad_dt, grad_gdn_nw, grad_w2,
             grad_b2, grad_gla_nw, grad_w_out, grad_ffn_nw, grad_w_gate, grad_w_up, grad_w_down, grad_final_nw]
    ms = [m_meta_tokens, m_attn_norm_w, m_w_in, m_gdn_conv_w, m_gdn_a_log, m_gdn_dt_bias, m_gdn_norm_w,
          m_gla_gate_w2, m_gla_gate_b, m_gla_norm_w, m_w_out, m_ffn_norm_w, m_w_gate, m_w_up, m_w_down, m_final_norm_w]
    vs = [v_meta_tokens, v_attn_norm_w, v_w_in, v_gdn_conv_w, v_gdn_a_log, v_gdn_dt_bias, v_gdn_norm_w,
          v_gla_gate_w2, v_gla_gate_b, v_gla_norm_w, v_w_out, v_ffn_norm_w, v_w_gate, v_w_up, v_w_down, v_final_norm_w]
    grads = [gr.reshape(w.shape) for gr, w in zip(grads, weights)]
    deltas, new_ms, new_vs = [], [], []
    for idx, (w, gr, m, v) in enumerate(zip(weights, grads, ms, vs)):
        d, nm, nv = _adamw(w, gr, m, v, name=f"adamw_{idx}")
        deltas.append(d)
        new_ms.append(nm)
        new_vs.append(nv)
    return (loss, g["grad_x"][None], *grads, *deltas, *new_ms, *new_vs)
```

```python
import functools

import jax
import jax.numpy as jnp
from jax import lax
from jax.experimental import pallas as pl
from jax.experimental.pallas import tpu as pltpu

F32 = jnp.float32
BF16 = jnp.bfloat16
_MXU_DTYPE = jnp.bfloat16

D_MODEL = 2048
N_META = 16
ROW_PAD = 48
HEAD_ROWS = ROW_PAD + N_META
CONV_K = 4
GDN_HEADS, GDN_DK, GDN_DV, GDN_CHUNK = 8, 128, 128, 64
GLA_HEADS, GLA_DK, GLA_DV, GLA_CHUNK = 4, 128, 256, 16
GLA_RANK = 16
GLA_GATE_NORMALIZER = 16.0
GDN_QK = GDN_HEADS * GDN_DK
GDN_V = GDN_HEADS * GDN_DV
GLA_QK = GLA_HEADS * GLA_DK
GLA_V = GLA_HEADS * GLA_DV
D_FF = 5632
D_IN = 7200
NORM_EPS = 1e-6
C_Z, C_GR, C_GQ, C_GK, C_GV, C_QKV, C_SM = 0, 1024, 2048, 2560, 3072, 4096, 7168
SM_W = 128
D_PROJ = 7680
R_Z, R_A, R_B, R_GQ, R_GK, R_GV, R_GR, R_LR = 3072, 4096, 4104, 4112, 4624, 5136, 6160, 7184

ADAM_LR, ADAM_B1, ADAM_B2, ADAM_EPS, ADAM_WD, ADAM_STEP = 0.001, 0.9, 0.999, 1e-08, 0.01, 10

N_DEV = 8
VMEM_LIMIT = 56 * 1024 * 1024

NN = (((1,), (0,)), ((), ()))
NT = (((1,), (1,)), ((), ()))
TN = (((0,), (0,)), ((), ()))


def _dot(a, b, dims=NN):
    return lax.dot_general(a.astype(_MXU_DTYPE), b.astype(_MXU_DTYPE), dims, preferred_element_type=F32)


def _dotx(a, b, dims=NN):
    return lax.dot_general(a, b, dims, precision=lax.Precision.HIGHEST, preferred_element_type=F32)


def _dot3(a, b):
    ah = a.astype(BF16)
    al = (a - ah.astype(F32)).astype(BF16)
    bh = b.astype(BF16)
    bl = (b - bh.astype(F32)).astype(BF16)
    d = functools.partial(lax.dot_general, dimension_numbers=NN, preferred_element_type=F32)
    return d(ah, bh) + (d(ah, bl) + d(al, bh))


def _tile(n, target, mult=8):
    best = None
    for t in range(mult, min(n, target) + 1, mult):
        if n % t == 0:
            best = t
    return best if best is not None else n


def _params(*sem):
    return pltpu.CompilerParams(dimension_semantics=sem, vmem_limit_bytes=VMEM_LIMIT)


def _sigmoid(x):
    return 0.5 * jnp.tanh(0.5 * x) + 0.5


def _softplus(x):
    return jnp.maximum(x, 0.0) + jnp.log1p(jnp.exp(-jnp.abs(x)))


def _silu_and_grad(c):
    s = _sigmoid(c)
    return c * s, s * (1.0 + c * (1.0 - s))


_ANY = pl.BlockSpec(memory_space=pl.ANY)


def _matmul(a, b, *, mode, name, out_dtype=F32, add=None, after=None, tm=1376, tn=512, tk=2064):
    if mode == "tn":
        K, M = a.shape
        N = b.shape[1]
    else:
        M, K = a.shape
        N = b.shape[0] if mode == "nt" else b.shape[1]
    tm = _tile(M, tm, 128 if mode == "tn" else 16)
    tn = _tile(N, tn, 128)
    tk = _tile(K, tk, 16 if mode == "tn" else 128)
    gm, gn, gk = M // tm, N // tn, K // tk
    dims = {"nn": NN, "nt": NT, "tn": TN}[mode]

    n_after = 0 if after is None else 1

    def body(*refs):
        refs = refs[n_after:]
        if add is None:
            a_ref, b_ref, o_ref = refs[:3]
            add_ref = None
        else:
            a_ref, b_ref, add_ref, o_ref = refs[:4]
        p = _dot(a_ref[...], b_ref[...], dims)

        def finish(r):
            if add_ref is not None:
                r = r + add_ref[...]
            o_ref[...] = r.astype(out_dtype)

        if gk == 1:
            finish(p)
        else:
            acc_ref = refs[-1]
            k = pl.program_id(2)

            @pl.when(k == 0)
            def _():
                acc_ref[...] = p

            @pl.when(k > 0)
            def _():
                acc_ref[...] += p

            @pl.when(k == gk - 1)
            def _():
                finish(acc_ref[...])

    if mode == "tn":
        a_spec = pl.BlockSpec((tk, tm), lambda i, j, k: (k, i))
    else:
        a_spec = pl.BlockSpec((tm, tk), lambda i, j, k: (i, k))
    if mode == "nt":
        b_spec = pl.BlockSpec((tn, tk), lambda i, j, k: (j, k))
    else:
        b_spec = pl.BlockSpec((tk, tn), lambda i, j, k: (k, j))
    o_spec = pl.BlockSpec((tm, tn), lambda i, j, k: (i, j))
    in_specs = [_ANY] * n_after + [a_spec, b_spec] + ([o_spec] if add is not None else [])
    args = ((after,) if n_after else ()) + (a, b) + ((add,) if add is not None else ())
    return pl.pallas_call(
        body, name=name, grid=(gm, gn, gk), in_specs=in_specs, out_specs=o_spec,
        out_shape=jax.ShapeDtypeStruct((M, N), out_dtype),
        scratch_shapes=[pltpu.VMEM((tm, tn), F32)] if gk > 1 else [],
        compiler_params=_params("parallel", "parallel", "arbitrary"),
    )(*args)


def _rmsnorm_fwd(h, w, *, name, after=None):
    M, D = h.shape
    tm = _tile(M, 688, 16)
    n_after = 0 if after is None else 1

    def body(*refs):
        h_ref, w_ref, n_ref = refs[n_after:]
        x = h_ref[...]
        r = lax.rsqrt(jnp.mean(x * x, axis=-1, keepdims=True) + NORM_EPS)
        n_ref[...] = (x * r * w_ref[...]).astype(n_ref.dtype)

    return pl.pallas_call(
        body, name=name, grid=(M // tm,),
        in_specs=[_ANY] * n_after + [pl.BlockSpec((tm, D), lambda i: (i, 0)), pl.BlockSpec((1, D), lambda i: (0, 0))],
        out_specs=pl.BlockSpec((tm, D), lambda i: (i, 0)),
        out_shape=jax.ShapeDtypeStruct((M, D), BF16),
        compiler_params=_params("parallel"),
    )(*((after,) if n_after else ()), h, w)


def _rmsnorm_bwd(h, w, dn, dres, *, name, also_bf16):
    M, D = h.shape
    tm = _tile(M, 344, 16)
    g = M // tm

    def body(h_ref, w_ref, dn_ref, dres_ref, dh_ref, *rest):
        dhb_ref = rest[0] if also_bf16 else None
        dw_ref, acc_ref = rest[-2:]
        i = pl.program_id(0)
        x = h_ref[...]
        r = lax.rsqrt(jnp.mean(x * x, axis=-1, keepdims=True) + NORM_EPS)
        xhat = x * r
        dn_ = dn_ref[...]
        dxhat = dn_ * w_ref[...]
        dh = dres_ref[...] + r * (dxhat - xhat * jnp.mean(dxhat * xhat, axis=-1, keepdims=True))
        dh_ref[...] = dh
        if also_bf16:
            dhb_ref[...] = dh.astype(dhb_ref.dtype)
        part = jnp.sum((dn_ * xhat).reshape(tm // 8, 8, D), axis=0)

        @pl.when(i == 0)
        def _():
            acc_ref[...] = part

        @pl.when(i > 0)
        def _():
            acc_ref[...] += part

        @pl.when(i == g - 1)
        def _():
            dw_ref[...] = jnp.sum(acc_ref[...], axis=0, keepdims=True)

    row = pl.BlockSpec((tm, D), lambda i: (i, 0))
    vec = pl.BlockSpec((1, D), lambda i: (0, 0))
    return pl.pallas_call(
        body, name=name, grid=(g,), in_specs=[row, vec, row, row],
        out_specs=[row] + ([row] if also_bf16 else []) + [vec],
        out_shape=[jax.ShapeDtypeStruct((M, D), F32)] + ([jax.ShapeDtypeStruct((M, D), BF16)] if also_bf16 else [])
        + [jax.ShapeDtypeStruct((1, D), F32)],
        scratch_shapes=[pltpu.VMEM((8, D), F32)],
        compiler_params=_params("arbitrary"),
    )(h, w, dn, dres)


def _loss_head(h, w, target_p, *, name):
    M, D = h.shape
    tm = _tile(M, 344, 16)
    g = M // tm

    def body(h_ref, w_ref, t_ref, dh_ref, dhb_ref, dw_ref, loss_ref, acc_ref, lacc_ref):
        i = pl.program_id(0)
        x = h_ref[...]
        row = i * tm + lax.broadcasted_iota(jnp.int32, (tm, 1), 0)
        live = row >= HEAD_ROWS
        r = lax.rsqrt(jnp.mean(x * x, axis=-1, keepdims=True) + NORM_EPS)
        xhat = x * r
        err = jnp.where(live, xhat * w_ref[...] - t_ref[...], 0.0)
        dy = err * (1.0 / D)
        dxhat = dy * w_ref[...]
        dh = r * (dxhat - xhat * jnp.mean(dxhat * xhat, axis=-1, keepdims=True))
        dh_ref[...] = dh
        dhb_ref[...] = dh.astype(dhb_ref.dtype)
        part = jnp.sum((dy * xhat).reshape(tm // 8, 8, D), axis=0)
        lpart = jnp.sum((err * err).reshape(tm // 8, 8, D), axis=0)

        @pl.when(i == 0)
        def _():
            acc_ref[...] = part
            lacc_ref[...] = lpart

        @pl.when(i > 0)
        def _():
            acc_ref[...] += part
            lacc_ref[...] += lpart

        @pl.when(i == g - 1)
        def _():
            dw_ref[...] = jnp.sum(acc_ref[...], axis=0, keepdims=True)
            tot = jnp.sum(jnp.sum(lacc_ref[...], axis=0, keepdims=True), axis=1, keepdims=True)
            loss_ref[...] = jnp.broadcast_to(tot * (0.5 / D), (1, 128))

    row = pl.BlockSpec((tm, D), lambda i: (i, 0))
    vec = pl.BlockSpec((1, D), lambda i: (0, 0))
    return pl.pallas_call(
        body, name=name, grid=(g,), in_specs=[row, vec, row],
        out_specs=[row, row, vec, pl.BlockSpec((1, 128), lambda i: (0, 0))],
        out_shape=[jax.ShapeDtypeStruct((M, D), F32), jax.ShapeDtypeStruct((M, D), BF16),
                   jax.ShapeDtypeStruct((1, D), F32), jax.ShapeDtypeStruct((1, 128), F32)],
        scratch_shapes=[pltpu.VMEM((8, D), F32), pltpu.VMEM((8, D), F32)],
        compiler_params=_params("arbitrary"),
    )(h, w, target_p)


def _gate_terms(sm, w2p, b2, alog_p, dt_p, row0):
    tm = sm.shape[0]
    lane = lax.broadcasted_iota(jnp.int32, (tm, SM_W), 1)
    live = (row0 + lax.broadcasted_iota(jnp.int32, (tm, 1), 0)) >= ROW_PAD
    pre = sm + dt_p
    neg_a = -jnp.exp(alog_p)
    g = neg_a * _softplus(pre)
    beta = _sigmoid(sm)
    z = _dot(sm, w2p) + b2
    return lane, live, pre, neg_a, g, beta, z


def _gates_fwd(proj, w2p, b2, alog_p, dt_p, *, name):
    M = proj.shape[0]
    tm = _tile(M, 688, 8)

    def body(sm_ref, w2_ref, b2_ref, al_ref, dt_ref, gb_ref, la_ref):
        row0 = pl.program_id(0) * tm
        lane, live, _, _, g, beta, z = _gate_terms(sm_ref[...], w2_ref[...], b2_ref[...], al_ref[...], dt_ref[...], row0)
        gb = jnp.where(lane < GDN_HEADS, g, jnp.where(lane < 2 * GDN_HEADS, beta, 0.0))
        gb_ref[...] = jnp.where(live, gb, 0.0)
        la = (jnp.minimum(z, 0.0) - jnp.log1p(jnp.exp(-jnp.abs(z)))) * (1.0 / GLA_GATE_NORMALIZER)
        la_ref[...] = jnp.where(live, la, 0.0)

    full = lambda s: pl.BlockSpec(s, lambda i: (0, 0))
    return pl.pallas_call(
        body, name=name, grid=(M // tm,),
        in_specs=[pl.BlockSpec((tm, SM_W), lambda i: (i, C_SM // SM_W)), full((SM_W, GLA_QK)), full((1, GLA_QK)),
                  full((1, SM_W)), full((1, SM_W))],
        out_specs=[pl.BlockSpec((tm, SM_W), lambda i: (i, 0)), pl.BlockSpec((tm, GLA_QK), lambda i: (i, 0))],
        out_shape=[jax.ShapeDtypeStruct((M, SM_W), F32), jax.ShapeDtypeStruct((M, GLA_QK), F32)],
        compiler_params=_params("parallel"),
    )(proj, w2p, b2, alog_p, dt_p)


def _gates_bwd(proj, w2p, b2, alog_p, dt_p, dgb_heads, dla, d_proj, *, name):
    M = proj.shape[0]
    tm = _tile(M, 688, 8)
    g_ = M // tm

    tail_w = D_PROJ - C_SM

    def body(sm_ref, w2_ref, b2_ref, al_ref, dt_ref, dgb_ref, dla_ref, _,
             dsm_ref, dw2_ref, db2_ref, dal_ref, ddt_ref):
        i = pl.program_id(0)
        sm = sm_ref[...]
        lane, live, pre, neg_a, g, beta, z = _gate_terms(sm, w2_ref[...], b2_ref[...], al_ref[...], dt_ref[...], i * tm)
        dz = jnp.where(live, dla_ref[...] * (_sigmoid(-z) * (1.0 / GLA_GATE_NORMALIZER)), 0.0)
        dsm_lr = _dot(dz, w2_ref[...], NT)
        dgb = dgb_ref[0]
        for hh in range(1, GDN_HEADS):
            dgb = dgb + dgb_ref[hh]
        dgb = jnp.where(live, dgb, 0.0)
        da = dgb * neg_a * _sigmoid(pre)
        db = dgb * beta * (1.0 - beta)
        dsm = jnp.where(lane < GDN_HEADS, da, jnp.where(lane < 2 * GDN_HEADS, db, dsm_lr))
        dsm_ref[:, 0:SM_W] = dsm.astype(dsm_ref.dtype)
        dsm_ref[:, SM_W:tail_w] = jnp.zeros((tm, tail_w - SM_W), dsm_ref.dtype)
        is_a = lane < GDN_HEADS
        dal = jnp.sum(jnp.where(is_a, dgb * g, 0.0), axis=0, keepdims=True)
        ddt = jnp.sum(jnp.where(is_a, da, 0.0), axis=0, keepdims=True)
        dw2 = _dot(sm, dz, TN)
        db2 = jnp.sum(dz, axis=0, keepdims=True)

        @pl.when(i == 0)
        def _():
            dw2_ref[...] = dw2
            db2_ref[...] = db2
            dal_ref[...] = dal
            ddt_ref[...] = ddt

        @pl.when(i > 0)
        def _():
            dw2_ref[...] += dw2
            db2_ref[...] += db2
            dal_ref[...] += dal
            ddt_ref[...] += ddt

    full = lambda s: pl.BlockSpec(s, lambda i: (0, 0))
    return pl.pallas_call(
        body, name=name, grid=(g_,),
        in_specs=[pl.BlockSpec((tm, SM_W), lambda i: (i, C_SM // SM_W)), full((SM_W, GLA_QK)), full((1, GLA_QK)),
                  full((1, SM_W)), full((1, SM_W)),
                  pl.BlockSpec((GDN_HEADS, tm, SM_W), lambda i: (0, i, 0)),
                  pl.BlockSpec((tm, GLA_QK), lambda i: (i, 0)), _ANY],
        out_specs=[pl.BlockSpec((tm, tail_w), lambda i: (i, C_SM // tail_w)), full((SM_W, GLA_QK)), full((1, GLA_QK)),
                   full((1, SM_W)), full((1, SM_W))],
        out_shape=[jax.ShapeDtypeStruct(d_proj.shape, d_proj.dtype), jax.ShapeDtypeStruct((SM_W, GLA_QK), F32),
                   jax.ShapeDtypeStruct((1, GLA_QK), F32), jax.ShapeDtypeStruct((1, SM_W), F32),
                   jax.ShapeDtypeStruct((1, SM_W), F32)],
        input_output_aliases={7: 0},
        compiler_params=_params("arbitrary"),
    )(proj, w2p, b2, alog_p, dt_p, dgb_heads, dla, d_proj)


QKV_W = GDN_QK
N_QKV_GROUPS = 3
QKV_B0 = C_QKV // QKV_W
HALO = 8


def _conv_terms(x_ref, halo_ref, cw_ref, xs_ref, i, tm):
    xs_ref[HALO:HALO + tm, :] = x_ref[...]
    xs_ref[0:HALO, :] = jnp.where(i > 0, halo_ref[...], 0.0)
    cw = cw_ref[...]
    xs = xs_ref[...]
    taps = [(pltpu.roll(xs, CONV_K - 1 - t, 0) if t < CONV_K - 1 else xs)[HALO:HALO + tm, :] for t in range(CONV_K)]
    c = taps[0] * cw[0:1, :]
    for t in range(1, CONV_K):
        c = c + taps[t] * cw[t:t + 1, :]
    return c, taps


def _prep_fwd(proj, conv_w8, *, name):
    M = proj.shape[0]
    tm = _tile(M, 344, 8)

    def body(x_ref, halo_ref, cw_ref, o_ref, xs_ref):
        j, i = pl.program_id(0), pl.program_id(1)
        c, _ = _conv_terms(x_ref, halo_ref, cw_ref, xs_ref, i, tm)
        s, _ = _silu_and_grad(c)
        scale = jnp.where(j == 0, GDN_DK ** -0.5, 1.0)
        for hh in range(GDN_HEADS):
            cols = slice(hh * 128, (hh + 1) * 128)
            sh = s[:, cols]
            r = lax.rsqrt(jnp.sum(sh * sh, axis=-1, keepdims=True) + NORM_EPS)
            o_ref[:, cols] = jnp.where(j < 2, sh * (r * scale), sh)

    hb = tm // HALO
    return pl.pallas_call(
        body, name=name, grid=(N_QKV_GROUPS, M // tm),
        in_specs=[pl.BlockSpec((tm, QKV_W), lambda j, i: (i, QKV_B0 + j)),
                  pl.BlockSpec((HALO, QKV_W), lambda j, i: (jnp.maximum(i * hb - 1, 0), QKV_B0 + j)),
                  pl.BlockSpec((8, QKV_W), lambda j, i: (0, j))],
        out_specs=pl.BlockSpec((tm, QKV_W), lambda j, i: (i, j)),
        out_shape=jax.ShapeDtypeStruct((M, N_QKV_GROUPS * QKV_W), F32),
        scratch_shapes=[pltpu.VMEM((tm + HALO, QKV_W), F32)],
        compiler_params=_params("parallel", "arbitrary"),
    )(proj, proj, conv_w8)


def _prep_bwd_a(proj, conv_w8, dact, *, name):
    M = proj.shape[0]
    tm = _tile(M, 344, 8)
    g_ = M // tm

    def body(x_ref, halo_ref, cw_ref, da_ref, dc_ref, dcw_ref, xs_ref):
        j, i = pl.program_id(0), pl.program_id(1)
        c, taps = _conv_terms(x_ref, halo_ref, cw_ref, xs_ref, i, tm)
        s, ds_dc = _silu_and_grad(c)
        scale = jnp.where(j == 0, GDN_DK ** -0.5, 1.0)
        for hh in range(GDN_HEADS):
            cols = slice(hh * 128, (hh + 1) * 128)
            sh = s[:, cols]
            r = lax.rsqrt(jnp.sum(sh * sh, axis=-1, keepdims=True) + NORM_EPS)
            da = da_ref[:, cols]
            y = sh * r
            dy = da * scale
            ds_norm = r * (dy - y * jnp.sum(dy * y, axis=-1, keepdims=True))
            dc_ref[:, cols] = jnp.where(j < 2, ds_norm, da) * ds_dc[:, cols]
        dc = dc_ref[...]
        r8 = lax.broadcasted_iota(jnp.int32, (8, QKV_W), 0)
        part = jnp.zeros((8, QKV_W), F32)
        for t in range(CONV_K):
            part = jnp.where(r8 == t, jnp.sum(dc * taps[t], axis=0, keepdims=True), part)

        @pl.when(i == 0)
        def _():
            dcw_ref[...] = part

        @pl.when(i > 0)
        def _():
            dcw_ref[...] += part

    hb = tm // HALO
    blk = pl.BlockSpec((tm, QKV_W), lambda j, i: (i, j))
    return pl.pallas_call(
        body, name=name, grid=(N_QKV_GROUPS, g_),
        in_specs=[pl.BlockSpec((tm, QKV_W), lambda j, i: (i, QKV_B0 + j)),
                  pl.BlockSpec((HALO, QKV_W), lambda j, i: (jnp.maximum(i * hb - 1, 0), QKV_B0 + j)),
                  pl.BlockSpec((8, QKV_W), lambda j, i: (0, j)), blk],
        out_specs=[blk, pl.BlockSpec((8, QKV_W), lambda j, i: (0, j))],
        out_shape=[jax.ShapeDtypeStruct((M, N_QKV_GROUPS * QKV_W), F32),
                   jax.ShapeDtypeStruct((8, N_QKV_GROUPS * QKV_W), F32)],
        scratch_shapes=[pltpu.VMEM((tm + HALO, QKV_W), F32)],
        compiler_params=_params("parallel", "arbitrary"),
    )(proj, proj, conv_w8, dact)


def _prep_bwd_b(dc, conv_w8, d_proj, *, name):
    M = dc.shape[0]
    tm = _tile(M, 344, 16)
    g_ = M // tm

    def body(d_ref, halo_ref, cw_ref, _, o_ref, ds_ref):
        i = pl.program_id(1)
        ds_ref[0:tm, :] = d_ref[...]
        ds_ref[tm:tm + HALO, :] = jnp.where(i < g_ - 1, halo_ref[...], 0.0)
        cw = cw_ref[...]
        ds = ds_ref[...]
        acc = ds[0:tm, :] * cw[CONV_K - 1:CONV_K, :]
        for t in range(CONV_K - 1):
            acc = acc + pltpu.roll(ds, tm + HALO - (CONV_K - 1 - t), 0)[0:tm, :] * cw[t:t + 1, :]
        o_ref[...] = acc.astype(o_ref.dtype)

    hb = tm // HALO
    last = M // HALO - 1
    blk = pl.BlockSpec((tm, QKV_W), lambda j, i: (i, j))
    return pl.pallas_call(
        body, name=name, grid=(N_QKV_GROUPS, g_),
        in_specs=[blk, pl.BlockSpec((HALO, QKV_W), lambda j, i: (jnp.minimum((i + 1) * hb, last), j)),
                  pl.BlockSpec((8, QKV_W), lambda j, i: (0, j)), _ANY],
        out_specs=pl.BlockSpec((tm, QKV_W), lambda j, i: (i, QKV_B0 + j)),
        out_shape=jax.ShapeDtypeStruct(d_proj.shape, d_proj.dtype), input_output_aliases={3: 0},
        scratch_shapes=[pltpu.VMEM((tm + HALO, QKV_W), F32)],
        compiler_params=_params("parallel", "arbitrary"),
    )(dc, dc, conv_w8, d_proj)


def _round_robin(gens):
    gens = list(gens)
    while gens:
        alive = []
        for gen in gens:
            try:
                next(gen)
                alive.append(gen)
            except StopIteration:
                pass
        gens = alive


def _unit_lower_inverse(a_low, eye):
    b = -a_low
    x = eye + b
    pw = b
    for _ in range(5):
        pw = _dot3(pw, pw)
        yield
        x = x + _dot3(x, pw)
        yield
    return x


class _GdnChunk:
    def build(self, q, k, v, gb, h):
        C = GDN_CHUNK
        lane = lax.broadcasted_iota(jnp.int32, (C, SM_W), 1)
        g = jnp.sum(jnp.where(lane == h, gb, 0.0), axis=1, keepdims=True)
        self.beta = jnp.sum(jnp.where(lane == h + GDN_HEADS, gb, 0.0), axis=1, keepdims=True)
        ri = lax.broadcasted_iota(jnp.int32, (C, C), 0)
        ci = lax.broadcasted_iota(jnp.int32, (C, C), 1)
        self.causal = ri >= ci
        self.strict = ri > ci
        self.eye = (ri == ci).astype(F32)
        gcb = _dotx(self.causal.astype(F32), jnp.broadcast_to(g, (C, SM_W)))
        yield
        self.gcol = gcb[:, 0:1]
        grow = gcb.T[0:1, 0:C]
        self.decay = jnp.exp(jnp.where(self.causal, self.gcol - grow, -1e30))
        self.egc = jnp.exp(self.gcol)
        glast = gcb[C - 1:C, 0:1]
        self.elast = jnp.exp(glast - self.gcol)
        self.gl = jnp.exp(glast)
        self.q, self.k, self.v = q, k, v
        self.kb = k * self.beta
        m = _dot(self.kb, k, NT)
        n_ = _dot(q, k, NT)
        yield
        self.a_low = jnp.where(self.strict, m * self.decay, 0.0)
        self.p = n_ * self.decay
        self.qd = q * self.egc
        self.kd = k * self.elast
        self.bu = v * self.beta
        self.bw = self.kb * self.egc


GDN_HB = 8
GDN_HG = GDN_HEADS // GDN_HB


def _gdn_specs(n_of):
    C, W = GDN_CHUNK, 128 * GDN_HB
    q_spec = pl.BlockSpec((C, W), lambda g, n: (n_of(n), g))
    k_spec = pl.BlockSpec((C, W), lambda g, n: (n_of(n), g + GDN_HG))
    v_spec = pl.BlockSpec((C, W), lambda g, n: (n_of(n), g + 2 * GDN_HG))
    gb_spec = pl.BlockSpec((C, SM_W), lambda g, n: (n_of(n), 0))
    o_spec = pl.BlockSpec((C, W), lambda g, n: (n_of(n), g))
    s_spec = pl.BlockSpec((GDN_HB, None, GDN_DK, GDN_DV), lambda g, n: (g, n_of(n), 0, 0))
    t_spec = pl.BlockSpec((GDN_HB, None, C, C), lambda g, n: (g, n_of(n), 0, 0))
    return q_spec, k_spec, v_spec, gb_spec, o_spec, s_spec, t_spec


def _gdn_fwd(act, gb, *, name):
    M = act.shape[0]
    N = M // GDN_CHUNK

    def body(q_ref, k_ref, v_ref, gb_ref, o_ref, s_ref, t_ref, state):
        g, n = pl.program_id(0), pl.program_id(1)

        @pl.when(n == 0)
        def _():
            state[...] = jnp.zeros_like(state)

        gb_ = gb_ref[...]

        def head(hh):
            cols = slice(hh * 128, (hh + 1) * 128)
            c = _GdnChunk()
            yield from c.build(q_ref[:, cols], k_ref[:, cols], v_ref[:, cols], gb_, g * GDN_HB + hh)
            tinv = yield from _unit_lower_inverse(c.a_low, c.eye)
            s = state[hh]
            s_ref[hh] = s
            t_ref[hh] = tinv
            u = _dot(tinv, c.bu)
            w = _dot(tinv, c.bw)
            yield
            vn = u - _dot(w, s)
            o1 = _dot(c.qd, s)
            yield
            o_ref[:, cols] = o1 + _dot(c.p, vn)
            state[hh] = c.gl * s + _dot(c.kd, vn, TN)

        _round_robin(head(hh) for hh in range(GDN_HB))

    q_spec, k_spec, v_spec, gb_spec, o_spec, s_spec, t_spec = _gdn_specs(lambda n: n)
    return pl.pallas_call(
        body, name=name, grid=(GDN_HG, N),
        in_specs=[q_spec, k_spec, v_spec, gb_spec], out_specs=[o_spec, s_spec, t_spec],
        out_shape=[jax.ShapeDtypeStruct((M, GDN_V), F32),
                   jax.ShapeDtypeStruct((GDN_HEADS, N, GDN_DK, GDN_DV), F32),
                   jax.ShapeDtypeStruct((GDN_HEADS, N, GDN_CHUNK, GDN_CHUNK), F32)],
        scratch_shapes=[pltpu.VMEM((GDN_HB, GDN_DK, GDN_DV), F32)],
        compiler_params=_params("parallel", "arbitrary"),
    )(act, act, act, gb)


def _gdn_bwd(act, gb, do, s_all, t_all, *, name):
    M = act.shape[0]
    N = M // GDN_CHUNK
    C = GDN_CHUNK
    assert GDN_HG == 1

    def body(q_ref, k_ref, v_ref, gb_ref, do_ref, s_ref, t_ref, dact_ref, dgb_ref, dstate):
        g, n = pl.program_id(0), pl.program_id(1)

        @pl.when(n == 0)
        def _():
            dstate[...] = jnp.zeros_like(dstate)

        gb_ = gb_ref[...]
        last = lax.broadcasted_iota(jnp.int32, (C, 1), 0) == C - 1
        upper = (lax.broadcasted_iota(jnp.int32, (C, C), 0) <= lax.broadcasted_iota(jnp.int32, (C, C), 1)).astype(F32)
        lane = lax.broadcasted_iota(jnp.int32, (C, SM_W), 1)
        def head(hh):
            cols = slice(hh * 128, (hh + 1) * 128)
            h = g * GDN_HB + hh
            c = _GdnChunk()
            yield from c.build(q_ref[:, cols], k_ref[:, cols], v_ref[:, cols], gb_, h)
            tinv = t_ref[hh]
            s = s_ref[hh]
            do_ = do_ref[:, cols]
            ds1 = dstate[hh]
            u = _dot(tinv, c.bu)
            w = _dot(tinv, c.bw)
            dqd = _dot(do_, s, NT)
            dvn0 = _dot(c.p, do_, TN) + _dot(c.kd, ds1)
            dst0 = _dot(c.qd, do_, TN) + c.gl * ds1
            yield
            vn = u - _dot(w, s)
            dvn = dvn0
            yield
            dp = jnp.where(c.causal, _dot(do_, vn, NT), 0.0)
            dstate[hh] = dst0 - _dot(w, dvn, TN)
            dkd = _dot(vn, ds1, NT)
            dw = -_dot(dvn, s, NT)
            dbu = _dot(tinv, dvn, TN)
            dgl = jnp.sum(jnp.sum(s * ds1, axis=1, keepdims=True), axis=0, keepdims=True)
            yield
            dbw = _dot(tinv, dw, TN)
            t1 = _dot(dbu, u, NT)
            yield
            da = jnp.where(c.strict, -(t1 + _dot(dbw, w, NT)), 0.0)
            dn_ = dp * c.decay
            dq0 = _dot(dn_, c.k)
            dk0 = _dot(dn_, c.q, TN)
            yield
            dm = da * c.decay
            e = da * c.a_low + dp * c.p
            dkb = _dot(dm, c.k) + dbw * c.egc
            dact_ref[:, GDN_QK + hh * 128:GDN_QK + (hh + 1) * 128] = (
                _dot(dm, c.kb, TN) + dk0 + dkb * c.beta + dkd * c.elast)
            dact_ref[:, cols] = dq0 + dqd * c.egc
            dact_ref[:, 2 * GDN_QK + hh * 128:2 * GDN_QK + (hh + 1) * 128] = dbu * c.beta
            dbeta = jnp.sum(dbu * c.v, axis=1, keepdims=True) + jnp.sum(dkb * c.k, axis=1, keepdims=True)
            t_kd = jnp.sum(dkd * c.kd, axis=1, keepdims=True)
            dgc = (jnp.sum(e, axis=1, keepdims=True) - jnp.sum(e.T, axis=1, keepdims=True)
                   + jnp.sum(dbw * c.bw, axis=1, keepdims=True) + jnp.sum(dqd * c.qd, axis=1, keepdims=True) - t_kd)
            dgc = dgc + jnp.where(last, jnp.sum(t_kd, axis=0, keepdims=True) + dgl * c.gl, 0.0)
            yield
            dg = _dotx(upper, jnp.broadcast_to(dgc, (C, SM_W)))
            dgb_ref[hh] = jnp.where(lane == h, dg, jnp.where(lane == h + GDN_HEADS, dbeta, 0.0))

        _round_robin(head(hh) for hh in range(GDN_HB))

    rev = lambda n: N - 1 - n
    q_spec, k_spec, v_spec, gb_spec, o_spec, s_spec, t_spec = _gdn_specs(rev)
    dgb_spec = pl.BlockSpec((GDN_HB, C, SM_W), lambda g, n: (g, rev(n), 0))
    return pl.pallas_call(
        body, name=name, grid=(GDN_HG, N),
        in_specs=[q_spec, k_spec, v_spec, gb_spec, o_spec, s_spec, t_spec],
        out_specs=[pl.BlockSpec((C, 2 * GDN_QK + GDN_V), lambda g, n: (rev(n), 0)), dgb_spec],
        out_shape=[jax.ShapeDtypeStruct((M, 2 * GDN_QK + GDN_V), F32),
                   jax.ShapeDtypeStruct((GDN_HEADS, M, SM_W), F32)],
        scratch_shapes=[pltpu.VMEM((GDN_HB, GDN_DK, GDN_DV), F32)],
        compiler_params=_params("parallel", "arbitrary"),
    )(act, act, act, gb, do, s_all, t_all)


GLA_STEP_ROWS = 64
GLA_SUB = GLA_STEP_ROWS // GLA_CHUNK


def _gla_cumsum(la):
    C = GLA_CHUNK
    ltri = (lax.broadcasted_iota(jnp.int32, (C, C), 0) >= lax.broadcasted_iota(jnp.int32, (C, C), 1)).astype(F32)
    return _dotx(ltri, la)


def _gla_decay_rows(b, i):
    rj = lax.broadcasted_iota(jnp.int32, (GLA_CHUNK, GLA_DK), 0)
    return jnp.where(rj <= i, jnp.exp(jnp.minimum(b[i:i + 1, :] - b, 0.0)), 0.0)


def _gla_scores_t(q, k, b):
    C = GLA_CHUNK
    lane = lax.broadcasted_iota(jnp.int32, (C, C), 1)
    st = jnp.zeros((C, C), F32)
    for i in range(C):
        si = jnp.sum(q[i:i + 1, :] * k * _gla_decay_rows(b, i), axis=1, keepdims=True)
        st = jnp.where(lane == i, si, st)
        if i % 4 == 3:
            yield
    return st


def _gla_specs(n_of):
    R = GLA_STEP_ROWS
    q_spec = pl.BlockSpec((R, GLA_QK), lambda n: (n_of(n), C_GQ // GLA_QK))
    k_spec = pl.BlockSpec((R, GLA_QK), lambda n: (n_of(n), C_GK // GLA_QK))
    v_spec = pl.BlockSpec((R, GLA_V), lambda n: (n_of(n), C_GV // GLA_V))
    la_spec = pl.BlockSpec((R, GLA_QK), lambda n: (n_of(n), 0))
    o_spec = pl.BlockSpec((R, GLA_V), lambda n: (n_of(n), 0))
    s_spec = pl.BlockSpec((GLA_HEADS, None, GLA_SUB, GLA_DV, GLA_DK), lambda n: (0, n_of(n), 0, 0, 0))
    return q_spec, k_spec, v_spec, la_spec, o_spec, s_spec


def _gla_fwd(proj, la, *, name):
    M = proj.shape[0]
    N = M // GLA_STEP_ROWS
    C = GLA_CHUNK

    def body(q_ref, k_ref, v_ref, la_ref, o_ref, s_ref, state):
        n = pl.program_id(0)

        @pl.when(n == 0)
        def _():
            state[...] = jnp.zeros_like(state)

        def head(hh):
            kc = slice(hh * GLA_DK, (hh + 1) * GLA_DK)
            vc = slice(hh * GLA_DV, (hh + 1) * GLA_DV)
            st = state[hh]
            for c in range(GLA_SUB):
                rows = slice(c * C, (c + 1) * C)
                q = q_ref[rows, kc] * (GLA_DK ** -0.5)
                k = k_ref[rows, kc]
                v = v_ref[rows, vc]
                b = _gla_cumsum(la_ref[rows, kc])
                yield
                s_ref[hh, c] = st
                blast = b[C - 1:C, :]
                sc_t = yield from _gla_scores_t(q, k, b)
                o1 = _dot(q * jnp.exp(b), st, NT)
                kv = _dot(v, k * jnp.exp(blast - b), TN)
                o2 = _dot(sc_t, v, TN)
                yield
                o_ref[rows, vc] = o1 + o2
                st = st * jnp.exp(blast) + kv
            state[hh] = st

        _round_robin(head(hh) for hh in range(GLA_HEADS))

    q_spec, k_spec, v_spec, la_spec, o_spec, s_spec = _gla_specs(lambda n: n)
    return pl.pallas_call(
        body, name=name, grid=(N,),
        in_specs=[q_spec, k_spec, v_spec, la_spec], out_specs=[o_spec, s_spec],
        out_shape=[jax.ShapeDtypeStruct((M, GLA_V), F32),
                   jax.ShapeDtypeStruct((GLA_HEADS, N, GLA_SUB, GLA_DV, GLA_DK), F32)],
        scratch_shapes=[pltpu.VMEM((GLA_HEADS, GLA_DV, GLA_DK), F32)],
        compiler_params=_params("arbitrary"),
    )(proj, proj, proj, la)


def _gla_bwd(proj, la, do, s_all, d_proj, *, name):
    M = proj.shape[0]
    N = M // GLA_STEP_ROWS
    C = GLA_CHUNK
    qkv_w = 2 * GLA_QK + GLA_V
    assert C_GK == C_GQ + GLA_QK and C_GV == C_GK + GLA_QK and C_GQ % qkv_w == 0

    def body(q_ref, k_ref, v_ref, la_ref, do_ref, s_ref, _, dp_ref, dla_ref, dstate):
        n = pl.program_id(0)

        @pl.when(n == 0)
        def _():
            dstate[...] = jnp.zeros_like(dstate)

        lane = lax.broadcasted_iota(jnp.int32, (C, C), 1)
        ri = lax.broadcasted_iota(jnp.int32, (C, GLA_DK), 0)
        upper = (lax.broadcasted_iota(jnp.int32, (C, C), 0) <= lane).astype(F32)
        def head(hh):
            kc = slice(hh * GLA_DK, (hh + 1) * GLA_DK)
            vc = slice(hh * GLA_DV, (hh + 1) * GLA_DV)
            ds1 = dstate[hh]
            for c in reversed(range(GLA_SUB)):
                rows = slice(c * C, (c + 1) * C)
                q = q_ref[rows, kc] * (GLA_DK ** -0.5)
                k = k_ref[rows, kc]
                v = v_ref[rows, vc]
                b = _gla_cumsum(la_ref[rows, kc])
                do_ = do_ref[rows, vc]
                st = s_ref[hh, c]
                dsc_t = _dot(v, do_, NT)
                dqe = _dot(do_, st)
                dke = _dot(v, ds1)
                yield
                blast = b[C - 1:C, :]
                eb = jnp.exp(b)
                elast = jnp.exp(blast - b)
                eblast = jnp.exp(blast)
                qe = q * eb
                ke = k * elast
                dv2 = _dot(ke, ds1, NT)
                ds_new = _dot(do_, qe, TN)
                deblast = jnp.sum(st * ds1, axis=0, keepdims=True)
                sc_t = jnp.zeros((C, C), F32)
                dq_sc = jnp.zeros((C, GLA_DK), F32)
                dk_sc = jnp.zeros((C, GLA_DK), F32)
                for i in range(C):
                    f = _gla_decay_rows(b, i)
                    kf = k * f
                    si = jnp.sum(q[i:i + 1, :] * kf, axis=1, keepdims=True)
                    sc_t = jnp.where(lane == i, si, sc_t)
                    dsi = jnp.sum(jnp.where(lane == i, dsc_t, 0.0), axis=1, keepdims=True)
                    dq_sc = jnp.where(ri == i, jnp.sum(dsi * kf, axis=0, keepdims=True), dq_sc)
                    dk_sc = dk_sc + (dsi * f) * q[i:i + 1, :]
                    if i % 4 == 3:
                        yield
                dv1 = _dot(sc_t, do_)
                dp_ref[rows, kc] = ((dq_sc + dqe * eb) * (GLA_DK ** -0.5)).astype(dp_ref.dtype)
                dp_ref[rows, GLA_QK + hh * GLA_DK:GLA_QK + (hh + 1) * GLA_DK] = (dk_sc + dke * elast).astype(dp_ref.dtype)
                t_ke = dke * ke
                db = q * dq_sc - k * dk_sc + dqe * qe - t_ke
                db = db + jnp.where(ri == C - 1, jnp.sum(t_ke, axis=0, keepdims=True) + deblast * eblast, 0.0)
                dla = _dotx(upper, db)
                yield
                dp_ref[rows, 2 * GLA_QK + hh * GLA_DV:2 * GLA_QK + (hh + 1) * GLA_DV] = (dv1 + dv2).astype(dp_ref.dtype)
                dla_ref[rows, kc] = dla
                ds1 = ds1 * eblast + ds_new
            dstate[hh] = ds1

        _round_robin(head(hh) for hh in range(GLA_HEADS))

    rev = lambda n: N - 1 - n
    q_spec, k_spec, v_spec, la_spec, o_spec, s_spec = _gla_specs(rev)
    return pl.pallas_call(
        body, name=name, grid=(N,),
        in_specs=[q_spec, k_spec, v_spec, la_spec, o_spec, s_spec, _ANY],
        out_specs=[pl.BlockSpec((GLA_STEP_ROWS, qkv_w), lambda n: (rev(n), C_GQ // qkv_w)), la_spec],
        out_shape=[jax.ShapeDtypeStruct(d_proj.shape, d_proj.dtype), jax.ShapeDtypeStruct((M, GLA_QK), F32)],
        input_output_aliases={6: 0},
        scratch_shapes=[pltpu.VMEM((GLA_HEADS, GLA_DV, GLA_DK), F32)],
        compiler_params=_params("arbitrary"),
    )(proj, proj, proj, la, do, s_all, d_proj)


def _head_norm(o, wn):
    r = lax.rsqrt(jnp.mean(o * o, axis=-1, keepdims=True) + NORM_EPS)
    return o * r, r


def _mix_heads():
    heads = [(0, GDN_DV, hh * GDN_DV, hh * GDN_DV) for hh in range(GDN_HEADS)]
    heads += [(1, GLA_DV, GDN_V + hh * GLA_DV, hh * GLA_DV) for hh in range(GLA_HEADS)]
    return heads


def _mix_fwd(o_gdn, o_gla, proj, wn_gdn, wn_gla, *, name):
    M = proj.shape[0]
    tm = _tile(M, 344, 16)

    def body(og_ref, ol_ref, z_ref, r_ref, wg_ref, wl_ref, m_ref):
        srcs = ((og_ref, z_ref, wg_ref), (ol_ref, r_ref, wl_ref))
        for grp, width, mcol, col in _mix_heads():
            o_ref, gate_ref, w_ref = srcs[grp]
            xhat, _ = _head_norm(o_ref[:, col:col + width], None)
            gate, _ = _silu_and_grad(gate_ref[:, col:col + width])
            m_ref[:, mcol:mcol + width] = (xhat * w_ref[...] * gate).astype(m_ref.dtype)

    full = lambda s: pl.BlockSpec(s, lambda i: (0, 0))
    return pl.pallas_call(
        body, name=name, grid=(M // tm,),
        in_specs=[pl.BlockSpec((tm, GDN_V), lambda i: (i, 0)), pl.BlockSpec((tm, GLA_V), lambda i: (i, 0)),
                  pl.BlockSpec((tm, GDN_V), lambda i: (i, C_Z // GDN_V)),
                  pl.BlockSpec((tm, GLA_V), lambda i: (i, C_GR // GLA_V)),
                  full((1, GDN_DV)), full((1, GLA_DV))],
        out_specs=pl.BlockSpec((tm, D_MODEL), lambda i: (i, 0)),
        out_shape=jax.ShapeDtypeStruct((M, D_MODEL), BF16),
        compiler_params=_params("parallel"),
    )(o_gdn, o_gla, proj, proj, wn_gdn, wn_gla)


def _mix_bwd(o_gdn, o_gla, proj, wn_gdn, wn_gla, dmixed, *, name):
    M = proj.shape[0]
    tm = _tile(M, 344, 16)
    g_ = M // tm
    assert C_Z == 0 and C_GR == GDN_V

    def body(og_ref, ol_ref, z_ref, r_ref, wg_ref, wl_ref, dm_ref,
             dog_ref, dol_ref, dzr_ref, dwg_ref, dwl_ref):
        i = pl.program_id(0)
        srcs = ((og_ref, z_ref, wg_ref, dog_ref), (ol_ref, r_ref, wl_ref, dol_ref))
        dws = [jnp.zeros((1, GDN_DV), F32), jnp.zeros((1, GLA_DV), F32)]
        for grp, width, mcol, col in _mix_heads():
            o_ref, gate_ref, w_ref, do_ref = srcs[grp]
            cols = slice(col, col + width)
            xhat, r = _head_norm(o_ref[:, cols], None)
            gate, dgate_dc = _silu_and_grad(gate_ref[:, cols])
            dm = dm_ref[:, mcol:mcol + width]
            dzr_ref[:, mcol:mcol + width] = (dm * xhat * w_ref[...] * dgate_dc).astype(dzr_ref.dtype)
            dnorm = dm * gate
            dws[grp] = dws[grp] + jnp.sum(dnorm * xhat, axis=0, keepdims=True)
            dxhat = dnorm * w_ref[...]
            do_ref[:, cols] = r * (dxhat - xhat * jnp.mean(dxhat * xhat, axis=-1, keepdims=True))

        @pl.when(i == 0)
        def _():
            dwg_ref[...] = dws[0]
            dwl_ref[...] = dws[1]

        @pl.when(i > 0)
        def _():
            dwg_ref[...] += dws[0]
            dwl_ref[...] += dws[1]

    full = lambda s: pl.BlockSpec(s, lambda i: (0, 0))
    half = pl.BlockSpec((tm, GDN_V), lambda i: (i, 0))
    return pl.pallas_call(
        body, name=name, grid=(g_,),
        in_specs=[half, half, pl.BlockSpec((tm, GDN_V), lambda i: (i, C_Z // GDN_V)),
                  pl.BlockSpec((tm, GLA_V), lambda i: (i, C_GR // GLA_V)),
                  full((1, GDN_DV)), full((1, GLA_DV)), pl.BlockSpec((tm, D_MODEL), lambda i: (i, 0))],
        out_specs=[half, half, pl.BlockSpec((tm, GDN_V + GLA_V), lambda i: (i, 0)),
                   full((1, GDN_DV)), full((1, GLA_DV))],
        out_shape=[jax.ShapeDtypeStruct((M, GDN_V), F32), jax.ShapeDtypeStruct((M, GLA_V), F32),
                   jax.ShapeDtypeStruct((M, D_PROJ), BF16),
                   jax.ShapeDtypeStruct((1, GDN_DV), F32), jax.ShapeDtypeStruct((1, GLA_DV), F32)],
        compiler_params=_params("arbitrary"),
    )(o_gdn, o_gla, proj, proj, wn_gdn, wn_gla, dmixed)


def _swiglu_fwd(n, w_gate_t, w_up_t, *, name, tm=1376, tn=512):
    M, D = n.shape
    F = w_gate_t.shape[0]
    tm, tn = _tile(M, tm, 16), _tile(F, tn, 128)

    def body(n_ref, wg_ref, wu_ref, g_ref, u_ref, a_ref):
        x = n_ref[...]
        g = _dot(x, wg_ref[...], NT)
        u = _dot(x, wu_ref[...], NT)
        s, _ = _silu_and_grad(g)
        g_ref[...] = g.astype(g_ref.dtype)
        u_ref[...] = u.astype(u_ref.dtype)
        a_ref[...] = (s * u).astype(a_ref.dtype)

    w_spec = pl.BlockSpec((tn, D), lambda i, j: (j, 0))
    o_spec = pl.BlockSpec((tm, tn), lambda i, j: (i, j))
    return pl.pallas_call(
        body, name=name, grid=(M // tm, F // tn),
        in_specs=[pl.BlockSpec((tm, D), lambda i, j: (i, 0)), w_spec, w_spec], out_specs=[o_spec] * 3,
        out_shape=[jax.ShapeDtypeStruct((M, F), BF16)] * 3, compiler_params=_params("parallel", "parallel"),
    )(n, w_gate_t, w_up_t)


def _swiglu_bwd(dh, w_down, gate, up, *, name, after=None, tm=1376, tn=512):
    M, D = dh.shape
    F = w_down.shape[0]
    tm, tn = _tile(M, tm, 16), _tile(F, tn, 128)
    n_after = 0 if after is None else 1

    def body(*refs):
        dh_ref, w_ref, g_ref, u_ref, dg_ref, du_ref = refs[n_after:]
        da = _dot(dh_ref[...], w_ref[...], NT)
        s, ds = _silu_and_grad(g_ref[...].astype(F32))
        dg_ref[...] = (da * u_ref[...].astype(F32) * ds).astype(dg_ref.dtype)
        du_ref[...] = (da * s).astype(du_ref.dtype)

    o_spec = pl.BlockSpec((tm, tn), lambda i, j: (i, j))
    return pl.pallas_call(
        body, name=name, grid=(M // tm, F // tn),
        in_specs=[_ANY] * n_after + [pl.BlockSpec((tm, D), lambda i, j: (i, 0)),
                                     pl.BlockSpec((tn, D), lambda i, j: (j, 0)), o_spec, o_spec],
        out_specs=[o_spec, o_spec], out_shape=[jax.ShapeDtypeStruct((M, F), BF16)] * 2,
        compiler_params=_params("parallel", "parallel"),
    )(*((after,) if n_after else ()), dh, w_down, gate, up)


def _adamw(w, g, m, v, *, name):
    shape = w.shape
    cols = shape[-1]
    rows = w.size // cols
    w2, g2, m2, v2 = (t.reshape(rows, cols) for t in (w, g, m, v))
    if rows % 8 == 0 or cols % 128 != 0:
        tr, tc = (_tile(rows, 256, 8) if rows % 8 == 0 else rows), cols
    else:
        tr, tc = rows, _tile(cols, 256, 128)

    def body(w_ref, g_ref, m_ref, v_ref, d_ref, nm_ref, nv_ref):
        g_ = g_ref[...]
        nm = ADAM_B1 * m_ref[...] + (1.0 - ADAM_B1) * g_
        nv = ADAM_B2 * v_ref[...] + (1.0 - ADAM_B2) * (g_ * g_)
        m_hat = nm / (1.0 - ADAM_B1 ** ADAM_STEP)
        v_hat = nv / (1.0 - ADAM_B2 ** ADAM_STEP)
        d_ref[...] = -ADAM_LR * (m_hat / (jnp.sqrt(v_hat) + ADAM_EPS) + ADAM_WD * w_ref[...])
        nm_ref[...] = nm
        nv_ref[...] = nv

    blk = pl.BlockSpec((tr, tc), lambda i, j: (i, j))
    outs = pl.pallas_call(
        body, name=name, grid=(rows // tr, cols // tc), in_specs=[blk] * 4, out_specs=[blk] * 3,
        out_shape=[jax.ShapeDtypeStruct((rows, cols), F32)] * 3, compiler_params=_params("parallel", "parallel"),
    )(w2, g2, m2, v2)
    return tuple(t.reshape(shape) for t in outs)


def _sum_slabs(x, *, name):
    _, R, C = x.shape
    sub = 16 if x.dtype == BF16 else 8
    if R % sub == 0:
        tr, tc = _tile(R, 128, sub), C
    else:
        tr, tc = R, _tile(C, 256, 128)

    def body(x_ref, o_ref):
        acc = x_ref[0].astype(F32)
        for s in range(1, N_DEV):
            acc = acc + x_ref[s].astype(F32)
        o_ref[...] = acc

    return pl.pallas_call(
        body, name=name, grid=(R // tr, C // tc),
        in_specs=[pl.BlockSpec((N_DEV, tr, tc), lambda i, j: (0, i, j))],
        out_specs=pl.BlockSpec((tr, tc), lambda i, j: (i, j)),
        out_shape=jax.ShapeDtypeStruct((R, C), F32), compiler_params=_params("parallel", "parallel"),
    )(x)


def _peers():
    x, y, c = lax.axis_index("x"), lax.axis_index("y"), lax.axis_index("c")
    me = 4 * x + 2 * y + c
    peers = []
    for k in range(1, N_DEV):
        px = 1 - x if k & 4 else x
        py = 1 - y if k & 2 else y
        pc = 1 - c if k & 1 else c
        peers.append(((px, py, pc), 4 * px + 2 * py + pc))
    return me, peers


def _exchange(x, *, gather, name):
    slab = x.shape if gather else x.shape[1:]

    def body(x_ref, o_ref, send_sems, recv_sems, own_sem):
        me, peers = _peers()
        own = pltpu.make_async_copy(x_ref if gather else x_ref.at[me], o_ref.at[me], own_sem)
        own.start()
        sends, recvs = [], []
        for k, (pos, idx) in enumerate(peers):
            sends.append(pltpu.make_async_remote_copy(
                src_ref=x_ref if gather else x_ref.at[idx], dst_ref=o_ref.at[me],
                send_sem=send_sems.at[k], recv_sem=recv_sems.at[k],
                device_id=pos, device_id_type=pl.DeviceIdType.MESH))
            recvs.append(pltpu.make_async_remote_copy(
                src_ref=x_ref if gather else x_ref.at[idx], dst_ref=o_ref.at[idx],
                send_sem=send_sems.at[k], recv_sem=recv_sems.at[k],
                device_id=pos, device_id_type=pl.DeviceIdType.MESH))
        for cp in sends:
            cp.start()
        for cp in recvs:
            cp.wait_recv()
        for cp in sends:
            cp.wait_send()
        own.wait()

    hbm = pl.BlockSpec(memory_space=pltpu.HBM)
    return pl.pallas_call(
        body, name=name, in_specs=[hbm], out_specs=hbm,
        out_shape=jax.ShapeDtypeStruct((N_DEV,) + tuple(slab), x.dtype),
        scratch_shapes=[pltpu.SemaphoreType.DMA((N_DEV - 1,)), pltpu.SemaphoreType.DMA((N_DEV - 1,)),
                        pltpu.SemaphoreType.DMA],
    )(x)


_HBM = pl.BlockSpec(memory_space=pltpu.HBM)
_SEM = pl.BlockSpec(memory_space=pltpu.SEMAPHORE)
_EFFECT = pltpu.SideEffectType.DATAFLOW_SIDE_EFFECTING


def _exchange_start(x, *, gather, name, after=None):
    slab = x.shape if gather else x.shape[1:]
    n_after = 0 if after is None else 1

    def body(*refs):
        x_ref, land_ref, send_sems, recv_sems, _, _, token = refs[n_after:]
        me, peers = _peers()
        for k, (pos, idx) in enumerate(peers):
            pltpu.make_async_remote_copy(
                src_ref=x_ref if gather else x_ref.at[idx], dst_ref=land_ref.at[me],
                send_sem=send_sems.at[k], recv_sem=recv_sems.at[k],
                device_id=pos, device_id_type=pl.DeviceIdType.MESH).start()
        token[...] = jnp.zeros_like(token)

    land = lax.empty((N_DEV,) + tuple(slab), x.dtype)
    return pl.pallas_call(
        body, name=name,
        out_shape=(pltpu.SemaphoreType.DMA((N_DEV - 1,)), pltpu.SemaphoreType.DMA((N_DEV - 1,)),
                   pltpu.HBM(x.shape, x.dtype), pltpu.HBM(land.shape, land.dtype), jax.ShapeDtypeStruct((8, 128), F32)),
        in_specs=[_ANY] * n_after + [_HBM, _HBM],
        out_specs=(_SEM, _SEM, _HBM, _HBM, pl.BlockSpec(memory_space=pltpu.VMEM)),
        input_output_aliases={n_after: 2, n_after + 1: 3},
        compiler_params=pltpu.CompilerParams(has_side_effects=_EFFECT),
    )(*((after,) if n_after else ()), pltpu.with_memory_space_constraint(x, pltpu.HBM),
      pltpu.with_memory_space_constraint(land, pltpu.HBM))


def _exchange_wait(handle, after, *, gather, name):
    send_sems, recv_sems, x_thru, land_thru, _ = handle

    def body(x_ref, land_ref, send_sems, recv_sems, after_ref, x_out, land_out):
        me, peers = _peers()
        for k, (pos, idx) in enumerate(peers):
            cp = pltpu.make_async_remote_copy(
                src_ref=x_ref if gather else x_ref.at[idx], dst_ref=land_ref.at[idx],
                send_sem=send_sems.at[k], recv_sem=recv_sems.at[k],
                device_id=pos, device_id_type=pl.DeviceIdType.MESH)
            cp.wait_send()
            cp.wait_recv()

    return pl.pallas_call(
        body, name=name,
        out_shape=(pltpu.HBM(x_thru.shape, x_thru.dtype), pltpu.HBM(land_thru.shape, land_thru.dtype)),
        in_specs=(_HBM, _HBM, _SEM, _SEM, _ANY), out_specs=(_HBM, _HBM), input_output_aliases={0: 0, 1: 1},
        compiler_params=pltpu.CompilerParams(has_side_effects=_EFFECT),
    )(x_thru, land_thru, send_sems, recv_sems, after)


def _to_proj_rows(t):
    z = jnp.zeros((D_PROJ - C_SM - 2 * GDN_HEADS - GLA_RANK,) + t.shape[1:], t.dtype)
    return jnp.concatenate([t[R_Z:R_A], t[R_GR:R_LR], t[R_GQ:R_GR], t[:R_Z], t[R_A:R_GQ], t[R_LR:], z], axis=0)


def _from_proj_rows(t):
    ab = C_SM + 2 * GDN_HEADS
    return jnp.concatenate([t[C_QKV:C_SM], t[C_Z:C_GR], t[C_SM:ab], t[C_GQ:C_QKV], t[C_GR:C_GQ],
                            t[ab:ab + GLA_RANK]], axis=0)


def _local_step(x, target, meta, attn_nw, conv_w, a_log, dt_bias, gdn_nw, w2, b2, gla_nw, ffn_nw, final_nw,
                fetch, emit, start=None):
    S = x.shape[0]
    h0 = jnp.concatenate([jnp.zeros((ROW_PAD, D_MODEL), F32), meta, x], axis=0)
    target_p = jnp.concatenate([jnp.zeros((HEAD_ROWS, D_MODEL), F32), target], axis=0)
    conv_w8 = jnp.concatenate([conv_w, jnp.zeros((8 - CONV_K, conv_w.shape[1]), F32)], axis=0)
    w2p = jnp.zeros((SM_W, GLA_QK), F32).at[2 * GDN_HEADS:2 * GDN_HEADS + GLA_RANK].set(w2)
    alog_p = jnp.zeros((1, SM_W), F32).at[:, :GDN_HEADS].set(a_log)
    dt_p = jnp.zeros((1, SM_W), F32).at[:, :GDN_HEADS].set(dt_bias)

    n1 = _rmsnorm_fwd(h0, attn_nw, name="attn_norm", after=start)
    w_in_t = fetch("w_in_t", n1)
    proj = _matmul(n1, w_in_t, mode="nt", name="in_proj")
    gb, la = _gates_fwd(proj, w2p, b2, alog_p, dt_p, name="gates")
    act = _prep_fwd(proj, conv_w8, name="gdn_prep")
    o_gdn, s_gdn, t_gdn = _gdn_fwd(act, gb, name="gdn_fwd")
    o_gla, s_gla = _gla_fwd(proj, la, name="gla_fwd")
    mixed = _mix_fwd(o_gdn, o_gla, proj, gdn_nw, gla_nw, name="mix")
    w_gate_t, w_up_t, w_out, w_down = fetch("rest", mixed)
    h1 = _matmul(mixed, w_out, mode="nn", add=h0, name="out_proj")
    n2 = _rmsnorm_fwd(h1, ffn_nw, name="ffn_norm")
    gate, up, hid = _swiglu_fwd(n2, w_gate_t, w_up_t, name="swiglu")
    h2 = _matmul(hid, w_down, mode="nn", add=h1, name="ffn_down", tm=688, tk=D_FF)
    dh2, dh2_b, d_final_nw, loss = _loss_head(h2, final_nw, target_p, name="loss_head")

    wg = dict(mode="tn", out_dtype=BF16, tn=512, tk=S + HEAD_ROWS)
    tok = emit("w_down", _matmul(hid, dh2_b, name="d_w_down", tm=704, **wg))
    d_gate, d_up = _swiglu_bwd(dh2_b, w_down, gate, up, name="d_swiglu", after=tok)
    tok = emit("w_gate_t", _matmul(d_gate, n2, name="d_w_gate", tm=704, **wg))
    tok = emit("w_up_t", _matmul(d_up, n2, name="d_w_up", tm=704, after=tok, **wg))
    d_n2 = _matmul(d_gate, w_gate_t, mode="nn", name="d_n2_gate", tm=688, tk=D_FF, after=tok)
    d_n2 = _matmul(d_up, w_up_t, mode="nn", add=d_n2, name="d_n2_up", tm=688, tk=D_FF)
    dh1, dh1_b, d_ffn_nw = _rmsnorm_bwd(h1, ffn_nw, d_n2, dh2, name="d_ffn_norm", also_bf16=True)

    tok = emit("w_out", _matmul(mixed, dh1_b, name="d_w_out", tm=512, **wg))
    d_mixed = _matmul(dh1_b, w_out, mode="nt", name="d_mixed", after=tok)
    do_gdn, do_gla, d_proj, d_gdn_nw, d_gla_nw = _mix_bwd(o_gdn, o_gla, proj, gdn_nw, gla_nw, d_mixed, name="d_mix")
    d_proj, d_la = _gla_bwd(proj, la, do_gla, s_gla, d_proj, name="gla_bwd")
    dact, dgb_heads = _gdn_bwd(act, gb, do_gdn, s_gdn, t_gdn, name="gdn_bwd")
    d_proj, d_w2p, d_b2, d_alog, d_dt = _gates_bwd(proj, w2p, b2, alog_p, dt_p, dgb_heads, d_la, d_proj, name="d_gates")
    dc, d_conv_w8 = _prep_bwd_a(proj, conv_w8, dact, name="d_gdn_prep")
    d_proj = _prep_bwd_b(dc, conv_w8, d_proj, name="d_conv")
    tok = emit("w_in_t", _matmul(d_proj, n1, name="d_w_in", tm=768, **wg))
    d_n1 = _matmul(d_proj, w_in_t, mode="nn", name="d_n1", tm=688, tk=D_PROJ, after=tok)
    dh0, d_attn_nw = _rmsnorm_bwd(h0, attn_nw, d_n1, dh1, name="d_attn_norm", also_bf16=False)

    return dict(
        loss=loss[0, 0], grad_x=dh0[HEAD_ROWS:], meta=dh0[ROW_PAD:HEAD_ROWS], attn_nw=d_attn_nw,
        conv_w=d_conv_w8[:CONV_K], a_log=d_alog[:, :GDN_HEADS], dt_bias=d_dt[:, :GDN_HEADS], gdn_nw=d_gdn_nw,
        w2=d_w2p[2 * GDN_HEADS:2 * GDN_HEADS + GLA_RANK], b2=d_b2, gla_nw=d_gla_nw, ffn_nw=d_ffn_nw,
        final_nw=d_final_nw)


SMALL_ROWS = 32


def kernel(x, meta_tokens, attn_norm_w, w_in, gdn_conv_w, gdn_a_log, gdn_dt_bias, gdn_norm_w, gla_gate_w2, gla_gate_b, gla_norm_w, w_out, ffn_norm_w, w_gate, w_up, w_down, final_norm_w, loss_target, m_meta_tokens, m_attn_norm_w, m_w_in, m_gdn_conv_w, m_gdn_a_log, m_gdn_dt_bias, m_gdn_norm_w, m_gla_gate_w2, m_gla_gate_b, m_gla_norm_w, m_w_out, m_ffn_norm_w, m_w_gate, m_w_up, m_w_down, m_final_norm_w, v_meta_tokens, v_attn_norm_w, v_w_in, v_gdn_conv_w, v_gdn_a_log, v_gdn_dt_bias, v_gdn_norm_w, v_gla_gate_w2, v_gla_gate_b, v_gla_norm_w, v_w_out, v_ffn_norm_w, v_w_gate, v_w_up, v_w_down, v_final_norm_w):
    me = 4 * lax.axis_index("x") + 2 * lax.axis_index("y") + lax.axis_index("c")
    n_in, n_ff, n_out = D_IN // N_DEV, D_FF // N_DEV, D_MODEL // N_DEV

    n_conv = gdn_conv_w.shape[2]
    n_w2 = gla_gate_w2.shape[2]
    n_meta = meta_tokens.shape[1]
    small = jnp.zeros((40, n_conv), F32)
    small = small.at[0:N_META, :n_meta].set(meta_tokens)
    small = small.at[N_META:N_META + CONV_K, :].set(gdn_conv_w[0])
    small = small.at[24:24 + GLA_RANK, :n_w2].set(gla_gate_w2[0])
    small_all = _exchange(small, gather=True, name="gather_small")
    meta_f = small_all[:, 0:N_META, :n_meta].transpose(1, 0, 2).reshape(N_META, D_MODEL)
    conv_f = small_all[:, N_META:N_META + CONV_K, :].transpose(1, 0, 2).reshape(CONV_K, N_DEV * n_conv)
    w2_f = small_all[:, 24:24 + GLA_RANK, :n_w2].transpose(1, 0, 2).reshape(GLA_RANK, N_DEV * n_w2)

    o1, o2, o3 = n_ff, 2 * n_ff, 2 * n_ff + n_out
    in_h = _exchange_start(w_in[0].T.astype(BF16), gather=True, name="gather_w_in_start")
    rest = jnp.concatenate([w_gate[0].T, w_up[0].T, w_out[0], w_down[0]], axis=0).astype(BF16)
    rest_h = _exchange_start(rest, gather=True, name="gather_rest_start", after=in_h[4])

    def fetch(name, after):
        handle = in_h if name == "w_in_t" else rest_h
        own, got = _exchange_wait(handle, after, gather=True, name="gather_" + name + "_wait")
        got = lax.dynamic_update_index_in_dim(got, own, me, 0)
        if name == "w_in_t":
            return _to_proj_rows(got.reshape(D_IN, D_MODEL))
        return (got[:, :o1].reshape(D_FF, D_MODEL), got[:, o1:o2].reshape(D_FF, D_MODEL),
                got[:, o2:o3].reshape(D_MODEL, D_MODEL), got[:, o3:].reshape(D_FF, D_MODEL))

    sent = {}

    def emit(name, grad):
        if name == "w_in_t":
            grad = _from_proj_rows(grad)
        parts = grad.reshape(N_DEV, grad.shape[0] // N_DEV, D_MODEL)
        sent[name] = _exchange_start(parts, gather=False, name="scatter_" + name + "_start")
        return sent[name][4]

    g = _local_step(x[0], loss_target[0], meta_f, attn_norm_w, conv_f, gdn_a_log, gdn_dt_bias, gdn_norm_w, w2_f,
                    gla_gate_b, gla_norm_w, ffn_norm_w, final_norm_w.reshape(1, D_MODEL), fetch, emit, start=rest_h[4])

    def total(name, after):
        handle = sent[name]
        own, got = _exchange_wait(handle, after, gather=False, name="scatter_" + name + "_wait")
        got = lax.dynamic_update_index_in_dim(got, lax.dynamic_index_in_dim(own, me, 0, keepdims=False), me, 0)
        return _sum_slabs(got, name="sum_" + name)

    grad_w_down = total("w_down", g["attn_nw"])[None]
    grad_w_gate = total("w_gate_t", grad_w_down)
    grad_w_up = total("w_up_t", grad_w_gate)
    grad_w_out = total("w_out", grad_w_up)[None]
    grad_w_in = total("w_in_t", grad_w_out)

    misc = jnp.concatenate([g["a_log"], g["dt_bias"], g["gdn_nw"], g["gla_nw"], g["b2"], g["loss"].reshape(1, 1)], axis=1)
    n_misc = misc.shape[1]
    misc = jnp.pad(misc, ((0, 0), (0, D_MODEL - n_misc)))
    rows = jnp.concatenate([g["attn_nw"], g["ffn_nw"], g["final_nw"], misc, g["meta"],
                            g["conv_w"].reshape(-1, D_MODEL), g["w2"].reshape(-1, D_MODEL)], axis=0)
    rows = jnp.pad(rows, ((0, SMALL_ROWS - rows.shape[0]), (0, 0)))
    tot = _sum_slabs(_exchange(rows, gather=True, name="gather_small_grads"), name="sum_small_grads")
    grad_attn_nw, grad_ffn_nw, grad_final_nw = tot[0:1], tot[1:2], tot[2]
    grad_a_log = tot[3:4, 0:8]
    grad_dt = tot[3:4, 8:16]
    grad_gdn_nw = tot[3:4, 16:16 + GDN_DV]
    grad_gla_nw = tot[3:4, 144:144 + GLA_DV]
    grad_b2 = tot[3:4, 400:400 + GLA_QK]
    loss = tot[3, n_misc - 1]
    r0 = 4 + N_META
    grad_meta = lax.dynamic_slice(tot[4:r0], (0, me * n_meta), (N_META, n_meta))
    r1 = r0 + CONV_K * N_DEV * n_conv // D_MODEL
    grad_conv = lax.dynamic_slice(tot[r0:r1].reshape(CONV_K, N_DEV * n_conv), (0, me * n_conv), (CONV_K, n_conv))[None]
    r2 = r1 + GLA_RANK * N_DEV * n_w2 // D_MODEL
    grad_w2 = lax.dynamic_slice(tot[r1:r2].reshape(GLA_RANK, N_DEV * n_w2), (0, me * n_w2), (GLA_RANK, n_w2))[None]

    weights = [meta_tokens, attn_norm_w, w_in, gdn_conv_w, gdn_a_log, gdn_dt_bias, gdn_norm_w, gla_gate_w2,
               gla_gate_b, gla_norm_w, w_out, ffn_norm_w, w_gate, w_up, w_down, final_norm_w]
    grads = [grad_meta, grad_attn_nw, grad_w_in, grad_conv, grad_a_log, grad_dt, grad_gdn_nw, grad_w2,
             grad_b2, grad_gla_nw, grad_w_out, grad_ffn_nw, grad_w_gate, grad_w_up, grad_w_down, grad_final_nw]
    ms = [m_meta_tokens, m_attn_norm_w, m_w_in, m_gdn_conv_w, m_gdn_a_log, m_gdn_dt_bias, m_gdn_norm_w,
          m_gla_gate_w2, m_gla_gate_b, m_gla_norm_w, m_w_out, m_ffn_norm_w, m_w_gate, m_w_up, m_w_down, m_final_norm_w]
    vs = [v_meta_tokens, v_attn_norm_w, v_w_in, v_gdn_conv_w, v_gdn_a_log, v_gdn_dt_bias, v_gdn_norm_w,
          v_gla_gate_w2, v_gla_gate_b, v_gla_norm_w, v_w_out, v_ffn_norm_w, v_w_gate, v_w_up, v_w_down, v_final_norm_w]
    transposed = (2, 12, 13)
    outs = [[], [], [], []]
    for idx, (w, gr, m, v) in enumerate(zip(weights, grads, ms, vs)):
        if idx in transposed:
            res = (gr,) + _adamw(w[0].T, gr, m[0].T, v[0].T, name=f"adamw_{idx}")
            res = [t.T[None] for t in res]
        else:
            gr = gr.reshape(w.shape)
            res = (gr,) + _adamw(w, gr, m, v, name=f"adamw_{idx}")
        for lst, t in zip(outs, res):
            lst.append(t)
    return (loss, g["grad_x"][None], *outs[0], *outs[1], *outs[2], *outs[3])
```

```python
import functools

import jax
import jax.numpy as jnp
from jax import lax
from jax.experimental import pallas as pl
from jax.experimental.pallas import tpu as pltpu

F32 = jnp.float32
BF16 = jnp.bfloat16
_MXU_DTYPE = jnp.bfloat16

D_MODEL = 2048
N_META = 16
ROW_PAD = 48
HEAD_ROWS = ROW_PAD + N_META
CONV_K = 4
GDN_HEADS, GDN_DK, GDN_DV, GDN_CHUNK = 8, 128, 128, 64
GLA_HEADS, GLA_DK, GLA_DV, GLA_CHUNK = 4, 128, 256, 16
GLA_RANK = 16
GLA_GATE_NORMALIZER = 16.0
GDN_QK = GDN_HEADS * GDN_DK
GDN_V = GDN_HEADS * GDN_DV
GLA_QK = GLA_HEADS * GLA_DK
GLA_V = GLA_HEADS * GLA_DV
D_FF = 5632
D_IN = 7200
NORM_EPS = 1e-6
C_Z, C_GR, C_GQ, C_GK, C_GV, C_QKV, C_SM = 0, 1024, 2048, 2560, 3072, 4096, 7168
SM_W = 128
D_PROJ = 7680
R_Z, R_A, R_B, R_GQ, R_GK, R_GV, R_GR, R_LR = 3072, 4096, 4104, 4112, 4624, 5136, 6160, 7184

ADAM_LR, ADAM_B1, ADAM_B2, ADAM_EPS, ADAM_WD, ADAM_STEP = 0.001, 0.9, 0.999, 1e-08, 0.01, 10

N_DEV = 8
VMEM_LIMIT = 56 * 1024 * 1024

NN = (((1,), (0,)), ((), ()))
NT = (((1,), (1,)), ((), ()))
TN = (((0,), (0,)), ((), ()))


def _dot(a, b, dims=NN):
    return lax.dot_general(a.astype(_MXU_DTYPE), b.astype(_MXU_DTYPE), dims, preferred_element_type=F32)


def _dotx(a, b, dims=NN):
    return lax.dot_general(a, b, dims, precision=lax.Precision.HIGHEST, preferred_element_type=F32)


def _dot3(a, b):
    ah = a.astype(BF16)
    al = (a - ah.astype(F32)).astype(BF16)
    bh = b.astype(BF16)
    bl = (b - bh.astype(F32)).astype(BF16)
    d = functools.partial(lax.dot_general, dimension_numbers=NN, preferred_element_type=F32)
    return d(ah, bh) + (d(ah, bl) + d(al, bh))


def _tile(n, target, mult=8):
    best = None
    for t in range(mult, min(n, target) + 1, mult):
        if n % t == 0:
            best = t
    return best if best is not None else n


def _params(*sem):
    return pltpu.CompilerParams(dimension_semantics=sem, vmem_limit_bytes=VMEM_LIMIT)


def _sigmoid(x):
    return 0.5 * jnp.tanh(0.5 * x) + 0.5


def _softplus(x):
    return jnp.maximum(x, 0.0) + jnp.log1p(jnp.exp(-jnp.abs(x)))


def _silu_and_grad(c):
    s = _sigmoid(c)
    return c * s, s * (1.0 + c * (1.0 - s))


_ANY = pl.BlockSpec(memory_space=pl.ANY)


def _matmul(a, b, *, mode, name, out_dtype=F32, add=None, after=None, tm=1376, tn=512, tk=2064):
    if mode == "tn":
        K, M = a.shape
        N = b.shape[1]
    else:
        M, K = a.shape
        N = b.shape[0] if mode == "nt" else b.shape[1]
    tm = _tile(M, tm, 128 if mode == "tn" else 16)
    tn = _tile(N, tn, 128)
    tk = _tile(K, tk, 16 if mode == "tn" else 128)
    gm, gn, gk = M // tm, N // tn, K // tk
    dims = {"nn": NN, "nt": NT, "tn": TN}[mode]

    n_after = 0 if after is None else 1

    def body(*refs):
        refs = refs[n_after:]
        if add is None:
            a_ref, b_ref, o_ref = refs[:3]
            add_ref = None
        else:
            a_ref, b_ref, add_ref, o_ref = refs[:4]
        p = _dot(a_ref[...], b_ref[...], dims)

        def finish(r):
            if add_ref is not None:
                r = r + add_ref[...]
            o_ref[...] = r.astype(out_dtype)

        if gk == 1:
            finish(p)
        else:
            acc_ref = refs[-1]
            k = pl.program_id(2)

            @pl.when(k == 0)
            def _():
                acc_ref[...] = p

            @pl.when(k > 0)
            def _():
                acc_ref[...] += p

            @pl.when(k == gk - 1)
            def _():
                finish(acc_ref[...])

    if mode == "tn":
        a_spec = pl.BlockSpec((tk, tm), lambda i, j, k: (k, i))
    else:
        a_spec = pl.BlockSpec((tm, tk), lambda i, j, k: (i, k))
    if mode == "nt":
        b_spec = pl.BlockSpec((tn, tk), lambda i, j, k: (j, k))
    else:
        b_spec = pl.BlockSpec((tk, tn), lambda i, j, k: (k, j))
    o_spec = pl.BlockSpec((tm, tn), lambda i, j, k: (i, j))
    in_specs = [_ANY] * n_after + [a_spec, b_spec] + ([o_spec] if add is not None else [])
    args = ((after,) if n_after else ()) + (a, b) + ((add,) if add is not None else ())
    return pl.pallas_call(
        body, name=name, grid=(gm, gn, gk), in_specs=in_specs, out_specs=o_spec,
        out_shape=jax.ShapeDtypeStruct((M, N), out_dtype),
        scratch_shapes=[pltpu.VMEM((tm, tn), F32)] if gk > 1 else [],
        compiler_params=_params("parallel", "parallel", "arbitrary"),
    )(*args)


def _rmsnorm_fwd(h, w, *, name, after=None):
    M, D = h.shape
    tm = _tile(M, 688, 16)
    n_after = 0 if after is None else 1

    def body(*refs):
        h_ref, w_ref, n_ref = refs[n_after:]
        x = h_ref[...]
        r = lax.rsqrt(jnp.mean(x * x, axis=-1, keepdims=True) + NORM_EPS)
        n_ref[...] = (x * r * w_ref[...]).astype(n_ref.dtype)

    return pl.pallas_call(
        body, name=name, grid=(M // tm,),
        in_specs=[_ANY] * n_after + [pl.BlockSpec((tm, D), lambda i: (i, 0)), pl.BlockSpec((1, D), lambda i: (0, 0))],
        out_specs=pl.BlockSpec((tm, D), lambda i: (i, 0)),
        out_shape=jax.ShapeDtypeStruct((M, D), BF16),
        compiler_params=_params("parallel"),
    )(*((after,) if n_after else ()), h, w)


def _rmsnorm_bwd(h, w, dn, dres, *, name, also_bf16):
    M, D = h.shape
    tm = _tile(M, 344, 16)
    g = M // tm

    def body(h_ref, w_ref, dn_ref, dres_ref, dh_ref, *rest):
        dhb_ref = rest[0] if also_bf16 else None
        dw_ref, acc_ref = rest[-2:]
        i = pl.program_id(0)
        x = h_ref[...]
        r = lax.rsqrt(jnp.mean(x * x, axis=-1, keepdims=True) + NORM_EPS)
        xhat = x * r
        dn_ = dn_ref[...]
        dxhat = dn_ * w_ref[...]
        dh = dres_ref[...] + r * (dxhat - xhat * jnp.mean(dxhat * xhat, axis=-1, keepdims=True))
        dh_ref[...] = dh
        if also_bf16:
            dhb_ref[...] = dh.astype(dhb_ref.dtype)
        part = jnp.sum((dn_ * xhat).reshape(tm // 8, 8, D), axis=0)

        @pl.when(i == 0)
        def _():
            acc_ref[...] = part

        @pl.when(i > 0)
        def _():
            acc_ref[...] += part

        @pl.when(i == g - 1)
        def _():
            dw_ref[...] = jnp.sum(acc_ref[...], axis=0, keepdims=True)

    row = pl.BlockSpec((tm, D), lambda i: (i, 0))
    vec = pl.BlockSpec((1, D), lambda i: (0, 0))
    return pl.pallas_call(
        body, name=name, grid=(g,), in_specs=[row, vec, row, row],
        out_specs=[row] + ([row] if also_bf16 else []) + [vec],
        out_shape=[jax.ShapeDtypeStruct((M, D), F32)] + ([jax.ShapeDtypeStruct((M, D), BF16)] if also_bf16 else [])
        + [jax.ShapeDtypeStruct((1, D), F32)],
        scratch_shapes=[pltpu.VMEM((8, D), F32)],
        compiler_params=_params("arbitrary"),
    )(h, w, dn, dres)


def _loss_head(h, w, target_p, *, name):
    M, D = h.shape
    tm = _tile(M, 344, 16)
    g = M // tm

    def body(h_ref, w_ref, t_ref, dh_ref, dhb_ref, dw_ref, loss_ref, acc_ref, lacc_ref):
        i = pl.program_id(0)
        x = h_ref[...]
        row = i * tm + lax.broadcasted_iota(jnp.int32, (tm, 1), 0)
        live = row >= HEAD_ROWS
        r = lax.rsqrt(jnp.mean(x * x, axis=-1, keepdims=True) + NORM_EPS)
        xhat = x * r
        err = jnp.where(live, xhat * w_ref[...] - t_ref[...], 0.0)
        dy = err * (1.0 / D)
        dxhat = dy * w_ref[...]
        dh = r * (dxhat - xhat * jnp.mean(dxhat * xhat, axis=-1, keepdims=True))
        dh_ref[...] = dh
        dhb_ref[...] = dh.astype(dhb_ref.dtype)
        part = jnp.sum((dy * xhat).reshape(tm // 8, 8, D), axis=0)
        lpart = jnp.sum((err * err).reshape(tm // 8, 8, D), axis=0)

        @pl.when(i == 0)
        def _():
            acc_ref[...] = part
            lacc_ref[...] = lpart

        @pl.when(i > 0)
        def _():
            acc_ref[...] += part
            lacc_ref[...] += lpart

        @pl.when(i == g - 1)
        def _():
            dw_ref[...] = jnp.sum(acc_ref[...], axis=0, keepdims=True)
            tot = jnp.sum(jnp.sum(lacc_ref[...], axis=0, keepdims=True), axis=1, keepdims=True)
            loss_ref[...] = jnp.broadcast_to(tot * (0.5 / D), (1, 128))

    row = pl.BlockSpec((tm, D), lambda i: (i, 0))
    vec = pl.BlockSpec((1, D), lambda i: (0, 0))
    return pl.pallas_call(
        body, name=name, grid=(g,), in_specs=[row, vec, row],
        out_specs=[row, row, vec, pl.BlockSpec((1, 128), lambda i: (0, 0))],
        out_shape=[jax.ShapeDtypeStruct((M, D), F32), jax.ShapeDtypeStruct((M, D), BF16),
                   jax.ShapeDtypeStruct((1, D), F32), jax.ShapeDtypeStruct((1, 128), F32)],
        scratch_shapes=[pltpu.VMEM((8, D), F32), pltpu.VMEM((8, D), F32)],
        compiler_params=_params("arbitrary"),
    )(h, w, target_p)


def _gate_terms(sm, w2p, b2, alog_p, dt_p, row0):
    tm = sm.shape[0]
    lane = lax.broadcasted_iota(jnp.int32, (tm, SM_W), 1)
    live = (row0 + lax.broadcasted_iota(jnp.int32, (tm, 1), 0)) >= ROW_PAD
    pre = sm + dt_p
    neg_a = -jnp.exp(alog_p)
    g = neg_a * _softplus(pre)
    beta = _sigmoid(sm)
    z = _dot(sm, w2p) + b2
    return lane, live, pre, neg_a, g, beta, z


def _gates_fwd(proj, w2p, b2, alog_p, dt_p, *, name):
    M = proj.shape[0]
    tm = _tile(M, 688, 8)

    def body(sm_ref, w2_ref, b2_ref, al_ref, dt_ref, gb_ref, la_ref):
        row0 = pl.program_id(0) * tm
        lane, live, _, _, g, beta, z = _gate_terms(sm_ref[...], w2_ref[...], b2_ref[...], al_ref[...], dt_ref[...], row0)
        gb = jnp.where(lane < GDN_HEADS, g, jnp.where(lane < 2 * GDN_HEADS, beta, 0.0))
        gb_ref[...] = jnp.where(live, gb, 0.0)
        la = (jnp.minimum(z, 0.0) - jnp.log1p(jnp.exp(-jnp.abs(z)))) * (1.0 / GLA_GATE_NORMALIZER)
        la_ref[...] = jnp.where(live, la, 0.0)

    full = lambda s: pl.BlockSpec(s, lambda i: (0, 0))
    return pl.pallas_call(
        body, name=name, grid=(M // tm,),
        in_specs=[pl.BlockSpec((tm, SM_W), lambda i: (i, C_SM // SM_W)), full((SM_W, GLA_QK)), full((1, GLA_QK)),
                  full((1, SM_W)), full((1, SM_W))],
        out_specs=[pl.BlockSpec((tm, SM_W), lambda i: (i, 0)), pl.BlockSpec((tm, GLA_QK), lambda i: (i, 0))],
        out_shape=[jax.ShapeDtypeStruct((M, SM_W), F32), jax.ShapeDtypeStruct((M, GLA_QK), F32)],
        compiler_params=_params("parallel"),
    )(proj, w2p, b2, alog_p, dt_p)


def _gates_bwd(proj, w2p, b2, alog_p, dt_p, dgb_heads, dla, d_proj, *, name):
    M = proj.shape[0]
    tm = _tile(M, 688, 8)
    g_ = M // tm

    tail_w = D_PROJ - C_SM

    def body(sm_ref, w2_ref, b2_ref, al_ref, dt_ref, dgb_ref, dla_ref, _,
             dsm_ref, dw2_ref, db2_ref, dal_ref, ddt_ref):
        i = pl.program_id(0)
        sm = sm_ref[...]
        lane, live, pre, neg_a, g, beta, z = _gate_terms(sm, w2_ref[...], b2_ref[...], al_ref[...], dt_ref[...], i * tm)
        dz = jnp.where(live, dla_ref[...] * (_sigmoid(-z) * (1.0 / GLA_GATE_NORMALIZER)), 0.0)
        dsm_lr = _dot(dz, w2_ref[...], NT)
        dgb = dgb_ref[0]
        for hh in range(1, GDN_HEADS):
            dgb = dgb + dgb_ref[hh]
        dgb = jnp.where(live, dgb, 0.0)
        da = dgb * neg_a * _sigmoid(pre)
        db = dgb * beta * (1.0 - beta)
        dsm = jnp.where(lane < GDN_HEADS, da, jnp.where(lane < 2 * GDN_HEADS, db, dsm_lr))
        dsm_ref[:, 0:SM_W] = dsm.astype(dsm_ref.dtype)
        dsm_ref[:, SM_W:tail_w] = jnp.zeros((tm, tail_w - SM_W), dsm_ref.dtype)
        is_a = lane < GDN_HEADS
        dal = jnp.sum(jnp.where(is_a, dgb * g, 0.0), axis=0, keepdims=True)
        ddt = jnp.sum(jnp.where(is_a, da, 0.0), axis=0, keepdims=True)
        dw2 = _dot(sm, dz, TN)
        db2 = jnp.sum(dz, axis=0, keepdims=True)

        @pl.when(i == 0)
        def _():
            dw2_ref[...] = dw2
            db2_ref[...] = db2
            dal_ref[...] = dal
            ddt_ref[...] = ddt

        @pl.when(i > 0)
        def _():
            dw2_ref[...] += dw2
            db2_ref[...] += db2
            dal_ref[...] += dal
            ddt_ref[...] += ddt

    full = lambda s: pl.BlockSpec(s, lambda i: (0, 0))
    return pl.pallas_call(
        body, name=name, grid=(g_,),
        in_specs=[pl.BlockSpec((tm, SM_W), lambda i: (i, C_SM // SM_W)), full((SM_W, GLA_QK)), full((1, GLA_QK)),
                  full((1, SM_W)), full((1, SM_W)),
                  pl.BlockSpec((GDN_HEADS, tm, SM_W), lambda i: (0, i, 0)),
                  pl.BlockSpec((tm, GLA_QK), lambda i: (i, 0)), _ANY],
        out_specs=[pl.BlockSpec((tm, tail_w), lambda i: (i, C_SM // tail_w)), full((SM_W, GLA_QK)), full((1, GLA_QK)),
                   full((1, SM_W)), full((1, SM_W))],
        out_shape=[jax.ShapeDtypeStruct(d_proj.shape, d_proj.dtype), jax.ShapeDtypeStruct((SM_W, GLA_QK), F32),
                   jax.ShapeDtypeStruct((1, GLA_QK), F32), jax.ShapeDtypeStruct((1, SM_W), F32),
                   jax.ShapeDtypeStruct((1, SM_W), F32)],
        input_output_aliases={7: 0},
        compiler_params=_params("arbitrary"),
    )(proj, w2p, b2, alog_p, dt_p, dgb_heads, dla, d_proj)


QKV_W = GDN_QK
N_QKV_GROUPS = 3
QKV_B0 = C_QKV // QKV_W
HALO = 8


def _conv_terms(x_ref, halo_ref, cw_ref, xs_ref, i, tm):
    xs_ref[HALO:HALO + tm, :] = x_ref[...]
    xs_ref[0:HALO, :] = jnp.where(i > 0, halo_ref[...], 0.0)
    cw = cw_ref[...]
    xs = xs_ref[...]
    taps = [(pltpu.roll(xs, CONV_K - 1 - t, 0) if t < CONV_K - 1 else xs)[HALO:HALO + tm, :] for t in range(CONV_K)]
    c = taps[0] * cw[0:1, :]
    for t in range(1, CONV_K):
        c = c + taps[t] * cw[t:t + 1, :]
    return c, taps


def _prep_fwd(proj, conv_w8, *, name):
    M = proj.shape[0]
    tm = _tile(M, 344, 8)

    def body(x_ref, halo_ref, cw_ref, o_ref, xs_ref):
        j, i = pl.program_id(0), pl.program_id(1)
        c, _ = _conv_terms(x_ref, halo_ref, cw_ref, xs_ref, i, tm)
        s, _ = _silu_and_grad(c)
        scale = jnp.where(j == 0, GDN_DK ** -0.5, 1.0)
        for hh in range(GDN_HEADS):
            cols = slice(hh * 128, (hh + 1) * 128)
            sh = s[:, cols]
            r = lax.rsqrt(jnp.sum(sh * sh, axis=-1, keepdims=True) + NORM_EPS)
            o_ref[:, cols] = jnp.where(j < 2, sh * (r * scale), sh)

    hb = tm // HALO
    return pl.pallas_call(
        body, name=name, grid=(N_QKV_GROUPS, M // tm),
        in_specs=[pl.BlockSpec((tm, QKV_W), lambda j, i: (i, QKV_B0 + j)),
                  pl.BlockSpec((HALO, QKV_W), lambda j, i: (jnp.maximum(i * hb - 1, 0), QKV_B0 + j)),
                  pl.BlockSpec((8, QKV_W), lambda j, i: (0, j))],
        out_specs=pl.BlockSpec((tm, QKV_W), lambda j, i: (i, j)),
        out_shape=jax.ShapeDtypeStruct((M, N_QKV_GROUPS * QKV_W), F32),
        scratch_shapes=[pltpu.VMEM((tm + HALO, QKV_W), F32)],
        compiler_params=_params("parallel", "arbitrary"),
    )(proj, proj, conv_w8)


def _prep_bwd_a(proj, conv_w8, dact, *, name):
    M = proj.shape[0]
    tm = _tile(M, 344, 8)
    g_ = M // tm

    def body(x_ref, halo_ref, cw_ref, da_ref, dc_ref, dcw_ref, xs_ref):
        j, i = pl.program_id(0), pl.program_id(1)
        c, taps = _conv_terms(x_ref, halo_ref, cw_ref, xs_ref, i, tm)
        s, ds_dc = _silu_and_grad(c)
        scale = jnp.where(j == 0, GDN_DK ** -0.5, 1.0)
        for hh in range(GDN_HEADS):
            cols = slice(hh * 128, (hh + 1) * 128)
            sh = s[:, cols]
            r = lax.rsqrt(jnp.sum(sh * sh, axis=-1, keepdims=True) + NORM_EPS)
            da = da_ref[:, cols]
            y = sh * r
            dy = da * scale
            ds_norm = r * (dy - y * jnp.sum(dy * y, axis=-1, keepdims=True))
            dc_ref[:, cols] = jnp.where(j < 2, ds_norm, da) * ds_dc[:, cols]
        dc = dc_ref[...]
        r8 = lax.broadcasted_iota(jnp.int32, (8, QKV_W), 0)
        part = jnp.zeros((8, QKV_W), F32)
        for t in range(CONV_K):
            part = jnp.where(r8 == t, jnp.sum(dc * taps[t], axis=0, keepdims=True), part)

        @pl.when(i == 0)
        def _():
            dcw_ref[...] = part

        @pl.when(i > 0)
        def _():
            dcw_ref[...] += part

    hb = tm // HALO
    blk = pl.BlockSpec((tm, QKV_W), lambda j, i: (i, j))
    return pl.pallas_call(
        body, name=name, grid=(N_QKV_GROUPS, g_),
        in_specs=[pl.BlockSpec((tm, QKV_W), lambda j, i: (i, QKV_B0 + j)),
                  pl.BlockSpec((HALO, QKV_W), lambda j, i: (jnp.maximum(i * hb - 1, 0), QKV_B0 + j)),
                  pl.BlockSpec((8, QKV_W), lambda j, i: (0, j)), blk],
        out_specs=[blk, pl.BlockSpec((8, QKV_W), lambda j, i: (0, j))],
        out_shape=[jax.ShapeDtypeStruct((M, N_QKV_GROUPS * QKV_W), F32),
                   jax.ShapeDtypeStruct((8, N_QKV_GROUPS * QKV_W), F32)],
        scratch_shapes=[pltpu.VMEM((tm + HALO, QKV_W), F32)],
        compiler_params=_params("parallel", "arbitrary"),
    )(proj, proj, conv_w8, dact)


def _prep_bwd_b(dc, conv_w8, d_proj, *, name):
    M = dc.shape[0]
    tm = _tile(M, 344, 16)
    g_ = M // tm

    def body(d_ref, halo_ref, cw_ref, _, o_ref, ds_ref):
        i = pl.program_id(1)
        ds_ref[0:tm, :] = d_ref[...]
        ds_ref[tm:tm + HALO, :] = jnp.where(i < g_ - 1, halo_ref[...], 0.0)
        cw = cw_ref[...]
        ds = ds_ref[...]
        acc = ds[0:tm, :] * cw[CONV_K - 1:CONV_K, :]
        for t in range(CONV_K - 1):
            acc = acc + pltpu.roll(ds, tm + HALO - (CONV_K - 1 - t), 0)[0:tm, :] * cw[t:t + 1, :]
        o_ref[...] = acc.astype(o_ref.dtype)

    hb = tm // HALO
    last = M // HALO - 1
    blk = pl.BlockSpec((tm, QKV_W), lambda j, i: (i, j))
    return pl.pallas_call(
        body, name=name, grid=(N_QKV_GROUPS, g_),
        in_specs=[blk, pl.BlockSpec((HALO, QKV_W), lambda j, i: (jnp.minimum((i + 1) * hb, last), j)),
                  pl.BlockSpec((8, QKV_W), lambda j, i: (0, j)), _ANY],
        out_specs=pl.BlockSpec((tm, QKV_W), lambda j, i: (i, QKV_B0 + j)),
        out_shape=jax.ShapeDtypeStruct(d_proj.shape, d_proj.dtype), input_output_aliases={3: 0},
        scratch_shapes=[pltpu.VMEM((tm + HALO, QKV_W), F32)],
        compiler_params=_params("parallel", "arbitrary"),
    )(dc, dc, conv_w8, d_proj)


def _round_robin(gens):
    gens = list(gens)
    while gens:
        alive = []
        for gen in gens:
            try:
                next(gen)
                alive.append(gen)
            except StopIteration:
                pass
        gens = alive


def _unit_lower_inverse(a_low, eye):
    n = a_low.shape[0]
    ri = lax.broadcasted_iota(jnp.int32, (n, n), 0)
    ci = lax.broadcasted_iota(jnp.int32, (n, n), 1)
    same = lambda shift: (ri >> shift) == (ci >> shift)
    b = jnp.where(same(3), -a_low, 0.0)
    x = eye + b
    p2 = _dot3(b, b)
    yield
    x = x + _dot3(x, p2)
    p4 = _dot3(p2, p2)
    yield
    x = x + _dot3(x, p4)
    yield
    for shift in (3, 4, 5):
        between = jnp.where(same(shift + 1) & ~same(shift), a_low, 0.0)
        t = _dot3(between, x)
        yield
        x = x - _dot3(x, t)
        yield
    return x


class _GdnChunk:
    def build(self, q, k, v, gb, h):
        C = GDN_CHUNK
        lane = lax.broadcasted_iota(jnp.int32, (C, SM_W), 1)
        g = jnp.sum(jnp.where(lane == h, gb, 0.0), axis=1, keepdims=True)
        self.beta = jnp.sum(jnp.where(lane == h + GDN_HEADS, gb, 0.0), axis=1, keepdims=True)
        ri = lax.broadcasted_iota(jnp.int32, (C, C), 0)
        ci = lax.broadcasted_iota(jnp.int32, (C, C), 1)
        self.causal = ri >= ci
        self.strict = ri > ci
        self.eye = (ri == ci).astype(F32)
        gcb = _dotx(self.causal.astype(F32), jnp.broadcast_to(g, (C, SM_W)))
        yield
        self.gcol = gcb[:, 0:1]
        grow = gcb.T[0:1, 0:C]
        self.decay = jnp.exp(jnp.where(self.causal, self.gcol - grow, -1e30))
        self.egc = jnp.exp(self.gcol)
        glast = gcb[C - 1:C, 0:1]
        self.elast = jnp.exp(glast - self.gcol)
        self.gl = jnp.exp(glast)
        self.q, self.k, self.v = q, k, v
        self.kb = k * self.beta
        m = _dot(self.kb, k, NT)
        n_ = _dot(q, k, NT)
        yield
        self.a_low = jnp.where(self.strict, m * self.decay, 0.0)
        self.p = n_ * self.decay
        self.qd = q * self.egc
        self.kd = k * self.elast
        self.bu = v * self.beta
        self.bw = self.kb * self.egc


GDN_HB = 8
GDN_HG = GDN_HEADS // GDN_HB


def _gdn_specs(n_of):
    C, W = GDN_CHUNK, 128 * GDN_HB
    q_spec = pl.BlockSpec((C, W), lambda g, n: (n_of(n), g))
    k_spec = pl.BlockSpec((C, W), lambda g, n: (n_of(n), g + GDN_HG))
    v_spec = pl.BlockSpec((C, W), lambda g, n: (n_of(n), g + 2 * GDN_HG))
    gb_spec = pl.BlockSpec((C, SM_W), lambda g, n: (n_of(n), 0))
    o_spec = pl.BlockSpec((C, W), lambda g, n: (n_of(n), g))
    s_spec = pl.BlockSpec((GDN_HB, None, GDN_DK, GDN_DV), lambda g, n: (g, n_of(n), 0, 0))
    t_spec = pl.BlockSpec((GDN_HB, None, C, C), lambda g, n: (g, n_of(n), 0, 0))
    return q_spec, k_spec, v_spec, gb_spec, o_spec, s_spec, t_spec


def _gdn_fwd(act, gb, *, name):
    M = act.shape[0]
    N = M // GDN_CHUNK

    def body(q_ref, k_ref, v_ref, gb_ref, o_ref, s_ref, t_ref, state):
        g, n = pl.program_id(0), pl.program_id(1)

        @pl.when(n == 0)
        def _():
            state[...] = jnp.zeros_like(state)

        gb_ = gb_ref[...]

        def head(hh):
            cols = slice(hh * 128, (hh + 1) * 128)
            c = _GdnChunk()
            yield from c.build(q_ref[:, cols], k_ref[:, cols], v_ref[:, cols], gb_, g * GDN_HB + hh)
            tinv = yield from _unit_lower_inverse(c.a_low, c.eye)
            s = state[hh]
            s_ref[hh] = s
            t_ref[hh] = tinv
            u = _dot(tinv, c.bu)
            w = _dot(tinv, c.bw)
            yield
            vn = u - _dot(w, s)
            o1 = _dot(c.qd, s)
            yield
            o_ref[:, cols] = o1 + _dot(c.p, vn)
            state[hh] = c.gl * s + _dot(c.kd, vn, TN)

        _round_robin(head(hh) for hh in range(GDN_HB))

    q_spec, k_spec, v_spec, gb_spec, o_spec, s_spec, t_spec = _gdn_specs(lambda n: n)
    return pl.pallas_call(
        body, name=name, grid=(GDN_HG, N),
        in_specs=[q_spec, k_spec, v_spec, gb_spec], out_specs=[o_spec, s_spec, t_spec],
        out_shape=[jax.ShapeDtypeStruct((M, GDN_V), F32),
                   jax.ShapeDtypeStruct((GDN_HEADS, N, GDN_DK, GDN_DV), F32),
                   jax.ShapeDtypeStruct((GDN_HEADS, N, GDN_CHUNK, GDN_CHUNK), F32)],
        scratch_shapes=[pltpu.VMEM((GDN_HB, GDN_DK, GDN_DV), F32)],
        compiler_params=_params("parallel", "arbitrary"),
    )(act, act, act, gb)


def _gdn_bwd(act, gb, do, s_all, t_all, *, name):
    M = act.shape[0]
    N = M // GDN_CHUNK
    C = GDN_CHUNK
    assert GDN_HG == 1

    def body(q_ref, k_ref, v_ref, gb_ref, do_ref, s_ref, t_ref, dact_ref, dgb_ref, dstate):
        g, n = pl.program_id(0), pl.program_id(1)

        @pl.when(n == 0)
        def _():
            dstate[...] = jnp.zeros_like(dstate)

        gb_ = gb_ref[...]
        last = lax.broadcasted_iota(jnp.int32, (C, 1), 0) == C - 1
        upper = (lax.broadcasted_iota(jnp.int32, (C, C), 0) <= lax.broadcasted_iota(jnp.int32, (C, C), 1)).astype(F32)
        lane = lax.broadcasted_iota(jnp.int32, (C, SM_W), 1)
        def head(hh):
            cols = slice(hh * 128, (hh + 1) * 128)
            h = g * GDN_HB + hh
            c = _GdnChunk()
            yield from c.build(q_ref[:, cols], k_ref[:, cols], v_ref[:, cols], gb_, h)
            tinv = t_ref[hh]
            s = s_ref[hh]
            do_ = do_ref[:, cols]
            ds1 = dstate[hh]
            u = _dot(tinv, c.bu)
            w = _dot(tinv, c.bw)
            dqd = _dot(do_, s, NT)
            dvn0 = _dot(c.p, do_, TN) + _dot(c.kd, ds1)
            dst0 = _dot(c.qd, do_, TN) + c.gl * ds1
            yield
            vn = u - _dot(w, s)
            dvn = dvn0
            yield
            dp = jnp.where(c.causal, _dot(do_, vn, NT), 0.0)
            dstate[hh] = dst0 - _dot(w, dvn, TN)
            dkd = _dot(vn, ds1, NT)
            dw = -_dot(dvn, s, NT)
            dbu = _dot(tinv, dvn, TN)
            dgl = jnp.sum(jnp.sum(s * ds1, axis=1, keepdims=True), axis=0, keepdims=True)
            yield
            dbw = _dot(tinv, dw, TN)
            t1 = _dot(dbu, u, NT)
            yield
            da = jnp.where(c.strict, -(t1 + _dot(dbw, w, NT)), 0.0)
            dn_ = dp * c.decay
            dq0 = _dot(dn_, c.k)
            dk0 = _dot(dn_, c.q, TN)
            yield
            dm = da * c.decay
            e = da * c.a_low + dp * c.p
            dkb = _dot(dm, c.k) + dbw * c.egc
            dact_ref[:, GDN_QK + hh * 128:GDN_QK + (hh + 1) * 128] = (
                _dot(dm, c.kb, TN) + dk0 + dkb * c.beta + dkd * c.elast)
            dact_ref[:, cols] = dq0 + dqd * c.egc
            dact_ref[:, 2 * GDN_QK + hh * 128:2 * GDN_QK + (hh + 1) * 128] = dbu * c.beta
            dbeta = jnp.sum(dbu * c.v, axis=1, keepdims=True) + jnp.sum(dkb * c.k, axis=1, keepdims=True)
            t_kd = jnp.sum(dkd * c.kd, axis=1, keepdims=True)
            dgc = (jnp.sum(e, axis=1, keepdims=True) - jnp.sum(e.T, axis=1, keepdims=True)
                   + jnp.sum(dbw * c.bw, axis=1, keepdims=True) + jnp.sum(dqd * c.qd, axis=1, keepdims=True) - t_kd)
            dgc = dgc + jnp.where(last, jnp.sum(t_kd, axis=0, keepdims=True) + dgl * c.gl, 0.0)
            yield
            dg = _dotx(upper, jnp.broadcast_to(dgc, (C, SM_W)))
            dgb_ref[hh] = jnp.where(lane == h, dg, jnp.where(lane == h + GDN_HEADS, dbeta, 0.0))

        _round_robin(head(hh) for hh in range(GDN_HB))

    rev = lambda n: N - 1 - n
    q_spec, k_spec, v_spec, gb_spec, o_spec, s_spec, t_spec = _gdn_specs(rev)
    dgb_spec = pl.BlockSpec((GDN_HB, C, SM_W), lambda g, n: (g, rev(n), 0))
    return pl.pallas_call(
        body, name=name, grid=(GDN_HG, N),
        in_specs=[q_spec, k_spec, v_spec, gb_spec, o_spec, s_spec, t_spec],
        out_specs=[pl.BlockSpec((C, 2 * GDN_QK + GDN_V), lambda g, n: (rev(n), 0)), dgb_spec],
        out_shape=[jax.ShapeDtypeStruct((M, 2 * GDN_QK + GDN_V), F32),
                   jax.ShapeDtypeStruct((GDN_HEADS, M, SM_W), F32)],
        scratch_shapes=[pltpu.VMEM((GDN_HB, GDN_DK, GDN_DV), F32)],
        compiler_params=_params("parallel", "arbitrary"),
    )(act, act, act, gb, do, s_all, t_all)


GLA_STEP_ROWS = 64
GLA_SUB = GLA_STEP_ROWS // GLA_CHUNK


def _gla_cumsum(la):
    C = GLA_CHUNK
    ltri = (lax.broadcasted_iota(jnp.int32, (C, C), 0) >= lax.broadcasted_iota(jnp.int32, (C, C), 1)).astype(F32)
    return _dotx(ltri, la)


def _gla_decay_rows(b, i):
    rj = lax.broadcasted_iota(jnp.int32, (GLA_CHUNK, GLA_DK), 0)
    return jnp.where(rj <= i, jnp.exp(jnp.minimum(b[i:i + 1, :] - b, 0.0)), 0.0)


def _gla_scores_t(q, k, b):
    C = GLA_CHUNK
    lane = lax.broadcasted_iota(jnp.int32, (C, C), 1)
    st = jnp.zeros((C, C), F32)
    for i in range(C):
        si = jnp.sum(q[i:i + 1, :] * k * _gla_decay_rows(b, i), axis=1, keepdims=True)
        st = jnp.where(lane == i, si, st)
        if i % 4 == 3:
            yield
    return st


def _gla_specs(n_of):
    R = GLA_STEP_ROWS
    q_spec = pl.BlockSpec((R, GLA_QK), lambda n: (n_of(n), C_GQ // GLA_QK))
    k_spec = pl.BlockSpec((R, GLA_QK), lambda n: (n_of(n), C_GK // GLA_QK))
    v_spec = pl.BlockSpec((R, GLA_V), lambda n: (n_of(n), C_GV // GLA_V))
    la_spec = pl.BlockSpec((R, GLA_QK), lambda n: (n_of(n), 0))
    o_spec = pl.BlockSpec((R, GLA_V), lambda n: (n_of(n), 0))
    s_spec = pl.BlockSpec((GLA_HEADS, None, GLA_SUB, GLA_DV, GLA_DK), lambda n: (0, n_of(n), 0, 0, 0))
    return q_spec, k_spec, v_spec, la_spec, o_spec, s_spec


def _gla_fwd(proj, la, *, name):
    M = proj.shape[0]
    N = M // GLA_STEP_ROWS
    C = GLA_CHUNK

    def body(q_ref, k_ref, v_ref, la_ref, o_ref, s_ref, state):
        n = pl.program_id(0)

        @pl.when(n == 0)
        def _():
            state[...] = jnp.zeros_like(state)

        def head(hh):
            kc = slice(hh * GLA_DK, (hh + 1) * GLA_DK)
            vc = slice(hh * GLA_DV, (hh + 1) * GLA_DV)
            st = state[hh]
            for c in range(GLA_SUB):
                rows = slice(c * C, (c + 1) * C)
                q = q_ref[rows, kc] * (GLA_DK ** -0.5)
                k = k_ref[rows, kc]
                v = v_ref[rows, vc]
                b = _gla_cumsum(la_ref[rows, kc])
                yield
                s_ref[hh, c] = st
                blast = b[C - 1:C, :]
                sc_t = yield from _gla_scores_t(q, k, b)
                o1 = _dot(q * jnp.exp(b), st, NT)
                kv = _dot(v, k * jnp.exp(blast - b), TN)
                o2 = _dot(sc_t, v, TN)
                yield
                o_ref[rows, vc] = o1 + o2
                st = st * jnp.exp(blast) + kv
            state[hh] = st

        _round_robin(head(hh) for hh in range(GLA_HEADS))

    q_spec, k_spec, v_spec, la_spec, o_spec, s_spec = _gla_specs(lambda n: n)
    return pl.pallas_call(
        body, name=name, grid=(N,),
        in_specs=[q_spec, k_spec, v_spec, la_spec], out_specs=[o_spec, s_spec],
        out_shape=[jax.ShapeDtypeStruct((M, GLA_V), F32),
                   jax.ShapeDtypeStruct((GLA_HEADS, N, GLA_SUB, GLA_DV, GLA_DK), F32)],
        scratch_shapes=[pltpu.VMEM((GLA_HEADS, GLA_DV, GLA_DK), F32)],
        compiler_params=_params("arbitrary"),
    )(proj, proj, proj, la)


def _gla_bwd(proj, la, do, s_all, d_proj, *, name):
    M = proj.shape[0]
    N = M // GLA_STEP_ROWS
    C = GLA_CHUNK
    qkv_w = 2 * GLA_QK + GLA_V
    assert C_GK == C_GQ + GLA_QK and C_GV == C_GK + GLA_QK and C_GQ % qkv_w == 0

    def body(q_ref, k_ref, v_ref, la_ref, do_ref, s_ref, _, dp_ref, dla_ref, dstate):
        n = pl.program_id(0)

        @pl.when(n == 0)
        def _():
            dstate[...] = jnp.zeros_like(dstate)

        lane = lax.broadcasted_iota(jnp.int32, (C, C), 1)
        ri = lax.broadcasted_iota(jnp.int32, (C, GLA_DK), 0)
        upper = (lax.broadcasted_iota(jnp.int32, (C, C), 0) <= lane).astype(F32)
        def head(hh):
            kc = slice(hh * GLA_DK, (hh + 1) * GLA_DK)
            vc = slice(hh * GLA_DV, (hh + 1) * GLA_DV)
            ds1 = dstate[hh]
            for c in reversed(range(GLA_SUB)):
                rows = slice(c * C, (c + 1) * C)
                q = q_ref[rows, kc] * (GLA_DK ** -0.5)
                k = k_ref[rows, kc]
                v = v_ref[rows, vc]
                b = _gla_cumsum(la_ref[rows, kc])
                do_ = do_ref[rows, vc]
                st = s_ref[hh, c]
                dsc_t = _dot(v, do_, NT)
                dqe = _dot(do_, st)
                dke = _dot(v, ds1)
                yield
                blast = b[C - 1:C, :]
                eb = jnp.exp(b)
                elast = jnp.exp(blast - b)
                eblast = jnp.exp(blast)
                qe = q * eb
                ke = k * elast
                dv2 = _dot(ke, ds1, NT)
                ds_new = _dot(do_, qe, TN)
                deblast = jnp.sum(st * ds1, axis=0, keepdims=True)
                sc_t = jnp.zeros((C, C), F32)
                dq_sc = jnp.zeros((C, GLA_DK), F32)
                dk_sc = jnp.zeros((C, GLA_DK), F32)
                for i in range(C):
                    f = _gla_decay_rows(b, i)
                    kf = k * f
                    si = jnp.sum(q[i:i + 1, :] * kf, axis=1, keepdims=True)
                    sc_t = jnp.where(lane == i, si, sc_t)
                    dsi = jnp.sum(jnp.where(lane == i, dsc_t, 0.0), axis=1, keepdims=True)
                    dq_sc = jnp.where(ri == i, jnp.sum(dsi * kf, axis=0, keepdims=True), dq_sc)
                    dk_sc = dk_sc + (dsi * f) * q[i:i + 1, :]
                    if i % 4 == 3:
                        yield
                dv1 = _dot(sc_t, do_)
                dp_ref[rows, kc] = ((dq_sc + dqe * eb) * (GLA_DK ** -0.5)).astype(dp_ref.dtype)
                dp_ref[rows, GLA_QK + hh * GLA_DK:GLA_QK + (hh + 1) * GLA_DK] = (dk_sc + dke * elast).astype(dp_ref.dtype)
                t_ke = dke * ke
                db = q * dq_sc - k * dk_sc + dqe * qe - t_ke
                db = db + jnp.where(ri == C - 1, jnp.sum(t_ke, axis=0, keepdims=True) + deblast * eblast, 0.0)
                dla = _dotx(upper, db)
                yield
                dp_ref[rows, 2 * GLA_QK + hh * GLA_DV:2 * GLA_QK + (hh + 1) * GLA_DV] = (dv1 + dv2).astype(dp_ref.dtype)
                dla_ref[rows, kc] = dla
                ds1 = ds1 * eblast + ds_new
            dstate[hh] = ds1

        _round_robin(head(hh) for hh in range(GLA_HEADS))

    rev = lambda n: N - 1 - n
    q_spec, k_spec, v_spec, la_spec, o_spec, s_spec = _gla_specs(rev)
    return pl.pallas_call(
        body, name=name, grid=(N,),
        in_specs=[q_spec, k_spec, v_spec, la_spec, o_spec, s_spec, _ANY],
        out_specs=[pl.BlockSpec((GLA_STEP_ROWS, qkv_w), lambda n: (rev(n), C_GQ // qkv_w)), la_spec],
        out_shape=[jax.ShapeDtypeStruct(d_proj.shape, d_proj.dtype), jax.ShapeDtypeStruct((M, GLA_QK), F32)],
        input_output_aliases={6: 0},
        scratch_shapes=[pltpu.VMEM((GLA_HEADS, GLA_DV, GLA_DK), F32)],
        compiler_params=_params("arbitrary"),
    )(proj, proj, proj, la, do, s_all, d_proj)


def _head_norm(o, wn):
    r = lax.rsqrt(jnp.mean(o * o, axis=-1, keepdims=True) + NORM_EPS)
    return o * r, r


def _mix_heads():
    heads = [(0, GDN_DV, hh * GDN_DV, hh * GDN_DV) for hh in range(GDN_HEADS)]
    heads += [(1, GLA_DV, GDN_V + hh * GLA_DV, hh * GLA_DV) for hh in range(GLA_HEADS)]
    return heads


def _mix_fwd(o_gdn, o_gla, proj, wn_gdn, wn_gla, *, name):
    M = proj.shape[0]
    tm = _tile(M, 344, 16)

    def body(og_ref, ol_ref, z_ref, r_ref, wg_ref, wl_ref, m_ref):
        srcs = ((og_ref, z_ref, wg_ref), (ol_ref, r_ref, wl_ref))
        for grp, width, mcol, col in _mix_heads():
            o_ref, gate_ref, w_ref = srcs[grp]
            xhat, _ = _head_norm(o_ref[:, col:col + width], None)
            gate, _ = _silu_and_grad(gate_ref[:, col:col + width])
            m_ref[:, mcol:mcol + width] = (xhat * w_ref[...] * gate).astype(m_ref.dtype)

    full = lambda s: pl.BlockSpec(s, lambda i: (0, 0))
    return pl.pallas_call(
        body, name=name, grid=(M // tm,),
        in_specs=[pl.BlockSpec((tm, GDN_V), lambda i: (i, 0)), pl.BlockSpec((tm, GLA_V), lambda i: (i, 0)),
                  pl.BlockSpec((tm, GDN_V), lambda i: (i, C_Z // GDN_V)),
                  pl.BlockSpec((tm, GLA_V), lambda i: (i, C_GR // GLA_V)),
                  full((1, GDN_DV)), full((1, GLA_DV))],
        out_specs=pl.BlockSpec((tm, D_MODEL), lambda i: (i, 0)),
        out_shape=jax.ShapeDtypeStruct((M, D_MODEL), BF16),
        compiler_params=_params("parallel"),
    )(o_gdn, o_gla, proj, proj, wn_gdn, wn_gla)


def _mix_bwd(o_gdn, o_gla, proj, wn_gdn, wn_gla, dmixed, *, name):
    M = proj.shape[0]
    tm = _tile(M, 344, 16)
    g_ = M // tm
    assert C_Z == 0 and C_GR == GDN_V

    def body(og_ref, ol_ref, z_ref, r_ref, wg_ref, wl_ref, dm_ref,
             dog_ref, dol_ref, dzr_ref, dwg_ref, dwl_ref):
        i = pl.program_id(0)
        srcs = ((og_ref, z_ref, wg_ref, dog_ref), (ol_ref, r_ref, wl_ref, dol_ref))
        dws = [jnp.zeros((1, GDN_DV), F32), jnp.zeros((1, GLA_DV), F32)]
        for grp, width, mcol, col in _mix_heads():
            o_ref, gate_ref, w_ref, do_ref = srcs[grp]
            cols = slice(col, col + width)
            xhat, r = _head_norm(o_ref[:, cols], None)
            gate, dgate_dc = _silu_and_grad(gate_ref[:, cols])
            dm = dm_ref[:, mcol:mcol + width]
            dzr_ref[:, mcol:mcol + width] = (dm * xhat * w_ref[...] * dgate_dc).astype(dzr_ref.dtype)
            dnorm = dm * gate
            dws[grp] = dws[grp] + jnp.sum(dnorm * xhat, axis=0, keepdims=True)
            dxhat = dnorm * w_ref[...]
            do_ref[:, cols] = r * (dxhat - xhat * jnp.mean(dxhat * xhat, axis=-1, keepdims=True))

        @pl.when(i == 0)
        def _():
            dwg_ref[...] = dws[0]
            dwl_ref[...] = dws[1]

        @pl.when(i > 0)
        def _():
            dwg_ref[...] += dws[0]
            dwl_ref[...] += dws[1]

    full = lambda s: pl.BlockSpec(s, lambda i: (0, 0))
    half = pl.BlockSpec((tm, GDN_V), lambda i: (i, 0))
    return pl.pallas_call(
        body, name=name, grid=(g_,),
        in_specs=[half, half, pl.BlockSpec((tm, GDN_V), lambda i: (i, C_Z // GDN_V)),
                  pl.BlockSpec((tm, GLA_V), lambda i: (i, C_GR // GLA_V)),
                  full((1, GDN_DV)), full((1, GLA_DV)), pl.BlockSpec((tm, D_MODEL), lambda i: (i, 0))],
        out_specs=[half, half, pl.BlockSpec((tm, GDN_V + GLA_V), lambda i: (i, 0)),
                   full((1, GDN_DV)), full((1, GLA_DV))],
        out_shape=[jax.ShapeDtypeStruct((M, GDN_V), F32), jax.ShapeDtypeStruct((M, GLA_V), F32),
                   jax.ShapeDtypeStruct((M, D_PROJ), BF16),
                   jax.ShapeDtypeStruct((1, GDN_DV), F32), jax.ShapeDtypeStruct((1, GLA_DV), F32)],
        compiler_params=_params("arbitrary"),
    )(o_gdn, o_gla, proj, proj, wn_gdn, wn_gla, dmixed)


def _swiglu_fwd(n, w_gate_t, w_up_t, *, name, tm=1376, tn=512):
    M, D = n.shape
    F = w_gate_t.shape[0]
    tm, tn = _tile(M, tm, 16), _tile(F, tn, 128)

    def body(n_ref, wg_ref, wu_ref, g_ref, u_ref, a_ref):
        x = n_ref[...]
        g = _dot(x, wg_ref[...], NT)
        u = _dot(x, wu_ref[...], NT)
        s, _ = _silu_and_grad(g)
        g_ref[...] = g.astype(g_ref.dtype)
        u_ref[...] = u.astype(u_ref.dtype)
        a_ref[...] = (s * u).astype(a_ref.dtype)

    w_spec = pl.BlockSpec((tn, D), lambda i, j: (j, 0))
    o_spec = pl.BlockSpec((tm, tn), lambda i, j: (i, j))
    return pl.pallas_call(
        body, name=name, grid=(M // tm, F // tn),
        in_specs=[pl.BlockSpec((tm, D), lambda i, j: (i, 0)), w_spec, w_spec], out_specs=[o_spec] * 3,
        out_shape=[jax.ShapeDtypeStruct((M, F), BF16)] * 3, compiler_params=_params("parallel", "parallel"),
    )(n, w_gate_t, w_up_t)


def _swiglu_bwd(dh, w_down, gate, up, *, name, after=None, tm=1376, tn=512):
    M, D = dh.shape
    F = w_down.shape[0]
    tm, tn = _tile(M, tm, 16), _tile(F, tn, 128)
    n_after = 0 if after is None else 1

    def body(*refs):
        dh_ref, w_ref, g_ref, u_ref, dg_ref, du_ref = refs[n_after:]
        da = _dot(dh_ref[...], w_ref[...], NT)
        s, ds = _silu_and_grad(g_ref[...].astype(F32))
        dg_ref[...] = (da * u_ref[...].astype(F32) * ds).astype(dg_ref.dtype)
        du_ref[...] = (da * s).astype(du_ref.dtype)

    o_spec = pl.BlockSpec((tm, tn), lambda i, j: (i, j))
    return pl.pallas_call(
        body, name=name, grid=(M // tm, F // tn),
        in_specs=[_ANY] * n_after + [pl.BlockSpec((tm, D), lambda i, j: (i, 0)),
                                     pl.BlockSpec((tn, D), lambda i, j: (j, 0)), o_spec, o_spec],
        out_specs=[o_spec, o_spec], out_shape=[jax.ShapeDtypeStruct((M, F), BF16)] * 2,
        compiler_params=_params("parallel", "parallel"),
    )(*((after,) if n_after else ()), dh, w_down, gate, up)


def _adamw(w, g, m, v, *, name):
    shape = w.shape
    cols = shape[-1]
    rows = w.size // cols
    w2, g2, m2, v2 = (t.reshape(rows, cols) for t in (w, g, m, v))
    if rows % 8 == 0 or cols % 128 != 0:
        tr, tc = (_tile(rows, 256, 8) if rows % 8 == 0 else rows), cols
    else:
        tr, tc = rows, _tile(cols, 256, 128)

    def body(w_ref, g_ref, m_ref, v_ref, d_ref, nm_ref, nv_ref):
        g_ = g_ref[...]
        nm = ADAM_B1 * m_ref[...] + (1.0 - ADAM_B1) * g_
        nv = ADAM_B2 * v_ref[...] + (1.0 - ADAM_B2) * (g_ * g_)
        m_hat = nm / (1.0 - ADAM_B1 ** ADAM_STEP)
        v_hat = nv / (1.0 - ADAM_B2 ** ADAM_STEP)
        d_ref[...] = -ADAM_LR * (m_hat / (jnp.sqrt(v_hat) + ADAM_EPS) + ADAM_WD * w_ref[...])
        nm_ref[...] = nm
        nv_ref[...] = nv

    blk = pl.BlockSpec((tr, tc), lambda i, j: (i, j))
    outs = pl.pallas_call(
        body, name=name, grid=(rows // tr, cols // tc), in_specs=[blk] * 4, out_specs=[blk] * 3,
        out_shape=[jax.ShapeDtypeStruct((rows, cols), F32)] * 3, compiler_params=_params("parallel", "parallel"),
    )(w2, g2, m2, v2)
    return tuple(t.reshape(shape) for t in outs)


def _sum_slabs(x, *, name):
    _, R, C = x.shape
    sub = 16 if x.dtype == BF16 else 8
    if R % sub == 0:
        tr, tc = _tile(R, 128, sub), C
    else:
        tr, tc = R, _tile(C, 256, 128)

    def body(x_ref, o_ref):
        acc = x_ref[0].astype(F32)
        for s in range(1, N_DEV):
            acc = acc + x_ref[s].astype(F32)
        o_ref[...] = acc

    return pl.pallas_call(
        body, name=name, grid=(R // tr, C // tc),
        in_specs=[pl.BlockSpec((N_DEV, tr, tc), lambda i, j: (0, i, j))],
        out_specs=pl.BlockSpec((tr, tc), lambda i, j: (i, j)),
        out_shape=jax.ShapeDtypeStruct((R, C), F32), compiler_params=_params("parallel", "parallel"),
    )(x)


def _peers():
    x, y, c = lax.axis_index("x"), lax.axis_index("y"), lax.axis_index("c")
    me = 4 * x + 2 * y + c
    peers = []
    for k in range(1, N_DEV):
        px = 1 - x if k & 4 else x
        py = 1 - y if k & 2 else y
        pc = 1 - c if k & 1 else c
        peers.append(((px, py, pc), 4 * px + 2 * py + pc))
    return me, peers


def _exchange(x, *, gather, name):
    slab = x.shape if gather else x.shape[1:]

    def body(x_ref, o_ref, send_sems, recv_sems, own_sem):
        me, peers = _peers()
        own = pltpu.make_async_copy(x_ref if gather else x_ref.at[me], o_ref.at[me], own_sem)
        own.start()
        sends, recvs = [], []
        for k, (pos, idx) in enumerate(peers):
            sends.append(pltpu.make_async_remote_copy(
                src_ref=x_ref if gather else x_ref.at[idx], dst_ref=o_ref.at[me],
                send_sem=send_sems.at[k], recv_sem=recv_sems.at[k],
                device_id=pos, device_id_type=pl.DeviceIdType.MESH))
            recvs.append(pltpu.make_async_remote_copy(
                src_ref=x_ref if gather else x_ref.at[idx], dst_ref=o_ref.at[idx],
                send_sem=send_sems.at[k], recv_sem=recv_sems.at[k],
                device_id=pos, device_id_type=pl.DeviceIdType.MESH))
        for cp in sends:
            cp.start()
        for cp in recvs:
            cp.wait_recv()
        for cp in sends:
            cp.wait_send()
        own.wait()

    hbm = pl.BlockSpec(memory_space=pltpu.HBM)
    return pl.pallas_call(
        body, name=name, in_specs=[hbm], out_specs=hbm,
        out_shape=jax.ShapeDtypeStruct((N_DEV,) + tuple(slab), x.dtype),
        scratch_shapes=[pltpu.SemaphoreType.DMA((N_DEV - 1,)), pltpu.SemaphoreType.DMA((N_DEV - 1,)),
                        pltpu.SemaphoreType.DMA],
    )(x)


_HBM = pl.BlockSpec(memory_space=pltpu.HBM)
_SEM = pl.BlockSpec(memory_space=pltpu.SEMAPHORE)
_EFFECT = pltpu.SideEffectType.DATAFLOW_SIDE_EFFECTING


def _exchange_start(x, *, gather, name, after=None):
    slab = x.shape if gather else x.shape[1:]
    n_after = 0 if after is None else 1

    def body(*refs):
        x_ref, land_ref, send_sems, recv_sems, _, _, token = refs[n_after:]
        me, peers = _peers()
        for k, (pos, idx) in enumerate(peers):
            pltpu.make_async_remote_copy(
                src_ref=x_ref if gather else x_ref.at[idx], dst_ref=land_ref.at[me],
                send_sem=send_sems.at[k], recv_sem=recv_sems.at[k],
                device_id=pos, device_id_type=pl.DeviceIdType.MESH).start()
        token[...] = jnp.zeros_like(token)

    land = lax.empty((N_DEV,) + tuple(slab), x.dtype)
    return pl.pallas_call(
        body, name=name,
        out_shape=(pltpu.SemaphoreType.DMA((N_DEV - 1,)), pltpu.SemaphoreType.DMA((N_DEV - 1,)),
                   pltpu.HBM(x.shape, x.dtype), pltpu.HBM(land.shape, land.dtype), jax.ShapeDtypeStruct((8, 128), F32)),
        in_specs=[_ANY] * n_after + [_HBM, _HBM],
        out_specs=(_SEM, _SEM, _HBM, _HBM, pl.BlockSpec(memory_space=pltpu.VMEM)),
        input_output_aliases={n_after: 2, n_after + 1: 3},
        compiler_params=pltpu.CompilerParams(has_side_effects=_EFFECT),
    )(*((after,) if n_after else ()), pltpu.with_memory_space_constraint(x, pltpu.HBM),
      pltpu.with_memory_space_constraint(land, pltpu.HBM))


def _exchange_wait(handle, after, *, gather, name):
    send_sems, recv_sems, x_thru, land_thru, _ = handle

    def body(x_ref, land_ref, send_sems, recv_sems, after_ref, x_out, land_out):
        me, peers = _peers()
        for k, (pos, idx) in enumerate(peers):
            cp = pltpu.make_async_remote_copy(
                src_ref=x_ref if gather else x_ref.at[idx], dst_ref=land_ref.at[idx],
                send_sem=send_sems.at[k], recv_sem=recv_sems.at[k],
                device_id=pos, device_id_type=pl.DeviceIdType.MESH)
            cp.wait_send()
            cp.wait_recv()

    return pl.pallas_call(
        body, name=name,
        out_shape=(pltpu.HBM(x_thru.shape, x_thru.dtype), pltpu.HBM(land_thru.shape, land_thru.dtype)),
        in_specs=(_HBM, _HBM, _SEM, _SEM, _ANY), out_specs=(_HBM, _HBM), input_output_aliases={0: 0, 1: 1},
        compiler_params=pltpu.CompilerParams(has_side_effects=_EFFECT),
    )(x_thru, land_thru, send_sems, recv_sems, after)


W_IN_SLAB = D_IN // N_DEV


def _to_proj_rows(t):
    z = jnp.zeros((D_PROJ - C_SM - 2 * GDN_HEADS - GLA_RANK,) + t.shape[1:], t.dtype)
    return jnp.concatenate([t[R_Z:R_A], t[R_GR:R_LR], t[R_GQ:R_GR], t[:R_Z], t[R_A:R_GQ], t[R_LR:], z], axis=0)


def _from_proj_rows(t):
    ab = C_SM + 2 * GDN_HEADS
    return jnp.concatenate([t[C_QKV:C_SM], t[C_Z:C_GR], t[C_SM:ab], t[C_GQ:C_QKV], t[C_GR:C_GQ],
                            t[ab:ab + GLA_RANK]], axis=0)


def _local_step(x, target, meta, attn_nw, conv_w, a_log, dt_bias, gdn_nw, w2, b2, gla_nw, ffn_nw, final_nw,
                fetch, emit, start=None):
    S = x.shape[0]
    h0 = jnp.concatenate([jnp.zeros((ROW_PAD, D_MODEL), F32), meta, x], axis=0)
    target_p = jnp.concatenate([jnp.zeros((HEAD_ROWS, D_MODEL), F32), target], axis=0)
    conv_w8 = jnp.concatenate([conv_w, jnp.zeros((8 - CONV_K, conv_w.shape[1]), F32)], axis=0)
    w2p = jnp.zeros((SM_W, GLA_QK), F32).at[2 * GDN_HEADS:2 * GDN_HEADS + GLA_RANK].set(w2)
    alog_p = jnp.zeros((1, SM_W), F32).at[:, :GDN_HEADS].set(a_log)
    dt_p = jnp.zeros((1, SM_W), F32).at[:, :GDN_HEADS].set(dt_bias)

    n1 = _rmsnorm_fwd(h0, attn_nw, name="attn_norm", after=start)
    w_in_t = fetch("w_in_t", n1)
    proj = _matmul(n1, w_in_t, mode="nt", name="in_proj")
    gb, la = _gates_fwd(proj, w2p, b2, alog_p, dt_p, name="gates")
    act = _prep_fwd(proj, conv_w8, name="gdn_prep")
    o_gdn, s_gdn, t_gdn = _gdn_fwd(act, gb, name="gdn_fwd")
    o_gla, s_gla = _gla_fwd(proj, la, name="gla_fwd")
    mixed = _mix_fwd(o_gdn, o_gla, proj, gdn_nw, gla_nw, name="mix")
    w_gate_t, w_up_t, w_out, w_down = fetch("rest", mixed)
    h1 = _matmul(mixed, w_out, mode="nn", add=h0, name="out_proj")
    n2 = _rmsnorm_fwd(h1, ffn_nw, name="ffn_norm")
    gate, up, hid = _swiglu_fwd(n2, w_gate_t, w_up_t, name="swiglu")
    h2 = _matmul(hid, w_down, mode="nn", add=h1, name="ffn_down", tm=688, tk=D_FF)
    dh2, dh2_b, d_final_nw, loss = _loss_head(h2, final_nw, target_p, name="loss_head")

    wg = dict(mode="tn", out_dtype=BF16, tn=512, tk=S + HEAD_ROWS)
    tok = emit("w_down", _matmul(hid, dh2_b, name="d_w_down", tm=704, **wg))
    d_gate, d_up = _swiglu_bwd(dh2_b, w_down, gate, up, name="d_swiglu", after=tok)
    tok = emit("w_gate_t", _matmul(d_gate, n2, name="d_w_gate", tm=704, **wg))
    tok = emit("w_up_t", _matmul(d_up, n2, name="d_w_up", tm=704, after=tok, **wg))
    d_n2 = _matmul(d_gate, w_gate_t, mode="nn", name="d_n2_gate", tm=688, tk=D_FF, after=tok)
    d_n2 = _matmul(d_up, w_up_t, mode="nn", add=d_n2, name="d_n2_up", tm=688, tk=D_FF)
    dh1, dh1_b, d_ffn_nw = _rmsnorm_bwd(h1, ffn_nw, d_n2, dh2, name="d_ffn_norm", also_bf16=True)

    tok = emit("w_out", _matmul(mixed, dh1_b, name="d_w_out", tm=512, **wg))
    d_mixed = _matmul(dh1_b, w_out, mode="nt", name="d_mixed", after=tok)
    do_gdn, do_gla, d_proj, d_gdn_nw, d_gla_nw = _mix_bwd(o_gdn, o_gla, proj, gdn_nw, gla_nw, d_mixed, name="d_mix")
    d_proj, d_la = _gla_bwd(proj, la, do_gla, s_gla, d_proj, name="gla_bwd")
    dact, dgb_heads = _gdn_bwd(act, gb, do_gdn, s_gdn, t_gdn, name="gdn_bwd")
    d_proj, d_w2p, d_b2, d_alog, d_dt = _gates_bwd(proj, w2p, b2, alog_p, dt_p, dgb_heads, d_la, d_proj, name="d_gates")
    dc, d_conv_w8 = _prep_bwd_a(proj, conv_w8, dact, name="d_gdn_prep")
    d_proj = _prep_bwd_b(dc, conv_w8, d_proj, name="d_conv")
    tok = emit("w_in_t", _matmul(d_proj, n1, name="d_w_in", tm=768, **wg))
    d_n1 = _matmul(d_proj, w_in_t, mode="nn", name="d_n1", tm=688, tk=D_PROJ, after=tok)
    dh0, d_attn_nw = _rmsnorm_bwd(h0, attn_nw, d_n1, dh1, name="d_attn_norm", also_bf16=False)

    return dict(
        loss=loss[0, 0], grad_x=dh0[HEAD_ROWS:], meta=dh0[ROW_PAD:HEAD_ROWS], attn_nw=d_attn_nw,
        conv_w=d_conv_w8[:CONV_K], a_log=d_alog[:, :GDN_HEADS], dt_bias=d_dt[:, :GDN_HEADS], gdn_nw=d_gdn_nw,
        w2=d_w2p[2 * GDN_HEADS:2 * GDN_HEADS + GLA_RANK], b2=d_b2, gla_nw=d_gla_nw, ffn_nw=d_ffn_nw,
        final_nw=d_final_nw)


SMALL_ROWS = 32


def kernel(x, meta_tokens, attn_norm_w, w_in, gdn_conv_w, gdn_a_log, gdn_dt_bias, gdn_norm_w, gla_gate_w2, gla_gate_b, gla_norm_w, w_out, ffn_norm_w, w_gate, w_up, w_down, final_norm_w, loss_target, m_meta_tokens, m_attn_norm_w, m_w_in, m_gdn_conv_w, m_gdn_a_log, m_gdn_dt_bias, m_gdn_norm_w, m_gla_gate_w2, m_gla_gate_b, m_gla_norm_w, m_w_out, m_ffn_norm_w, m_w_gate, m_w_up, m_w_down, m_final_norm_w, v_meta_tokens, v_attn_norm_w, v_w_in, v_gdn_conv_w, v_gdn_a_log, v_gdn_dt_bias, v_gdn_norm_w, v_gla_gate_w2, v_gla_gate_b, v_gla_norm_w, v_w_out, v_ffn_norm_w, v_w_gate, v_w_up, v_w_down, v_final_norm_w):
    me = 4 * lax.axis_index("x") + 2 * lax.axis_index("y") + lax.axis_index("c")
    n_in, n_ff, n_out = D_IN // N_DEV, D_FF // N_DEV, D_MODEL // N_DEV

    n_conv = gdn_conv_w.shape[2]
    n_w2 = gla_gate_w2.shape[2]
    n_meta = meta_tokens.shape[1]
    small = jnp.zeros((40, n_conv), F32)
    small = small.at[0:N_META, :n_meta].set(meta_tokens)
    small = small.at[N_META:N_META + CONV_K, :].set(gdn_conv_w[0])
    small = small.at[24:24 + GLA_RANK, :n_w2].set(gla_gate_w2[0])
    small_all = _exchange(small, gather=True, name="gather_small")
    meta_f = small_all[:, 0:N_META, :n_meta].transpose(1, 0, 2).reshape(N_META, D_MODEL)
    conv_f = small_all[:, N_META:N_META + CONV_K, :].transpose(1, 0, 2).reshape(CONV_K, N_DEV * n_conv)
    w2_f = small_all[:, 24:24 + GLA_RANK, :n_w2].transpose(1, 0, 2).reshape(GLA_RANK, N_DEV * n_w2)

    o1, o2, o3 = n_ff, 2 * n_ff, 2 * n_ff + n_out
    in_h = _exchange_start(w_in[0].T.astype(BF16), gather=True, name="gather_w_in_start")
    rest = jnp.concatenate([w_gate[0].T, w_up[0].T, w_out[0], w_down[0]], axis=0).astype(BF16)
    rest_h = _exchange_start(rest, gather=True, name="gather_rest_start", after=in_h[4])

    def fetch(name, after):
        handle = in_h if name == "w_in_t" else rest_h
        own, got = _exchange_wait(handle, after, gather=True, name="gather_" + name + "_wait")
        got = lax.dynamic_update_index_in_dim(got, own, me, 0)
        if name == "w_in_t":
            return _to_proj_rows(got.reshape(D_IN, D_MODEL))
        return (got[:, :o1].reshape(D_FF, D_MODEL), got[:, o1:o2].reshape(D_FF, D_MODEL),
                got[:, o2:o3].reshape(D_MODEL, D_MODEL), got[:, o3:].reshape(D_FF, D_MODEL))

    sent = {}

    def emit(name, grad):
        if name == "w_in_t":
            grad = _from_proj_rows(grad)
        parts = grad.reshape(N_DEV, grad.shape[0] // N_DEV, D_MODEL)
        sent[name] = _exchange_start(parts, gather=False, name="scatter_" + name + "_start")
        return sent[name][4]

    g = _local_step(x[0], loss_target[0], meta_f, attn_norm_w, conv_f, gdn_a_log, gdn_dt_bias, gdn_norm_w, w2_f,
                    gla_gate_b, gla_norm_w, ffn_norm_w, final_norm_w.reshape(1, D_MODEL), fetch, emit, start=rest_h[4])

    def total(name, after):
        handle = sent[name]
        own, got = _exchange_wait(handle, after, gather=False, name="scatter_" + name + "_wait")
        got = lax.dynamic_update_index_in_dim(got, lax.dynamic_index_in_dim(own, me, 0, keepdims=False), me, 0)
        return _sum_slabs(got, name="sum_" + name)

    grad_w_down = total("w_down", g["attn_nw"])[None]
    grad_w_gate = total("w_gate_t", grad_w_down)
    grad_w_up = total("w_up_t", grad_w_gate)
    grad_w_out = total("w_out", grad_w_up)[None]
    grad_w_in = total("w_in_t", grad_w_out)

    misc = jnp.concatenate([g["a_log"], g["dt_bias"], g["gdn_nw"], g["gla_nw"], g["b2"], g["loss"].reshape(1, 1)], axis=1)
    n_misc = misc.shape[1]
    misc = jnp.pad(misc, ((0, 0), (0, D_MODEL - n_misc)))
    rows = jnp.concatenate([g["attn_nw"], g["ffn_nw"], g["final_nw"], misc, g["meta"],
                            g["conv_w"].reshape(-1, D_MODEL), g["w2"].reshape(-1, D_MODEL)], axis=0)
    rows = jnp.pad(rows, ((0, SMALL_ROWS - rows.shape[0]), (0, 0)))
    tot = _sum_slabs(_exchange(rows, gather=True, name="gather_small_grads"), name="sum_small_grads")
    grad_attn_nw, grad_ffn_nw, grad_final_nw = tot[0:1], tot[1:2], tot[2]
    grad_a_log = tot[3:4, 0:8]
    grad_dt = tot[3:4, 8:16]
    grad_gdn_nw = tot[3:4, 16:16 + GDN_DV]
    grad_gla_nw = tot[3:4, 144:144 + GLA_DV]
    grad_b2 = tot[3:4, 400:400 + GLA_QK]
    loss = tot[3, n_misc - 1]
    r0 = 4 + N_META
    grad_meta = lax.dynamic_slice(tot[4:r0], (0, me * n_meta), (N_META, n_meta))
    r1 = r0 + CONV_K * N_DEV * n_conv // D_MODEL
    grad_conv = lax.dynamic_slice(tot[r0:r1].reshape(CONV_K, N_DEV * n_conv), (0, me * n_conv), (CONV_K, n_conv))[None]
    r2 = r1 + GLA_RANK * N_DEV * n_w2 // D_MODEL
    grad_w2 = lax.dynamic_slice(tot[r1:r2].reshape(GLA_RANK, N_DEV * n_w2), (0, me * n_w2), (GLA_RANK, n_w2))[None]

    weights = [meta_tokens, attn_norm_w, w_in, gdn_conv_w, gdn_a_log, gdn_dt_bias, gdn_norm_w, gla_gate_w2,
               gla_gate_b, gla_norm_w, w_out, ffn_norm_w, w_gate, w_up, w_down, final_norm_w]
    grads = [grad_meta, grad_attn_nw, grad_w_in, grad_conv, grad_a_log, grad_dt, grad_gdn_nw, grad_w2,
             grad_b2, grad_gla_nw, grad_w_out, grad_ffn_nw, grad_w_gate, grad_w_up, grad_w_down, grad_final_nw]
    ms = [m_meta_tokens, m_attn_norm_w, m_w_in, m_gdn_conv_w, m_gdn_a_log, m_gdn_dt_bias, m_gdn_norm_w,
          m_gla_gate_w2, m_gla_gate_b, m_gla_norm_w, m_w_out, m_ffn_norm_w, m_w_gate, m_w_up, m_w_down, m_final_norm_w]
    vs = [v_meta_tokens, v_attn_norm_w, v_w_in, v_gdn_conv_w, v_gdn_a_log, v_gdn_dt_bias, v_gdn_norm_w,
          v_gla_gate_w2, v_gla_gate_b, v_gla_norm_w, v_w_out, v_ffn_norm_w, v_w_gate, v_w_up, v_w_down, v_final_norm_w]
    transposed = (2, 12, 13)
    outs = [[], [], [], []]
    for idx, (w, gr, m, v) in enumerate(zip(weights, grads, ms, vs)):
        if idx in transposed:
            res = (gr,) + _adamw(w[0].T, gr, m[0].T, v[0].T, name=f"adamw_{idx}")
            res = [t.T[None] for t in res]
        else:
            gr = gr.reshape(w.shape)
            res = (gr,) + _adamw(w, gr, m, v, name=f"adamw_{idx}")
        for lst, t in zip(outs, res):
            lst.append(t)
    return (loss, g["grad_x"][None], *outs[0], *outs[1], *outs[2], *outs[3])
```

```python
import functools

import jax
import jax.numpy as jnp
from jax import lax
from jax.experimental import pallas as pl
from jax.experimental.pallas import tpu as pltpu

F32 = jnp.float32
BF16 = jnp.bfloat16
_MXU_DTYPE = jnp.bfloat16

D_MODEL = 2048
N_META = 16
ROW_PAD = 48
HEAD_ROWS = ROW_PAD + N_META
CONV_K = 4
GDN_HEADS, GDN_DK, GDN_DV, GDN_CHUNK = 8, 128, 128, 64
GLA_HEADS, GLA_DK, GLA_DV, GLA_CHUNK = 4, 128, 256, 16
GLA_RANK = 16
GLA_GATE_NORMALIZER = 16.0
GDN_QK = GDN_HEADS * GDN_DK
GDN_V = GDN_HEADS * GDN_DV
GLA_QK = GLA_HEADS * GLA_DK
GLA_V = GLA_HEADS * GLA_DV
D_FF = 5632
D_IN = 7200
NORM_EPS = 1e-6
C_Z, C_GR, C_GQ, C_GK, C_GV, C_QKV, C_SM = 0, 1024, 2048, 2560, 3072, 4096, 7168
SM_W = 128
D_PROJ = 7680
R_Z, R_A, R_B, R_GQ, R_GK, R_GV, R_GR, R_LR = 3072, 4096, 4104, 4112, 4624, 5136, 6160, 7184

ADAM_LR, ADAM_B1, ADAM_B2, ADAM_EPS, ADAM_WD, ADAM_STEP = 0.001, 0.9, 0.999, 1e-08, 0.01, 10

N_DEV = 8
VMEM_LIMIT = 56 * 1024 * 1024

NN = (((1,), (0,)), ((), ()))
NT = (((1,), (1,)), ((), ()))
TN = (((0,), (0,)), ((), ()))


def _dot(a, b, dims=NN):
    return lax.dot_general(a.astype(_MXU_DTYPE), b.astype(_MXU_DTYPE), dims, preferred_element_type=F32)


def _dotx(a, b, dims=NN):
    return lax.dot_general(a, b, dims, precision=lax.Precision.HIGHEST, preferred_element_type=F32)


def _dot3(a, b):
    ah = a.astype(BF16)
    al = (a - ah.astype(F32)).astype(BF16)
    bh = b.astype(BF16)
    bl = (b - bh.astype(F32)).astype(BF16)
    d = functools.partial(lax.dot_general, dimension_numbers=NN, preferred_element_type=F32)
    return d(ah, bh) + (d(ah, bl) + d(al, bh))


def _tile(n, target, mult=8):
    best = None
    for t in range(mult, min(n, target) + 1, mult):
        if n % t == 0:
            best = t
    return best if best is not None else n


def _params(*sem):
    return pltpu.CompilerParams(dimension_semantics=sem, vmem_limit_bytes=VMEM_LIMIT)


def _sigmoid(x):
    return 0.5 * jnp.tanh(0.5 * x) + 0.5


def _softplus(x):
    return jnp.maximum(x, 0.0) + jnp.log1p(jnp.exp(-jnp.abs(x)))


def _silu_and_grad(c):
    s = _sigmoid(c)
    return c * s, s * (1.0 + c * (1.0 - s))


_ANY = pl.BlockSpec(memory_space=pl.ANY)


def _matmul(a, b, *, mode, name, out_dtype=F32, add=None, after=None, tm=1376, tn=512, tk=2064):
    if mode == "tn":
        K, M = a.shape
        N = b.shape[1]
    else:
        M, K = a.shape
        N = b.shape[0] if mode == "nt" else b.shape[1]
    tm = _tile(M, tm, 128 if mode == "tn" else 16)
    tn = _tile(N, tn, 128)
    tk = _tile(K, tk, 16 if mode == "tn" else 128)
    gm, gn, gk = M // tm, N // tn, K // tk
    dims = {"nn": NN, "nt": NT, "tn": TN}[mode]

    n_after = 0 if after is None else 1

    def body(*refs):
        refs = refs[n_after:]
        if add is None:
            a_ref, b_ref, o_ref = refs[:3]
            add_ref = None
        else:
            a_ref, b_ref, add_ref, o_ref = refs[:4]
        p = _dot(a_ref[...], b_ref[...], dims)

        def finish(r):
            if add_ref is not None:
                r = r + add_ref[...]
            o_ref[...] = r.astype(out_dtype)

        if gk == 1:
            finish(p)
        else:
            acc_ref = refs[-1]
            k = pl.program_id(2)

            @pl.when(k == 0)
            def _():
                acc_ref[...] = p

            @pl.when(k > 0)
            def _():
                acc_ref[...] += p

            @pl.when(k == gk - 1)
            def _():
                finish(acc_ref[...])

    if mode == "tn":
        a_spec = pl.BlockSpec((tk, tm), lambda i, j, k: (k, i))
    else:
        a_spec = pl.BlockSpec((tm, tk), lambda i, j, k: (i, k))
    if mode == "nt":
        b_spec = pl.BlockSpec((tn, tk), lambda i, j, k: (j, k))
    else:
        b_spec = pl.BlockSpec((tk, tn), lambda i, j, k: (k, j))
    o_spec = pl.BlockSpec((tm, tn), lambda i, j, k: (i, j))
    in_specs = [_ANY] * n_after + [a_spec, b_spec] + ([o_spec] if add is not None else [])
    args = ((after,) if n_after else ()) + (a, b) + ((add,) if add is not None else ())
    return pl.pallas_call(
        body, name=name, grid=(gm, gn, gk), in_specs=in_specs, out_specs=o_spec,
        out_shape=jax.ShapeDtypeStruct((M, N), out_dtype),
        scratch_shapes=[pltpu.VMEM((tm, tn), F32)] if gk > 1 else [],
        compiler_params=_params("parallel", "parallel", "arbitrary"),
    )(*args)


def _rmsnorm_fwd(h, w, *, name, after=None):
    M, D = h.shape
    tm = _tile(M, 688, 16)
    n_after = 0 if after is None else 1

    def body(*refs):
        h_ref, w_ref, n_ref = refs[n_after:]
        x = h_ref[...]
        r = lax.rsqrt(jnp.mean(x * x, axis=-1, keepdims=True) + NORM_EPS)
        n_ref[...] = (x * r * w_ref[...]).astype(n_ref.dtype)

    return pl.pallas_call(
        body, name=name, grid=(M // tm,),
        in_specs=[_ANY] * n_after + [pl.BlockSpec((tm, D), lambda i: (i, 0)), pl.BlockSpec((1, D), lambda i: (0, 0))],
        out_specs=pl.BlockSpec((tm, D), lambda i: (i, 0)),
        out_shape=jax.ShapeDtypeStruct((M, D), BF16),
        compiler_params=_params("parallel"),
    )(*((after,) if n_after else ()), h, w)


def _rmsnorm_bwd(h, w, dn, dres, *, name, also_bf16):
    M, D = h.shape
    tm = _tile(M, 344, 16)
    g = M // tm

    def body(h_ref, w_ref, dn_ref, dres_ref, dh_ref, *rest):
        dhb_ref = rest[0] if also_bf16 else None
        dw_ref, acc_ref = rest[-2:]
        i = pl.program_id(0)
        x = h_ref[...]
        r = lax.rsqrt(jnp.mean(x * x, axis=-1, keepdims=True) + NORM_EPS)
        xhat = x * r
        dn_ = dn_ref[...]
        dxhat = dn_ * w_ref[...]
        dh = dres_ref[...] + r * (dxhat - xhat * jnp.mean(dxhat * xhat, axis=-1, keepdims=True))
        dh_ref[...] = dh
        if also_bf16:
            dhb_ref[...] = dh.astype(dhb_ref.dtype)
        part = jnp.sum((dn_ * xhat).reshape(tm // 8, 8, D), axis=0)

        @pl.when(i == 0)
        def _():
            acc_ref[...] = part

        @pl.when(i > 0)
        def _():
            acc_ref[...] += part

        @pl.when(i == g - 1)
        def _():
            dw_ref[...] = jnp.sum(acc_ref[...], axis=0, keepdims=True)

    row = pl.BlockSpec((tm, D), lambda i: (i, 0))
    vec = pl.BlockSpec((1, D), lambda i: (0, 0))
    return pl.pallas_call(
        body, name=name, grid=(g,), in_specs=[row, vec, row, row],
        out_specs=[row] + ([row] if also_bf16 else []) + [vec],
        out_shape=[jax.ShapeDtypeStruct((M, D), F32)] + ([jax.ShapeDtypeStruct((M, D), BF16)] if also_bf16 else [])
        + [jax.ShapeDtypeStruct((1, D), F32)],
        scratch_shapes=[pltpu.VMEM((8, D), F32)],
        compiler_params=_params("arbitrary"),
    )(h, w, dn, dres)


def _loss_head(h, w, target_p, *, name):
    M, D = h.shape
    tm = _tile(M, 344, 16)
    g = M // tm

    def body(h_ref, w_ref, t_ref, dh_ref, dhb_ref, dw_ref, loss_ref, acc_ref, lacc_ref):
        i = pl.program_id(0)
        x = h_ref[...]
        row = i * tm + lax.broadcasted_iota(jnp.int32, (tm, 1), 0)
        live = row >= HEAD_ROWS
        r = lax.rsqrt(jnp.mean(x * x, axis=-1, keepdims=True) + NORM_EPS)
        xhat = x * r
        err = jnp.where(live, xhat * w_ref[...] - t_ref[...], 0.0)
        dy = err * (1.0 / D)
        dxhat = dy * w_ref[...]
        dh = r * (dxhat - xhat * jnp.mean(dxhat * xhat, axis=-1, keepdims=True))
        dh_ref[...] = dh
        dhb_ref[...] = dh.astype(dhb_ref.dtype)
        part = jnp.sum((dy * xhat).reshape(tm // 8, 8, D), axis=0)
        lpart = jnp.sum((err * err).reshape(tm // 8, 8, D), axis=0)

        @pl.when(i == 0)
        def _():
            acc_ref[...] = part
            lacc_ref[...] = lpart

        @pl.when(i > 0)
        def _():
            acc_ref[...] += part
            lacc_ref[...] += lpart

        @pl.when(i == g - 1)
        def _():
            dw_ref[...] = jnp.sum(acc_ref[...], axis=0, keepdims=True)
            tot = jnp.sum(jnp.sum(lacc_ref[...], axis=0, keepdims=True), axis=1, keepdims=True)
            loss_ref[...] = jnp.broadcast_to(tot * (0.5 / D), (1, 128))

    row = pl.BlockSpec((tm, D), lambda i: (i, 0))
    vec = pl.BlockSpec((1, D), lambda i: (0, 0))
    return pl.pallas_call(
        body, name=name, grid=(g,), in_specs=[row, vec, row],
        out_specs=[row, row, vec, pl.BlockSpec((1, 128), lambda i: (0, 0))],
        out_shape=[jax.ShapeDtypeStruct((M, D), F32), jax.ShapeDtypeStruct((M, D), BF16),
                   jax.ShapeDtypeStruct((1, D), F32), jax.ShapeDtypeStruct((1, 128), F32)],
        scratch_shapes=[pltpu.VMEM((8, D), F32), pltpu.VMEM((8, D), F32)],
        compiler_params=_params("arbitrary"),
    )(h, w, target_p)


def _gate_terms(sm, w2p, b2, alog_p, dt_p, row0):
    tm = sm.shape[0]
    lane = lax.broadcasted_iota(jnp.int32, (tm, SM_W), 1)
    live = (row0 + lax.broadcasted_iota(jnp.int32, (tm, 1), 0)) >= ROW_PAD
    pre = sm + dt_p
    neg_a = -jnp.exp(alog_p)
    g = neg_a * _softplus(pre)
    beta = _sigmoid(sm)
    z = _dot(sm, w2p) + b2
    return lane, live, pre, neg_a, g, beta, z


def _gates_fwd(proj, w2p, b2, alog_p, dt_p, *, name):
    M = proj.shape[0]
    tm = _tile(M, 688, 8)

    def body(sm_ref, w2_ref, b2_ref, al_ref, dt_ref, gb_ref, la_ref):
        row0 = pl.program_id(0) * tm
        lane, live, _, _, g, beta, z = _gate_terms(sm_ref[...], w2_ref[...], b2_ref[...], al_ref[...], dt_ref[...], row0)
        gb = jnp.where(lane < GDN_HEADS, g, jnp.where(lane < 2 * GDN_HEADS, beta, 0.0))
        gb_ref[...] = jnp.where(live, gb, 0.0)
        la = (jnp.minimum(z, 0.0) - jnp.log1p(jnp.exp(-jnp.abs(z)))) * (1.0 / GLA_GATE_NORMALIZER)
        la_ref[...] = jnp.where(live, la, 0.0)

    full = lambda s: pl.BlockSpec(s, lambda i: (0, 0))
    return pl.pallas_call(
        body, name=name, grid=(M // tm,),
        in_specs=[pl.BlockSpec((tm, SM_W), lambda i: (i, C_SM // SM_W)), full((SM_W, GLA_QK)), full((1, GLA_QK)),
                  full((1, SM_W)), full((1, SM_W))],
        out_specs=[pl.BlockSpec((tm, SM_W), lambda i: (i, 0)), pl.BlockSpec((tm, GLA_QK), lambda i: (i, 0))],
        out_shape=[jax.ShapeDtypeStruct((M, SM_W), F32), jax.ShapeDtypeStruct((M, GLA_QK), F32)],
        compiler_params=_params("parallel"),
    )(proj, w2p, b2, alog_p, dt_p)


def _gates_bwd(proj, w2p, b2, alog_p, dt_p, dgb_heads, dla, d_proj, *, name):
    M = proj.shape[0]
    tm = _tile(M, 688, 8)
    g_ = M // tm

    tail_w = D_PROJ - C_SM

    def body(sm_ref, w2_ref, b2_ref, al_ref, dt_ref, dgb_ref, dla_ref, _,
             dsm_ref, dw2_ref, db2_ref, dal_ref, ddt_ref):
        i = pl.program_id(0)
        sm = sm_ref[...]
        lane, live, pre, neg_a, g, beta, z = _gate_terms(sm, w2_ref[...], b2_ref[...], al_ref[...], dt_ref[...], i * tm)
        dz = jnp.where(live, dla_ref[...] * (_sigmoid(-z) * (1.0 / GLA_GATE_NORMALIZER)), 0.0)
        dsm_lr = _dot(dz, w2_ref[...], NT)
        dgb = dgb_ref[0]
        for hh in range(1, GDN_HEADS):
            dgb = dgb + dgb_ref[hh]
        dgb = jnp.where(live, dgb, 0.0)
        da = dgb * neg_a * _sigmoid(pre)
        db = dgb * beta * (1.0 - beta)
        dsm = jnp.where(lane < GDN_HEADS, da, jnp.where(lane < 2 * GDN_HEADS, db, dsm_lr))
        dsm_ref[:, 0:SM_W] = dsm.astype(dsm_ref.dtype)
        dsm_ref[:, SM_W:tail_w] = jnp.zeros((tm, tail_w - SM_W), dsm_ref.dtype)
        is_a = lane < GDN_HEADS
        dal = jnp.sum(jnp.where(is_a, dgb * g, 0.0), axis=0, keepdims=True)
        ddt = jnp.sum(jnp.where(is_a, da, 0.0), axis=0, keepdims=True)
        dw2 = _dot(sm, dz, TN)
        db2 = jnp.sum(dz, axis=0, keepdims=True)

        @pl.when(i == 0)
        def _():
            dw2_ref[...] = dw2
            db2_ref[...] = db2
            dal_ref[...] = dal
            ddt_ref[...] = ddt

        @pl.when(i > 0)
        def _():
            dw2_ref[...] += dw2
            db2_ref[...] += db2
            dal_ref[...] += dal
            ddt_ref[...] += ddt

    full = lambda s: pl.BlockSpec(s, lambda i: (0, 0))
    return pl.pallas_call(
        body, name=name, grid=(g_,),
        in_specs=[pl.BlockSpec((tm, SM_W), lambda i: (i, C_SM // SM_W)), full((SM_W, GLA_QK)), full((1, GLA_QK)),
                  full((1, SM_W)), full((1, SM_W)),
                  pl.BlockSpec((GDN_HEADS, tm, SM_W), lambda i: (0, i, 0)),
                  pl.BlockSpec((tm, GLA_QK), lambda i: (i, 0)), _ANY],
        out_specs=[pl.BlockSpec((tm, tail_w), lambda i: (i, C_SM // tail_w)), full((SM_W, GLA_QK)), full((1, GLA_QK)),
                   full((1, SM_W)), full((1, SM_W))],
        out_shape=[jax.ShapeDtypeStruct(d_proj.shape, d_proj.dtype), jax.ShapeDtypeStruct((SM_W, GLA_QK), F32),
                   jax.ShapeDtypeStruct((1, GLA_QK), F32), jax.ShapeDtypeStruct((1, SM_W), F32),
                   jax.ShapeDtypeStruct((1, SM_W), F32)],
        input_output_aliases={7: 0},
        compiler_params=_params("arbitrary"),
    )(proj, w2p, b2, alog_p, dt_p, dgb_heads, dla, d_proj)


QKV_W = GDN_QK
N_QKV_GROUPS = 3
QKV_B0 = C_QKV // QKV_W
HALO = 8


def _conv_terms(x_ref, halo_ref, cw_ref, xs_ref, i, tm):
    xs_ref[HALO:HALO + tm, :] = x_ref[...]
    xs_ref[0:HALO, :] = jnp.where(i > 0, halo_ref[...], 0.0)
    cw = cw_ref[...]
    xs = xs_ref[...]
    taps = [(pltpu.roll(xs, CONV_K - 1 - t, 0) if t < CONV_K - 1 else xs)[HALO:HALO + tm, :] for t in range(CONV_K)]
    c = taps[0] * cw[0:1, :]
    for t in range(1, CONV_K):
        c = c + taps[t] * cw[t:t + 1, :]
    return c, taps


def _prep_fwd(proj, conv_w8, *, name):
    M = proj.shape[0]
    tm = _tile(M, 344, 8)

    def body(x_ref, halo_ref, cw_ref, o_ref, xs_ref):
        j, i = pl.program_id(0), pl.program_id(1)
        c, _ = _conv_terms(x_ref, halo_ref, cw_ref, xs_ref, i, tm)
        s, _ = _silu_and_grad(c)
        scale = jnp.where(j == 0, GDN_DK ** -0.5, 1.0)
        for hh in range(GDN_HEADS):
            cols = slice(hh * 128, (hh + 1) * 128)
            sh = s[:, cols]
            r = lax.rsqrt(jnp.sum(sh * sh, axis=-1, keepdims=True) + NORM_EPS)
            o_ref[:, cols] = jnp.where(j < 2, sh * (r * scale), sh)

    hb = tm // HALO
    return pl.pallas_call(
        body, name=name, grid=(N_QKV_GROUPS, M // tm),
        in_specs=[pl.BlockSpec((tm, QKV_W), lambda j, i: (i, QKV_B0 + j)),
                  pl.BlockSpec((HALO, QKV_W), lambda j, i: (jnp.maximum(i * hb - 1, 0), QKV_B0 + j)),
                  pl.BlockSpec((8, QKV_W), lambda j, i: (0, j))],
        out_specs=pl.BlockSpec((tm, QKV_W), lambda j, i: (i, j)),
        out_shape=jax.ShapeDtypeStruct((M, N_QKV_GROUPS * QKV_W), F32),
        scratch_shapes=[pltpu.VMEM((tm + HALO, QKV_W), F32)],
        compiler_params=_params("parallel", "arbitrary"),
    )(proj, proj, conv_w8)


def _prep_bwd(proj, conv_w8, dact, d_proj, *, name):
    M = proj.shape[0]
    tm = _tile(M, 688, 16)
    g_ = M // tm
    ext = tm + HALO

    def body(x_ref, prev_ref, next_ref, cw_ref, da_ref, dan_ref, _, o_ref, dcw_ref, xs_ref, das_ref, dcs_ref):
        j, i = pl.program_id(0), pl.program_id(1)
        not_last = i < g_ - 1
        xs_ref[0:HALO, :] = jnp.where(i > 0, prev_ref[...], 0.0)
        xs_ref[HALO:HALO + tm, :] = x_ref[...]
        xs_ref[HALO + tm:HALO + ext, :] = jnp.where(not_last, next_ref[...], 0.0)
        das_ref[0:tm, :] = da_ref[...]
        das_ref[tm:ext, :] = jnp.where(not_last, dan_ref[...], 0.0)
        cw = cw_ref[...]
        xs = xs_ref[...]
        taps = [(pltpu.roll(xs, CONV_K - 1 - t, 0) if t < CONV_K - 1 else xs)[HALO:HALO + ext, :] for t in range(CONV_K)]
        c = taps[0] * cw[0:1, :]
        for t in range(1, CONV_K):
            c = c + taps[t] * cw[t:t + 1, :]
        s, ds_dc = _silu_and_grad(c)
        scale = jnp.where(j == 0, GDN_DK ** -0.5, 1.0)
        for hh in range(GDN_HEADS):
            cols = slice(hh * 128, (hh + 1) * 128)
            sh = s[:, cols]
            r = lax.rsqrt(jnp.sum(sh * sh, axis=-1, keepdims=True) + NORM_EPS)
            da = das_ref[:, cols]
            y = sh * r
            dy = da * scale
            ds_norm = r * (dy - y * jnp.sum(dy * y, axis=-1, keepdims=True))
            dcs_ref[:, cols] = jnp.where(j < 2, ds_norm, da) * ds_dc[:, cols]
        dc = dcs_ref[...]
        acc = dc[0:tm, :] * cw[CONV_K - 1:CONV_K, :]
        for t in range(CONV_K - 1):
            acc = acc + pltpu.roll(dc, ext - (CONV_K - 1 - t), 0)[0:tm, :] * cw[t:t + 1, :]
        o_ref[...] = acc.astype(o_ref.dtype)
        r8 = lax.broadcasted_iota(jnp.int32, (8, QKV_W), 0)
        part = jnp.zeros((8, QKV_W), F32)
        for t in range(CONV_K):
            part = jnp.where(r8 == t, jnp.sum(dc[0:tm, :] * taps[t][0:tm, :], axis=0, keepdims=True), part)

        @pl.when(i == 0)
        def _():
            dcw_ref[...] = part

        @pl.when(i > 0)
        def _():
            dcw_ref[...] += part

    hb = tm // HALO
    last = M // HALO - 1
    prev_of = lambda i: jnp.maximum(i * hb - 1, 0)
    next_of = lambda i: jnp.minimum((i + 1) * hb, last)
    return pl.pallas_call(
        body, name=name, grid=(N_QKV_GROUPS, g_),
        in_specs=[pl.BlockSpec((tm, QKV_W), lambda j, i: (i, QKV_B0 + j)),
                  pl.BlockSpec((HALO, QKV_W), lambda j, i: (prev_of(i), QKV_B0 + j)),
                  pl.BlockSpec((HALO, QKV_W), lambda j, i: (next_of(i), QKV_B0 + j)),
                  pl.BlockSpec((8, QKV_W), lambda j, i: (0, j)),
                  pl.BlockSpec((tm, QKV_W), lambda j, i: (i, j)),
                  pl.BlockSpec((HALO, QKV_W), lambda j, i: (next_of(i), j)), _ANY],
        out_specs=[pl.BlockSpec((tm, QKV_W), lambda j, i: (i, QKV_B0 + j)), pl.BlockSpec((8, QKV_W), lambda j, i: (0, j))],
        out_shape=[jax.ShapeDtypeStruct(d_proj.shape, d_proj.dtype),
                   jax.ShapeDtypeStruct((8, N_QKV_GROUPS * QKV_W), F32)],
        input_output_aliases={6: 0},
        scratch_shapes=[pltpu.VMEM((HALO + ext, QKV_W), F32), pltpu.VMEM((ext, QKV_W), F32), pltpu.VMEM((ext, QKV_W), F32)],
        compiler_params=_params("parallel", "arbitrary"),
    )(proj, proj, proj, conv_w8, dact, dact, d_proj)


def _round_robin(gens):
    gens = list(gens)
    while gens:
        alive = []
        for gen in gens:
            try:
                next(gen)
                alive.append(gen)
            except StopIteration:
                pass
        gens = alive


def _unit_lower_inverse(a_low, eye):
    n = a_low.shape[0]
    ri = lax.broadcasted_iota(jnp.int32, (n, n), 0)
    ci = lax.broadcasted_iota(jnp.int32, (n, n), 1)
    same = lambda shift: (ri >> shift) == (ci >> shift)
    b = jnp.where(same(3), -a_low, 0.0)
    x = eye + b
    p2 = _dot3(b, b)
    yield
    x = x + _dot3(x, p2)
    p4 = _dot3(p2, p2)
    yield
    x = x + _dot3(x, p4)
    yield
    for shift in (3, 4, 5):
        between = jnp.where(same(shift + 1) & ~same(shift), a_low, 0.0)
        t = _dot3(between, x)
        yield
        x = x - _dot3(x, t)
        yield
    return x


class _GdnChunk:
    def build(self, q, k, v, gb, h):
        C = GDN_CHUNK
        lane = lax.broadcasted_iota(jnp.int32, (C, SM_W), 1)
        g = jnp.sum(jnp.where(lane == h, gb, 0.0), axis=1, keepdims=True)
        self.beta = jnp.sum(jnp.where(lane == h + GDN_HEADS, gb, 0.0), axis=1, keepdims=True)
        ri = lax.broadcasted_iota(jnp.int32, (C, C), 0)
        ci = lax.broadcasted_iota(jnp.int32, (C, C), 1)
        self.causal = ri >= ci
        self.strict = ri > ci
        self.eye = (ri == ci).astype(F32)
        gcb = _dotx(self.causal.astype(F32), jnp.broadcast_to(g, (C, SM_W)))
        yield
        self.gcol = gcb[:, 0:1]
        grow = gcb.T[0:1, 0:C]
        self.decay = jnp.exp(jnp.where(self.causal, self.gcol - grow, -1e30))
        self.egc = jnp.exp(self.gcol)
        glast = gcb[C - 1:C, 0:1]
        self.elast = jnp.exp(glast - self.gcol)
        self.gl = jnp.exp(glast)
        self.q, self.k, self.v = q, k, v
        self.kb = k * self.beta
        m = _dot(self.kb, k, NT)
        n_ = _dot(q, k, NT)
        yield
        self.a_low = jnp.where(self.strict, m * self.decay, 0.0)
        self.p = n_ * self.decay
        self.qd = q * self.egc
        self.kd = k * self.elast
        self.bu = v * self.beta
        self.bw = self.kb * self.egc


GDN_HB = 8
GDN_HG = GDN_HEADS // GDN_HB


def _gdn_specs(n_of):
    C, W = GDN_CHUNK, 128 * GDN_HB
    q_spec = pl.BlockSpec((C, W), lambda g, n: (n_of(n), g))
    k_spec = pl.BlockSpec((C, W), lambda g, n: (n_of(n), g + GDN_HG))
    v_spec = pl.BlockSpec((C, W), lambda g, n: (n_of(n), g + 2 * GDN_HG))
    gb_spec = pl.BlockSpec((C, SM_W), lambda g, n: (n_of(n), 0))
    o_spec = pl.BlockSpec((C, W), lambda g, n: (n_of(n), g))
    s_spec = pl.BlockSpec((GDN_HB, None, GDN_DK, GDN_DV), lambda g, n: (g, n_of(n), 0, 0))
    t_spec = pl.BlockSpec((GDN_HB, None, C, C), lambda g, n: (g, n_of(n), 0, 0))
    return q_spec, k_spec, v_spec, gb_spec, o_spec, s_spec, t_spec


def _gdn_fwd(act, gb, *, name):
    M = act.shape[0]
    N = M // GDN_CHUNK

    def body(q_ref, k_ref, v_ref, gb_ref, o_ref, s_ref, t_ref, state):
        g, n = pl.program_id(0), pl.program_id(1)

        @pl.when(n == 0)
        def _():
            state[...] = jnp.zeros_like(state)

        gb_ = gb_ref[...]

        def head(hh):
            cols = slice(hh * 128, (hh + 1) * 128)
            c = _GdnChunk()
            yield from c.build(q_ref[:, cols], k_ref[:, cols], v_ref[:, cols], gb_, g * GDN_HB + hh)
            tinv = yield from _unit_lower_inverse(c.a_low, c.eye)
            s = state[hh]
            s_ref[hh] = s
            t_ref[hh] = tinv
            u = _dot(tinv, c.bu)
            w = _dot(tinv, c.bw)
            yield
            vn = u - _dot(w, s)
            o1 = _dot(c.qd, s)
            yield
            o_ref[:, cols] = o1 + _dot(c.p, vn)
            state[hh] = c.gl * s + _dot(c.kd, vn, TN)

        _round_robin(head(hh) for hh in range(GDN_HB))

    q_spec, k_spec, v_spec, gb_spec, o_spec, s_spec, t_spec = _gdn_specs(lambda n: n)
    return pl.pallas_call(
        body, name=name, grid=(GDN_HG, N),
        in_specs=[q_spec, k_spec, v_spec, gb_spec], out_specs=[o_spec, s_spec, t_spec],
        out_shape=[jax.ShapeDtypeStruct((M, GDN_V), F32),
                   jax.ShapeDtypeStruct((GDN_HEADS, N, GDN_DK, GDN_DV), F32),
                   jax.ShapeDtypeStruct((GDN_HEADS, N, GDN_CHUNK, GDN_CHUNK), F32)],
        scratch_shapes=[pltpu.VMEM((GDN_HB, GDN_DK, GDN_DV), F32)],
        compiler_params=_params("parallel", "arbitrary"),
    )(act, act, act, gb)


def _gdn_bwd(act, gb, do, s_all, t_all, *, name):
    M = act.shape[0]
    N = M // GDN_CHUNK
    C = GDN_CHUNK
    assert GDN_HG == 1

    def body(q_ref, k_ref, v_ref, gb_ref, do_ref, s_ref, t_ref, dact_ref, dgb_ref, dstate):
        g, n = pl.program_id(0), pl.program_id(1)

        @pl.when(n == 0)
        def _():
            dstate[...] = jnp.zeros_like(dstate)

        gb_ = gb_ref[...]
        last = lax.broadcasted_iota(jnp.int32, (C, 1), 0) == C - 1
        upper = (lax.broadcasted_iota(jnp.int32, (C, C), 0) <= lax.broadcasted_iota(jnp.int32, (C, C), 1)).astype(F32)
        lane = lax.broadcasted_iota(jnp.int32, (C, SM_W), 1)
        def head(hh):
            cols = slice(hh * 128, (hh + 1) * 128)
            h = g * GDN_HB + hh
            c = _GdnChunk()
            yield from c.build(q_ref[:, cols], k_ref[:, cols], v_ref[:, cols], gb_, h)
            tinv = t_ref[hh]
            s = s_ref[hh]
            do_ = do_ref[:, cols]
            ds1 = dstate[hh]
            u = _dot(tinv, c.bu)
            w = _dot(tinv, c.bw)
            dqd = _dot(do_, s, NT)
            dvn0 = _dot(c.p, do_, TN) + _dot(c.kd, ds1)
            dst0 = _dot(c.qd, do_, TN) + c.gl * ds1
            yield
            vn = u - _dot(w, s)
            dvn = dvn0
            yield
            dp = jnp.where(c.causal, _dot(do_, vn, NT), 0.0)
            dstate[hh] = dst0 - _dot(w, dvn, TN)
            dkd = _dot(vn, ds1, NT)
            dw = -_dot(dvn, s, NT)
            dbu = _dot(tinv, dvn, TN)
            dgl = jnp.sum(jnp.sum(s * ds1, axis=1, keepdims=True), axis=0, keepdims=True)
            yield
            dbw = _dot(tinv, dw, TN)
            t1 = _dot(dbu, u, NT)
            yield
            da = jnp.where(c.strict, -(t1 + _dot(dbw, w, NT)), 0.0)
            dn_ = dp * c.decay
            dq0 = _dot(dn_, c.k)
            dk0 = _dot(dn_, c.q, TN)
            yield
            dm = da * c.decay
            e = da * c.a_low + dp * c.p
            dkb = _dot(dm, c.k) + dbw * c.egc
            dact_ref[:, GDN_QK + hh * 128:GDN_QK + (hh + 1) * 128] = (
                _dot(dm, c.kb, TN) + dk0 + dkb * c.beta + dkd * c.elast)
            dact_ref[:, cols] = dq0 + dqd * c.egc
            dact_ref[:, 2 * GDN_QK + hh * 128:2 * GDN_QK + (hh + 1) * 128] = dbu * c.beta
            dbeta = jnp.sum(dbu * c.v, axis=1, keepdims=True) + jnp.sum(dkb * c.k, axis=1, keepdims=True)
            t_kd = jnp.sum(dkd * c.kd, axis=1, keepdims=True)
            dgc = (jnp.sum(e, axis=1, keepdims=True) - jnp.sum(e.T, axis=1, keepdims=True)
                   + jnp.sum(dbw * c.bw, axis=1, keepdims=True) + jnp.sum(dqd * c.qd, axis=1, keepdims=True) - t_kd)
            dgc = dgc + jnp.where(last, jnp.sum(t_kd, axis=0, keepdims=True) + dgl * c.gl, 0.0)
            yield
            dg = _dotx(upper, jnp.broadcast_to(dgc, (C, SM_W)))
            dgb_ref[hh] = jnp.where(lane == h, dg, jnp.where(lane == h + GDN_HEADS, dbeta, 0.0))

        _round_robin(head(hh) for hh in range(GDN_HB))

    rev = lambda n: N - 1 - n
    q_spec, k_spec, v_spec, gb_spec, o_spec, s_spec, t_spec = _gdn_specs(rev)
    dgb_spec = pl.BlockSpec((GDN_HB, C, SM_W), lambda g, n: (g, rev(n), 0))
    return pl.pallas_call(
        body, name=name, grid=(GDN_HG, N),
        in_specs=[q_spec, k_spec, v_spec, gb_spec, o_spec, s_spec, t_spec],
        out_specs=[pl.BlockSpec((C, 2 * GDN_QK + GDN_V), lambda g, n: (rev(n), 0)), dgb_spec],
        out_shape=[jax.ShapeDtypeStruct((M, 2 * GDN_QK + GDN_V), F32),
                   jax.ShapeDtypeStruct((GDN_HEADS, M, SM_W), F32)],
        scratch_shapes=[pltpu.VMEM((GDN_HB, GDN_DK, GDN_DV), F32)],
        compiler_params=_params("parallel", "arbitrary"),
    )(act, act, act, gb, do, s_all, t_all)


GLA_STEP_ROWS = 64
GLA_SUB = GLA_STEP_ROWS // GLA_CHUNK


def _gla_cumsum(la):
    C = GLA_CHUNK
    ltri = (lax.broadcasted_iota(jnp.int32, (C, C), 0) >= lax.broadcasted_iota(jnp.int32, (C, C), 1)).astype(F32)
    return _dotx(ltri, la)


def _gla_decay_rows(b, i):
    rj = lax.broadcasted_iota(jnp.int32, (GLA_CHUNK, GLA_DK), 0)
    return jnp.where(rj <= i, jnp.exp(jnp.minimum(b[i:i + 1, :] - b, 0.0)), 0.0)


def _gla_scores_t(q, k, b):
    C = GLA_CHUNK
    lane = lax.broadcasted_iota(jnp.int32, (C, C), 1)
    st = jnp.zeros((C, C), F32)
    for i in range(C):
        si = jnp.sum(q[i:i + 1, :] * k * _gla_decay_rows(b, i), axis=1, keepdims=True)
        st = jnp.where(lane == i, si, st)
        if i % 4 == 3:
            yield
    return st


def _gla_specs(n_of):
    R = GLA_STEP_ROWS
    q_spec = pl.BlockSpec((R, GLA_QK), lambda n: (n_of(n), C_GQ // GLA_QK))
    k_spec = pl.BlockSpec((R, GLA_QK), lambda n: (n_of(n), C_GK // GLA_QK))
    v_spec = pl.BlockSpec((R, GLA_V), lambda n: (n_of(n), C_GV // GLA_V))
    la_spec = pl.BlockSpec((R, GLA_QK), lambda n: (n_of(n), 0))
    o_spec = pl.BlockSpec((R, GLA_V), lambda n: (n_of(n), 0))
    s_spec = pl.BlockSpec((GLA_HEADS, None, GLA_SUB, GLA_DV, GLA_DK), lambda n: (0, n_of(n), 0, 0, 0))
    return q_spec, k_spec, v_spec, la_spec, o_spec, s_spec


def _gla_fwd(proj, la, *, name):
    M = proj.shape[0]
    N = M // GLA_STEP_ROWS
    C = GLA_CHUNK

    def body(q_ref, k_ref, v_ref, la_ref, o_ref, s_ref, state):
        n = pl.program_id(0)

        @pl.when(n == 0)
        def _():
            state[...] = jnp.zeros_like(state)

        def head(hh):
            kc = slice(hh * GLA_DK, (hh + 1) * GLA_DK)
            vc = slice(hh * GLA_DV, (hh + 1) * GLA_DV)
            st = state[hh]
            for c in range(GLA_SUB):
                rows = slice(c * C, (c + 1) * C)
                q = q_ref[rows, kc] * (GLA_DK ** -0.5)
                k = k_ref[rows, kc]
                v = v_ref[rows, vc]
                b = _gla_cumsum(la_ref[rows, kc])
                yield
                s_ref[hh, c] = st
                blast = b[C - 1:C, :]
                sc_t = yield from _gla_scores_t(q, k, b)
                o1 = _dot(q * jnp.exp(b), st, NT)
                kv = _dot(v, k * jnp.exp(blast - b), TN)
                o2 = _dot(sc_t, v, TN)
                yield
                o_ref[rows, vc] = o1 + o2
                st = st * jnp.exp(blast) + kv
            state[hh] = st

        _round_robin(head(hh) for hh in range(GLA_HEADS))

    q_spec, k_spec, v_spec, la_spec, o_spec, s_spec = _gla_specs(lambda n: n)
    return pl.pallas_call(
        body, name=name, grid=(N,),
        in_specs=[q_spec, k_spec, v_spec, la_spec], out_specs=[o_spec, s_spec],
        out_shape=[jax.ShapeDtypeStruct((M, GLA_V), F32),
                   jax.ShapeDtypeStruct((GLA_HEADS, N, GLA_SUB, GLA_DV, GLA_DK), F32)],
        scratch_shapes=[pltpu.VMEM((GLA_HEADS, GLA_DV, GLA_DK), F32)],
        compiler_params=_params("arbitrary"),
    )(proj, proj, proj, la)


def _gla_bwd(proj, la, do, s_all, d_proj, *, name):
    M = proj.shape[0]
    N = M // GLA_STEP_ROWS
    C = GLA_CHUNK
    qkv_w = 2 * GLA_QK + GLA_V
    assert C_GK == C_GQ + GLA_QK and C_GV == C_GK + GLA_QK and C_GQ % qkv_w == 0

    def body(q_ref, k_ref, v_ref, la_ref, do_ref, s_ref, _, dp_ref, dla_ref, dstate):
        n = pl.program_id(0)

        @pl.when(n == 0)
        def _():
            dstate[...] = jnp.zeros_like(dstate)

        lane = lax.broadcasted_iota(jnp.int32, (C, C), 1)
        ri = lax.broadcasted_iota(jnp.int32, (C, GLA_DK), 0)
        upper = (lax.broadcasted_iota(jnp.int32, (C, C), 0) <= lane).astype(F32)
        def head(hh):
            kc = slice(hh * GLA_DK, (hh + 1) * GLA_DK)
            vc = slice(hh * GLA_DV, (hh + 1) * GLA_DV)
            ds1 = dstate[hh]
            for c in reversed(range(GLA_SUB)):
                rows = slice(c * C, (c + 1) * C)
                q = q_ref[rows, kc] * (GLA_DK ** -0.5)
                k = k_ref[rows, kc]
                v = v_ref[rows, vc]
                b = _gla_cumsum(la_ref[rows, kc])
                do_ = do_ref[rows, vc]
                st = s_ref[hh, c]
                dsc_t = _dot(v, do_, NT)
                dqe = _dot(do_, st)
                dke = _dot(v, ds1)
                yield
                blast = b[C - 1:C, :]
                eb = jnp.exp(b)
                elast = jnp.exp(blast - b)
                eblast = jnp.exp(blast)
                qe = q * eb
                ke = k * elast
                dv2 = _dot(ke, ds1, NT)
                ds_new = _dot(do_, qe, TN)
                deblast = jnp.sum(st * ds1, axis=0, keepdims=True)
                sc_t = jnp.zeros((C, C), F32)
                dq_sc = jnp.zeros((C, GLA_DK), F32)
                dk_sc = jnp.zeros((C, GLA_DK), F32)
                for i in range(C):
                    f = _gla_decay_rows(b, i)
                    kf = k * f
                    si = jnp.sum(q[i:i + 1, :] * kf, axis=1, keepdims=True)
                    sc_t = jnp.where(lane == i, si, sc_t)
                    dsi = jnp.sum(jnp.where(lane == i, dsc_t, 0.0), axis=1, keepdims=True)
                    dq_sc = jnp.where(ri == i, jnp.sum(dsi * kf, axis=0, keepdims=True), dq_sc)
                    dk_sc = dk_sc + (dsi * f) * q[i:i + 1, :]
                    if i % 4 == 3:
                        yield
                dv1 = _dot(sc_t, do_)
                dp_ref[rows, kc] = ((dq_sc + dqe * eb) * (GLA_DK ** -0.5)).astype(dp_ref.dtype)
                dp_ref[rows, GLA_QK + hh * GLA_DK:GLA_QK + (hh + 1) * GLA_DK] = (dk_sc + dke * elast).astype(dp_ref.dtype)
                t_ke = dke * ke
                db = q * dq_sc - k * dk_sc + dqe * qe - t_ke
                db = db + jnp.where(ri == C - 1, jnp.sum(t_ke, axis=0, keepdims=True) + deblast * eblast, 0.0)
                dla = _dotx(upper, db)
                yield
                dp_ref[rows, 2 * GLA_QK + hh * GLA_DV:2 * GLA_QK + (hh + 1) * GLA_DV] = (dv1 + dv2).astype(dp_ref.dtype)
                dla_ref[rows, kc] = dla
                ds1 = ds1 * eblast + ds_new
            dstate[hh] = ds1

        _round_robin(head(hh) for hh in range(GLA_HEADS))

    rev = lambda n: N - 1 - n
    q_spec, k_spec, v_spec, la_spec, o_spec, s_spec = _gla_specs(rev)
    return pl.pallas_call(
        body, name=name, grid=(N,),
        in_specs=[q_spec, k_spec, v_spec, la_spec, o_spec, s_spec, _ANY],
        out_specs=[pl.BlockSpec((GLA_STEP_ROWS, qkv_w), lambda n: (rev(n), C_GQ // qkv_w)), la_spec],
        out_shape=[jax.ShapeDtypeStruct(d_proj.shape, d_proj.dtype), jax.ShapeDtypeStruct((M, GLA_QK), F32)],
        input_output_aliases={6: 0},
        scratch_shapes=[pltpu.VMEM((GLA_HEADS, GLA_DV, GLA_DK), F32)],
        compiler_params=_params("arbitrary"),
    )(proj, proj, proj, la, do, s_all, d_proj)


def _head_norm(o, wn):
    r = lax.rsqrt(jnp.mean(o * o, axis=-1, keepdims=True) + NORM_EPS)
    return o * r, r


def _mix_heads():
    heads = [(0, GDN_DV, hh * GDN_DV, hh * GDN_DV) for hh in range(GDN_HEADS)]
    heads += [(1, GLA_DV, GDN_V + hh * GLA_DV, hh * GLA_DV) for hh in range(GLA_HEADS)]
    return heads


def _mix_fwd(o_gdn, o_gla, proj, wn_gdn, wn_gla, *, name):
    M = proj.shape[0]
    tm = _tile(M, 344, 16)

    def body(og_ref, ol_ref, z_ref, r_ref, wg_ref, wl_ref, m_ref):
        srcs = ((og_ref, z_ref, wg_ref), (ol_ref, r_ref, wl_ref))
        for grp, width, mcol, col in _mix_heads():
            o_ref, gate_ref, w_ref = srcs[grp]
            xhat, _ = _head_norm(o_ref[:, col:col + width], None)
            gate, _ = _silu_and_grad(gate_ref[:, col:col + width])
            m_ref[:, mcol:mcol + width] = (xhat * w_ref[...] * gate).astype(m_ref.dtype)

    full = lambda s: pl.BlockSpec(s, lambda i: (0, 0))
    return pl.pallas_call(
        body, name=name, grid=(M // tm,),
        in_specs=[pl.BlockSpec((tm, GDN_V), lambda i: (i, 0)), pl.BlockSpec((tm, GLA_V), lambda i: (i, 0)),
                  pl.BlockSpec((tm, GDN_V), lambda i: (i, C_Z // GDN_V)),
                  pl.BlockSpec((tm, GLA_V), lambda i: (i, C_GR // GLA_V)),
                  full((1, GDN_DV)), full((1, GLA_DV))],
        out_specs=pl.BlockSpec((tm, D_MODEL), lambda i: (i, 0)),
        out_shape=jax.ShapeDtypeStruct((M, D_MODEL), BF16),
        compiler_params=_params("parallel"),
    )(o_gdn, o_gla, proj, proj, wn_gdn, wn_gla)


def _mix_bwd(o_gdn, o_gla, proj, wn_gdn, wn_gla, dmixed, *, name):
    M = proj.shape[0]
    tm = _tile(M, 344, 16)
    g_ = M // tm
    assert C_Z == 0 and C_GR == GDN_V

    def body(og_ref, ol_ref, z_ref, r_ref, wg_ref, wl_ref, dm_ref,
             dog_ref, dol_ref, dzr_ref, dwg_ref, dwl_ref):
        i = pl.program_id(0)
        srcs = ((og_ref, z_ref, wg_ref, dog_ref), (ol_ref, r_ref, wl_ref, dol_ref))
        dws = [jnp.zeros((1, GDN_DV), F32), jnp.zeros((1, GLA_DV), F32)]
        for grp, width, mcol, col in _mix_heads():
            o_ref, gate_ref, w_ref, do_ref = srcs[grp]
            cols = slice(col, col + width)
            xhat, r = _head_norm(o_ref[:, cols], None)
            gate, dgate_dc = _silu_and_grad(gate_ref[:, cols])
            dm = dm_ref[:, mcol:mcol + width]
            dzr_ref[:, mcol:mcol + width] = (dm * xhat * w_ref[...] * dgate_dc).astype(dzr_ref.dtype)
            dnorm = dm * gate
            dws[grp] = dws[grp] + jnp.sum(dnorm * xhat, axis=0, keepdims=True)
            dxhat = dnorm * w_ref[...]
            do_ref[:, cols] = r * (dxhat - xhat * jnp.mean(dxhat * xhat, axis=-1, keepdims=True))

        @pl.when(i == 0)
        def _():
            dwg_ref[...] = dws[0]
            dwl_ref[...] = dws[1]

        @pl.when(i > 0)
        def _():
            dwg_ref[...] += dws[0]
            dwl_ref[...] += dws[1]

    full = lambda s: pl.BlockSpec(s, lambda i: (0, 0))
    half = pl.BlockSpec((tm, GDN_V), lambda i: (i, 0))
    return pl.pallas_call(
        body, name=name, grid=(g_,),
        in_specs=[half, half, pl.BlockSpec((tm, GDN_V), lambda i: (i, C_Z // GDN_V)),
                  pl.BlockSpec((tm, GLA_V), lambda i: (i, C_GR // GLA_V)),
                  full((1, GDN_DV)), full((1, GLA_DV)), pl.BlockSpec((tm, D_MODEL), lambda i: (i, 0))],
        out_specs=[half, half, pl.BlockSpec((tm, GDN_V + GLA_V), lambda i: (i, 0)),
                   full((1, GDN_DV)), full((1, GLA_DV))],
        out_shape=[jax.ShapeDtypeStruct((M, GDN_V), F32), jax.ShapeDtypeStruct((M, GLA_V), F32),
                   jax.ShapeDtypeStruct((M, D_PROJ), BF16),
                   jax.ShapeDtypeStruct((1, GDN_DV), F32), jax.ShapeDtypeStruct((1, GLA_DV), F32)],
        compiler_params=_params("arbitrary"),
    )(o_gdn, o_gla, proj, proj, wn_gdn, wn_gla, dmixed)


def _swiglu_fwd(n, w_gate_t, w_up_t, *, name, tm=1376, tn=512):
    M, D = n.shape
    F = w_gate_t.shape[0]
    tm, tn = _tile(M, tm, 16), _tile(F, tn, 128)

    def body(n_ref, wg_ref, wu_ref, g_ref, u_ref, a_ref):
        x = n_ref[...]
        g = _dot(x, wg_ref[...], NT)
        u = _dot(x, wu_ref[...], NT)
        s, _ = _silu_and_grad(g)
        g_ref[...] = g.astype(g_ref.dtype)
        u_ref[...] = u.astype(u_ref.dtype)
        a_ref[...] = (s * u).astype(a_ref.dtype)

    w_spec = pl.BlockSpec((tn, D), lambda i, j: (j, 0))
    o_spec = pl.BlockSpec((tm, tn), lambda i, j: (i, j))
    return pl.pallas_call(
        body, name=name, grid=(M // tm, F // tn),
        in_specs=[pl.BlockSpec((tm, D), lambda i, j: (i, 0)), w_spec, w_spec], out_specs=[o_spec] * 3,
        out_shape=[jax.ShapeDtypeStruct((M, F), BF16)] * 3, compiler_params=_params("parallel", "parallel"),
    )(n, w_gate_t, w_up_t)


def _swiglu_bwd(dh, w_down, gate, up, *, name, after=None, tm=1376, tn=512):
    M, D = dh.shape
    F = w_down.shape[0]
    tm, tn = _tile(M, tm, 16), _tile(F, tn, 128)
    n_after = 0 if after is None else 1

    def body(*refs):
        dh_ref, w_ref, g_ref, u_ref, dg_ref, du_ref = refs[n_after:]
        da = _dot(dh_ref[...], w_ref[...], NT)
        s, ds = _silu_and_grad(g_ref[...].astype(F32))
        dg_ref[...] = (da * u_ref[...].astype(F32) * ds).astype(dg_ref.dtype)
        du_ref[...] = (da * s).astype(du_ref.dtype)

    o_spec = pl.BlockSpec((tm, tn), lambda i, j: (i, j))
    return pl.pallas_call(
        body, name=name, grid=(M // tm, F // tn),
        in_specs=[_ANY] * n_after + [pl.BlockSpec((tm, D), lambda i, j: (i, 0)),
                                     pl.BlockSpec((tn, D), lambda i, j: (j, 0)), o_spec, o_spec],
        out_specs=[o_spec, o_spec], out_shape=[jax.ShapeDtypeStruct((M, F), BF16)] * 2,
        compiler_params=_params("parallel", "parallel"),
    )(*((after,) if n_after else ()), dh, w_down, gate, up)


def _adamw(w, g, m, v, *, name):
    shape = w.shape
    cols = shape[-1]
    rows = w.size // cols
    w2, g2, m2, v2 = (t.reshape(rows, cols) for t in (w, g, m, v))
    if rows % 8 == 0 or cols % 128 != 0:
        tr, tc = (_tile(rows, 256, 8) if rows % 8 == 0 else rows), cols
    else:
        tr, tc = rows, _tile(cols, 256, 128)

    def body(w_ref, g_ref, m_ref, v_ref, d_ref, nm_ref, nv_ref):
        g_ = g_ref[...]
        nm = ADAM_B1 * m_ref[...] + (1.0 - ADAM_B1) * g_
        nv = ADAM_B2 * v_ref[...] + (1.0 - ADAM_B2) * (g_ * g_)
        m_hat = nm / (1.0 - ADAM_B1 ** ADAM_STEP)
        v_hat = nv / (1.0 - ADAM_B2 ** ADAM_STEP)
        d_ref[...] = -ADAM_LR * (m_hat / (jnp.sqrt(v_hat) + ADAM_EPS) + ADAM_WD * w_ref[...])
        nm_ref[...] = nm
        nv_ref[...] = nv

    blk = pl.BlockSpec((tr, tc), lambda i, j: (i, j))
    outs = pl.pallas_call(
        body, name=name, grid=(rows // tr, cols // tc), in_specs=[blk] * 4, out_specs=[blk] * 3,
        out_shape=[jax.ShapeDtypeStruct((rows, cols), F32)] * 3, compiler_params=_params("parallel", "parallel"),
    )(w2, g2, m2, v2)
    return tuple(t.reshape(shape) for t in outs)


def _sum_slabs(x, *, name):
    _, R, C = x.shape
    sub = 16 if x.dtype == BF16 else 8
    if R % sub == 0:
        tr, tc = _tile(R, 128, sub), C
    else:
        tr, tc = R, _tile(C, 256, 128)

    def body(x_ref, o_ref):
        acc = x_ref[0].astype(F32)
        for s in range(1, N_DEV):
            acc = acc + x_ref[s].astype(F32)
        o_ref[...] = acc

    return pl.pallas_call(
        body, name=name, grid=(R // tr, C // tc),
        in_specs=[pl.BlockSpec((N_DEV, tr, tc), lambda i, j: (0, i, j))],
        out_specs=pl.BlockSpec((tr, tc), lambda i, j: (i, j)),
        out_shape=jax.ShapeDtypeStruct((R, C), F32), compiler_params=_params("parallel", "parallel"),
    )(x)


def _peers():
    x, y, c = lax.axis_index("x"), lax.axis_index("y"), lax.axis_index("c")
    me = 4 * x + 2 * y + c
    peers = []
    for k in range(1, N_DEV):
        px = 1 - x if k & 4 else x
        py = 1 - y if k & 2 else y
        pc = 1 - c if k & 1 else c
        peers.append(((px, py, pc), 4 * px + 2 * py + pc))
    return me, peers


def _exchange(x, *, gather, name):
    slab = x.shape if gather else x.shape[1:]

    def body(x_ref, o_ref, send_sems, recv_sems, own_sem):
        me, peers = _peers()
        own = pltpu.make_async_copy(x_ref if gather else x_ref.at[me], o_ref.at[me], own_sem)
        own.start()
        sends, recvs = [], []
        for k, (pos, idx) in enumerate(peers):
            sends.append(pltpu.make_async_remote_copy(
                src_ref=x_ref if gather else x_ref.at[idx], dst_ref=o_ref.at[me],
                send_sem=send_sems.at[k], recv_sem=recv_sems.at[k],
                device_id=pos, device_id_type=pl.DeviceIdType.MESH))
            recvs.append(pltpu.make_async_remote_copy(
                src_ref=x_ref if gather else x_ref.at[idx], dst_ref=o_ref.at[idx],
                send_sem=send_sems.at[k], recv_sem=recv_sems.at[k],
                device_id=pos, device_id_type=pl.DeviceIdType.MESH))
        for cp in sends:
            cp.start()
        for cp in recvs:
            cp.wait_recv()
        for cp in sends:
            cp.wait_send()
        own.wait()

    hbm = pl.BlockSpec(memory_space=pltpu.HBM)
    return pl.pallas_call(
        body, name=name, in_specs=[hbm], out_specs=hbm,
        out_shape=jax.ShapeDtypeStruct((N_DEV,) + tuple(slab), x.dtype),
        scratch_shapes=[pltpu.SemaphoreType.DMA((N_DEV - 1,)), pltpu.SemaphoreType.DMA((N_DEV - 1,)),
                        pltpu.SemaphoreType.DMA],
    )(x)


_HBM = pl.BlockSpec(memory_space=pltpu.HBM)
_SEM = pl.BlockSpec(memory_space=pltpu.SEMAPHORE)
_EFFECT = pltpu.SideEffectType.DATAFLOW_SIDE_EFFECTING


def _exchange_start(x, *, gather, name, after=None):
    slab = x.shape if gather else x.shape[1:]
    n_after = 0 if after is None else 1

    def body(*refs):
        x_ref, land_ref, send_sems, recv_sems, _, _, token = refs[n_after:]
        me, peers = _peers()
        for k, (pos, idx) in enumerate(peers):
            pltpu.make_async_remote_copy(
                src_ref=x_ref if gather else x_ref.at[idx], dst_ref=land_ref.at[me],
                send_sem=send_sems.at[k], recv_sem=recv_sems.at[k],
                device_id=pos, device_id_type=pl.DeviceIdType.MESH).start()
        token[...] = jnp.zeros_like(token)

    land = lax.empty((N_DEV,) + tuple(slab), x.dtype)
    return pl.pallas_call(
        body, name=name,
        out_shape=(pltpu.SemaphoreType.DMA((N_DEV - 1,)), pltpu.SemaphoreType.DMA((N_DEV - 1,)),
                   pltpu.HBM(x.shape, x.dtype), pltpu.HBM(land.shape, land.dtype), jax.ShapeDtypeStruct((8, 128), F32)),
        in_specs=[_ANY] * n_after + [_HBM, _HBM],
        out_specs=(_SEM, _SEM, _HBM, _HBM, pl.BlockSpec(memory_space=pltpu.VMEM)),
        input_output_aliases={n_after: 2, n_after + 1: 3},
        compiler_params=pltpu.CompilerParams(has_side_effects=_EFFECT),
    )(*((after,) if n_after else ()), pltpu.with_memory_space_constraint(x, pltpu.HBM),
      pltpu.with_memory_space_constraint(land, pltpu.HBM))


def _exchange_wait(handle, after, *, gather, name):
    send_sems, recv_sems, x_thru, land_thru, _ = handle
    afters = list(after) if isinstance(after, (list, tuple)) else [after]

    def body(x_ref, land_ref, send_sems, recv_sems, *rest):
        me, peers = _peers()
        for k, (pos, idx) in enumerate(peers):
            cp = pltpu.make_async_remote_copy(
                src_ref=x_ref if gather else x_ref.at[idx], dst_ref=land_ref.at[idx],
                send_sem=send_sems.at[k], recv_sem=recv_sems.at[k],
                device_id=pos, device_id_type=pl.DeviceIdType.MESH)
            cp.wait_send()
            cp.wait_recv()

    return pl.pallas_call(
        body, name=name,
        out_shape=(pltpu.HBM(x_thru.shape, x_thru.dtype), pltpu.HBM(land_thru.shape, land_thru.dtype)),
        in_specs=[_HBM, _HBM, _SEM, _SEM] + [_ANY] * len(afters), out_specs=(_HBM, _HBM),
        input_output_aliases={0: 0, 1: 1}, compiler_params=pltpu.CompilerParams(has_side_effects=_EFFECT),
    )(x_thru, land_thru, send_sems, recv_sems, *afters)


W_IN_SLAB = D_IN // N_DEV


def _to_proj_rows(t):
    z = jnp.zeros((D_PROJ - C_SM - 2 * GDN_HEADS - GLA_RANK,) + t.shape[1:], t.dtype)
    return jnp.concatenate([t[R_Z:R_A], t[R_GR:R_LR], t[R_GQ:R_GR], t[:R_Z], t[R_A:R_GQ], t[R_LR:], z], axis=0)


def _from_proj_rows(t):
    ab = C_SM + 2 * GDN_HEADS
    return jnp.concatenate([t[C_QKV:C_SM], t[C_Z:C_GR], t[C_SM:ab], t[C_GQ:C_QKV], t[C_GR:C_GQ],
                            t[ab:ab + GLA_RANK]], axis=0)


def _local_step(x, target, meta, attn_nw, conv_w, a_log, dt_bias, gdn_nw, w2, b2, gla_nw, ffn_nw, final_nw,
                fetch, emit, start=None):
    S = x.shape[0]
    h0 = jnp.concatenate([jnp.zeros((ROW_PAD, D_MODEL), F32), meta, x], axis=0)
    target_p = jnp.concatenate([jnp.zeros((HEAD_ROWS, D_MODEL), F32), target], axis=0)
    conv_w8 = jnp.concatenate([conv_w, jnp.zeros((8 - CONV_K, conv_w.shape[1]), F32)], axis=0)
    w2p = jnp.zeros((SM_W, GLA_QK), F32).at[2 * GDN_HEADS:2 * GDN_HEADS + GLA_RANK].set(w2)
    alog_p = jnp.zeros((1, SM_W), F32).at[:, :GDN_HEADS].set(a_log)
    dt_p = jnp.zeros((1, SM_W), F32).at[:, :GDN_HEADS].set(dt_bias)

    n1 = _rmsnorm_fwd(h0, attn_nw, name="attn_norm", after=start)
    w_in_t = fetch("w_in_t", (n1, target_p, conv_w8, w2p, alog_p, dt_p))
    proj = _matmul(n1, w_in_t, mode="nt", name="in_proj")
    gb, la = _gates_fwd(proj, w2p, b2, alog_p, dt_p, name="gates")
    act = _prep_fwd(proj, conv_w8, name="gdn_prep")
    o_gdn, s_gdn, t_gdn = _gdn_fwd(act, gb, name="gdn_fwd")
    o_gla, s_gla = _gla_fwd(proj, la, name="gla_fwd")
    mixed = _mix_fwd(o_gdn, o_gla, proj, gdn_nw, gla_nw, name="mix")
    w_gate_t, w_up_t, w_out, w_down = fetch("rest", mixed)
    h1 = _matmul(mixed, w_out, mode="nn", add=h0, name="out_proj")
    n2 = _rmsnorm_fwd(h1, ffn_nw, name="ffn_norm")
    gate, up, hid = _swiglu_fwd(n2, w_gate_t, w_up_t, name="swiglu")
    h2 = _matmul(hid, w_down, mode="nn", add=h1, name="ffn_down", tm=688, tk=D_FF)
    dh2, dh2_b, d_final_nw, loss = _loss_head(h2, final_nw, target_p, name="loss_head")

    wg = dict(mode="tn", out_dtype=BF16, tn=512, tk=S + HEAD_ROWS)
    tok = emit("w_down", _matmul(hid, dh2_b, name="d_w_down", tm=704, **wg))
    d_gate, d_up = _swiglu_bwd(dh2_b, w_down, gate, up, name="d_swiglu", after=tok)
    tok = emit("w_gate_t", _matmul(d_gate, n2, name="d_w_gate", tm=704, **wg))
    tok = emit("w_up_t", _matmul(d_up, n2, name="d_w_up", tm=704, after=tok, **wg))
    d_n2 = _matmul(d_gate, w_gate_t, mode="nn", name="d_n2_gate", tm=688, tk=D_FF, after=tok)
    d_n2 = _matmul(d_up, w_up_t, mode="nn", add=d_n2, name="d_n2_up", tm=688, tk=D_FF)
    dh1, dh1_b, d_ffn_nw = _rmsnorm_bwd(h1, ffn_nw, d_n2, dh2, name="d_ffn_norm", also_bf16=True)

    tok = emit("w_out", _matmul(mixed, dh1_b, name="d_w_out", tm=512, **wg))
    d_mixed = _matmul(dh1_b, w_out, mode="nt", name="d_mixed", after=tok)
    do_gdn, do_gla, d_proj, d_gdn_nw, d_gla_nw = _mix_bwd(o_gdn, o_gla, proj, gdn_nw, gla_nw, d_mixed, name="d_mix")
    d_proj, d_la = _gla_bwd(proj, la, do_gla, s_gla, d_proj, name="gla_bwd")
    dact, dgb_heads = _gdn_bwd(act, gb, do_gdn, s_gdn, t_gdn, name="gdn_bwd")
    d_proj, d_w2p, d_b2, d_alog, d_dt = _gates_bwd(proj, w2p, b2, alog_p, dt_p, dgb_heads, d_la, d_proj, name="d_gates")
    d_proj, d_conv_w8 = _prep_bwd(proj, conv_w8, dact, d_proj, name="d_gdn_prep")
    tok = emit("w_in_t", _matmul(d_proj, n1, name="d_w_in", tm=768, **wg))
    d_n1 = _matmul(d_proj, w_in_t, mode="nn", name="d_n1", tm=688, tk=D_PROJ, after=tok)
    dh0, d_attn_nw = _rmsnorm_bwd(h0, attn_nw, d_n1, dh1, name="d_attn_norm", also_bf16=False)

    return dict(
        loss=loss[0, 0], grad_x=dh0[HEAD_ROWS:], meta=dh0[ROW_PAD:HEAD_ROWS], attn_nw=d_attn_nw,
        conv_w=d_conv_w8[:CONV_K], a_log=d_alog[:, :GDN_HEADS], dt_bias=d_dt[:, :GDN_HEADS], gdn_nw=d_gdn_nw,
        w2=d_w2p[2 * GDN_HEADS:2 * GDN_HEADS + GLA_RANK], b2=d_b2, gla_nw=d_gla_nw, ffn_nw=d_ffn_nw,
        final_nw=d_final_nw)


SMALL_ROWS = 32


def kernel(x, meta_tokens, attn_norm_w, w_in, gdn_conv_w, gdn_a_log, gdn_dt_bias, gdn_norm_w, gla_gate_w2, gla_gate_b, gla_norm_w, w_out, ffn_norm_w, w_gate, w_up, w_down, final_norm_w, loss_target, m_meta_tokens, m_attn_norm_w, m_w_in, m_gdn_conv_w, m_gdn_a_log, m_gdn_dt_bias, m_gdn_norm_w, m_gla_gate_w2, m_gla_gate_b, m_gla_norm_w, m_w_out, m_ffn_norm_w, m_w_gate, m_w_up, m_w_down, m_final_norm_w, v_meta_tokens, v_attn_norm_w, v_w_in, v_gdn_conv_w, v_gdn_a_log, v_gdn_dt_bias, v_gdn_norm_w, v_gla_gate_w2, v_gla_gate_b, v_gla_norm_w, v_w_out, v_ffn_norm_w, v_w_gate, v_w_up, v_w_down, v_final_norm_w):
    me = 4 * lax.axis_index("x") + 2 * lax.axis_index("y") + lax.axis_index("c")
    n_in, n_ff, n_out = D_IN // N_DEV, D_FF // N_DEV, D_MODEL // N_DEV

    n_conv = gdn_conv_w.shape[2]
    n_w2 = gla_gate_w2.shape[2]
    n_meta = meta_tokens.shape[1]
    small = jnp.zeros((40, n_conv), F32)
    small = small.at[0:N_META, :n_meta].set(meta_tokens)
    small = small.at[N_META:N_META + CONV_K, :].set(gdn_conv_w[0])
    small = small.at[24:24 + GLA_RANK, :n_w2].set(gla_gate_w2[0])
    small_all = _exchange(small, gather=True, name="gather_small")
    meta_f = small_all[:, 0:N_META, :n_meta].transpose(1, 0, 2).reshape(N_META, D_MODEL)
    conv_f = small_all[:, N_META:N_META + CONV_K, :].transpose(1, 0, 2).reshape(CONV_K, N_DEV * n_conv)
    w2_f = small_all[:, 24:24 + GLA_RANK, :n_w2].transpose(1, 0, 2).reshape(GLA_RANK, N_DEV * n_w2)

    o1, o2, o3 = n_ff, 2 * n_ff, 2 * n_ff + n_out
    in_h = _exchange_start(w_in[0].T.astype(BF16), gather=True, name="gather_w_in_start")
    rest = jnp.concatenate([w_gate[0].T, w_up[0].T, w_out[0], w_down[0]], axis=0).astype(BF16)
    rest_h = _exchange_start(rest, gather=True, name="gather_rest_start", after=in_h[4])

    def fetch(name, after):
        handle = in_h if name == "w_in_t" else rest_h
        own, got = _exchange_wait(handle, after, gather=True, name="gather_" + name + "_wait")
        got = lax.dynamic_update_index_in_dim(got, own, me, 0)
        if name == "w_in_t":
            return _to_proj_rows(got.reshape(D_IN, D_MODEL))
        return (got[:, :o1].reshape(D_FF, D_MODEL), got[:, o1:o2].reshape(D_FF, D_MODEL),
                got[:, o2:o3].reshape(D_MODEL, D_MODEL), got[:, o3:].reshape(D_FF, D_MODEL))

    sent = {}

    def emit(name, grad):
        if name == "w_in_t":
            grad = _from_proj_rows(grad)
        parts = grad.reshape(N_DEV, grad.shape[0] // N_DEV, D_MODEL)
        sent[name] = _exchange_start(parts, gather=False, name="scatter_" + name + "_start")
        return sent[name][4]

    g = _local_step(x[0], loss_target[0], meta_f, attn_norm_w, conv_f, gdn_a_log, gdn_dt_bias, gdn_norm_w, w2_f,
                    gla_gate_b, gla_norm_w, ffn_norm_w, final_norm_w.reshape(1, D_MODEL), fetch, emit, start=rest_h[4])

    def total(name, after):
        handle = sent[name]
        own, got = _exchange_wait(handle, after, gather=False, name="scatter_" + name + "_wait")
        got = lax.dynamic_update_index_in_dim(got, lax.dynamic_index_in_dim(own, me, 0, keepdims=False), me, 0)
        return _sum_slabs(got, name="sum_" + name)

    grad_w_down = total("w_down", g["attn_nw"])[None]
    grad_w_gate = total("w_gate_t", grad_w_down)
    grad_w_up = total("w_up_t", grad_w_gate)
    grad_w_out = total("w_out", grad_w_up)[None]
    grad_w_in = total("w_in_t", grad_w_out)

    misc = jnp.concatenate([g["a_log"], g["dt_bias"], g["gdn_nw"], g["gla_nw"], g["b2"], g["loss"].reshape(1, 1)], axis=1)
    n_misc = misc.shape[1]
    misc = jnp.pad(misc, ((0, 0), (0, D_MODEL - n_misc)))
    rows = jnp.concatenate([g["attn_nw"], g["ffn_nw"], g["final_nw"], misc, g["meta"],
                            g["conv_w"].reshape(-1, D_MODEL), g["w2"].reshape(-1, D_MODEL)], axis=0)
    rows = jnp.pad(rows, ((0, SMALL_ROWS - rows.shape[0]), (0, 0)))
    tot = _sum_slabs(_exchange(rows, gather=True, name="gather_small_grads"), name="sum_small_grads")
    grad_attn_nw, grad_ffn_nw, grad_final_nw = tot[0:1], tot[1:2], tot[2]
    grad_a_log = tot[3:4, 0:8]
    grad_dt = tot[3:4, 8:16]
    grad_gdn_nw = tot[3:4, 16:16 + GDN_DV]
    grad_gla_nw = tot[3:4, 144:144 + GLA_DV]
    grad_b2 = tot[3:4, 400:400 + GLA_QK]
    loss = tot[3, n_misc - 1]
    r0 = 4 + N_META
    grad_meta = lax.dynamic_slice(tot[4:r0], (0, me * n_meta), (N_META, n_meta))
    r1 = r0 + CONV_K * N_DEV * n_conv // D_MODEL
    grad_conv = lax.dynamic_slice(tot[r0:r1].reshape(CONV_K, N_DEV * n_conv), (0, me * n_conv), (CONV_K, n_conv))[None]
    r2 = r1 + GLA_RANK * N_DEV * n_w2 // D_MODEL
    grad_w2 = lax.dynamic_slice(tot[r1:r2].reshape(GLA_RANK, N_DEV * n_w2), (0, me * n_w2), (GLA_RANK, n_w2))[None]

    weights = [meta_tokens, attn_norm_w, w_in, gdn_conv_w, gdn_a_log, gdn_dt_bias, gdn_norm_w, gla_gate_w2,
               gla_gate_b, gla_norm_w, w_out, ffn_norm_w, w_gate, w_up, w_down, final_norm_w]
    grads = [grad_meta, grad_attn_nw, grad_w_in, grad_conv, grad_a_log, grad_dt, grad_gdn_nw, grad_w2,
             grad_b2, grad_gla_nw, grad_w_out, grad_ffn_nw, grad_w_gate, grad_w_up, grad_w_down, grad_final_nw]
    ms = [m_meta_tokens, m_attn_norm_w, m_w_in, m_gdn_conv_w, m_gdn_a_log, m_gdn_dt_bias, m_gdn_norm_w,
          m_gla_gate_w2, m_gla_gate_b, m_gla_norm_w, m_w_out, m_ffn_norm_w, m_w_gate, m_w_up, m_w_down, m_final_norm_w]
    vs = [v_meta_tokens, v_attn_norm_w, v_w_in, v_gdn_conv_w, v_gdn_a_log, v_gdn_dt_bias, v_gdn_norm_w,
          v_gla_gate_w2, v_gla_gate_b, v_gla_norm_w, v_w_out, v_ffn_norm_w, v_w_gate, v_w_up, v_w_down, v_final_norm_w]
    transposed = (2, 12, 13)
    outs = [[], [], [], []]
    for idx, (w, gr, m, v) in enumerate(zip(weights, grads, ms, vs)):
        if idx in transposed:
            res = (gr,) + _adamw(w[0].T, gr, m[0].T, v[0].T, name=f"adamw_{idx}")
            res = [t.T[None] for t in res]
        else:
            gr = gr.reshape(w.shape)
            res = (gr,) + _adamw(w, gr, m, v, name=f"adamw_{idx}")
        for lst, t in zip(outs, res):
            lst.append(t)
    return (loss, g["grad_x"][None], *outs[0], *outs[1], *outs[2], *outs[3])
```

```python
import functools

import jax
import jax.numpy as jnp
from jax import lax
from jax.experimental import pallas as pl
from jax.experimental.pallas import tpu as pltpu

F32 = jnp.float32
BF16 = jnp.bfloat16
_MXU_DTYPE = jnp.bfloat16

D_MODEL = 2048
N_META = 16
ROW_PAD = 48
HEAD_ROWS = ROW_PAD + N_META
CONV_K = 4
GDN_HEADS, GDN_DK, GDN_DV, GDN_CHUNK = 8, 128, 128, 64
GLA_HEADS, GLA_DK, GLA_DV, GLA_CHUNK = 4, 128, 256, 16
GLA_RANK = 16
GLA_GATE_NORMALIZER = 16.0
GDN_QK = GDN_HEADS * GDN_DK
GDN_V = GDN_HEADS * GDN_DV
GLA_QK = GLA_HEADS * GLA_DK
GLA_V = GLA_HEADS * GLA_DV
D_FF = 5632
D_IN = 7200
NORM_EPS = 1e-6
C_Z, C_GR, C_GQ, C_GK, C_GV, C_QKV, C_SM = 0, 1024, 2048, 2560, 3072, 4096, 7168
SM_W = 128
D_PROJ = 7680
R_Z, R_A, R_B, R_GQ, R_GK, R_GV, R_GR, R_LR = 3072, 4096, 4104, 4112, 4624, 5136, 6160, 7184

ADAM_LR, ADAM_B1, ADAM_B2, ADAM_EPS, ADAM_WD, ADAM_STEP = 0.001, 0.9, 0.999, 1e-08, 0.01, 10

N_DEV = 8
VMEM_LIMIT = 56 * 1024 * 1024

NN = (((1,), (0,)), ((), ()))
NT = (((1,), (1,)), ((), ()))
TN = (((0,), (0,)), ((), ()))


def _dot(a, b, dims=NN):
    return lax.dot_general(a.astype(_MXU_DTYPE), b.astype(_MXU_DTYPE), dims, preferred_element_type=F32)


def _dotx(a, b, dims=NN):
    return lax.dot_general(a, b, dims, precision=lax.Precision.HIGHEST, preferred_element_type=F32)


def _dot3(a, b):
    ah = a.astype(BF16)
    al = (a - ah.astype(F32)).astype(BF16)
    bh = b.astype(BF16)
    bl = (b - bh.astype(F32)).astype(BF16)
    d = functools.partial(lax.dot_general, dimension_numbers=NN, preferred_element_type=F32)
    return d(ah, bh) + (d(ah, bl) + d(al, bh))


def _tile(n, target, mult=8):
    best = None
    for t in range(mult, min(n, target) + 1, mult):
        if n % t == 0:
            best = t
    return best if best is not None else n


def _params(*sem):
    return pltpu.CompilerParams(dimension_semantics=sem, vmem_limit_bytes=VMEM_LIMIT)


def _sigmoid(x):
    return 0.5 * jnp.tanh(0.5 * x) + 0.5


def _softplus(x):
    return jnp.maximum(x, 0.0) + jnp.log1p(jnp.exp(-jnp.abs(x)))


def _silu_and_grad(c):
    s = _sigmoid(c)
    return c * s, s * (1.0 + c * (1.0 - s))


_ANY = pl.BlockSpec(memory_space=pl.ANY)


def _matmul(a, b, *, mode, name, out_dtype=F32, add=None, after=None, tm=1376, tn=512, tk=2064):
    if mode == "tn":
        K, M = a.shape
        N = b.shape[1]
    else:
        M, K = a.shape
        N = b.shape[0] if mode == "nt" else b.shape[1]
    tm = _tile(M, tm, 128 if mode == "tn" else 16)
    tn = _tile(N, tn, 128)
    tk = _tile(K, tk, 16 if mode == "tn" else 128)
    gm, gn, gk = M // tm, N // tn, K // tk
    dims = {"nn": NN, "nt": NT, "tn": TN}[mode]

    n_after = 0 if after is None else 1

    def body(*refs):
        refs = refs[n_after:]
        if add is None:
            a_ref, b_ref, o_ref = refs[:3]
            add_ref = None
        else:
            a_ref, b_ref, add_ref, o_ref = refs[:4]
        p = _dot(a_ref[...], b_ref[...], dims)

        def finish(r):
            if add_ref is not None:
                r = r + add_ref[...]
            o_ref[...] = r.astype(out_dtype)

        if gk == 1:
            finish(p)
        else:
            acc_ref = refs[-1]
            k = pl.program_id(2)

            @pl.when(k == 0)
            def _():
                acc_ref[...] = p

            @pl.when(k > 0)
            def _():
                acc_ref[...] += p

            @pl.when(k == gk - 1)
            def _():
                finish(acc_ref[...])

    if mode == "tn":
        a_spec = pl.BlockSpec((tk, tm), lambda i, j, k: (k, i))
    else:
        a_spec = pl.BlockSpec((tm, tk), lambda i, j, k: (i, k))
    if mode == "nt":
        b_spec = pl.BlockSpec((tn, tk), lambda i, j, k: (j, k))
    else:
        b_spec = pl.BlockSpec((tk, tn), lambda i, j, k: (k, j))
    o_spec = pl.BlockSpec((tm, tn), lambda i, j, k: (i, j))
    in_specs = [_ANY] * n_after + [a_spec, b_spec] + ([o_spec] if add is not None else [])
    args = ((after,) if n_after else ()) + (a, b) + ((add,) if add is not None else ())
    return pl.pallas_call(
        body, name=name, grid=(gm, gn, gk), in_specs=in_specs, out_specs=o_spec,
        out_shape=jax.ShapeDtypeStruct((M, N), out_dtype),
        scratch_shapes=[pltpu.VMEM((tm, tn), F32)] if gk > 1 else [],
        compiler_params=_params("parallel", "parallel", "arbitrary"),
    )(*args)


def _rmsnorm_fwd(h, w, *, name, after=None):
    M, D = h.shape
    tm = _tile(M, 688, 16)
    n_after = 0 if after is None else 1

    def body(*refs):
        h_ref, w_ref, n_ref = refs[n_after:]
        x = h_ref[...]
        r = lax.rsqrt(jnp.mean(x * x, axis=-1, keepdims=True) + NORM_EPS)
        n_ref[...] = (x * r * w_ref[...]).astype(n_ref.dtype)

    return pl.pallas_call(
        body, name=name, grid=(M // tm,),
        in_specs=[_ANY] * n_after + [pl.BlockSpec((tm, D), lambda i: (i, 0)), pl.BlockSpec((1, D), lambda i: (0, 0))],
        out_specs=pl.BlockSpec((tm, D), lambda i: (i, 0)),
        out_shape=jax.ShapeDtypeStruct((M, D), BF16),
        compiler_params=_params("parallel"),
    )(*((after,) if n_after else ()), h, w)


def _rmsnorm_bwd(h, w, dn, dres, *, name, also_bf16):
    M, D = h.shape
    tm = _tile(M, 344, 16)
    g = M // tm

    def body(h_ref, w_ref, dn_ref, dres_ref, dh_ref, *rest):
        dhb_ref = rest[0] if also_bf16 else None
        dw_ref, acc_ref = rest[-2:]
        i = pl.program_id(0)
        x = h_ref[...]
        r = lax.rsqrt(jnp.mean(x * x, axis=-1, keepdims=True) + NORM_EPS)
        xhat = x * r
        dn_ = dn_ref[...]
        dxhat = dn_ * w_ref[...]
        dh = dres_ref[...] + r * (dxhat - xhat * jnp.mean(dxhat * xhat, axis=-1, keepdims=True))
        dh_ref[...] = dh
        if also_bf16:
            dhb_ref[...] = dh.astype(dhb_ref.dtype)
        part = jnp.sum((dn_ * xhat).reshape(tm // 8, 8, D), axis=0)

        @pl.when(i == 0)
        def _():
            acc_ref[...] = part

        @pl.when(i > 0)
        def _():
            acc_ref[...] += part

        @pl.when(i == g - 1)
        def _():
            dw_ref[...] = jnp.sum(acc_ref[...], axis=0, keepdims=True)

    row = pl.BlockSpec((tm, D), lambda i: (i, 0))
    vec = pl.BlockSpec((1, D), lambda i: (0, 0))
    return pl.pallas_call(
        body, name=name, grid=(g,), in_specs=[row, vec, row, row],
        out_specs=[row] + ([row] if also_bf16 else []) + [vec],
        out_shape=[jax.ShapeDtypeStruct((M, D), F32)] + ([jax.ShapeDtypeStruct((M, D), BF16)] if also_bf16 else [])
        + [jax.ShapeDtypeStruct((1, D), F32)],
        scratch_shapes=[pltpu.VMEM((8, D), F32)],
        compiler_params=_params("arbitrary"),
    )(h, w, dn, dres)


def _loss_head(h, w, target_p, *, name):
    M, D = h.shape
    tm = _tile(M, 344, 16)
    g = M // tm

    def body(h_ref, w_ref, t_ref, dh_ref, dhb_ref, dw_ref, loss_ref, acc_ref, lacc_ref):
        i = pl.program_id(0)
        x = h_ref[...]
        row = i * tm + lax.broadcasted_iota(jnp.int32, (tm, 1), 0)
        live = row >= HEAD_ROWS
        r = lax.rsqrt(jnp.mean(x * x, axis=-1, keepdims=True) + NORM_EPS)
        xhat = x * r
        err = jnp.where(live, xhat * w_ref[...] - t_ref[...], 0.0)
        dy = err * (1.0 / D)
        dxhat = dy * w_ref[...]
        dh = r * (dxhat - xhat * jnp.mean(dxhat * xhat, axis=-1, keepdims=True))
        dh_ref[...] = dh
        dhb_ref[...] = dh.astype(dhb_ref.dtype)
        part = jnp.sum((dy * xhat).reshape(tm // 8, 8, D), axis=0)
        lpart = jnp.sum((err * err).reshape(tm // 8, 8, D), axis=0)

        @pl.when(i == 0)
        def _():
            acc_ref[...] = part
            lacc_ref[...] = lpart

        @pl.when(i > 0)
        def _():
            acc_ref[...] += part
            lacc_ref[...] += lpart

        @pl.when(i == g - 1)
        def _():
            dw_ref[...] = jnp.sum(acc_ref[...], axis=0, keepdims=True)
            tot = jnp.sum(jnp.sum(lacc_ref[...], axis=0, keepdims=True), axis=1, keepdims=True)
            loss_ref[...] = jnp.broadcast_to(tot * (0.5 / D), (1, 128))

    row = pl.BlockSpec((tm, D), lambda i: (i, 0))
    vec = pl.BlockSpec((1, D), lambda i: (0, 0))
    return pl.pallas_call(
        body, name=name, grid=(g,), in_specs=[row, vec, row],
        out_specs=[row, row, vec, pl.BlockSpec((1, 128), lambda i: (0, 0))],
        out_shape=[jax.ShapeDtypeStruct((M, D), F32), jax.ShapeDtypeStruct((M, D), BF16),
                   jax.ShapeDtypeStruct((1, D), F32), jax.ShapeDtypeStruct((1, 128), F32)],
        scratch_shapes=[pltpu.VMEM((8, D), F32), pltpu.VMEM((8, D), F32)],
        compiler_params=_params("arbitrary"),
    )(h, w, target_p)


def _gate_terms(sm, w2p, b2, alog_p, dt_p, row0):
    tm = sm.shape[0]
    lane = lax.broadcasted_iota(jnp.int32, (tm, SM_W), 1)
    live = (row0 + lax.broadcasted_iota(jnp.int32, (tm, 1), 0)) >= ROW_PAD
    pre = sm + dt_p
    neg_a = -jnp.exp(alog_p)
    g = neg_a * _softplus(pre)
    beta = _sigmoid(sm)
    z = _dot(sm, w2p) + b2
    return lane, live, pre, neg_a, g, beta, z


def _gates_fwd(proj, w2p, b2, alog_p, dt_p, *, name):
    M = proj.shape[0]
    tm = _tile(M, 688, 8)

    def body(sm_ref, w2_ref, b2_ref, al_ref, dt_ref, gb_ref, la_ref):
        row0 = pl.program_id(0) * tm
        lane, live, _, _, g, beta, z = _gate_terms(sm_ref[...], w2_ref[...], b2_ref[...], al_ref[...], dt_ref[...], row0)
        gb = jnp.where(lane < GDN_HEADS, g, jnp.where(lane < 2 * GDN_HEADS, beta, 0.0))
        gb_ref[...] = jnp.where(live, gb, 0.0)
        la = (jnp.minimum(z, 0.0) - jnp.log1p(jnp.exp(-jnp.abs(z)))) * (1.0 / GLA_GATE_NORMALIZER)
        la_ref[...] = jnp.where(live, la, 0.0)

    full = lambda s: pl.BlockSpec(s, lambda i: (0, 0))
    return pl.pallas_call(
        body, name=name, grid=(M // tm,),
        in_specs=[pl.BlockSpec((tm, SM_W), lambda i: (i, C_SM // SM_W)), full((SM_W, GLA_QK)), full((1, GLA_QK)),
                  full((1, SM_W)), full((1, SM_W))],
        out_specs=[pl.BlockSpec((tm, SM_W), lambda i: (i, 0)), pl.BlockSpec((tm, GLA_QK), lambda i: (i, 0))],
        out_shape=[jax.ShapeDtypeStruct((M, SM_W), F32), jax.ShapeDtypeStruct((M, GLA_QK), F32)],
        compiler_params=_params("parallel"),
    )(proj, w2p, b2, alog_p, dt_p)


def _gates_bwd(proj, w2p, b2, alog_p, dt_p, dgb_heads, dla, d_proj, *, name):
    M = proj.shape[0]
    tm = _tile(M, 688, 8)
    g_ = M // tm

    tail_w = D_PROJ - C_SM

    def body(sm_ref, w2_ref, b2_ref, al_ref, dt_ref, dgb_ref, dla_ref, _,
             dsm_ref, dw2_ref, db2_ref, dal_ref, ddt_ref):
        i = pl.program_id(0)
        sm = sm_ref[...]
        lane, live, pre, neg_a, g, beta, z = _gate_terms(sm, w2_ref[...], b2_ref[...], al_ref[...], dt_ref[...], i * tm)
        dz = jnp.where(live, dla_ref[...] * (_sigmoid(-z) * (1.0 / GLA_GATE_NORMALIZER)), 0.0)
        dsm_lr = _dot(dz, w2_ref[...], NT)
        dgb = dgb_ref[0]
        for hh in range(1, GDN_HEADS):
            dgb = dgb + dgb_ref[hh]
        dgb = jnp.where(live, dgb, 0.0)
        da = dgb * neg_a * _sigmoid(pre)
        db = dgb * beta * (1.0 - beta)
        dsm = jnp.where(lane < GDN_HEADS, da, jnp.where(lane < 2 * GDN_HEADS, db, dsm_lr))
        dsm_ref[:, 0:SM_W] = dsm.astype(dsm_ref.dtype)
        dsm_ref[:, SM_W:tail_w] = jnp.zeros((tm, tail_w - SM_W), dsm_ref.dtype)
        is_a = lane < GDN_HEADS
        dal = jnp.sum(jnp.where(is_a, dgb * g, 0.0), axis=0, keepdims=True)
        ddt = jnp.sum(jnp.where(is_a, da, 0.0), axis=0, keepdims=True)
        dw2 = _dot(sm, dz, TN)
        db2 = jnp.sum(dz, axis=0, keepdims=True)

        @pl.when(i == 0)
        def _():
            dw2_ref[...] = dw2
            db2_ref[...] = db2
            dal_ref[...] = dal
            ddt_ref[...] = ddt

        @pl.when(i > 0)
        def _():
            dw2_ref[...] += dw2
            db2_ref[...] += db2
            dal_ref[...] += dal
            ddt_ref[...] += ddt

    full = lambda s: pl.BlockSpec(s, lambda i: (0, 0))
    return pl.pallas_call(
        body, name=name, grid=(g_,),
        in_specs=[pl.BlockSpec((tm, SM_W), lambda i: (i, C_SM // SM_W)), full((SM_W, GLA_QK)), full((1, GLA_QK)),
                  full((1, SM_W)), full((1, SM_W)),
                  pl.BlockSpec((GDN_HEADS, tm, SM_W), lambda i: (0, i, 0)),
                  pl.BlockSpec((tm, GLA_QK), lambda i: (i, 0)), _ANY],
        out_specs=[pl.BlockSpec((tm, tail_w), lambda i: (i, C_SM // tail_w)), full((SM_W, GLA_QK)), full((1, GLA_QK)),
                   full((1, SM_W)), full((1, SM_W))],
        out_shape=[jax.ShapeDtypeStruct(d_proj.shape, d_proj.dtype), jax.ShapeDtypeStruct((SM_W, GLA_QK), F32),
                   jax.ShapeDtypeStruct((1, GLA_QK), F32), jax.ShapeDtypeStruct((1, SM_W), F32),
                   jax.ShapeDtypeStruct((1, SM_W), F32)],
        input_output_aliases={7: 0},
        compiler_params=_params("arbitrary"),
    )(proj, w2p, b2, alog_p, dt_p, dgb_heads, dla, d_proj)


QKV_W = GDN_QK
N_QKV_GROUPS = 3
QKV_B0 = C_QKV // QKV_W
HALO = 8


def _conv_terms(x_ref, halo_ref, cw_ref, xs_ref, i, tm):
    xs_ref[HALO:HALO + tm, :] = x_ref[...]
    xs_ref[0:HALO, :] = jnp.where(i > 0, halo_ref[...], 0.0)
    cw = cw_ref[...]
    xs = xs_ref[...]
    taps = [(pltpu.roll(xs, CONV_K - 1 - t, 0) if t < CONV_K - 1 else xs)[HALO:HALO + tm, :] for t in range(CONV_K)]
    c = taps[0] * cw[0:1, :]
    for t in range(1, CONV_K):
        c = c + taps[t] * cw[t:t + 1, :]
    return c, taps


def _prep_fwd(proj, conv_w8, *, name):
    M = proj.shape[0]
    tm = _tile(M, 344, 8)

    def body(x_ref, halo_ref, cw_ref, o_ref, xs_ref):
        j, i = pl.program_id(0), pl.program_id(1)
        c, _ = _conv_terms(x_ref, halo_ref, cw_ref, xs_ref, i, tm)
        s, _ = _silu_and_grad(c)
        scale = jnp.where(j == 0, GDN_DK ** -0.5, 1.0)
        for hh in range(GDN_HEADS):
            cols = slice(hh * 128, (hh + 1) * 128)
            sh = s[:, cols]
            r = lax.rsqrt(jnp.sum(sh * sh, axis=-1, keepdims=True) + NORM_EPS)
            o_ref[:, cols] = jnp.where(j < 2, sh * (r * scale), sh)

    hb = tm // HALO
    return pl.pallas_call(
        body, name=name, grid=(N_QKV_GROUPS, M // tm),
        in_specs=[pl.BlockSpec((tm, QKV_W), lambda j, i: (i, QKV_B0 + j)),
                  pl.BlockSpec((HALO, QKV_W), lambda j, i: (jnp.maximum(i * hb - 1, 0), QKV_B0 + j)),
                  pl.BlockSpec((8, QKV_W), lambda j, i: (0, j))],
        out_specs=pl.BlockSpec((tm, QKV_W), lambda j, i: (i, j)),
        out_shape=jax.ShapeDtypeStruct((M, N_QKV_GROUPS * QKV_W), F32),
        scratch_shapes=[pltpu.VMEM((tm + HALO, QKV_W), F32)],
        compiler_params=_params("parallel", "arbitrary"),
    )(proj, proj, conv_w8)


def _prep_bwd(proj, conv_w8, dact, d_proj, *, name):
    M = proj.shape[0]
    tm = _tile(M, 688, 16)
    g_ = M // tm
    ext = tm + HALO

    def body(x_ref, prev_ref, next_ref, cw_ref, da_ref, dan_ref, _, o_ref, dcw_ref, xs_ref, das_ref, dcs_ref):
        j, i = pl.program_id(0), pl.program_id(1)
        not_last = i < g_ - 1
        xs_ref[0:HALO, :] = jnp.where(i > 0, prev_ref[...], 0.0)
        xs_ref[HALO:HALO + tm, :] = x_ref[...]
        xs_ref[HALO + tm:HALO + ext, :] = jnp.where(not_last, next_ref[...], 0.0)
        das_ref[0:tm, :] = da_ref[...]
        das_ref[tm:ext, :] = jnp.where(not_last, dan_ref[...], 0.0)
        cw = cw_ref[...]
        xs = xs_ref[...]
        taps = [(pltpu.roll(xs, CONV_K - 1 - t, 0) if t < CONV_K - 1 else xs)[HALO:HALO + ext, :] for t in range(CONV_K)]
        c = taps[0] * cw[0:1, :]
        for t in range(1, CONV_K):
            c = c + taps[t] * cw[t:t + 1, :]
        s, ds_dc = _silu_and_grad(c)
        scale = jnp.where(j == 0, GDN_DK ** -0.5, 1.0)
        for hh in range(GDN_HEADS):
            cols = slice(hh * 128, (hh + 1) * 128)
            sh = s[:, cols]
            r = lax.rsqrt(jnp.sum(sh * sh, axis=-1, keepdims=True) + NORM_EPS)
            da = das_ref[:, cols]
            y = sh * r
            dy = da * scale
            ds_norm = r * (dy - y * jnp.sum(dy * y, axis=-1, keepdims=True))
            dcs_ref[:, cols] = jnp.where(j < 2, ds_norm, da) * ds_dc[:, cols]
        dc = dcs_ref[...]
        acc = dc[0:tm, :] * cw[CONV_K - 1:CONV_K, :]
        for t in range(CONV_K - 1):
            acc = acc + pltpu.roll(dc, ext - (CONV_K - 1 - t), 0)[0:tm, :] * cw[t:t + 1, :]
        o_ref[...] = acc.astype(o_ref.dtype)
        r8 = lax.broadcasted_iota(jnp.int32, (8, QKV_W), 0)
        part = jnp.zeros((8, QKV_W), F32)
        for t in range(CONV_K):
            part = jnp.where(r8 == t, jnp.sum(dc[0:tm, :] * taps[t][0:tm, :], axis=0, keepdims=True), part)

        @pl.when(i == 0)
        def _():
            dcw_ref[...] = part

        @pl.when(i > 0)
        def _():
            dcw_ref[...] += part

    hb = tm // HALO
    last = M // HALO - 1
    prev_of = lambda i: jnp.maximum(i * hb - 1, 0)
    next_of = lambda i: jnp.minimum((i + 1) * hb, last)
    return pl.pallas_call(
        body, name=name, grid=(N_QKV_GROUPS, g_),
        in_specs=[pl.BlockSpec((tm, QKV_W), lambda j, i: (i, QKV_B0 + j)),
                  pl.BlockSpec((HALO, QKV_W), lambda j, i: (prev_of(i), QKV_B0 + j)),
                  pl.BlockSpec((HALO, QKV_W), lambda j, i: (next_of(i), QKV_B0 + j)),
                  pl.BlockSpec((8, QKV_W), lambda j, i: (0, j)),
                  pl.BlockSpec((tm, QKV_W), lambda j, i: (i, j)),
                  pl.BlockSpec((HALO, QKV_W), lambda j, i: (next_of(i), j)), _ANY],
        out_specs=[pl.BlockSpec((tm, QKV_W), lambda j, i: (i, QKV_B0 + j)), pl.BlockSpec((8, QKV_W), lambda j, i: (0, j))],
        out_shape=[jax.ShapeDtypeStruct(d_proj.shape, d_proj.dtype),
                   jax.ShapeDtypeStruct((8, N_QKV_GROUPS * QKV_W), F32)],
        input_output_aliases={6: 0},
        scratch_shapes=[pltpu.VMEM((HALO + ext, QKV_W), F32), pltpu.VMEM((ext, QKV_W), F32), pltpu.VMEM((ext, QKV_W), F32)],
        compiler_params=_params("parallel", "arbitrary"),
    )(proj, proj, proj, conv_w8, dact, dact, d_proj)


def _round_robin(gens):
    gens = list(gens)
    while gens:
        alive = []
        for gen in gens:
            try:
                next(gen)
                alive.append(gen)
            except StopIteration:
                pass
        gens = alive


def _unit_lower_inverse(a_low, eye):
    n = a_low.shape[0]
    ri = lax.broadcasted_iota(jnp.int32, (n, n), 0)
    ci = lax.broadcasted_iota(jnp.int32, (n, n), 1)
    same = lambda shift: (ri >> shift) == (ci >> shift)
    b = jnp.where(same(3), -a_low, 0.0)
    x = eye + b
    p2 = _dot3(b, b)
    yield
    x = x + _dot3(x, p2)
    p4 = _dot3(p2, p2)
    yield
    x = x + _dot3(x, p4)
    yield
    for shift in (3, 4, 5):
        between = jnp.where(same(shift + 1) & ~same(shift), a_low, 0.0)
        t = _dot3(between, x)
        yield
        x = x - _dot3(x, t)
        yield
    return x


class _GdnChunk:
    def build(self, q, k, v, gb, h):
        C = GDN_CHUNK
        lane = lax.broadcasted_iota(jnp.int32, (C, SM_W), 1)
        g = jnp.sum(jnp.where(lane == h, gb, 0.0), axis=1, keepdims=True)
        self.beta = jnp.sum(jnp.where(lane == h + GDN_HEADS, gb, 0.0), axis=1, keepdims=True)
        ri = lax.broadcasted_iota(jnp.int32, (C, C), 0)
        ci = lax.broadcasted_iota(jnp.int32, (C, C), 1)
        self.causal = ri >= ci
        self.strict = ri > ci
        self.eye = (ri == ci).astype(F32)
        gcb = _dotx(self.causal.astype(F32), jnp.broadcast_to(g, (C, SM_W)))
        yield
        self.gcol = gcb[:, 0:1]
        grow = gcb.T[0:1, 0:C]
        self.decay = jnp.exp(jnp.where(self.causal, self.gcol - grow, -1e30))
        self.egc = jnp.exp(self.gcol)
        glast = gcb[C - 1:C, 0:1]
        self.elast = jnp.exp(glast - self.gcol)
        self.gl = jnp.exp(glast)
        self.q, self.k, self.v = q, k, v
        self.kb = k * self.beta
        m = _dot(self.kb, k, NT)
        n_ = _dot(q, k, NT)
        yield
        self.a_low = jnp.where(self.strict, m * self.decay, 0.0)
        self.p = n_ * self.decay
        self.qd = q * self.egc
        self.kd = k * self.elast
        self.bu = v * self.beta
        self.bw = self.kb * self.egc


GDN_HB = 8
GDN_HG = GDN_HEADS // GDN_HB


def _gdn_specs(n_of):
    C, W = GDN_CHUNK, 128 * GDN_HB
    q_spec = pl.BlockSpec((C, W), lambda g, n: (n_of(n), g))
    k_spec = pl.BlockSpec((C, W), lambda g, n: (n_of(n), g + GDN_HG))
    v_spec = pl.BlockSpec((C, W), lambda g, n: (n_of(n), g + 2 * GDN_HG))
    gb_spec = pl.BlockSpec((C, SM_W), lambda g, n: (n_of(n), 0))
    o_spec = pl.BlockSpec((C, W), lambda g, n: (n_of(n), g))
    s_spec = pl.BlockSpec((GDN_HB, None, GDN_DK, GDN_DV), lambda g, n: (g, n_of(n), 0, 0))
    t_spec = pl.BlockSpec((GDN_HB, None, C, C), lambda g, n: (g, n_of(n), 0, 0))
    return q_spec, k_spec, v_spec, gb_spec, o_spec, s_spec, t_spec


def _gdn_fwd(act, gb, *, name):
    M = act.shape[0]
    N = M // GDN_CHUNK

    def body(q_ref, k_ref, v_ref, gb_ref, o_ref, s_ref, t_ref, state):
        g, n = pl.program_id(0), pl.program_id(1)

        @pl.when(n == 0)
        def _():
            state[...] = jnp.zeros_like(state)

        gb_ = gb_ref[...]

        def head(hh):
            cols = slice(hh * 128, (hh + 1) * 128)
            c = _GdnChunk()
            yield from c.build(q_ref[:, cols], k_ref[:, cols], v_ref[:, cols], gb_, g * GDN_HB + hh)
            tinv = yield from _unit_lower_inverse(c.a_low, c.eye)
            s = state[hh]
            s_ref[hh] = s
            t_ref[hh] = tinv
            u = _dot(tinv, c.bu)
            w = _dot(tinv, c.bw)
            yield
            vn = u - _dot(w, s)
            o1 = _dot(c.qd, s)
            yield
            o_ref[:, cols] = o1 + _dot(c.p, vn)
            state[hh] = c.gl * s + _dot(c.kd, vn, TN)

        _round_robin(head(hh) for hh in range(GDN_HB))

    q_spec, k_spec, v_spec, gb_spec, o_spec, s_spec, t_spec = _gdn_specs(lambda n: n)
    return pl.pallas_call(
        body, name=name, grid=(GDN_HG, N),
        in_specs=[q_spec, k_spec, v_spec, gb_spec], out_specs=[o_spec, s_spec, t_spec],
        out_shape=[jax.ShapeDtypeStruct((M, GDN_V), F32),
                   jax.ShapeDtypeStruct((GDN_HEADS, N, GDN_DK, GDN_DV), F32),
                   jax.ShapeDtypeStruct((GDN_HEADS, N, GDN_CHUNK, GDN_CHUNK), F32)],
        scratch_shapes=[pltpu.VMEM((GDN_HB, GDN_DK, GDN_DV), F32)],
        compiler_params=_params("parallel", "arbitrary"),
    )(act, act, act, gb)


def _gdn_bwd(act, gb, do, s_all, t_all, *, name):
    M = act.shape[0]
    N = M // GDN_CHUNK
    C = GDN_CHUNK
    assert GDN_HG == 1

    def body(q_ref, k_ref, v_ref, gb_ref, do_ref, s_ref, t_ref, dact_ref, dgb_ref, dstate):
        g, n = pl.program_id(0), pl.program_id(1)

        @pl.when(n == 0)
        def _():
            dstate[...] = jnp.zeros_like(dstate)

        gb_ = gb_ref[...]
        last = lax.broadcasted_iota(jnp.int32, (C, 1), 0) == C - 1
        upper = (lax.broadcasted_iota(jnp.int32, (C, C), 0) <= lax.broadcasted_iota(jnp.int32, (C, C), 1)).astype(F32)
        lane = lax.broadcasted_iota(jnp.int32, (C, SM_W), 1)
        def head(hh):
            cols = slice(hh * 128, (hh + 1) * 128)
            h = g * GDN_HB + hh
            c = _GdnChunk()
            yield from c.build(q_ref[:, cols], k_ref[:, cols], v_ref[:, cols], gb_, h)
            tinv = t_ref[hh]
            s = s_ref[hh]
            do_ = do_ref[:, cols]
            ds1 = dstate[hh]
            u = _dot(tinv, c.bu)
            w = _dot(tinv, c.bw)
            dqd = _dot(do_, s, NT)
            dvn0 = _dot(c.p, do_, TN) + _dot(c.kd, ds1)
            dst0 = _dot(c.qd, do_, TN) + c.gl * ds1
            yield
            vn = u - _dot(w, s)
            dvn = dvn0
            yield
            dp = jnp.where(c.causal, _dot(do_, vn, NT), 0.0)
            dstate[hh] = dst0 - _dot(w, dvn, TN)
            dkd = _dot(vn, ds1, NT)
            dw = -_dot(dvn, s, NT)
            dbu = _dot(tinv, dvn, TN)
            dgl = jnp.sum(jnp.sum(s * ds1, axis=1, keepdims=True), axis=0, keepdims=True)
            yield
            dbw = _dot(tinv, dw, TN)
            t1 = _dot(dbu, u, NT)
            yield
            da = jnp.where(c.strict, -(t1 + _dot(dbw, w, NT)), 0.0)
            dn_ = dp * c.decay
            dq0 = _dot(dn_, c.k)
            dk0 = _dot(dn_, c.q, TN)
            yield
            dm = da * c.decay
            e = da * c.a_low + dp * c.p
            dkb = _dot(dm, c.k) + dbw * c.egc
            dact_ref[:, GDN_QK + hh * 128:GDN_QK + (hh + 1) * 128] = (
                _dot(dm, c.kb, TN) + dk0 + dkb * c.beta + dkd * c.elast)
            dact_ref[:, cols] = dq0 + dqd * c.egc
            dact_ref[:, 2 * GDN_QK + hh * 128:2 * GDN_QK + (hh + 1) * 128] = dbu * c.beta
            dbeta = jnp.sum(dbu * c.v, axis=1, keepdims=True) + jnp.sum(dkb * c.k, axis=1, keepdims=True)
            t_kd = jnp.sum(dkd * c.kd, axis=1, keepdims=True)
            dgc = (jnp.sum(e, axis=1, keepdims=True) - jnp.sum(e.T, axis=1, keepdims=True)
                   + jnp.sum(dbw * c.bw, axis=1, keepdims=True) + jnp.sum(dqd * c.qd, axis=1, keepdims=True) - t_kd)
            dgc = dgc + jnp.where(last, jnp.sum(t_kd, axis=0, keepdims=True) + dgl * c.gl, 0.0)
            yield
            dg = _dotx(upper, jnp.broadcast_to(dgc, (C, SM_W)))
            dgb_ref[hh] = jnp.where(lane == h, dg, jnp.where(lane == h + GDN_HEADS, dbeta, 0.0))

        _round_robin(head(hh) for hh in range(GDN_HB))

    rev = lambda n: N - 1 - n
    q_spec, k_spec, v_spec, gb_spec, o_spec, s_spec, t_spec = _gdn_specs(rev)
    dgb_spec = pl.BlockSpec((GDN_HB, C, SM_W), lambda g, n: (g, rev(n), 0))
    return pl.pallas_call(
        body, name=name, grid=(GDN_HG, N),
        in_specs=[q_spec, k_spec, v_spec, gb_spec, o_spec, s_spec, t_spec],
        out_specs=[pl.BlockSpec((C, 2 * GDN_QK + GDN_V), lambda g, n: (rev(n), 0)), dgb_spec],
        out_shape=[jax.ShapeDtypeStruct((M, 2 * GDN_QK + GDN_V), F32),
                   jax.ShapeDtypeStruct((GDN_HEADS, M, SM_W), F32)],
        scratch_shapes=[pltpu.VMEM((GDN_HB, GDN_DK, GDN_DV), F32)],
        compiler_params=_params("parallel", "arbitrary"),
    )(act, act, act, gb, do, s_all, t_all)


GLA_STEP_ROWS = 64
GLA_SUB = GLA_STEP_ROWS // GLA_CHUNK


def _gla_cumsum(la):
    C = GLA_CHUNK
    ltri = (lax.broadcasted_iota(jnp.int32, (C, C), 0) >= lax.broadcasted_iota(jnp.int32, (C, C), 1)).astype(F32)
    return _dotx(ltri, la)


def _gla_decay_rows(b, i):
    rj = lax.broadcasted_iota(jnp.int32, (GLA_CHUNK, GLA_DK), 0)
    return jnp.where(rj <= i, jnp.exp(jnp.minimum(b[i:i + 1, :] - b, 0.0)), 0.0)


def _gla_scores_t(q, k, b):
    C = GLA_CHUNK
    lane = lax.broadcasted_iota(jnp.int32, (C, C), 1)
    st = jnp.zeros((C, C), F32)
    for i in range(C):
        si = jnp.sum(q[i:i + 1, :] * k * _gla_decay_rows(b, i), axis=1, keepdims=True)
        st = jnp.where(lane == i, si, st)
        if i % 4 == 3:
            yield
    return st


def _gla_specs(n_of):
    R = GLA_STEP_ROWS
    q_spec = pl.BlockSpec((R, GLA_QK), lambda n: (n_of(n), C_GQ // GLA_QK))
    k_spec = pl.BlockSpec((R, GLA_QK), lambda n: (n_of(n), C_GK // GLA_QK))
    v_spec = pl.BlockSpec((R, GLA_V), lambda n: (n_of(n), C_GV // GLA_V))
    la_spec = pl.BlockSpec((R, GLA_QK), lambda n: (n_of(n), 0))
    o_spec = pl.BlockSpec((R, GLA_V), lambda n: (n_of(n), 0))
    s_spec = pl.BlockSpec((GLA_HEADS, None, GLA_SUB, GLA_DV, GLA_DK), lambda n: (0, n_of(n), 0, 0, 0))
    return q_spec, k_spec, v_spec, la_spec, o_spec, s_spec


def _gla_fwd(proj, la, *, name):
    M = proj.shape[0]
    N = M // GLA_STEP_ROWS
    C = GLA_CHUNK

    def body(q_ref, k_ref, v_ref, la_ref, o_ref, s_ref, state):
        n = pl.program_id(0)

        @pl.when(n == 0)
        def _():
            state[...] = jnp.zeros_like(state)

        def head(hh):
            kc = slice(hh * GLA_DK, (hh + 1) * GLA_DK)
            vc = slice(hh * GLA_DV, (hh + 1) * GLA_DV)
            st = state[hh]
            for c in range(GLA_SUB):
                rows = slice(c * C, (c + 1) * C)
                q = q_ref[rows, kc] * (GLA_DK ** -0.5)
                k = k_ref[rows, kc]
                v = v_ref[rows, vc]
                b = _gla_cumsum(la_ref[rows, kc])
                yield
                s_ref[hh, c] = st
                blast = b[C - 1:C, :]
                sc_t = yield from _gla_scores_t(q, k, b)
                o1 = _dot(q * jnp.exp(b), st, NT)
                kv = _dot(v, k * jnp.exp(blast - b), TN)
                o2 = _dot(sc_t, v, TN)
                yield
                o_ref[rows, vc] = o1 + o2
                st = st * jnp.exp(blast) + kv
            state[hh] = st

        _round_robin(head(hh) for hh in range(GLA_HEADS))

    q_spec, k_spec, v_spec, la_spec, o_spec, s_spec = _gla_specs(lambda n: n)
    return pl.pallas_call(
        body, name=name, grid=(N,),
        in_specs=[q_spec, k_spec, v_spec, la_spec], out_specs=[o_spec, s_spec],
        out_shape=[jax.ShapeDtypeStruct((M, GLA_V), F32),
                   jax.ShapeDtypeStruct((GLA_HEADS, N, GLA_SUB, GLA_DV, GLA_DK), F32)],
        scratch_shapes=[pltpu.VMEM((GLA_HEADS, GLA_DV, GLA_DK), F32)],
        compiler_params=_params("arbitrary"),
    )(proj, proj, proj, la)


def _gla_bwd(proj, la, do, s_all, d_proj, *, name):
    M = proj.shape[0]
    N = M // GLA_STEP_ROWS
    C = GLA_CHUNK
    qkv_w = 2 * GLA_QK + GLA_V
    assert C_GK == C_GQ + GLA_QK and C_GV == C_GK + GLA_QK and C_GQ % qkv_w == 0

    def body(q_ref, k_ref, v_ref, la_ref, do_ref, s_ref, _, dp_ref, dla_ref, dstate):
        n = pl.program_id(0)

        @pl.when(n == 0)
        def _():
            dstate[...] = jnp.zeros_like(dstate)

        lane = lax.broadcasted_iota(jnp.int32, (C, C), 1)
        ri = lax.broadcasted_iota(jnp.int32, (C, GLA_DK), 0)
        upper = (lax.broadcasted_iota(jnp.int32, (C, C), 0) <= lane).astype(F32)
        def head(hh):
            kc = slice(hh * GLA_DK, (hh + 1) * GLA_DK)
            vc = slice(hh * GLA_DV, (hh + 1) * GLA_DV)
            ds1 = dstate[hh]
            for c in reversed(range(GLA_SUB)):
                rows = slice(c * C, (c + 1) * C)
                q = q_ref[rows, kc] * (GLA_DK ** -0.5)
                k = k_ref[rows, kc]
                v = v_ref[rows, vc]
                b = _gla_cumsum(la_ref[rows, kc])
                do_ = do_ref[rows, vc]
                st = s_ref[hh, c]
                dsc_t = _dot(v, do_, NT)
                dqe = _dot(do_, st)
                dke = _dot(v, ds1)
                yield
                blast = b[C - 1:C, :]
                eb = jnp.exp(b)
                elast = jnp.exp(blast - b)
                eblast = jnp.exp(blast)
                qe = q * eb
                ke = k * elast
                dv2 = _dot(ke, ds1, NT)
                ds_new = _dot(do_, qe, TN)
                deblast = jnp.sum(st * ds1, axis=0, keepdims=True)
                sc_t = jnp.zeros((C, C), F32)
                dq_sc = jnp.zeros((C, GLA_DK), F32)
                dk_sc = jnp.zeros((C, GLA_DK), F32)
                for i in range(C):
                    f = _gla_decay_rows(b, i)
                    kf = k * f
                    si = jnp.sum(q[i:i + 1, :] * kf, axis=1, keepdims=True)
                    sc_t = jnp.where(lane == i, si, sc_t)
                    dsi = jnp.sum(jnp.where(lane == i, dsc_t, 0.0), axis=1, keepdims=True)
                    dq_sc = jnp.where(ri == i, jnp.sum(dsi * kf, axis=0, keepdims=True), dq_sc)
                    dk_sc = dk_sc + (dsi * f) * q[i:i + 1, :]
                    if i % 4 == 3:
                        yield
                dv1 = _dot(sc_t, do_)
                dp_ref[rows, kc] = ((dq_sc + dqe * eb) * (GLA_DK ** -0.5)).astype(dp_ref.dtype)
                dp_ref[rows, GLA_QK + hh * GLA_DK:GLA_QK + (hh + 1) * GLA_DK] = (dk_sc + dke * elast).astype(dp_ref.dtype)
                t_ke = dke * ke
                db = q * dq_sc - k * dk_sc + dqe * qe - t_ke
                db = db + jnp.where(ri == C - 1, jnp.sum(t_ke, axis=0, keepdims=True) + deblast * eblast, 0.0)
                dla = _dotx(upper, db)
                yield
                dp_ref[rows, 2 * GLA_QK + hh * GLA_DV:2 * GLA_QK + (hh + 1) * GLA_DV] = (dv1 + dv2).astype(dp_ref.dtype)
                dla_ref[rows, kc] = dla
                ds1 = ds1 * eblast + ds_new
            dstate[hh] = ds1

        _round_robin(head(hh) for hh in range(GLA_HEADS))

    rev = lambda n: N - 1 - n
    q_spec, k_spec, v_spec, la_spec, o_spec, s_spec = _gla_specs(rev)
    return pl.pallas_call(
        body, name=name, grid=(N,),
        in_specs=[q_spec, k_spec, v_spec, la_spec, o_spec, s_spec, _ANY],
        out_specs=[pl.BlockSpec((GLA_STEP_ROWS, qkv_w), lambda n: (rev(n), C_GQ // qkv_w)), la_spec],
        out_shape=[jax.ShapeDtypeStruct(d_proj.shape, d_proj.dtype), jax.ShapeDtypeStruct((M, GLA_QK), F32)],
        input_output_aliases={6: 0},
        scratch_shapes=[pltpu.VMEM((GLA_HEADS, GLA_DV, GLA_DK), F32)],
        compiler_params=_params("arbitrary"),
    )(proj, proj, proj, la, do, s_all, d_proj)


def _head_norm(o, wn):
    r = lax.rsqrt(jnp.mean(o * o, axis=-1, keepdims=True) + NORM_EPS)
    return o * r, r


def _mix_heads():
    heads = [(0, GDN_DV, hh * GDN_DV, hh * GDN_DV) for hh in range(GDN_HEADS)]
    heads += [(1, GLA_DV, GDN_V + hh * GLA_DV, hh * GLA_DV) for hh in range(GLA_HEADS)]
    return heads


def _mix_fwd(o_gdn, o_gla, proj, wn_gdn, wn_gla, *, name):
    M = proj.shape[0]
    tm = _tile(M, 344, 16)

    def body(og_ref, ol_ref, z_ref, r_ref, wg_ref, wl_ref, m_ref):
        srcs = ((og_ref, z_ref, wg_ref), (ol_ref, r_ref, wl_ref))
        for grp, width, mcol, col in _mix_heads():
            o_ref, gate_ref, w_ref = srcs[grp]
            xhat, _ = _head_norm(o_ref[:, col:col + width], None)
            gate, _ = _silu_and_grad(gate_ref[:, col:col + width])
            m_ref[:, mcol:mcol + width] = (xhat * w_ref[...] * gate).astype(m_ref.dtype)

    full = lambda s: pl.BlockSpec(s, lambda i: (0, 0))
    return pl.pallas_call(
        body, name=name, grid=(M // tm,),
        in_specs=[pl.BlockSpec((tm, GDN_V), lambda i: (i, 0)), pl.BlockSpec((tm, GLA_V), lambda i: (i, 0)),
                  pl.BlockSpec((tm, GDN_V), lambda i: (i, C_Z // GDN_V)),
                  pl.BlockSpec((tm, GLA_V), lambda i: (i, C_GR // GLA_V)),
                  full((1, GDN_DV)), full((1, GLA_DV))],
        out_specs=pl.BlockSpec((tm, D_MODEL), lambda i: (i, 0)),
        out_shape=jax.ShapeDtypeStruct((M, D_MODEL), BF16),
        compiler_params=_params("parallel"),
    )(o_gdn, o_gla, proj, proj, wn_gdn, wn_gla)


def _mix_bwd(o_gdn, o_gla, proj, wn_gdn, wn_gla, dmixed, *, name):
    M = proj.shape[0]
    tm = _tile(M, 344, 16)
    g_ = M // tm
    assert C_Z == 0 and C_GR == GDN_V

    def body(og_ref, ol_ref, z_ref, r_ref, wg_ref, wl_ref, dm_ref,
             dog_ref, dol_ref, dzr_ref, dwg_ref, dwl_ref):
        i = pl.program_id(0)
        srcs = ((og_ref, z_ref, wg_ref, dog_ref), (ol_ref, r_ref, wl_ref, dol_ref))
        dws = [jnp.zeros((1, GDN_DV), F32), jnp.zeros((1, GLA_DV), F32)]
        for grp, width, mcol, col in _mix_heads():
            o_ref, gate_ref, w_ref, do_ref = srcs[grp]
            cols = slice(col, col + width)
            xhat, r = _head_norm(o_ref[:, cols], None)
            gate, dgate_dc = _silu_and_grad(gate_ref[:, cols])
            dm = dm_ref[:, mcol:mcol + width]
            dzr_ref[:, mcol:mcol + width] = (dm * xhat * w_ref[...] * dgate_dc).astype(dzr_ref.dtype)
            dnorm = dm * gate
            dws[grp] = dws[grp] + jnp.sum(dnorm * xhat, axis=0, keepdims=True)
            dxhat = dnorm * w_ref[...]
            do_ref[:, cols] = r * (dxhat - xhat * jnp.mean(dxhat * xhat, axis=-1, keepdims=True))

        @pl.when(i == 0)
        def _():
            dwg_ref[...] = dws[0]
            dwl_ref[...] = dws[1]

        @pl.when(i > 0)
        def _():
            dwg_ref[...] += dws[0]
            dwl_ref[...] += dws[1]

    full = lambda s: pl.BlockSpec(s, lambda i: (0, 0))
    half = pl.BlockSpec((tm, GDN_V), lambda i: (i, 0))
    return pl.pallas_call(
        body, name=name, grid=(g_,),
        in_specs=[half, half, pl.BlockSpec((tm, GDN_V), lambda i: (i, C_Z // GDN_V)),
                  pl.BlockSpec((tm, GLA_V), lambda i: (i, C_GR // GLA_V)),
                  full((1, GDN_DV)), full((1, GLA_DV)), pl.BlockSpec((tm, D_MODEL), lambda i: (i, 0))],
        out_specs=[half, half, pl.BlockSpec((tm, GDN_V + GLA_V), lambda i: (i, 0)),
                   full((1, GDN_DV)), full((1, GLA_DV))],
        out_shape=[jax.ShapeDtypeStruct((M, GDN_V), F32), jax.ShapeDtypeStruct((M, GLA_V), F32),
                   jax.ShapeDtypeStruct((M, D_PROJ), BF16),
                   jax.ShapeDtypeStruct((1, GDN_DV), F32), jax.ShapeDtypeStruct((1, GLA_DV), F32)],
        compiler_params=_params("arbitrary"),
    )(o_gdn, o_gla, proj, proj, wn_gdn, wn_gla, dmixed)


def _swiglu_fwd(n, w_gate_t, w_up_t, *, name, tm=1376, tn=512):
    M, D = n.shape
    F = w_gate_t.shape[0]
    tm, tn = _tile(M, tm, 16), _tile(F, tn, 128)

    def body(n_ref, wg_ref, wu_ref, g_ref, u_ref, a_ref):
        x = n_ref[...]
        g = _dot(x, wg_ref[...], NT)
        u = _dot(x, wu_ref[...], NT)
        s, _ = _silu_and_grad(g)
        g_ref[...] = g.astype(g_ref.dtype)
        u_ref[...] = u.astype(u_ref.dtype)
        a_ref[...] = (s * u).astype(a_ref.dtype)

    w_spec = pl.BlockSpec((tn, D), lambda i, j: (j, 0))
    o_spec = pl.BlockSpec((tm, tn), lambda i, j: (i, j))
    return pl.pallas_call(
        body, name=name, grid=(M // tm, F // tn),
        in_specs=[pl.BlockSpec((tm, D), lambda i, j: (i, 0)), w_spec, w_spec], out_specs=[o_spec] * 3,
        out_shape=[jax.ShapeDtypeStruct((M, F), BF16)] * 3, compiler_params=_params("parallel", "parallel"),
    )(n, w_gate_t, w_up_t)


def _swiglu_bwd(dh, w_down, gate, up, *, name, after=None, tm=1376, tn=512):
    M, D = dh.shape
    F = w_down.shape[0]
    tm, tn = _tile(M, tm, 16), _tile(F, tn, 128)
    n_after = 0 if after is None else 1

    def body(*refs):
        dh_ref, w_ref, g_ref, u_ref, dg_ref, du_ref = refs[n_after:]
        da = _dot(dh_ref[...], w_ref[...], NT)
        s, ds = _silu_and_grad(g_ref[...].astype(F32))
        dg_ref[...] = (da * u_ref[...].astype(F32) * ds).astype(dg_ref.dtype)
        du_ref[...] = (da * s).astype(du_ref.dtype)

    o_spec = pl.BlockSpec((tm, tn), lambda i, j: (i, j))
    return pl.pallas_call(
        body, name=name, grid=(M // tm, F // tn),
        in_specs=[_ANY] * n_after + [pl.BlockSpec((tm, D), lambda i, j: (i, 0)),
                                     pl.BlockSpec((tn, D), lambda i, j: (j, 0)), o_spec, o_spec],
        out_specs=[o_spec, o_spec], out_shape=[jax.ShapeDtypeStruct((M, F), BF16)] * 2,
        compiler_params=_params("parallel", "parallel"),
    )(*((after,) if n_after else ()), dh, w_down, gate, up)


def _adamw(w, g, m, v, *, name):
    shape = w.shape
    cols = shape[-1]
    rows = w.size // cols
    w2, g2, m2, v2 = (t.reshape(rows, cols) for t in (w, g, m, v))
    if rows % 8 == 0 or cols % 128 != 0:
        tr, tc = (_tile(rows, 256, 8) if rows % 8 == 0 else rows), cols
    else:
        tr, tc = rows, _tile(cols, 256, 128)

    def body(w_ref, g_ref, m_ref, v_ref, d_ref, nm_ref, nv_ref):
        g_ = g_ref[...]
        nm = ADAM_B1 * m_ref[...] + (1.0 - ADAM_B1) * g_
        nv = ADAM_B2 * v_ref[...] + (1.0 - ADAM_B2) * (g_ * g_)
        m_hat = nm / (1.0 - ADAM_B1 ** ADAM_STEP)
        v_hat = nv / (1.0 - ADAM_B2 ** ADAM_STEP)
        d_ref[...] = -ADAM_LR * (m_hat / (jnp.sqrt(v_hat) + ADAM_EPS) + ADAM_WD * w_ref[...])
        nm_ref[...] = nm
        nv_ref[...] = nv

    blk = pl.BlockSpec((tr, tc), lambda i, j: (i, j))
    outs = pl.pallas_call(
        body, name=name, grid=(rows // tr, cols // tc), in_specs=[blk] * 4, out_specs=[blk] * 3,
        out_shape=[jax.ShapeDtypeStruct((rows, cols), F32)] * 3, compiler_params=_params("parallel", "parallel"),
    )(w2, g2, m2, v2)
    return tuple(t.reshape(shape) for t in outs)


def _sum_slabs(x, *, name):
    _, R, C = x.shape
    sub = 16 if x.dtype == BF16 else 8
    if R % sub == 0:
        tr, tc = _tile(R, 128, sub), C
    else:
        tr, tc = R, _tile(C, 256, 128)

    def body(x_ref, o_ref):
        acc = x_ref[0].astype(F32)
        for s in range(1, N_DEV):
            acc = acc + x_ref[s].astype(F32)
        o_ref[...] = acc

    return pl.pallas_call(
        body, name=name, grid=(R // tr, C // tc),
        in_specs=[pl.BlockSpec((N_DEV, tr, tc), lambda i, j: (0, i, j))],
        out_specs=pl.BlockSpec((tr, tc), lambda i, j: (i, j)),
        out_shape=jax.ShapeDtypeStruct((R, C), F32), compiler_params=_params("parallel", "parallel"),
    )(x)


def _peers():
    x, y, c = lax.axis_index("x"), lax.axis_index("y"), lax.axis_index("c")
    me = 4 * x + 2 * y + c
    peers = []
    for k in range(1, N_DEV):
        px = 1 - x if k & 4 else x
        py = 1 - y if k & 2 else y
        pc = 1 - c if k & 1 else c
        peers.append(((px, py, pc), 4 * px + 2 * py + pc))
    return me, peers


def _exchange(x, *, gather, name):
    slab = x.shape if gather else x.shape[1:]

    def body(x_ref, o_ref, send_sems, recv_sems, own_sem):
        me, peers = _peers()
        own = pltpu.make_async_copy(x_ref if gather else x_ref.at[me], o_ref.at[me], own_sem)
        own.start()
        sends, recvs = [], []
        for k, (pos, idx) in enumerate(peers):
            sends.append(pltpu.make_async_remote_copy(
                src_ref=x_ref if gather else x_ref.at[idx], dst_ref=o_ref.at[me],
                send_sem=send_sems.at[k], recv_sem=recv_sems.at[k],
                device_id=pos, device_id_type=pl.DeviceIdType.MESH))
            recvs.append(pltpu.make_async_remote_copy(
                src_ref=x_ref if gather else x_ref.at[idx], dst_ref=o_ref.at[idx],
                send_sem=send_sems.at[k], recv_sem=recv_sems.at[k],
                device_id=pos, device_id_type=pl.DeviceIdType.MESH))
        for cp in sends:
            cp.start()
        for cp in recvs:
            cp.wait_recv()
        for cp in sends:
            cp.wait_send()
        own.wait()

    hbm = pl.BlockSpec(memory_space=pltpu.HBM)
    return pl.pallas_call(
        body, name=name, in_specs=[hbm], out_specs=hbm,
        out_shape=jax.ShapeDtypeStruct((N_DEV,) + tuple(slab), x.dtype),
        scratch_shapes=[pltpu.SemaphoreType.DMA((N_DEV - 1,)), pltpu.SemaphoreType.DMA((N_DEV - 1,)),
                        pltpu.SemaphoreType.DMA],
    )(x)


_HBM = pl.BlockSpec(memory_space=pltpu.HBM)
_SEM = pl.BlockSpec(memory_space=pltpu.SEMAPHORE)
_EFFECT = pltpu.SideEffectType.DATAFLOW_SIDE_EFFECTING


PLAN_GATHER = tuple((k, "x", 0) for k in range(1, N_DEV))
PLAN_SCATTER = tuple((k, "xk", 0) for k in range(1, N_DEV))
PLAN_GATHER_CHIPS = tuple((k, "x", 0) for k in (1, 2, 4, 6))
PLAN_GATHER_PASS_ON = tuple((1, ("land", q), q) for q in (2, 4, 6))


def _plan_refs(plan, j, x_ref, land_ref, me, peers, receiving):
    k, source, r = plan[j]
    index_of = lambda q: me if q == 0 else peers[q - 1][1]
    pos, target = peers[k - 1]
    if source == "x":
        src = x_ref
    elif source == "xk":
        src = x_ref.at[target]
    else:
        src = land_ref.at[index_of(source[1])]
    return pos, src, land_ref.at[index_of(k ^ r) if receiving else index_of(r)]


def _exchange_start(x, *, plan, name, after=None, land=None, slab=None):
    n_after = 0 if after is None else 1
    n = len(plan)

    def body(*refs):
        x_ref, land_ref, send_sems, recv_sems, _, _, token = refs[n_after:]
        me, peers = _peers()
        for j in range(n):
            pos, src, dst = _plan_refs(plan, j, x_ref, land_ref, me, peers, receiving=False)
            pltpu.make_async_remote_copy(src_ref=src, dst_ref=dst, send_sem=send_sems.at[j], recv_sem=recv_sems.at[j],
                                         device_id=pos, device_id_type=pl.DeviceIdType.MESH).start()
        token[...] = jnp.zeros_like(token)

    if land is None:
        land = lax.empty((N_DEV,) + tuple(slab), x.dtype)
    return pl.pallas_call(
        body, name=name,
        out_shape=(pltpu.SemaphoreType.DMA((n,)), pltpu.SemaphoreType.DMA((n,)),
                   pltpu.HBM(x.shape, x.dtype), pltpu.HBM(land.shape, land.dtype), jax.ShapeDtypeStruct((8, 128), F32)),
        in_specs=[_ANY] * n_after + [_HBM, _HBM],
        out_specs=(_SEM, _SEM, _HBM, _HBM, pl.BlockSpec(memory_space=pltpu.VMEM)),
        input_output_aliases={n_after: 2, n_after + 1: 3},
        compiler_params=pltpu.CompilerParams(has_side_effects=_EFFECT),
    )(*((after,) if n_after else ()), pltpu.with_memory_space_constraint(x, pltpu.HBM),
      pltpu.with_memory_space_constraint(land, pltpu.HBM))


def _exchange_wait(handle, after, *, plan, name):
    send_sems, recv_sems, x_thru, land_thru, _ = handle
    afters = list(after) if isinstance(after, (list, tuple)) else [after]

    def body(x_ref, land_ref, send_sems, recv_sems, *rest):
        me, peers = _peers()
        for j in range(len(plan)):
            pos, src, dst = _plan_refs(plan, j, x_ref, land_ref, me, peers, receiving=True)
            cp = pltpu.make_async_remote_copy(src_ref=src, dst_ref=dst, send_sem=send_sems.at[j], recv_sem=recv_sems.at[j],
                                              device_id=pos, device_id_type=pl.DeviceIdType.MESH)
            cp.wait_send()
            cp.wait_recv()

    return pl.pallas_call(
        body, name=name,
        out_shape=(pltpu.HBM(x_thru.shape, x_thru.dtype), pltpu.HBM(land_thru.shape, land_thru.dtype)),
        in_specs=[_HBM, _HBM, _SEM, _SEM] + [_ANY] * len(afters), out_specs=(_HBM, _HBM),
        input_output_aliases={0: 0, 1: 1}, compiler_params=pltpu.CompilerParams(has_side_effects=_EFFECT),
    )(x_thru, land_thru, send_sems, recv_sems, *afters)


W_IN_SLAB = D_IN // N_DEV


def _to_proj_rows(t):
    z = jnp.zeros((D_PROJ - C_SM - 2 * GDN_HEADS - GLA_RANK,) + t.shape[1:], t.dtype)
    return jnp.concatenate([t[R_Z:R_A], t[R_GR:R_LR], t[R_GQ:R_GR], t[:R_Z], t[R_A:R_GQ], t[R_LR:], z], axis=0)


def _from_proj_rows(t):
    ab = C_SM + 2 * GDN_HEADS
    return jnp.concatenate([t[C_QKV:C_SM], t[C_Z:C_GR], t[C_SM:ab], t[C_GQ:C_QKV], t[C_GR:C_GQ],
                            t[ab:ab + GLA_RANK]], axis=0)


def _local_step(x, target, meta, attn_nw, conv_w, a_log, dt_bias, gdn_nw, w2, b2, gla_nw, ffn_nw, final_nw,
                fetch, emit, start=None):
    S = x.shape[0]
    h0 = jnp.concatenate([jnp.zeros((ROW_PAD, D_MODEL), F32), meta, x], axis=0)
    target_p = jnp.concatenate([jnp.zeros((HEAD_ROWS, D_MODEL), F32), target], axis=0)
    conv_w8 = jnp.concatenate([conv_w, jnp.zeros((8 - CONV_K, conv_w.shape[1]), F32)], axis=0)
    w2p = jnp.zeros((SM_W, GLA_QK), F32).at[2 * GDN_HEADS:2 * GDN_HEADS + GLA_RANK].set(w2)
    alog_p = jnp.zeros((1, SM_W), F32).at[:, :GDN_HEADS].set(a_log)
    dt_p = jnp.zeros((1, SM_W), F32).at[:, :GDN_HEADS].set(dt_bias)

    n1 = _rmsnorm_fwd(h0, attn_nw, name="attn_norm", after=start)
    w_in_t = fetch("w_in_t", (n1, target_p, conv_w8, w2p, alog_p, dt_p))
    proj = _matmul(n1, w_in_t, mode="nt", name="in_proj")
    gb, la = _gates_fwd(proj, w2p, b2, alog_p, dt_p, name="gates")
    act = _prep_fwd(proj, conv_w8, name="gdn_prep")
    o_gdn, s_gdn, t_gdn = _gdn_fwd(act, gb, name="gdn_fwd")
    o_gla, s_gla = _gla_fwd(proj, la, name="gla_fwd")
    mixed = _mix_fwd(o_gdn, o_gla, proj, gdn_nw, gla_nw, name="mix")
    w_gate_t, w_up_t, w_out, w_down = fetch("rest", mixed)
    h1 = _matmul(mixed, w_out, mode="nn", add=h0, name="out_proj")
    n2 = _rmsnorm_fwd(h1, ffn_nw, name="ffn_norm")
    gate, up, hid = _swiglu_fwd(n2, w_gate_t, w_up_t, name="swiglu")
    h2 = _matmul(hid, w_down, mode="nn", add=h1, name="ffn_down", tm=688, tk=D_FF)
    dh2, dh2_b, d_final_nw, loss = _loss_head(h2, final_nw, target_p, name="loss_head")

    wg = dict(mode="tn", out_dtype=BF16, tn=512, tk=S + HEAD_ROWS)
    tok = emit("w_down", _matmul(hid, dh2_b, name="d_w_down", tm=704, **wg))
    d_gate, d_up = _swiglu_bwd(dh2_b, w_down, gate, up, name="d_swiglu", after=tok)
    tok = emit("w_gate_t", _matmul(d_gate, n2, name="d_w_gate", tm=704, **wg))
    tok = emit("w_up_t", _matmul(d_up, n2, name="d_w_up", tm=704, after=tok, **wg))
    d_n2 = _matmul(d_gate, w_gate_t, mode="nn", name="d_n2_gate", tm=688, tk=D_FF, after=tok)
    d_n2 = _matmul(d_up, w_up_t, mode="nn", add=d_n2, name="d_n2_up", tm=688, tk=D_FF)
    dh1, dh1_b, d_ffn_nw = _rmsnorm_bwd(h1, ffn_nw, d_n2, dh2, name="d_ffn_norm", also_bf16=True)

    tok = emit("w_out", _matmul(mixed, dh1_b, name="d_w_out", tm=512, **wg))
    d_mixed = _matmul(dh1_b, w_out, mode="nt", name="d_mixed", after=tok)
    do_gdn, do_gla, d_proj, d_gdn_nw, d_gla_nw = _mix_bwd(o_gdn, o_gla, proj, gdn_nw, gla_nw, d_mixed, name="d_mix")
    d_proj, d_la = _gla_bwd(proj, la, do_gla, s_gla, d_proj, name="gla_bwd")
    dact, dgb_heads = _gdn_bwd(act, gb, do_gdn, s_gdn, t_gdn, name="gdn_bwd")
    d_proj, d_w2p, d_b2, d_alog, d_dt = _gates_bwd(proj, w2p, b2, alog_p, dt_p, dgb_heads, d_la, d_proj, name="d_gates")
    d_proj, d_conv_w8 = _prep_bwd(proj, conv_w8, dact, d_proj, name="d_gdn_prep")
    tok = emit("w_in_t", _matmul(d_proj, n1, name="d_w_in", tm=768, **wg))
    d_n1 = _matmul(d_proj, w_in_t, mode="nn", name="d_n1", tm=688, tk=D_PROJ, after=tok)
    dh0, d_attn_nw = _rmsnorm_bwd(h0, attn_nw, d_n1, dh1, name="d_attn_norm", also_bf16=False)

    return dict(
        loss=loss[0, 0], grad_x=dh0[HEAD_ROWS:], meta=dh0[ROW_PAD:HEAD_ROWS], attn_nw=d_attn_nw,
        conv_w=d_conv_w8[:CONV_K], a_log=d_alog[:, :GDN_HEADS], dt_bias=d_dt[:, :GDN_HEADS], gdn_nw=d_gdn_nw,
        w2=d_w2p[2 * GDN_HEADS:2 * GDN_HEADS + GLA_RANK], b2=d_b2, gla_nw=d_gla_nw, ffn_nw=d_ffn_nw,
        final_nw=d_final_nw)


SMALL_ROWS = 32


def kernel(x, meta_tokens, attn_norm_w, w_in, gdn_conv_w, gdn_a_log, gdn_dt_bias, gdn_norm_w, gla_gate_w2, gla_gate_b, gla_norm_w, w_out, ffn_norm_w, w_gate, w_up, w_down, final_norm_w, loss_target, m_meta_tokens, m_attn_norm_w, m_w_in, m_gdn_conv_w, m_gdn_a_log, m_gdn_dt_bias, m_gdn_norm_w, m_gla_gate_w2, m_gla_gate_b, m_gla_norm_w, m_w_out, m_ffn_norm_w, m_w_gate, m_w_up, m_w_down, m_final_norm_w, v_meta_tokens, v_attn_norm_w, v_w_in, v_gdn_conv_w, v_gdn_a_log, v_gdn_dt_bias, v_gdn_norm_w, v_gla_gate_w2, v_gla_gate_b, v_gla_norm_w, v_w_out, v_ffn_norm_w, v_w_gate, v_w_up, v_w_down, v_final_norm_w):
    me = 4 * lax.axis_index("x") + 2 * lax.axis_index("y") + lax.axis_index("c")
    n_in, n_ff, n_out = D_IN // N_DEV, D_FF // N_DEV, D_MODEL // N_DEV

    n_conv = gdn_conv_w.shape[2]
    n_w2 = gla_gate_w2.shape[2]
    n_meta = meta_tokens.shape[1]
    small = jnp.zeros((40, n_conv), F32)
    small = small.at[0:N_META, :n_meta].set(meta_tokens)
    small = small.at[N_META:N_META + CONV_K, :].set(gdn_conv_w[0])
    small = small.at[24:24 + GLA_RANK, :n_w2].set(gla_gate_w2[0])
    small_all = _exchange(small, gather=True, name="gather_small")
    meta_f = small_all[:, 0:N_META, :n_meta].transpose(1, 0, 2).reshape(N_META, D_MODEL)
    conv_f = small_all[:, N_META:N_META + CONV_K, :].transpose(1, 0, 2).reshape(CONV_K, N_DEV * n_conv)
    w2_f = small_all[:, 24:24 + GLA_RANK, :n_w2].transpose(1, 0, 2).reshape(GLA_RANK, N_DEV * n_w2)

    o1, o2, o3 = n_ff, 2 * n_ff, 2 * n_ff + n_out
    w_in_slab = w_in[0].T.astype(BF16)
    in_h = _exchange_start(w_in_slab, plan=PLAN_GATHER_CHIPS, slab=w_in_slab.shape, name="gather_w_in_start")
    rest = jnp.concatenate([w_gate[0].T, w_up[0].T, w_out[0], w_down[0]], axis=0).astype(BF16)
    rest_h = _exchange_start(rest, plan=PLAN_GATHER, slab=rest.shape, name="gather_rest_start", after=in_h[4])

    def fetch(name, after):
        if name == "w_in_t":
            own, got = _exchange_wait(in_h, after, plan=PLAN_GATHER_CHIPS, name="gather_w_in_wait")
            pass_h = _exchange_start(own, plan=PLAN_GATHER_PASS_ON, land=got, name="pass_w_in_start")
            own, got = _exchange_wait(pass_h, pass_h[4], plan=PLAN_GATHER_PASS_ON, name="pass_w_in_wait")
            got = lax.dynamic_update_index_in_dim(got, own, me, 0)
            return _to_proj_rows(got.reshape(D_IN, D_MODEL))
        own, got = _exchange_wait(rest_h, after, plan=PLAN_GATHER, name="gather_rest_wait")
        got = lax.dynamic_update_index_in_dim(got, own, me, 0)
        return (got[:, :o1].reshape(D_FF, D_MODEL), got[:, o1:o2].reshape(D_FF, D_MODEL),
                got[:, o2:o3].reshape(D_MODEL, D_MODEL), got[:, o3:].reshape(D_FF, D_MODEL))

    sent = {}

    def emit(name, grad):
        if name == "w_in_t":
            grad = _from_proj_rows(grad)
        parts = grad.reshape(N_DEV, grad.shape[0] // N_DEV, D_MODEL)
        sent[name] = _exchange_start(parts, plan=PLAN_SCATTER, slab=parts.shape[1:], name="scatter_" + name + "_start")
        return sent[name][4]

    g = _local_step(x[0], loss_target[0], meta_f, attn_norm_w, conv_f, gdn_a_log, gdn_dt_bias, gdn_norm_w, w2_f,
                    gla_gate_b, gla_norm_w, ffn_norm_w, final_norm_w.reshape(1, D_MODEL), fetch, emit, start=rest_h[4])

    def total(name, after):
        handle = sent[name]
        own, got = _exchange_wait(handle, after, plan=PLAN_SCATTER, name="scatter_" + name + "_wait")
        got = lax.dynamic_update_index_in_dim(got, lax.dynamic_index_in_dim(own, me, 0, keepdims=False), me, 0)
        return _sum_slabs(got, name="sum_" + name)

    grad_w_down = total("w_down", g["attn_nw"])[None]
    grad_w_gate = total("w_gate_t", grad_w_down)
    grad_w_up = total("w_up_t", grad_w_gate)
    grad_w_out = total("w_out", grad_w_up)[None]
    grad_w_in = total("w_in_t", grad_w_out)

    misc = jnp.concatenate([g["a_log"], g["dt_bias"], g["gdn_nw"], g["gla_nw"], g["b2"], g["loss"].reshape(1, 1)], axis=1)
    n_misc = misc.shape[1]
    misc = jnp.pad(misc, ((0, 0), (0, D_MODEL - n_misc)))
    rows = jnp.concatenate([g["attn_nw"], g["ffn_nw"], g["final_nw"], misc, g["meta"],
                            g["conv_w"].reshape(-1, D_MODEL), g["w2"].reshape(-1, D_MODEL)], axis=0)
    rows = jnp.pad(rows, ((0, SMALL_ROWS - rows.shape[0]), (0, 0)))
    tot = _sum_slabs(_exchange(rows, gather=True, name="gather_small_grads"), name="sum_small_grads")
    grad_attn_nw, grad_ffn_nw, grad_final_nw = tot[0:1], tot[1:2], tot[2]
    grad_a_log = tot[3:4, 0:8]
    grad_dt = tot[3:4, 8:16]
    grad_gdn_nw = tot[3:4, 16:16 + GDN_DV]
    grad_gla_nw = tot[3:4, 144:144 + GLA_DV]
    grad_b2 = tot[3:4, 400:400 + GLA_QK]
    loss = tot[3, n_misc - 1]
    r0 = 4 + N_META
    grad_meta = lax.dynamic_slice(tot[4:r0], (0, me * n_meta), (N_META, n_meta))
    r1 = r0 + CONV_K * N_DEV * n_conv // D_MODEL
    grad_conv = lax.dynamic_slice(tot[r0:r1].reshape(CONV_K, N_DEV * n_conv), (0, me * n_conv), (CONV_K, n_conv))[None]
    r2 = r1 + GLA_RANK * N_DEV * n_w2 // D_MODEL
    grad_w2 = lax.dynamic_slice(tot[r1:r2].reshape(GLA_RANK, N_DEV * n_w2), (0, me * n_w2), (GLA_RANK, n_w2))[None]

    weights = [meta_tokens, attn_norm_w, w_in, gdn_conv_w, gdn_a_log, gdn_dt_bias, gdn_norm_w, gla_gate_w2,
               gla_gate_b, gla_norm_w, w_out, ffn_norm_w, w_gate, w_up, w_down, final_norm_w]
    grads = [grad_meta, grad_attn_nw, grad_w_in, grad_conv, grad_a_log, grad_dt, grad_gdn_nw, grad_w2,
             grad_b2, grad_gla_nw, grad_w_out, grad_ffn_nw, grad_w_gate, grad_w_up, grad_w_down, grad_final_nw]
    ms = [m_meta_tokens, m_attn_norm_w, m_w_in, m_gdn_conv_w, m_gdn_a_log, m_gdn_dt_bias, m_gdn_norm_w,
          m_gla_gate_w2, m_gla_gate_b, m_gla_norm_w, m_w_out, m_ffn_norm_w, m_w_gate, m_w_up, m_w_down, m_final_norm_w]
    vs = [v_meta_tokens, v_attn_norm_w, v_w_in, v_gdn_conv_w, v_gdn_a_log, v_gdn_dt_bias, v_gdn_norm_w,
          v_gla_gate_w2, v_gla_gate_b, v_gla_norm_w, v_w_out, v_ffn_norm_w, v_w_gate, v_w_up, v_w_down, v_final_norm_w]
    transposed = (2, 12, 13)
    outs = [[], [], [], []]
    for idx, (w, gr, m, v) in enumerate(zip(weights, grads, ms, vs)):
        if idx in transposed:
            res = (gr,) + _adamw(w[0].T, gr, m[0].T, v[0].T, name=f"adamw_{idx}")
            res = [t.T[None] for t in res]
        else:
            gr = gr.reshape(w.shape)
            res = (gr,) + _adamw(w, gr, m, v, name=f"adamw_{idx}")
        for lst, t in zip(outs, res):
            lst.append(t)
    return (loss, g["grad_x"][None], *outs[0], *outs[1], *outs[2], *outs[3])
```

```python
import functools

import jax
import jax.numpy as jnp
from jax import lax
from jax.experimental import pallas as pl
from jax.experimental.pallas import tpu as pltpu

F32 = jnp.float32
BF16 = jnp.bfloat16
_MXU_DTYPE = jnp.bfloat16

D_MODEL = 2048
N_META = 16
ROW_PAD = 48
HEAD_ROWS = ROW_PAD + N_META
CONV_K = 4
GDN_HEADS, GDN_DK, GDN_DV, GDN_CHUNK = 8, 128, 128, 64
GLA_HEADS, GLA_DK, GLA_DV, GLA_CHUNK = 4, 128, 256, 16
GLA_RANK = 16
GLA_GATE_NORMALIZER = 16.0
GDN_QK = GDN_HEADS * GDN_DK
GDN_V = GDN_HEADS * GDN_DV
GLA_QK = GLA_HEADS * GLA_DK
GLA_V = GLA_HEADS * GLA_DV
D_FF = 5632
D_IN = 7200
NORM_EPS = 1e-6
C_Z, C_GR, C_GQ, C_GK, C_GV, C_QKV, C_SM = 0, 1024, 2048, 2560, 3072, 4096, 7168
SM_W = 128
D_PROJ = 7680
R_Z, R_A, R_B, R_GQ, R_GK, R_GV, R_GR, R_LR = 3072, 4096, 4104, 4112, 4624, 5136, 6160, 7184

ADAM_LR, ADAM_B1, ADAM_B2, ADAM_EPS, ADAM_WD, ADAM_STEP = 0.001, 0.9, 0.999, 1e-08, 0.01, 10

N_DEV = 8
VMEM_LIMIT = 56 * 1024 * 1024

NN = (((1,), (0,)), ((), ()))
NT = (((1,), (1,)), ((), ()))
TN = (((0,), (0,)), ((), ()))


def _dot(a, b, dims=NN):
    return lax.dot_general(a.astype(_MXU_DTYPE), b.astype(_MXU_DTYPE), dims, preferred_element_type=F32)


def _dotx(a, b, dims=NN):
    return lax.dot_general(a, b, dims, precision=lax.Precision.HIGHEST, preferred_element_type=F32)


def _dot3(a, b):
    ah = a.astype(BF16)
    al = (a - ah.astype(F32)).astype(BF16)
    bh = b.astype(BF16)
    bl = (b - bh.astype(F32)).astype(BF16)
    d = functools.partial(lax.dot_general, dimension_numbers=NN, preferred_element_type=F32)
    return d(ah, bh) + (d(ah, bl) + d(al, bh))


def _tile(n, target, mult=8):
    best = None
    for t in range(mult, min(n, target) + 1, mult):
        if n % t == 0:
            best = t
    return best if best is not None else n


def _params(*sem):
    return pltpu.CompilerParams(dimension_semantics=sem, vmem_limit_bytes=VMEM_LIMIT)


def _sigmoid(x):
    return 0.5 * jnp.tanh(0.5 * x) + 0.5


def _softplus(x):
    return jnp.maximum(x, 0.0) + jnp.log1p(jnp.exp(-jnp.abs(x)))


def _silu_and_grad(c):
    s = _sigmoid(c)
    return c * s, s * (1.0 + c * (1.0 - s))


_ANY = pl.BlockSpec(memory_space=pl.ANY)


def _matmul(a, b, *, mode, name, out_dtype=F32, add=None, after=None, tm=1376, tn=512, tk=2064):
    if mode == "tn":
        K, M = a.shape
        N = b.shape[1]
    else:
        M, K = a.shape
        N = b.shape[0] if mode == "nt" else b.shape[1]
    tm = _tile(M, tm, 128 if mode == "tn" else 16)
    tn = _tile(N, tn, 128)
    tk = _tile(K, tk, 16 if mode == "tn" else 128)
    gm, gn, gk = M // tm, N // tn, K // tk
    dims = {"nn": NN, "nt": NT, "tn": TN}[mode]

    n_after = 0 if after is None else 1

    def body(*refs):
        refs = refs[n_after:]
        if add is None:
            a_ref, b_ref, o_ref = refs[:3]
            add_ref = None
        else:
            a_ref, b_ref, add_ref, o_ref = refs[:4]
        p = _dot(a_ref[...], b_ref[...], dims)

        def finish(r):
            if add_ref is not None:
                r = r + add_ref[...]
            o_ref[...] = r.astype(out_dtype)

        if gk == 1:
            finish(p)
        else:
            acc_ref = refs[-1]
            k = pl.program_id(2)

            @pl.when(k == 0)
            def _():
                acc_ref[...] = p

            @pl.when(k > 0)
            def _():
                acc_ref[...] += p

            @pl.when(k == gk - 1)
            def _():
                finish(acc_ref[...])

    if mode == "tn":
        a_spec = pl.BlockSpec((tk, tm), lambda i, j, k: (k, i))
    else:
        a_spec = pl.BlockSpec((tm, tk), lambda i, j, k: (i, k))
    if mode == "nt":
        b_spec = pl.BlockSpec((tn, tk), lambda i, j, k: (j, k))
    else:
        b_spec = pl.BlockSpec((tk, tn), lambda i, j, k: (k, j))
    o_spec = pl.BlockSpec((tm, tn), lambda i, j, k: (i, j))
    in_specs = [_ANY] * n_after + [a_spec, b_spec] + ([o_spec] if add is not None else [])
    args = ((after,) if n_after else ()) + (a, b) + ((add,) if add is not None else ())
    return pl.pallas_call(
        body, name=name, grid=(gm, gn, gk), in_specs=in_specs, out_specs=o_spec,
        out_shape=jax.ShapeDtypeStruct((M, N), out_dtype),
        scratch_shapes=[pltpu.VMEM((tm, tn), F32)] if gk > 1 else [],
        compiler_params=_params("parallel", "parallel", "arbitrary"),
    )(*args)


def _rmsnorm_fwd(h, w, *, name, after=None):
    M, D = h.shape
    tm = _tile(M, 688, 16)
    n_after = 0 if after is None else 1

    def body(*refs):
        h_ref, w_ref, n_ref = refs[n_after:]
        x = h_ref[...]
        r = lax.rsqrt(jnp.mean(x * x, axis=-1, keepdims=True) + NORM_EPS)
        n_ref[...] = (x * r * w_ref[...]).astype(n_ref.dtype)

    return pl.pallas_call(
        body, name=name, grid=(M // tm,),
        in_specs=[_ANY] * n_after + [pl.BlockSpec((tm, D), lambda i: (i, 0)), pl.BlockSpec((1, D), lambda i: (0, 0))],
        out_specs=pl.BlockSpec((tm, D), lambda i: (i, 0)),
        out_shape=jax.ShapeDtypeStruct((M, D), BF16),
        compiler_params=_params("parallel"),
    )(*((after,) if n_after else ()), h, w)


def _rmsnorm_bwd(h, w, dn, dres, *, name, also_bf16):
    M, D = h.shape
    tm = _tile(M, 344, 16)
    g = M // tm

    def body(h_ref, w_ref, dn_ref, dres_ref, dh_ref, *rest):
        dhb_ref = rest[0] if also_bf16 else None
        dw_ref, acc_ref = rest[-2:]
        i = pl.program_id(0)
        x = h_ref[...]
        r = lax.rsqrt(jnp.mean(x * x, axis=-1, keepdims=True) + NORM_EPS)
        xhat = x * r
        dn_ = dn_ref[...]
        dxhat = dn_ * w_ref[...]
        dh = dres_ref[...] + r * (dxhat - xhat * jnp.mean(dxhat * xhat, axis=-1, keepdims=True))
        dh_ref[...] = dh
        if also_bf16:
            dhb_ref[...] = dh.astype(dhb_ref.dtype)
        part = jnp.sum((dn_ * xhat).reshape(tm // 8, 8, D), axis=0)

        @pl.when(i == 0)
        def _():
            acc_ref[...] = part

        @pl.when(i > 0)
        def _():
            acc_ref[...] += part

        @pl.when(i == g - 1)
        def _():
            dw_ref[...] = jnp.sum(acc_ref[...], axis=0, keepdims=True)

    row = pl.BlockSpec((tm, D), lambda i: (i, 0))
    vec = pl.BlockSpec((1, D), lambda i: (0, 0))
    return pl.pallas_call(
        body, name=name, grid=(g,), in_specs=[row, vec, row, row],
        out_specs=[row] + ([row] if also_bf16 else []) + [vec],
        out_shape=[jax.ShapeDtypeStruct((M, D), F32)] + ([jax.ShapeDtypeStruct((M, D), BF16)] if also_bf16 else [])
        + [jax.ShapeDtypeStruct((1, D), F32)],
        scratch_shapes=[pltpu.VMEM((8, D), F32)],
        compiler_params=_params("arbitrary"),
    )(h, w, dn, dres)


def _loss_head(h, w, target_p, *, name):
    M, D = h.shape
    tm = _tile(M, 344, 16)
    g = M // tm

    def body(h_ref, w_ref, t_ref, dh_ref, dhb_ref, dw_ref, loss_ref, acc_ref, lacc_ref):
        i = pl.program_id(0)
        x = h_ref[...]
        row = i * tm + lax.broadcasted_iota(jnp.int32, (tm, 1), 0)
        live = row >= HEAD_ROWS
        r = lax.rsqrt(jnp.mean(x * x, axis=-1, keepdims=True) + NORM_EPS)
        xhat = x * r
        err = jnp.where(live, xhat * w_ref[...] - t_ref[...], 0.0)
        dy = err * (1.0 / D)
        dxhat = dy * w_ref[...]
        dh = r * (dxhat - xhat * jnp.mean(dxhat * xhat, axis=-1, keepdims=True))
        dh_ref[...] = dh
        dhb_ref[...] = dh.astype(dhb_ref.dtype)
        part = jnp.sum((dy * xhat).reshape(tm // 8, 8, D), axis=0)
        lpart = jnp.sum((err * err).reshape(tm // 8, 8, D), axis=0)

        @pl.when(i == 0)
        def _():
            acc_ref[...] = part
            lacc_ref[...] = lpart

        @pl.when(i > 0)
        def _():
            acc_ref[...] += part
            lacc_ref[...] += lpart

        @pl.when(i == g - 1)
        def _():
            dw_ref[...] = jnp.sum(acc_ref[...], axis=0, keepdims=True)
            tot = jnp.sum(jnp.sum(lacc_ref[...], axis=0, keepdims=True), axis=1, keepdims=True)
            loss_ref[...] = jnp.broadcast_to(tot * (0.5 / D), (1, 128))

    row = pl.BlockSpec((tm, D), lambda i: (i, 0))
    vec = pl.BlockSpec((1, D), lambda i: (0, 0))
    return pl.pallas_call(
        body, name=name, grid=(g,), in_specs=[row, vec, row],
        out_specs=[row, row, vec, pl.BlockSpec((1, 128), lambda i: (0, 0))],
        out_shape=[jax.ShapeDtypeStruct((M, D), F32), jax.ShapeDtypeStruct((M, D), BF16),
                   jax.ShapeDtypeStruct((1, D), F32), jax.ShapeDtypeStruct((1, 128), F32)],
        scratch_shapes=[pltpu.VMEM((8, D), F32), pltpu.VMEM((8, D), F32)],
        compiler_params=_params("arbitrary"),
    )(h, w, target_p)


def _gate_terms(sm, w2p, b2, alog_p, dt_p, row0):
    tm = sm.shape[0]
    lane = lax.broadcasted_iota(jnp.int32, (tm, SM_W), 1)
    live = (row0 + lax.broadcasted_iota(jnp.int32, (tm, 1), 0)) >= ROW_PAD
    pre = sm + dt_p
    neg_a = -jnp.exp(alog_p)
    g = neg_a * _softplus(pre)
    beta = _sigmoid(sm)
    z = _dot(sm, w2p) + b2
    return lane, live, pre, neg_a, g, beta, z


def _gates_fwd(proj, w2p, b2, alog_p, dt_p, *, name):
    M = proj.shape[0]
    tm = _tile(M, 688, 8)

    def body(sm_ref, w2_ref, b2_ref, al_ref, dt_ref, gb_ref, la_ref):
        row0 = pl.program_id(0) * tm
        lane, live, _, _, g, beta, z = _gate_terms(sm_ref[...], w2_ref[...], b2_ref[...], al_ref[...], dt_ref[...], row0)
        gb = jnp.where(lane < GDN_HEADS, g, jnp.where(lane < 2 * GDN_HEADS, beta, 0.0))
        gb_ref[...] = jnp.where(live, gb, 0.0)
        la = (jnp.minimum(z, 0.0) - jnp.log1p(jnp.exp(-jnp.abs(z)))) * (1.0 / GLA_GATE_NORMALIZER)
        la_ref[...] = jnp.where(live, la, 0.0)

    full = lambda s: pl.BlockSpec(s, lambda i: (0, 0))
    return pl.pallas_call(
        body, name=name, grid=(M // tm,),
        in_specs=[pl.BlockSpec((tm, SM_W), lambda i: (i, C_SM // SM_W)), full((SM_W, GLA_QK)), full((1, GLA_QK)),
                  full((1, SM_W)), full((1, SM_W))],
        out_specs=[pl.BlockSpec((tm, SM_W), lambda i: (i, 0)), pl.BlockSpec((tm, GLA_QK), lambda i: (i, 0))],
        out_shape=[jax.ShapeDtypeStruct((M, SM_W), F32), jax.ShapeDtypeStruct((M, GLA_QK), F32)],
        compiler_params=_params("parallel"),
    )(proj, w2p, b2, alog_p, dt_p)


def _gates_bwd(proj, w2p, b2, alog_p, dt_p, dgb_heads, dla, d_proj, *, name):
    M = proj.shape[0]
    tm = _tile(M, 688, 8)
    g_ = M // tm

    tail_w = D_PROJ - C_SM

    def body(sm_ref, w2_ref, b2_ref, al_ref, dt_ref, dgb_ref, dla_ref, _,
             dsm_ref, dw2_ref, db2_ref, dal_ref, ddt_ref):
        i = pl.program_id(0)
        sm = sm_ref[...]
        lane, live, pre, neg_a, g, beta, z = _gate_terms(sm, w2_ref[...], b2_ref[...], al_ref[...], dt_ref[...], i * tm)
        dz = jnp.where(live, dla_ref[...] * (_sigmoid(-z) * (1.0 / GLA_GATE_NORMALIZER)), 0.0)
        dsm_lr = _dot(dz, w2_ref[...], NT)
        dgb = dgb_ref[0]
        for hh in range(1, GDN_HEADS):
            dgb = dgb + dgb_ref[hh]
        dgb = jnp.where(live, dgb, 0.0)
        da = dgb * neg_a * _sigmoid(pre)
        db = dgb * beta * (1.0 - beta)
        dsm = jnp.where(lane < GDN_HEADS, da, jnp.where(lane < 2 * GDN_HEADS, db, dsm_lr))
        dsm_ref[:, 0:SM_W] = dsm.astype(dsm_ref.dtype)
        dsm_ref[:, SM_W:tail_w] = jnp.zeros((tm, tail_w - SM_W), dsm_ref.dtype)
        is_a = lane < GDN_HEADS
        dal = jnp.sum(jnp.where(is_a, dgb * g, 0.0), axis=0, keepdims=True)
        ddt = jnp.sum(jnp.where(is_a, da, 0.0), axis=0, keepdims=True)
        dw2 = _dot(sm, dz, TN)
        db2 = jnp.sum(dz, axis=0, keepdims=True)

        @pl.when(i == 0)
        def _():
            dw2_ref[...] = dw2
            db2_ref[...] = db2
            dal_ref[...] = dal
            ddt_ref[...] = ddt

        @pl.when(i > 0)
        def _():
            dw2_ref[...] += dw2
            db2_ref[...] += db2
            dal_ref[...] += dal
            ddt_ref[...] += ddt

    full = lambda s: pl.BlockSpec(s, lambda i: (0, 0))
    return pl.pallas_call(
        body, name=name, grid=(g_,),
        in_specs=[pl.BlockSpec((tm, SM_W), lambda i: (i, C_SM // SM_W)), full((SM_W, GLA_QK)), full((1, GLA_QK)),
                  full((1, SM_W)), full((1, SM_W)),
                  pl.BlockSpec((GDN_HEADS, tm, SM_W), lambda i: (0, i, 0)),
                  pl.BlockSpec((tm, GLA_QK), lambda i: (i, 0)), _ANY],
        out_specs=[pl.BlockSpec((tm, tail_w), lambda i: (i, C_SM // tail_w)), full((SM_W, GLA_QK)), full((1, GLA_QK)),
                   full((1, SM_W)), full((1, SM_W))],
        out_shape=[jax.ShapeDtypeStruct(d_proj.shape, d_proj.dtype), jax.ShapeDtypeStruct((SM_W, GLA_QK), F32),
                   jax.ShapeDtypeStruct((1, GLA_QK), F32), jax.ShapeDtypeStruct((1, SM_W), F32),
                   jax.ShapeDtypeStruct((1, SM_W), F32)],
        input_output_aliases={7: 0},
        compiler_params=_params("arbitrary"),
    )(proj, w2p, b2, alog_p, dt_p, dgb_heads, dla, d_proj)


QKV_W = GDN_QK
N_QKV_GROUPS = 3
QKV_B0 = C_QKV // QKV_W
HALO = 8


def _conv_terms(x_ref, halo_ref, cw_ref, xs_ref, i, tm):
    xs_ref[HALO:HALO + tm, :] = x_ref[...]
    xs_ref[0:HALO, :] = jnp.where(i > 0, halo_ref[...], 0.0)
    cw = cw_ref[...]
    xs = xs_ref[...]
    taps = [(pltpu.roll(xs, CONV_K - 1 - t, 0) if t < CONV_K - 1 else xs)[HALO:HALO + tm, :] for t in range(CONV_K)]
    c = taps[0] * cw[0:1, :]
    for t in range(1, CONV_K):
        c = c + taps[t] * cw[t:t + 1, :]
    return c, taps


def _prep_fwd(proj, conv_w8, *, name):
    M = proj.shape[0]
    tm = _tile(M, 344, 8)

    def body(x_ref, halo_ref, cw_ref, o_ref, xs_ref):
        j, i = pl.program_id(0), pl.program_id(1)
        c, _ = _conv_terms(x_ref, halo_ref, cw_ref, xs_ref, i, tm)
        s, _ = _silu_and_grad(c)
        scale = jnp.where(j == 0, GDN_DK ** -0.5, 1.0)
        for hh in range(GDN_HEADS):
            cols = slice(hh * 128, (hh + 1) * 128)
            sh = s[:, cols]
            r = lax.rsqrt(jnp.sum(sh * sh, axis=-1, keepdims=True) + NORM_EPS)
            o_ref[:, cols] = jnp.where(j < 2, sh * (r * scale), sh)

    hb = tm // HALO
    return pl.pallas_call(
        body, name=name, grid=(N_QKV_GROUPS, M // tm),
        in_specs=[pl.BlockSpec((tm, QKV_W), lambda j, i: (i, QKV_B0 + j)),
                  pl.BlockSpec((HALO, QKV_W), lambda j, i: (jnp.maximum(i * hb - 1, 0), QKV_B0 + j)),
                  pl.BlockSpec((8, QKV_W), lambda j, i: (0, j))],
        out_specs=pl.BlockSpec((tm, QKV_W), lambda j, i: (i, j)),
        out_shape=jax.ShapeDtypeStruct((M, N_QKV_GROUPS * QKV_W), F32),
        scratch_shapes=[pltpu.VMEM((tm + HALO, QKV_W), F32)],
        compiler_params=_params("parallel", "arbitrary"),
    )(proj, proj, conv_w8)


def _prep_bwd(proj, conv_w8, dact, d_proj, *, name):
    M = proj.shape[0]
    tm = _tile(M, 688, 16)
    g_ = M // tm
    ext = tm + HALO

    def body(x_ref, prev_ref, next_ref, cw_ref, da_ref, dan_ref, _, o_ref, dcw_ref, xs_ref, das_ref, dcs_ref):
        j, i = pl.program_id(0), pl.program_id(1)
        not_last = i < g_ - 1
        xs_ref[0:HALO, :] = jnp.where(i > 0, prev_ref[...], 0.0)
        xs_ref[HALO:HALO + tm, :] = x_ref[...]
        xs_ref[HALO + tm:HALO + ext, :] = jnp.where(not_last, next_ref[...], 0.0)
        das_ref[0:tm, :] = da_ref[...]
        das_ref[tm:ext, :] = jnp.where(not_last, dan_ref[...], 0.0)
        cw = cw_ref[...]
        xs = xs_ref[...]
        taps = [(pltpu.roll(xs, CONV_K - 1 - t, 0) if t < CONV_K - 1 else xs)[HALO:HALO + ext, :] for t in range(CONV_K)]
        c = taps[0] * cw[0:1, :]
        for t in range(1, CONV_K):
            c = c + taps[t] * cw[t:t + 1, :]
        s, ds_dc = _silu_and_grad(c)
        scale = jnp.where(j == 0, GDN_DK ** -0.5, 1.0)
        for hh in range(GDN_HEADS):
            cols = slice(hh * 128, (hh + 1) * 128)
            sh = s[:, cols]
            r = lax.rsqrt(jnp.sum(sh * sh, axis=-1, keepdims=True) + NORM_EPS)
            da = das_ref[:, cols]
            y = sh * r
            dy = da * scale
            ds_norm = r * (dy - y * jnp.sum(dy * y, axis=-1, keepdims=True))
            dcs_ref[:, cols] = jnp.where(j < 2, ds_norm, da) * ds_dc[:, cols]
        dc = dcs_ref[...]
        acc = dc[0:tm, :] * cw[CONV_K - 1:CONV_K, :]
        for t in range(CONV_K - 1):
            acc = acc + pltpu.roll(dc, ext - (CONV_K - 1 - t), 0)[0:tm, :] * cw[t:t + 1, :]
        o_ref[...] = acc.astype(o_ref.dtype)
        r8 = lax.broadcasted_iota(jnp.int32, (8, QKV_W), 0)
        part = jnp.zeros((8, QKV_W), F32)
        for t in range(CONV_K):
            part = jnp.where(r8 == t, jnp.sum(dc[0:tm, :] * taps[t][0:tm, :], axis=0, keepdims=True), part)

        @pl.when(i == 0)
        def _():
            dcw_ref[...] = part

        @pl.when(i > 0)
        def _():
            dcw_ref[...] += part

    hb = tm // HALO
    last = M // HALO - 1
    prev_of = lambda i: jnp.maximum(i * hb - 1, 0)
    next_of = lambda i: jnp.minimum((i + 1) * hb, last)
    return pl.pallas_call(
        body, name=name, grid=(N_QKV_GROUPS, g_),
        in_specs=[pl.BlockSpec((tm, QKV_W), lambda j, i: (i, QKV_B0 + j)),
                  pl.BlockSpec((HALO, QKV_W), lambda j, i: (prev_of(i), QKV_B0 + j)),
                  pl.BlockSpec((HALO, QKV_W), lambda j, i: (next_of(i), QKV_B0 + j)),
                  pl.BlockSpec((8, QKV_W), lambda j, i: (0, j)),
                  pl.BlockSpec((tm, QKV_W), lambda j, i: (i, j)),
                  pl.BlockSpec((HALO, QKV_W), lambda j, i: (next_of(i), j)), _ANY],
        out_specs=[pl.BlockSpec((tm, QKV_W), lambda j, i: (i, QKV_B0 + j)), pl.BlockSpec((8, QKV_W), lambda j, i: (0, j))],
        out_shape=[jax.ShapeDtypeStruct(d_proj.shape, d_proj.dtype),
                   jax.ShapeDtypeStruct((8, N_QKV_GROUPS * QKV_W), F32)],
        input_output_aliases={6: 0},
        scratch_shapes=[pltpu.VMEM((HALO + ext, QKV_W), F32), pltpu.VMEM((ext, QKV_W), F32), pltpu.VMEM((ext, QKV_W), F32)],
        compiler_params=_params("parallel", "arbitrary"),
    )(proj, proj, proj, conv_w8, dact, dact, d_proj)


def _round_robin(gens):
    gens = list(gens)
    while gens:
        alive = []
        for gen in gens:
            try:
                next(gen)
                alive.append(gen)
            except StopIteration:
                pass
        gens = alive


def _unit_lower_inverse(a_low, eye):
    n = a_low.shape[0]
    ri = lax.broadcasted_iota(jnp.int32, (n, n), 0)
    ci = lax.broadcasted_iota(jnp.int32, (n, n), 1)
    same = lambda shift: (ri >> shift) == (ci >> shift)
    b = jnp.where(same(3), -a_low, 0.0)
    x = eye + b
    p2 = _dot3(b, b)
    yield
    x = x + _dot3(x, p2)
    p4 = _dot3(p2, p2)
    yield
    x = x + _dot3(x, p4)
    yield
    for shift in (3, 4, 5):
        between = jnp.where(same(shift + 1) & ~same(shift), a_low, 0.0)
        t = _dot3(between, x)
        yield
        x = x - _dot3(x, t)
        yield
    return x


class _GdnChunk:
    def build(self, q, k, v, gb, h):
        C = GDN_CHUNK
        lane = lax.broadcasted_iota(jnp.int32, (C, SM_W), 1)
        g = jnp.sum(jnp.where(lane == h, gb, 0.0), axis=1, keepdims=True)
        self.beta = jnp.sum(jnp.where(lane == h + GDN_HEADS, gb, 0.0), axis=1, keepdims=True)
        ri = lax.broadcasted_iota(jnp.int32, (C, C), 0)
        ci = lax.broadcasted_iota(jnp.int32, (C, C), 1)
        self.causal = ri >= ci
        self.strict = ri > ci
        self.eye = (ri == ci).astype(F32)
        gcb = _dotx(self.causal.astype(F32), jnp.broadcast_to(g, (C, SM_W)))
        yield
        self.gcol = gcb[:, 0:1]
        grow = gcb.T[0:1, 0:C]
        self.decay = jnp.exp(jnp.where(self.causal, self.gcol - grow, -1e30))
        self.egc = jnp.exp(self.gcol)
        glast = gcb[C - 1:C, 0:1]
        self.elast = jnp.exp(glast - self.gcol)
        self.gl = jnp.exp(glast)
        self.q, self.k, self.v = q, k, v
        self.kb = k * self.beta
        m = _dot(self.kb, k, NT)
        n_ = _dot(q, k, NT)
        yield
        self.a_low = jnp.where(self.strict, m * self.decay, 0.0)
        self.p = n_ * self.decay
        self.qd = q * self.egc
        self.kd = k * self.elast
        self.bu = v * self.beta
        self.bw = self.kb * self.egc


GDN_HB = 8
GDN_HG = GDN_HEADS // GDN_HB


def _gdn_specs(n_of):
    C, W = GDN_CHUNK, 128 * GDN_HB
    q_spec = pl.BlockSpec((C, W), lambda g, n: (n_of(n), g))
    k_spec = pl.BlockSpec((C, W), lambda g, n: (n_of(n), g + GDN_HG))
    v_spec = pl.BlockSpec((C, W), lambda g, n: (n_of(n), g + 2 * GDN_HG))
    gb_spec = pl.BlockSpec((C, SM_W), lambda g, n: (n_of(n), 0))
    o_spec = pl.BlockSpec((C, W), lambda g, n: (n_of(n), g))
    s_spec = pl.BlockSpec((GDN_HB, None, GDN_DK, GDN_DV), lambda g, n: (g, n_of(n), 0, 0))
    t_spec = pl.BlockSpec((GDN_HB, None, C, C), lambda g, n: (g, n_of(n), 0, 0))
    return q_spec, k_spec, v_spec, gb_spec, o_spec, s_spec, t_spec


def _gdn_fwd(act, gb, *, name):
    M = act.shape[0]
    N = M // GDN_CHUNK

    def body(q_ref, k_ref, v_ref, gb_ref, o_ref, s_ref, t_ref, state):
        g, n = pl.program_id(0), pl.program_id(1)

        @pl.when(n == 0)
        def _():
            state[...] = jnp.zeros_like(state)

        gb_ = gb_ref[...]

        def head(hh):
            cols = slice(hh * 128, (hh + 1) * 128)
            c = _GdnChunk()
            yield from c.build(q_ref[:, cols], k_ref[:, cols], v_ref[:, cols], gb_, g * GDN_HB + hh)
            tinv = yield from _unit_lower_inverse(c.a_low, c.eye)
            s = state[hh]
            s_ref[hh] = s
            t_ref[hh] = tinv
            u = _dot(tinv, c.bu)
            w = _dot(tinv, c.bw)
            yield
            vn = u - _dot(w, s)
            o1 = _dot(c.qd, s)
            yield
            o_ref[:, cols] = o1 + _dot(c.p, vn)
            state[hh] = c.gl * s + _dot(c.kd, vn, TN)

        _round_robin(head(hh) for hh in range(GDN_HB))

    q_spec, k_spec, v_spec, gb_spec, o_spec, s_spec, t_spec = _gdn_specs(lambda n: n)
    return pl.pallas_call(
        body, name=name, grid=(GDN_HG, N),
        in_specs=[q_spec, k_spec, v_spec, gb_spec], out_specs=[o_spec, s_spec, t_spec],
        out_shape=[jax.ShapeDtypeStruct((M, GDN_V), F32),
                   jax.ShapeDtypeStruct((GDN_HEADS, N, GDN_DK, GDN_DV), F32),
                   jax.ShapeDtypeStruct((GDN_HEADS, N, GDN_CHUNK, GDN_CHUNK), F32)],
        scratch_shapes=[pltpu.VMEM((GDN_HB, GDN_DK, GDN_DV), F32)],
        compiler_params=_params("parallel", "arbitrary"),
    )(act, act, act, gb)


def _gdn_bwd(act, gb, do, s_all, t_all, *, name):
    M = act.shape[0]
    N = M // GDN_CHUNK
    C = GDN_CHUNK
    assert GDN_HG == 1

    def body(q_ref, k_ref, v_ref, gb_ref, do_ref, s_ref, t_ref, dact_ref, dgb_ref, dstate):
        g, n = pl.program_id(0), pl.program_id(1)

        @pl.when(n == 0)
        def _():
            dstate[...] = jnp.zeros_like(dstate)

        gb_ = gb_ref[...]
        last = lax.broadcasted_iota(jnp.int32, (C, 1), 0) == C - 1
        upper = (lax.broadcasted_iota(jnp.int32, (C, C), 0) <= lax.broadcasted_iota(jnp.int32, (C, C), 1)).astype(F32)
        lane = lax.broadcasted_iota(jnp.int32, (C, SM_W), 1)
        def head(hh):
            cols = slice(hh * 128, (hh + 1) * 128)
            h = g * GDN_HB + hh
            c = _GdnChunk()
            yield from c.build(q_ref[:, cols], k_ref[:, cols], v_ref[:, cols], gb_, h)
            tinv = t_ref[hh]
            s = s_ref[hh]
            do_ = do_ref[:, cols]
            ds1 = dstate[hh]
            u = _dot(tinv, c.bu)
            w = _dot(tinv, c.bw)
            dqd = _dot(do_, s, NT)
            dvn0 = _dot(c.p, do_, TN) + _dot(c.kd, ds1)
            dst0 = _dot(c.qd, do_, TN) + c.gl * ds1
            yield
            vn = u - _dot(w, s)
            dvn = dvn0
            yield
            dp = jnp.where(c.causal, _dot(do_, vn, NT), 0.0)
            dstate[hh] = dst0 - _dot(w, dvn, TN)
            dkd = _dot(vn, ds1, NT)
            dw = -_dot(dvn, s, NT)
            dbu = _dot(tinv, dvn, TN)
            dgl = jnp.sum(jnp.sum(s * ds1, axis=1, keepdims=True), axis=0, keepdims=True)
            yield
            dbw = _dot(tinv, dw, TN)
            t1 = _dot(dbu, u, NT)
            yield
            da = jnp.where(c.strict, -(t1 + _dot(dbw, w, NT)), 0.0)
            dn_ = dp * c.decay
            dq0 = _dot(dn_, c.k)
            dk0 = _dot(dn_, c.q, TN)
            yield
            dm = da * c.decay
            e = da * c.a_low + dp * c.p
            dkb = _dot(dm, c.k) + dbw * c.egc
            dact_ref[:, GDN_QK + hh * 128:GDN_QK + (hh + 1) * 128] = (
                _dot(dm, c.kb, TN) + dk0 + dkb * c.beta + dkd * c.elast)
            dact_ref[:, cols] = dq0 + dqd * c.egc
            dact_ref[:, 2 * GDN_QK + hh * 128:2 * GDN_QK + (hh + 1) * 128] = dbu * c.beta
            dbeta = jnp.sum(dbu * c.v, axis=1, keepdims=True) + jnp.sum(dkb * c.k, axis=1, keepdims=True)
            t_kd = jnp.sum(dkd * c.kd, axis=1, keepdims=True)
            dgc = (jnp.sum(e, axis=1, keepdims=True) - jnp.sum(e.T, axis=1, keepdims=True)
                   + jnp.sum(dbw * c.bw, axis=1, keepdims=True) + jnp.sum(dqd * c.qd, axis=1, keepdims=True) - t_kd)
            dgc = dgc + jnp.where(last, jnp.sum(t_kd, axis=0, keepdims=True) + dgl * c.gl, 0.0)
            yield
            dg = _dotx(upper, jnp.broadcast_to(dgc, (C, SM_W)))
            dgb_ref[hh] = jnp.where(lane == h, dg, jnp.where(lane == h + GDN_HEADS, dbeta, 0.0))

        _round_robin(head(hh) for hh in range(GDN_HB))

    rev = lambda n: N - 1 - n
    q_spec, k_spec, v_spec, gb_spec, o_spec, s_spec, t_spec = _gdn_specs(rev)
    dgb_spec = pl.BlockSpec((GDN_HB, C, SM_W), lambda g, n: (g, rev(n), 0))
    return pl.pallas_call(
        body, name=name, grid=(GDN_HG, N),
        in_specs=[q_spec, k_spec, v_spec, gb_spec, o_spec, s_spec, t_spec],
        out_specs=[pl.BlockSpec((C, 2 * GDN_QK + GDN_V), lambda g, n: (rev(n), 0)), dgb_spec],
        out_shape=[jax.ShapeDtypeStruct((M, 2 * GDN_QK + GDN_V), F32),
                   jax.ShapeDtypeStruct((GDN_HEADS, M, SM_W), F32)],
        scratch_shapes=[pltpu.VMEM((GDN_HB, GDN_DK, GDN_DV), F32)],
        compiler_params=_params("parallel", "arbitrary"),
    )(act, act, act, gb, do, s_all, t_all)


GLA_STEP_ROWS = 64
GLA_SUB = GLA_STEP_ROWS // GLA_CHUNK


def _gla_cumsum(la):
    C = GLA_CHUNK
    ltri = (lax.broadcasted_iota(jnp.int32, (C, C), 0) >= lax.broadcasted_iota(jnp.int32, (C, C), 1)).astype(F32)
    return _dotx(ltri, la)


def _gla_decay_rows(b, i):
    rj = lax.broadcasted_iota(jnp.int32, (GLA_CHUNK, GLA_DK), 0)
    return jnp.where(rj <= i, jnp.exp(jnp.minimum(b[i:i + 1, :] - b, 0.0)), 0.0)


GLA_HALF = GLA_CHUNK // 2


def _gla_cross_factors(b):
    top = lax.broadcasted_iota(jnp.int32, b.shape, 0) < GLA_HALF
    bm = b[GLA_HALF - 1:GLA_HALF, :]
    late = jnp.where(top, 0.0, jnp.exp(jnp.minimum(b - bm, 0.0)))
    early = jnp.where(top, jnp.exp(jnp.minimum(bm - b, 0.0)), 0.0)
    return late, early


def _gla_half_decay(bh, ii):
    rj = lax.broadcasted_iota(jnp.int32, bh.shape, 0)
    return jnp.where(rj <= ii, jnp.exp(jnp.minimum(bh[ii:ii + 1, :] - bh, 0.0)), 0.0)


def _gla_scores_t(q, k, b):
    C, H = GLA_CHUNK, GLA_HALF
    lane = lax.broadcasted_iota(jnp.int32, (H, C), 1)
    halves = []
    for h0 in (0, H):
        qh, kh, bh = q[h0:h0 + H], k[h0:h0 + H], b[h0:h0 + H]
        sth = jnp.zeros((H, C), F32)
        for ii in range(H):
            si = jnp.sum(qh[ii:ii + 1, :] * kh * _gla_half_decay(bh, ii), axis=1, keepdims=True)
            sth = jnp.where(lane == h0 + ii, si, sth)
            if ii % 4 == 3:
                yield
        halves.append(sth)
    late, early = _gla_cross_factors(b)
    between = _dot(k * early, q * late, NT)
    yield
    return jnp.concatenate(halves, axis=0) + between


def _gla_specs(n_of):
    R = GLA_STEP_ROWS
    q_spec = pl.BlockSpec((R, GLA_QK), lambda n: (n_of(n), C_GQ // GLA_QK))
    k_spec = pl.BlockSpec((R, GLA_QK), lambda n: (n_of(n), C_GK // GLA_QK))
    v_spec = pl.BlockSpec((R, GLA_V), lambda n: (n_of(n), C_GV // GLA_V))
    la_spec = pl.BlockSpec((R, GLA_QK), lambda n: (n_of(n), 0))
    o_spec = pl.BlockSpec((R, GLA_V), lambda n: (n_of(n), 0))
    s_spec = pl.BlockSpec((GLA_HEADS, None, GLA_SUB, GLA_DV, GLA_DK), lambda n: (0, n_of(n), 0, 0, 0))
    return q_spec, k_spec, v_spec, la_spec, o_spec, s_spec


def _gla_fwd(proj, la, *, name):
    M = proj.shape[0]
    N = M // GLA_STEP_ROWS
    C = GLA_CHUNK

    def body(q_ref, k_ref, v_ref, la_ref, o_ref, s_ref, state):
        n = pl.program_id(0)

        @pl.when(n == 0)
        def _():
            state[...] = jnp.zeros_like(state)

        local = {}

        def within(hh, c):
            kc = slice(hh * GLA_DK, (hh + 1) * GLA_DK)
            vc = slice(hh * GLA_DV, (hh + 1) * GLA_DV)
            rows = slice(c * C, (c + 1) * C)
            q = q_ref[rows, kc] * (GLA_DK ** -0.5)
            k = k_ref[rows, kc]
            v = v_ref[rows, vc]
            b = _gla_cumsum(la_ref[rows, kc])
            yield
            blast = b[C - 1:C, :]
            sc_t = yield from _gla_scores_t(q, k, b)
            kv = _dot(v, k * jnp.exp(blast - b), TN)
            o2 = _dot(sc_t, v, TN)
            yield
            local[hh, c] = (q * jnp.exp(b), jnp.exp(blast), kv, o2)

        def across(hh):
            vc = slice(hh * GLA_DV, (hh + 1) * GLA_DV)
            st = state[hh]
            for c in range(GLA_SUB):
                qe, eblast, kv, o2 = local[hh, c]
                s_ref[hh, c] = st
                o1 = _dot(qe, st, NT)
                yield
                o_ref[c * C:(c + 1) * C, vc] = o1 + o2
                st = st * eblast + kv
            state[hh] = st

        _round_robin(within(hh, c) for c in range(GLA_SUB) for hh in range(GLA_HEADS))
        _round_robin(across(hh) for hh in range(GLA_HEADS))

    q_spec, k_spec, v_spec, la_spec, o_spec, s_spec = _gla_specs(lambda n: n)
    return pl.pallas_call(
        body, name=name, grid=(N,),
        in_specs=[q_spec, k_spec, v_spec, la_spec], out_specs=[o_spec, s_spec],
        out_shape=[jax.ShapeDtypeStruct((M, GLA_V), F32),
                   jax.ShapeDtypeStruct((GLA_HEADS, N, GLA_SUB, GLA_DV, GLA_DK), F32)],
        scratch_shapes=[pltpu.VMEM((GLA_HEADS, GLA_DV, GLA_DK), F32)],
        compiler_params=_params("arbitrary"),
    )(proj, proj, proj, la)


def _gla_bwd(proj, la, do, s_all, d_proj, *, name):
    M = proj.shape[0]
    N = M // GLA_STEP_ROWS
    C = GLA_CHUNK
    qkv_w = 2 * GLA_QK + GLA_V
    assert C_GK == C_GQ + GLA_QK and C_GV == C_GK + GLA_QK and C_GQ % qkv_w == 0

    def body(q_ref, k_ref, v_ref, la_ref, do_ref, s_ref, _, dp_ref, dla_ref, dstate):
        n = pl.program_id(0)

        @pl.when(n == 0)
        def _():
            dstate[...] = jnp.zeros_like(dstate)

        H = GLA_HALF
        lane = lax.broadcasted_iota(jnp.int32, (C, C), 1)
        row = lax.broadcasted_iota(jnp.int32, (C, C), 0)
        ri = lax.broadcasted_iota(jnp.int32, (C, GLA_DK), 0)
        lane_h = lax.broadcasted_iota(jnp.int32, (H, C), 1)
        ri_h = lax.broadcasted_iota(jnp.int32, (H, GLA_DK), 0)
        cross = (row < H) & (lane >= H)
        upper = (row <= lane).astype(F32)
        def head(hh):
            kc = slice(hh * GLA_DK, (hh + 1) * GLA_DK)
            vc = slice(hh * GLA_DV, (hh + 1) * GLA_DV)
            ds1 = dstate[hh]
            for c in reversed(range(GLA_SUB)):
                rows = slice(c * C, (c + 1) * C)
                q = q_ref[rows, kc] * (GLA_DK ** -0.5)
                k = k_ref[rows, kc]
                v = v_ref[rows, vc]
                b = _gla_cumsum(la_ref[rows, kc])
                do_ = do_ref[rows, vc]
                st = s_ref[hh, c]
                dsc_t = _dot(v, do_, NT)
                dqe = _dot(do_, st)
                dke = _dot(v, ds1)
                yield
                blast = b[C - 1:C, :]
                eb = jnp.exp(b)
                elast = jnp.exp(blast - b)
                eblast = jnp.exp(blast)
                qe = q * eb
                ke = k * elast
                dv2 = _dot(ke, ds1, NT)
                ds_new = _dot(do_, qe, TN)
                deblast = jnp.sum(st * ds1, axis=0, keepdims=True)
                sc_halves, dq_halves, dk_halves = [], [], []
                for h0 in (0, H):
                    qh, kh, bh, dsch = q[h0:h0 + H], k[h0:h0 + H], b[h0:h0 + H], dsc_t[h0:h0 + H]
                    sch = jnp.zeros((H, C), F32)
                    dqh = jnp.zeros((H, GLA_DK), F32)
                    dkh = jnp.zeros((H, GLA_DK), F32)
                    for ii in range(H):
                        f = _gla_half_decay(bh, ii)
                        kf = kh * f
                        si = jnp.sum(qh[ii:ii + 1, :] * kf, axis=1, keepdims=True)
                        sch = jnp.where(lane_h == h0 + ii, si, sch)
                        dsi = jnp.sum(jnp.where(lane_h == h0 + ii, dsch, 0.0), axis=1, keepdims=True)
                        dqh = jnp.where(ri_h == ii, jnp.sum(dsi * kf, axis=0, keepdims=True), dqh)
                        dkh = dkh + (dsi * f) * qh[ii:ii + 1, :]
                        if ii % 4 == 3:
                            yield
                    sc_halves.append(sch)
                    dq_halves.append(dqh)
                    dk_halves.append(dkh)
                late, early = _gla_cross_factors(b)
                q_late, k_early = q * late, k * early
                dsc_x = jnp.where(cross, dsc_t, 0.0)
                sc_t = jnp.concatenate(sc_halves, axis=0) + _dot(k_early, q_late, NT)
                dq_sc = jnp.concatenate(dq_halves, axis=0) + _dot(dsc_x, k_early, TN) * late
                dk_sc = jnp.concatenate(dk_halves, axis=0) + _dot(dsc_x, q_late) * early
                yield
                dv1 = _dot(sc_t, do_)
                dp_ref[rows, kc] = ((dq_sc + dqe * eb) * (GLA_DK ** -0.5)).astype(dp_ref.dtype)
                dp_ref[rows, GLA_QK + hh * GLA_DK:GLA_QK + (hh + 1) * GLA_DK] = (dk_sc + dke * elast).astype(dp_ref.dtype)
                t_ke = dke * ke
                db = q * dq_sc - k * dk_sc + dqe * qe - t_ke
                db = db + jnp.where(ri == C - 1, jnp.sum(t_ke, axis=0, keepdims=True) + deblast * eblast, 0.0)
                dla = _dotx(upper, db)
                yield
                dp_ref[rows, 2 * GLA_QK + hh * GLA_DV:2 * GLA_QK + (hh + 1) * GLA_DV] = (dv1 + dv2).astype(dp_ref.dtype)
                dla_ref[rows, kc] = dla
                ds1 = ds1 * eblast + ds_new
            dstate[hh] = ds1

        _round_robin(head(hh) for hh in range(GLA_HEADS))

    rev = lambda n: N - 1 - n
    q_spec, k_spec, v_spec, la_spec, o_spec, s_spec = _gla_specs(rev)
    return pl.pallas_call(
        body, name=name, grid=(N,),
        in_specs=[q_spec, k_spec, v_spec, la_spec, o_spec, s_spec, _ANY],
        out_specs=[pl.BlockSpec((GLA_STEP_ROWS, qkv_w), lambda n: (rev(n), C_GQ // qkv_w)), la_spec],
        out_shape=[jax.ShapeDtypeStruct(d_proj.shape, d_proj.dtype), jax.ShapeDtypeStruct((M, GLA_QK), F32)],
        input_output_aliases={6: 0},
        scratch_shapes=[pltpu.VMEM((GLA_HEADS, GLA_DV, GLA_DK), F32)],
        compiler_params=_params("arbitrary"),
    )(proj, proj, proj, la, do, s_all, d_proj)


def _head_norm(o, wn):
    r = lax.rsqrt(jnp.mean(o * o, axis=-1, keepdims=True) + NORM_EPS)
    return o * r, r


def _mix_heads():
    heads = [(0, GDN_DV, hh * GDN_DV, hh * GDN_DV) for hh in range(GDN_HEADS)]
    heads += [(1, GLA_DV, GDN_V + hh * GLA_DV, hh * GLA_DV) for hh in range(GLA_HEADS)]
    return heads


def _mix_fwd(o_gdn, o_gla, proj, wn_gdn, wn_gla, *, name):
    M = proj.shape[0]
    tm = _tile(M, 344, 16)

    def body(og_ref, ol_ref, z_ref, r_ref, wg_ref, wl_ref, m_ref):
        srcs = ((og_ref, z_ref, wg_ref), (ol_ref, r_ref, wl_ref))
        for grp, width, mcol, col in _mix_heads():
            o_ref, gate_ref, w_ref = srcs[grp]
            xhat, _ = _head_norm(o_ref[:, col:col + width], None)
            gate, _ = _silu_and_grad(gate_ref[:, col:col + width])
            m_ref[:, mcol:mcol + width] = (xhat * w_ref[...] * gate).astype(m_ref.dtype)

    full = lambda s: pl.BlockSpec(s, lambda i: (0, 0))
    return pl.pallas_call(
        body, name=name, grid=(M // tm,),
        in_specs=[pl.BlockSpec((tm, GDN_V), lambda i: (i, 0)), pl.BlockSpec((tm, GLA_V), lambda i: (i, 0)),
                  pl.BlockSpec((tm, GDN_V), lambda i: (i, C_Z // GDN_V)),
                  pl.BlockSpec((tm, GLA_V), lambda i: (i, C_GR // GLA_V)),
                  full((1, GDN_DV)), full((1, GLA_DV))],
        out_specs=pl.BlockSpec((tm, D_MODEL), lambda i: (i, 0)),
        out_shape=jax.ShapeDtypeStruct((M, D_MODEL), BF16),
        compiler_params=_params("parallel"),
    )(o_gdn, o_gla, proj, proj, wn_gdn, wn_gla)


def _mix_bwd(o_gdn, o_gla, proj, wn_gdn, wn_gla, dmixed, *, name):
    M = proj.shape[0]
    tm = _tile(M, 344, 16)
    g_ = M // tm
    assert C_Z == 0 and C_GR == GDN_V

    def body(og_ref, ol_ref, z_ref, r_ref, wg_ref, wl_ref, dm_ref,
             dog_ref, dol_ref, dzr_ref, dwg_ref, dwl_ref):
        i = pl.program_id(0)
        srcs = ((og_ref, z_ref, wg_ref, dog_ref), (ol_ref, r_ref, wl_ref, dol_ref))
        dws = [jnp.zeros((1, GDN_DV), F32), jnp.zeros((1, GLA_DV), F32)]
        for grp, width, mcol, col in _mix_heads():
            o_ref, gate_ref, w_ref, do_ref = srcs[grp]
            cols = slice(col, col + width)
            xhat, r = _head_norm(o_ref[:, cols], None)
            gate, dgate_dc = _silu_and_grad(gate_ref[:, cols])
            dm = dm_ref[:, mcol:mcol + width]
            dzr_ref[:, mcol:mcol + width] = (dm * xhat * w_ref[...] * dgate_dc).astype(dzr_ref.dtype)
            dnorm = dm * gate
            dws[grp] = dws[grp] + jnp.sum(dnorm * xhat, axis=0, keepdims=True)
            dxhat = dnorm * w_ref[...]
            do_ref[:, cols] = r * (dxhat - xhat * jnp.mean(dxhat * xhat, axis=-1, keepdims=True))

        @pl.when(i == 0)
        def _():
            dwg_ref[...] = dws[0]
            dwl_ref[...] = dws[1]

        @pl.when(i > 0)
        def _():
            dwg_ref[...] += dws[0]
            dwl_ref[...] += dws[1]

    full = lambda s: pl.BlockSpec(s, lambda i: (0, 0))
    half = pl.BlockSpec((tm, GDN_V), lambda i: (i, 0))
    return pl.pallas_call(
        body, name=name, grid=(g_,),
        in_specs=[half, half, pl.BlockSpec((tm, GDN_V), lambda i: (i, C_Z // GDN_V)),
                  pl.BlockSpec((tm, GLA_V), lambda i: (i, C_GR // GLA_V)),
                  full((1, GDN_DV)), full((1, GLA_DV)), pl.BlockSpec((tm, D_MODEL), lambda i: (i, 0))],
        out_specs=[half, half, pl.BlockSpec((tm, GDN_V + GLA_V), lambda i: (i, 0)),
                   full((1, GDN_DV)), full((1, GLA_DV))],
        out_shape=[jax.ShapeDtypeStruct((M, GDN_V), F32), jax.ShapeDtypeStruct((M, GLA_V), F32),
                   jax.ShapeDtypeStruct((M, D_PROJ), BF16),
                   jax.ShapeDtypeStruct((1, GDN_DV), F32), jax.ShapeDtypeStruct((1, GLA_DV), F32)],
        compiler_params=_params("arbitrary"),
    )(o_gdn, o_gla, proj, proj, wn_gdn, wn_gla, dmixed)


def _swiglu_fwd(n, w_gate_t, w_up_t, *, name, tm=1376, tn=512):
    M, D = n.shape
    F = w_gate_t.shape[0]
    tm, tn = _tile(M, tm, 16), _tile(F, tn, 128)

    def body(n_ref, wg_ref, wu_ref, g_ref, u_ref, a_ref):
        x = n_ref[...]
        g = _dot(x, wg_ref[...], NT)
        u = _dot(x, wu_ref[...], NT)
        s, _ = _silu_and_grad(g)
        g_ref[...] = g.astype(g_ref.dtype)
        u_ref[...] = u.astype(u_ref.dtype)
        a_ref[...] = (s * u).astype(a_ref.dtype)

    w_spec = pl.BlockSpec((tn, D), lambda i, j: (j, 0))
    o_spec = pl.BlockSpec((tm, tn), lambda i, j: (i, j))
    return pl.pallas_call(
        body, name=name, grid=(M // tm, F // tn),
        in_specs=[pl.BlockSpec((tm, D), lambda i, j: (i, 0)), w_spec, w_spec], out_specs=[o_spec] * 3,
        out_shape=[jax.ShapeDtypeStruct((M, F), BF16)] * 3, compiler_params=_params("parallel", "parallel"),
    )(n, w_gate_t, w_up_t)


def _swiglu_bwd(dh, w_down, gate, up, *, name, after=None, tm=1376, tn=512):
    M, D = dh.shape
    F = w_down.shape[0]
    tm, tn = _tile(M, tm, 16), _tile(F, tn, 128)
    n_after = 0 if after is None else 1

    def body(*refs):
        dh_ref, w_ref, g_ref, u_ref, dg_ref, du_ref = refs[n_after:]
        da = _dot(dh_ref[...], w_ref[...], NT)
        s, ds = _silu_and_grad(g_ref[...].astype(F32))
        dg_ref[...] = (da * u_ref[...].astype(F32) * ds).astype(dg_ref.dtype)
        du_ref[...] = (da * s).astype(du_ref.dtype)

    o_spec = pl.BlockSpec((tm, tn), lambda i, j: (i, j))
    return pl.pallas_call(
        body, name=name, grid=(M // tm, F // tn),
        in_specs=[_ANY] * n_after + [pl.BlockSpec((tm, D), lambda i, j: (i, 0)),
                                     pl.BlockSpec((tn, D), lambda i, j: (j, 0)), o_spec, o_spec],
        out_specs=[o_spec, o_spec], out_shape=[jax.ShapeDtypeStruct((M, F), BF16)] * 2,
        compiler_params=_params("parallel", "parallel"),
    )(*((after,) if n_after else ()), dh, w_down, gate, up)


def _adamw(w, g, m, v, *, name):
    shape = w.shape
    cols = shape[-1]
    rows = w.size // cols
    w2, g2, m2, v2 = (t.reshape(rows, cols) for t in (w, g, m, v))
    if rows % 8 == 0 or cols % 128 != 0:
        tr, tc = (_tile(rows, 256, 8) if rows % 8 == 0 else rows), cols
    else:
        tr, tc = rows, _tile(cols, 256, 128)

    def body(w_ref, g_ref, m_ref, v_ref, d_ref, nm_ref, nv_ref):
        g_ = g_ref[...]
        nm = ADAM_B1 * m_ref[...] + (1.0 - ADAM_B1) * g_
        nv = ADAM_B2 * v_ref[...] + (1.0 - ADAM_B2) * (g_ * g_)
        m_hat = nm / (1.0 - ADAM_B1 ** ADAM_STEP)
        v_hat = nv / (1.0 - ADAM_B2 ** ADAM_STEP)
        d_ref[...] = -ADAM_LR * (m_hat / (jnp.sqrt(v_hat) + ADAM_EPS) + ADAM_WD * w_ref[...])
        nm_ref[...] = nm
        nv_ref[...] = nv

    blk = pl.BlockSpec((tr, tc), lambda i, j: (i, j))
    outs = pl.pallas_call(
        body, name=name, grid=(rows // tr, cols // tc), in_specs=[blk] * 4, out_specs=[blk] * 3,
        out_shape=[jax.ShapeDtypeStruct((rows, cols), F32)] * 3, compiler_params=_params("parallel", "parallel"),
    )(w2, g2, m2, v2)
    return tuple(t.reshape(shape) for t in outs)


def _sum_slabs(x, *, name):
    _, R, C = x.shape
    sub = 16 if x.dtype == BF16 else 8
    if R % sub == 0:
        tr, tc = _tile(R, 128, sub), C
    else:
        tr, tc = R, _tile(C, 256, 128)

    def body(x_ref, o_ref):
        acc = x_ref[0].astype(F32)
        for s in range(1, N_DEV):
            acc = acc + x_ref[s].astype(F32)
        o_ref[...] = acc

    return pl.pallas_call(
        body, name=name, grid=(R // tr, C // tc),
        in_specs=[pl.BlockSpec((N_DEV, tr, tc), lambda i, j: (0, i, j))],
        out_specs=pl.BlockSpec((tr, tc), lambda i, j: (i, j)),
        out_shape=jax.ShapeDtypeStruct((R, C), F32), compiler_params=_params("parallel", "parallel"),
    )(x)


def _peers():
    x, y, c = lax.axis_index("x"), lax.axis_index("y"), lax.axis_index("c")
    me = 4 * x + 2 * y + c
    peers = []
    for k in range(1, N_DEV):
        px = 1 - x if k & 4 else x
        py = 1 - y if k & 2 else y
        pc = 1 - c if k & 1 else c
        peers.append(((px, py, pc), 4 * px + 2 * py + pc))
    return me, peers


def _exchange(x, *, gather, name):
    slab = x.shape if gather else x.shape[1:]

    def body(x_ref, o_ref, send_sems, recv_sems, own_sem):
        me, peers = _peers()
        own = pltpu.make_async_copy(x_ref if gather else x_ref.at[me], o_ref.at[me], own_sem)
        own.start()
        sends, recvs = [], []
        for k, (pos, idx) in enumerate(peers):
            sends.append(pltpu.make_async_remote_copy(
                src_ref=x_ref if gather else x_ref.at[idx], dst_ref=o_ref.at[me],
                send_sem=send_sems.at[k], recv_sem=recv_sems.at[k],
                device_id=pos, device_id_type=pl.DeviceIdType.MESH))
            recvs.append(pltpu.make_async_remote_copy(
                src_ref=x_ref if gather else x_ref.at[idx], dst_ref=o_ref.at[idx],
                send_sem=send_sems.at[k], recv_sem=recv_sems.at[k],
                device_id=pos, device_id_type=pl.DeviceIdType.MESH))
        for cp in sends:
            cp.start()
        for cp in recvs:
            cp.wait_recv()
        for cp in sends:
            cp.wait_send()
        own.wait()

    hbm = pl.BlockSpec(memory_space=pltpu.HBM)
    return pl.pallas_call(
        body, name=name, in_specs=[hbm], out_specs=hbm,
        out_shape=jax.ShapeDtypeStruct((N_DEV,) + tuple(slab), x.dtype),
        scratch_shapes=[pltpu.SemaphoreType.DMA((N_DEV - 1,)), pltpu.SemaphoreType.DMA((N_DEV - 1,)),
                        pltpu.SemaphoreType.DMA],
    )(x)


_HBM = pl.BlockSpec(memory_space=pltpu.HBM)
_SEM = pl.BlockSpec(memory_space=pltpu.SEMAPHORE)
_EFFECT = pltpu.SideEffectType.DATAFLOW_SIDE_EFFECTING


PLAN_GATHER = tuple((k, "x", 0) for k in range(1, N_DEV))
PLAN_SCATTER = tuple((k, "xk", 0) for k in range(1, N_DEV))
PLAN_GATHER_CHIPS = tuple((k, "x", 0) for k in (1, 2, 4, 6))
PLAN_GATHER_PASS_ON = tuple((1, ("land", q), q) for q in (2, 4, 6))


def _plan_refs(plan, j, x_ref, land_ref, me, peers, receiving):
    k, source, r = plan[j]
    index_of = lambda q: me if q == 0 else peers[q - 1][1]
    pos, target = peers[k - 1]
    if source == "x":
        src = x_ref
    elif source == "xk":
        src = x_ref.at[target]
    else:
        src = land_ref.at[index_of(source[1])]
    return pos, src, land_ref.at[index_of(k ^ r) if receiving else index_of(r)]


def _exchange_start(x, *, plan, name, after=None, land=None, slab=None):
    n_after = 0 if after is None else 1
    n = len(plan)

    def body(*refs):
        x_ref, land_ref, send_sems, recv_sems, _, _, token = refs[n_after:]
        me, peers = _peers()
        for j in range(n):
            pos, src, dst = _plan_refs(plan, j, x_ref, land_ref, me, peers, receiving=False)
            pltpu.make_async_remote_copy(src_ref=src, dst_ref=dst, send_sem=send_sems.at[j], recv_sem=recv_sems.at[j],
                                         device_id=pos, device_id_type=pl.DeviceIdType.MESH).start()
        token[...] = jnp.zeros_like(token)

    if land is None:
        land = lax.empty((N_DEV,) + tuple(slab), x.dtype)
    return pl.pallas_call(
        body, name=name,
        out_shape=(pltpu.SemaphoreType.DMA((n,)), pltpu.SemaphoreType.DMA((n,)),
                   pltpu.HBM(x.shape, x.dtype), pltpu.HBM(land.shape, land.dtype), jax.ShapeDtypeStruct((8, 128), F32)),
        in_specs=[_ANY] * n_after + [_HBM, _HBM],
        out_specs=(_SEM, _SEM, _HBM, _HBM, pl.BlockSpec(memory_space=pltpu.VMEM)),
        input_output_aliases={n_after: 2, n_after + 1: 3},
        compiler_params=pltpu.CompilerParams(has_side_effects=_EFFECT),
    )(*((after,) if n_after else ()), pltpu.with_memory_space_constraint(x, pltpu.HBM),
      pltpu.with_memory_space_constraint(land, pltpu.HBM))


def _exchange_wait(handle, after, *, plan, name):
    send_sems, recv_sems, x_thru, land_thru, _ = handle
    afters = list(after) if isinstance(after, (list, tuple)) else [after]

    def body(x_ref, land_ref, send_sems, recv_sems, *rest):
        me, peers = _peers()
        for j in range(len(plan)):
            pos, src, dst = _plan_refs(plan, j, x_ref, land_ref, me, peers, receiving=True)
            cp = pltpu.make_async_remote_copy(src_ref=src, dst_ref=dst, send_sem=send_sems.at[j], recv_sem=recv_sems.at[j],
                                              device_id=pos, device_id_type=pl.DeviceIdType.MESH)
            cp.wait_send()
            cp.wait_recv()

    return pl.pallas_call(
        body, name=name,
        out_shape=(pltpu.HBM(x_thru.shape, x_thru.dtype), pltpu.HBM(land_thru.shape, land_thru.dtype)),
        in_specs=[_HBM, _HBM, _SEM, _SEM] + [_ANY] * len(afters), out_specs=(_HBM, _HBM),
        input_output_aliases={0: 0, 1: 1}, compiler_params=pltpu.CompilerParams(has_side_effects=_EFFECT),
    )(x_thru, land_thru, send_sems, recv_sems, *afters)


W_IN_SLAB = D_IN // N_DEV


def _to_proj_rows(t):
    z = jnp.zeros((D_PROJ - C_SM - 2 * GDN_HEADS - GLA_RANK,) + t.shape[1:], t.dtype)
    return jnp.concatenate([t[R_Z:R_A], t[R_GR:R_LR], t[R_GQ:R_GR], t[:R_Z], t[R_A:R_GQ], t[R_LR:], z], axis=0)


def _from_proj_rows(t):
    ab = C_SM + 2 * GDN_HEADS
    return jnp.concatenate([t[C_QKV:C_SM], t[C_Z:C_GR], t[C_SM:ab], t[C_GQ:C_QKV], t[C_GR:C_GQ],
                            t[ab:ab + GLA_RANK]], axis=0)


def _local_step(x, target, meta, attn_nw, conv_w, a_log, dt_bias, gdn_nw, w2, b2, gla_nw, ffn_nw, final_nw,
                fetch, emit, start=None):
    S = x.shape[0]
    h0 = jnp.concatenate([jnp.zeros((ROW_PAD, D_MODEL), F32), meta, x], axis=0)
    target_p = jnp.concatenate([jnp.zeros((HEAD_ROWS, D_MODEL), F32), target], axis=0)
    conv_w8 = jnp.concatenate([conv_w, jnp.zeros((8 - CONV_K, conv_w.shape[1]), F32)], axis=0)
    w2p = jnp.zeros((SM_W, GLA_QK), F32).at[2 * GDN_HEADS:2 * GDN_HEADS + GLA_RANK].set(w2)
    alog_p = jnp.zeros((1, SM_W), F32).at[:, :GDN_HEADS].set(a_log)
    dt_p = jnp.zeros((1, SM_W), F32).at[:, :GDN_HEADS].set(dt_bias)

    n1 = _rmsnorm_fwd(h0, attn_nw, name="attn_norm", after=start)
    w_in_t = fetch("w_in_t", (n1, target_p, conv_w8, w2p, alog_p, dt_p))
    proj = _matmul(n1, w_in_t, mode="nt", name="in_proj")
    gb, la = _gates_fwd(proj, w2p, b2, alog_p, dt_p, name="gates")
    act = _prep_fwd(proj, conv_w8, name="gdn_prep")
    o_gdn, s_gdn, t_gdn = _gdn_fwd(act, gb, name="gdn_fwd")
    o_gla, s_gla = _gla_fwd(proj, la, name="gla_fwd")
    mixed = _mix_fwd(o_gdn, o_gla, proj, gdn_nw, gla_nw, name="mix")
    w_gate_t, w_up_t, w_out, w_down = fetch("rest", mixed)
    h1 = _matmul(mixed, w_out, mode="nn", add=h0, name="out_proj")
    n2 = _rmsnorm_fwd(h1, ffn_nw, name="ffn_norm")
    gate, up, hid = _swiglu_fwd(n2, w_gate_t, w_up_t, name="swiglu")
    h2 = _matmul(hid, w_down, mode="nn", add=h1, name="ffn_down", tm=688, tk=D_FF)
    dh2, dh2_b, d_final_nw, loss = _loss_head(h2, final_nw, target_p, name="loss_head")

    wg = dict(mode="tn", out_dtype=BF16, tn=512, tk=S + HEAD_ROWS)
    tok = emit("w_down", _matmul(hid, dh2_b, name="d_w_down", tm=704, **wg))
    d_gate, d_up = _swiglu_bwd(dh2_b, w_down, gate, up, name="d_swiglu", after=tok)
    tok = emit("w_gate_t", _matmul(d_gate, n2, name="d_w_gate", tm=704, **wg))
    tok = emit("w_up_t", _matmul(d_up, n2, name="d_w_up", tm=704, after=tok, **wg))
    d_n2 = _matmul(d_gate, w_gate_t, mode="nn", name="d_n2_gate", tm=688, tk=D_FF, after=tok)
    d_n2 = _matmul(d_up, w_up_t, mode="nn", add=d_n2, name="d_n2_up", tm=688, tk=D_FF)
    dh1, dh1_b, d_ffn_nw = _rmsnorm_bwd(h1, ffn_nw, d_n2, dh2, name="d_ffn_norm", also_bf16=True)

    tok = emit("w_out", _matmul(mixed, dh1_b, name="d_w_out", tm=512, **wg))
    d_mixed = _matmul(dh1_b, w_out, mode="nt", name="d_mixed", after=tok)
    do_gdn, do_gla, d_proj, d_gdn_nw, d_gla_nw = _mix_bwd(o_gdn, o_gla, proj, gdn_nw, gla_nw, d_mixed, name="d_mix")
    d_proj, d_la = _gla_bwd(proj, la, do_gla, s_gla, d_proj, name="gla_bwd")
    dact, dgb_heads = _gdn_bwd(act, gb, do_gdn, s_gdn, t_gdn, name="gdn_bwd")
    d_proj, d_w2p, d_b2, d_alog, d_dt = _gates_bwd(proj, w2p, b2, alog_p, dt_p, dgb_heads, d_la, d_proj, name="d_gates")
    d_proj, d_conv_w8 = _prep_bwd(proj, conv_w8, dact, d_proj, name="d_gdn_prep")
    tok = emit("w_in_t", _matmul(d_proj, n1, name="d_w_in", tm=768, **wg))
    d_n1 = _matmul(d_proj, w_in_t, mode="nn", name="d_n1", tm=688, tk=D_PROJ, after=tok)
    dh0, d_attn_nw = _rmsnorm_bwd(h0, attn_nw, d_n1, dh1, name="d_attn_norm", also_bf16=False)

    return dict(
        loss=loss[0, 0], grad_x=dh0[HEAD_ROWS:], meta=dh0[ROW_PAD:HEAD_ROWS], attn_nw=d_attn_nw,
        conv_w=d_conv_w8[:CONV_K], a_log=d_alog[:, :GDN_HEADS], dt_bias=d_dt[:, :GDN_HEADS], gdn_nw=d_gdn_nw,
        w2=d_w2p[2 * GDN_HEADS:2 * GDN_HEADS + GLA_RANK], b2=d_b2, gla_nw=d_gla_nw, ffn_nw=d_ffn_nw,
        final_nw=d_final_nw)


SMALL_ROWS = 32


def kernel(x, meta_tokens, attn_norm_w, w_in, gdn_conv_w, gdn_a_log, gdn_dt_bias, gdn_norm_w, gla_gate_w2, gla_gate_b, gla_norm_w, w_out, ffn_norm_w, w_gate, w_up, w_down, final_norm_w, loss_target, m_meta_tokens, m_attn_norm_w, m_w_in, m_gdn_conv_w, m_gdn_a_log, m_gdn_dt_bias, m_gdn_norm_w, m_gla_gate_w2, m_gla_gate_b, m_gla_norm_w, m_w_out, m_ffn_norm_w, m_w_gate, m_w_up, m_w_down, m_final_norm_w, v_meta_tokens, v_attn_norm_w, v_w_in, v_gdn_conv_w, v_gdn_a_log, v_gdn_dt_bias, v_gdn_norm_w, v_gla_gate_w2, v_gla_gate_b, v_gla_norm_w, v_w_out, v_ffn_norm_w, v_w_gate, v_w_up, v_w_down, v_final_norm_w):
    me = 4 * lax.axis_index("x") + 2 * lax.axis_index("y") + lax.axis_index("c")
    n_in, n_ff, n_out = D_IN // N_DEV, D_FF // N_DEV, D_MODEL // N_DEV

    n_conv = gdn_conv_w.shape[2]
    n_w2 = gla_gate_w2.shape[2]
    n_meta = meta_tokens.shape[1]
    small = jnp.zeros((40, n_conv), F32)
    small = small.at[0:N_META, :n_meta].set(meta_tokens)
    small = small.at[N_META:N_META + CONV_K, :].set(gdn_conv_w[0])
    small = small.at[24:24 + GLA_RANK, :n_w2].set(gla_gate_w2[0])
    small_all = _exchange(small, gather=True, name="gather_small")
    meta_f = small_all[:, 0:N_META, :n_meta].transpose(1, 0, 2).reshape(N_META, D_MODEL)
    conv_f = small_all[:, N_META:N_META + CONV_K, :].transpose(1, 0, 2).reshape(CONV_K, N_DEV * n_conv)
    w2_f = small_all[:, 24:24 + GLA_RANK, :n_w2].transpose(1, 0, 2).reshape(GLA_RANK, N_DEV * n_w2)

    o1, o2, o3 = n_ff, 2 * n_ff, 2 * n_ff + n_out
    w_in_slab = w_in[0].T.astype(BF16)
    in_h = _exchange_start(w_in_slab, plan=PLAN_GATHER_CHIPS, slab=w_in_slab.shape, name="gather_w_in_start")
    rest = jnp.concatenate([w_gate[0].T, w_up[0].T, w_out[0], w_down[0]], axis=0).astype(BF16)
    rest_h = _exchange_start(rest, plan=PLAN_GATHER, slab=rest.shape, name="gather_rest_start", after=in_h[4])

    def fetch(name, after):
        if name == "w_in_t":
            own, got = _exchange_wait(in_h, after, plan=PLAN_GATHER_CHIPS, name="gather_w_in_wait")
            pass_h = _exchange_start(own, plan=PLAN_GATHER_PASS_ON, land=got, name="pass_w_in_start")
            own, got = _exchange_wait(pass_h, pass_h[4], plan=PLAN_GATHER_PASS_ON, name="pass_w_in_wait")
            got = lax.dynamic_update_index_in_dim(got, own, me, 0)
            return _to_proj_rows(got.reshape(D_IN, D_MODEL))
        own, got = _exchange_wait(rest_h, after, plan=PLAN_GATHER, name="gather_rest_wait")
        got = lax.dynamic_update_index_in_dim(got, own, me, 0)
        return (got[:, :o1].reshape(D_FF, D_MODEL), got[:, o1:o2].reshape(D_FF, D_MODEL),
                got[:, o2:o3].reshape(D_MODEL, D_MODEL), got[:, o3:].reshape(D_FF, D_MODEL))

    sent = {}

    def emit(name, grad):
        if name == "w_in_t":
            grad = _from_proj_rows(grad)
        parts = grad.reshape(N_DEV, grad.shape[0] // N_DEV, D_MODEL)
        sent[name] = _exchange_start(parts, plan=PLAN_SCATTER, slab=parts.shape[1:], name="scatter_" + name + "_start")
        return sent[name][4]

    g = _local_step(x[0], loss_target[0], meta_f, attn_norm_w, conv_f, gdn_a_log, gdn_dt_bias, gdn_norm_w, w2_f,
                    gla_gate_b, gla_norm_w, ffn_norm_w, final_norm_w.reshape(1, D_MODEL), fetch, emit, start=rest_h[4])

    def total(name, after):
        handle = sent[name]
        own, got = _exchange_wait(handle, after, plan=PLAN_SCATTER, name="scatter_" + name + "_wait")
        got = lax.dynamic_update_index_in_dim(got, lax.dynamic_index_in_dim(own, me, 0, keepdims=False), me, 0)
        return _sum_slabs(got, name="sum_" + name)

    grad_w_down = total("w_down", g["attn_nw"])[None]
    grad_w_gate = total("w_gate_t", grad_w_down)
    grad_w_up = total("w_up_t", grad_w_gate)
    grad_w_out = total("w_out", grad_w_up)[None]
    grad_w_in = total("w_in_t", grad_w_out)

    misc = jnp.concatenate([g["a_log"], g["dt_bias"], g["gdn_nw"], g["gla_nw"], g["b2"], g["loss"].reshape(1, 1)], axis=1)
    n_misc = misc.shape[1]
    misc = jnp.pad(misc, ((0, 0), (0, D_MODEL - n_misc)))
    rows = jnp.concatenate([g["attn_nw"], g["ffn_nw"], g["final_nw"], misc, g["meta"],
                            g["conv_w"].reshape(-1, D_MODEL), g["w2"].reshape(-1, D_MODEL)], axis=0)
    rows = jnp.pad(rows, ((0, SMALL_ROWS - rows.shape[0]), (0, 0)))
    tot = _sum_slabs(_exchange(rows, gather=True, name="gather_small_grads"), name="sum_small_grads")
    grad_attn_nw, grad_ffn_nw, grad_final_nw = tot[0:1], tot[1:2], tot[2]
    grad_a_log = tot[3:4, 0:8]
    grad_dt = tot[3:4, 8:16]
    grad_gdn_nw = tot[3:4, 16:16 + GDN_DV]
    grad_gla_nw = tot[3:4, 144:144 + GLA_DV]
    grad_b2 = tot[3:4, 400:400 + GLA_QK]
    loss = tot[3, n_misc - 1]
    r0 = 4 + N_META
    grad_meta = lax.dynamic_slice(tot[4:r0], (0, me * n_meta), (N_META, n_meta))
    r1 = r0 + CONV_K * N_DEV * n_conv // D_MODEL
    grad_conv = lax.dynamic_slice(tot[r0:r1].reshape(CONV_K, N_DEV * n_conv), (0, me * n_conv), (CONV_K, n_conv))[None]
    r2 = r1 + GLA_RANK * N_DEV * n_w2 // D_MODEL
    grad_w2 = lax.dynamic_slice(tot[r1:r2].reshape(GLA_RANK, N_DEV * n_w2), (0, me * n_w2), (GLA_RANK, n_w2))[None]

    weights = [meta_tokens, attn_norm_w, w_in, gdn_conv_w, gdn_a_log, gdn_dt_bias, gdn_norm_w, gla_gate_w2,
               gla_gate_b, gla_norm_w, w_out, ffn_norm_w, w_gate, w_up, w_down, final_norm_w]
    grads = [grad_meta, grad_attn_nw, grad_w_in, grad_conv, grad_a_log, grad_dt, grad_gdn_nw, grad_w2,
             grad_b2, grad_gla_nw, grad_w_out, grad_ffn_nw, grad_w_gate, grad_w_up, grad_w_down, grad_final_nw]
    ms = [m_meta_tokens, m_attn_norm_w, m_w_in, m_gdn_conv_w, m_gdn_a_log, m_gdn_dt_bias, m_gdn_norm_w,
          m_gla_gate_w2, m_gla_gate_b, m_gla_norm_w, m_w_out, m_ffn_norm_w, m_w_gate, m_w_up, m_w_down, m_final_norm_w]
    vs = [v_meta_tokens, v_attn_norm_w, v_w_in, v_gdn_conv_w, v_gdn_a_log, v_gdn_dt_bias, v_gdn_norm_w,
          v_gla_gate_w2, v_gla_gate_b, v_gla_norm_w, v_w_out, v_ffn_norm_w, v_w_gate, v_w_up, v_w_down, v_final_norm_w]
    transposed = (2, 12, 13)
    outs = [[], [], [], []]
    for idx, (w, gr, m, v) in enumerate(zip(weights, grads, ms, vs)):
        if idx in transposed:
            res = (gr,) + _adamw(w[0].T, gr, m[0].T, v[0].T, name=f"adamw_{idx}")
            res = [t.T[None] for t in res]
        else:
            gr = gr.reshape(w.shape)
            res = (gr,) + _adamw(w, gr, m, v, name=f"adamw_{idx}")
        for lst, t in zip(outs, res):
            lst.append(t)
    return (loss, g["grad_x"][None], *outs[0], *outs[1], *outs[2], *outs[3])
```

```python
import functools

import jax
import jax.numpy as jnp
from jax import lax
from jax.experimental import pallas as pl
from jax.experimental.pallas import tpu as pltpu

F32 = jnp.float32
BF16 = jnp.bfloat16
_MXU_DTYPE = jnp.bfloat16

D_MODEL = 2048
N_META = 16
ROW_PAD = 48
HEAD_ROWS = ROW_PAD + N_META
CONV_K = 4
GDN_HEADS, GDN_DK, GDN_DV, GDN_CHUNK = 8, 128, 128, 64
GLA_HEADS, GLA_DK, GLA_DV, GLA_CHUNK = 4, 128, 256, 16
GLA_RANK = 16
GLA_GATE_NORMALIZER = 16.0
GDN_QK = GDN_HEADS * GDN_DK
GDN_V = GDN_HEADS * GDN_DV
GLA_QK = GLA_HEADS * GLA_DK
GLA_V = GLA_HEADS * GLA_DV
D_FF = 5632
D_IN = 7200
NORM_EPS = 1e-6
C_Z, C_GR, C_GQ, C_GK, C_GV, C_QKV, C_SM = 0, 1024, 2048, 2560, 3072, 4096, 7168
SM_W = 128
D_PROJ = 7680
R_Z, R_A, R_B, R_GQ, R_GK, R_GV, R_GR, R_LR = 3072, 4096, 4104, 4112, 4624, 5136, 6160, 7184

ADAM_LR, ADAM_B1, ADAM_B2, ADAM_EPS, ADAM_WD, ADAM_STEP = 0.001, 0.9, 0.999, 1e-08, 0.01, 10

N_DEV = 8
VMEM_LIMIT = 56 * 1024 * 1024

NN = (((1,), (0,)), ((), ()))
NT = (((1,), (1,)), ((), ()))
TN = (((0,), (0,)), ((), ()))


def _dot(a, b, dims=NN):
    return lax.dot_general(a.astype(_MXU_DTYPE), b.astype(_MXU_DTYPE), dims, preferred_element_type=F32)


def _dotx(a, b, dims=NN):
    return lax.dot_general(a, b, dims, precision=lax.Precision.HIGHEST, preferred_element_type=F32)


def _dot3(a, b):
    ah = a.astype(BF16)
    al = (a - ah.astype(F32)).astype(BF16)
    bh = b.astype(BF16)
    bl = (b - bh.astype(F32)).astype(BF16)
    d = functools.partial(lax.dot_general, dimension_numbers=NN, preferred_element_type=F32)
    return d(ah, bh) + (d(ah, bl) + d(al, bh))


def _tile(n, target, mult=8):
    best = None
    for t in range(mult, min(n, target) + 1, mult):
        if n % t == 0:
            best = t
    return best if best is not None else n


def _params(*sem):
    return pltpu.CompilerParams(dimension_semantics=sem, vmem_limit_bytes=VMEM_LIMIT)


def _sigmoid(x):
    return 0.5 * jnp.tanh(0.5 * x) + 0.5


def _softplus(x):
    return jnp.maximum(x, 0.0) + jnp.log1p(jnp.exp(-jnp.abs(x)))


def _silu_and_grad(c):
    s = _sigmoid(c)
    return c * s, s * (1.0 + c * (1.0 - s))


_ANY = pl.BlockSpec(memory_space=pl.ANY)


def _matmul(a, b, *, mode, name, out_dtype=F32, add=None, after=None, tm=1376, tn=512, tk=2064):
    if mode == "tn":
        K, M = a.shape
        N = b.shape[1]
    else:
        M, K = a.shape
        N = b.shape[0] if mode == "nt" else b.shape[1]
    tm = _tile(M, tm, 128 if mode == "tn" else 16)
    tn = _tile(N, tn, 128)
    tk = _tile(K, tk, 16 if mode == "tn" else 128)
    gm, gn, gk = M // tm, N // tn, K // tk
    dims = {"nn": NN, "nt": NT, "tn": TN}[mode]

    n_after = 0 if after is None else 1

    def body(*refs):
        refs = refs[n_after:]
        if add is None:
            a_ref, b_ref, o_ref = refs[:3]
            add_ref = None
        else:
            a_ref, b_ref, add_ref, o_ref = refs[:4]
        p = _dot(a_ref[...], b_ref[...], dims)

        def finish(r):
            if add_ref is not None:
                r = r + add_ref[...]
            o_ref[...] = r.astype(out_dtype)

        if gk == 1:
            finish(p)
        else:
            acc_ref = refs[-1]
            k = pl.program_id(2)

            @pl.when(k == 0)
            def _():
                acc_ref[...] = p

            @pl.when(k > 0)
            def _():
                acc_ref[...] += p

            @pl.when(k == gk - 1)
            def _():
                finish(acc_ref[...])

    if mode == "tn":
        a_spec = pl.BlockSpec((tk, tm), lambda i, j, k: (k, i))
    else:
        a_spec = pl.BlockSpec((tm, tk), lambda i, j, k: (i, k))
    if mode == "nt":
        b_spec = pl.BlockSpec((tn, tk), lambda i, j, k: (j, k))
    else:
        b_spec = pl.BlockSpec((tk, tn), lambda i, j, k: (k, j))
    o_spec = pl.BlockSpec((tm, tn), lambda i, j, k: (i, j))
    in_specs = [_ANY] * n_after + [a_spec, b_spec] + ([o_spec] if add is not None else [])
    args = ((after,) if n_after else ()) + (a, b) + ((add,) if add is not None else ())
    return pl.pallas_call(
        body, name=name, grid=(gm, gn, gk), in_specs=in_specs, out_specs=o_spec,
        out_shape=jax.ShapeDtypeStruct((M, N), out_dtype),
        scratch_shapes=[pltpu.VMEM((tm, tn), F32)] if gk > 1 else [],
        compiler_params=_params("parallel", "parallel", "arbitrary"),
    )(*args)


def _matmul_pair(a1, b1, a2, b2, *, name, after=None, tm=688, tn=256):
    M, K = a1.shape
    N = b1.shape[1]
    tm, tn = _tile(M, tm, 16), _tile(N, tn, 128)
    n_after = 0 if after is None else 1

    def body(*refs):
        a1_ref, b1_ref, a2_ref, b2_ref, o_ref = refs[n_after:]
        o_ref[...] = _dot(a1_ref[...], b1_ref[...]) + _dot(a2_ref[...], b2_ref[...])

    a_spec = pl.BlockSpec((tm, K), lambda i, j: (i, 0))
    b_spec = pl.BlockSpec((K, tn), lambda i, j: (0, j))
    return pl.pallas_call(
        body, name=name, grid=(M // tm, N // tn), in_specs=[_ANY] * n_after + [a_spec, b_spec, a_spec, b_spec],
        out_specs=pl.BlockSpec((tm, tn), lambda i, j: (i, j)), out_shape=jax.ShapeDtypeStruct((M, N), F32),
        compiler_params=_params("parallel", "parallel"),
    )(*((after,) if n_after else ()), a1, b1, a2, b2)


def _rmsnorm_fwd(h, w, *, name, after=None):
    M, D = h.shape
    tm = _tile(M, 688, 16)
    n_after = 0 if after is None else 1

    def body(*refs):
        h_ref, w_ref, n_ref = refs[n_after:]
        x = h_ref[...]
        r = lax.rsqrt(jnp.mean(x * x, axis=-1, keepdims=True) + NORM_EPS)
        n_ref[...] = (x * r * w_ref[...]).astype(n_ref.dtype)

    return pl.pallas_call(
        body, name=name, grid=(M // tm,),
        in_specs=[_ANY] * n_after + [pl.BlockSpec((tm, D), lambda i: (i, 0)), pl.BlockSpec((1, D), lambda i: (0, 0))],
        out_specs=pl.BlockSpec((tm, D), lambda i: (i, 0)),
        out_shape=jax.ShapeDtypeStruct((M, D), BF16),
        compiler_params=_params("parallel"),
    )(*((after,) if n_after else ()), h, w)


def _rmsnorm_bwd(h, w, dn, dres, *, name, also_bf16):
    M, D = h.shape
    tm = _tile(M, 344, 16)
    g = M // tm

    def body(h_ref, w_ref, dn_ref, dres_ref, dh_ref, *rest):
        dhb_ref = rest[0] if also_bf16 else None
        dw_ref, acc_ref = rest[-2:]
        i = pl.program_id(0)
        x = h_ref[...]
        r = lax.rsqrt(jnp.mean(x * x, axis=-1, keepdims=True) + NORM_EPS)
        xhat = x * r
        dn_ = dn_ref[...]
        dxhat = dn_ * w_ref[...]
        dh = dres_ref[...] + r * (dxhat - xhat * jnp.mean(dxhat * xhat, axis=-1, keepdims=True))
        dh_ref[...] = dh
        if also_bf16:
            dhb_ref[...] = dh.astype(dhb_ref.dtype)
        part = jnp.sum((dn_ * xhat).reshape(tm // 8, 8, D), axis=0)

        @pl.when(i == 0)
        def _():
            acc_ref[...] = part

        @pl.when(i > 0)
        def _():
            acc_ref[...] += part

        @pl.when(i == g - 1)
        def _():
            dw_ref[...] = jnp.sum(acc_ref[...], axis=0, keepdims=True)

    row = pl.BlockSpec((tm, D), lambda i: (i, 0))
    vec = pl.BlockSpec((1, D), lambda i: (0, 0))
    return pl.pallas_call(
        body, name=name, grid=(g,), in_specs=[row, vec, row, row],
        out_specs=[row] + ([row] if also_bf16 else []) + [vec],
        out_shape=[jax.ShapeDtypeStruct((M, D), F32)] + ([jax.ShapeDtypeStruct((M, D), BF16)] if also_bf16 else [])
        + [jax.ShapeDtypeStruct((1, D), F32)],
        scratch_shapes=[pltpu.VMEM((8, D), F32)],
        compiler_params=_params("arbitrary"),
    )(h, w, dn, dres)


def _loss_head(h, w, target_p, *, name):
    M, D = h.shape
    tm = _tile(M, 344, 16)
    g = M // tm

    def body(h_ref, w_ref, t_ref, dh_ref, dhb_ref, dw_ref, loss_ref, acc_ref, lacc_ref):
        i = pl.program_id(0)
        x = h_ref[...]
        row = i * tm + lax.broadcasted_iota(jnp.int32, (tm, 1), 0)
        live = row >= HEAD_ROWS
        r = lax.rsqrt(jnp.mean(x * x, axis=-1, keepdims=True) + NORM_EPS)
        xhat = x * r
        err = jnp.where(live, xhat * w_ref[...] - t_ref[...], 0.0)
        dy = err * (1.0 / D)
        dxhat = dy * w_ref[...]
        dh = r * (dxhat - xhat * jnp.mean(dxhat * xhat, axis=-1, keepdims=True))
        dh_ref[...] = dh
        dhb_ref[...] = dh.astype(dhb_ref.dtype)
        part = jnp.sum((dy * xhat).reshape(tm // 8, 8, D), axis=0)
        lpart = jnp.sum((err * err).reshape(tm // 8, 8, D), axis=0)

        @pl.when(i == 0)
        def _():
            acc_ref[...] = part
            lacc_ref[...] = lpart

        @pl.when(i > 0)
        def _():
            acc_ref[...] += part
            lacc_ref[...] += lpart

        @pl.when(i == g - 1)
        def _():
            dw_ref[...] = jnp.sum(acc_ref[...], axis=0, keepdims=True)
            tot = jnp.sum(jnp.sum(lacc_ref[...], axis=0, keepdims=True), axis=1, keepdims=True)
            loss_ref[...] = jnp.broadcast_to(tot * (0.5 / D), (1, 128))

    row = pl.BlockSpec((tm, D), lambda i: (i, 0))
    vec = pl.BlockSpec((1, D), lambda i: (0, 0))
    return pl.pallas_call(
        body, name=name, grid=(g,), in_specs=[row, vec, row],
        out_specs=[row, row, vec, pl.BlockSpec((1, 128), lambda i: (0, 0))],
        out_shape=[jax.ShapeDtypeStruct((M, D), F32), jax.ShapeDtypeStruct((M, D), BF16),
                   jax.ShapeDtypeStruct((1, D), F32), jax.ShapeDtypeStruct((1, 128), F32)],
        scratch_shapes=[pltpu.VMEM((8, D), F32), pltpu.VMEM((8, D), F32)],
        compiler_params=_params("arbitrary"),
    )(h, w, target_p)


def _gate_terms(sm, w2p, b2, alog_p, dt_p, row0):
    tm = sm.shape[0]
    lane = lax.broadcasted_iota(jnp.int32, (tm, SM_W), 1)
    live = (row0 + lax.broadcasted_iota(jnp.int32, (tm, 1), 0)) >= ROW_PAD
    pre = sm + dt_p
    neg_a = -jnp.exp(alog_p)
    g = neg_a * _softplus(pre)
    beta = _sigmoid(sm)
    z = _dot(sm, w2p) + b2
    return lane, live, pre, neg_a, g, beta, z


def _gates_fwd(proj, w2p, b2, alog_p, dt_p, *, name):
    M = proj.shape[0]
    tm = _tile(M, 688, 8)

    def body(sm_ref, w2_ref, b2_ref, al_ref, dt_ref, gb_ref, la_ref):
        row0 = pl.program_id(0) * tm
        lane, live, _, _, g, beta, z = _gate_terms(sm_ref[...], w2_ref[...], b2_ref[...], al_ref[...], dt_ref[...], row0)
        gb = jnp.where(lane < GDN_HEADS, g, jnp.where(lane < 2 * GDN_HEADS, beta, 0.0))
        gb_ref[...] = jnp.where(live, gb, 0.0)
        la = (jnp.minimum(z, 0.0) - jnp.log1p(jnp.exp(-jnp.abs(z)))) * (1.0 / GLA_GATE_NORMALIZER)
        la_ref[...] = jnp.where(live, la, 0.0)

    full = lambda s: pl.BlockSpec(s, lambda i: (0, 0))
    return pl.pallas_call(
        body, name=name, grid=(M // tm,),
        in_specs=[pl.BlockSpec((tm, SM_W), lambda i: (i, C_SM // SM_W)), full((SM_W, GLA_QK)), full((1, GLA_QK)),
                  full((1, SM_W)), full((1, SM_W))],
        out_specs=[pl.BlockSpec((tm, SM_W), lambda i: (i, 0)), pl.BlockSpec((tm, GLA_QK), lambda i: (i, 0))],
        out_shape=[jax.ShapeDtypeStruct((M, SM_W), F32), jax.ShapeDtypeStruct((M, GLA_QK), F32)],
        compiler_params=_params("parallel"),
    )(proj, w2p, b2, alog_p, dt_p)


def _gates_bwd(proj, w2p, b2, alog_p, dt_p, dgb_heads, dla, d_proj, *, name):
    M = proj.shape[0]
    tm = _tile(M, 688, 8)
    g_ = M // tm

    tail_w = D_PROJ - C_SM

    def body(sm_ref, w2_ref, b2_ref, al_ref, dt_ref, dgb_ref, dla_ref, _,
             dsm_ref, dw2_ref, db2_ref, dal_ref, ddt_ref):
        i = pl.program_id(0)
        sm = sm_ref[...]
        lane, live, pre, neg_a, g, beta, z = _gate_terms(sm, w2_ref[...], b2_ref[...], al_ref[...], dt_ref[...], i * tm)
        dz = jnp.where(live, dla_ref[...] * (_sigmoid(-z) * (1.0 / GLA_GATE_NORMALIZER)), 0.0)
        dsm_lr = _dot(dz, w2_ref[...], NT)
        dgb = dgb_ref[0]
        for hh in range(1, GDN_HEADS):
            dgb = dgb + dgb_ref[hh]
        dgb = jnp.where(live, dgb, 0.0)
        da = dgb * neg_a * _sigmoid(pre)
        db = dgb * beta * (1.0 - beta)
        dsm = jnp.where(lane < GDN_HEADS, da, jnp.where(lane < 2 * GDN_HEADS, db, dsm_lr))
        dsm_ref[:, 0:SM_W] = dsm.astype(dsm_ref.dtype)
        dsm_ref[:, SM_W:tail_w] = jnp.zeros((tm, tail_w - SM_W), dsm_ref.dtype)
        is_a = lane < GDN_HEADS
        dal = jnp.sum(jnp.where(is_a, dgb * g, 0.0), axis=0, keepdims=True)
        ddt = jnp.sum(jnp.where(is_a, da, 0.0), axis=0, keepdims=True)
        dw2 = _dot(sm, dz, TN)
        db2 = jnp.sum(dz, axis=0, keepdims=True)

        @pl.when(i == 0)
        def _():
            dw2_ref[...] = dw2
            db2_ref[...] = db2
            dal_ref[...] = dal
            ddt_ref[...] = ddt

        @pl.when(i > 0)
        def _():
            dw2_ref[...] += dw2
            db2_ref[...] += db2
            dal_ref[...] += dal
            ddt_ref[...] += ddt

    full = lambda s: pl.BlockSpec(s, lambda i: (0, 0))
    return pl.pallas_call(
        body, name=name, grid=(g_,),
        in_specs=[pl.BlockSpec((tm, SM_W), lambda i: (i, C_SM // SM_W)), full((SM_W, GLA_QK)), full((1, GLA_QK)),
                  full((1, SM_W)), full((1, SM_W)),
                  pl.BlockSpec((GDN_HEADS, tm, SM_W), lambda i: (0, i, 0)),
                  pl.BlockSpec((tm, GLA_QK), lambda i: (i, 0)), _ANY],
        out_specs=[pl.BlockSpec((tm, tail_w), lambda i: (i, C_SM // tail_w)), full((SM_W, GLA_QK)), full((1, GLA_QK)),
                   full((1, SM_W)), full((1, SM_W))],
        out_shape=[jax.ShapeDtypeStruct(d_proj.shape, d_proj.dtype), jax.ShapeDtypeStruct((SM_W, GLA_QK), F32),
                   jax.ShapeDtypeStruct((1, GLA_QK), F32), jax.ShapeDtypeStruct((1, SM_W), F32),
                   jax.ShapeDtypeStruct((1, SM_W), F32)],
        input_output_aliases={7: 0},
        compiler_params=_params("arbitrary"),
    )(proj, w2p, b2, alog_p, dt_p, dgb_heads, dla, d_proj)


QKV_W = GDN_QK
N_QKV_GROUPS = 3
QKV_B0 = C_QKV // QKV_W
HALO = 8


def _conv_terms(x_ref, halo_ref, cw_ref, xs_ref, i, tm):
    xs_ref[HALO:HALO + tm, :] = x_ref[...]
    xs_ref[0:HALO, :] = jnp.where(i > 0, halo_ref[...], 0.0)
    cw = cw_ref[...]
    xs = xs_ref[...]
    taps = [(pltpu.roll(xs, CONV_K - 1 - t, 0) if t < CONV_K - 1 else xs)[HALO:HALO + tm, :] for t in range(CONV_K)]
    c = taps[0] * cw[0:1, :]
    for t in range(1, CONV_K):
        c = c + taps[t] * cw[t:t + 1, :]
    return c, taps


def _prep_fwd(proj, conv_w8, *, name):
    M = proj.shape[0]
    tm = _tile(M, 344, 8)

    def body(x_ref, halo_ref, cw_ref, o_ref, xs_ref):
        j, i = pl.program_id(0), pl.program_id(1)
        c, _ = _conv_terms(x_ref, halo_ref, cw_ref, xs_ref, i, tm)
        s, _ = _silu_and_grad(c)
        scale = jnp.where(j == 0, GDN_DK ** -0.5, 1.0)
        for hh in range(GDN_HEADS):
            cols = slice(hh * 128, (hh + 1) * 128)
            sh = s[:, cols]
            r = lax.rsqrt(jnp.sum(sh * sh, axis=-1, keepdims=True) + NORM_EPS)
            o_ref[:, cols] = jnp.where(j < 2, sh * (r * scale), sh)

    hb = tm // HALO
    return pl.pallas_call(
        body, name=name, grid=(N_QKV_GROUPS, M // tm),
        in_specs=[pl.BlockSpec((tm, QKV_W), lambda j, i: (i, QKV_B0 + j)),
                  pl.BlockSpec((HALO, QKV_W), lambda j, i: (jnp.maximum(i * hb - 1, 0), QKV_B0 + j)),
                  pl.BlockSpec((8, QKV_W), lambda j, i: (0, j))],
        out_specs=pl.BlockSpec((tm, QKV_W), lambda j, i: (i, j)),
        out_shape=jax.ShapeDtypeStruct((M, N_QKV_GROUPS * QKV_W), F32),
        scratch_shapes=[pltpu.VMEM((tm + HALO, QKV_W), F32)],
        compiler_params=_params("parallel", "arbitrary"),
    )(proj, proj, conv_w8)


def _prep_bwd(proj, conv_w8, dact, d_proj, *, name):
    M = proj.shape[0]
    tm = _tile(M, 688, 16)
    g_ = M // tm
    ext = tm + HALO

    def body(x_ref, prev_ref, next_ref, cw_ref, da_ref, dan_ref, _, o_ref, dcw_ref, xs_ref, das_ref, dcs_ref):
        j, i = pl.program_id(0), pl.program_id(1)
        not_last = i < g_ - 1
        xs_ref[0:HALO, :] = jnp.where(i > 0, prev_ref[...], 0.0)
        xs_ref[HALO:HALO + tm, :] = x_ref[...]
        xs_ref[HALO + tm:HALO + ext, :] = jnp.where(not_last, next_ref[...], 0.0)
        das_ref[0:tm, :] = da_ref[...]
        das_ref[tm:ext, :] = jnp.where(not_last, dan_ref[...], 0.0)
        cw = cw_ref[...]
        xs = xs_ref[...]
        taps = [(pltpu.roll(xs, CONV_K - 1 - t, 0) if t < CONV_K - 1 else xs)[HALO:HALO + ext, :] for t in range(CONV_K)]
        c = taps[0] * cw[0:1, :]
        for t in range(1, CONV_K):
            c = c + taps[t] * cw[t:t + 1, :]
        s, ds_dc = _silu_and_grad(c)
        scale = jnp.where(j == 0, GDN_DK ** -0.5, 1.0)
        for hh in range(GDN_HEADS):
            cols = slice(hh * 128, (hh + 1) * 128)
            sh = s[:, cols]
            r = lax.rsqrt(jnp.sum(sh * sh, axis=-1, keepdims=True) + NORM_EPS)
            da = das_ref[:, cols]
            y = sh * r
            dy = da * scale
            ds_norm = r * (dy - y * jnp.sum(dy * y, axis=-1, keepdims=True))
            dcs_ref[:, cols] = jnp.where(j < 2, ds_norm, da) * ds_dc[:, cols]
        dc = dcs_ref[...]
        acc = dc[0:tm, :] * cw[CONV_K - 1:CONV_K, :]
        for t in range(CONV_K - 1):
            acc = acc + pltpu.roll(dc, ext - (CONV_K - 1 - t), 0)[0:tm, :] * cw[t:t + 1, :]
        o_ref[...] = acc.astype(o_ref.dtype)
        r8 = lax.broadcasted_iota(jnp.int32, (8, QKV_W), 0)
        part = jnp.zeros((8, QKV_W), F32)
        for t in range(CONV_K):
            part = jnp.where(r8 == t, jnp.sum(dc[0:tm, :] * taps[t][0:tm, :], axis=0, keepdims=True), part)

        @pl.when(i == 0)
        def _():
            dcw_ref[...] = part

        @pl.when(i > 0)
        def _():
            dcw_ref[...] += part

    hb = tm // HALO
    last = M // HALO - 1
    prev_of = lambda i: jnp.maximum(i * hb - 1, 0)
    next_of = lambda i: jnp.minimum((i + 1) * hb, last)
    return pl.pallas_call(
        body, name=name, grid=(N_QKV_GROUPS, g_),
        in_specs=[pl.BlockSpec((tm, QKV_W), lambda j, i: (i, QKV_B0 + j)),
                  pl.BlockSpec((HALO, QKV_W), lambda j, i: (prev_of(i), QKV_B0 + j)),
                  pl.BlockSpec((HALO, QKV_W), lambda j, i: (next_of(i), QKV_B0 + j)),
                  pl.BlockSpec((8, QKV_W), lambda j, i: (0, j)),
                  pl.BlockSpec((tm, QKV_W), lambda j, i: (i, j)),
                  pl.BlockSpec((HALO, QKV_W), lambda j, i: (next_of(i), j)), _ANY],
        out_specs=[pl.BlockSpec((tm, QKV_W), lambda j, i: (i, QKV_B0 + j)), pl.BlockSpec((8, QKV_W), lambda j, i: (0, j))],
        out_shape=[jax.ShapeDtypeStruct(d_proj.shape, d_proj.dtype),
                   jax.ShapeDtypeStruct((8, N_QKV_GROUPS * QKV_W), F32)],
        input_output_aliases={6: 0},
        scratch_shapes=[pltpu.VMEM((HALO + ext, QKV_W), F32), pltpu.VMEM((ext, QKV_W), F32), pltpu.VMEM((ext, QKV_W), F32)],
        compiler_params=_params("parallel", "arbitrary"),
    )(proj, proj, proj, conv_w8, dact, dact, d_proj)


def _round_robin(gens):
    gens = list(gens)
    while gens:
        alive = []
        for gen in gens:
            try:
                next(gen)
                alive.append(gen)
            except StopIteration:
                pass
        gens = alive


def _unit_lower_inverse(a_low, eye):
    n = a_low.shape[0]
    ri = lax.broadcasted_iota(jnp.int32, (n, n), 0)
    ci = lax.broadcasted_iota(jnp.int32, (n, n), 1)
    same = lambda shift: (ri >> shift) == (ci >> shift)
    b = jnp.where(same(3), -a_low, 0.0)
    x = eye + b
    p2 = _dot3(b, b)
    yield
    x = x + _dot3(x, p2)
    p4 = _dot3(p2, p2)
    yield
    x = x + _dot3(x, p4)
    yield
    for shift in (3, 4, 5):
        between = jnp.where(same(shift + 1) & ~same(shift), a_low, 0.0)
        t = _dot3(between, x)
        yield
        x = x - _dot3(x, t)
        yield
    return x


class _GdnChunk:
    def build(self, q, k, v, gb, h):
        C = GDN_CHUNK
        lane = lax.broadcasted_iota(jnp.int32, (C, SM_W), 1)
        g = jnp.sum(jnp.where(lane == h, gb, 0.0), axis=1, keepdims=True)
        self.beta = jnp.sum(jnp.where(lane == h + GDN_HEADS, gb, 0.0), axis=1, keepdims=True)
        ri = lax.broadcasted_iota(jnp.int32, (C, C), 0)
        ci = lax.broadcasted_iota(jnp.int32, (C, C), 1)
        self.causal = ri >= ci
        self.strict = ri > ci
        self.eye = (ri == ci).astype(F32)
        gcb = _dotx(self.causal.astype(F32), jnp.broadcast_to(g, (C, SM_W)))
        yield
        self.gcol = gcb[:, 0:1]
        grow = gcb.T[0:1, 0:C]
        self.decay = jnp.exp(jnp.where(self.causal, self.gcol - grow, -1e30))
        self.egc = jnp.exp(self.gcol)
        glast = gcb[C - 1:C, 0:1]
        self.elast = jnp.exp(glast - self.gcol)
        self.gl = jnp.exp(glast)
        self.q, self.k, self.v = q, k, v
        self.kb = k * self.beta
        m = _dot(self.kb, k, NT)
        n_ = _dot(q, k, NT)
        yield
        self.a_low = jnp.where(self.strict, m * self.decay, 0.0)
        self.p = n_ * self.decay
        self.qd = q * self.egc
        self.kd = k * self.elast
        self.bu = v * self.beta
        self.bw = self.kb * self.egc


GDN_HB = 8
GDN_HG = GDN_HEADS // GDN_HB


def _gdn_specs(n_of):
    C, W = GDN_CHUNK, 128 * GDN_HB
    q_spec = pl.BlockSpec((C, W), lambda g, n: (n_of(n), g))
    k_spec = pl.BlockSpec((C, W), lambda g, n: (n_of(n), g + GDN_HG))
    v_spec = pl.BlockSpec((C, W), lambda g, n: (n_of(n), g + 2 * GDN_HG))
    gb_spec = pl.BlockSpec((C, SM_W), lambda g, n: (n_of(n), 0))
    o_spec = pl.BlockSpec((C, W), lambda g, n: (n_of(n), g))
    s_spec = pl.BlockSpec((GDN_HB, None, GDN_DK, GDN_DV), lambda g, n: (g, n_of(n), 0, 0))
    t_spec = pl.BlockSpec((GDN_HB, None, C, C), lambda g, n: (g, n_of(n), 0, 0))
    return q_spec, k_spec, v_spec, gb_spec, o_spec, s_spec, t_spec


def _gdn_fwd(act, gb, *, name):
    M = act.shape[0]
    N = M // GDN_CHUNK

    def body(q_ref, k_ref, v_ref, gb_ref, o_ref, s_ref, t_ref, state):
        g, n = pl.program_id(0), pl.program_id(1)

        @pl.when(n == 0)
        def _():
            state[...] = jnp.zeros_like(state)

        gb_ = gb_ref[...]

        def head(hh):
            cols = slice(hh * 128, (hh + 1) * 128)
            c = _GdnChunk()
            yield from c.build(q_ref[:, cols], k_ref[:, cols], v_ref[:, cols], gb_, g * GDN_HB + hh)
            tinv = yield from _unit_lower_inverse(c.a_low, c.eye)
            s = state[hh]
            s_ref[hh] = s
            t_ref[hh] = tinv
            u = _dot(tinv, c.bu)
            w = _dot(tinv, c.bw)
            yield
            vn = u - _dot(w, s)
            o1 = _dot(c.qd, s)
            yield
            o_ref[:, cols] = o1 + _dot(c.p, vn)
            state[hh] = c.gl * s + _dot(c.kd, vn, TN)

        _round_robin(head(hh) for hh in range(GDN_HB))

    q_spec, k_spec, v_spec, gb_spec, o_spec, s_spec, t_spec = _gdn_specs(lambda n: n)
    return pl.pallas_call(
        body, name=name, grid=(GDN_HG, N),
        in_specs=[q_spec, k_spec, v_spec, gb_spec], out_specs=[o_spec, s_spec, t_spec],
        out_shape=[jax.ShapeDtypeStruct((M, GDN_V), F32),
                   jax.ShapeDtypeStruct((GDN_HEADS, N, GDN_DK, GDN_DV), F32),
                   jax.ShapeDtypeStruct((GDN_HEADS, N, GDN_CHUNK, GDN_CHUNK), F32)],
        scratch_shapes=[pltpu.VMEM((GDN_HB, GDN_DK, GDN_DV), F32)],
        compiler_params=_params("parallel", "arbitrary"),
    )(act, act, act, gb)


def _gdn_bwd(act, gb, do, s_all, t_all, *, name):
    M = act.shape[0]
    N = M // GDN_CHUNK
    C = GDN_CHUNK
    assert GDN_HG == 1

    def body(q_ref, k_ref, v_ref, gb_ref, do_ref, s_ref, t_ref, dact_ref, dgb_ref, dstate):
        g, n = pl.program_id(0), pl.program_id(1)

        @pl.when(n == 0)
        def _():
            dstate[...] = jnp.zeros_like(dstate)

        gb_ = gb_ref[...]
        last = lax.broadcasted_iota(jnp.int32, (C, 1), 0) == C - 1
        upper = (lax.broadcasted_iota(jnp.int32, (C, C), 0) <= lax.broadcasted_iota(jnp.int32, (C, C), 1)).astype(F32)
        lane = lax.broadcasted_iota(jnp.int32, (C, SM_W), 1)
        def head(hh):
            cols = slice(hh * 128, (hh + 1) * 128)
            h = g * GDN_HB + hh
            c = _GdnChunk()
            yield from c.build(q_ref[:, cols], k_ref[:, cols], v_ref[:, cols], gb_, h)
            tinv = t_ref[hh]
            s = s_ref[hh]
            do_ = do_ref[:, cols]
            ds1 = dstate[hh]
            u = _dot(tinv, c.bu)
            w = _dot(tinv, c.bw)
            dqd = _dot(do_, s, NT)
            dvn0 = _dot(c.p, do_, TN) + _dot(c.kd, ds1)
            dst0 = _dot(c.qd, do_, TN) + c.gl * ds1
            yield
            vn = u - _dot(w, s)
            dvn = dvn0
            yield
            dp = jnp.where(c.causal, _dot(do_, vn, NT), 0.0)
            dstate[hh] = dst0 - _dot(w, dvn, TN)
            dkd = _dot(vn, ds1, NT)
            dw = -_dot(dvn, s, NT)
            dbu = _dot(tinv, dvn, TN)
            dgl = jnp.sum(jnp.sum(s * ds1, axis=1, keepdims=True), axis=0, keepdims=True)
            yield
            dbw = _dot(tinv, dw, TN)
            t1 = _dot(dbu, u, NT)
            yield
            da = jnp.where(c.strict, -(t1 + _dot(dbw, w, NT)), 0.0)
            dn_ = dp * c.decay
            dq0 = _dot(dn_, c.k)
            dk0 = _dot(dn_, c.q, TN)
            yield
            dm = da * c.decay
            e = da * c.a_low + dp * c.p
            dkb = _dot(dm, c.k) + dbw * c.egc
            dact_ref[:, GDN_QK + hh * 128:GDN_QK + (hh + 1) * 128] = (
                _dot(dm, c.kb, TN) + dk0 + dkb * c.beta + dkd * c.elast)
            dact_ref[:, cols] = dq0 + dqd * c.egc
            dact_ref[:, 2 * GDN_QK + hh * 128:2 * GDN_QK + (hh + 1) * 128] = dbu * c.beta
            dbeta = jnp.sum(dbu * c.v, axis=1, keepdims=True) + jnp.sum(dkb * c.k, axis=1, keepdims=True)
            t_kd = jnp.sum(dkd * c.kd, axis=1, keepdims=True)
            dgc = (jnp.sum(e, axis=1, keepdims=True) - jnp.sum(e.T, axis=1, keepdims=True)
                   + jnp.sum(dbw * c.bw, axis=1, keepdims=True) + jnp.sum(dqd * c.qd, axis=1, keepdims=True) - t_kd)
            dgc = dgc + jnp.where(last, jnp.sum(t_kd, axis=0, keepdims=True) + dgl * c.gl, 0.0)
            yield
            dg = _dotx(upper, jnp.broadcast_to(dgc, (C, SM_W)))
            dgb_ref[hh] = jnp.where(lane == h, dg, jnp.where(lane == h + GDN_HEADS, dbeta, 0.0))

        _round_robin(head(hh) for hh in range(GDN_HB))

    rev = lambda n: N - 1 - n
    q_spec, k_spec, v_spec, gb_spec, o_spec, s_spec, t_spec = _gdn_specs(rev)
    dgb_spec = pl.BlockSpec((GDN_HB, C, SM_W), lambda g, n: (g, rev(n), 0))
    return pl.pallas_call(
        body, name=name, grid=(GDN_HG, N),
        in_specs=[q_spec, k_spec, v_spec, gb_spec, o_spec, s_spec, t_spec],
        out_specs=[pl.BlockSpec((C, 2 * GDN_QK + GDN_V), lambda g, n: (rev(n), 0)), dgb_spec],
        out_shape=[jax.ShapeDtypeStruct((M, 2 * GDN_QK + GDN_V), F32),
                   jax.ShapeDtypeStruct((GDN_HEADS, M, SM_W), F32)],
        scratch_shapes=[pltpu.VMEM((GDN_HB, GDN_DK, GDN_DV), F32)],
        compiler_params=_params("parallel", "arbitrary"),
    )(act, act, act, gb, do, s_all, t_all)


GLA_STEP_ROWS = 64
GLA_SUB = GLA_STEP_ROWS // GLA_CHUNK


def _gla_cumsum(la):
    C = GLA_CHUNK
    ltri = (lax.broadcasted_iota(jnp.int32, (C, C), 0) >= lax.broadcasted_iota(jnp.int32, (C, C), 1)).astype(F32)
    return _dotx(ltri, la)


def _gla_decay_rows(b, i):
    rj = lax.broadcasted_iota(jnp.int32, (GLA_CHUNK, GLA_DK), 0)
    return jnp.where(rj <= i, jnp.exp(jnp.minimum(b[i:i + 1, :] - b, 0.0)), 0.0)


GLA_HALF = GLA_CHUNK // 2


def _gla_cross_factors(b):
    top = lax.broadcasted_iota(jnp.int32, b.shape, 0) < GLA_HALF
    bm = b[GLA_HALF - 1:GLA_HALF, :]
    late = jnp.where(top, 0.0, jnp.exp(jnp.minimum(b - bm, 0.0)))
    early = jnp.where(top, jnp.exp(jnp.minimum(bm - b, 0.0)), 0.0)
    return late, early


def _gla_half_decay(bh, ii):
    rj = lax.broadcasted_iota(jnp.int32, bh.shape, 0)
    return jnp.where(rj <= ii, jnp.exp(jnp.minimum(bh[ii:ii + 1, :] - bh, 0.0)), 0.0)


def _gla_scores_t(q, k, b):
    C, H = GLA_CHUNK, GLA_HALF
    lane = lax.broadcasted_iota(jnp.int32, (H, C), 1)
    halves = []
    for h0 in (0, H):
        qh, kh, bh = q[h0:h0 + H], k[h0:h0 + H], b[h0:h0 + H]
        sth = jnp.zeros((H, C), F32)
        for ii in range(H):
            si = jnp.sum(qh[ii:ii + 1, :] * kh * _gla_half_decay(bh, ii), axis=1, keepdims=True)
            sth = jnp.where(lane == h0 + ii, si, sth)
            if ii % 4 == 3:
                yield
        halves.append(sth)
    late, early = _gla_cross_factors(b)
    between = _dot(k * early, q * late, NT)
    yield
    return jnp.concatenate(halves, axis=0) + between


def _gla_specs(n_of):
    R = GLA_STEP_ROWS
    q_spec = pl.BlockSpec((R, GLA_QK), lambda n: (n_of(n), C_GQ // GLA_QK))
    k_spec = pl.BlockSpec((R, GLA_QK), lambda n: (n_of(n), C_GK // GLA_QK))
    v_spec = pl.BlockSpec((R, GLA_V), lambda n: (n_of(n), C_GV // GLA_V))
    la_spec = pl.BlockSpec((R, GLA_QK), lambda n: (n_of(n), 0))
    o_spec = pl.BlockSpec((R, GLA_V), lambda n: (n_of(n), 0))
    s_spec = pl.BlockSpec((GLA_HEADS, None, GLA_SUB, GLA_DV, GLA_DK), lambda n: (0, n_of(n), 0, 0, 0))
    return q_spec, k_spec, v_spec, la_spec, o_spec, s_spec


def _gla_fwd(proj, la, *, name):
    M = proj.shape[0]
    N = M // GLA_STEP_ROWS
    C = GLA_CHUNK

    def body(q_ref, k_ref, v_ref, la_ref, o_ref, s_ref, state):
        n = pl.program_id(0)

        @pl.when(n == 0)
        def _():
            state[...] = jnp.zeros_like(state)

        local = {}

        def within(hh, c):
            kc = slice(hh * GLA_DK, (hh + 1) * GLA_DK)
            vc = slice(hh * GLA_DV, (hh + 1) * GLA_DV)
            rows = slice(c * C, (c + 1) * C)
            q = q_ref[rows, kc] * (GLA_DK ** -0.5)
            k = k_ref[rows, kc]
            v = v_ref[rows, vc]
            b = _gla_cumsum(la_ref[rows, kc])
            yield
            blast = b[C - 1:C, :]
            sc_t = yield from _gla_scores_t(q, k, b)
            kv = _dot(v, k * jnp.exp(blast - b), TN)
            o2 = _dot(sc_t, v, TN)
            yield
            local[hh, c] = (q * jnp.exp(b), jnp.exp(blast), kv, o2)

        def across(hh):
            vc = slice(hh * GLA_DV, (hh + 1) * GLA_DV)
            st = state[hh]
            for c in range(GLA_SUB):
                qe, eblast, kv, o2 = local[hh, c]
                s_ref[hh, c] = st
                o1 = _dot(qe, st, NT)
                yield
                o_ref[c * C:(c + 1) * C, vc] = o1 + o2
                st = st * eblast + kv
            state[hh] = st

        _round_robin(within(hh, c) for c in range(GLA_SUB) for hh in range(GLA_HEADS))
        _round_robin(across(hh) for hh in range(GLA_HEADS))

    q_spec, k_spec, v_spec, la_spec, o_spec, s_spec = _gla_specs(lambda n: n)
    return pl.pallas_call(
        body, name=name, grid=(N,),
        in_specs=[q_spec, k_spec, v_spec, la_spec], out_specs=[o_spec, s_spec],
        out_shape=[jax.ShapeDtypeStruct((M, GLA_V), F32),
                   jax.ShapeDtypeStruct((GLA_HEADS, N, GLA_SUB, GLA_DV, GLA_DK), F32)],
        scratch_shapes=[pltpu.VMEM((GLA_HEADS, GLA_DV, GLA_DK), F32)],
        compiler_params=_params("arbitrary"),
    )(proj, proj, proj, la)


def _gla_bwd(proj, la, do, s_all, d_proj, *, name):
    M = proj.shape[0]
    N = M // GLA_STEP_ROWS
    C = GLA_CHUNK
    qkv_w = 2 * GLA_QK + GLA_V
    assert C_GK == C_GQ + GLA_QK and C_GV == C_GK + GLA_QK and C_GQ % qkv_w == 0

    def body(q_ref, k_ref, v_ref, la_ref, do_ref, s_ref, _, dp_ref, dla_ref, dstate):
        n = pl.program_id(0)

        @pl.when(n == 0)
        def _():
            dstate[...] = jnp.zeros_like(dstate)

        H = GLA_HALF
        lane = lax.broadcasted_iota(jnp.int32, (C, C), 1)
        row = lax.broadcasted_iota(jnp.int32, (C, C), 0)
        ri = lax.broadcasted_iota(jnp.int32, (C, GLA_DK), 0)
        lane_h = lax.broadcasted_iota(jnp.int32, (H, C), 1)
        ri_h = lax.broadcasted_iota(jnp.int32, (H, GLA_DK), 0)
        cross = (row < H) & (lane >= H)
        upper = (row <= lane).astype(F32)
        def head(hh):
            kc = slice(hh * GLA_DK, (hh + 1) * GLA_DK)
            vc = slice(hh * GLA_DV, (hh + 1) * GLA_DV)
            ds1 = dstate[hh]
            for c in reversed(range(GLA_SUB)):
                rows = slice(c * C, (c + 1) * C)
                q = q_ref[rows, kc] * (GLA_DK ** -0.5)
                k = k_ref[rows, kc]
                v = v_ref[rows, vc]
                b = _gla_cumsum(la_ref[rows, kc])
                do_ = do_ref[rows, vc]
                st = s_ref[hh, c]
                dsc_t = _dot(v, do_, NT)
                dqe = _dot(do_, st)
                dke = _dot(v, ds1)
                yield
                blast = b[C - 1:C, :]
                eb = jnp.exp(b)
                elast = jnp.exp(blast - b)
                eblast = jnp.exp(blast)
                qe = q * eb
                ke = k * elast
                dv2 = _dot(ke, ds1, NT)
                ds_new = _dot(do_, qe, TN)
                deblast = jnp.sum(st * ds1, axis=0, keepdims=True)
                sc_halves, dq_halves, dk_halves = [], [], []
                for h0 in (0, H):
                    qh, kh, bh, dsch = q[h0:h0 + H], k[h0:h0 + H], b[h0:h0 + H], dsc_t[h0:h0 + H]
                    sch = jnp.zeros((H, C), F32)
                    dqh = jnp.zeros((H, GLA_DK), F32)
                    dkh = jnp.zeros((H, GLA_DK), F32)
                    for ii in range(H):
                        f = _gla_half_decay(bh, ii)
                        kf = kh * f
                        si = jnp.sum(qh[ii:ii + 1, :] * kf, axis=1, keepdims=True)
                        sch = jnp.where(lane_h == h0 + ii, si, sch)
                        dsi = jnp.sum(jnp.where(lane_h == h0 + ii, dsch, 0.0), axis=1, keepdims=True)
                        dqh = jnp.where(ri_h == ii, jnp.sum(dsi * kf, axis=0, keepdims=True), dqh)
                        dkh = dkh + (dsi * f) * qh[ii:ii + 1, :]
                        if ii % 4 == 3:
                            yield
                    sc_halves.append(sch)
                    dq_halves.append(dqh)
                    dk_halves.append(dkh)
                late, early = _gla_cross_factors(b)
                q_late, k_early = q * late, k * early
                dsc_x = jnp.where(cross, dsc_t, 0.0)
                sc_t = jnp.concatenate(sc_halves, axis=0) + _dot(k_early, q_late, NT)
                dq_sc = jnp.concatenate(dq_halves, axis=0) + _dot(dsc_x, k_early, TN) * late
                dk_sc = jnp.concatenate(dk_halves, axis=0) + _dot(dsc_x, q_late) * early
                yield
                dv1 = _dot(sc_t, do_)
                dp_ref[rows, kc] = ((dq_sc + dqe * eb) * (GLA_DK ** -0.5)).astype(dp_ref.dtype)
                dp_ref[rows, GLA_QK + hh * GLA_DK:GLA_QK + (hh + 1) * GLA_DK] = (dk_sc + dke * elast).astype(dp_ref.dtype)
                t_ke = dke * ke
                db = q * dq_sc - k * dk_sc + dqe * qe - t_ke
                db = db + jnp.where(ri == C - 1, jnp.sum(t_ke, axis=0, keepdims=True) + deblast * eblast, 0.0)
                dla = _dotx(upper, db)
                yield
                dp_ref[rows, 2 * GLA_QK + hh * GLA_DV:2 * GLA_QK + (hh + 1) * GLA_DV] = (dv1 + dv2).astype(dp_ref.dtype)
                dla_ref[rows, kc] = dla
                ds1 = ds1 * eblast + ds_new
            dstate[hh] = ds1

        _round_robin(head(hh) for hh in range(GLA_HEADS))

    rev = lambda n: N - 1 - n
    q_spec, k_spec, v_spec, la_spec, o_spec, s_spec = _gla_specs(rev)
    return pl.pallas_call(
        body, name=name, grid=(N,),
        in_specs=[q_spec, k_spec, v_spec, la_spec, o_spec, s_spec, _ANY],
        out_specs=[pl.BlockSpec((GLA_STEP_ROWS, qkv_w), lambda n: (rev(n), C_GQ // qkv_w)), la_spec],
        out_shape=[jax.ShapeDtypeStruct(d_proj.shape, d_proj.dtype), jax.ShapeDtypeStruct((M, GLA_QK), F32)],
        input_output_aliases={6: 0},
        scratch_shapes=[pltpu.VMEM((GLA_HEADS, GLA_DV, GLA_DK), F32)],
        compiler_params=_params("arbitrary"),
    )(proj, proj, proj, la, do, s_all, d_proj)


def _head_norm(o, wn):
    r = lax.rsqrt(jnp.mean(o * o, axis=-1, keepdims=True) + NORM_EPS)
    return o * r, r


def _mix_heads():
    heads = [(0, GDN_DV, hh * GDN_DV, hh * GDN_DV) for hh in range(GDN_HEADS)]
    heads += [(1, GLA_DV, GDN_V + hh * GLA_DV, hh * GLA_DV) for hh in range(GLA_HEADS)]
    return heads


def _mix_fwd(o_gdn, o_gla, proj, wn_gdn, wn_gla, *, name):
    M = proj.shape[0]
    tm = _tile(M, 344, 16)

    def body(og_ref, ol_ref, z_ref, r_ref, wg_ref, wl_ref, m_ref):
        srcs = ((og_ref, z_ref, wg_ref), (ol_ref, r_ref, wl_ref))
        for grp, width, mcol, col in _mix_heads():
            o_ref, gate_ref, w_ref = srcs[grp]
            xhat, _ = _head_norm(o_ref[:, col:col + width], None)
            gate, _ = _silu_and_grad(gate_ref[:, col:col + width])
            m_ref[:, mcol:mcol + width] = (xhat * w_ref[...] * gate).astype(m_ref.dtype)

    full = lambda s: pl.BlockSpec(s, lambda i: (0, 0))
    return pl.pallas_call(
        body, name=name, grid=(M // tm,),
        in_specs=[pl.BlockSpec((tm, GDN_V), lambda i: (i, 0)), pl.BlockSpec((tm, GLA_V), lambda i: (i, 0)),
                  pl.BlockSpec((tm, GDN_V), lambda i: (i, C_Z // GDN_V)),
                  pl.BlockSpec((tm, GLA_V), lambda i: (i, C_GR // GLA_V)),
                  full((1, GDN_DV)), full((1, GLA_DV))],
        out_specs=pl.BlockSpec((tm, D_MODEL), lambda i: (i, 0)),
        out_shape=jax.ShapeDtypeStruct((M, D_MODEL), BF16),
        compiler_params=_params("parallel"),
    )(o_gdn, o_gla, proj, proj, wn_gdn, wn_gla)


def _mix_bwd(o_gdn, o_gla, proj, wn_gdn, wn_gla, dmixed, *, name):
    M = proj.shape[0]
    tm = _tile(M, 344, 16)
    g_ = M // tm
    assert C_Z == 0 and C_GR == GDN_V

    def body(og_ref, ol_ref, z_ref, r_ref, wg_ref, wl_ref, dm_ref,
             dog_ref, dol_ref, dzr_ref, dwg_ref, dwl_ref):
        i = pl.program_id(0)
        srcs = ((og_ref, z_ref, wg_ref, dog_ref), (ol_ref, r_ref, wl_ref, dol_ref))
        dws = [jnp.zeros((1, GDN_DV), F32), jnp.zeros((1, GLA_DV), F32)]
        for grp, width, mcol, col in _mix_heads():
            o_ref, gate_ref, w_ref, do_ref = srcs[grp]
            cols = slice(col, col + width)
            xhat, r = _head_norm(o_ref[:, cols], None)
            gate, dgate_dc = _silu_and_grad(gate_ref[:, cols])
            dm = dm_ref[:, mcol:mcol + width]
            dzr_ref[:, mcol:mcol + width] = (dm * xhat * w_ref[...] * dgate_dc).astype(dzr_ref.dtype)
            dnorm = dm * gate
            dws[grp] = dws[grp] + jnp.sum(dnorm * xhat, axis=0, keepdims=True)
            dxhat = dnorm * w_ref[...]
            do_ref[:, cols] = r * (dxhat - xhat * jnp.mean(dxhat * xhat, axis=-1, keepdims=True))

        @pl.when(i == 0)
        def _():
            dwg_ref[...] = dws[0]
            dwl_ref[...] = dws[1]

        @pl.when(i > 0)
        def _():
            dwg_ref[...] += dws[0]
            dwl_ref[...] += dws[1]

    full = lambda s: pl.BlockSpec(s, lambda i: (0, 0))
    half = pl.BlockSpec((tm, GDN_V), lambda i: (i, 0))
    return pl.pallas_call(
        body, name=name, grid=(g_,),
        in_specs=[half, half, pl.BlockSpec((tm, GDN_V), lambda i: (i, C_Z // GDN_V)),
                  pl.BlockSpec((tm, GLA_V), lambda i: (i, C_GR // GLA_V)),
                  full((1, GDN_DV)), full((1, GLA_DV)), pl.BlockSpec((tm, D_MODEL), lambda i: (i, 0))],
        out_specs=[half, half, pl.BlockSpec((tm, GDN_V + GLA_V), lambda i: (i, 0)),
                   full((1, GDN_DV)), full((1, GLA_DV))],
        out_shape=[jax.ShapeDtypeStruct((M, GDN_V), F32), jax.ShapeDtypeStruct((M, GLA_V), F32),
                   jax.ShapeDtypeStruct((M, D_PROJ), BF16),
                   jax.ShapeDtypeStruct((1, GDN_DV), F32), jax.ShapeDtypeStruct((1, GLA_DV), F32)],
        compiler_params=_params("arbitrary"),
    )(o_gdn, o_gla, proj, proj, wn_gdn, wn_gla, dmixed)


def _swiglu_fwd(n, w_gate_t, w_up_t, *, name, tm=1376, tn=512):
    M, D = n.shape
    F = w_gate_t.shape[0]
    tm, tn = _tile(M, tm, 16), _tile(F, tn, 128)

    def body(n_ref, wg_ref, wu_ref, g_ref, u_ref, a_ref):
        x = n_ref[...]
        g = _dot(x, wg_ref[...], NT)
        u = _dot(x, wu_ref[...], NT)
        s, _ = _silu_and_grad(g)
        g_ref[...] = g.astype(g_ref.dtype)
        u_ref[...] = u.astype(u_ref.dtype)
        a_ref[...] = (s * u).astype(a_ref.dtype)

    w_spec = pl.BlockSpec((tn, D), lambda i, j: (j, 0))
    o_spec = pl.BlockSpec((tm, tn), lambda i, j: (i, j))
    return pl.pallas_call(
        body, name=name, grid=(M // tm, F // tn),
        in_specs=[pl.BlockSpec((tm, D), lambda i, j: (i, 0)), w_spec, w_spec], out_specs=[o_spec] * 3,
        out_shape=[jax.ShapeDtypeStruct((M, F), BF16)] * 3, compiler_params=_params("parallel", "parallel"),
    )(n, w_gate_t, w_up_t)


def _swiglu_bwd(dh, w_down, gate, up, *, name, after=None, tm=1376, tn=512):
    M, D = dh.shape
    F = w_down.shape[0]
    tm, tn = _tile(M, tm, 16), _tile(F, tn, 128)
    n_after = 0 if after is None else 1

    def body(*refs):
        dh_ref, w_ref, g_ref, u_ref, dg_ref, du_ref = refs[n_after:]
        da = _dot(dh_ref[...], w_ref[...], NT)
        s, ds = _silu_and_grad(g_ref[...].astype(F32))
        dg_ref[...] = (da * u_ref[...].astype(F32) * ds).astype(dg_ref.dtype)
        du_ref[...] = (da * s).astype(du_ref.dtype)

    o_spec = pl.BlockSpec((tm, tn), lambda i, j: (i, j))
    return pl.pallas_call(
        body, name=name, grid=(M // tm, F // tn),
        in_specs=[_ANY] * n_after + [pl.BlockSpec((tm, D), lambda i, j: (i, 0)),
                                     pl.BlockSpec((tn, D), lambda i, j: (j, 0)), o_spec, o_spec],
        out_specs=[o_spec, o_spec], out_shape=[jax.ShapeDtypeStruct((M, F), BF16)] * 2,
        compiler_params=_params("parallel", "parallel"),
    )(*((after,) if n_after else ()), dh, w_down, gate, up)


def _adamw(w, g, m, v, *, name):
    shape = w.shape
    cols = shape[-1]
    rows = w.size // cols
    w2, g2, m2, v2 = (t.reshape(rows, cols) for t in (w, g, m, v))
    if rows % 8 == 0 or cols % 128 != 0:
        tr, tc = (_tile(rows, 256, 8) if rows % 8 == 0 else rows), cols
    else:
        tr, tc = rows, _tile(cols, 256, 128)

    def body(w_ref, g_ref, m_ref, v_ref, d_ref, nm_ref, nv_ref):
        g_ = g_ref[...]
        nm = ADAM_B1 * m_ref[...] + (1.0 - ADAM_B1) * g_
        nv = ADAM_B2 * v_ref[...] + (1.0 - ADAM_B2) * (g_ * g_)
        m_hat = nm / (1.0 - ADAM_B1 ** ADAM_STEP)
        v_hat = nv / (1.0 - ADAM_B2 ** ADAM_STEP)
        d_ref[...] = -ADAM_LR * (m_hat / (jnp.sqrt(v_hat) + ADAM_EPS) + ADAM_WD * w_ref[...])
        nm_ref[...] = nm
        nv_ref[...] = nv

    blk = pl.BlockSpec((tr, tc), lambda i, j: (i, j))
    outs = pl.pallas_call(
        body, name=name, grid=(rows // tr, cols // tc), in_specs=[blk] * 4, out_specs=[blk] * 3,
        out_shape=[jax.ShapeDtypeStruct((rows, cols), F32)] * 3, compiler_params=_params("parallel", "parallel"),
    )(w2, g2, m2, v2)
    return tuple(t.reshape(shape) for t in outs)


def _sum_slabs(x, *, name):
    _, R, C = x.shape
    sub = 16 if x.dtype == BF16 else 8
    if R % sub == 0:
        tr, tc = _tile(R, 128, sub), C
    else:
        tr, tc = R, _tile(C, 256, 128)

    def body(x_ref, o_ref):
        acc = x_ref[0].astype(F32)
        for s in range(1, N_DEV):
            acc = acc + x_ref[s].astype(F32)
        o_ref[...] = acc

    return pl.pallas_call(
        body, name=name, grid=(R // tr, C // tc),
        in_specs=[pl.BlockSpec((N_DEV, tr, tc), lambda i, j: (0, i, j))],
        out_specs=pl.BlockSpec((tr, tc), lambda i, j: (i, j)),
        out_shape=jax.ShapeDtypeStruct((R, C), F32), compiler_params=_params("parallel", "parallel"),
    )(x)


def _peers():
    x, y, c = lax.axis_index("x"), lax.axis_index("y"), lax.axis_index("c")
    me = 4 * x + 2 * y + c
    peers = []
    for k in range(1, N_DEV):
        px = 1 - x if k & 4 else x
        py = 1 - y if k & 2 else y
        pc = 1 - c if k & 1 else c
        peers.append(((px, py, pc), 4 * px + 2 * py + pc))
    return me, peers


def _exchange(x, *, gather, name):
    slab = x.shape if gather else x.shape[1:]

    def body(x_ref, o_ref, send_sems, recv_sems, own_sem):
        me, peers = _peers()
        own = pltpu.make_async_copy(x_ref if gather else x_ref.at[me], o_ref.at[me], own_sem)
        own.start()
        sends, recvs = [], []
        for k, (pos, idx) in enumerate(peers):
            sends.append(pltpu.make_async_remote_copy(
                src_ref=x_ref if gather else x_ref.at[idx], dst_ref=o_ref.at[me],
                send_sem=send_sems.at[k], recv_sem=recv_sems.at[k],
                device_id=pos, device_id_type=pl.DeviceIdType.MESH))
            recvs.append(pltpu.make_async_remote_copy(
                src_ref=x_ref if gather else x_ref.at[idx], dst_ref=o_ref.at[idx],
                send_sem=send_sems.at[k], recv_sem=recv_sems.at[k],
                device_id=pos, device_id_type=pl.DeviceIdType.MESH))
        for cp in sends:
            cp.start()
        for cp in recvs:
            cp.wait_recv()
        for cp in sends:
            cp.wait_send()
        own.wait()

    hbm = pl.BlockSpec(memory_space=pltpu.HBM)
    return pl.pallas_call(
        body, name=name, in_specs=[hbm], out_specs=hbm,
        out_shape=jax.ShapeDtypeStruct((N_DEV,) + tuple(slab), x.dtype),
        scratch_shapes=[pltpu.SemaphoreType.DMA((N_DEV - 1,)), pltpu.SemaphoreType.DMA((N_DEV - 1,)),
                        pltpu.SemaphoreType.DMA],
    )(x)


_HBM = pl.BlockSpec(memory_space=pltpu.HBM)
_SEM = pl.BlockSpec(memory_space=pltpu.SEMAPHORE)
_EFFECT = pltpu.SideEffectType.DATAFLOW_SIDE_EFFECTING


PLAN_GATHER = tuple((k, "x", 0) for k in range(1, N_DEV))
PLAN_SCATTER = tuple((k, "xk", 0) for k in range(1, N_DEV))
PLAN_GATHER_CHIPS = tuple((k, "x", 0) for k in (1, 2, 4, 6))
PLAN_GATHER_PASS_ON = tuple((1, ("land", q), q) for q in (2, 4, 6))


def _plan_refs(plan, j, x_ref, land_ref, me, peers, receiving):
    k, source, r = plan[j]
    index_of = lambda q: me if q == 0 else peers[q - 1][1]
    pos, target = peers[k - 1]
    if source == "x":
        src = x_ref
    elif source == "xk":
        src = x_ref.at[target]
    else:
        src = land_ref.at[index_of(source[1])]
    return pos, src, land_ref.at[index_of(k ^ r) if receiving else index_of(r)]


def _exchange_start(x, *, plan, name, after=None, land=None, slab=None):
    n_after = 0 if after is None else 1
    n = len(plan)

    def body(*refs):
        x_ref, land_ref, send_sems, recv_sems, _, _, token = refs[n_after:]
        me, peers = _peers()
        for j in range(n):
            pos, src, dst = _plan_refs(plan, j, x_ref, land_ref, me, peers, receiving=False)
            pltpu.make_async_remote_copy(src_ref=src, dst_ref=dst, send_sem=send_sems.at[j], recv_sem=recv_sems.at[j],
                                         device_id=pos, device_id_type=pl.DeviceIdType.MESH).start()
        token[...] = jnp.zeros_like(token)

    if land is None:
        land = lax.empty((N_DEV,) + tuple(slab), x.dtype)
    return pl.pallas_call(
        body, name=name,
        out_shape=(pltpu.SemaphoreType.DMA((n,)), pltpu.SemaphoreType.DMA((n,)),
                   pltpu.HBM(x.shape, x.dtype), pltpu.HBM(land.shape, land.dtype), jax.ShapeDtypeStruct((8, 128), F32)),
        in_specs=[_ANY] * n_after + [_HBM, _HBM],
        out_specs=(_SEM, _SEM, _HBM, _HBM, pl.BlockSpec(memory_space=pltpu.VMEM)),
        input_output_aliases={n_after: 2, n_after + 1: 3},
        compiler_params=pltpu.CompilerParams(has_side_effects=_EFFECT),
    )(*((after,) if n_after else ()), pltpu.with_memory_space_constraint(x, pltpu.HBM),
      pltpu.with_memory_space_constraint(land, pltpu.HBM))


def _exchange_wait(handle, after, *, plan, name):
    send_sems, recv_sems, x_thru, land_thru, _ = handle
    afters = list(after) if isinstance(after, (list, tuple)) else [after]

    def body(x_ref, land_ref, send_sems, recv_sems, *rest):
        me, peers = _peers()
        for j in range(len(plan)):
            pos, src, dst = _plan_refs(plan, j, x_ref, land_ref, me, peers, receiving=True)
            cp = pltpu.make_async_remote_copy(src_ref=src, dst_ref=dst, send_sem=send_sems.at[j], recv_sem=recv_sems.at[j],
                                              device_id=pos, device_id_type=pl.DeviceIdType.MESH)
            cp.wait_send()
            cp.wait_recv()

    return pl.pallas_call(
        body, name=name,
        out_shape=(pltpu.HBM(x_thru.shape, x_thru.dtype), pltpu.HBM(land_thru.shape, land_thru.dtype)),
        in_specs=[_HBM, _HBM, _SEM, _SEM] + [_ANY] * len(afters), out_specs=(_HBM, _HBM),
        input_output_aliases={0: 0, 1: 1}, compiler_params=pltpu.CompilerParams(has_side_effects=_EFFECT),
    )(x_thru, land_thru, send_sems, recv_sems, *afters)


W_IN_SLAB = D_IN // N_DEV


def _to_proj_rows(t):
    z = jnp.zeros((D_PROJ - C_SM - 2 * GDN_HEADS - GLA_RANK,) + t.shape[1:], t.dtype)
    return jnp.concatenate([t[R_Z:R_A], t[R_GR:R_LR], t[R_GQ:R_GR], t[:R_Z], t[R_A:R_GQ], t[R_LR:], z], axis=0)


def _from_proj_rows(t):
    ab = C_SM + 2 * GDN_HEADS
    return jnp.concatenate([t[C_QKV:C_SM], t[C_Z:C_GR], t[C_SM:ab], t[C_GQ:C_QKV], t[C_GR:C_GQ],
                            t[ab:ab + GLA_RANK]], axis=0)


def _local_step(x, target, meta, attn_nw, conv_w, a_log, dt_bias, gdn_nw, w2, b2, gla_nw, ffn_nw, final_nw,
                fetch, emit, start=None):
    S = x.shape[0]
    h0 = jnp.concatenate([jnp.zeros((ROW_PAD, D_MODEL), F32), meta, x], axis=0)
    target_p = jnp.concatenate([jnp.zeros((HEAD_ROWS, D_MODEL), F32), target], axis=0)
    conv_w8 = jnp.concatenate([conv_w, jnp.zeros((8 - CONV_K, conv_w.shape[1]), F32)], axis=0)
    w2p = jnp.zeros((SM_W, GLA_QK), F32).at[2 * GDN_HEADS:2 * GDN_HEADS + GLA_RANK].set(w2)
    alog_p = jnp.zeros((1, SM_W), F32).at[:, :GDN_HEADS].set(a_log)
    dt_p = jnp.zeros((1, SM_W), F32).at[:, :GDN_HEADS].set(dt_bias)

    n1 = _rmsnorm_fwd(h0, attn_nw, name="attn_norm", after=start)
    w_in_t = fetch("w_in_t", (n1, target_p, conv_w8, w2p, alog_p, dt_p))
    proj = _matmul(n1, w_in_t, mode="nt", name="in_proj")
    gb, la = _gates_fwd(proj, w2p, b2, alog_p, dt_p, name="gates")
    act = _prep_fwd(proj, conv_w8, name="gdn_prep")
    o_gdn, s_gdn, t_gdn = _gdn_fwd(act, gb, name="gdn_fwd")
    o_gla, s_gla = _gla_fwd(proj, la, name="gla_fwd")
    mixed = _mix_fwd(o_gdn, o_gla, proj, gdn_nw, gla_nw, name="mix")
    w_out = fetch("w_out", mixed)
    h1 = _matmul(mixed, w_out, mode="nn", add=h0, name="out_proj")
    n2 = _rmsnorm_fwd(h1, ffn_nw, name="ffn_norm")
    w_gate_t, w_up_t = fetch("w_gate_t", n2), fetch("w_up_t", n2)
    gate, up, hid = _swiglu_fwd(n2, w_gate_t, w_up_t, name="swiglu")
    w_down = fetch("w_down", hid)
    h2 = _matmul(hid, w_down, mode="nn", add=h1, name="ffn_down", tm=688, tk=D_FF)
    dh2, dh2_b, d_final_nw, loss = _loss_head(h2, final_nw, target_p, name="loss_head")

    wg = dict(mode="tn", out_dtype=BF16, tn=512, tk=S + HEAD_ROWS)
    tok = emit("w_down", _matmul(hid, dh2_b, name="d_w_down", tm=704, **wg))
    d_gate, d_up = _swiglu_bwd(dh2_b, w_down, gate, up, name="d_swiglu", after=tok)
    tok = emit("w_gate_t", _matmul(d_gate, n2, name="d_w_gate", tm=704, **wg))
    tok = emit("w_up_t", _matmul(d_up, n2, name="d_w_up", tm=704, after=tok, **wg))
    d_n2 = _matmul_pair(d_gate, w_gate_t, d_up, w_up_t, name="d_n2", after=tok)
    dh1, dh1_b, d_ffn_nw = _rmsnorm_bwd(h1, ffn_nw, d_n2, dh2, name="d_ffn_norm", also_bf16=True)

    tok = emit("w_out", _matmul(mixed, dh1_b, name="d_w_out", tm=512, **wg))
    d_mixed = _matmul(dh1_b, w_out, mode="nt", name="d_mixed", after=tok)
    do_gdn, do_gla, d_proj, d_gdn_nw, d_gla_nw = _mix_bwd(o_gdn, o_gla, proj, gdn_nw, gla_nw, d_mixed, name="d_mix")
    d_proj, d_la = _gla_bwd(proj, la, do_gla, s_gla, d_proj, name="gla_bwd")
    dact, dgb_heads = _gdn_bwd(act, gb, do_gdn, s_gdn, t_gdn, name="gdn_bwd")
    d_proj, d_w2p, d_b2, d_alog, d_dt = _gates_bwd(proj, w2p, b2, alog_p, dt_p, dgb_heads, d_la, d_proj, name="d_gates")
    d_proj, d_conv_w8 = _prep_bwd(proj, conv_w8, dact, d_proj, name="d_gdn_prep")
    tok = emit("w_in_t", _matmul(d_proj, n1, name="d_w_in", tm=768, **wg))
    d_n1 = _matmul(d_proj, w_in_t, mode="nn", name="d_n1", tm=688, tk=D_PROJ, after=tok)
    dh0, d_attn_nw = _rmsnorm_bwd(h0, attn_nw, d_n1, dh1, name="d_attn_norm", also_bf16=False)

    return dict(
        loss=loss[0, 0], grad_x=dh0[HEAD_ROWS:], meta=dh0[ROW_PAD:HEAD_ROWS], attn_nw=d_attn_nw,
        conv_w=d_conv_w8[:CONV_K], a_log=d_alog[:, :GDN_HEADS], dt_bias=d_dt[:, :GDN_HEADS], gdn_nw=d_gdn_nw,
        w2=d_w2p[2 * GDN_HEADS:2 * GDN_HEADS + GLA_RANK], b2=d_b2, gla_nw=d_gla_nw, ffn_nw=d_ffn_nw,
        final_nw=d_final_nw)


SMALL_ROWS = 32


def kernel(x, meta_tokens, attn_norm_w, w_in, gdn_conv_w, gdn_a_log, gdn_dt_bias, gdn_norm_w, gla_gate_w2, gla_gate_b, gla_norm_w, w_out, ffn_norm_w, w_gate, w_up, w_down, final_norm_w, loss_target, m_meta_tokens, m_attn_norm_w, m_w_in, m_gdn_conv_w, m_gdn_a_log, m_gdn_dt_bias, m_gdn_norm_w, m_gla_gate_w2, m_gla_gate_b, m_gla_norm_w, m_w_out, m_ffn_norm_w, m_w_gate, m_w_up, m_w_down, m_final_norm_w, v_meta_tokens, v_attn_norm_w, v_w_in, v_gdn_conv_w, v_gdn_a_log, v_gdn_dt_bias, v_gdn_norm_w, v_gla_gate_w2, v_gla_gate_b, v_gla_norm_w, v_w_out, v_ffn_norm_w, v_w_gate, v_w_up, v_w_down, v_final_norm_w):
    me = 4 * lax.axis_index("x") + 2 * lax.axis_index("y") + lax.axis_index("c")
    n_in, n_ff, n_out = D_IN // N_DEV, D_FF // N_DEV, D_MODEL // N_DEV

    n_conv = gdn_conv_w.shape[2]
    n_w2 = gla_gate_w2.shape[2]
    n_meta = meta_tokens.shape[1]
    small = jnp.zeros((40, n_conv), F32)
    small = small.at[0:N_META, :n_meta].set(meta_tokens)
    small = small.at[N_META:N_META + CONV_K, :].set(gdn_conv_w[0])
    small = small.at[24:24 + GLA_RANK, :n_w2].set(gla_gate_w2[0])
    small_all = _exchange(small, gather=True, name="gather_small")
    meta_f = small_all[:, 0:N_META, :n_meta].transpose(1, 0, 2).reshape(N_META, D_MODEL)
    conv_f = small_all[:, N_META:N_META + CONV_K, :].transpose(1, 0, 2).reshape(CONV_K, N_DEV * n_conv)
    w2_f = small_all[:, 24:24 + GLA_RANK, :n_w2].transpose(1, 0, 2).reshape(GLA_RANK, N_DEV * n_w2)

    w_in_slab = w_in[0].T.astype(BF16)
    in_h = _exchange_start(w_in_slab, plan=PLAN_GATHER_CHIPS, slab=w_in_slab.shape, name="gather_w_in_start")
    handles, tok = {}, in_h[4]
    for wname, slab in (("w_out", w_out[0]), ("w_gate_t", w_gate[0].T), ("w_up_t", w_up[0].T), ("w_down", w_down[0])):
        slab = slab.astype(BF16)
        handles[wname] = _exchange_start(slab, plan=PLAN_GATHER, slab=slab.shape, name="gather_" + wname + "_start", after=tok)
        tok = handles[wname][4]

    def fetch(name, after):
        if name == "w_in_t":
            own, got = _exchange_wait(in_h, after, plan=PLAN_GATHER_CHIPS, name="gather_w_in_wait")
            pass_h = _exchange_start(own, plan=PLAN_GATHER_PASS_ON, land=got, name="pass_w_in_start")
            own, got = _exchange_wait(pass_h, pass_h[4], plan=PLAN_GATHER_PASS_ON, name="pass_w_in_wait")
            got = lax.dynamic_update_index_in_dim(got, own, me, 0)
            return _to_proj_rows(got.reshape(D_IN, D_MODEL))
        own, got = _exchange_wait(handles[name], after, plan=PLAN_GATHER, name="gather_" + name + "_wait")
        got = lax.dynamic_update_index_in_dim(got, own, me, 0)
        return got.reshape(N_DEV * got.shape[1], D_MODEL)

    sent = {}

    def emit(name, grad):
        if name == "w_in_t":
            grad = _from_proj_rows(grad)
        parts = grad.reshape(N_DEV, grad.shape[0] // N_DEV, D_MODEL)
        sent[name] = _exchange_start(parts, plan=PLAN_SCATTER, slab=parts.shape[1:], name="scatter_" + name + "_start")
        return sent[name][4]

    g = _local_step(x[0], loss_target[0], meta_f, attn_norm_w, conv_f, gdn_a_log, gdn_dt_bias, gdn_norm_w, w2_f,
                    gla_gate_b, gla_norm_w, ffn_norm_w, final_norm_w.reshape(1, D_MODEL), fetch, emit, start=tok)

    def total(name, after):
        handle = sent[name]
        own, got = _exchange_wait(handle, after, plan=PLAN_SCATTER, name="scatter_" + name + "_wait")
        got = lax.dynamic_update_index_in_dim(got, lax.dynamic_index_in_dim(own, me, 0, keepdims=False), me, 0)
        return _sum_slabs(got, name="sum_" + name)

    grad_w_down = total("w_down", g["attn_nw"])[None]
    grad_w_gate = total("w_gate_t", grad_w_down)
    grad_w_up = total("w_up_t", grad_w_gate)
    grad_w_out = total("w_out", grad_w_up)[None]
    grad_w_in = total("w_in_t", grad_w_out)

    misc = jnp.concatenate([g["a_log"], g["dt_bias"], g["gdn_nw"], g["gla_nw"], g["b2"], g["loss"].reshape(1, 1)], axis=1)
    n_misc = misc.shape[1]
    misc = jnp.pad(misc, ((0, 0), (0, D_MODEL - n_misc)))
    rows = jnp.concatenate([g["attn_nw"], g["ffn_nw"], g["final_nw"], misc, g["meta"],
                            g["conv_w"].reshape(-1, D_MODEL), g["w2"].reshape(-1, D_MODEL)], axis=0)
    rows = jnp.pad(rows, ((0, SMALL_ROWS - rows.shape[0]), (0, 0)))
    tot = _sum_slabs(_exchange(rows, gather=True, name="gather_small_grads"), name="sum_small_grads")
    grad_attn_nw, grad_ffn_nw, grad_final_nw = tot[0:1], tot[1:2], tot[2]
    grad_a_log = tot[3:4, 0:8]
    grad_dt = tot[3:4, 8:16]
    grad_gdn_nw = tot[3:4, 16:16 + GDN_DV]
    grad_gla_nw = tot[3:4, 144:144 + GLA_DV]
    grad_b2 = tot[3:4, 400:400 + GLA_QK]
    loss = tot[3, n_misc - 1]
    r0 = 4 + N_META
    grad_meta = lax.dynamic_slice(tot[4:r0], (0, me * n_meta), (N_META, n_meta))
    r1 = r0 + CONV_K * N_DEV * n_conv // D_MODEL
    grad_conv = lax.dynamic_slice(tot[r0:r1].reshape(CONV_K, N_DEV * n_conv), (0, me * n_conv), (CONV_K, n_conv))[None]
    r2 = r1 + GLA_RANK * N_DEV * n_w2 // D_MODEL
    grad_w2 = lax.dynamic_slice(tot[r1:r2].reshape(GLA_RANK, N_DEV * n_w2), (0, me * n_w2), (GLA_RANK, n_w2))[None]

    weights = [meta_tokens, attn_norm_w, w_in, gdn_conv_w, gdn_a_log, gdn_dt_bias, gdn_norm_w, gla_gate_w2,
               gla_gate_b, gla_norm_w, w_out, ffn_norm_w, w_gate, w_up, w_down, final_norm_w]
    grads = [grad_meta, grad_attn_nw, grad_w_in, grad_conv, grad_a_log, grad_dt, grad_gdn_nw, grad_w2,
             grad_b2, grad_gla_nw, grad_w_out, grad_ffn_nw, grad_w_gate, grad_w_up, grad_w_down, grad_final_nw]
    ms = [m_meta_tokens, m_attn_norm_w, m_w_in, m_gdn_conv_w, m_gdn_a_log, m_gdn_dt_bias, m_gdn_norm_w,
          m_gla_gate_w2, m_gla_gate_b, m_gla_norm_w, m_w_out, m_ffn_norm_w, m_w_gate, m_w_up, m_w_down, m_final_norm_w]
    vs = [v_meta_tokens, v_attn_norm_w, v_w_in, v_gdn_conv_w, v_gdn_a_log, v_gdn_dt_bias, v_gdn_norm_w,
          v_gla_gate_w2, v_gla_gate_b, v_gla_norm_w, v_w_out, v_ffn_norm_w, v_w_gate, v_w_up, v_w_down, v_final_norm_w]
    transposed = (2, 12, 13)
    outs = [[], [], [], []]
    for idx, (w, gr, m, v) in enumerate(zip(weights, grads, ms, vs)):
        if idx in transposed:
            res = (gr,) + _adamw(w[0].T, gr, m[0].T, v[0].T, name=f"adamw_{idx}")
            res = [t.T[None] for t in res]
        else:
            gr = gr.reshape(w.shape)
            res = (gr,) + _adamw(w, gr, m, v, name=f"adamw_{idx}")
        for lst, t in zip(outs, res):
            lst.append(t)
    return (loss, g["grad_x"][None], *outs[0], *outs[1], *outs[2], *outs[3])
```

```python
import functools

import jax
import jax.numpy as jnp
from jax import lax
from jax.experimental import pallas as pl
from jax.experimental.pallas import tpu as pltpu

F32 = jnp.float32
BF16 = jnp.bfloat16
_MXU_DTYPE = jnp.bfloat16

D_MODEL = 2048
N_META = 16
ROW_PAD = 48
HEAD_ROWS = ROW_PAD + N_META
CONV_K = 4
GDN_HEADS, GDN_DK, GDN_DV, GDN_CHUNK = 8, 128, 128, 64
GLA_HEADS, GLA_DK, GLA_DV, GLA_CHUNK = 4, 128, 256, 16
GLA_RANK = 16
GLA_GATE_NORMALIZER = 16.0
GDN_QK = GDN_HEADS * GDN_DK
GDN_V = GDN_HEADS * GDN_DV
GLA_QK = GLA_HEADS * GLA_DK
GLA_V = GLA_HEADS * GLA_DV
D_FF = 5632
D_IN = 7200
NORM_EPS = 1e-6
C_Z, C_GR, C_GQ, C_GK, C_GV, C_QKV, C_SM = 0, 1024, 2048, 2560, 3072, 4096, 7168
SM_W = 128
D_PROJ = 7680
R_Z, R_A, R_B, R_GQ, R_GK, R_GV, R_GR, R_LR = 3072, 4096, 4104, 4112, 4624, 5136, 6160, 7184

ADAM_LR, ADAM_B1, ADAM_B2, ADAM_EPS, ADAM_WD, ADAM_STEP = 0.001, 0.9, 0.999, 1e-08, 0.01, 10

N_DEV = 8
VMEM_LIMIT = 56 * 1024 * 1024

NN = (((1,), (0,)), ((), ()))
NT = (((1,), (1,)), ((), ()))
TN = (((0,), (0,)), ((), ()))


def _dot(a, b, dims=NN):
    return lax.dot_general(a.astype(_MXU_DTYPE), b.astype(_MXU_DTYPE), dims, preferred_element_type=F32)


def _dotx(a, b, dims=NN):
    return lax.dot_general(a, b, dims, precision=lax.Precision.HIGHEST, preferred_element_type=F32)


def _dot3(a, b):
    ah = a.astype(BF16)
    al = (a - ah.astype(F32)).astype(BF16)
    bh = b.astype(BF16)
    bl = (b - bh.astype(F32)).astype(BF16)
    d = functools.partial(lax.dot_general, dimension_numbers=NN, preferred_element_type=F32)
    return d(ah, bh) + (d(ah, bl) + d(al, bh))


def _tile(n, target, mult=8):
    best = None
    for t in range(mult, min(n, target) + 1, mult):
        if n % t == 0:
            best = t
    return best if best is not None else n


def _params(*sem):
    return pltpu.CompilerParams(dimension_semantics=sem, vmem_limit_bytes=VMEM_LIMIT)


def _sigmoid(x):
    return 0.5 * jnp.tanh(0.5 * x) + 0.5


def _softplus(x):
    return jnp.maximum(x, 0.0) + jnp.log1p(jnp.exp(-jnp.abs(x)))


def _silu_and_grad(c):
    s = _sigmoid(c)
    return c * s, s * (1.0 + c * (1.0 - s))


_ANY = pl.BlockSpec(memory_space=pl.ANY)


def _matmul(a, b, *, mode, name, out_dtype=F32, add=None, after=None, tm=1376, tn=512, tk=2064):
    if mode == "tn":
        K, M = a.shape
        N = b.shape[1]
    else:
        M, K = a.shape
        N = b.shape[0] if mode == "nt" else b.shape[1]
    tm = _tile(M, tm, 128 if mode == "tn" else 16)
    tn = _tile(N, tn, 128)
    tk = _tile(K, tk, 16 if mode == "tn" else 128)
    gm, gn, gk = M // tm, N // tn, K // tk
    dims = {"nn": NN, "nt": NT, "tn": TN}[mode]

    n_after = 0 if after is None else 1

    def body(*refs):
        refs = refs[n_after:]
        if add is None:
            a_ref, b_ref, o_ref = refs[:3]
            add_ref = None
        else:
            a_ref, b_ref, add_ref, o_ref = refs[:4]
        p = _dot(a_ref[...], b_ref[...], dims)

        def finish(r):
            if add_ref is not None:
                r = r + add_ref[...]
            o_ref[...] = r.astype(out_dtype)

        if gk == 1:
            finish(p)
        else:
            acc_ref = refs[-1]
            k = pl.program_id(2)

            @pl.when(k == 0)
            def _():
                acc_ref[...] = p

            @pl.when(k > 0)
            def _():
                acc_ref[...] += p

            @pl.when(k == gk - 1)
            def _():
                finish(acc_ref[...])

    if mode == "tn":
        a_spec = pl.BlockSpec((tk, tm), lambda i, j, k: (k, i))
    else:
        a_spec = pl.BlockSpec((tm, tk), lambda i, j, k: (i, k))
    if mode == "nt":
        b_spec = pl.BlockSpec((tn, tk), lambda i, j, k: (j, k))
    else:
        b_spec = pl.BlockSpec((tk, tn), lambda i, j, k: (k, j))
    o_spec = pl.BlockSpec((tm, tn), lambda i, j, k: (i, j))
    in_specs = [_ANY] * n_after + [a_spec, b_spec] + ([o_spec] if add is not None else [])
    args = ((after,) if n_after else ()) + (a, b) + ((add,) if add is not None else ())
    return pl.pallas_call(
        body, name=name, grid=(gm, gn, gk), in_specs=in_specs, out_specs=o_spec,
        out_shape=jax.ShapeDtypeStruct((M, N), out_dtype),
        scratch_shapes=[pltpu.VMEM((tm, tn), F32)] if gk > 1 else [],
        compiler_params=_params("parallel", "parallel", "arbitrary"),
    )(*args)


def _matmul_pair(a1, b1, a2, b2, *, name, after=None, tm=688, tn=256):
    M, K = a1.shape
    N = b1.shape[1]
    tm, tn = _tile(M, tm, 16), _tile(N, tn, 128)
    n_after = 0 if after is None else 1

    def body(*refs):
        a1_ref, b1_ref, a2_ref, b2_ref, o_ref = refs[n_after:]
        o_ref[...] = _dot(a1_ref[...], b1_ref[...]) + _dot(a2_ref[...], b2_ref[...])

    a_spec = pl.BlockSpec((tm, K), lambda i, j: (i, 0))
    b_spec = pl.BlockSpec((K, tn), lambda i, j: (0, j))
    return pl.pallas_call(
        body, name=name, grid=(M // tm, N // tn), in_specs=[_ANY] * n_after + [a_spec, b_spec, a_spec, b_spec],
        out_specs=pl.BlockSpec((tm, tn), lambda i, j: (i, j)), out_shape=jax.ShapeDtypeStruct((M, N), F32),
        compiler_params=_params("parallel", "parallel"),
    )(*((after,) if n_after else ()), a1, b1, a2, b2)


def _rmsnorm_fwd(h, w, *, name, after=None):
    M, D = h.shape
    tm = _tile(M, 688, 16)
    n_after = 0 if after is None else 1

    def body(*refs):
        h_ref, w_ref, n_ref = refs[n_after:]
        x = h_ref[...]
        r = lax.rsqrt(jnp.mean(x * x, axis=-1, keepdims=True) + NORM_EPS)
        n_ref[...] = (x * r * w_ref[...]).astype(n_ref.dtype)

    return pl.pallas_call(
        body, name=name, grid=(M // tm,),
        in_specs=[_ANY] * n_after + [pl.BlockSpec((tm, D), lambda i: (i, 0)), pl.BlockSpec((1, D), lambda i: (0, 0))],
        out_specs=pl.BlockSpec((tm, D), lambda i: (i, 0)),
        out_shape=jax.ShapeDtypeStruct((M, D), BF16),
        compiler_params=_params("parallel"),
    )(*((after,) if n_after else ()), h, w)


SEQ_BLOCK = HEAD_ROWS


def _seq_blocks_per_tile(rows):
    n = rows // SEQ_BLOCK
    return max(m for m in (1, 2, 3, 4) if n % m == 0)


def _seq_specs(m, D):
    return [pl.BlockSpec((SEQ_BLOCK, D), functools.partial(lambda i, k: (jnp.maximum(m * i + k - 1, 0), 0), k=k))
            for k in range(m)]


def _embed_norm(head, x, w, *, name, after=None):
    S, D = x.shape
    m = _seq_blocks_per_tile(S + HEAD_ROWS)
    n_after = 0 if after is None else 1

    def body(*refs):
        refs = refs[n_after:]
        head_ref, x_refs, w_ref, h_ref, n_ref = refs[0], refs[1:1 + m], refs[1 + m], refs[2 + m], refs[3 + m]
        i = pl.program_id(0)
        for k in range(m):
            blk = x_refs[k][...]
            if k == 0:
                blk = jnp.where(i == 0, head_ref[...], blk)
            rows = slice(k * SEQ_BLOCK, (k + 1) * SEQ_BLOCK)
            h_ref[rows, :] = blk
            r = lax.rsqrt(jnp.mean(blk * blk, axis=-1, keepdims=True) + NORM_EPS)
            n_ref[rows, :] = (blk * r * w_ref[...]).astype(n_ref.dtype)

    tile = pl.BlockSpec((m * SEQ_BLOCK, D), lambda i: (i, 0))
    return pl.pallas_call(
        body, name=name, grid=((S + HEAD_ROWS) // (m * SEQ_BLOCK),),
        in_specs=[_ANY] * n_after + [pl.BlockSpec((SEQ_BLOCK, D), lambda i: (0, 0))] + _seq_specs(m, D)
        + [pl.BlockSpec((1, D), lambda i: (0, 0))],
        out_specs=[tile, tile],
        out_shape=[jax.ShapeDtypeStruct((S + HEAD_ROWS, D), F32), jax.ShapeDtypeStruct((S + HEAD_ROWS, D), BF16)],
        compiler_params=_params("parallel"),
    )(*((after,) if n_after else ()), head, *([x] * m), w)


def _embed_norm_bwd(h, w, dn, dres, *, name):
    M, D = h.shape
    S = M - HEAD_ROWS
    m = _seq_blocks_per_tile(S)
    g = S // (m * SEQ_BLOCK)

    def one(x, dn_, dres_, w_):
        r = lax.rsqrt(jnp.mean(x * x, axis=-1, keepdims=True) + NORM_EPS)
        xhat = x * r
        dxhat = dn_ * w_
        dh = dres_ + r * (dxhat - xhat * jnp.mean(dxhat * xhat, axis=-1, keepdims=True))
        return dh, jnp.sum((dn_ * xhat).reshape(SEQ_BLOCK // 8, 8, D), axis=0)

    def body(*refs):
        w_ref = refs[0]
        groups = [refs[1 + a * (m + 1):1 + (a + 1) * (m + 1)] for a in range(3)]
        gx_ref, dhead_ref, dw_ref, acc_ref = refs[1 + 3 * (m + 1):]
        i = pl.program_id(0)
        w_ = w_ref[...]
        part = jnp.zeros((8, D), F32)
        for k in range(m):
            dh, p = one(*(grp[1 + k][...] for grp in groups), w_)
            gx_ref[k * SEQ_BLOCK:(k + 1) * SEQ_BLOCK, :] = dh
            part = part + p

        @pl.when(i == 0)
        def _():
            dh, p = one(*(grp[0][...] for grp in groups), w_)
            dhead_ref[...] = dh
            acc_ref[...] = part + p

        @pl.when(i > 0)
        def _():
            acc_ref[...] += part

        @pl.when(i == g - 1)
        def _():
            dw_ref[...] = jnp.sum(acc_ref[...], axis=0, keepdims=True)

    first = pl.BlockSpec((SEQ_BLOCK, D), lambda i: (0, 0))
    blocks = [pl.BlockSpec((SEQ_BLOCK, D), functools.partial(lambda i, k: (m * i + k + 1, 0), k=k)) for k in range(m)]
    vec = pl.BlockSpec((1, D), lambda i: (0, 0))
    return pl.pallas_call(
        body, name=name, grid=(g,), in_specs=[vec] + ([first] + blocks) * 3,
        out_specs=[pl.BlockSpec((m * SEQ_BLOCK, D), lambda i: (i, 0)), first, vec],
        out_shape=[jax.ShapeDtypeStruct((S, D), F32), jax.ShapeDtypeStruct((SEQ_BLOCK, D), F32),
                   jax.ShapeDtypeStruct((1, D), F32)],
        scratch_shapes=[pltpu.VMEM((8, D), F32)],
        compiler_params=_params("arbitrary"),
    )(w, *([h] * (m + 1)), *([dn] * (m + 1)), *([dres] * (m + 1)))


def _rmsnorm_bwd(h, w, dn, dres, *, name, also_bf16):
    M, D = h.shape
    tm = _tile(M, 344, 16)
    g = M // tm

    def body(h_ref, w_ref, dn_ref, dres_ref, dh_ref, *rest):
        dhb_ref = rest[0] if also_bf16 else None
        dw_ref, acc_ref = rest[-2:]
        i = pl.program_id(0)
        x = h_ref[...]
        r = lax.rsqrt(jnp.mean(x * x, axis=-1, keepdims=True) + NORM_EPS)
        xhat = x * r
        dn_ = dn_ref[...]
        dxhat = dn_ * w_ref[...]
        dh = dres_ref[...] + r * (dxhat - xhat * jnp.mean(dxhat * xhat, axis=-1, keepdims=True))
        dh_ref[...] = dh
        if also_bf16:
            dhb_ref[...] = dh.astype(dhb_ref.dtype)
        part = jnp.sum((dn_ * xhat).reshape(tm // 8, 8, D), axis=0)

        @pl.when(i == 0)
        def _():
            acc_ref[...] = part

        @pl.when(i > 0)
        def _():
            acc_ref[...] += part

        @pl.when(i == g - 1)
        def _():
            dw_ref[...] = jnp.sum(acc_ref[...], axis=0, keepdims=True)

    row = pl.BlockSpec((tm, D), lambda i: (i, 0))
    vec = pl.BlockSpec((1, D), lambda i: (0, 0))
    return pl.pallas_call(
        body, name=name, grid=(g,), in_specs=[row, vec, row, row],
        out_specs=[row] + ([row] if also_bf16 else []) + [vec],
        out_shape=[jax.ShapeDtypeStruct((M, D), F32)] + ([jax.ShapeDtypeStruct((M, D), BF16)] if also_bf16 else [])
        + [jax.ShapeDtypeStruct((1, D), F32)],
        scratch_shapes=[pltpu.VMEM((8, D), F32)],
        compiler_params=_params("arbitrary"),
    )(h, w, dn, dres)


def _loss_head(h, w, target, *, name):
    M, D = h.shape
    m = _seq_blocks_per_tile(M)
    tm = m * SEQ_BLOCK
    g = M // tm

    def body(h_ref, w_ref, *rest):
        t_refs = rest[:m]
        dh_ref, dhb_ref, dw_ref, loss_ref, acc_ref, lacc_ref = rest[m:]
        i = pl.program_id(0)
        x = h_ref[...]
        row = i * tm + lax.broadcasted_iota(jnp.int32, (tm, 1), 0)
        live = row >= HEAD_ROWS
        r = lax.rsqrt(jnp.mean(x * x, axis=-1, keepdims=True) + NORM_EPS)
        xhat = x * r
        t = jnp.concatenate([t_ref[...] for t_ref in t_refs], axis=0)
        err = jnp.where(live, xhat * w_ref[...] - t, 0.0)
        dy = err * (1.0 / D)
        dxhat = dy * w_ref[...]
        dh = r * (dxhat - xhat * jnp.mean(dxhat * xhat, axis=-1, keepdims=True))
        dh_ref[...] = dh
        dhb_ref[...] = dh.astype(dhb_ref.dtype)
        part = jnp.sum((dy * xhat).reshape(tm // 8, 8, D), axis=0)
        lpart = jnp.sum((err * err).reshape(tm // 8, 8, D), axis=0)

        @pl.when(i == 0)
        def _():
            acc_ref[...] = part
            lacc_ref[...] = lpart

        @pl.when(i > 0)
        def _():
            acc_ref[...] += part
            lacc_ref[...] += lpart

        @pl.when(i == g - 1)
        def _():
            dw_ref[...] = jnp.sum(acc_ref[...], axis=0, keepdims=True)
            tot = jnp.sum(jnp.sum(lacc_ref[...], axis=0, keepdims=True), axis=1, keepdims=True)
            loss_ref[...] = jnp.broadcast_to(tot * (0.5 / D), (1, 128))

    row = pl.BlockSpec((tm, D), lambda i: (i, 0))
    vec = pl.BlockSpec((1, D), lambda i: (0, 0))
    return pl.pallas_call(
        body, name=name, grid=(g,), in_specs=[row, vec] + _seq_specs(m, D),
        out_specs=[row, row, vec, pl.BlockSpec((1, 128), lambda i: (0, 0))],
        out_shape=[jax.ShapeDtypeStruct((M, D), F32), jax.ShapeDtypeStruct((M, D), BF16),
                   jax.ShapeDtypeStruct((1, D), F32), jax.ShapeDtypeStruct((1, 128), F32)],
        scratch_shapes=[pltpu.VMEM((8, D), F32), pltpu.VMEM((8, D), F32)],
        compiler_params=_params("arbitrary"),
    )(h, w, *([target] * m))


def _gate_terms(sm, w2p, b2, alog_p, dt_p, row0):
    tm = sm.shape[0]
    lane = lax.broadcasted_iota(jnp.int32, (tm, SM_W), 1)
    live = (row0 + lax.broadcasted_iota(jnp.int32, (tm, 1), 0)) >= ROW_PAD
    pre = sm + dt_p
    neg_a = -jnp.exp(alog_p)
    g = neg_a * _softplus(pre)
    beta = _sigmoid(sm)
    z = _dot(sm, w2p) + b2
    return lane, live, pre, neg_a, g, beta, z


def _gates_fwd(proj, w2p, b2, alog_p, dt_p, *, name):
    M = proj.shape[0]
    tm = _tile(M, 688, 8)

    def body(sm_ref, w2_ref, b2_ref, al_ref, dt_ref, gb_ref, la_ref):
        row0 = pl.program_id(0) * tm
        lane, live, _, _, g, beta, z = _gate_terms(sm_ref[...], w2_ref[...], b2_ref[...], al_ref[...], dt_ref[...], row0)
        gb = jnp.where(lane < GDN_HEADS, g, jnp.where(lane < 2 * GDN_HEADS, beta, 0.0))
        gb_ref[...] = jnp.where(live, gb, 0.0)
        la = (jnp.minimum(z, 0.0) - jnp.log1p(jnp.exp(-jnp.abs(z)))) * (1.0 / GLA_GATE_NORMALIZER)
        la_ref[...] = jnp.where(live, la, 0.0)

    full = lambda s: pl.BlockSpec(s, lambda i: (0, 0))
    return pl.pallas_call(
        body, name=name, grid=(M // tm,),
        in_specs=[pl.BlockSpec((tm, SM_W), lambda i: (i, C_SM // SM_W)), full((SM_W, GLA_QK)), full((1, GLA_QK)),
                  full((1, SM_W)), full((1, SM_W))],
        out_specs=[pl.BlockSpec((tm, SM_W), lambda i: (i, 0)), pl.BlockSpec((tm, GLA_QK), lambda i: (i, 0))],
        out_shape=[jax.ShapeDtypeStruct((M, SM_W), F32), jax.ShapeDtypeStruct((M, GLA_QK), F32)],
        compiler_params=_params("parallel"),
    )(proj, w2p, b2, alog_p, dt_p)


def _gates_bwd(proj, w2p, b2, alog_p, dt_p, dgb_heads, dla, d_proj, *, name):
    M = proj.shape[0]
    tm = _tile(M, 688, 8)
    g_ = M // tm

    tail_w = D_PROJ - C_SM

    def body(sm_ref, w2_ref, b2_ref, al_ref, dt_ref, dgb_ref, dla_ref, _,
             dsm_ref, dw2_ref, db2_ref, dal_ref, ddt_ref):
        i = pl.program_id(0)
        sm = sm_ref[...]
        lane, live, pre, neg_a, g, beta, z = _gate_terms(sm, w2_ref[...], b2_ref[...], al_ref[...], dt_ref[...], i * tm)
        dz = jnp.where(live, dla_ref[...] * (_sigmoid(-z) * (1.0 / GLA_GATE_NORMALIZER)), 0.0)
        dsm_lr = _dot(dz, w2_ref[...], NT)
        dgb = dgb_ref[0]
        for hh in range(1, GDN_HEADS):
            dgb = dgb + dgb_ref[hh]
        dgb = jnp.where(live, dgb, 0.0)
        da = dgb * neg_a * _sigmoid(pre)
        db = dgb * beta * (1.0 - beta)
        dsm = jnp.where(lane < GDN_HEADS, da, jnp.where(lane < 2 * GDN_HEADS, db, dsm_lr))
        dsm_ref[:, 0:SM_W] = dsm.astype(dsm_ref.dtype)
        dsm_ref[:, SM_W:tail_w] = jnp.zeros((tm, tail_w - SM_W), dsm_ref.dtype)
        is_a = lane < GDN_HEADS
        dal = jnp.sum(jnp.where(is_a, dgb * g, 0.0), axis=0, keepdims=True)
        ddt = jnp.sum(jnp.where(is_a, da, 0.0), axis=0, keepdims=True)
        dw2 = _dot(sm, dz, TN)
        db2 = jnp.sum(dz, axis=0, keepdims=True)

        @pl.when(i == 0)
        def _():
            dw2_ref[...] = dw2
            db2_ref[...] = db2
            dal_ref[...] = dal
            ddt_ref[...] = ddt

        @pl.when(i > 0)
        def _():
            dw2_ref[...] += dw2
            db2_ref[...] += db2
            dal_ref[...] += dal
            ddt_ref[...] += ddt

    full = lambda s: pl.BlockSpec(s, lambda i: (0, 0))
    return pl.pallas_call(
        body, name=name, grid=(g_,),
        in_specs=[pl.BlockSpec((tm, SM_W), lambda i: (i, C_SM // SM_W)), full((SM_W, GLA_QK)), full((1, GLA_QK)),
                  full((1, SM_W)), full((1, SM_W)),
                  pl.BlockSpec((GDN_HEADS, tm, SM_W), lambda i: (0, i, 0)),
                  pl.BlockSpec((tm, GLA_QK), lambda i: (i, 0)), _ANY],
        out_specs=[pl.BlockSpec((tm, tail_w), lambda i: (i, C_SM // tail_w)), full((SM_W, GLA_QK)), full((1, GLA_QK)),
                   full((1, SM_W)), full((1, SM_W))],
        out_shape=[jax.ShapeDtypeStruct(d_proj.shape, d_proj.dtype), jax.ShapeDtypeStruct((SM_W, GLA_QK), F32),
                   jax.ShapeDtypeStruct((1, GLA_QK), F32), jax.ShapeDtypeStruct((1, SM_W), F32),
                   jax.ShapeDtypeStruct((1, SM_W), F32)],
        input_output_aliases={7: 0},
        compiler_params=_params("arbitrary"),
    )(proj, w2p, b2, alog_p, dt_p, dgb_heads, dla, d_proj)


QKV_W = GDN_QK
N_QKV_GROUPS = 3
QKV_B0 = C_QKV // QKV_W
HALO = 8


def _conv_terms(x_ref, halo_ref, cw_ref, xs_ref, i, tm):
    xs_ref[HALO:HALO + tm, :] = x_ref[...]
    xs_ref[0:HALO, :] = jnp.where(i > 0, halo_ref[...], 0.0)
    cw = cw_ref[...]
    xs = xs_ref[...]
    taps = [(pltpu.roll(xs, CONV_K - 1 - t, 0) if t < CONV_K - 1 else xs)[HALO:HALO + tm, :] for t in range(CONV_K)]
    c = taps[0] * cw[0:1, :]
    for t in range(1, CONV_K):
        c = c + taps[t] * cw[t:t + 1, :]
    return c, taps


def _prep_fwd(proj, conv_w8, *, name):
    M = proj.shape[0]
    tm = _tile(M, 344, 8)

    def body(x_ref, halo_ref, cw_ref, o_ref, xs_ref):
        j, i = pl.program_id(0), pl.program_id(1)
        c, _ = _conv_terms(x_ref, halo_ref, cw_ref, xs_ref, i, tm)
        s, _ = _silu_and_grad(c)
        scale = jnp.where(j == 0, GDN_DK ** -0.5, 1.0)
        for hh in range(GDN_HEADS):
            cols = slice(hh * 128, (hh + 1) * 128)
            sh = s[:, cols]
            r = lax.rsqrt(jnp.sum(sh * sh, axis=-1, keepdims=True) + NORM_EPS)
            o_ref[:, cols] = jnp.where(j < 2, sh * (r * scale), sh)

    hb = tm // HALO
    return pl.pallas_call(
        body, name=name, grid=(N_QKV_GROUPS, M // tm),
        in_specs=[pl.BlockSpec((tm, QKV_W), lambda j, i: (i, QKV_B0 + j)),
                  pl.BlockSpec((HALO, QKV_W), lambda j, i: (jnp.maximum(i * hb - 1, 0), QKV_B0 + j)),
                  pl.BlockSpec((8, QKV_W), lambda j, i: (0, j))],
        out_specs=pl.BlockSpec((tm, QKV_W), lambda j, i: (i, j)),
        out_shape=jax.ShapeDtypeStruct((M, N_QKV_GROUPS * QKV_W), F32),
        scratch_shapes=[pltpu.VMEM((tm + HALO, QKV_W), F32)],
        compiler_params=_params("parallel", "arbitrary"),
    )(proj, proj, conv_w8)


def _prep_bwd(proj, conv_w8, dact, d_proj, *, name):
    M = proj.shape[0]
    tm = _tile(M, 688, 16)
    g_ = M // tm
    ext = tm + HALO

    def body(x_ref, prev_ref, next_ref, cw_ref, da_ref, dan_ref, _, o_ref, dcw_ref, xs_ref, das_ref, dcs_ref):
        j, i = pl.program_id(0), pl.program_id(1)
        not_last = i < g_ - 1
        xs_ref[0:HALO, :] = jnp.where(i > 0, prev_ref[...], 0.0)
        xs_ref[HALO:HALO + tm, :] = x_ref[...]
        xs_ref[HALO + tm:HALO + ext, :] = jnp.where(not_last, next_ref[...], 0.0)
        das_ref[0:tm, :] = da_ref[...]
        das_ref[tm:ext, :] = jnp.where(not_last, dan_ref[...], 0.0)
        cw = cw_ref[...]
        xs = xs_ref[...]
        taps = [(pltpu.roll(xs, CONV_K - 1 - t, 0) if t < CONV_K - 1 else xs)[HALO:HALO + ext, :] for t in range(CONV_K)]
        c = taps[0] * cw[0:1, :]
        for t in range(1, CONV_K):
            c = c + taps[t] * cw[t:t + 1, :]
        s, ds_dc = _silu_and_grad(c)
        scale = jnp.where(j == 0, GDN_DK ** -0.5, 1.0)
        for hh in range(GDN_HEADS):
            cols = slice(hh * 128, (hh + 1) * 128)
            sh = s[:, cols]
            r = lax.rsqrt(jnp.sum(sh * sh, axis=-1, keepdims=True) + NORM_EPS)
            da = das_ref[:, cols]
            y = sh * r
            dy = da * scale
            ds_norm = r * (dy - y * jnp.sum(dy * y, axis=-1, keepdims=True))
            dcs_ref[:, cols] = jnp.where(j < 2, ds_norm, da) * ds_dc[:, cols]
        dc = dcs_ref[...]
        acc = dc[0:tm, :] * cw[CONV_K - 1:CONV_K, :]
        for t in range(CONV_K - 1):
            acc = acc + pltpu.roll(dc, ext - (CONV_K - 1 - t), 0)[0:tm, :] * cw[t:t + 1, :]
        o_ref[...] = acc.astype(o_ref.dtype)
        r8 = lax.broadcasted_iota(jnp.int32, (8, QKV_W), 0)
        part = jnp.zeros((8, QKV_W), F32)
        for t in range(CONV_K):
            part = jnp.where(r8 == t, jnp.sum(dc[0:tm, :] * taps[t][0:tm, :], axis=0, keepdims=True), part)

        @pl.when(i == 0)
        def _():
            dcw_ref[...] = part

        @pl.when(i > 0)
        def _():
            dcw_ref[...] += part

    hb = tm // HALO
    last = M // HALO - 1
    prev_of = lambda i: jnp.maximum(i * hb - 1, 0)
    next_of = lambda i: jnp.minimum((i + 1) * hb, last)
    return pl.pallas_call(
        body, name=name, grid=(N_QKV_GROUPS, g_),
        in_specs=[pl.BlockSpec((tm, QKV_W), lambda j, i: (i, QKV_B0 + j)),
                  pl.BlockSpec((HALO, QKV_W), lambda j, i: (prev_of(i), QKV_B0 + j)),
                  pl.BlockSpec((HALO, QKV_W), lambda j, i: (next_of(i), QKV_B0 + j)),
                  pl.BlockSpec((8, QKV_W), lambda j, i: (0, j)),
                  pl.BlockSpec((tm, QKV_W), lambda j, i: (i, j)),
                  pl.BlockSpec((HALO, QKV_W), lambda j, i: (next_of(i), j)), _ANY],
        out_specs=[pl.BlockSpec((tm, QKV_W), lambda j, i: (i, QKV_B0 + j)), pl.BlockSpec((8, QKV_W), lambda j, i: (0, j))],
        out_shape=[jax.ShapeDtypeStruct(d_proj.shape, d_proj.dtype),
                   jax.ShapeDtypeStruct((8, N_QKV_GROUPS * QKV_W), F32)],
        input_output_aliases={6: 0},
        scratch_shapes=[pltpu.VMEM((HALO + ext, QKV_W), F32), pltpu.VMEM((ext, QKV_W), F32), pltpu.VMEM((ext, QKV_W), F32)],
        compiler_params=_params("parallel", "arbitrary"),
    )(proj, proj, proj, conv_w8, dact, dact, d_proj)


def _round_robin(gens):
    gens = list(gens)
    while gens:
        alive = []
        for gen in gens:
            try:
                next(gen)
                alive.append(gen)
            except StopIteration:
                pass
        gens = alive


def _unit_lower_inverse(a_low, eye):
    n = a_low.shape[0]
    ri = lax.broadcasted_iota(jnp.int32, (n, n), 0)
    ci = lax.broadcasted_iota(jnp.int32, (n, n), 1)
    same = lambda shift: (ri >> shift) == (ci >> shift)
    b = jnp.where(same(3), -a_low, 0.0)
    x = eye + b
    p2 = _dot3(b, b)
    yield
    x = x + _dot3(x, p2)
    p4 = _dot3(p2, p2)
    yield
    x = x + _dot3(x, p4)
    yield
    for shift in (3, 4, 5):
        between = jnp.where(same(shift + 1) & ~same(shift), a_low, 0.0)
        t = _dot3(between, x)
        yield
        x = x - _dot3(x, t)
        yield
    return x


class _GdnChunk:
    def build(self, q, k, v, gb, h):
        C = GDN_CHUNK
        lane = lax.broadcasted_iota(jnp.int32, (C, SM_W), 1)
        g = jnp.sum(jnp.where(lane == h, gb, 0.0), axis=1, keepdims=True)
        self.beta = jnp.sum(jnp.where(lane == h + GDN_HEADS, gb, 0.0), axis=1, keepdims=True)
        ri = lax.broadcasted_iota(jnp.int32, (C, C), 0)
        ci = lax.broadcasted_iota(jnp.int32, (C, C), 1)
        self.causal = ri >= ci
        self.strict = ri > ci
        self.eye = (ri == ci).astype(F32)
        gcb = _dotx(self.causal.astype(F32), jnp.broadcast_to(g, (C, SM_W)))
        yield
        self.gcol = gcb[:, 0:1]
        grow = gcb.T[0:1, 0:C]
        self.decay = jnp.exp(jnp.where(self.causal, self.gcol - grow, -1e30))
        self.egc = jnp.exp(self.gcol)
        glast = gcb[C - 1:C, 0:1]
        self.elast = jnp.exp(glast - self.gcol)
        self.gl = jnp.exp(glast)
        self.q, self.k, self.v = q, k, v
        self.kb = k * self.beta
        m = _dot(self.kb, k, NT)
        n_ = _dot(q, k, NT)
        yield
        self.a_low = jnp.where(self.strict, m * self.decay, 0.0)
        self.p = n_ * self.decay
        self.qd = q * self.egc
        self.kd = k * self.elast
        self.bu = v * self.beta
        self.bw = self.kb * self.egc


GDN_HB = 8
GDN_HG = GDN_HEADS // GDN_HB


def _gdn_specs(n_of):
    C, W = GDN_CHUNK, 128 * GDN_HB
    q_spec = pl.BlockSpec((C, W), lambda g, n: (n_of(n), g))
    k_spec = pl.BlockSpec((C, W), lambda g, n: (n_of(n), g + GDN_HG))
    v_spec = pl.BlockSpec((C, W), lambda g, n: (n_of(n), g + 2 * GDN_HG))
    gb_spec = pl.BlockSpec((C, SM_W), lambda g, n: (n_of(n), 0))
    o_spec = pl.BlockSpec((C, W), lambda g, n: (n_of(n), g))
    s_spec = pl.BlockSpec((GDN_HB, None, GDN_DK, GDN_DV), lambda g, n: (g, n_of(n), 0, 0))
    t_spec = pl.BlockSpec((GDN_HB, None, C, C), lambda g, n: (g, n_of(n), 0, 0))
    return q_spec, k_spec, v_spec, gb_spec, o_spec, s_spec, t_spec


def _gdn_fwd(act, gb, *, name):
    M = act.shape[0]
    N = M // GDN_CHUNK

    def body(q_ref, k_ref, v_ref, gb_ref, o_ref, s_ref, t_ref, state):
        g, n = pl.program_id(0), pl.program_id(1)

        @pl.when(n == 0)
        def _():
            state[...] = jnp.zeros_like(state)

        gb_ = gb_ref[...]

        def head(hh):
            cols = slice(hh * 128, (hh + 1) * 128)
            c = _GdnChunk()
            yield from c.build(q_ref[:, cols], k_ref[:, cols], v_ref[:, cols], gb_, g * GDN_HB + hh)
            tinv = yield from _unit_lower_inverse(c.a_low, c.eye)
            s = state[hh]
            s_ref[hh] = s
            t_ref[hh] = tinv
            u = _dot(tinv, c.bu)
            w = _dot(tinv, c.bw)
            yield
            vn = u - _dot(w, s)
            o1 = _dot(c.qd, s)
            yield
            o_ref[:, cols] = o1 + _dot(c.p, vn)
            state[hh] = c.gl * s + _dot(c.kd, vn, TN)

        _round_robin(head(hh) for hh in range(GDN_HB))

    q_spec, k_spec, v_spec, gb_spec, o_spec, s_spec, t_spec = _gdn_specs(lambda n: n)
    return pl.pallas_call(
        body, name=name, grid=(GDN_HG, N),
        in_specs=[q_spec, k_spec, v_spec, gb_spec], out_specs=[o_spec, s_spec, t_spec],
        out_shape=[jax.ShapeDtypeStruct((M, GDN_V), F32),
                   jax.ShapeDtypeStruct((GDN_HEADS, N, GDN_DK, GDN_DV), F32),
                   jax.ShapeDtypeStruct((GDN_HEADS, N, GDN_CHUNK, GDN_CHUNK), F32)],
        scratch_shapes=[pltpu.VMEM((GDN_HB, GDN_DK, GDN_DV), F32)],
        compiler_params=_params("parallel", "arbitrary"),
    )(act, act, act, gb)


def _gdn_bwd(act, gb, do, s_all, t_all, *, name):
    M = act.shape[0]
    N = M // GDN_CHUNK
    C = GDN_CHUNK
    assert GDN_HG == 1

    def body(q_ref, k_ref, v_ref, gb_ref, do_ref, s_ref, t_ref, dact_ref, dgb_ref, dstate):
        g, n = pl.program_id(0), pl.program_id(1)

        @pl.when(n == 0)
        def _():
            dstate[...] = jnp.zeros_like(dstate)

        gb_ = gb_ref[...]
        last = lax.broadcasted_iota(jnp.int32, (C, 1), 0) == C - 1
        upper = (lax.broadcasted_iota(jnp.int32, (C, C), 0) <= lax.broadcasted_iota(jnp.int32, (C, C), 1)).astype(F32)
        lane = lax.broadcasted_iota(jnp.int32, (C, SM_W), 1)
        def head(hh):
            cols = slice(hh * 128, (hh + 1) * 128)
            h = g * GDN_HB + hh
            c = _GdnChunk()
            yield from c.build(q_ref[:, cols], k_ref[:, cols], v_ref[:, cols], gb_, h)
            tinv = t_ref[hh]
            s = s_ref[hh]
            do_ = do_ref[:, cols]
            ds1 = dstate[hh]
            u = _dot(tinv, c.bu)
            w = _dot(tinv, c.bw)
            dqd = _dot(do_, s, NT)
            dvn0 = _dot(c.p, do_, TN) + _dot(c.kd, ds1)
            dst0 = _dot(c.qd, do_, TN) + c.gl * ds1
            yield
            vn = u - _dot(w, s)
            dvn = dvn0
            yield
            dp = jnp.where(c.causal, _dot(do_, vn, NT), 0.0)
            dstate[hh] = dst0 - _dot(w, dvn, TN)
            dkd = _dot(vn, ds1, NT)
            dw = -_dot(dvn, s, NT)
            dbu = _dot(tinv, dvn, TN)
            dgl = jnp.sum(jnp.sum(s * ds1, axis=1, keepdims=True), axis=0, keepdims=True)
            yield
            dbw = _dot(tinv, dw, TN)
            t1 = _dot(dbu, u, NT)
            yield
            da = jnp.where(c.strict, -(t1 + _dot(dbw, w, NT)), 0.0)
            dn_ = dp * c.decay
            dq0 = _dot(dn_, c.k)
            dk0 = _dot(dn_, c.q, TN)
            yield
            dm = da * c.decay
            e = da * c.a_low + dp * c.p
            dkb = _dot(dm, c.k) + dbw * c.egc
            dact_ref[:, GDN_QK + hh * 128:GDN_QK + (hh + 1) * 128] = (
                _dot(dm, c.kb, TN) + dk0 + dkb * c.beta + dkd * c.elast)
            dact_ref[:, cols] = dq0 + dqd * c.egc
            dact_ref[:, 2 * GDN_QK + hh * 128:2 * GDN_QK + (hh + 1) * 128] = dbu * c.beta
            dbeta = jnp.sum(dbu * c.v, axis=1, keepdims=True) + jnp.sum(dkb * c.k, axis=1, keepdims=True)
            t_kd = jnp.sum(dkd * c.kd, axis=1, keepdims=True)
            dgc = (jnp.sum(e, axis=1, keepdims=True) - jnp.sum(e.T, axis=1, keepdims=True)
                   + jnp.sum(dbw * c.bw, axis=1, keepdims=True) + jnp.sum(dqd * c.qd, axis=1, keepdims=True) - t_kd)
            dgc = dgc + jnp.where(last, jnp.sum(t_kd, axis=0, keepdims=True) + dgl * c.gl, 0.0)
            yield
            dg = _dotx(upper, jnp.broadcast_to(dgc, (C, SM_W)))
            dgb_ref[hh] = jnp.where(lane == h, dg, jnp.where(lane == h + GDN_HEADS, dbeta, 0.0))

        _round_robin(head(hh) for hh in range(GDN_HB))

    rev = lambda n: N - 1 - n
    q_spec, k_spec, v_spec, gb_spec, o_spec, s_spec, t_spec = _gdn_specs(rev)
    dgb_spec = pl.BlockSpec((GDN_HB, C, SM_W), lambda g, n: (g, rev(n), 0))
    return pl.pallas_call(
        body, name=name, grid=(GDN_HG, N),
        in_specs=[q_spec, k_spec, v_spec, gb_spec, o_spec, s_spec, t_spec],
        out_specs=[pl.BlockSpec((C, 2 * GDN_QK + GDN_V), lambda g, n: (rev(n), 0)), dgb_spec],
        out_shape=[jax.ShapeDtypeStruct((M, 2 * GDN_QK + GDN_V), F32),
                   jax.ShapeDtypeStruct((GDN_HEADS, M, SM_W), F32)],
        scratch_shapes=[pltpu.VMEM((GDN_HB, GDN_DK, GDN_DV), F32)],
        compiler_params=_params("parallel", "arbitrary"),
    )(act, act, act, gb, do, s_all, t_all)


GLA_STEP_ROWS = 64
GLA_SUB = GLA_STEP_ROWS // GLA_CHUNK


def _gla_cumsum(la):
    C = GLA_CHUNK
    ltri = (lax.broadcasted_iota(jnp.int32, (C, C), 0) >= lax.broadcasted_iota(jnp.int32, (C, C), 1)).astype(F32)
    return _dotx(ltri, la)


def _gla_decay_rows(b, i):
    rj = lax.broadcasted_iota(jnp.int32, (GLA_CHUNK, GLA_DK), 0)
    return jnp.where(rj <= i, jnp.exp(jnp.minimum(b[i:i + 1, :] - b, 0.0)), 0.0)


GLA_HALF = GLA_CHUNK // 2


def _gla_cross_factors(b):
    top = lax.broadcasted_iota(jnp.int32, b.shape, 0) < GLA_HALF
    bm = b[GLA_HALF - 1:GLA_HALF, :]
    late = jnp.where(top, 0.0, jnp.exp(jnp.minimum(b - bm, 0.0)))
    early = jnp.where(top, jnp.exp(jnp.minimum(bm - b, 0.0)), 0.0)
    return late, early


def _gla_half_decay(bh, ii):
    rj = lax.broadcasted_iota(jnp.int32, bh.shape, 0)
    return jnp.where(rj <= ii, jnp.exp(jnp.minimum(bh[ii:ii + 1, :] - bh, 0.0)), 0.0)


def _gla_scores_t(q, k, b):
    C, H = GLA_CHUNK, GLA_HALF
    lane = lax.broadcasted_iota(jnp.int32, (H, C), 1)
    halves = []
    for h0 in (0, H):
        qh, kh, bh = q[h0:h0 + H], k[h0:h0 + H], b[h0:h0 + H]
        sth = jnp.zeros((H, C), F32)
        for ii in range(H):
            si = jnp.sum(qh[ii:ii + 1, :] * kh * _gla_half_decay(bh, ii), axis=1, keepdims=True)
            sth = jnp.where(lane == h0 + ii, si, sth)
            if ii % 4 == 3:
                yield
        halves.append(sth)
    late, early = _gla_cross_factors(b)
    between = _dot(k * early, q * late, NT)
    yield
    return jnp.concatenate(halves, axis=0) + between


def _gla_specs(n_of):
    R = GLA_STEP_ROWS
    q_spec = pl.BlockSpec((R, GLA_QK), lambda n: (n_of(n), C_GQ // GLA_QK))
    k_spec = pl.BlockSpec((R, GLA_QK), lambda n: (n_of(n), C_GK // GLA_QK))
    v_spec = pl.BlockSpec((R, GLA_V), lambda n: (n_of(n), C_GV // GLA_V))
    la_spec = pl.BlockSpec((R, GLA_QK), lambda n: (n_of(n), 0))
    o_spec = pl.BlockSpec((R, GLA_V), lambda n: (n_of(n), 0))
    s_spec = pl.BlockSpec((GLA_HEADS, None, GLA_SUB, GLA_DV, GLA_DK), lambda n: (0, n_of(n), 0, 0, 0))
    return q_spec, k_spec, v_spec, la_spec, o_spec, s_spec


def _gla_fwd(proj, la, *, name):
    M = proj.shape[0]
    N = M // GLA_STEP_ROWS
    C = GLA_CHUNK

    def body(q_ref, k_ref, v_ref, la_ref, o_ref, s_ref, state):
        n = pl.program_id(0)

        @pl.when(n == 0)
        def _():
            state[...] = jnp.zeros_like(state)

        local = {}

        def within(hh, c):
            kc = slice(hh * GLA_DK, (hh + 1) * GLA_DK)
            vc = slice(hh * GLA_DV, (hh + 1) * GLA_DV)
            rows = slice(c * C, (c + 1) * C)
            q = q_ref[rows, kc] * (GLA_DK ** -0.5)
            k = k_ref[rows, kc]
            v = v_ref[rows, vc]
            b = _gla_cumsum(la_ref[rows, kc])
            yield
            blast = b[C - 1:C, :]
            sc_t = yield from _gla_scores_t(q, k, b)
            kv = _dot(v, k * jnp.exp(blast - b), TN)
            o2 = _dot(sc_t, v, TN)
            yield
            local[hh, c] = (q * jnp.exp(b), jnp.exp(blast), kv, o2)

        def across(hh):
            vc = slice(hh * GLA_DV, (hh + 1) * GLA_DV)
            st = state[hh]
            for c in range(GLA_SUB):
                qe, eblast, kv, o2 = local[hh, c]
                s_ref[hh, c] = st
                o1 = _dot(qe, st, NT)
                yield
                o_ref[c * C:(c + 1) * C, vc] = o1 + o2
                st = st * eblast + kv
            state[hh] = st

        _round_robin(within(hh, c) for c in range(GLA_SUB) for hh in range(GLA_HEADS))
        _round_robin(across(hh) for hh in range(GLA_HEADS))

    q_spec, k_spec, v_spec, la_spec, o_spec, s_spec = _gla_specs(lambda n: n)
    return pl.pallas_call(
        body, name=name, grid=(N,),
        in_specs=[q_spec, k_spec, v_spec, la_spec], out_specs=[o_spec, s_spec],
        out_shape=[jax.ShapeDtypeStruct((M, GLA_V), F32),
                   jax.ShapeDtypeStruct((GLA_HEADS, N, GLA_SUB, GLA_DV, GLA_DK), F32)],
        scratch_shapes=[pltpu.VMEM((GLA_HEADS, GLA_DV, GLA_DK), F32)],
        compiler_params=_params("arbitrary"),
    )(proj, proj, proj, la)


def _gla_bwd(proj, la, do, s_all, d_proj, *, name):
    M = proj.shape[0]
    N = M // GLA_STEP_ROWS
    C = GLA_CHUNK
    qkv_w = 2 * GLA_QK + GLA_V
    assert C_GK == C_GQ + GLA_QK and C_GV == C_GK + GLA_QK and C_GQ % qkv_w == 0

    def body(q_ref, k_ref, v_ref, la_ref, do_ref, s_ref, _, dp_ref, dla_ref, dstate):
        n = pl.program_id(0)

        @pl.when(n == 0)
        def _():
            dstate[...] = jnp.zeros_like(dstate)

        H = GLA_HALF
        lane = lax.broadcasted_iota(jnp.int32, (C, C), 1)
        row = lax.broadcasted_iota(jnp.int32, (C, C), 0)
        ri = lax.broadcasted_iota(jnp.int32, (C, GLA_DK), 0)
        lane_h = lax.broadcasted_iota(jnp.int32, (H, C), 1)
        ri_h = lax.broadcasted_iota(jnp.int32, (H, GLA_DK), 0)
        cross = (row < H) & (lane >= H)
        upper = (row <= lane).astype(F32)
        def head(hh):
            kc = slice(hh * GLA_DK, (hh + 1) * GLA_DK)
            vc = slice(hh * GLA_DV, (hh + 1) * GLA_DV)
            ds1 = dstate[hh]
            for c in reversed(range(GLA_SUB)):
                rows = slice(c * C, (c + 1) * C)
                q = q_ref[rows, kc] * (GLA_DK ** -0.5)
                k = k_ref[rows, kc]
                v = v_ref[rows, vc]
                b = _gla_cumsum(la_ref[rows, kc])
                do_ = do_ref[rows, vc]
                st = s_ref[hh, c]
                dsc_t = _dot(v, do_, NT)
                dqe = _dot(do_, st)
                dke = _dot(v, ds1)
                yield
                blast = b[C - 1:C, :]
                eb = jnp.exp(b)
                elast = jnp.exp(blast - b)
                eblast = jnp.exp(blast)
                qe = q * eb
                ke = k * elast
                dv2 = _dot(ke, ds1, NT)
                ds_new = _dot(do_, qe, TN)
                deblast = jnp.sum(st * ds1, axis=0, keepdims=True)
                sc_halves, dq_halves, dk_halves = [], [], []
                for h0 in (0, H):
                    qh, kh, bh, dsch = q[h0:h0 + H], k[h0:h0 + H], b[h0:h0 + H], dsc_t[h0:h0 + H]
                    sch = jnp.zeros((H, C), F32)
                    dqh = jnp.zeros((H, GLA_DK), F32)
                    dkh = jnp.zeros((H, GLA_DK), F32)
                    for ii in range(H):
                        f = _gla_half_decay(bh, ii)
                        kf = kh * f
                        si = jnp.sum(qh[ii:ii + 1, :] * kf, axis=1, keepdims=True)
                        sch = jnp.where(lane_h == h0 + ii, si, sch)
                        dsi = jnp.sum(jnp.where(lane_h == h0 + ii, dsch, 0.0), axis=1, keepdims=True)
                        dqh = jnp.where(ri_h == ii, jnp.sum(dsi * kf, axis=0, keepdims=True), dqh)
                        dkh = dkh + (dsi * f) * qh[ii:ii + 1, :]
                        if ii % 4 == 3:
                            yield
                    sc_halves.append(sch)
                    dq_halves.append(dqh)
                    dk_halves.append(dkh)
                late, early = _gla_cross_factors(b)
                q_late, k_early = q * late, k * early
                dsc_x = jnp.where(cross, dsc_t, 0.0)
                sc_t = jnp.concatenate(sc_halves, axis=0) + _dot(k_early, q_late, NT)
                dq_sc = jnp.concatenate(dq_halves, axis=0) + _dot(dsc_x, k_early, TN) * late
                dk_sc = jnp.concatenate(dk_halves, axis=0) + _dot(dsc_x, q_late) * early
                yield
                dv1 = _dot(sc_t, do_)
                dp_ref[rows, kc] = ((dq_sc + dqe * eb) * (GLA_DK ** -0.5)).astype(dp_ref.dtype)
                dp_ref[rows, GLA_QK + hh * GLA_DK:GLA_QK + (hh + 1) * GLA_DK] = (dk_sc + dke * elast).astype(dp_ref.dtype)
                t_ke = dke * ke
                db = q * dq_sc - k * dk_sc + dqe * qe - t_ke
                db = db + jnp.where(ri == C - 1, jnp.sum(t_ke, axis=0, keepdims=True) + deblast * eblast, 0.0)
                dla = _dotx(upper, db)
                yield
                dp_ref[rows, 2 * GLA_QK + hh * GLA_DV:2 * GLA_QK + (hh + 1) * GLA_DV] = (dv1 + dv2).astype(dp_ref.dtype)
                dla_ref[rows, kc] = dla
                ds1 = ds1 * eblast + ds_new
            dstate[hh] = ds1

        _round_robin(head(hh) for hh in range(GLA_HEADS))

    rev = lambda n: N - 1 - n
    q_spec, k_spec, v_spec, la_spec, o_spec, s_spec = _gla_specs(rev)
    return pl.pallas_call(
        body, name=name, grid=(N,),
        in_specs=[q_spec, k_spec, v_spec, la_spec, o_spec, s_spec, _ANY],
        out_specs=[pl.BlockSpec((GLA_STEP_ROWS, qkv_w), lambda n: (rev(n), C_GQ // qkv_w)), la_spec],
        out_shape=[jax.ShapeDtypeStruct(d_proj.shape, d_proj.dtype), jax.ShapeDtypeStruct((M, GLA_QK), F32)],
        input_output_aliases={6: 0},
        scratch_shapes=[pltpu.VMEM((GLA_HEADS, GLA_DV, GLA_DK), F32)],
        compiler_params=_params("arbitrary"),
    )(proj, proj, proj, la, do, s_all, d_proj)


def _head_norm(o, wn):
    r = lax.rsqrt(jnp.mean(o * o, axis=-1, keepdims=True) + NORM_EPS)
    return o * r, r


def _mix_heads():
    heads = [(0, GDN_DV, hh * GDN_DV, hh * GDN_DV) for hh in range(GDN_HEADS)]
    heads += [(1, GLA_DV, GDN_V + hh * GLA_DV, hh * GLA_DV) for hh in range(GLA_HEADS)]
    return heads


def _mix_fwd(o_gdn, o_gla, proj, wn_gdn, wn_gla, *, name):
    M = proj.shape[0]
    tm = _tile(M, 344, 16)

    def body(og_ref, ol_ref, z_ref, r_ref, wg_ref, wl_ref, m_ref):
        srcs = ((og_ref, z_ref, wg_ref), (ol_ref, r_ref, wl_ref))
        for grp, width, mcol, col in _mix_heads():
            o_ref, gate_ref, w_ref = srcs[grp]
            xhat, _ = _head_norm(o_ref[:, col:col + width], None)
            gate, _ = _silu_and_grad(gate_ref[:, col:col + width])
            m_ref[:, mcol:mcol + width] = (xhat * w_ref[...] * gate).astype(m_ref.dtype)

    full = lambda s: pl.BlockSpec(s, lambda i: (0, 0))
    return pl.pallas_call(
        body, name=name, grid=(M // tm,),
        in_specs=[pl.BlockSpec((tm, GDN_V), lambda i: (i, 0)), pl.BlockSpec((tm, GLA_V), lambda i: (i, 0)),
                  pl.BlockSpec((tm, GDN_V), lambda i: (i, C_Z // GDN_V)),
                  pl.BlockSpec((tm, GLA_V), lambda i: (i, C_GR // GLA_V)),
                  full((1, GDN_DV)), full((1, GLA_DV))],
        out_specs=pl.BlockSpec((tm, D_MODEL), lambda i: (i, 0)),
        out_shape=jax.ShapeDtypeStruct((M, D_MODEL), BF16),
        compiler_params=_params("parallel"),
    )(o_gdn, o_gla, proj, proj, wn_gdn, wn_gla)


def _mix_bwd(o_gdn, o_gla, proj, wn_gdn, wn_gla, dmixed, *, name):
    M = proj.shape[0]
    tm = _tile(M, 344, 16)
    g_ = M // tm
    assert C_Z == 0 and C_GR == GDN_V

    def body(og_ref, ol_ref, z_ref, r_ref, wg_ref, wl_ref, dm_ref,
             dog_ref, dol_ref, dzr_ref, dwg_ref, dwl_ref):
        i = pl.program_id(0)
        srcs = ((og_ref, z_ref, wg_ref, dog_ref), (ol_ref, r_ref, wl_ref, dol_ref))
        dws = [jnp.zeros((1, GDN_DV), F32), jnp.zeros((1, GLA_DV), F32)]
        for grp, width, mcol, col in _mix_heads():
            o_ref, gate_ref, w_ref, do_ref = srcs[grp]
            cols = slice(col, col + width)
            xhat, r = _head_norm(o_ref[:, cols], None)
            gate, dgate_dc = _silu_and_grad(gate_ref[:, cols])
            dm = dm_ref[:, mcol:mcol + width]
            dzr_ref[:, mcol:mcol + width] = (dm * xhat * w_ref[...] * dgate_dc).astype(dzr_ref.dtype)
            dnorm = dm * gate
            dws[grp] = dws[grp] + jnp.sum(dnorm * xhat, axis=0, keepdims=True)
            dxhat = dnorm * w_ref[...]
            do_ref[:, cols] = r * (dxhat - xhat * jnp.mean(dxhat * xhat, axis=-1, keepdims=True))

        @pl.when(i == 0)
        def _():
            dwg_ref[...] = dws[0]
            dwl_ref[...] = dws[1]

        @pl.when(i > 0)
        def _():
            dwg_ref[...] += dws[0]
            dwl_ref[...] += dws[1]

    full = lambda s: pl.BlockSpec(s, lambda i: (0, 0))
    half = pl.BlockSpec((tm, GDN_V), lambda i: (i, 0))
    return pl.pallas_call(
        body, name=name, grid=(g_,),
        in_specs=[half, half, pl.BlockSpec((tm, GDN_V), lambda i: (i, C_Z // GDN_V)),
                  pl.BlockSpec((tm, GLA_V), lambda i: (i, C_GR // GLA_V)),
                  full((1, GDN_DV)), full((1, GLA_DV)), pl.BlockSpec((tm, D_MODEL), lambda i: (i, 0))],
        out_specs=[half, half, pl.BlockSpec((tm, GDN_V + GLA_V), lambda i: (i, 0)),
                   full((1, GDN_DV)), full((1, GLA_DV))],
        out_shape=[jax.ShapeDtypeStruct((M, GDN_V), F32), jax.ShapeDtypeStruct((M, GLA_V), F32),
                   jax.ShapeDtypeStruct((M, D_PROJ), BF16),
                   jax.ShapeDtypeStruct((1, GDN_DV), F32), jax.ShapeDtypeStruct((1, GLA_DV), F32)],
        compiler_params=_params("arbitrary"),
    )(o_gdn, o_gla, proj, proj, wn_gdn, wn_gla, dmixed)


def _swiglu_fwd(n, w_gate_t, w_up_t, *, name, tm=1376, tn=512):
    M, D = n.shape
    F = w_gate_t.shape[0]
    tm, tn = _tile(M, tm, 16), _tile(F, tn, 128)

    def body(n_ref, wg_ref, wu_ref, g_ref, u_ref, a_ref):
        x = n_ref[...]
        g = _dot(x, wg_ref[...], NT)
        u = _dot(x, wu_ref[...], NT)
        s, _ = _silu_and_grad(g)
        g_ref[...] = g.astype(g_ref.dtype)
        u_ref[...] = u.astype(u_ref.dtype)
        a_ref[...] = (s * u).astype(a_ref.dtype)

    w_spec = pl.BlockSpec((tn, D), lambda i, j: (j, 0))
    o_spec = pl.BlockSpec((tm, tn), lambda i, j: (i, j))
    return pl.pallas_call(
        body, name=name, grid=(M // tm, F // tn),
        in_specs=[pl.BlockSpec((tm, D), lambda i, j: (i, 0)), w_spec, w_spec], out_specs=[o_spec] * 3,
        out_shape=[jax.ShapeDtypeStruct((M, F), BF16)] * 3, compiler_params=_params("parallel", "parallel"),
    )(n, w_gate_t, w_up_t)


def _swiglu_bwd(dh, w_down, gate, up, *, name, after=None, tm=1376, tn=512):
    M, D = dh.shape
    F = w_down.shape[0]
    tm, tn = _tile(M, tm, 16), _tile(F, tn, 128)
    n_after = 0 if after is None else 1

    def body(*refs):
        dh_ref, w_ref, g_ref, u_ref, dg_ref, du_ref = refs[n_after:]
        da = _dot(dh_ref[...], w_ref[...], NT)
        s, ds = _silu_and_grad(g_ref[...].astype(F32))
        dg_ref[...] = (da * u_ref[...].astype(F32) * ds).astype(dg_ref.dtype)
        du_ref[...] = (da * s).astype(du_ref.dtype)

    o_spec = pl.BlockSpec((tm, tn), lambda i, j: (i, j))
    return pl.pallas_call(
        body, name=name, grid=(M // tm, F // tn),
        in_specs=[_ANY] * n_after + [pl.BlockSpec((tm, D), lambda i, j: (i, 0)),
                                     pl.BlockSpec((tn, D), lambda i, j: (j, 0)), o_spec, o_spec],
        out_specs=[o_spec, o_spec], out_shape=[jax.ShapeDtypeStruct((M, F), BF16)] * 2,
        compiler_params=_params("parallel", "parallel"),
    )(*((after,) if n_after else ()), dh, w_down, gate, up)


def _adamw(w, g, m, v, *, name):
    shape = w.shape
    cols = shape[-1]
    rows = w.size // cols
    w2, g2, m2, v2 = (t.reshape(rows, cols) for t in (w, g, m, v))
    if rows % 8 == 0 or cols % 128 != 0:
        tr, tc = (_tile(rows, 256, 8) if rows % 8 == 0 else rows), cols
    else:
        tr, tc = rows, _tile(cols, 256, 128)

    def body(w_ref, g_ref, m_ref, v_ref, d_ref, nm_ref, nv_ref):
        g_ = g_ref[...]
        nm = ADAM_B1 * m_ref[...] + (1.0 - ADAM_B1) * g_
        nv = ADAM_B2 * v_ref[...] + (1.0 - ADAM_B2) * (g_ * g_)
        m_hat = nm / (1.0 - ADAM_B1 ** ADAM_STEP)
        v_hat = nv / (1.0 - ADAM_B2 ** ADAM_STEP)
        d_ref[...] = -ADAM_LR * (m_hat / (jnp.sqrt(v_hat) + ADAM_EPS) + ADAM_WD * w_ref[...])
        nm_ref[...] = nm
        nv_ref[...] = nv

    blk = pl.BlockSpec((tr, tc), lambda i, j: (i, j))
    outs = pl.pallas_call(
        body, name=name, grid=(rows // tr, cols // tc), in_specs=[blk] * 4, out_specs=[blk] * 3,
        out_shape=[jax.ShapeDtypeStruct((rows, cols), F32)] * 3, compiler_params=_params("parallel", "parallel"),
    )(w2, g2, m2, v2)
    return tuple(t.reshape(shape) for t in outs)


def _sum_slabs(x, *, name):
    _, R, C = x.shape
    sub = 16 if x.dtype == BF16 else 8
    if R % sub == 0:
        tr, tc = _tile(R, 128, sub), C
    else:
        tr, tc = R, _tile(C, 256, 128)

    def body(x_ref, o_ref):
        acc = x_ref[0].astype(F32)
        for s in range(1, N_DEV):
            acc = acc + x_ref[s].astype(F32)
        o_ref[...] = acc

    return pl.pallas_call(
        body, name=name, grid=(R // tr, C // tc),
        in_specs=[pl.BlockSpec((N_DEV, tr, tc), lambda i, j: (0, i, j))],
        out_specs=pl.BlockSpec((tr, tc), lambda i, j: (i, j)),
        out_shape=jax.ShapeDtypeStruct((R, C), F32), compiler_params=_params("parallel", "parallel"),
    )(x)


def _peers():
    x, y, c = lax.axis_index("x"), lax.axis_index("y"), lax.axis_index("c")
    me = 4 * x + 2 * y + c
    peers = []
    for k in range(1, N_DEV):
        px = 1 - x if k & 4 else x
        py = 1 - y if k & 2 else y
        pc = 1 - c if k & 1 else c
        peers.append(((px, py, pc), 4 * px + 2 * py + pc))
    return me, peers


def _exchange(x, *, gather, name):
    slab = x.shape if gather else x.shape[1:]

    def body(x_ref, o_ref, send_sems, recv_sems, own_sem):
        me, peers = _peers()
        own = pltpu.make_async_copy(x_ref if gather else x_ref.at[me], o_ref.at[me], own_sem)
        own.start()
        sends, recvs = [], []
        for k, (pos, idx) in enumerate(peers):
            sends.append(pltpu.make_async_remote_copy(
                src_ref=x_ref if gather else x_ref.at[idx], dst_ref=o_ref.at[me],
                send_sem=send_sems.at[k], recv_sem=recv_sems.at[k],
                device_id=pos, device_id_type=pl.DeviceIdType.MESH))
            recvs.append(pltpu.make_async_remote_copy(
                src_ref=x_ref if gather else x_ref.at[idx], dst_ref=o_ref.at[idx],
                send_sem=send_sems.at[k], recv_sem=recv_sems.at[k],
                device_id=pos, device_id_type=pl.DeviceIdType.MESH))
        for cp in sends:
            cp.start()
        for cp in recvs:
            cp.wait_recv()
        for cp in sends:
            cp.wait_send()
        own.wait()

    hbm = pl.BlockSpec(memory_space=pltpu.HBM)
    return pl.pallas_call(
        body, name=name, in_specs=[hbm], out_specs=hbm,
        out_shape=jax.ShapeDtypeStruct((N_DEV,) + tuple(slab), x.dtype),
        scratch_shapes=[pltpu.SemaphoreType.DMA((N_DEV - 1,)), pltpu.SemaphoreType.DMA((N_DEV - 1,)),
                        pltpu.SemaphoreType.DMA],
    )(x)


_HBM = pl.BlockSpec(memory_space=pltpu.HBM)
_SEM = pl.BlockSpec(memory_space=pltpu.SEMAPHORE)
_EFFECT = pltpu.SideEffectType.DATAFLOW_SIDE_EFFECTING


PLAN_GATHER = tuple((k, "x", 0) for k in range(1, N_DEV))
PLAN_SCATTER = tuple((k, "xk", 0) for k in range(1, N_DEV))
PLAN_GATHER_CHIPS = tuple((k, "x", 0) for k in (1, 2, 4, 6))
PLAN_GATHER_PASS_ON = tuple((1, ("land", q), q) for q in (2, 4, 6))


def _plan_refs(plan, j, x_ref, land_ref, me, peers, receiving):
    k, source, r = plan[j]
    index_of = lambda q: me if q == 0 else peers[q - 1][1]
    pos, target = peers[k - 1]
    if source == "x":
        src = x_ref
    elif source == "xk":
        src = x_ref.at[target]
    else:
        src = land_ref.at[index_of(source[1])]
    return pos, src, land_ref.at[index_of(k ^ r) if receiving else index_of(r)]


def _exchange_start(x, *, plan, name, after=None, land=None, slab=None):
    n_after = 0 if after is None else 1
    n = len(plan)

    def body(*refs):
        x_ref, land_ref, send_sems, recv_sems, _, _, token = refs[n_after:]
        me, peers = _peers()
        for j in range(n):
            pos, src, dst = _plan_refs(plan, j, x_ref, land_ref, me, peers, receiving=False)
            pltpu.make_async_remote_copy(src_ref=src, dst_ref=dst, send_sem=send_sems.at[j], recv_sem=recv_sems.at[j],
                                         device_id=pos, device_id_type=pl.DeviceIdType.MESH).start()
        token[...] = jnp.zeros_like(token)

    if land is None:
        land = lax.empty((N_DEV,) + tuple(slab), x.dtype)
    return pl.pallas_call(
        body, name=name,
        out_shape=(pltpu.SemaphoreType.DMA((n,)), pltpu.SemaphoreType.DMA((n,)),
                   pltpu.HBM(x.shape, x.dtype), pltpu.HBM(land.shape, land.dtype), jax.ShapeDtypeStruct((8, 128), F32)),
        in_specs=[_ANY] * n_after + [_HBM, _HBM],
        out_specs=(_SEM, _SEM, _HBM, _HBM, pl.BlockSpec(memory_space=pltpu.VMEM)),
        input_output_aliases={n_after: 2, n_after + 1: 3},
        compiler_params=pltpu.CompilerParams(has_side_effects=_EFFECT),
    )(*((after,) if n_after else ()), pltpu.with_memory_space_constraint(x, pltpu.HBM),
      pltpu.with_memory_space_constraint(land, pltpu.HBM))


def _exchange_wait(handle, after, *, plan, name):
    send_sems, recv_sems, x_thru, land_thru, _ = handle
    afters = list(after) if isinstance(after, (list, tuple)) else [after]

    def body(x_ref, land_ref, send_sems, recv_sems, *rest):
        me, peers = _peers()
        for j in range(len(plan)):
            pos, src, dst = _plan_refs(plan, j, x_ref, land_ref, me, peers, receiving=True)
            cp = pltpu.make_async_remote_copy(src_ref=src, dst_ref=dst, send_sem=send_sems.at[j], recv_sem=recv_sems.at[j],
                                              device_id=pos, device_id_type=pl.DeviceIdType.MESH)
            cp.wait_send()
            cp.wait_recv()

    return pl.pallas_call(
        body, name=name,
        out_shape=(pltpu.HBM(x_thru.shape, x_thru.dtype), pltpu.HBM(land_thru.shape, land_thru.dtype)),
        in_specs=[_HBM, _HBM, _SEM, _SEM] + [_ANY] * len(afters), out_specs=(_HBM, _HBM),
        input_output_aliases={0: 0, 1: 1}, compiler_params=pltpu.CompilerParams(has_side_effects=_EFFECT),
    )(x_thru, land_thru, send_sems, recv_sems, *afters)


W_IN_SLAB = D_IN // N_DEV


def _to_proj_rows(t):
    z = jnp.zeros((D_PROJ - C_SM - 2 * GDN_HEADS - GLA_RANK,) + t.shape[1:], t.dtype)
    return jnp.concatenate([t[R_Z:R_A], t[R_GR:R_LR], t[R_GQ:R_GR], t[:R_Z], t[R_A:R_GQ], t[R_LR:], z], axis=0)


def _from_proj_rows(t):
    ab = C_SM + 2 * GDN_HEADS
    return jnp.concatenate([t[C_QKV:C_SM], t[C_Z:C_GR], t[C_SM:ab], t[C_GQ:C_QKV], t[C_GR:C_GQ],
                            t[ab:ab + GLA_RANK]], axis=0)


def _local_step(x, target, meta, attn_nw, conv_w, a_log, dt_bias, gdn_nw, w2, b2, gla_nw, ffn_nw, final_nw,
                fetch, emit, start=None):
    S = x.shape[0]
    head = jnp.concatenate([jnp.zeros((ROW_PAD, D_MODEL), F32), meta], axis=0)
    conv_w8 = jnp.concatenate([conv_w, jnp.zeros((8 - CONV_K, conv_w.shape[1]), F32)], axis=0)
    w2p = jnp.zeros((SM_W, GLA_QK), F32).at[2 * GDN_HEADS:2 * GDN_HEADS + GLA_RANK].set(w2)
    alog_p = jnp.zeros((1, SM_W), F32).at[:, :GDN_HEADS].set(a_log)
    dt_p = jnp.zeros((1, SM_W), F32).at[:, :GDN_HEADS].set(dt_bias)

    h0, n1 = _embed_norm(head, x, attn_nw, name="attn_norm", after=start)
    w_in_t = fetch("w_in_t", (n1, conv_w8, w2p, alog_p, dt_p))
    proj = _matmul(n1, w_in_t, mode="nt", name="in_proj")
    gb, la = _gates_fwd(proj, w2p, b2, alog_p, dt_p, name="gates")
    act = _prep_fwd(proj, conv_w8, name="gdn_prep")
    o_gdn, s_gdn, t_gdn = _gdn_fwd(act, gb, name="gdn_fwd")
    o_gla, s_gla = _gla_fwd(proj, la, name="gla_fwd")
    mixed = _mix_fwd(o_gdn, o_gla, proj, gdn_nw, gla_nw, name="mix")
    w_out = fetch("w_out", mixed)
    h1 = _matmul(mixed, w_out, mode="nn", add=h0, name="out_proj")
    n2 = _rmsnorm_fwd(h1, ffn_nw, name="ffn_norm")
    w_gate_t, w_up_t = fetch("w_gate_t", n2), fetch("w_up_t", n2)
    gate, up, hid = _swiglu_fwd(n2, w_gate_t, w_up_t, name="swiglu")
    w_down = fetch("w_down", hid)
    h2 = _matmul(hid, w_down, mode="nn", add=h1, name="ffn_down", tm=688, tk=D_FF)
    dh2, dh2_b, d_final_nw, loss = _loss_head(h2, final_nw, target, name="loss_head")

    wg = dict(mode="tn", out_dtype=BF16, tn=512, tk=S + HEAD_ROWS)
    tok = emit("w_down", _matmul(hid, dh2_b, name="d_w_down", tm=704, **wg))
    d_gate, d_up = _swiglu_bwd(dh2_b, w_down, gate, up, name="d_swiglu", after=tok)
    tok = emit("w_gate_t", _matmul(d_gate, n2, name="d_w_gate", tm=704, **wg))
    tok = emit("w_up_t", _matmul(d_up, n2, name="d_w_up", tm=704, after=tok, **wg))
    d_n2 = _matmul_pair(d_gate, w_gate_t, d_up, w_up_t, name="d_n2", after=tok)
    dh1, dh1_b, d_ffn_nw = _rmsnorm_bwd(h1, ffn_nw, d_n2, dh2, name="d_ffn_norm", also_bf16=True)

    tok = emit("w_out", _matmul(mixed, dh1_b, name="d_w_out", tm=512, **wg))
    d_mixed = _matmul(dh1_b, w_out, mode="nt", name="d_mixed", after=tok)
    do_gdn, do_gla, d_proj, d_gdn_nw, d_gla_nw = _mix_bwd(o_gdn, o_gla, proj, gdn_nw, gla_nw, d_mixed, name="d_mix")
    d_proj, d_la = _gla_bwd(proj, la, do_gla, s_gla, d_proj, name="gla_bwd")
    dact, dgb_heads = _gdn_bwd(act, gb, do_gdn, s_gdn, t_gdn, name="gdn_bwd")
    d_proj, d_w2p, d_b2, d_alog, d_dt = _gates_bwd(proj, w2p, b2, alog_p, dt_p, dgb_heads, d_la, d_proj, name="d_gates")
    d_proj, d_conv_w8 = _prep_bwd(proj, conv_w8, dact, d_proj, name="d_gdn_prep")
    tok = emit("w_in_t", _matmul(d_proj, n1, name="d_w_in", tm=768, **wg))
    d_n1 = _matmul(d_proj, w_in_t, mode="nn", name="d_n1", tm=688, tk=D_PROJ, after=tok)
    grad_x, d_head, d_attn_nw = _embed_norm_bwd(h0, attn_nw, d_n1, dh1, name="d_attn_norm")

    return dict(
        loss=loss[0, 0], grad_x=grad_x, meta=d_head[ROW_PAD:HEAD_ROWS], attn_nw=d_attn_nw,
        conv_w=d_conv_w8[:CONV_K], a_log=d_alog[:, :GDN_HEADS], dt_bias=d_dt[:, :GDN_HEADS], gdn_nw=d_gdn_nw,
        w2=d_w2p[2 * GDN_HEADS:2 * GDN_HEADS + GLA_RANK], b2=d_b2, gla_nw=d_gla_nw, ffn_nw=d_ffn_nw,
        final_nw=d_final_nw)


SMALL_ROWS = 32


def kernel(x, meta_tokens, attn_norm_w, w_in, gdn_conv_w, gdn_a_log, gdn_dt_bias, gdn_norm_w, gla_gate_w2, gla_gate_b, gla_norm_w, w_out, ffn_norm_w, w_gate, w_up, w_down, final_norm_w, loss_target, m_meta_tokens, m_attn_norm_w, m_w_in, m_gdn_conv_w, m_gdn_a_log, m_gdn_dt_bias, m_gdn_norm_w, m_gla_gate_w2, m_gla_gate_b, m_gla_norm_w, m_w_out, m_ffn_norm_w, m_w_gate, m_w_up, m_w_down, m_final_norm_w, v_meta_tokens, v_attn_norm_w, v_w_in, v_gdn_conv_w, v_gdn_a_log, v_gdn_dt_bias, v_gdn_norm_w, v_gla_gate_w2, v_gla_gate_b, v_gla_norm_w, v_w_out, v_ffn_norm_w, v_w_gate, v_w_up, v_w_down, v_final_norm_w):
    me = 4 * lax.axis_index("x") + 2 * lax.axis_index("y") + lax.axis_index("c")
    n_in, n_ff, n_out = D_IN // N_DEV, D_FF // N_DEV, D_MODEL // N_DEV

    n_conv = gdn_conv_w.shape[2]
    n_w2 = gla_gate_w2.shape[2]
    n_meta = meta_tokens.shape[1]
    small = jnp.zeros((40, n_conv), F32)
    small = small.at[0:N_META, :n_meta].set(meta_tokens)
    small = small.at[N_META:N_META + CONV_K, :].set(gdn_conv_w[0])
    small = small.at[24:24 + GLA_RANK, :n_w2].set(gla_gate_w2[0])
    small_all = _exchange(small, gather=True, name="gather_small")
    meta_f = small_all[:, 0:N_META, :n_meta].transpose(1, 0, 2).reshape(N_META, D_MODEL)
    conv_f = small_all[:, N_META:N_META + CONV_K, :].transpose(1, 0, 2).reshape(CONV_K, N_DEV * n_conv)
    w2_f = small_all[:, 24:24 + GLA_RANK, :n_w2].transpose(1, 0, 2).reshape(GLA_RANK, N_DEV * n_w2)

    w_in_slab = w_in[0].T.astype(BF16)
    in_h = _exchange_start(w_in_slab, plan=PLAN_GATHER_CHIPS, slab=w_in_slab.shape, name="gather_w_in_start")
    handles, tok = {}, in_h[4]
    for wname, slab in (("w_out", w_out[0]), ("w_gate_t", w_gate[0].T), ("w_up_t", w_up[0].T), ("w_down", w_down[0])):
        slab = slab.astype(BF16)
        handles[wname] = _exchange_start(slab, plan=PLAN_GATHER, slab=slab.shape, name="gather_" + wname + "_start", after=tok)
        tok = handles[wname][4]

    def fetch(name, after):
        if name == "w_in_t":
            own, got = _exchange_wait(in_h, after, plan=PLAN_GATHER_CHIPS, name="gather_w_in_wait")
            pass_h = _exchange_start(own, plan=PLAN_GATHER_PASS_ON, land=got, name="pass_w_in_start")
            own, got = _exchange_wait(pass_h, pass_h[4], plan=PLAN_GATHER_PASS_ON, name="pass_w_in_wait")
            got = lax.dynamic_update_index_in_dim(got, own, me, 0)
            return _to_proj_rows(got.reshape(D_IN, D_MODEL))
        own, got = _exchange_wait(handles[name], after, plan=PLAN_GATHER, name="gather_" + name + "_wait")
        got = lax.dynamic_update_index_in_dim(got, own, me, 0)
        return got.reshape(N_DEV * got.shape[1], D_MODEL)

    sent = {}

    def emit(name, grad):
        if name == "w_in_t":
            grad = _from_proj_rows(grad)
        parts = grad.reshape(N_DEV, grad.shape[0] // N_DEV, D_MODEL)
        sent[name] = _exchange_start(parts, plan=PLAN_SCATTER, slab=parts.shape[1:], name="scatter_" + name + "_start")
        return sent[name][4]

    g = _local_step(x[0], loss_target[0], meta_f, attn_norm_w, conv_f, gdn_a_log, gdn_dt_bias, gdn_norm_w, w2_f,
                    gla_gate_b, gla_norm_w, ffn_norm_w, final_norm_w.reshape(1, D_MODEL), fetch, emit, start=tok)

    def total(name, after):
        handle = sent[name]
        own, got = _exchange_wait(handle, after, plan=PLAN_SCATTER, name="scatter_" + name + "_wait")
        got = lax.dynamic_update_index_in_dim(got, lax.dynamic_index_in_dim(own, me, 0, keepdims=False), me, 0)
        return _sum_slabs(got, name="sum_" + name)

    grad_w_down = total("w_down", g["attn_nw"])[None]
    grad_w_gate = total("w_gate_t", grad_w_down)
    grad_w_up = total("w_up_t", grad_w_gate)
    grad_w_out = total("w_out", grad_w_up)[None]
    grad_w_in = total("w_in_t", grad_w_out)

    misc = jnp.concatenate([g["a_log"], g["dt_bias"], g["gdn_nw"], g["gla_nw"], g["b2"], g["loss"].reshape(1, 1)], axis=1)
    n_misc = misc.shape[1]
    misc = jnp.pad(misc, ((0, 0), (0, D_MODEL - n_misc)))
    rows = jnp.concatenate([g["attn_nw"], g["ffn_nw"], g["final_nw"], misc, g["meta"],
                            g["conv_w"].reshape(-1, D_MODEL), g["w2"].reshape(-1, D_MODEL)], axis=0)
    rows = jnp.pad(rows, ((0, SMALL_ROWS - rows.shape[0]), (0, 0)))
    tot = _sum_slabs(_exchange(rows, gather=True, name="gather_small_grads"), name="sum_small_grads")
    grad_attn_nw, grad_ffn_nw, grad_final_nw = tot[0:1], tot[1:2], tot[2]
    grad_a_log = tot[3:4, 0:8]
    grad_dt = tot[3:4, 8:16]
    grad_gdn_nw = tot[3:4, 16:16 + GDN_DV]
    grad_gla_nw = tot[3:4, 144:144 + GLA_DV]
    grad_b2 = tot[3:4, 400:400 + GLA_QK]
    loss = tot[3, n_misc - 1]
    r0 = 4 + N_META
    grad_meta = lax.dynamic_slice(tot[4:r0], (0, me * n_meta), (N_META, n_meta))
    r1 = r0 + CONV_K * N_DEV * n_conv // D_MODEL
    grad_conv = lax.dynamic_slice(tot[r0:r1].reshape(CONV_K, N_DEV * n_conv), (0, me * n_conv), (CONV_K, n_conv))[None]
    r2 = r1 + GLA_RANK * N_DEV * n_w2 // D_MODEL
    grad_w2 = lax.dynamic_slice(tot[r1:r2].reshape(GLA_RANK, N_DEV * n_w2), (0, me * n_w2), (GLA_RANK, n_w2))[None]

    weights = [meta_tokens, attn_norm_w, w_in, gdn_conv_w, gdn_a_log, gdn_dt_bias, gdn_norm_w, gla_gate_w2,
               gla_gate_b, gla_norm_w, w_out, ffn_norm_w, w_gate, w_up, w_down, final_norm_w]
    grads = [grad_meta, grad_attn_nw, grad_w_in, grad_conv, grad_a_log, grad_dt, grad_gdn_nw, grad_w2,
             grad_b2, grad_gla_nw, grad_w_out, grad_ffn_nw, grad_w_gate, grad_w_up, grad_w_down, grad_final_nw]
    ms = [m_meta_tokens, m_attn_norm_w, m_w_in, m_gdn_conv_w, m_gdn_a_log, m_gdn_dt_bias, m_gdn_norm_w,
          m_gla_gate_w2, m_gla_gate_b, m_gla_norm_w, m_w_out, m_ffn_norm_w, m_w_gate, m_w_up, m_w_down, m_final_norm_w]
    vs = [v_meta_tokens, v_attn_norm_w, v_w_in, v_gdn_conv_w, v_gdn_a_log, v_gdn_dt_bias, v_gdn_norm_w,
          v_gla_gate_w2, v_gla_gate_b, v_gla_norm_w, v_w_out, v_ffn_norm_w, v_w_gate, v_w_up, v_w_down, v_final_norm_w]
    transposed = (2, 12, 13)
    outs = [[], [], [], []]
    for idx, (w, gr, m, v) in enumerate(zip(weights, grads, ms, vs)):
        if idx in transposed:
            res = (gr,) + _adamw(w[0].T, gr, m[0].T, v[0].T, name=f"adamw_{idx}")
            res = [t.T[None] for t in res]
        else:
            gr = gr.reshape(w.shape)
            res = (gr,) + _adamw(w, gr, m, v, name=f"adamw_{idx}")
        for lst, t in zip(outs, res):
            lst.append(t)
    return (loss, g["grad_x"][None], *outs[0], *outs[1], *outs[2], *outs[3])
```

```python
import functools

import jax
import jax.numpy as jnp
from jax import lax
from jax.experimental import pallas as pl
from jax.experimental.pallas import tpu as pltpu

F32 = jnp.float32
BF16 = jnp.bfloat16
_MXU_DTYPE = jnp.bfloat16

D_MODEL = 2048
N_META = 16
ROW_PAD = 48
HEAD_ROWS = ROW_PAD + N_META
CONV_K = 4
GDN_HEADS, GDN_DK, GDN_DV, GDN_CHUNK = 8, 128, 128, 64
GLA_HEADS, GLA_DK, GLA_DV, GLA_CHUNK = 4, 128, 256, 16
GLA_RANK = 16
GLA_GATE_NORMALIZER = 16.0
GDN_QK = GDN_HEADS * GDN_DK
GDN_V = GDN_HEADS * GDN_DV
GLA_QK = GLA_HEADS * GLA_DK
GLA_V = GLA_HEADS * GLA_DV
D_FF = 5632
D_IN = 7200
NORM_EPS = 1e-6
C_Z, C_GR, C_GQ, C_GK, C_GV, C_QKV, C_SM = 0, 1024, 2048, 2560, 3072, 4096, 7168
SM_W = 128
D_PROJ = 7680
R_Z, R_A, R_B, R_GQ, R_GK, R_GV, R_GR, R_LR = 3072, 4096, 4104, 4112, 4624, 5136, 6160, 7184

ADAM_LR, ADAM_B1, ADAM_B2, ADAM_EPS, ADAM_WD, ADAM_STEP = 0.001, 0.9, 0.999, 1e-08, 0.01, 10

N_DEV = 8
VMEM_LIMIT = 56 * 1024 * 1024

NN = (((1,), (0,)), ((), ()))
NT = (((1,), (1,)), ((), ()))
TN = (((0,), (0,)), ((), ()))


def _dot(a, b, dims=NN):
    return lax.dot_general(a.astype(_MXU_DTYPE), b.astype(_MXU_DTYPE), dims, preferred_element_type=F32)


def _dotx(a, b, dims=NN):
    return lax.dot_general(a, b, dims, precision=lax.Precision.HIGHEST, preferred_element_type=F32)


def _dot3(a, b):
    ah = a.astype(BF16)
    al = (a - ah.astype(F32)).astype(BF16)
    bh = b.astype(BF16)
    bl = (b - bh.astype(F32)).astype(BF16)
    d = functools.partial(lax.dot_general, dimension_numbers=NN, preferred_element_type=F32)
    return d(ah, bh) + (d(ah, bl) + d(al, bh))


def _tile(n, target, mult=8):
    best = None
    for t in range(mult, min(n, target) + 1, mult):
        if n % t == 0:
            best = t
    return best if best is not None else n


def _params(*sem):
    return pltpu.CompilerParams(dimension_semantics=sem, vmem_limit_bytes=VMEM_LIMIT)


def _sigmoid(x):
    return 0.5 * jnp.tanh(0.5 * x) + 0.5


def _softplus(x):
    return jnp.maximum(x, 0.0) + jnp.log1p(jnp.exp(-jnp.abs(x)))


def _silu_and_grad(c):
    s = _sigmoid(c)
    return c * s, s * (1.0 + c * (1.0 - s))


_ANY = pl.BlockSpec(memory_space=pl.ANY)


def _matmul(a, b, *, mode, name, out_dtype=F32, add=None, after=None, tm=1376, tn=512):
    if mode == "tn":
        K, M = a.shape
        N = b.shape[1]
    else:
        M, K = a.shape
        N = b.shape[0] if mode == "nt" else b.shape[1]
    tm = _tile(M, tm, 128 if mode == "tn" else 16)
    tn = _tile(N, tn, 128)
    dims = {"nn": NN, "nt": NT, "tn": TN}[mode]
    n_after = 0 if after is None else 1

    def body(*refs):
        refs = refs[n_after:]
        r = _dot(refs[0][...], refs[1][...], dims)
        if add is not None:
            r = r + refs[2][...]
        refs[-1][...] = r.astype(out_dtype)

    a_spec = pl.BlockSpec((K, tm), lambda i, j: (0, i)) if mode == "tn" else pl.BlockSpec((tm, K), lambda i, j: (i, 0))
    b_spec = pl.BlockSpec((tn, K), lambda i, j: (j, 0)) if mode == "nt" else pl.BlockSpec((K, tn), lambda i, j: (0, j))
    o_spec = pl.BlockSpec((tm, tn), lambda i, j: (i, j))
    in_specs = [_ANY] * n_after + [a_spec, b_spec] + ([o_spec] if add is not None else [])
    args = ((after,) if n_after else ()) + (a, b) + ((add,) if add is not None else ())
    return pl.pallas_call(
        body, name=name, grid=(M // tm, N // tn), in_specs=in_specs, out_specs=o_spec,
        out_shape=jax.ShapeDtypeStruct((M, N), out_dtype), compiler_params=_params("parallel", "parallel"),
    )(*args)


def _matmul_pair(a1, b1, a2, b2, *, name, after=None, tm=688, tn=256):
    M, K = a1.shape
    N = b1.shape[1]
    tm, tn = _tile(M, tm, 16), _tile(N, tn, 128)
    n_after = 0 if after is None else 1

    def body(*refs):
        a1_ref, b1_ref, a2_ref, b2_ref, o_ref = refs[n_after:]
        o_ref[...] = _dot(a1_ref[...], b1_ref[...]) + _dot(a2_ref[...], b2_ref[...])

    a_spec = pl.BlockSpec((tm, K), lambda i, j: (i, 0))
    b_spec = pl.BlockSpec((K, tn), lambda i, j: (0, j))
    return pl.pallas_call(
        body, name=name, grid=(M // tm, N // tn), in_specs=[_ANY] * n_after + [a_spec, b_spec, a_spec, b_spec],
        out_specs=pl.BlockSpec((tm, tn), lambda i, j: (i, j)), out_shape=jax.ShapeDtypeStruct((M, N), F32),
        compiler_params=_params("parallel", "parallel"),
    )(*((after,) if n_after else ()), a1, b1, a2, b2)


def _rmsnorm_fwd(h, w, *, name):
    M, D = h.shape
    tm = _tile(M, 688, 16)

    def body(h_ref, w_ref, n_ref):
        x = h_ref[...]
        r = lax.rsqrt(jnp.mean(x * x, axis=-1, keepdims=True) + NORM_EPS)
        n_ref[...] = (x * r * w_ref[...]).astype(n_ref.dtype)

    return pl.pallas_call(
        body, name=name, grid=(M // tm,),
        in_specs=[pl.BlockSpec((tm, D), lambda i: (i, 0)), pl.BlockSpec((1, D), lambda i: (0, 0))],
        out_specs=pl.BlockSpec((tm, D), lambda i: (i, 0)),
        out_shape=jax.ShapeDtypeStruct((M, D), BF16),
        compiler_params=_params("parallel"),
    )(h, w)


SEQ_BLOCK = HEAD_ROWS


def _seq_blocks_per_tile(rows):
    n = rows // SEQ_BLOCK
    return max(m for m in (1, 2, 3, 4) if n % m == 0)


def _seq_specs(m, D):
    return [pl.BlockSpec((SEQ_BLOCK, D), functools.partial(lambda i, k: (jnp.maximum(m * i + k - 1, 0), 0), k=k))
            for k in range(m)]


def _embed_norm(head, x, w, *, name, after=None):
    S, D = x.shape
    m = _seq_blocks_per_tile(S + HEAD_ROWS)
    n_after = 0 if after is None else 1

    def body(*refs):
        refs = refs[n_after:]
        head_ref, x_refs, w_ref, h_ref, n_ref = refs[0], refs[1:1 + m], refs[1 + m], refs[2 + m], refs[3 + m]
        i = pl.program_id(0)
        for k in range(m):
            blk = x_refs[k][...]
            if k == 0:
                blk = jnp.where(i == 0, head_ref[...], blk)
            rows = slice(k * SEQ_BLOCK, (k + 1) * SEQ_BLOCK)
            h_ref[rows, :] = blk
            r = lax.rsqrt(jnp.mean(blk * blk, axis=-1, keepdims=True) + NORM_EPS)
            n_ref[rows, :] = (blk * r * w_ref[...]).astype(n_ref.dtype)

    tile = pl.BlockSpec((m * SEQ_BLOCK, D), lambda i: (i, 0))
    return pl.pallas_call(
        body, name=name, grid=((S + HEAD_ROWS) // (m * SEQ_BLOCK),),
        in_specs=[_ANY] * n_after + [pl.BlockSpec((SEQ_BLOCK, D), lambda i: (0, 0))] + _seq_specs(m, D)
        + [pl.BlockSpec((1, D), lambda i: (0, 0))],
        out_specs=[tile, tile],
        out_shape=[jax.ShapeDtypeStruct((S + HEAD_ROWS, D), F32), jax.ShapeDtypeStruct((S + HEAD_ROWS, D), BF16)],
        compiler_params=_params("parallel"),
    )(*((after,) if n_after else ()), head, *([x] * m), w)


def _embed_norm_bwd(h, w, dn, dres, *, name):
    M, D = h.shape
    S = M - HEAD_ROWS
    m = _seq_blocks_per_tile(S)
    g = S // (m * SEQ_BLOCK)

    def one(x, dn_, dres_, w_):
        r = lax.rsqrt(jnp.mean(x * x, axis=-1, keepdims=True) + NORM_EPS)
        xhat = x * r
        dxhat = dn_ * w_
        dh = dres_ + r * (dxhat - xhat * jnp.mean(dxhat * xhat, axis=-1, keepdims=True))
        return dh, jnp.sum((dn_ * xhat).reshape(SEQ_BLOCK // 8, 8, D), axis=0)

    def body(*refs):
        w_ref = refs[0]
        groups = [refs[1 + a * (m + 1):1 + (a + 1) * (m + 1)] for a in range(3)]
        gx_ref, dhead_ref, dw_ref, acc_ref = refs[1 + 3 * (m + 1):]
        i = pl.program_id(0)
        w_ = w_ref[...]
        part = jnp.zeros((8, D), F32)
        for k in range(m):
            dh, p = one(*(grp[1 + k][...] for grp in groups), w_)
            gx_ref[k * SEQ_BLOCK:(k + 1) * SEQ_BLOCK, :] = dh
            part = part + p

        @pl.when(i == 0)
        def _():
            dh, p = one(*(grp[0][...] for grp in groups), w_)
            dhead_ref[...] = dh
            acc_ref[...] = part + p

        @pl.when(i > 0)
        def _():
            acc_ref[...] += part

        @pl.when(i == g - 1)
        def _():
            dw_ref[...] = jnp.sum(acc_ref[...], axis=0, keepdims=True)

    first = pl.BlockSpec((SEQ_BLOCK, D), lambda i: (0, 0))
    blocks = [pl.BlockSpec((SEQ_BLOCK, D), functools.partial(lambda i, k: (m * i + k + 1, 0), k=k)) for k in range(m)]
    vec = pl.BlockSpec((1, D), lambda i: (0, 0))
    return pl.pallas_call(
        body, name=name, grid=(g,), in_specs=[vec] + ([first] + blocks) * 3,
        out_specs=[pl.BlockSpec((m * SEQ_BLOCK, D), lambda i: (i, 0)), first, vec],
        out_shape=[jax.ShapeDtypeStruct((S, D), F32), jax.ShapeDtypeStruct((SEQ_BLOCK, D), F32),
                   jax.ShapeDtypeStruct((1, D), F32)],
        scratch_shapes=[pltpu.VMEM((8, D), F32)],
        compiler_params=_params("arbitrary"),
    )(w, *([h] * (m + 1)), *([dn] * (m + 1)), *([dres] * (m + 1)))


def _rmsnorm_bwd(h, w, dn, dres, *, name):
    M, D = h.shape
    tm = _tile(M, 344, 16)
    g = M // tm

    def body(h_ref, w_ref, dn_ref, dres_ref, dh_ref, dhb_ref, dw_ref, acc_ref):
        i = pl.program_id(0)
        x = h_ref[...]
        r = lax.rsqrt(jnp.mean(x * x, axis=-1, keepdims=True) + NORM_EPS)
        xhat = x * r
        dn_ = dn_ref[...]
        dxhat = dn_ * w_ref[...]
        dh = dres_ref[...] + r * (dxhat - xhat * jnp.mean(dxhat * xhat, axis=-1, keepdims=True))
        dh_ref[...] = dh
        dhb_ref[...] = dh.astype(dhb_ref.dtype)
        part = jnp.sum((dn_ * xhat).reshape(tm // 8, 8, D), axis=0)

        @pl.when(i == 0)
        def _():
            acc_ref[...] = part

        @pl.when(i > 0)
        def _():
            acc_ref[...] += part

        @pl.when(i == g - 1)
        def _():
            dw_ref[...] = jnp.sum(acc_ref[...], axis=0, keepdims=True)

    row = pl.BlockSpec((tm, D), lambda i: (i, 0))
    vec = pl.BlockSpec((1, D), lambda i: (0, 0))
    return pl.pallas_call(
        body, name=name, grid=(g,), in_specs=[row, vec, row, row],
        out_specs=[row, row, vec],
        out_shape=[jax.ShapeDtypeStruct((M, D), F32), jax.ShapeDtypeStruct((M, D), BF16),
                   jax.ShapeDtypeStruct((1, D), F32)],
        scratch_shapes=[pltpu.VMEM((8, D), F32)],
        compiler_params=_params("arbitrary"),
    )(h, w, dn, dres)


def _loss_head(h, w, target, *, name):
    M, D = h.shape
    m = _seq_blocks_per_tile(M)
    tm = m * SEQ_BLOCK
    g = M // tm

    def body(h_ref, w_ref, *rest):
        t_refs = rest[:m]
        dh_ref, dhb_ref, dw_ref, loss_ref, acc_ref, lacc_ref = rest[m:]
        i = pl.program_id(0)
        x = h_ref[...]
        row = i * tm + lax.broadcasted_iota(jnp.int32, (tm, 1), 0)
        live = row >= HEAD_ROWS
        r = lax.rsqrt(jnp.mean(x * x, axis=-1, keepdims=True) + NORM_EPS)
        xhat = x * r
        t = jnp.concatenate([t_ref[...] for t_ref in t_refs], axis=0)
        err = jnp.where(live, xhat * w_ref[...] - t, 0.0)
        dy = err * (1.0 / D)
        dxhat = dy * w_ref[...]
        dh = r * (dxhat - xhat * jnp.mean(dxhat * xhat, axis=-1, keepdims=True))
        dh_ref[...] = dh
        dhb_ref[...] = dh.astype(dhb_ref.dtype)
        part = jnp.sum((dy * xhat).reshape(tm // 8, 8, D), axis=0)
        lpart = jnp.sum((err * err).reshape(tm // 8, 8, D), axis=0)

        @pl.when(i == 0)
        def _():
            acc_ref[...] = part
            lacc_ref[...] = lpart

        @pl.when(i > 0)
        def _():
            acc_ref[...] += part
            lacc_ref[...] += lpart

        @pl.when(i == g - 1)
        def _():
            dw_ref[...] = jnp.sum(acc_ref[...], axis=0, keepdims=True)
            tot = jnp.sum(jnp.sum(lacc_ref[...], axis=0, keepdims=True), axis=1, keepdims=True)
            loss_ref[...] = jnp.broadcast_to(tot * (0.5 / D), (1, 128))

    row = pl.BlockSpec((tm, D), lambda i: (i, 0))
    vec = pl.BlockSpec((1, D), lambda i: (0, 0))
    return pl.pallas_call(
        body, name=name, grid=(g,), in_specs=[row, vec] + _seq_specs(m, D),
        out_specs=[row, row, vec, pl.BlockSpec((1, 128), lambda i: (0, 0))],
        out_shape=[jax.ShapeDtypeStruct((M, D), F32), jax.ShapeDtypeStruct((M, D), BF16),
                   jax.ShapeDtypeStruct((1, D), F32), jax.ShapeDtypeStruct((1, 128), F32)],
        scratch_shapes=[pltpu.VMEM((8, D), F32), pltpu.VMEM((8, D), F32)],
        compiler_params=_params("arbitrary"),
    )(h, w, *([target] * m))


def _gate_terms(sm, w2p, b2, alog_p, dt_p, row0):
    tm = sm.shape[0]
    lane = lax.broadcasted_iota(jnp.int32, (tm, SM_W), 1)
    live = (row0 + lax.broadcasted_iota(jnp.int32, (tm, 1), 0)) >= ROW_PAD
    pre = sm + dt_p
    neg_a = -jnp.exp(alog_p)
    g = neg_a * _softplus(pre)
    beta = _sigmoid(sm)
    z = _dot(sm, w2p) + b2
    return lane, live, pre, neg_a, g, beta, z


def _gates_fwd(proj, w2p, b2, alog_p, dt_p, *, name):
    M = proj.shape[0]
    tm = _tile(M, 688, 8)

    def body(sm_ref, w2_ref, b2_ref, al_ref, dt_ref, gb_ref, la_ref):
        row0 = pl.program_id(0) * tm
        lane, live, _, _, g, beta, z = _gate_terms(sm_ref[...], w2_ref[...], b2_ref[...], al_ref[...], dt_ref[...], row0)
        gb = jnp.where(lane < GDN_HEADS, g, jnp.where(lane < 2 * GDN_HEADS, beta, 0.0))
        gb_ref[...] = jnp.where(live, gb, 0.0)
        la = (jnp.minimum(z, 0.0) - jnp.log1p(jnp.exp(-jnp.abs(z)))) * (1.0 / GLA_GATE_NORMALIZER)
        la_ref[...] = jnp.where(live, la, 0.0)

    full = lambda s: pl.BlockSpec(s, lambda i: (0, 0))
    return pl.pallas_call(
        body, name=name, grid=(M // tm,),
        in_specs=[pl.BlockSpec((tm, SM_W), lambda i: (i, C_SM // SM_W)), full((SM_W, GLA_QK)), full((1, GLA_QK)),
                  full((1, SM_W)), full((1, SM_W))],
        out_specs=[pl.BlockSpec((tm, SM_W), lambda i: (i, 0)), pl.BlockSpec((tm, GLA_QK), lambda i: (i, 0))],
        out_shape=[jax.ShapeDtypeStruct((M, SM_W), F32), jax.ShapeDtypeStruct((M, GLA_QK), F32)],
        compiler_params=_params("parallel"),
    )(proj, w2p, b2, alog_p, dt_p)


def _gates_bwd(proj, w2p, b2, alog_p, dt_p, dgb_heads, dla, d_proj, *, name):
    M = proj.shape[0]
    tm = _tile(M, 688, 8)
    g_ = M // tm

    tail_w = D_PROJ - C_SM

    def body(sm_ref, w2_ref, b2_ref, al_ref, dt_ref, dgb_ref, dla_ref, _,
             dsm_ref, dw2_ref, db2_ref, dal_ref, ddt_ref):
        i = pl.program_id(0)
        sm = sm_ref[...]
        lane, live, pre, neg_a, g, beta, z = _gate_terms(sm, w2_ref[...], b2_ref[...], al_ref[...], dt_ref[...], i * tm)
        dz = jnp.where(live, dla_ref[...] * (_sigmoid(-z) * (1.0 / GLA_GATE_NORMALIZER)), 0.0)
        dsm_lr = _dot(dz, w2_ref[...], NT)
        dgb = dgb_ref[0]
        for hh in range(1, GDN_HEADS):
            dgb = dgb + dgb_ref[hh]
        dgb = jnp.where(live, dgb, 0.0)
        da = dgb * neg_a * _sigmoid(pre)
        db = dgb * beta * (1.0 - beta)
        dsm = jnp.where(lane < GDN_HEADS, da, jnp.where(lane < 2 * GDN_HEADS, db, dsm_lr))
        dsm_ref[:, 0:SM_W] = dsm.astype(dsm_ref.dtype)
        dsm_ref[:, SM_W:tail_w] = jnp.zeros((tm, tail_w - SM_W), dsm_ref.dtype)
        is_a = lane < GDN_HEADS
        dal = jnp.sum(jnp.where(is_a, dgb * g, 0.0), axis=0, keepdims=True)
        ddt = jnp.sum(jnp.where(is_a, da, 0.0), axis=0, keepdims=True)
        dw2 = _dot(sm, dz, TN)
        db2 = jnp.sum(dz, axis=0, keepdims=True)

        @pl.when(i == 0)
        def _():
            dw2_ref[...] = dw2
            db2_ref[...] = db2
            dal_ref[...] = dal
            ddt_ref[...] = ddt

        @pl.when(i > 0)
        def _():
            dw2_ref[...] += dw2
            db2_ref[...] += db2
            dal_ref[...] += dal
            ddt_ref[...] += ddt

    full = lambda s: pl.BlockSpec(s, lambda i: (0, 0))
    return pl.pallas_call(
        body, name=name, grid=(g_,),
        in_specs=[pl.BlockSpec((tm, SM_W), lambda i: (i, C_SM // SM_W)), full((SM_W, GLA_QK)), full((1, GLA_QK)),
                  full((1, SM_W)), full((1, SM_W)),
                  pl.BlockSpec((GDN_HEADS, tm, SM_W), lambda i: (0, i, 0)),
                  pl.BlockSpec((tm, GLA_QK), lambda i: (i, 0)), _ANY],
        out_specs=[pl.BlockSpec((tm, tail_w), lambda i: (i, C_SM // tail_w)), full((SM_W, GLA_QK)), full((1, GLA_QK)),
                   full((1, SM_W)), full((1, SM_W))],
        out_shape=[jax.ShapeDtypeStruct(d_proj.shape, d_proj.dtype), jax.ShapeDtypeStruct((SM_W, GLA_QK), F32),
                   jax.ShapeDtypeStruct((1, GLA_QK), F32), jax.ShapeDtypeStruct((1, SM_W), F32),
                   jax.ShapeDtypeStruct((1, SM_W), F32)],
        input_output_aliases={7: 0},
        compiler_params=_params("arbitrary"),
    )(proj, w2p, b2, alog_p, dt_p, dgb_heads, dla, d_proj)


QKV_W = GDN_QK
N_QKV_GROUPS = 3
QKV_B0 = C_QKV // QKV_W
HALO = 8


def _conv_terms(x_ref, halo_ref, cw_ref, xs_ref, i, tm):
    xs_ref[HALO:HALO + tm, :] = x_ref[...]
    xs_ref[0:HALO, :] = jnp.where(i > 0, halo_ref[...], 0.0)
    cw = cw_ref[...]
    xs = xs_ref[...]
    taps = [(pltpu.roll(xs, CONV_K - 1 - t, 0) if t < CONV_K - 1 else xs)[HALO:HALO + tm, :] for t in range(CONV_K)]
    c = taps[0] * cw[0:1, :]
    for t in range(1, CONV_K):
        c = c + taps[t] * cw[t:t + 1, :]
    return c, taps


def _prep_fwd(proj, conv_w8, *, name):
    M = proj.shape[0]
    tm = _tile(M, 344, 8)

    def body(x_ref, halo_ref, cw_ref, o_ref, xs_ref):
        j, i = pl.program_id(0), pl.program_id(1)
        c, _ = _conv_terms(x_ref, halo_ref, cw_ref, xs_ref, i, tm)
        s, _ = _silu_and_grad(c)
        scale = jnp.where(j == 0, GDN_DK ** -0.5, 1.0)
        for hh in range(GDN_HEADS):
            cols = slice(hh * 128, (hh + 1) * 128)
            sh = s[:, cols]
            r = lax.rsqrt(jnp.sum(sh * sh, axis=-1, keepdims=True) + NORM_EPS)
            o_ref[:, cols] = jnp.where(j < 2, sh * (r * scale), sh)

    hb = tm // HALO
    return pl.pallas_call(
        body, name=name, grid=(N_QKV_GROUPS, M // tm),
        in_specs=[pl.BlockSpec((tm, QKV_W), lambda j, i: (i, QKV_B0 + j)),
                  pl.BlockSpec((HALO, QKV_W), lambda j, i: (jnp.maximum(i * hb - 1, 0), QKV_B0 + j)),
                  pl.BlockSpec((8, QKV_W), lambda j, i: (0, j))],
        out_specs=pl.BlockSpec((tm, QKV_W), lambda j, i: (i, j)),
        out_shape=jax.ShapeDtypeStruct((M, N_QKV_GROUPS * QKV_W), F32),
        scratch_shapes=[pltpu.VMEM((tm + HALO, QKV_W), F32)],
        compiler_params=_params("parallel", "arbitrary"),
    )(proj, proj, conv_w8)


def _prep_bwd(proj, conv_w8, dact, d_proj, *, name):
    M = proj.shape[0]
    tm = _tile(M, 688, 16)
    g_ = M // tm
    ext = tm + HALO

    def body(x_ref, prev_ref, next_ref, cw_ref, da_ref, dan_ref, _, o_ref, dcw_ref, xs_ref, das_ref, dcs_ref):
        j, i = pl.program_id(0), pl.program_id(1)
        not_last = i < g_ - 1
        xs_ref[0:HALO, :] = jnp.where(i > 0, prev_ref[...], 0.0)
        xs_ref[HALO:HALO + tm, :] = x_ref[...]
        xs_ref[HALO + tm:HALO + ext, :] = jnp.where(not_last, next_ref[...], 0.0)
        das_ref[0:tm, :] = da_ref[...]
        das_ref[tm:ext, :] = jnp.where(not_last, dan_ref[...], 0.0)
        cw = cw_ref[...]
        xs = xs_ref[...]
        taps = [(pltpu.roll(xs, CONV_K - 1 - t, 0) if t < CONV_K - 1 else xs)[HALO:HALO + ext, :] for t in range(CONV_K)]
        c = taps[0] * cw[0:1, :]
        for t in range(1, CONV_K):
            c = c + taps[t] * cw[t:t + 1, :]
        s, ds_dc = _silu_and_grad(c)
        scale = jnp.where(j == 0, GDN_DK ** -0.5, 1.0)
        for hh in range(GDN_HEADS):
            cols = slice(hh * 128, (hh + 1) * 128)
            sh = s[:, cols]
            r = lax.rsqrt(jnp.sum(sh * sh, axis=-1, keepdims=True) + NORM_EPS)
            da = das_ref[:, cols]
            y = sh * r
            dy = da * scale
            ds_norm = r * (dy - y * jnp.sum(dy * y, axis=-1, keepdims=True))
            dcs_ref[:, cols] = jnp.where(j < 2, ds_norm, da) * ds_dc[:, cols]
        dc = dcs_ref[...]
        acc = dc[0:tm, :] * cw[CONV_K - 1:CONV_K, :]
        for t in range(CONV_K - 1):
            acc = acc + pltpu.roll(dc, ext - (CONV_K - 1 - t), 0)[0:tm, :] * cw[t:t + 1, :]
        o_ref[...] = acc.astype(o_ref.dtype)
        r8 = lax.broadcasted_iota(jnp.int32, (8, QKV_W), 0)
        part = jnp.zeros((8, QKV_W), F32)
        for t in range(CONV_K):
            part = jnp.where(r8 == t, jnp.sum(dc[0:tm, :] * taps[t][0:tm, :], axis=0, keepdims=True), part)

        @pl.when(i == 0)
        def _():
            dcw_ref[...] = part

        @pl.when(i > 0)
        def _():
            dcw_ref[...] += part

    hb = tm // HALO
    last = M // HALO - 1
    prev_of = lambda i: jnp.maximum(i * hb - 1, 0)
    next_of = lambda i: jnp.minimum((i + 1) * hb, last)
    return pl.pallas_call(
        body, name=name, grid=(N_QKV_GROUPS, g_),
        in_specs=[pl.BlockSpec((tm, QKV_W), lambda j, i: (i, QKV_B0 + j)),
                  pl.BlockSpec((HALO, QKV_W), lambda j, i: (prev_of(i), QKV_B0 + j)),
                  pl.BlockSpec((HALO, QKV_W), lambda j, i: (next_of(i), QKV_B0 + j)),
                  pl.BlockSpec((8, QKV_W), lambda j, i: (0, j)),
                  pl.BlockSpec((tm, QKV_W), lambda j, i: (i, j)),
                  pl.BlockSpec((HALO, QKV_W), lambda j, i: (next_of(i), j)), _ANY],
        out_specs=[pl.BlockSpec((tm, QKV_W), lambda j, i: (i, QKV_B0 + j)), pl.BlockSpec((8, QKV_W), lambda j, i: (0, j))],
        out_shape=[jax.ShapeDtypeStruct(d_proj.shape, d_proj.dtype),
                   jax.ShapeDtypeStruct((8, N_QKV_GROUPS * QKV_W), F32)],
        input_output_aliases={6: 0},
        scratch_shapes=[pltpu.VMEM((HALO + ext, QKV_W), F32), pltpu.VMEM((ext, QKV_W), F32), pltpu.VMEM((ext, QKV_W), F32)],
        compiler_params=_params("parallel", "arbitrary"),
    )(proj, proj, proj, conv_w8, dact, dact, d_proj)


def _round_robin(gens):
    gens = list(gens)
    while gens:
        alive = []
        for gen in gens:
            try:
                next(gen)
                alive.append(gen)
            except StopIteration:
                pass
        gens = alive


def _unit_lower_inverse(a_low, eye):
    n = a_low.shape[0]
    ri = lax.broadcasted_iota(jnp.int32, (n, n), 0)
    ci = lax.broadcasted_iota(jnp.int32, (n, n), 1)
    same = lambda shift: (ri >> shift) == (ci >> shift)
    b = jnp.where(same(3), -a_low, 0.0)
    x = eye + b
    p2 = _dot3(b, b)
    yield
    x = x + _dot3(x, p2)
    p4 = _dot3(p2, p2)
    yield
    x = x + _dot3(x, p4)
    yield
    for shift in (3, 4, 5):
        between = jnp.where(same(shift + 1) & ~same(shift), a_low, 0.0)
        t = _dot3(between, x)
        yield
        x = x - _dot3(x, t)
        yield
    return x


class _GdnChunk:
    def build(self, q, k, v, gb, h):
        C = GDN_CHUNK
        lane = lax.broadcasted_iota(jnp.int32, (C, SM_W), 1)
        g = jnp.sum(jnp.where(lane == h, gb, 0.0), axis=1, keepdims=True)
        self.beta = jnp.sum(jnp.where(lane == h + GDN_HEADS, gb, 0.0), axis=1, keepdims=True)
        ri = lax.broadcasted_iota(jnp.int32, (C, C), 0)
        ci = lax.broadcasted_iota(jnp.int32, (C, C), 1)
        self.causal = ri >= ci
        self.strict = ri > ci
        self.eye = (ri == ci).astype(F32)
        gcb = _dotx(self.causal.astype(F32), jnp.broadcast_to(g, (C, SM_W)))
        yield
        self.gcol = gcb[:, 0:1]
        grow = gcb.T[0:1, 0:C]
        self.decay = jnp.exp(jnp.where(self.causal, self.gcol - grow, -1e30))
        self.egc = jnp.exp(self.gcol)
        glast = gcb[C - 1:C, 0:1]
        self.elast = jnp.exp(glast - self.gcol)
        self.gl = jnp.exp(glast)
        self.q, self.k, self.v = q, k, v
        self.kb = k * self.beta
        m = _dot(self.kb, k, NT)
        n_ = _dot(q, k, NT)
        yield
        self.a_low = jnp.where(self.strict, m * self.decay, 0.0)
        self.p = n_ * self.decay
        self.qd = q * self.egc
        self.kd = k * self.elast
        self.bu = v * self.beta
        self.bw = self.kb * self.egc


GDN_HB = 8
GDN_HG = GDN_HEADS // GDN_HB


def _gdn_specs(n_of):
    C, W = GDN_CHUNK, 128 * GDN_HB
    q_spec = pl.BlockSpec((C, W), lambda g, n: (n_of(n), g))
    k_spec = pl.BlockSpec((C, W), lambda g, n: (n_of(n), g + GDN_HG))
    v_spec = pl.BlockSpec((C, W), lambda g, n: (n_of(n), g + 2 * GDN_HG))
    gb_spec = pl.BlockSpec((C, SM_W), lambda g, n: (n_of(n), 0))
    o_spec = pl.BlockSpec((C, W), lambda g, n: (n_of(n), g))
    s_spec = pl.BlockSpec((GDN_HB, None, GDN_DK, GDN_DV), lambda g, n: (g, n_of(n), 0, 0))
    t_spec = pl.BlockSpec((GDN_HB, None, C, C), lambda g, n: (g, n_of(n), 0, 0))
    return q_spec, k_spec, v_spec, gb_spec, o_spec, s_spec, t_spec


def _gdn_fwd(act, gb, *, name):
    M = act.shape[0]
    N = M // GDN_CHUNK

    def body(q_ref, k_ref, v_ref, gb_ref, o_ref, s_ref, t_ref, state):
        g, n = pl.program_id(0), pl.program_id(1)

        @pl.when(n == 0)
        def _():
            state[...] = jnp.zeros_like(state)

        gb_ = gb_ref[...]

        def head(hh):
            cols = slice(hh * 128, (hh + 1) * 128)
            c = _GdnChunk()
            yield from c.build(q_ref[:, cols], k_ref[:, cols], v_ref[:, cols], gb_, g * GDN_HB + hh)
            tinv = yield from _unit_lower_inverse(c.a_low, c.eye)
            s = state[hh]
            s_ref[hh] = s
            t_ref[hh] = tinv
            u = _dot(tinv, c.bu)
            w = _dot(tinv, c.bw)
            yield
            vn = u - _dot(w, s)
            o1 = _dot(c.qd, s)
            yield
            o_ref[:, cols] = o1 + _dot(c.p, vn)
            state[hh] = c.gl * s + _dot(c.kd, vn, TN)

        _round_robin(head(hh) for hh in range(GDN_HB))

    q_spec, k_spec, v_spec, gb_spec, o_spec, s_spec, t_spec = _gdn_specs(lambda n: n)
    return pl.pallas_call(
        body, name=name, grid=(GDN_HG, N),
        in_specs=[q_spec, k_spec, v_spec, gb_spec], out_specs=[o_spec, s_spec, t_spec],
        out_shape=[jax.ShapeDtypeStruct((M, GDN_V), F32),
                   jax.ShapeDtypeStruct((GDN_HEADS, N, GDN_DK, GDN_DV), F32),
                   jax.ShapeDtypeStruct((GDN_HEADS, N, GDN_CHUNK, GDN_CHUNK), F32)],
        scratch_shapes=[pltpu.VMEM((GDN_HB, GDN_DK, GDN_DV), F32)],
        compiler_params=_params("parallel", "arbitrary"),
    )(act, act, act, gb)


def _gdn_bwd(act, gb, do, s_all, t_all, *, name):
    M = act.shape[0]
    N = M // GDN_CHUNK
    C = GDN_CHUNK
    assert GDN_HG == 1

    def body(q_ref, k_ref, v_ref, gb_ref, do_ref, s_ref, t_ref, dact_ref, dgb_ref, dstate):
        g, n = pl.program_id(0), pl.program_id(1)

        @pl.when(n == 0)
        def _():
            dstate[...] = jnp.zeros_like(dstate)

        gb_ = gb_ref[...]
        last = lax.broadcasted_iota(jnp.int32, (C, 1), 0) == C - 1
        upper = (lax.broadcasted_iota(jnp.int32, (C, C), 0) <= lax.broadcasted_iota(jnp.int32, (C, C), 1)).astype(F32)
        lane = lax.broadcasted_iota(jnp.int32, (C, SM_W), 1)
        def head(hh):
            cols = slice(hh * 128, (hh + 1) * 128)
            h = g * GDN_HB + hh
            c = _GdnChunk()
            yield from c.build(q_ref[:, cols], k_ref[:, cols], v_ref[:, cols], gb_, h)
            tinv = t_ref[hh]
            tinv_t = tinv.T
            s = s_ref[hh]
            do_ = do_ref[:, cols]
            ds1 = dstate[hh]
            u = _dot(tinv, c.bu)
            w = _dot(tinv, c.bw)
            dqd = _dot(do_, s, NT)
            dvn0 = _dot(c.p, do_, TN) + _dot(c.kd, ds1)
            dst0 = _dot(c.qd, do_, TN) + c.gl * ds1
            yield
            vn = u - _dot(w, s)
            dvn = dvn0
            yield
            dp = jnp.where(c.causal, _dot(do_, vn, NT), 0.0)
            dstate[hh] = dst0 - _dot(w, dvn, TN)
            dkd = _dot(vn, ds1, NT)
            dw = -_dot(dvn, s, NT)
            dbu = _dot(tinv_t, dvn)
            dgl = jnp.sum(jnp.sum(s * ds1, axis=1, keepdims=True), axis=0, keepdims=True)
            yield
            dbw = _dot(tinv_t, dw)
            t1 = _dot(dbu, u, NT)
            yield
            da = jnp.where(c.strict, -(t1 + _dot(dbw, w, NT)), 0.0)
            dn_ = dp * c.decay
            dq0 = _dot(dn_, c.k)
            dk0 = _dot(dn_, c.q, TN)
            yield
            dm = da * c.decay
            e = da * c.a_low + dp * c.p
            dkb = _dot(dm, c.k) + dbw * c.egc
            dact_ref[:, GDN_QK + hh * 128:GDN_QK + (hh + 1) * 128] = (
                _dot(dm, c.kb, TN) + dk0 + dkb * c.beta + dkd * c.elast)
            dact_ref[:, cols] = dq0 + dqd * c.egc
            dact_ref[:, 2 * GDN_QK + hh * 128:2 * GDN_QK + (hh + 1) * 128] = dbu * c.beta
            dbeta = jnp.sum(dbu * c.v, axis=1, keepdims=True) + jnp.sum(dkb * c.k, axis=1, keepdims=True)
            t_kd = jnp.sum(dkd * c.kd, axis=1, keepdims=True)
            dgc = (jnp.sum(e, axis=1, keepdims=True) - jnp.sum(e.T, axis=1, keepdims=True)
                   + jnp.sum(dbw * c.bw, axis=1, keepdims=True) + jnp.sum(dqd * c.qd, axis=1, keepdims=True) - t_kd)
            dgc = dgc + jnp.where(last, jnp.sum(t_kd, axis=0, keepdims=True) + dgl * c.gl, 0.0)
            yield
            dg = _dotx(upper, jnp.broadcast_to(dgc, (C, SM_W)))
            dgb_ref[hh] = jnp.where(lane == h, dg, jnp.where(lane == h + GDN_HEADS, dbeta, 0.0))

        _round_robin(head(hh) for hh in range(GDN_HB))

    rev = lambda n: N - 1 - n
    q_spec, k_spec, v_spec, gb_spec, o_spec, s_spec, t_spec = _gdn_specs(rev)
    dgb_spec = pl.BlockSpec((GDN_HB, C, SM_W), lambda g, n: (g, rev(n), 0))
    return pl.pallas_call(
        body, name=name, grid=(GDN_HG, N),
        in_specs=[q_spec, k_spec, v_spec, gb_spec, o_spec, s_spec, t_spec],
        out_specs=[pl.BlockSpec((C, 2 * GDN_QK + GDN_V), lambda g, n: (rev(n), 0)), dgb_spec],
        out_shape=[jax.ShapeDtypeStruct((M, 2 * GDN_QK + GDN_V), F32),
                   jax.ShapeDtypeStruct((GDN_HEADS, M, SM_W), F32)],
        scratch_shapes=[pltpu.VMEM((GDN_HB, GDN_DK, GDN_DV), F32)],
        compiler_params=_params("parallel", "arbitrary"),
    )(act, act, act, gb, do, s_all, t_all)


GLA_STEP_ROWS = 64
GLA_SUB = GLA_STEP_ROWS // GLA_CHUNK


def _gla_cumsum(la):
    C = GLA_CHUNK
    ltri = (lax.broadcasted_iota(jnp.int32, (C, C), 0) >= lax.broadcasted_iota(jnp.int32, (C, C), 1)).astype(F32)
    return _dotx(ltri, la)


GLA_HALF = GLA_CHUNK // 2


def _gla_cross_factors(b):
    top = lax.broadcasted_iota(jnp.int32, b.shape, 0) < GLA_HALF
    bm = b[GLA_HALF - 1:GLA_HALF, :]
    late = jnp.where(top, 0.0, jnp.exp(jnp.minimum(b - bm, 0.0)))
    early = jnp.where(top, jnp.exp(jnp.minimum(bm - b, 0.0)), 0.0)
    return late, early


def _gla_half_decay(bh, ii):
    rj = lax.broadcasted_iota(jnp.int32, bh.shape, 0)
    return jnp.where(rj <= ii, jnp.exp(jnp.minimum(bh[ii:ii + 1, :] - bh, 0.0)), 0.0)


def _gla_scores_t(q, k, b):
    C, H = GLA_CHUNK, GLA_HALF
    lane = lax.broadcasted_iota(jnp.int32, (H, C), 1)
    halves = []
    for h0 in (0, H):
        qh, kh, bh = q[h0:h0 + H], k[h0:h0 + H], b[h0:h0 + H]
        sth = jnp.zeros((H, C), F32)
        for ii in range(H):
            si = jnp.sum(qh[ii:ii + 1, :] * kh * _gla_half_decay(bh, ii), axis=1, keepdims=True)
            sth = jnp.where(lane == h0 + ii, si, sth)
            if ii % 4 == 3:
                yield
        halves.append(sth)
    late, early = _gla_cross_factors(b)
    between = _dot(k * early, q * late, NT)
    yield
    return jnp.concatenate(halves, axis=0) + between


def _gla_specs(n_of):
    R = GLA_STEP_ROWS
    q_spec = pl.BlockSpec((R, GLA_QK), lambda n: (n_of(n), C_GQ // GLA_QK))
    k_spec = pl.BlockSpec((R, GLA_QK), lambda n: (n_of(n), C_GK // GLA_QK))
    v_spec = pl.BlockSpec((R, GLA_V), lambda n: (n_of(n), C_GV // GLA_V))
    la_spec = pl.BlockSpec((R, GLA_QK), lambda n: (n_of(n), 0))
    o_spec = pl.BlockSpec((R, GLA_V), lambda n: (n_of(n), 0))
    s_spec = pl.BlockSpec((GLA_HEADS, None, GLA_SUB, GLA_DV, GLA_DK), lambda n: (0, n_of(n), 0, 0, 0))
    return q_spec, k_spec, v_spec, la_spec, o_spec, s_spec


def _gla_fwd(proj, la, *, name):
    M = proj.shape[0]
    N = M // GLA_STEP_ROWS
    C = GLA_CHUNK

    def body(q_ref, k_ref, v_ref, la_ref, o_ref, s_ref, state):
        n = pl.program_id(0)

        @pl.when(n == 0)
        def _():
            state[...] = jnp.zeros_like(state)

        local = {}

        def within(hh, c):
            kc = slice(hh * GLA_DK, (hh + 1) * GLA_DK)
            vc = slice(hh * GLA_DV, (hh + 1) * GLA_DV)
            rows = slice(c * C, (c + 1) * C)
            q = q_ref[rows, kc] * (GLA_DK ** -0.5)
            k = k_ref[rows, kc]
            v = v_ref[rows, vc]
            b = _gla_cumsum(la_ref[rows, kc])
            yield
            blast = b[C - 1:C, :]
            sc_t = yield from _gla_scores_t(q, k, b)
            kv = _dot(v, k * jnp.exp(blast - b), TN)
            o2 = _dot(sc_t, v, TN)
            yield
            local[hh, c] = (q * jnp.exp(b), jnp.exp(blast), kv, o2)

        def across(hh):
            vc = slice(hh * GLA_DV, (hh + 1) * GLA_DV)
            st = state[hh]
            for c in range(GLA_SUB):
                qe, eblast, kv, o2 = local[hh, c]
                s_ref[hh, c] = st
                o1 = _dot(qe, st, NT)
                yield
                o_ref[c * C:(c + 1) * C, vc] = o1 + o2
                st = st * eblast + kv
            state[hh] = st

        _round_robin(within(hh, c) for c in range(GLA_SUB) for hh in range(GLA_HEADS))
        _round_robin(across(hh) for hh in range(GLA_HEADS))

    q_spec, k_spec, v_spec, la_spec, o_spec, s_spec = _gla_specs(lambda n: n)
    return pl.pallas_call(
        body, name=name, grid=(N,),
        in_specs=[q_spec, k_spec, v_spec, la_spec], out_specs=[o_spec, s_spec],
        out_shape=[jax.ShapeDtypeStruct((M, GLA_V), F32),
                   jax.ShapeDtypeStruct((GLA_HEADS, N, GLA_SUB, GLA_DV, GLA_DK), F32)],
        scratch_shapes=[pltpu.VMEM((GLA_HEADS, GLA_DV, GLA_DK), F32)],
        compiler_params=_params("arbitrary"),
    )(proj, proj, proj, la)


def _gla_bwd(proj, la, do, s_all, d_proj, *, name):
    M = proj.shape[0]
    N = M // GLA_STEP_ROWS
    C = GLA_CHUNK
    qkv_w = 2 * GLA_QK + GLA_V
    assert C_GK == C_GQ + GLA_QK and C_GV == C_GK + GLA_QK and C_GQ % qkv_w == 0

    def body(q_ref, k_ref, v_ref, la_ref, do_ref, s_ref, _, dp_ref, dla_ref, dstate):
        n = pl.program_id(0)

        @pl.when(n == 0)
        def _():
            dstate[...] = jnp.zeros_like(dstate)

        H = GLA_HALF
        lane = lax.broadcasted_iota(jnp.int32, (C, C), 1)
        row = lax.broadcasted_iota(jnp.int32, (C, C), 0)
        ri = lax.broadcasted_iota(jnp.int32, (C, GLA_DK), 0)
        lane_h = lax.broadcasted_iota(jnp.int32, (H, C), 1)
        ri_h = lax.broadcasted_iota(jnp.int32, (H, GLA_DK), 0)
        cross = (row < H) & (lane >= H)
        upper = (row <= lane).astype(F32)
        def head(hh):
            kc = slice(hh * GLA_DK, (hh + 1) * GLA_DK)
            vc = slice(hh * GLA_DV, (hh + 1) * GLA_DV)
            ds1 = dstate[hh]
            for c in reversed(range(GLA_SUB)):
                rows = slice(c * C, (c + 1) * C)
                q = q_ref[rows, kc] * (GLA_DK ** -0.5)
                k = k_ref[rows, kc]
                v = v_ref[rows, vc]
                b = _gla_cumsum(la_ref[rows, kc])
                do_ = do_ref[rows, vc]
                st = s_ref[hh, c]
                dsc_t = _dot(v, do_, NT)
                dqe = _dot(do_, st)
                dke = _dot(v, ds1)
                yield
                blast = b[C - 1:C, :]
                eb = jnp.exp(b)
                elast = jnp.exp(blast - b)
                eblast = jnp.exp(blast)
                qe = q * eb
                ke = k * elast
                dv2 = _dot(ke, ds1, NT)
                ds_new = _dot(do_, qe, TN)
                deblast = jnp.sum(st * ds1, axis=0, keepdims=True)
                sc_halves, dq_halves, dk_halves = [], [], []
                for h0 in (0, H):
                    qh, kh, bh, dsch = q[h0:h0 + H], k[h0:h0 + H], b[h0:h0 + H], dsc_t[h0:h0 + H]
                    sch = jnp.zeros((H, C), F32)
                    dqh = jnp.zeros((H, GLA_DK), F32)
                    dkh = jnp.zeros((H, GLA_DK), F32)
                    for ii in range(H):
                        f = _gla_half_decay(bh, ii)
                        kf = kh * f
                        si = jnp.sum(qh[ii:ii + 1, :] * kf, axis=1, keepdims=True)
                        sch = jnp.where(lane_h == h0 + ii, si, sch)
                        dsi = jnp.sum(jnp.where(lane_h == h0 + ii, dsch, 0.0), axis=1, keepdims=True)
                        dqh = jnp.where(ri_h == ii, jnp.sum(dsi * kf, axis=0, keepdims=True), dqh)
                        dkh = dkh + (dsi * f) * qh[ii:ii + 1, :]
                        if ii % 4 == 3:
                            yield
                    sc_halves.append(sch)
                    dq_halves.append(dqh)
                    dk_halves.append(dkh)
                late, early = _gla_cross_factors(b)
                q_late, k_early = q * late, k * early
                dsc_x = jnp.where(cross, dsc_t, 0.0)
                sc_t = jnp.concatenate(sc_halves, axis=0) + _dot(k_early, q_late, NT)
                dq_sc = jnp.concatenate(dq_halves, axis=0) + _dot(dsc_x, k_early, TN) * late
                dk_sc = jnp.concatenate(dk_halves, axis=0) + _dot(dsc_x, q_late) * early
                yield
                dv1 = _dot(sc_t, do_)
                dp_ref[rows, kc] = ((dq_sc + dqe * eb) * (GLA_DK ** -0.5)).astype(dp_ref.dtype)
                dp_ref[rows, GLA_QK + hh * GLA_DK:GLA_QK + (hh + 1) * GLA_DK] = (dk_sc + dke * elast).astype(dp_ref.dtype)
                t_ke = dke * ke
                db = q * dq_sc - k * dk_sc + dqe * qe - t_ke
                db = db + jnp.where(ri == C - 1, jnp.sum(t_ke, axis=0, keepdims=True) + deblast * eblast, 0.0)
                dla = _dotx(upper, db)
                yield
                dp_ref[rows, 2 * GLA_QK + hh * GLA_DV:2 * GLA_QK + (hh + 1) * GLA_DV] = (dv1 + dv2).astype(dp_ref.dtype)
                dla_ref[rows, kc] = dla
                ds1 = ds1 * eblast + ds_new
            dstate[hh] = ds1

        _round_robin(head(hh) for hh in range(GLA_HEADS))

    rev = lambda n: N - 1 - n
    q_spec, k_spec, v_spec, la_spec, o_spec, s_spec = _gla_specs(rev)
    return pl.pallas_call(
        body, name=name, grid=(N,),
        in_specs=[q_spec, k_spec, v_spec, la_spec, o_spec, s_spec, _ANY],
        out_specs=[pl.BlockSpec((GLA_STEP_ROWS, qkv_w), lambda n: (rev(n), C_GQ // qkv_w)), la_spec],
        out_shape=[jax.ShapeDtypeStruct(d_proj.shape, d_proj.dtype), jax.ShapeDtypeStruct((M, GLA_QK), F32)],
        input_output_aliases={6: 0},
        scratch_shapes=[pltpu.VMEM((GLA_HEADS, GLA_DV, GLA_DK), F32)],
        compiler_params=_params("arbitrary"),
    )(proj, proj, proj, la, do, s_all, d_proj)


def _head_norm(o, wn):
    r = lax.rsqrt(jnp.mean(o * o, axis=-1, keepdims=True) + NORM_EPS)
    return o * r, r


def _mix_heads():
    heads = [(0, GDN_DV, hh * GDN_DV, hh * GDN_DV) for hh in range(GDN_HEADS)]
    heads += [(1, GLA_DV, GDN_V + hh * GLA_DV, hh * GLA_DV) for hh in range(GLA_HEADS)]
    return heads


def _mix_fwd(o_gdn, o_gla, proj, wn_gdn, wn_gla, *, name):
    M = proj.shape[0]
    tm = _tile(M, 344, 16)

    def body(og_ref, ol_ref, z_ref, r_ref, wg_ref, wl_ref, m_ref):
        srcs = ((og_ref, z_ref, wg_ref), (ol_ref, r_ref, wl_ref))
        for grp, width, mcol, col in _mix_heads():
            o_ref, gate_ref, w_ref = srcs[grp]
            xhat, _ = _head_norm(o_ref[:, col:col + width], None)
            gate, _ = _silu_and_grad(gate_ref[:, col:col + width])
            m_ref[:, mcol:mcol + width] = (xhat * w_ref[...] * gate).astype(m_ref.dtype)

    full = lambda s: pl.BlockSpec(s, lambda i: (0, 0))
    return pl.pallas_call(
        body, name=name, grid=(M // tm,),
        in_specs=[pl.BlockSpec((tm, GDN_V), lambda i: (i, 0)), pl.BlockSpec((tm, GLA_V), lambda i: (i, 0)),
                  pl.BlockSpec((tm, GDN_V), lambda i: (i, C_Z // GDN_V)),
                  pl.BlockSpec((tm, GLA_V), lambda i: (i, C_GR // GLA_V)),
                  full((1, GDN_DV)), full((1, GLA_DV))],
        out_specs=pl.BlockSpec((tm, D_MODEL), lambda i: (i, 0)),
        out_shape=jax.ShapeDtypeStruct((M, D_MODEL), BF16),
        compiler_params=_params("parallel"),
    )(o_gdn, o_gla, proj, proj, wn_gdn, wn_gla)


def _mix_bwd(o_gdn, o_gla, proj, wn_gdn, wn_gla, dmixed, *, name):
    M = proj.shape[0]
    tm = _tile(M, 344, 16)
    g_ = M // tm
    assert C_Z == 0 and C_GR == GDN_V

    def body(og_ref, ol_ref, z_ref, r_ref, wg_ref, wl_ref, dm_ref,
             dog_ref, dol_ref, dzr_ref, dwg_ref, dwl_ref):
        i = pl.program_id(0)
        srcs = ((og_ref, z_ref, wg_ref, dog_ref), (ol_ref, r_ref, wl_ref, dol_ref))
        dws = [jnp.zeros((1, GDN_DV), F32), jnp.zeros((1, GLA_DV), F32)]
        for grp, width, mcol, col in _mix_heads():
            o_ref, gate_ref, w_ref, do_ref = srcs[grp]
            cols = slice(col, col + width)
            xhat, r = _head_norm(o_ref[:, cols], None)
            gate, dgate_dc = _silu_and_grad(gate_ref[:, cols])
            dm = dm_ref[:, mcol:mcol + width]
            dzr_ref[:, mcol:mcol + width] = (dm * xhat * w_ref[...] * dgate_dc).astype(dzr_ref.dtype)
            dnorm = dm * gate
            dws[grp] = dws[grp] + jnp.sum(dnorm * xhat, axis=0, keepdims=True)
            dxhat = dnorm * w_ref[...]
            do_ref[:, cols] = r * (dxhat - xhat * jnp.mean(dxhat * xhat, axis=-1, keepdims=True))

        @pl.when(i == 0)
        def _():
            dwg_ref[...] = dws[0]
            dwl_ref[...] = dws[1]

        @pl.when(i > 0)
        def _():
            dwg_ref[...] += dws[0]
            dwl_ref[...] += dws[1]

    full = lambda s: pl.BlockSpec(s, lambda i: (0, 0))
    half = pl.BlockSpec((tm, GDN_V), lambda i: (i, 0))
    return pl.pallas_call(
        body, name=name, grid=(g_,),
        in_specs=[half, half, pl.BlockSpec((tm, GDN_V), lambda i: (i, C_Z // GDN_V)),
                  pl.BlockSpec((tm, GLA_V), lambda i: (i, C_GR // GLA_V)),
                  full((1, GDN_DV)), full((1, GLA_DV)), pl.BlockSpec((tm, D_MODEL), lambda i: (i, 0))],
        out_specs=[half, half, pl.BlockSpec((tm, GDN_V + GLA_V), lambda i: (i, 0)),
                   full((1, GDN_DV)), full((1, GLA_DV))],
        out_shape=[jax.ShapeDtypeStruct((M, GDN_V), F32), jax.ShapeDtypeStruct((M, GLA_V), F32),
                   jax.ShapeDtypeStruct((M, D_PROJ), BF16),
                   jax.ShapeDtypeStruct((1, GDN_DV), F32), jax.ShapeDtypeStruct((1, GLA_DV), F32)],
        compiler_params=_params("arbitrary"),
    )(o_gdn, o_gla, proj, proj, wn_gdn, wn_gla, dmixed)


def _swiglu_fwd(n, w_gate_t, w_up_t, *, name, tm=1376, tn=512):
    M, D = n.shape
    F = w_gate_t.shape[0]
    tm, tn = _tile(M, tm, 16), _tile(F, tn, 128)

    def body(n_ref, wg_ref, wu_ref, g_ref, u_ref, a_ref):
        x = n_ref[...]
        g = _dot(x, wg_ref[...], NT)
        u = _dot(x, wu_ref[...], NT)
        s, _ = _silu_and_grad(g)
        g_ref[...] = g.astype(g_ref.dtype)
        u_ref[...] = u.astype(u_ref.dtype)
        a_ref[...] = (s * u).astype(a_ref.dtype)

    w_spec = pl.BlockSpec((tn, D), lambda i, j: (j, 0))
    o_spec = pl.BlockSpec((tm, tn), lambda i, j: (i, j))
    return pl.pallas_call(
        body, name=name, grid=(M // tm, F // tn),
        in_specs=[pl.BlockSpec((tm, D), lambda i, j: (i, 0)), w_spec, w_spec], out_specs=[o_spec] * 3,
        out_shape=[jax.ShapeDtypeStruct((M, F), BF16)] * 3, compiler_params=_params("parallel", "parallel"),
    )(n, w_gate_t, w_up_t)


def _swiglu_bwd(dh, w_down, gate, up, *, name, after=None, tm=1376, tn=512):
    M, D = dh.shape
    F = w_down.shape[0]
    tm, tn = _tile(M, tm, 16), _tile(F, tn, 128)
    n_after = 0 if after is None else 1

    def body(*refs):
        dh_ref, w_ref, g_ref, u_ref, dg_ref, du_ref = refs[n_after:]
        da = _dot(dh_ref[...], w_ref[...], NT)
        s, ds = _silu_and_grad(g_ref[...].astype(F32))
        dg_ref[...] = (da * u_ref[...].astype(F32) * ds).astype(dg_ref.dtype)
        du_ref[...] = (da * s).astype(du_ref.dtype)

    o_spec = pl.BlockSpec((tm, tn), lambda i, j: (i, j))
    return pl.pallas_call(
        body, name=name, grid=(M // tm, F // tn),
        in_specs=[_ANY] * n_after + [pl.BlockSpec((tm, D), lambda i, j: (i, 0)),
                                     pl.BlockSpec((tn, D), lambda i, j: (j, 0)), o_spec, o_spec],
        out_specs=[o_spec, o_spec], out_shape=[jax.ShapeDtypeStruct((M, F), BF16)] * 2,
        compiler_params=_params("parallel", "parallel"),
    )(*((after,) if n_after else ()), dh, w_down, gate, up)


def _adamw(w, g, m, v, *, name):
    shape = w.shape
    cols = shape[-1]
    rows = w.size // cols
    w2, g2, m2, v2 = (t.reshape(rows, cols) for t in (w, g, m, v))
    if rows % 8 == 0 or cols % 128 != 0:
        tr, tc = (_tile(rows, 256, 8) if rows % 8 == 0 else rows), cols
    else:
        tr, tc = rows, _tile(cols, 256, 128)

    def body(w_ref, g_ref, m_ref, v_ref, d_ref, nm_ref, nv_ref):
        g_ = g_ref[...]
        nm = ADAM_B1 * m_ref[...] + (1.0 - ADAM_B1) * g_
        nv = ADAM_B2 * v_ref[...] + (1.0 - ADAM_B2) * (g_ * g_)
        m_hat = nm / (1.0 - ADAM_B1 ** ADAM_STEP)
        v_hat = nv / (1.0 - ADAM_B2 ** ADAM_STEP)
        d_ref[...] = -ADAM_LR * (m_hat / (jnp.sqrt(v_hat) + ADAM_EPS) + ADAM_WD * w_ref[...])
        nm_ref[...] = nm
        nv_ref[...] = nv

    blk = pl.BlockSpec((tr, tc), lambda i, j: (i, j))
    outs = pl.pallas_call(
        body, name=name, grid=(rows // tr, cols // tc), in_specs=[blk] * 4, out_specs=[blk] * 3,
        out_shape=[jax.ShapeDtypeStruct((rows, cols), F32)] * 3, compiler_params=_params("parallel", "parallel"),
    )(w2, g2, m2, v2)
    return tuple(t.reshape(shape) for t in outs)


def _sum_slabs(x, *, name):
    _, R, C = x.shape
    sub = 16 if x.dtype == BF16 else 8
    if R % sub == 0:
        tr, tc = _tile(R, 128, sub), C
    else:
        tr, tc = R, _tile(C, 256, 128)

    def body(x_ref, o_ref):
        acc = x_ref[0].astype(F32)
        for s in range(1, N_DEV):
            acc = acc + x_ref[s].astype(F32)
        o_ref[...] = acc

    return pl.pallas_call(
        body, name=name, grid=(R // tr, C // tc),
        in_specs=[pl.BlockSpec((N_DEV, tr, tc), lambda i, j: (0, i, j))],
        out_specs=pl.BlockSpec((tr, tc), lambda i, j: (i, j)),
        out_shape=jax.ShapeDtypeStruct((R, C), F32), compiler_params=_params("parallel", "parallel"),
    )(x)


def _peers():
    x, y, c = lax.axis_index("x"), lax.axis_index("y"), lax.axis_index("c")
    me = 4 * x + 2 * y + c
    peers = []
    for k in range(1, N_DEV):
        px = 1 - x if k & 4 else x
        py = 1 - y if k & 2 else y
        pc = 1 - c if k & 1 else c
        peers.append(((px, py, pc), 4 * px + 2 * py + pc))
    return me, peers


def _gather(x, *, name, after=None):
    n_after = 0 if after is None else 1

    def body(*refs):
        x_ref, o_ref, send_sems, recv_sems, own_sem = refs[n_after:]
        me, peers = _peers()
        own = pltpu.make_async_copy(x_ref, o_ref.at[me], own_sem)
        own.start()
        sends, recvs = [], []
        for k, (pos, idx) in enumerate(peers):
            sends.append(pltpu.make_async_remote_copy(
                src_ref=x_ref, dst_ref=o_ref.at[me], send_sem=send_sems.at[k], recv_sem=recv_sems.at[k],
                device_id=pos, device_id_type=pl.DeviceIdType.MESH))
            recvs.append(pltpu.make_async_remote_copy(
                src_ref=x_ref, dst_ref=o_ref.at[idx], send_sem=send_sems.at[k], recv_sem=recv_sems.at[k],
                device_id=pos, device_id_type=pl.DeviceIdType.MESH))
        for cp in sends:
            cp.start()
        for cp in recvs:
            cp.wait_recv()
        for cp in sends:
            cp.wait_send()
        own.wait()

    hbm = pl.BlockSpec(memory_space=pltpu.HBM)
    return pl.pallas_call(
        body, name=name, in_specs=[_ANY] * n_after + [hbm], out_specs=hbm,
        out_shape=jax.ShapeDtypeStruct((N_DEV,) + tuple(x.shape), x.dtype),
        scratch_shapes=[pltpu.SemaphoreType.DMA((N_DEV - 1,)), pltpu.SemaphoreType.DMA((N_DEV - 1,)),
                        pltpu.SemaphoreType.DMA],
    )(*((after,) if n_after else ()), x)


_HBM = pl.BlockSpec(memory_space=pltpu.HBM)
_SEM = pl.BlockSpec(memory_space=pltpu.SEMAPHORE)
_EFFECT = pltpu.SideEffectType.DATAFLOW_SIDE_EFFECTING


PLAN_GATHER = tuple((k, "x", 0) for k in range(1, N_DEV))
PLAN_SCATTER = tuple((k, "xk", 0) for k in range(1, N_DEV))
PLAN_GATHER_CHIPS = tuple((k, "x", 0) for k in (1, 2, 4, 6))
PLAN_GATHER_PASS_ON = tuple((1, ("land", q), q) for q in (2, 4, 6))


def _plan_refs(plan, j, x_ref, land_ref, me, peers, receiving):
    k, source, r = plan[j]
    index_of = lambda q: me if q == 0 else peers[q - 1][1]
    pos, target = peers[k - 1]
    if source == "x":
        src = x_ref
    elif source == "xk":
        src = x_ref.at[target]
    else:
        src = land_ref.at[index_of(source[1])]
    return pos, src, land_ref.at[index_of(k ^ r) if receiving else index_of(r)]


def _exchange_start(x, *, plan, name, after=None, land=None, slab=None):
    n_after = 0 if after is None else 1
    n = len(plan)

    def body(*refs):
        x_ref, land_ref, send_sems, recv_sems, _, _, token = refs[n_after:]
        me, peers = _peers()
        for j in range(n):
            pos, src, dst = _plan_refs(plan, j, x_ref, land_ref, me, peers, receiving=False)
            pltpu.make_async_remote_copy(src_ref=src, dst_ref=dst, send_sem=send_sems.at[j], recv_sem=recv_sems.at[j],
                                         device_id=pos, device_id_type=pl.DeviceIdType.MESH).start()
        token[...] = jnp.zeros_like(token)

    if land is None:
        land = lax.empty((N_DEV,) + tuple(slab), x.dtype)
    return pl.pallas_call(
        body, name=name,
        out_shape=(pltpu.SemaphoreType.DMA((n,)), pltpu.SemaphoreType.DMA((n,)),
                   pltpu.HBM(x.shape, x.dtype), pltpu.HBM(land.shape, land.dtype), jax.ShapeDtypeStruct((8, 128), F32)),
        in_specs=[_ANY] * n_after + [_HBM, _HBM],
        out_specs=(_SEM, _SEM, _HBM, _HBM, pl.BlockSpec(memory_space=pltpu.VMEM)),
        input_output_aliases={n_after: 2, n_after + 1: 3},
        compiler_params=pltpu.CompilerParams(has_side_effects=_EFFECT),
    )(*((after,) if n_after else ()), pltpu.with_memory_space_constraint(x, pltpu.HBM),
      pltpu.with_memory_space_constraint(land, pltpu.HBM))


def _exchange_wait(handle, after, *, plan, name):
    send_sems, recv_sems, x_thru, land_thru, _ = handle
    afters = list(after) if isinstance(after, (list, tuple)) else [after]

    def body(x_ref, land_ref, send_sems, recv_sems, *rest):
        me, peers = _peers()
        for j in range(len(plan)):
            pos, src, dst = _plan_refs(plan, j, x_ref, land_ref, me, peers, receiving=True)
            cp = pltpu.make_async_remote_copy(src_ref=src, dst_ref=dst, send_sem=send_sems.at[j], recv_sem=recv_sems.at[j],
                                              device_id=pos, device_id_type=pl.DeviceIdType.MESH)
            cp.wait_send()
            cp.wait_recv()

    return pl.pallas_call(
        body, name=name,
        out_shape=(pltpu.HBM(x_thru.shape, x_thru.dtype), pltpu.HBM(land_thru.shape, land_thru.dtype)),
        in_specs=[_HBM, _HBM, _SEM, _SEM] + [_ANY] * len(afters), out_specs=(_HBM, _HBM),
        input_output_aliases={0: 0, 1: 1}, compiler_params=pltpu.CompilerParams(has_side_effects=_EFFECT),
    )(x_thru, land_thru, send_sems, recv_sems, *afters)


W_IN_SLAB = D_IN // N_DEV


def _to_proj_rows(t):
    z = jnp.zeros((D_PROJ - C_SM - 2 * GDN_HEADS - GLA_RANK,) + t.shape[1:], t.dtype)
    return jnp.concatenate([t[R_Z:R_A], t[R_GR:R_LR], t[R_GQ:R_GR], t[:R_Z], t[R_A:R_GQ], t[R_LR:], z], axis=0)


def _from_proj_rows(t):
    ab = C_SM + 2 * GDN_HEADS
    return jnp.concatenate([t[C_QKV:C_SM], t[C_Z:C_GR], t[C_SM:ab], t[C_GQ:C_QKV], t[C_GR:C_GQ],
                            t[ab:ab + GLA_RANK]], axis=0)


def _local_step(x, target, meta, attn_nw, conv_w, a_log, dt_bias, gdn_nw, w2, b2, gla_nw, ffn_nw, final_nw,
                fetch, emit, start=None):
    S = x.shape[0]
    head = jnp.concatenate([jnp.zeros((ROW_PAD, D_MODEL), F32), meta], axis=0)
    conv_w8 = jnp.concatenate([conv_w, jnp.zeros((8 - CONV_K, conv_w.shape[1]), F32)], axis=0)
    w2p = jnp.zeros((SM_W, GLA_QK), F32).at[2 * GDN_HEADS:2 * GDN_HEADS + GLA_RANK].set(w2)
    alog_p = jnp.zeros((1, SM_W), F32).at[:, :GDN_HEADS].set(a_log)
    dt_p = jnp.zeros((1, SM_W), F32).at[:, :GDN_HEADS].set(dt_bias)

    h0, n1 = _embed_norm(head, x, attn_nw, name="attn_norm", after=start)
    w_in_t = fetch("w_in_t", (n1, conv_w8, w2p, alog_p, dt_p))
    proj = _matmul(n1, w_in_t, mode="nt", name="in_proj")
    gb, la = _gates_fwd(proj, w2p, b2, alog_p, dt_p, name="gates")
    act = _prep_fwd(proj, conv_w8, name="gdn_prep")
    o_gdn, s_gdn, t_gdn = _gdn_fwd(act, gb, name="gdn_fwd")
    o_gla, s_gla = _gla_fwd(proj, la, name="gla_fwd")
    mixed = _mix_fwd(o_gdn, o_gla, proj, gdn_nw, gla_nw, name="mix")
    w_out = fetch("w_out", mixed)
    h1 = _matmul(mixed, w_out, mode="nn", add=h0, name="out_proj")
    n2 = _rmsnorm_fwd(h1, ffn_nw, name="ffn_norm")
    w_gate_t, w_up_t = fetch("w_gate_t", n2), fetch("w_up_t", n2)
    gate, up, hid = _swiglu_fwd(n2, w_gate_t, w_up_t, name="swiglu")
    w_down = fetch("w_down", hid)
    h2 = _matmul(hid, w_down, mode="nn", add=h1, name="ffn_down", tm=688)
    dh2, dh2_b, d_final_nw, loss = _loss_head(h2, final_nw, target, name="loss_head")

    wg = dict(mode="tn", out_dtype=BF16, tn=512)
    tok = emit("w_down", _matmul(hid, dh2_b, name="d_w_down", tm=704, **wg))
    d_gate, d_up = _swiglu_bwd(dh2_b, w_down, gate, up, name="d_swiglu", after=tok)
    tok = emit("w_gate_t", _matmul(d_gate, n2, name="d_w_gate", tm=704, **wg))
    tok = emit("w_up_t", _matmul(d_up, n2, name="d_w_up", tm=704, after=tok, **wg))
    d_n2 = _matmul_pair(d_gate, w_gate_t, d_up, w_up_t, name="d_n2", after=tok)
    dh1, dh1_b, d_ffn_nw = _rmsnorm_bwd(h1, ffn_nw, d_n2, dh2, name="d_ffn_norm")

    tok = emit("w_out", _matmul(mixed, dh1_b, name="d_w_out", tm=512, **wg))
    d_mixed = _matmul(dh1_b, w_out, mode="nt", name="d_mixed", after=tok)
    do_gdn, do_gla, d_proj, d_gdn_nw, d_gla_nw = _mix_bwd(o_gdn, o_gla, proj, gdn_nw, gla_nw, d_mixed, name="d_mix")
    d_proj, d_la = _gla_bwd(proj, la, do_gla, s_gla, d_proj, name="gla_bwd")
    dact, dgb_heads = _gdn_bwd(act, gb, do_gdn, s_gdn, t_gdn, name="gdn_bwd")
    d_proj, d_w2p, d_b2, d_alog, d_dt = _gates_bwd(proj, w2p, b2, alog_p, dt_p, dgb_heads, d_la, d_proj, name="d_gates")
    d_proj, d_conv_w8 = _prep_bwd(proj, conv_w8, dact, d_proj, name="d_gdn_prep")
    tok = emit("w_in_t", _matmul(d_proj, n1, name="d_w_in", tm=768, **wg))
    d_n1 = _matmul(d_proj, w_in_t, mode="nn", name="d_n1", tm=688, after=tok)
    grad_x, d_head, d_attn_nw = _embed_norm_bwd(h0, attn_nw, d_n1, dh1, name="d_attn_norm")

    return dict(
        loss=loss[0, 0], grad_x=grad_x, meta=d_head[ROW_PAD:HEAD_ROWS], attn_nw=d_attn_nw,
        conv_w=d_conv_w8[:CONV_K], a_log=d_alog[:, :GDN_HEADS], dt_bias=d_dt[:, :GDN_HEADS], gdn_nw=d_gdn_nw,
        w2=d_w2p[2 * GDN_HEADS:2 * GDN_HEADS + GLA_RANK], b2=d_b2, gla_nw=d_gla_nw, ffn_nw=d_ffn_nw,
        final_nw=d_final_nw)


SMALL_ROWS = 32


def kernel(x, meta_tokens, attn_norm_w, w_in, gdn_conv_w, gdn_a_log, gdn_dt_bias, gdn_norm_w, gla_gate_w2, gla_gate_b, gla_norm_w, w_out, ffn_norm_w, w_gate, w_up, w_down, final_norm_w, loss_target, m_meta_tokens, m_attn_norm_w, m_w_in, m_gdn_conv_w, m_gdn_a_log, m_gdn_dt_bias, m_gdn_norm_w, m_gla_gate_w2, m_gla_gate_b, m_gla_norm_w, m_w_out, m_ffn_norm_w, m_w_gate, m_w_up, m_w_down, m_final_norm_w, v_meta_tokens, v_attn_norm_w, v_w_in, v_gdn_conv_w, v_gdn_a_log, v_gdn_dt_bias, v_gdn_norm_w, v_gla_gate_w2, v_gla_gate_b, v_gla_norm_w, v_w_out, v_ffn_norm_w, v_w_gate, v_w_up, v_w_down, v_final_norm_w):
    me = 4 * lax.axis_index("x") + 2 * lax.axis_index("y") + lax.axis_index("c")

    w_in_slab = w_in[0].T.astype(BF16)
    in_h = _exchange_start(w_in_slab, plan=PLAN_GATHER_CHIPS, slab=w_in_slab.shape, name="gather_w_in_start")

    n_conv = gdn_conv_w.shape[2]
    n_w2 = gla_gate_w2.shape[2]
    n_meta = meta_tokens.shape[1]
    small = jnp.zeros((40, n_conv), F32)
    small = small.at[0:N_META, :n_meta].set(meta_tokens)
    small = small.at[N_META:N_META + CONV_K, :].set(gdn_conv_w[0])
    small = small.at[24:24 + GLA_RANK, :n_w2].set(gla_gate_w2[0])
    small_all = _gather(small, name="gather_small", after=in_h[4])
    meta_f = small_all[:, 0:N_META, :n_meta].transpose(1, 0, 2).reshape(N_META, D_MODEL)
    conv_f = small_all[:, N_META:N_META + CONV_K, :].transpose(1, 0, 2).reshape(CONV_K, N_DEV * n_conv)
    w2_f = small_all[:, 24:24 + GLA_RANK, :n_w2].transpose(1, 0, 2).reshape(GLA_RANK, N_DEV * n_w2)

    handles, tok = {}, in_h[4]
    for wname, slab in (("w_out", w_out[0]), ("w_gate_t", w_gate[0].T), ("w_up_t", w_up[0].T), ("w_down", w_down[0])):
        slab = slab.astype(BF16)
        handles[wname] = _exchange_start(slab, plan=PLAN_GATHER, slab=slab.shape, name="gather_" + wname + "_start", after=tok)
        tok = handles[wname][4]

    def fetch(name, after):
        if name == "w_in_t":
            own, got = _exchange_wait(in_h, after, plan=PLAN_GATHER_CHIPS, name="gather_w_in_wait")
            pass_h = _exchange_start(own, plan=PLAN_GATHER_PASS_ON, land=got, name="pass_w_in_start")
            own, got = _exchange_wait(pass_h, pass_h[4], plan=PLAN_GATHER_PASS_ON, name="pass_w_in_wait")
            got = lax.dynamic_update_index_in_dim(got, own, me, 0)
            return _to_proj_rows(got.reshape(D_IN, D_MODEL))
        own, got = _exchange_wait(handles[name], after, plan=PLAN_GATHER, name="gather_" + name + "_wait")
        got = lax.dynamic_update_index_in_dim(got, own, me, 0)
        return got.reshape(N_DEV * got.shape[1], D_MODEL)

    sent = {}

    def emit(name, grad):
        if name == "w_in_t":
            grad = _from_proj_rows(grad)
        parts = grad.reshape(N_DEV, grad.shape[0] // N_DEV, D_MODEL)
        sent[name] = _exchange_start(parts, plan=PLAN_SCATTER, slab=parts.shape[1:], name="scatter_" + name + "_start")
        return sent[name][4]

    g = _local_step(x[0], loss_target[0], meta_f, attn_norm_w, conv_f, gdn_a_log, gdn_dt_bias, gdn_norm_w, w2_f,
                    gla_gate_b, gla_norm_w, ffn_norm_w, final_norm_w.reshape(1, D_MODEL), fetch, emit, start=tok)

    def total(name, after):
        handle = sent[name]
        own, got = _exchange_wait(handle, after, plan=PLAN_SCATTER, name="scatter_" + name + "_wait")
        got = lax.dynamic_update_index_in_dim(got, lax.dynamic_index_in_dim(own, me, 0, keepdims=False), me, 0)
        return _sum_slabs(got, name="sum_" + name)

    grad_w_down = total("w_down", g["attn_nw"])[None]
    grad_w_gate = total("w_gate_t", grad_w_down)
    grad_w_up = total("w_up_t", grad_w_gate)
    grad_w_out = total("w_out", grad_w_up)[None]
    grad_w_in = total("w_in_t", grad_w_out)

    misc = jnp.concatenate([g["a_log"], g["dt_bias"], g["gdn_nw"], g["gla_nw"], g["b2"], g["loss"].reshape(1, 1)], axis=1)
    n_misc = misc.shape[1]
    misc = jnp.pad(misc, ((0, 0), (0, D_MODEL - n_misc)))
    rows = jnp.concatenate([g["attn_nw"], g["ffn_nw"], g["final_nw"], misc, g["meta"],
                            g["conv_w"].reshape(-1, D_MODEL), g["w2"].reshape(-1, D_MODEL)], axis=0)
    rows = jnp.pad(rows, ((0, SMALL_ROWS - rows.shape[0]), (0, 0)))
    tot = _sum_slabs(_gather(rows, name="gather_small_grads"), name="sum_small_grads")
    grad_attn_nw, grad_ffn_nw, grad_final_nw = tot[0:1], tot[1:2], tot[2]
    grad_a_log = tot[3:4, 0:8]
    grad_dt = tot[3:4, 8:16]
    grad_gdn_nw = tot[3:4, 16:16 + GDN_DV]
    grad_gla_nw = tot[3:4, 144:144 + GLA_DV]
    grad_b2 = tot[3:4, 400:400 + GLA_QK]
    loss = tot[3, n_misc - 1]
    r0 = 4 + N_META
    grad_meta = lax.dynamic_slice(tot[4:r0], (0, me * n_meta), (N_META, n_meta))
    r1 = r0 + CONV_K * N_DEV * n_conv // D_MODEL
    grad_conv = lax.dynamic_slice(tot[r0:r1].reshape(CONV_K, N_DEV * n_conv), (0, me * n_conv), (CONV_K, n_conv))[None]
    r2 = r1 + GLA_RANK * N_DEV * n_w2 // D_MODEL
    grad_w2 = lax.dynamic_slice(tot[r1:r2].reshape(GLA_RANK, N_DEV * n_w2), (0, me * n_w2), (GLA_RANK, n_w2))[None]

    weights = [meta_tokens, attn_norm_w, w_in, gdn_conv_w, gdn_a_log, gdn_dt_bias, gdn_norm_w, gla_gate_w2,
               gla_gate_b, gla_norm_w, w_out, ffn_norm_w, w_gate, w_up, w_down, final_norm_w]
    grads = [grad_meta, grad_attn_nw, grad_w_in, grad_conv, grad_a_log, grad_dt, grad_gdn_nw, grad_w2,
             grad_b2, grad_gla_nw, grad_w_out, grad_ffn_nw, grad_w_gate, grad_w_up, grad_w_down, grad_final_nw]
    ms = [m_meta_tokens, m_attn_norm_w, m_w_in, m_gdn_conv_w, m_gdn_a_log, m_gdn_dt_bias, m_gdn_norm_w,
          m_gla_gate_w2, m_gla_gate_b, m_gla_norm_w, m_w_out, m_ffn_norm_w, m_w_gate, m_w_up, m_w_down, m_final_norm_w]
    vs = [v_meta_tokens, v_attn_norm_w, v_w_in, v_gdn_conv_w, v_gdn_a_log, v_gdn_dt_bias, v_gdn_norm_w,
          v_gla_gate_w2, v_gla_gate_b, v_gla_norm_w, v_w_out, v_ffn_norm_w, v_w_gate, v_w_up, v_w_down, v_final_norm_w]
    transposed = (2, 12, 13)
    outs = [[], [], [], []]
    for idx, (w, gr, m, v) in enumerate(zip(weights, grads, ms, vs)):
        if idx in transposed:
            res = (gr,) + _adamw(w[0].T, gr, m[0].T, v[0].T, name=f"adamw_{idx}")
            res = [t.T[None] for t in res]
        else:
            gr = gr.reshape(w.shape)
            res = (gr,) + _adamw(w, gr, m, v, name=f"adamw_{idx}")
        for lst, t in zip(outs, res):
            lst.append(t)
    return (loss, g["grad_x"][None], *outs[0], *outs[1], *outs[2], *outs[3])
```

```python
import functools

import jax
import jax.numpy as jnp
from jax import lax
from jax.experimental import pallas as pl
from jax.experimental.pallas import tpu as pltpu

F32 = jnp.float32
BF16 = jnp.bfloat16
_MXU_DTYPE = jnp.bfloat16

D_MODEL = 2048
N_META = 16
ROW_PAD = 48
HEAD_ROWS = ROW_PAD + N_META
CONV_K = 4
GDN_HEADS, GDN_DK, GDN_DV, GDN_CHUNK = 8, 128, 128, 64
GLA_HEADS, GLA_DK, GLA_DV, GLA_CHUNK = 4, 128, 256, 16
GLA_RANK = 16
GLA_GATE_NORMALIZER = 16.0
GDN_QK = GDN_HEADS * GDN_DK
GDN_V = GDN_HEADS * GDN_DV
GLA_QK = GLA_HEADS * GLA_DK
GLA_V = GLA_HEADS * GLA_DV
D_FF = 5632
D_IN = 7200
NORM_EPS = 1e-6
C_Z, C_GR, C_GQ, C_GK, C_GV, C_QKV, C_SM = 0, 1024, 2048, 2560, 3072, 4096, 7168
SM_W = 128
D_PROJ = 7680
R_Z, R_A, R_B, R_GQ, R_GK, R_GV, R_GR, R_LR = 3072, 4096, 4104, 4112, 4624, 5136, 6160, 7184

ADAM_LR, ADAM_B1, ADAM_B2, ADAM_EPS, ADAM_WD, ADAM_STEP = 0.001, 0.9, 0.999, 1e-08, 0.01, 10

N_DEV = 8
VMEM_LIMIT = 56 * 1024 * 1024

NN = (((1,), (0,)), ((), ()))
NT = (((1,), (1,)), ((), ()))
TN = (((0,), (0,)), ((), ()))


def _dot(a, b, dims=NN):
    return lax.dot_general(a.astype(_MXU_DTYPE), b.astype(_MXU_DTYPE), dims, preferred_element_type=F32)


def _dotx(a, b, dims=NN):
    return lax.dot_general(a, b, dims, precision=lax.Precision.HIGHEST, preferred_element_type=F32)


def _dot3(a, b):
    ah = a.astype(BF16)
    al = (a - ah.astype(F32)).astype(BF16)
    bh = b.astype(BF16)
    bl = (b - bh.astype(F32)).astype(BF16)
    d = functools.partial(lax.dot_general, dimension_numbers=NN, preferred_element_type=F32)
    return d(ah, bh) + (d(ah, bl) + d(al, bh))


def _tile(n, target, mult=8):
    best = None
    for t in range(mult, min(n, target) + 1, mult):
        if n % t == 0:
            best = t
    return best if best is not None else n


def _params(*sem):
    return pltpu.CompilerParams(dimension_semantics=sem, vmem_limit_bytes=VMEM_LIMIT)


def _sigmoid(x):
    return 0.5 * jnp.tanh(0.5 * x) + 0.5


def _softplus(x):
    return jnp.maximum(x, 0.0) + jnp.log1p(jnp.exp(-jnp.abs(x)))


def _silu_and_grad(c):
    s = _sigmoid(c)
    return c * s, s * (1.0 + c * (1.0 - s))


_ANY = pl.BlockSpec(memory_space=pl.ANY)


def _matmul(a, b, *, mode, name, out_dtype=F32, add=None, after=None, tm=1376, tn=512):
    if mode == "tn":
        K, M = a.shape
        N = b.shape[1]
    else:
        M, K = a.shape
        N = b.shape[0] if mode == "nt" else b.shape[1]
    tm = _tile(M, tm, 128 if mode == "tn" else 16)
    tn = _tile(N, tn, 128)
    dims = {"nn": NN, "nt": NT, "tn": TN}[mode]
    n_after = 0 if after is None else 1

    def body(*refs):
        refs = refs[n_after:]
        r = _dot(refs[0][...], refs[1][...], dims)
        if add is not None:
            r = r + refs[2][...]
        refs[-1][...] = r.astype(out_dtype)

    a_spec = pl.BlockSpec((K, tm), lambda i, j: (0, i)) if mode == "tn" else pl.BlockSpec((tm, K), lambda i, j: (i, 0))
    b_spec = pl.BlockSpec((tn, K), lambda i, j: (j, 0)) if mode == "nt" else pl.BlockSpec((K, tn), lambda i, j: (0, j))
    o_spec = pl.BlockSpec((tm, tn), lambda i, j: (i, j))
    in_specs = [_ANY] * n_after + [a_spec, b_spec] + ([o_spec] if add is not None else [])
    args = ((after,) if n_after else ()) + (a, b) + ((add,) if add is not None else ())
    return pl.pallas_call(
        body, name=name, grid=(M // tm, N // tn), in_specs=in_specs, out_specs=o_spec,
        out_shape=jax.ShapeDtypeStruct((M, N), out_dtype), compiler_params=_params("parallel", "parallel"),
    )(*args)


def _matmul_pair(a1, b1, a2, b2, *, name, after=None, tm=688, tn=256):
    M, K = a1.shape
    N = b1.shape[1]
    tm, tn = _tile(M, tm, 16), _tile(N, tn, 128)
    n_after = 0 if after is None else 1

    def body(*refs):
        a1_ref, b1_ref, a2_ref, b2_ref, o_ref = refs[n_after:]
        o_ref[...] = _dot(a1_ref[...], b1_ref[...]) + _dot(a2_ref[...], b2_ref[...])

    a_spec = pl.BlockSpec((tm, K), lambda i, j: (i, 0))
    b_spec = pl.BlockSpec((K, tn), lambda i, j: (0, j))
    return pl.pallas_call(
        body, name=name, grid=(M // tm, N // tn), in_specs=[_ANY] * n_after + [a_spec, b_spec, a_spec, b_spec],
        out_specs=pl.BlockSpec((tm, tn), lambda i, j: (i, j)), out_shape=jax.ShapeDtypeStruct((M, N), F32),
        compiler_params=_params("parallel", "parallel"),
    )(*((after,) if n_after else ()), a1, b1, a2, b2)


def _rmsnorm_fwd(h, w, *, name):
    M, D = h.shape
    tm = _tile(M, 688, 16)

    def body(h_ref, w_ref, n_ref):
        x = h_ref[...]
        r = lax.rsqrt(jnp.mean(x * x, axis=-1, keepdims=True) + NORM_EPS)
        n_ref[...] = (x * r * w_ref[...]).astype(n_ref.dtype)

    return pl.pallas_call(
        body, name=name, grid=(M // tm,),
        in_specs=[pl.BlockSpec((tm, D), lambda i: (i, 0)), pl.BlockSpec((1, D), lambda i: (0, 0))],
        out_specs=pl.BlockSpec((tm, D), lambda i: (i, 0)),
        out_shape=jax.ShapeDtypeStruct((M, D), BF16),
        compiler_params=_params("parallel"),
    )(h, w)


SEQ_BLOCK = HEAD_ROWS


def _seq_blocks_per_tile(rows):
    n = rows // SEQ_BLOCK
    return max(m for m in (1, 2, 3, 4) if n % m == 0)


def _seq_specs(m, D):
    return [pl.BlockSpec((SEQ_BLOCK, D), functools.partial(lambda i, k: (jnp.maximum(m * i + k - 1, 0), 0), k=k))
            for k in range(m)]


def _embed_norm(head, x, w, *, name, after=None):
    S, D = x.shape
    m = _seq_blocks_per_tile(S + HEAD_ROWS)
    n_after = 0 if after is None else 1

    def body(*refs):
        refs = refs[n_after:]
        head_ref, x_refs, w_ref, h_ref, n_ref = refs[0], refs[1:1 + m], refs[1 + m], refs[2 + m], refs[3 + m]
        i = pl.program_id(0)
        for k in range(m):
            blk = x_refs[k][...]
            if k == 0:
                blk = jnp.where(i == 0, head_ref[...], blk)
            rows = slice(k * SEQ_BLOCK, (k + 1) * SEQ_BLOCK)
            h_ref[rows, :] = blk
            r = lax.rsqrt(jnp.mean(blk * blk, axis=-1, keepdims=True) + NORM_EPS)
            n_ref[rows, :] = (blk * r * w_ref[...]).astype(n_ref.dtype)

    tile = pl.BlockSpec((m * SEQ_BLOCK, D), lambda i: (i, 0))
    return pl.pallas_call(
        body, name=name, grid=((S + HEAD_ROWS) // (m * SEQ_BLOCK),),
        in_specs=[_ANY] * n_after + [pl.BlockSpec((SEQ_BLOCK, D), lambda i: (0, 0))] + _seq_specs(m, D)
        + [pl.BlockSpec((1, D), lambda i: (0, 0))],
        out_specs=[tile, tile],
        out_shape=[jax.ShapeDtypeStruct((S + HEAD_ROWS, D), F32), jax.ShapeDtypeStruct((S + HEAD_ROWS, D), BF16)],
        compiler_params=_params("parallel"),
    )(*((after,) if n_after else ()), head, *([x] * m), w)


def _embed_norm_bwd(h, w, dn, dres, *, name):
    M, D = h.shape
    S = M - HEAD_ROWS
    m = _seq_blocks_per_tile(S)
    g = S // (m * SEQ_BLOCK)

    def one(x, dn_, dres_, w_):
        r = lax.rsqrt(jnp.mean(x * x, axis=-1, keepdims=True) + NORM_EPS)
        xhat = x * r
        dxhat = dn_ * w_
        dh = dres_ + r * (dxhat - xhat * jnp.mean(dxhat * xhat, axis=-1, keepdims=True))
        return dh, jnp.sum((dn_ * xhat).reshape(SEQ_BLOCK // 8, 8, D), axis=0)

    def body(*refs):
        w_ref = refs[0]
        groups = [refs[1 + a * (m + 1):1 + (a + 1) * (m + 1)] for a in range(3)]
        gx_ref, dhead_ref, dw_ref, acc_ref = refs[1 + 3 * (m + 1):]
        i = pl.program_id(0)
        w_ = w_ref[...]
        part = jnp.zeros((8, D), F32)
        for k in range(m):
            dh, p = one(*(grp[1 + k][...] for grp in groups), w_)
            gx_ref[k * SEQ_BLOCK:(k + 1) * SEQ_BLOCK, :] = dh
            part = part + p

        @pl.when(i == 0)
        def _():
            dh, p = one(*(grp[0][...] for grp in groups), w_)
            dhead_ref[...] = dh
            acc_ref[...] = part + p

        @pl.when(i > 0)
        def _():
            acc_ref[...] += part

        @pl.when(i == g - 1)
        def _():
            dw_ref[...] = jnp.sum(acc_ref[...], axis=0, keepdims=True)

    first = pl.BlockSpec((SEQ_BLOCK, D), lambda i: (0, 0))
    blocks = [pl.BlockSpec((SEQ_BLOCK, D), functools.partial(lambda i, k: (m * i + k + 1, 0), k=k)) for k in range(m)]
    vec = pl.BlockSpec((1, D), lambda i: (0, 0))
    return pl.pallas_call(
        body, name=name, grid=(g,), in_specs=[vec] + ([first] + blocks) * 3,
        out_specs=[pl.BlockSpec((m * SEQ_BLOCK, D), lambda i: (i, 0)), first, vec],
        out_shape=[jax.ShapeDtypeStruct((S, D), F32), jax.ShapeDtypeStruct((SEQ_BLOCK, D), F32),
                   jax.ShapeDtypeStruct((1, D), F32)],
        scratch_shapes=[pltpu.VMEM((8, D), F32)],
        compiler_params=_params("arbitrary"),
    )(w, *([h] * (m + 1)), *([dn] * (m + 1)), *([dres] * (m + 1)))


def _rmsnorm_bwd(h, w, dn, dres, *, name):
    M, D = h.shape
    tm = _tile(M, 344, 16)
    g = M // tm

    def body(h_ref, w_ref, dn_ref, dres_ref, dh_ref, dhb_ref, dw_ref, acc_ref):
        i = pl.program_id(0)
        x = h_ref[...]
        r = lax.rsqrt(jnp.mean(x * x, axis=-1, keepdims=True) + NORM_EPS)
        xhat = x * r
        dn_ = dn_ref[...]
        dxhat = dn_ * w_ref[...]
        dh = dres_ref[...] + r * (dxhat - xhat * jnp.mean(dxhat * xhat, axis=-1, keepdims=True))
        dh_ref[...] = dh
        dhb_ref[...] = dh.astype(dhb_ref.dtype)
        part = jnp.sum((dn_ * xhat).reshape(tm // 8, 8, D), axis=0)

        @pl.when(i == 0)
        def _():
            acc_ref[...] = part

        @pl.when(i > 0)
        def _():
            acc_ref[...] += part

        @pl.when(i == g - 1)
        def _():
            dw_ref[...] = jnp.sum(acc_ref[...], axis=0, keepdims=True)

    row = pl.BlockSpec((tm, D), lambda i: (i, 0))
    vec = pl.BlockSpec((1, D), lambda i: (0, 0))
    return pl.pallas_call(
        body, name=name, grid=(g,), in_specs=[row, vec, row, row],
        out_specs=[row, row, vec],
        out_shape=[jax.ShapeDtypeStruct((M, D), F32), jax.ShapeDtypeStruct((M, D), BF16),
                   jax.ShapeDtypeStruct((1, D), F32)],
        scratch_shapes=[pltpu.VMEM((8, D), F32)],
        compiler_params=_params("arbitrary"),
    )(h, w, dn, dres)


def _loss_head(h, w, target, *, name):
    M, D = h.shape
    m = _seq_blocks_per_tile(M)
    tm = m * SEQ_BLOCK
    g = M // tm

    def body(h_ref, w_ref, *rest):
        t_refs = rest[:m]
        dh_ref, dhb_ref, dw_ref, loss_ref, acc_ref, lacc_ref = rest[m:]
        i = pl.program_id(0)
        x = h_ref[...]
        row = i * tm + lax.broadcasted_iota(jnp.int32, (tm, 1), 0)
        live = row >= HEAD_ROWS
        r = lax.rsqrt(jnp.mean(x * x, axis=-1, keepdims=True) + NORM_EPS)
        xhat = x * r
        t = jnp.concatenate([t_ref[...] for t_ref in t_refs], axis=0)
        err = jnp.where(live, xhat * w_ref[...] - t, 0.0)
        dy = err * (1.0 / D)
        dxhat = dy * w_ref[...]
        dh = r * (dxhat - xhat * jnp.mean(dxhat * xhat, axis=-1, keepdims=True))
        dh_ref[...] = dh
        dhb_ref[...] = dh.astype(dhb_ref.dtype)
        part = jnp.sum((dy * xhat).reshape(tm // 8, 8, D), axis=0)
        lpart = jnp.sum((err * err).reshape(tm // 8, 8, D), axis=0)

        @pl.when(i == 0)
        def _():
            acc_ref[...] = part
            lacc_ref[...] = lpart

        @pl.when(i > 0)
        def _():
            acc_ref[...] += part
            lacc_ref[...] += lpart

        @pl.when(i == g - 1)
        def _():
            dw_ref[...] = jnp.sum(acc_ref[...], axis=0, keepdims=True)
            tot = jnp.sum(jnp.sum(lacc_ref[...], axis=0, keepdims=True), axis=1, keepdims=True)
            loss_ref[...] = jnp.broadcast_to(tot * (0.5 / D), (1, 128))

    row = pl.BlockSpec((tm, D), lambda i: (i, 0))
    vec = pl.BlockSpec((1, D), lambda i: (0, 0))
    return pl.pallas_call(
        body, name=name, grid=(g,), in_specs=[row, vec] + _seq_specs(m, D),
        out_specs=[row, row, vec, pl.BlockSpec((1, 128), lambda i: (0, 0))],
        out_shape=[jax.ShapeDtypeStruct((M, D), F32), jax.ShapeDtypeStruct((M, D), BF16),
                   jax.ShapeDtypeStruct((1, D), F32), jax.ShapeDtypeStruct((1, 128), F32)],
        scratch_shapes=[pltpu.VMEM((8, D), F32), pltpu.VMEM((8, D), F32)],
        compiler_params=_params("arbitrary"),
    )(h, w, *([target] * m))


def _gate_terms(sm, w2p, b2, alog_p, dt_p, row0):
    tm = sm.shape[0]
    lane = lax.broadcasted_iota(jnp.int32, (tm, SM_W), 1)
    live = (row0 + lax.broadcasted_iota(jnp.int32, (tm, 1), 0)) >= ROW_PAD
    pre = sm + dt_p
    neg_a = -jnp.exp(alog_p)
    g = neg_a * _softplus(pre)
    beta = _sigmoid(sm)
    z = _dot(sm, w2p) + b2
    return lane, live, pre, neg_a, g, beta, z


def _gates_fwd(proj, w2p, b2, alog_p, dt_p, *, name):
    M = proj.shape[0]
    tm = _tile(M, 688, 8)

    def body(sm_ref, w2_ref, b2_ref, al_ref, dt_ref, gb_ref, la_ref):
        row0 = pl.program_id(0) * tm
        lane, live, _, _, g, beta, z = _gate_terms(sm_ref[...], w2_ref[...], b2_ref[...], al_ref[...], dt_ref[...], row0)
        gb = jnp.where(lane < GDN_HEADS, g, jnp.where(lane < 2 * GDN_HEADS, beta, 0.0))
        gb_ref[...] = jnp.where(live, gb, 0.0)
        la = (jnp.minimum(z, 0.0) - jnp.log1p(jnp.exp(-jnp.abs(z)))) * (1.0 / GLA_GATE_NORMALIZER)
        la_ref[...] = jnp.where(live, la, 0.0)

    full = lambda s: pl.BlockSpec(s, lambda i: (0, 0))
    return pl.pallas_call(
        body, name=name, grid=(M // tm,),
        in_specs=[pl.BlockSpec((tm, SM_W), lambda i: (i, C_SM // SM_W)), full((SM_W, GLA_QK)), full((1, GLA_QK)),
                  full((1, SM_W)), full((1, SM_W))],
        out_specs=[pl.BlockSpec((tm, SM_W), lambda i: (i, 0)), pl.BlockSpec((tm, GLA_QK), lambda i: (i, 0))],
        out_shape=[jax.ShapeDtypeStruct((M, SM_W), F32), jax.ShapeDtypeStruct((M, GLA_QK), F32)],
        compiler_params=_params("parallel"),
    )(proj, w2p, b2, alog_p, dt_p)


def _gates_bwd(proj, w2p, b2, alog_p, dt_p, dgb_heads, dla, d_proj, *, name):
    M = proj.shape[0]
    tm = _tile(M, 688, 8)
    g_ = M // tm

    tail_w = D_PROJ - C_SM

    def body(sm_ref, w2_ref, b2_ref, al_ref, dt_ref, dgb_ref, dla_ref, _,
             dsm_ref, dw2_ref, db2_ref, dal_ref, ddt_ref):
        i = pl.program_id(0)
        sm = sm_ref[...]
        lane, live, pre, neg_a, g, beta, z = _gate_terms(sm, w2_ref[...], b2_ref[...], al_ref[...], dt_ref[...], i * tm)
        dz = jnp.where(live, dla_ref[...] * (_sigmoid(-z) * (1.0 / GLA_GATE_NORMALIZER)), 0.0)
        dsm_lr = _dot(dz, w2_ref[...], NT)
        dgb = dgb_ref[0]
        for hh in range(1, GDN_HEADS):
            dgb = dgb + dgb_ref[hh]
        dgb = jnp.where(live, dgb, 0.0)
        da = dgb * neg_a * _sigmoid(pre)
        db = dgb * beta * (1.0 - beta)
        dsm = jnp.where(lane < GDN_HEADS, da, jnp.where(lane < 2 * GDN_HEADS, db, dsm_lr))
        dsm_ref[:, 0:SM_W] = dsm.astype(dsm_ref.dtype)
        dsm_ref[:, SM_W:tail_w] = jnp.zeros((tm, tail_w - SM_W), dsm_ref.dtype)
        is_a = lane < GDN_HEADS
        dal = jnp.sum(jnp.where(is_a, dgb * g, 0.0), axis=0, keepdims=True)
        ddt = jnp.sum(jnp.where(is_a, da, 0.0), axis=0, keepdims=True)
        dw2 = _dot(sm, dz, TN)
        db2 = jnp.sum(dz, axis=0, keepdims=True)

        @pl.when(i == 0)
        def _():
            dw2_ref[...] = dw2
            db2_ref[...] = db2
            dal_ref[...] = dal
            ddt_ref[...] = ddt

        @pl.when(i > 0)
        def _():
            dw2_ref[...] += dw2
            db2_ref[...] += db2
            dal_ref[...] += dal
            ddt_ref[...] += ddt

    full = lambda s: pl.BlockSpec(s, lambda i: (0, 0))
    return pl.pallas_call(
        body, name=name, grid=(g_,),
        in_specs=[pl.BlockSpec((tm, SM_W), lambda i: (i, C_SM // SM_W)), full((SM_W, GLA_QK)), full((1, GLA_QK)),
                  full((1, SM_W)), full((1, SM_W)),
                  pl.BlockSpec((GDN_HEADS, tm, SM_W), lambda i: (0, i, 0)),
                  pl.BlockSpec((tm, GLA_QK), lambda i: (i, 0)), _ANY],
        out_specs=[pl.BlockSpec((tm, tail_w), lambda i: (i, C_SM // tail_w)), full((SM_W, GLA_QK)), full((1, GLA_QK)),
                   full((1, SM_W)), full((1, SM_W))],
        out_shape=[jax.ShapeDtypeStruct(d_proj.shape, d_proj.dtype), jax.ShapeDtypeStruct((SM_W, GLA_QK), F32),
                   jax.ShapeDtypeStruct((1, GLA_QK), F32), jax.ShapeDtypeStruct((1, SM_W), F32),
                   jax.ShapeDtypeStruct((1, SM_W), F32)],
        input_output_aliases={7: 0},
        compiler_params=_params("arbitrary"),
    )(proj, w2p, b2, alog_p, dt_p, dgb_heads, dla, d_proj)


QKV_W = GDN_QK
N_QKV_GROUPS = 3
QKV_B0 = C_QKV // QKV_W
HALO = 8


def _conv_terms(x_ref, halo_ref, cw_ref, xs_ref, i, tm):
    xs_ref[HALO:HALO + tm, :] = x_ref[...]
    xs_ref[0:HALO, :] = jnp.where(i > 0, halo_ref[...], 0.0)
    cw = cw_ref[...]
    xs = xs_ref[...]
    taps = [(pltpu.roll(xs, CONV_K - 1 - t, 0) if t < CONV_K - 1 else xs)[HALO:HALO + tm, :] for t in range(CONV_K)]
    c = taps[0] * cw[0:1, :]
    for t in range(1, CONV_K):
        c = c + taps[t] * cw[t:t + 1, :]
    return c, taps


def _prep_fwd(proj, conv_w8, *, name):
    M = proj.shape[0]
    tm = _tile(M, 344, 8)

    def body(x_ref, halo_ref, cw_ref, o_ref, xs_ref):
        j, i = pl.program_id(0), pl.program_id(1)
        c, _ = _conv_terms(x_ref, halo_ref, cw_ref, xs_ref, i, tm)
        s, _ = _silu_and_grad(c)
        scale = jnp.where(j == 0, GDN_DK ** -0.5, 1.0)
        for hh in range(GDN_HEADS):
            cols = slice(hh * 128, (hh + 1) * 128)
            sh = s[:, cols]
            r = lax.rsqrt(jnp.sum(sh * sh, axis=-1, keepdims=True) + NORM_EPS)
            o_ref[:, cols] = jnp.where(j < 2, sh * (r * scale), sh)

    hb = tm // HALO
    return pl.pallas_call(
        body, name=name, grid=(N_QKV_GROUPS, M // tm),
        in_specs=[pl.BlockSpec((tm, QKV_W), lambda j, i: (i, QKV_B0 + j)),
                  pl.BlockSpec((HALO, QKV_W), lambda j, i: (jnp.maximum(i * hb - 1, 0), QKV_B0 + j)),
                  pl.BlockSpec((8, QKV_W), lambda j, i: (0, j))],
        out_specs=pl.BlockSpec((tm, QKV_W), lambda j, i: (i, j)),
        out_shape=jax.ShapeDtypeStruct((M, N_QKV_GROUPS * QKV_W), F32),
        scratch_shapes=[pltpu.VMEM((tm + HALO, QKV_W), F32)],
        compiler_params=_params("parallel", "arbitrary"),
    )(proj, proj, conv_w8)


def _prep_bwd(proj, conv_w8, dact, d_proj, *, name):
    M = proj.shape[0]
    tm = _tile(M, 688, 16)
    g_ = M // tm
    ext = tm + HALO

    def body(x_ref, prev_ref, next_ref, cw_ref, da_ref, dan_ref, _, o_ref, dcw_ref, xs_ref, das_ref, dcs_ref):
        j, i = pl.program_id(0), pl.program_id(1)
        not_last = i < g_ - 1
        xs_ref[0:HALO, :] = jnp.where(i > 0, prev_ref[...], 0.0)
        xs_ref[HALO:HALO + tm, :] = x_ref[...]
        xs_ref[HALO + tm:HALO + ext, :] = jnp.where(not_last, next_ref[...], 0.0)
        das_ref[0:tm, :] = da_ref[...]
        das_ref[tm:ext, :] = jnp.where(not_last, dan_ref[...], 0.0)
        cw = cw_ref[...]
        xs = xs_ref[...]
        taps = [(pltpu.roll(xs, CONV_K - 1 - t, 0) if t < CONV_K - 1 else xs)[HALO:HALO + ext, :] for t in range(CONV_K)]
        c = taps[0] * cw[0:1, :]
        for t in range(1, CONV_K):
            c = c + taps[t] * cw[t:t + 1, :]
        s, ds_dc = _silu_and_grad(c)
        scale = jnp.where(j == 0, GDN_DK ** -0.5, 1.0)
        for hh in range(GDN_HEADS):
            cols = slice(hh * 128, (hh + 1) * 128)
            sh = s[:, cols]
            r = lax.rsqrt(jnp.sum(sh * sh, axis=-1, keepdims=True) + NORM_EPS)
            da = das_ref[:, cols]
            y = sh * r
            dy = da * scale
            ds_norm = r * (dy - y * jnp.sum(dy * y, axis=-1, keepdims=True))
            dcs_ref[:, cols] = jnp.where(j < 2, ds_norm, da) * ds_dc[:, cols]
        dc = dcs_ref[...]
        acc = dc[0:tm, :] * cw[CONV_K - 1:CONV_K, :]
        for t in range(CONV_K - 1):
            acc = acc + pltpu.roll(dc, ext - (CONV_K - 1 - t), 0)[0:tm, :] * cw[t:t + 1, :]
        o_ref[...] = acc.astype(o_ref.dtype)
        r8 = lax.broadcasted_iota(jnp.int32, (8, QKV_W), 0)
        part = jnp.zeros((8, QKV_W), F32)
        for t in range(CONV_K):
            part = jnp.where(r8 == t, jnp.sum(dc[0:tm, :] * taps[t][0:tm, :], axis=0, keepdims=True), part)

        @pl.when(i == 0)
        def _():
            dcw_ref[...] = part

        @pl.when(i > 0)
        def _():
            dcw_ref[...] += part

    hb = tm // HALO
    last = M // HALO - 1
    prev_of = lambda i: jnp.maximum(i * hb - 1, 0)
    next_of = lambda i: jnp.minimum((i + 1) * hb, last)
    return pl.pallas_call(
        body, name=name, grid=(N_QKV_GROUPS, g_),
        in_specs=[pl.BlockSpec((tm, QKV_W), lambda j, i: (i, QKV_B0 + j)),
                  pl.BlockSpec((HALO, QKV_W), lambda j, i: (prev_of(i), QKV_B0 + j)),
                  pl.BlockSpec((HALO, QKV_W), lambda j, i: (next_of(i), QKV_B0 + j)),
                  pl.BlockSpec((8, QKV_W), lambda j, i: (0, j)),
                  pl.BlockSpec((tm, QKV_W), lambda j, i: (i, j)),
                  pl.BlockSpec((HALO, QKV_W), lambda j, i: (next_of(i), j)), _ANY],
        out_specs=[pl.BlockSpec((tm, QKV_W), lambda j, i: (i, QKV_B0 + j)), pl.BlockSpec((8, QKV_W), lambda j, i: (0, j))],
        out_shape=[jax.ShapeDtypeStruct(d_proj.shape, d_proj.dtype),
                   jax.ShapeDtypeStruct((8, N_QKV_GROUPS * QKV_W), F32)],
        input_output_aliases={6: 0},
        scratch_shapes=[pltpu.VMEM((HALO + ext, QKV_W), F32), pltpu.VMEM((ext, QKV_W), F32), pltpu.VMEM((ext, QKV_W), F32)],
        compiler_params=_params("parallel", "arbitrary"),
    )(proj, proj, proj, conv_w8, dact, dact, d_proj)


def _round_robin(gens):
    gens = list(gens)
    while gens:
        alive = []
        for gen in gens:
            try:
                next(gen)
                alive.append(gen)
            except StopIteration:
                pass
        gens = alive


def _unit_lower_inverse(a_low, eye):
    n = a_low.shape[0]
    ri = lax.broadcasted_iota(jnp.int32, (n, n), 0)
    ci = lax.broadcasted_iota(jnp.int32, (n, n), 1)
    same = lambda shift: (ri >> shift) == (ci >> shift)
    b = jnp.where(same(3), -a_low, 0.0)
    x = eye + b
    p2 = _dot3(b, b)
    yield
    x = x + _dot3(x, p2)
    p4 = _dot3(p2, p2)
    yield
    x = x + _dot3(x, p4)
    yield
    for shift in (3, 4, 5):
        between = jnp.where(same(shift + 1) & ~same(shift), a_low, 0.0)
        t = _dot3(between, x)
        yield
        x = x - _dot3(x, t)
        yield
    return x


class _GdnChunk:
    def build(self, q, k, v, gb, h):
        C = GDN_CHUNK
        lane = lax.broadcasted_iota(jnp.int32, (C, SM_W), 1)
        g = jnp.sum(jnp.where(lane == h, gb, 0.0), axis=1, keepdims=True)
        self.beta = jnp.sum(jnp.where(lane == h + GDN_HEADS, gb, 0.0), axis=1, keepdims=True)
        ri = lax.broadcasted_iota(jnp.int32, (C, C), 0)
        ci = lax.broadcasted_iota(jnp.int32, (C, C), 1)
        self.causal = ri >= ci
        self.strict = ri > ci
        self.eye = (ri == ci).astype(F32)
        gcb = _dotx(self.causal.astype(F32), jnp.broadcast_to(g, (C, SM_W)))
        yield
        self.gcol = gcb[:, 0:1]
        grow = gcb.T[0:1, 0:C]
        self.decay = jnp.exp(jnp.where(self.causal, self.gcol - grow, -1e30))
        self.egc = jnp.exp(self.gcol)
        glast = gcb[C - 1:C, 0:1]
        self.elast = jnp.exp(glast - self.gcol)
        self.gl = jnp.exp(glast)
        self.q, self.k, self.v = q, k, v
        self.kb = k * self.beta
        m = _dot(self.kb, k, NT)
        n_ = _dot(q, k, NT)
        yield
        self.a_low = jnp.where(self.strict, m * self.decay, 0.0)
        self.p = n_ * self.decay
        self.qd = q * self.egc
        self.kd = k * self.elast
        self.bu = v * self.beta
        self.bw = self.kb * self.egc


GDN_HB = 8
GDN_HG = GDN_HEADS // GDN_HB


def _gdn_specs(n_of):
    C, W = GDN_CHUNK, 128 * GDN_HB
    q_spec = pl.BlockSpec((C, W), lambda g, n: (n_of(n), g))
    k_spec = pl.BlockSpec((C, W), lambda g, n: (n_of(n), g + GDN_HG))
    v_spec = pl.BlockSpec((C, W), lambda g, n: (n_of(n), g + 2 * GDN_HG))
    gb_spec = pl.BlockSpec((C, SM_W), lambda g, n: (n_of(n), 0))
    o_spec = pl.BlockSpec((C, W), lambda g, n: (n_of(n), g))
    s_spec = pl.BlockSpec((GDN_HB, None, GDN_DK, GDN_DV), lambda g, n: (g, n_of(n), 0, 0))
    t_spec = pl.BlockSpec((GDN_HB, None, C, C), lambda g, n: (g, n_of(n), 0, 0))
    return q_spec, k_spec, v_spec, gb_spec, o_spec, s_spec, t_spec


def _gdn_fwd(act, gb, *, name):
    M = act.shape[0]
    N = M // GDN_CHUNK

    def body(q_ref, k_ref, v_ref, gb_ref, o_ref, s_ref, t_ref, state):
        g, n = pl.program_id(0), pl.program_id(1)

        @pl.when(n == 0)
        def _():
            state[...] = jnp.zeros_like(state)

        gb_ = gb_ref[...]

        def head(hh):
            cols = slice(hh * 128, (hh + 1) * 128)
            c = _GdnChunk()
            yield from c.build(q_ref[:, cols], k_ref[:, cols], v_ref[:, cols], gb_, g * GDN_HB + hh)
            tinv = yield from _unit_lower_inverse(c.a_low, c.eye)
            s = state[hh]
            s_ref[hh] = s
            t_ref[hh] = tinv
            u = _dot(tinv, c.bu)
            w = _dot(tinv, c.bw)
            yield
            vn = u - _dot(w, s)
            o1 = _dot(c.qd, s)
            yield
            o_ref[:, cols] = o1 + _dot(c.p, vn)
            state[hh] = c.gl * s + _dot(c.kd, vn, TN)

        _round_robin(head(hh) for hh in range(GDN_HB))

    q_spec, k_spec, v_spec, gb_spec, o_spec, s_spec, t_spec = _gdn_specs(lambda n: n)
    return pl.pallas_call(
        body, name=name, grid=(GDN_HG, N),
        in_specs=[q_spec, k_spec, v_spec, gb_spec], out_specs=[o_spec, s_spec, t_spec],
        out_shape=[jax.ShapeDtypeStruct((M, GDN_V), F32),
                   jax.ShapeDtypeStruct((GDN_HEADS, N, GDN_DK, GDN_DV), F32),
                   jax.ShapeDtypeStruct((GDN_HEADS, N, GDN_CHUNK, GDN_CHUNK), F32)],
        scratch_shapes=[pltpu.VMEM((GDN_HB, GDN_DK, GDN_DV), F32)],
        compiler_params=_params("parallel", "arbitrary"),
    )(act, act, act, gb)


def _gdn_bwd(act, gb, do, s_all, t_all, *, name):
    M = act.shape[0]
    N = M // GDN_CHUNK
    C = GDN_CHUNK
    assert GDN_HG == 1

    def body(q_ref, k_ref, v_ref, gb_ref, do_ref, s_ref, t_ref, dact_ref, dgb_ref, dstate):
        g, n = pl.program_id(0), pl.program_id(1)

        @pl.when(n == 0)
        def _():
            dstate[...] = jnp.zeros_like(dstate)

        gb_ = gb_ref[...]
        last = lax.broadcasted_iota(jnp.int32, (C, 1), 0) == C - 1
        upper = (lax.broadcasted_iota(jnp.int32, (C, C), 0) <= lax.broadcasted_iota(jnp.int32, (C, C), 1)).astype(F32)
        lane = lax.broadcasted_iota(jnp.int32, (C, SM_W), 1)
        def head(hh):
            cols = slice(hh * 128, (hh + 1) * 128)
            h = g * GDN_HB + hh
            c = _GdnChunk()
            yield from c.build(q_ref[:, cols], k_ref[:, cols], v_ref[:, cols], gb_, h)
            tinv = t_ref[hh]
            tinv_t = tinv.T
            s = s_ref[hh]
            do_ = do_ref[:, cols]
            ds1 = dstate[hh]
            u = _dot(tinv, c.bu)
            w = _dot(tinv, c.bw)
            dqd = _dot(do_, s, NT)
            dvn0 = _dot(c.p, do_, TN) + _dot(c.kd, ds1)
            dst0 = _dot(c.qd, do_, TN) + c.gl * ds1
            yield
            vn = u - _dot(w, s)
            dvn = dvn0
            yield
            dp = jnp.where(c.causal, _dot(do_, vn, NT), 0.0)
            dstate[hh] = dst0 - _dot(w, dvn, TN)
            dkd = _dot(vn, ds1, NT)
            dw = -_dot(dvn, s, NT)
            dbu = _dot(tinv_t, dvn)
            dgl = jnp.sum(jnp.sum(s * ds1, axis=1, keepdims=True), axis=0, keepdims=True)
            yield
            dbw = _dot(tinv_t, dw)
            t1 = _dot(dbu, u, NT)
            yield
            da = jnp.where(c.strict, -(t1 + _dot(dbw, w, NT)), 0.0)
            dn_ = dp * c.decay
            dq0 = _dot(dn_, c.k)
            dk0 = _dot(dn_, c.q, TN)
            yield
            dm = da * c.decay
            e = da * c.a_low + dp * c.p
            dkb = _dot(dm, c.k) + dbw * c.egc
            dact_ref[:, GDN_QK + hh * 128:GDN_QK + (hh + 1) * 128] = (
                _dot(dm, c.kb, TN) + dk0 + dkb * c.beta + dkd * c.elast)
            dact_ref[:, cols] = dq0 + dqd * c.egc
            dact_ref[:, 2 * GDN_QK + hh * 128:2 * GDN_QK + (hh + 1) * 128] = dbu * c.beta
            dbeta = jnp.sum(dbu * c.v, axis=1, keepdims=True) + jnp.sum(dkb * c.k, axis=1, keepdims=True)
            t_kd = jnp.sum(dkd * c.kd, axis=1, keepdims=True)
            dgc = (jnp.sum(e, axis=1, keepdims=True) - jnp.sum(e.T, axis=1, keepdims=True)
                   + jnp.sum(dbw * c.bw, axis=1, keepdims=True) + jnp.sum(dqd * c.qd, axis=1, keepdims=True) - t_kd)
            dgc = dgc + jnp.where(last, jnp.sum(t_kd, axis=0, keepdims=True) + dgl * c.gl, 0.0)
            yield
            dg = _dotx(upper, jnp.broadcast_to(dgc, (C, SM_W)))
            dgb_ref[hh] = jnp.where(lane == h, dg, jnp.where(lane == h + GDN_HEADS, dbeta, 0.0))

        _round_robin(head(hh) for hh in range(GDN_HB))

    rev = lambda n: N - 1 - n
    q_spec, k_spec, v_spec, gb_spec, o_spec, s_spec, t_spec = _gdn_specs(rev)
    dgb_spec = pl.BlockSpec((GDN_HB, C, SM_W), lambda g, n: (g, rev(n), 0))
    return pl.pallas_call(
        body, name=name, grid=(GDN_HG, N),
        in_specs=[q_spec, k_spec, v_spec, gb_spec, o_spec, s_spec, t_spec],
        out_specs=[pl.BlockSpec((C, 2 * GDN_QK + GDN_V), lambda g, n: (rev(n), 0)), dgb_spec],
        out_shape=[jax.ShapeDtypeStruct((M, 2 * GDN_QK + GDN_V), F32),
                   jax.ShapeDtypeStruct((GDN_HEADS, M, SM_W), F32)],
        scratch_shapes=[pltpu.VMEM((GDN_HB, GDN_DK, GDN_DV), F32)],
        compiler_params=_params("parallel", "arbitrary"),
    )(act, act, act, gb, do, s_all, t_all)


GLA_STEP_ROWS = 64
GLA_SUB = GLA_STEP_ROWS // GLA_CHUNK


def _gla_cumsum(la):
    C = GLA_CHUNK
    ltri = (lax.broadcasted_iota(jnp.int32, (C, C), 0) >= lax.broadcasted_iota(jnp.int32, (C, C), 1)).astype(F32)
    return _dotx(ltri, la)


GLA_HALF = GLA_CHUNK // 2


def _gla_cross_factors(b):
    top = lax.broadcasted_iota(jnp.int32, b.shape, 0) < GLA_HALF
    bm = b[GLA_HALF - 1:GLA_HALF, :]
    late = jnp.where(top, 0.0, jnp.exp(jnp.minimum(b - bm, 0.0)))
    early = jnp.where(top, jnp.exp(jnp.minimum(bm - b, 0.0)), 0.0)
    return late, early


def _gla_half_decay(bh, ii):
    rj = lax.broadcasted_iota(jnp.int32, bh.shape, 0)
    return jnp.where(rj <= ii, jnp.exp(jnp.minimum(bh[ii:ii + 1, :] - bh, 0.0)), 0.0)


def _gla_scores_t(q, k, b):
    C, H = GLA_CHUNK, GLA_HALF
    lane = lax.broadcasted_iota(jnp.int32, (H, C), 1)
    halves = []
    for h0 in (0, H):
        qh, kh, bh = q[h0:h0 + H], k[h0:h0 + H], b[h0:h0 + H]
        sth = jnp.zeros((H, C), F32)
        for ii in range(H):
            si = jnp.sum(qh[ii:ii + 1, :] * kh * _gla_half_decay(bh, ii), axis=1, keepdims=True)
            sth = jnp.where(lane == h0 + ii, si, sth)
            if ii % 4 == 3:
                yield
        halves.append(sth)
    late, early = _gla_cross_factors(b)
    between = _dot(k * early, q * late, NT)
    yield
    return jnp.concatenate(halves, axis=0) + between


def _gla_specs(n_of):
    R = GLA_STEP_ROWS
    q_spec = pl.BlockSpec((R, GLA_QK), lambda n: (n_of(n), C_GQ // GLA_QK))
    k_spec = pl.BlockSpec((R, GLA_QK), lambda n: (n_of(n), C_GK // GLA_QK))
    v_spec = pl.BlockSpec((R, GLA_V), lambda n: (n_of(n), C_GV // GLA_V))
    la_spec = pl.BlockSpec((R, GLA_QK), lambda n: (n_of(n), 0))
    o_spec = pl.BlockSpec((R, GLA_V), lambda n: (n_of(n), 0))
    s_spec = pl.BlockSpec((GLA_HEADS, None, GLA_SUB, GLA_DV, GLA_DK), lambda n: (0, n_of(n), 0, 0, 0))
    return q_spec, k_spec, v_spec, la_spec, o_spec, s_spec


def _gla_fwd(proj, la, *, name):
    M = proj.shape[0]
    N = M // GLA_STEP_ROWS
    C = GLA_CHUNK

    def body(q_ref, k_ref, v_ref, la_ref, o_ref, s_ref, state):
        n = pl.program_id(0)

        @pl.when(n == 0)
        def _():
            state[...] = jnp.zeros_like(state)

        local = {}

        def within(hh, c):
            kc = slice(hh * GLA_DK, (hh + 1) * GLA_DK)
            vc = slice(hh * GLA_DV, (hh + 1) * GLA_DV)
            rows = slice(c * C, (c + 1) * C)
            q = q_ref[rows, kc] * (GLA_DK ** -0.5)
            k = k_ref[rows, kc]
            v = v_ref[rows, vc]
            b = _gla_cumsum(la_ref[rows, kc])
            yield
            blast = b[C - 1:C, :]
            sc_t = yield from _gla_scores_t(q, k, b)
            kv = _dot(v, k * jnp.exp(blast - b), TN)
            o2 = _dot(sc_t, v, TN)
            yield
            local[hh, c] = (q * jnp.exp(b), jnp.exp(blast), kv, o2)

        def across(hh):
            vc = slice(hh * GLA_DV, (hh + 1) * GLA_DV)
            st = state[hh]
            for c in range(GLA_SUB):
                qe, eblast, kv, o2 = local[hh, c]
                s_ref[hh, c] = st
                o1 = _dot(qe, st, NT)
                yield
                o_ref[c * C:(c + 1) * C, vc] = o1 + o2
                st = st * eblast + kv
            state[hh] = st

        _round_robin(within(hh, c) for c in range(GLA_SUB) for hh in range(GLA_HEADS))
        _round_robin(across(hh) for hh in range(GLA_HEADS))

    q_spec, k_spec, v_spec, la_spec, o_spec, s_spec = _gla_specs(lambda n: n)
    return pl.pallas_call(
        body, name=name, grid=(N,),
        in_specs=[q_spec, k_spec, v_spec, la_spec], out_specs=[o_spec, s_spec],
        out_shape=[jax.ShapeDtypeStruct((M, GLA_V), F32),
                   jax.ShapeDtypeStruct((GLA_HEADS, N, GLA_SUB, GLA_DV, GLA_DK), F32)],
        scratch_shapes=[pltpu.VMEM((GLA_HEADS, GLA_DV, GLA_DK), F32)],
        compiler_params=_params("arbitrary"),
    )(proj, proj, proj, la)


def _gla_bwd(proj, la, do, s_all, d_proj, *, name):
    M = proj.shape[0]
    N = M // GLA_STEP_ROWS
    C = GLA_CHUNK
    qkv_w = 2 * GLA_QK + GLA_V
    assert C_GK == C_GQ + GLA_QK and C_GV == C_GK + GLA_QK and C_GQ % qkv_w == 0

    def body(q_ref, k_ref, v_ref, la_ref, do_ref, s_ref, _, dp_ref, dla_ref, dstate):
        n = pl.program_id(0)

        @pl.when(n == 0)
        def _():
            dstate[...] = jnp.zeros_like(dstate)

        H = GLA_HALF
        lane = lax.broadcasted_iota(jnp.int32, (C, C), 1)
        row = lax.broadcasted_iota(jnp.int32, (C, C), 0)
        ri = lax.broadcasted_iota(jnp.int32, (C, GLA_DK), 0)
        lane_h = lax.broadcasted_iota(jnp.int32, (H, C), 1)
        ri_h = lax.broadcasted_iota(jnp.int32, (H, GLA_DK), 0)
        cross = (row < H) & (lane >= H)
        upper = (row <= lane).astype(F32)
        def head(hh):
            kc = slice(hh * GLA_DK, (hh + 1) * GLA_DK)
            vc = slice(hh * GLA_DV, (hh + 1) * GLA_DV)
            ds1 = dstate[hh]
            for c in reversed(range(GLA_SUB)):
                rows = slice(c * C, (c + 1) * C)
                q = q_ref[rows, kc] * (GLA_DK ** -0.5)
                k = k_ref[rows, kc]
                v = v_ref[rows, vc]
                b = _gla_cumsum(la_ref[rows, kc])
                do_ = do_ref[rows, vc]
                st = s_ref[hh, c]
                dsc_t = _dot(v, do_, NT)
                dqe = _dot(do_, st)
                dke = _dot(v, ds1)
                yield
                blast = b[C - 1:C, :]
                eb = jnp.exp(b)
                elast = jnp.exp(blast - b)
                eblast = jnp.exp(blast)
                qe = q * eb
                ke = k * elast
                dv2 = _dot(ke, ds1, NT)
                ds_new = _dot(do_, qe, TN)
                deblast = jnp.sum(st * ds1, axis=0, keepdims=True)
                sc_halves, dq_halves, dk_halves = [], [], []
                for h0 in (0, H):
                    qh, kh, bh, dsch = q[h0:h0 + H], k[h0:h0 + H], b[h0:h0 + H], dsc_t[h0:h0 + H]
                    sch = jnp.zeros((H, C), F32)
                    dqh = jnp.zeros((H, GLA_DK), F32)
                    dkh = jnp.zeros((H, GLA_DK), F32)
                    for ii in range(H):
                        f = _gla_half_decay(bh, ii)
                        kf = kh * f
                        si = jnp.sum(qh[ii:ii + 1, :] * kf, axis=1, keepdims=True)
                        sch = jnp.where(lane_h == h0 + ii, si, sch)
                        dsi = jnp.sum(jnp.where(lane_h == h0 + ii, dsch, 0.0), axis=1, keepdims=True)
                        dqh = jnp.where(ri_h == ii, jnp.sum(dsi * kf, axis=0, keepdims=True), dqh)
                        dkh = dkh + (dsi * f) * qh[ii:ii + 1, :]
                        if ii % 4 == 3:
                            yield
                    sc_halves.append(sch)
                    dq_halves.append(dqh)
                    dk_halves.append(dkh)
                late, early = _gla_cross_factors(b)
                q_late, k_early = q * late, k * early
                dsc_x = jnp.where(cross, dsc_t, 0.0)
                sc_t = jnp.concatenate(sc_halves, axis=0) + _dot(k_early, q_late, NT)
                dq_sc = jnp.concatenate(dq_halves, axis=0) + _dot(dsc_x, k_early, TN) * late
                dk_sc = jnp.concatenate(dk_halves, axis=0) + _dot(dsc_x, q_late) * early
                yield
                dv1 = _dot(sc_t, do_)
                dp_ref[rows, kc] = ((dq_sc + dqe * eb) * (GLA_DK ** -0.5)).astype(dp_ref.dtype)
                dp_ref[rows, GLA_QK + hh * GLA_DK:GLA_QK + (hh + 1) * GLA_DK] = (dk_sc + dke * elast).astype(dp_ref.dtype)
                t_ke = dke * ke
                db = q * dq_sc - k * dk_sc + dqe * qe - t_ke
                db = db + jnp.where(ri == C - 1, jnp.sum(t_ke, axis=0, keepdims=True) + deblast * eblast, 0.0)
                dla = _dotx(upper, db)
                yield
                dp_ref[rows, 2 * GLA_QK + hh * GLA_DV:2 * GLA_QK + (hh + 1) * GLA_DV] = (dv1 + dv2).astype(dp_ref.dtype)
                dla_ref[rows, kc] = dla
                ds1 = ds1 * eblast + ds_new
            dstate[hh] = ds1

        _round_robin(head(hh) for hh in range(GLA_HEADS))

    rev = lambda n: N - 1 - n
    q_spec, k_spec, v_spec, la_spec, o_spec, s_spec = _gla_specs(rev)
    return pl.pallas_call(
        body, name=name, grid=(N,),
        in_specs=[q_spec, k_spec, v_spec, la_spec, o_spec, s_spec, _ANY],
        out_specs=[pl.BlockSpec((GLA_STEP_ROWS, qkv_w), lambda n: (rev(n), C_GQ // qkv_w)), la_spec],
        out_shape=[jax.ShapeDtypeStruct(d_proj.shape, d_proj.dtype), jax.ShapeDtypeStruct((M, GLA_QK), F32)],
        input_output_aliases={6: 0},
        scratch_shapes=[pltpu.VMEM((GLA_HEADS, GLA_DV, GLA_DK), F32)],
        compiler_params=_params("arbitrary"),
    )(proj, proj, proj, la, do, s_all, d_proj)


def _head_norm(o, wn):
    r = lax.rsqrt(jnp.mean(o * o, axis=-1, keepdims=True) + NORM_EPS)
    return o * r, r


def _mix_heads():
    heads = [(0, GDN_DV, hh * GDN_DV, hh * GDN_DV) for hh in range(GDN_HEADS)]
    heads += [(1, GLA_DV, GDN_V + hh * GLA_DV, hh * GLA_DV) for hh in range(GLA_HEADS)]
    return heads


def _mix_fwd(o_gdn, o_gla, proj, wn_gdn, wn_gla, *, name):
    M = proj.shape[0]
    tm = _tile(M, 344, 16)

    def body(og_ref, ol_ref, z_ref, r_ref, wg_ref, wl_ref, m_ref):
        srcs = ((og_ref, z_ref, wg_ref), (ol_ref, r_ref, wl_ref))
        for grp, width, mcol, col in _mix_heads():
            o_ref, gate_ref, w_ref = srcs[grp]
            xhat, _ = _head_norm(o_ref[:, col:col + width], None)
            gate, _ = _silu_and_grad(gate_ref[:, col:col + width])
            m_ref[:, mcol:mcol + width] = (xhat * w_ref[...] * gate).astype(m_ref.dtype)

    full = lambda s: pl.BlockSpec(s, lambda i: (0, 0))
    return pl.pallas_call(
        body, name=name, grid=(M // tm,),
        in_specs=[pl.BlockSpec((tm, GDN_V), lambda i: (i, 0)), pl.BlockSpec((tm, GLA_V), lambda i: (i, 0)),
                  pl.BlockSpec((tm, GDN_V), lambda i: (i, C_Z // GDN_V)),
                  pl.BlockSpec((tm, GLA_V), lambda i: (i, C_GR // GLA_V)),
                  full((1, GDN_DV)), full((1, GLA_DV))],
        out_specs=pl.BlockSpec((tm, D_MODEL), lambda i: (i, 0)),
        out_shape=jax.ShapeDtypeStruct((M, D_MODEL), BF16),
        compiler_params=_params("parallel"),
    )(o_gdn, o_gla, proj, proj, wn_gdn, wn_gla)


def _mix_bwd(o_gdn, o_gla, proj, wn_gdn, wn_gla, dmixed, *, name):
    M = proj.shape[0]
    tm = _tile(M, 344, 16)
    g_ = M // tm
    assert C_Z == 0 and C_GR == GDN_V

    def body(og_ref, ol_ref, z_ref, r_ref, wg_ref, wl_ref, dm_ref,
             dog_ref, dol_ref, dzr_ref, dwg_ref, dwl_ref):
        i = pl.program_id(0)
        srcs = ((og_ref, z_ref, wg_ref, dog_ref), (ol_ref, r_ref, wl_ref, dol_ref))
        dws = [jnp.zeros((1, GDN_DV), F32), jnp.zeros((1, GLA_DV), F32)]
        for grp, width, mcol, col in _mix_heads():
            o_ref, gate_ref, w_ref, do_ref = srcs[grp]
            cols = slice(col, col + width)
            xhat, r = _head_norm(o_ref[:, cols], None)
            gate, dgate_dc = _silu_and_grad(gate_ref[:, cols])
            dm = dm_ref[:, mcol:mcol + width]
            dzr_ref[:, mcol:mcol + width] = (dm * xhat * w_ref[...] * dgate_dc).astype(dzr_ref.dtype)
            dnorm = dm * gate
            dws[grp] = dws[grp] + jnp.sum(dnorm * xhat, axis=0, keepdims=True)
            dxhat = dnorm * w_ref[...]
            do_ref[:, cols] = r * (dxhat - xhat * jnp.mean(dxhat * xhat, axis=-1, keepdims=True))

        @pl.when(i == 0)
        def _():
            dwg_ref[...] = dws[0]
            dwl_ref[...] = dws[1]

        @pl.when(i > 0)
        def _():
            dwg_ref[...] += dws[0]
            dwl_ref[...] += dws[1]

    full = lambda s: pl.BlockSpec(s, lambda i: (0, 0))
    half = pl.BlockSpec((tm, GDN_V), lambda i: (i, 0))
    return pl.pallas_call(
        body, name=name, grid=(g_,),
        in_specs=[half, half, pl.BlockSpec((tm, GDN_V), lambda i: (i, C_Z // GDN_V)),
                  pl.BlockSpec((tm, GLA_V), lambda i: (i, C_GR // GLA_V)),
                  full((1, GDN_DV)), full((1, GLA_DV)), pl.BlockSpec((tm, D_MODEL), lambda i: (i, 0))],
        out_specs=[half, half, pl.BlockSpec((tm, GDN_V + GLA_V), lambda i: (i, 0)),
                   full((1, GDN_DV)), full((1, GLA_DV))],
        out_shape=[jax.ShapeDtypeStruct((M, GDN_V), F32), jax.ShapeDtypeStruct((M, GLA_V), F32),
                   jax.ShapeDtypeStruct((M, D_PROJ), BF16),
                   jax.ShapeDtypeStruct((1, GDN_DV), F32), jax.ShapeDtypeStruct((1, GLA_DV), F32)],
        compiler_params=_params("arbitrary"),
    )(o_gdn, o_gla, proj, proj, wn_gdn, wn_gla, dmixed)


def _swiglu_fwd(n, w_gate_t, w_up_t, *, name, tm=1376, tn=512):
    M, D = n.shape
    F = w_gate_t.shape[0]
    tm, tn = _tile(M, tm, 16), _tile(F, tn, 128)

    def body(n_ref, wg_ref, wu_ref, g_ref, u_ref, a_ref):
        x = n_ref[...]
        g = _dot(x, wg_ref[...], NT)
        u = _dot(x, wu_ref[...], NT)
        s, _ = _silu_and_grad(g)
        g_ref[...] = g.astype(g_ref.dtype)
        u_ref[...] = u.astype(u_ref.dtype)
        a_ref[...] = (s * u).astype(a_ref.dtype)

    w_spec = pl.BlockSpec((tn, D), lambda i, j: (j, 0))
    o_spec = pl.BlockSpec((tm, tn), lambda i, j: (i, j))
    return pl.pallas_call(
        body, name=name, grid=(M // tm, F // tn),
        in_specs=[pl.BlockSpec((tm, D), lambda i, j: (i, 0)), w_spec, w_spec], out_specs=[o_spec] * 3,
        out_shape=[jax.ShapeDtypeStruct((M, F), BF16)] * 3, compiler_params=_params("parallel", "parallel"),
    )(n, w_gate_t, w_up_t)


def _swiglu_bwd(dh, w_down, gate, up, *, name, after=None, tm=1376, tn=512):
    M, D = dh.shape
    F = w_down.shape[0]
    tm, tn = _tile(M, tm, 16), _tile(F, tn, 128)
    n_after = 0 if after is None else 1

    def body(*refs):
        dh_ref, w_ref, g_ref, u_ref, dg_ref, du_ref = refs[n_after:]
        da = _dot(dh_ref[...], w_ref[...], NT)
        s, ds = _silu_and_grad(g_ref[...].astype(F32))
        dg_ref[...] = (da * u_ref[...].astype(F32) * ds).astype(dg_ref.dtype)
        du_ref[...] = (da * s).astype(du_ref.dtype)

    o_spec = pl.BlockSpec((tm, tn), lambda i, j: (i, j))
    return pl.pallas_call(
        body, name=name, grid=(M // tm, F // tn),
        in_specs=[_ANY] * n_after + [pl.BlockSpec((tm, D), lambda i, j: (i, 0)),
                                     pl.BlockSpec((tn, D), lambda i, j: (j, 0)), o_spec, o_spec],
        out_specs=[o_spec, o_spec], out_shape=[jax.ShapeDtypeStruct((M, F), BF16)] * 2,
        compiler_params=_params("parallel", "parallel"),
    )(*((after,) if n_after else ()), dh, w_down, gate, up)


def _adamw(w, g, m, v, *, name):
    shape = w.shape
    cols = shape[-1]
    rows = w.size // cols
    w2, g2, m2, v2 = (t.reshape(rows, cols) for t in (w, g, m, v))
    if rows % 8 == 0 or cols % 128 != 0:
        tr, tc = (_tile(rows, 256, 8) if rows % 8 == 0 else rows), cols
    else:
        tr, tc = rows, _tile(cols, 256, 128)

    def body(w_ref, g_ref, m_ref, v_ref, d_ref, nm_ref, nv_ref):
        g_ = g_ref[...]
        nm = ADAM_B1 * m_ref[...] + (1.0 - ADAM_B1) * g_
        nv = ADAM_B2 * v_ref[...] + (1.0 - ADAM_B2) * (g_ * g_)
        m_hat = nm / (1.0 - ADAM_B1 ** ADAM_STEP)
        v_hat = nv / (1.0 - ADAM_B2 ** ADAM_STEP)
        d_ref[...] = -ADAM_LR * (m_hat / (jnp.sqrt(v_hat) + ADAM_EPS) + ADAM_WD * w_ref[...])
        nm_ref[...] = nm
        nv_ref[...] = nv

    blk = pl.BlockSpec((tr, tc), lambda i, j: (i, j))
    outs = pl.pallas_call(
        body, name=name, grid=(rows // tr, cols // tc), in_specs=[blk] * 4, out_specs=[blk] * 3,
        out_shape=[jax.ShapeDtypeStruct((rows, cols), F32)] * 3, compiler_params=_params("parallel", "parallel"),
    )(w2, g2, m2, v2)
    return tuple(t.reshape(shape) for t in outs)


def _sum_slabs(x, *, name):
    _, R, C = x.shape
    sub = 16 if x.dtype == BF16 else 8
    if R % sub == 0:
        tr, tc = _tile(R, 128, sub), C
    else:
        tr, tc = R, _tile(C, 256, 128)

    def body(x_ref, o_ref):
        acc = x_ref[0].astype(F32)
        for s in range(1, N_DEV):
            acc = acc + x_ref[s].astype(F32)
        o_ref[...] = acc

    return pl.pallas_call(
        body, name=name, grid=(R // tr, C // tc),
        in_specs=[pl.BlockSpec((N_DEV, tr, tc), lambda i, j: (0, i, j))],
        out_specs=pl.BlockSpec((tr, tc), lambda i, j: (i, j)),
        out_shape=jax.ShapeDtypeStruct((R, C), F32), compiler_params=_params("parallel", "parallel"),
    )(x)


def _peers():
    x, y, c = lax.axis_index("x"), lax.axis_index("y"), lax.axis_index("c")
    me = 4 * x + 2 * y + c
    peers = []
    for k in range(1, N_DEV):
        px = 1 - x if k & 4 else x
        py = 1 - y if k & 2 else y
        pc = 1 - c if k & 1 else c
        peers.append(((px, py, pc), 4 * px + 2 * py + pc))
    return me, peers


def _gather(x, *, name):
    def body(x_ref, o_ref, send_sems, recv_sems, own_sem):
        me, peers = _peers()
        own = pltpu.make_async_copy(x_ref, o_ref.at[me], own_sem)
        own.start()
        sends, recvs = [], []
        for k, (pos, idx) in enumerate(peers):
            sends.append(pltpu.make_async_remote_copy(
                src_ref=x_ref, dst_ref=o_ref.at[me], send_sem=send_sems.at[k], recv_sem=recv_sems.at[k],
                device_id=pos, device_id_type=pl.DeviceIdType.MESH))
            recvs.append(pltpu.make_async_remote_copy(
                src_ref=x_ref, dst_ref=o_ref.at[idx], send_sem=send_sems.at[k], recv_sem=recv_sems.at[k],
                device_id=pos, device_id_type=pl.DeviceIdType.MESH))
        for cp in sends:
            cp.start()
        for cp in recvs:
            cp.wait_recv()
        for cp in sends:
            cp.wait_send()
        own.wait()

    hbm = pl.BlockSpec(memory_space=pltpu.HBM)
    return pl.pallas_call(
        body, name=name, in_specs=[hbm], out_specs=hbm,
        out_shape=jax.ShapeDtypeStruct((N_DEV,) + tuple(x.shape), x.dtype),
        scratch_shapes=[pltpu.SemaphoreType.DMA((N_DEV - 1,)), pltpu.SemaphoreType.DMA((N_DEV - 1,)),
                        pltpu.SemaphoreType.DMA],
    )(x)


_HBM = pl.BlockSpec(memory_space=pltpu.HBM)
_SEM = pl.BlockSpec(memory_space=pltpu.SEMAPHORE)
_EFFECT = pltpu.SideEffectType.DATAFLOW_SIDE_EFFECTING


PLAN_GATHER = tuple((k, "x", 0) for k in range(1, N_DEV))
PLAN_SCATTER = tuple((k, "xk", 0) for k in range(1, N_DEV))
PLAN_GATHER_CHIPS = tuple((k, "x", 0) for k in (1, 2, 4, 6))
PLAN_GATHER_PASS_ON = tuple((1, ("land", q), q) for q in (2, 4, 6))


def _plan_refs(plan, j, x_ref, land_ref, me, peers, receiving):
    k, source, r = plan[j]
    index_of = lambda q: me if q == 0 else peers[q - 1][1]
    pos, target = peers[k - 1]
    if source == "x":
        src = x_ref
    elif source == "xk":
        src = x_ref.at[target]
    else:
        src = land_ref.at[index_of(source[1])]
    return pos, src, land_ref.at[index_of(k ^ r) if receiving else index_of(r)]


def _exchange_start(x, *, plan, name, after=None, land=None, slab=None):
    n_after = 0 if after is None else 1
    n = len(plan)

    def body(*refs):
        x_ref, land_ref, send_sems, recv_sems, _, _, token = refs[n_after:]
        me, peers = _peers()
        for j in range(n):
            pos, src, dst = _plan_refs(plan, j, x_ref, land_ref, me, peers, receiving=False)
            pltpu.make_async_remote_copy(src_ref=src, dst_ref=dst, send_sem=send_sems.at[j], recv_sem=recv_sems.at[j],
                                         device_id=pos, device_id_type=pl.DeviceIdType.MESH).start()
        token[...] = jnp.zeros_like(token)

    if land is None:
        land = lax.empty((N_DEV,) + tuple(slab), x.dtype)
    return pl.pallas_call(
        body, name=name,
        out_shape=(pltpu.SemaphoreType.DMA((n,)), pltpu.SemaphoreType.DMA((n,)),
                   pltpu.HBM(x.shape, x.dtype), pltpu.HBM(land.shape, land.dtype), jax.ShapeDtypeStruct((8, 128), F32)),
        in_specs=[_ANY] * n_after + [_HBM, _HBM],
        out_specs=(_SEM, _SEM, _HBM, _HBM, pl.BlockSpec(memory_space=pltpu.VMEM)),
        input_output_aliases={n_after: 2, n_after + 1: 3},
        compiler_params=pltpu.CompilerParams(has_side_effects=_EFFECT),
    )(*((after,) if n_after else ()), pltpu.with_memory_space_constraint(x, pltpu.HBM),
      pltpu.with_memory_space_constraint(land, pltpu.HBM))


def _exchange_wait(handle, after, *, plan, name):
    send_sems, recv_sems, x_thru, land_thru, _ = handle
    afters = list(after) if isinstance(after, (list, tuple)) else [after]

    def body(x_ref, land_ref, send_sems, recv_sems, *rest):
        me, peers = _peers()
        for j in range(len(plan)):
            pos, src, dst = _plan_refs(plan, j, x_ref, land_ref, me, peers, receiving=True)
            cp = pltpu.make_async_remote_copy(src_ref=src, dst_ref=dst, send_sem=send_sems.at[j], recv_sem=recv_sems.at[j],
                                              device_id=pos, device_id_type=pl.DeviceIdType.MESH)
            cp.wait_send()
            cp.wait_recv()

    return pl.pallas_call(
        body, name=name,
        out_shape=(pltpu.HBM(x_thru.shape, x_thru.dtype), pltpu.HBM(land_thru.shape, land_thru.dtype)),
        in_specs=[_HBM, _HBM, _SEM, _SEM] + [_ANY] * len(afters), out_specs=(_HBM, _HBM),
        input_output_aliases={0: 0, 1: 1}, compiler_params=pltpu.CompilerParams(has_side_effects=_EFFECT),
    )(x_thru, land_thru, send_sems, recv_sems, *afters)


W_IN_SLAB = D_IN // N_DEV


def _to_proj_rows(t):
    z = jnp.zeros((D_PROJ - C_SM - 2 * GDN_HEADS - GLA_RANK,) + t.shape[1:], t.dtype)
    return jnp.concatenate([t[R_Z:R_A], t[R_GR:R_LR], t[R_GQ:R_GR], t[:R_Z], t[R_A:R_GQ], t[R_LR:], z], axis=0)


def _from_proj_rows(t):
    ab = C_SM + 2 * GDN_HEADS
    return jnp.concatenate([t[C_QKV:C_SM], t[C_Z:C_GR], t[C_SM:ab], t[C_GQ:C_QKV], t[C_GR:C_GQ],
                            t[ab:ab + GLA_RANK]], axis=0)


def _local_step(x, target, meta, attn_nw, conv_w, a_log, dt_bias, gdn_nw, w2, b2, gla_nw, ffn_nw, final_nw,
                fetch, emit, start=None):
    S = x.shape[0]
    head = jnp.concatenate([jnp.zeros((ROW_PAD, D_MODEL), F32), meta], axis=0)
    conv_w8 = jnp.concatenate([conv_w, jnp.zeros((8 - CONV_K, conv_w.shape[1]), F32)], axis=0)
    w2p = jnp.zeros((SM_W, GLA_QK), F32).at[2 * GDN_HEADS:2 * GDN_HEADS + GLA_RANK].set(w2)
    alog_p = jnp.zeros((1, SM_W), F32).at[:, :GDN_HEADS].set(a_log)
    dt_p = jnp.zeros((1, SM_W), F32).at[:, :GDN_HEADS].set(dt_bias)

    h0, n1 = _embed_norm(head, x, attn_nw, name="attn_norm", after=start)
    w_in_t = fetch("w_in_t", (n1, conv_w8, w2p, alog_p, dt_p))
    proj = _matmul(n1, w_in_t, mode="nt", name="in_proj")
    gb, la = _gates_fwd(proj, w2p, b2, alog_p, dt_p, name="gates")
    act = _prep_fwd(proj, conv_w8, name="gdn_prep")
    o_gdn, s_gdn, t_gdn = _gdn_fwd(act, gb, name="gdn_fwd")
    o_gla, s_gla = _gla_fwd(proj, la, name="gla_fwd")
    mixed = _mix_fwd(o_gdn, o_gla, proj, gdn_nw, gla_nw, name="mix")
    w_out = fetch("w_out", mixed)
    h1 = _matmul(mixed, w_out, mode="nn", add=h0, name="out_proj")
    n2 = _rmsnorm_fwd(h1, ffn_nw, name="ffn_norm")
    w_gate_t, w_up_t = fetch("w_gate_t", n2), fetch("w_up_t", n2)
    gate, up, hid = _swiglu_fwd(n2, w_gate_t, w_up_t, name="swiglu")
    w_down = fetch("w_down", hid)
    h2 = _matmul(hid, w_down, mode="nn", add=h1, name="ffn_down", tm=688)
    dh2, dh2_b, d_final_nw, loss = _loss_head(h2, final_nw, target, name="loss_head")

    wg = dict(mode="tn", out_dtype=BF16, tn=512)
    tok = emit("w_down", _matmul(hid, dh2_b, name="d_w_down", tm=704, **wg))
    d_gate, d_up = _swiglu_bwd(dh2_b, w_down, gate, up, name="d_swiglu", after=tok)
    tok = emit("w_gate_t", _matmul(d_gate, n2, name="d_w_gate", tm=704, **wg))
    tok = emit("w_up_t", _matmul(d_up, n2, name="d_w_up", tm=704, after=tok, **wg))
    d_n2 = _matmul_pair(d_gate, w_gate_t, d_up, w_up_t, name="d_n2", after=tok)
    dh1, dh1_b, d_ffn_nw = _rmsnorm_bwd(h1, ffn_nw, d_n2, dh2, name="d_ffn_norm")

    tok = emit("w_out", _matmul(mixed, dh1_b, name="d_w_out", tm=512, **wg))
    d_mixed = _matmul(dh1_b, w_out, mode="nt", name="d_mixed", after=tok)
    do_gdn, do_gla, d_proj, d_gdn_nw, d_gla_nw = _mix_bwd(o_gdn, o_gla, proj, gdn_nw, gla_nw, d_mixed, name="d_mix")
    d_proj, d_la = _gla_bwd(proj, la, do_gla, s_gla, d_proj, name="gla_bwd")
    dact, dgb_heads = _gdn_bwd(act, gb, do_gdn, s_gdn, t_gdn, name="gdn_bwd")
    d_proj, d_w2p, d_b2, d_alog, d_dt = _gates_bwd(proj, w2p, b2, alog_p, dt_p, dgb_heads, d_la, d_proj, name="d_gates")
    d_proj, d_conv_w8 = _prep_bwd(proj, conv_w8, dact, d_proj, name="d_gdn_prep")
    tok = emit("w_in_t", _matmul(d_proj, n1, name="d_w_in", tm=768, **wg))
    d_n1 = _matmul(d_proj, w_in_t, mode="nn", name="d_n1", tm=688, after=tok)
    grad_x, d_head, d_attn_nw = _embed_norm_bwd(h0, attn_nw, d_n1, dh1, name="d_attn_norm")

    return dict(
        loss=loss[0, 0], grad_x=grad_x, meta=d_head[ROW_PAD:HEAD_ROWS], attn_nw=d_attn_nw,
        conv_w=d_conv_w8[:CONV_K], a_log=d_alog[:, :GDN_HEADS], dt_bias=d_dt[:, :GDN_HEADS], gdn_nw=d_gdn_nw,
        w2=d_w2p[2 * GDN_HEADS:2 * GDN_HEADS + GLA_RANK], b2=d_b2, gla_nw=d_gla_nw, ffn_nw=d_ffn_nw,
        final_nw=d_final_nw)


SMALL_ROWS = 32


def kernel(x, meta_tokens, attn_norm_w, w_in, gdn_conv_w, gdn_a_log, gdn_dt_bias, gdn_norm_w, gla_gate_w2, gla_gate_b, gla_norm_w, w_out, ffn_norm_w, w_gate, w_up, w_down, final_norm_w, loss_target, m_meta_tokens, m_attn_norm_w, m_w_in, m_gdn_conv_w, m_gdn_a_log, m_gdn_dt_bias, m_gdn_norm_w, m_gla_gate_w2, m_gla_gate_b, m_gla_norm_w, m_w_out, m_ffn_norm_w, m_w_gate, m_w_up, m_w_down, m_final_norm_w, v_meta_tokens, v_attn_norm_w, v_w_in, v_gdn_conv_w, v_gdn_a_log, v_gdn_dt_bias, v_gdn_norm_w, v_gla_gate_w2, v_gla_gate_b, v_gla_norm_w, v_w_out, v_ffn_norm_w, v_w_gate, v_w_up, v_w_down, v_final_norm_w):
    me = 4 * lax.axis_index("x") + 2 * lax.axis_index("y") + lax.axis_index("c")

    n_conv = gdn_conv_w.shape[2]
    n_w2 = gla_gate_w2.shape[2]
    n_meta = meta_tokens.shape[1]
    small = jnp.zeros((40, n_conv), F32)
    small = small.at[0:N_META, :n_meta].set(meta_tokens)
    small = small.at[N_META:N_META + CONV_K, :].set(gdn_conv_w[0])
    small = small.at[24:24 + GLA_RANK, :n_w2].set(gla_gate_w2[0])
    small_all = _gather(small, name="gather_small")
    meta_f = small_all[:, 0:N_META, :n_meta].transpose(1, 0, 2).reshape(N_META, D_MODEL)
    conv_f = small_all[:, N_META:N_META + CONV_K, :].transpose(1, 0, 2).reshape(CONV_K, N_DEV * n_conv)
    w2_f = small_all[:, 24:24 + GLA_RANK, :n_w2].transpose(1, 0, 2).reshape(GLA_RANK, N_DEV * n_w2)

    w_in_slab = w_in[0].T.astype(BF16)
    in_h = _exchange_start(w_in_slab, plan=PLAN_GATHER_CHIPS, slab=w_in_slab.shape, name="gather_w_in_start",
                           after=small_all)
    handles, tok = {}, in_h[4]
    for wname, slab in (("w_out", w_out[0]), ("w_gate_t", w_gate[0].T), ("w_up_t", w_up[0].T), ("w_down", w_down[0])):
        slab = slab.astype(BF16)
        handles[wname] = _exchange_start(slab, plan=PLAN_GATHER, slab=slab.shape, name="gather_" + wname + "_start", after=tok)
        tok = handles[wname][4]

    def fetch(name, after):
        if name == "w_in_t":
            own, got = _exchange_wait(in_h, after, plan=PLAN_GATHER_CHIPS, name="gather_w_in_wait")
            pass_h = _exchange_start(own, plan=PLAN_GATHER_PASS_ON, land=got, name="pass_w_in_start")
            own, got = _exchange_wait(pass_h, pass_h[4], plan=PLAN_GATHER_PASS_ON, name="pass_w_in_wait")
            got = lax.dynamic_update_index_in_dim(got, own, me, 0)
            return _to_proj_rows(got.reshape(D_IN, D_MODEL))
        own, got = _exchange_wait(handles[name], after, plan=PLAN_GATHER, name="gather_" + name + "_wait")
        got = lax.dynamic_update_index_in_dim(got, own, me, 0)
        return got.reshape(N_DEV * got.shape[1], D_MODEL)

    sent = {}

    def emit(name, grad):
        if name == "w_in_t":
            grad = _from_proj_rows(grad)
        parts = grad.reshape(N_DEV, grad.shape[0] // N_DEV, D_MODEL)
        sent[name] = _exchange_start(parts, plan=PLAN_SCATTER, slab=parts.shape[1:], name="scatter_" + name + "_start")
        return sent[name][4]

    g = _local_step(x[0], loss_target[0], meta_f, attn_norm_w, conv_f, gdn_a_log, gdn_dt_bias, gdn_norm_w, w2_f,
                    gla_gate_b, gla_norm_w, ffn_norm_w, final_norm_w.reshape(1, D_MODEL), fetch, emit, start=tok)

    def total(name, after):
        handle = sent[name]
        own, got = _exchange_wait(handle, after, plan=PLAN_SCATTER, name="scatter_" + name + "_wait")
        got = lax.dynamic_update_index_in_dim(got, lax.dynamic_index_in_dim(own, me, 0, keepdims=False), me, 0)
        return _sum_slabs(got, name="sum_" + name)

    grad_w_down = total("w_down", g["attn_nw"])[None]
    grad_w_gate = total("w_gate_t", grad_w_down)
    grad_w_up = total("w_up_t", grad_w_gate)
    grad_w_out = total("w_out", grad_w_up)[None]
    grad_w_in = total("w_in_t", grad_w_out)

    misc = jnp.concatenate([g["a_log"], g["dt_bias"], g["gdn_nw"], g["gla_nw"], g["b2"], g["loss"].reshape(1, 1)], axis=1)
    n_misc = misc.shape[1]
    misc = jnp.pad(misc, ((0, 0), (0, D_MODEL - n_misc)))
    rows = jnp.concatenate([g["attn_nw"], g["ffn_nw"], g["final_nw"], misc, g["meta"],
                            g["conv_w"].reshape(-1, D_MODEL), g["w2"].reshape(-1, D_MODEL)], axis=0)
    rows = jnp.pad(rows, ((0, SMALL_ROWS - rows.shape[0]), (0, 0)))
    tot = _sum_slabs(_gather(rows, name="gather_small_grads"), name="sum_small_grads")
    grad_attn_nw, grad_ffn_nw, grad_final_nw = tot[0:1], tot[1:2], tot[2]
    grad_a_log = tot[3:4, 0:8]
    grad_dt = tot[3:4, 8:16]
    grad_gdn_nw = tot[3:4, 16:16 + GDN_DV]
    grad_gla_nw = tot[3:4, 144:144 + GLA_DV]
    grad_b2 = tot[3:4, 400:400 + GLA_QK]
    loss = tot[3, n_misc - 1]
    r0 = 4 + N_META
    grad_meta = lax.dynamic_slice(tot[4:r0], (0, me * n_meta), (N_META, n_meta))
    r1 = r0 + CONV_K * N_DEV * n_conv // D_MODEL
    grad_conv = lax.dynamic_slice(tot[r0:r1].reshape(CONV_K, N_DEV * n_conv), (0, me * n_conv), (CONV_K, n_conv))[None]
    r2 = r1 + GLA_RANK * N_DEV * n_w2 // D_MODEL
    grad_w2 = lax.dynamic_slice(tot[r1:r2].reshape(GLA_RANK, N_DEV * n_w2), (0, me * n_w2), (GLA_RANK, n_w2))[None]

    weights = [meta_tokens, attn_norm_w, w_in, gdn_conv_w, gdn_a_log, gdn_dt_bias, gdn_norm_w, gla_gate_w2,
               gla_gate_b, gla_norm_w, w_out, ffn_norm_w, w_gate, w_up, w_down, final_norm_w]
    grads = [grad_meta, grad_attn_nw, grad_w_in, grad_conv, grad_a_log, grad_dt, grad_gdn_nw, grad_w2,
             grad_b2, grad_gla_nw, grad_w_out, grad_ffn_nw, grad_w_gate, grad_w_up, grad_w_down, grad_final_nw]
    ms = [m_meta_tokens, m_attn_norm_w, m_w_in, m_gdn_conv_w, m_gdn_a_log, m_gdn_dt_bias, m_gdn_norm_w,
          m_gla_gate_w2, m_gla_gate_b, m_gla_norm_w, m_w_out, m_ffn_norm_w, m_w_gate, m_w_up, m_w_down, m_final_norm_w]
    vs = [v_meta_tokens, v_attn_norm_w, v_w_in, v_gdn_conv_w, v_gdn_a_log, v_gdn_dt_bias, v_gdn_norm_w,
          v_gla_gate_w2, v_gla_gate_b, v_gla_norm_w, v_w_out, v_ffn_norm_w, v_w_gate, v_w_up, v_w_down, v_final_norm_w]
    transposed = (2, 12, 13)
    outs = [[], [], [], []]
    for idx, (w, gr, m, v) in enumerate(zip(weights, grads, ms, vs)):
        if idx in transposed:
            res = (gr,) + _adamw(w[0].T, gr, m[0].T, v[0].T, name=f"adamw_{idx}")
            res = [t.T[None] for t in res]
        else:
            gr = gr.reshape(w.shape)
            res = (gr,) + _adamw(w, gr, m, v, name=f"adamw_{idx}")
        for lst, t in zip(outs, res):
            lst.append(t)
    return (loss, g["grad_x"][None], *outs[0], *outs[1], *outs[2], *outs[3])
```

```python
import functools

import jax
import jax.numpy as jnp
from jax import lax
from jax.experimental import pallas as pl
from jax.experimental.pallas import tpu as pltpu

F32 = jnp.float32
BF16 = jnp.bfloat16
_MXU_DTYPE = jnp.bfloat16

D_MODEL = 2048
N_META = 16
ROW_PAD = 48
HEAD_ROWS = ROW_PAD + N_META
CONV_K = 4
GDN_HEADS, GDN_DK, GDN_DV, GDN_CHUNK = 8, 128, 128, 64
GLA_HEADS, GLA_DK, GLA_DV, GLA_CHUNK = 4, 128, 256, 16
GLA_RANK = 16
GLA_GATE_NORMALIZER = 16.0
GDN_QK = GDN_HEADS * GDN_DK
GDN_V = GDN_HEADS * GDN_DV
GLA_QK = GLA_HEADS * GLA_DK
GLA_V = GLA_HEADS * GLA_DV
D_FF = 5632
D_IN = 7200
NORM_EPS = 1e-6
C_Z, C_GR, C_GQ, C_GK, C_GV, C_QKV, C_SM = 0, 1024, 2048, 2560, 3072, 4096, 7168
SM_W = 128
D_PROJ = 7680
R_Z, R_A, R_B, R_GQ, R_GK, R_GV, R_GR, R_LR = 3072, 4096, 4104, 4112, 4624, 5136, 6160, 7184

ADAM_LR, ADAM_B1, ADAM_B2, ADAM_EPS, ADAM_WD, ADAM_STEP = 0.001, 0.9, 0.999, 1e-08, 0.01, 10

N_DEV = 8
VMEM_LIMIT = 56 * 1024 * 1024

NN = (((1,), (0,)), ((), ()))
NT = (((1,), (1,)), ((), ()))
TN = (((0,), (0,)), ((), ()))


def _dot(a, b, dims=NN):
    return lax.dot_general(a.astype(_MXU_DTYPE), b.astype(_MXU_DTYPE), dims, preferred_element_type=F32)


def _dotx(a, b, dims=NN):
    return lax.dot_general(a, b, dims, precision=lax.Precision.HIGHEST, preferred_element_type=F32)


def _dot3(a, b):
    ah = a.astype(BF16)
    al = (a - ah.astype(F32)).astype(BF16)
    bh = b.astype(BF16)
    bl = (b - bh.astype(F32)).astype(BF16)
    d = functools.partial(lax.dot_general, dimension_numbers=NN, preferred_element_type=F32)
    return d(ah, bh) + (d(ah, bl) + d(al, bh))


def _tile(n, target, mult=8):
    best = None
    for t in range(mult, min(n, target) + 1, mult):
        if n % t == 0:
            best = t
    return best if best is not None else n


def _params(*sem):
    return pltpu.CompilerParams(dimension_semantics=sem, vmem_limit_bytes=VMEM_LIMIT)


def _sigmoid(x):
    return 0.5 * jnp.tanh(0.5 * x) + 0.5


def _softplus(x):
    return jnp.maximum(x, 0.0) + jnp.log1p(jnp.exp(-jnp.abs(x)))


def _silu_and_grad(c):
    s = _sigmoid(c)
    return c * s, s * (1.0 + c * (1.0 - s))


_ANY = pl.BlockSpec(memory_space=pl.ANY)


def _matmul(a, b, *, mode, name, out_dtype=F32, add=None, after=None, tm=1376, tn=512):
    if mode == "tn":
        K, M = a.shape
        N = b.shape[1]
    else:
        M, K = a.shape
        N = b.shape[0] if mode == "nt" else b.shape[1]
    tm = _tile(M, tm, 128 if mode == "tn" else 16)
    tn = _tile(N, tn, 128)
    dims = {"nn": NN, "nt": NT, "tn": TN}[mode]
    n_after = 0 if after is None else 1

    def body(*refs):
        refs = refs[n_after:]
        r = _dot(refs[0][...], refs[1][...], dims)
        if add is not None:
            r = r + refs[2][...]
        refs[-1][...] = r.astype(out_dtype)

    a_spec = pl.BlockSpec((K, tm), lambda i, j: (0, i)) if mode == "tn" else pl.BlockSpec((tm, K), lambda i, j: (i, 0))
    b_spec = pl.BlockSpec((tn, K), lambda i, j: (j, 0)) if mode == "nt" else pl.BlockSpec((K, tn), lambda i, j: (0, j))
    o_spec = pl.BlockSpec((tm, tn), lambda i, j: (i, j))
    in_specs = [_ANY] * n_after + [a_spec, b_spec] + ([o_spec] if add is not None else [])
    args = ((after,) if n_after else ()) + (a, b) + ((add,) if add is not None else ())
    return pl.pallas_call(
        body, name=name, grid=(M // tm, N // tn), in_specs=in_specs, out_specs=o_spec,
        out_shape=jax.ShapeDtypeStruct((M, N), out_dtype), compiler_params=_params("parallel", "parallel"),
    )(*args)


def _matmul_pair(a1, b1, a2, b2, *, name, after=None, tm=688, tn=256):
    M, K = a1.shape
    N = b1.shape[1]
    tm, tn = _tile(M, tm, 16), _tile(N, tn, 128)
    n_after = 0 if after is None else 1

    def body(*refs):
        a1_ref, b1_ref, a2_ref, b2_ref, o_ref = refs[n_after:]
        o_ref[...] = _dot(a1_ref[...], b1_ref[...]) + _dot(a2_ref[...], b2_ref[...])

    a_spec = pl.BlockSpec((tm, K), lambda i, j: (i, 0))
    b_spec = pl.BlockSpec((K, tn), lambda i, j: (0, j))
    return pl.pallas_call(
        body, name=name, grid=(M // tm, N // tn), in_specs=[_ANY] * n_after + [a_spec, b_spec, a_spec, b_spec],
        out_specs=pl.BlockSpec((tm, tn), lambda i, j: (i, j)), out_shape=jax.ShapeDtypeStruct((M, N), F32),
        compiler_params=_params("parallel", "parallel"),
    )(*((after,) if n_after else ()), a1, b1, a2, b2)


def _rmsnorm_fwd(h, w, *, name):
    M, D = h.shape
    tm = _tile(M, 688, 16)

    def body(h_ref, w_ref, n_ref):
        x = h_ref[...]
        r = lax.rsqrt(jnp.mean(x * x, axis=-1, keepdims=True) + NORM_EPS)
        n_ref[...] = (x * r * w_ref[...]).astype(n_ref.dtype)

    return pl.pallas_call(
        body, name=name, grid=(M // tm,),
        in_specs=[pl.BlockSpec((tm, D), lambda i: (i, 0)), pl.BlockSpec((1, D), lambda i: (0, 0))],
        out_specs=pl.BlockSpec((tm, D), lambda i: (i, 0)),
        out_shape=jax.ShapeDtypeStruct((M, D), BF16),
        compiler_params=_params("parallel"),
    )(h, w)


SEQ_BLOCK = HEAD_ROWS


def _seq_blocks_per_tile(rows):
    n = rows // SEQ_BLOCK
    return max(m for m in (1, 2, 3, 4) if n % m == 0)


def _seq_specs(m, D):
    return [pl.BlockSpec((SEQ_BLOCK, D), functools.partial(lambda i, k: (jnp.maximum(m * i + k - 1, 0), 0), k=k))
            for k in range(m)]


def _embed_norm(head, x, w, *, name, after=None):
    S, D = x.shape
    m = _seq_blocks_per_tile(S + HEAD_ROWS)
    n_after = 0 if after is None else 1

    def body(*refs):
        refs = refs[n_after:]
        head_ref, x_refs, w_ref, h_ref, n_ref = refs[0], refs[1:1 + m], refs[1 + m], refs[2 + m], refs[3 + m]
        i = pl.program_id(0)
        for k in range(m):
            blk = x_refs[k][...]
            if k == 0:
                blk = jnp.where(i == 0, head_ref[...], blk)
            rows = slice(k * SEQ_BLOCK, (k + 1) * SEQ_BLOCK)
            h_ref[rows, :] = blk
            r = lax.rsqrt(jnp.mean(blk * blk, axis=-1, keepdims=True) + NORM_EPS)
            n_ref[rows, :] = (blk * r * w_ref[...]).astype(n_ref.dtype)

    tile = pl.BlockSpec((m * SEQ_BLOCK, D), lambda i: (i, 0))
    return pl.pallas_call(
        body, name=name, grid=((S + HEAD_ROWS) // (m * SEQ_BLOCK),),
        in_specs=[_ANY] * n_after + [pl.BlockSpec((SEQ_BLOCK, D), lambda i: (0, 0))] + _seq_specs(m, D)
        + [pl.BlockSpec((1, D), lambda i: (0, 0))],
        out_specs=[tile, tile],
        out_shape=[jax.ShapeDtypeStruct((S + HEAD_ROWS, D), F32), jax.ShapeDtypeStruct((S + HEAD_ROWS, D), BF16)],
        compiler_params=_params("parallel"),
    )(*((after,) if n_after else ()), head, *([x] * m), w)


def _embed_norm_bwd(h, w, dn, dres, *, name):
    M, D = h.shape
    S = M - HEAD_ROWS
    m = _seq_blocks_per_tile(S)
    g = S // (m * SEQ_BLOCK)

    def one(x, dn_, dres_, w_):
        r = lax.rsqrt(jnp.mean(x * x, axis=-1, keepdims=True) + NORM_EPS)
        xhat = x * r
        dxhat = dn_ * w_
        dh = dres_ + r * (dxhat - xhat * jnp.mean(dxhat * xhat, axis=-1, keepdims=True))
        return dh, jnp.sum((dn_ * xhat).reshape(SEQ_BLOCK // 8, 8, D), axis=0)

    def body(*refs):
        w_ref = refs[0]
        groups = [refs[1 + a * (m + 1):1 + (a + 1) * (m + 1)] for a in range(3)]
        gx_ref, dhead_ref, dw_ref, acc_ref = refs[1 + 3 * (m + 1):]
        i = pl.program_id(0)
        w_ = w_ref[...]
        part = jnp.zeros((8, D), F32)
        for k in range(m):
            dh, p = one(*(grp[1 + k][...] for grp in groups), w_)
            gx_ref[k * SEQ_BLOCK:(k + 1) * SEQ_BLOCK, :] = dh
            part = part + p

        @pl.when(i == 0)
        def _():
            dh, p = one(*(grp[0][...] for grp in groups), w_)
            dhead_ref[...] = dh
            acc_ref[...] = part + p

        @pl.when(i > 0)
        def _():
            acc_ref[...] += part

        @pl.when(i == g - 1)
        def _():
            dw_ref[...] = jnp.sum(acc_ref[...], axis=0, keepdims=True)

    first = pl.BlockSpec((SEQ_BLOCK, D), lambda i: (0, 0))
    blocks = [pl.BlockSpec((SEQ_BLOCK, D), functools.partial(lambda i, k: (m * i + k + 1, 0), k=k)) for k in range(m)]
    vec = pl.BlockSpec((1, D), lambda i: (0, 0))
    return pl.pallas_call(
        body, name=name, grid=(g,), in_specs=[vec] + ([first] + blocks) * 3,
        out_specs=[pl.BlockSpec((m * SEQ_BLOCK, D), lambda i: (i, 0)), first, vec],
        out_shape=[jax.ShapeDtypeStruct((S, D), F32), jax.ShapeDtypeStruct((SEQ_BLOCK, D), F32),
                   jax.ShapeDtypeStruct((1, D), F32)],
        scratch_shapes=[pltpu.VMEM((8, D), F32)],
        compiler_params=_params("arbitrary"),
    )(w, *([h] * (m + 1)), *([dn] * (m + 1)), *([dres] * (m + 1)))


def _rmsnorm_bwd(h, w, dn, dres, *, name):
    M, D = h.shape
    tm = _tile(M, 344, 16)
    g = M // tm

    def body(h_ref, w_ref, dn_ref, dres_ref, dh_ref, dhb_ref, dw_ref, acc_ref):
        i = pl.program_id(0)
        x = h_ref[...]
        r = lax.rsqrt(jnp.mean(x * x, axis=-1, keepdims=True) + NORM_EPS)
        xhat = x * r
        dn_ = dn_ref[...]
        dxhat = dn_ * w_ref[...]
        dh = dres_ref[...] + r * (dxhat - xhat * jnp.mean(dxhat * xhat, axis=-1, keepdims=True))
        dh_ref[...] = dh
        dhb_ref[...] = dh.astype(dhb_ref.dtype)
        part = jnp.sum((dn_ * xhat).reshape(tm // 8, 8, D), axis=0)

        @pl.when(i == 0)
        def _():
            acc_ref[...] = part

        @pl.when(i > 0)
        def _():
            acc_ref[...] += part

        @pl.when(i == g - 1)
        def _():
            dw_ref[...] = jnp.sum(acc_ref[...], axis=0, keepdims=True)

    row = pl.BlockSpec((tm, D), lambda i: (i, 0))
    vec = pl.BlockSpec((1, D), lambda i: (0, 0))
    return pl.pallas_call(
        body, name=name, grid=(g,), in_specs=[row, vec, row, row],
        out_specs=[row, row, vec],
        out_shape=[jax.ShapeDtypeStruct((M, D), F32), jax.ShapeDtypeStruct((M, D), BF16),
                   jax.ShapeDtypeStruct((1, D), F32)],
        scratch_shapes=[pltpu.VMEM((8, D), F32)],
        compiler_params=_params("arbitrary"),
    )(h, w, dn, dres)


def _loss_head(h, w, target, *, name):
    M, D = h.shape
    m = _seq_blocks_per_tile(M)
    tm = m * SEQ_BLOCK
    g = M // tm

    def body(h_ref, w_ref, *rest):
        t_refs = rest[:m]
        dh_ref, dhb_ref, dw_ref, loss_ref, acc_ref, lacc_ref = rest[m:]
        i = pl.program_id(0)
        x = h_ref[...]
        row = i * tm + lax.broadcasted_iota(jnp.int32, (tm, 1), 0)
        live = row >= HEAD_ROWS
        r = lax.rsqrt(jnp.mean(x * x, axis=-1, keepdims=True) + NORM_EPS)
        xhat = x * r
        t = jnp.concatenate([t_ref[...] for t_ref in t_refs], axis=0)
        err = jnp.where(live, xhat * w_ref[...] - t, 0.0)
        dy = err * (1.0 / D)
        dxhat = dy * w_ref[...]
        dh = r * (dxhat - xhat * jnp.mean(dxhat * xhat, axis=-1, keepdims=True))
        dh_ref[...] = dh
        dhb_ref[...] = dh.astype(dhb_ref.dtype)
        part = jnp.sum((dy * xhat).reshape(tm // 8, 8, D), axis=0)
        lpart = jnp.sum((err * err).reshape(tm // 8, 8, D), axis=0)

        @pl.when(i == 0)
        def _():
            acc_ref[...] = part
            lacc_ref[...] = lpart

        @pl.when(i > 0)
        def _():
            acc_ref[...] += part
            lacc_ref[...] += lpart

        @pl.when(i == g - 1)
        def _():
            dw_ref[...] = jnp.sum(acc_ref[...], axis=0, keepdims=True)
            tot = jnp.sum(jnp.sum(lacc_ref[...], axis=0, keepdims=True), axis=1, keepdims=True)
            loss_ref[...] = jnp.broadcast_to(tot * (0.5 / D), (1, 128))

    row = pl.BlockSpec((tm, D), lambda i: (i, 0))
    vec = pl.BlockSpec((1, D), lambda i: (0, 0))
    return pl.pallas_call(
        body, name=name, grid=(g,), in_specs=[row, vec] + _seq_specs(m, D),
        out_specs=[row, row, vec, pl.BlockSpec((1, 128), lambda i: (0, 0))],
        out_shape=[jax.ShapeDtypeStruct((M, D), F32), jax.ShapeDtypeStruct((M, D), BF16),
                   jax.ShapeDtypeStruct((1, D), F32), jax.ShapeDtypeStruct((1, 128), F32)],
        scratch_shapes=[pltpu.VMEM((8, D), F32), pltpu.VMEM((8, D), F32)],
        compiler_params=_params("arbitrary"),
    )(h, w, *([target] * m))


def _gate_terms(sm, w2p, b2, alog_p, dt_p, row0):
    tm = sm.shape[0]
    lane = lax.broadcasted_iota(jnp.int32, (tm, SM_W), 1)
    live = (row0 + lax.broadcasted_iota(jnp.int32, (tm, 1), 0)) >= ROW_PAD
    pre = sm + dt_p
    neg_a = -jnp.exp(alog_p)
    g = neg_a * _softplus(pre)
    beta = _sigmoid(sm)
    z = _dot(sm, w2p) + b2
    return lane, live, pre, neg_a, g, beta, z


def _gates_fwd(proj, w2p, b2, alog_p, dt_p, *, name):
    M = proj.shape[0]
    tm = _tile(M, 688, 8)

    def body(sm_ref, w2_ref, b2_ref, al_ref, dt_ref, gb_ref, la_ref):
        row0 = pl.program_id(0) * tm
        lane, live, _, _, g, beta, z = _gate_terms(sm_ref[...], w2_ref[...], b2_ref[...], al_ref[...], dt_ref[...], row0)
        gb = jnp.where(lane < GDN_HEADS, g, jnp.where(lane < 2 * GDN_HEADS, beta, 0.0))
        gb_ref[...] = jnp.where(live, gb, 0.0)
        la = (jnp.minimum(z, 0.0) - jnp.log1p(jnp.exp(-jnp.abs(z)))) * (1.0 / GLA_GATE_NORMALIZER)
        la_ref[...] = jnp.where(live, la, 0.0)

    full = lambda s: pl.BlockSpec(s, lambda i: (0, 0))
    return pl.pallas_call(
        body, name=name, grid=(M // tm,),
        in_specs=[pl.BlockSpec((tm, SM_W), lambda i: (i, C_SM // SM_W)), full((SM_W, GLA_QK)), full((1, GLA_QK)),
                  full((1, SM_W)), full((1, SM_W))],
        out_specs=[pl.BlockSpec((tm, SM_W), lambda i: (i, 0)), pl.BlockSpec((tm, GLA_QK), lambda i: (i, 0))],
        out_shape=[jax.ShapeDtypeStruct((M, SM_W), F32), jax.ShapeDtypeStruct((M, GLA_QK), F32)],
        compiler_params=_params("parallel"),
    )(proj, w2p, b2, alog_p, dt_p)


def _gates_bwd(proj, w2p, b2, alog_p, dt_p, dgb_heads, dla, d_proj, *, name):
    M = proj.shape[0]
    tm = _tile(M, 688, 8)
    g_ = M // tm

    tail_w = D_PROJ - C_SM

    def body(sm_ref, w2_ref, b2_ref, al_ref, dt_ref, dgb_ref, dla_ref, _,
             dsm_ref, dw2_ref, db2_ref, dal_ref, ddt_ref):
        i = pl.program_id(0)
        sm = sm_ref[...]
        lane, live, pre, neg_a, g, beta, z = _gate_terms(sm, w2_ref[...], b2_ref[...], al_ref[...], dt_ref[...], i * tm)
        dz = jnp.where(live, dla_ref[...] * (_sigmoid(-z) * (1.0 / GLA_GATE_NORMALIZER)), 0.0)
        dsm_lr = _dot(dz, w2_ref[...], NT)
        dgb = dgb_ref[0]
        for hh in range(1, GDN_HEADS):
            dgb = dgb + dgb_ref[hh]
        dgb = jnp.where(live, dgb, 0.0)
        da = dgb * neg_a * _sigmoid(pre)
        db = dgb * beta * (1.0 - beta)
        dsm = jnp.where(lane < GDN_HEADS, da, jnp.where(lane < 2 * GDN_HEADS, db, dsm_lr))
        dsm_ref[:, 0:SM_W] = dsm.astype(dsm_ref.dtype)
        dsm_ref[:, SM_W:tail_w] = jnp.zeros((tm, tail_w - SM_W), dsm_ref.dtype)
        is_a = lane < GDN_HEADS
        dal = jnp.sum(jnp.where(is_a, dgb * g, 0.0), axis=0, keepdims=True)
        ddt = jnp.sum(jnp.where(is_a, da, 0.0), axis=0, keepdims=True)
        dw2 = _dot(sm, dz, TN)
        db2 = jnp.sum(dz, axis=0, keepdims=True)

        @pl.when(i == 0)
        def _():
            dw2_ref[...] = dw2
            db2_ref[...] = db2
            dal_ref[...] = dal
            ddt_ref[...] = ddt

        @pl.when(i > 0)
        def _():
            dw2_ref[...] += dw2
            db2_ref[...] += db2
            dal_ref[...] += dal
            ddt_ref[...] += ddt

    full = lambda s: pl.BlockSpec(s, lambda i: (0, 0))
    return pl.pallas_call(
        body, name=name, grid=(g_,),
        in_specs=[pl.BlockSpec((tm, SM_W), lambda i: (i, C_SM // SM_W)), full((SM_W, GLA_QK)), full((1, GLA_QK)),
                  full((1, SM_W)), full((1, SM_W)),
                  pl.BlockSpec((GDN_HEADS, tm, SM_W), lambda i: (0, i, 0)),
                  pl.BlockSpec((tm, GLA_QK), lambda i: (i, 0)), _ANY],
        out_specs=[pl.BlockSpec((tm, tail_w), lambda i: (i, C_SM // tail_w)), full((SM_W, GLA_QK)), full((1, GLA_QK)),
                   full((1, SM_W)), full((1, SM_W))],
        out_shape=[jax.ShapeDtypeStruct(d_proj.shape, d_proj.dtype), jax.ShapeDtypeStruct((SM_W, GLA_QK), F32),
                   jax.ShapeDtypeStruct((1, GLA_QK), F32), jax.ShapeDtypeStruct((1, SM_W), F32),
                   jax.ShapeDtypeStruct((1, SM_W), F32)],
        input_output_aliases={7: 0},
        compiler_params=_params("arbitrary"),
    )(proj, w2p, b2, alog_p, dt_p, dgb_heads, dla, d_proj)


QKV_W = GDN_QK
N_QKV_GROUPS = 3
QKV_B0 = C_QKV // QKV_W
HALO = 8


def _conv_terms(x_ref, halo_ref, cw_ref, xs_ref, i, tm):
    xs_ref[HALO:HALO + tm, :] = x_ref[...]
    xs_ref[0:HALO, :] = jnp.where(i > 0, halo_ref[...], 0.0)
    cw = cw_ref[...]
    xs = xs_ref[...]
    taps = [(pltpu.roll(xs, CONV_K - 1 - t, 0) if t < CONV_K - 1 else xs)[HALO:HALO + tm, :] for t in range(CONV_K)]
    c = taps[0] * cw[0:1, :]
    for t in range(1, CONV_K):
        c = c + taps[t] * cw[t:t + 1, :]
    return c, taps


def _prep_fwd(proj, conv_w8, *, name):
    M = proj.shape[0]
    tm = _tile(M, 344, 8)

    def body(x_ref, halo_ref, cw_ref, o_ref, xs_ref):
        j, i = pl.program_id(0), pl.program_id(1)
        c, _ = _conv_terms(x_ref, halo_ref, cw_ref, xs_ref, i, tm)
        s, _ = _silu_and_grad(c)
        scale = jnp.where(j == 0, GDN_DK ** -0.5, 1.0)
        for hh in range(GDN_HEADS):
            cols = slice(hh * 128, (hh + 1) * 128)
            sh = s[:, cols]
            r = lax.rsqrt(jnp.sum(sh * sh, axis=-1, keepdims=True) + NORM_EPS)
            o_ref[:, cols] = jnp.where(j < 2, sh * (r * scale), sh)

    hb = tm // HALO
    return pl.pallas_call(
        body, name=name, grid=(N_QKV_GROUPS, M // tm),
        in_specs=[pl.BlockSpec((tm, QKV_W), lambda j, i: (i, QKV_B0 + j)),
                  pl.BlockSpec((HALO, QKV_W), lambda j, i: (jnp.maximum(i * hb - 1, 0), QKV_B0 + j)),
                  pl.BlockSpec((8, QKV_W), lambda j, i: (0, j))],
        out_specs=pl.BlockSpec((tm, QKV_W), lambda j, i: (i, j)),
        out_shape=jax.ShapeDtypeStruct((M, N_QKV_GROUPS * QKV_W), F32),
        scratch_shapes=[pltpu.VMEM((tm + HALO, QKV_W), F32)],
        compiler_params=_params("parallel", "arbitrary"),
    )(proj, proj, conv_w8)


def _prep_bwd(proj, conv_w8, dact, d_proj, *, name):
    M = proj.shape[0]
    tm = _tile(M, 688, 16)
    g_ = M // tm
    ext = tm + HALO

    def body(x_ref, prev_ref, next_ref, cw_ref, da_ref, dan_ref, _, o_ref, dcw_ref, xs_ref, das_ref, dcs_ref):
        j, i = pl.program_id(0), pl.program_id(1)
        not_last = i < g_ - 1
        xs_ref[0:HALO, :] = jnp.where(i > 0, prev_ref[...], 0.0)
        xs_ref[HALO:HALO + tm, :] = x_ref[...]
        xs_ref[HALO + tm:HALO + ext, :] = jnp.where(not_last, next_ref[...], 0.0)
        das_ref[0:tm, :] = da_ref[...]
        das_ref[tm:ext, :] = jnp.where(not_last, dan_ref[...], 0.0)
        cw = cw_ref[...]
        xs = xs_ref[...]
        taps = [(pltpu.roll(xs, CONV_K - 1 - t, 0) if t < CONV_K - 1 else xs)[HALO:HALO + ext, :] for t in range(CONV_K)]
        c = taps[0] * cw[0:1, :]
        for t in range(1, CONV_K):
            c = c + taps[t] * cw[t:t + 1, :]
        s, ds_dc = _silu_and_grad(c)
        scale = jnp.where(j == 0, GDN_DK ** -0.5, 1.0)
        for hh in range(GDN_HEADS):
            cols = slice(hh * 128, (hh + 1) * 128)
            sh = s[:, cols]
            r = lax.rsqrt(jnp.sum(sh * sh, axis=-1, keepdims=True) + NORM_EPS)
            da = das_ref[:, cols]
            y = sh * r
            dy = da * scale
            ds_norm = r * (dy - y * jnp.sum(dy * y, axis=-1, keepdims=True))
            dcs_ref[:, cols] = jnp.where(j < 2, ds_norm, da) * ds_dc[:, cols]
        dc = dcs_ref[...]
        acc = dc[0:tm, :] * cw[CONV_K - 1:CONV_K, :]
        for t in range(CONV_K - 1):
            acc = acc + pltpu.roll(dc, ext - (CONV_K - 1 - t), 0)[0:tm, :] * cw[t:t + 1, :]
        o_ref[...] = acc.astype(o_ref.dtype)
        r8 = lax.broadcasted_iota(jnp.int32, (8, QKV_W), 0)
        part = jnp.zeros((8, QKV_W), F32)
        for t in range(CONV_K):
            part = jnp.where(r8 == t, jnp.sum(dc[0:tm, :] * taps[t][0:tm, :], axis=0, keepdims=True), part)

        @pl.when(i == 0)
        def _():
            dcw_ref[...] = part

        @pl.when(i > 0)
        def _():
            dcw_ref[...] += part

    hb = tm // HALO
    last = M // HALO - 1
    prev_of = lambda i: jnp.maximum(i * hb - 1, 0)
    next_of = lambda i: jnp.minimum((i + 1) * hb, last)
    return pl.pallas_call(
        body, name=name, grid=(N_QKV_GROUPS, g_),
        in_specs=[pl.BlockSpec((tm, QKV_W), lambda j, i: (i, QKV_B0 + j)),
                  pl.BlockSpec((HALO, QKV_W), lambda j, i: (prev_of(i), QKV_B0 + j)),
                  pl.BlockSpec((HALO, QKV_W), lambda j, i: (next_of(i), QKV_B0 + j)),
                  pl.BlockSpec((8, QKV_W), lambda j, i: (0, j)),
                  pl.BlockSpec((tm, QKV_W), lambda j, i: (i, j)),
                  pl.BlockSpec((HALO, QKV_W), lambda j, i: (next_of(i), j)), _ANY],
        out_specs=[pl.BlockSpec((tm, QKV_W), lambda j, i: (i, QKV_B0 + j)), pl.BlockSpec((8, QKV_W), lambda j, i: (0, j))],
        out_shape=[jax.ShapeDtypeStruct(d_proj.shape, d_proj.dtype),
                   jax.ShapeDtypeStruct((8, N_QKV_GROUPS * QKV_W), F32)],
        input_output_aliases={6: 0},
        scratch_shapes=[pltpu.VMEM((HALO + ext, QKV_W), F32), pltpu.VMEM((ext, QKV_W), F32), pltpu.VMEM((ext, QKV_W), F32)],
        compiler_params=_params("parallel", "arbitrary"),
    )(proj, proj, proj, conv_w8, dact, dact, d_proj)


def _round_robin(gens):
    gens = list(gens)
    while gens:
        alive = []
        for gen in gens:
            try:
                next(gen)
                alive.append(gen)
            except StopIteration:
                pass
        gens = alive


def _unit_lower_inverse(a_low, eye):
    n = a_low.shape[0]
    ri = lax.broadcasted_iota(jnp.int32, (n, n), 0)
    ci = lax.broadcasted_iota(jnp.int32, (n, n), 1)
    same = lambda shift: (ri >> shift) == (ci >> shift)
    b = jnp.where(same(3), -a_low, 0.0)
    x = eye + b
    p2 = _dot3(b, b)
    yield
    x = x + _dot3(x, p2)
    p4 = _dot3(p2, p2)
    yield
    x = x + _dot3(x, p4)
    yield
    for shift in (3, 4, 5):
        between = jnp.where(same(shift + 1) & ~same(shift), a_low, 0.0)
        t = _dot3(between, x)
        yield
        x = x - _dot3(x, t)
        yield
    return x


class _GdnChunk:
    def build(self, q, k, v, gb, h):
        C = GDN_CHUNK
        lane = lax.broadcasted_iota(jnp.int32, (C, SM_W), 1)
        g = jnp.sum(jnp.where(lane == h, gb, 0.0), axis=1, keepdims=True)
        self.beta = jnp.sum(jnp.where(lane == h + GDN_HEADS, gb, 0.0), axis=1, keepdims=True)
        ri = lax.broadcasted_iota(jnp.int32, (C, C), 0)
        ci = lax.broadcasted_iota(jnp.int32, (C, C), 1)
        self.causal = ri >= ci
        self.strict = ri > ci
        self.eye = (ri == ci).astype(F32)
        gcb = _dotx(self.causal.astype(F32), jnp.broadcast_to(g, (C, SM_W)))
        yield
        self.gcol = gcb[:, 0:1]
        grow = gcb.T[0:1, 0:C]
        self.decay = jnp.exp(jnp.where(self.causal, self.gcol - grow, -1e30))
        self.egc = jnp.exp(self.gcol)
        glast = gcb[C - 1:C, 0:1]
        self.elast = jnp.exp(glast - self.gcol)
        self.gl = jnp.exp(glast)
        self.q, self.k, self.v = q, k, v
        self.kb = k * self.beta
        m = _dot(self.kb, k, NT)
        n_ = _dot(q, k, NT)
        yield
        self.a_low = jnp.where(self.strict, m * self.decay, 0.0)
        self.p = n_ * self.decay
        self.qd = q * self.egc
        self.kd = k * self.elast
        self.bu = v * self.beta
        self.bw = self.kb * self.egc


GDN_HB = 8
GDN_HG = GDN_HEADS // GDN_HB


def _gdn_specs(n_of):
    C, W = GDN_CHUNK, 128 * GDN_HB
    q_spec = pl.BlockSpec((C, W), lambda g, n: (n_of(n), g))
    k_spec = pl.BlockSpec((C, W), lambda g, n: (n_of(n), g + GDN_HG))
    v_spec = pl.BlockSpec((C, W), lambda g, n: (n_of(n), g + 2 * GDN_HG))
    gb_spec = pl.BlockSpec((C, SM_W), lambda g, n: (n_of(n), 0))
    o_spec = pl.BlockSpec((C, W), lambda g, n: (n_of(n), g))
    s_spec = pl.BlockSpec((GDN_HB, None, GDN_DK, GDN_DV), lambda g, n: (g, n_of(n), 0, 0))
    t_spec = pl.BlockSpec((GDN_HB, None, C, C), lambda g, n: (g, n_of(n), 0, 0))
    return q_spec, k_spec, v_spec, gb_spec, o_spec, s_spec, t_spec


def _gdn_fwd(act, gb, *, name):
    M = act.shape[0]
    N = M // GDN_CHUNK

    def body(q_ref, k_ref, v_ref, gb_ref, o_ref, s_ref, t_ref, state):
        g, n = pl.program_id(0), pl.program_id(1)

        @pl.when(n == 0)
        def _():
            state[...] = jnp.zeros_like(state)

        gb_ = gb_ref[...]

        def head(hh):
            cols = slice(hh * 128, (hh + 1) * 128)
            c = _GdnChunk()
            yield from c.build(q_ref[:, cols], k_ref[:, cols], v_ref[:, cols], gb_, g * GDN_HB + hh)
            tinv = yield from _unit_lower_inverse(c.a_low, c.eye)
            s = state[hh]
            s_ref[hh] = s
            t_ref[hh] = tinv
            u = _dot(tinv, c.bu)
            w = _dot(tinv, c.bw)
            yield
            vn = u - _dot(w, s)
            o1 = _dot(c.qd, s)
            yield
            o_ref[:, cols] = o1 + _dot(c.p, vn)
            state[hh] = c.gl * s + _dot(c.kd, vn, TN)

        _round_robin(head(hh) for hh in range(GDN_HB))

    q_spec, k_spec, v_spec, gb_spec, o_spec, s_spec, t_spec = _gdn_specs(lambda n: n)
    return pl.pallas_call(
        body, name=name, grid=(GDN_HG, N),
        in_specs=[q_spec, k_spec, v_spec, gb_spec], out_specs=[o_spec, s_spec, t_spec],
        out_shape=[jax.ShapeDtypeStruct((M, GDN_V), F32),
                   jax.ShapeDtypeStruct((GDN_HEADS, N, GDN_DK, GDN_DV), F32),
                   jax.ShapeDtypeStruct((GDN_HEADS, N, GDN_CHUNK, GDN_CHUNK), F32)],
        scratch_shapes=[pltpu.VMEM((GDN_HB, GDN_DK, GDN_DV), F32)],
        compiler_params=_params("parallel", "arbitrary"),
    )(act, act, act, gb)


def _gdn_bwd(act, gb, do, s_all, t_all, *, name):
    M = act.shape[0]
    N = M // GDN_CHUNK
    C = GDN_CHUNK
    assert GDN_HG == 1

    def body(q_ref, k_ref, v_ref, gb_ref, do_ref, s_ref, t_ref, dact_ref, dgb_ref, dstate):
        g, n = pl.program_id(0), pl.program_id(1)

        @pl.when(n == 0)
        def _():
            dstate[...] = jnp.zeros_like(dstate)

        gb_ = gb_ref[...]
        last = lax.broadcasted_iota(jnp.int32, (C, 1), 0) == C - 1
        upper = (lax.broadcasted_iota(jnp.int32, (C, C), 0) <= lax.broadcasted_iota(jnp.int32, (C, C), 1)).astype(F32)
        lane = lax.broadcasted_iota(jnp.int32, (C, SM_W), 1)
        def head(hh):
            cols = slice(hh * 128, (hh + 1) * 128)
            h = g * GDN_HB + hh
            c = _GdnChunk()
            yield from c.build(q_ref[:, cols], k_ref[:, cols], v_ref[:, cols], gb_, h)
            tinv = t_ref[hh]
            tinv_t = tinv.T
            s = s_ref[hh]
            do_ = do_ref[:, cols]
            ds1 = dstate[hh]
            u = _dot(tinv, c.bu)
            w = _dot(tinv, c.bw)
            dqd = _dot(do_, s, NT)
            dvn0 = _dot(c.p, do_, TN) + _dot(c.kd, ds1)
            dst0 = _dot(c.qd, do_, TN) + c.gl * ds1
            yield
            vn = u - _dot(w, s)
            dvn = dvn0
            yield
            dp = jnp.where(c.causal, _dot(do_, vn, NT), 0.0)
            dstate[hh] = dst0 - _dot(w, dvn, TN)
            dkd = _dot(vn, ds1, NT)
            dw = -_dot(dvn, s, NT)
            dbu = _dot(tinv_t, dvn)
            dgl = jnp.sum(jnp.sum(s * ds1, axis=1, keepdims=True), axis=0, keepdims=True)
            yield
            dbw = _dot(tinv_t, dw)
            t1 = _dot(dbu, u, NT)
            yield
            da = jnp.where(c.strict, -(t1 + _dot(dbw, w, NT)), 0.0)
            dn_ = dp * c.decay
            dq0 = _dot(dn_, c.k)
            dk0 = _dot(dn_, c.q, TN)
            yield
            dm = da * c.decay
            e = da * c.a_low + dp * c.p
            dkb = _dot(dm, c.k) + dbw * c.egc
            dact_ref[:, GDN_QK + hh * 128:GDN_QK + (hh + 1) * 128] = (
                _dot(dm, c.kb, TN) + dk0 + dkb * c.beta + dkd * c.elast)
            dact_ref[:, cols] = dq0 + dqd * c.egc
            dact_ref[:, 2 * GDN_QK + hh * 128:2 * GDN_QK + (hh + 1) * 128] = dbu * c.beta
            dbeta = jnp.sum(dbu * c.v, axis=1, keepdims=True) + jnp.sum(dkb * c.k, axis=1, keepdims=True)
            t_kd = jnp.sum(dkd * c.kd, axis=1, keepdims=True)
            dgc = (jnp.sum(e, axis=1, keepdims=True) - jnp.sum(e.T, axis=1, keepdims=True)
                   + jnp.sum(dbw * c.bw, axis=1, keepdims=True) + jnp.sum(dqd * c.qd, axis=1, keepdims=True) - t_kd)
            dgc = dgc + jnp.where(last, jnp.sum(t_kd, axis=0, keepdims=True) + dgl * c.gl, 0.0)
            yield
            dg = _dotx(upper, jnp.broadcast_to(dgc, (C, SM_W)))
            dgb_ref[hh] = jnp.where(lane == h, dg, jnp.where(lane == h + GDN_HEADS, dbeta, 0.0))

        _round_robin(head(hh) for hh in range(GDN_HB))

    rev = lambda n: N - 1 - n
    q_spec, k_spec, v_spec, gb_spec, o_spec, s_spec, t_spec = _gdn_specs(rev)
    dgb_spec = pl.BlockSpec((GDN_HB, C, SM_W), lambda g, n: (g, rev(n), 0))
    return pl.pallas_call(
        body, name=name, grid=(GDN_HG, N),
        in_specs=[q_spec, k_spec, v_spec, gb_spec, o_spec, s_spec, t_spec],
        out_specs=[pl.BlockSpec((C, 2 * GDN_QK + GDN_V), lambda g, n: (rev(n), 0)), dgb_spec],
        out_shape=[jax.ShapeDtypeStruct((M, 2 * GDN_QK + GDN_V), F32),
                   jax.ShapeDtypeStruct((GDN_HEADS, M, SM_W), F32)],
        scratch_shapes=[pltpu.VMEM((GDN_HB, GDN_DK, GDN_DV), F32)],
        compiler_params=_params("parallel", "arbitrary"),
    )(act, act, act, gb, do, s_all, t_all)


GLA_STEP_ROWS = 64
GLA_SUB = GLA_STEP_ROWS // GLA_CHUNK


def _gla_cumsum(la):
    C = GLA_CHUNK
    ltri = (lax.broadcasted_iota(jnp.int32, (C, C), 0) >= lax.broadcasted_iota(jnp.int32, (C, C), 1)).astype(F32)
    return _dotx(ltri, la)


GLA_HALF = GLA_CHUNK // 2


def _gla_cross_factors(b):
    top = lax.broadcasted_iota(jnp.int32, b.shape, 0) < GLA_HALF
    bm = b[GLA_HALF - 1:GLA_HALF, :]
    late = jnp.where(top, 0.0, jnp.exp(jnp.minimum(b - bm, 0.0)))
    early = jnp.where(top, jnp.exp(jnp.minimum(bm - b, 0.0)), 0.0)
    return late, early


def _gla_half_decay(bh, ii):
    rj = lax.broadcasted_iota(jnp.int32, bh.shape, 0)
    return jnp.where(rj <= ii, jnp.exp(jnp.minimum(bh[ii:ii + 1, :] - bh, 0.0)), 0.0)


def _gla_scores_t(q, k, b):
    C, H = GLA_CHUNK, GLA_HALF
    lane = lax.broadcasted_iota(jnp.int32, (H, C), 1)
    halves = []
    for h0 in (0, H):
        qh, kh, bh = q[h0:h0 + H], k[h0:h0 + H], b[h0:h0 + H]
        sth = jnp.zeros((H, C), F32)
        for ii in range(H):
            si = jnp.sum(qh[ii:ii + 1, :] * kh * _gla_half_decay(bh, ii), axis=1, keepdims=True)
            sth = jnp.where(lane == h0 + ii, si, sth)
            if ii % 4 == 3:
                yield
        halves.append(sth)
    late, early = _gla_cross_factors(b)
    between = _dot(k * early, q * late, NT)
    yield
    return jnp.concatenate(halves, axis=0) + between


def _gla_specs(n_of):
    R = GLA_STEP_ROWS
    q_spec = pl.BlockSpec((R, GLA_QK), lambda n: (n_of(n), C_GQ // GLA_QK))
    k_spec = pl.BlockSpec((R, GLA_QK), lambda n: (n_of(n), C_GK // GLA_QK))
    v_spec = pl.BlockSpec((R, GLA_V), lambda n: (n_of(n), C_GV // GLA_V))
    la_spec = pl.BlockSpec((R, GLA_QK), lambda n: (n_of(n), 0))
    o_spec = pl.BlockSpec((R, GLA_V), lambda n: (n_of(n), 0))
    s_spec = pl.BlockSpec((GLA_HEADS, None, GLA_SUB, GLA_DV, GLA_DK), lambda n: (0, n_of(n), 0, 0, 0))
    return q_spec, k_spec, v_spec, la_spec, o_spec, s_spec


def _gla_fwd(proj, la, *, name):
    M = proj.shape[0]
    N = M // GLA_STEP_ROWS
    C = GLA_CHUNK

    def body(q_ref, k_ref, v_ref, la_ref, o_ref, s_ref, state):
        n = pl.program_id(0)

        @pl.when(n == 0)
        def _():
            state[...] = jnp.zeros_like(state)

        local = {}

        def within(hh, c):
            kc = slice(hh * GLA_DK, (hh + 1) * GLA_DK)
            vc = slice(hh * GLA_DV, (hh + 1) * GLA_DV)
            rows = slice(c * C, (c + 1) * C)
            q = q_ref[rows, kc] * (GLA_DK ** -0.5)
            k = k_ref[rows, kc]
            v = v_ref[rows, vc]
            b = _gla_cumsum(la_ref[rows, kc])
            yield
            blast = b[C - 1:C, :]
            sc_t = yield from _gla_scores_t(q, k, b)
            kv = _dot(v, k * jnp.exp(blast - b), TN)
            o2 = _dot(sc_t, v, TN)
            yield
            local[hh, c] = (q * jnp.exp(b), jnp.exp(blast), kv, o2)

        def across(hh):
            vc = slice(hh * GLA_DV, (hh + 1) * GLA_DV)
            st = state[hh]
            for c in range(GLA_SUB):
                qe, eblast, kv, o2 = local[hh, c]
                s_ref[hh, c] = st
                o1 = _dot(qe, st, NT)
                yield
                o_ref[c * C:(c + 1) * C, vc] = o1 + o2
                st = st * eblast + kv
            state[hh] = st

        _round_robin(within(hh, c) for c in range(GLA_SUB) for hh in range(GLA_HEADS))
        _round_robin(across(hh) for hh in range(GLA_HEADS))

    q_spec, k_spec, v_spec, la_spec, o_spec, s_spec = _gla_specs(lambda n: n)
    return pl.pallas_call(
        body, name=name, grid=(N,),
        in_specs=[q_spec, k_spec, v_spec, la_spec], out_specs=[o_spec, s_spec],
        out_shape=[jax.ShapeDtypeStruct((M, GLA_V), F32),
                   jax.ShapeDtypeStruct((GLA_HEADS, N, GLA_SUB, GLA_DV, GLA_DK), F32)],
        scratch_shapes=[pltpu.VMEM((GLA_HEADS, GLA_DV, GLA_DK), F32)],
        compiler_params=_params("arbitrary"),
    )(proj, proj, proj, la)


def _gla_bwd(proj, la, do, s_all, d_proj, *, name):
    M = proj.shape[0]
    N = M // GLA_STEP_ROWS
    C = GLA_CHUNK
    qkv_w = 2 * GLA_QK + GLA_V
    assert C_GK == C_GQ + GLA_QK and C_GV == C_GK + GLA_QK and C_GQ % qkv_w == 0

    def body(q_ref, k_ref, v_ref, la_ref, do_ref, s_ref, _, dp_ref, dla_ref, dstate):
        n = pl.program_id(0)

        @pl.when(n == 0)
        def _():
            dstate[...] = jnp.zeros_like(dstate)

        H = GLA_HALF
        lane = lax.broadcasted_iota(jnp.int32, (C, C), 1)
        row = lax.broadcasted_iota(jnp.int32, (C, C), 0)
        ri = lax.broadcasted_iota(jnp.int32, (C, GLA_DK), 0)
        lane_h = lax.broadcasted_iota(jnp.int32, (H, C), 1)
        ri_h = lax.broadcasted_iota(jnp.int32, (H, GLA_DK), 0)
        cross = (row < H) & (lane >= H)
        upper = (row <= lane).astype(F32)
        def head(hh):
            kc = slice(hh * GLA_DK, (hh + 1) * GLA_DK)
            vc = slice(hh * GLA_DV, (hh + 1) * GLA_DV)
            ds1 = dstate[hh]
            for c in reversed(range(GLA_SUB)):
                rows = slice(c * C, (c + 1) * C)
                q = q_ref[rows, kc] * (GLA_DK ** -0.5)
                k = k_ref[rows, kc]
                v = v_ref[rows, vc]
                b = _gla_cumsum(la_ref[rows, kc])
                do_ = do_ref[rows, vc]
                st = s_ref[hh, c]
                dsc_t = _dot(v, do_, NT)
                dqe = _dot(do_, st)
                dke = _dot(v, ds1)
                yield
                blast = b[C - 1:C, :]
                eb = jnp.exp(b)
                elast = jnp.exp(blast - b)
                eblast = jnp.exp(blast)
                qe = q * eb
                ke = k * elast
                dv2 = _dot(ke, ds1, NT)
                ds_new = _dot(do_, qe, TN)
                deblast = jnp.sum(st * ds1, axis=0, keepdims=True)
                sc_halves, dq_halves, dk_halves = [], [], []
                for h0 in (0, H):
                    qh, kh, bh, dsch = q[h0:h0 + H], k[h0:h0 + H], b[h0:h0 + H], dsc_t[h0:h0 + H]
                    sch = jnp.zeros((H, C), F32)
                    dqh = jnp.zeros((H, GLA_DK), F32)
                    dkh = jnp.zeros((H, GLA_DK), F32)
                    for ii in range(H):
                        f = _gla_half_decay(bh, ii)
                        kf = kh * f
                        si = jnp.sum(qh[ii:ii + 1, :] * kf, axis=1, keepdims=True)
                        sch = jnp.where(lane_h == h0 + ii, si, sch)
                        dsi = jnp.sum(jnp.where(lane_h == h0 + ii, dsch, 0.0), axis=1, keepdims=True)
                        dqh = jnp.where(ri_h == ii, jnp.sum(dsi * kf, axis=0, keepdims=True), dqh)
                        dkh = dkh + (dsi * f) * qh[ii:ii + 1, :]
                        if ii % 4 == 3:
                            yield
                    sc_halves.append(sch)
                    dq_halves.append(dqh)
                    dk_halves.append(dkh)
                late, early = _gla_cross_factors(b)
                q_late, k_early = q * late, k * early
                dsc_x = jnp.where(cross, dsc_t, 0.0)
                sc_t = jnp.concatenate(sc_halves, axis=0) + _dot(k_early, q_late, NT)
                dq_sc = jnp.concatenate(dq_halves, axis=0) + _dot(dsc_x, k_early, TN) * late
                dk_sc = jnp.concatenate(dk_halves, axis=0) + _dot(dsc_x, q_late) * early
                yield
                dv1 = _dot(sc_t, do_)
                dp_ref[rows, kc] = ((dq_sc + dqe * eb) * (GLA_DK ** -0.5)).astype(dp_ref.dtype)
                dp_ref[rows, GLA_QK + hh * GLA_DK:GLA_QK + (hh + 1) * GLA_DK] = (dk_sc + dke * elast).astype(dp_ref.dtype)
                t_ke = dke * ke
                db = q * dq_sc - k * dk_sc + dqe * qe - t_ke
                db = db + jnp.where(ri == C - 1, jnp.sum(t_ke, axis=0, keepdims=True) + deblast * eblast, 0.0)
                dla = _dotx(upper, db)
                yield
                dp_ref[rows, 2 * GLA_QK + hh * GLA_DV:2 * GLA_QK + (hh + 1) * GLA_DV] = (dv1 + dv2).astype(dp_ref.dtype)
                dla_ref[rows, kc] = dla
                ds1 = ds1 * eblast + ds_new
            dstate[hh] = ds1

        _round_robin(head(hh) for hh in range(GLA_HEADS))

    rev = lambda n: N - 1 - n
    q_spec, k_spec, v_spec, la_spec, o_spec, s_spec = _gla_specs(rev)
    return pl.pallas_call(
        body, name=name, grid=(N,),
        in_specs=[q_spec, k_spec, v_spec, la_spec, o_spec, s_spec, _ANY],
        out_specs=[pl.BlockSpec((GLA_STEP_ROWS, qkv_w), lambda n: (rev(n), C_GQ // qkv_w)), la_spec],
        out_shape=[jax.ShapeDtypeStruct(d_proj.shape, d_proj.dtype), jax.ShapeDtypeStruct((M, GLA_QK), F32)],
        input_output_aliases={6: 0},
        scratch_shapes=[pltpu.VMEM((GLA_HEADS, GLA_DV, GLA_DK), F32)],
        compiler_params=_params("arbitrary"),
    )(proj, proj, proj, la, do, s_all, d_proj)


def _head_norm(o, wn):
    r = lax.rsqrt(jnp.mean(o * o, axis=-1, keepdims=True) + NORM_EPS)
    return o * r, r


def _mix_heads():
    heads = [(0, GDN_DV, hh * GDN_DV, hh * GDN_DV) for hh in range(GDN_HEADS)]
    heads += [(1, GLA_DV, GDN_V + hh * GLA_DV, hh * GLA_DV) for hh in range(GLA_HEADS)]
    return heads


def _mix_fwd(o_gdn, o_gla, proj, wn_gdn, wn_gla, *, name):
    M = proj.shape[0]
    tm = _tile(M, 344, 16)

    def body(og_ref, ol_ref, z_ref, r_ref, wg_ref, wl_ref, m_ref):
        srcs = ((og_ref, z_ref, wg_ref), (ol_ref, r_ref, wl_ref))
        for grp, width, mcol, col in _mix_heads():
            o_ref, gate_ref, w_ref = srcs[grp]
            xhat, _ = _head_norm(o_ref[:, col:col + width], None)
            gate, _ = _silu_and_grad(gate_ref[:, col:col + width])
            m_ref[:, mcol:mcol + width] = (xhat * w_ref[...] * gate).astype(m_ref.dtype)

    full = lambda s: pl.BlockSpec(s, lambda i: (0, 0))
    return pl.pallas_call(
        body, name=name, grid=(M // tm,),
        in_specs=[pl.BlockSpec((tm, GDN_V), lambda i: (i, 0)), pl.BlockSpec((tm, GLA_V), lambda i: (i, 0)),
                  pl.BlockSpec((tm, GDN_V), lambda i: (i, C_Z // GDN_V)),
                  pl.BlockSpec((tm, GLA_V), lambda i: (i, C_GR // GLA_V)),
                  full((1, GDN_DV)), full((1, GLA_DV))],
        out_specs=pl.BlockSpec((tm, D_MODEL), lambda i: (i, 0)),
        out_shape=jax.ShapeDtypeStruct((M, D_MODEL), BF16),
        compiler_params=_params("parallel"),
    )(o_gdn, o_gla, proj, proj, wn_gdn, wn_gla)


def _mix_bwd(o_gdn, o_gla, proj, wn_gdn, wn_gla, dmixed, *, name):
    M = proj.shape[0]
    tm = _tile(M, 344, 16)
    g_ = M // tm
    assert C_Z == 0 and C_GR == GDN_V

    def body(og_ref, ol_ref, z_ref, r_ref, wg_ref, wl_ref, dm_ref,
             dog_ref, dol_ref, dzr_ref, dwg_ref, dwl_ref):
        i = pl.program_id(0)
        srcs = ((og_ref, z_ref, wg_ref, dog_ref), (ol_ref, r_ref, wl_ref, dol_ref))
        dws = [jnp.zeros((1, GDN_DV), F32), jnp.zeros((1, GLA_DV), F32)]
        for grp, width, mcol, col in _mix_heads():
            o_ref, gate_ref, w_ref, do_ref = srcs[grp]
            cols = slice(col, col + width)
            xhat, r = _head_norm(o_ref[:, cols], None)
            gate, dgate_dc = _silu_and_grad(gate_ref[:, cols])
            dm = dm_ref[:, mcol:mcol + width]
            dzr_ref[:, mcol:mcol + width] = (dm * xhat * w_ref[...] * dgate_dc).astype(dzr_ref.dtype)
            dnorm = dm * gate
            dws[grp] = dws[grp] + jnp.sum(dnorm * xhat, axis=0, keepdims=True)
            dxhat = dnorm * w_ref[...]
            do_ref[:, cols] = r * (dxhat - xhat * jnp.mean(dxhat * xhat, axis=-1, keepdims=True))

        @pl.when(i == 0)
        def _():
            dwg_ref[...] = dws[0]
            dwl_ref[...] = dws[1]

        @pl.when(i > 0)
        def _():
            dwg_ref[...] += dws[0]
            dwl_ref[...] += dws[1]

    full = lambda s: pl.BlockSpec(s, lambda i: (0, 0))
    half = pl.BlockSpec((tm, GDN_V), lambda i: (i, 0))
    return pl.pallas_call(
        body, name=name, grid=(g_,),
        in_specs=[half, half, pl.BlockSpec((tm, GDN_V), lambda i: (i, C_Z // GDN_V)),
                  pl.BlockSpec((tm, GLA_V), lambda i: (i, C_GR // GLA_V)),
                  full((1, GDN_DV)), full((1, GLA_DV)), pl.BlockSpec((tm, D_MODEL), lambda i: (i, 0))],
        out_specs=[half, half, pl.BlockSpec((tm, GDN_V + GLA_V), lambda i: (i, 0)),
                   full((1, GDN_DV)), full((1, GLA_DV))],
        out_shape=[jax.ShapeDtypeStruct((M, GDN_V), F32), jax.ShapeDtypeStruct((M, GLA_V), F32),
                   jax.ShapeDtypeStruct((M, D_PROJ), BF16),
                   jax.ShapeDtypeStruct((1, GDN_DV), F32), jax.ShapeDtypeStruct((1, GLA_DV), F32)],
        compiler_params=_params("arbitrary"),
    )(o_gdn, o_gla, proj, proj, wn_gdn, wn_gla, dmixed)


def _swiglu_fwd(n, w_gate_t, w_up_t, *, name, tm=1376, tn=512):
    M, D = n.shape
    F = w_gate_t.shape[0]
    tm, tn = _tile(M, tm, 16), _tile(F, tn, 128)

    def body(n_ref, wg_ref, wu_ref, g_ref, u_ref, a_ref):
        x = n_ref[...]
        g = _dot(x, wg_ref[...], NT)
        u = _dot(x, wu_ref[...], NT)
        s, _ = _silu_and_grad(g)
        g_ref[...] = g.astype(g_ref.dtype)
        u_ref[...] = u.astype(u_ref.dtype)
        a_ref[...] = (s * u).astype(a_ref.dtype)

    w_spec = pl.BlockSpec((tn, D), lambda i, j: (j, 0))
    o_spec = pl.BlockSpec((tm, tn), lambda i, j: (i, j))
    return pl.pallas_call(
        body, name=name, grid=(M // tm, F // tn),
        in_specs=[pl.BlockSpec((tm, D), lambda i, j: (i, 0)), w_spec, w_spec], out_specs=[o_spec] * 3,
        out_shape=[jax.ShapeDtypeStruct((M, F), BF16)] * 3, compiler_params=_params("parallel", "parallel"),
    )(n, w_gate_t, w_up_t)


def _swiglu_bwd(dh, w_down, gate, up, *, name, after=None, tm=1376, tn=512):
    M, D = dh.shape
    F = w_down.shape[0]
    tm, tn = _tile(M, tm, 16), _tile(F, tn, 128)
    n_after = 0 if after is None else 1

    def body(*refs):
        dh_ref, w_ref, g_ref, u_ref, dg_ref, du_ref = refs[n_after:]
        da = _dot(dh_ref[...], w_ref[...], NT)
        s, ds = _silu_and_grad(g_ref[...].astype(F32))
        dg_ref[...] = (da * u_ref[...].astype(F32) * ds).astype(dg_ref.dtype)
        du_ref[...] = (da * s).astype(du_ref.dtype)

    o_spec = pl.BlockSpec((tm, tn), lambda i, j: (i, j))
    return pl.pallas_call(
        body, name=name, grid=(M // tm, F // tn),
        in_specs=[_ANY] * n_after + [pl.BlockSpec((tm, D), lambda i, j: (i, 0)),
                                     pl.BlockSpec((tn, D), lambda i, j: (j, 0)), o_spec, o_spec],
        out_specs=[o_spec, o_spec], out_shape=[jax.ShapeDtypeStruct((M, F), BF16)] * 2,
        compiler_params=_params("parallel", "parallel"),
    )(*((after,) if n_after else ()), dh, w_down, gate, up)


def _adamw_update(w, g, m, v):
    nm = ADAM_B1 * m + (1.0 - ADAM_B1) * g
    nv = ADAM_B2 * v + (1.0 - ADAM_B2) * (g * g)
    m_hat = nm / (1.0 - ADAM_B1 ** ADAM_STEP)
    v_hat = nv / (1.0 - ADAM_B2 ** ADAM_STEP)
    return -ADAM_LR * (m_hat / (jnp.sqrt(v_hat) + ADAM_EPS) + ADAM_WD * w), nm, nv


def _adamw(w, g, m, v, *, name):
    shape = w.shape
    cols = shape[-1]
    rows = w.size // cols
    w2, g2, m2, v2 = (t.reshape(rows, cols) for t in (w, g, m, v))
    if rows % 8 == 0 or cols % 128 != 0:
        tr, tc = (_tile(rows, 256, 8) if rows % 8 == 0 else rows), cols
    else:
        tr, tc = rows, _tile(cols, 256, 128)

    def body(w_ref, g_ref, m_ref, v_ref, d_ref, nm_ref, nv_ref):
        d_ref[...], nm_ref[...], nv_ref[...] = _adamw_update(w_ref[...], g_ref[...], m_ref[...], v_ref[...])

    blk = pl.BlockSpec((tr, tc), lambda i, j: (i, j))
    outs = pl.pallas_call(
        body, name=name, grid=(rows // tr, cols // tc), in_specs=[blk] * 4, out_specs=[blk] * 3,
        out_shape=[jax.ShapeDtypeStruct((rows, cols), F32)] * 3, compiler_params=_params("parallel", "parallel"),
    )(w2, g2, m2, v2)
    return tuple(t.reshape(shape) for t in outs)


def _sum_slabs(x, *, name):
    _, R, C = x.shape
    sub = 16 if x.dtype == BF16 else 8
    if R % sub == 0:
        tr, tc = _tile(R, 128, sub), C
    else:
        tr, tc = R, _tile(C, 256, 128)

    def body(x_ref, o_ref):
        acc = x_ref[0].astype(F32)
        for s in range(1, N_DEV):
            acc = acc + x_ref[s].astype(F32)
        o_ref[...] = acc

    return pl.pallas_call(
        body, name=name, grid=(R // tr, C // tc),
        in_specs=[pl.BlockSpec((N_DEV, tr, tc), lambda i, j: (0, i, j))],
        out_specs=pl.BlockSpec((tr, tc), lambda i, j: (i, j)),
        out_shape=jax.ShapeDtypeStruct((R, C), F32), compiler_params=_params("parallel", "parallel"),
    )(x)


def _sum_adamw(x, w, m, v, *, name):
    _, R, C = x.shape
    if R % 16 == 0:
        tr, tc = _tile(R, 128, 16), C
    else:
        tr, tc = R, _tile(C, 256, 128)

    def body(x_ref, w_ref, m_ref, v_ref, g_ref, d_ref, nm_ref, nv_ref):
        g = x_ref[0].astype(F32)
        for s in range(1, N_DEV):
            g = g + x_ref[s].astype(F32)
        g_ref[...] = g
        d_ref[...], nm_ref[...], nv_ref[...] = _adamw_update(w_ref[...], g, m_ref[...], v_ref[...])

    blk = pl.BlockSpec((tr, tc), lambda i, j: (i, j))
    return pl.pallas_call(
        body, name=name, grid=(R // tr, C // tc),
        in_specs=[pl.BlockSpec((N_DEV, tr, tc), lambda i, j: (0, i, j)), blk, blk, blk], out_specs=[blk] * 4,
        out_shape=[jax.ShapeDtypeStruct((R, C), F32)] * 4, compiler_params=_params("parallel", "parallel"),
    )(x, w, m, v)


def _peers():
    x, y, c = lax.axis_index("x"), lax.axis_index("y"), lax.axis_index("c")
    me = 4 * x + 2 * y + c
    peers = []
    for k in range(1, N_DEV):
        px = 1 - x if k & 4 else x
        py = 1 - y if k & 2 else y
        pc = 1 - c if k & 1 else c
        peers.append(((px, py, pc), 4 * px + 2 * py + pc))
    return me, peers


def _gather(x, *, name):
    def body(x_ref, o_ref, send_sems, recv_sems, own_sem):
        me, peers = _peers()
        own = pltpu.make_async_copy(x_ref, o_ref.at[me], own_sem)
        own.start()
        sends, recvs = [], []
        for k, (pos, idx) in enumerate(peers):
            sends.append(pltpu.make_async_remote_copy(
                src_ref=x_ref, dst_ref=o_ref.at[me], send_sem=send_sems.at[k], recv_sem=recv_sems.at[k],
                device_id=pos, device_id_type=pl.DeviceIdType.MESH))
            recvs.append(pltpu.make_async_remote_copy(
                src_ref=x_ref, dst_ref=o_ref.at[idx], send_sem=send_sems.at[k], recv_sem=recv_sems.at[k],
                device_id=pos, device_id_type=pl.DeviceIdType.MESH))
        for cp in sends:
            cp.start()
        for cp in recvs:
            cp.wait_recv()
        for cp in sends:
            cp.wait_send()
        own.wait()

    hbm = pl.BlockSpec(memory_space=pltpu.HBM)
    return pl.pallas_call(
        body, name=name, in_specs=[hbm], out_specs=hbm,
        out_shape=jax.ShapeDtypeStruct((N_DEV,) + tuple(x.shape), x.dtype),
        scratch_shapes=[pltpu.SemaphoreType.DMA((N_DEV - 1,)), pltpu.SemaphoreType.DMA((N_DEV - 1,)),
                        pltpu.SemaphoreType.DMA],
    )(x)


_HBM = pl.BlockSpec(memory_space=pltpu.HBM)
_SEM = pl.BlockSpec(memory_space=pltpu.SEMAPHORE)
_EFFECT = pltpu.SideEffectType.DATAFLOW_SIDE_EFFECTING


PLAN_GATHER = tuple((k, "x", 0) for k in range(1, N_DEV))
PLAN_SCATTER = tuple((k, "xk", 0) for k in range(1, N_DEV))
PLAN_GATHER_CHIPS = tuple((k, "x", 0) for k in (1, 2, 4, 6))
PLAN_GATHER_PASS_ON = tuple((1, ("land", q), q) for q in (2, 4, 6))


def _plan_refs(plan, j, x_ref, land_ref, me, peers, receiving):
    k, source, r = plan[j]
    index_of = lambda q: me if q == 0 else peers[q - 1][1]
    pos, target = peers[k - 1]
    if source == "x":
        src = x_ref
    elif source == "xk":
        src = x_ref.at[target]
    else:
        src = land_ref.at[index_of(source[1])]
    return pos, src, land_ref.at[index_of(k ^ r) if receiving else index_of(r)]


def _exchange_start(x, *, plan, name, after=None, land=None, slab=None):
    n_after = 0 if after is None else 1
    n = len(plan)

    def body(*refs):
        x_ref, land_ref, send_sems, recv_sems, _, _, token = refs[n_after:]
        me, peers = _peers()
        for j in range(n):
            pos, src, dst = _plan_refs(plan, j, x_ref, land_ref, me, peers, receiving=False)
            pltpu.make_async_remote_copy(src_ref=src, dst_ref=dst, send_sem=send_sems.at[j], recv_sem=recv_sems.at[j],
                                         device_id=pos, device_id_type=pl.DeviceIdType.MESH).start()
        token[...] = jnp.zeros_like(token)

    if land is None:
        land = lax.empty((N_DEV,) + tuple(slab), x.dtype)
    return pl.pallas_call(
        body, name=name,
        out_shape=(pltpu.SemaphoreType.DMA((n,)), pltpu.SemaphoreType.DMA((n,)),
                   pltpu.HBM(x.shape, x.dtype), pltpu.HBM(land.shape, land.dtype), jax.ShapeDtypeStruct((8, 128), F32)),
        in_specs=[_ANY] * n_after + [_HBM, _HBM],
        out_specs=(_SEM, _SEM, _HBM, _HBM, pl.BlockSpec(memory_space=pltpu.VMEM)),
        input_output_aliases={n_after: 2, n_after + 1: 3},
        compiler_params=pltpu.CompilerParams(has_side_effects=_EFFECT),
    )(*((after,) if n_after else ()), pltpu.with_memory_space_constraint(x, pltpu.HBM),
      pltpu.with_memory_space_constraint(land, pltpu.HBM))


def _exchange_wait(handle, after, *, plan, name):
    send_sems, recv_sems, x_thru, land_thru, _ = handle
    afters = list(after) if isinstance(after, (list, tuple)) else [after]

    def body(x_ref, land_ref, send_sems, recv_sems, *rest):
        me, peers = _peers()
        for j in range(len(plan)):
            pos, src, dst = _plan_refs(plan, j, x_ref, land_ref, me, peers, receiving=True)
            cp = pltpu.make_async_remote_copy(src_ref=src, dst_ref=dst, send_sem=send_sems.at[j], recv_sem=recv_sems.at[j],
                                              device_id=pos, device_id_type=pl.DeviceIdType.MESH)
            cp.wait_send()
            cp.wait_recv()

    return pl.pallas_call(
        body, name=name,
        out_shape=(pltpu.HBM(x_thru.shape, x_thru.dtype), pltpu.HBM(land_thru.shape, land_thru.dtype)),
        in_specs=[_HBM, _HBM, _SEM, _SEM] + [_ANY] * len(afters), out_specs=(_HBM, _HBM),
        input_output_aliases={0: 0, 1: 1}, compiler_params=pltpu.CompilerParams(has_side_effects=_EFFECT),
    )(x_thru, land_thru, send_sems, recv_sems, *afters)


W_IN_SLAB = D_IN // N_DEV


def _to_proj_rows(t):
    z = jnp.zeros((D_PROJ - C_SM - 2 * GDN_HEADS - GLA_RANK,) + t.shape[1:], t.dtype)
    return jnp.concatenate([t[R_Z:R_A], t[R_GR:R_LR], t[R_GQ:R_GR], t[:R_Z], t[R_A:R_GQ], t[R_LR:], z], axis=0)


def _from_proj_rows(t):
    ab = C_SM + 2 * GDN_HEADS
    return jnp.concatenate([t[C_QKV:C_SM], t[C_Z:C_GR], t[C_SM:ab], t[C_GQ:C_QKV], t[C_GR:C_GQ],
                            t[ab:ab + GLA_RANK]], axis=0)


def _local_step(x, target, meta, attn_nw, conv_w, a_log, dt_bias, gdn_nw, w2, b2, gla_nw, ffn_nw, final_nw,
                fetch, emit, start=None):
    S = x.shape[0]
    head = jnp.concatenate([jnp.zeros((ROW_PAD, D_MODEL), F32), meta], axis=0)
    conv_w8 = jnp.concatenate([conv_w, jnp.zeros((8 - CONV_K, conv_w.shape[1]), F32)], axis=0)
    w2p = jnp.zeros((SM_W, GLA_QK), F32).at[2 * GDN_HEADS:2 * GDN_HEADS + GLA_RANK].set(w2)
    alog_p = jnp.zeros((1, SM_W), F32).at[:, :GDN_HEADS].set(a_log)
    dt_p = jnp.zeros((1, SM_W), F32).at[:, :GDN_HEADS].set(dt_bias)

    h0, n1 = _embed_norm(head, x, attn_nw, name="attn_norm", after=start)
    w_in_t = fetch("w_in_t", (n1, conv_w8, w2p, alog_p, dt_p))
    proj = _matmul(n1, w_in_t, mode="nt", name="in_proj")
    gb, la = _gates_fwd(proj, w2p, b2, alog_p, dt_p, name="gates")
    act = _prep_fwd(proj, conv_w8, name="gdn_prep")
    o_gdn, s_gdn, t_gdn = _gdn_fwd(act, gb, name="gdn_fwd")
    o_gla, s_gla = _gla_fwd(proj, la, name="gla_fwd")
    mixed = _mix_fwd(o_gdn, o_gla, proj, gdn_nw, gla_nw, name="mix")
    w_out = fetch("w_out", mixed)
    h1 = _matmul(mixed, w_out, mode="nn", add=h0, name="out_proj")
    n2 = _rmsnorm_fwd(h1, ffn_nw, name="ffn_norm")
    w_gate_t, w_up_t = fetch("w_gate_t", n2), fetch("w_up_t", n2)
    gate, up, hid = _swiglu_fwd(n2, w_gate_t, w_up_t, name="swiglu")
    w_down = fetch("w_down", hid)
    h2 = _matmul(hid, w_down, mode="nn", add=h1, name="ffn_down", tm=1376, tn=256)
    dh2, dh2_b, d_final_nw, loss = _loss_head(h2, final_nw, target, name="loss_head")

    wg = dict(mode="tn", out_dtype=BF16, tn=512)
    tok = emit("w_down", _matmul(hid, dh2_b, name="d_w_down", tm=704, **wg))
    d_gate, d_up = _swiglu_bwd(dh2_b, w_down, gate, up, name="d_swiglu", after=tok)
    tok = emit("w_gate_t", _matmul(d_gate, n2, name="d_w_gate", tm=704, **wg))
    tok = emit("w_up_t", _matmul(d_up, n2, name="d_w_up", tm=704, after=tok, **wg))
    d_n2 = _matmul_pair(d_gate, w_gate_t, d_up, w_up_t, name="d_n2", after=tok)
    dh1, dh1_b, d_ffn_nw = _rmsnorm_bwd(h1, ffn_nw, d_n2, dh2, name="d_ffn_norm")

    tok = emit("w_out", _matmul(mixed, dh1_b, name="d_w_out", tm=512, **wg))
    d_mixed = _matmul(dh1_b, w_out, mode="nt", name="d_mixed", after=tok)
    do_gdn, do_gla, d_proj, d_gdn_nw, d_gla_nw = _mix_bwd(o_gdn, o_gla, proj, gdn_nw, gla_nw, d_mixed, name="d_mix")
    d_proj, d_la = _gla_bwd(proj, la, do_gla, s_gla, d_proj, name="gla_bwd")
    dact, dgb_heads = _gdn_bwd(act, gb, do_gdn, s_gdn, t_gdn, name="gdn_bwd")
    d_proj, d_w2p, d_b2, d_alog, d_dt = _gates_bwd(proj, w2p, b2, alog_p, dt_p, dgb_heads, d_la, d_proj, name="d_gates")
    d_proj, d_conv_w8 = _prep_bwd(proj, conv_w8, dact, d_proj, name="d_gdn_prep")
    tok = emit("w_in_t", _matmul(d_proj, n1, name="d_w_in", tm=768, **wg))
    d_n1 = _matmul(d_proj, w_in_t, mode="nn", name="d_n1", tm=688, after=tok)
    grad_x, d_head, d_attn_nw = _embed_norm_bwd(h0, attn_nw, d_n1, dh1, name="d_attn_norm")

    return dict(
        loss=loss[0, 0], grad_x=grad_x, meta=d_head[ROW_PAD:HEAD_ROWS], attn_nw=d_attn_nw,
        conv_w=d_conv_w8[:CONV_K], a_log=d_alog[:, :GDN_HEADS], dt_bias=d_dt[:, :GDN_HEADS], gdn_nw=d_gdn_nw,
        w2=d_w2p[2 * GDN_HEADS:2 * GDN_HEADS + GLA_RANK], b2=d_b2, gla_nw=d_gla_nw, ffn_nw=d_ffn_nw,
        final_nw=d_final_nw)


SMALL_ROWS = 32


def kernel(x, meta_tokens, attn_norm_w, w_in, gdn_conv_w, gdn_a_log, gdn_dt_bias, gdn_norm_w, gla_gate_w2, gla_gate_b, gla_norm_w, w_out, ffn_norm_w, w_gate, w_up, w_down, final_norm_w, loss_target, m_meta_tokens, m_attn_norm_w, m_w_in, m_gdn_conv_w, m_gdn_a_log, m_gdn_dt_bias, m_gdn_norm_w, m_gla_gate_w2, m_gla_gate_b, m_gla_norm_w, m_w_out, m_ffn_norm_w, m_w_gate, m_w_up, m_w_down, m_final_norm_w, v_meta_tokens, v_attn_norm_w, v_w_in, v_gdn_conv_w, v_gdn_a_log, v_gdn_dt_bias, v_gdn_norm_w, v_gla_gate_w2, v_gla_gate_b, v_gla_norm_w, v_w_out, v_ffn_norm_w, v_w_gate, v_w_up, v_w_down, v_final_norm_w):
    me = 4 * lax.axis_index("x") + 2 * lax.axis_index("y") + lax.axis_index("c")

    n_conv = gdn_conv_w.shape[2]
    n_w2 = gla_gate_w2.shape[2]
    n_meta = meta_tokens.shape[1]
    small = jnp.zeros((40, n_conv), F32)
    small = small.at[0:N_META, :n_meta].set(meta_tokens)
    small = small.at[N_META:N_META + CONV_K, :].set(gdn_conv_w[0])
    small = small.at[24:24 + GLA_RANK, :n_w2].set(gla_gate_w2[0])
    small_all = _gather(small, name="gather_small")
    meta_f = small_all[:, 0:N_META, :n_meta].transpose(1, 0, 2).reshape(N_META, D_MODEL)
    conv_f = small_all[:, N_META:N_META + CONV_K, :].transpose(1, 0, 2).reshape(CONV_K, N_DEV * n_conv)
    w2_f = small_all[:, 24:24 + GLA_RANK, :n_w2].transpose(1, 0, 2).reshape(GLA_RANK, N_DEV * n_w2)

    w_in_slab = w_in[0].T.astype(BF16)
    in_h = _exchange_start(w_in_slab, plan=PLAN_GATHER_CHIPS, slab=w_in_slab.shape, name="gather_w_in_start",
                           after=small_all)
    handles, tok = {}, in_h[4]
    for wname, slab in (("w_out", w_out[0]), ("w_gate_t", w_gate[0].T), ("w_up_t", w_up[0].T), ("w_down", w_down[0])):
        slab = slab.astype(BF16)
        handles[wname] = _exchange_start(slab, plan=PLAN_GATHER, slab=slab.shape, name="gather_" + wname + "_start", after=tok)
        tok = handles[wname][4]

    def fetch(name, after):
        if name == "w_in_t":
            own, got = _exchange_wait(in_h, after, plan=PLAN_GATHER_CHIPS, name="gather_w_in_wait")
            pass_h = _exchange_start(own, plan=PLAN_GATHER_PASS_ON, land=got, name="pass_w_in_start")
            own, got = _exchange_wait(pass_h, pass_h[4], plan=PLAN_GATHER_PASS_ON, name="pass_w_in_wait")
            got = lax.dynamic_update_index_in_dim(got, own, me, 0)
            return _to_proj_rows(got.reshape(D_IN, D_MODEL))
        own, got = _exchange_wait(handles[name], after, plan=PLAN_GATHER, name="gather_" + name + "_wait")
        got = lax.dynamic_update_index_in_dim(got, own, me, 0)
        return got.reshape(N_DEV * got.shape[1], D_MODEL)

    sent = {}

    def emit(name, grad):
        if name == "w_in_t":
            grad = _from_proj_rows(grad)
        parts = grad.reshape(N_DEV, grad.shape[0] // N_DEV, D_MODEL)
        sent[name] = _exchange_start(parts, plan=PLAN_SCATTER, slab=parts.shape[1:], name="scatter_" + name + "_start")
        return sent[name][4]

    g = _local_step(x[0], loss_target[0], meta_f, attn_norm_w, conv_f, gdn_a_log, gdn_dt_bias, gdn_norm_w, w2_f,
                    gla_gate_b, gla_norm_w, ffn_norm_w, final_norm_w.reshape(1, D_MODEL), fetch, emit, start=tok)

    big = {}
    after = g["attn_nw"]
    for name, w, m, v, transposed in (("w_down", w_down, m_w_down, v_w_down, False), ("w_gate_t", w_gate, m_w_gate, v_w_gate, True),
                                      ("w_up_t", w_up, m_w_up, v_w_up, True), ("w_out", w_out, m_w_out, v_w_out, False),
                                      ("w_in_t", w_in, m_w_in, v_w_in, True)):
        own, got = _exchange_wait(sent[name], after, plan=PLAN_SCATTER, name="scatter_" + name + "_wait")
        got = lax.dynamic_update_index_in_dim(got, lax.dynamic_index_in_dim(own, me, 0, keepdims=False), me, 0)
        local = [t[0].T if transposed else t[0] for t in (w, m, v)]
        res = _sum_adamw(got, *local, name="adamw_" + name)
        big[name] = [t.T[None] if transposed else t[None] for t in res]
        after = res[0]

    misc = jnp.concatenate([g["a_log"], g["dt_bias"], g["gdn_nw"], g["gla_nw"], g["b2"], g["loss"].reshape(1, 1)], axis=1)
    n_misc = misc.shape[1]
    misc = jnp.pad(misc, ((0, 0), (0, D_MODEL - n_misc)))
    rows = jnp.concatenate([g["attn_nw"], g["ffn_nw"], g["final_nw"], misc, g["meta"],
                            g["conv_w"].reshape(-1, D_MODEL), g["w2"].reshape(-1, D_MODEL)], axis=0)
    rows = jnp.pad(rows, ((0, SMALL_ROWS - rows.shape[0]), (0, 0)))
    tot = _sum_slabs(_gather(rows, name="gather_small_grads"), name="sum_small_grads")
    grad_attn_nw, grad_ffn_nw, grad_final_nw = tot[0:1], tot[1:2], tot[2]
    grad_a_log = tot[3:4, 0:8]
    grad_dt = tot[3:4, 8:16]
    grad_gdn_nw = tot[3:4, 16:16 + GDN_DV]
    grad_gla_nw = tot[3:4, 144:144 + GLA_DV]
    grad_b2 = tot[3:4, 400:400 + GLA_QK]
    loss = tot[3, n_misc - 1]
    r0 = 4 + N_META
    grad_meta = lax.dynamic_slice(tot[4:r0], (0, me * n_meta), (N_META, n_meta))
    r1 = r0 + CONV_K * N_DEV * n_conv // D_MODEL
    grad_conv = lax.dynamic_slice(tot[r0:r1].reshape(CONV_K, N_DEV * n_conv), (0, me * n_conv), (CONV_K, n_conv))[None]
    r2 = r1 + GLA_RANK * N_DEV * n_w2 // D_MODEL
    grad_w2 = lax.dynamic_slice(tot[r1:r2].reshape(GLA_RANK, N_DEV * n_w2), (0, me * n_w2), (GLA_RANK, n_w2))[None]

    weights = [meta_tokens, attn_norm_w, w_in, gdn_conv_w, gdn_a_log, gdn_dt_bias, gdn_norm_w, gla_gate_w2,
               gla_gate_b, gla_norm_w, w_out, ffn_norm_w, w_gate, w_up, w_down, final_norm_w]
    grads = [grad_meta, grad_attn_nw, "w_in_t", grad_conv, grad_a_log, grad_dt, grad_gdn_nw, grad_w2,
             grad_b2, grad_gla_nw, "w_out", grad_ffn_nw, "w_gate_t", "w_up_t", "w_down", grad_final_nw]
    ms = [m_meta_tokens, m_attn_norm_w, m_w_in, m_gdn_conv_w, m_gdn_a_log, m_gdn_dt_bias, m_gdn_norm_w,
          m_gla_gate_w2, m_gla_gate_b, m_gla_norm_w, m_w_out, m_ffn_norm_w, m_w_gate, m_w_up, m_w_down, m_final_norm_w]
    vs = [v_meta_tokens, v_attn_norm_w, v_w_in, v_gdn_conv_w, v_gdn_a_log, v_gdn_dt_bias, v_gdn_norm_w,
          v_gla_gate_w2, v_gla_gate_b, v_gla_norm_w, v_w_out, v_ffn_norm_w, v_w_gate, v_w_up, v_w_down, v_final_norm_w]
    outs = [[], [], [], []]
    for idx, (w, gr, m, v) in enumerate(zip(weights, grads, ms, vs)):
        if isinstance(gr, str):
            res = big[gr]
        else:
            gr = gr.reshape(w.shape)
            res = (gr,) + _adamw(w, gr, m, v, name=f"adamw_{idx}")
        for lst, t in zip(outs, res):
            lst.append(t)
    return (loss, g["grad_x"][None], *outs[0], *outs[1], *outs[2], *outs[3])
```

```python
import functools

import jax
import jax.numpy as jnp
from jax import lax
from jax.experimental import pallas as pl
from jax.experimental.pallas import tpu as pltpu

F32 = jnp.float32
BF16 = jnp.bfloat16
_MXU_DTYPE = jnp.bfloat16

D_MODEL = 2048
N_META = 16
ROW_PAD = 48
HEAD_ROWS = ROW_PAD + N_META
CONV_K = 4
GDN_HEADS, GDN_DK, GDN_DV, GDN_CHUNK = 8, 128, 128, 64
GLA_HEADS, GLA_DK, GLA_DV, GLA_CHUNK = 4, 128, 256, 16
GLA_RANK = 16
GLA_GATE_NORMALIZER = 16.0
GDN_QK = GDN_HEADS * GDN_DK
GDN_V = GDN_HEADS * GDN_DV
GLA_QK = GLA_HEADS * GLA_DK
GLA_V = GLA_HEADS * GLA_DV
D_FF = 5632
D_IN = 7200
NORM_EPS = 1e-6
C_Z, C_GR, C_GQ, C_GK, C_GV, C_QKV, C_SM = 0, 1024, 2048, 2560, 3072, 4096, 7168
SM_W = 128
D_PROJ = 7680
R_Z, R_A, R_B, R_GQ, R_GK, R_GV, R_GR, R_LR = 3072, 4096, 4104, 4112, 4624, 5136, 6160, 7184

ADAM_LR, ADAM_B1, ADAM_B2, ADAM_EPS, ADAM_WD, ADAM_STEP = 0.001, 0.9, 0.999, 1e-08, 0.01, 10

N_DEV = 8
VMEM_LIMIT = 56 * 1024 * 1024

NN = (((1,), (0,)), ((), ()))
NT = (((1,), (1,)), ((), ()))
TN = (((0,), (0,)), ((), ()))


def _dot(a, b, dims=NN):
    return lax.dot_general(a.astype(_MXU_DTYPE), b.astype(_MXU_DTYPE), dims, preferred_element_type=F32)


def _dotx(a, b, dims=NN):
    return lax.dot_general(a, b, dims, precision=lax.Precision.HIGHEST, preferred_element_type=F32)


def _dot3(a, b):
    ah = a.astype(BF16)
    al = (a - ah.astype(F32)).astype(BF16)
    bh = b.astype(BF16)
    bl = (b - bh.astype(F32)).astype(BF16)
    d = functools.partial(lax.dot_general, dimension_numbers=NN, preferred_element_type=F32)
    return d(ah, bh) + (d(ah, bl) + d(al, bh))


def _tile(n, target, mult=8):
    best = None
    for t in range(mult, min(n, target) + 1, mult):
        if n % t == 0:
            best = t
    return best if best is not None else n


def _params(*sem):
    return pltpu.CompilerParams(dimension_semantics=sem, vmem_limit_bytes=VMEM_LIMIT)


def _sigmoid(x):
    return 0.5 * jnp.tanh(0.5 * x) + 0.5


def _softplus(x):
    return jnp.maximum(x, 0.0) + jnp.log1p(jnp.exp(-jnp.abs(x)))


def _silu_and_grad(c):
    s = _sigmoid(c)
    return c * s, s * (1.0 + c * (1.0 - s))


_ANY = pl.BlockSpec(memory_space=pl.ANY)


def _matmul(a, b, *, mode, name, out_dtype=F32, add=None, after=None, tm=1376, tn=512):
    if mode == "tn":
        K, M = a.shape
        N = b.shape[1]
    else:
        M, K = a.shape
        N = b.shape[0] if mode == "nt" else b.shape[1]
    tm = _tile(M, tm, 128 if mode == "tn" else 16)
    tn = _tile(N, tn, 128)
    dims = {"nn": NN, "nt": NT, "tn": TN}[mode]
    n_after = 0 if after is None else 1

    def body(*refs):
        refs = refs[n_after:]
        r = _dot(refs[0][...], refs[1][...], dims)
        if add is not None:
            r = r + refs[2][...]
        refs[-1][...] = r.astype(out_dtype)

    a_spec = pl.BlockSpec((K, tm), lambda i, j: (0, i)) if mode == "tn" else pl.BlockSpec((tm, K), lambda i, j: (i, 0))
    b_spec = pl.BlockSpec((tn, K), lambda i, j: (j, 0)) if mode == "nt" else pl.BlockSpec((K, tn), lambda i, j: (0, j))
    o_spec = pl.BlockSpec((tm, tn), lambda i, j: (i, j))
    in_specs = [_ANY] * n_after + [a_spec, b_spec] + ([o_spec] if add is not None else [])
    args = ((after,) if n_after else ()) + (a, b) + ((add,) if add is not None else ())
    return pl.pallas_call(
        body, name=name, grid=(M // tm, N // tn), in_specs=in_specs, out_specs=o_spec,
        out_shape=jax.ShapeDtypeStruct((M, N), out_dtype), compiler_params=_params("parallel", "parallel"),
    )(*args)


def _matmul_pair(a1, b1, a2, b2, *, name, after=None, tm=688, tn=256):
    M, K = a1.shape
    N = b1.shape[1]
    tm, tn = _tile(M, tm, 16), _tile(N, tn, 128)
    n_after = 0 if after is None else 1

    def body(*refs):
        a1_ref, b1_ref, a2_ref, b2_ref, o_ref = refs[n_after:]
        o_ref[...] = _dot(a1_ref[...], b1_ref[...]) + _dot(a2_ref[...], b2_ref[...])

    a_spec = pl.BlockSpec((tm, K), lambda i, j: (i, 0))
    b_spec = pl.BlockSpec((K, tn), lambda i, j: (0, j))
    return pl.pallas_call(
        body, name=name, grid=(M // tm, N // tn), in_specs=[_ANY] * n_after + [a_spec, b_spec, a_spec, b_spec],
        out_specs=pl.BlockSpec((tm, tn), lambda i, j: (i, j)), out_shape=jax.ShapeDtypeStruct((M, N), F32),
        compiler_params=_params("parallel", "parallel"),
    )(*((after,) if n_after else ()), a1, b1, a2, b2)


def _rmsnorm_fwd(h, w, *, name):
    M, D = h.shape
    tm = _tile(M, 688, 16)

    def body(h_ref, w_ref, n_ref):
        x = h_ref[...]
        r = lax.rsqrt(jnp.mean(x * x, axis=-1, keepdims=True) + NORM_EPS)
        n_ref[...] = (x * r * w_ref[...]).astype(n_ref.dtype)

    return pl.pallas_call(
        body, name=name, grid=(M // tm,),
        in_specs=[pl.BlockSpec((tm, D), lambda i: (i, 0)), pl.BlockSpec((1, D), lambda i: (0, 0))],
        out_specs=pl.BlockSpec((tm, D), lambda i: (i, 0)),
        out_shape=jax.ShapeDtypeStruct((M, D), BF16),
        compiler_params=_params("parallel"),
    )(h, w)


SEQ_BLOCK = HEAD_ROWS


def _seq_blocks_per_tile(rows):
    n = rows // SEQ_BLOCK
    return max(m for m in (1, 2, 3, 4) if n % m == 0)


def _seq_specs(m, D):
    return [pl.BlockSpec((SEQ_BLOCK, D), functools.partial(lambda i, k: (jnp.maximum(m * i + k - 1, 0), 0), k=k))
            for k in range(m)]


def _embed_norm(head, x, w, *, name, after=None):
    S, D = x.shape
    m = _seq_blocks_per_tile(S + HEAD_ROWS)
    n_after = 0 if after is None else 1

    def body(*refs):
        refs = refs[n_after:]
        head_ref, x_refs, w_ref, h_ref, n_ref = refs[0], refs[1:1 + m], refs[1 + m], refs[2 + m], refs[3 + m]
        i = pl.program_id(0)
        for k in range(m):
            blk = x_refs[k][...]
            if k == 0:
                blk = jnp.where(i == 0, head_ref[...], blk)
            rows = slice(k * SEQ_BLOCK, (k + 1) * SEQ_BLOCK)
            h_ref[rows, :] = blk
            r = lax.rsqrt(jnp.mean(blk * blk, axis=-1, keepdims=True) + NORM_EPS)
            n_ref[rows, :] = (blk * r * w_ref[...]).astype(n_ref.dtype)

    tile = pl.BlockSpec((m * SEQ_BLOCK, D), lambda i: (i, 0))
    return pl.pallas_call(
        body, name=name, grid=((S + HEAD_ROWS) // (m * SEQ_BLOCK),),
        in_specs=[_ANY] * n_after + [pl.BlockSpec((SEQ_BLOCK, D), lambda i: (0, 0))] + _seq_specs(m, D)
        + [pl.BlockSpec((1, D), lambda i: (0, 0))],
        out_specs=[tile, tile],
        out_shape=[jax.ShapeDtypeStruct((S + HEAD_ROWS, D), F32), jax.ShapeDtypeStruct((S + HEAD_ROWS, D), BF16)],
        compiler_params=_params("parallel"),
    )(*((after,) if n_after else ()), head, *([x] * m), w)


def _embed_norm_bwd(h, w, dn, dres, *, name):
    M, D = h.shape
    S = M - HEAD_ROWS
    m = _seq_blocks_per_tile(S)
    g = S // (m * SEQ_BLOCK)

    def one(x, dn_, dres_, w_):
        r = lax.rsqrt(jnp.mean(x * x, axis=-1, keepdims=True) + NORM_EPS)
        xhat = x * r
        dxhat = dn_ * w_
        dh = dres_ + r * (dxhat - xhat * jnp.mean(dxhat * xhat, axis=-1, keepdims=True))
        return dh, jnp.sum((dn_ * xhat).reshape(SEQ_BLOCK // 8, 8, D), axis=0)

    def body(*refs):
        w_ref = refs[0]
        groups = [refs[1 + a * (m + 1):1 + (a + 1) * (m + 1)] for a in range(3)]
        gx_ref, dhead_ref, dw_ref, acc_ref = refs[1 + 3 * (m + 1):]
        i = pl.program_id(0)
        w_ = w_ref[...]
        part = jnp.zeros((8, D), F32)
        for k in range(m):
            dh, p = one(*(grp[1 + k][...] for grp in groups), w_)
            gx_ref[k * SEQ_BLOCK:(k + 1) * SEQ_BLOCK, :] = dh
            part = part + p

        @pl.when(i == 0)
        def _():
            dh, p = one(*(grp[0][...] for grp in groups), w_)
            dhead_ref[...] = dh
            acc_ref[...] = part + p

        @pl.when(i > 0)
        def _():
            acc_ref[...] += part

        @pl.when(i == g - 1)
        def _():
            dw_ref[...] = jnp.sum(acc_ref[...], axis=0, keepdims=True)

    first = pl.BlockSpec((SEQ_BLOCK, D), lambda i: (0, 0))
    blocks = [pl.BlockSpec((SEQ_BLOCK, D), functools.partial(lambda i, k: (m * i + k + 1, 0), k=k)) for k in range(m)]
    vec = pl.BlockSpec((1, D), lambda i: (0, 0))
    return pl.pallas_call(
        body, name=name, grid=(g,), in_specs=[vec] + ([first] + blocks) * 3,
        out_specs=[pl.BlockSpec((m * SEQ_BLOCK, D), lambda i: (i, 0)), first, vec],
        out_shape=[jax.ShapeDtypeStruct((S, D), F32), jax.ShapeDtypeStruct((SEQ_BLOCK, D), F32),
                   jax.ShapeDtypeStruct((1, D), F32)],
        scratch_shapes=[pltpu.VMEM((8, D), F32)],
        compiler_params=_params("arbitrary"),
    )(w, *([h] * (m + 1)), *([dn] * (m + 1)), *([dres] * (m + 1)))


def _rmsnorm_bwd(h, w, dn, dres, *, name):
    M, D = h.shape
    tm = _tile(M, 344, 16)
    g = M // tm

    def body(h_ref, w_ref, dn_ref, dres_ref, dh_ref, dhb_ref, dw_ref, acc_ref):
        i = pl.program_id(0)
        x = h_ref[...]
        r = lax.rsqrt(jnp.mean(x * x, axis=-1, keepdims=True) + NORM_EPS)
        xhat = x * r
        dn_ = dn_ref[...]
        dxhat = dn_ * w_ref[...]
        dh = dres_ref[...] + r * (dxhat - xhat * jnp.mean(dxhat * xhat, axis=-1, keepdims=True))
        dh_ref[...] = dh
        dhb_ref[...] = dh.astype(dhb_ref.dtype)
        part = jnp.sum((dn_ * xhat).reshape(tm // 8, 8, D), axis=0)

        @pl.when(i == 0)
        def _():
            acc_ref[...] = part

        @pl.when(i > 0)
        def _():
            acc_ref[...] += part

        @pl.when(i == g - 1)
        def _():
            dw_ref[...] = jnp.sum(acc_ref[...], axis=0, keepdims=True)

    row = pl.BlockSpec((tm, D), lambda i: (i, 0))
    vec = pl.BlockSpec((1, D), lambda i: (0, 0))
    return pl.pallas_call(
        body, name=name, grid=(g,), in_specs=[row, vec, row, row],
        out_specs=[row, row, vec],
        out_shape=[jax.ShapeDtypeStruct((M, D), F32), jax.ShapeDtypeStruct((M, D), BF16),
                   jax.ShapeDtypeStruct((1, D), F32)],
        scratch_shapes=[pltpu.VMEM((8, D), F32)],
        compiler_params=_params("arbitrary"),
    )(h, w, dn, dres)


def _loss_head(h, w, target, *, name):
    M, D = h.shape
    m = _seq_blocks_per_tile(M)
    tm = m * SEQ_BLOCK
    g = M // tm

    def body(h_ref, w_ref, *rest):
        t_refs = rest[:m]
        dh_ref, dhb_ref, dw_ref, loss_ref, acc_ref, lacc_ref = rest[m:]
        i = pl.program_id(0)
        x = h_ref[...]
        row = i * tm + lax.broadcasted_iota(jnp.int32, (tm, 1), 0)
        live = row >= HEAD_ROWS
        r = lax.rsqrt(jnp.mean(x * x, axis=-1, keepdims=True) + NORM_EPS)
        xhat = x * r
        t = jnp.concatenate([t_ref[...] for t_ref in t_refs], axis=0)
        err = jnp.where(live, xhat * w_ref[...] - t, 0.0)
        dy = err * (1.0 / D)
        dxhat = dy * w_ref[...]
        dh = r * (dxhat - xhat * jnp.mean(dxhat * xhat, axis=-1, keepdims=True))
        dh_ref[...] = dh
        dhb_ref[...] = dh.astype(dhb_ref.dtype)
        part = jnp.sum((dy * xhat).reshape(tm // 8, 8, D), axis=0)
        lpart = jnp.sum((err * err).reshape(tm // 8, 8, D), axis=0)

        @pl.when(i == 0)
        def _():
            acc_ref[...] = part
            lacc_ref[...] = lpart

        @pl.when(i > 0)
        def _():
            acc_ref[...] += part
            lacc_ref[...] += lpart

        @pl.when(i == g - 1)
        def _():
            dw_ref[...] = jnp.sum(acc_ref[...], axis=0, keepdims=True)
            tot = jnp.sum(jnp.sum(lacc_ref[...], axis=0, keepdims=True), axis=1, keepdims=True)
            loss_ref[...] = jnp.broadcast_to(tot * (0.5 / D), (1, 128))

    row = pl.BlockSpec((tm, D), lambda i: (i, 0))
    vec = pl.BlockSpec((1, D), lambda i: (0, 0))
    return pl.pallas_call(
        body, name=name, grid=(g,), in_specs=[row, vec] + _seq_specs(m, D),
        out_specs=[row, row, vec, pl.BlockSpec((1, 128), lambda i: (0, 0))],
        out_shape=[jax.ShapeDtypeStruct((M, D), F32), jax.ShapeDtypeStruct((M, D), BF16),
                   jax.ShapeDtypeStruct((1, D), F32), jax.ShapeDtypeStruct((1, 128), F32)],
        scratch_shapes=[pltpu.VMEM((8, D), F32), pltpu.VMEM((8, D), F32)],
        compiler_params=_params("arbitrary"),
    )(h, w, *([target] * m))


def _gate_terms(sm, w2p, b2, alog_p, dt_p, row0):
    tm = sm.shape[0]
    lane = lax.broadcasted_iota(jnp.int32, (tm, SM_W), 1)
    live = (row0 + lax.broadcasted_iota(jnp.int32, (tm, 1), 0)) >= ROW_PAD
    pre = sm + dt_p
    neg_a = -jnp.exp(alog_p)
    g = neg_a * _softplus(pre)
    beta = _sigmoid(sm)
    z = _dot(sm, w2p) + b2
    return lane, live, pre, neg_a, g, beta, z


def _gates_fwd(proj, w2p, b2, alog_p, dt_p, *, name):
    M = proj.shape[0]
    tm = _tile(M, 688, 8)

    def body(sm_ref, w2_ref, b2_ref, al_ref, dt_ref, gb_ref, la_ref):
        row0 = pl.program_id(0) * tm
        lane, live, _, _, g, beta, z = _gate_terms(sm_ref[...], w2_ref[...], b2_ref[...], al_ref[...], dt_ref[...], row0)
        gb = jnp.where(lane < GDN_HEADS, g, jnp.where(lane < 2 * GDN_HEADS, beta, 0.0))
        gb_ref[...] = jnp.where(live, gb, 0.0)
        la = (jnp.minimum(z, 0.0) - jnp.log1p(jnp.exp(-jnp.abs(z)))) * (1.0 / GLA_GATE_NORMALIZER)
        la_ref[...] = jnp.where(live, la, 0.0)

    full = lambda s: pl.BlockSpec(s, lambda i: (0, 0))
    return pl.pallas_call(
        body, name=name, grid=(M // tm,),
        in_specs=[pl.BlockSpec((tm, SM_W), lambda i: (i, C_SM // SM_W)), full((SM_W, GLA_QK)), full((1, GLA_QK)),
                  full((1, SM_W)), full((1, SM_W))],
        out_specs=[pl.BlockSpec((tm, SM_W), lambda i: (i, 0)), pl.BlockSpec((tm, GLA_QK), lambda i: (i, 0))],
        out_shape=[jax.ShapeDtypeStruct((M, SM_W), F32), jax.ShapeDtypeStruct((M, GLA_QK), F32)],
        compiler_params=_params("parallel"),
    )(proj, w2p, b2, alog_p, dt_p)


def _gates_bwd(proj, w2p, b2, alog_p, dt_p, dgb_heads, dla, d_proj, *, name):
    M = proj.shape[0]
    tm = _tile(M, 688, 8)
    g_ = M // tm

    tail_w = D_PROJ - C_SM

    def body(sm_ref, w2_ref, b2_ref, al_ref, dt_ref, dgb_ref, dla_ref, _,
             dsm_ref, dw2_ref, db2_ref, dal_ref, ddt_ref):
        i = pl.program_id(0)
        sm = sm_ref[...]
        lane, live, pre, neg_a, g, beta, z = _gate_terms(sm, w2_ref[...], b2_ref[...], al_ref[...], dt_ref[...], i * tm)
        dz = jnp.where(live, dla_ref[...] * (_sigmoid(-z) * (1.0 / GLA_GATE_NORMALIZER)), 0.0)
        dsm_lr = _dot(dz, w2_ref[...], NT)
        dgb = dgb_ref[0]
        for hh in range(1, GDN_HEADS):
            dgb = dgb + dgb_ref[hh]
        dgb = jnp.where(live, dgb, 0.0)
        da = dgb * neg_a * _sigmoid(pre)
        db = dgb * beta * (1.0 - beta)
        dsm = jnp.where(lane < GDN_HEADS, da, jnp.where(lane < 2 * GDN_HEADS, db, dsm_lr))
        dsm_ref[:, 0:SM_W] = dsm.astype(dsm_ref.dtype)
        dsm_ref[:, SM_W:tail_w] = jnp.zeros((tm, tail_w - SM_W), dsm_ref.dtype)
        is_a = lane < GDN_HEADS
        dal = jnp.sum(jnp.where(is_a, dgb * g, 0.0), axis=0, keepdims=True)
        ddt = jnp.sum(jnp.where(is_a, da, 0.0), axis=0, keepdims=True)
        dw2 = _dot(sm, dz, TN)
        db2 = jnp.sum(dz, axis=0, keepdims=True)

        @pl.when(i == 0)
        def _():
            dw2_ref[...] = dw2
            db2_ref[...] = db2
            dal_ref[...] = dal
            ddt_ref[...] = ddt

        @pl.when(i > 0)
        def _():
            dw2_ref[...] += dw2
            db2_ref[...] += db2
            dal_ref[...] += dal
            ddt_ref[...] += ddt

    full = lambda s: pl.BlockSpec(s, lambda i: (0, 0))
    return pl.pallas_call(
        body, name=name, grid=(g_,),
        in_specs=[pl.BlockSpec((tm, SM_W), lambda i: (i, C_SM // SM_W)), full((SM_W, GLA_QK)), full((1, GLA_QK)),
                  full((1, SM_W)), full((1, SM_W)),
                  pl.BlockSpec((GDN_HEADS, tm, SM_W), lambda i: (0, i, 0)),
                  pl.BlockSpec((tm, GLA_QK), lambda i: (i, 0)), _ANY],
        out_specs=[pl.BlockSpec((tm, tail_w), lambda i: (i, C_SM // tail_w)), full((SM_W, GLA_QK)), full((1, GLA_QK)),
                   full((1, SM_W)), full((1, SM_W))],
        out_shape=[jax.ShapeDtypeStruct(d_proj.shape, d_proj.dtype), jax.ShapeDtypeStruct((SM_W, GLA_QK), F32),
                   jax.ShapeDtypeStruct((1, GLA_QK), F32), jax.ShapeDtypeStruct((1, SM_W), F32),
                   jax.ShapeDtypeStruct((1, SM_W), F32)],
        input_output_aliases={7: 0},
        compiler_params=_params("arbitrary"),
    )(proj, w2p, b2, alog_p, dt_p, dgb_heads, dla, d_proj)


QKV_W = GDN_QK
N_QKV_GROUPS = 3
QKV_B0 = C_QKV // QKV_W
HALO = 8


def _conv_terms(x_ref, halo_ref, cw_ref, xs_ref, i, tm):
    xs_ref[HALO:HALO + tm, :] = x_ref[...]
    xs_ref[0:HALO, :] = jnp.where(i > 0, halo_ref[...], 0.0)
    cw = cw_ref[...]
    xs = xs_ref[...]
    taps = [(pltpu.roll(xs, CONV_K - 1 - t, 0) if t < CONV_K - 1 else xs)[HALO:HALO + tm, :] for t in range(CONV_K)]
    c = taps[0] * cw[0:1, :]
    for t in range(1, CONV_K):
        c = c + taps[t] * cw[t:t + 1, :]
    return c, taps


def _prep_fwd(proj, conv_w8, *, name):
    M = proj.shape[0]
    tm = _tile(M, 344, 8)

    def body(x_ref, halo_ref, cw_ref, o_ref, xs_ref):
        j, i = pl.program_id(0), pl.program_id(1)
        c, _ = _conv_terms(x_ref, halo_ref, cw_ref, xs_ref, i, tm)
        s, _ = _silu_and_grad(c)
        scale = jnp.where(j == 0, GDN_DK ** -0.5, 1.0)
        for hh in range(GDN_HEADS):
            cols = slice(hh * 128, (hh + 1) * 128)
            sh = s[:, cols]
            r = lax.rsqrt(jnp.sum(sh * sh, axis=-1, keepdims=True) + NORM_EPS)
            o_ref[:, cols] = jnp.where(j < 2, sh * (r * scale), sh)

    hb = tm // HALO
    return pl.pallas_call(
        body, name=name, grid=(N_QKV_GROUPS, M // tm),
        in_specs=[pl.BlockSpec((tm, QKV_W), lambda j, i: (i, QKV_B0 + j)),
                  pl.BlockSpec((HALO, QKV_W), lambda j, i: (jnp.maximum(i * hb - 1, 0), QKV_B0 + j)),
                  pl.BlockSpec((8, QKV_W), lambda j, i: (0, j))],
        out_specs=pl.BlockSpec((tm, QKV_W), lambda j, i: (i, j)),
        out_shape=jax.ShapeDtypeStruct((M, N_QKV_GROUPS * QKV_W), F32),
        scratch_shapes=[pltpu.VMEM((tm + HALO, QKV_W), F32)],
        compiler_params=_params("parallel", "arbitrary"),
    )(proj, proj, conv_w8)


def _prep_bwd(proj, conv_w8, dact, d_proj, *, name):
    M = proj.shape[0]
    tm = _tile(M, 688, 16)
    g_ = M // tm
    ext = tm + HALO

    def body(x_ref, prev_ref, next_ref, cw_ref, da_ref, dan_ref, _, o_ref, dcw_ref, xs_ref, das_ref, dcs_ref):
        j, i = pl.program_id(0), pl.program_id(1)
        not_last = i < g_ - 1
        xs_ref[0:HALO, :] = jnp.where(i > 0, prev_ref[...], 0.0)
        xs_ref[HALO:HALO + tm, :] = x_ref[...]
        xs_ref[HALO + tm:HALO + ext, :] = jnp.where(not_last, next_ref[...], 0.0)
        das_ref[0:tm, :] = da_ref[...]
        das_ref[tm:ext, :] = jnp.where(not_last, dan_ref[...], 0.0)
        cw = cw_ref[...]
        xs = xs_ref[...]
        taps = [(pltpu.roll(xs, CONV_K - 1 - t, 0) if t < CONV_K - 1 else xs)[HALO:HALO + ext, :] for t in range(CONV_K)]
        c = taps[0] * cw[0:1, :]
        for t in range(1, CONV_K):
            c = c + taps[t] * cw[t:t + 1, :]
        s, ds_dc = _silu_and_grad(c)
        scale = jnp.where(j == 0, GDN_DK ** -0.5, 1.0)
        for hh in range(GDN_HEADS):
            cols = slice(hh * 128, (hh + 1) * 128)
            sh = s[:, cols]
            r = lax.rsqrt(jnp.sum(sh * sh, axis=-1, keepdims=True) + NORM_EPS)
            da = das_ref[:, cols]
            y = sh * r
            dy = da * scale
            ds_norm = r * (dy - y * jnp.sum(dy * y, axis=-1, keepdims=True))
            dcs_ref[:, cols] = jnp.where(j < 2, ds_norm, da) * ds_dc[:, cols]
        dc = dcs_ref[...]
        acc = dc[0:tm, :] * cw[CONV_K - 1:CONV_K, :]
        for t in range(CONV_K - 1):
            acc = acc + pltpu.roll(dc, ext - (CONV_K - 1 - t), 0)[0:tm, :] * cw[t:t + 1, :]
        o_ref[...] = acc.astype(o_ref.dtype)
        r8 = lax.broadcasted_iota(jnp.int32, (8, QKV_W), 0)
        part = jnp.zeros((8, QKV_W), F32)
        for t in range(CONV_K):
            part = jnp.where(r8 == t, jnp.sum(dc[0:tm, :] * taps[t][0:tm, :], axis=0, keepdims=True), part)

        @pl.when(i == 0)
        def _():
            dcw_ref[...] = part

        @pl.when(i > 0)
        def _():
            dcw_ref[...] += part

    hb = tm // HALO
    last = M // HALO - 1
    prev_of = lambda i: jnp.maximum(i * hb - 1, 0)
    next_of = lambda i: jnp.minimum((i + 1) * hb, last)
    return pl.pallas_call(
        body, name=name, grid=(N_QKV_GROUPS, g_),
        in_specs=[pl.BlockSpec((tm, QKV_W), lambda j, i: (i, QKV_B0 + j)),
                  pl.BlockSpec((HALO, QKV_W), lambda j, i: (prev_of(i), QKV_B0 + j)),
                  pl.BlockSpec((HALO, QKV_W), lambda j, i: (next_of(i), QKV_B0 + j)),
                  pl.BlockSpec((8, QKV_W), lambda j, i: (0, j)),
                  pl.BlockSpec((tm, QKV_W), lambda j, i: (i, j)),
                  pl.BlockSpec((HALO, QKV_W), lambda j, i: (next_of(i), j)), _ANY],
        out_specs=[pl.BlockSpec((tm, QKV_W), lambda j, i: (i, QKV_B0 + j)), pl.BlockSpec((8, QKV_W), lambda j, i: (0, j))],
        out_shape=[jax.ShapeDtypeStruct(d_proj.shape, d_proj.dtype),
                   jax.ShapeDtypeStruct((8, N_QKV_GROUPS * QKV_W), F32)],
        input_output_aliases={6: 0},
        scratch_shapes=[pltpu.VMEM((HALO + ext, QKV_W), F32), pltpu.VMEM((ext, QKV_W), F32), pltpu.VMEM((ext, QKV_W), F32)],
        compiler_params=_params("parallel", "arbitrary"),
    )(proj, proj, proj, conv_w8, dact, dact, d_proj)


def _round_robin(gens):
    gens = list(gens)
    while gens:
        alive = []
        for gen in gens:
            try:
                next(gen)
                alive.append(gen)
            except StopIteration:
                pass
        gens = alive


def _unit_lower_inverse(a_low, eye):
    n = a_low.shape[0]
    ri = lax.broadcasted_iota(jnp.int32, (n, n), 0)
    ci = lax.broadcasted_iota(jnp.int32, (n, n), 1)
    same = lambda shift: (ri >> shift) == (ci >> shift)
    b = jnp.where(same(3), -a_low, 0.0)
    x = eye + b
    p2 = _dot3(b, b)
    yield
    x = x + _dot3(x, p2)
    p4 = _dot3(p2, p2)
    yield
    x = x + _dot3(x, p4)
    yield
    for shift in (3, 4, 5):
        between = jnp.where(same(shift + 1) & ~same(shift), a_low, 0.0)
        t = _dot3(between, x)
        yield
        x = x - _dot3(x, t)
        yield
    return x


class _GdnChunk:
    def build(self, q, k, v, gb, h):
        C = GDN_CHUNK
        lane = lax.broadcasted_iota(jnp.int32, (C, SM_W), 1)
        g = jnp.sum(jnp.where(lane == h, gb, 0.0), axis=1, keepdims=True)
        self.beta = jnp.sum(jnp.where(lane == h + GDN_HEADS, gb, 0.0), axis=1, keepdims=True)
        ri = lax.broadcasted_iota(jnp.int32, (C, C), 0)
        ci = lax.broadcasted_iota(jnp.int32, (C, C), 1)
        self.causal = ri >= ci
        self.strict = ri > ci
        self.eye = (ri == ci).astype(F32)
        gcb = _dotx(self.causal.astype(F32), jnp.broadcast_to(g, (C, SM_W)))
        yield
        self.gcol = gcb[:, 0:1]
        grow = gcb.T[0:1, 0:C]
        self.decay = jnp.exp(jnp.where(self.causal, self.gcol - grow, -1e30))
        self.egc = jnp.exp(self.gcol)
        glast = gcb[C - 1:C, 0:1]
        self.elast = jnp.exp(glast - self.gcol)
        self.gl = jnp.exp(glast)
        self.q, self.k, self.v = q, k, v
        self.kb = k * self.beta
        m = _dot(self.kb, k, NT)
        n_ = _dot(q, k, NT)
        yield
        self.a_low = jnp.where(self.strict, m * self.decay, 0.0)
        self.p = n_ * self.decay
        self.qd = q * self.egc
        self.kd = k * self.elast
        self.bu = v * self.beta
        self.bw = self.kb * self.egc


GDN_HB = 8
GDN_HG = GDN_HEADS // GDN_HB


def _gdn_specs(n_of):
    C, W = GDN_CHUNK, 128 * GDN_HB
    q_spec = pl.BlockSpec((C, W), lambda g, n: (n_of(n), g))
    k_spec = pl.BlockSpec((C, W), lambda g, n: (n_of(n), g + GDN_HG))
    v_spec = pl.BlockSpec((C, W), lambda g, n: (n_of(n), g + 2 * GDN_HG))
    gb_spec = pl.BlockSpec((C, SM_W), lambda g, n: (n_of(n), 0))
    o_spec = pl.BlockSpec((C, W), lambda g, n: (n_of(n), g))
    s_spec = pl.BlockSpec((GDN_HB, None, GDN_DK, GDN_DV), lambda g, n: (g, n_of(n), 0, 0))
    t_spec = pl.BlockSpec((GDN_HB, None, C, C), lambda g, n: (g, n_of(n), 0, 0))
    return q_spec, k_spec, v_spec, gb_spec, o_spec, s_spec, t_spec


def _gdn_fwd(act, gb, *, name):
    M = act.shape[0]
    N = M // GDN_CHUNK

    def body(q_ref, k_ref, v_ref, gb_ref, o_ref, s_ref, t_ref, state):
        g, n = pl.program_id(0), pl.program_id(1)

        @pl.when(n == 0)
        def _():
            state[...] = jnp.zeros_like(state)

        gb_ = gb_ref[...]

        def head(hh):
            cols = slice(hh * 128, (hh + 1) * 128)
            c = _GdnChunk()
            yield from c.build(q_ref[:, cols], k_ref[:, cols], v_ref[:, cols], gb_, g * GDN_HB + hh)
            tinv = yield from _unit_lower_inverse(c.a_low, c.eye)
            s = state[hh]
            s_ref[hh] = s
            t_ref[hh] = tinv
            u = _dot(tinv, c.bu)
            w = _dot(tinv, c.bw)
            yield
            vn = u - _dot(w, s)
            o1 = _dot(c.qd, s)
            yield
            o_ref[:, cols] = o1 + _dot(c.p, vn)
            state[hh] = c.gl * s + _dot(c.kd, vn, TN)

        _round_robin(head(hh) for hh in range(GDN_HB))

    q_spec, k_spec, v_spec, gb_spec, o_spec, s_spec, t_spec = _gdn_specs(lambda n: n)
    return pl.pallas_call(
        body, name=name, grid=(GDN_HG, N),
        in_specs=[q_spec, k_spec, v_spec, gb_spec], out_specs=[o_spec, s_spec, t_spec],
        out_shape=[jax.ShapeDtypeStruct((M, GDN_V), F32),
                   jax.ShapeDtypeStruct((GDN_HEADS, N, GDN_DK, GDN_DV), F32),
                   jax.ShapeDtypeStruct((GDN_HEADS, N, GDN_CHUNK, GDN_CHUNK), F32)],
        scratch_shapes=[pltpu.VMEM((GDN_HB, GDN_DK, GDN_DV), F32)],
        compiler_params=_params("parallel", "arbitrary"),
    )(act, act, act, gb)


def _gdn_bwd(act, gb, do, s_all, t_all, *, name):
    M = act.shape[0]
    N = M // GDN_CHUNK
    C = GDN_CHUNK
    assert GDN_HG == 1

    def body(q_ref, k_ref, v_ref, gb_ref, do_ref, s_ref, t_ref, dact_ref, dgb_ref, dstate):
        g, n = pl.program_id(0), pl.program_id(1)

        @pl.when(n == 0)
        def _():
            dstate[...] = jnp.zeros_like(dstate)

        gb_ = gb_ref[...]
        last = lax.broadcasted_iota(jnp.int32, (C, 1), 0) == C - 1
        upper = (lax.broadcasted_iota(jnp.int32, (C, C), 0) <= lax.broadcasted_iota(jnp.int32, (C, C), 1)).astype(F32)
        lane = lax.broadcasted_iota(jnp.int32, (C, SM_W), 1)
        def head(hh):
            cols = slice(hh * 128, (hh + 1) * 128)
            h = g * GDN_HB + hh
            c = _GdnChunk()
            yield from c.build(q_ref[:, cols], k_ref[:, cols], v_ref[:, cols], gb_, h)
            tinv = t_ref[hh]
            tinv_t = tinv.T
            s = s_ref[hh]
            do_ = do_ref[:, cols]
            ds1 = dstate[hh]
            u = _dot(tinv, c.bu)
            w = _dot(tinv, c.bw)
            dqd = _dot(do_, s, NT)
            dvn0 = _dot(c.p, do_, TN) + _dot(c.kd, ds1)
            dst0 = _dot(c.qd, do_, TN) + c.gl * ds1
            yield
            vn = u - _dot(w, s)
            dvn = dvn0
            yield
            dp = jnp.where(c.causal, _dot(do_, vn, NT), 0.0)
            dstate[hh] = dst0 - _dot(w, dvn, TN)
            dkd = _dot(vn, ds1, NT)
            dw = -_dot(dvn, s, NT)
            dbu = _dot(tinv_t, dvn)
            dgl = jnp.sum(jnp.sum(s * ds1, axis=1, keepdims=True), axis=0, keepdims=True)
            yield
            dbw = _dot(tinv_t, dw)
            t1 = _dot(dbu, u, NT)
            yield
            da = jnp.where(c.strict, -(t1 + _dot(dbw, w, NT)), 0.0)
            dn_ = dp * c.decay
            dq0 = _dot(dn_, c.k)
            dk0 = _dot(dn_, c.q, TN)
            yield
            dm = da * c.decay
            e = da * c.a_low + dp * c.p
            dkb = _dot(dm, c.k) + dbw * c.egc
            dact_ref[:, GDN_QK + hh * 128:GDN_QK + (hh + 1) * 128] = (
                _dot(dm, c.kb, TN) + dk0 + dkb * c.beta + dkd * c.elast)
            dact_ref[:, cols] = dq0 + dqd * c.egc
            dact_ref[:, 2 * GDN_QK + hh * 128:2 * GDN_QK + (hh + 1) * 128] = dbu * c.beta
            dbeta = jnp.sum(dbu * c.v, axis=1, keepdims=True) + jnp.sum(dkb * c.k, axis=1, keepdims=True)
            t_kd = jnp.sum(dkd * c.kd, axis=1, keepdims=True)
            dgc = (jnp.sum(e, axis=1, keepdims=True) - jnp.sum(e.T, axis=1, keepdims=True)
                   + jnp.sum(dbw * c.bw, axis=1, keepdims=True) + jnp.sum(dqd * c.qd, axis=1, keepdims=True) - t_kd)
            dgc = dgc + jnp.where(last, jnp.sum(t_kd, axis=0, keepdims=True) + dgl * c.gl, 0.0)
            yield
            dg = _dotx(upper, jnp.broadcast_to(dgc, (C, SM_W)))
            dgb_ref[hh] = jnp.where(lane == h, dg, jnp.where(lane == h + GDN_HEADS, dbeta, 0.0))

        _round_robin(head(hh) for hh in range(GDN_HB))

    rev = lambda n: N - 1 - n
    q_spec, k_spec, v_spec, gb_spec, o_spec, s_spec, t_spec = _gdn_specs(rev)
    dgb_spec = pl.BlockSpec((GDN_HB, C, SM_W), lambda g, n: (g, rev(n), 0))
    return pl.pallas_call(
        body, name=name, grid=(GDN_HG, N),
        in_specs=[q_spec, k_spec, v_spec, gb_spec, o_spec, s_spec, t_spec],
        out_specs=[pl.BlockSpec((C, 2 * GDN_QK + GDN_V), lambda g, n: (rev(n), 0)), dgb_spec],
        out_shape=[jax.ShapeDtypeStruct((M, 2 * GDN_QK + GDN_V), F32),
                   jax.ShapeDtypeStruct((GDN_HEADS, M, SM_W), F32)],
        scratch_shapes=[pltpu.VMEM((GDN_HB, GDN_DK, GDN_DV), F32)],
        compiler_params=_params("parallel", "arbitrary"),
    )(act, act, act, gb, do, s_all, t_all)


GLA_STEP_ROWS = 64
GLA_SUB = GLA_STEP_ROWS // GLA_CHUNK


def _gla_cumsum(la):
    C = GLA_CHUNK
    ltri = (lax.broadcasted_iota(jnp.int32, (C, C), 0) >= lax.broadcasted_iota(jnp.int32, (C, C), 1)).astype(F32)
    return _dotx(ltri, la)


GLA_HALF = GLA_CHUNK // 2


def _gla_cross_factors(b):
    top = lax.broadcasted_iota(jnp.int32, b.shape, 0) < GLA_HALF
    bm = b[GLA_HALF - 1:GLA_HALF, :]
    late = jnp.where(top, 0.0, jnp.exp(jnp.minimum(b - bm, 0.0)))
    early = jnp.where(top, jnp.exp(jnp.minimum(bm - b, 0.0)), 0.0)
    return late, early


def _gla_half_decay(bh, ii):
    rj = lax.broadcasted_iota(jnp.int32, bh.shape, 0)
    return jnp.where(rj <= ii, jnp.exp(jnp.minimum(bh[ii:ii + 1, :] - bh, 0.0)), 0.0)


def _gla_scores_t(q, k, b):
    C, H = GLA_CHUNK, GLA_HALF
    lane = lax.broadcasted_iota(jnp.int32, (H, C), 1)
    halves = []
    for h0 in (0, H):
        qh, kh, bh = q[h0:h0 + H], k[h0:h0 + H], b[h0:h0 + H]
        sth = jnp.zeros((H, C), F32)
        for ii in range(H):
            si = jnp.sum(qh[ii:ii + 1, :] * kh * _gla_half_decay(bh, ii), axis=1, keepdims=True)
            sth = jnp.where(lane == h0 + ii, si, sth)
            if ii % 4 == 3:
                yield
        halves.append(sth)
    late, early = _gla_cross_factors(b)
    between = _dot(k * early, q * late, NT)
    yield
    return jnp.concatenate(halves, axis=0) + between


def _gla_specs(n_of):
    R = GLA_STEP_ROWS
    q_spec = pl.BlockSpec((R, GLA_QK), lambda n: (n_of(n), C_GQ // GLA_QK))
    k_spec = pl.BlockSpec((R, GLA_QK), lambda n: (n_of(n), C_GK // GLA_QK))
    v_spec = pl.BlockSpec((R, GLA_V), lambda n: (n_of(n), C_GV // GLA_V))
    la_spec = pl.BlockSpec((R, GLA_QK), lambda n: (n_of(n), 0))
    o_spec = pl.BlockSpec((R, GLA_V), lambda n: (n_of(n), 0))
    s_spec = pl.BlockSpec((GLA_HEADS, None, GLA_SUB, GLA_DV, GLA_DK), lambda n: (0, n_of(n), 0, 0, 0))
    return q_spec, k_spec, v_spec, la_spec, o_spec, s_spec


def _gla_fwd(proj, la, *, name):
    M = proj.shape[0]
    N = M // GLA_STEP_ROWS
    C = GLA_CHUNK

    def body(q_ref, k_ref, v_ref, la_ref, o_ref, s_ref, state):
        n = pl.program_id(0)

        @pl.when(n == 0)
        def _():
            state[...] = jnp.zeros_like(state)

        local = {}

        def within(hh, c):
            kc = slice(hh * GLA_DK, (hh + 1) * GLA_DK)
            vc = slice(hh * GLA_DV, (hh + 1) * GLA_DV)
            rows = slice(c * C, (c + 1) * C)
            q = q_ref[rows, kc] * (GLA_DK ** -0.5)
            k = k_ref[rows, kc]
            v = v_ref[rows, vc]
            b = _gla_cumsum(la_ref[rows, kc])
            yield
            blast = b[C - 1:C, :]
            sc_t = yield from _gla_scores_t(q, k, b)
            kv = _dot(v, k * jnp.exp(blast - b), TN)
            o2 = _dot(sc_t, v, TN)
            yield
            local[hh, c] = (q * jnp.exp(b), jnp.exp(blast), kv, o2)

        def across(hh):
            vc = slice(hh * GLA_DV, (hh + 1) * GLA_DV)
            st = state[hh]
            for c in range(GLA_SUB):
                qe, eblast, kv, o2 = local[hh, c]
                s_ref[hh, c] = st
                o1 = _dot(qe, st, NT)
                yield
                o_ref[c * C:(c + 1) * C, vc] = o1 + o2
                st = st * eblast + kv
            state[hh] = st

        _round_robin(within(hh, c) for c in range(GLA_SUB) for hh in range(GLA_HEADS))
        _round_robin(across(hh) for hh in range(GLA_HEADS))

    q_spec, k_spec, v_spec, la_spec, o_spec, s_spec = _gla_specs(lambda n: n)
    return pl.pallas_call(
        body, name=name, grid=(N,),
        in_specs=[q_spec, k_spec, v_spec, la_spec], out_specs=[o_spec, s_spec],
        out_shape=[jax.ShapeDtypeStruct((M, GLA_V), F32),
                   jax.ShapeDtypeStruct((GLA_HEADS, N, GLA_SUB, GLA_DV, GLA_DK), F32)],
        scratch_shapes=[pltpu.VMEM((GLA_HEADS, GLA_DV, GLA_DK), F32)],
        compiler_params=_params("arbitrary"),
    )(proj, proj, proj, la)


def _gla_bwd(proj, la, do, s_all, d_proj, *, name):
    M = proj.shape[0]
    N = M // GLA_STEP_ROWS
    C = GLA_CHUNK
    qkv_w = 2 * GLA_QK + GLA_V
    assert C_GK == C_GQ + GLA_QK and C_GV == C_GK + GLA_QK and C_GQ % qkv_w == 0

    def body(q_ref, k_ref, v_ref, la_ref, do_ref, s_ref, _, dp_ref, dla_ref, dstate):
        n = pl.program_id(0)

        @pl.when(n == 0)
        def _():
            dstate[...] = jnp.zeros_like(dstate)

        H = GLA_HALF
        lane = lax.broadcasted_iota(jnp.int32, (C, C), 1)
        row = lax.broadcasted_iota(jnp.int32, (C, C), 0)
        ri = lax.broadcasted_iota(jnp.int32, (C, GLA_DK), 0)
        lane_h = lax.broadcasted_iota(jnp.int32, (H, C), 1)
        ri_h = lax.broadcasted_iota(jnp.int32, (H, GLA_DK), 0)
        cross = (row < H) & (lane >= H)
        upper = (row <= lane).astype(F32)
        def head(hh):
            kc = slice(hh * GLA_DK, (hh + 1) * GLA_DK)
            vc = slice(hh * GLA_DV, (hh + 1) * GLA_DV)
            ds1 = dstate[hh]
            for c in reversed(range(GLA_SUB)):
                rows = slice(c * C, (c + 1) * C)
                q = q_ref[rows, kc] * (GLA_DK ** -0.5)
                k = k_ref[rows, kc]
                v = v_ref[rows, vc]
                b = _gla_cumsum(la_ref[rows, kc])
                do_ = do_ref[rows, vc]
                st = s_ref[hh, c]
                dsc_t = _dot(v, do_, NT)
                dqe = _dot(do_, st)
                dke = _dot(v, ds1)
                yield
                blast = b[C - 1:C, :]
                eb = jnp.exp(b)
                elast = jnp.exp(blast - b)
                eblast = jnp.exp(blast)
                qe = q * eb
                ke = k * elast
                dv2 = _dot(ke, ds1, NT)
                ds_new = _dot(do_, qe, TN)
                deblast = jnp.sum(st * ds1, axis=0, keepdims=True)
                sc_halves, dq_halves, dk_halves = [], [], []
                for h0 in (0, H):
                    qh, kh, bh, dsch = q[h0:h0 + H], k[h0:h0 + H], b[h0:h0 + H], dsc_t[h0:h0 + H]
                    sch = jnp.zeros((H, C), F32)
                    dqh = jnp.zeros((H, GLA_DK), F32)
                    dkh = jnp.zeros((H, GLA_DK), F32)
                    for ii in range(H):
                        f = _gla_half_decay(bh, ii)
                        kf = kh * f
                        si = jnp.sum(qh[ii:ii + 1, :] * kf, axis=1, keepdims=True)
                        sch = jnp.where(lane_h == h0 + ii, si, sch)
                        dsi = jnp.sum(jnp.where(lane_h == h0 + ii, dsch, 0.0), axis=1, keepdims=True)
                        dqh = jnp.where(ri_h == ii, jnp.sum(dsi * kf, axis=0, keepdims=True), dqh)
                        dkh = dkh + (dsi * f) * qh[ii:ii + 1, :]
                        if ii % 4 == 3:
                            yield
                    sc_halves.append(sch)
                    dq_halves.append(dqh)
                    dk_halves.append(dkh)
                late, early = _gla_cross_factors(b)
                q_late, k_early = q * late, k * early
                dsc_x = jnp.where(cross, dsc_t, 0.0)
                sc_t = jnp.concatenate(sc_halves, axis=0) + _dot(k_early, q_late, NT)
                dq_sc = jnp.concatenate(dq_halves, axis=0) + _dot(dsc_x, k_early, TN) * late
                dk_sc = jnp.concatenate(dk_halves, axis=0) + _dot(dsc_x, q_late) * early
                yield
                dv1 = _dot(sc_t, do_)
                dp_ref[rows, kc] = ((dq_sc + dqe * eb) * (GLA_DK ** -0.5)).astype(dp_ref.dtype)
                dp_ref[rows, GLA_QK + hh * GLA_DK:GLA_QK + (hh + 1) * GLA_DK] = (dk_sc + dke * elast).astype(dp_ref.dtype)
                t_ke = dke * ke
                db = q * dq_sc - k * dk_sc + dqe * qe - t_ke
                db = db + jnp.where(ri == C - 1, jnp.sum(t_ke, axis=0, keepdims=True) + deblast * eblast, 0.0)
                dla = _dotx(upper, db)
                yield
                dp_ref[rows, 2 * GLA_QK + hh * GLA_DV:2 * GLA_QK + (hh + 1) * GLA_DV] = (dv1 + dv2).astype(dp_ref.dtype)
                dla_ref[rows, kc] = dla
                ds1 = ds1 * eblast + ds_new
            dstate[hh] = ds1

        _round_robin(head(hh) for hh in range(GLA_HEADS))

    rev = lambda n: N - 1 - n
    q_spec, k_spec, v_spec, la_spec, o_spec, s_spec = _gla_specs(rev)
    return pl.pallas_call(
        body, name=name, grid=(N,),
        in_specs=[q_spec, k_spec, v_spec, la_spec, o_spec, s_spec, _ANY],
        out_specs=[pl.BlockSpec((GLA_STEP_ROWS, qkv_w), lambda n: (rev(n), C_GQ // qkv_w)), la_spec],
        out_shape=[jax.ShapeDtypeStruct(d_proj.shape, d_proj.dtype), jax.ShapeDtypeStruct((M, GLA_QK), F32)],
        input_output_aliases={6: 0},
        scratch_shapes=[pltpu.VMEM((GLA_HEADS, GLA_DV, GLA_DK), F32)],
        compiler_params=_params("arbitrary"),
    )(proj, proj, proj, la, do, s_all, d_proj)


def _head_norm(o, wn):
    r = lax.rsqrt(jnp.mean(o * o, axis=-1, keepdims=True) + NORM_EPS)
    return o * r, r


def _mix_heads():
    heads = [(0, GDN_DV, hh * GDN_DV, hh * GDN_DV) for hh in range(GDN_HEADS)]
    heads += [(1, GLA_DV, GDN_V + hh * GLA_DV, hh * GLA_DV) for hh in range(GLA_HEADS)]
    return heads


def _mix_fwd(o_gdn, o_gla, proj, wn_gdn, wn_gla, *, name):
    M = proj.shape[0]
    tm = _tile(M, 344, 16)

    def body(og_ref, ol_ref, z_ref, r_ref, wg_ref, wl_ref, m_ref):
        srcs = ((og_ref, z_ref, wg_ref), (ol_ref, r_ref, wl_ref))
        for grp, width, mcol, col in _mix_heads():
            o_ref, gate_ref, w_ref = srcs[grp]
            xhat, _ = _head_norm(o_ref[:, col:col + width], None)
            gate, _ = _silu_and_grad(gate_ref[:, col:col + width])
            m_ref[:, mcol:mcol + width] = (xhat * w_ref[...] * gate).astype(m_ref.dtype)

    full = lambda s: pl.BlockSpec(s, lambda i: (0, 0))
    return pl.pallas_call(
        body, name=name, grid=(M // tm,),
        in_specs=[pl.BlockSpec((tm, GDN_V), lambda i: (i, 0)), pl.BlockSpec((tm, GLA_V), lambda i: (i, 0)),
                  pl.BlockSpec((tm, GDN_V), lambda i: (i, C_Z // GDN_V)),
                  pl.BlockSpec((tm, GLA_V), lambda i: (i, C_GR // GLA_V)),
                  full((1, GDN_DV)), full((1, GLA_DV))],
        out_specs=pl.BlockSpec((tm, D_MODEL), lambda i: (i, 0)),
        out_shape=jax.ShapeDtypeStruct((M, D_MODEL), BF16),
        compiler_params=_params("parallel"),
    )(o_gdn, o_gla, proj, proj, wn_gdn, wn_gla)


def _mix_bwd(o_gdn, o_gla, proj, wn_gdn, wn_gla, dmixed, *, name):
    M = proj.shape[0]
    tm = _tile(M, 344, 16)
    g_ = M // tm
    assert C_Z == 0 and C_GR == GDN_V

    def body(og_ref, ol_ref, z_ref, r_ref, wg_ref, wl_ref, dm_ref,
             dog_ref, dol_ref, dzr_ref, dwg_ref, dwl_ref):
        i = pl.program_id(0)
        srcs = ((og_ref, z_ref, wg_ref, dog_ref), (ol_ref, r_ref, wl_ref, dol_ref))
        dws = [jnp.zeros((1, GDN_DV), F32), jnp.zeros((1, GLA_DV), F32)]
        for grp, width, mcol, col in _mix_heads():
            o_ref, gate_ref, w_ref, do_ref = srcs[grp]
            cols = slice(col, col + width)
            xhat, r = _head_norm(o_ref[:, cols], None)
            gate, dgate_dc = _silu_and_grad(gate_ref[:, cols])
            dm = dm_ref[:, mcol:mcol + width]
            dzr_ref[:, mcol:mcol + width] = (dm * xhat * w_ref[...] * dgate_dc).astype(dzr_ref.dtype)
            dnorm = dm * gate
            dws[grp] = dws[grp] + jnp.sum(dnorm * xhat, axis=0, keepdims=True)
            dxhat = dnorm * w_ref[...]
            do_ref[:, cols] = r * (dxhat - xhat * jnp.mean(dxhat * xhat, axis=-1, keepdims=True))

        @pl.when(i == 0)
        def _():
            dwg_ref[...] = dws[0]
            dwl_ref[...] = dws[1]

        @pl.when(i > 0)
        def _():
            dwg_ref[...] += dws[0]
            dwl_ref[...] += dws[1]

    full = lambda s: pl.BlockSpec(s, lambda i: (0, 0))
    half = pl.BlockSpec((tm, GDN_V), lambda i: (i, 0))
    return pl.pallas_call(
        body, name=name, grid=(g_,),
        in_specs=[half, half, pl.BlockSpec((tm, GDN_V), lambda i: (i, C_Z // GDN_V)),
                  pl.BlockSpec((tm, GLA_V), lambda i: (i, C_GR // GLA_V)),
                  full((1, GDN_DV)), full((1, GLA_DV)), pl.BlockSpec((tm, D_MODEL), lambda i: (i, 0))],
        out_specs=[half, half, pl.BlockSpec((tm, GDN_V + GLA_V), lambda i: (i, 0)),
                   full((1, GDN_DV)), full((1, GLA_DV))],
        out_shape=[jax.ShapeDtypeStruct((M, GDN_V), F32), jax.ShapeDtypeStruct((M, GLA_V), F32),
                   jax.ShapeDtypeStruct((M, D_PROJ), BF16),
                   jax.ShapeDtypeStruct((1, GDN_DV), F32), jax.ShapeDtypeStruct((1, GLA_DV), F32)],
        compiler_params=_params("arbitrary"),
    )(o_gdn, o_gla, proj, proj, wn_gdn, wn_gla, dmixed)


def _row_chunks(tm, parts=2):
    if tm % (16 * parts):
        return [slice(0, tm)]
    return [slice(p * (tm // parts), (p + 1) * (tm // parts)) for p in range(parts)]


def _swiglu_fwd(n, w_gate_t, w_up_t, *, name, tm=1376, tn=512):
    M, D = n.shape
    F = w_gate_t.shape[0]
    tm, tn = _tile(M, tm, 16), _tile(F, tn, 128)

    def body(n_ref, wg_ref, wu_ref, g_ref, u_ref, a_ref):
        wg, wu = wg_ref[...], wu_ref[...]
        for rows in _row_chunks(tm):
            x = n_ref[rows, :]
            g = _dot(x, wg, NT)
            u = _dot(x, wu, NT)
            s, _ = _silu_and_grad(g)
            g_ref[rows, :] = g.astype(g_ref.dtype)
            u_ref[rows, :] = u.astype(u_ref.dtype)
            a_ref[rows, :] = (s * u).astype(a_ref.dtype)

    w_spec = pl.BlockSpec((tn, D), lambda i, j: (j, 0))
    o_spec = pl.BlockSpec((tm, tn), lambda i, j: (i, j))
    return pl.pallas_call(
        body, name=name, grid=(M // tm, F // tn),
        in_specs=[pl.BlockSpec((tm, D), lambda i, j: (i, 0)), w_spec, w_spec], out_specs=[o_spec] * 3,
        out_shape=[jax.ShapeDtypeStruct((M, F), BF16)] * 3, compiler_params=_params("parallel", "parallel"),
    )(n, w_gate_t, w_up_t)


def _swiglu_bwd(dh, w_down, gate, up, *, name, after=None, tm=1376, tn=512):
    M, D = dh.shape
    F = w_down.shape[0]
    tm, tn = _tile(M, tm, 16), _tile(F, tn, 128)
    n_after = 0 if after is None else 1

    def body(*refs):
        dh_ref, w_ref, g_ref, u_ref, dg_ref, du_ref = refs[n_after:]
        w = w_ref[...]
        for rows in _row_chunks(tm):
            da = _dot(dh_ref[rows, :], w, NT)
            s, ds = _silu_and_grad(g_ref[rows, :].astype(F32))
            dg_ref[rows, :] = (da * u_ref[rows, :].astype(F32) * ds).astype(dg_ref.dtype)
            du_ref[rows, :] = (da * s).astype(du_ref.dtype)

    o_spec = pl.BlockSpec((tm, tn), lambda i, j: (i, j))
    return pl.pallas_call(
        body, name=name, grid=(M // tm, F // tn),
        in_specs=[_ANY] * n_after + [pl.BlockSpec((tm, D), lambda i, j: (i, 0)),
                                     pl.BlockSpec((tn, D), lambda i, j: (j, 0)), o_spec, o_spec],
        out_specs=[o_spec, o_spec], out_shape=[jax.ShapeDtypeStruct((M, F), BF16)] * 2,
        compiler_params=_params("parallel", "parallel"),
    )(*((after,) if n_after else ()), dh, w_down, gate, up)


def _adamw_update(w, g, m, v):
    nm = ADAM_B1 * m + (1.0 - ADAM_B1) * g
    nv = ADAM_B2 * v + (1.0 - ADAM_B2) * (g * g)
    m_hat = nm / (1.0 - ADAM_B1 ** ADAM_STEP)
    v_hat = nv / (1.0 - ADAM_B2 ** ADAM_STEP)
    return -ADAM_LR * (m_hat / (jnp.sqrt(v_hat) + ADAM_EPS) + ADAM_WD * w), nm, nv


def _adamw(w, g, m, v, *, name):
    shape = w.shape
    cols = shape[-1]
    rows = w.size // cols
    w2, g2, m2, v2 = (t.reshape(rows, cols) for t in (w, g, m, v))
    if rows % 8 == 0 or cols % 128 != 0:
        tr, tc = (_tile(rows, 256, 8) if rows % 8 == 0 else rows), cols
    else:
        tr, tc = rows, _tile(cols, 256, 128)

    def body(w_ref, g_ref, m_ref, v_ref, d_ref, nm_ref, nv_ref):
        d_ref[...], nm_ref[...], nv_ref[...] = _adamw_update(w_ref[...], g_ref[...], m_ref[...], v_ref[...])

    blk = pl.BlockSpec((tr, tc), lambda i, j: (i, j))
    outs = pl.pallas_call(
        body, name=name, grid=(rows // tr, cols // tc), in_specs=[blk] * 4, out_specs=[blk] * 3,
        out_shape=[jax.ShapeDtypeStruct((rows, cols), F32)] * 3, compiler_params=_params("parallel", "parallel"),
    )(w2, g2, m2, v2)
    return tuple(t.reshape(shape) for t in outs)


def _sum_slabs(x, *, name):
    _, R, C = x.shape
    sub = 16 if x.dtype == BF16 else 8
    if R % sub == 0:
        tr, tc = _tile(R, 128, sub), C
    else:
        tr, tc = R, _tile(C, 256, 128)

    def body(x_ref, o_ref):
        acc = x_ref[0].astype(F32)
        for s in range(1, N_DEV):
            acc = acc + x_ref[s].astype(F32)
        o_ref[...] = acc

    return pl.pallas_call(
        body, name=name, grid=(R // tr, C // tc),
        in_specs=[pl.BlockSpec((N_DEV, tr, tc), lambda i, j: (0, i, j))],
        out_specs=pl.BlockSpec((tr, tc), lambda i, j: (i, j)),
        out_shape=jax.ShapeDtypeStruct((R, C), F32), compiler_params=_params("parallel", "parallel"),
    )(x)


def _sum_adamw(x, w, m, v, *, name):
    _, R, C = x.shape
    if R % 16 == 0:
        tr, tc = _tile(R, 128, 16), C
    else:
        tr, tc = R, _tile(C, 256, 128)

    def body(x_ref, w_ref, m_ref, v_ref, g_ref, d_ref, nm_ref, nv_ref):
        g = x_ref[0].astype(F32)
        for s in range(1, N_DEV):
            g = g + x_ref[s].astype(F32)
        g_ref[...] = g
        d_ref[...], nm_ref[...], nv_ref[...] = _adamw_update(w_ref[...], g, m_ref[...], v_ref[...])

    blk = pl.BlockSpec((tr, tc), lambda i, j: (i, j))
    return pl.pallas_call(
        body, name=name, grid=(R // tr, C // tc),
        in_specs=[pl.BlockSpec((N_DEV, tr, tc), lambda i, j: (0, i, j)), blk, blk, blk], out_specs=[blk] * 4,
        out_shape=[jax.ShapeDtypeStruct((R, C), F32)] * 4, compiler_params=_params("parallel", "parallel"),
    )(x, w, m, v)


def _peers():
    x, y, c = lax.axis_index("x"), lax.axis_index("y"), lax.axis_index("c")
    me = 4 * x + 2 * y + c
    peers = []
    for k in range(1, N_DEV):
        px = 1 - x if k & 4 else x
        py = 1 - y if k & 2 else y
        pc = 1 - c if k & 1 else c
        peers.append(((px, py, pc), 4 * px + 2 * py + pc))
    return me, peers


def _gather(x, *, name):
    def body(x_ref, o_ref, send_sems, recv_sems, own_sem):
        me, peers = _peers()
        own = pltpu.make_async_copy(x_ref, o_ref.at[me], own_sem)
        own.start()
        sends, recvs = [], []
        for k, (pos, idx) in enumerate(peers):
            sends.append(pltpu.make_async_remote_copy(
                src_ref=x_ref, dst_ref=o_ref.at[me], send_sem=send_sems.at[k], recv_sem=recv_sems.at[k],
                device_id=pos, device_id_type=pl.DeviceIdType.MESH))
            recvs.append(pltpu.make_async_remote_copy(
                src_ref=x_ref, dst_ref=o_ref.at[idx], send_sem=send_sems.at[k], recv_sem=recv_sems.at[k],
                device_id=pos, device_id_type=pl.DeviceIdType.MESH))
        for cp in sends:
            cp.start()
        for cp in recvs:
            cp.wait_recv()
        for cp in sends:
            cp.wait_send()
        own.wait()

    hbm = pl.BlockSpec(memory_space=pltpu.HBM)
    return pl.pallas_call(
        body, name=name, in_specs=[hbm], out_specs=hbm,
        out_shape=jax.ShapeDtypeStruct((N_DEV,) + tuple(x.shape), x.dtype),
        scratch_shapes=[pltpu.SemaphoreType.DMA((N_DEV - 1,)), pltpu.SemaphoreType.DMA((N_DEV - 1,)),
                        pltpu.SemaphoreType.DMA],
    )(x)


_HBM = pl.BlockSpec(memory_space=pltpu.HBM)
_SEM = pl.BlockSpec(memory_space=pltpu.SEMAPHORE)
_EFFECT = pltpu.SideEffectType.DATAFLOW_SIDE_EFFECTING


PLAN_GATHER = tuple((k, "x", 0) for k in range(1, N_DEV))
PLAN_SCATTER = tuple((k, "xk", 0) for k in range(1, N_DEV))
PLAN_GATHER_CHIPS = tuple((k, "x", 0) for k in (1, 2, 4, 6))
PLAN_GATHER_PASS_ON = tuple((1, ("land", q), q) for q in (2, 4, 6))


def _plan_refs(plan, j, x_ref, land_ref, me, peers, receiving):
    k, source, r = plan[j]
    index_of = lambda q: me if q == 0 else peers[q - 1][1]
    pos, target = peers[k - 1]
    if source == "x":
        src = x_ref
    elif source == "xk":
        src = x_ref.at[target]
    else:
        src = land_ref.at[index_of(source[1])]
    return pos, src, land_ref.at[index_of(k ^ r) if receiving else index_of(r)]


def _exchange_start(x, *, plan, name, after=None, land=None, slab=None):
    n_after = 0 if after is None else 1
    n = len(plan)

    def body(*refs):
        x_ref, land_ref, send_sems, recv_sems, _, _, token = refs[n_after:]
        me, peers = _peers()
        for j in range(n):
            pos, src, dst = _plan_refs(plan, j, x_ref, land_ref, me, peers, receiving=False)
            pltpu.make_async_remote_copy(src_ref=src, dst_ref=dst, send_sem=send_sems.at[j], recv_sem=recv_sems.at[j],
                                         device_id=pos, device_id_type=pl.DeviceIdType.MESH).start()
        token[...] = jnp.zeros_like(token)

    if land is None:
        land = lax.empty((N_DEV,) + tuple(slab), x.dtype)
    return pl.pallas_call(
        body, name=name,
        out_shape=(pltpu.SemaphoreType.DMA((n,)), pltpu.SemaphoreType.DMA((n,)),
                   pltpu.HBM(x.shape, x.dtype), pltpu.HBM(land.shape, land.dtype), jax.ShapeDtypeStruct((8, 128), F32)),
        in_specs=[_ANY] * n_after + [_HBM, _HBM],
        out_specs=(_SEM, _SEM, _HBM, _HBM, pl.BlockSpec(memory_space=pltpu.VMEM)),
        input_output_aliases={n_after: 2, n_after + 1: 3},
        compiler_params=pltpu.CompilerParams(has_side_effects=_EFFECT),
    )(*((after,) if n_after else ()), pltpu.with_memory_space_constraint(x, pltpu.HBM),
      pltpu.with_memory_space_constraint(land, pltpu.HBM))


def _exchange_wait(handle, after, *, plan, name):
    send_sems, recv_sems, x_thru, land_thru, _ = handle
    afters = list(after) if isinstance(after, (list, tuple)) else [after]

    def body(x_ref, land_ref, send_sems, recv_sems, *rest):
        me, peers = _peers()
        for j in range(len(plan)):
            pos, src, dst = _plan_refs(plan, j, x_ref, land_ref, me, peers, receiving=True)
            cp = pltpu.make_async_remote_copy(src_ref=src, dst_ref=dst, send_sem=send_sems.at[j], recv_sem=recv_sems.at[j],
                                              device_id=pos, device_id_type=pl.DeviceIdType.MESH)
            cp.wait_send()
            cp.wait_recv()

    return pl.pallas_call(
        body, name=name,
        out_shape=(pltpu.HBM(x_thru.shape, x_thru.dtype), pltpu.HBM(land_thru.shape, land_thru.dtype)),
        in_specs=[_HBM, _HBM, _SEM, _SEM] + [_ANY] * len(afters), out_specs=(_HBM, _HBM),
        input_output_aliases={0: 0, 1: 1}, compiler_params=pltpu.CompilerParams(has_side_effects=_EFFECT),
    )(x_thru, land_thru, send_sems, recv_sems, *afters)


W_IN_SLAB = D_IN // N_DEV


def _to_proj_rows(t):
    z = jnp.zeros((D_PROJ - C_SM - 2 * GDN_HEADS - GLA_RANK,) + t.shape[1:], t.dtype)
    return jnp.concatenate([t[R_Z:R_A], t[R_GR:R_LR], t[R_GQ:R_GR], t[:R_Z], t[R_A:R_GQ], t[R_LR:], z], axis=0)


def _from_proj_rows(t):
    ab = C_SM + 2 * GDN_HEADS
    return jnp.concatenate([t[C_QKV:C_SM], t[C_Z:C_GR], t[C_SM:ab], t[C_GQ:C_QKV], t[C_GR:C_GQ],
                            t[ab:ab + GLA_RANK]], axis=0)


def _local_step(x, target, meta, attn_nw, conv_w, a_log, dt_bias, gdn_nw, w2, b2, gla_nw, ffn_nw, final_nw,
                fetch, emit, start=None):
    S = x.shape[0]
    head = jnp.concatenate([jnp.zeros((ROW_PAD, D_MODEL), F32), meta], axis=0)
    conv_w8 = jnp.concatenate([conv_w, jnp.zeros((8 - CONV_K, conv_w.shape[1]), F32)], axis=0)
    w2p = jnp.zeros((SM_W, GLA_QK), F32).at[2 * GDN_HEADS:2 * GDN_HEADS + GLA_RANK].set(w2)
    alog_p = jnp.zeros((1, SM_W), F32).at[:, :GDN_HEADS].set(a_log)
    dt_p = jnp.zeros((1, SM_W), F32).at[:, :GDN_HEADS].set(dt_bias)

    h0, n1 = _embed_norm(head, x, attn_nw, name="attn_norm", after=start)
    w_in_t = fetch("w_in_t", (n1, conv_w8, w2p, alog_p, dt_p))
    proj = _matmul(n1, w_in_t, mode="nt", name="in_proj")
    gb, la = _gates_fwd(proj, w2p, b2, alog_p, dt_p, name="gates")
    act = _prep_fwd(proj, conv_w8, name="gdn_prep")
    o_gdn, s_gdn, t_gdn = _gdn_fwd(act, gb, name="gdn_fwd")
    o_gla, s_gla = _gla_fwd(proj, la, name="gla_fwd")
    mixed = _mix_fwd(o_gdn, o_gla, proj, gdn_nw, gla_nw, name="mix")
    w_out = fetch("w_out", mixed)
    h1 = _matmul(mixed, w_out, mode="nn", add=h0, name="out_proj")
    n2 = _rmsnorm_fwd(h1, ffn_nw, name="ffn_norm")
    w_gate_t, w_up_t = fetch("w_gate_t", n2), fetch("w_up_t", n2)
    gate, up, hid = _swiglu_fwd(n2, w_gate_t, w_up_t, name="swiglu")
    w_down = fetch("w_down", hid)
    h2 = _matmul(hid, w_down, mode="nn", add=h1, name="ffn_down", tm=1376, tn=256)
    dh2, dh2_b, d_final_nw, loss = _loss_head(h2, final_nw, target, name="loss_head")

    wg = dict(mode="tn", out_dtype=BF16, tn=512)
    tok = emit("w_down", _matmul(hid, dh2_b, name="d_w_down", tm=704, **wg))
    d_gate, d_up = _swiglu_bwd(dh2_b, w_down, gate, up, name="d_swiglu", after=tok)
    tok = emit("w_gate_t", _matmul(d_gate, n2, name="d_w_gate", tm=704, **wg))
    tok = emit("w_up_t", _matmul(d_up, n2, name="d_w_up", tm=704, after=tok, **wg))
    d_n2 = _matmul_pair(d_gate, w_gate_t, d_up, w_up_t, name="d_n2", after=tok)
    dh1, dh1_b, d_ffn_nw = _rmsnorm_bwd(h1, ffn_nw, d_n2, dh2, name="d_ffn_norm")

    tok = emit("w_out", _matmul(mixed, dh1_b, name="d_w_out", tm=512, **wg))
    d_mixed = _matmul(dh1_b, w_out, mode="nt", name="d_mixed", after=tok)
    do_gdn, do_gla, d_proj, d_gdn_nw, d_gla_nw = _mix_bwd(o_gdn, o_gla, proj, gdn_nw, gla_nw, d_mixed, name="d_mix")
    d_proj, d_la = _gla_bwd(proj, la, do_gla, s_gla, d_proj, name="gla_bwd")
    dact, dgb_heads = _gdn_bwd(act, gb, do_gdn, s_gdn, t_gdn, name="gdn_bwd")
    d_proj, d_w2p, d_b2, d_alog, d_dt = _gates_bwd(proj, w2p, b2, alog_p, dt_p, dgb_heads, d_la, d_proj, name="d_gates")
    d_proj, d_conv_w8 = _prep_bwd(proj, conv_w8, dact, d_proj, name="d_gdn_prep")
    tok = emit("w_in_t", _matmul(d_proj, n1, name="d_w_in", tm=768, **wg))
    d_n1 = _matmul(d_proj, w_in_t, mode="nn", name="d_n1", tm=688, after=tok)
    grad_x, d_head, d_attn_nw = _embed_norm_bwd(h0, attn_nw, d_n1, dh1, name="d_attn_norm")

    return dict(
        loss=loss[0, 0], grad_x=grad_x, meta=d_head[ROW_PAD:HEAD_ROWS], attn_nw=d_attn_nw,
        conv_w=d_conv_w8[:CONV_K], a_log=d_alog[:, :GDN_HEADS], dt_bias=d_dt[:, :GDN_HEADS], gdn_nw=d_gdn_nw,
        w2=d_w2p[2 * GDN_HEADS:2 * GDN_HEADS + GLA_RANK], b2=d_b2, gla_nw=d_gla_nw, ffn_nw=d_ffn_nw,
        final_nw=d_final_nw)


SMALL_ROWS = 32


def kernel(x, meta_tokens, attn_norm_w, w_in, gdn_conv_w, gdn_a_log, gdn_dt_bias, gdn_norm_w, gla_gate_w2, gla_gate_b, gla_norm_w, w_out, ffn_norm_w, w_gate, w_up, w_down, final_norm_w, loss_target, m_meta_tokens, m_attn_norm_w, m_w_in, m_gdn_conv_w, m_gdn_a_log, m_gdn_dt_bias, m_gdn_norm_w, m_gla_gate_w2, m_gla_gate_b, m_gla_norm_w, m_w_out, m_ffn_norm_w, m_w_gate, m_w_up, m_w_down, m_final_norm_w, v_meta_tokens, v_attn_norm_w, v_w_in, v_gdn_conv_w, v_gdn_a_log, v_gdn_dt_bias, v_gdn_norm_w, v_gla_gate_w2, v_gla_gate_b, v_gla_norm_w, v_w_out, v_ffn_norm_w, v_w_gate, v_w_up, v_w_down, v_final_norm_w):
    me = 4 * lax.axis_index("x") + 2 * lax.axis_index("y") + lax.axis_index("c")

    n_conv = gdn_conv_w.shape[2]
    n_w2 = gla_gate_w2.shape[2]
    n_meta = meta_tokens.shape[1]
    small = jnp.zeros((40, n_conv), F32)
    small = small.at[0:N_META, :n_meta].set(meta_tokens)
    small = small.at[N_META:N_META + CONV_K, :].set(gdn_conv_w[0])
    small = small.at[24:24 + GLA_RANK, :n_w2].set(gla_gate_w2[0])
    small_all = _gather(small, name="gather_small")
    meta_f = small_all[:, 0:N_META, :n_meta].transpose(1, 0, 2).reshape(N_META, D_MODEL)
    conv_f = small_all[:, N_META:N_META + CONV_K, :].transpose(1, 0, 2).reshape(CONV_K, N_DEV * n_conv)
    w2_f = small_all[:, 24:24 + GLA_RANK, :n_w2].transpose(1, 0, 2).reshape(GLA_RANK, N_DEV * n_w2)

    w_in_slab = w_in[0].T.astype(BF16)
    in_h = _exchange_start(w_in_slab, plan=PLAN_GATHER_CHIPS, slab=w_in_slab.shape, name="gather_w_in_start",
                           after=small_all)
    handles, tok = {}, in_h[4]
    for wname, slab in (("w_out", w_out[0]), ("w_gate_t", w_gate[0].T), ("w_up_t", w_up[0].T), ("w_down", w_down[0])):
        slab = slab.astype(BF16)
        handles[wname] = _exchange_start(slab, plan=PLAN_GATHER, slab=slab.shape, name="gather_" + wname + "_start", after=tok)
        tok = handles[wname][4]

    def fetch(name, after):
        if name == "w_in_t":
            own, got = _exchange_wait(in_h, after, plan=PLAN_GATHER_CHIPS, name="gather_w_in_wait")
            pass_h = _exchange_start(own, plan=PLAN_GATHER_PASS_ON, land=got, name="pass_w_in_start")
            own, got = _exchange_wait(pass_h, pass_h[4], plan=PLAN_GATHER_PASS_ON, name="pass_w_in_wait")
            got = lax.dynamic_update_index_in_dim(got, own, me, 0)
            return _to_proj_rows(got.reshape(D_IN, D_MODEL))
        own, got = _exchange_wait(handles[name], after, plan=PLAN_GATHER, name="gather_" + name + "_wait")
        got = lax.dynamic_update_index_in_dim(got, own, me, 0)
        return got.reshape(N_DEV * got.shape[1], D_MODEL)

    sent = {}

    def emit(name, grad):
        if name == "w_in_t":
            grad = _from_proj_rows(grad)
        parts = grad.reshape(N_DEV, grad.shape[0] // N_DEV, D_MODEL)
        sent[name] = _exchange_start(parts, plan=PLAN_SCATTER, slab=parts.shape[1:], name="scatter_" + name + "_start")
        return sent[name][4]

    g = _local_step(x[0], loss_target[0], meta_f, attn_norm_w, conv_f, gdn_a_log, gdn_dt_bias, gdn_norm_w, w2_f,
                    gla_gate_b, gla_norm_w, ffn_norm_w, final_norm_w.reshape(1, D_MODEL), fetch, emit, start=tok)

    big = {}
    after = g["attn_nw"]
    for name, w, m, v, transposed in (("w_down", w_down, m_w_down, v_w_down, False), ("w_gate_t", w_gate, m_w_gate, v_w_gate, True),
                                      ("w_up_t", w_up, m_w_up, v_w_up, True), ("w_out", w_out, m_w_out, v_w_out, False),
                                      ("w_in_t", w_in, m_w_in, v_w_in, True)):
        own, got = _exchange_wait(sent[name], after, plan=PLAN_SCATTER, name="scatter_" + name + "_wait")
        got = lax.dynamic_update_index_in_dim(got, lax.dynamic_index_in_dim(own, me, 0, keepdims=False), me, 0)
        local = [t[0].T if transposed else t[0] for t in (w, m, v)]
        res = _sum_adamw(got, *local, name="adamw_" + name)
        big[name] = [t.T[None] if transposed else t[None] for t in res]
        after = res[0]

    misc = jnp.concatenate([g["a_log"], g["dt_bias"], g["gdn_nw"], g["gla_nw"], g["b2"], g["loss"].reshape(1, 1)], axis=1)
    n_misc = misc.shape[1]
    misc = jnp.pad(misc, ((0, 0), (0, D_MODEL - n_misc)))
    rows = jnp.concatenate([g["attn_nw"], g["ffn_nw"], g["final_nw"], misc, g["meta"],
                            g["conv_w"].reshape(-1, D_MODEL), g["w2"].reshape(-1, D_MODEL)], axis=0)
    rows = jnp.pad(rows, ((0, SMALL_ROWS - rows.shape[0]), (0, 0)))
    tot = _sum_slabs(_gather(rows, name="gather_small_grads"), name="sum_small_grads")
    grad_attn_nw, grad_ffn_nw, grad_final_nw = tot[0:1], tot[1:2], tot[2]
    grad_a_log = tot[3:4, 0:8]
    grad_dt = tot[3:4, 8:16]
    grad_gdn_nw = tot[3:4, 16:16 + GDN_DV]
    grad_gla_nw = tot[3:4, 144:144 + GLA_DV]
    grad_b2 = tot[3:4, 400:400 + GLA_QK]
    loss = tot[3, n_misc - 1]
    r0 = 4 + N_META
    grad_meta = lax.dynamic_slice(tot[4:r0], (0, me * n_meta), (N_META, n_meta))
    r1 = r0 + CONV_K * N_DEV * n_conv // D_MODEL
    grad_conv = lax.dynamic_slice(tot[r0:r1].reshape(CONV_K, N_DEV * n_conv), (0, me * n_conv), (CONV_K, n_conv))[None]
    r2 = r1 + GLA_RANK * N_DEV * n_w2 // D_MODEL
    grad_w2 = lax.dynamic_slice(tot[r1:r2].reshape(GLA_RANK, N_DEV * n_w2), (0, me * n_w2), (GLA_RANK, n_w2))[None]

    weights = [meta_tokens, attn_norm_w, w_in, gdn_conv_w, gdn_a_log, gdn_dt_bias, gdn_norm_w, gla_gate_w2,
               gla_gate_b, gla_norm_w, w_out, ffn_norm_w, w_gate, w_up, w_down, final_norm_w]
    grads = [grad_meta, grad_attn_nw, "w_in_t", grad_conv, grad_a_log, grad_dt, grad_gdn_nw, grad_w2,
             grad_b2, grad_gla_nw, "w_out", grad_ffn_nw, "w_gate_t", "w_up_t", "w_down", grad_final_nw]
    ms = [m_meta_tokens, m_attn_norm_w, m_w_in, m_gdn_conv_w, m_gdn_a_log, m_gdn_dt_bias, m_gdn_norm_w,
          m_gla_gate_w2, m_gla_gate_b, m_gla_norm_w, m_w_out, m_ffn_norm_w, m_w_gate, m_w_up, m_w_down, m_final_norm_w]
    vs = [v_meta_tokens, v_attn_norm_w, v_w_in, v_gdn_conv_w, v_gdn_a_log, v_gdn_dt_bias, v_gdn_norm_w,
          v_gla_gate_w2, v_gla_gate_b, v_gla_norm_w, v_w_out, v_ffn_norm_w, v_w_gate, v_w_up, v_w_down, v_final_norm_w]
    outs = [[], [], [], []]
    for idx, (w, gr, m, v) in enumerate(zip(weights, grads, ms, vs)):
        if isinstance(gr, str):
            res = big[gr]
        else:
            gr = gr.reshape(w.shape)
            res = (gr,) + _adamw(w, gr, m, v, name=f"adamw_{idx}")
        for lst, t in zip(outs, res):
            lst.append(t)
    return (loss, g["grad_x"][None], *outs[0], *outs[1], *outs[2], *outs[3])
```

```python
import functools

import jax
import jax.numpy as jnp
from jax import lax
from jax.experimental import pallas as pl
from jax.experimental.pallas import tpu as pltpu

F32 = jnp.float32
BF16 = jnp.bfloat16
_MXU_DTYPE = jnp.bfloat16

D_MODEL = 2048
N_META = 16
ROW_PAD = 48
HEAD_ROWS = ROW_PAD + N_META
CONV_K = 4
GDN_HEADS, GDN_DK, GDN_DV, GDN_CHUNK = 8, 128, 128, 64
GLA_HEADS, GLA_DK, GLA_DV, GLA_CHUNK = 4, 128, 256, 16
GLA_RANK = 16
GLA_GATE_NORMALIZER = 16.0
GDN_QK = GDN_HEADS * GDN_DK
GDN_V = GDN_HEADS * GDN_DV
GLA_QK = GLA_HEADS * GLA_DK
GLA_V = GLA_HEADS * GLA_DV
D_FF = 5632
D_IN = 7200
NORM_EPS = 1e-6
C_Z, C_GR, C_GQ, C_GK, C_GV, C_QKV, C_SM = 0, 1024, 2048, 2560, 3072, 4096, 7168
SM_W = 128
D_PROJ = 7680
R_Z, R_A, R_B, R_GQ, R_GK, R_GV, R_GR, R_LR = 3072, 4096, 4104, 4112, 4624, 5136, 6160, 7184

ADAM_LR, ADAM_B1, ADAM_B2, ADAM_EPS, ADAM_WD, ADAM_STEP = 0.001, 0.9, 0.999, 1e-08, 0.01, 10

N_DEV = 8
VMEM_LIMIT = 56 * 1024 * 1024

NN = (((1,), (0,)), ((), ()))
NT = (((1,), (1,)), ((), ()))
TN = (((0,), (0,)), ((), ()))


def _dot(a, b, dims=NN):
    return lax.dot_general(a.astype(_MXU_DTYPE), b.astype(_MXU_DTYPE), dims, preferred_element_type=F32)


def _dotx(a, b, dims=NN):
    return lax.dot_general(a, b, dims, precision=lax.Precision.HIGHEST, preferred_element_type=F32)


def _dot3(a, b):
    ah = a.astype(BF16)
    al = (a - ah.astype(F32)).astype(BF16)
    bh = b.astype(BF16)
    bl = (b - bh.astype(F32)).astype(BF16)
    d = functools.partial(lax.dot_general, dimension_numbers=NN, preferred_element_type=F32)
    return d(ah, bh) + (d(ah, bl) + d(al, bh))


def _tile(n, target, mult=8):
    best = None
    for t in range(mult, min(n, target) + 1, mult):
        if n % t == 0:
            best = t
    return best if best is not None else n


def _params(*sem):
    return pltpu.CompilerParams(dimension_semantics=sem, vmem_limit_bytes=VMEM_LIMIT)


def _sigmoid(x):
    return 0.5 * jnp.tanh(0.5 * x) + 0.5


def _softplus(x):
    return jnp.maximum(x, 0.0) + jnp.log1p(jnp.exp(-jnp.abs(x)))


def _silu_and_grad(c):
    s = _sigmoid(c)
    return c * s, s * (1.0 + c * (1.0 - s))


_ANY = pl.BlockSpec(memory_space=pl.ANY)


def _matmul(a, b, *, mode, name, out_dtype=F32, add=None, after=None, tm=1376, tn=512):
    if mode == "tn":
        K, M = a.shape
        N = b.shape[1]
    else:
        M, K = a.shape
        N = b.shape[0] if mode == "nt" else b.shape[1]
    tm = _tile(M, tm, 128 if mode == "tn" else 16)
    tn = _tile(N, tn, 128)
    dims = {"nn": NN, "nt": NT, "tn": TN}[mode]
    n_after = 0 if after is None else 1

    def body(*refs):
        refs = refs[n_after:]
        r = _dot(refs[0][...], refs[1][...], dims)
        if add is not None:
            r = r + refs[2][...]
        refs[-1][...] = r.astype(out_dtype)

    a_spec = pl.BlockSpec((K, tm), lambda i, j: (0, i)) if mode == "tn" else pl.BlockSpec((tm, K), lambda i, j: (i, 0))
    b_spec = pl.BlockSpec((tn, K), lambda i, j: (j, 0)) if mode == "nt" else pl.BlockSpec((K, tn), lambda i, j: (0, j))
    o_spec = pl.BlockSpec((tm, tn), lambda i, j: (i, j))
    in_specs = [_ANY] * n_after + [a_spec, b_spec] + ([o_spec] if add is not None else [])
    args = ((after,) if n_after else ()) + (a, b) + ((add,) if add is not None else ())
    return pl.pallas_call(
        body, name=name, grid=(M // tm, N // tn), in_specs=in_specs, out_specs=o_spec,
        out_shape=jax.ShapeDtypeStruct((M, N), out_dtype), compiler_params=_params("parallel", "parallel"),
    )(*args)


def _matmul_pair(a1, b1, a2, b2, *, name, after=None, tm=688, tn=256):
    M, K = a1.shape
    N = b1.shape[1]
    tm, tn = _tile(M, tm, 16), _tile(N, tn, 128)
    n_after = 0 if after is None else 1

    def body(*refs):
        a1_ref, b1_ref, a2_ref, b2_ref, o_ref = refs[n_after:]
        o_ref[...] = _dot(a1_ref[...], b1_ref[...]) + _dot(a2_ref[...], b2_ref[...])

    a_spec = pl.BlockSpec((tm, K), lambda i, j: (i, 0))
    b_spec = pl.BlockSpec((K, tn), lambda i, j: (0, j))
    return pl.pallas_call(
        body, name=name, grid=(M // tm, N // tn), in_specs=[_ANY] * n_after + [a_spec, b_spec, a_spec, b_spec],
        out_specs=pl.BlockSpec((tm, tn), lambda i, j: (i, j)), out_shape=jax.ShapeDtypeStruct((M, N), F32),
        compiler_params=_params("parallel", "parallel"),
    )(*((after,) if n_after else ()), a1, b1, a2, b2)


def _rmsnorm_fwd(h, w, *, name):
    M, D = h.shape
    tm = _tile(M, 688, 16)

    def body(h_ref, w_ref, n_ref):
        x = h_ref[...]
        r = lax.rsqrt(jnp.mean(x * x, axis=-1, keepdims=True) + NORM_EPS)
        n_ref[...] = (x * r * w_ref[...]).astype(n_ref.dtype)

    return pl.pallas_call(
        body, name=name, grid=(M // tm,),
        in_specs=[pl.BlockSpec((tm, D), lambda i: (i, 0)), pl.BlockSpec((1, D), lambda i: (0, 0))],
        out_specs=pl.BlockSpec((tm, D), lambda i: (i, 0)),
        out_shape=jax.ShapeDtypeStruct((M, D), BF16),
        compiler_params=_params("parallel"),
    )(h, w)


SEQ_BLOCK = HEAD_ROWS


def _seq_blocks_per_tile(rows):
    n = rows // SEQ_BLOCK
    return max(m for m in (1, 2, 3, 4) if n % m == 0)


def _seq_specs(m, D):
    return [pl.BlockSpec((SEQ_BLOCK, D), functools.partial(lambda i, k: (jnp.maximum(m * i + k - 1, 0), 0), k=k))
            for k in range(m)]


def _embed_norm(head, x, w, *, name, after=None):
    S, D = x.shape
    m = _seq_blocks_per_tile(S + HEAD_ROWS)
    n_after = 0 if after is None else 1

    def body(*refs):
        refs = refs[n_after:]
        head_ref, x_refs, w_ref, h_ref, n_ref = refs[0], refs[1:1 + m], refs[1 + m], refs[2 + m], refs[3 + m]
        i = pl.program_id(0)
        for k in range(m):
            blk = x_refs[k][...]
            if k == 0:
                blk = jnp.where(i == 0, head_ref[...], blk)
            rows = slice(k * SEQ_BLOCK, (k + 1) * SEQ_BLOCK)
            h_ref[rows, :] = blk
            r = lax.rsqrt(jnp.mean(blk * blk, axis=-1, keepdims=True) + NORM_EPS)
            n_ref[rows, :] = (blk * r * w_ref[...]).astype(n_ref.dtype)

    tile = pl.BlockSpec((m * SEQ_BLOCK, D), lambda i: (i, 0))
    return pl.pallas_call(
        body, name=name, grid=((S + HEAD_ROWS) // (m * SEQ_BLOCK),),
        in_specs=[_ANY] * n_after + [pl.BlockSpec((SEQ_BLOCK, D), lambda i: (0, 0))] + _seq_specs(m, D)
        + [pl.BlockSpec((1, D), lambda i: (0, 0))],
        out_specs=[tile, tile],
        out_shape=[jax.ShapeDtypeStruct((S + HEAD_ROWS, D), F32), jax.ShapeDtypeStruct((S + HEAD_ROWS, D), BF16)],
        compiler_params=_params("parallel"),
    )(*((after,) if n_after else ()), head, *([x] * m), w)


def _embed_norm_bwd(h, w, dn, dres, *, name):
    M, D = h.shape
    S = M - HEAD_ROWS
    m = _seq_blocks_per_tile(S)
    g = S // (m * SEQ_BLOCK)

    def one(x, dn_, dres_, w_):
        r = lax.rsqrt(jnp.mean(x * x, axis=-1, keepdims=True) + NORM_EPS)
        xhat = x * r
        dxhat = dn_ * w_
        dh = dres_ + r * (dxhat - xhat * jnp.mean(dxhat * xhat, axis=-1, keepdims=True))
        return dh, jnp.sum((dn_ * xhat).reshape(SEQ_BLOCK // 8, 8, D), axis=0)

    def body(*refs):
        w_ref = refs[0]
        groups = [refs[1 + a * (m + 1):1 + (a + 1) * (m + 1)] for a in range(3)]
        gx_ref, dhead_ref, dw_ref, acc_ref = refs[1 + 3 * (m + 1):]
        i = pl.program_id(0)
        w_ = w_ref[...]
        part = jnp.zeros((8, D), F32)
        for k in range(m):
            dh, p = one(*(grp[1 + k][...] for grp in groups), w_)
            gx_ref[k * SEQ_BLOCK:(k + 1) * SEQ_BLOCK, :] = dh
            part = part + p

        @pl.when(i == 0)
        def _():
            dh, p = one(*(grp[0][...] for grp in groups), w_)
            dhead_ref[...] = dh
            acc_ref[...] = part + p

        @pl.when(i > 0)
        def _():
            acc_ref[...] += part

        @pl.when(i == g - 1)
        def _():
            dw_ref[...] = jnp.sum(acc_ref[...], axis=0, keepdims=True)

    first = pl.BlockSpec((SEQ_BLOCK, D), lambda i: (0, 0))
    blocks = [pl.BlockSpec((SEQ_BLOCK, D), functools.partial(lambda i, k: (m * i + k + 1, 0), k=k)) for k in range(m)]
    vec = pl.BlockSpec((1, D), lambda i: (0, 0))
    return pl.pallas_call(
        body, name=name, grid=(g,), in_specs=[vec] + ([first] + blocks) * 3,
        out_specs=[pl.BlockSpec((m * SEQ_BLOCK, D), lambda i: (i, 0)), first, vec],
        out_shape=[jax.ShapeDtypeStruct((S, D), F32), jax.ShapeDtypeStruct((SEQ_BLOCK, D), F32),
                   jax.ShapeDtypeStruct((1, D), F32)],
        scratch_shapes=[pltpu.VMEM((8, D), F32)],
        compiler_params=_params("arbitrary"),
    )(w, *([h] * (m + 1)), *([dn] * (m + 1)), *([dres] * (m + 1)))


def _rmsnorm_bwd(h, w, dn, dres, *, name):
    M, D = h.shape
    tm = _tile(M, 344, 16)
    g = M // tm

    def body(h_ref, w_ref, dn_ref, dres_ref, dh_ref, dhb_ref, dw_ref, acc_ref):
        i = pl.program_id(0)
        x = h_ref[...]
        r = lax.rsqrt(jnp.mean(x * x, axis=-1, keepdims=True) + NORM_EPS)
        xhat = x * r
        dn_ = dn_ref[...]
        dxhat = dn_ * w_ref[...]
        dh = dres_ref[...] + r * (dxhat - xhat * jnp.mean(dxhat * xhat, axis=-1, keepdims=True))
        dh_ref[...] = dh
        dhb_ref[...] = dh.astype(dhb_ref.dtype)
        part = jnp.sum((dn_ * xhat).reshape(tm // 8, 8, D), axis=0)

        @pl.when(i == 0)
        def _():
            acc_ref[...] = part

        @pl.when(i > 0)
        def _():
            acc_ref[...] += part

        @pl.when(i == g - 1)
        def _():
            dw_ref[...] = jnp.sum(acc_ref[...], axis=0, keepdims=True)

    row = pl.BlockSpec((tm, D), lambda i: (i, 0))
    vec = pl.BlockSpec((1, D), lambda i: (0, 0))
    return pl.pallas_call(
        body, name=name, grid=(g,), in_specs=[row, vec, row, row],
        out_specs=[row, row, vec],
        out_shape=[jax.ShapeDtypeStruct((M, D), F32), jax.ShapeDtypeStruct((M, D), BF16),
                   jax.ShapeDtypeStruct((1, D), F32)],
        scratch_shapes=[pltpu.VMEM((8, D), F32)],
        compiler_params=_params("arbitrary"),
    )(h, w, dn, dres)


def _loss_head(h, w, target, *, name):
    M, D = h.shape
    m = _seq_blocks_per_tile(M)
    tm = m * SEQ_BLOCK
    g = M // tm

    def body(h_ref, w_ref, *rest):
        t_refs = rest[:m]
        dh_ref, dhb_ref, dw_ref, loss_ref, acc_ref, lacc_ref = rest[m:]
        i = pl.program_id(0)
        x = h_ref[...]
        row = i * tm + lax.broadcasted_iota(jnp.int32, (tm, 1), 0)
        live = row >= HEAD_ROWS
        r = lax.rsqrt(jnp.mean(x * x, axis=-1, keepdims=True) + NORM_EPS)
        xhat = x * r
        t = jnp.concatenate([t_ref[...] for t_ref in t_refs], axis=0)
        err = jnp.where(live, xhat * w_ref[...] - t, 0.0)
        dy = err * (1.0 / D)
        dxhat = dy * w_ref[...]
        dh = r * (dxhat - xhat * jnp.mean(dxhat * xhat, axis=-1, keepdims=True))
        dh_ref[...] = dh
        dhb_ref[...] = dh.astype(dhb_ref.dtype)
        part = jnp.sum((dy * xhat).reshape(tm // 8, 8, D), axis=0)
        lpart = jnp.sum((err * err).reshape(tm // 8, 8, D), axis=0)

        @pl.when(i == 0)
        def _():
            acc_ref[...] = part
            lacc_ref[...] = lpart

        @pl.when(i > 0)
        def _():
            acc_ref[...] += part
            lacc_ref[...] += lpart

        @pl.when(i == g - 1)
        def _():
            dw_ref[...] = jnp.sum(acc_ref[...], axis=0, keepdims=True)
            tot = jnp.sum(jnp.sum(lacc_ref[...], axis=0, keepdims=True), axis=1, keepdims=True)
            loss_ref[...] = jnp.broadcast_to(tot * (0.5 / D), (1, 128))

    row = pl.BlockSpec((tm, D), lambda i: (i, 0))
    vec = pl.BlockSpec((1, D), lambda i: (0, 0))
    return pl.pallas_call(
        body, name=name, grid=(g,), in_specs=[row, vec] + _seq_specs(m, D),
        out_specs=[row, row, vec, pl.BlockSpec((1, 128), lambda i: (0, 0))],
        out_shape=[jax.ShapeDtypeStruct((M, D), F32), jax.ShapeDtypeStruct((M, D), BF16),
                   jax.ShapeDtypeStruct((1, D), F32), jax.ShapeDtypeStruct((1, 128), F32)],
        scratch_shapes=[pltpu.VMEM((8, D), F32), pltpu.VMEM((8, D), F32)],
        compiler_params=_params("arbitrary"),
    )(h, w, *([target] * m))


def _gate_terms(sm, w2p, b2, alog_p, dt_p, row0):
    tm = sm.shape[0]
    lane = lax.broadcasted_iota(jnp.int32, (tm, SM_W), 1)
    live = (row0 + lax.broadcasted_iota(jnp.int32, (tm, 1), 0)) >= ROW_PAD
    pre = sm + dt_p
    neg_a = -jnp.exp(alog_p)
    g = neg_a * _softplus(pre)
    beta = _sigmoid(sm)
    z = _dot(sm, w2p) + b2
    return lane, live, pre, neg_a, g, beta, z


def _gates_fwd(proj, w2p, b2, alog_p, dt_p, *, name):
    M = proj.shape[0]
    tm = _tile(M, 688, 8)

    def body(sm_ref, w2_ref, b2_ref, al_ref, dt_ref, gb_ref, la_ref):
        row0 = pl.program_id(0) * tm
        lane, live, _, _, g, beta, z = _gate_terms(sm_ref[...], w2_ref[...], b2_ref[...], al_ref[...], dt_ref[...], row0)
        gb = jnp.where(lane < GDN_HEADS, g, jnp.where(lane < 2 * GDN_HEADS, beta, 0.0))
        gb_ref[...] = jnp.where(live, gb, 0.0)
        la = (jnp.minimum(z, 0.0) - jnp.log1p(jnp.exp(-jnp.abs(z)))) * (1.0 / GLA_GATE_NORMALIZER)
        la_ref[...] = jnp.where(live, la, 0.0)

    full = lambda s: pl.BlockSpec(s, lambda i: (0, 0))
    return pl.pallas_call(
        body, name=name, grid=(M // tm,),
        in_specs=[pl.BlockSpec((tm, SM_W), lambda i: (i, C_SM // SM_W)), full((SM_W, GLA_QK)), full((1, GLA_QK)),
                  full((1, SM_W)), full((1, SM_W))],
        out_specs=[pl.BlockSpec((tm, SM_W), lambda i: (i, 0)), pl.BlockSpec((tm, GLA_QK), lambda i: (i, 0))],
        out_shape=[jax.ShapeDtypeStruct((M, SM_W), F32), jax.ShapeDtypeStruct((M, GLA_QK), F32)],
        compiler_params=_params("parallel"),
    )(proj, w2p, b2, alog_p, dt_p)


def _gates_bwd(proj, w2p, b2, alog_p, dt_p, dgb_heads, dla, d_proj, *, name):
    M = proj.shape[0]
    tm = _tile(M, 688, 8)
    g_ = M // tm

    tail_w = D_PROJ - C_SM

    def body(sm_ref, w2_ref, b2_ref, al_ref, dt_ref, dgb_ref, dla_ref, _,
             dsm_ref, dw2_ref, db2_ref, dal_ref, ddt_ref):
        i = pl.program_id(0)
        sm = sm_ref[...]
        lane, live, pre, neg_a, g, beta, z = _gate_terms(sm, w2_ref[...], b2_ref[...], al_ref[...], dt_ref[...], i * tm)
        dz = jnp.where(live, dla_ref[...] * (_sigmoid(-z) * (1.0 / GLA_GATE_NORMALIZER)), 0.0)
        dsm_lr = _dot(dz, w2_ref[...], NT)
        dgb = dgb_ref[0]
        for hh in range(1, GDN_HEADS):
            dgb = dgb + dgb_ref[hh]
        dgb = jnp.where(live, dgb, 0.0)
        da = dgb * neg_a * _sigmoid(pre)
        db = dgb * beta * (1.0 - beta)
        dsm = jnp.where(lane < GDN_HEADS, da, jnp.where(lane < 2 * GDN_HEADS, db, dsm_lr))
        dsm_ref[:, 0:SM_W] = dsm.astype(dsm_ref.dtype)
        dsm_ref[:, SM_W:tail_w] = jnp.zeros((tm, tail_w - SM_W), dsm_ref.dtype)
        is_a = lane < GDN_HEADS
        dal = jnp.sum(jnp.where(is_a, dgb * g, 0.0), axis=0, keepdims=True)
        ddt = jnp.sum(jnp.where(is_a, da, 0.0), axis=0, keepdims=True)
        dw2 = _dot(sm, dz, TN)
        db2 = jnp.sum(dz, axis=0, keepdims=True)

        @pl.when(i == 0)
        def _():
            dw2_ref[...] = dw2
            db2_ref[...] = db2
            dal_ref[...] = dal
            ddt_ref[...] = ddt

        @pl.when(i > 0)
        def _():
            dw2_ref[...] += dw2
            db2_ref[...] += db2
            dal_ref[...] += dal
            ddt_ref[...] += ddt

    full = lambda s: pl.BlockSpec(s, lambda i: (0, 0))
    return pl.pallas_call(
        body, name=name, grid=(g_,),
        in_specs=[pl.BlockSpec((tm, SM_W), lambda i: (i, C_SM // SM_W)), full((SM_W, GLA_QK)), full((1, GLA_QK)),
                  full((1, SM_W)), full((1, SM_W)),
                  pl.BlockSpec((GDN_HEADS, tm, SM_W), lambda i: (0, i, 0)),
                  pl.BlockSpec((tm, GLA_QK), lambda i: (i, 0)), _ANY],
        out_specs=[pl.BlockSpec((tm, tail_w), lambda i: (i, C_SM // tail_w)), full((SM_W, GLA_QK)), full((1, GLA_QK)),
                   full((1, SM_W)), full((1, SM_W))],
        out_shape=[jax.ShapeDtypeStruct(d_proj.shape, d_proj.dtype), jax.ShapeDtypeStruct((SM_W, GLA_QK), F32),
                   jax.ShapeDtypeStruct((1, GLA_QK), F32), jax.ShapeDtypeStruct((1, SM_W), F32),
                   jax.ShapeDtypeStruct((1, SM_W), F32)],
        input_output_aliases={7: 0},
        compiler_params=_params("arbitrary"),
    )(proj, w2p, b2, alog_p, dt_p, dgb_heads, dla, d_proj)


QKV_W = GDN_QK
N_QKV_GROUPS = 3
QKV_B0 = C_QKV // QKV_W
HALO = 8


def _conv_terms(x_ref, halo_ref, cw_ref, xs_ref, i, tm):
    xs_ref[HALO:HALO + tm, :] = x_ref[...]
    xs_ref[0:HALO, :] = jnp.where(i > 0, halo_ref[...], 0.0)
    cw = cw_ref[...]
    xs = xs_ref[...]
    taps = [(pltpu.roll(xs, CONV_K - 1 - t, 0) if t < CONV_K - 1 else xs)[HALO:HALO + tm, :] for t in range(CONV_K)]
    c = taps[0] * cw[0:1, :]
    for t in range(1, CONV_K):
        c = c + taps[t] * cw[t:t + 1, :]
    return c, taps


def _prep_fwd(proj, conv_w8, *, name):
    M = proj.shape[0]
    tm = _tile(M, 344, 8)

    def body(x_ref, halo_ref, cw_ref, o_ref, xs_ref):
        j, i = pl.program_id(0), pl.program_id(1)
        c, _ = _conv_terms(x_ref, halo_ref, cw_ref, xs_ref, i, tm)
        s, _ = _silu_and_grad(c)
        scale = jnp.where(j == 0, GDN_DK ** -0.5, 1.0)
        for hh in range(GDN_HEADS):
            cols = slice(hh * 128, (hh + 1) * 128)
            sh = s[:, cols]
            r = lax.rsqrt(jnp.sum(sh * sh, axis=-1, keepdims=True) + NORM_EPS)
            o_ref[:, cols] = jnp.where(j < 2, sh * (r * scale), sh)

    hb = tm // HALO
    return pl.pallas_call(
        body, name=name, grid=(N_QKV_GROUPS, M // tm),
        in_specs=[pl.BlockSpec((tm, QKV_W), lambda j, i: (i, QKV_B0 + j)),
                  pl.BlockSpec((HALO, QKV_W), lambda j, i: (jnp.maximum(i * hb - 1, 0), QKV_B0 + j)),
                  pl.BlockSpec((8, QKV_W), lambda j, i: (0, j))],
        out_specs=pl.BlockSpec((tm, QKV_W), lambda j, i: (i, j)),
        out_shape=jax.ShapeDtypeStruct((M, N_QKV_GROUPS * QKV_W), F32),
        scratch_shapes=[pltpu.VMEM((tm + HALO, QKV_W), F32)],
        compiler_params=_params("parallel", "arbitrary"),
    )(proj, proj, conv_w8)


def _prep_bwd(proj, conv_w8, dact, d_proj, *, name):
    M = proj.shape[0]
    tm = _tile(M, 688, 16)
    g_ = M // tm
    ext = tm + HALO

    def body(x_ref, prev_ref, next_ref, cw_ref, da_ref, dan_ref, _, o_ref, dcw_ref, xs_ref, das_ref, dcs_ref):
        j, i = pl.program_id(0), pl.program_id(1)
        not_last = i < g_ - 1
        xs_ref[0:HALO, :] = jnp.where(i > 0, prev_ref[...], 0.0)
        xs_ref[HALO:HALO + tm, :] = x_ref[...]
        xs_ref[HALO + tm:HALO + ext, :] = jnp.where(not_last, next_ref[...], 0.0)
        das_ref[0:tm, :] = da_ref[...]
        das_ref[tm:ext, :] = jnp.where(not_last, dan_ref[...], 0.0)
        cw = cw_ref[...]
        xs = xs_ref[...]
        taps = [(pltpu.roll(xs, CONV_K - 1 - t, 0) if t < CONV_K - 1 else xs)[HALO:HALO + ext, :] for t in range(CONV_K)]
        c = taps[0] * cw[0:1, :]
        for t in range(1, CONV_K):
            c = c + taps[t] * cw[t:t + 1, :]
        s, ds_dc = _silu_and_grad(c)
        scale = jnp.where(j == 0, GDN_DK ** -0.5, 1.0)
        for hh in range(GDN_HEADS):
            cols = slice(hh * 128, (hh + 1) * 128)
            sh = s[:, cols]
            r = lax.rsqrt(jnp.sum(sh * sh, axis=-1, keepdims=True) + NORM_EPS)
            da = das_ref[:, cols]
            y = sh * r
            dy = da * scale
            ds_norm = r * (dy - y * jnp.sum(dy * y, axis=-1, keepdims=True))
            dcs_ref[:, cols] = jnp.where(j < 2, ds_norm, da) * ds_dc[:, cols]
        dc = dcs_ref[...]
        acc = dc[0:tm, :] * cw[CONV_K - 1:CONV_K, :]
        for t in range(CONV_K - 1):
            acc = acc + pltpu.roll(dc, ext - (CONV_K - 1 - t), 0)[0:tm, :] * cw[t:t + 1, :]
        o_ref[...] = acc.astype(o_ref.dtype)
        r8 = lax.broadcasted_iota(jnp.int32, (8, QKV_W), 0)
        part = jnp.zeros((8, QKV_W), F32)
        for t in range(CONV_K):
            part = jnp.where(r8 == t, jnp.sum(dc[0:tm, :] * taps[t][0:tm, :], axis=0, keepdims=True), part)

        @pl.when(i == 0)
        def _():
            dcw_ref[...] = part

        @pl.when(i > 0)
        def _():
            dcw_ref[...] += part

    hb = tm // HALO
    last = M // HALO - 1
    prev_of = lambda i: jnp.maximum(i * hb - 1, 0)
    next_of = lambda i: jnp.minimum((i + 1) * hb, last)
    return pl.pallas_call(
        body, name=name, grid=(N_QKV_GROUPS, g_),
        in_specs=[pl.BlockSpec((tm, QKV_W), lambda j, i: (i, QKV_B0 + j)),
                  pl.BlockSpec((HALO, QKV_W), lambda j, i: (prev_of(i), QKV_B0 + j)),
                  pl.BlockSpec((HALO, QKV_W), lambda j, i: (next_of(i), QKV_B0 + j)),
                  pl.BlockSpec((8, QKV_W), lambda j, i: (0, j)),
                  pl.BlockSpec((tm, QKV_W), lambda j, i: (i, j)),
                  pl.BlockSpec((HALO, QKV_W), lambda j, i: (next_of(i), j)), _ANY],
        out_specs=[pl.BlockSpec((tm, QKV_W), lambda j, i: (i, QKV_B0 + j)), pl.BlockSpec((8, QKV_W), lambda j, i: (0, j))],
        out_shape=[jax.ShapeDtypeStruct(d_proj.shape, d_proj.dtype),
                   jax.ShapeDtypeStruct((8, N_QKV_GROUPS * QKV_W), F32)],
        input_output_aliases={6: 0},
        scratch_shapes=[pltpu.VMEM((HALO + ext, QKV_W), F32), pltpu.VMEM((ext, QKV_W), F32), pltpu.VMEM((ext, QKV_W), F32)],
        compiler_params=_params("parallel", "arbitrary"),
    )(proj, proj, proj, conv_w8, dact, dact, d_proj)


def _round_robin(gens):
    gens = list(gens)
    while gens:
        alive = []
        for gen in gens:
            try:
                next(gen)
                alive.append(gen)
            except StopIteration:
                pass
        gens = alive


def _unit_lower_inverse(a_low, eye):
    n = a_low.shape[0]
    ri = lax.broadcasted_iota(jnp.int32, (n, n), 0)
    ci = lax.broadcasted_iota(jnp.int32, (n, n), 1)
    same = lambda shift: (ri >> shift) == (ci >> shift)
    b = jnp.where(same(3), -a_low, 0.0)
    x = eye + b
    p2 = _dot3(b, b)
    yield
    x = x + _dot3(x, p2)
    p4 = _dot3(p2, p2)
    yield
    x = x + _dot3(x, p4)
    yield
    for shift in (3, 4, 5):
        between = jnp.where(same(shift + 1) & ~same(shift), a_low, 0.0)
        t = _dot3(between, x)
        yield
        x = x - _dot3(x, t)
        yield
    return x


class _GdnChunk:
    def build(self, q, k, v, gb, h):
        C = GDN_CHUNK
        lane = lax.broadcasted_iota(jnp.int32, (C, SM_W), 1)
        g = jnp.sum(jnp.where(lane == h, gb, 0.0), axis=1, keepdims=True)
        self.beta = jnp.sum(jnp.where(lane == h + GDN_HEADS, gb, 0.0), axis=1, keepdims=True)
        ri = lax.broadcasted_iota(jnp.int32, (C, C), 0)
        ci = lax.broadcasted_iota(jnp.int32, (C, C), 1)
        self.causal = ri >= ci
        self.strict = ri > ci
        self.eye = (ri == ci).astype(F32)
        gcb = _dotx(self.causal.astype(F32), jnp.broadcast_to(g, (C, SM_W)))
        yield
        self.gcol = gcb[:, 0:1]
        grow = gcb.T[0:1, 0:C]
        self.decay = jnp.exp(jnp.where(self.causal, self.gcol - grow, -1e30))
        self.egc = jnp.exp(self.gcol)
        glast = gcb[C - 1:C, 0:1]
        self.elast = jnp.exp(glast - self.gcol)
        self.gl = jnp.exp(glast)
        self.q, self.k, self.v = q, k, v
        self.kb = k * self.beta
        m = _dot(self.kb, k, NT)
        n_ = _dot(q, k, NT)
        yield
        self.a_low = jnp.where(self.strict, m * self.decay, 0.0)
        self.p = n_ * self.decay
        self.qd = q * self.egc
        self.kd = k * self.elast
        self.bu = v * self.beta
        self.bw = self.kb * self.egc


GDN_HB = 8
GDN_HG = GDN_HEADS // GDN_HB


def _gdn_specs(n_of):
    C, W = GDN_CHUNK, 128 * GDN_HB
    q_spec = pl.BlockSpec((C, W), lambda g, n: (n_of(n), g))
    k_spec = pl.BlockSpec((C, W), lambda g, n: (n_of(n), g + GDN_HG))
    v_spec = pl.BlockSpec((C, W), lambda g, n: (n_of(n), g + 2 * GDN_HG))
    gb_spec = pl.BlockSpec((C, SM_W), lambda g, n: (n_of(n), 0))
    o_spec = pl.BlockSpec((C, W), lambda g, n: (n_of(n), g))
    s_spec = pl.BlockSpec((GDN_HB, None, GDN_DK, GDN_DV), lambda g, n: (g, n_of(n), 0, 0))
    t_spec = pl.BlockSpec((GDN_HB, None, C, C), lambda g, n: (g, n_of(n), 0, 0))
    return q_spec, k_spec, v_spec, gb_spec, o_spec, s_spec, t_spec


def _gdn_fwd(act, gb, *, name):
    M = act.shape[0]
    N = M // GDN_CHUNK

    def body(q_ref, k_ref, v_ref, gb_ref, o_ref, s_ref, t_ref, state):
        g, n = pl.program_id(0), pl.program_id(1)

        @pl.when(n == 0)
        def _():
            state[...] = jnp.zeros_like(state)

        gb_ = gb_ref[...]

        def head(hh):
            cols = slice(hh * 128, (hh + 1) * 128)
            c = _GdnChunk()
            yield from c.build(q_ref[:, cols], k_ref[:, cols], v_ref[:, cols], gb_, g * GDN_HB + hh)
            tinv = yield from _unit_lower_inverse(c.a_low, c.eye)
            s = state[hh]
            s_ref[hh] = s
            t_ref[hh] = tinv
            u = _dot(tinv, c.bu)
            w = _dot(tinv, c.bw)
            yield
            vn = u - _dot(w, s)
            o1 = _dot(c.qd, s)
            yield
            o_ref[:, cols] = (o1 + _dot(c.p, vn)).astype(o_ref.dtype)
            state[hh] = c.gl * s + _dot(c.kd, vn, TN)

        _round_robin(head(hh) for hh in range(GDN_HB))

    q_spec, k_spec, v_spec, gb_spec, o_spec, s_spec, t_spec = _gdn_specs(lambda n: n)
    return pl.pallas_call(
        body, name=name, grid=(GDN_HG, N),
        in_specs=[q_spec, k_spec, v_spec, gb_spec], out_specs=[o_spec, s_spec, t_spec],
        out_shape=[jax.ShapeDtypeStruct((M, GDN_V), BF16),
                   jax.ShapeDtypeStruct((GDN_HEADS, N, GDN_DK, GDN_DV), F32),
                   jax.ShapeDtypeStruct((GDN_HEADS, N, GDN_CHUNK, GDN_CHUNK), F32)],
        scratch_shapes=[pltpu.VMEM((GDN_HB, GDN_DK, GDN_DV), F32)],
        compiler_params=_params("parallel", "arbitrary"),
    )(act, act, act, gb)


def _gdn_bwd(act, gb, do, s_all, t_all, *, name):
    M = act.shape[0]
    N = M // GDN_CHUNK
    C = GDN_CHUNK
    assert GDN_HG == 1

    def body(q_ref, k_ref, v_ref, gb_ref, do_ref, s_ref, t_ref, dact_ref, dgb_ref, dstate):
        g, n = pl.program_id(0), pl.program_id(1)

        @pl.when(n == 0)
        def _():
            dstate[...] = jnp.zeros_like(dstate)

        gb_ = gb_ref[...]
        last = lax.broadcasted_iota(jnp.int32, (C, 1), 0) == C - 1
        upper = (lax.broadcasted_iota(jnp.int32, (C, C), 0) <= lax.broadcasted_iota(jnp.int32, (C, C), 1)).astype(F32)
        lane = lax.broadcasted_iota(jnp.int32, (C, SM_W), 1)
        def head(hh):
            cols = slice(hh * 128, (hh + 1) * 128)
            h = g * GDN_HB + hh
            c = _GdnChunk()
            yield from c.build(q_ref[:, cols], k_ref[:, cols], v_ref[:, cols], gb_, h)
            tinv = t_ref[hh]
            tinv_t = tinv.T
            s = s_ref[hh]
            do_ = do_ref[:, cols]
            ds1 = dstate[hh]
            u = _dot(tinv, c.bu)
            w = _dot(tinv, c.bw)
            dqd = _dot(do_, s, NT)
            dvn0 = _dot(c.p, do_, TN) + _dot(c.kd, ds1)
            dst0 = _dot(c.qd, do_, TN) + c.gl * ds1
            yield
            vn = u - _dot(w, s)
            dvn = dvn0
            yield
            dp = jnp.where(c.causal, _dot(do_, vn, NT), 0.0)
            dstate[hh] = dst0 - _dot(w, dvn, TN)
            dkd = _dot(vn, ds1, NT)
            dw = -_dot(dvn, s, NT)
            dbu = _dot(tinv_t, dvn)
            dgl = jnp.sum(jnp.sum(s * ds1, axis=1, keepdims=True), axis=0, keepdims=True)
            yield
            dbw = _dot(tinv_t, dw)
            t1 = _dot(dbu, u, NT)
            yield
            da = jnp.where(c.strict, -(t1 + _dot(dbw, w, NT)), 0.0)
            dn_ = dp * c.decay
            dq0 = _dot(dn_, c.k)
            dk0 = _dot(dn_, c.q, TN)
            yield
            dm = da * c.decay
            e = da * c.a_low + dp * c.p
            dkb = _dot(dm, c.k) + dbw * c.egc
            dact_ref[:, GDN_QK + hh * 128:GDN_QK + (hh + 1) * 128] = (
                _dot(dm, c.kb, TN) + dk0 + dkb * c.beta + dkd * c.elast)
            dact_ref[:, cols] = dq0 + dqd * c.egc
            dact_ref[:, 2 * GDN_QK + hh * 128:2 * GDN_QK + (hh + 1) * 128] = dbu * c.beta
            dbeta = jnp.sum(dbu * c.v, axis=1, keepdims=True) + jnp.sum(dkb * c.k, axis=1, keepdims=True)
            t_kd = jnp.sum(dkd * c.kd, axis=1, keepdims=True)
            dgc = (jnp.sum(e, axis=1, keepdims=True) - jnp.sum(e.T, axis=1, keepdims=True)
                   + jnp.sum(dbw * c.bw, axis=1, keepdims=True) + jnp.sum(dqd * c.qd, axis=1, keepdims=True) - t_kd)
            dgc = dgc + jnp.where(last, jnp.sum(t_kd, axis=0, keepdims=True) + dgl * c.gl, 0.0)
            yield
            dg = _dotx(upper, jnp.broadcast_to(dgc, (C, SM_W)))
            dgb_ref[hh] = jnp.where(lane == h, dg, jnp.where(lane == h + GDN_HEADS, dbeta, 0.0))

        _round_robin(head(hh) for hh in range(GDN_HB))

    rev = lambda n: N - 1 - n
    q_spec, k_spec, v_spec, gb_spec, o_spec, s_spec, t_spec = _gdn_specs(rev)
    dgb_spec = pl.BlockSpec((GDN_HB, C, SM_W), lambda g, n: (g, rev(n), 0))
    return pl.pallas_call(
        body, name=name, grid=(GDN_HG, N),
        in_specs=[q_spec, k_spec, v_spec, gb_spec, o_spec, s_spec, t_spec],
        out_specs=[pl.BlockSpec((C, 2 * GDN_QK + GDN_V), lambda g, n: (rev(n), 0)), dgb_spec],
        out_shape=[jax.ShapeDtypeStruct((M, 2 * GDN_QK + GDN_V), F32),
                   jax.ShapeDtypeStruct((GDN_HEADS, M, SM_W), F32)],
        scratch_shapes=[pltpu.VMEM((GDN_HB, GDN_DK, GDN_DV), F32)],
        compiler_params=_params("parallel", "arbitrary"),
    )(act, act, act, gb, do, s_all, t_all)


GLA_STEP_ROWS = 64
GLA_SUB = GLA_STEP_ROWS // GLA_CHUNK


def _gla_cumsum(la):
    C = GLA_CHUNK
    ltri = (lax.broadcasted_iota(jnp.int32, (C, C), 0) >= lax.broadcasted_iota(jnp.int32, (C, C), 1)).astype(F32)
    return _dotx(ltri, la)


GLA_HALF = GLA_CHUNK // 2


def _gla_cross_factors(b):
    top = lax.broadcasted_iota(jnp.int32, b.shape, 0) < GLA_HALF
    bm = b[GLA_HALF - 1:GLA_HALF, :]
    late = jnp.where(top, 0.0, jnp.exp(jnp.minimum(b - bm, 0.0)))
    early = jnp.where(top, jnp.exp(jnp.minimum(bm - b, 0.0)), 0.0)
    return late, early


def _gla_half_decay(bh, ii):
    rj = lax.broadcasted_iota(jnp.int32, bh.shape, 0)
    return jnp.where(rj <= ii, jnp.exp(jnp.minimum(bh[ii:ii + 1, :] - bh, 0.0)), 0.0)


def _gla_scores_t(q, k, b):
    C, H = GLA_CHUNK, GLA_HALF
    lane = lax.broadcasted_iota(jnp.int32, (H, C), 1)
    halves = []
    for h0 in (0, H):
        qh, kh, bh = q[h0:h0 + H], k[h0:h0 + H], b[h0:h0 + H]
        sth = jnp.zeros((H, C), F32)
        for ii in range(H):
            si = jnp.sum(qh[ii:ii + 1, :] * kh * _gla_half_decay(bh, ii), axis=1, keepdims=True)
            sth = jnp.where(lane == h0 + ii, si, sth)
            if ii % 4 == 3:
                yield
        halves.append(sth)
    late, early = _gla_cross_factors(b)
    between = _dot(k * early, q * late, NT)
    yield
    return jnp.concatenate(halves, axis=0) + between


def _gla_specs(n_of):
    R = GLA_STEP_ROWS
    q_spec = pl.BlockSpec((R, GLA_QK), lambda n: (n_of(n), C_GQ // GLA_QK))
    k_spec = pl.BlockSpec((R, GLA_QK), lambda n: (n_of(n), C_GK // GLA_QK))
    v_spec = pl.BlockSpec((R, GLA_V), lambda n: (n_of(n), C_GV // GLA_V))
    la_spec = pl.BlockSpec((R, GLA_QK), lambda n: (n_of(n), 0))
    o_spec = pl.BlockSpec((R, GLA_V), lambda n: (n_of(n), 0))
    s_spec = pl.BlockSpec((GLA_HEADS, None, GLA_SUB, GLA_DV, GLA_DK), lambda n: (0, n_of(n), 0, 0, 0))
    return q_spec, k_spec, v_spec, la_spec, o_spec, s_spec


def _gla_fwd(proj, la, *, name):
    M = proj.shape[0]
    N = M // GLA_STEP_ROWS
    C = GLA_CHUNK

    def body(q_ref, k_ref, v_ref, la_ref, o_ref, s_ref, state):
        n = pl.program_id(0)

        @pl.when(n == 0)
        def _():
            state[...] = jnp.zeros_like(state)

        local = {}

        def within(hh, c):
            kc = slice(hh * GLA_DK, (hh + 1) * GLA_DK)
            vc = slice(hh * GLA_DV, (hh + 1) * GLA_DV)
            rows = slice(c * C, (c + 1) * C)
            q = q_ref[rows, kc] * (GLA_DK ** -0.5)
            k = k_ref[rows, kc]
            v = v_ref[rows, vc]
            b = _gla_cumsum(la_ref[rows, kc])
            yield
            blast = b[C - 1:C, :]
            sc_t = yield from _gla_scores_t(q, k, b)
            kv = _dot(v, k * jnp.exp(blast - b), TN)
            o2 = _dot(sc_t, v, TN)
            yield
            local[hh, c] = (q * jnp.exp(b), jnp.exp(blast), kv, o2)

        def across(hh):
            vc = slice(hh * GLA_DV, (hh + 1) * GLA_DV)
            st = state[hh]
            for c in range(GLA_SUB):
                qe, eblast, kv, o2 = local[hh, c]
                s_ref[hh, c] = st
                o1 = _dot(qe, st, NT)
                yield
                o_ref[c * C:(c + 1) * C, vc] = (o1 + o2).astype(o_ref.dtype)
                st = st * eblast + kv
            state[hh] = st

        _round_robin(within(hh, c) for c in range(GLA_SUB) for hh in range(GLA_HEADS))
        _round_robin(across(hh) for hh in range(GLA_HEADS))

    q_spec, k_spec, v_spec, la_spec, o_spec, s_spec = _gla_specs(lambda n: n)
    return pl.pallas_call(
        body, name=name, grid=(N,),
        in_specs=[q_spec, k_spec, v_spec, la_spec], out_specs=[o_spec, s_spec],
        out_shape=[jax.ShapeDtypeStruct((M, GLA_V), BF16),
                   jax.ShapeDtypeStruct((GLA_HEADS, N, GLA_SUB, GLA_DV, GLA_DK), F32)],
        scratch_shapes=[pltpu.VMEM((GLA_HEADS, GLA_DV, GLA_DK), F32)],
        compiler_params=_params("arbitrary"),
    )(proj, proj, proj, la)


def _gla_bwd(proj, la, do, s_all, d_proj, *, name):
    M = proj.shape[0]
    N = M // GLA_STEP_ROWS
    C = GLA_CHUNK
    qkv_w = 2 * GLA_QK + GLA_V
    assert C_GK == C_GQ + GLA_QK and C_GV == C_GK + GLA_QK and C_GQ % qkv_w == 0

    def body(q_ref, k_ref, v_ref, la_ref, do_ref, s_ref, _, dp_ref, dla_ref, dstate):
        n = pl.program_id(0)

        @pl.when(n == 0)
        def _():
            dstate[...] = jnp.zeros_like(dstate)

        H = GLA_HALF
        lane = lax.broadcasted_iota(jnp.int32, (C, C), 1)
        row = lax.broadcasted_iota(jnp.int32, (C, C), 0)
        ri = lax.broadcasted_iota(jnp.int32, (C, GLA_DK), 0)
        lane_h = lax.broadcasted_iota(jnp.int32, (H, C), 1)
        ri_h = lax.broadcasted_iota(jnp.int32, (H, GLA_DK), 0)
        cross = (row < H) & (lane >= H)
        upper = (row <= lane).astype(F32)
        def head(hh):
            kc = slice(hh * GLA_DK, (hh + 1) * GLA_DK)
            vc = slice(hh * GLA_DV, (hh + 1) * GLA_DV)
            ds1 = dstate[hh]
            for c in reversed(range(GLA_SUB)):
                rows = slice(c * C, (c + 1) * C)
                q = q_ref[rows, kc] * (GLA_DK ** -0.5)
                k = k_ref[rows, kc]
                v = v_ref[rows, vc]
                b = _gla_cumsum(la_ref[rows, kc])
                do_ = do_ref[rows, vc]
                st = s_ref[hh, c]
                dsc_t = _dot(v, do_, NT)
                dqe = _dot(do_, st)
                dke = _dot(v, ds1)
                yield
                blast = b[C - 1:C, :]
                eb = jnp.exp(b)
                elast = jnp.exp(blast - b)
                eblast = jnp.exp(blast)
                qe = q * eb
                ke = k * elast
                dv2 = _dot(ke, ds1, NT)
                ds_new = _dot(do_, qe, TN)
                deblast = jnp.sum(st * ds1, axis=0, keepdims=True)
                sc_halves, dq_halves, dk_halves = [], [], []
                for h0 in (0, H):
                    qh, kh, bh, dsch = q[h0:h0 + H], k[h0:h0 + H], b[h0:h0 + H], dsc_t[h0:h0 + H]
                    sch = jnp.zeros((H, C), F32)
                    dqh = jnp.zeros((H, GLA_DK), F32)
                    dkh = jnp.zeros((H, GLA_DK), F32)
                    for ii in range(H):
                        f = _gla_half_decay(bh, ii)
                        kf = kh * f
                        si = jnp.sum(qh[ii:ii + 1, :] * kf, axis=1, keepdims=True)
                        sch = jnp.where(lane_h == h0 + ii, si, sch)
                        dsi = jnp.sum(jnp.where(lane_h == h0 + ii, dsch, 0.0), axis=1, keepdims=True)
                        dqh = jnp.where(ri_h == ii, jnp.sum(dsi * kf, axis=0, keepdims=True), dqh)
                        dkh = dkh + (dsi * f) * qh[ii:ii + 1, :]
                        if ii % 4 == 3:
                            yield
                    sc_halves.append(sch)
                    dq_halves.append(dqh)
                    dk_halves.append(dkh)
                late, early = _gla_cross_factors(b)
                q_late, k_early = q * late, k * early
                dsc_x = jnp.where(cross, dsc_t, 0.0)
                sc_t = jnp.concatenate(sc_halves, axis=0) + _dot(k_early, q_late, NT)
                dq_sc = jnp.concatenate(dq_halves, axis=0) + _dot(dsc_x, k_early, TN) * late
                dk_sc = jnp.concatenate(dk_halves, axis=0) + _dot(dsc_x, q_late) * early
                yield
                dv1 = _dot(sc_t, do_)
                dp_ref[rows, kc] = ((dq_sc + dqe * eb) * (GLA_DK ** -0.5)).astype(dp_ref.dtype)
                dp_ref[rows, GLA_QK + hh * GLA_DK:GLA_QK + (hh + 1) * GLA_DK] = (dk_sc + dke * elast).astype(dp_ref.dtype)
                t_ke = dke * ke
                db = q * dq_sc - k * dk_sc + dqe * qe - t_ke
                db = db + jnp.where(ri == C - 1, jnp.sum(t_ke, axis=0, keepdims=True) + deblast * eblast, 0.0)
                dla = _dotx(upper, db)
                yield
                dp_ref[rows, 2 * GLA_QK + hh * GLA_DV:2 * GLA_QK + (hh + 1) * GLA_DV] = (dv1 + dv2).astype(dp_ref.dtype)
                dla_ref[rows, kc] = dla
                ds1 = ds1 * eblast + ds_new
            dstate[hh] = ds1

        _round_robin(head(hh) for hh in range(GLA_HEADS))

    rev = lambda n: N - 1 - n
    q_spec, k_spec, v_spec, la_spec, o_spec, s_spec = _gla_specs(rev)
    return pl.pallas_call(
        body, name=name, grid=(N,),
        in_specs=[q_spec, k_spec, v_spec, la_spec, o_spec, s_spec, _ANY],
        out_specs=[pl.BlockSpec((GLA_STEP_ROWS, qkv_w), lambda n: (rev(n), C_GQ // qkv_w)), la_spec],
        out_shape=[jax.ShapeDtypeStruct(d_proj.shape, d_proj.dtype), jax.ShapeDtypeStruct((M, GLA_QK), F32)],
        input_output_aliases={6: 0},
        scratch_shapes=[pltpu.VMEM((GLA_HEADS, GLA_DV, GLA_DK), F32)],
        compiler_params=_params("arbitrary"),
    )(proj, proj, proj, la, do, s_all, d_proj)


def _head_norm(o, wn):
    r = lax.rsqrt(jnp.mean(o * o, axis=-1, keepdims=True) + NORM_EPS)
    return o * r, r


def _mix_heads():
    heads = [(0, GDN_DV, hh * GDN_DV, hh * GDN_DV) for hh in range(GDN_HEADS)]
    heads += [(1, GLA_DV, GDN_V + hh * GLA_DV, hh * GLA_DV) for hh in range(GLA_HEADS)]
    return heads


def _mix_fwd(o_gdn, o_gla, proj, wn_gdn, wn_gla, *, name):
    M = proj.shape[0]
    tm = _tile(M, 344, 16)

    def body(og_ref, ol_ref, z_ref, r_ref, wg_ref, wl_ref, m_ref):
        srcs = ((og_ref, z_ref, wg_ref), (ol_ref, r_ref, wl_ref))
        for grp, width, mcol, col in _mix_heads():
            o_ref, gate_ref, w_ref = srcs[grp]
            xhat, _ = _head_norm(o_ref[:, col:col + width].astype(F32), None)
            gate, _ = _silu_and_grad(gate_ref[:, col:col + width])
            m_ref[:, mcol:mcol + width] = (xhat * w_ref[...] * gate).astype(m_ref.dtype)

    full = lambda s: pl.BlockSpec(s, lambda i: (0, 0))
    return pl.pallas_call(
        body, name=name, grid=(M // tm,),
        in_specs=[pl.BlockSpec((tm, GDN_V), lambda i: (i, 0)), pl.BlockSpec((tm, GLA_V), lambda i: (i, 0)),
                  pl.BlockSpec((tm, GDN_V), lambda i: (i, C_Z // GDN_V)),
                  pl.BlockSpec((tm, GLA_V), lambda i: (i, C_GR // GLA_V)),
                  full((1, GDN_DV)), full((1, GLA_DV))],
        out_specs=pl.BlockSpec((tm, D_MODEL), lambda i: (i, 0)),
        out_shape=jax.ShapeDtypeStruct((M, D_MODEL), BF16),
        compiler_params=_params("parallel"),
    )(o_gdn, o_gla, proj, proj, wn_gdn, wn_gla)


def _mix_bwd(o_gdn, o_gla, proj, wn_gdn, wn_gla, dmixed, *, name):
    M = proj.shape[0]
    tm = _tile(M, 344, 16)
    g_ = M // tm
    assert C_Z == 0 and C_GR == GDN_V

    def body(og_ref, ol_ref, z_ref, r_ref, wg_ref, wl_ref, dm_ref,
             dog_ref, dol_ref, dzr_ref, dwg_ref, dwl_ref):
        i = pl.program_id(0)
        srcs = ((og_ref, z_ref, wg_ref, dog_ref), (ol_ref, r_ref, wl_ref, dol_ref))
        dws = [jnp.zeros((1, GDN_DV), F32), jnp.zeros((1, GLA_DV), F32)]
        for grp, width, mcol, col in _mix_heads():
            o_ref, gate_ref, w_ref, do_ref = srcs[grp]
            cols = slice(col, col + width)
            xhat, r = _head_norm(o_ref[:, cols].astype(F32), None)
            gate, dgate_dc = _silu_and_grad(gate_ref[:, cols])
            dm = dm_ref[:, mcol:mcol + width]
            dzr_ref[:, mcol:mcol + width] = (dm * xhat * w_ref[...] * dgate_dc).astype(dzr_ref.dtype)
            dnorm = dm * gate
            dws[grp] = dws[grp] + jnp.sum(dnorm * xhat, axis=0, keepdims=True)
            dxhat = dnorm * w_ref[...]
            do_ref[:, cols] = r * (dxhat - xhat * jnp.mean(dxhat * xhat, axis=-1, keepdims=True))

        @pl.when(i == 0)
        def _():
            dwg_ref[...] = dws[0]
            dwl_ref[...] = dws[1]

        @pl.when(i > 0)
        def _():
            dwg_ref[...] += dws[0]
            dwl_ref[...] += dws[1]

    full = lambda s: pl.BlockSpec(s, lambda i: (0, 0))
    half = pl.BlockSpec((tm, GDN_V), lambda i: (i, 0))
    return pl.pallas_call(
        body, name=name, grid=(g_,),
        in_specs=[half, half, pl.BlockSpec((tm, GDN_V), lambda i: (i, C_Z // GDN_V)),
                  pl.BlockSpec((tm, GLA_V), lambda i: (i, C_GR // GLA_V)),
                  full((1, GDN_DV)), full((1, GLA_DV)), pl.BlockSpec((tm, D_MODEL), lambda i: (i, 0))],
        out_specs=[half, half, pl.BlockSpec((tm, GDN_V + GLA_V), lambda i: (i, 0)),
                   full((1, GDN_DV)), full((1, GLA_DV))],
        out_shape=[jax.ShapeDtypeStruct((M, GDN_V), F32), jax.ShapeDtypeStruct((M, GLA_V), F32),
                   jax.ShapeDtypeStruct((M, D_PROJ), BF16),
                   jax.ShapeDtypeStruct((1, GDN_DV), F32), jax.ShapeDtypeStruct((1, GLA_DV), F32)],
        compiler_params=_params("arbitrary"),
    )(o_gdn, o_gla, proj, proj, wn_gdn, wn_gla, dmixed)


def _row_chunks(tm, parts=2):
    if tm % (16 * parts):
        return [slice(0, tm)]
    return [slice(p * (tm // parts), (p + 1) * (tm // parts)) for p in range(parts)]


def _swiglu_fwd(n, w_gate_t, w_up_t, *, name, tm=1376, tn=512):
    M, D = n.shape
    F = w_gate_t.shape[0]
    tm, tn = _tile(M, tm, 16), _tile(F, tn, 128)

    def body(n_ref, wg_ref, wu_ref, g_ref, u_ref, a_ref):
        wg, wu = wg_ref[...], wu_ref[...]
        for rows in _row_chunks(tm):
            x = n_ref[rows, :]
            g = _dot(x, wg, NT)
            u = _dot(x, wu, NT)
            s, _ = _silu_and_grad(g)
            g_ref[rows, :] = g.astype(g_ref.dtype)
            u_ref[rows, :] = u.astype(u_ref.dtype)
            a_ref[rows, :] = (s * u).astype(a_ref.dtype)

    w_spec = pl.BlockSpec((tn, D), lambda i, j: (j, 0))
    o_spec = pl.BlockSpec((tm, tn), lambda i, j: (i, j))
    return pl.pallas_call(
        body, name=name, grid=(M // tm, F // tn),
        in_specs=[pl.BlockSpec((tm, D), lambda i, j: (i, 0)), w_spec, w_spec], out_specs=[o_spec] * 3,
        out_shape=[jax.ShapeDtypeStruct((M, F), BF16)] * 3, compiler_params=_params("parallel", "parallel"),
    )(n, w_gate_t, w_up_t)


def _swiglu_bwd(dh, w_down, gate, up, *, name, after=None, tm=1376, tn=512):
    M, D = dh.shape
    F = w_down.shape[0]
    tm, tn = _tile(M, tm, 16), _tile(F, tn, 128)
    n_after = 0 if after is None else 1

    def body(*refs):
        dh_ref, w_ref, g_ref, u_ref, dg_ref, du_ref = refs[n_after:]
        w = w_ref[...]
        for rows in _row_chunks(tm):
            da = _dot(dh_ref[rows, :], w, NT)
            s, ds = _silu_and_grad(g_ref[rows, :].astype(F32))
            dg_ref[rows, :] = (da * u_ref[rows, :].astype(F32) * ds).astype(dg_ref.dtype)
            du_ref[rows, :] = (da * s).astype(du_ref.dtype)

    o_spec = pl.BlockSpec((tm, tn), lambda i, j: (i, j))
    return pl.pallas_call(
        body, name=name, grid=(M // tm, F // tn),
        in_specs=[_ANY] * n_after + [pl.BlockSpec((tm, D), lambda i, j: (i, 0)),
                                     pl.BlockSpec((tn, D), lambda i, j: (j, 0)), o_spec, o_spec],
        out_specs=[o_spec, o_spec], out_shape=[jax.ShapeDtypeStruct((M, F), BF16)] * 2,
        compiler_params=_params("parallel", "parallel"),
    )(*((after,) if n_after else ()), dh, w_down, gate, up)


def _adamw_update(w, g, m, v):
    nm = ADAM_B1 * m + (1.0 - ADAM_B1) * g
    nv = ADAM_B2 * v + (1.0 - ADAM_B2) * (g * g)
    m_hat = nm / (1.0 - ADAM_B1 ** ADAM_STEP)
    v_hat = nv / (1.0 - ADAM_B2 ** ADAM_STEP)
    return -ADAM_LR * (m_hat / (jnp.sqrt(v_hat) + ADAM_EPS) + ADAM_WD * w), nm, nv


def _adamw(w, g, m, v, *, name):
    shape = w.shape
    cols = shape[-1]
    rows = w.size // cols
    w2, g2, m2, v2 = (t.reshape(rows, cols) for t in (w, g, m, v))
    if rows % 8 == 0 or cols % 128 != 0:
        tr, tc = (_tile(rows, 256, 8) if rows % 8 == 0 else rows), cols
    else:
        tr, tc = rows, _tile(cols, 256, 128)

    def body(w_ref, g_ref, m_ref, v_ref, d_ref, nm_ref, nv_ref):
        d_ref[...], nm_ref[...], nv_ref[...] = _adamw_update(w_ref[...], g_ref[...], m_ref[...], v_ref[...])

    blk = pl.BlockSpec((tr, tc), lambda i, j: (i, j))
    outs = pl.pallas_call(
        body, name=name, grid=(rows // tr, cols // tc), in_specs=[blk] * 4, out_specs=[blk] * 3,
        out_shape=[jax.ShapeDtypeStruct((rows, cols), F32)] * 3, compiler_params=_params("parallel", "parallel"),
    )(w2, g2, m2, v2)
    return tuple(t.reshape(shape) for t in outs)


def _sum_slabs(x, *, name):
    _, R, C = x.shape
    sub = 16 if x.dtype == BF16 else 8
    if R % sub == 0:
        tr, tc = _tile(R, 128, sub), C
    else:
        tr, tc = R, _tile(C, 256, 128)

    def body(x_ref, o_ref):
        acc = x_ref[0].astype(F32)
        for s in range(1, N_DEV):
            acc = acc + x_ref[s].astype(F32)
        o_ref[...] = acc

    return pl.pallas_call(
        body, name=name, grid=(R // tr, C // tc),
        in_specs=[pl.BlockSpec((N_DEV, tr, tc), lambda i, j: (0, i, j))],
        out_specs=pl.BlockSpec((tr, tc), lambda i, j: (i, j)),
        out_shape=jax.ShapeDtypeStruct((R, C), F32), compiler_params=_params("parallel", "parallel"),
    )(x)


def _sum_adamw(x, w, m, v, *, name):
    _, R, C = x.shape
    if R % 16 == 0:
        tr, tc = _tile(R, 128, 16), C
    else:
        tr, tc = R, _tile(C, 256, 128)

    def body(x_ref, w_ref, m_ref, v_ref, g_ref, d_ref, nm_ref, nv_ref):
        g = x_ref[0].astype(F32)
        for s in range(1, N_DEV):
            g = g + x_ref[s].astype(F32)
        g_ref[...] = g
        d_ref[...], nm_ref[...], nv_ref[...] = _adamw_update(w_ref[...], g, m_ref[...], v_ref[...])

    blk = pl.BlockSpec((tr, tc), lambda i, j: (i, j))
    return pl.pallas_call(
        body, name=name, grid=(R // tr, C // tc),
        in_specs=[pl.BlockSpec((N_DEV, tr, tc), lambda i, j: (0, i, j)), blk, blk, blk], out_specs=[blk] * 4,
        out_shape=[jax.ShapeDtypeStruct((R, C), F32)] * 4, compiler_params=_params("parallel", "parallel"),
    )(x, w, m, v)


def _peers():
    x, y, c = lax.axis_index("x"), lax.axis_index("y"), lax.axis_index("c")
    me = 4 * x + 2 * y + c
    peers = []
    for k in range(1, N_DEV):
        px = 1 - x if k & 4 else x
        py = 1 - y if k & 2 else y
        pc = 1 - c if k & 1 else c
        peers.append(((px, py, pc), 4 * px + 2 * py + pc))
    return me, peers


def _gather(x, *, name):
    def body(x_ref, o_ref, send_sems, recv_sems, own_sem):
        me, peers = _peers()
        own = pltpu.make_async_copy(x_ref, o_ref.at[me], own_sem)
        own.start()
        sends, recvs = [], []
        for k, (pos, idx) in enumerate(peers):
            sends.append(pltpu.make_async_remote_copy(
                src_ref=x_ref, dst_ref=o_ref.at[me], send_sem=send_sems.at[k], recv_sem=recv_sems.at[k],
                device_id=pos, device_id_type=pl.DeviceIdType.MESH))
            recvs.append(pltpu.make_async_remote_copy(
                src_ref=x_ref, dst_ref=o_ref.at[idx], send_sem=send_sems.at[k], recv_sem=recv_sems.at[k],
                device_id=pos, device_id_type=pl.DeviceIdType.MESH))
        for cp in sends:
            cp.start()
        for cp in recvs:
            cp.wait_recv()
        for cp in sends:
            cp.wait_send()
        own.wait()

    hbm = pl.BlockSpec(memory_space=pltpu.HBM)
    return pl.pallas_call(
        body, name=name, in_specs=[hbm], out_specs=hbm,
        out_shape=jax.ShapeDtypeStruct((N_DEV,) + tuple(x.shape), x.dtype),
        scratch_shapes=[pltpu.SemaphoreType.DMA((N_DEV - 1,)), pltpu.SemaphoreType.DMA((N_DEV - 1,)),
                        pltpu.SemaphoreType.DMA],
    )(x)


_HBM = pl.BlockSpec(memory_space=pltpu.HBM)
_SEM = pl.BlockSpec(memory_space=pltpu.SEMAPHORE)
_EFFECT = pltpu.SideEffectType.DATAFLOW_SIDE_EFFECTING


PLAN_GATHER = tuple((k, "x", 0) for k in range(1, N_DEV))
PLAN_SCATTER = tuple((k, "xk", 0) for k in range(1, N_DEV))
PLAN_GATHER_CHIPS = tuple((k, "x", 0) for k in (1, 2, 4, 6))
PLAN_GATHER_PASS_ON = tuple((1, ("land", q), q) for q in (2, 4, 6))


def _plan_refs(plan, j, x_ref, land_ref, me, peers, receiving):
    k, source, r = plan[j]
    index_of = lambda q: me if q == 0 else peers[q - 1][1]
    pos, target = peers[k - 1]
    if source == "x":
        src = x_ref
    elif source == "xk":
        src = x_ref.at[target]
    else:
        src = land_ref.at[index_of(source[1])]
    return pos, src, land_ref.at[index_of(k ^ r) if receiving else index_of(r)]


def _exchange_start(x, *, plan, name, after=None, land=None, slab=None):
    n_after = 0 if after is None else 1
    n = len(plan)

    def body(*refs):
        x_ref, land_ref, send_sems, recv_sems, _, _, token = refs[n_after:]
        me, peers = _peers()
        for j in range(n):
            pos, src, dst = _plan_refs(plan, j, x_ref, land_ref, me, peers, receiving=False)
            pltpu.make_async_remote_copy(src_ref=src, dst_ref=dst, send_sem=send_sems.at[j], recv_sem=recv_sems.at[j],
                                         device_id=pos, device_id_type=pl.DeviceIdType.MESH).start()
        token[...] = jnp.zeros_like(token)

    if land is None:
        land = lax.empty((N_DEV,) + tuple(slab), x.dtype)
    return pl.pallas_call(
        body, name=name,
        out_shape=(pltpu.SemaphoreType.DMA((n,)), pltpu.SemaphoreType.DMA((n,)),
                   pltpu.HBM(x.shape, x.dtype), pltpu.HBM(land.shape, land.dtype), jax.ShapeDtypeStruct((8, 128), F32)),
        in_specs=[_ANY] * n_after + [_HBM, _HBM],
        out_specs=(_SEM, _SEM, _HBM, _HBM, pl.BlockSpec(memory_space=pltpu.VMEM)),
        input_output_aliases={n_after: 2, n_after + 1: 3},
        compiler_params=pltpu.CompilerParams(has_side_effects=_EFFECT),
    )(*((after,) if n_after else ()), pltpu.with_memory_space_constraint(x, pltpu.HBM),
      pltpu.with_memory_space_constraint(land, pltpu.HBM))


def _exchange_wait(handle, after, *, plan, name):
    send_sems, recv_sems, x_thru, land_thru, _ = handle
    afters = list(after) if isinstance(after, (list, tuple)) else [after]

    def body(x_ref, land_ref, send_sems, recv_sems, *rest):
        me, peers = _peers()
        for j in range(len(plan)):
            pos, src, dst = _plan_refs(plan, j, x_ref, land_ref, me, peers, receiving=True)
            cp = pltpu.make_async_remote_copy(src_ref=src, dst_ref=dst, send_sem=send_sems.at[j], recv_sem=recv_sems.at[j],
                                              device_id=pos, device_id_type=pl.DeviceIdType.MESH)
            cp.wait_send()
            cp.wait_recv()

    return pl.pallas_call(
        body, name=name,
        out_shape=(pltpu.HBM(x_thru.shape, x_thru.dtype), pltpu.HBM(land_thru.shape, land_thru.dtype)),
        in_specs=[_HBM, _HBM, _SEM, _SEM] + [_ANY] * len(afters), out_specs=(_HBM, _HBM),
        input_output_aliases={0: 0, 1: 1}, compiler_params=pltpu.CompilerParams(has_side_effects=_EFFECT),
    )(x_thru, land_thru, send_sems, recv_sems, *afters)


def _to_proj_rows(t):
    z = jnp.zeros((D_PROJ - C_SM - 2 * GDN_HEADS - GLA_RANK,) + t.shape[1:], t.dtype)
    return jnp.concatenate([t[R_Z:R_A], t[R_GR:R_LR], t[R_GQ:R_GR], t[:R_Z], t[R_A:R_GQ], t[R_LR:], z], axis=0)


def _from_proj_rows(t):
    ab = C_SM + 2 * GDN_HEADS
    return jnp.concatenate([t[C_QKV:C_SM], t[C_Z:C_GR], t[C_SM:ab], t[C_GQ:C_QKV], t[C_GR:C_GQ],
                            t[ab:ab + GLA_RANK]], axis=0)


def _local_step(x, target, meta, attn_nw, conv_w, a_log, dt_bias, gdn_nw, w2, b2, gla_nw, ffn_nw, final_nw,
                fetch, emit, start=None):
    S = x.shape[0]
    head = jnp.concatenate([jnp.zeros((ROW_PAD, D_MODEL), F32), meta], axis=0)
    conv_w8 = jnp.concatenate([conv_w, jnp.zeros((8 - CONV_K, conv_w.shape[1]), F32)], axis=0)
    w2p = jnp.zeros((SM_W, GLA_QK), F32).at[2 * GDN_HEADS:2 * GDN_HEADS + GLA_RANK].set(w2)
    alog_p = jnp.zeros((1, SM_W), F32).at[:, :GDN_HEADS].set(a_log)
    dt_p = jnp.zeros((1, SM_W), F32).at[:, :GDN_HEADS].set(dt_bias)

    h0, n1 = _embed_norm(head, x, attn_nw, name="attn_norm", after=start)
    w_in_t = fetch("w_in_t", (n1, conv_w8, w2p, alog_p, dt_p))
    proj = _matmul(n1, w_in_t, mode="nt", name="in_proj")
    gb, la = _gates_fwd(proj, w2p, b2, alog_p, dt_p, name="gates")
    act = _prep_fwd(proj, conv_w8, name="gdn_prep")
    o_gdn, s_gdn, t_gdn = _gdn_fwd(act, gb, name="gdn_fwd")
    o_gla, s_gla = _gla_fwd(proj, la, name="gla_fwd")
    mixed = _mix_fwd(o_gdn, o_gla, proj, gdn_nw, gla_nw, name="mix")
    w_out = fetch("w_out", mixed)
    h1 = _matmul(mixed, w_out, mode="nn", add=h0, name="out_proj")
    n2 = _rmsnorm_fwd(h1, ffn_nw, name="ffn_norm")
    w_gate_t, w_up_t = fetch("w_gate_t", n2), fetch("w_up_t", n2)
    gate, up, hid = _swiglu_fwd(n2, w_gate_t, w_up_t, name="swiglu")
    w_down = fetch("w_down", hid)
    h2 = _matmul(hid, w_down, mode="nn", add=h1, name="ffn_down", tm=1376, tn=256)
    dh2, dh2_b, d_final_nw, loss = _loss_head(h2, final_nw, target, name="loss_head")

    wg = dict(mode="tn", out_dtype=BF16, tn=512)
    tok = emit("w_down", _matmul(hid, dh2_b, name="d_w_down", tm=704, **wg))
    d_gate, d_up = _swiglu_bwd(dh2_b, w_down, gate, up, name="d_swiglu", after=tok)
    tok = emit("w_gate_t", _matmul(d_gate, n2, name="d_w_gate", tm=704, **wg))
    tok = emit("w_up_t", _matmul(d_up, n2, name="d_w_up", tm=704, after=tok, **wg))
    d_n2 = _matmul_pair(d_gate, w_gate_t, d_up, w_up_t, name="d_n2", after=tok)
    dh1, dh1_b, d_ffn_nw = _rmsnorm_bwd(h1, ffn_nw, d_n2, dh2, name="d_ffn_norm")

    tok = emit("w_out", _matmul(mixed, dh1_b, name="d_w_out", tm=512, **wg))
    d_mixed = _matmul(dh1_b, w_out, mode="nt", name="d_mixed", after=tok)
    do_gdn, do_gla, d_proj, d_gdn_nw, d_gla_nw = _mix_bwd(o_gdn, o_gla, proj, gdn_nw, gla_nw, d_mixed, name="d_mix")
    d_proj, d_la = _gla_bwd(proj, la, do_gla, s_gla, d_proj, name="gla_bwd")
    dact, dgb_heads = _gdn_bwd(act, gb, do_gdn, s_gdn, t_gdn, name="gdn_bwd")
    d_proj, d_w2p, d_b2, d_alog, d_dt = _gates_bwd(proj, w2p, b2, alog_p, dt_p, dgb_heads, d_la, d_proj, name="d_gates")
    d_proj, d_conv_w8 = _prep_bwd(proj, conv_w8, dact, d_proj, name="d_gdn_prep")
    tok = emit("w_in_t", _matmul(d_proj, n1, name="d_w_in", tm=768, **wg))
    d_n1 = _matmul(d_proj, w_in_t, mode="nn", name="d_n1", tm=688, after=tok)
    grad_x, d_head, d_attn_nw = _embed_norm_bwd(h0, attn_nw, d_n1, dh1, name="d_attn_norm")

    return dict(
        loss=loss[0, 0], grad_x=grad_x, meta=d_head[ROW_PAD:HEAD_ROWS], attn_nw=d_attn_nw,
        conv_w=d_conv_w8[:CONV_K], a_log=d_alog[:, :GDN_HEADS], dt_bias=d_dt[:, :GDN_HEADS], gdn_nw=d_gdn_nw,
        w2=d_w2p[2 * GDN_HEADS:2 * GDN_HEADS + GLA_RANK], b2=d_b2, gla_nw=d_gla_nw, ffn_nw=d_ffn_nw,
        final_nw=d_final_nw)


SMALL_ROWS = 32


def kernel(x, meta_tokens, attn_norm_w, w_in, gdn_conv_w, gdn_a_log, gdn_dt_bias, gdn_norm_w, gla_gate_w2, gla_gate_b, gla_norm_w, w_out, ffn_norm_w, w_gate, w_up, w_down, final_norm_w, loss_target, m_meta_tokens, m_attn_norm_w, m_w_in, m_gdn_conv_w, m_gdn_a_log, m_gdn_dt_bias, m_gdn_norm_w, m_gla_gate_w2, m_gla_gate_b, m_gla_norm_w, m_w_out, m_ffn_norm_w, m_w_gate, m_w_up, m_w_down, m_final_norm_w, v_meta_tokens, v_attn_norm_w, v_w_in, v_gdn_conv_w, v_gdn_a_log, v_gdn_dt_bias, v_gdn_norm_w, v_gla_gate_w2, v_gla_gate_b, v_gla_norm_w, v_w_out, v_ffn_norm_w, v_w_gate, v_w_up, v_w_down, v_final_norm_w):
    me = 4 * lax.axis_index("x") + 2 * lax.axis_index("y") + lax.axis_index("c")

    n_conv = gdn_conv_w.shape[2]
    n_w2 = gla_gate_w2.shape[2]
    n_meta = meta_tokens.shape[1]
    small = jnp.zeros((40, n_conv), F32)
    small = small.at[0:N_META, :n_meta].set(meta_tokens)
    small = small.at[N_META:N_META + CONV_K, :].set(gdn_conv_w[0])
    small = small.at[24:24 + GLA_RANK, :n_w2].set(gla_gate_w2[0])
    small_all = _gather(small, name="gather_small")
    meta_f = small_all[:, 0:N_META, :n_meta].transpose(1, 0, 2).reshape(N_META, D_MODEL)
    conv_f = small_all[:, N_META:N_META + CONV_K, :].transpose(1, 0, 2).reshape(CONV_K, N_DEV * n_conv)
    w2_f = small_all[:, 24:24 + GLA_RANK, :n_w2].transpose(1, 0, 2).reshape(GLA_RANK, N_DEV * n_w2)

    w_in_slab = w_in[0].T.astype(BF16)
    in_h = _exchange_start(w_in_slab, plan=PLAN_GATHER_CHIPS, slab=w_in_slab.shape, name="gather_w_in_start",
                           after=small_all)
    handles, tok = {}, in_h[4]
    for wname, slab in (("w_out", w_out[0]), ("w_gate_t", w_gate[0].T), ("w_up_t", w_up[0].T), ("w_down", w_down[0])):
        slab = slab.astype(BF16)
        handles[wname] = _exchange_start(slab, plan=PLAN_GATHER, slab=slab.shape, name="gather_" + wname + "_start", after=tok)
        tok = handles[wname][4]

    def fetch(name, after):
        if name == "w_in_t":
            own, got = _exchange_wait(in_h, after, plan=PLAN_GATHER_CHIPS, name="gather_w_in_wait")
            pass_h = _exchange_start(own, plan=PLAN_GATHER_PASS_ON, land=got, name="pass_w_in_start")
            own, got = _exchange_wait(pass_h, pass_h[4], plan=PLAN_GATHER_PASS_ON, name="pass_w_in_wait")
            got = lax.dynamic_update_index_in_dim(got, own, me, 0)
            return _to_proj_rows(got.reshape(D_IN, D_MODEL))
        own, got = _exchange_wait(handles[name], after, plan=PLAN_GATHER, name="gather_" + name + "_wait")
        got = lax.dynamic_update_index_in_dim(got, own, me, 0)
        return got.reshape(N_DEV * got.shape[1], D_MODEL)

    sent = {}

    def emit(name, grad):
        if name == "w_in_t":
            grad = _from_proj_rows(grad)
        parts = grad.reshape(N_DEV, grad.shape[0] // N_DEV, D_MODEL)
        sent[name] = _exchange_start(parts, plan=PLAN_SCATTER, slab=parts.shape[1:], name="scatter_" + name + "_start")
        return sent[name][4]

    g = _local_step(x[0], loss_target[0], meta_f, attn_norm_w, conv_f, gdn_a_log, gdn_dt_bias, gdn_norm_w, w2_f,
                    gla_gate_b, gla_norm_w, ffn_norm_w, final_norm_w.reshape(1, D_MODEL), fetch, emit, start=tok)

    misc = jnp.concatenate([g["a_log"], g["dt_bias"], g["gdn_nw"], g["gla_nw"], g["b2"], g["loss"].reshape(1, 1)], axis=1)
    n_misc = misc.shape[1]
    misc = jnp.pad(misc, ((0, 0), (0, D_MODEL - n_misc)))
    rows = jnp.concatenate([g["attn_nw"], g["ffn_nw"], g["final_nw"], misc, g["meta"],
                            g["conv_w"].reshape(-1, D_MODEL), g["w2"].reshape(-1, D_MODEL)], axis=0)
    rows = jnp.pad(rows, ((0, SMALL_ROWS - rows.shape[0]), (0, 0)))
    rows_h = _exchange_start(rows, plan=PLAN_GATHER, slab=rows.shape, name="gather_small_grads_start")

    big = {}
    after = rows_h[4]
    for name, w, m, v, transposed in (("w_down", w_down, m_w_down, v_w_down, False), ("w_gate_t", w_gate, m_w_gate, v_w_gate, True),
                                      ("w_up_t", w_up, m_w_up, v_w_up, True), ("w_out", w_out, m_w_out, v_w_out, False),
                                      ("w_in_t", w_in, m_w_in, v_w_in, True)):
        own, got = _exchange_wait(sent[name], after, plan=PLAN_SCATTER, name="scatter_" + name + "_wait")
        got = lax.dynamic_update_index_in_dim(got, lax.dynamic_index_in_dim(own, me, 0, keepdims=False), me, 0)
        local = [t[0].T if transposed else t[0] for t in (w, m, v)]
        res = _sum_adamw(got, *local, name="adamw_" + name)
        big[name] = [t.T[None] if transposed else t[None] for t in res]
        after = res[0]

    own, got = _exchange_wait(rows_h, after, plan=PLAN_GATHER, name="gather_small_grads_wait")
    tot = _sum_slabs(lax.dynamic_update_index_in_dim(got, own, me, 0), name="sum_small_grads")
    grad_attn_nw, grad_ffn_nw, grad_final_nw = tot[0:1], tot[1:2], tot[2]
    grad_a_log = tot[3:4, 0:8]
    grad_dt = tot[3:4, 8:16]
    grad_gdn_nw = tot[3:4, 16:16 + GDN_DV]
    grad_gla_nw = tot[3:4, 144:144 + GLA_DV]
    grad_b2 = tot[3:4, 400:400 + GLA_QK]
    loss = tot[3, n_misc - 1]
    r0 = 4 + N_META
    grad_meta = lax.dynamic_slice(tot[4:r0], (0, me * n_meta), (N_META, n_meta))
    r1 = r0 + CONV_K * N_DEV * n_conv // D_MODEL
    grad_conv = lax.dynamic_slice(tot[r0:r1].reshape(CONV_K, N_DEV * n_conv), (0, me * n_conv), (CONV_K, n_conv))[None]
    r2 = r1 + GLA_RANK * N_DEV * n_w2 // D_MODEL
    grad_w2 = lax.dynamic_slice(tot[r1:r2].reshape(GLA_RANK, N_DEV * n_w2), (0, me * n_w2), (GLA_RANK, n_w2))[None]

    weights = [meta_tokens, attn_norm_w, w_in, gdn_conv_w, gdn_a_log, gdn_dt_bias, gdn_norm_w, gla_gate_w2,
               gla_gate_b, gla_norm_w, w_out, ffn_norm_w, w_gate, w_up, w_down, final_norm_w]
    grads = [grad_meta, grad_attn_nw, "w_in_t", grad_conv, grad_a_log, grad_dt, grad_gdn_nw, grad_w2,
             grad_b2, grad_gla_nw, "w_out", grad_ffn_nw, "w_gate_t", "w_up_t", "w_down", grad_final_nw]
    ms = [m_meta_tokens, m_attn_norm_w, m_w_in, m_gdn_conv_w, m_gdn_a_log, m_gdn_dt_bias, m_gdn_norm_w,
          m_gla_gate_w2, m_gla_gate_b, m_gla_norm_w, m_w_out, m_ffn_norm_w, m_w_gate, m_w_up, m_w_down, m_final_norm_w]
    vs = [v_meta_tokens, v_attn_norm_w, v_w_in, v_gdn_conv_w, v_gdn_a_log, v_gdn_dt_bias, v_gdn_norm_w,
          v_gla_gate_w2, v_gla_gate_b, v_gla_norm_w, v_w_out, v_ffn_norm_w, v_w_gate, v_w_up, v_w_down, v_final_norm_w]
    outs = [[], [], [], []]
    for idx, (w, gr, m, v) in enumerate(zip(weights, grads, ms, vs)):
        if isinstance(gr, str):
            res = big[gr]
        else:
            gr = gr.reshape(w.shape)
            res = (gr,) + _adamw(w, gr, m, v, name=f"adamw_{idx}")
        for lst, t in zip(outs, res):
            lst.append(t)
    return (loss, g["grad_x"][None], *outs[0], *outs[1], *outs[2], *outs[3])
```

```python
import functools

import jax
import jax.numpy as jnp
from jax import lax
from jax.experimental import pallas as pl
from jax.experimental.pallas import tpu as pltpu

F32 = jnp.float32
BF16 = jnp.bfloat16
_MXU_DTYPE = jnp.bfloat16

D_MODEL = 2048
N_META = 16
ROW_PAD = 48
HEAD_ROWS = ROW_PAD + N_META
CONV_K = 4
GDN_HEADS, GDN_DK, GDN_DV, GDN_CHUNK = 8, 128, 128, 64
GLA_HEADS, GLA_DK, GLA_DV, GLA_CHUNK = 4, 128, 256, 16
GLA_RANK = 16
GLA_GATE_NORMALIZER = 16.0
GDN_QK = GDN_HEADS * GDN_DK
GDN_V = GDN_HEADS * GDN_DV
GLA_QK = GLA_HEADS * GLA_DK
GLA_V = GLA_HEADS * GLA_DV
D_FF = 5632
D_IN = 7200
NORM_EPS = 1e-6
C_Z, C_GR, C_GQ, C_GK, C_GV, C_QKV, C_SM = 0, 1024, 2048, 2560, 3072, 4096, 7168
SM_W = 128
D_PROJ = 7680
R_Z, R_A, R_B, R_GQ, R_GK, R_GV, R_GR, R_LR = 3072, 4096, 4104, 4112, 4624, 5136, 6160, 7184

ADAM_LR, ADAM_B1, ADAM_B2, ADAM_EPS, ADAM_WD, ADAM_STEP = 0.001, 0.9, 0.999, 1e-08, 0.01, 10

N_DEV = 8
VMEM_LIMIT = 56 * 1024 * 1024

NN = (((1,), (0,)), ((), ()))
NT = (((1,), (1,)), ((), ()))
TN = (((0,), (0,)), ((), ()))


def _dot(a, b, dims=NN):
    return lax.dot_general(a.astype(_MXU_DTYPE), b.astype(_MXU_DTYPE), dims, preferred_element_type=F32)


def _running_sum(x, reverse=False):
    n = x.shape[0]
    row = lax.broadcasted_iota(jnp.int32, x.shape, 0)
    s = 1
    while s < n:
        if reverse:
            x = x + jnp.where(row < n - s, pltpu.roll(x, n - s, 0), 0.0)
        else:
            x = x + jnp.where(row >= s, pltpu.roll(x, s, 0), 0.0)
        s *= 2
    return x


def _dot3(a, b):
    ah = a.astype(BF16)
    al = (a - ah.astype(F32)).astype(BF16)
    bh = b.astype(BF16)
    bl = (b - bh.astype(F32)).astype(BF16)
    d = functools.partial(lax.dot_general, dimension_numbers=NN, preferred_element_type=F32)
    return d(ah, bh) + (d(ah, bl) + d(al, bh))


def _tile(n, target, mult=8):
    best = None
    for t in range(mult, min(n, target) + 1, mult):
        if n % t == 0:
            best = t
    return best if best is not None else n


def _params(*sem):
    return pltpu.CompilerParams(dimension_semantics=sem, vmem_limit_bytes=VMEM_LIMIT)


def _sigmoid(x):
    return 0.5 * jnp.tanh(0.5 * x) + 0.5


def _softplus(x):
    return jnp.maximum(x, 0.0) + jnp.log1p(jnp.exp(-jnp.abs(x)))


def _silu_and_grad(c):
    s = _sigmoid(c)
    return c * s, s * (1.0 + c * (1.0 - s))


_ANY = pl.BlockSpec(memory_space=pl.ANY)


def _matmul(a, b, *, mode, name, out_dtype=F32, add=None, after=None, tm=1376, tn=512):
    if mode == "tn":
        K, M = a.shape
        N = b.shape[1]
    else:
        M, K = a.shape
        N = b.shape[0] if mode == "nt" else b.shape[1]
    tm = _tile(M, tm, 128 if mode == "tn" else 16)
    tn = _tile(N, tn, 128)
    dims = {"nn": NN, "nt": NT, "tn": TN}[mode]
    n_after = 0 if after is None else 1

    def body(*refs):
        refs = refs[n_after:]
        r = _dot(refs[0][...], refs[1][...], dims)
        if add is not None:
            r = r + refs[2][...]
        refs[-1][...] = r.astype(out_dtype)

    a_spec = pl.BlockSpec((K, tm), lambda i, j: (0, i)) if mode == "tn" else pl.BlockSpec((tm, K), lambda i, j: (i, 0))
    b_spec = pl.BlockSpec((tn, K), lambda i, j: (j, 0)) if mode == "nt" else pl.BlockSpec((K, tn), lambda i, j: (0, j))
    o_spec = pl.BlockSpec((tm, tn), lambda i, j: (i, j))
    in_specs = [_ANY] * n_after + [a_spec, b_spec] + ([o_spec] if add is not None else [])
    args = ((after,) if n_after else ()) + (a, b) + ((add,) if add is not None else ())
    return pl.pallas_call(
        body, name=name, grid=(M // tm, N // tn), in_specs=in_specs, out_specs=o_spec,
        out_shape=jax.ShapeDtypeStruct((M, N), out_dtype), compiler_params=_params("parallel", "parallel"),
    )(*args)


def _matmul_pair(a1, b1, a2, b2, *, name, after=None, tm=688, tn=256):
    M, K = a1.shape
    N = b1.shape[1]
    tm, tn = _tile(M, tm, 16), _tile(N, tn, 128)
    n_after = 0 if after is None else 1

    def body(*refs):
        a1_ref, b1_ref, a2_ref, b2_ref, o_ref = refs[n_after:]
        o_ref[...] = _dot(a1_ref[...], b1_ref[...]) + _dot(a2_ref[...], b2_ref[...])

    a_spec = pl.BlockSpec((tm, K), lambda i, j: (i, 0))
    b_spec = pl.BlockSpec((K, tn), lambda i, j: (0, j))
    return pl.pallas_call(
        body, name=name, grid=(M // tm, N // tn), in_specs=[_ANY] * n_after + [a_spec, b_spec, a_spec, b_spec],
        out_specs=pl.BlockSpec((tm, tn), lambda i, j: (i, j)), out_shape=jax.ShapeDtypeStruct((M, N), F32),
        compiler_params=_params("parallel", "parallel"),
    )(*((after,) if n_after else ()), a1, b1, a2, b2)


def _rmsnorm_fwd(h, w, *, name):
    M, D = h.shape
    tm = _tile(M, 688, 16)

    def body(h_ref, w_ref, n_ref):
        x = h_ref[...]
        r = lax.rsqrt(jnp.mean(x * x, axis=-1, keepdims=True) + NORM_EPS)
        n_ref[...] = (x * r * w_ref[...]).astype(n_ref.dtype)

    return pl.pallas_call(
        body, name=name, grid=(M // tm,),
        in_specs=[pl.BlockSpec((tm, D), lambda i: (i, 0)), pl.BlockSpec((1, D), lambda i: (0, 0))],
        out_specs=pl.BlockSpec((tm, D), lambda i: (i, 0)),
        out_shape=jax.ShapeDtypeStruct((M, D), BF16),
        compiler_params=_params("parallel"),
    )(h, w)


SEQ_BLOCK = HEAD_ROWS


def _seq_blocks_per_tile(rows):
    n = rows // SEQ_BLOCK
    return max(m for m in (1, 2, 3, 4) if n % m == 0)


def _seq_specs(m, D):
    return [pl.BlockSpec((SEQ_BLOCK, D), functools.partial(lambda i, k: (jnp.maximum(m * i + k - 1, 0), 0), k=k))
            for k in range(m)]


def _embed_norm(head, x, w, *, name, after=None):
    S, D = x.shape
    m = _seq_blocks_per_tile(S + HEAD_ROWS)
    n_after = 0 if after is None else 1

    def body(*refs):
        refs = refs[n_after:]
        head_ref, x_refs, w_ref, h_ref, n_ref = refs[0], refs[1:1 + m], refs[1 + m], refs[2 + m], refs[3 + m]
        i = pl.program_id(0)
        for k in range(m):
            blk = x_refs[k][...]
            if k == 0:
                blk = jnp.where(i == 0, head_ref[...], blk)
            rows = slice(k * SEQ_BLOCK, (k + 1) * SEQ_BLOCK)
            h_ref[rows, :] = blk
            r = lax.rsqrt(jnp.mean(blk * blk, axis=-1, keepdims=True) + NORM_EPS)
            n_ref[rows, :] = (blk * r * w_ref[...]).astype(n_ref.dtype)

    tile = pl.BlockSpec((m * SEQ_BLOCK, D), lambda i: (i, 0))
    return pl.pallas_call(
        body, name=name, grid=((S + HEAD_ROWS) // (m * SEQ_BLOCK),),
        in_specs=[_ANY] * n_after + [pl.BlockSpec((SEQ_BLOCK, D), lambda i: (0, 0))] + _seq_specs(m, D)
        + [pl.BlockSpec((1, D), lambda i: (0, 0))],
        out_specs=[tile, tile],
        out_shape=[jax.ShapeDtypeStruct((S + HEAD_ROWS, D), F32), jax.ShapeDtypeStruct((S + HEAD_ROWS, D), BF16)],
        compiler_params=_params("parallel"),
    )(*((after,) if n_after else ()), head, *([x] * m), w)


def _embed_norm_bwd(h, w, dn, dres, *, name):
    M, D = h.shape
    S = M - HEAD_ROWS
    m = _seq_blocks_per_tile(S)
    g = S // (m * SEQ_BLOCK)

    def one(x, dn_, dres_, w_):
        r = lax.rsqrt(jnp.mean(x * x, axis=-1, keepdims=True) + NORM_EPS)
        xhat = x * r
        dxhat = dn_ * w_
        dh = dres_ + r * (dxhat - xhat * jnp.mean(dxhat * xhat, axis=-1, keepdims=True))
        return dh, jnp.sum((dn_ * xhat).reshape(SEQ_BLOCK // 8, 8, D), axis=0)

    def body(*refs):
        w_ref = refs[0]
        groups = [refs[1 + a * (m + 1):1 + (a + 1) * (m + 1)] for a in range(3)]
        gx_ref, dhead_ref, dw_ref, acc_ref = refs[1 + 3 * (m + 1):]
        i = pl.program_id(0)
        w_ = w_ref[...]
        part = jnp.zeros((8, D), F32)
        for k in range(m):
            dh, p = one(*(grp[1 + k][...] for grp in groups), w_)
            gx_ref[k * SEQ_BLOCK:(k + 1) * SEQ_BLOCK, :] = dh
            part = part + p

        @pl.when(i == 0)
        def _():
            dh, p = one(*(grp[0][...] for grp in groups), w_)
            dhead_ref[...] = dh
            acc_ref[...] = part + p

        @pl.when(i > 0)
        def _():
            acc_ref[...] += part

        @pl.when(i == g - 1)
        def _():
            dw_ref[...] = jnp.sum(acc_ref[...], axis=0, keepdims=True)

    first = pl.BlockSpec((SEQ_BLOCK, D), lambda i: (0, 0))
    blocks = [pl.BlockSpec((SEQ_BLOCK, D), functools.partial(lambda i, k: (m * i + k + 1, 0), k=k)) for k in range(m)]
    vec = pl.BlockSpec((1, D), lambda i: (0, 0))
    return pl.pallas_call(
        body, name=name, grid=(g,), in_specs=[vec] + ([first] + blocks) * 3,
        out_specs=[pl.BlockSpec((m * SEQ_BLOCK, D), lambda i: (i, 0)), first, vec],
        out_shape=[jax.ShapeDtypeStruct((S, D), F32), jax.ShapeDtypeStruct((SEQ_BLOCK, D), F32),
                   jax.ShapeDtypeStruct((1, D), F32)],
        scratch_shapes=[pltpu.VMEM((8, D), F32)],
        compiler_params=_params("arbitrary"),
    )(w, *([h] * (m + 1)), *([dn] * (m + 1)), *([dres] * (m + 1)))


def _rmsnorm_bwd(h, w, dn, dres, *, name):
    M, D = h.shape
    tm = _tile(M, 344, 16)
    g = M // tm

    def body(h_ref, w_ref, dn_ref, dres_ref, dh_ref, dhb_ref, dw_ref, acc_ref):
        i = pl.program_id(0)
        x = h_ref[...]
        r = lax.rsqrt(jnp.mean(x * x, axis=-1, keepdims=True) + NORM_EPS)
        xhat = x * r
        dn_ = dn_ref[...]
        dxhat = dn_ * w_ref[...]
        dh = dres_ref[...] + r * (dxhat - xhat * jnp.mean(dxhat * xhat, axis=-1, keepdims=True))
        dh_ref[...] = dh
        dhb_ref[...] = dh.astype(dhb_ref.dtype)
        part = jnp.sum((dn_ * xhat).reshape(tm // 8, 8, D), axis=0)

        @pl.when(i == 0)
        def _():
            acc_ref[...] = part

        @pl.when(i > 0)
        def _():
            acc_ref[...] += part

        @pl.when(i == g - 1)
        def _():
            dw_ref[...] = jnp.sum(acc_ref[...], axis=0, keepdims=True)

    row = pl.BlockSpec((tm, D), lambda i: (i, 0))
    vec = pl.BlockSpec((1, D), lambda i: (0, 0))
    return pl.pallas_call(
        body, name=name, grid=(g,), in_specs=[row, vec, row, row],
        out_specs=[row, row, vec],
        out_shape=[jax.ShapeDtypeStruct((M, D), F32), jax.ShapeDtypeStruct((M, D), BF16),
                   jax.ShapeDtypeStruct((1, D), F32)],
        scratch_shapes=[pltpu.VMEM((8, D), F32)],
        compiler_params=_params("arbitrary"),
    )(h, w, dn, dres)


def _loss_head(h, w, target, *, name):
    M, D = h.shape
    m = _seq_blocks_per_tile(M)
    tm = m * SEQ_BLOCK
    g = M // tm

    def body(h_ref, w_ref, *rest):
        t_refs = rest[:m]
        dh_ref, dhb_ref, dw_ref, loss_ref, acc_ref, lacc_ref = rest[m:]
        i = pl.program_id(0)
        x = h_ref[...]
        row = i * tm + lax.broadcasted_iota(jnp.int32, (tm, 1), 0)
        live = row >= HEAD_ROWS
        r = lax.rsqrt(jnp.mean(x * x, axis=-1, keepdims=True) + NORM_EPS)
        xhat = x * r
        t = jnp.concatenate([t_ref[...] for t_ref in t_refs], axis=0)
        err = jnp.where(live, xhat * w_ref[...] - t, 0.0)
        dy = err * (1.0 / D)
        dxhat = dy * w_ref[...]
        dh = r * (dxhat - xhat * jnp.mean(dxhat * xhat, axis=-1, keepdims=True))
        dh_ref[...] = dh
        dhb_ref[...] = dh.astype(dhb_ref.dtype)
        part = jnp.sum((dy * xhat).reshape(tm // 8, 8, D), axis=0)
        lpart = jnp.sum((err * err).reshape(tm // 8, 8, D), axis=0)

        @pl.when(i == 0)
        def _():
            acc_ref[...] = part
            lacc_ref[...] = lpart

        @pl.when(i > 0)
        def _():
            acc_ref[...] += part
            lacc_ref[...] += lpart

        @pl.when(i == g - 1)
        def _():
            dw_ref[...] = jnp.sum(acc_ref[...], axis=0, keepdims=True)
            tot = jnp.sum(jnp.sum(lacc_ref[...], axis=0, keepdims=True), axis=1, keepdims=True)
            loss_ref[...] = jnp.broadcast_to(tot * (0.5 / D), (1, 128))

    row = pl.BlockSpec((tm, D), lambda i: (i, 0))
    vec = pl.BlockSpec((1, D), lambda i: (0, 0))
    return pl.pallas_call(
        body, name=name, grid=(g,), in_specs=[row, vec] + _seq_specs(m, D),
        out_specs=[row, row, vec, pl.BlockSpec((1, 128), lambda i: (0, 0))],
        out_shape=[jax.ShapeDtypeStruct((M, D), F32), jax.ShapeDtypeStruct((M, D), BF16),
                   jax.ShapeDtypeStruct((1, D), F32), jax.ShapeDtypeStruct((1, 128), F32)],
        scratch_shapes=[pltpu.VMEM((8, D), F32), pltpu.VMEM((8, D), F32)],
        compiler_params=_params("arbitrary"),
    )(h, w, *([target] * m))


def _gate_terms(sm, w2p, b2, alog_p, dt_p, row0):
    tm = sm.shape[0]
    lane = lax.broadcasted_iota(jnp.int32, (tm, SM_W), 1)
    live = (row0 + lax.broadcasted_iota(jnp.int32, (tm, 1), 0)) >= ROW_PAD
    pre = sm + dt_p
    neg_a = -jnp.exp(alog_p)
    g = neg_a * _softplus(pre)
    beta = _sigmoid(sm)
    z = _dot(sm, w2p) + b2
    return lane, live, pre, neg_a, g, beta, z


def _gates_fwd(proj, w2p, b2, alog_p, dt_p, *, name):
    M = proj.shape[0]
    tm = _tile(M, 688, 8)

    def body(sm_ref, w2_ref, b2_ref, al_ref, dt_ref, gb_ref, la_ref):
        row0 = pl.program_id(0) * tm
        lane, live, _, _, g, beta, z = _gate_terms(sm_ref[...], w2_ref[...], b2_ref[...], al_ref[...], dt_ref[...], row0)
        gb = jnp.where(lane < GDN_HEADS, g, jnp.where(lane < 2 * GDN_HEADS, beta, 0.0))
        gb_ref[...] = jnp.where(live, gb, 0.0)
        la = (jnp.minimum(z, 0.0) - jnp.log1p(jnp.exp(-jnp.abs(z)))) * (1.0 / GLA_GATE_NORMALIZER)
        la_ref[...] = jnp.where(live, la, 0.0)

    full = lambda s: pl.BlockSpec(s, lambda i: (0, 0))
    return pl.pallas_call(
        body, name=name, grid=(M // tm,),
        in_specs=[pl.BlockSpec((tm, SM_W), lambda i: (i, C_SM // SM_W)), full((SM_W, GLA_QK)), full((1, GLA_QK)),
                  full((1, SM_W)), full((1, SM_W))],
        out_specs=[pl.BlockSpec((tm, SM_W), lambda i: (i, 0)), pl.BlockSpec((tm, GLA_QK), lambda i: (i, 0))],
        out_shape=[jax.ShapeDtypeStruct((M, SM_W), F32), jax.ShapeDtypeStruct((M, GLA_QK), F32)],
        compiler_params=_params("parallel"),
    )(proj, w2p, b2, alog_p, dt_p)


def _gates_bwd(proj, w2p, b2, alog_p, dt_p, dgb_heads, dla, d_proj, *, name):
    M = proj.shape[0]
    tm = _tile(M, 688, 8)
    g_ = M // tm

    tail_w = D_PROJ - C_SM

    def body(sm_ref, w2_ref, b2_ref, al_ref, dt_ref, dgb_ref, dla_ref, _,
             dsm_ref, dw2_ref, db2_ref, dal_ref, ddt_ref):
        i = pl.program_id(0)
        sm = sm_ref[...]
        lane, live, pre, neg_a, g, beta, z = _gate_terms(sm, w2_ref[...], b2_ref[...], al_ref[...], dt_ref[...], i * tm)
        dz = jnp.where(live, dla_ref[...] * (_sigmoid(-z) * (1.0 / GLA_GATE_NORMALIZER)), 0.0)
        dsm_lr = _dot(dz, w2_ref[...], NT)
        dgb = dgb_ref[0]
        for hh in range(1, GDN_HEADS):
            dgb = dgb + dgb_ref[hh]
        dgb = jnp.where(live, dgb, 0.0)
        da = dgb * neg_a * _sigmoid(pre)
        db = dgb * beta * (1.0 - beta)
        dsm = jnp.where(lane < GDN_HEADS, da, jnp.where(lane < 2 * GDN_HEADS, db, dsm_lr))
        dsm_ref[:, 0:SM_W] = dsm.astype(dsm_ref.dtype)
        dsm_ref[:, SM_W:tail_w] = jnp.zeros((tm, tail_w - SM_W), dsm_ref.dtype)
        is_a = lane < GDN_HEADS
        dal = jnp.sum(jnp.where(is_a, dgb * g, 0.0), axis=0, keepdims=True)
        ddt = jnp.sum(jnp.where(is_a, da, 0.0), axis=0, keepdims=True)
        dw2 = _dot(sm, dz, TN)
        db2 = jnp.sum(dz, axis=0, keepdims=True)

        @pl.when(i == 0)
        def _():
            dw2_ref[...] = dw2
            db2_ref[...] = db2
            dal_ref[...] = dal
            ddt_ref[...] = ddt

        @pl.when(i > 0)
        def _():
            dw2_ref[...] += dw2
            db2_ref[...] += db2
            dal_ref[...] += dal
            ddt_ref[...] += ddt

    full = lambda s: pl.BlockSpec(s, lambda i: (0, 0))
    return pl.pallas_call(
        body, name=name, grid=(g_,),
        in_specs=[pl.BlockSpec((tm, SM_W), lambda i: (i, C_SM // SM_W)), full((SM_W, GLA_QK)), full((1, GLA_QK)),
                  full((1, SM_W)), full((1, SM_W)),
                  pl.BlockSpec((GDN_HEADS, tm, SM_W), lambda i: (0, i, 0)),
                  pl.BlockSpec((tm, GLA_QK), lambda i: (i, 0)), _ANY],
        out_specs=[pl.BlockSpec((tm, tail_w), lambda i: (i, C_SM // tail_w)), full((SM_W, GLA_QK)), full((1, GLA_QK)),
                   full((1, SM_W)), full((1, SM_W))],
        out_shape=[jax.ShapeDtypeStruct(d_proj.shape, d_proj.dtype), jax.ShapeDtypeStruct((SM_W, GLA_QK), F32),
                   jax.ShapeDtypeStruct((1, GLA_QK), F32), jax.ShapeDtypeStruct((1, SM_W), F32),
                   jax.ShapeDtypeStruct((1, SM_W), F32)],
        input_output_aliases={7: 0},
        compiler_params=_params("arbitrary"),
    )(proj, w2p, b2, alog_p, dt_p, dgb_heads, dla, d_proj)


QKV_W = GDN_QK
N_QKV_GROUPS = 3
QKV_B0 = C_QKV // QKV_W
HALO = 8


def _conv_terms(x_ref, halo_ref, cw_ref, xs_ref, i, tm):
    xs_ref[HALO:HALO + tm, :] = x_ref[...]
    xs_ref[0:HALO, :] = jnp.where(i > 0, halo_ref[...], 0.0)
    cw = cw_ref[...]
    xs = xs_ref[...]
    taps = [(pltpu.roll(xs, CONV_K - 1 - t, 0) if t < CONV_K - 1 else xs)[HALO:HALO + tm, :] for t in range(CONV_K)]
    c = taps[0] * cw[0:1, :]
    for t in range(1, CONV_K):
        c = c + taps[t] * cw[t:t + 1, :]
    return c, taps


def _prep_fwd(proj, conv_w8, *, name):
    M = proj.shape[0]
    tm = _tile(M, 344, 8)

    def body(x_ref, halo_ref, cw_ref, o_ref, xs_ref):
        j, i = pl.program_id(0), pl.program_id(1)
        c, _ = _conv_terms(x_ref, halo_ref, cw_ref, xs_ref, i, tm)
        s, _ = _silu_and_grad(c)
        scale = jnp.where(j == 0, GDN_DK ** -0.5, 1.0)
        for hh in range(GDN_HEADS):
            cols = slice(hh * 128, (hh + 1) * 128)
            sh = s[:, cols]
            r = lax.rsqrt(jnp.sum(sh * sh, axis=-1, keepdims=True) + NORM_EPS)
            o_ref[:, cols] = jnp.where(j < 2, sh * (r * scale), sh)

    hb = tm // HALO
    return pl.pallas_call(
        body, name=name, grid=(N_QKV_GROUPS, M // tm),
        in_specs=[pl.BlockSpec((tm, QKV_W), lambda j, i: (i, QKV_B0 + j)),
                  pl.BlockSpec((HALO, QKV_W), lambda j, i: (jnp.maximum(i * hb - 1, 0), QKV_B0 + j)),
                  pl.BlockSpec((8, QKV_W), lambda j, i: (0, j))],
        out_specs=pl.BlockSpec((tm, QKV_W), lambda j, i: (i, j)),
        out_shape=jax.ShapeDtypeStruct((M, N_QKV_GROUPS * QKV_W), F32),
        scratch_shapes=[pltpu.VMEM((tm + HALO, QKV_W), F32)],
        compiler_params=_params("parallel", "arbitrary"),
    )(proj, proj, conv_w8)


def _prep_bwd(proj, conv_w8, dact, d_proj, *, name):
    M = proj.shape[0]
    tm = _tile(M, 688, 16)
    g_ = M // tm
    ext = tm + HALO

    def body(x_ref, prev_ref, next_ref, cw_ref, da_ref, dan_ref, _, o_ref, dcw_ref, xs_ref, das_ref, dcs_ref):
        j, i = pl.program_id(0), pl.program_id(1)
        not_last = i < g_ - 1
        xs_ref[0:HALO, :] = jnp.where(i > 0, prev_ref[...], 0.0)
        xs_ref[HALO:HALO + tm, :] = x_ref[...]
        xs_ref[HALO + tm:HALO + ext, :] = jnp.where(not_last, next_ref[...], 0.0)
        das_ref[0:tm, :] = da_ref[...]
        das_ref[tm:ext, :] = jnp.where(not_last, dan_ref[...], 0.0)
        cw = cw_ref[...]
        xs = xs_ref[...]
        taps = [(pltpu.roll(xs, CONV_K - 1 - t, 0) if t < CONV_K - 1 else xs)[HALO:HALO + ext, :] for t in range(CONV_K)]
        c = taps[0] * cw[0:1, :]
        for t in range(1, CONV_K):
            c = c + taps[t] * cw[t:t + 1, :]
        s, ds_dc = _silu_and_grad(c)
        scale = jnp.where(j == 0, GDN_DK ** -0.5, 1.0)
        for hh in range(GDN_HEADS):
            cols = slice(hh * 128, (hh + 1) * 128)
            sh = s[:, cols]
            r = lax.rsqrt(jnp.sum(sh * sh, axis=-1, keepdims=True) + NORM_EPS)
            da = das_ref[:, cols]
            y = sh * r
            dy = da * scale
            ds_norm = r * (dy - y * jnp.sum(dy * y, axis=-1, keepdims=True))
            dcs_ref[:, cols] = jnp.where(j < 2, ds_norm, da) * ds_dc[:, cols]
        dc = dcs_ref[...]
        acc = dc[0:tm, :] * cw[CONV_K - 1:CONV_K, :]
        for t in range(CONV_K - 1):
            acc = acc + pltpu.roll(dc, ext - (CONV_K - 1 - t), 0)[0:tm, :] * cw[t:t + 1, :]
        o_ref[...] = acc.astype(o_ref.dtype)
        r8 = lax.broadcasted_iota(jnp.int32, (8, QKV_W), 0)
        part = jnp.zeros((8, QKV_W), F32)
        for t in range(CONV_K):
            part = jnp.where(r8 == t, jnp.sum(dc[0:tm, :] * taps[t][0:tm, :], axis=0, keepdims=True), part)

        @pl.when(i == 0)
        def _():
            dcw_ref[...] = part

        @pl.when(i > 0)
        def _():
            dcw_ref[...] += part

    hb = tm // HALO
    last = M // HALO - 1
    prev_of = lambda i: jnp.maximum(i * hb - 1, 0)
    next_of = lambda i: jnp.minimum((i + 1) * hb, last)
    return pl.pallas_call(
        body, name=name, grid=(N_QKV_GROUPS, g_),
        in_specs=[pl.BlockSpec((tm, QKV_W), lambda j, i: (i, QKV_B0 + j)),
                  pl.BlockSpec((HALO, QKV_W), lambda j, i: (prev_of(i), QKV_B0 + j)),
                  pl.BlockSpec((HALO, QKV_W), lambda j, i: (next_of(i), QKV_B0 + j)),
                  pl.BlockSpec((8, QKV_W), lambda j, i: (0, j)),
                  pl.BlockSpec((tm, QKV_W), lambda j, i: (i, j)),
                  pl.BlockSpec((HALO, QKV_W), lambda j, i: (next_of(i), j)), _ANY],
        out_specs=[pl.BlockSpec((tm, QKV_W), lambda j, i: (i, QKV_B0 + j)), pl.BlockSpec((8, QKV_W), lambda j, i: (0, j))],
        out_shape=[jax.ShapeDtypeStruct(d_proj.shape, d_proj.dtype),
                   jax.ShapeDtypeStruct((8, N_QKV_GROUPS * QKV_W), F32)],
        input_output_aliases={6: 0},
        scratch_shapes=[pltpu.VMEM((HALO + ext, QKV_W), F32), pltpu.VMEM((ext, QKV_W), F32), pltpu.VMEM((ext, QKV_W), F32)],
        compiler_params=_params("parallel", "arbitrary"),
    )(proj, proj, proj, conv_w8, dact, dact, d_proj)


def _round_robin(gens):
    gens = list(gens)
    while gens:
        alive = []
        for gen in gens:
            try:
                next(gen)
                alive.append(gen)
            except StopIteration:
                pass
        gens = alive


def _unit_lower_inverse(a_low, eye):
    n = a_low.shape[0]
    ri = lax.broadcasted_iota(jnp.int32, (n, n), 0)
    ci = lax.broadcasted_iota(jnp.int32, (n, n), 1)
    same = lambda shift: (ri >> shift) == (ci >> shift)
    b = jnp.where(same(3), -a_low, 0.0)
    x = eye + b
    p2 = _dot3(b, b)
    yield
    x = x + _dot3(x, p2)
    p4 = _dot3(p2, p2)
    yield
    x = x + _dot3(x, p4)
    yield
    for shift in (3, 4, 5):
        between = jnp.where(same(shift + 1) & ~same(shift), a_low, 0.0)
        t = _dot3(between, x)
        yield
        x = x - _dot3(x, t)
        yield
    return x


class _GdnChunk:
    def build(self, q, k, v, gb, h, sum_on_mxu):
        C = GDN_CHUNK
        lane = lax.broadcasted_iota(jnp.int32, (C, SM_W), 1)
        g = jnp.sum(jnp.where(lane == h, gb, 0.0), axis=1, keepdims=True)
        self.beta = jnp.sum(jnp.where(lane == h + GDN_HEADS, gb, 0.0), axis=1, keepdims=True)
        ri = lax.broadcasted_iota(jnp.int32, (C, C), 0)
        ci = lax.broadcasted_iota(jnp.int32, (C, C), 1)
        self.causal = ri >= ci
        self.strict = ri > ci
        self.eye = (ri == ci).astype(F32)
        if sum_on_mxu:
            gcb = lax.dot_general(self.causal.astype(F32), jnp.broadcast_to(g, (C, SM_W)), NN,
                                  precision=lax.Precision.HIGHEST, preferred_element_type=F32)
        else:
            gcb = _running_sum(jnp.broadcast_to(g, (C, SM_W)))
        yield
        self.gcol = gcb[:, 0:1]
        grow = gcb.T[0:1, 0:C]
        self.decay = jnp.exp(jnp.where(self.causal, self.gcol - grow, -1e30))
        self.egc = jnp.exp(self.gcol)
        glast = gcb[C - 1:C, 0:1]
        self.elast = jnp.exp(glast - self.gcol)
        self.gl = jnp.exp(glast)
        self.q, self.k, self.v = q, k, v
        self.kb = k * self.beta
        m = _dot(self.kb, k, NT)
        n_ = _dot(q, k, NT)
        yield
        self.a_low = jnp.where(self.strict, m * self.decay, 0.0)
        self.p = n_ * self.decay
        self.qd = q * self.egc
        self.kd = k * self.elast
        self.bu = v * self.beta
        self.bw = self.kb * self.egc


GDN_HB = 8
GDN_HG = GDN_HEADS // GDN_HB


def _gdn_specs(n_of):
    C, W = GDN_CHUNK, 128 * GDN_HB
    q_spec = pl.BlockSpec((C, W), lambda g, n: (n_of(n), g))
    k_spec = pl.BlockSpec((C, W), lambda g, n: (n_of(n), g + GDN_HG))
    v_spec = pl.BlockSpec((C, W), lambda g, n: (n_of(n), g + 2 * GDN_HG))
    gb_spec = pl.BlockSpec((C, SM_W), lambda g, n: (n_of(n), 0))
    o_spec = pl.BlockSpec((C, W), lambda g, n: (n_of(n), g))
    s_spec = pl.BlockSpec((GDN_HB, None, GDN_DK, GDN_DV), lambda g, n: (g, n_of(n), 0, 0))
    t_spec = pl.BlockSpec((GDN_HB, None, C, C), lambda g, n: (g, n_of(n), 0, 0))
    return q_spec, k_spec, v_spec, gb_spec, o_spec, s_spec, t_spec


def _gdn_fwd(act, gb, *, name):
    M = act.shape[0]
    N = M // GDN_CHUNK

    def body(q_ref, k_ref, v_ref, gb_ref, o_ref, s_ref, t_ref, state):
        g, n = pl.program_id(0), pl.program_id(1)

        @pl.when(n == 0)
        def _():
            state[...] = jnp.zeros_like(state)

        gb_ = gb_ref[...]

        def head(hh):
            cols = slice(hh * 128, (hh + 1) * 128)
            c = _GdnChunk()
            yield from c.build(q_ref[:, cols], k_ref[:, cols], v_ref[:, cols], gb_, g * GDN_HB + hh, sum_on_mxu=True)
            tinv = yield from _unit_lower_inverse(c.a_low, c.eye)
            s = state[hh]
            s_ref[hh] = s
            t_ref[hh] = tinv
            u = _dot(tinv, c.bu)
            w = _dot(tinv, c.bw)
            yield
            vn = u - _dot(w, s)
            o1 = _dot(c.qd, s)
            yield
            o_ref[:, cols] = (o1 + _dot(c.p, vn)).astype(o_ref.dtype)
            state[hh] = c.gl * s + _dot(c.kd, vn, TN)

        _round_robin(head(hh) for hh in range(GDN_HB))

    q_spec, k_spec, v_spec, gb_spec, o_spec, s_spec, t_spec = _gdn_specs(lambda n: n)
    return pl.pallas_call(
        body, name=name, grid=(GDN_HG, N),
        in_specs=[q_spec, k_spec, v_spec, gb_spec], out_specs=[o_spec, s_spec, t_spec],
        out_shape=[jax.ShapeDtypeStruct((M, GDN_V), BF16),
                   jax.ShapeDtypeStruct((GDN_HEADS, N, GDN_DK, GDN_DV), F32),
                   jax.ShapeDtypeStruct((GDN_HEADS, N, GDN_CHUNK, GDN_CHUNK), F32)],
        scratch_shapes=[pltpu.VMEM((GDN_HB, GDN_DK, GDN_DV), F32)],
        compiler_params=_params("parallel", "arbitrary"),
    )(act, act, act, gb)


def _gdn_bwd(act, gb, do, s_all, t_all, *, name):
    M = act.shape[0]
    N = M // GDN_CHUNK
    C = GDN_CHUNK
    assert GDN_HG == 1

    def body(q_ref, k_ref, v_ref, gb_ref, do_ref, s_ref, t_ref, dact_ref, dgb_ref, dstate):
        g, n = pl.program_id(0), pl.program_id(1)

        @pl.when(n == 0)
        def _():
            dstate[...] = jnp.zeros_like(dstate)

        gb_ = gb_ref[...]
        last = lax.broadcasted_iota(jnp.int32, (C, 1), 0) == C - 1
        lane = lax.broadcasted_iota(jnp.int32, (C, SM_W), 1)
        def head(hh):
            cols = slice(hh * 128, (hh + 1) * 128)
            h = g * GDN_HB + hh
            c = _GdnChunk()
            yield from c.build(q_ref[:, cols], k_ref[:, cols], v_ref[:, cols], gb_, h, sum_on_mxu=False)
            tinv = t_ref[hh]
            tinv_t = tinv.T
            s = s_ref[hh]
            do_ = do_ref[:, cols]
            ds1 = dstate[hh]
            u = _dot(tinv, c.bu)
            w = _dot(tinv, c.bw)
            dqd = _dot(do_, s, NT)
            dvn0 = _dot(c.p, do_, TN) + _dot(c.kd, ds1)
            dst0 = _dot(c.qd, do_, TN) + c.gl * ds1
            yield
            vn = u - _dot(w, s)
            dvn = dvn0
            yield
            dp = jnp.where(c.causal, _dot(do_, vn, NT), 0.0)
            dstate[hh] = dst0 - _dot(w, dvn, TN)
            dkd = _dot(vn, ds1, NT)
            dw = -_dot(dvn, s, NT)
            dbu = _dot(tinv_t, dvn)
            dgl = jnp.sum(jnp.sum(s * ds1, axis=1, keepdims=True), axis=0, keepdims=True)
            yield
            dbw = _dot(tinv_t, dw)
            t1 = _dot(dbu, u, NT)
            yield
            da = jnp.where(c.strict, -(t1 + _dot(dbw, w, NT)), 0.0)
            dn_ = dp * c.decay
            dq0 = _dot(dn_, c.k)
            dk0 = _dot(dn_, c.q, TN)
            yield
            dm = da * c.decay
            e = da * c.a_low + dp * c.p
            dkb = _dot(dm, c.k) + dbw * c.egc
            dact_ref[:, GDN_QK + hh * 128:GDN_QK + (hh + 1) * 128] = (
                _dot(dm, c.kb, TN) + dk0 + dkb * c.beta + dkd * c.elast)
            dact_ref[:, cols] = dq0 + dqd * c.egc
            dact_ref[:, 2 * GDN_QK + hh * 128:2 * GDN_QK + (hh + 1) * 128] = dbu * c.beta
            dbeta = jnp.sum(dbu * c.v, axis=1, keepdims=True) + jnp.sum(dkb * c.k, axis=1, keepdims=True)
            t_kd = jnp.sum(dkd * c.kd, axis=1, keepdims=True)
            dgc = (jnp.sum(e, axis=1, keepdims=True) - jnp.sum(e.T, axis=1, keepdims=True)
                   + jnp.sum(dbw * c.bw, axis=1, keepdims=True) + jnp.sum(dqd * c.qd, axis=1, keepdims=True) - t_kd)
            dgc = dgc + jnp.where(last, jnp.sum(t_kd, axis=0, keepdims=True) + dgl * c.gl, 0.0)
            yield
            dg = _running_sum(jnp.broadcast_to(dgc, (C, SM_W)), reverse=True)
            dgb_ref[hh] = jnp.where(lane == h, dg, jnp.where(lane == h + GDN_HEADS, dbeta, 0.0))

        _round_robin(head(hh) for hh in range(GDN_HB))

    rev = lambda n: N - 1 - n
    q_spec, k_spec, v_spec, gb_spec, o_spec, s_spec, t_spec = _gdn_specs(rev)
    dgb_spec = pl.BlockSpec((GDN_HB, C, SM_W), lambda g, n: (g, rev(n), 0))
    return pl.pallas_call(
        body, name=name, grid=(GDN_HG, N),
        in_specs=[q_spec, k_spec, v_spec, gb_spec, o_spec, s_spec, t_spec],
        out_specs=[pl.BlockSpec((C, 2 * GDN_QK + GDN_V), lambda g, n: (rev(n), 0)), dgb_spec],
        out_shape=[jax.ShapeDtypeStruct((M, 2 * GDN_QK + GDN_V), F32),
                   jax.ShapeDtypeStruct((GDN_HEADS, M, SM_W), F32)],
        scratch_shapes=[pltpu.VMEM((GDN_HB, GDN_DK, GDN_DV), F32)],
        compiler_params=_params("parallel", "arbitrary"),
    )(act, act, act, gb, do, s_all, t_all)


GLA_STEP_ROWS = 64
GLA_SUB = GLA_STEP_ROWS // GLA_CHUNK


def _gla_cumsum(la):
    return _running_sum(la)


GLA_HALF = GLA_CHUNK // 2


def _gla_cross_factors(b):
    top = lax.broadcasted_iota(jnp.int32, b.shape, 0) < GLA_HALF
    bm = b[GLA_HALF - 1:GLA_HALF, :]
    late = jnp.where(top, 0.0, jnp.exp(jnp.minimum(b - bm, 0.0)))
    early = jnp.where(top, jnp.exp(jnp.minimum(bm - b, 0.0)), 0.0)
    return late, early


def _gla_half_decay(bh, ii):
    rj = lax.broadcasted_iota(jnp.int32, bh.shape, 0)
    return jnp.where(rj <= ii, jnp.exp(jnp.minimum(bh[ii:ii + 1, :] - bh, 0.0)), 0.0)


def _gla_scores_t(q, k, b):
    C, H = GLA_CHUNK, GLA_HALF
    lane = lax.broadcasted_iota(jnp.int32, (H, C), 1)
    halves = []
    for h0 in (0, H):
        qh, kh, bh = q[h0:h0 + H], k[h0:h0 + H], b[h0:h0 + H]
        sth = jnp.zeros((H, C), F32)
        for ii in range(H):
            si = jnp.sum(qh[ii:ii + 1, :] * kh * _gla_half_decay(bh, ii), axis=1, keepdims=True)
            sth = jnp.where(lane == h0 + ii, si, sth)
            if ii % 4 == 3:
                yield
        halves.append(sth)
    late, early = _gla_cross_factors(b)
    between = _dot(k * early, q * late, NT)
    yield
    return jnp.concatenate(halves, axis=0) + between


def _gla_specs(n_of):
    R = GLA_STEP_ROWS
    q_spec = pl.BlockSpec((R, GLA_QK), lambda n: (n_of(n), C_GQ // GLA_QK))
    k_spec = pl.BlockSpec((R, GLA_QK), lambda n: (n_of(n), C_GK // GLA_QK))
    v_spec = pl.BlockSpec((R, GLA_V), lambda n: (n_of(n), C_GV // GLA_V))
    la_spec = pl.BlockSpec((R, GLA_QK), lambda n: (n_of(n), 0))
    o_spec = pl.BlockSpec((R, GLA_V), lambda n: (n_of(n), 0))
    s_spec = pl.BlockSpec((GLA_HEADS, None, GLA_SUB, GLA_DV, GLA_DK), lambda n: (0, n_of(n), 0, 0, 0))
    return q_spec, k_spec, v_spec, la_spec, o_spec, s_spec


def _gla_fwd(proj, la, *, name):
    M = proj.shape[0]
    N = M // GLA_STEP_ROWS
    C = GLA_CHUNK

    def body(q_ref, k_ref, v_ref, la_ref, o_ref, s_ref, state):
        n = pl.program_id(0)

        @pl.when(n == 0)
        def _():
            state[...] = jnp.zeros_like(state)

        local = {}

        def within(hh, c):
            kc = slice(hh * GLA_DK, (hh + 1) * GLA_DK)
            vc = slice(hh * GLA_DV, (hh + 1) * GLA_DV)
            rows = slice(c * C, (c + 1) * C)
            q = q_ref[rows, kc] * (GLA_DK ** -0.5)
            k = k_ref[rows, kc]
            v = v_ref[rows, vc]
            b = _gla_cumsum(la_ref[rows, kc])
            yield
            blast = b[C - 1:C, :]
            sc_t = yield from _gla_scores_t(q, k, b)
            kv = _dot(v, k * jnp.exp(blast - b), TN)
            o2 = _dot(sc_t, v, TN)
            yield
            local[hh, c] = (q * jnp.exp(b), jnp.exp(blast), kv, o2)

        def across(hh):
            vc = slice(hh * GLA_DV, (hh + 1) * GLA_DV)
            st = state[hh]
            for c in range(GLA_SUB):
                qe, eblast, kv, o2 = local[hh, c]
                s_ref[hh, c] = st
                o1 = _dot(qe, st, NT)
                yield
                o_ref[c * C:(c + 1) * C, vc] = (o1 + o2).astype(o_ref.dtype)
                st = st * eblast + kv
            state[hh] = st

        _round_robin(within(hh, c) for c in range(GLA_SUB) for hh in range(GLA_HEADS))
        _round_robin(across(hh) for hh in range(GLA_HEADS))

    q_spec, k_spec, v_spec, la_spec, o_spec, s_spec = _gla_specs(lambda n: n)
    return pl.pallas_call(
        body, name=name, grid=(N,),
        in_specs=[q_spec, k_spec, v_spec, la_spec], out_specs=[o_spec, s_spec],
        out_shape=[jax.ShapeDtypeStruct((M, GLA_V), BF16),
                   jax.ShapeDtypeStruct((GLA_HEADS, N, GLA_SUB, GLA_DV, GLA_DK), F32)],
        scratch_shapes=[pltpu.VMEM((GLA_HEADS, GLA_DV, GLA_DK), F32)],
        compiler_params=_params("arbitrary"),
    )(proj, proj, proj, la)


def _gla_bwd(proj, la, do, s_all, d_proj, *, name):
    M = proj.shape[0]
    N = M // GLA_STEP_ROWS
    C = GLA_CHUNK
    qkv_w = 2 * GLA_QK + GLA_V
    assert C_GK == C_GQ + GLA_QK and C_GV == C_GK + GLA_QK and C_GQ % qkv_w == 0

    def body(q_ref, k_ref, v_ref, la_ref, do_ref, s_ref, _, dp_ref, dla_ref, dstate):
        n = pl.program_id(0)

        @pl.when(n == 0)
        def _():
            dstate[...] = jnp.zeros_like(dstate)

        H = GLA_HALF
        lane = lax.broadcasted_iota(jnp.int32, (C, C), 1)
        row = lax.broadcasted_iota(jnp.int32, (C, C), 0)
        ri = lax.broadcasted_iota(jnp.int32, (C, GLA_DK), 0)
        lane_h = lax.broadcasted_iota(jnp.int32, (H, C), 1)
        ri_h = lax.broadcasted_iota(jnp.int32, (H, GLA_DK), 0)
        cross = (row < H) & (lane >= H)
        def head(hh):
            kc = slice(hh * GLA_DK, (hh + 1) * GLA_DK)
            vc = slice(hh * GLA_DV, (hh + 1) * GLA_DV)
            ds1 = dstate[hh]
            for c in reversed(range(GLA_SUB)):
                rows = slice(c * C, (c + 1) * C)
                q = q_ref[rows, kc] * (GLA_DK ** -0.5)
                k = k_ref[rows, kc]
                v = v_ref[rows, vc]
                b = _gla_cumsum(la_ref[rows, kc])
                do_ = do_ref[rows, vc]
                st = s_ref[hh, c]
                dsc_t = _dot(v, do_, NT)
                dqe = _dot(do_, st)
                dke = _dot(v, ds1)
                yield
                blast = b[C - 1:C, :]
                eb = jnp.exp(b)
                elast = jnp.exp(blast - b)
                eblast = jnp.exp(blast)
                qe = q * eb
                ke = k * elast
                dv2 = _dot(ke, ds1, NT)
                ds_new = _dot(do_, qe, TN)
                deblast = jnp.sum(st * ds1, axis=0, keepdims=True)
                sc_halves, dq_halves, dk_halves = [], [], []
                for h0 in (0, H):
                    qh, kh, bh, dsch = q[h0:h0 + H], k[h0:h0 + H], b[h0:h0 + H], dsc_t[h0:h0 + H]
                    sch = jnp.zeros((H, C), F32)
                    dqh = jnp.zeros((H, GLA_DK), F32)
                    dkh = jnp.zeros((H, GLA_DK), F32)
                    for ii in range(H):
                        f = _gla_half_decay(bh, ii)
                        kf = kh * f
                        si = jnp.sum(qh[ii:ii + 1, :] * kf, axis=1, keepdims=True)
                        sch = jnp.where(lane_h == h0 + ii, si, sch)
                        dsi = jnp.sum(jnp.where(lane_h == h0 + ii, dsch, 0.0), axis=1, keepdims=True)
                        dqh = jnp.where(ri_h == ii, jnp.sum(dsi * kf, axis=0, keepdims=True), dqh)
                        dkh = dkh + (dsi * f) * qh[ii:ii + 1, :]
                        if ii % 4 == 3:
                            yield
                    sc_halves.append(sch)
                    dq_halves.append(dqh)
                    dk_halves.append(dkh)
                late, early = _gla_cross_factors(b)
                q_late, k_early = q * late, k * early
                dsc_x = jnp.where(cross, dsc_t, 0.0)
                sc_t = jnp.concatenate(sc_halves, axis=0) + _dot(k_early, q_late, NT)
                dq_sc = jnp.concatenate(dq_halves, axis=0) + _dot(dsc_x, k_early, TN) * late
                dk_sc = jnp.concatenate(dk_halves, axis=0) + _dot(dsc_x, q_late) * early
                yield
                dv1 = _dot(sc_t, do_)
                dp_ref[rows, kc] = ((dq_sc + dqe * eb) * (GLA_DK ** -0.5)).astype(dp_ref.dtype)
                dp_ref[rows, GLA_QK + hh * GLA_DK:GLA_QK + (hh + 1) * GLA_DK] = (dk_sc + dke * elast).astype(dp_ref.dtype)
                t_ke = dke * ke
                db = q * dq_sc - k * dk_sc + dqe * qe - t_ke
                db = db + jnp.where(ri == C - 1, jnp.sum(t_ke, axis=0, keepdims=True) + deblast * eblast, 0.0)
                dla = _running_sum(db, reverse=True)
                yield
                dp_ref[rows, 2 * GLA_QK + hh * GLA_DV:2 * GLA_QK + (hh + 1) * GLA_DV] = (dv1 + dv2).astype(dp_ref.dtype)
                dla_ref[rows, kc] = dla
                ds1 = ds1 * eblast + ds_new
            dstate[hh] = ds1

        _round_robin(head(hh) for hh in range(GLA_HEADS))

    rev = lambda n: N - 1 - n
    q_spec, k_spec, v_spec, la_spec, o_spec, s_spec = _gla_specs(rev)
    return pl.pallas_call(
        body, name=name, grid=(N,),
        in_specs=[q_spec, k_spec, v_spec, la_spec, o_spec, s_spec, _ANY],
        out_specs=[pl.BlockSpec((GLA_STEP_ROWS, qkv_w), lambda n: (rev(n), C_GQ // qkv_w)), la_spec],
        out_shape=[jax.ShapeDtypeStruct(d_proj.shape, d_proj.dtype), jax.ShapeDtypeStruct((M, GLA_QK), F32)],
        input_output_aliases={6: 0},
        scratch_shapes=[pltpu.VMEM((GLA_HEADS, GLA_DV, GLA_DK), F32)],
        compiler_params=_params("arbitrary"),
    )(proj, proj, proj, la, do, s_all, d_proj)


def _head_norm(o, wn):
    r = lax.rsqrt(jnp.mean(o * o, axis=-1, keepdims=True) + NORM_EPS)
    return o * r, r


def _mix_heads():
    heads = [(0, GDN_DV, hh * GDN_DV, hh * GDN_DV) for hh in range(GDN_HEADS)]
    heads += [(1, GLA_DV, GDN_V + hh * GLA_DV, hh * GLA_DV) for hh in range(GLA_HEADS)]
    return heads


def _mix_fwd(o_gdn, o_gla, proj, wn_gdn, wn_gla, *, name):
    M = proj.shape[0]
    tm = _tile(M, 344, 16)

    def body(og_ref, ol_ref, z_ref, r_ref, wg_ref, wl_ref, m_ref):
        srcs = ((og_ref, z_ref, wg_ref), (ol_ref, r_ref, wl_ref))
        for grp, width, mcol, col in _mix_heads():
            o_ref, gate_ref, w_ref = srcs[grp]
            xhat, _ = _head_norm(o_ref[:, col:col + width].astype(F32), None)
            gate, _ = _silu_and_grad(gate_ref[:, col:col + width])
            m_ref[:, mcol:mcol + width] = (xhat * w_ref[...] * gate).astype(m_ref.dtype)

    full = lambda s: pl.BlockSpec(s, lambda i: (0, 0))
    return pl.pallas_call(
        body, name=name, grid=(M // tm,),
        in_specs=[pl.BlockSpec((tm, GDN_V), lambda i: (i, 0)), pl.BlockSpec((tm, GLA_V), lambda i: (i, 0)),
                  pl.BlockSpec((tm, GDN_V), lambda i: (i, C_Z // GDN_V)),
                  pl.BlockSpec((tm, GLA_V), lambda i: (i, C_GR // GLA_V)),
                  full((1, GDN_DV)), full((1, GLA_DV))],
        out_specs=pl.BlockSpec((tm, D_MODEL), lambda i: (i, 0)),
        out_shape=jax.ShapeDtypeStruct((M, D_MODEL), BF16),
        compiler_params=_params("parallel"),
    )(o_gdn, o_gla, proj, proj, wn_gdn, wn_gla)


def _mix_bwd(o_gdn, o_gla, proj, wn_gdn, wn_gla, dmixed, *, name):
    M = proj.shape[0]
    tm = _tile(M, 344, 16)
    g_ = M // tm
    assert C_Z == 0 and C_GR == GDN_V

    def body(og_ref, ol_ref, z_ref, r_ref, wg_ref, wl_ref, dm_ref,
             dog_ref, dol_ref, dzr_ref, dwg_ref, dwl_ref):
        i = pl.program_id(0)
        srcs = ((og_ref, z_ref, wg_ref, dog_ref), (ol_ref, r_ref, wl_ref, dol_ref))
        dws = [jnp.zeros((1, GDN_DV), F32), jnp.zeros((1, GLA_DV), F32)]
        for grp, width, mcol, col in _mix_heads():
            o_ref, gate_ref, w_ref, do_ref = srcs[grp]
            cols = slice(col, col + width)
            xhat, r = _head_norm(o_ref[:, cols].astype(F32), None)
            gate, dgate_dc = _silu_and_grad(gate_ref[:, cols])
            dm = dm_ref[:, mcol:mcol + width]
            dzr_ref[:, mcol:mcol + width] = (dm * xhat * w_ref[...] * dgate_dc).astype(dzr_ref.dtype)
            dnorm = dm * gate
            dws[grp] = dws[grp] + jnp.sum(dnorm * xhat, axis=0, keepdims=True)
            dxhat = dnorm * w_ref[...]
            do_ref[:, cols] = r * (dxhat - xhat * jnp.mean(dxhat * xhat, axis=-1, keepdims=True))

        @pl.when(i == 0)
        def _():
            dwg_ref[...] = dws[0]
            dwl_ref[...] = dws[1]

        @pl.when(i > 0)
        def _():
            dwg_ref[...] += dws[0]
            dwl_ref[...] += dws[1]

    full = lambda s: pl.BlockSpec(s, lambda i: (0, 0))
    half = pl.BlockSpec((tm, GDN_V), lambda i: (i, 0))
    return pl.pallas_call(
        body, name=name, grid=(g_,),
        in_specs=[half, half, pl.BlockSpec((tm, GDN_V), lambda i: (i, C_Z // GDN_V)),
                  pl.BlockSpec((tm, GLA_V), lambda i: (i, C_GR // GLA_V)),
                  full((1, GDN_DV)), full((1, GLA_DV)), pl.BlockSpec((tm, D_MODEL), lambda i: (i, 0))],
        out_specs=[half, half, pl.BlockSpec((tm, GDN_V + GLA_V), lambda i: (i, 0)),
                   full((1, GDN_DV)), full((1, GLA_DV))],
        out_shape=[jax.ShapeDtypeStruct((M, GDN_V), F32), jax.ShapeDtypeStruct((M, GLA_V), F32),
                   jax.ShapeDtypeStruct((M, D_PROJ), BF16),
                   jax.ShapeDtypeStruct((1, GDN_DV), F32), jax.ShapeDtypeStruct((1, GLA_DV), F32)],
        compiler_params=_params("arbitrary"),
    )(o_gdn, o_gla, proj, proj, wn_gdn, wn_gla, dmixed)


def _row_chunks(tm, parts=2):
    if tm % (16 * parts):
        return [slice(0, tm)]
    return [slice(p * (tm // parts), (p + 1) * (tm // parts)) for p in range(parts)]


def _swiglu_fwd(n, w_gate_t, w_up_t, *, name, tm=1376, tn=512):
    M, D = n.shape
    F = w_gate_t.shape[0]
    tm, tn = _tile(M, tm, 16), _tile(F, tn, 128)

    def body(n_ref, wg_ref, wu_ref, g_ref, u_ref, a_ref):
        wg, wu = wg_ref[...], wu_ref[...]
        for rows in _row_chunks(tm):
            x = n_ref[rows, :]
            g = _dot(x, wg, NT)
            u = _dot(x, wu, NT)
            s, _ = _silu_and_grad(g)
            g_ref[rows, :] = g.astype(g_ref.dtype)
            u_ref[rows, :] = u.astype(u_ref.dtype)
            a_ref[rows, :] = (s * u).astype(a_ref.dtype)

    w_spec = pl.BlockSpec((tn, D), lambda i, j: (j, 0))
    o_spec = pl.BlockSpec((tm, tn), lambda i, j: (i, j))
    return pl.pallas_call(
        body, name=name, grid=(M // tm, F // tn),
        in_specs=[pl.BlockSpec((tm, D), lambda i, j: (i, 0)), w_spec, w_spec], out_specs=[o_spec] * 3,
        out_shape=[jax.ShapeDtypeStruct((M, F), BF16)] * 3, compiler_params=_params("parallel", "parallel"),
    )(n, w_gate_t, w_up_t)


def _swiglu_bwd(dh, w_down, gate, up, *, name, after=None, tm=1376, tn=512):
    M, D = dh.shape
    F = w_down.shape[0]
    tm, tn = _tile(M, tm, 16), _tile(F, tn, 128)
    n_after = 0 if after is None else 1

    def body(*refs):
        dh_ref, w_ref, g_ref, u_ref, dg_ref, du_ref = refs[n_after:]
        w = w_ref[...]
        for rows in _row_chunks(tm):
            da = _dot(dh_ref[rows, :], w, NT)
            s, ds = _silu_and_grad(g_ref[rows, :].astype(F32))
            dg_ref[rows, :] = (da * u_ref[rows, :].astype(F32) * ds).astype(dg_ref.dtype)
            du_ref[rows, :] = (da * s).astype(du_ref.dtype)

    o_spec = pl.BlockSpec((tm, tn), lambda i, j: (i, j))
    return pl.pallas_call(
        body, name=name, grid=(M // tm, F // tn),
        in_specs=[_ANY] * n_after + [pl.BlockSpec((tm, D), lambda i, j: (i, 0)),
                                     pl.BlockSpec((tn, D), lambda i, j: (j, 0)), o_spec, o_spec],
        out_specs=[o_spec, o_spec], out_shape=[jax.ShapeDtypeStruct((M, F), BF16)] * 2,
        compiler_params=_params("parallel", "parallel"),
    )(*((after,) if n_after else ()), dh, w_down, gate, up)


def _adamw_update(w, g, m, v):
    nm = ADAM_B1 * m + (1.0 - ADAM_B1) * g
    nv = ADAM_B2 * v + (1.0 - ADAM_B2) * (g * g)
    m_hat = nm / (1.0 - ADAM_B1 ** ADAM_STEP)
    v_hat = nv / (1.0 - ADAM_B2 ** ADAM_STEP)
    return -ADAM_LR * (m_hat / (jnp.sqrt(v_hat) + ADAM_EPS) + ADAM_WD * w), nm, nv


def _adamw(w, g, m, v, *, name):
    shape = w.shape
    cols = shape[-1]
    rows = w.size // cols
    w2, g2, m2, v2 = (t.reshape(rows, cols) for t in (w, g, m, v))
    if rows % 8 == 0 or cols % 128 != 0:
        tr, tc = (_tile(rows, 256, 8) if rows % 8 == 0 else rows), cols
    else:
        tr, tc = rows, _tile(cols, 256, 128)

    def body(w_ref, g_ref, m_ref, v_ref, d_ref, nm_ref, nv_ref):
        d_ref[...], nm_ref[...], nv_ref[...] = _adamw_update(w_ref[...], g_ref[...], m_ref[...], v_ref[...])

    blk = pl.BlockSpec((tr, tc), lambda i, j: (i, j))
    outs = pl.pallas_call(
        body, name=name, grid=(rows // tr, cols // tc), in_specs=[blk] * 4, out_specs=[blk] * 3,
        out_shape=[jax.ShapeDtypeStruct((rows, cols), F32)] * 3, compiler_params=_params("parallel", "parallel"),
    )(w2, g2, m2, v2)
    return tuple(t.reshape(shape) for t in outs)


def _sum_slabs(x, *, name):
    _, R, C = x.shape
    sub = 16 if x.dtype == BF16 else 8
    if R % sub == 0:
        tr, tc = _tile(R, 128, sub), C
    else:
        tr, tc = R, _tile(C, 256, 128)

    def body(x_ref, o_ref):
        acc = x_ref[0].astype(F32)
        for s in range(1, N_DEV):
            acc = acc + x_ref[s].astype(F32)
        o_ref[...] = acc

    return pl.pallas_call(
        body, name=name, grid=(R // tr, C // tc),
        in_specs=[pl.BlockSpec((N_DEV, tr, tc), lambda i, j: (0, i, j))],
        out_specs=pl.BlockSpec((tr, tc), lambda i, j: (i, j)),
        out_shape=jax.ShapeDtypeStruct((R, C), F32), compiler_params=_params("parallel", "parallel"),
    )(x)


def _sum_adamw(x, w, m, v, *, name):
    _, R, C = x.shape
    if R % 16 == 0:
        tr, tc = _tile(R, 128, 16), C
    else:
        tr, tc = R, _tile(C, 256, 128)

    def body(x_ref, w_ref, m_ref, v_ref, g_ref, d_ref, nm_ref, nv_ref):
        g = x_ref[0].astype(F32)
        for s in range(1, N_DEV):
            g = g + x_ref[s].astype(F32)
        g_ref[...] = g
        d_ref[...], nm_ref[...], nv_ref[...] = _adamw_update(w_ref[...], g, m_ref[...], v_ref[...])

    blk = pl.BlockSpec((tr, tc), lambda i, j: (i, j))
    return pl.pallas_call(
        body, name=name, grid=(R // tr, C // tc),
        in_specs=[pl.BlockSpec((N_DEV, tr, tc), lambda i, j: (0, i, j)), blk, blk, blk], out_specs=[blk] * 4,
        out_shape=[jax.ShapeDtypeStruct((R, C), F32)] * 4, compiler_params=_params("parallel", "parallel"),
    )(x, w, m, v)


def _peers():
    x, y, c = lax.axis_index("x"), lax.axis_index("y"), lax.axis_index("c")
    me = 4 * x + 2 * y + c
    peers = []
    for k in range(1, N_DEV):
        px = 1 - x if k & 4 else x
        py = 1 - y if k & 2 else y
        pc = 1 - c if k & 1 else c
        peers.append(((px, py, pc), 4 * px + 2 * py + pc))
    return me, peers


def _gather(x, *, name):
    def body(x_ref, o_ref, send_sems, recv_sems, own_sem):
        me, peers = _peers()
        own = pltpu.make_async_copy(x_ref, o_ref.at[me], own_sem)
        own.start()
        sends, recvs = [], []
        for k, (pos, idx) in enumerate(peers):
            sends.append(pltpu.make_async_remote_copy(
                src_ref=x_ref, dst_ref=o_ref.at[me], send_sem=send_sems.at[k], recv_sem=recv_sems.at[k],
                device_id=pos, device_id_type=pl.DeviceIdType.MESH))
            recvs.append(pltpu.make_async_remote_copy(
                src_ref=x_ref, dst_ref=o_ref.at[idx], send_sem=send_sems.at[k], recv_sem=recv_sems.at[k],
                device_id=pos, device_id_type=pl.DeviceIdType.MESH))
        for cp in sends:
            cp.start()
        for cp in recvs:
            cp.wait_recv()
        for cp in sends:
            cp.wait_send()
        own.wait()

    hbm = pl.BlockSpec(memory_space=pltpu.HBM)
    return pl.pallas_call(
        body, name=name, in_specs=[hbm], out_specs=hbm,
        out_shape=jax.ShapeDtypeStruct((N_DEV,) + tuple(x.shape), x.dtype),
        scratch_shapes=[pltpu.SemaphoreType.DMA((N_DEV - 1,)), pltpu.SemaphoreType.DMA((N_DEV - 1,)),
                        pltpu.SemaphoreType.DMA],
    )(x)


_HBM = pl.BlockSpec(memory_space=pltpu.HBM)
_SEM = pl.BlockSpec(memory_space=pltpu.SEMAPHORE)
_EFFECT = pltpu.SideEffectType.DATAFLOW_SIDE_EFFECTING


PLAN_GATHER = tuple((k, "x", 0) for k in range(1, N_DEV))
PLAN_SCATTER = tuple((k, "xk", 0) for k in range(1, N_DEV))
PLAN_GATHER_CHIPS = tuple((k, "x", 0) for k in (1, 2, 4, 6))
PLAN_GATHER_PASS_ON = tuple((1, ("land", q), q) for q in (2, 4, 6))


def _plan_refs(plan, j, x_ref, land_ref, me, peers, receiving):
    k, source, r = plan[j]
    index_of = lambda q: me if q == 0 else peers[q - 1][1]
    pos, target = peers[k - 1]
    if source == "x":
        src = x_ref
    elif source == "xk":
        src = x_ref.at[target]
    else:
        src = land_ref.at[index_of(source[1])]
    return pos, src, land_ref.at[index_of(k ^ r) if receiving else index_of(r)]


def _exchange_start(x, *, plan, name, after=None, land=None, slab=None):
    n_after = 0 if after is None else 1
    n = len(plan)

    def body(*refs):
        x_ref, land_ref, send_sems, recv_sems, _, _, token = refs[n_after:]
        me, peers = _peers()
        for j in range(n):
            pos, src, dst = _plan_refs(plan, j, x_ref, land_ref, me, peers, receiving=False)
            pltpu.make_async_remote_copy(src_ref=src, dst_ref=dst, send_sem=send_sems.at[j], recv_sem=recv_sems.at[j],
                                         device_id=pos, device_id_type=pl.DeviceIdType.MESH).start()
        token[...] = jnp.zeros_like(token)

    if land is None:
        land = lax.empty((N_DEV,) + tuple(slab), x.dtype)
    return pl.pallas_call(
        body, name=name,
        out_shape=(pltpu.SemaphoreType.DMA((n,)), pltpu.SemaphoreType.DMA((n,)),
                   pltpu.HBM(x.shape, x.dtype), pltpu.HBM(land.shape, land.dtype), jax.ShapeDtypeStruct((8, 128), F32)),
        in_specs=[_ANY] * n_after + [_HBM, _HBM],
        out_specs=(_SEM, _SEM, _HBM, _HBM, pl.BlockSpec(memory_space=pltpu.VMEM)),
        input_output_aliases={n_after: 2, n_after + 1: 3},
        compiler_params=pltpu.CompilerParams(has_side_effects=_EFFECT),
    )(*((after,) if n_after else ()), pltpu.with_memory_space_constraint(x, pltpu.HBM),
      pltpu.with_memory_space_constraint(land, pltpu.HBM))


def _exchange_wait(handle, after, *, plan, name):
    send_sems, recv_sems, x_thru, land_thru, _ = handle
    afters = list(after) if isinstance(after, (list, tuple)) else [after]

    def body(x_ref, land_ref, send_sems, recv_sems, *rest):
        me, peers = _peers()
        for j in range(len(plan)):
            pos, src, dst = _plan_refs(plan, j, x_ref, land_ref, me, peers, receiving=True)
            cp = pltpu.make_async_remote_copy(src_ref=src, dst_ref=dst, send_sem=send_sems.at[j], recv_sem=recv_sems.at[j],
                                              device_id=pos, device_id_type=pl.DeviceIdType.MESH)
            cp.wait_send()
            cp.wait_recv()

    return pl.pallas_call(
        body, name=name,
        out_shape=(pltpu.HBM(x_thru.shape, x_thru.dtype), pltpu.HBM(land_thru.shape, land_thru.dtype)),
        in_specs=[_HBM, _HBM, _SEM, _SEM] + [_ANY] * len(afters), out_specs=(_HBM, _HBM),
        input_output_aliases={0: 0, 1: 1}, compiler_params=pltpu.CompilerParams(has_side_effects=_EFFECT),
    )(x_thru, land_thru, send_sems, recv_sems, *afters)


def _to_proj_rows(t):
    z = jnp.zeros((D_PROJ - C_SM - 2 * GDN_HEADS - GLA_RANK,) + t.shape[1:], t.dtype)
    return jnp.concatenate([t[R_Z:R_A], t[R_GR:R_LR], t[R_GQ:R_GR], t[:R_Z], t[R_A:R_GQ], t[R_LR:], z], axis=0)


def _from_proj_rows(t):
    ab = C_SM + 2 * GDN_HEADS
    return jnp.concatenate([t[C_QKV:C_SM], t[C_Z:C_GR], t[C_SM:ab], t[C_GQ:C_QKV], t[C_GR:C_GQ],
                            t[ab:ab + GLA_RANK]], axis=0)


def _local_step(x, target, meta, attn_nw, conv_w, a_log, dt_bias, gdn_nw, w2, b2, gla_nw, ffn_nw, final_nw,
                fetch, emit, start=None):
    head = jnp.concatenate([jnp.zeros((ROW_PAD, D_MODEL), F32), meta], axis=0)
    conv_w8 = jnp.concatenate([conv_w, jnp.zeros((8 - CONV_K, conv_w.shape[1]), F32)], axis=0)
    w2p = jnp.zeros((SM_W, GLA_QK), F32).at[2 * GDN_HEADS:2 * GDN_HEADS + GLA_RANK].set(w2)
    alog_p = jnp.zeros((1, SM_W), F32).at[:, :GDN_HEADS].set(a_log)
    dt_p = jnp.zeros((1, SM_W), F32).at[:, :GDN_HEADS].set(dt_bias)

    h0, n1 = _embed_norm(head, x, attn_nw, name="attn_norm", after=start)
    w_in_t = fetch("w_in_t", (n1, conv_w8, w2p, alog_p, dt_p))
    proj = _matmul(n1, w_in_t, mode="nt", name="in_proj")
    gb, la = _gates_fwd(proj, w2p, b2, alog_p, dt_p, name="gates")
    act = _prep_fwd(proj, conv_w8, name="gdn_prep")
    o_gdn, s_gdn, t_gdn = _gdn_fwd(act, gb, name="gdn_fwd")
    o_gla, s_gla = _gla_fwd(proj, la, name="gla_fwd")
    mixed = _mix_fwd(o_gdn, o_gla, proj, gdn_nw, gla_nw, name="mix")
    w_out = fetch("w_out", mixed)
    h1 = _matmul(mixed, w_out, mode="nn", add=h0, name="out_proj")
    n2 = _rmsnorm_fwd(h1, ffn_nw, name="ffn_norm")
    w_gate_t, w_up_t = fetch("w_gate_t", n2), fetch("w_up_t", n2)
    gate, up, hid = _swiglu_fwd(n2, w_gate_t, w_up_t, name="swiglu")
    w_down = fetch("w_down", hid)
    h2 = _matmul(hid, w_down, mode="nn", add=h1, name="ffn_down", tm=1376, tn=256)
    dh2, dh2_b, d_final_nw, loss = _loss_head(h2, final_nw, target, name="loss_head")

    wg = dict(mode="tn", out_dtype=BF16, tn=512)
    tok = emit("w_down", _matmul(hid, dh2_b, name="d_w_down", tm=704, **wg))
    d_gate, d_up = _swiglu_bwd(dh2_b, w_down, gate, up, name="d_swiglu", after=tok)
    tok = emit("w_gate_t", _matmul(d_gate, n2, name="d_w_gate", tm=704, **wg))
    tok = emit("w_up_t", _matmul(d_up, n2, name="d_w_up", tm=704, after=tok, **wg))
    d_n2 = _matmul_pair(d_gate, w_gate_t, d_up, w_up_t, name="d_n2", after=tok)
    dh1, dh1_b, d_ffn_nw = _rmsnorm_bwd(h1, ffn_nw, d_n2, dh2, name="d_ffn_norm")

    tok = emit("w_out", _matmul(mixed, dh1_b, name="d_w_out", tm=512, **wg))
    d_mixed = _matmul(dh1_b, w_out, mode="nt", name="d_mixed", after=tok)
    do_gdn, do_gla, d_proj, d_gdn_nw, d_gla_nw = _mix_bwd(o_gdn, o_gla, proj, gdn_nw, gla_nw, d_mixed, name="d_mix")
    d_proj, d_la = _gla_bwd(proj, la, do_gla, s_gla, d_proj, name="gla_bwd")
    dact, dgb_heads = _gdn_bwd(act, gb, do_gdn, s_gdn, t_gdn, name="gdn_bwd")
    d_proj, d_w2p, d_b2, d_alog, d_dt = _gates_bwd(proj, w2p, b2, alog_p, dt_p, dgb_heads, d_la, d_proj, name="d_gates")
    d_proj, d_conv_w8 = _prep_bwd(proj, conv_w8, dact, d_proj, name="d_gdn_prep")
    tok = emit("w_in_t", _matmul(d_proj, n1, name="d_w_in", tm=768, **wg))
    d_n1 = _matmul(d_proj, w_in_t, mode="nn", name="d_n1", tm=688, after=tok)
    grad_x, d_head, d_attn_nw = _embed_norm_bwd(h0, attn_nw, d_n1, dh1, name="d_attn_norm")

    return dict(
        loss=loss[0, 0], grad_x=grad_x, meta=d_head[ROW_PAD:HEAD_ROWS], attn_nw=d_attn_nw,
        conv_w=d_conv_w8[:CONV_K], a_log=d_alog[:, :GDN_HEADS], dt_bias=d_dt[:, :GDN_HEADS], gdn_nw=d_gdn_nw,
        w2=d_w2p[2 * GDN_HEADS:2 * GDN_HEADS + GLA_RANK], b2=d_b2, gla_nw=d_gla_nw, ffn_nw=d_ffn_nw,
        final_nw=d_final_nw)


SMALL_ROWS = 32


def kernel(x, meta_tokens, attn_norm_w, w_in, gdn_conv_w, gdn_a_log, gdn_dt_bias, gdn_norm_w, gla_gate_w2, gla_gate_b, gla_norm_w, w_out, ffn_norm_w, w_gate, w_up, w_down, final_norm_w, loss_target, m_meta_tokens, m_attn_norm_w, m_w_in, m_gdn_conv_w, m_gdn_a_log, m_gdn_dt_bias, m_gdn_norm_w, m_gla_gate_w2, m_gla_gate_b, m_gla_norm_w, m_w_out, m_ffn_norm_w, m_w_gate, m_w_up, m_w_down, m_final_norm_w, v_meta_tokens, v_attn_norm_w, v_w_in, v_gdn_conv_w, v_gdn_a_log, v_gdn_dt_bias, v_gdn_norm_w, v_gla_gate_w2, v_gla_gate_b, v_gla_norm_w, v_w_out, v_ffn_norm_w, v_w_gate, v_w_up, v_w_down, v_final_norm_w):
    me = 4 * lax.axis_index("x") + 2 * lax.axis_index("y") + lax.axis_index("c")

    n_conv = gdn_conv_w.shape[2]
    n_w2 = gla_gate_w2.shape[2]
    n_meta = meta_tokens.shape[1]
    small = jnp.zeros((40, n_conv), F32)
    small = small.at[0:N_META, :n_meta].set(meta_tokens)
    small = small.at[N_META:N_META + CONV_K, :].set(gdn_conv_w[0])
    small = small.at[24:24 + GLA_RANK, :n_w2].set(gla_gate_w2[0])
    small_all = _gather(small, name="gather_small")
    meta_f = small_all[:, 0:N_META, :n_meta].transpose(1, 0, 2).reshape(N_META, D_MODEL)
    conv_f = small_all[:, N_META:N_META + CONV_K, :].transpose(1, 0, 2).reshape(CONV_K, N_DEV * n_conv)
    w2_f = small_all[:, 24:24 + GLA_RANK, :n_w2].transpose(1, 0, 2).reshape(GLA_RANK, N_DEV * n_w2)

    w_in_slab = w_in[0].T.astype(BF16)
    in_h = _exchange_start(w_in_slab, plan=PLAN_GATHER_CHIPS, slab=w_in_slab.shape, name="gather_w_in_start",
                           after=small_all)
    handles, tok = {}, in_h[4]
    for wname, slab in (("w_out", w_out[0]), ("w_gate_t", w_gate[0].T), ("w_up_t", w_up[0].T), ("w_down", w_down[0])):
        slab = slab.astype(BF16)
        handles[wname] = _exchange_start(slab, plan=PLAN_GATHER, slab=slab.shape, name="gather_" + wname + "_start", after=tok)
        tok = handles[wname][4]

    def fetch(name, after):
        if name == "w_in_t":
            own, got = _exchange_wait(in_h, after, plan=PLAN_GATHER_CHIPS, name="gather_w_in_wait")
            pass_h = _exchange_start(own, plan=PLAN_GATHER_PASS_ON, land=got, name="pass_w_in_start")
            own, got = _exchange_wait(pass_h, pass_h[4], plan=PLAN_GATHER_PASS_ON, name="pass_w_in_wait")
            got = lax.dynamic_update_index_in_dim(got, own, me, 0)
            return _to_proj_rows(got.reshape(D_IN, D_MODEL))
        own, got = _exchange_wait(handles[name], after, plan=PLAN_GATHER, name="gather_" + name + "_wait")
        got = lax.dynamic_update_index_in_dim(got, own, me, 0)
        return got.reshape(N_DEV * got.shape[1], D_MODEL)

    sent = {}

    def emit(name, grad):
        if name == "w_in_t":
            grad = _from_proj_rows(grad)
        parts = grad.reshape(N_DEV, grad.shape[0] // N_DEV, D_MODEL)
        sent[name] = _exchange_start(parts, plan=PLAN_SCATTER, slab=parts.shape[1:], name="scatter_" + name + "_start")
        return sent[name][4]

    g = _local_step(x[0], loss_target[0], meta_f, attn_norm_w, conv_f, gdn_a_log, gdn_dt_bias, gdn_norm_w, w2_f,
                    gla_gate_b, gla_norm_w, ffn_norm_w, final_norm_w.reshape(1, D_MODEL), fetch, emit, start=tok)

    misc = jnp.concatenate([g["a_log"], g["dt_bias"], g["gdn_nw"], g["gla_nw"], g["b2"], g["loss"].reshape(1, 1)], axis=1)
    n_misc = misc.shape[1]
    misc = jnp.pad(misc, ((0, 0), (0, D_MODEL - n_misc)))
    rows = jnp.concatenate([g["attn_nw"], g["ffn_nw"], g["final_nw"], misc, g["meta"],
                            g["conv_w"].reshape(-1, D_MODEL), g["w2"].reshape(-1, D_MODEL)], axis=0)
    rows = jnp.pad(rows, ((0, SMALL_ROWS - rows.shape[0]), (0, 0)))
    rows_h = _exchange_start(rows, plan=PLAN_GATHER, slab=rows.shape, name="gather_small_grads_start")

    big = {}
    after = rows_h[4]
    for name, w, m, v, transposed in (("w_down", w_down, m_w_down, v_w_down, False), ("w_gate_t", w_gate, m_w_gate, v_w_gate, True),
                                      ("w_up_t", w_up, m_w_up, v_w_up, True), ("w_out", w_out, m_w_out, v_w_out, False),
                                      ("w_in_t", w_in, m_w_in, v_w_in, True)):
        own, got = _exchange_wait(sent[name], after, plan=PLAN_SCATTER, name="scatter_" + name + "_wait")
        got = lax.dynamic_update_index_in_dim(got, lax.dynamic_index_in_dim(own, me, 0, keepdims=False), me, 0)
        local = [t[0].T if transposed else t[0] for t in (w, m, v)]
        res = _sum_adamw(got, *local, name="adamw_" + name)
        big[name] = [t.T[None] if transposed else t[None] for t in res]
        after = res[0]

    own, got = _exchange_wait(rows_h, after, plan=PLAN_GATHER, name="gather_small_grads_wait")
    tot = _sum_slabs(lax.dynamic_update_index_in_dim(got, own, me, 0), name="sum_small_grads")
    grad_attn_nw, grad_ffn_nw, grad_final_nw = tot[0:1], tot[1:2], tot[2]
    grad_a_log = tot[3:4, 0:8]
    grad_dt = tot[3:4, 8:16]
    grad_gdn_nw = tot[3:4, 16:16 + GDN_DV]
    grad_gla_nw = tot[3:4, 144:144 + GLA_DV]
    grad_b2 = tot[3:4, 400:400 + GLA_QK]
    loss = tot[3, n_misc - 1]
    r0 = 4 + N_META
    grad_meta = lax.dynamic_slice(tot[4:r0], (0, me * n_meta), (N_META, n_meta))
    r1 = r0 + CONV_K * N_DEV * n_conv // D_MODEL
    grad_conv = lax.dynamic_slice(tot[r0:r1].reshape(CONV_K, N_DEV * n_conv), (0, me * n_conv), (CONV_K, n_conv))[None]
    r2 = r1 + GLA_RANK * N_DEV * n_w2 // D_MODEL
    grad_w2 = lax.dynamic_slice(tot[r1:r2].reshape(GLA_RANK, N_DEV * n_w2), (0, me * n_w2), (GLA_RANK, n_w2))[None]

    weights = [meta_tokens, attn_norm_w, w_in, gdn_conv_w, gdn_a_log, gdn_dt_bias, gdn_norm_w, gla_gate_w2,
               gla_gate_b, gla_norm_w, w_out, ffn_norm_w, w_gate, w_up, w_down, final_norm_w]
    grads = [grad_meta, grad_attn_nw, "w_in_t", grad_conv, grad_a_log, grad_dt, grad_gdn_nw, grad_w2,
             grad_b2, grad_gla_nw, "w_out", grad_ffn_nw, "w_gate_t", "w_up_t", "w_down", grad_final_nw]
    ms = [m_meta_tokens, m_attn_norm_w, m_w_in, m_gdn_conv_w, m_gdn_a_log, m_gdn_dt_bias, m_gdn_norm_w,
          m_gla_gate_w2, m_gla_gate_b, m_gla_norm_w, m_w_out, m_ffn_norm_w, m_w_gate, m_w_up, m_w_down, m_final_norm_w]
    vs = [v_meta_tokens, v_attn_norm_w, v_w_in, v_gdn_conv_w, v_gdn_a_log, v_gdn_dt_bias, v_gdn_norm_w,
          v_gla_gate_w2, v_gla_gate_b, v_gla_norm_w, v_w_out, v_ffn_norm_w, v_w_gate, v_w_up, v_w_down, v_final_norm_w]
    outs = [[], [], [], []]
    for idx, (w, gr, m, v) in enumerate(zip(weights, grads, ms, vs)):
        if isinstance(gr, str):
            res = big[gr]
        else:
            gr = gr.reshape(w.shape)
            res = (gr,) + _adamw(w, gr, m, v, name=f"adamw_{idx}")
        for lst, t in zip(outs, res):
            lst.append(t)
    return (loss, g["grad_x"][None], *outs[0], *outs[1], *outs[2], *outs[3])
```

```python
import functools

import jax
import jax.numpy as jnp
from jax import lax
from jax.experimental import pallas as pl
from jax.experimental.pallas import tpu as pltpu

F32 = jnp.float32
BF16 = jnp.bfloat16
_MXU_DTYPE = jnp.bfloat16

D_MODEL = 2048
N_META = 16
ROW_PAD = 48
HEAD_ROWS = ROW_PAD + N_META
CONV_K = 4
GDN_HEADS, GDN_DK, GDN_DV, GDN_CHUNK = 8, 128, 128, 64
GLA_HEADS, GLA_DK, GLA_DV, GLA_CHUNK = 4, 128, 256, 16
GLA_RANK = 16
GLA_GATE_NORMALIZER = 16.0
GDN_QK = GDN_HEADS * GDN_DK
GDN_V = GDN_HEADS * GDN_DV
GLA_QK = GLA_HEADS * GLA_DK
GLA_V = GLA_HEADS * GLA_DV
D_FF = 5632
D_IN = 7200
NORM_EPS = 1e-6
C_Z, C_GR, C_GQ, C_GK, C_GV, C_QKV, C_SM = 0, 1024, 2048, 2560, 3072, 4096, 7168
SM_W = 128
D_PROJ = 7680
R_Z, R_A, R_B, R_GQ, R_GK, R_GV, R_GR, R_LR = 3072, 4096, 4104, 4112, 4624, 5136, 6160, 7184

ADAM_LR, ADAM_B1, ADAM_B2, ADAM_EPS, ADAM_WD, ADAM_STEP = 0.001, 0.9, 0.999, 1e-08, 0.01, 10

N_DEV = 8
VMEM_LIMIT = 56 * 1024 * 1024

NN = (((1,), (0,)), ((), ()))
NT = (((1,), (1,)), ((), ()))
TN = (((0,), (0,)), ((), ()))


def _dot(a, b, dims=NN):
    return lax.dot_general(a.astype(_MXU_DTYPE), b.astype(_MXU_DTYPE), dims, preferred_element_type=F32)


def _running_sum(x, reverse=False):
    n = x.shape[0]
    row = lax.broadcasted_iota(jnp.int32, x.shape, 0)
    s = 1
    while s < n:
        if reverse:
            x = x + jnp.where(row < n - s, pltpu.roll(x, n - s, 0), 0.0)
        else:
            x = x + jnp.where(row >= s, pltpu.roll(x, s, 0), 0.0)
        s *= 2
    return x


def _dot3(a, b):
    ah = a.astype(BF16)
    al = (a - ah.astype(F32)).astype(BF16)
    bh = b.astype(BF16)
    bl = (b - bh.astype(F32)).astype(BF16)
    d = functools.partial(lax.dot_general, dimension_numbers=NN, preferred_element_type=F32)
    return d(ah, bh) + (d(ah, bl) + d(al, bh))


def _tile(n, target, mult=8):
    best = None
    for t in range(mult, min(n, target) + 1, mult):
        if n % t == 0:
            best = t
    return best if best is not None else n


def _params(*sem):
    return pltpu.CompilerParams(dimension_semantics=sem, vmem_limit_bytes=VMEM_LIMIT)


def _sigmoid(x):
    return 0.5 * jnp.tanh(0.5 * x) + 0.5


def _softplus(x):
    return jnp.maximum(x, 0.0) + jnp.log1p(jnp.exp(-jnp.abs(x)))


def _silu_and_grad(c):
    s = _sigmoid(c)
    return c * s, s * (1.0 + c * (1.0 - s))


_ANY = pl.BlockSpec(memory_space=pl.ANY)


def _matmul(a, b, *, mode, name, out_dtype=F32, add=None, after=None, tm=1376, tn=512):
    if mode == "tn":
        K, M = a.shape
        N = b.shape[1]
    else:
        M, K = a.shape
        N = b.shape[0] if mode == "nt" else b.shape[1]
    tm = _tile(M, tm, 128 if mode == "tn" else 16)
    tn = _tile(N, tn, 128)
    dims = {"nn": NN, "nt": NT, "tn": TN}[mode]
    n_after = 0 if after is None else 1

    def body(*refs):
        refs = refs[n_after:]
        r = _dot(refs[0][...], refs[1][...], dims)
        if add is not None:
            r = r + refs[2][...]
        refs[-1][...] = r.astype(out_dtype)

    a_spec = pl.BlockSpec((K, tm), lambda i, j: (0, i)) if mode == "tn" else pl.BlockSpec((tm, K), lambda i, j: (i, 0))
    b_spec = pl.BlockSpec((tn, K), lambda i, j: (j, 0)) if mode == "nt" else pl.BlockSpec((K, tn), lambda i, j: (0, j))
    o_spec = pl.BlockSpec((tm, tn), lambda i, j: (i, j))
    in_specs = [_ANY] * n_after + [a_spec, b_spec] + ([o_spec] if add is not None else [])
    args = ((after,) if n_after else ()) + (a, b) + ((add,) if add is not None else ())
    return pl.pallas_call(
        body, name=name, grid=(M // tm, N // tn), in_specs=in_specs, out_specs=o_spec,
        out_shape=jax.ShapeDtypeStruct((M, N), out_dtype), compiler_params=_params("parallel", "parallel"),
    )(*args)


def _matmul_pair(a1, b1, a2, b2, *, name, after=None, tm=688, tn=256):
    M, K = a1.shape
    N = b1.shape[1]
    tm, tn = _tile(M, tm, 16), _tile(N, tn, 128)
    n_after = 0 if after is None else 1

    def body(*refs):
        a1_ref, b1_ref, a2_ref, b2_ref, o_ref = refs[n_after:]
        o_ref[...] = _dot(a1_ref[...], b1_ref[...]) + _dot(a2_ref[...], b2_ref[...])

    a_spec = pl.BlockSpec((tm, K), lambda i, j: (i, 0))
    b_spec = pl.BlockSpec((K, tn), lambda i, j: (0, j))
    return pl.pallas_call(
        body, name=name, grid=(M // tm, N // tn), in_specs=[_ANY] * n_after + [a_spec, b_spec, a_spec, b_spec],
        out_specs=pl.BlockSpec((tm, tn), lambda i, j: (i, j)), out_shape=jax.ShapeDtypeStruct((M, N), F32),
        compiler_params=_params("parallel", "parallel"),
    )(*((after,) if n_after else ()), a1, b1, a2, b2)


def _rmsnorm_fwd(h, w, *, name):
    M, D = h.shape
    tm = _tile(M, 688, 16)

    def body(h_ref, w_ref, n_ref):
        x = h_ref[...]
        r = lax.rsqrt(jnp.mean(x * x, axis=-1, keepdims=True) + NORM_EPS)
        n_ref[...] = (x * r * w_ref[...]).astype(n_ref.dtype)

    return pl.pallas_call(
        body, name=name, grid=(M // tm,),
        in_specs=[pl.BlockSpec((tm, D), lambda i: (i, 0)), pl.BlockSpec((1, D), lambda i: (0, 0))],
        out_specs=pl.BlockSpec((tm, D), lambda i: (i, 0)),
        out_shape=jax.ShapeDtypeStruct((M, D), BF16),
        compiler_params=_params("parallel"),
    )(h, w)


SEQ_BLOCK = HEAD_ROWS


def _seq_blocks_per_tile(rows):
    n = rows // SEQ_BLOCK
    return max(m for m in (1, 2, 3, 4) if n % m == 0)


def _seq_specs(m, D):
    return [pl.BlockSpec((SEQ_BLOCK, D), functools.partial(lambda i, k: (jnp.maximum(m * i + k - 1, 0), 0), k=k))
            for k in range(m)]


def _embed_norm(head, x, w, *, name, after=None):
    S, D = x.shape
    m = _seq_blocks_per_tile(S + HEAD_ROWS)
    n_after = 0 if after is None else 1

    def body(*refs):
        refs = refs[n_after:]
        head_ref, x_refs, w_ref, h_ref, n_ref = refs[0], refs[1:1 + m], refs[1 + m], refs[2 + m], refs[3 + m]
        i = pl.program_id(0)
        for k in range(m):
            blk = x_refs[k][...]
            if k == 0:
                blk = jnp.where(i == 0, head_ref[...], blk)
            rows = slice(k * SEQ_BLOCK, (k + 1) * SEQ_BLOCK)
            h_ref[rows, :] = blk
            r = lax.rsqrt(jnp.mean(blk * blk, axis=-1, keepdims=True) + NORM_EPS)
            n_ref[rows, :] = (blk * r * w_ref[...]).astype(n_ref.dtype)

    tile = pl.BlockSpec((m * SEQ_BLOCK, D), lambda i: (i, 0))
    return pl.pallas_call(
        body, name=name, grid=((S + HEAD_ROWS) // (m * SEQ_BLOCK),),
        in_specs=[_ANY] * n_after + [pl.BlockSpec((SEQ_BLOCK, D), lambda i: (0, 0))] + _seq_specs(m, D)
        + [pl.BlockSpec((1, D), lambda i: (0, 0))],
        out_specs=[tile, tile],
        out_shape=[jax.ShapeDtypeStruct((S + HEAD_ROWS, D), F32), jax.ShapeDtypeStruct((S + HEAD_ROWS, D), BF16)],
        compiler_params=_params("parallel"),
    )(*((after,) if n_after else ()), head, *([x] * m), w)


def _embed_norm_bwd(h, w, dn, dres, *, name):
    M, D = h.shape
    S = M - HEAD_ROWS
    m = _seq_blocks_per_tile(S)
    g = S // (m * SEQ_BLOCK)

    def one(x, dn_, dres_, w_):
        r = lax.rsqrt(jnp.mean(x * x, axis=-1, keepdims=True) + NORM_EPS)
        xhat = x * r
        dxhat = dn_ * w_
        dh = dres_ + r * (dxhat - xhat * jnp.mean(dxhat * xhat, axis=-1, keepdims=True))
        return dh, jnp.sum((dn_ * xhat).reshape(SEQ_BLOCK // 8, 8, D), axis=0)

    def body(*refs):
        w_ref = refs[0]
        groups = [refs[1 + a * (m + 1):1 + (a + 1) * (m + 1)] for a in range(3)]
        gx_ref, dhead_ref, dw_ref, acc_ref = refs[1 + 3 * (m + 1):]
        i = pl.program_id(0)
        w_ = w_ref[...]
        part = jnp.zeros((8, D), F32)
        for k in range(m):
            dh, p = one(*(grp[1 + k][...] for grp in groups), w_)
            gx_ref[k * SEQ_BLOCK:(k + 1) * SEQ_BLOCK, :] = dh
            part = part + p

        @pl.when(i == 0)
        def _():
            dh, p = one(*(grp[0][...] for grp in groups), w_)
            dhead_ref[...] = dh
            acc_ref[...] = part + p

        @pl.when(i > 0)
        def _():
            acc_ref[...] += part

        @pl.when(i == g - 1)
        def _():
            dw_ref[...] = jnp.sum(acc_ref[...], axis=0, keepdims=True)

    first = pl.BlockSpec((SEQ_BLOCK, D), lambda i: (0, 0))
    blocks = [pl.BlockSpec((SEQ_BLOCK, D), functools.partial(lambda i, k: (m * i + k + 1, 0), k=k)) for k in range(m)]
    vec = pl.BlockSpec((1, D), lambda i: (0, 0))
    return pl.pallas_call(
        body, name=name, grid=(g,), in_specs=[vec] + ([first] + blocks) * 3,
        out_specs=[pl.BlockSpec((m * SEQ_BLOCK, D), lambda i: (i, 0)), first, vec],
        out_shape=[jax.ShapeDtypeStruct((S, D), F32), jax.ShapeDtypeStruct((SEQ_BLOCK, D), F32),
                   jax.ShapeDtypeStruct((1, D), F32)],
        scratch_shapes=[pltpu.VMEM((8, D), F32)],
        compiler_params=_params("arbitrary"),
    )(w, *([h] * (m + 1)), *([dn] * (m + 1)), *([dres] * (m + 1)))


def _rmsnorm_bwd(h, w, dn, dres, *, name):
    M, D = h.shape
    tm = _tile(M, 344, 16)
    g = M // tm

    def body(h_ref, w_ref, dn_ref, dres_ref, dh_ref, dhb_ref, dw_ref, acc_ref):
        i = pl.program_id(0)
        x = h_ref[...]
        r = lax.rsqrt(jnp.mean(x * x, axis=-1, keepdims=True) + NORM_EPS)
        xhat = x * r
        dn_ = dn_ref[...]
        dxhat = dn_ * w_ref[...]
        dh = dres_ref[...] + r * (dxhat - xhat * jnp.mean(dxhat * xhat, axis=-1, keepdims=True))
        dh_ref[...] = dh
        dhb_ref[...] = dh.astype(dhb_ref.dtype)
        part = jnp.sum((dn_ * xhat).reshape(tm // 8, 8, D), axis=0)

        @pl.when(i == 0)
        def _():
            acc_ref[...] = part

        @pl.when(i > 0)
        def _():
            acc_ref[...] += part

        @pl.when(i == g - 1)
        def _():
            dw_ref[...] = jnp.sum(acc_ref[...], axis=0, keepdims=True)

    row = pl.BlockSpec((tm, D), lambda i: (i, 0))
    vec = pl.BlockSpec((1, D), lambda i: (0, 0))
    return pl.pallas_call(
        body, name=name, grid=(g,), in_specs=[row, vec, row, row],
        out_specs=[row, row, vec],
        out_shape=[jax.ShapeDtypeStruct((M, D), F32), jax.ShapeDtypeStruct((M, D), BF16),
                   jax.ShapeDtypeStruct((1, D), F32)],
        scratch_shapes=[pltpu.VMEM((8, D), F32)],
        compiler_params=_params("arbitrary"),
    )(h, w, dn, dres)


def _loss_head(h, w, target, *, name):
    M, D = h.shape
    m = _seq_blocks_per_tile(M)
    tm = m * SEQ_BLOCK
    g = M // tm

    def body(h_ref, w_ref, *rest):
        t_refs = rest[:m]
        dh_ref, dhb_ref, dw_ref, loss_ref, acc_ref, lacc_ref = rest[m:]
        i = pl.program_id(0)
        x = h_ref[...]
        row = i * tm + lax.broadcasted_iota(jnp.int32, (tm, 1), 0)
        live = row >= HEAD_ROWS
        r = lax.rsqrt(jnp.mean(x * x, axis=-1, keepdims=True) + NORM_EPS)
        xhat = x * r
        t = jnp.concatenate([t_ref[...] for t_ref in t_refs], axis=0)
        err = jnp.where(live, xhat * w_ref[...] - t, 0.0)
        dy = err * (1.0 / D)
        dxhat = dy * w_ref[...]
        dh = r * (dxhat - xhat * jnp.mean(dxhat * xhat, axis=-1, keepdims=True))
        dh_ref[...] = dh
        dhb_ref[...] = dh.astype(dhb_ref.dtype)
        part = jnp.sum((dy * xhat).reshape(tm // 8, 8, D), axis=0)
        lpart = jnp.sum((err * err).reshape(tm // 8, 8, D), axis=0)

        @pl.when(i == 0)
        def _():
            acc_ref[...] = part
            lacc_ref[...] = lpart

        @pl.when(i > 0)
        def _():
            acc_ref[...] += part
            lacc_ref[...] += lpart

        @pl.when(i == g - 1)
        def _():
            dw_ref[...] = jnp.sum(acc_ref[...], axis=0, keepdims=True)
            tot = jnp.sum(jnp.sum(lacc_ref[...], axis=0, keepdims=True), axis=1, keepdims=True)
            loss_ref[...] = jnp.broadcast_to(tot * (0.5 / D), (1, 128))

    row = pl.BlockSpec((tm, D), lambda i: (i, 0))
    vec = pl.BlockSpec((1, D), lambda i: (0, 0))
    return pl.pallas_call(
        body, name=name, grid=(g,), in_specs=[row, vec] + _seq_specs(m, D),
        out_specs=[row, row, vec, pl.BlockSpec((1, 128), lambda i: (0, 0))],
        out_shape=[jax.ShapeDtypeStruct((M, D), F32), jax.ShapeDtypeStruct((M, D), BF16),
                   jax.ShapeDtypeStruct((1, D), F32), jax.ShapeDtypeStruct((1, 128), F32)],
        scratch_shapes=[pltpu.VMEM((8, D), F32), pltpu.VMEM((8, D), F32)],
        compiler_params=_params("arbitrary"),
    )(h, w, *([target] * m))


def _gate_terms(sm, w2p, b2, alog_p, dt_p, row0):
    tm = sm.shape[0]
    lane = lax.broadcasted_iota(jnp.int32, (tm, SM_W), 1)
    live = (row0 + lax.broadcasted_iota(jnp.int32, (tm, 1), 0)) >= ROW_PAD
    pre = sm + dt_p
    neg_a = -jnp.exp(alog_p)
    g = neg_a * _softplus(pre)
    beta = _sigmoid(sm)
    z = _dot(sm, w2p) + b2
    return lane, live, pre, neg_a, g, beta, z


def _gates_fwd(proj, w2p, b2, alog_p, dt_p, *, name):
    M = proj.shape[0]
    tm = _tile(M, 688, 8)

    def body(sm_ref, w2_ref, b2_ref, al_ref, dt_ref, gb_ref, la_ref):
        row0 = pl.program_id(0) * tm
        lane, live, _, _, g, beta, z = _gate_terms(sm_ref[...].astype(F32), w2_ref[...], b2_ref[...], al_ref[...], dt_ref[...], row0)
        gb = jnp.where(lane < GDN_HEADS, g, jnp.where(lane < 2 * GDN_HEADS, beta, 0.0))
        gb_ref[...] = jnp.where(live, gb, 0.0)
        la = (jnp.minimum(z, 0.0) - jnp.log1p(jnp.exp(-jnp.abs(z)))) * (1.0 / GLA_GATE_NORMALIZER)
        la_ref[...] = jnp.where(live, la, 0.0)

    full = lambda s: pl.BlockSpec(s, lambda i: (0, 0))
    return pl.pallas_call(
        body, name=name, grid=(M // tm,),
        in_specs=[pl.BlockSpec((tm, SM_W), lambda i: (i, C_SM // SM_W)), full((SM_W, GLA_QK)), full((1, GLA_QK)),
                  full((1, SM_W)), full((1, SM_W))],
        out_specs=[pl.BlockSpec((tm, SM_W), lambda i: (i, 0)), pl.BlockSpec((tm, GLA_QK), lambda i: (i, 0))],
        out_shape=[jax.ShapeDtypeStruct((M, SM_W), F32), jax.ShapeDtypeStruct((M, GLA_QK), F32)],
        compiler_params=_params("parallel"),
    )(proj, w2p, b2, alog_p, dt_p)


def _gates_bwd(proj, w2p, b2, alog_p, dt_p, dgb_heads, dla, d_proj, *, name):
    M = proj.shape[0]
    tm = _tile(M, 688, 8)
    g_ = M // tm

    tail_w = D_PROJ - C_SM

    def body(sm_ref, w2_ref, b2_ref, al_ref, dt_ref, dgb_ref, dla_ref, _,
             dsm_ref, dw2_ref, db2_ref, dal_ref, ddt_ref):
        i = pl.program_id(0)
        sm = sm_ref[...].astype(F32)
        lane, live, pre, neg_a, g, beta, z = _gate_terms(sm, w2_ref[...], b2_ref[...], al_ref[...], dt_ref[...], i * tm)
        dz = jnp.where(live, dla_ref[...] * (_sigmoid(-z) * (1.0 / GLA_GATE_NORMALIZER)), 0.0)
        dsm_lr = _dot(dz, w2_ref[...], NT)
        dgb = dgb_ref[0]
        for hh in range(1, GDN_HEADS):
            dgb = dgb + dgb_ref[hh]
        dgb = jnp.where(live, dgb, 0.0)
        da = dgb * neg_a * _sigmoid(pre)
        db = dgb * beta * (1.0 - beta)
        dsm = jnp.where(lane < GDN_HEADS, da, jnp.where(lane < 2 * GDN_HEADS, db, dsm_lr))
        dsm_ref[:, 0:SM_W] = dsm.astype(dsm_ref.dtype)
        dsm_ref[:, SM_W:tail_w] = jnp.zeros((tm, tail_w - SM_W), dsm_ref.dtype)
        is_a = lane < GDN_HEADS
        dal = jnp.sum(jnp.where(is_a, dgb * g, 0.0), axis=0, keepdims=True)
        ddt = jnp.sum(jnp.where(is_a, da, 0.0), axis=0, keepdims=True)
        dw2 = _dot(sm, dz, TN)
        db2 = jnp.sum(dz, axis=0, keepdims=True)

        @pl.when(i == 0)
        def _():
            dw2_ref[...] = dw2
            db2_ref[...] = db2
            dal_ref[...] = dal
            ddt_ref[...] = ddt

        @pl.when(i > 0)
        def _():
            dw2_ref[...] += dw2
            db2_ref[...] += db2
            dal_ref[...] += dal
            ddt_ref[...] += ddt

    full = lambda s: pl.BlockSpec(s, lambda i: (0, 0))
    return pl.pallas_call(
        body, name=name, grid=(g_,),
        in_specs=[pl.BlockSpec((tm, SM_W), lambda i: (i, C_SM // SM_W)), full((SM_W, GLA_QK)), full((1, GLA_QK)),
                  full((1, SM_W)), full((1, SM_W)),
                  pl.BlockSpec((GDN_HEADS, tm, SM_W), lambda i: (0, i, 0)),
                  pl.BlockSpec((tm, GLA_QK), lambda i: (i, 0)), _ANY],
        out_specs=[pl.BlockSpec((tm, tail_w), lambda i: (i, C_SM // tail_w)), full((SM_W, GLA_QK)), full((1, GLA_QK)),
                   full((1, SM_W)), full((1, SM_W))],
        out_shape=[jax.ShapeDtypeStruct(d_proj.shape, d_proj.dtype), jax.ShapeDtypeStruct((SM_W, GLA_QK), F32),
                   jax.ShapeDtypeStruct((1, GLA_QK), F32), jax.ShapeDtypeStruct((1, SM_W), F32),
                   jax.ShapeDtypeStruct((1, SM_W), F32)],
        input_output_aliases={7: 0},
        compiler_params=_params("arbitrary"),
    )(proj, w2p, b2, alog_p, dt_p, dgb_heads, dla, d_proj)


QKV_W = GDN_QK
N_QKV_GROUPS = 3
QKV_B0 = C_QKV // QKV_W
HALO = 16


def _conv_terms(x_ref, halo_ref, cw_ref, xs_ref, i, tm):
    xs_ref[HALO:HALO + tm, :] = x_ref[...].astype(F32)
    xs_ref[0:HALO, :] = jnp.where(i > 0, halo_ref[...].astype(F32), 0.0)
    cw = cw_ref[...]
    xs = xs_ref[...]
    taps = [(pltpu.roll(xs, CONV_K - 1 - t, 0) if t < CONV_K - 1 else xs)[HALO:HALO + tm, :] for t in range(CONV_K)]
    c = taps[0] * cw[0:1, :]
    for t in range(1, CONV_K):
        c = c + taps[t] * cw[t:t + 1, :]
    return c, taps


def _prep_fwd(proj, conv_w8, *, name):
    M = proj.shape[0]
    tm = _tile(M, 688, 16)

    def body(x_ref, halo_ref, cw_ref, o_ref, xs_ref):
        j, i = pl.program_id(0), pl.program_id(1)
        c, _ = _conv_terms(x_ref, halo_ref, cw_ref, xs_ref, i, tm)
        s, _ = _silu_and_grad(c)
        scale = jnp.where(j == 0, GDN_DK ** -0.5, 1.0)
        for hh in range(GDN_HEADS):
            cols = slice(hh * 128, (hh + 1) * 128)
            sh = s[:, cols]
            r = lax.rsqrt(jnp.sum(sh * sh, axis=-1, keepdims=True) + NORM_EPS)
            o_ref[:, cols] = jnp.where(j < 2, sh * (r * scale), sh)

    hb = tm // HALO
    return pl.pallas_call(
        body, name=name, grid=(N_QKV_GROUPS, M // tm),
        in_specs=[pl.BlockSpec((tm, QKV_W), lambda j, i: (i, QKV_B0 + j)),
                  pl.BlockSpec((HALO, QKV_W), lambda j, i: (jnp.maximum(i * hb - 1, 0), QKV_B0 + j)),
                  pl.BlockSpec((8, QKV_W), lambda j, i: (0, j))],
        out_specs=pl.BlockSpec((tm, QKV_W), lambda j, i: (i, j)),
        out_shape=jax.ShapeDtypeStruct((M, N_QKV_GROUPS * QKV_W), F32),
        scratch_shapes=[pltpu.VMEM((tm + HALO, QKV_W), F32)],
        compiler_params=_params("parallel", "arbitrary"),
    )(proj, proj, conv_w8)


def _prep_bwd(proj, conv_w8, dact, d_proj, *, name):
    M = proj.shape[0]
    tm = _tile(M, 688, 16)
    g_ = M // tm
    ext = tm + HALO

    def body(x_ref, prev_ref, next_ref, cw_ref, da_ref, dan_ref, _, o_ref, dcw_ref, xs_ref, das_ref, dcs_ref):
        j, i = pl.program_id(0), pl.program_id(1)
        not_last = i < g_ - 1
        xs_ref[0:HALO, :] = jnp.where(i > 0, prev_ref[...].astype(F32), 0.0)
        xs_ref[HALO:HALO + tm, :] = x_ref[...].astype(F32)
        xs_ref[HALO + tm:HALO + ext, :] = jnp.where(not_last, next_ref[...].astype(F32), 0.0)
        das_ref[0:tm, :] = da_ref[...]
        das_ref[tm:ext, :] = jnp.where(not_last, dan_ref[...], 0.0)
        cw = cw_ref[...]
        xs = xs_ref[...]
        taps = [(pltpu.roll(xs, CONV_K - 1 - t, 0) if t < CONV_K - 1 else xs)[HALO:HALO + ext, :] for t in range(CONV_K)]
        c = taps[0] * cw[0:1, :]
        for t in range(1, CONV_K):
            c = c + taps[t] * cw[t:t + 1, :]
        s, ds_dc = _silu_and_grad(c)
        scale = jnp.where(j == 0, GDN_DK ** -0.5, 1.0)
        for hh in range(GDN_HEADS):
            cols = slice(hh * 128, (hh + 1) * 128)
            sh = s[:, cols]
            r = lax.rsqrt(jnp.sum(sh * sh, axis=-1, keepdims=True) + NORM_EPS)
            da = das_ref[:, cols]
            y = sh * r
            dy = da * scale
            ds_norm = r * (dy - y * jnp.sum(dy * y, axis=-1, keepdims=True))
            dcs_ref[:, cols] = jnp.where(j < 2, ds_norm, da) * ds_dc[:, cols]
        dc = dcs_ref[...]
        acc = dc[0:tm, :] * cw[CONV_K - 1:CONV_K, :]
        for t in range(CONV_K - 1):
            acc = acc + pltpu.roll(dc, ext - (CONV_K - 1 - t), 0)[0:tm, :] * cw[t:t + 1, :]
        o_ref[...] = acc.astype(o_ref.dtype)
        r8 = lax.broadcasted_iota(jnp.int32, (8, QKV_W), 0)
        part = jnp.zeros((8, QKV_W), F32)
        for t in range(CONV_K):
            part = jnp.where(r8 == t, jnp.sum(dc[0:tm, :] * taps[t][0:tm, :], axis=0, keepdims=True), part)

        @pl.when(i == 0)
        def _():
            dcw_ref[...] = part

        @pl.when(i > 0)
        def _():
            dcw_ref[...] += part

    hb = tm // HALO
    last = M // HALO - 1
    prev_of = lambda i: jnp.maximum(i * hb - 1, 0)
    next_of = lambda i: jnp.minimum((i + 1) * hb, last)
    return pl.pallas_call(
        body, name=name, grid=(N_QKV_GROUPS, g_),
        in_specs=[pl.BlockSpec((tm, QKV_W), lambda j, i: (i, QKV_B0 + j)),
                  pl.BlockSpec((HALO, QKV_W), lambda j, i: (prev_of(i), QKV_B0 + j)),
                  pl.BlockSpec((HALO, QKV_W), lambda j, i: (next_of(i), QKV_B0 + j)),
                  pl.BlockSpec((8, QKV_W), lambda j, i: (0, j)),
                  pl.BlockSpec((tm, QKV_W), lambda j, i: (i, j)),
                  pl.BlockSpec((HALO, QKV_W), lambda j, i: (next_of(i), j)), _ANY],
        out_specs=[pl.BlockSpec((tm, QKV_W), lambda j, i: (i, QKV_B0 + j)), pl.BlockSpec((8, QKV_W), lambda j, i: (0, j))],
        out_shape=[jax.ShapeDtypeStruct(d_proj.shape, d_proj.dtype),
                   jax.ShapeDtypeStruct((8, N_QKV_GROUPS * QKV_W), F32)],
        input_output_aliases={6: 0},
        scratch_shapes=[pltpu.VMEM((HALO + ext, QKV_W), F32), pltpu.VMEM((ext, QKV_W), F32), pltpu.VMEM((ext, QKV_W), F32)],
        compiler_params=_params("parallel", "arbitrary"),
    )(proj, proj, proj, conv_w8, dact, dact, d_proj)


def _round_robin(gens):
    gens = list(gens)
    while gens:
        alive = []
        for gen in gens:
            try:
                next(gen)
                alive.append(gen)
            except StopIteration:
                pass
        gens = alive


def _unit_lower_inverse(a_low, eye):
    n = a_low.shape[0]
    ri = lax.broadcasted_iota(jnp.int32, (n, n), 0)
    ci = lax.broadcasted_iota(jnp.int32, (n, n), 1)
    same = lambda shift: (ri >> shift) == (ci >> shift)
    b = jnp.where(same(3), -a_low, 0.0)
    x = eye + b
    p2 = _dot3(b, b)
    yield
    x = x + _dot3(x, p2)
    p4 = _dot3(p2, p2)
    yield
    x = x + _dot3(x, p4)
    yield
    for shift in (3, 4, 5):
        between = jnp.where(same(shift + 1) & ~same(shift), a_low, 0.0)
        t = _dot3(between, x)
        yield
        x = x - _dot3(x, t)
        yield
    return x


class _GdnChunk:
    def build(self, q, k, v, gb, h, sum_on_mxu):
        C = GDN_CHUNK
        lane = lax.broadcasted_iota(jnp.int32, (C, SM_W), 1)
        g = jnp.sum(jnp.where(lane == h, gb, 0.0), axis=1, keepdims=True)
        self.beta = jnp.sum(jnp.where(lane == h + GDN_HEADS, gb, 0.0), axis=1, keepdims=True)
        ri = lax.broadcasted_iota(jnp.int32, (C, C), 0)
        ci = lax.broadcasted_iota(jnp.int32, (C, C), 1)
        self.causal = ri >= ci
        self.strict = ri > ci
        self.eye = (ri == ci).astype(F32)
        if sum_on_mxu:
            gcb = lax.dot_general(self.causal.astype(F32), jnp.broadcast_to(g, (C, SM_W)), NN,
                                  precision=lax.Precision.HIGHEST, preferred_element_type=F32)
        else:
            gcb = _running_sum(jnp.broadcast_to(g, (C, SM_W)))
        yield
        self.gcol = gcb[:, 0:1]
        grow = gcb.T[0:1, 0:C]
        self.decay = jnp.exp(jnp.where(self.causal, self.gcol - grow, -1e30))
        self.egc = jnp.exp(self.gcol)
        glast = gcb[C - 1:C, 0:1]
        self.elast = jnp.exp(glast - self.gcol)
        self.gl = jnp.exp(glast)
        self.q, self.k, self.v = q, k, v
        self.kb = k * self.beta
        m = _dot(self.kb, k, NT)
        n_ = _dot(q, k, NT)
        yield
        self.a_low = jnp.where(self.strict, m * self.decay, 0.0)
        self.p = n_ * self.decay
        self.qd = q * self.egc
        self.kd = k * self.elast
        self.bu = v * self.beta
        self.bw = self.kb * self.egc


GDN_HB = 8
GDN_HG = GDN_HEADS // GDN_HB


def _gdn_specs(n_of):
    C, W = GDN_CHUNK, 128 * GDN_HB
    q_spec = pl.BlockSpec((C, W), lambda g, n: (n_of(n), g))
    k_spec = pl.BlockSpec((C, W), lambda g, n: (n_of(n), g + GDN_HG))
    v_spec = pl.BlockSpec((C, W), lambda g, n: (n_of(n), g + 2 * GDN_HG))
    gb_spec = pl.BlockSpec((C, SM_W), lambda g, n: (n_of(n), 0))
    o_spec = pl.BlockSpec((C, W), lambda g, n: (n_of(n), g))
    s_spec = pl.BlockSpec((GDN_HB, None, GDN_DK, GDN_DV), lambda g, n: (g, n_of(n), 0, 0))
    t_spec = pl.BlockSpec((GDN_HB, None, C, C), lambda g, n: (g, n_of(n), 0, 0))
    return q_spec, k_spec, v_spec, gb_spec, o_spec, s_spec, t_spec


def _gdn_fwd(act, gb, *, name):
    M = act.shape[0]
    N = M // GDN_CHUNK

    def body(q_ref, k_ref, v_ref, gb_ref, o_ref, s_ref, t_ref, state):
        g, n = pl.program_id(0), pl.program_id(1)

        @pl.when(n == 0)
        def _():
            state[...] = jnp.zeros_like(state)

        gb_ = gb_ref[...]

        def head(hh):
            cols = slice(hh * 128, (hh + 1) * 128)
            c = _GdnChunk()
            yield from c.build(q_ref[:, cols], k_ref[:, cols], v_ref[:, cols], gb_, g * GDN_HB + hh, sum_on_mxu=True)
            tinv = yield from _unit_lower_inverse(c.a_low, c.eye)
            s = state[hh]
            s_ref[hh] = s
            t_ref[hh] = tinv
            u = _dot(tinv, c.bu)
            w = _dot(tinv, c.bw)
            yield
            vn = u - _dot(w, s)
            o1 = _dot(c.qd, s)
            yield
            o_ref[:, cols] = (o1 + _dot(c.p, vn)).astype(o_ref.dtype)
            state[hh] = c.gl * s + _dot(c.kd, vn, TN)

        _round_robin(head(hh) for hh in range(GDN_HB))

    q_spec, k_spec, v_spec, gb_spec, o_spec, s_spec, t_spec = _gdn_specs(lambda n: n)
    return pl.pallas_call(
        body, name=name, grid=(GDN_HG, N),
        in_specs=[q_spec, k_spec, v_spec, gb_spec], out_specs=[o_spec, s_spec, t_spec],
        out_shape=[jax.ShapeDtypeStruct((M, GDN_V), BF16),
                   jax.ShapeDtypeStruct((GDN_HEADS, N, GDN_DK, GDN_DV), F32),
                   jax.ShapeDtypeStruct((GDN_HEADS, N, GDN_CHUNK, GDN_CHUNK), F32)],
        scratch_shapes=[pltpu.VMEM((GDN_HB, GDN_DK, GDN_DV), F32)],
        compiler_params=_params("parallel", "arbitrary"),
    )(act, act, act, gb)


def _gdn_bwd(act, gb, do, s_all, t_all, *, name):
    M = act.shape[0]
    N = M // GDN_CHUNK
    C = GDN_CHUNK
    assert GDN_HG == 1

    def body(q_ref, k_ref, v_ref, gb_ref, do_ref, s_ref, t_ref, dact_ref, dgb_ref, dstate):
        g, n = pl.program_id(0), pl.program_id(1)

        @pl.when(n == 0)
        def _():
            dstate[...] = jnp.zeros_like(dstate)

        gb_ = gb_ref[...]
        last = lax.broadcasted_iota(jnp.int32, (C, 1), 0) == C - 1
        lane = lax.broadcasted_iota(jnp.int32, (C, SM_W), 1)
        def head(hh):
            cols = slice(hh * 128, (hh + 1) * 128)
            h = g * GDN_HB + hh
            c = _GdnChunk()
            yield from c.build(q_ref[:, cols], k_ref[:, cols], v_ref[:, cols], gb_, h, sum_on_mxu=False)
            tinv = t_ref[hh]
            tinv_t = tinv.T
            s = s_ref[hh]
            do_ = do_ref[:, cols]
            ds1 = dstate[hh]
            u = _dot(tinv, c.bu)
            w = _dot(tinv, c.bw)
            dqd = _dot(do_, s, NT)
            dvn0 = _dot(c.p, do_, TN) + _dot(c.kd, ds1)
            dst0 = _dot(c.qd, do_, TN) + c.gl * ds1
            yield
            vn = u - _dot(w, s)
            dvn = dvn0
            yield
            dp = jnp.where(c.causal, _dot(do_, vn, NT), 0.0)
            dstate[hh] = dst0 - _dot(w, dvn, TN)
            dkd = _dot(vn, ds1, NT)
            dw = -_dot(dvn, s, NT)
            dbu = _dot(tinv_t, dvn)
            dgl = jnp.sum(jnp.sum(s * ds1, axis=1, keepdims=True), axis=0, keepdims=True)
            yield
            dbw = _dot(tinv_t, dw)
            t1 = _dot(dbu, u, NT)
            yield
            da = jnp.where(c.strict, -(t1 + _dot(dbw, w, NT)), 0.0)
            dn_ = dp * c.decay
            dq0 = _dot(dn_, c.k)
            dk0 = _dot(dn_, c.q, TN)
            yield
            dm = da * c.decay
            e = da * c.a_low + dp * c.p
            dkb = _dot(dm, c.k) + dbw * c.egc
            dact_ref[:, GDN_QK + hh * 128:GDN_QK + (hh + 1) * 128] = (
                _dot(dm, c.kb, TN) + dk0 + dkb * c.beta + dkd * c.elast)
            dact_ref[:, cols] = dq0 + dqd * c.egc
            dact_ref[:, 2 * GDN_QK + hh * 128:2 * GDN_QK + (hh + 1) * 128] = dbu * c.beta
            dbeta = jnp.sum(dbu * c.v, axis=1, keepdims=True) + jnp.sum(dkb * c.k, axis=1, keepdims=True)
            t_kd = jnp.sum(dkd * c.kd, axis=1, keepdims=True)
            dgc = (jnp.sum(e, axis=1, keepdims=True) - jnp.sum(e.T, axis=1, keepdims=True)
                   + jnp.sum(dbw * c.bw, axis=1, keepdims=True) + jnp.sum(dqd * c.qd, axis=1, keepdims=True) - t_kd)
            dgc = dgc + jnp.where(last, jnp.sum(t_kd, axis=0, keepdims=True) + dgl * c.gl, 0.0)
            yield
            dg = _running_sum(jnp.broadcast_to(dgc, (C, SM_W)), reverse=True)
            dgb_ref[hh] = jnp.where(lane == h, dg, jnp.where(lane == h + GDN_HEADS, dbeta, 0.0))

        _round_robin(head(hh) for hh in range(GDN_HB))

    rev = lambda n: N - 1 - n
    q_spec, k_spec, v_spec, gb_spec, o_spec, s_spec, t_spec = _gdn_specs(rev)
    dgb_spec = pl.BlockSpec((GDN_HB, C, SM_W), lambda g, n: (g, rev(n), 0))
    return pl.pallas_call(
        body, name=name, grid=(GDN_HG, N),
        in_specs=[q_spec, k_spec, v_spec, gb_spec, o_spec, s_spec, t_spec],
        out_specs=[pl.BlockSpec((C, 2 * GDN_QK + GDN_V), lambda g, n: (rev(n), 0)), dgb_spec],
        out_shape=[jax.ShapeDtypeStruct((M, 2 * GDN_QK + GDN_V), F32),
                   jax.ShapeDtypeStruct((GDN_HEADS, M, SM_W), F32)],
        scratch_shapes=[pltpu.VMEM((GDN_HB, GDN_DK, GDN_DV), F32)],
        compiler_params=_params("parallel", "arbitrary"),
    )(act, act, act, gb, do, s_all, t_all)


GLA_STEP_ROWS = 64
GLA_SUB = GLA_STEP_ROWS // GLA_CHUNK


def _gla_cumsum(la):
    return _running_sum(la)


GLA_HALF = GLA_CHUNK // 2


def _gla_cross_factors(b):
    top = lax.broadcasted_iota(jnp.int32, b.shape, 0) < GLA_HALF
    bm = b[GLA_HALF - 1:GLA_HALF, :]
    late = jnp.where(top, 0.0, jnp.exp(jnp.minimum(b - bm, 0.0)))
    early = jnp.where(top, jnp.exp(jnp.minimum(bm - b, 0.0)), 0.0)
    return late, early


def _gla_half_decay(bh, ii):
    rj = lax.broadcasted_iota(jnp.int32, bh.shape, 0)
    return jnp.where(rj <= ii, jnp.exp(jnp.minimum(bh[ii:ii + 1, :] - bh, 0.0)), 0.0)


def _gla_scores_t(q, k, b):
    C, H = GLA_CHUNK, GLA_HALF
    lane = lax.broadcasted_iota(jnp.int32, (H, C), 1)
    halves = []
    for h0 in (0, H):
        qh, kh, bh = q[h0:h0 + H], k[h0:h0 + H], b[h0:h0 + H]
        sth = jnp.zeros((H, C), F32)
        for ii in range(H):
            si = jnp.sum(qh[ii:ii + 1, :] * kh * _gla_half_decay(bh, ii), axis=1, keepdims=True)
            sth = jnp.where(lane == h0 + ii, si, sth)
            if ii % 4 == 3:
                yield
        halves.append(sth)
    late, early = _gla_cross_factors(b)
    between = _dot(k * early, q * late, NT)
    yield
    return jnp.concatenate(halves, axis=0) + between


def _gla_specs(n_of):
    R = GLA_STEP_ROWS
    q_spec = pl.BlockSpec((R, GLA_QK), lambda n: (n_of(n), C_GQ // GLA_QK))
    k_spec = pl.BlockSpec((R, GLA_QK), lambda n: (n_of(n), C_GK // GLA_QK))
    v_spec = pl.BlockSpec((R, GLA_V), lambda n: (n_of(n), C_GV // GLA_V))
    la_spec = pl.BlockSpec((R, GLA_QK), lambda n: (n_of(n), 0))
    o_spec = pl.BlockSpec((R, GLA_V), lambda n: (n_of(n), 0))
    s_spec = pl.BlockSpec((GLA_HEADS, None, GLA_SUB, GLA_DV, GLA_DK), lambda n: (0, n_of(n), 0, 0, 0))
    return q_spec, k_spec, v_spec, la_spec, o_spec, s_spec


def _gla_fwd(proj, la, *, name):
    M = proj.shape[0]
    N = M // GLA_STEP_ROWS
    C = GLA_CHUNK

    def body(q_ref, k_ref, v_ref, la_ref, o_ref, s_ref, state):
        n = pl.program_id(0)

        @pl.when(n == 0)
        def _():
            state[...] = jnp.zeros_like(state)

        local = {}

        def within(hh, c):
            kc = slice(hh * GLA_DK, (hh + 1) * GLA_DK)
            vc = slice(hh * GLA_DV, (hh + 1) * GLA_DV)
            rows = slice(c * C, (c + 1) * C)
            q = q_ref[rows, kc].astype(F32) * (GLA_DK ** -0.5)
            k = k_ref[rows, kc].astype(F32)
            v = v_ref[rows, vc].astype(F32)
            b = _gla_cumsum(la_ref[rows, kc])
            yield
            blast = b[C - 1:C, :]
            sc_t = yield from _gla_scores_t(q, k, b)
            kv = _dot(v, k * jnp.exp(blast - b), TN)
            o2 = _dot(sc_t, v, TN)
            yield
            local[hh, c] = (q * jnp.exp(b), jnp.exp(blast), kv, o2)

        def across(hh):
            vc = slice(hh * GLA_DV, (hh + 1) * GLA_DV)
            st = state[hh]
            for c in range(GLA_SUB):
                qe, eblast, kv, o2 = local[hh, c]
                s_ref[hh, c] = st
                o1 = _dot(qe, st, NT)
                yield
                o_ref[c * C:(c + 1) * C, vc] = (o1 + o2).astype(o_ref.dtype)
                st = st * eblast + kv
            state[hh] = st

        _round_robin(within(hh, c) for c in range(GLA_SUB) for hh in range(GLA_HEADS))
        _round_robin(across(hh) for hh in range(GLA_HEADS))

    q_spec, k_spec, v_spec, la_spec, o_spec, s_spec = _gla_specs(lambda n: n)
    return pl.pallas_call(
        body, name=name, grid=(N,),
        in_specs=[q_spec, k_spec, v_spec, la_spec], out_specs=[o_spec, s_spec],
        out_shape=[jax.ShapeDtypeStruct((M, GLA_V), BF16),
                   jax.ShapeDtypeStruct((GLA_HEADS, N, GLA_SUB, GLA_DV, GLA_DK), F32)],
        scratch_shapes=[pltpu.VMEM((GLA_HEADS, GLA_DV, GLA_DK), F32)],
        compiler_params=_params("arbitrary"),
    )(proj, proj, proj, la)


def _gla_bwd(proj, la, do, s_all, d_proj, *, name):
    M = proj.shape[0]
    N = M // GLA_STEP_ROWS
    C = GLA_CHUNK
    qkv_w = 2 * GLA_QK + GLA_V
    assert C_GK == C_GQ + GLA_QK and C_GV == C_GK + GLA_QK and C_GQ % qkv_w == 0

    def body(q_ref, k_ref, v_ref, la_ref, do_ref, s_ref, _, dp_ref, dla_ref, dstate):
        n = pl.program_id(0)

        @pl.when(n == 0)
        def _():
            dstate[...] = jnp.zeros_like(dstate)

        H = GLA_HALF
        lane = lax.broadcasted_iota(jnp.int32, (C, C), 1)
        row = lax.broadcasted_iota(jnp.int32, (C, C), 0)
        ri = lax.broadcasted_iota(jnp.int32, (C, GLA_DK), 0)
        lane_h = lax.broadcasted_iota(jnp.int32, (H, C), 1)
        ri_h = lax.broadcasted_iota(jnp.int32, (H, GLA_DK), 0)
        cross = (row < H) & (lane >= H)
        def head(hh):
            kc = slice(hh * GLA_DK, (hh + 1) * GLA_DK)
            vc = slice(hh * GLA_DV, (hh + 1) * GLA_DV)
            ds1 = dstate[hh]
            for c in reversed(range(GLA_SUB)):
                rows = slice(c * C, (c + 1) * C)
                q = q_ref[rows, kc].astype(F32) * (GLA_DK ** -0.5)
                k = k_ref[rows, kc].astype(F32)
                v = v_ref[rows, vc].astype(F32)
                b = _gla_cumsum(la_ref[rows, kc])
                do_ = do_ref[rows, vc]
                st = s_ref[hh, c]
                dsc_t = _dot(v, do_, NT)
                dqe = _dot(do_, st)
                dke = _dot(v, ds1)
                yield
                blast = b[C - 1:C, :]
                eb = jnp.exp(b)
                elast = jnp.exp(blast - b)
                eblast = jnp.exp(blast)
                qe = q * eb
                ke = k * elast
                dv2 = _dot(ke, ds1, NT)
                ds_new = _dot(do_, qe, TN)
                deblast = jnp.sum(st * ds1, axis=0, keepdims=True)
                sc_halves, dq_halves, dk_halves = [], [], []
                for h0 in (0, H):
                    qh, kh, bh, dsch = q[h0:h0 + H], k[h0:h0 + H], b[h0:h0 + H], dsc_t[h0:h0 + H]
                    sch = jnp.zeros((H, C), F32)
                    dqh = jnp.zeros((H, GLA_DK), F32)
                    dkh = jnp.zeros((H, GLA_DK), F32)
                    for ii in range(H):
                        f = _gla_half_decay(bh, ii)
                        kf = kh * f
                        si = jnp.sum(qh[ii:ii + 1, :] * kf, axis=1, keepdims=True)
                        sch = jnp.where(lane_h == h0 + ii, si, sch)
                        dsi = jnp.sum(jnp.where(lane_h == h0 + ii, dsch, 0.0), axis=1, keepdims=True)
                        dqh = jnp.where(ri_h == ii, jnp.sum(dsi * kf, axis=0, keepdims=True), dqh)
                        dkh = dkh + (dsi * f) * qh[ii:ii + 1, :]
                        if ii % 4 == 3:
                            yield
                    sc_halves.append(sch)
                    dq_halves.append(dqh)
                    dk_halves.append(dkh)
                late, early = _gla_cross_factors(b)
                q_late, k_early = q * late, k * early
                dsc_x = jnp.where(cross, dsc_t, 0.0)
                sc_t = jnp.concatenate(sc_halves, axis=0) + _dot(k_early, q_late, NT)
                dq_sc = jnp.concatenate(dq_halves, axis=0) + _dot(dsc_x, k_early, TN) * late
                dk_sc = jnp.concatenate(dk_halves, axis=0) + _dot(dsc_x, q_late) * early
                yield
                dv1 = _dot(sc_t, do_)
                dp_ref[rows, kc] = ((dq_sc + dqe * eb) * (GLA_DK ** -0.5)).astype(dp_ref.dtype)
                dp_ref[rows, GLA_QK + hh * GLA_DK:GLA_QK + (hh + 1) * GLA_DK] = (dk_sc + dke * elast).astype(dp_ref.dtype)
                t_ke = dke * ke
                db = q * dq_sc - k * dk_sc + dqe * qe - t_ke
                db = db + jnp.where(ri == C - 1, jnp.sum(t_ke, axis=0, keepdims=True) + deblast * eblast, 0.0)
                dla = _running_sum(db, reverse=True)
                yield
                dp_ref[rows, 2 * GLA_QK + hh * GLA_DV:2 * GLA_QK + (hh + 1) * GLA_DV] = (dv1 + dv2).astype(dp_ref.dtype)
                dla_ref[rows, kc] = dla
                ds1 = ds1 * eblast + ds_new
            dstate[hh] = ds1

        _round_robin(head(hh) for hh in range(GLA_HEADS))

    rev = lambda n: N - 1 - n
    q_spec, k_spec, v_spec, la_spec, o_spec, s_spec = _gla_specs(rev)
    return pl.pallas_call(
        body, name=name, grid=(N,),
        in_specs=[q_spec, k_spec, v_spec, la_spec, o_spec, s_spec, _ANY],
        out_specs=[pl.BlockSpec((GLA_STEP_ROWS, qkv_w), lambda n: (rev(n), C_GQ // qkv_w)), la_spec],
        out_shape=[jax.ShapeDtypeStruct(d_proj.shape, d_proj.dtype), jax.ShapeDtypeStruct((M, GLA_QK), F32)],
        input_output_aliases={6: 0},
        scratch_shapes=[pltpu.VMEM((GLA_HEADS, GLA_DV, GLA_DK), F32)],
        compiler_params=_params("arbitrary"),
    )(proj, proj, proj, la, do, s_all, d_proj)


def _head_norm(o, wn):
    r = lax.rsqrt(jnp.mean(o * o, axis=-1, keepdims=True) + NORM_EPS)
    return o * r, r


def _mix_heads():
    heads = [(0, GDN_DV, hh * GDN_DV, hh * GDN_DV) for hh in range(GDN_HEADS)]
    heads += [(1, GLA_DV, GDN_V + hh * GLA_DV, hh * GLA_DV) for hh in range(GLA_HEADS)]
    return heads


def _mix_fwd(o_gdn, o_gla, proj, wn_gdn, wn_gla, *, name):
    M = proj.shape[0]
    tm = _tile(M, 344, 16)

    def body(og_ref, ol_ref, z_ref, r_ref, wg_ref, wl_ref, m_ref):
        srcs = ((og_ref, z_ref, wg_ref), (ol_ref, r_ref, wl_ref))
        for grp, width, mcol, col in _mix_heads():
            o_ref, gate_ref, w_ref = srcs[grp]
            xhat, _ = _head_norm(o_ref[:, col:col + width].astype(F32), None)
            gate, _ = _silu_and_grad(gate_ref[:, col:col + width].astype(F32))
            m_ref[:, mcol:mcol + width] = (xhat * w_ref[...] * gate).astype(m_ref.dtype)

    full = lambda s: pl.BlockSpec(s, lambda i: (0, 0))
    return pl.pallas_call(
        body, name=name, grid=(M // tm,),
        in_specs=[pl.BlockSpec((tm, GDN_V), lambda i: (i, 0)), pl.BlockSpec((tm, GLA_V), lambda i: (i, 0)),
                  pl.BlockSpec((tm, GDN_V), lambda i: (i, C_Z // GDN_V)),
                  pl.BlockSpec((tm, GLA_V), lambda i: (i, C_GR // GLA_V)),
                  full((1, GDN_DV)), full((1, GLA_DV))],
        out_specs=pl.BlockSpec((tm, D_MODEL), lambda i: (i, 0)),
        out_shape=jax.ShapeDtypeStruct((M, D_MODEL), BF16),
        compiler_params=_params("parallel"),
    )(o_gdn, o_gla, proj, proj, wn_gdn, wn_gla)


def _mix_bwd(o_gdn, o_gla, proj, wn_gdn, wn_gla, dmixed, *, name):
    M = proj.shape[0]
    tm = _tile(M, 344, 16)
    g_ = M // tm
    assert C_Z == 0 and C_GR == GDN_V

    def body(og_ref, ol_ref, z_ref, r_ref, wg_ref, wl_ref, dm_ref,
             dog_ref, dol_ref, dzr_ref, dwg_ref, dwl_ref):
        i = pl.program_id(0)
        srcs = ((og_ref, z_ref, wg_ref, dog_ref), (ol_ref, r_ref, wl_ref, dol_ref))
        dws = [jnp.zeros((1, GDN_DV), F32), jnp.zeros((1, GLA_DV), F32)]
        for grp, width, mcol, col in _mix_heads():
            o_ref, gate_ref, w_ref, do_ref = srcs[grp]
            cols = slice(col, col + width)
            xhat, r = _head_norm(o_ref[:, cols].astype(F32), None)
            gate, dgate_dc = _silu_and_grad(gate_ref[:, cols].astype(F32))
            dm = dm_ref[:, mcol:mcol + width]
            dzr_ref[:, mcol:mcol + width] = (dm * xhat * w_ref[...] * dgate_dc).astype(dzr_ref.dtype)
            dnorm = dm * gate
            dws[grp] = dws[grp] + jnp.sum(dnorm * xhat, axis=0, keepdims=True)
            dxhat = dnorm * w_ref[...]
            do_ref[:, cols] = r * (dxhat - xhat * jnp.mean(dxhat * xhat, axis=-1, keepdims=True))

        @pl.when(i == 0)
        def _():
            dwg_ref[...] = dws[0]
            dwl_ref[...] = dws[1]

        @pl.when(i > 0)
        def _():
            dwg_ref[...] += dws[0]
            dwl_ref[...] += dws[1]

    full = lambda s: pl.BlockSpec(s, lambda i: (0, 0))
    half = pl.BlockSpec((tm, GDN_V), lambda i: (i, 0))
    return pl.pallas_call(
        body, name=name, grid=(g_,),
        in_specs=[half, half, pl.BlockSpec((tm, GDN_V), lambda i: (i, C_Z // GDN_V)),
                  pl.BlockSpec((tm, GLA_V), lambda i: (i, C_GR // GLA_V)),
                  full((1, GDN_DV)), full((1, GLA_DV)), pl.BlockSpec((tm, D_MODEL), lambda i: (i, 0))],
        out_specs=[half, half, pl.BlockSpec((tm, GDN_V + GLA_V), lambda i: (i, 0)),
                   full((1, GDN_DV)), full((1, GLA_DV))],
        out_shape=[jax.ShapeDtypeStruct((M, GDN_V), F32), jax.ShapeDtypeStruct((M, GLA_V), F32),
                   jax.ShapeDtypeStruct((M, D_PROJ), BF16),
                   jax.ShapeDtypeStruct((1, GDN_DV), F32), jax.ShapeDtypeStruct((1, GLA_DV), F32)],
        compiler_params=_params("arbitrary"),
    )(o_gdn, o_gla, proj, proj, wn_gdn, wn_gla, dmixed)


def _row_chunks(tm, parts=2):
    if tm % (16 * parts):
        return [slice(0, tm)]
    return [slice(p * (tm // parts), (p + 1) * (tm // parts)) for p in range(parts)]


def _swiglu_fwd(n, w_gate_t, w_up_t, *, name, tm=1376, tn=512):
    M, D = n.shape
    F = w_gate_t.shape[0]
    tm, tn = _tile(M, tm, 16), _tile(F, tn, 128)

    def body(n_ref, wg_ref, wu_ref, g_ref, u_ref, a_ref):
        wg, wu = wg_ref[...], wu_ref[...]
        for rows in _row_chunks(tm):
            x = n_ref[rows, :]
            g = _dot(x, wg, NT)
            u = _dot(x, wu, NT)
            s, _ = _silu_and_grad(g)
            g_ref[rows, :] = g.astype(g_ref.dtype)
            u_ref[rows, :] = u.astype(u_ref.dtype)
            a_ref[rows, :] = (s * u).astype(a_ref.dtype)

    w_spec = pl.BlockSpec((tn, D), lambda i, j: (j, 0))
    o_spec = pl.BlockSpec((tm, tn), lambda i, j: (i, j))
    return pl.pallas_call(
        body, name=name, grid=(M // tm, F // tn),
        in_specs=[pl.BlockSpec((tm, D), lambda i, j: (i, 0)), w_spec, w_spec], out_specs=[o_spec] * 3,
        out_shape=[jax.ShapeDtypeStruct((M, F), BF16)] * 3, compiler_params=_params("parallel", "parallel"),
    )(n, w_gate_t, w_up_t)


def _swiglu_bwd(dh, w_down, gate, up, *, name, after=None, tm=1376, tn=512):
    M, D = dh.shape
    F = w_down.shape[0]
    tm, tn = _tile(M, tm, 16), _tile(F, tn, 128)
    n_after = 0 if after is None else 1

    def body(*refs):
        dh_ref, w_ref, g_ref, u_ref, dg_ref, du_ref = refs[n_after:]
        w = w_ref[...]
        for rows in _row_chunks(tm):
            da = _dot(dh_ref[rows, :], w, NT)
            s, ds = _silu_and_grad(g_ref[rows, :].astype(F32))
            dg_ref[rows, :] = (da * u_ref[rows, :].astype(F32) * ds).astype(dg_ref.dtype)
            du_ref[rows, :] = (da * s).astype(du_ref.dtype)

    o_spec = pl.BlockSpec((tm, tn), lambda i, j: (i, j))
    return pl.pallas_call(
        body, name=name, grid=(M // tm, F // tn),
        in_specs=[_ANY] * n_after + [pl.BlockSpec((tm, D), lambda i, j: (i, 0)),
                                     pl.BlockSpec((tn, D), lambda i, j: (j, 0)), o_spec, o_spec],
        out_specs=[o_spec, o_spec], out_shape=[jax.ShapeDtypeStruct((M, F), BF16)] * 2,
        compiler_params=_params("parallel", "parallel"),
    )(*((after,) if n_after else ()), dh, w_down, gate, up)


def _adamw_update(w, g, m, v):
    nm = ADAM_B1 * m + (1.0 - ADAM_B1) * g
    nv = ADAM_B2 * v + (1.0 - ADAM_B2) * (g * g)
    m_hat = nm / (1.0 - ADAM_B1 ** ADAM_STEP)
    v_hat = nv / (1.0 - ADAM_B2 ** ADAM_STEP)
    return -ADAM_LR * (m_hat / (jnp.sqrt(v_hat) + ADAM_EPS) + ADAM_WD * w), nm, nv


def _adamw(w, g, m, v, *, name):
    shape = w.shape
    cols = shape[-1]
    rows = w.size // cols
    w2, g2, m2, v2 = (t.reshape(rows, cols) for t in (w, g, m, v))
    if rows % 8 == 0 or cols % 128 != 0:
        tr, tc = (_tile(rows, 256, 8) if rows % 8 == 0 else rows), cols
    else:
        tr, tc = rows, _tile(cols, 256, 128)

    def body(w_ref, g_ref, m_ref, v_ref, d_ref, nm_ref, nv_ref):
        d_ref[...], nm_ref[...], nv_ref[...] = _adamw_update(w_ref[...], g_ref[...], m_ref[...], v_ref[...])

    blk = pl.BlockSpec((tr, tc), lambda i, j: (i, j))
    outs = pl.pallas_call(
        body, name=name, grid=(rows // tr, cols // tc), in_specs=[blk] * 4, out_specs=[blk] * 3,
        out_shape=[jax.ShapeDtypeStruct((rows, cols), F32)] * 3, compiler_params=_params("parallel", "parallel"),
    )(w2, g2, m2, v2)
    return tuple(t.reshape(shape) for t in outs)


def _sum_slabs(x, *, name):
    _, R, C = x.shape
    sub = 16 if x.dtype == BF16 else 8
    if R % sub == 0:
        tr, tc = _tile(R, 128, sub), C
    else:
        tr, tc = R, _tile(C, 256, 128)

    def body(x_ref, o_ref):
        acc = x_ref[0].astype(F32)
        for s in range(1, N_DEV):
            acc = acc + x_ref[s].astype(F32)
        o_ref[...] = acc

    return pl.pallas_call(
        body, name=name, grid=(R // tr, C // tc),
        in_specs=[pl.BlockSpec((N_DEV, tr, tc), lambda i, j: (0, i, j))],
        out_specs=pl.BlockSpec((tr, tc), lambda i, j: (i, j)),
        out_shape=jax.ShapeDtypeStruct((R, C), F32), compiler_params=_params("parallel", "parallel"),
    )(x)


def _sum_adamw(x, w, m, v, *, name):
    _, R, C = x.shape
    if R % 16 == 0:
        tr, tc = _tile(R, 128, 16), C
    else:
        tr, tc = R, _tile(C, 256, 128)

    def body(x_ref, w_ref, m_ref, v_ref, g_ref, d_ref, nm_ref, nv_ref):
        g = x_ref[0].astype(F32)
        for s in range(1, N_DEV):
            g = g + x_ref[s].astype(F32)
        g_ref[...] = g
        d_ref[...], nm_ref[...], nv_ref[...] = _adamw_update(w_ref[...], g, m_ref[...], v_ref[...])

    blk = pl.BlockSpec((tr, tc), lambda i, j: (i, j))
    return pl.pallas_call(
        body, name=name, grid=(R // tr, C // tc),
        in_specs=[pl.BlockSpec((N_DEV, tr, tc), lambda i, j: (0, i, j)), blk, blk, blk], out_specs=[blk] * 4,
        out_shape=[jax.ShapeDtypeStruct((R, C), F32)] * 4, compiler_params=_params("parallel", "parallel"),
    )(x, w, m, v)


def _peers():
    x, y, c = lax.axis_index("x"), lax.axis_index("y"), lax.axis_index("c")
    me = 4 * x + 2 * y + c
    peers = []
    for k in range(1, N_DEV):
        px = 1 - x if k & 4 else x
        py = 1 - y if k & 2 else y
        pc = 1 - c if k & 1 else c
        peers.append(((px, py, pc), 4 * px + 2 * py + pc))
    return me, peers


def _gather(x, *, name):
    def body(x_ref, o_ref, send_sems, recv_sems, own_sem):
        me, peers = _peers()
        own = pltpu.make_async_copy(x_ref, o_ref.at[me], own_sem)
        own.start()
        sends, recvs = [], []
        for k, (pos, idx) in enumerate(peers):
            sends.append(pltpu.make_async_remote_copy(
                src_ref=x_ref, dst_ref=o_ref.at[me], send_sem=send_sems.at[k], recv_sem=recv_sems.at[k],
                device_id=pos, device_id_type=pl.DeviceIdType.MESH))
            recvs.append(pltpu.make_async_remote_copy(
                src_ref=x_ref, dst_ref=o_ref.at[idx], send_sem=send_sems.at[k], recv_sem=recv_sems.at[k],
                device_id=pos, device_id_type=pl.DeviceIdType.MESH))
        for cp in sends:
            cp.start()
        for cp in recvs:
            cp.wait_recv()
        for cp in sends:
            cp.wait_send()
        own.wait()

    hbm = pl.BlockSpec(memory_space=pltpu.HBM)
    return pl.pallas_call(
        body, name=name, in_specs=[hbm], out_specs=hbm,
        out_shape=jax.ShapeDtypeStruct((N_DEV,) + tuple(x.shape), x.dtype),
        scratch_shapes=[pltpu.SemaphoreType.DMA((N_DEV - 1,)), pltpu.SemaphoreType.DMA((N_DEV - 1,)),
                        pltpu.SemaphoreType.DMA],
    )(x)


_HBM = pl.BlockSpec(memory_space=pltpu.HBM)
_SEM = pl.BlockSpec(memory_space=pltpu.SEMAPHORE)
_EFFECT = pltpu.SideEffectType.DATAFLOW_SIDE_EFFECTING


PLAN_GATHER = tuple((k, "x", 0) for k in range(1, N_DEV))
PLAN_SCATTER = tuple((k, "xk", 0) for k in range(1, N_DEV))
PLAN_GATHER_CHIPS = tuple((k, "x", 0) for k in (1, 2, 4, 6))
PLAN_GATHER_PASS_ON = tuple((1, ("land", q), q) for q in (2, 4, 6))


def _plan_refs(plan, j, x_ref, land_ref, me, peers, receiving):
    k, source, r = plan[j]
    index_of = lambda q: me if q == 0 else peers[q - 1][1]
    pos, target = peers[k - 1]
    if source == "x":
        src = x_ref
    elif source == "xk":
        src = x_ref.at[target]
    else:
        src = land_ref.at[index_of(source[1])]
    return pos, src, land_ref.at[index_of(k ^ r) if receiving else index_of(r)]


def _exchange_start(x, *, plan, name, after=None, land=None, slab=None):
    n_after = 0 if after is None else 1
    n = len(plan)

    def body(*refs):
        x_ref, land_ref, send_sems, recv_sems, _, _, token = refs[n_after:]
        me, peers = _peers()
        for j in range(n):
            pos, src, dst = _plan_refs(plan, j, x_ref, land_ref, me, peers, receiving=False)
            pltpu.make_async_remote_copy(src_ref=src, dst_ref=dst, send_sem=send_sems.at[j], recv_sem=recv_sems.at[j],
                                         device_id=pos, device_id_type=pl.DeviceIdType.MESH).start()
        token[...] = jnp.zeros_like(token)

    if land is None:
        land = lax.empty((N_DEV,) + tuple(slab), x.dtype)
    return pl.pallas_call(
        body, name=name,
        out_shape=(pltpu.SemaphoreType.DMA((n,)), pltpu.SemaphoreType.DMA((n,)),
                   pltpu.HBM(x.shape, x.dtype), pltpu.HBM(land.shape, land.dtype), jax.ShapeDtypeStruct((8, 128), F32)),
        in_specs=[_ANY] * n_after + [_HBM, _HBM],
        out_specs=(_SEM, _SEM, _HBM, _HBM, pl.BlockSpec(memory_space=pltpu.VMEM)),
        input_output_aliases={n_after: 2, n_after + 1: 3},
        compiler_params=pltpu.CompilerParams(has_side_effects=_EFFECT),
    )(*((after,) if n_after else ()), pltpu.with_memory_space_constraint(x, pltpu.HBM),
      pltpu.with_memory_space_constraint(land, pltpu.HBM))


def _exchange_wait(handle, after, *, plan, name):
    send_sems, recv_sems, x_thru, land_thru, _ = handle
    afters = list(after) if isinstance(after, (list, tuple)) else [after]

    def body(x_ref, land_ref, send_sems, recv_sems, *rest):
        me, peers = _peers()
        for j in range(len(plan)):
            pos, src, dst = _plan_refs(plan, j, x_ref, land_ref, me, peers, receiving=True)
            cp = pltpu.make_async_remote_copy(src_ref=src, dst_ref=dst, send_sem=send_sems.at[j], recv_sem=recv_sems.at[j],
                                              device_id=pos, device_id_type=pl.DeviceIdType.MESH)
            cp.wait_send()
            cp.wait_recv()

    return pl.pallas_call(
        body, name=name,
        out_shape=(pltpu.HBM(x_thru.shape, x_thru.dtype), pltpu.HBM(land_thru.shape, land_thru.dtype)),
        in_specs=[_HBM, _HBM, _SEM, _SEM] + [_ANY] * len(afters), out_specs=(_HBM, _HBM),
        input_output_aliases={0: 0, 1: 1}, compiler_params=pltpu.CompilerParams(has_side_effects=_EFFECT),
    )(x_thru, land_thru, send_sems, recv_sems, *afters)


def _to_proj_rows(t):
    z = jnp.zeros((D_PROJ - C_SM - 2 * GDN_HEADS - GLA_RANK,) + t.shape[1:], t.dtype)
    return jnp.concatenate([t[R_Z:R_A], t[R_GR:R_LR], t[R_GQ:R_GR], t[:R_Z], t[R_A:R_GQ], t[R_LR:], z], axis=0)


def _from_proj_rows(t):
    ab = C_SM + 2 * GDN_HEADS
    return jnp.concatenate([t[C_QKV:C_SM], t[C_Z:C_GR], t[C_SM:ab], t[C_GQ:C_QKV], t[C_GR:C_GQ],
                            t[ab:ab + GLA_RANK]], axis=0)


def _local_step(x, target, meta, attn_nw, conv_w, a_log, dt_bias, gdn_nw, w2, b2, gla_nw, ffn_nw, final_nw,
                fetch, emit, start=None):
    head = jnp.concatenate([jnp.zeros((ROW_PAD, D_MODEL), F32), meta], axis=0)
    conv_w8 = jnp.concatenate([conv_w, jnp.zeros((8 - CONV_K, conv_w.shape[1]), F32)], axis=0)
    w2p = jnp.zeros((SM_W, GLA_QK), F32).at[2 * GDN_HEADS:2 * GDN_HEADS + GLA_RANK].set(w2)
    alog_p = jnp.zeros((1, SM_W), F32).at[:, :GDN_HEADS].set(a_log)
    dt_p = jnp.zeros((1, SM_W), F32).at[:, :GDN_HEADS].set(dt_bias)

    h0, n1 = _embed_norm(head, x, attn_nw, name="attn_norm", after=start)
    w_in_t = fetch("w_in_t", (n1, conv_w8, w2p, alog_p, dt_p))
    proj = _matmul(n1, w_in_t, mode="nt", name="in_proj", out_dtype=BF16)
    gb, la = _gates_fwd(proj, w2p, b2, alog_p, dt_p, name="gates")
    act = _prep_fwd(proj, conv_w8, name="gdn_prep")
    o_gdn, s_gdn, t_gdn = _gdn_fwd(act, gb, name="gdn_fwd")
    o_gla, s_gla = _gla_fwd(proj, la, name="gla_fwd")
    mixed = _mix_fwd(o_gdn, o_gla, proj, gdn_nw, gla_nw, name="mix")
    w_out = fetch("w_out", mixed)
    h1 = _matmul(mixed, w_out, mode="nn", add=h0, name="out_proj")
    n2 = _rmsnorm_fwd(h1, ffn_nw, name="ffn_norm")
    w_gate_t, w_up_t = fetch("w_gate_t", n2), fetch("w_up_t", n2)
    gate, up, hid = _swiglu_fwd(n2, w_gate_t, w_up_t, name="swiglu")
    w_down = fetch("w_down", hid)
    h2 = _matmul(hid, w_down, mode="nn", add=h1, name="ffn_down", tm=1376, tn=256)
    dh2, dh2_b, d_final_nw, loss = _loss_head(h2, final_nw, target, name="loss_head")

    wg = dict(mode="tn", out_dtype=BF16, tn=512)
    tok = emit("w_down", _matmul(hid, dh2_b, name="d_w_down", tm=704, **wg))
    d_gate, d_up = _swiglu_bwd(dh2_b, w_down, gate, up, name="d_swiglu", after=tok)
    tok = emit("w_gate_t", _matmul(d_gate, n2, name="d_w_gate", tm=704, **wg))
    tok = emit("w_up_t", _matmul(d_up, n2, name="d_w_up", tm=704, after=tok, **wg))
    d_n2 = _matmul_pair(d_gate, w_gate_t, d_up, w_up_t, name="d_n2", after=tok)
    dh1, dh1_b, d_ffn_nw = _rmsnorm_bwd(h1, ffn_nw, d_n2, dh2, name="d_ffn_norm")

    tok = emit("w_out", _matmul(mixed, dh1_b, name="d_w_out", tm=512, **wg))
    d_mixed = _matmul(dh1_b, w_out, mode="nt", name="d_mixed", after=tok)
    do_gdn, do_gla, d_proj, d_gdn_nw, d_gla_nw = _mix_bwd(o_gdn, o_gla, proj, gdn_nw, gla_nw, d_mixed, name="d_mix")
    d_proj, d_la = _gla_bwd(proj, la, do_gla, s_gla, d_proj, name="gla_bwd")
    dact, dgb_heads = _gdn_bwd(act, gb, do_gdn, s_gdn, t_gdn, name="gdn_bwd")
    d_proj, d_w2p, d_b2, d_alog, d_dt = _gates_bwd(proj, w2p, b2, alog_p, dt_p, dgb_heads, d_la, d_proj, name="d_gates")
    d_proj, d_conv_w8 = _prep_bwd(proj, conv_w8, dact, d_proj, name="d_gdn_prep")
    tok = emit("w_in_t", _matmul(d_proj, n1, name="d_w_in", tm=768, **wg))
    d_n1 = _matmul(d_proj, w_in_t, mode="nn", name="d_n1", tm=688, after=tok)
    grad_x, d_head, d_attn_nw = _embed_norm_bwd(h0, attn_nw, d_n1, dh1, name="d_attn_norm")

    return dict(
        loss=loss[0, 0], grad_x=grad_x, meta=d_head[ROW_PAD:HEAD_ROWS], attn_nw=d_attn_nw,
        conv_w=d_conv_w8[:CONV_K], a_log=d_alog[:, :GDN_HEADS], dt_bias=d_dt[:, :GDN_HEADS], gdn_nw=d_gdn_nw,
        w2=d_w2p[2 * GDN_HEADS:2 * GDN_HEADS + GLA_RANK], b2=d_b2, gla_nw=d_gla_nw, ffn_nw=d_ffn_nw,
        final_nw=d_final_nw)


SMALL_ROWS = 32


def kernel(x, meta_tokens, attn_norm_w, w_in, gdn_conv_w, gdn_a_log, gdn_dt_bias, gdn_norm_w, gla_gate_w2, gla_gate_b, gla_norm_w, w_out, ffn_norm_w, w_gate, w_up, w_down, final_norm_w, loss_target, m_meta_tokens, m_attn_norm_w, m_w_in, m_gdn_conv_w, m_gdn_a_log, m_gdn_dt_bias, m_gdn_norm_w, m_gla_gate_w2, m_gla_gate_b, m_gla_norm_w, m_w_out, m_ffn_norm_w, m_w_gate, m_w_up, m_w_down, m_final_norm_w, v_meta_tokens, v_attn_norm_w, v_w_in, v_gdn_conv_w, v_gdn_a_log, v_gdn_dt_bias, v_gdn_norm_w, v_gla_gate_w2, v_gla_gate_b, v_gla_norm_w, v_w_out, v_ffn_norm_w, v_w_gate, v_w_up, v_w_down, v_final_norm_w):
    me = 4 * lax.axis_index("x") + 2 * lax.axis_index("y") + lax.axis_index("c")

    n_conv = gdn_conv_w.shape[2]
    n_w2 = gla_gate_w2.shape[2]
    n_meta = meta_tokens.shape[1]
    small = jnp.zeros((40, n_conv), F32)
    small = small.at[0:N_META, :n_meta].set(meta_tokens)
    small = small.at[N_META:N_META + CONV_K, :].set(gdn_conv_w[0])
    small = small.at[24:24 + GLA_RANK, :n_w2].set(gla_gate_w2[0])
    small_all = _gather(small, name="gather_small")
    meta_f = small_all[:, 0:N_META, :n_meta].transpose(1, 0, 2).reshape(N_META, D_MODEL)
    conv_f = small_all[:, N_META:N_META + CONV_K, :].transpose(1, 0, 2).reshape(CONV_K, N_DEV * n_conv)
    w2_f = small_all[:, 24:24 + GLA_RANK, :n_w2].transpose(1, 0, 2).reshape(GLA_RANK, N_DEV * n_w2)

    w_in_slab = w_in[0].T.astype(BF16)
    in_h = _exchange_start(w_in_slab, plan=PLAN_GATHER_CHIPS, slab=w_in_slab.shape, name="gather_w_in_start",
                           after=small_all)
    handles, tok = {}, in_h[4]
    for wname, slab in (("w_out", w_out[0]), ("w_gate_t", w_gate[0].T), ("w_up_t", w_up[0].T), ("w_down", w_down[0])):
        slab = slab.astype(BF16)
        handles[wname] = _exchange_start(slab, plan=PLAN_GATHER, slab=slab.shape, name="gather_" + wname + "_start", after=tok)
        tok = handles[wname][4]

    def fetch(name, after):
        if name == "w_in_t":
            own, got = _exchange_wait(in_h, after, plan=PLAN_GATHER_CHIPS, name="gather_w_in_wait")
            pass_h = _exchange_start(own, plan=PLAN_GATHER_PASS_ON, land=got, name="pass_w_in_start")
            own, got = _exchange_wait(pass_h, pass_h[4], plan=PLAN_GATHER_PASS_ON, name="pass_w_in_wait")
            got = lax.dynamic_update_index_in_dim(got, own, me, 0)
            return _to_proj_rows(got.reshape(D_IN, D_MODEL))
        own, got = _exchange_wait(handles[name], after, plan=PLAN_GATHER, name="gather_" + name + "_wait")
        got = lax.dynamic_update_index_in_dim(got, own, me, 0)
        return got.reshape(N_DEV * got.shape[1], D_MODEL)

    sent = {}

    def emit(name, grad):
        if name == "w_in_t":
            grad = _from_proj_rows(grad)
        parts = grad.reshape(N_DEV, grad.shape[0] // N_DEV, D_MODEL)
        sent[name] = _exchange_start(parts, plan=PLAN_SCATTER, slab=parts.shape[1:], name="scatter_" + name + "_start")
        return sent[name][4]

    g = _local_step(x[0], loss_target[0], meta_f, attn_norm_w, conv_f, gdn_a_log, gdn_dt_bias, gdn_norm_w, w2_f,
                    gla_gate_b, gla_norm_w, ffn_norm_w, final_norm_w.reshape(1, D_MODEL), fetch, emit, start=tok)

    misc = jnp.concatenate([g["a_log"], g["dt_bias"], g["gdn_nw"], g["gla_nw"], g["b2"], g["loss"].reshape(1, 1)], axis=1)
    n_misc = misc.shape[1]
    misc = jnp.pad(misc, ((0, 0), (0, D_MODEL - n_misc)))
    rows = jnp.concatenate([g["attn_nw"], g["ffn_nw"], g["final_nw"], misc, g["meta"],
                            g["conv_w"].reshape(-1, D_MODEL), g["w2"].reshape(-1, D_MODEL)], axis=0)
    rows = jnp.pad(rows, ((0, SMALL_ROWS - rows.shape[0]), (0, 0)))
    rows_h = _exchange_start(rows, plan=PLAN_GATHER, slab=rows.shape, name="gather_small_grads_start")

    big = {}
    after = rows_h[4]
    for name, w, m, v, transposed in (("w_down", w_down, m_w_down, v_w_down, False), ("w_gate_t", w_gate, m_w_gate, v_w_gate, True),
                                      ("w_up_t", w_up, m_w_up, v_w_up, True), ("w_out", w_out, m_w_out, v_w_out, False),
                                      ("w_in_t", w_in, m_w_in, v_w_in, True)):
        own, got = _exchange_wait(sent[name], after, plan=PLAN_SCATTER, name="scatter_" + name + "_wait")
        got = lax.dynamic_update_index_in_dim(got, lax.dynamic_index_in_dim(own, me, 0, keepdims=False), me, 0)
        local = [t[0].T if transposed else t[0] for t in (w, m, v)]
        res = _sum_adamw(got, *local, name="adamw_" + name)
        big[name] = [t.T[None] if transposed else t[None] for t in res]
        after = res[0]

    own, got = _exchange_wait(rows_h, after, plan=PLAN_GATHER, name="gather_small_grads_wait")
    tot = _sum_slabs(lax.dynamic_update_index_in_dim(got, own, me, 0), name="sum_small_grads")
    grad_attn_nw, grad_ffn_nw, grad_final_nw = tot[0:1], tot[1:2], tot[2]
    grad_a_log = tot[3:4, 0:8]
    grad_dt = tot[3:4, 8:16]
    grad_gdn_nw = tot[3:4, 16:16 + GDN_DV]
    grad_gla_nw = tot[3:4, 144:144 + GLA_DV]
    grad_b2 = tot[3:4, 400:400 + GLA_QK]
    loss = tot[3, n_misc - 1]
    r0 = 4 + N_META
    grad_meta = lax.dynamic_slice(tot[4:r0], (0, me * n_meta), (N_META, n_meta))
    r1 = r0 + CONV_K * N_DEV * n_conv // D_MODEL
    grad_conv = lax.dynamic_slice(tot[r0:r1].reshape(CONV_K, N_DEV * n_conv), (0, me * n_conv), (CONV_K, n_conv))[None]
    r2 = r1 + GLA_RANK * N_DEV * n_w2 // D_MODEL
    grad_w2 = lax.dynamic_slice(tot[r1:r2].reshape(GLA_RANK, N_DEV * n_w2), (0, me * n_w2), (GLA_RANK, n_w2))[None]

    weights = [meta_tokens, attn_norm_w, w_in, gdn_conv_w, gdn_a_log, gdn_dt_bias, gdn_norm_w, gla_gate_w2,
               gla_gate_b, gla_norm_w, w_out, ffn_norm_w, w_gate, w_up, w_down, final_norm_w]
    grads = [grad_meta, grad_attn_nw, "w_in_t", grad_conv, grad_a_log, grad_dt, grad_gdn_nw, grad_w2,
             grad_b2, grad_gla_nw, "w_out", grad_ffn_nw, "w_gate_t", "w_up_t", "w_down", grad_final_nw]
    ms = [m_meta_tokens, m_attn_norm_w, m_w_in, m_gdn_conv_w, m_gdn_a_log, m_gdn_dt_bias, m_gdn_norm_w,
          m_gla_gate_w2, m_gla_gate_b, m_gla_norm_w, m_w_out, m_ffn_norm_w, m_w_gate, m_w_up, m_w_down, m_final_norm_w]
    vs = [v_meta_tokens, v_attn_norm_w, v_w_in, v_gdn_conv_w, v_gdn_a_log, v_gdn_dt_bias, v_gdn_norm_w,
          v_gla_gate_w2, v_gla_gate_b, v_gla_norm_w, v_w_out, v_ffn_norm_w, v_w_gate, v_w_up, v_w_down, v_final_norm_w]
    outs = [[], [], [], []]
    for idx, (w, gr, m, v) in enumerate(zip(weights, grads, ms, vs)):
        if isinstance(gr, str):
            res = big[gr]
        else:
            gr = gr.reshape(w.shape)
            res = (gr,) + _adamw(w, gr, m, v, name=f"adamw_{idx}")
        for lst, t in zip(outs, res):
            lst.append(t)
    return (loss, g["grad_x"][None], *outs[0], *outs[1], *outs[2], *outs[3])
```

```python
import functools

import jax
import jax.numpy as jnp
from jax import lax
from jax.experimental import pallas as pl
from jax.experimental.pallas import tpu as pltpu

F32 = jnp.float32
BF16 = jnp.bfloat16
_MXU_DTYPE = jnp.bfloat16

D_MODEL = 2048
N_META = 16
ROW_PAD = 48
HEAD_ROWS = ROW_PAD + N_META
CONV_K = 4
GDN_HEADS, GDN_DK, GDN_DV, GDN_CHUNK = 8, 128, 128, 64
GLA_HEADS, GLA_DK, GLA_DV, GLA_CHUNK = 4, 128, 256, 16
GLA_RANK = 16
GLA_GATE_NORMALIZER = 16.0
GDN_QK = GDN_HEADS * GDN_DK
GDN_V = GDN_HEADS * GDN_DV
GLA_QK = GLA_HEADS * GLA_DK
GLA_V = GLA_HEADS * GLA_DV
D_FF = 5632
D_IN = 7200
NORM_EPS = 1e-6
C_Z, C_GR, C_GQ, C_GK, C_GV, C_QKV, C_SM = 0, 1024, 2048, 2560, 3072, 4096, 7168
SM_W = 128
D_PROJ = 7680
R_Z, R_A, R_B, R_GQ, R_GK, R_GV, R_GR, R_LR = 3072, 4096, 4104, 4112, 4624, 5136, 6160, 7184

ADAM_LR, ADAM_B1, ADAM_B2, ADAM_EPS, ADAM_WD, ADAM_STEP = 0.001, 0.9, 0.999, 1e-08, 0.01, 10

N_DEV = 8
VMEM_LIMIT = 56 * 1024 * 1024

NN = (((1,), (0,)), ((), ()))
NT = (((1,), (1,)), ((), ()))
TN = (((0,), (0,)), ((), ()))


def _dot(a, b, dims=NN):
    return lax.dot_general(a.astype(_MXU_DTYPE), b.astype(_MXU_DTYPE), dims, preferred_element_type=F32)


def _running_sum(x, reverse=False):
    n = x.shape[0]
    row = lax.broadcasted_iota(jnp.int32, x.shape, 0)
    s = 1
    while s < n:
        if reverse:
            x = x + jnp.where(row < n - s, pltpu.roll(x, n - s, 0), 0.0)
        else:
            x = x + jnp.where(row >= s, pltpu.roll(x, s, 0), 0.0)
        s *= 2
    return x


def _dot3(a, b):
    ah = a.astype(BF16)
    al = (a - ah.astype(F32)).astype(BF16)
    bh = b.astype(BF16)
    bl = (b - bh.astype(F32)).astype(BF16)
    d = functools.partial(lax.dot_general, dimension_numbers=NN, preferred_element_type=F32)
    return d(ah, bh) + (d(ah, bl) + d(al, bh))


def _tile(n, target, mult=8):
    best = None
    for t in range(mult, min(n, target) + 1, mult):
        if n % t == 0:
            best = t
    return best if best is not None else n


def _params(*sem):
    return pltpu.CompilerParams(dimension_semantics=sem, vmem_limit_bytes=VMEM_LIMIT)


def _sigmoid(x):
    return 0.5 * jnp.tanh(0.5 * x) + 0.5


def _softplus(x):
    return jnp.maximum(x, 0.0) + jnp.log1p(jnp.exp(-jnp.abs(x)))


def _silu_and_grad(c):
    s = _sigmoid(c)
    return c * s, s * (1.0 + c * (1.0 - s))


_ANY = pl.BlockSpec(memory_space=pl.ANY)


def _matmul(a, b, *, mode, name, out_dtype=F32, add=None, after=None, tm=1376, tn=512):
    if mode == "tn":
        K, M = a.shape
        N = b.shape[1]
    else:
        M, K = a.shape
        N = b.shape[0] if mode == "nt" else b.shape[1]
    tm = _tile(M, tm, 128 if mode == "tn" else 16)
    tn = _tile(N, tn, 128)
    dims = {"nn": NN, "nt": NT, "tn": TN}[mode]
    n_after = 0 if after is None else 1

    def body(*refs):
        refs = refs[n_after:]
        r = _dot(refs[0][...], refs[1][...], dims)
        if add is not None:
            r = r + refs[2][...]
        refs[-1][...] = r.astype(out_dtype)

    a_spec = pl.BlockSpec((K, tm), lambda i, j: (0, i)) if mode == "tn" else pl.BlockSpec((tm, K), lambda i, j: (i, 0))
    b_spec = pl.BlockSpec((tn, K), lambda i, j: (j, 0)) if mode == "nt" else pl.BlockSpec((K, tn), lambda i, j: (0, j))
    o_spec = pl.BlockSpec((tm, tn), lambda i, j: (i, j))
    in_specs = [_ANY] * n_after + [a_spec, b_spec] + ([o_spec] if add is not None else [])
    args = ((after,) if n_after else ()) + (a, b) + ((add,) if add is not None else ())
    return pl.pallas_call(
        body, name=name, grid=(M // tm, N // tn), in_specs=in_specs, out_specs=o_spec,
        out_shape=jax.ShapeDtypeStruct((M, N), out_dtype), compiler_params=_params("parallel", "parallel"),
    )(*args)


def _matmul_pair(a1, b1, a2, b2, *, name, after=None, tm=688, tn=256):
    M, K = a1.shape
    N = b1.shape[1]
    tm, tn = _tile(M, tm, 16), _tile(N, tn, 128)
    n_after = 0 if after is None else 1

    def body(*refs):
        a1_ref, b1_ref, a2_ref, b2_ref, o_ref = refs[n_after:]
        o_ref[...] = _dot(a1_ref[...], b1_ref[...]) + _dot(a2_ref[...], b2_ref[...])

    a_spec = pl.BlockSpec((tm, K), lambda i, j: (i, 0))
    b_spec = pl.BlockSpec((K, tn), lambda i, j: (0, j))
    return pl.pallas_call(
        body, name=name, grid=(M // tm, N // tn), in_specs=[_ANY] * n_after + [a_spec, b_spec, a_spec, b_spec],
        out_specs=pl.BlockSpec((tm, tn), lambda i, j: (i, j)), out_shape=jax.ShapeDtypeStruct((M, N), F32),
        compiler_params=_params("parallel", "parallel"),
    )(*((after,) if n_after else ()), a1, b1, a2, b2)


def _rmsnorm_fwd(h, w, *, name):
    M, D = h.shape
    tm = _tile(M, 688, 16)

    def body(h_ref, w_ref, n_ref):
        x = h_ref[...]
        r = lax.rsqrt(jnp.mean(x * x, axis=-1, keepdims=True) + NORM_EPS)
        n_ref[...] = (x * r * w_ref[...]).astype(n_ref.dtype)

    return pl.pallas_call(
        body, name=name, grid=(M // tm,),
        in_specs=[pl.BlockSpec((tm, D), lambda i: (i, 0)), pl.BlockSpec((1, D), lambda i: (0, 0))],
        out_specs=pl.BlockSpec((tm, D), lambda i: (i, 0)),
        out_shape=jax.ShapeDtypeStruct((M, D), BF16),
        compiler_params=_params("parallel"),
    )(h, w)


SEQ_BLOCK = HEAD_ROWS


def _seq_blocks_per_tile(rows):
    n = rows // SEQ_BLOCK
    return max(m for m in (1, 2, 3, 4) if n % m == 0)


def _seq_specs(m, D):
    return [pl.BlockSpec((SEQ_BLOCK, D), functools.partial(lambda i, k: (jnp.maximum(m * i + k - 1, 0), 0), k=k))
            for k in range(m)]


def _embed_norm(head, x, w, *, name, after=None):
    S, D = x.shape
    m = _seq_blocks_per_tile(S + HEAD_ROWS)
    n_after = 0 if after is None else 1

    def body(*refs):
        refs = refs[n_after:]
        head_ref, x_refs, w_ref, h_ref, n_ref = refs[0], refs[1:1 + m], refs[1 + m], refs[2 + m], refs[3 + m]
        i = pl.program_id(0)
        for k in range(m):
            blk = x_refs[k][...]
            if k == 0:
                blk = jnp.where(i == 0, head_ref[...], blk)
            rows = slice(k * SEQ_BLOCK, (k + 1) * SEQ_BLOCK)
            h_ref[rows, :] = blk
            r = lax.rsqrt(jnp.mean(blk * blk, axis=-1, keepdims=True) + NORM_EPS)
            n_ref[rows, :] = (blk * r * w_ref[...]).astype(n_ref.dtype)

    tile = pl.BlockSpec((m * SEQ_BLOCK, D), lambda i: (i, 0))
    return pl.pallas_call(
        body, name=name, grid=((S + HEAD_ROWS) // (m * SEQ_BLOCK),),
        in_specs=[_ANY] * n_after + [pl.BlockSpec((SEQ_BLOCK, D), lambda i: (0, 0))] + _seq_specs(m, D)
        + [pl.BlockSpec((1, D), lambda i: (0, 0))],
        out_specs=[tile, tile],
        out_shape=[jax.ShapeDtypeStruct((S + HEAD_ROWS, D), F32), jax.ShapeDtypeStruct((S + HEAD_ROWS, D), BF16)],
        compiler_params=_params("parallel"),
    )(*((after,) if n_after else ()), head, *([x] * m), w)


def _embed_norm_bwd(h, w, dn, dres, *, name):
    M, D = h.shape
    S = M - HEAD_ROWS
    m = _seq_blocks_per_tile(S)
    g = S // (m * SEQ_BLOCK)

    def one(x, dn_, dres_, w_):
        r = lax.rsqrt(jnp.mean(x * x, axis=-1, keepdims=True) + NORM_EPS)
        xhat = x * r
        dxhat = dn_ * w_
        dh = dres_ + r * (dxhat - xhat * jnp.mean(dxhat * xhat, axis=-1, keepdims=True))
        return dh, jnp.sum((dn_ * xhat).reshape(SEQ_BLOCK // 8, 8, D), axis=0)

    def body(*refs):
        w_ref = refs[0]
        groups = [refs[1 + a * (m + 1):1 + (a + 1) * (m + 1)] for a in range(3)]
        gx_ref, dhead_ref, dw_ref, acc_ref = refs[1 + 3 * (m + 1):]
        i = pl.program_id(0)
        w_ = w_ref[...]
        part = jnp.zeros((8, D), F32)
        for k in range(m):
            dh, p = one(*(grp[1 + k][...] for grp in groups), w_)
            gx_ref[k * SEQ_BLOCK:(k + 1) * SEQ_BLOCK, :] = dh
            part = part + p

        @pl.when(i == 0)
        def _():
            dh, p = one(*(grp[0][...] for grp in groups), w_)
            dhead_ref[...] = dh
            acc_ref[...] = part + p

        @pl.when(i > 0)
        def _():
            acc_ref[...] += part

        @pl.when(i == g - 1)
        def _():
            dw_ref[...] = jnp.sum(acc_ref[...], axis=0, keepdims=True)

    first = pl.BlockSpec((SEQ_BLOCK, D), lambda i: (0, 0))
    blocks = [pl.BlockSpec((SEQ_BLOCK, D), functools.partial(lambda i, k: (m * i + k + 1, 0), k=k)) for k in range(m)]
    vec = pl.BlockSpec((1, D), lambda i: (0, 0))
    return pl.pallas_call(
        body, name=name, grid=(g,), in_specs=[vec] + ([first] + blocks) * 3,
        out_specs=[pl.BlockSpec((m * SEQ_BLOCK, D), lambda i: (i, 0)), first, vec],
        out_shape=[jax.ShapeDtypeStruct((S, D), F32), jax.ShapeDtypeStruct((SEQ_BLOCK, D), F32),
                   jax.ShapeDtypeStruct((1, D), F32)],
        scratch_shapes=[pltpu.VMEM((8, D), F32)],
        compiler_params=_params("arbitrary"),
    )(w, *([h] * (m + 1)), *([dn] * (m + 1)), *([dres] * (m + 1)))


def _rmsnorm_bwd(h, w, dn, dres, *, name):
    M, D = h.shape
    tm = _tile(M, 344, 16)
    g = M // tm

    def body(h_ref, w_ref, dn_ref, dres_ref, dh_ref, dhb_ref, dw_ref, acc_ref):
        i = pl.program_id(0)
        x = h_ref[...]
        r = lax.rsqrt(jnp.mean(x * x, axis=-1, keepdims=True) + NORM_EPS)
        xhat = x * r
        dn_ = dn_ref[...]
        dxhat = dn_ * w_ref[...]
        dh = dres_ref[...] + r * (dxhat - xhat * jnp.mean(dxhat * xhat, axis=-1, keepdims=True))
        dh_ref[...] = dh
        dhb_ref[...] = dh.astype(dhb_ref.dtype)
        part = jnp.sum((dn_ * xhat).reshape(tm // 8, 8, D), axis=0)

        @pl.when(i == 0)
        def _():
            acc_ref[...] = part

        @pl.when(i > 0)
        def _():
            acc_ref[...] += part

        @pl.when(i == g - 1)
        def _():
            dw_ref[...] = jnp.sum(acc_ref[...], axis=0, keepdims=True)

    row = pl.BlockSpec((tm, D), lambda i: (i, 0))
    vec = pl.BlockSpec((1, D), lambda i: (0, 0))
    return pl.pallas_call(
        body, name=name, grid=(g,), in_specs=[row, vec, row, row],
        out_specs=[row, row, vec],
        out_shape=[jax.ShapeDtypeStruct((M, D), F32), jax.ShapeDtypeStruct((M, D), BF16),
                   jax.ShapeDtypeStruct((1, D), F32)],
        scratch_shapes=[pltpu.VMEM((8, D), F32)],
        compiler_params=_params("arbitrary"),
    )(h, w, dn, dres)


def _loss_head(h, w, target, *, name):
    M, D = h.shape
    m = _seq_blocks_per_tile(M)
    tm = m * SEQ_BLOCK
    g = M // tm

    def body(h_ref, w_ref, *rest):
        t_refs = rest[:m]
        dh_ref, dhb_ref, dw_ref, loss_ref, acc_ref, lacc_ref = rest[m:]
        i = pl.program_id(0)
        x = h_ref[...]
        row = i * tm + lax.broadcasted_iota(jnp.int32, (tm, 1), 0)
        live = row >= HEAD_ROWS
        r = lax.rsqrt(jnp.mean(x * x, axis=-1, keepdims=True) + NORM_EPS)
        xhat = x * r
        t = jnp.concatenate([t_ref[...] for t_ref in t_refs], axis=0)
        err = jnp.where(live, xhat * w_ref[...] - t, 0.0)
        dy = err * (1.0 / D)
        dxhat = dy * w_ref[...]
        dh = r * (dxhat - xhat * jnp.mean(dxhat * xhat, axis=-1, keepdims=True))
        dh_ref[...] = dh
        dhb_ref[...] = dh.astype(dhb_ref.dtype)
        part = jnp.sum((dy * xhat).reshape(tm // 8, 8, D), axis=0)
        lpart = jnp.sum((err * err).reshape(tm // 8, 8, D), axis=0)

        @pl.when(i == 0)
        def _():
            acc_ref[...] = part
            lacc_ref[...] = lpart

        @pl.when(i > 0)
        def _():
            acc_ref[...] += part
            lacc_ref[...] += lpart

        @pl.when(i == g - 1)
        def _():
            dw_ref[...] = jnp.sum(acc_ref[...], axis=0, keepdims=True)
            tot = jnp.sum(jnp.sum(lacc_ref[...], axis=0, keepdims=True), axis=1, keepdims=True)
            loss_ref[...] = jnp.broadcast_to(tot * (0.5 / D), (1, 128))

    row = pl.BlockSpec((tm, D), lambda i: (i, 0))
    vec = pl.BlockSpec((1, D), lambda i: (0, 0))
    return pl.pallas_call(
        body, name=name, grid=(g,), in_specs=[row, vec] + _seq_specs(m, D),
        out_specs=[row, row, vec, pl.BlockSpec((1, 128), lambda i: (0, 0))],
        out_shape=[jax.ShapeDtypeStruct((M, D), F32), jax.ShapeDtypeStruct((M, D), BF16),
                   jax.ShapeDtypeStruct((1, D), F32), jax.ShapeDtypeStruct((1, 128), F32)],
        scratch_shapes=[pltpu.VMEM((8, D), F32), pltpu.VMEM((8, D), F32)],
        compiler_params=_params("arbitrary"),
    )(h, w, *([target] * m))


def _gate_terms(sm, w2p, b2, alog_p, dt_p, row0):
    tm = sm.shape[0]
    lane = lax.broadcasted_iota(jnp.int32, (tm, SM_W), 1)
    live = (row0 + lax.broadcasted_iota(jnp.int32, (tm, 1), 0)) >= ROW_PAD
    pre = sm + dt_p
    neg_a = -jnp.exp(alog_p)
    g = neg_a * _softplus(pre)
    beta = _sigmoid(sm)
    z = _dot(sm, w2p) + b2
    return lane, live, pre, neg_a, g, beta, z


def _gates_fwd(proj, w2p, b2, alog_p, dt_p, *, name):
    M = proj.shape[0]
    tm = _tile(M, 688, 8)

    def body(sm_ref, w2_ref, b2_ref, al_ref, dt_ref, gb_ref, la_ref):
        row0 = pl.program_id(0) * tm
        lane, live, _, _, g, beta, z = _gate_terms(sm_ref[...].astype(F32), w2_ref[...], b2_ref[...], al_ref[...], dt_ref[...], row0)
        gb = jnp.where(lane < GDN_HEADS, g, jnp.where(lane < 2 * GDN_HEADS, beta, 0.0))
        gb_ref[...] = jnp.where(live, gb, 0.0)
        la = (jnp.minimum(z, 0.0) - jnp.log1p(jnp.exp(-jnp.abs(z)))) * (1.0 / GLA_GATE_NORMALIZER)
        la_ref[...] = jnp.where(live, la, 0.0)

    full = lambda s: pl.BlockSpec(s, lambda i: (0, 0))
    return pl.pallas_call(
        body, name=name, grid=(M // tm,),
        in_specs=[pl.BlockSpec((tm, SM_W), lambda i: (i, C_SM // SM_W)), full((SM_W, GLA_QK)), full((1, GLA_QK)),
                  full((1, SM_W)), full((1, SM_W))],
        out_specs=[pl.BlockSpec((tm, SM_W), lambda i: (i, 0)), pl.BlockSpec((tm, GLA_QK), lambda i: (i, 0))],
        out_shape=[jax.ShapeDtypeStruct((M, SM_W), F32), jax.ShapeDtypeStruct((M, GLA_QK), F32)],
        compiler_params=_params("parallel"),
    )(proj, w2p, b2, alog_p, dt_p)


def _gates_bwd(proj, w2p, b2, alog_p, dt_p, dgb_heads, dla, d_proj, *, name):
    M = proj.shape[0]
    tm = _tile(M, 688, 8)
    g_ = M // tm

    tail_w = D_PROJ - C_SM

    def body(sm_ref, w2_ref, b2_ref, al_ref, dt_ref, dgb_ref, dla_ref, _,
             dsm_ref, dw2_ref, db2_ref, dal_ref, ddt_ref):
        i = pl.program_id(0)
        sm = sm_ref[...].astype(F32)
        lane, live, pre, neg_a, g, beta, z = _gate_terms(sm, w2_ref[...], b2_ref[...], al_ref[...], dt_ref[...], i * tm)
        dz = jnp.where(live, dla_ref[...] * (_sigmoid(-z) * (1.0 / GLA_GATE_NORMALIZER)), 0.0)
        dsm_lr = _dot(dz, w2_ref[...], NT)
        dgb = dgb_ref[0]
        for hh in range(1, GDN_HEADS):
            dgb = dgb + dgb_ref[hh]
        dgb = jnp.where(live, dgb, 0.0)
        da = dgb * neg_a * _sigmoid(pre)
        db = dgb * beta * (1.0 - beta)
        dsm = jnp.where(lane < GDN_HEADS, da, jnp.where(lane < 2 * GDN_HEADS, db, dsm_lr))
        dsm_ref[:, 0:SM_W] = dsm.astype(dsm_ref.dtype)
        if tail_w > SM_W:
            dsm_ref[:, SM_W:tail_w] = jnp.zeros((tm, tail_w - SM_W), dsm_ref.dtype)
        is_a = lane < GDN_HEADS
        dal = jnp.sum(jnp.where(is_a, dgb * g, 0.0), axis=0, keepdims=True)
        ddt = jnp.sum(jnp.where(is_a, da, 0.0), axis=0, keepdims=True)
        dw2 = _dot(sm, dz, TN)
        db2 = jnp.sum(dz, axis=0, keepdims=True)

        @pl.when(i == 0)
        def _():
            dw2_ref[...] = dw2
            db2_ref[...] = db2
            dal_ref[...] = dal
            ddt_ref[...] = ddt

        @pl.when(i > 0)
        def _():
            dw2_ref[...] += dw2
            db2_ref[...] += db2
            dal_ref[...] += dal
            ddt_ref[...] += ddt

    full = lambda s: pl.BlockSpec(s, lambda i: (0, 0))
    return pl.pallas_call(
        body, name=name, grid=(g_,),
        in_specs=[pl.BlockSpec((tm, SM_W), lambda i: (i, C_SM // SM_W)), full((SM_W, GLA_QK)), full((1, GLA_QK)),
                  full((1, SM_W)), full((1, SM_W)),
                  pl.BlockSpec((GDN_HEADS, tm, SM_W), lambda i: (0, i, 0)),
                  pl.BlockSpec((tm, GLA_QK), lambda i: (i, 0)), _ANY],
        out_specs=[pl.BlockSpec((tm, tail_w), lambda i: (i, C_SM // tail_w)), full((SM_W, GLA_QK)), full((1, GLA_QK)),
                   full((1, SM_W)), full((1, SM_W))],
        out_shape=[jax.ShapeDtypeStruct(d_proj.shape, d_proj.dtype), jax.ShapeDtypeStruct((SM_W, GLA_QK), F32),
                   jax.ShapeDtypeStruct((1, GLA_QK), F32), jax.ShapeDtypeStruct((1, SM_W), F32),
                   jax.ShapeDtypeStruct((1, SM_W), F32)],
        input_output_aliases={7: 0},
        compiler_params=_params("arbitrary"),
    )(proj, w2p, b2, alog_p, dt_p, dgb_heads, dla, d_proj)


QKV_W = GDN_QK
N_QKV_GROUPS = 3
QKV_B0 = C_QKV // QKV_W
HALO = 16


def _conv_terms(x_ref, halo_ref, cw_ref, xs_ref, i, tm):
    xs_ref[HALO:HALO + tm, :] = x_ref[...].astype(F32)
    xs_ref[0:HALO, :] = jnp.where(i > 0, halo_ref[...].astype(F32), 0.0)
    cw = cw_ref[...]
    xs = xs_ref[...]
    taps = [(pltpu.roll(xs, CONV_K - 1 - t, 0) if t < CONV_K - 1 else xs)[HALO:HALO + tm, :] for t in range(CONV_K)]
    c = taps[0] * cw[0:1, :]
    for t in range(1, CONV_K):
        c = c + taps[t] * cw[t:t + 1, :]
    return c, taps


def _prep_fwd(proj, conv_w8, *, name):
    M = proj.shape[0]
    tm = _tile(M, 688, 16)

    def body(x_ref, halo_ref, cw_ref, o_ref, xs_ref):
        j, i = pl.program_id(0), pl.program_id(1)
        c, _ = _conv_terms(x_ref, halo_ref, cw_ref, xs_ref, i, tm)
        s, _ = _silu_and_grad(c)
        scale = jnp.where(j == 0, GDN_DK ** -0.5, 1.0)
        for hh in range(GDN_HEADS):
            cols = slice(hh * 128, (hh + 1) * 128)
            sh = s[:, cols]
            r = lax.rsqrt(jnp.sum(sh * sh, axis=-1, keepdims=True) + NORM_EPS)
            o_ref[:, cols] = jnp.where(j < 2, sh * (r * scale), sh)

    hb = tm // HALO
    return pl.pallas_call(
        body, name=name, grid=(N_QKV_GROUPS, M // tm),
        in_specs=[pl.BlockSpec((tm, QKV_W), lambda j, i: (i, QKV_B0 + j)),
                  pl.BlockSpec((HALO, QKV_W), lambda j, i: (jnp.maximum(i * hb - 1, 0), QKV_B0 + j)),
                  pl.BlockSpec((8, QKV_W), lambda j, i: (0, j))],
        out_specs=pl.BlockSpec((tm, QKV_W), lambda j, i: (i, j)),
        out_shape=jax.ShapeDtypeStruct((M, N_QKV_GROUPS * QKV_W), F32),
        scratch_shapes=[pltpu.VMEM((tm + HALO, QKV_W), F32)],
        compiler_params=_params("parallel", "arbitrary"),
    )(proj, proj, conv_w8)


def _prep_bwd(proj, conv_w8, dact, d_proj, *, name):
    M = proj.shape[0]
    tm = _tile(M, 688, 16)
    g_ = M // tm
    ext = tm + HALO

    def body(x_ref, prev_ref, next_ref, cw_ref, da_ref, dan_ref, _, o_ref, dcw_ref, xs_ref, das_ref, dcs_ref):
        j, i = pl.program_id(0), pl.program_id(1)
        not_last = i < g_ - 1
        xs_ref[0:HALO, :] = jnp.where(i > 0, prev_ref[...].astype(F32), 0.0)
        xs_ref[HALO:HALO + tm, :] = x_ref[...].astype(F32)
        xs_ref[HALO + tm:HALO + ext, :] = jnp.where(not_last, next_ref[...].astype(F32), 0.0)
        das_ref[0:tm, :] = da_ref[...]
        das_ref[tm:ext, :] = jnp.where(not_last, dan_ref[...], 0.0)
        cw = cw_ref[...]
        xs = xs_ref[...]
        taps = [(pltpu.roll(xs, CONV_K - 1 - t, 0) if t < CONV_K - 1 else xs)[HALO:HALO + ext, :] for t in range(CONV_K)]
        c = taps[0] * cw[0:1, :]
        for t in range(1, CONV_K):
            c = c + taps[t] * cw[t:t + 1, :]
        s, ds_dc = _silu_and_grad(c)
        scale = jnp.where(j == 0, GDN_DK ** -0.5, 1.0)
        for hh in range(GDN_HEADS):
            cols = slice(hh * 128, (hh + 1) * 128)
            sh = s[:, cols]
            r = lax.rsqrt(jnp.sum(sh * sh, axis=-1, keepdims=True) + NORM_EPS)
            da = das_ref[:, cols]
            y = sh * r
            dy = da * scale
            ds_norm = r * (dy - y * jnp.sum(dy * y, axis=-1, keepdims=True))
            dcs_ref[:, cols] = jnp.where(j < 2, ds_norm, da) * ds_dc[:, cols]
        dc = dcs_ref[...]
        acc = dc[0:tm, :] * cw[CONV_K - 1:CONV_K, :]
        for t in range(CONV_K - 1):
            acc = acc + pltpu.roll(dc, ext - (CONV_K - 1 - t), 0)[0:tm, :] * cw[t:t + 1, :]
        o_ref[...] = acc.astype(o_ref.dtype)
        r8 = lax.broadcasted_iota(jnp.int32, (8, QKV_W), 0)
        part = jnp.zeros((8, QKV_W), F32)
        for t in range(CONV_K):
            part = jnp.where(r8 == t, jnp.sum(dc[0:tm, :] * taps[t][0:tm, :], axis=0, keepdims=True), part)

        @pl.when(i == 0)
        def _():
            dcw_ref[...] = part

        @pl.when(i > 0)
        def _():
            dcw_ref[...] += part

    hb = tm // HALO
    last = M // HALO - 1
    prev_of = lambda i: jnp.maximum(i * hb - 1, 0)
    next_of = lambda i: jnp.minimum((i + 1) * hb, last)
    return pl.pallas_call(
        body, name=name, grid=(N_QKV_GROUPS, g_),
        in_specs=[pl.BlockSpec((tm, QKV_W), lambda j, i: (i, QKV_B0 + j)),
                  pl.BlockSpec((HALO, QKV_W), lambda j, i: (prev_of(i), QKV_B0 + j)),
                  pl.BlockSpec((HALO, QKV_W), lambda j, i: (next_of(i), QKV_B0 + j)),
                  pl.BlockSpec((8, QKV_W), lambda j, i: (0, j)),
                  pl.BlockSpec((tm, QKV_W), lambda j, i: (i, j)),
                  pl.BlockSpec((HALO, QKV_W), lambda j, i: (next_of(i), j)), _ANY],
        out_specs=[pl.BlockSpec((tm, QKV_W), lambda j, i: (i, QKV_B0 + j)), pl.BlockSpec((8, QKV_W), lambda j, i: (0, j))],
        out_shape=[jax.ShapeDtypeStruct(d_proj.shape, d_proj.dtype),
                   jax.ShapeDtypeStruct((8, N_QKV_GROUPS * QKV_W), F32)],
        input_output_aliases={6: 0},
        scratch_shapes=[pltpu.VMEM((HALO + ext, QKV_W), F32), pltpu.VMEM((ext, QKV_W), F32), pltpu.VMEM((ext, QKV_W), F32)],
        compiler_params=_params("parallel", "arbitrary"),
    )(proj, proj, proj, conv_w8, dact, dact, d_proj)


def _round_robin(gens):
    gens = list(gens)
    while gens:
        alive = []
        for gen in gens:
            try:
                next(gen)
                alive.append(gen)
            except StopIteration:
                pass
        gens = alive


def _unit_lower_inverse(a_low, eye):
    n = a_low.shape[0]
    ri = lax.broadcasted_iota(jnp.int32, (n, n), 0)
    ci = lax.broadcasted_iota(jnp.int32, (n, n), 1)
    same = lambda shift: (ri >> shift) == (ci >> shift)
    b = jnp.where(same(3), -a_low, 0.0)
    x = eye + b
    p2 = _dot3(b, b)
    yield
    x = x + _dot3(x, p2)
    p4 = _dot3(p2, p2)
    yield
    x = x + _dot3(x, p4)
    yield
    for shift in (3, 4, 5):
        between = jnp.where(same(shift + 1) & ~same(shift), a_low, 0.0)
        t = _dot3(between, x)
        yield
        x = x - _dot3(x, t)
        yield
    return x


class _GdnChunk:
    def build(self, q, k, v, gb, h, sum_on_mxu):
        C = GDN_CHUNK
        lane = lax.broadcasted_iota(jnp.int32, (C, SM_W), 1)
        g = jnp.sum(jnp.where(lane == h, gb, 0.0), axis=1, keepdims=True)
        self.beta = jnp.sum(jnp.where(lane == h + GDN_HEADS, gb, 0.0), axis=1, keepdims=True)
        ri = lax.broadcasted_iota(jnp.int32, (C, C), 0)
        ci = lax.broadcasted_iota(jnp.int32, (C, C), 1)
        self.causal = ri >= ci
        self.strict = ri > ci
        self.eye = (ri == ci).astype(F32)
        if sum_on_mxu:
            gcb = lax.dot_general(self.causal.astype(F32), jnp.broadcast_to(g, (C, SM_W)), NN,
                                  precision=lax.Precision.HIGHEST, preferred_element_type=F32)
        else:
            gcb = _running_sum(jnp.broadcast_to(g, (C, SM_W)))
        yield
        self.gcol = gcb[:, 0:1]
        grow = gcb.T[0:1, 0:C]
        self.decay = jnp.exp(jnp.where(self.causal, self.gcol - grow, -1e30))
        self.egc = jnp.exp(self.gcol)
        glast = gcb[C - 1:C, 0:1]
        self.elast = jnp.exp(glast - self.gcol)
        self.gl = jnp.exp(glast)
        self.q, self.k, self.v = q, k, v
        self.kb = k * self.beta
        m = _dot(self.kb, k, NT)
        n_ = _dot(q, k, NT)
        yield
        self.a_low = jnp.where(self.strict, m * self.decay, 0.0)
        self.p = n_ * self.decay
        self.qd = q * self.egc
        self.kd = k * self.elast
        self.bu = v * self.beta
        self.bw = self.kb * self.egc


GDN_HB = 8
GDN_HG = GDN_HEADS // GDN_HB


def _gdn_specs(n_of):
    C, W = GDN_CHUNK, 128 * GDN_HB
    q_spec = pl.BlockSpec((C, W), lambda g, n: (n_of(n), g))
    k_spec = pl.BlockSpec((C, W), lambda g, n: (n_of(n), g + GDN_HG))
    v_spec = pl.BlockSpec((C, W), lambda g, n: (n_of(n), g + 2 * GDN_HG))
    gb_spec = pl.BlockSpec((C, SM_W), lambda g, n: (n_of(n), 0))
    o_spec = pl.BlockSpec((C, W), lambda g, n: (n_of(n), g))
    s_spec = pl.BlockSpec((GDN_HB, None, GDN_DK, GDN_DV), lambda g, n: (g, n_of(n), 0, 0))
    t_spec = pl.BlockSpec((GDN_HB, None, C, C), lambda g, n: (g, n_of(n), 0, 0))
    return q_spec, k_spec, v_spec, gb_spec, o_spec, s_spec, t_spec


def _gdn_fwd(act, gb, *, name):
    M = act.shape[0]
    N = M // GDN_CHUNK

    def body(q_ref, k_ref, v_ref, gb_ref, o_ref, s_ref, t_ref, state):
        g, n = pl.program_id(0), pl.program_id(1)

        @pl.when(n == 0)
        def _():
            state[...] = jnp.zeros_like(state)

        gb_ = gb_ref[...]

        def head(hh):
            cols = slice(hh * 128, (hh + 1) * 128)
            c = _GdnChunk()
            yield from c.build(q_ref[:, cols], k_ref[:, cols], v_ref[:, cols], gb_, g * GDN_HB + hh, sum_on_mxu=True)
            tinv = yield from _unit_lower_inverse(c.a_low, c.eye)
            s = state[hh]
            s_ref[hh] = s
            t_ref[hh] = tinv
            u = _dot(tinv, c.bu)
            w = _dot(tinv, c.bw)
            yield
            vn = u - _dot(w, s)
            o1 = _dot(c.qd, s)
            yield
            o_ref[:, cols] = (o1 + _dot(c.p, vn)).astype(o_ref.dtype)
            state[hh] = c.gl * s + _dot(c.kd, vn, TN)

        _round_robin(head(hh) for hh in range(GDN_HB))

    q_spec, k_spec, v_spec, gb_spec, o_spec, s_spec, t_spec = _gdn_specs(lambda n: n)
    return pl.pallas_call(
        body, name=name, grid=(GDN_HG, N),
        in_specs=[q_spec, k_spec, v_spec, gb_spec], out_specs=[o_spec, s_spec, t_spec],
        out_shape=[jax.ShapeDtypeStruct((M, GDN_V), BF16),
                   jax.ShapeDtypeStruct((GDN_HEADS, N, GDN_DK, GDN_DV), F32),
                   jax.ShapeDtypeStruct((GDN_HEADS, N, GDN_CHUNK, GDN_CHUNK), F32)],
        scratch_shapes=[pltpu.VMEM((GDN_HB, GDN_DK, GDN_DV), F32)],
        compiler_params=_params("parallel", "arbitrary"),
    )(act, act, act, gb)


def _gdn_bwd(act, gb, do, s_all, t_all, *, name):
    M = act.shape[0]
    N = M // GDN_CHUNK
    C = GDN_CHUNK
    assert GDN_HG == 1

    def body(q_ref, k_ref, v_ref, gb_ref, do_ref, s_ref, t_ref, dact_ref, dgb_ref, dstate):
        g, n = pl.program_id(0), pl.program_id(1)

        @pl.when(n == 0)
        def _():
            dstate[...] = jnp.zeros_like(dstate)

        gb_ = gb_ref[...]
        last = lax.broadcasted_iota(jnp.int32, (C, 1), 0) == C - 1
        lane = lax.broadcasted_iota(jnp.int32, (C, SM_W), 1)
        def head(hh):
            cols = slice(hh * 128, (hh + 1) * 128)
            h = g * GDN_HB + hh
            c = _GdnChunk()
            yield from c.build(q_ref[:, cols], k_ref[:, cols], v_ref[:, cols], gb_, h, sum_on_mxu=False)
            tinv = t_ref[hh]
            tinv_t = tinv.T
            s = s_ref[hh]
            do_ = do_ref[:, cols]
            ds1 = dstate[hh]
            u = _dot(tinv, c.bu)
            w = _dot(tinv, c.bw)
            dqd = _dot(do_, s, NT)
            yield
            dvn0 = _dot(c.p, do_, TN) + _dot(c.kd, ds1)
            dst0 = _dot(c.qd, do_, TN) + c.gl * ds1
            yield
            vn = u - _dot(w, s)
            dvn = dvn0
            yield
            dp = jnp.where(c.causal, _dot(do_, vn, NT), 0.0)
            dstate[hh] = dst0 - _dot(w, dvn, TN)
            dkd = _dot(vn, ds1, NT)
            dw = -_dot(dvn, s, NT)
            dbu = _dot(tinv_t, dvn)
            dgl = jnp.sum(jnp.sum(s * ds1, axis=1, keepdims=True), axis=0, keepdims=True)
            yield
            dbw = _dot(tinv_t, dw)
            t1 = _dot(dbu, u, NT)
            yield
            da = jnp.where(c.strict, -(t1 + _dot(dbw, w, NT)), 0.0)
            dn_ = dp * c.decay
            dq0 = _dot(dn_, c.k)
            dk0 = _dot(dn_, c.q, TN)
            yield
            dm = da * c.decay
            e = da * c.a_low + dp * c.p
            dkb = _dot(dm, c.k) + dbw * c.egc
            dact_ref[:, GDN_QK + hh * 128:GDN_QK + (hh + 1) * 128] = (
                _dot(dm, c.kb, TN) + dk0 + dkb * c.beta + dkd * c.elast)
            dact_ref[:, cols] = dq0 + dqd * c.egc
            dact_ref[:, 2 * GDN_QK + hh * 128:2 * GDN_QK + (hh + 1) * 128] = dbu * c.beta
            dbeta = jnp.sum(dbu * c.v, axis=1, keepdims=True) + jnp.sum(dkb * c.k, axis=1, keepdims=True)
            t_kd = jnp.sum(dkd * c.kd, axis=1, keepdims=True)
            dgc = (jnp.sum(e, axis=1, keepdims=True) - jnp.sum(e.T, axis=1, keepdims=True)
                   + jnp.sum(dbw * c.bw, axis=1, keepdims=True) + jnp.sum(dqd * c.qd, axis=1, keepdims=True) - t_kd)
            dgc = dgc + jnp.where(last, jnp.sum(t_kd, axis=0, keepdims=True) + dgl * c.gl, 0.0)
            yield
            dg = _running_sum(jnp.broadcast_to(dgc, (C, SM_W)), reverse=True)
            dgb_ref[hh] = jnp.where(lane == h, dg, jnp.where(lane == h + GDN_HEADS, dbeta, 0.0))

        _round_robin(head(hh) for hh in range(GDN_HB))

    rev = lambda n: N - 1 - n
    q_spec, k_spec, v_spec, gb_spec, o_spec, s_spec, t_spec = _gdn_specs(rev)
    dgb_spec = pl.BlockSpec((GDN_HB, C, SM_W), lambda g, n: (g, rev(n), 0))
    return pl.pallas_call(
        body, name=name, grid=(GDN_HG, N),
        in_specs=[q_spec, k_spec, v_spec, gb_spec, o_spec, s_spec, t_spec],
        out_specs=[pl.BlockSpec((C, 2 * GDN_QK + GDN_V), lambda g, n: (rev(n), 0)), dgb_spec],
        out_shape=[jax.ShapeDtypeStruct((M, 2 * GDN_QK + GDN_V), F32),
                   jax.ShapeDtypeStruct((GDN_HEADS, M, SM_W), F32)],
        scratch_shapes=[pltpu.VMEM((GDN_HB, GDN_DK, GDN_DV), F32)],
        compiler_params=_params("parallel", "arbitrary"),
    )(act, act, act, gb, do, s_all, t_all)


GLA_STEP_ROWS = 64
GLA_SUB = GLA_STEP_ROWS // GLA_CHUNK


def _gla_cumsum(la):
    return _running_sum(la)


GLA_HALF = GLA_CHUNK // 2


def _gla_cross_factors(b):
    top = lax.broadcasted_iota(jnp.int32, b.shape, 0) < GLA_HALF
    bm = b[GLA_HALF - 1:GLA_HALF, :]
    late = jnp.where(top, 0.0, jnp.exp(jnp.minimum(b - bm, 0.0)))
    early = jnp.where(top, jnp.exp(jnp.minimum(bm - b, 0.0)), 0.0)
    return late, early


def _gla_half_decay(bh, ii):
    rj = lax.broadcasted_iota(jnp.int32, bh.shape, 0)
    return jnp.where(rj <= ii, jnp.exp(jnp.minimum(bh[ii:ii + 1, :] - bh, 0.0)), 0.0)


def _gla_scores_t(q, k, b):
    C, H = GLA_CHUNK, GLA_HALF
    lane = lax.broadcasted_iota(jnp.int32, (H, C), 1)
    halves = []
    for h0 in (0, H):
        qh, kh, bh = q[h0:h0 + H], k[h0:h0 + H], b[h0:h0 + H]
        sth = jnp.zeros((H, C), F32)
        for ii in range(H):
            si = jnp.sum(qh[ii:ii + 1, :] * kh * _gla_half_decay(bh, ii), axis=1, keepdims=True)
            sth = jnp.where(lane == h0 + ii, si, sth)
            if ii % 4 == 3:
                yield
        halves.append(sth)
    late, early = _gla_cross_factors(b)
    between = _dot(k * early, q * late, NT)
    yield
    return jnp.concatenate(halves, axis=0) + between


def _gla_specs(n_of):
    R = GLA_STEP_ROWS
    q_spec = pl.BlockSpec((R, GLA_QK), lambda n: (n_of(n), C_GQ // GLA_QK))
    k_spec = pl.BlockSpec((R, GLA_QK), lambda n: (n_of(n), C_GK // GLA_QK))
    v_spec = pl.BlockSpec((R, GLA_V), lambda n: (n_of(n), C_GV // GLA_V))
    la_spec = pl.BlockSpec((R, GLA_QK), lambda n: (n_of(n), 0))
    o_spec = pl.BlockSpec((R, GLA_V), lambda n: (n_of(n), 0))
    s_spec = pl.BlockSpec((GLA_HEADS, None, GLA_SUB, GLA_DV, GLA_DK), lambda n: (0, n_of(n), 0, 0, 0))
    return q_spec, k_spec, v_spec, la_spec, o_spec, s_spec


def _gla_fwd(proj, la, *, name):
    M = proj.shape[0]
    N = M // GLA_STEP_ROWS
    C = GLA_CHUNK

    def body(q_ref, k_ref, v_ref, la_ref, o_ref, s_ref, state):
        n = pl.program_id(0)

        @pl.when(n == 0)
        def _():
            state[...] = jnp.zeros_like(state)

        local = {}

        def within(hh, c):
            kc = slice(hh * GLA_DK, (hh + 1) * GLA_DK)
            vc = slice(hh * GLA_DV, (hh + 1) * GLA_DV)
            rows = slice(c * C, (c + 1) * C)
            q = q_ref[rows, kc].astype(F32) * (GLA_DK ** -0.5)
            k = k_ref[rows, kc].astype(F32)
            v = v_ref[rows, vc].astype(F32)
            b = _gla_cumsum(la_ref[rows, kc])
            yield
            blast = b[C - 1:C, :]
            sc_t = yield from _gla_scores_t(q, k, b)
            kv = _dot(v, k * jnp.exp(blast - b), TN)
            o2 = _dot(sc_t, v, TN)
            yield
            local[hh, c] = (q * jnp.exp(b), jnp.exp(blast), kv, o2)

        def across(hh):
            vc = slice(hh * GLA_DV, (hh + 1) * GLA_DV)
            st = state[hh]
            for c in range(GLA_SUB):
                qe, eblast, kv, o2 = local[hh, c]
                s_ref[hh, c] = st
                o1 = _dot(qe, st, NT)
                yield
                o_ref[c * C:(c + 1) * C, vc] = (o1 + o2).astype(o_ref.dtype)
                st = st * eblast + kv
            state[hh] = st

        _round_robin(within(hh, c) for c in range(GLA_SUB) for hh in range(GLA_HEADS))
        _round_robin(across(hh) for hh in range(GLA_HEADS))

    q_spec, k_spec, v_spec, la_spec, o_spec, s_spec = _gla_specs(lambda n: n)
    return pl.pallas_call(
        body, name=name, grid=(N,),
        in_specs=[q_spec, k_spec, v_spec, la_spec], out_specs=[o_spec, s_spec],
        out_shape=[jax.ShapeDtypeStruct((M, GLA_V), BF16),
                   jax.ShapeDtypeStruct((GLA_HEADS, N, GLA_SUB, GLA_DV, GLA_DK), F32)],
        scratch_shapes=[pltpu.VMEM((GLA_HEADS, GLA_DV, GLA_DK), F32)],
        compiler_params=_params("arbitrary"),
    )(proj, proj, proj, la)


def _gla_bwd(proj, la, do, s_all, d_proj, *, name):
    M = proj.shape[0]
    N = M // GLA_STEP_ROWS
    C = GLA_CHUNK
    qkv_w = 2 * GLA_QK + GLA_V
    assert C_GK == C_GQ + GLA_QK and C_GV == C_GK + GLA_QK and C_GQ % qkv_w == 0

    def body(q_ref, k_ref, v_ref, la_ref, do_ref, s_ref, _, dp_ref, dla_ref, dstate):
        n = pl.program_id(0)

        @pl.when(n == 0)
        def _():
            dstate[...] = jnp.zeros_like(dstate)

        H = GLA_HALF
        lane = lax.broadcasted_iota(jnp.int32, (C, C), 1)
        row = lax.broadcasted_iota(jnp.int32, (C, C), 0)
        ri = lax.broadcasted_iota(jnp.int32, (C, GLA_DK), 0)
        lane_h = lax.broadcasted_iota(jnp.int32, (H, C), 1)
        ri_h = lax.broadcasted_iota(jnp.int32, (H, GLA_DK), 0)
        cross = (row < H) & (lane >= H)
        def head(hh):
            kc = slice(hh * GLA_DK, (hh + 1) * GLA_DK)
            vc = slice(hh * GLA_DV, (hh + 1) * GLA_DV)
            ds1 = dstate[hh]
            for c in reversed(range(GLA_SUB)):
                rows = slice(c * C, (c + 1) * C)
                q = q_ref[rows, kc].astype(F32) * (GLA_DK ** -0.5)
                k = k_ref[rows, kc].astype(F32)
                v = v_ref[rows, vc].astype(F32)
                b = _gla_cumsum(la_ref[rows, kc])
                do_ = do_ref[rows, vc]
                st = s_ref[hh, c]
                dsc_t = _dot(v, do_, NT)
                dqe = _dot(do_, st)
                dke = _dot(v, ds1)
                yield
                blast = b[C - 1:C, :]
                eb = jnp.exp(b)
                elast = jnp.exp(blast - b)
                eblast = jnp.exp(blast)
                qe = q * eb
                ke = k * elast
                dv2 = _dot(ke, ds1, NT)
                ds_new = _dot(do_, qe, TN)
                deblast = jnp.sum(st * ds1, axis=0, keepdims=True)
                sc_halves, dq_halves, dk_halves = [], [], []
                for h0 in (0, H):
                    qh, kh, bh, dsch = q[h0:h0 + H], k[h0:h0 + H], b[h0:h0 + H], dsc_t[h0:h0 + H]
                    sch = jnp.zeros((H, C), F32)
                    dqh = jnp.zeros((H, GLA_DK), F32)
                    dkh = jnp.zeros((H, GLA_DK), F32)
                    for ii in range(H):
                        f = _gla_half_decay(bh, ii)
                        kf = kh * f
                        si = jnp.sum(qh[ii:ii + 1, :] * kf, axis=1, keepdims=True)
                        sch = jnp.where(lane_h == h0 + ii, si, sch)
                        dsi = jnp.sum(jnp.where(lane_h == h0 + ii, dsch, 0.0), axis=1, keepdims=True)
                        dqh = jnp.where(ri_h == ii, jnp.sum(dsi * kf, axis=0, keepdims=True), dqh)
                        dkh = dkh + (dsi * f) * qh[ii:ii + 1, :]
                        if ii % 4 == 3:
                            yield
                    sc_halves.append(sch)
                    dq_halves.append(dqh)
                    dk_halves.append(dkh)
                late, early = _gla_cross_factors(b)
                q_late, k_early = q * late, k * early
                dsc_x = jnp.where(cross, dsc_t, 0.0)
                sc_t = jnp.concatenate(sc_halves, axis=0) + _dot(k_early, q_late, NT)
                dq_sc = jnp.concatenate(dq_halves, axis=0) + _dot(dsc_x, k_early, TN) * late
                dk_sc = jnp.concatenate(dk_halves, axis=0) + _dot(dsc_x, q_late) * early
                yield
                dv1 = _dot(sc_t, do_)
                dp_ref[rows, kc] = ((dq_sc + dqe * eb) * (GLA_DK ** -0.5)).astype(dp_ref.dtype)
                dp_ref[rows, GLA_QK + hh * GLA_DK:GLA_QK + (hh + 1) * GLA_DK] = (dk_sc + dke * elast).astype(dp_ref.dtype)
                t_ke = dke * ke
                db = q * dq_sc - k * dk_sc + dqe * qe - t_ke
                db = db + jnp.where(ri == C - 1, jnp.sum(t_ke, axis=0, keepdims=True) + deblast * eblast, 0.0)
                dla = _running_sum(db, reverse=True)
                yield
                dp_ref[rows, 2 * GLA_QK + hh * GLA_DV:2 * GLA_QK + (hh + 1) * GLA_DV] = (dv1 + dv2).astype(dp_ref.dtype)
                dla_ref[rows, kc] = dla
                ds1 = ds1 * eblast + ds_new
            dstate[hh] = ds1

        _round_robin(head(hh) for hh in range(GLA_HEADS))

    rev = lambda n: N - 1 - n
    q_spec, k_spec, v_spec, la_spec, o_spec, s_spec = _gla_specs(rev)
    return pl.pallas_call(
        body, name=name, grid=(N,),
        in_specs=[q_spec, k_spec, v_spec, la_spec, o_spec, s_spec, _ANY],
        out_specs=[pl.BlockSpec((GLA_STEP_ROWS, qkv_w), lambda n: (rev(n), C_GQ // qkv_w)), la_spec],
        out_shape=[jax.ShapeDtypeStruct(d_proj.shape, d_proj.dtype), jax.ShapeDtypeStruct((M, GLA_QK), F32)],
        input_output_aliases={6: 0},
        scratch_shapes=[pltpu.VMEM((GLA_HEADS, GLA_DV, GLA_DK), F32)],
        compiler_params=_params("arbitrary"),
    )(proj, proj, proj, la, do, s_all, d_proj)


def _head_norm(o, wn):
    r = lax.rsqrt(jnp.mean(o * o, axis=-1, keepdims=True) + NORM_EPS)
    return o * r, r


def _mix_heads():
    heads = [(0, GDN_DV, hh * GDN_DV, hh * GDN_DV) for hh in range(GDN_HEADS)]
    heads += [(1, GLA_DV, GDN_V + hh * GLA_DV, hh * GLA_DV) for hh in range(GLA_HEADS)]
    return heads


def _mix_fwd(o_gdn, o_gla, proj, wn_gdn, wn_gla, *, name):
    M = proj.shape[0]
    tm = _tile(M, 344, 16)

    def body(og_ref, ol_ref, z_ref, r_ref, wg_ref, wl_ref, m_ref):
        srcs = ((og_ref, z_ref, wg_ref), (ol_ref, r_ref, wl_ref))
        for grp, width, mcol, col in _mix_heads():
            o_ref, gate_ref, w_ref = srcs[grp]
            xhat, _ = _head_norm(o_ref[:, col:col + width].astype(F32), None)
            gate, _ = _silu_and_grad(gate_ref[:, col:col + width].astype(F32))
            m_ref[:, mcol:mcol + width] = (xhat * w_ref[...] * gate).astype(m_ref.dtype)

    full = lambda s: pl.BlockSpec(s, lambda i: (0, 0))
    return pl.pallas_call(
        body, name=name, grid=(M // tm,),
        in_specs=[pl.BlockSpec((tm, GDN_V), lambda i: (i, 0)), pl.BlockSpec((tm, GLA_V), lambda i: (i, 0)),
                  pl.BlockSpec((tm, GDN_V), lambda i: (i, C_Z // GDN_V)),
                  pl.BlockSpec((tm, GLA_V), lambda i: (i, C_GR // GLA_V)),
                  full((1, GDN_DV)), full((1, GLA_DV))],
        out_specs=pl.BlockSpec((tm, D_MODEL), lambda i: (i, 0)),
        out_shape=jax.ShapeDtypeStruct((M, D_MODEL), BF16),
        compiler_params=_params("parallel"),
    )(o_gdn, o_gla, proj, proj, wn_gdn, wn_gla)


def _mix_bwd(o_gdn, o_gla, proj, wn_gdn, wn_gla, dmixed, *, name):
    M = proj.shape[0]
    tm = _tile(M, 344, 16)
    g_ = M // tm
    assert C_Z == 0 and C_GR == GDN_V

    def body(og_ref, ol_ref, z_ref, r_ref, wg_ref, wl_ref, dm_ref,
             dog_ref, dol_ref, dzr_ref, dwg_ref, dwl_ref):
        i = pl.program_id(0)
        srcs = ((og_ref, z_ref, wg_ref, dog_ref), (ol_ref, r_ref, wl_ref, dol_ref))
        dws = [jnp.zeros((1, GDN_DV), F32), jnp.zeros((1, GLA_DV), F32)]
        for grp, width, mcol, col in _mix_heads():
            o_ref, gate_ref, w_ref, do_ref = srcs[grp]
            cols = slice(col, col + width)
            xhat, r = _head_norm(o_ref[:, cols].astype(F32), None)
            gate, dgate_dc = _silu_and_grad(gate_ref[:, cols].astype(F32))
            dm = dm_ref[:, mcol:mcol + width]
            dzr_ref[:, mcol:mcol + width] = (dm * xhat * w_ref[...] * dgate_dc).astype(dzr_ref.dtype)
            dnorm = dm * gate
            dws[grp] = dws[grp] + jnp.sum(dnorm * xhat, axis=0, keepdims=True)
            dxhat = dnorm * w_ref[...]
            do_ref[:, cols] = r * (dxhat - xhat * jnp.mean(dxhat * xhat, axis=-1, keepdims=True))

        @pl.when(i == 0)
        def _():
            dwg_ref[...] = dws[0]
            dwl_ref[...] = dws[1]

        @pl.when(i > 0)
        def _():
            dwg_ref[...] += dws[0]
            dwl_ref[...] += dws[1]

    full = lambda s: pl.BlockSpec(s, lambda i: (0, 0))
    half = pl.BlockSpec((tm, GDN_V), lambda i: (i, 0))
    return pl.pallas_call(
        body, name=name, grid=(g_,),
        in_specs=[half, half, pl.BlockSpec((tm, GDN_V), lambda i: (i, C_Z // GDN_V)),
                  pl.BlockSpec((tm, GLA_V), lambda i: (i, C_GR // GLA_V)),
                  full((1, GDN_DV)), full((1, GLA_DV)), pl.BlockSpec((tm, D_MODEL), lambda i: (i, 0))],
        out_specs=[half, half, pl.BlockSpec((tm, GDN_V + GLA_V), lambda i: (i, 0)),
                   full((1, GDN_DV)), full((1, GLA_DV))],
        out_shape=[jax.ShapeDtypeStruct((M, GDN_V), F32), jax.ShapeDtypeStruct((M, GLA_V), F32),
                   jax.ShapeDtypeStruct((M, D_PROJ), BF16),
                   jax.ShapeDtypeStruct((1, GDN_DV), F32), jax.ShapeDtypeStruct((1, GLA_DV), F32)],
        compiler_params=_params("arbitrary"),
    )(o_gdn, o_gla, proj, proj, wn_gdn, wn_gla, dmixed)


def _row_chunks(tm, parts=2):
    if tm % (16 * parts):
        return [slice(0, tm)]
    return [slice(p * (tm // parts), (p + 1) * (tm // parts)) for p in range(parts)]


def _swiglu_fwd(n, w_gate_t, w_up_t, *, name, tm=1376, tn=512):
    M, D = n.shape
    F = w_gate_t.shape[0]
    tm, tn = _tile(M, tm, 16), _tile(F, tn, 128)

    def body(n_ref, wg_ref, wu_ref, g_ref, u_ref, a_ref):
        wg, wu = wg_ref[...], wu_ref[...]
        for rows in _row_chunks(tm):
            x = n_ref[rows, :]
            g = _dot(x, wg, NT)
            u = _dot(x, wu, NT)
            s, _ = _silu_and_grad(g)
            g_ref[rows, :] = g.astype(g_ref.dtype)
            u_ref[rows, :] = u.astype(u_ref.dtype)
            a_ref[rows, :] = (s * u).astype(a_ref.dtype)

    w_spec = pl.BlockSpec((tn, D), lambda i, j: (j, 0))
    o_spec = pl.BlockSpec((tm, tn), lambda i, j: (i, j))
    return pl.pallas_call(
        body, name=name, grid=(M // tm, F // tn),
        in_specs=[pl.BlockSpec((tm, D), lambda i, j: (i, 0)), w_spec, w_spec], out_specs=[o_spec] * 3,
        out_shape=[jax.ShapeDtypeStruct((M, F), BF16)] * 3, compiler_params=_params("parallel", "parallel"),
    )(n, w_gate_t, w_up_t)


def _swiglu_bwd(dh, w_down, gate, up, *, name, after=None, tm=1376, tn=512):
    M, D = dh.shape
    F = w_down.shape[0]
    tm, tn = _tile(M, tm, 16), _tile(F, tn, 128)
    n_after = 0 if after is None else 1

    def body(*refs):
        dh_ref, w_ref, g_ref, u_ref, dg_ref, du_ref = refs[n_after:]
        w = w_ref[...]
        for rows in _row_chunks(tm):
            da = _dot(dh_ref[rows, :], w, NT)
            s, ds = _silu_and_grad(g_ref[rows, :].astype(F32))
            dg_ref[rows, :] = (da * u_ref[rows, :].astype(F32) * ds).astype(dg_ref.dtype)
            du_ref[rows, :] = (da * s).astype(du_ref.dtype)

    o_spec = pl.BlockSpec((tm, tn), lambda i, j: (i, j))
    return pl.pallas_call(
        body, name=name, grid=(M // tm, F // tn),
        in_specs=[_ANY] * n_after + [pl.BlockSpec((tm, D), lambda i, j: (i, 0)),
                                     pl.BlockSpec((tn, D), lambda i, j: (j, 0)), o_spec, o_spec],
        out_specs=[o_spec, o_spec], out_shape=[jax.ShapeDtypeStruct((M, F), BF16)] * 2,
        compiler_params=_params("parallel", "parallel"),
    )(*((after,) if n_after else ()), dh, w_down, gate, up)


def _adamw_update(w, g, m, v):
    nm = ADAM_B1 * m + (1.0 - ADAM_B1) * g
    nv = ADAM_B2 * v + (1.0 - ADAM_B2) * (g * g)
    m_hat = nm / (1.0 - ADAM_B1 ** ADAM_STEP)
    v_hat = nv / (1.0 - ADAM_B2 ** ADAM_STEP)
    return -ADAM_LR * (m_hat / (jnp.sqrt(v_hat) + ADAM_EPS) + ADAM_WD * w), nm, nv


def _adamw(w, g, m, v, *, name):
    shape = w.shape
    cols = shape[-1]
    rows = w.size // cols
    w2, g2, m2, v2 = (t.reshape(rows, cols) for t in (w, g, m, v))
    if rows % 8 == 0 or cols % 128 != 0:
        tr, tc = (_tile(rows, 256, 8) if rows % 8 == 0 else rows), cols
    else:
        tr, tc = rows, _tile(cols, 256, 128)

    def body(w_ref, g_ref, m_ref, v_ref, d_ref, nm_ref, nv_ref):
        d_ref[...], nm_ref[...], nv_ref[...] = _adamw_update(w_ref[...], g_ref[...], m_ref[...], v_ref[...])

    blk = pl.BlockSpec((tr, tc), lambda i, j: (i, j))
    outs = pl.pallas_call(
        body, name=name, grid=(rows // tr, cols // tc), in_specs=[blk] * 4, out_specs=[blk] * 3,
        out_shape=[jax.ShapeDtypeStruct((rows, cols), F32)] * 3, compiler_params=_params("parallel", "parallel"),
    )(w2, g2, m2, v2)
    return tuple(t.reshape(shape) for t in outs)


def _sum_slabs(x, *, name):
    _, R, C = x.shape
    sub = 16 if x.dtype == BF16 else 8
    if R % sub == 0:
        tr, tc = _tile(R, 128, sub), C
    else:
        tr, tc = R, _tile(C, 256, 128)

    def body(x_ref, o_ref):
        acc = x_ref[0].astype(F32)
        for s in range(1, N_DEV):
            acc = acc + x_ref[s].astype(F32)
        o_ref[...] = acc

    return pl.pallas_call(
        body, name=name, grid=(R // tr, C // tc),
        in_specs=[pl.BlockSpec((N_DEV, tr, tc), lambda i, j: (0, i, j))],
        out_specs=pl.BlockSpec((tr, tc), lambda i, j: (i, j)),
        out_shape=jax.ShapeDtypeStruct((R, C), F32), compiler_params=_params("parallel", "parallel"),
    )(x)


def _sum_adamw(x, w, m, v, *, name):
    _, R, C = x.shape
    if R % 16 == 0:
        tr, tc = _tile(R, 128, 16), C
    else:
        tr, tc = R, _tile(C, 256, 128)

    def body(x_ref, w_ref, m_ref, v_ref, g_ref, d_ref, nm_ref, nv_ref):
        g = x_ref[0].astype(F32)
        for s in range(1, N_DEV):
            g = g + x_ref[s].astype(F32)
        g_ref[...] = g
        d_ref[...], nm_ref[...], nv_ref[...] = _adamw_update(w_ref[...], g, m_ref[...], v_ref[...])

    blk = pl.BlockSpec((tr, tc), lambda i, j: (i, j))
    return pl.pallas_call(
        body, name=name, grid=(R // tr, C // tc),
        in_specs=[pl.BlockSpec((N_DEV, tr, tc), lambda i, j: (0, i, j)), blk, blk, blk], out_specs=[blk] * 4,
        out_shape=[jax.ShapeDtypeStruct((R, C), F32)] * 4, compiler_params=_params("parallel", "parallel"),
    )(x, w, m, v)


def _peers():
    x, y, c = lax.axis_index("x"), lax.axis_index("y"), lax.axis_index("c")
    me = 4 * x + 2 * y + c
    peers = []
    for k in range(1, N_DEV):
        px = 1 - x if k & 4 else x
        py = 1 - y if k & 2 else y
        pc = 1 - c if k & 1 else c
        peers.append(((px, py, pc), 4 * px + 2 * py + pc))
    return me, peers


_HBM = pl.BlockSpec(memory_space=pltpu.HBM)
_SEM = pl.BlockSpec(memory_space=pltpu.SEMAPHORE)
_EFFECT = pltpu.SideEffectType.DATAFLOW_SIDE_EFFECTING


PLAN_GATHER = tuple((k, "x", 0) for k in range(1, N_DEV))
PLAN_SCATTER = tuple((k, "xk", 0) for k in range(1, N_DEV))
PLAN_GATHER_CHIPS = tuple((k, "x", 0) for k in (1, 2, 4, 6))
PLAN_GATHER_PASS_ON = tuple((1, ("land", q), q) for q in (2, 4, 6))


def _plan_refs(plan, j, x_ref, land_ref, me, peers, receiving):
    k, source, r = plan[j]
    index_of = lambda q: me if q == 0 else peers[q - 1][1]
    pos, target = peers[k - 1]
    if source == "x":
        src = x_ref
    elif source == "xk":
        src = x_ref.at[target]
    else:
        src = land_ref.at[index_of(source[1])]
    return pos, src, land_ref.at[index_of(k ^ r) if receiving else index_of(r)]


def _exchange_start(x, *, plan, name, after=None, land=None, slab=None):
    n_after = 0 if after is None else 1
    n = len(plan)

    def body(*refs):
        x_ref, land_ref, send_sems, recv_sems, _, _, token = refs[n_after:]
        me, peers = _peers()
        for j in range(n):
            pos, src, dst = _plan_refs(plan, j, x_ref, land_ref, me, peers, receiving=False)
            pltpu.make_async_remote_copy(src_ref=src, dst_ref=dst, send_sem=send_sems.at[j], recv_sem=recv_sems.at[j],
                                         device_id=pos, device_id_type=pl.DeviceIdType.MESH).start()
        token[...] = jnp.zeros_like(token)

    if land is None:
        land = lax.empty((N_DEV,) + tuple(slab), x.dtype)
    return pl.pallas_call(
        body, name=name,
        out_shape=(pltpu.SemaphoreType.DMA((n,)), pltpu.SemaphoreType.DMA((n,)),
                   pltpu.HBM(x.shape, x.dtype), pltpu.HBM(land.shape, land.dtype), jax.ShapeDtypeStruct((8, 128), F32)),
        in_specs=[_ANY] * n_after + [_HBM, _HBM],
        out_specs=(_SEM, _SEM, _HBM, _HBM, pl.BlockSpec(memory_space=pltpu.VMEM)),
        input_output_aliases={n_after: 2, n_after + 1: 3},
        compiler_params=pltpu.CompilerParams(has_side_effects=_EFFECT),
    )(*((after,) if n_after else ()), pltpu.with_memory_space_constraint(x, pltpu.HBM),
      pltpu.with_memory_space_constraint(land, pltpu.HBM))


def _exchange_wait(handle, after, *, plan, name):
    send_sems, recv_sems, x_thru, land_thru, _ = handle
    afters = list(after) if isinstance(after, (list, tuple)) else [after]

    def body(x_ref, land_ref, send_sems, recv_sems, *rest):
        me, peers = _peers()
        for j in range(len(plan)):
            pos, src, dst = _plan_refs(plan, j, x_ref, land_ref, me, peers, receiving=True)
            cp = pltpu.make_async_remote_copy(src_ref=src, dst_ref=dst, send_sem=send_sems.at[j], recv_sem=recv_sems.at[j],
                                              device_id=pos, device_id_type=pl.DeviceIdType.MESH)
            cp.wait_send()
            cp.wait_recv()

    return pl.pallas_call(
        body, name=name,
        out_shape=(pltpu.HBM(x_thru.shape, x_thru.dtype), pltpu.HBM(land_thru.shape, land_thru.dtype)),
        in_specs=[_HBM, _HBM, _SEM, _SEM] + [_ANY] * len(afters), out_specs=(_HBM, _HBM),
        input_output_aliases={0: 0, 1: 1}, compiler_params=pltpu.CompilerParams(has_side_effects=_EFFECT),
    )(x_thru, land_thru, send_sems, recv_sems, *afters)


def _to_proj_rows(t):
    z = jnp.zeros((D_PROJ - C_SM - 2 * GDN_HEADS - GLA_RANK,) + t.shape[1:], t.dtype)
    return jnp.concatenate([t[R_Z:R_A], t[R_GR:R_LR], t[R_GQ:R_GR], t[:R_Z], t[R_A:R_GQ], t[R_LR:], z], axis=0)


def _from_proj_rows(t):
    ab = C_SM + 2 * GDN_HEADS
    return jnp.concatenate([t[C_QKV:C_SM], t[C_Z:C_GR], t[C_SM:ab], t[C_GQ:C_QKV], t[C_GR:C_GQ],
                            t[ab:ab + GLA_RANK]], axis=0)


def _local_step(x, target, meta, attn_nw, conv_w, a_log, dt_bias, gdn_nw, w2, b2, gla_nw, ffn_nw, final_nw,
                fetch, emit, start=None):
    head = jnp.concatenate([jnp.zeros((ROW_PAD, D_MODEL), F32), meta], axis=0)
    conv_w8 = jnp.concatenate([conv_w, jnp.zeros((8 - CONV_K, conv_w.shape[1]), F32)], axis=0)
    w2p = jnp.zeros((SM_W, GLA_QK), F32).at[2 * GDN_HEADS:2 * GDN_HEADS + GLA_RANK].set(w2)
    alog_p = jnp.zeros((1, SM_W), F32).at[:, :GDN_HEADS].set(a_log)
    dt_p = jnp.zeros((1, SM_W), F32).at[:, :GDN_HEADS].set(dt_bias)

    h0, n1 = _embed_norm(head, x, attn_nw, name="attn_norm", after=start)
    w_in_t = fetch("w_in_t", (n1, conv_w8, w2p, alog_p, dt_p))
    proj = _matmul(n1, w_in_t, mode="nt", name="in_proj", out_dtype=BF16)
    gb, la = _gates_fwd(proj, w2p, b2, alog_p, dt_p, name="gates")
    act = _prep_fwd(proj, conv_w8, name="gdn_prep")
    o_gdn, s_gdn, t_gdn = _gdn_fwd(act, gb, name="gdn_fwd")
    o_gla, s_gla = _gla_fwd(proj, la, name="gla_fwd")
    mixed = _mix_fwd(o_gdn, o_gla, proj, gdn_nw, gla_nw, name="mix")
    w_out = fetch("w_out", mixed)
    h1 = _matmul(mixed, w_out, mode="nn", add=h0, name="out_proj")
    n2 = _rmsnorm_fwd(h1, ffn_nw, name="ffn_norm")
    w_gate_t, w_up_t = fetch("w_gate_t", n2), fetch("w_up_t", n2)
    gate, up, hid = _swiglu_fwd(n2, w_gate_t, w_up_t, name="swiglu")
    w_down = fetch("w_down", hid)
    h2 = _matmul(hid, w_down, mode="nn", add=h1, name="ffn_down", tm=1376, tn=256)
    dh2, dh2_b, d_final_nw, loss = _loss_head(h2, final_nw, target, name="loss_head")

    wg = dict(mode="tn", out_dtype=BF16, tn=512)
    tok = emit("w_down", _matmul(hid, dh2_b, name="d_w_down", tm=704, **wg))
    d_gate, d_up = _swiglu_bwd(dh2_b, w_down, gate, up, name="d_swiglu", after=tok)
    tok = emit("w_gate_t", _matmul(d_gate, n2, name="d_w_gate", tm=704, **wg))
    tok = emit("w_up_t", _matmul(d_up, n2, name="d_w_up", tm=704, after=tok, **wg))
    d_n2 = _matmul_pair(d_gate, w_gate_t, d_up, w_up_t, name="d_n2", after=tok)
    dh1, dh1_b, d_ffn_nw = _rmsnorm_bwd(h1, ffn_nw, d_n2, dh2, name="d_ffn_norm")

    tok = emit("w_out", _matmul(mixed, dh1_b, name="d_w_out", tm=512, **wg))
    d_mixed = _matmul(dh1_b, w_out, mode="nt", name="d_mixed", after=tok)
    do_gdn, do_gla, d_proj, d_gdn_nw, d_gla_nw = _mix_bwd(o_gdn, o_gla, proj, gdn_nw, gla_nw, d_mixed, name="d_mix")
    d_proj, d_la = _gla_bwd(proj, la, do_gla, s_gla, d_proj, name="gla_bwd")
    dact, dgb_heads = _gdn_bwd(act, gb, do_gdn, s_gdn, t_gdn, name="gdn_bwd")
    d_proj, d_w2p, d_b2, d_alog, d_dt = _gates_bwd(proj, w2p, b2, alog_p, dt_p, dgb_heads, d_la, d_proj, name="d_gates")
    d_proj, d_conv_w8 = _prep_bwd(proj, conv_w8, dact, d_proj, name="d_gdn_prep")
    tok = emit("w_in_t", _matmul(d_proj, n1, name="d_w_in", tm=768, **wg))
    d_n1 = _matmul(d_proj, w_in_t, mode="nn", name="d_n1", tm=688, after=tok)
    grad_x, d_head, d_attn_nw = _embed_norm_bwd(h0, attn_nw, d_n1, dh1, name="d_attn_norm")

    return dict(
        loss=loss[0, 0], grad_x=grad_x, meta=d_head[ROW_PAD:HEAD_ROWS], attn_nw=d_attn_nw,
        conv_w=d_conv_w8[:CONV_K], a_log=d_alog[:, :GDN_HEADS], dt_bias=d_dt[:, :GDN_HEADS], gdn_nw=d_gdn_nw,
        w2=d_w2p[2 * GDN_HEADS:2 * GDN_HEADS + GLA_RANK], b2=d_b2, gla_nw=d_gla_nw, ffn_nw=d_ffn_nw,
        final_nw=d_final_nw)


SMALL_ROWS = 32


def kernel(x, meta_tokens, attn_norm_w, w_in, gdn_conv_w, gdn_a_log, gdn_dt_bias, gdn_norm_w, gla_gate_w2, gla_gate_b, gla_norm_w, w_out, ffn_norm_w, w_gate, w_up, w_down, final_norm_w, loss_target, m_meta_tokens, m_attn_norm_w, m_w_in, m_gdn_conv_w, m_gdn_a_log, m_gdn_dt_bias, m_gdn_norm_w, m_gla_gate_w2, m_gla_gate_b, m_gla_norm_w, m_w_out, m_ffn_norm_w, m_w_gate, m_w_up, m_w_down, m_final_norm_w, v_meta_tokens, v_attn_norm_w, v_w_in, v_gdn_conv_w, v_gdn_a_log, v_gdn_dt_bias, v_gdn_norm_w, v_gla_gate_w2, v_gla_gate_b, v_gla_norm_w, v_w_out, v_ffn_norm_w, v_w_gate, v_w_up, v_w_down, v_final_norm_w):
    me = 4 * lax.axis_index("x") + 2 * lax.axis_index("y") + lax.axis_index("c")

    n_conv = gdn_conv_w.shape[2]
    n_w2 = gla_gate_w2.shape[2]
    n_meta = meta_tokens.shape[1]
    small = jnp.zeros((40, n_conv), F32)
    small = small.at[0:N_META, :n_meta].set(meta_tokens)
    small = small.at[N_META:N_META + CONV_K, :].set(gdn_conv_w[0])
    small = small.at[24:24 + GLA_RANK, :n_w2].set(gla_gate_w2[0])
    small_h = _exchange_start(small, plan=PLAN_GATHER, slab=small.shape, name="gather_small_start")

    w_in_slab = w_in[0].T.astype(BF16)
    in_h = _exchange_start(w_in_slab, plan=PLAN_GATHER_CHIPS, slab=w_in_slab.shape, name="gather_w_in_start",
                           after=small_h[4])
    handles, tok = {}, in_h[4]
    for wname, slab in (("w_out", w_out[0]), ("w_gate_t", w_gate[0].T), ("w_up_t", w_up[0].T), ("w_down", w_down[0])):
        slab = slab.astype(BF16)
        handles[wname] = _exchange_start(slab, plan=PLAN_GATHER, slab=slab.shape, name="gather_" + wname + "_start", after=tok)
        tok = handles[wname][4]

    own, small_all = _exchange_wait(small_h, tok, plan=PLAN_GATHER, name="gather_small_wait")
    small_all = lax.dynamic_update_index_in_dim(small_all, own, me, 0)
    meta_f = small_all[:, 0:N_META, :n_meta].transpose(1, 0, 2).reshape(N_META, D_MODEL)
    conv_f = small_all[:, N_META:N_META + CONV_K, :].transpose(1, 0, 2).reshape(CONV_K, N_DEV * n_conv)
    w2_f = small_all[:, 24:24 + GLA_RANK, :n_w2].transpose(1, 0, 2).reshape(GLA_RANK, N_DEV * n_w2)

    def fetch(name, after):
        if name == "w_in_t":
            own, got = _exchange_wait(in_h, after, plan=PLAN_GATHER_CHIPS, name="gather_w_in_wait")
            pass_h = _exchange_start(own, plan=PLAN_GATHER_PASS_ON, land=got, name="pass_w_in_start")
            own, got = _exchange_wait(pass_h, pass_h[4], plan=PLAN_GATHER_PASS_ON, name="pass_w_in_wait")
            got = lax.dynamic_update_index_in_dim(got, own, me, 0)
            return _to_proj_rows(got.reshape(D_IN, D_MODEL))
        own, got = _exchange_wait(handles[name], after, plan=PLAN_GATHER, name="gather_" + name + "_wait")
        got = lax.dynamic_update_index_in_dim(got, own, me, 0)
        return got.reshape(N_DEV * got.shape[1], D_MODEL)

    sent = {}

    def emit(name, grad):
        if name == "w_in_t":
            grad = _from_proj_rows(grad)
        parts = grad.reshape(N_DEV, grad.shape[0] // N_DEV, D_MODEL)
        sent[name] = _exchange_start(parts, plan=PLAN_SCATTER, slab=parts.shape[1:], name="scatter_" + name + "_start")
        return sent[name][4]

    g = _local_step(x[0], loss_target[0], meta_f, attn_norm_w, conv_f, gdn_a_log, gdn_dt_bias, gdn_norm_w, w2_f,
                    gla_gate_b, gla_norm_w, ffn_norm_w, final_norm_w.reshape(1, D_MODEL), fetch, emit, start=tok)

    misc = jnp.concatenate([g["a_log"], g["dt_bias"], g["gdn_nw"], g["gla_nw"], g["b2"], g["loss"].reshape(1, 1)], axis=1)
    n_misc = misc.shape[1]
    misc = jnp.pad(misc, ((0, 0), (0, D_MODEL - n_misc)))
    rows = jnp.concatenate([g["attn_nw"], g["ffn_nw"], g["final_nw"], misc, g["meta"],
                            g["conv_w"].reshape(-1, D_MODEL), g["w2"].reshape(-1, D_MODEL)], axis=0)
    rows = jnp.pad(rows, ((0, SMALL_ROWS - rows.shape[0]), (0, 0)))
    rows_h = _exchange_start(rows, plan=PLAN_GATHER, slab=rows.shape, name="gather_small_grads_start")

    big = {}
    after = rows_h[4]
    for name, w, m, v, transposed in (("w_down", w_down, m_w_down, v_w_down, False), ("w_gate_t", w_gate, m_w_gate, v_w_gate, True),
                                      ("w_up_t", w_up, m_w_up, v_w_up, True), ("w_out", w_out, m_w_out, v_w_out, False),
                                      ("w_in_t", w_in, m_w_in, v_w_in, True)):
        own, got = _exchange_wait(sent[name], after, plan=PLAN_SCATTER, name="scatter_" + name + "_wait")
        got = lax.dynamic_update_index_in_dim(got, lax.dynamic_index_in_dim(own, me, 0, keepdims=False), me, 0)
        local = [t[0].T if transposed else t[0] for t in (w, m, v)]
        res = _sum_adamw(got, *local, name="adamw_" + name)
        big[name] = [t.T[None] if transposed else t[None] for t in res]
        after = res[0]

    own, got = _exchange_wait(rows_h, after, plan=PLAN_GATHER, name="gather_small_grads_wait")
    tot = _sum_slabs(lax.dynamic_update_index_in_dim(got, own, me, 0), name="sum_small_grads")
    grad_attn_nw, grad_ffn_nw, grad_final_nw = tot[0:1], tot[1:2], tot[2]
    grad_a_log = tot[3:4, 0:8]
    grad_dt = tot[3:4, 8:16]
    grad_gdn_nw = tot[3:4, 16:16 + GDN_DV]
    grad_gla_nw = tot[3:4, 144:144 + GLA_DV]
    grad_b2 = tot[3:4, 400:400 + GLA_QK]
    loss = tot[3, n_misc - 1]
    r0 = 4 + N_META
    grad_meta = lax.dynamic_slice(tot[4:r0], (0, me * n_meta), (N_META, n_meta))
    r1 = r0 + CONV_K * N_DEV * n_conv // D_MODEL
    grad_conv = lax.dynamic_slice(tot[r0:r1].reshape(CONV_K, N_DEV * n_conv), (0, me * n_conv), (CONV_K, n_conv))[None]
    r2 = r1 + GLA_RANK * N_DEV * n_w2 // D_MODEL
    grad_w2 = lax.dynamic_slice(tot[r1:r2].reshape(GLA_RANK, N_DEV * n_w2), (0, me * n_w2), (GLA_RANK, n_w2))[None]

    weights = [meta_tokens, attn_norm_w, w_in, gdn_conv_w, gdn_a_log, gdn_dt_bias, gdn_norm_w, gla_gate_w2,
               gla_gate_b, gla_norm_w, w_out, ffn_norm_w, w_gate, w_up, w_down, final_norm_w]
    grads = [grad_meta, grad_attn_nw, "w_in_t", grad_conv, grad_a_log, grad_dt, grad_gdn_nw, grad_w2,
             grad_b2, grad_gla_nw, "w_out", grad_ffn_nw, "w_gate_t", "w_up_t", "w_down", grad_final_nw]
    ms = [m_meta_tokens, m_attn_norm_w, m_w_in, m_gdn_conv_w, m_gdn_a_log, m_gdn_dt_bias, m_gdn_norm_w,
          m_gla_gate_w2, m_gla_gate_b, m_gla_norm_w, m_w_out, m_ffn_norm_w, m_w_gate, m_w_up, m_w_down, m_final_norm_w]
    vs = [v_meta_tokens, v_attn_norm_w, v_w_in, v_gdn_conv_w, v_gdn_a_log, v_gdn_dt_bias, v_gdn_norm_w,
          v_gla_gate_w2, v_gla_gate_b, v_gla_norm_w, v_w_out, v_ffn_norm_w, v_w_gate, v_w_up, v_w_down, v_final_norm_w]
    outs = [[], [], [], []]
    for idx, (w, gr, m, v) in enumerate(zip(weights, grads, ms, vs)):
        if isinstance(gr, str):
            res = big[gr]
        else:
            gr = gr.reshape(w.shape)
            res = (gr,) + _adamw(w, gr, m, v, name=f"adamw_{idx}")
        for lst, t in zip(outs, res):
            lst.append(t)
    return (loss, g["grad_x"][None], *outs[0], *outs[1], *outs[2], *outs[3])
```

```python
import functools

import jax
import jax.numpy as jnp
from jax import lax
from jax.experimental import pallas as pl
from jax.experimental.pallas import tpu as pltpu

F32 = jnp.float32
BF16 = jnp.bfloat16
_MXU_DTYPE = jnp.bfloat16

D_MODEL = 2048
N_META = 16
ROW_PAD = 48
HEAD_ROWS = ROW_PAD + N_META
CONV_K = 4
GDN_HEADS, GDN_DK, GDN_DV, GDN_CHUNK = 8, 128, 128, 64
GLA_HEADS, GLA_DK, GLA_DV, GLA_CHUNK = 4, 128, 256, 16
GLA_RANK = 16
GLA_GATE_NORMALIZER = 16.0
GDN_QK = GDN_HEADS * GDN_DK
GDN_V = GDN_HEADS * GDN_DV
GLA_QK = GLA_HEADS * GLA_DK
GLA_V = GLA_HEADS * GLA_DV
D_FF = 5632
D_IN = 7200
NORM_EPS = 1e-6
C_Z, C_GR, C_GQ, C_GK, C_GV, C_QKV, C_SM = 0, 1024, 2048, 2560, 3072, 4096, 7168
SM_W = 128
D_PROJ = 7680
R_Z, R_A, R_B, R_GQ, R_GK, R_GV, R_GR, R_LR = 3072, 4096, 4104, 4112, 4624, 5136, 6160, 7184

ADAM_LR, ADAM_B1, ADAM_B2, ADAM_EPS, ADAM_WD, ADAM_STEP = 0.001, 0.9, 0.999, 1e-08, 0.01, 10

N_DEV = 8
VMEM_LIMIT = 56 * 1024 * 1024

NN = (((1,), (0,)), ((), ()))
NT = (((1,), (1,)), ((), ()))
TN = (((0,), (0,)), ((), ()))


def _dot(a, b, dims=NN):
    return lax.dot_general(a.astype(_MXU_DTYPE), b.astype(_MXU_DTYPE), dims, preferred_element_type=F32)


def _running_sum(x, reverse=False):
    n = x.shape[0]
    row = lax.broadcasted_iota(jnp.int32, x.shape, 0)
    s = 1
    while s < n:
        if reverse:
            x = x + jnp.where(row < n - s, pltpu.roll(x, n - s, 0), 0.0)
        else:
            x = x + jnp.where(row >= s, pltpu.roll(x, s, 0), 0.0)
        s *= 2
    return x


def _dot3(a, b):
    ah = a.astype(BF16)
    al = (a - ah.astype(F32)).astype(BF16)
    bh = b.astype(BF16)
    bl = (b - bh.astype(F32)).astype(BF16)
    d = functools.partial(lax.dot_general, dimension_numbers=NN, preferred_element_type=F32)
    return d(ah, bh) + (d(ah, bl) + d(al, bh))


def _tile(n, target, mult=8):
    best = None
    for t in range(mult, min(n, target) + 1, mult):
        if n % t == 0:
            best = t
    return best if best is not None else n


def _params(*sem):
    return pltpu.CompilerParams(dimension_semantics=sem, vmem_limit_bytes=VMEM_LIMIT)


def _sigmoid(x):
    return 0.5 * jnp.tanh(0.5 * x) + 0.5


def _softplus(x):
    return jnp.maximum(x, 0.0) + jnp.log1p(jnp.exp(-jnp.abs(x)))


def _silu_and_grad(c):
    s = _sigmoid(c)
    return c * s, s * (1.0 + c * (1.0 - s))


_ANY = pl.BlockSpec(memory_space=pl.ANY)


def _matmul(a, b, *, mode, name, out_dtype=F32, add=None, after=None, tm=1376, tn=512):
    if mode == "tn":
        K, M = a.shape
        N = b.shape[1]
    else:
        M, K = a.shape
        N = b.shape[0] if mode == "nt" else b.shape[1]
    tm = _tile(M, tm, 128 if mode == "tn" else 16)
    tn = _tile(N, tn, 128)
    dims = {"nn": NN, "nt": NT, "tn": TN}[mode]
    n_after = 0 if after is None else 1

    def body(*refs):
        refs = refs[n_after:]
        r = _dot(refs[0][...], refs[1][...], dims)
        if add is not None:
            r = r + refs[2][...]
        refs[-1][...] = r.astype(out_dtype)

    a_spec = pl.BlockSpec((K, tm), lambda i, j: (0, i)) if mode == "tn" else pl.BlockSpec((tm, K), lambda i, j: (i, 0))
    b_spec = pl.BlockSpec((tn, K), lambda i, j: (j, 0)) if mode == "nt" else pl.BlockSpec((K, tn), lambda i, j: (0, j))
    o_spec = pl.BlockSpec((tm, tn), lambda i, j: (i, j))
    in_specs = [_ANY] * n_after + [a_spec, b_spec] + ([o_spec] if add is not None else [])
    args = ((after,) if n_after else ()) + (a, b) + ((add,) if add is not None else ())
    return pl.pallas_call(
        body, name=name, grid=(M // tm, N // tn), in_specs=in_specs, out_specs=o_spec,
        out_shape=jax.ShapeDtypeStruct((M, N), out_dtype), compiler_params=_params("parallel", "parallel"),
    )(*args)


def _matmul_pair(a1, b1, a2, b2, *, name, after=None, tm=688, tn=256):
    M, K = a1.shape
    N = b1.shape[1]
    tm, tn = _tile(M, tm, 16), _tile(N, tn, 128)
    n_after = 0 if after is None else 1

    def body(*refs):
        a1_ref, b1_ref, a2_ref, b2_ref, o_ref = refs[n_after:]
        o_ref[...] = _dot(a1_ref[...], b1_ref[...]) + _dot(a2_ref[...], b2_ref[...])

    a_spec = pl.BlockSpec((tm, K), lambda i, j: (i, 0))
    b_spec = pl.BlockSpec((K, tn), lambda i, j: (0, j))
    return pl.pallas_call(
        body, name=name, grid=(M // tm, N // tn), in_specs=[_ANY] * n_after + [a_spec, b_spec, a_spec, b_spec],
        out_specs=pl.BlockSpec((tm, tn), lambda i, j: (i, j)), out_shape=jax.ShapeDtypeStruct((M, N), F32),
        compiler_params=_params("parallel", "parallel"),
    )(*((after,) if n_after else ()), a1, b1, a2, b2)


def _rmsnorm_fwd(h, w, *, name):
    M, D = h.shape
    tm = _tile(M, 688, 16)

    def body(h_ref, w_ref, n_ref):
        x = h_ref[...]
        r = lax.rsqrt(jnp.mean(x * x, axis=-1, keepdims=True) + NORM_EPS)
        n_ref[...] = (x * r * w_ref[...]).astype(n_ref.dtype)

    return pl.pallas_call(
        body, name=name, grid=(M // tm,),
        in_specs=[pl.BlockSpec((tm, D), lambda i: (i, 0)), pl.BlockSpec((1, D), lambda i: (0, 0))],
        out_specs=pl.BlockSpec((tm, D), lambda i: (i, 0)),
        out_shape=jax.ShapeDtypeStruct((M, D), BF16),
        compiler_params=_params("parallel"),
    )(h, w)


SEQ_BLOCK = HEAD_ROWS


def _seq_blocks_per_tile(rows):
    n = rows // SEQ_BLOCK
    return max(m for m in (1, 2, 3, 4) if n % m == 0)


def _seq_specs(m, D):
    return [pl.BlockSpec((SEQ_BLOCK, D), functools.partial(lambda i, k: (jnp.maximum(m * i + k - 1, 0), 0), k=k))
            for k in range(m)]


def _embed_norm(head, x, w, *, name, after=None):
    S, D = x.shape
    m = _seq_blocks_per_tile(S + HEAD_ROWS)
    n_after = 0 if after is None else 1

    def body(*refs):
        refs = refs[n_after:]
        head_ref, x_refs, w_ref, h_ref, n_ref = refs[0], refs[1:1 + m], refs[1 + m], refs[2 + m], refs[3 + m]
        i = pl.program_id(0)
        for k in range(m):
            blk = x_refs[k][...]
            if k == 0:
                blk = jnp.where(i == 0, head_ref[...], blk)
            rows = slice(k * SEQ_BLOCK, (k + 1) * SEQ_BLOCK)
            h_ref[rows, :] = blk
            r = lax.rsqrt(jnp.mean(blk * blk, axis=-1, keepdims=True) + NORM_EPS)
            n_ref[rows, :] = (blk * r * w_ref[...]).astype(n_ref.dtype)

    tile = pl.BlockSpec((m * SEQ_BLOCK, D), lambda i: (i, 0))
    return pl.pallas_call(
        body, name=name, grid=((S + HEAD_ROWS) // (m * SEQ_BLOCK),),
        in_specs=[_ANY] * n_after + [pl.BlockSpec((SEQ_BLOCK, D), lambda i: (0, 0))] + _seq_specs(m, D)
        + [pl.BlockSpec((1, D), lambda i: (0, 0))],
        out_specs=[tile, tile],
        out_shape=[jax.ShapeDtypeStruct((S + HEAD_ROWS, D), F32), jax.ShapeDtypeStruct((S + HEAD_ROWS, D), BF16)],
        compiler_params=_params("parallel"),
    )(*((after,) if n_after else ()), head, *([x] * m), w)


def _embed_norm_bwd(h, w, dn, dres, *, name):
    M, D = h.shape
    S = M - HEAD_ROWS
    m = _seq_blocks_per_tile(S)
    g = S // (m * SEQ_BLOCK)

    def one(x, dn_, dres_, w_):
        r = lax.rsqrt(jnp.mean(x * x, axis=-1, keepdims=True) + NORM_EPS)
        xhat = x * r
        dxhat = dn_ * w_
        dh = dres_ + r * (dxhat - xhat * jnp.mean(dxhat * xhat, axis=-1, keepdims=True))
        return dh, jnp.sum((dn_ * xhat).reshape(SEQ_BLOCK // 8, 8, D), axis=0)

    def body(*refs):
        w_ref = refs[0]
        groups = [refs[1 + a * (m + 1):1 + (a + 1) * (m + 1)] for a in range(3)]
        gx_ref, dhead_ref, dw_ref, acc_ref = refs[1 + 3 * (m + 1):]
        i = pl.program_id(0)
        w_ = w_ref[...]
        part = jnp.zeros((8, D), F32)
        for k in range(m):
            dh, p = one(*(grp[1 + k][...] for grp in groups), w_)
            gx_ref[k * SEQ_BLOCK:(k + 1) * SEQ_BLOCK, :] = dh
            part = part + p

        @pl.when(i == 0)
        def _():
            dh, p = one(*(grp[0][...] for grp in groups), w_)
            dhead_ref[...] = dh
            acc_ref[...] = part + p

        @pl.when(i > 0)
        def _():
            acc_ref[...] += part

        @pl.when(i == g - 1)
        def _():
            dw_ref[...] = jnp.sum(acc_ref[...], axis=0, keepdims=True)

    first = pl.BlockSpec((SEQ_BLOCK, D), lambda i: (0, 0))
    blocks = [pl.BlockSpec((SEQ_BLOCK, D), functools.partial(lambda i, k: (m * i + k + 1, 0), k=k)) for k in range(m)]
    vec = pl.BlockSpec((1, D), lambda i: (0, 0))
    return pl.pallas_call(
        body, name=name, grid=(g,), in_specs=[vec] + ([first] + blocks) * 3,
        out_specs=[pl.BlockSpec((m * SEQ_BLOCK, D), lambda i: (i, 0)), first, vec],
        out_shape=[jax.ShapeDtypeStruct((S, D), F32), jax.ShapeDtypeStruct((SEQ_BLOCK, D), F32),
                   jax.ShapeDtypeStruct((1, D), F32)],
        scratch_shapes=[pltpu.VMEM((8, D), F32)],
        compiler_params=_params("arbitrary"),
    )(w, *([h] * (m + 1)), *([dn] * (m + 1)), *([dres] * (m + 1)))


def _rmsnorm_bwd(h, w, dn, dres, *, name):
    M, D = h.shape
    tm = _tile(M, 344, 16)
    g = M // tm

    def body(h_ref, w_ref, dn_ref, dres_ref, dh_ref, dhb_ref, dw_ref, acc_ref):
        i = pl.program_id(0)
        x = h_ref[...]
        r = lax.rsqrt(jnp.mean(x * x, axis=-1, keepdims=True) + NORM_EPS)
        xhat = x * r
        dn_ = dn_ref[...]
        dxhat = dn_ * w_ref[...]
        dh = dres_ref[...] + r * (dxhat - xhat * jnp.mean(dxhat * xhat, axis=-1, keepdims=True))
        dh_ref[...] = dh
        dhb_ref[...] = dh.astype(dhb_ref.dtype)
        part = jnp.sum((dn_ * xhat).reshape(tm // 8, 8, D), axis=0)

        @pl.when(i == 0)
        def _():
            acc_ref[...] = part

        @pl.when(i > 0)
        def _():
            acc_ref[...] += part

        @pl.when(i == g - 1)
        def _():
            dw_ref[...] = jnp.sum(acc_ref[...], axis=0, keepdims=True)

    row = pl.BlockSpec((tm, D), lambda i: (i, 0))
    vec = pl.BlockSpec((1, D), lambda i: (0, 0))
    return pl.pallas_call(
        body, name=name, grid=(g,), in_specs=[row, vec, row, row],
        out_specs=[row, row, vec],
        out_shape=[jax.ShapeDtypeStruct((M, D), F32), jax.ShapeDtypeStruct((M, D), BF16),
                   jax.ShapeDtypeStruct((1, D), F32)],
        scratch_shapes=[pltpu.VMEM((8, D), F32)],
        compiler_params=_params("arbitrary"),
    )(h, w, dn, dres)


def _loss_head(h, w, target, *, name):
    M, D = h.shape
    m = _seq_blocks_per_tile(M)
    tm = m * SEQ_BLOCK
    g = M // tm

    def body(h_ref, w_ref, *rest):
        t_refs = rest[:m]
        dh_ref, dhb_ref, dw_ref, loss_ref, acc_ref, lacc_ref = rest[m:]
        i = pl.program_id(0)
        x = h_ref[...]
        row = i * tm + lax.broadcasted_iota(jnp.int32, (tm, 1), 0)
        live = row >= HEAD_ROWS
        r = lax.rsqrt(jnp.mean(x * x, axis=-1, keepdims=True) + NORM_EPS)
        xhat = x * r
        t = jnp.concatenate([t_ref[...] for t_ref in t_refs], axis=0)
        err = jnp.where(live, xhat * w_ref[...] - t, 0.0)
        dy = err * (1.0 / D)
        dxhat = dy * w_ref[...]
        dh = r * (dxhat - xhat * jnp.mean(dxhat * xhat, axis=-1, keepdims=True))
        dh_ref[...] = dh
        dhb_ref[...] = dh.astype(dhb_ref.dtype)
        part = jnp.sum((dy * xhat).reshape(tm // 8, 8, D), axis=0)
        lpart = jnp.sum((err * err).reshape(tm // 8, 8, D), axis=0)

        @pl.when(i == 0)
        def _():
            acc_ref[...] = part
            lacc_ref[...] = lpart

        @pl.when(i > 0)
        def _():
            acc_ref[...] += part
            lacc_ref[...] += lpart

        @pl.when(i == g - 1)
        def _():
            dw_ref[...] = jnp.sum(acc_ref[...], axis=0, keepdims=True)
            tot = jnp.sum(jnp.sum(lacc_ref[...], axis=0, keepdims=True), axis=1, keepdims=True)
            loss_ref[...] = jnp.broadcast_to(tot * (0.5 / D), (1, 128))

    row = pl.BlockSpec((tm, D), lambda i: (i, 0))
    vec = pl.BlockSpec((1, D), lambda i: (0, 0))
    return pl.pallas_call(
        body, name=name, grid=(g,), in_specs=[row, vec] + _seq_specs(m, D),
        out_specs=[row, row, vec, pl.BlockSpec((1, 128), lambda i: (0, 0))],
        out_shape=[jax.ShapeDtypeStruct((M, D), F32), jax.ShapeDtypeStruct((M, D), BF16),
                   jax.ShapeDtypeStruct((1, D), F32), jax.ShapeDtypeStruct((1, 128), F32)],
        scratch_shapes=[pltpu.VMEM((8, D), F32), pltpu.VMEM((8, D), F32)],
        compiler_params=_params("arbitrary"),
    )(h, w, *([target] * m))


def _gate_terms(sm, w2p, b2, alog_p, dt_p, row0):
    tm = sm.shape[0]
    lane = lax.broadcasted_iota(jnp.int32, (tm, SM_W), 1)
    live = (row0 + lax.broadcasted_iota(jnp.int32, (tm, 1), 0)) >= ROW_PAD
    pre = sm + dt_p
    neg_a = -jnp.exp(alog_p)
    g = neg_a * _softplus(pre)
    beta = _sigmoid(sm)
    z = _dot(sm, w2p) + b2
    return lane, live, pre, neg_a, g, beta, z


def _gates_fwd(sm, w2p, b2, alog_p, dt_p, *, name):
    M = sm.shape[0]
    tm = _tile(M, 688, 8)

    def body(sm_ref, w2_ref, b2_ref, al_ref, dt_ref, gb_ref, la_ref):
        row0 = pl.program_id(0) * tm
        lane, live, _, _, g, beta, z = _gate_terms(sm_ref[...].astype(F32), w2_ref[...], b2_ref[...], al_ref[...], dt_ref[...], row0)
        gb = jnp.where(lane < GDN_HEADS, g, jnp.where(lane < 2 * GDN_HEADS, beta, 0.0))
        gb_ref[...] = jnp.where(live, gb, 0.0)
        la = (jnp.minimum(z, 0.0) - jnp.log1p(jnp.exp(-jnp.abs(z)))) * (1.0 / GLA_GATE_NORMALIZER)
        la_ref[...] = jnp.where(live, la, 0.0)

    full = lambda s: pl.BlockSpec(s, lambda i: (0, 0))
    return pl.pallas_call(
        body, name=name, grid=(M // tm,),
        in_specs=[pl.BlockSpec((tm, SM_W), lambda i: (i, 0)), full((SM_W, GLA_QK)), full((1, GLA_QK)),
                  full((1, SM_W)), full((1, SM_W))],
        out_specs=[pl.BlockSpec((tm, SM_W), lambda i: (i, 0)), pl.BlockSpec((tm, GLA_QK), lambda i: (i, 0))],
        out_shape=[jax.ShapeDtypeStruct((M, SM_W), F32), jax.ShapeDtypeStruct((M, GLA_QK), F32)],
        compiler_params=_params("parallel"),
    )(sm, w2p, b2, alog_p, dt_p)


def _gates_bwd(sm, w2p, b2, alog_p, dt_p, dgb_heads, dla, d_proj, *, name):
    M = sm.shape[0]
    tm = _tile(M, 688, 8)
    g_ = M // tm

    tail_w = D_PROJ - C_SM

    def body(sm_ref, w2_ref, b2_ref, al_ref, dt_ref, dgb_ref, dla_ref, _,
             dsm_ref, dw2_ref, db2_ref, dal_ref, ddt_ref):
        i = pl.program_id(0)
        sm = sm_ref[...].astype(F32)
        lane, live, pre, neg_a, g, beta, z = _gate_terms(sm, w2_ref[...], b2_ref[...], al_ref[...], dt_ref[...], i * tm)
        dz = jnp.where(live, dla_ref[...] * (_sigmoid(-z) * (1.0 / GLA_GATE_NORMALIZER)), 0.0)
        dsm_lr = _dot(dz, w2_ref[...], NT)
        dgb = dgb_ref[0]
        for hh in range(1, GDN_HEADS):
            dgb = dgb + dgb_ref[hh]
        dgb = jnp.where(live, dgb, 0.0)
        da = dgb * neg_a * _sigmoid(pre)
        db = dgb * beta * (1.0 - beta)
        dsm = jnp.where(lane < GDN_HEADS, da, jnp.where(lane < 2 * GDN_HEADS, db, dsm_lr))
        dsm_ref[:, 0:SM_W] = dsm.astype(dsm_ref.dtype)
        if tail_w > SM_W:
            dsm_ref[:, SM_W:tail_w] = jnp.zeros((tm, tail_w - SM_W), dsm_ref.dtype)
        is_a = lane < GDN_HEADS
        dal = jnp.sum(jnp.where(is_a, dgb * g, 0.0), axis=0, keepdims=True)
        ddt = jnp.sum(jnp.where(is_a, da, 0.0), axis=0, keepdims=True)
        dw2 = _dot(sm, dz, TN)
        db2 = jnp.sum(dz, axis=0, keepdims=True)

        @pl.when(i == 0)
        def _():
            dw2_ref[...] = dw2
            db2_ref[...] = db2
            dal_ref[...] = dal
            ddt_ref[...] = ddt

        @pl.when(i > 0)
        def _():
            dw2_ref[...] += dw2
            db2_ref[...] += db2
            dal_ref[...] += dal
            ddt_ref[...] += ddt

    full = lambda s: pl.BlockSpec(s, lambda i: (0, 0))
    return pl.pallas_call(
        body, name=name, grid=(g_,),
        in_specs=[pl.BlockSpec((tm, SM_W), lambda i: (i, 0)), full((SM_W, GLA_QK)), full((1, GLA_QK)),
                  full((1, SM_W)), full((1, SM_W)),
                  pl.BlockSpec((GDN_HEADS, tm, SM_W), lambda i: (0, i, 0)),
                  pl.BlockSpec((tm, GLA_QK), lambda i: (i, 0)), _ANY],
        out_specs=[pl.BlockSpec((tm, tail_w), lambda i: (i, C_SM // tail_w)), full((SM_W, GLA_QK)), full((1, GLA_QK)),
                   full((1, SM_W)), full((1, SM_W))],
        out_shape=[jax.ShapeDtypeStruct(d_proj.shape, d_proj.dtype), jax.ShapeDtypeStruct((SM_W, GLA_QK), F32),
                   jax.ShapeDtypeStruct((1, GLA_QK), F32), jax.ShapeDtypeStruct((1, SM_W), F32),
                   jax.ShapeDtypeStruct((1, SM_W), F32)],
        input_output_aliases={7: 0},
        compiler_params=_params("arbitrary"),
    )(sm, w2p, b2, alog_p, dt_p, dgb_heads, dla, d_proj)


QKV_W = GDN_QK
N_QKV_GROUPS = 3
QKV_B0 = C_QKV // QKV_W
HALO = 16


def _conv_terms(x_ref, halo_ref, cw_ref, xs_ref, i, tm):
    xs_ref[HALO:HALO + tm, :] = x_ref[...].astype(F32)
    xs_ref[0:HALO, :] = jnp.where(i > 0, halo_ref[...].astype(F32), 0.0)
    cw = cw_ref[...]
    xs = xs_ref[...]
    taps = [(pltpu.roll(xs, CONV_K - 1 - t, 0) if t < CONV_K - 1 else xs)[HALO:HALO + tm, :] for t in range(CONV_K)]
    c = taps[0] * cw[0:1, :]
    for t in range(1, CONV_K):
        c = c + taps[t] * cw[t:t + 1, :]
    return c, taps


def _prep_fwd(proj, conv_w8, *, name):
    M = proj.shape[0]
    tm = _tile(M, 688, 16)

    def body(x_ref, halo_ref, cw_ref, o_ref, xs_ref):
        j, i = pl.program_id(0), pl.program_id(1)
        c, _ = _conv_terms(x_ref, halo_ref, cw_ref, xs_ref, i, tm)
        s, _ = _silu_and_grad(c)
        scale = jnp.where(j == 0, GDN_DK ** -0.5, 1.0)
        for hh in range(GDN_HEADS):
            cols = slice(hh * 128, (hh + 1) * 128)
            sh = s[:, cols]
            r = lax.rsqrt(jnp.sum(sh * sh, axis=-1, keepdims=True) + NORM_EPS)
            o_ref[:, cols] = jnp.where(j < 2, sh * (r * scale), sh)

    hb = tm // HALO
    return pl.pallas_call(
        body, name=name, grid=(N_QKV_GROUPS, M // tm),
        in_specs=[pl.BlockSpec((tm, QKV_W), lambda j, i: (i, QKV_B0 + j)),
                  pl.BlockSpec((HALO, QKV_W), lambda j, i: (jnp.maximum(i * hb - 1, 0), QKV_B0 + j)),
                  pl.BlockSpec((8, QKV_W), lambda j, i: (0, j))],
        out_specs=pl.BlockSpec((tm, QKV_W), lambda j, i: (i, j)),
        out_shape=jax.ShapeDtypeStruct((M, N_QKV_GROUPS * QKV_W), F32),
        scratch_shapes=[pltpu.VMEM((tm + HALO, QKV_W), F32)],
        compiler_params=_params("parallel", "arbitrary"),
    )(proj, proj, conv_w8)


def _prep_bwd(proj, conv_w8, dact, d_proj, *, name):
    M = proj.shape[0]
    tm = _tile(M, 688, 16)
    g_ = M // tm
    ext = tm + HALO

    def body(x_ref, prev_ref, next_ref, cw_ref, da_ref, dan_ref, _, o_ref, dcw_ref, xs_ref, das_ref, dcs_ref):
        j, i = pl.program_id(0), pl.program_id(1)
        not_last = i < g_ - 1
        xs_ref[0:HALO, :] = jnp.where(i > 0, prev_ref[...].astype(F32), 0.0)
        xs_ref[HALO:HALO + tm, :] = x_ref[...].astype(F32)
        xs_ref[HALO + tm:HALO + ext, :] = jnp.where(not_last, next_ref[...].astype(F32), 0.0)
        das_ref[0:tm, :] = da_ref[...]
        das_ref[tm:ext, :] = jnp.where(not_last, dan_ref[...], 0.0)
        cw = cw_ref[...]
        xs = xs_ref[...]
        taps = [(pltpu.roll(xs, CONV_K - 1 - t, 0) if t < CONV_K - 1 else xs)[HALO:HALO + ext, :] for t in range(CONV_K)]
        c = taps[0] * cw[0:1, :]
        for t in range(1, CONV_K):
            c = c + taps[t] * cw[t:t + 1, :]
        s, ds_dc = _silu_and_grad(c)
        scale = jnp.where(j == 0, GDN_DK ** -0.5, 1.0)
        for hh in range(GDN_HEADS):
            cols = slice(hh * 128, (hh + 1) * 128)
            sh = s[:, cols]
            r = lax.rsqrt(jnp.sum(sh * sh, axis=-1, keepdims=True) + NORM_EPS)
            da = das_ref[:, cols]
            y = sh * r
            dy = da * scale
            ds_norm = r * (dy - y * jnp.sum(dy * y, axis=-1, keepdims=True))
            dcs_ref[:, cols] = jnp.where(j < 2, ds_norm, da) * ds_dc[:, cols]
        dc = dcs_ref[...]
        acc = dc[0:tm, :] * cw[CONV_K - 1:CONV_K, :]
        for t in range(CONV_K - 1):
            acc = acc + pltpu.roll(dc, ext - (CONV_K - 1 - t), 0)[0:tm, :] * cw[t:t + 1, :]
        o_ref[...] = acc.astype(o_ref.dtype)
        r8 = lax.broadcasted_iota(jnp.int32, (8, QKV_W), 0)
        part = jnp.zeros((8, QKV_W), F32)
        for t in range(CONV_K):
            part = jnp.where(r8 == t, jnp.sum(dc[0:tm, :] * taps[t][0:tm, :], axis=0, keepdims=True), part)

        @pl.when(i == 0)
        def _():
            dcw_ref[...] = part

        @pl.when(i > 0)
        def _():
            dcw_ref[...] += part

    hb = tm // HALO
    last = M // HALO - 1
    prev_of = lambda i: jnp.maximum(i * hb - 1, 0)
    next_of = lambda i: jnp.minimum((i + 1) * hb, last)
    return pl.pallas_call(
        body, name=name, grid=(N_QKV_GROUPS, g_),
        in_specs=[pl.BlockSpec((tm, QKV_W), lambda j, i: (i, QKV_B0 + j)),
                  pl.BlockSpec((HALO, QKV_W), lambda j, i: (prev_of(i), QKV_B0 + j)),
                  pl.BlockSpec((HALO, QKV_W), lambda j, i: (next_of(i), QKV_B0 + j)),
                  pl.BlockSpec((8, QKV_W), lambda j, i: (0, j)),
                  pl.BlockSpec((tm, QKV_W), lambda j, i: (i, j)),
                  pl.BlockSpec((HALO, QKV_W), lambda j, i: (next_of(i), j)), _ANY],
        out_specs=[pl.BlockSpec((tm, QKV_W), lambda j, i: (i, QKV_B0 + j)), pl.BlockSpec((8, QKV_W), lambda j, i: (0, j))],
        out_shape=[jax.ShapeDtypeStruct(d_proj.shape, d_proj.dtype),
                   jax.ShapeDtypeStruct((8, N_QKV_GROUPS * QKV_W), F32)],
        input_output_aliases={6: 0},
        scratch_shapes=[pltpu.VMEM((HALO + ext, QKV_W), F32), pltpu.VMEM((ext, QKV_W), F32), pltpu.VMEM((ext, QKV_W), F32)],
        compiler_params=_params("parallel", "arbitrary"),
    )(proj, proj, proj, conv_w8, dact, dact, d_proj)


def _round_robin(gens):
    gens = list(gens)
    while gens:
        alive = []
        for gen in gens:
            try:
                next(gen)
                alive.append(gen)
            except StopIteration:
                pass
        gens = alive


def _unit_lower_inverse(a_low, eye):
    n = a_low.shape[0]
    ri = lax.broadcasted_iota(jnp.int32, (n, n), 0)
    ci = lax.broadcasted_iota(jnp.int32, (n, n), 1)
    same = lambda shift: (ri >> shift) == (ci >> shift)
    b = jnp.where(same(3), -a_low, 0.0)
    x = eye + b
    p2 = _dot3(b, b)
    yield
    x = x + _dot3(x, p2)
    p4 = _dot3(p2, p2)
    yield
    x = x + _dot3(x, p4)
    yield
    for shift in (3, 4, 5):
        between = jnp.where(same(shift + 1) & ~same(shift), a_low, 0.0)
        t = _dot3(between, x)
        yield
        x = x - _dot3(x, t)
        yield
    return x


class _GdnChunk:
    def build(self, q, k, v, gb, h, sum_on_mxu):
        C = GDN_CHUNK
        lane = lax.broadcasted_iota(jnp.int32, (C, SM_W), 1)
        g = jnp.sum(jnp.where(lane == h, gb, 0.0), axis=1, keepdims=True)
        self.beta = jnp.sum(jnp.where(lane == h + GDN_HEADS, gb, 0.0), axis=1, keepdims=True)
        ri = lax.broadcasted_iota(jnp.int32, (C, C), 0)
        ci = lax.broadcasted_iota(jnp.int32, (C, C), 1)
        self.causal = ri >= ci
        self.strict = ri > ci
        self.eye = (ri == ci).astype(F32)
        if sum_on_mxu:
            gcb = lax.dot_general(self.causal.astype(F32), jnp.broadcast_to(g, (C, SM_W)), NN,
                                  precision=lax.Precision.HIGHEST, preferred_element_type=F32)
        else:
            gcb = _running_sum(jnp.broadcast_to(g, (C, SM_W)))
        yield
        self.gcol = gcb[:, 0:1]
        grow = gcb.T[0:1, 0:C]
        self.decay = jnp.exp(jnp.where(self.causal, self.gcol - grow, -1e30))
        self.egc = jnp.exp(self.gcol)
        glast = gcb[C - 1:C, 0:1]
        self.elast = jnp.exp(glast - self.gcol)
        self.gl = jnp.exp(glast)
        self.q, self.k, self.v = q, k, v
        self.kb = k * self.beta
        m = _dot(self.kb, k, NT)
        n_ = _dot(q, k, NT)
        yield
        self.a_low = jnp.where(self.strict, m * self.decay, 0.0)
        self.p = n_ * self.decay
        self.qd = q * self.egc
        self.kd = k * self.elast
        self.bu = v * self.beta
        self.bw = self.kb * self.egc


GDN_HB = 8
GDN_HG = GDN_HEADS // GDN_HB


def _gdn_specs(n_of):
    C, W = GDN_CHUNK, 128 * GDN_HB
    q_spec = pl.BlockSpec((C, W), lambda g, n: (n_of(n), g))
    k_spec = pl.BlockSpec((C, W), lambda g, n: (n_of(n), g + GDN_HG))
    v_spec = pl.BlockSpec((C, W), lambda g, n: (n_of(n), g + 2 * GDN_HG))
    gb_spec = pl.BlockSpec((C, SM_W), lambda g, n: (n_of(n), 0))
    o_spec = pl.BlockSpec((C, W), lambda g, n: (n_of(n), g))
    s_spec = pl.BlockSpec((GDN_HB, None, GDN_DK, GDN_DV), lambda g, n: (g, n_of(n), 0, 0))
    t_spec = pl.BlockSpec((GDN_HB, None, C, C), lambda g, n: (g, n_of(n), 0, 0))
    return q_spec, k_spec, v_spec, gb_spec, o_spec, s_spec, t_spec


def _gdn_fwd(act, gb, *, name):
    M = act.shape[0]
    N = M // GDN_CHUNK

    def body(q_ref, k_ref, v_ref, gb_ref, o_ref, s_ref, t_ref, state):
        g, n = pl.program_id(0), pl.program_id(1)

        @pl.when(n == 0)
        def _():
            state[...] = jnp.zeros_like(state)

        gb_ = gb_ref[...]

        def head(hh):
            cols = slice(hh * 128, (hh + 1) * 128)
            c = _GdnChunk()
            yield from c.build(q_ref[:, cols], k_ref[:, cols], v_ref[:, cols], gb_, g * GDN_HB + hh, sum_on_mxu=True)
            tinv = yield from _unit_lower_inverse(c.a_low, c.eye)
            s = state[hh]
            s_ref[hh] = s
            t_ref[hh] = tinv
            u = _dot(tinv, c.bu)
            w = _dot(tinv, c.bw)
            yield
            vn = u - _dot(w, s)
            o1 = _dot(c.qd, s)
            yield
            o_ref[:, cols] = (o1 + _dot(c.p, vn)).astype(o_ref.dtype)
            state[hh] = c.gl * s + _dot(c.kd, vn, TN)

        _round_robin(head(hh) for hh in range(GDN_HB))

    q_spec, k_spec, v_spec, gb_spec, o_spec, s_spec, t_spec = _gdn_specs(lambda n: n)
    return pl.pallas_call(
        body, name=name, grid=(GDN_HG, N),
        in_specs=[q_spec, k_spec, v_spec, gb_spec], out_specs=[o_spec, s_spec, t_spec],
        out_shape=[jax.ShapeDtypeStruct((M, GDN_V), BF16),
                   jax.ShapeDtypeStruct((GDN_HEADS, N, GDN_DK, GDN_DV), F32),
                   jax.ShapeDtypeStruct((GDN_HEADS, N, GDN_CHUNK, GDN_CHUNK), F32)],
        scratch_shapes=[pltpu.VMEM((GDN_HB, GDN_DK, GDN_DV), F32)],
        compiler_params=_params("parallel", "arbitrary"),
    )(act, act, act, gb)


def _gdn_bwd(act, gb, do, s_all, t_all, *, name):
    M = act.shape[0]
    N = M // GDN_CHUNK
    C = GDN_CHUNK
    assert GDN_HG == 1

    def body(q_ref, k_ref, v_ref, gb_ref, do_ref, s_ref, t_ref, dact_ref, dgb_ref, dstate):
        g, n = pl.program_id(0), pl.program_id(1)

        @pl.when(n == 0)
        def _():
            dstate[...] = jnp.zeros_like(dstate)

        gb_ = gb_ref[...]
        last = lax.broadcasted_iota(jnp.int32, (C, 1), 0) == C - 1
        lane = lax.broadcasted_iota(jnp.int32, (C, SM_W), 1)
        def head(hh):
            cols = slice(hh * 128, (hh + 1) * 128)
            h = g * GDN_HB + hh
            c = _GdnChunk()
            yield from c.build(q_ref[:, cols], k_ref[:, cols], v_ref[:, cols], gb_, h, sum_on_mxu=False)
            tinv = t_ref[hh]
            tinv_t = tinv.T
            s = s_ref[hh]
            do_ = do_ref[:, cols]
            ds1 = dstate[hh]
            u = _dot(tinv, c.bu)
            w = _dot(tinv, c.bw)
            dqd = _dot(do_, s, NT)
            yield
            dvn0 = _dot(c.p, do_, TN) + _dot(c.kd, ds1)
            dst0 = _dot(c.qd, do_, TN) + c.gl * ds1
            yield
            vn = u - _dot(w, s)
            dvn = dvn0
            yield
            dp = jnp.where(c.causal, _dot(do_, vn, NT), 0.0)
            dstate[hh] = dst0 - _dot(w, dvn, TN)
            dkd = _dot(vn, ds1, NT)
            dw = -_dot(dvn, s, NT)
            dbu = _dot(tinv_t, dvn)
            dgl = jnp.sum(jnp.sum(s * ds1, axis=1, keepdims=True), axis=0, keepdims=True)
            yield
            dbw = _dot(tinv_t, dw)
            t1 = _dot(dbu, u, NT)
            yield
            da = jnp.where(c.strict, -(t1 + _dot(dbw, w, NT)), 0.0)
            dn_ = dp * c.decay
            dq0 = _dot(dn_, c.k)
            dk0 = _dot(dn_, c.q, TN)
            yield
            dm = da * c.decay
            e = da * c.a_low + dp * c.p
            dkb = _dot(dm, c.k) + dbw * c.egc
            dact_ref[:, GDN_QK + hh * 128:GDN_QK + (hh + 1) * 128] = (
                _dot(dm, c.kb, TN) + dk0 + dkb * c.beta + dkd * c.elast)
            dact_ref[:, cols] = dq0 + dqd * c.egc
            dact_ref[:, 2 * GDN_QK + hh * 128:2 * GDN_QK + (hh + 1) * 128] = dbu * c.beta
            dbeta = jnp.sum(dbu * c.v, axis=1, keepdims=True) + jnp.sum(dkb * c.k, axis=1, keepdims=True)
            t_kd = jnp.sum(dkd * c.kd, axis=1, keepdims=True)
            dgc = (jnp.sum(e, axis=1, keepdims=True) - jnp.sum(e.T, axis=1, keepdims=True)
                   + jnp.sum(dbw * c.bw, axis=1, keepdims=True) + jnp.sum(dqd * c.qd, axis=1, keepdims=True) - t_kd)
            dgc = dgc + jnp.where(last, jnp.sum(t_kd, axis=0, keepdims=True) + dgl * c.gl, 0.0)
            yield
            dg = _running_sum(jnp.broadcast_to(dgc, (C, SM_W)), reverse=True)
            dgb_ref[hh] = jnp.where(lane == h, dg, jnp.where(lane == h + GDN_HEADS, dbeta, 0.0))

        _round_robin(head(hh) for hh in range(GDN_HB))

    rev = lambda n: N - 1 - n
    q_spec, k_spec, v_spec, gb_spec, o_spec, s_spec, t_spec = _gdn_specs(rev)
    dgb_spec = pl.BlockSpec((GDN_HB, C, SM_W), lambda g, n: (g, rev(n), 0))
    return pl.pallas_call(
        body, name=name, grid=(GDN_HG, N),
        in_specs=[q_spec, k_spec, v_spec, gb_spec, o_spec, s_spec, t_spec],
        out_specs=[pl.BlockSpec((C, 2 * GDN_QK + GDN_V), lambda g, n: (rev(n), 0)), dgb_spec],
        out_shape=[jax.ShapeDtypeStruct((M, 2 * GDN_QK + GDN_V), F32),
                   jax.ShapeDtypeStruct((GDN_HEADS, M, SM_W), F32)],
        scratch_shapes=[pltpu.VMEM((GDN_HB, GDN_DK, GDN_DV), F32)],
        compiler_params=_params("parallel", "arbitrary"),
    )(act, act, act, gb, do, s_all, t_all)


GLA_STEP_ROWS = 64
GLA_SUB = GLA_STEP_ROWS // GLA_CHUNK


def _gla_cumsum(la):
    return _running_sum(la)


GLA_HALF = GLA_CHUNK // 2


def _gla_cross_factors(b):
    top = lax.broadcasted_iota(jnp.int32, b.shape, 0) < GLA_HALF
    bm = b[GLA_HALF - 1:GLA_HALF, :]
    late = jnp.where(top, 0.0, jnp.exp(jnp.minimum(b - bm, 0.0)))
    early = jnp.where(top, jnp.exp(jnp.minimum(bm - b, 0.0)), 0.0)
    return late, early


def _gla_half_decay(bh, ii):
    rj = lax.broadcasted_iota(jnp.int32, bh.shape, 0)
    return jnp.where(rj <= ii, jnp.exp(jnp.minimum(bh[ii:ii + 1, :] - bh, 0.0)), 0.0)


def _gla_scores_t(q, k, b):
    C, H = GLA_CHUNK, GLA_HALF
    lane = lax.broadcasted_iota(jnp.int32, (H, C), 1)
    halves = []
    for h0 in (0, H):
        qh, kh, bh = q[h0:h0 + H], k[h0:h0 + H], b[h0:h0 + H]
        sth = jnp.zeros((H, C), F32)
        for ii in range(H):
            si = jnp.sum(qh[ii:ii + 1, :] * kh * _gla_half_decay(bh, ii), axis=1, keepdims=True)
            sth = jnp.where(lane == h0 + ii, si, sth)
            if ii % 4 == 3:
                yield
        halves.append(sth)
    late, early = _gla_cross_factors(b)
    between = _dot(k * early, q * late, NT)
    yield
    return jnp.concatenate(halves, axis=0) + between


def _gla_specs(n_of):
    R = GLA_STEP_ROWS
    q_spec = pl.BlockSpec((R, GLA_QK), lambda n: (n_of(n), C_GQ // GLA_QK))
    k_spec = pl.BlockSpec((R, GLA_QK), lambda n: (n_of(n), C_GK // GLA_QK))
    v_spec = pl.BlockSpec((R, GLA_V), lambda n: (n_of(n), C_GV // GLA_V))
    la_spec = pl.BlockSpec((R, GLA_QK), lambda n: (n_of(n), 0))
    o_spec = pl.BlockSpec((R, GLA_V), lambda n: (n_of(n), 0))
    s_spec = pl.BlockSpec((GLA_HEADS, None, GLA_SUB, GLA_DV, GLA_DK), lambda n: (0, n_of(n), 0, 0, 0))
    return q_spec, k_spec, v_spec, la_spec, o_spec, s_spec


def _gla_fwd(proj, la, *, name):
    M = proj.shape[0]
    N = M // GLA_STEP_ROWS
    C = GLA_CHUNK

    def body(q_ref, k_ref, v_ref, la_ref, o_ref, s_ref, state):
        n = pl.program_id(0)

        @pl.when(n == 0)
        def _():
            state[...] = jnp.zeros_like(state)

        local = {}

        def within(hh, c):
            kc = slice(hh * GLA_DK, (hh + 1) * GLA_DK)
            vc = slice(hh * GLA_DV, (hh + 1) * GLA_DV)
            rows = slice(c * C, (c + 1) * C)
            q = q_ref[rows, kc].astype(F32) * (GLA_DK ** -0.5)
            k = k_ref[rows, kc].astype(F32)
            v = v_ref[rows, vc].astype(F32)
            b = _gla_cumsum(la_ref[rows, kc])
            yield
            blast = b[C - 1:C, :]
            sc_t = yield from _gla_scores_t(q, k, b)
            kv = _dot(v, k * jnp.exp(blast - b), TN)
            o2 = _dot(sc_t, v, TN)
            yield
            local[hh, c] = (q * jnp.exp(b), jnp.exp(blast), kv, o2)

        def across(hh):
            vc = slice(hh * GLA_DV, (hh + 1) * GLA_DV)
            st = state[hh]
            for c in range(GLA_SUB):
                qe, eblast, kv, o2 = local[hh, c]
                s_ref[hh, c] = st
                o1 = _dot(qe, st, NT)
                yield
                o_ref[c * C:(c + 1) * C, vc] = (o1 + o2).astype(o_ref.dtype)
                st = st * eblast + kv
            state[hh] = st

        _round_robin(within(hh, c) for c in range(GLA_SUB) for hh in range(GLA_HEADS))
        _round_robin(across(hh) for hh in range(GLA_HEADS))

    q_spec, k_spec, v_spec, la_spec, o_spec, s_spec = _gla_specs(lambda n: n)
    return pl.pallas_call(
        body, name=name, grid=(N,),
        in_specs=[q_spec, k_spec, v_spec, la_spec], out_specs=[o_spec, s_spec],
        out_shape=[jax.ShapeDtypeStruct((M, GLA_V), BF16),
                   jax.ShapeDtypeStruct((GLA_HEADS, N, GLA_SUB, GLA_DV, GLA_DK), F32)],
        scratch_shapes=[pltpu.VMEM((GLA_HEADS, GLA_DV, GLA_DK), F32)],
        compiler_params=_params("arbitrary"),
    )(proj, proj, proj, la)


def _gla_bwd(proj, la, do, s_all, d_proj, *, name):
    M = proj.shape[0]
    N = M // GLA_STEP_ROWS
    C = GLA_CHUNK
    qkv_w = 2 * GLA_QK + GLA_V
    assert C_GK == C_GQ + GLA_QK and C_GV == C_GK + GLA_QK and C_GQ % qkv_w == 0

    def body(q_ref, k_ref, v_ref, la_ref, do_ref, s_ref, _, dp_ref, dla_ref, dstate):
        n = pl.program_id(0)

        @pl.when(n == 0)
        def _():
            dstate[...] = jnp.zeros_like(dstate)

        H = GLA_HALF
        lane = lax.broadcasted_iota(jnp.int32, (C, C), 1)
        row = lax.broadcasted_iota(jnp.int32, (C, C), 0)
        ri = lax.broadcasted_iota(jnp.int32, (C, GLA_DK), 0)
        lane_h = lax.broadcasted_iota(jnp.int32, (H, C), 1)
        ri_h = lax.broadcasted_iota(jnp.int32, (H, GLA_DK), 0)
        cross = (row < H) & (lane >= H)
        def head(hh):
            kc = slice(hh * GLA_DK, (hh + 1) * GLA_DK)
            vc = slice(hh * GLA_DV, (hh + 1) * GLA_DV)
            ds1 = dstate[hh]
            for c in reversed(range(GLA_SUB)):
                rows = slice(c * C, (c + 1) * C)
                q = q_ref[rows, kc].astype(F32) * (GLA_DK ** -0.5)
                k = k_ref[rows, kc].astype(F32)
                v = v_ref[rows, vc].astype(F32)
                b = _gla_cumsum(la_ref[rows, kc])
                do_ = do_ref[rows, vc]
                st = s_ref[hh, c]
                dsc_t = _dot(v, do_, NT)
                dqe = _dot(do_, st)
                dke = _dot(v, ds1)
                yield
                blast = b[C - 1:C, :]
                eb = jnp.exp(b)
                elast = jnp.exp(blast - b)
                eblast = jnp.exp(blast)
                qe = q * eb
                ke = k * elast
                dv2 = _dot(ke, ds1, NT)
                ds_new = _dot(do_, qe, TN)
                deblast = jnp.sum(st * ds1, axis=0, keepdims=True)
                sc_halves, dq_halves, dk_halves = [], [], []
                for h0 in (0, H):
                    qh, kh, bh, dsch = q[h0:h0 + H], k[h0:h0 + H], b[h0:h0 + H], dsc_t[h0:h0 + H]
                    sch = jnp.zeros((H, C), F32)
                    dqh = jnp.zeros((H, GLA_DK), F32)
                    dkh = jnp.zeros((H, GLA_DK), F32)
                    for ii in range(H):
                        f = _gla_half_decay(bh, ii)
                        kf = kh * f
                        si = jnp.sum(qh[ii:ii + 1, :] * kf, axis=1, keepdims=True)
                        sch = jnp.where(lane_h == h0 + ii, si, sch)
                        dsi = jnp.sum(jnp.where(lane_h == h0 + ii, dsch, 0.0), axis=1, keepdims=True)
                        dqh = jnp.where(ri_h == ii, jnp.sum(dsi * kf, axis=0, keepdims=True), dqh)
                        dkh = dkh + (dsi * f) * qh[ii:ii + 1, :]
                        if ii % 4 == 3:
                            yield
                    sc_halves.append(sch)
                    dq_halves.append(dqh)
                    dk_halves.append(dkh)
                late, early = _gla_cross_factors(b)
                q_late, k_early = q * late, k * early
                dsc_x = jnp.where(cross, dsc_t, 0.0)
                sc_t = jnp.concatenate(sc_halves, axis=0) + _dot(k_early, q_late, NT)
                dq_sc = jnp.concatenate(dq_halves, axis=0) + _dot(dsc_x, k_early, TN) * late
                dk_sc = jnp.concatenate(dk_halves, axis=0) + _dot(dsc_x, q_late) * early
                yield
                dv1 = _dot(sc_t, do_)
                dp_ref[rows, kc] = ((dq_sc + dqe * eb) * (GLA_DK ** -0.5)).astype(dp_ref.dtype)
                dp_ref[rows, GLA_QK + hh * GLA_DK:GLA_QK + (hh + 1) * GLA_DK] = (dk_sc + dke * elast).astype(dp_ref.dtype)
                t_ke = dke * ke
                db = q * dq_sc - k * dk_sc + dqe * qe - t_ke
                db = db + jnp.where(ri == C - 1, jnp.sum(t_ke, axis=0, keepdims=True) + deblast * eblast, 0.0)
                dla = _running_sum(db, reverse=True)
                yield
                dp_ref[rows, 2 * GLA_QK + hh * GLA_DV:2 * GLA_QK + (hh + 1) * GLA_DV] = (dv1 + dv2).astype(dp_ref.dtype)
                dla_ref[rows, kc] = dla
                ds1 = ds1 * eblast + ds_new
            dstate[hh] = ds1

        _round_robin(head(hh) for hh in range(GLA_HEADS))

    rev = lambda n: N - 1 - n
    q_spec, k_spec, v_spec, la_spec, o_spec, s_spec = _gla_specs(rev)
    return pl.pallas_call(
        body, name=name, grid=(N,),
        in_specs=[q_spec, k_spec, v_spec, la_spec, o_spec, s_spec, _ANY],
        out_specs=[pl.BlockSpec((GLA_STEP_ROWS, qkv_w), lambda n: (rev(n), C_GQ // qkv_w)), la_spec],
        out_shape=[jax.ShapeDtypeStruct(d_proj.shape, d_proj.dtype), jax.ShapeDtypeStruct((M, GLA_QK), F32)],
        input_output_aliases={6: 0},
        scratch_shapes=[pltpu.VMEM((GLA_HEADS, GLA_DV, GLA_DK), F32)],
        compiler_params=_params("arbitrary"),
    )(proj, proj, proj, la, do, s_all, d_proj)


def _head_norm(o, wn):
    r = lax.rsqrt(jnp.mean(o * o, axis=-1, keepdims=True) + NORM_EPS)
    return o * r, r


def _mix_heads():
    heads = [(0, GDN_DV, hh * GDN_DV, hh * GDN_DV) for hh in range(GDN_HEADS)]
    heads += [(1, GLA_DV, GDN_V + hh * GLA_DV, hh * GLA_DV) for hh in range(GLA_HEADS)]
    return heads


def _mix_fwd(o_gdn, o_gla, proj, wn_gdn, wn_gla, *, name):
    M = proj.shape[0]
    tm = _tile(M, 344, 16)

    def body(og_ref, ol_ref, z_ref, r_ref, wg_ref, wl_ref, m_ref):
        srcs = ((og_ref, z_ref, wg_ref), (ol_ref, r_ref, wl_ref))
        for grp, width, mcol, col in _mix_heads():
            o_ref, gate_ref, w_ref = srcs[grp]
            xhat, _ = _head_norm(o_ref[:, col:col + width].astype(F32), None)
            gate, _ = _silu_and_grad(gate_ref[:, col:col + width].astype(F32))
            m_ref[:, mcol:mcol + width] = (xhat * w_ref[...] * gate).astype(m_ref.dtype)

    full = lambda s: pl.BlockSpec(s, lambda i: (0, 0))
    return pl.pallas_call(
        body, name=name, grid=(M // tm,),
        in_specs=[pl.BlockSpec((tm, GDN_V), lambda i: (i, 0)), pl.BlockSpec((tm, GLA_V), lambda i: (i, 0)),
                  pl.BlockSpec((tm, GDN_V), lambda i: (i, C_Z // GDN_V)),
                  pl.BlockSpec((tm, GLA_V), lambda i: (i, C_GR // GLA_V)),
                  full((1, GDN_DV)), full((1, GLA_DV))],
        out_specs=pl.BlockSpec((tm, D_MODEL), lambda i: (i, 0)),
        out_shape=jax.ShapeDtypeStruct((M, D_MODEL), BF16),
        compiler_params=_params("parallel"),
    )(o_gdn, o_gla, proj, proj, wn_gdn, wn_gla)


def _mix_bwd(o_gdn, o_gla, proj, wn_gdn, wn_gla, dmixed, *, name):
    M = proj.shape[0]
    tm = _tile(M, 344, 16)
    g_ = M // tm
    assert C_Z == 0 and C_GR == GDN_V

    def body(og_ref, ol_ref, z_ref, r_ref, wg_ref, wl_ref, dm_ref,
             dog_ref, dol_ref, dzr_ref, dwg_ref, dwl_ref):
        i = pl.program_id(0)
        srcs = ((og_ref, z_ref, wg_ref, dog_ref), (ol_ref, r_ref, wl_ref, dol_ref))
        dws = [jnp.zeros((1, GDN_DV), F32), jnp.zeros((1, GLA_DV), F32)]
        for grp, width, mcol, col in _mix_heads():
            o_ref, gate_ref, w_ref, do_ref = srcs[grp]
            cols = slice(col, col + width)
            xhat, r = _head_norm(o_ref[:, cols].astype(F32), None)
            gate, dgate_dc = _silu_and_grad(gate_ref[:, cols].astype(F32))
            dm = dm_ref[:, mcol:mcol + width]
            dzr_ref[:, mcol:mcol + width] = (dm * xhat * w_ref[...] * dgate_dc).astype(dzr_ref.dtype)
            dnorm = dm * gate
            dws[grp] = dws[grp] + jnp.sum(dnorm * xhat, axis=0, keepdims=True)
            dxhat = dnorm * w_ref[...]
            do_ref[:, cols] = r * (dxhat - xhat * jnp.mean(dxhat * xhat, axis=-1, keepdims=True))

        @pl.when(i == 0)
        def _():
            dwg_ref[...] = dws[0]
            dwl_ref[...] = dws[1]

        @pl.when(i > 0)
        def _():
            dwg_ref[...] += dws[0]
            dwl_ref[...] += dws[1]

    full = lambda s: pl.BlockSpec(s, lambda i: (0, 0))
    half = pl.BlockSpec((tm, GDN_V), lambda i: (i, 0))
    return pl.pallas_call(
        body, name=name, grid=(g_,),
        in_specs=[half, half, pl.BlockSpec((tm, GDN_V), lambda i: (i, C_Z // GDN_V)),
                  pl.BlockSpec((tm, GLA_V), lambda i: (i, C_GR // GLA_V)),
                  full((1, GDN_DV)), full((1, GLA_DV)), pl.BlockSpec((tm, D_MODEL), lambda i: (i, 0))],
        out_specs=[half, half, pl.BlockSpec((tm, GDN_V + GLA_V), lambda i: (i, 0)),
                   full((1, GDN_DV)), full((1, GLA_DV))],
        out_shape=[jax.ShapeDtypeStruct((M, GDN_V), F32), jax.ShapeDtypeStruct((M, GLA_V), F32),
                   jax.ShapeDtypeStruct((M, D_PROJ), BF16),
                   jax.ShapeDtypeStruct((1, GDN_DV), F32), jax.ShapeDtypeStruct((1, GLA_DV), F32)],
        compiler_params=_params("arbitrary"),
    )(o_gdn, o_gla, proj, proj, wn_gdn, wn_gla, dmixed)


def _row_chunks(tm, parts=2):
    if tm % (16 * parts):
        return [slice(0, tm)]
    return [slice(p * (tm // parts), (p + 1) * (tm // parts)) for p in range(parts)]


def _swiglu_fwd(n, w_gate_t, w_up_t, *, name, tm=1376, tn=512):
    M, D = n.shape
    F = w_gate_t.shape[0]
    tm, tn = _tile(M, tm, 16), _tile(F, tn, 128)

    def body(n_ref, wg_ref, wu_ref, g_ref, u_ref, a_ref):
        wg, wu = wg_ref[...], wu_ref[...]
        for rows in _row_chunks(tm):
            x = n_ref[rows, :]
            g = _dot(x, wg, NT)
            u = _dot(x, wu, NT)
            s, _ = _silu_and_grad(g)
            g_ref[rows, :] = g.astype(g_ref.dtype)
            u_ref[rows, :] = u.astype(u_ref.dtype)
            a_ref[rows, :] = (s * u).astype(a_ref.dtype)

    w_spec = pl.BlockSpec((tn, D), lambda i, j: (j, 0))
    o_spec = pl.BlockSpec((tm, tn), lambda i, j: (i, j))
    return pl.pallas_call(
        body, name=name, grid=(M // tm, F // tn),
        in_specs=[pl.BlockSpec((tm, D), lambda i, j: (i, 0)), w_spec, w_spec], out_specs=[o_spec] * 3,
        out_shape=[jax.ShapeDtypeStruct((M, F), BF16)] * 3, compiler_params=_params("parallel", "parallel"),
    )(n, w_gate_t, w_up_t)


def _swiglu_bwd(dh, w_down, gate, up, *, name, after=None, tm=1376, tn=512):
    M, D = dh.shape
    F = w_down.shape[0]
    tm, tn = _tile(M, tm, 16), _tile(F, tn, 128)
    n_after = 0 if after is None else 1

    def body(*refs):
        dh_ref, w_ref, g_ref, u_ref, dg_ref, du_ref = refs[n_after:]
        w = w_ref[...]
        for rows in _row_chunks(tm):
            da = _dot(dh_ref[rows, :], w, NT)
            s, ds = _silu_and_grad(g_ref[rows, :].astype(F32))
            dg_ref[rows, :] = (da * u_ref[rows, :].astype(F32) * ds).astype(dg_ref.dtype)
            du_ref[rows, :] = (da * s).astype(du_ref.dtype)

    o_spec = pl.BlockSpec((tm, tn), lambda i, j: (i, j))
    return pl.pallas_call(
        body, name=name, grid=(M // tm, F // tn),
        in_specs=[_ANY] * n_after + [pl.BlockSpec((tm, D), lambda i, j: (i, 0)),
                                     pl.BlockSpec((tn, D), lambda i, j: (j, 0)), o_spec, o_spec],
        out_specs=[o_spec, o_spec], out_shape=[jax.ShapeDtypeStruct((M, F), BF16)] * 2,
        compiler_params=_params("parallel", "parallel"),
    )(*((after,) if n_after else ()), dh, w_down, gate, up)


def _adamw_update(w, g, m, v):
    nm = ADAM_B1 * m + (1.0 - ADAM_B1) * g
    nv = ADAM_B2 * v + (1.0 - ADAM_B2) * (g * g)
    m_hat = nm / (1.0 - ADAM_B1 ** ADAM_STEP)
    v_hat = nv / (1.0 - ADAM_B2 ** ADAM_STEP)
    return -ADAM_LR * (m_hat / (jnp.sqrt(v_hat) + ADAM_EPS) + ADAM_WD * w), nm, nv


def _adamw(w, g, m, v, *, name):
    shape = w.shape
    cols = shape[-1]
    rows = w.size // cols
    w2, g2, m2, v2 = (t.reshape(rows, cols) for t in (w, g, m, v))
    if rows % 8 == 0 or cols % 128 != 0:
        tr, tc = (_tile(rows, 256, 8) if rows % 8 == 0 else rows), cols
    else:
        tr, tc = rows, _tile(cols, 256, 128)

    def body(w_ref, g_ref, m_ref, v_ref, d_ref, nm_ref, nv_ref):
        d_ref[...], nm_ref[...], nv_ref[...] = _adamw_update(w_ref[...], g_ref[...], m_ref[...], v_ref[...])

    blk = pl.BlockSpec((tr, tc), lambda i, j: (i, j))
    outs = pl.pallas_call(
        body, name=name, grid=(rows // tr, cols // tc), in_specs=[blk] * 4, out_specs=[blk] * 3,
        out_shape=[jax.ShapeDtypeStruct((rows, cols), F32)] * 3, compiler_params=_params("parallel", "parallel"),
    )(w2, g2, m2, v2)
    return tuple(t.reshape(shape) for t in outs)


def _sum_slabs(x, *, name):
    _, R, C = x.shape
    sub = 16 if x.dtype == BF16 else 8
    if R % sub == 0:
        tr, tc = _tile(R, 128, sub), C
    else:
        tr, tc = R, _tile(C, 256, 128)

    def body(x_ref, o_ref):
        acc = x_ref[0].astype(F32)
        for s in range(1, N_DEV):
            acc = acc + x_ref[s].astype(F32)
        o_ref[...] = acc

    return pl.pallas_call(
        body, name=name, grid=(R // tr, C // tc),
        in_specs=[pl.BlockSpec((N_DEV, tr, tc), lambda i, j: (0, i, j))],
        out_specs=pl.BlockSpec((tr, tc), lambda i, j: (i, j)),
        out_shape=jax.ShapeDtypeStruct((R, C), F32), compiler_params=_params("parallel", "parallel"),
    )(x)


def _sum_adamw(x, w, m, v, *, name):
    _, R, C = x.shape
    if R % 16 == 0:
        tr, tc = _tile(R, 128, 16), C
    else:
        tr, tc = R, _tile(C, 256, 128)

    def body(x_ref, w_ref, m_ref, v_ref, g_ref, d_ref, nm_ref, nv_ref):
        g = x_ref[0].astype(F32)
        for s in range(1, N_DEV):
            g = g + x_ref[s].astype(F32)
        g_ref[...] = g
        d_ref[...], nm_ref[...], nv_ref[...] = _adamw_update(w_ref[...], g, m_ref[...], v_ref[...])

    blk = pl.BlockSpec((tr, tc), lambda i, j: (i, j))
    return pl.pallas_call(
        body, name=name, grid=(R // tr, C // tc),
        in_specs=[pl.BlockSpec((N_DEV, tr, tc), lambda i, j: (0, i, j)), blk, blk, blk], out_specs=[blk] * 4,
        out_shape=[jax.ShapeDtypeStruct((R, C), F32)] * 4, compiler_params=_params("parallel", "parallel"),
    )(x, w, m, v)


def _peers():
    x, y, c = lax.axis_index("x"), lax.axis_index("y"), lax.axis_index("c")
    me = 4 * x + 2 * y + c
    peers = []
    for k in range(1, N_DEV):
        px = 1 - x if k & 4 else x
        py = 1 - y if k & 2 else y
        pc = 1 - c if k & 1 else c
        peers.append(((px, py, pc), 4 * px + 2 * py + pc))
    return me, peers


_HBM = pl.BlockSpec(memory_space=pltpu.HBM)
_SEM = pl.BlockSpec(memory_space=pltpu.SEMAPHORE)
_EFFECT = pltpu.SideEffectType.DATAFLOW_SIDE_EFFECTING


PLAN_GATHER = tuple((k, "x", 0) for k in range(1, N_DEV))
PLAN_SCATTER = tuple((k, "xk", 0) for k in range(1, N_DEV))
PLAN_GATHER_CHIPS = tuple((k, "x", 0) for k in (1, 2, 4, 6))
PLAN_GATHER_PASS_ON = tuple((1, ("land", q), q) for q in (2, 4, 6))


def _plan_refs(plan, j, x_ref, land_ref, me, peers, receiving):
    k, source, r = plan[j]
    index_of = lambda q: me if q == 0 else peers[q - 1][1]
    pos, target = peers[k - 1]
    if source == "x":
        src = x_ref
    elif source == "xk":
        src = x_ref.at[target]
    else:
        src = land_ref.at[index_of(source[1])]
    return pos, src, land_ref.at[index_of(k ^ r) if receiving else index_of(r)]


def _exchange_start(x, *, plan, name, after=None, land=None, slab=None):
    n_after = 0 if after is None else 1
    n = len(plan)

    def body(*refs):
        x_ref, land_ref, send_sems, recv_sems, _, _, token = refs[n_after:]
        me, peers = _peers()
        for j in range(n):
            pos, src, dst = _plan_refs(plan, j, x_ref, land_ref, me, peers, receiving=False)
            pltpu.make_async_remote_copy(src_ref=src, dst_ref=dst, send_sem=send_sems.at[j], recv_sem=recv_sems.at[j],
                                         device_id=pos, device_id_type=pl.DeviceIdType.MESH).start()
        token[...] = jnp.zeros_like(token)

    if land is None:
        land = lax.empty((N_DEV,) + tuple(slab), x.dtype)
    return pl.pallas_call(
        body, name=name,
        out_shape=(pltpu.SemaphoreType.DMA((n,)), pltpu.SemaphoreType.DMA((n,)),
                   pltpu.HBM(x.shape, x.dtype), pltpu.HBM(land.shape, land.dtype), jax.ShapeDtypeStruct((8, 128), F32)),
        in_specs=[_ANY] * n_after + [_HBM, _HBM],
        out_specs=(_SEM, _SEM, _HBM, _HBM, pl.BlockSpec(memory_space=pltpu.VMEM)),
        input_output_aliases={n_after: 2, n_after + 1: 3},
        compiler_params=pltpu.CompilerParams(has_side_effects=_EFFECT),
    )(*((after,) if n_after else ()), pltpu.with_memory_space_constraint(x, pltpu.HBM),
      pltpu.with_memory_space_constraint(land, pltpu.HBM))


def _exchange_wait(handle, after, *, plan, name):
    send_sems, recv_sems, x_thru, land_thru, _ = handle
    afters = list(after) if isinstance(after, (list, tuple)) else [after]

    def body(x_ref, land_ref, send_sems, recv_sems, *rest):
        me, peers = _peers()
        for j in range(len(plan)):
            pos, src, dst = _plan_refs(plan, j, x_ref, land_ref, me, peers, receiving=True)
            cp = pltpu.make_async_remote_copy(src_ref=src, dst_ref=dst, send_sem=send_sems.at[j], recv_sem=recv_sems.at[j],
                                              device_id=pos, device_id_type=pl.DeviceIdType.MESH)
            cp.wait_send()
            cp.wait_recv()

    return pl.pallas_call(
        body, name=name,
        out_shape=(pltpu.HBM(x_thru.shape, x_thru.dtype), pltpu.HBM(land_thru.shape, land_thru.dtype)),
        in_specs=[_HBM, _HBM, _SEM, _SEM] + [_ANY] * len(afters), out_specs=(_HBM, _HBM),
        input_output_aliases={0: 0, 1: 1}, compiler_params=pltpu.CompilerParams(has_side_effects=_EFFECT),
    )(x_thru, land_thru, send_sems, recv_sems, *afters)


def _to_proj_rows(t):
    z = jnp.zeros((D_PROJ - C_SM - 2 * GDN_HEADS - GLA_RANK,) + t.shape[1:], t.dtype)
    return jnp.concatenate([t[R_Z:R_A], t[R_GR:R_LR], t[R_GQ:R_GR], t[:R_Z], t[R_A:R_GQ], t[R_LR:], z], axis=0)


def _from_proj_rows(t):
    ab = C_SM + 2 * GDN_HEADS
    return jnp.concatenate([t[C_QKV:C_SM], t[C_Z:C_GR], t[C_SM:ab], t[C_GQ:C_QKV], t[C_GR:C_GQ],
                            t[ab:ab + GLA_RANK]], axis=0)


def _local_step(x, target, meta, attn_nw, conv_w, a_log, dt_bias, gdn_nw, w2, b2, gla_nw, ffn_nw, final_nw,
                fetch, emit, start=None):
    head = jnp.concatenate([jnp.zeros((ROW_PAD, D_MODEL), F32), meta], axis=0)
    conv_w8 = jnp.concatenate([conv_w, jnp.zeros((8 - CONV_K, conv_w.shape[1]), F32)], axis=0)
    w2p = jnp.zeros((SM_W, GLA_QK), F32).at[2 * GDN_HEADS:2 * GDN_HEADS + GLA_RANK].set(w2)
    alog_p = jnp.zeros((1, SM_W), F32).at[:, :GDN_HEADS].set(a_log)
    dt_p = jnp.zeros((1, SM_W), F32).at[:, :GDN_HEADS].set(dt_bias)

    h0, n1 = _embed_norm(head, x, attn_nw, name="attn_norm", after=start)
    w_in_t = fetch("w_in_t", (n1, conv_w8, w2p, alog_p, dt_p))
    proj = _matmul(n1, w_in_t, mode="nt", name="in_proj", out_dtype=BF16)
    sm = _matmul(n1, w_in_t[C_SM:C_SM + SM_W], mode="nt", name="in_proj_gates", tn=SM_W)
    gb, la = _gates_fwd(sm, w2p, b2, alog_p, dt_p, name="gates")
    act = _prep_fwd(proj, conv_w8, name="gdn_prep")
    o_gdn, s_gdn, t_gdn = _gdn_fwd(act, gb, name="gdn_fwd")
    o_gla, s_gla = _gla_fwd(proj, la, name="gla_fwd")
    mixed = _mix_fwd(o_gdn, o_gla, proj, gdn_nw, gla_nw, name="mix")
    w_out = fetch("w_out", mixed)
    h1 = _matmul(mixed, w_out, mode="nn", add=h0, name="out_proj")
    n2 = _rmsnorm_fwd(h1, ffn_nw, name="ffn_norm")
    w_gate_t, w_up_t = fetch("w_gate_t", n2), fetch("w_up_t", n2)
    gate, up, hid = _swiglu_fwd(n2, w_gate_t, w_up_t, name="swiglu")
    w_down = fetch("w_down", hid)
    h2 = _matmul(hid, w_down, mode="nn", add=h1, name="ffn_down", tm=1376, tn=256)
    dh2, dh2_b, d_final_nw, loss = _loss_head(h2, final_nw, target, name="loss_head")

    wg = dict(mode="tn", out_dtype=BF16, tn=512)
    tok = emit("w_down", _matmul(hid, dh2_b, name="d_w_down", tm=704, **wg))
    d_gate, d_up = _swiglu_bwd(dh2_b, w_down, gate, up, name="d_swiglu", after=tok)
    tok = emit("w_gate_t", _matmul(d_gate, n2, name="d_w_gate", tm=704, **wg))
    tok = emit("w_up_t", _matmul(d_up, n2, name="d_w_up", tm=704, after=tok, **wg))
    d_n2 = _matmul_pair(d_gate, w_gate_t, d_up, w_up_t, name="d_n2", after=tok)
    dh1, dh1_b, d_ffn_nw = _rmsnorm_bwd(h1, ffn_nw, d_n2, dh2, name="d_ffn_norm")

    tok = emit("w_out", _matmul(mixed, dh1_b, name="d_w_out", tm=512, **wg))
    d_mixed = _matmul(dh1_b, w_out, mode="nt", name="d_mixed", after=tok)
    do_gdn, do_gla, d_proj, d_gdn_nw, d_gla_nw = _mix_bwd(o_gdn, o_gla, proj, gdn_nw, gla_nw, d_mixed, name="d_mix")
    d_proj, d_la = _gla_bwd(proj, la, do_gla, s_gla, d_proj, name="gla_bwd")
    dact, dgb_heads = _gdn_bwd(act, gb, do_gdn, s_gdn, t_gdn, name="gdn_bwd")
    d_proj, d_w2p, d_b2, d_alog, d_dt = _gates_bwd(sm, w2p, b2, alog_p, dt_p, dgb_heads, d_la, d_proj, name="d_gates")
    d_proj, d_conv_w8 = _prep_bwd(proj, conv_w8, dact, d_proj, name="d_gdn_prep")
    tok = emit("w_in_t", _matmul(d_proj, n1, name="d_w_in", tm=768, **wg))
    d_n1 = _matmul(d_proj, w_in_t, mode="nn", name="d_n1", tm=688, after=tok)
    grad_x, d_head, d_attn_nw = _embed_norm_bwd(h0, attn_nw, d_n1, dh1, name="d_attn_norm")

    return dict(
        loss=loss[0, 0], grad_x=grad_x, meta=d_head[ROW_PAD:HEAD_ROWS], attn_nw=d_attn_nw,
        conv_w=d_conv_w8[:CONV_K], a_log=d_alog[:, :GDN_HEADS], dt_bias=d_dt[:, :GDN_HEADS], gdn_nw=d_gdn_nw,
        w2=d_w2p[2 * GDN_HEADS:2 * GDN_HEADS + GLA_RANK], b2=d_b2, gla_nw=d_gla_nw, ffn_nw=d_ffn_nw,
        final_nw=d_final_nw)


SMALL_ROWS = 32


def kernel(x, meta_tokens, attn_norm_w, w_in, gdn_conv_w, gdn_a_log, gdn_dt_bias, gdn_norm_w, gla_gate_w2, gla_gate_b, gla_norm_w, w_out, ffn_norm_w, w_gate, w_up, w_down, final_norm_w, loss_target, m_meta_tokens, m_attn_norm_w, m_w_in, m_gdn_conv_w, m_gdn_a_log, m_gdn_dt_bias, m_gdn_norm_w, m_gla_gate_w2, m_gla_gate_b, m_gla_norm_w, m_w_out, m_ffn_norm_w, m_w_gate, m_w_up, m_w_down, m_final_norm_w, v_meta_tokens, v_attn_norm_w, v_w_in, v_gdn_conv_w, v_gdn_a_log, v_gdn_dt_bias, v_gdn_norm_w, v_gla_gate_w2, v_gla_gate_b, v_gla_norm_w, v_w_out, v_ffn_norm_w, v_w_gate, v_w_up, v_w_down, v_final_norm_w):
    me = 4 * lax.axis_index("x") + 2 * lax.axis_index("y") + lax.axis_index("c")

    n_conv = gdn_conv_w.shape[2]
    n_w2 = gla_gate_w2.shape[2]
    n_meta = meta_tokens.shape[1]
    small = jnp.zeros((40, n_conv), F32)
    small = small.at[0:N_META, :n_meta].set(meta_tokens)
    small = small.at[N_META:N_META + CONV_K, :].set(gdn_conv_w[0])
    small = small.at[24:24 + GLA_RANK, :n_w2].set(gla_gate_w2[0])
    small_h = _exchange_start(small, plan=PLAN_GATHER, slab=small.shape, name="gather_small_start")

    w_in_slab = w_in[0].T.astype(BF16)
    in_h = _exchange_start(w_in_slab, plan=PLAN_GATHER_CHIPS, slab=w_in_slab.shape, name="gather_w_in_start",
                           after=small_h[4])
    handles, tok = {}, in_h[4]
    for wname, slab in (("w_out", w_out[0]), ("w_gate_t", w_gate[0].T), ("w_up_t", w_up[0].T), ("w_down", w_down[0])):
        slab = slab.astype(BF16)
        handles[wname] = _exchange_start(slab, plan=PLAN_GATHER, slab=slab.shape, name="gather_" + wname + "_start", after=tok)
        tok = handles[wname][4]

    own, small_all = _exchange_wait(small_h, tok, plan=PLAN_GATHER, name="gather_small_wait")
    small_all = lax.dynamic_update_index_in_dim(small_all, own, me, 0)
    meta_f = small_all[:, 0:N_META, :n_meta].transpose(1, 0, 2).reshape(N_META, D_MODEL)
    conv_f = small_all[:, N_META:N_META + CONV_K, :].transpose(1, 0, 2).reshape(CONV_K, N_DEV * n_conv)
    w2_f = small_all[:, 24:24 + GLA_RANK, :n_w2].transpose(1, 0, 2).reshape(GLA_RANK, N_DEV * n_w2)

    def fetch(name, after):
        if name == "w_in_t":
            own, got = _exchange_wait(in_h, after, plan=PLAN_GATHER_CHIPS, name="gather_w_in_wait")
            pass_h = _exchange_start(own, plan=PLAN_GATHER_PASS_ON, land=got, name="pass_w_in_start")
            own, got = _exchange_wait(pass_h, pass_h[4], plan=PLAN_GATHER_PASS_ON, name="pass_w_in_wait")
            got = lax.dynamic_update_index_in_dim(got, own, me, 0)
            return _to_proj_rows(got.reshape(D_IN, D_MODEL))
        own, got = _exchange_wait(handles[name], after, plan=PLAN_GATHER, name="gather_" + name + "_wait")
        got = lax.dynamic_update_index_in_dim(got, own, me, 0)
        return got.reshape(N_DEV * got.shape[1], D_MODEL)

    sent = {}

    def emit(name, grad):
        if name == "w_in_t":
            grad = _from_proj_rows(grad)
        parts = grad.reshape(N_DEV, grad.shape[0] // N_DEV, D_MODEL)
        sent[name] = _exchange_start(parts, plan=PLAN_SCATTER, slab=parts.shape[1:], name="scatter_" + name + "_start")
        return sent[name][4]

    g = _local_step(x[0], loss_target[0], meta_f, attn_norm_w, conv_f, gdn_a_log, gdn_dt_bias, gdn_norm_w, w2_f,
                    gla_gate_b, gla_norm_w, ffn_norm_w, final_norm_w.reshape(1, D_MODEL), fetch, emit, start=tok)

    misc = jnp.concatenate([g["a_log"], g["dt_bias"], g["gdn_nw"], g["gla_nw"], g["b2"], g["loss"].reshape(1, 1)], axis=1)
    n_misc = misc.shape[1]
    misc = jnp.pad(misc, ((0, 0), (0, D_MODEL - n_misc)))
    rows = jnp.concatenate([g["attn_nw"], g["ffn_nw"], g["final_nw"], misc, g["meta"],
                            g["conv_w"].reshape(-1, D_MODEL), g["w2"].reshape(-1, D_MODEL)], axis=0)
    rows = jnp.pad(rows, ((0, SMALL_ROWS - rows.shape[0]), (0, 0)))
    rows_h = _exchange_start(rows, plan=PLAN_GATHER, slab=rows.shape, name="gather_small_grads_start")

    big = {}
    after = rows_h[4]
    for name, w, m, v, transposed in (("w_down", w_down, m_w_down, v_w_down, False), ("w_gate_t", w_gate, m_w_gate, v_w_gate, True),
                                      ("w_up_t", w_up, m_w_up, v_w_up, True), ("w_out", w_out, m_w_out, v_w_out, False),
                                      ("w_in_t", w_in, m_w_in, v_w_in, True)):
        own, got = _exchange_wait(sent[name], after, plan=PLAN_SCATTER, name="scatter_" + name + "_wait")
        got = lax.dynamic_update_index_in_dim(got, lax.dynamic_index_in_dim(own, me, 0, keepdims=False), me, 0)
        local = [t[0].T if transposed else t[0] for t in (w, m, v)]
        res = _sum_adamw(got, *local, name="adamw_" + name)
        big[name] = [t.T[None] if transposed else t[None] for t in res]
        after = res[0]

    own, got = _exchange_wait(rows_h, after, plan=PLAN_GATHER, name="gather_small_grads_wait")
    tot = _sum_slabs(lax.dynamic_update_index_in_dim(got, own, me, 0), name="sum_small_grads")
    grad_attn_nw, grad_ffn_nw, grad_final_nw = tot[0:1], tot[1:2], tot[2]
    grad_a_log = tot[3:4, 0:8]
    grad_dt = tot[3:4, 8:16]
    grad_gdn_nw = tot[3:4, 16:16 + GDN_DV]
    grad_gla_nw = tot[3:4, 144:144 + GLA_DV]
    grad_b2 = tot[3:4, 400:400 + GLA_QK]
    loss = tot[3, n_misc - 1]
    r0 = 4 + N_META
    grad_meta = lax.dynamic_slice(tot[4:r0], (0, me * n_meta), (N_META, n_meta))
    r1 = r0 + CONV_K * N_DEV * n_conv // D_MODEL
    grad_conv = lax.dynamic_slice(tot[r0:r1].reshape(CONV_K, N_DEV * n_conv), (0, me * n_conv), (CONV_K, n_conv))[None]
    r2 = r1 + GLA_RANK * N_DEV * n_w2 // D_MODEL
    grad_w2 = lax.dynamic_slice(tot[r1:r2].reshape(GLA_RANK, N_DEV * n_w2), (0, me * n_w2), (GLA_RANK, n_w2))[None]

    weights = [meta_tokens, attn_norm_w, w_in, gdn_conv_w, gdn_a_log, gdn_dt_bias, gdn_norm_w, gla_gate_w2,
               gla_gate_b, gla_norm_w, w_out, ffn_norm_w, w_gate, w_up, w_down, final_norm_w]
    grads = [grad_meta, grad_attn_nw, "w_in_t", grad_conv, grad_a_log, grad_dt, grad_gdn_nw, grad_w2,
             grad_b2, grad_gla_nw, "w_out", grad_ffn_nw, "w_gate_t", "w_up_t", "w_down", grad_final_nw]
    ms = [m_meta_tokens, m_attn_norm_w, m_w_in, m_gdn_conv_w, m_gdn_a_log, m_gdn_dt_bias, m_gdn_norm_w,
          m_gla_gate_w2, m_gla_gate_b, m_gla_norm_w, m_w_out, m_ffn_norm_w, m_w_gate, m_w_up, m_w_down, m_final_norm_w]
    vs = [v_meta_tokens, v_attn_norm_w, v_w_in, v_gdn_conv_w, v_gdn_a_log, v_gdn_dt_bias, v_gdn_norm_w,
          v_gla_gate_w2, v_gla_gate_b, v_gla_norm_w, v_w_out, v_ffn_norm_w, v_w_gate, v_w_up, v_w_down, v_final_norm_w]
    outs = [[], [], [], []]
    for idx, (w, gr, m, v) in enumerate(zip(weights, grads, ms, vs)):
        if isinstance(gr, str):
            res = big[gr]
        else:
            gr = gr.reshape(w.shape)
            res = (gr,) + _adamw(w, gr, m, v, name=f"adamw_{idx}")
        for lst, t in zip(outs, res):
            lst.append(t)
    return (loss, g["grad_x"][None], *outs[0], *outs[1], *outs[2], *outs[3])
```

```python
import functools

import jax
import jax.numpy as jnp
from jax import lax
from jax.experimental import pallas as pl
from jax.experimental.pallas import tpu as pltpu

F32 = jnp.float32
BF16 = jnp.bfloat16
_MXU_DTYPE = jnp.bfloat16

D_MODEL = 2048
N_META = 16
ROW_PAD = 48
HEAD_ROWS = ROW_PAD + N_META
CONV_K = 4
GDN_HEADS, GDN_DK, GDN_DV, GDN_CHUNK = 8, 128, 128, 64
GLA_HEADS, GLA_DK, GLA_DV, GLA_CHUNK = 4, 128, 256, 16
GLA_RANK = 16
GLA_GATE_NORMALIZER = 16.0
GDN_QK = GDN_HEADS * GDN_DK
GDN_V = GDN_HEADS * GDN_DV
GLA_QK = GLA_HEADS * GLA_DK
GLA_V = GLA_HEADS * GLA_DV
D_FF = 5632
D_IN = 7200
NORM_EPS = 1e-6
C_Z, C_GR, C_GQ, C_GK, C_GV, C_QKV, C_SM = 0, 1024, 2048, 2560, 3072, 4096, 7168
SM_W = 128
D_PROJ = 7680
R_Z, R_A, R_B, R_GQ, R_GK, R_GV, R_GR, R_LR = 3072, 4096, 4104, 4112, 4624, 5136, 6160, 7184

ADAM_LR, ADAM_B1, ADAM_B2, ADAM_EPS, ADAM_WD, ADAM_STEP = 0.001, 0.9, 0.999, 1e-08, 0.01, 10

N_DEV = 8
VMEM_LIMIT = 56 * 1024 * 1024

NN = (((1,), (0,)), ((), ()))
NT = (((1,), (1,)), ((), ()))
TN = (((0,), (0,)), ((), ()))


def _dot(a, b, dims=NN):
    return lax.dot_general(a.astype(_MXU_DTYPE), b.astype(_MXU_DTYPE), dims, preferred_element_type=F32)


def _running_sum(x, reverse=False):
    n = x.shape[0]
    row = lax.broadcasted_iota(jnp.int32, x.shape, 0)
    s = 1
    while s < n:
        if reverse:
            x = x + jnp.where(row < n - s, pltpu.roll(x, n - s, 0), 0.0)
        else:
            x = x + jnp.where(row >= s, pltpu.roll(x, s, 0), 0.0)
        s *= 2
    return x


def _dot3(a, b):
    ah = a.astype(BF16)
    al = (a - ah.astype(F32)).astype(BF16)
    bh = b.astype(BF16)
    bl = (b - bh.astype(F32)).astype(BF16)
    d = functools.partial(lax.dot_general, dimension_numbers=NN, preferred_element_type=F32)
    return d(ah, bh) + (d(ah, bl) + d(al, bh))


def _tile(n, target, mult=8):
    best = None
    for t in range(mult, min(n, target) + 1, mult):
        if n % t == 0:
            best = t
    return best if best is not None else n


def _params(*sem):
    return pltpu.CompilerParams(dimension_semantics=sem, vmem_limit_bytes=VMEM_LIMIT)


def _sigmoid(x):
    return 0.5 * jnp.tanh(0.5 * x) + 0.5


def _softplus(x):
    return jnp.maximum(x, 0.0) + jnp.log1p(jnp.exp(-jnp.abs(x)))


def _silu_and_grad(c):
    s = _sigmoid(c)
    return c * s, s * (1.0 + c * (1.0 - s))


_ANY = pl.BlockSpec(memory_space=pl.ANY)


def _matmul(a, b, *, mode, name, out_dtype=F32, add=None, after=None, tm=1376, tn=512):
    if mode == "tn":
        K, M = a.shape
        N = b.shape[1]
    else:
        M, K = a.shape
        N = b.shape[0] if mode == "nt" else b.shape[1]
    tm = _tile(M, tm, 128 if mode == "tn" else 16)
    tn = _tile(N, tn, 128)
    dims = {"nn": NN, "nt": NT, "tn": TN}[mode]
    n_after = 0 if after is None else 1

    def body(*refs):
        refs = refs[n_after:]
        r = _dot(refs[0][...], refs[1][...], dims)
        if add is not None:
            r = r + refs[2][...]
        refs[-1][...] = r.astype(out_dtype)

    a_spec = pl.BlockSpec((K, tm), lambda i, j: (0, i)) if mode == "tn" else pl.BlockSpec((tm, K), lambda i, j: (i, 0))
    b_spec = pl.BlockSpec((tn, K), lambda i, j: (j, 0)) if mode == "nt" else pl.BlockSpec((K, tn), lambda i, j: (0, j))
    o_spec = pl.BlockSpec((tm, tn), lambda i, j: (i, j))
    in_specs = [_ANY] * n_after + [a_spec, b_spec] + ([o_spec] if add is not None else [])
    args = ((after,) if n_after else ()) + (a, b) + ((add,) if add is not None else ())
    return pl.pallas_call(
        body, name=name, grid=(M // tm, N // tn), in_specs=in_specs, out_specs=o_spec,
        out_shape=jax.ShapeDtypeStruct((M, N), out_dtype), compiler_params=_params("parallel", "parallel"),
    )(*args)


def _in_proj(n, w_in_t, *, name, tm=1376, tn=512):
    M, K = n.shape
    N = w_in_t.shape[0]
    tm, tn = _tile(M, tm, 16), _tile(N, tn, 128)
    assert C_SM % tn == 0
    j_small = C_SM // tn

    def body(n_ref, w_ref, o_ref, sm_ref):
        r = _dot(n_ref[...], w_ref[...], NT)
        o_ref[...] = r.astype(o_ref.dtype)

        @pl.when(pl.program_id(1) == j_small)
        def _():
            sm_ref[...] = r[:, 0:SM_W]

    return pl.pallas_call(
        body, name=name, grid=(M // tm, N // tn),
        in_specs=[pl.BlockSpec((tm, K), lambda i, j: (i, 0)), pl.BlockSpec((tn, K), lambda i, j: (j, 0))],
        out_specs=[pl.BlockSpec((tm, tn), lambda i, j: (i, j)), pl.BlockSpec((tm, SM_W), lambda i, j: (i, 0))],
        out_shape=[jax.ShapeDtypeStruct((M, N), BF16), jax.ShapeDtypeStruct((M, SM_W), F32)],
        compiler_params=_params("parallel", "arbitrary"),
    )(n, w_in_t)


def _matmul_pair(a1, b1, a2, b2, *, name, after=None, tm=688, tn=256):
    M, K = a1.shape
    N = b1.shape[1]
    tm, tn = _tile(M, tm, 16), _tile(N, tn, 128)
    n_after = 0 if after is None else 1

    def body(*refs):
        a1_ref, b1_ref, a2_ref, b2_ref, o_ref = refs[n_after:]
        o_ref[...] = _dot(a1_ref[...], b1_ref[...]) + _dot(a2_ref[...], b2_ref[...])

    a_spec = pl.BlockSpec((tm, K), lambda i, j: (i, 0))
    b_spec = pl.BlockSpec((K, tn), lambda i, j: (0, j))
    return pl.pallas_call(
        body, name=name, grid=(M // tm, N // tn), in_specs=[_ANY] * n_after + [a_spec, b_spec, a_spec, b_spec],
        out_specs=pl.BlockSpec((tm, tn), lambda i, j: (i, j)), out_shape=jax.ShapeDtypeStruct((M, N), F32),
        compiler_params=_params("parallel", "parallel"),
    )(*((after,) if n_after else ()), a1, b1, a2, b2)


def _rmsnorm_fwd(h, w, *, name):
    M, D = h.shape
    tm = _tile(M, 688, 16)

    def body(h_ref, w_ref, n_ref):
        x = h_ref[...]
        r = lax.rsqrt(jnp.mean(x * x, axis=-1, keepdims=True) + NORM_EPS)
        n_ref[...] = (x * r * w_ref[...]).astype(n_ref.dtype)

    return pl.pallas_call(
        body, name=name, grid=(M // tm,),
        in_specs=[pl.BlockSpec((tm, D), lambda i: (i, 0)), pl.BlockSpec((1, D), lambda i: (0, 0))],
        out_specs=pl.BlockSpec((tm, D), lambda i: (i, 0)),
        out_shape=jax.ShapeDtypeStruct((M, D), BF16),
        compiler_params=_params("parallel"),
    )(h, w)


SEQ_BLOCK = HEAD_ROWS


def _seq_blocks_per_tile(rows):
    n = rows // SEQ_BLOCK
    return max(m for m in (1, 2, 3, 4) if n % m == 0)


def _seq_specs(m, D):
    return [pl.BlockSpec((SEQ_BLOCK, D), functools.partial(lambda i, k: (jnp.maximum(m * i + k - 1, 0), 0), k=k))
            for k in range(m)]


def _embed_norm(head, x, w, *, name, after=None):
    S, D = x.shape
    m = _seq_blocks_per_tile(S + HEAD_ROWS)
    n_after = 0 if after is None else 1

    def body(*refs):
        refs = refs[n_after:]
        head_ref, x_refs, w_ref, h_ref, n_ref = refs[0], refs[1:1 + m], refs[1 + m], refs[2 + m], refs[3 + m]
        i = pl.program_id(0)
        for k in range(m):
            blk = x_refs[k][...]
            if k == 0:
                blk = jnp.where(i == 0, head_ref[...], blk)
            rows = slice(k * SEQ_BLOCK, (k + 1) * SEQ_BLOCK)
            h_ref[rows, :] = blk
            r = lax.rsqrt(jnp.mean(blk * blk, axis=-1, keepdims=True) + NORM_EPS)
            n_ref[rows, :] = (blk * r * w_ref[...]).astype(n_ref.dtype)

    tile = pl.BlockSpec((m * SEQ_BLOCK, D), lambda i: (i, 0))
    return pl.pallas_call(
        body, name=name, grid=((S + HEAD_ROWS) // (m * SEQ_BLOCK),),
        in_specs=[_ANY] * n_after + [pl.BlockSpec((SEQ_BLOCK, D), lambda i: (0, 0))] + _seq_specs(m, D)
        + [pl.BlockSpec((1, D), lambda i: (0, 0))],
        out_specs=[tile, tile],
        out_shape=[jax.ShapeDtypeStruct((S + HEAD_ROWS, D), F32), jax.ShapeDtypeStruct((S + HEAD_ROWS, D), BF16)],
        compiler_params=_params("parallel"),
    )(*((after,) if n_after else ()), head, *([x] * m), w)


def _embed_norm_bwd(h, w, dn, dres, *, name):
    M, D = h.shape
    S = M - HEAD_ROWS
    m = _seq_blocks_per_tile(S)
    g = S // (m * SEQ_BLOCK)

    def one(x, dn_, dres_, w_):
        r = lax.rsqrt(jnp.mean(x * x, axis=-1, keepdims=True) + NORM_EPS)
        xhat = x * r
        dxhat = dn_ * w_
        dh = dres_ + r * (dxhat - xhat * jnp.mean(dxhat * xhat, axis=-1, keepdims=True))
        return dh, jnp.sum((dn_ * xhat).reshape(SEQ_BLOCK // 8, 8, D), axis=0)

    def body(*refs):
        w_ref = refs[0]
        groups = [refs[1 + a * (m + 1):1 + (a + 1) * (m + 1)] for a in range(3)]
        gx_ref, dhead_ref, dw_ref, acc_ref = refs[1 + 3 * (m + 1):]
        i = pl.program_id(0)
        w_ = w_ref[...]
        part = jnp.zeros((8, D), F32)
        for k in range(m):
            dh, p = one(*(grp[1 + k][...] for grp in groups), w_)
            gx_ref[k * SEQ_BLOCK:(k + 1) * SEQ_BLOCK, :] = dh
            part = part + p

        @pl.when(i == 0)
        def _():
            dh, p = one(*(grp[0][...] for grp in groups), w_)
            dhead_ref[...] = dh
            acc_ref[...] = part + p

        @pl.when(i > 0)
        def _():
            acc_ref[...] += part

        @pl.when(i == g - 1)
        def _():
            dw_ref[...] = jnp.sum(acc_ref[...], axis=0, keepdims=True)

    first = pl.BlockSpec((SEQ_BLOCK, D), lambda i: (0, 0))
    blocks = [pl.BlockSpec((SEQ_BLOCK, D), functools.partial(lambda i, k: (m * i + k + 1, 0), k=k)) for k in range(m)]
    vec = pl.BlockSpec((1, D), lambda i: (0, 0))
    return pl.pallas_call(
        body, name=name, grid=(g,), in_specs=[vec] + ([first] + blocks) * 3,
        out_specs=[pl.BlockSpec((m * SEQ_BLOCK, D), lambda i: (i, 0)), first, vec],
        out_shape=[jax.ShapeDtypeStruct((S, D), F32), jax.ShapeDtypeStruct((SEQ_BLOCK, D), F32),
                   jax.ShapeDtypeStruct((1, D), F32)],
        scratch_shapes=[pltpu.VMEM((8, D), F32)],
        compiler_params=_params("arbitrary"),
    )(w, *([h] * (m + 1)), *([dn] * (m + 1)), *([dres] * (m + 1)))


def _rmsnorm_bwd(h, w, dn, dres, *, name):
    M, D = h.shape
    tm = _tile(M, 344, 16)
    g = M // tm

    def body(h_ref, w_ref, dn_ref, dres_ref, dh_ref, dhb_ref, dw_ref, acc_ref):
        i = pl.program_id(0)
        x = h_ref[...]
        r = lax.rsqrt(jnp.mean(x * x, axis=-1, keepdims=True) + NORM_EPS)
        xhat = x * r
        dn_ = dn_ref[...]
        dxhat = dn_ * w_ref[...]
        dh = dres_ref[...] + r * (dxhat - xhat * jnp.mean(dxhat * xhat, axis=-1, keepdims=True))
        dh_ref[...] = dh
        dhb_ref[...] = dh.astype(dhb_ref.dtype)
        part = jnp.sum((dn_ * xhat).reshape(tm // 8, 8, D), axis=0)

        @pl.when(i == 0)
        def _():
            acc_ref[...] = part

        @pl.when(i > 0)
        def _():
            acc_ref[...] += part

        @pl.when(i == g - 1)
        def _():
            dw_ref[...] = jnp.sum(acc_ref[...], axis=0, keepdims=True)

    row = pl.BlockSpec((tm, D), lambda i: (i, 0))
    vec = pl.BlockSpec((1, D), lambda i: (0, 0))
    return pl.pallas_call(
        body, name=name, grid=(g,), in_specs=[row, vec, row, row],
        out_specs=[row, row, vec],
        out_shape=[jax.ShapeDtypeStruct((M, D), F32), jax.ShapeDtypeStruct((M, D), BF16),
                   jax.ShapeDtypeStruct((1, D), F32)],
        scratch_shapes=[pltpu.VMEM((8, D), F32)],
        compiler_params=_params("arbitrary"),
    )(h, w, dn, dres)


def _loss_head(h, w, target, *, name):
    M, D = h.shape
    m = _seq_blocks_per_tile(M)
    tm = m * SEQ_BLOCK
    g = M // tm

    def body(h_ref, w_ref, *rest):
        t_refs = rest[:m]
        dh_ref, dhb_ref, dw_ref, loss_ref, acc_ref, lacc_ref = rest[m:]
        i = pl.program_id(0)
        x = h_ref[...]
        row = i * tm + lax.broadcasted_iota(jnp.int32, (tm, 1), 0)
        live = row >= HEAD_ROWS
        r = lax.rsqrt(jnp.mean(x * x, axis=-1, keepdims=True) + NORM_EPS)
        xhat = x * r
        t = jnp.concatenate([t_ref[...] for t_ref in t_refs], axis=0)
        err = jnp.where(live, xhat * w_ref[...] - t, 0.0)
        dy = err * (1.0 / D)
        dxhat = dy * w_ref[...]
        dh = r * (dxhat - xhat * jnp.mean(dxhat * xhat, axis=-1, keepdims=True))
        dh_ref[...] = dh
        dhb_ref[...] = dh.astype(dhb_ref.dtype)
        part = jnp.sum((dy * xhat).reshape(tm // 8, 8, D), axis=0)
        lpart = jnp.sum((err * err).reshape(tm // 8, 8, D), axis=0)

        @pl.when(i == 0)
        def _():
            acc_ref[...] = part
            lacc_ref[...] = lpart

        @pl.when(i > 0)
        def _():
            acc_ref[...] += part
            lacc_ref[...] += lpart

        @pl.when(i == g - 1)
        def _():
            dw_ref[...] = jnp.sum(acc_ref[...], axis=0, keepdims=True)
            tot = jnp.sum(jnp.sum(lacc_ref[...], axis=0, keepdims=True), axis=1, keepdims=True)
            loss_ref[...] = jnp.broadcast_to(tot * (0.5 / D), (1, 128))

    row = pl.BlockSpec((tm, D), lambda i: (i, 0))
    vec = pl.BlockSpec((1, D), lambda i: (0, 0))
    return pl.pallas_call(
        body, name=name, grid=(g,), in_specs=[row, vec] + _seq_specs(m, D),
        out_specs=[row, row, vec, pl.BlockSpec((1, 128), lambda i: (0, 0))],
        out_shape=[jax.ShapeDtypeStruct((M, D), F32), jax.ShapeDtypeStruct((M, D), BF16),
                   jax.ShapeDtypeStruct((1, D), F32), jax.ShapeDtypeStruct((1, 128), F32)],
        scratch_shapes=[pltpu.VMEM((8, D), F32), pltpu.VMEM((8, D), F32)],
        compiler_params=_params("arbitrary"),
    )(h, w, *([target] * m))


def _gate_terms(sm, w2p, b2, alog_p, dt_p, row0):
    tm = sm.shape[0]
    lane = lax.broadcasted_iota(jnp.int32, (tm, SM_W), 1)
    live = (row0 + lax.broadcasted_iota(jnp.int32, (tm, 1), 0)) >= ROW_PAD
    pre = sm + dt_p
    neg_a = -jnp.exp(alog_p)
    g = neg_a * _softplus(pre)
    beta = _sigmoid(sm)
    z = _dot(sm, w2p) + b2
    return lane, live, pre, neg_a, g, beta, z


def _gates_fwd(sm, w2p, b2, alog_p, dt_p, *, name):
    M = sm.shape[0]
    tm = _tile(M, 688, 8)

    def body(sm_ref, w2_ref, b2_ref, al_ref, dt_ref, gb_ref, la_ref):
        row0 = pl.program_id(0) * tm
        lane, live, _, _, g, beta, z = _gate_terms(sm_ref[...].astype(F32), w2_ref[...], b2_ref[...], al_ref[...], dt_ref[...], row0)
        gb = jnp.where(lane < GDN_HEADS, g, jnp.where(lane < 2 * GDN_HEADS, beta, 0.0))
        gb_ref[...] = jnp.where(live, gb, 0.0)
        la = (jnp.minimum(z, 0.0) - jnp.log1p(jnp.exp(-jnp.abs(z)))) * (1.0 / GLA_GATE_NORMALIZER)
        la_ref[...] = jnp.where(live, la, 0.0)

    full = lambda s: pl.BlockSpec(s, lambda i: (0, 0))
    return pl.pallas_call(
        body, name=name, grid=(M // tm,),
        in_specs=[pl.BlockSpec((tm, SM_W), lambda i: (i, 0)), full((SM_W, GLA_QK)), full((1, GLA_QK)),
                  full((1, SM_W)), full((1, SM_W))],
        out_specs=[pl.BlockSpec((tm, SM_W), lambda i: (i, 0)), pl.BlockSpec((tm, GLA_QK), lambda i: (i, 0))],
        out_shape=[jax.ShapeDtypeStruct((M, SM_W), F32), jax.ShapeDtypeStruct((M, GLA_QK), F32)],
        compiler_params=_params("parallel"),
    )(sm, w2p, b2, alog_p, dt_p)


def _gates_bwd(sm, w2p, b2, alog_p, dt_p, dgb_heads, dla, d_proj, *, name):
    M = sm.shape[0]
    tm = _tile(M, 688, 8)
    g_ = M // tm

    tail_w = D_PROJ - C_SM

    def body(sm_ref, w2_ref, b2_ref, al_ref, dt_ref, dgb_ref, dla_ref, _,
             dsm_ref, dw2_ref, db2_ref, dal_ref, ddt_ref):
        i = pl.program_id(0)
        sm = sm_ref[...].astype(F32)
        lane, live, pre, neg_a, g, beta, z = _gate_terms(sm, w2_ref[...], b2_ref[...], al_ref[...], dt_ref[...], i * tm)
        dz = jnp.where(live, dla_ref[...] * (_sigmoid(-z) * (1.0 / GLA_GATE_NORMALIZER)), 0.0)
        dsm_lr = _dot(dz, w2_ref[...], NT)
        dgb = dgb_ref[0]
        for hh in range(1, GDN_HEADS):
            dgb = dgb + dgb_ref[hh]
        dgb = jnp.where(live, dgb, 0.0)
        da = dgb * neg_a * _sigmoid(pre)
        db = dgb * beta * (1.0 - beta)
        dsm = jnp.where(lane < GDN_HEADS, da, jnp.where(lane < 2 * GDN_HEADS, db, dsm_lr))
        dsm_ref[:, 0:SM_W] = dsm.astype(dsm_ref.dtype)
        if tail_w > SM_W:
            dsm_ref[:, SM_W:tail_w] = jnp.zeros((tm, tail_w - SM_W), dsm_ref.dtype)
        is_a = lane < GDN_HEADS
        dal = jnp.sum(jnp.where(is_a, dgb * g, 0.0), axis=0, keepdims=True)
        ddt = jnp.sum(jnp.where(is_a, da, 0.0), axis=0, keepdims=True)
        dw2 = _dot(sm, dz, TN)
        db2 = jnp.sum(dz, axis=0, keepdims=True)

        @pl.when(i == 0)
        def _():
            dw2_ref[...] = dw2
            db2_ref[...] = db2
            dal_ref[...] = dal
            ddt_ref[...] = ddt

        @pl.when(i > 0)
        def _():
            dw2_ref[...] += dw2
            db2_ref[...] += db2
            dal_ref[...] += dal
            ddt_ref[...] += ddt

    full = lambda s: pl.BlockSpec(s, lambda i: (0, 0))
    return pl.pallas_call(
        body, name=name, grid=(g_,),
        in_specs=[pl.BlockSpec((tm, SM_W), lambda i: (i, 0)), full((SM_W, GLA_QK)), full((1, GLA_QK)),
                  full((1, SM_W)), full((1, SM_W)),
                  pl.BlockSpec((GDN_HEADS, tm, SM_W), lambda i: (0, i, 0)),
                  pl.BlockSpec((tm, GLA_QK), lambda i: (i, 0)), _ANY],
        out_specs=[pl.BlockSpec((tm, tail_w), lambda i: (i, C_SM // tail_w)), full((SM_W, GLA_QK)), full((1, GLA_QK)),
                   full((1, SM_W)), full((1, SM_W))],
        out_shape=[jax.ShapeDtypeStruct(d_proj.shape, d_proj.dtype), jax.ShapeDtypeStruct((SM_W, GLA_QK), F32),
                   jax.ShapeDtypeStruct((1, GLA_QK), F32), jax.ShapeDtypeStruct((1, SM_W), F32),
                   jax.ShapeDtypeStruct((1, SM_W), F32)],
        input_output_aliases={7: 0},
        compiler_params=_params("arbitrary"),
    )(sm, w2p, b2, alog_p, dt_p, dgb_heads, dla, d_proj)


QKV_W = GDN_QK
N_QKV_GROUPS = 3
QKV_B0 = C_QKV // QKV_W
HALO = 16


def _conv_terms(x_ref, halo_ref, cw_ref, xs_ref, i, tm):
    xs_ref[HALO:HALO + tm, :] = x_ref[...].astype(F32)
    xs_ref[0:HALO, :] = jnp.where(i > 0, halo_ref[...].astype(F32), 0.0)
    cw = cw_ref[...]
    xs = xs_ref[...]
    taps = [(pltpu.roll(xs, CONV_K - 1 - t, 0) if t < CONV_K - 1 else xs)[HALO:HALO + tm, :] for t in range(CONV_K)]
    c = taps[0] * cw[0:1, :]
    for t in range(1, CONV_K):
        c = c + taps[t] * cw[t:t + 1, :]
    return c, taps


def _prep_fwd(proj, conv_w8, *, name):
    M = proj.shape[0]
    tm = _tile(M, 688, 16)

    def body(x_ref, halo_ref, cw_ref, o_ref, xs_ref):
        j, i = pl.program_id(0), pl.program_id(1)
        c, _ = _conv_terms(x_ref, halo_ref, cw_ref, xs_ref, i, tm)
        s, _ = _silu_and_grad(c)
        scale = jnp.where(j == 0, GDN_DK ** -0.5, 1.0)
        for hh in range(GDN_HEADS):
            cols = slice(hh * 128, (hh + 1) * 128)
            sh = s[:, cols]
            r = lax.rsqrt(jnp.sum(sh * sh, axis=-1, keepdims=True) + NORM_EPS)
            o_ref[:, cols] = jnp.where(j < 2, sh * (r * scale), sh)

    hb = tm // HALO
    return pl.pallas_call(
        body, name=name, grid=(N_QKV_GROUPS, M // tm),
        in_specs=[pl.BlockSpec((tm, QKV_W), lambda j, i: (i, QKV_B0 + j)),
                  pl.BlockSpec((HALO, QKV_W), lambda j, i: (jnp.maximum(i * hb - 1, 0), QKV_B0 + j)),
                  pl.BlockSpec((8, QKV_W), lambda j, i: (0, j))],
        out_specs=pl.BlockSpec((tm, QKV_W), lambda j, i: (i, j)),
        out_shape=jax.ShapeDtypeStruct((M, N_QKV_GROUPS * QKV_W), F32),
        scratch_shapes=[pltpu.VMEM((tm + HALO, QKV_W), F32)],
        compiler_params=_params("parallel", "arbitrary"),
    )(proj, proj, conv_w8)


def _prep_bwd(proj, conv_w8, dact, d_proj, *, name):
    M = proj.shape[0]
    tm = _tile(M, 688, 16)
    g_ = M // tm
    ext = tm + HALO

    def body(x_ref, prev_ref, next_ref, cw_ref, da_ref, dan_ref, _, o_ref, dcw_ref, xs_ref, das_ref, dcs_ref):
        j, i = pl.program_id(0), pl.program_id(1)
        not_last = i < g_ - 1
        xs_ref[0:HALO, :] = jnp.where(i > 0, prev_ref[...].astype(F32), 0.0)
        xs_ref[HALO:HALO + tm, :] = x_ref[...].astype(F32)
        xs_ref[HALO + tm:HALO + ext, :] = jnp.where(not_last, next_ref[...].astype(F32), 0.0)
        das_ref[0:tm, :] = da_ref[...]
        das_ref[tm:ext, :] = jnp.where(not_last, dan_ref[...], 0.0)
        cw = cw_ref[...]
        xs = xs_ref[...]
        taps = [(pltpu.roll(xs, CONV_K - 1 - t, 0) if t < CONV_K - 1 else xs)[HALO:HALO + ext, :] for t in range(CONV_K)]
        c = taps[0] * cw[0:1, :]
        for t in range(1, CONV_K):
            c = c + taps[t] * cw[t:t + 1, :]
        s, ds_dc = _silu_and_grad(c)
        scale = jnp.where(j == 0, GDN_DK ** -0.5, 1.0)
        for hh in range(GDN_HEADS):
            cols = slice(hh * 128, (hh + 1) * 128)
            sh = s[:, cols]
            r = lax.rsqrt(jnp.sum(sh * sh, axis=-1, keepdims=True) + NORM_EPS)
            da = das_ref[:, cols]
            y = sh * r
            dy = da * scale
            ds_norm = r * (dy - y * jnp.sum(dy * y, axis=-1, keepdims=True))
            dcs_ref[:, cols] = jnp.where(j < 2, ds_norm, da) * ds_dc[:, cols]
        dc = dcs_ref[...]
        acc = dc[0:tm, :] * cw[CONV_K - 1:CONV_K, :]
        for t in range(CONV_K - 1):
            acc = acc + pltpu.roll(dc, ext - (CONV_K - 1 - t), 0)[0:tm, :] * cw[t:t + 1, :]
        o_ref[...] = acc.astype(o_ref.dtype)
        r8 = lax.broadcasted_iota(jnp.int32, (8, QKV_W), 0)
        part = jnp.zeros((8, QKV_W), F32)
        for t in range(CONV_K):
            part = jnp.where(r8 == t, jnp.sum(dc[0:tm, :] * taps[t][0:tm, :], axis=0, keepdims=True), part)

        @pl.when(i == 0)
        def _():
            dcw_ref[...] = part

        @pl.when(i > 0)
        def _():
            dcw_ref[...] += part

    hb = tm // HALO
    last = M // HALO - 1
    prev_of = lambda i: jnp.maximum(i * hb - 1, 0)
    next_of = lambda i: jnp.minimum((i + 1) * hb, last)
    return pl.pallas_call(
        body, name=name, grid=(N_QKV_GROUPS, g_),
        in_specs=[pl.BlockSpec((tm, QKV_W), lambda j, i: (i, QKV_B0 + j)),
                  pl.BlockSpec((HALO, QKV_W), lambda j, i: (prev_of(i), QKV_B0 + j)),
                  pl.BlockSpec((HALO, QKV_W), lambda j, i: (next_of(i), QKV_B0 + j)),
                  pl.BlockSpec((8, QKV_W), lambda j, i: (0, j)),
                  pl.BlockSpec((tm, QKV_W), lambda j, i: (i, j)),
                  pl.BlockSpec((HALO, QKV_W), lambda j, i: (next_of(i), j)), _ANY],
        out_specs=[pl.BlockSpec((tm, QKV_W), lambda j, i: (i, QKV_B0 + j)), pl.BlockSpec((8, QKV_W), lambda j, i: (0, j))],
        out_shape=[jax.ShapeDtypeStruct(d_proj.shape, d_proj.dtype),
                   jax.ShapeDtypeStruct((8, N_QKV_GROUPS * QKV_W), F32)],
        input_output_aliases={6: 0},
        scratch_shapes=[pltpu.VMEM((HALO + ext, QKV_W), F32), pltpu.VMEM((ext, QKV_W), F32), pltpu.VMEM((ext, QKV_W), F32)],
        compiler_params=_params("parallel", "arbitrary"),
    )(proj, proj, proj, conv_w8, dact, dact, d_proj)


def _round_robin(gens):
    gens = list(gens)
    while gens:
        alive = []
        for gen in gens:
            try:
                next(gen)
                alive.append(gen)
            except StopIteration:
                pass
        gens = alive


def _unit_lower_inverse(a_low, eye):
    n = a_low.shape[0]
    ri = lax.broadcasted_iota(jnp.int32, (n, n), 0)
    ci = lax.broadcasted_iota(jnp.int32, (n, n), 1)
    same = lambda shift: (ri >> shift) == (ci >> shift)
    b = jnp.where(same(3), -a_low, 0.0)
    x = eye + b
    p2 = _dot3(b, b)
    yield
    x = x + _dot3(x, p2)
    p4 = _dot3(p2, p2)
    yield
    x = x + _dot3(x, p4)
    yield
    for shift in (3, 4, 5):
        between = jnp.where(same(shift + 1) & ~same(shift), a_low, 0.0)
        t = _dot3(between, x)
        yield
        x = x - _dot3(x, t)
        yield
    return x


class _GdnChunk:
    def build(self, q, k, v, gb, h, sum_on_mxu):
        C = GDN_CHUNK
        lane = lax.broadcasted_iota(jnp.int32, (C, SM_W), 1)
        g = jnp.sum(jnp.where(lane == h, gb, 0.0), axis=1, keepdims=True)
        self.beta = jnp.sum(jnp.where(lane == h + GDN_HEADS, gb, 0.0), axis=1, keepdims=True)
        ri = lax.broadcasted_iota(jnp.int32, (C, C), 0)
        ci = lax.broadcasted_iota(jnp.int32, (C, C), 1)
        self.causal = ri >= ci
        self.strict = ri > ci
        self.eye = (ri == ci).astype(F32)
        if sum_on_mxu:
            gcb = lax.dot_general(self.causal.astype(F32), jnp.broadcast_to(g, (C, SM_W)), NN,
                                  precision=lax.Precision.HIGHEST, preferred_element_type=F32)
        else:
            gcb = _running_sum(jnp.broadcast_to(g, (C, SM_W)))
        yield
        self.gcol = gcb[:, 0:1]
        grow = gcb.T[0:1, 0:C]
        self.decay = jnp.exp(jnp.where(self.causal, self.gcol - grow, -1e30))
        self.egc = jnp.exp(self.gcol)
        glast = gcb[C - 1:C, 0:1]
        self.elast = jnp.exp(glast - self.gcol)
        self.gl = jnp.exp(glast)
        self.q, self.k, self.v = q, k, v
        self.kb = k * self.beta
        m = _dot(self.kb, k, NT)
        n_ = _dot(q, k, NT)
        yield
        self.a_low = jnp.where(self.strict, m * self.decay, 0.0)
        self.p = n_ * self.decay
        self.qd = q * self.egc
        self.kd = k * self.elast
        self.bu = v * self.beta
        self.bw = self.kb * self.egc


GDN_HB = 8
GDN_HG = GDN_HEADS // GDN_HB


def _gdn_specs(n_of):
    C, W = GDN_CHUNK, 128 * GDN_HB
    q_spec = pl.BlockSpec((C, W), lambda g, n: (n_of(n), g))
    k_spec = pl.BlockSpec((C, W), lambda g, n: (n_of(n), g + GDN_HG))
    v_spec = pl.BlockSpec((C, W), lambda g, n: (n_of(n), g + 2 * GDN_HG))
    gb_spec = pl.BlockSpec((C, SM_W), lambda g, n: (n_of(n), 0))
    o_spec = pl.BlockSpec((C, W), lambda g, n: (n_of(n), g))
    s_spec = pl.BlockSpec((GDN_HB, None, GDN_DK, GDN_DV), lambda g, n: (g, n_of(n), 0, 0))
    t_spec = pl.BlockSpec((GDN_HB, None, C, C), lambda g, n: (g, n_of(n), 0, 0))
    return q_spec, k_spec, v_spec, gb_spec, o_spec, s_spec, t_spec


def _gdn_fwd(act, gb, *, name):
    M = act.shape[0]
    N = M // GDN_CHUNK

    def body(q_ref, k_ref, v_ref, gb_ref, o_ref, s_ref, t_ref, state):
        g, n = pl.program_id(0), pl.program_id(1)

        @pl.when(n == 0)
        def _():
            state[...] = jnp.zeros_like(state)

        gb_ = gb_ref[...]

        def head(hh):
            cols = slice(hh * 128, (hh + 1) * 128)
            c = _GdnChunk()
            yield from c.build(q_ref[:, cols], k_ref[:, cols], v_ref[:, cols], gb_, g * GDN_HB + hh, sum_on_mxu=True)
            tinv = yield from _unit_lower_inverse(c.a_low, c.eye)
            s = state[hh]
            s_ref[hh] = s
            t_ref[hh] = tinv
            u = _dot(tinv, c.bu)
            w = _dot(tinv, c.bw)
            yield
            vn = u - _dot(w, s)
            o1 = _dot(c.qd, s)
            yield
            o_ref[:, cols] = (o1 + _dot(c.p, vn)).astype(o_ref.dtype)
            state[hh] = c.gl * s + _dot(c.kd, vn, TN)

        _round_robin(head(hh) for hh in range(GDN_HB))

    q_spec, k_spec, v_spec, gb_spec, o_spec, s_spec, t_spec = _gdn_specs(lambda n: n)
    return pl.pallas_call(
        body, name=name, grid=(GDN_HG, N),
        in_specs=[q_spec, k_spec, v_spec, gb_spec], out_specs=[o_spec, s_spec, t_spec],
        out_shape=[jax.ShapeDtypeStruct((M, GDN_V), BF16),
                   jax.ShapeDtypeStruct((GDN_HEADS, N, GDN_DK, GDN_DV), F32),
                   jax.ShapeDtypeStruct((GDN_HEADS, N, GDN_CHUNK, GDN_CHUNK), F32)],
        scratch_shapes=[pltpu.VMEM((GDN_HB, GDN_DK, GDN_DV), F32)],
        compiler_params=_params("parallel", "arbitrary"),
    )(act, act, act, gb)


def _gdn_bwd(act, gb, do, s_all, t_all, *, name):
    M = act.shape[0]
    N = M // GDN_CHUNK
    C = GDN_CHUNK
    assert GDN_HG == 1

    def body(q_ref, k_ref, v_ref, gb_ref, do_ref, s_ref, t_ref, dact_ref, dgb_ref, dstate):
        g, n = pl.program_id(0), pl.program_id(1)

        @pl.when(n == 0)
        def _():
            dstate[...] = jnp.zeros_like(dstate)

        gb_ = gb_ref[...]
        last = lax.broadcasted_iota(jnp.int32, (C, 1), 0) == C - 1
        lane = lax.broadcasted_iota(jnp.int32, (C, SM_W), 1)
        def head(hh):
            cols = slice(hh * 128, (hh + 1) * 128)
            h = g * GDN_HB + hh
            c = _GdnChunk()
            yield from c.build(q_ref[:, cols], k_ref[:, cols], v_ref[:, cols], gb_, h, sum_on_mxu=False)
            tinv = t_ref[hh]
            tinv_t = tinv.T
            s = s_ref[hh]
            do_ = do_ref[:, cols]
            ds1 = dstate[hh]
            u = _dot(tinv, c.bu)
            w = _dot(tinv, c.bw)
            dqd = _dot(do_, s, NT)
            yield
            dvn0 = _dot(c.p, do_, TN) + _dot(c.kd, ds1)
            dst0 = _dot(c.qd, do_, TN) + c.gl * ds1
            yield
            vn = u - _dot(w, s)
            dvn = dvn0
            yield
            dp = jnp.where(c.causal, _dot(do_, vn, NT), 0.0)
            dstate[hh] = dst0 - _dot(w, dvn, TN)
            dkd = _dot(vn, ds1, NT)
            dw = -_dot(dvn, s, NT)
            dbu = _dot(tinv_t, dvn)
            dgl = jnp.sum(jnp.sum(s * ds1, axis=1, keepdims=True), axis=0, keepdims=True)
            yield
            dbw = _dot(tinv_t, dw)
            t1 = _dot(dbu, u, NT)
            yield
            da = jnp.where(c.strict, -(t1 + _dot(dbw, w, NT)), 0.0)
            dn_ = dp * c.decay
            dq0 = _dot(dn_, c.k)
            dk0 = _dot(dn_, c.q, TN)
            yield
            dm = da * c.decay
            e = da * c.a_low + dp * c.p
            dkb = _dot(dm, c.k) + dbw * c.egc
            dact_ref[:, GDN_QK + hh * 128:GDN_QK + (hh + 1) * 128] = (
                _dot(dm, c.kb, TN) + dk0 + dkb * c.beta + dkd * c.elast)
            dact_ref[:, cols] = dq0 + dqd * c.egc
            dact_ref[:, 2 * GDN_QK + hh * 128:2 * GDN_QK + (hh + 1) * 128] = dbu * c.beta
            dbeta = jnp.sum(dbu * c.v, axis=1, keepdims=True) + jnp.sum(dkb * c.k, axis=1, keepdims=True)
            t_kd = jnp.sum(dkd * c.kd, axis=1, keepdims=True)
            dgc = (jnp.sum(e, axis=1, keepdims=True) - jnp.sum(e.T, axis=1, keepdims=True)
                   + jnp.sum(dbw * c.bw, axis=1, keepdims=True) + jnp.sum(dqd * c.qd, axis=1, keepdims=True) - t_kd)
            dgc = dgc + jnp.where(last, jnp.sum(t_kd, axis=0, keepdims=True) + dgl * c.gl, 0.0)
            yield
            dg = _running_sum(jnp.broadcast_to(dgc, (C, SM_W)), reverse=True)
            dgb_ref[hh] = jnp.where(lane == h, dg, jnp.where(lane == h + GDN_HEADS, dbeta, 0.0))

        _round_robin(head(hh) for hh in range(GDN_HB))

    rev = lambda n: N - 1 - n
    q_spec, k_spec, v_spec, gb_spec, o_spec, s_spec, t_spec = _gdn_specs(rev)
    dgb_spec = pl.BlockSpec((GDN_HB, C, SM_W), lambda g, n: (g, rev(n), 0))
    return pl.pallas_call(
        body, name=name, grid=(GDN_HG, N),
        in_specs=[q_spec, k_spec, v_spec, gb_spec, o_spec, s_spec, t_spec],
        out_specs=[pl.BlockSpec((C, 2 * GDN_QK + GDN_V), lambda g, n: (rev(n), 0)), dgb_spec],
        out_shape=[jax.ShapeDtypeStruct((M, 2 * GDN_QK + GDN_V), F32),
                   jax.ShapeDtypeStruct((GDN_HEADS, M, SM_W), F32)],
        scratch_shapes=[pltpu.VMEM((GDN_HB, GDN_DK, GDN_DV), F32)],
        compiler_params=_params("parallel", "arbitrary"),
    )(act, act, act, gb, do, s_all, t_all)


GLA_STEP_ROWS = 64
GLA_SUB = GLA_STEP_ROWS // GLA_CHUNK


def _gla_cumsum(la):
    return _running_sum(la)


GLA_HALF = GLA_CHUNK // 2


def _gla_cross_factors(b):
    top = lax.broadcasted_iota(jnp.int32, b.shape, 0) < GLA_HALF
    bm = b[GLA_HALF - 1:GLA_HALF, :]
    late = jnp.where(top, 0.0, jnp.exp(jnp.minimum(b - bm, 0.0)))
    early = jnp.where(top, jnp.exp(jnp.minimum(bm - b, 0.0)), 0.0)
    return late, early


def _gla_half_decay(bh, ii):
    rj = lax.broadcasted_iota(jnp.int32, bh.shape, 0)
    return jnp.where(rj <= ii, jnp.exp(jnp.minimum(bh[ii:ii + 1, :] - bh, 0.0)), 0.0)


def _gla_scores_t(q, k, b):
    C, H = GLA_CHUNK, GLA_HALF
    lane = lax.broadcasted_iota(jnp.int32, (H, C), 1)
    halves = []
    for h0 in (0, H):
        qh, kh, bh = q[h0:h0 + H], k[h0:h0 + H], b[h0:h0 + H]
        sth = jnp.zeros((H, C), F32)
        for ii in range(H):
            si = jnp.sum(qh[ii:ii + 1, :] * kh * _gla_half_decay(bh, ii), axis=1, keepdims=True)
            sth = jnp.where(lane == h0 + ii, si, sth)
            if ii % 4 == 3:
                yield
        halves.append(sth)
    late, early = _gla_cross_factors(b)
    between = _dot(k * early, q * late, NT)
    yield
    return jnp.concatenate(halves, axis=0) + between


def _gla_specs(n_of):
    R = GLA_STEP_ROWS
    q_spec = pl.BlockSpec((R, GLA_QK), lambda n: (n_of(n), C_GQ // GLA_QK))
    k_spec = pl.BlockSpec((R, GLA_QK), lambda n: (n_of(n), C_GK // GLA_QK))
    v_spec = pl.BlockSpec((R, GLA_V), lambda n: (n_of(n), C_GV // GLA_V))
    la_spec = pl.BlockSpec((R, GLA_QK), lambda n: (n_of(n), 0))
    o_spec = pl.BlockSpec((R, GLA_V), lambda n: (n_of(n), 0))
    s_spec = pl.BlockSpec((GLA_HEADS, None, GLA_SUB, GLA_DV, GLA_DK), lambda n: (0, n_of(n), 0, 0, 0))
    return q_spec, k_spec, v_spec, la_spec, o_spec, s_spec


def _gla_fwd(proj, la, *, name):
    M = proj.shape[0]
    N = M // GLA_STEP_ROWS
    C = GLA_CHUNK

    def body(q_ref, k_ref, v_ref, la_ref, o_ref, s_ref, state):
        n = pl.program_id(0)

        @pl.when(n == 0)
        def _():
            state[...] = jnp.zeros_like(state)

        local = {}

        def within(hh, c):
            kc = slice(hh * GLA_DK, (hh + 1) * GLA_DK)
            vc = slice(hh * GLA_DV, (hh + 1) * GLA_DV)
            rows = slice(c * C, (c + 1) * C)
            q = q_ref[rows, kc].astype(F32) * (GLA_DK ** -0.5)
            k = k_ref[rows, kc].astype(F32)
            v = v_ref[rows, vc].astype(F32)
            b = _gla_cumsum(la_ref[rows, kc])
            yield
            blast = b[C - 1:C, :]
            sc_t = yield from _gla_scores_t(q, k, b)
            kv = _dot(v, k * jnp.exp(blast - b), TN)
            o2 = _dot(sc_t, v, TN)
            yield
            local[hh, c] = (q * jnp.exp(b), jnp.exp(blast), kv, o2)

        def across(hh):
            vc = slice(hh * GLA_DV, (hh + 1) * GLA_DV)
            st = state[hh]
            for c in range(GLA_SUB):
                qe, eblast, kv, o2 = local[hh, c]
                s_ref[hh, c] = st
                o1 = _dot(qe, st, NT)
                yield
                o_ref[c * C:(c + 1) * C, vc] = (o1 + o2).astype(o_ref.dtype)
                st = st * eblast + kv
            state[hh] = st

        _round_robin(within(hh, c) for c in range(GLA_SUB) for hh in range(GLA_HEADS))
        _round_robin(across(hh) for hh in range(GLA_HEADS))

    q_spec, k_spec, v_spec, la_spec, o_spec, s_spec = _gla_specs(lambda n: n)
    return pl.pallas_call(
        body, name=name, grid=(N,),
        in_specs=[q_spec, k_spec, v_spec, la_spec], out_specs=[o_spec, s_spec],
        out_shape=[jax.ShapeDtypeStruct((M, GLA_V), BF16),
                   jax.ShapeDtypeStruct((GLA_HEADS, N, GLA_SUB, GLA_DV, GLA_DK), F32)],
        scratch_shapes=[pltpu.VMEM((GLA_HEADS, GLA_DV, GLA_DK), F32)],
        compiler_params=_params("arbitrary"),
    )(proj, proj, proj, la)


def _gla_bwd(proj, la, do, s_all, d_proj, *, name):
    M = proj.shape[0]
    N = M // GLA_STEP_ROWS
    C = GLA_CHUNK
    qkv_w = 2 * GLA_QK + GLA_V
    assert C_GK == C_GQ + GLA_QK and C_GV == C_GK + GLA_QK and C_GQ % qkv_w == 0

    def body(q_ref, k_ref, v_ref, la_ref, do_ref, s_ref, _, dp_ref, dla_ref, dstate):
        n = pl.program_id(0)

        @pl.when(n == 0)
        def _():
            dstate[...] = jnp.zeros_like(dstate)

        H = GLA_HALF
        lane = lax.broadcasted_iota(jnp.int32, (C, C), 1)
        row = lax.broadcasted_iota(jnp.int32, (C, C), 0)
        ri = lax.broadcasted_iota(jnp.int32, (C, GLA_DK), 0)
        lane_h = lax.broadcasted_iota(jnp.int32, (H, C), 1)
        ri_h = lax.broadcasted_iota(jnp.int32, (H, GLA_DK), 0)
        cross = (row < H) & (lane >= H)
        def head(hh):
            kc = slice(hh * GLA_DK, (hh + 1) * GLA_DK)
            vc = slice(hh * GLA_DV, (hh + 1) * GLA_DV)
            ds1 = dstate[hh]
            for c in reversed(range(GLA_SUB)):
                rows = slice(c * C, (c + 1) * C)
                q = q_ref[rows, kc].astype(F32) * (GLA_DK ** -0.5)
                k = k_ref[rows, kc].astype(F32)
                v = v_ref[rows, vc].astype(F32)
                b = _gla_cumsum(la_ref[rows, kc])
                do_ = do_ref[rows, vc]
                st = s_ref[hh, c]
                dsc_t = _dot(v, do_, NT)
                dqe = _dot(do_, st)
                dke = _dot(v, ds1)
                yield
                blast = b[C - 1:C, :]
                eb = jnp.exp(b)
                elast = jnp.exp(blast - b)
                eblast = jnp.exp(blast)
                qe = q * eb
                ke = k * elast
                dv2 = _dot(ke, ds1, NT)
                ds_new = _dot(do_, qe, TN)
                deblast = jnp.sum(st * ds1, axis=0, keepdims=True)
                sc_halves, dq_halves, dk_halves = [], [], []
                for h0 in (0, H):
                    qh, kh, bh, dsch = q[h0:h0 + H], k[h0:h0 + H], b[h0:h0 + H], dsc_t[h0:h0 + H]
                    sch = jnp.zeros((H, C), F32)
                    dqh = jnp.zeros((H, GLA_DK), F32)
                    dkh = jnp.zeros((H, GLA_DK), F32)
                    for ii in range(H):
                        f = _gla_half_decay(bh, ii)
                        kf = kh * f
                        si = jnp.sum(qh[ii:ii + 1, :] * kf, axis=1, keepdims=True)
                        sch = jnp.where(lane_h == h0 + ii, si, sch)
                        dsi = jnp.sum(jnp.where(lane_h == h0 + ii, dsch, 0.0), axis=1, keepdims=True)
                        dqh = jnp.where(ri_h == ii, jnp.sum(dsi * kf, axis=0, keepdims=True), dqh)
                        dkh = dkh + (dsi * f) * qh[ii:ii + 1, :]
                        if ii % 4 == 3:
                            yield
                    sc_halves.append(sch)
                    dq_halves.append(dqh)
                    dk_halves.append(dkh)
                late, early = _gla_cross_factors(b)
                q_late, k_early = q * late, k * early
                dsc_x = jnp.where(cross, dsc_t, 0.0)
                sc_t = jnp.concatenate(sc_halves, axis=0) + _dot(k_early, q_late, NT)
                dq_sc = jnp.concatenate(dq_halves, axis=0) + _dot(dsc_x, k_early, TN) * late
                dk_sc = jnp.concatenate(dk_halves, axis=0) + _dot(dsc_x, q_late) * early
                yield
                dv1 = _dot(sc_t, do_)
                dp_ref[rows, kc] = ((dq_sc + dqe * eb) * (GLA_DK ** -0.5)).astype(dp_ref.dtype)
                dp_ref[rows, GLA_QK + hh * GLA_DK:GLA_QK + (hh + 1) * GLA_DK] = (dk_sc + dke * elast).astype(dp_ref.dtype)
                t_ke = dke * ke
                db = q * dq_sc - k * dk_sc + dqe * qe - t_ke
                db = db + jnp.where(ri == C - 1, jnp.sum(t_ke, axis=0, keepdims=True) + deblast * eblast, 0.0)
                dla = _running_sum(db, reverse=True)
                yield
                dp_ref[rows, 2 * GLA_QK + hh * GLA_DV:2 * GLA_QK + (hh + 1) * GLA_DV] = (dv1 + dv2).astype(dp_ref.dtype)
                dla_ref[rows, kc] = dla
                ds1 = ds1 * eblast + ds_new
            dstate[hh] = ds1

        _round_robin(head(hh) for hh in range(GLA_HEADS))

    rev = lambda n: N - 1 - n
    q_spec, k_spec, v_spec, la_spec, o_spec, s_spec = _gla_specs(rev)
    return pl.pallas_call(
        body, name=name, grid=(N,),
        in_specs=[q_spec, k_spec, v_spec, la_spec, o_spec, s_spec, _ANY],
        out_specs=[pl.BlockSpec((GLA_STEP_ROWS, qkv_w), lambda n: (rev(n), C_GQ // qkv_w)), la_spec],
        out_shape=[jax.ShapeDtypeStruct(d_proj.shape, d_proj.dtype), jax.ShapeDtypeStruct((M, GLA_QK), F32)],
        input_output_aliases={6: 0},
        scratch_shapes=[pltpu.VMEM((GLA_HEADS, GLA_DV, GLA_DK), F32)],
        compiler_params=_params("arbitrary"),
    )(proj, proj, proj, la, do, s_all, d_proj)


def _head_norm(o, wn):
    r = lax.rsqrt(jnp.mean(o * o, axis=-1, keepdims=True) + NORM_EPS)
    return o * r, r


def _mix_heads():
    heads = [(0, GDN_DV, hh * GDN_DV, hh * GDN_DV) for hh in range(GDN_HEADS)]
    heads += [(1, GLA_DV, GDN_V + hh * GLA_DV, hh * GLA_DV) for hh in range(GLA_HEADS)]
    return heads


def _mix_fwd(o_gdn, o_gla, proj, wn_gdn, wn_gla, *, name):
    M = proj.shape[0]
    tm = _tile(M, 344, 16)

    def body(og_ref, ol_ref, z_ref, r_ref, wg_ref, wl_ref, m_ref):
        srcs = ((og_ref, z_ref, wg_ref), (ol_ref, r_ref, wl_ref))
        for grp, width, mcol, col in _mix_heads():
            o_ref, gate_ref, w_ref = srcs[grp]
            xhat, _ = _head_norm(o_ref[:, col:col + width].astype(F32), None)
            gate, _ = _silu_and_grad(gate_ref[:, col:col + width].astype(F32))
            m_ref[:, mcol:mcol + width] = (xhat * w_ref[...] * gate).astype(m_ref.dtype)

    full = lambda s: pl.BlockSpec(s, lambda i: (0, 0))
    return pl.pallas_call(
        body, name=name, grid=(M // tm,),
        in_specs=[pl.BlockSpec((tm, GDN_V), lambda i: (i, 0)), pl.BlockSpec((tm, GLA_V), lambda i: (i, 0)),
                  pl.BlockSpec((tm, GDN_V), lambda i: (i, C_Z // GDN_V)),
                  pl.BlockSpec((tm, GLA_V), lambda i: (i, C_GR // GLA_V)),
                  full((1, GDN_DV)), full((1, GLA_DV))],
        out_specs=pl.BlockSpec((tm, D_MODEL), lambda i: (i, 0)),
        out_shape=jax.ShapeDtypeStruct((M, D_MODEL), BF16),
        compiler_params=_params("parallel"),
    )(o_gdn, o_gla, proj, proj, wn_gdn, wn_gla)


def _mix_bwd(o_gdn, o_gla, proj, wn_gdn, wn_gla, dmixed, *, name):
    M = proj.shape[0]
    tm = _tile(M, 344, 16)
    g_ = M // tm
    assert C_Z == 0 and C_GR == GDN_V

    def body(og_ref, ol_ref, z_ref, r_ref, wg_ref, wl_ref, dm_ref,
             dog_ref, dol_ref, dzr_ref, dwg_ref, dwl_ref):
        i = pl.program_id(0)
        srcs = ((og_ref, z_ref, wg_ref, dog_ref), (ol_ref, r_ref, wl_ref, dol_ref))
        dws = [jnp.zeros((1, GDN_DV), F32), jnp.zeros((1, GLA_DV), F32)]
        for grp, width, mcol, col in _mix_heads():
            o_ref, gate_ref, w_ref, do_ref = srcs[grp]
            cols = slice(col, col + width)
            xhat, r = _head_norm(o_ref[:, cols].astype(F32), None)
            gate, dgate_dc = _silu_and_grad(gate_ref[:, cols].astype(F32))
            dm = dm_ref[:, mcol:mcol + width]
            dzr_ref[:, mcol:mcol + width] = (dm * xhat * w_ref[...] * dgate_dc).astype(dzr_ref.dtype)
            dnorm = dm * gate
            dws[grp] = dws[grp] + jnp.sum(dnorm * xhat, axis=0, keepdims=True)
            dxhat = dnorm * w_ref[...]
            do_ref[:, cols] = r * (dxhat - xhat * jnp.mean(dxhat * xhat, axis=-1, keepdims=True))

        @pl.when(i == 0)
        def _():
            dwg_ref[...] = dws[0]
            dwl_ref[...] = dws[1]

        @pl.when(i > 0)
        def _():
            dwg_ref[...] += dws[0]
            dwl_ref[...] += dws[1]

    full = lambda s: pl.BlockSpec(s, lambda i: (0, 0))
    half = pl.BlockSpec((tm, GDN_V), lambda i: (i, 0))
    return pl.pallas_call(
        body, name=name, grid=(g_,),
        in_specs=[half, half, pl.BlockSpec((tm, GDN_V), lambda i: (i, C_Z // GDN_V)),
                  pl.BlockSpec((tm, GLA_V), lambda i: (i, C_GR // GLA_V)),
                  full((1, GDN_DV)), full((1, GLA_DV)), pl.BlockSpec((tm, D_MODEL), lambda i: (i, 0))],
        out_specs=[half, half, pl.BlockSpec((tm, GDN_V + GLA_V), lambda i: (i, 0)),
                   full((1, GDN_DV)), full((1, GLA_DV))],
        out_shape=[jax.ShapeDtypeStruct((M, GDN_V), F32), jax.ShapeDtypeStruct((M, GLA_V), F32),
                   jax.ShapeDtypeStruct((M, D_PROJ), BF16),
                   jax.ShapeDtypeStruct((1, GDN_DV), F32), jax.ShapeDtypeStruct((1, GLA_DV), F32)],
        compiler_params=_params("arbitrary"),
    )(o_gdn, o_gla, proj, proj, wn_gdn, wn_gla, dmixed)


def _row_chunks(tm, parts=2):
    if tm % (16 * parts):
        return [slice(0, tm)]
    return [slice(p * (tm // parts), (p + 1) * (tm // parts)) for p in range(parts)]


def _swiglu_fwd(n, w_gate_t, w_up_t, *, name, tm=1376, tn=512):
    M, D = n.shape
    F = w_gate_t.shape[0]
    tm, tn = _tile(M, tm, 16), _tile(F, tn, 128)

    def body(n_ref, wg_ref, wu_ref, g_ref, u_ref, a_ref):
        wg, wu = wg_ref[...], wu_ref[...]
        for rows in _row_chunks(tm):
            x = n_ref[rows, :]
            g = _dot(x, wg, NT)
            u = _dot(x, wu, NT)
            s, _ = _silu_and_grad(g)
            g_ref[rows, :] = g.astype(g_ref.dtype)
            u_ref[rows, :] = u.astype(u_ref.dtype)
            a_ref[rows, :] = (s * u).astype(a_ref.dtype)

    w_spec = pl.BlockSpec((tn, D), lambda i, j: (j, 0))
    o_spec = pl.BlockSpec((tm, tn), lambda i, j: (i, j))
    return pl.pallas_call(
        body, name=name, grid=(M // tm, F // tn),
        in_specs=[pl.BlockSpec((tm, D), lambda i, j: (i, 0)), w_spec, w_spec], out_specs=[o_spec] * 3,
        out_shape=[jax.ShapeDtypeStruct((M, F), BF16)] * 3, compiler_params=_params("parallel", "parallel"),
    )(n, w_gate_t, w_up_t)


def _swiglu_bwd(dh, w_down, gate, up, *, name, after=None, tm=1376, tn=512):
    M, D = dh.shape
    F = w_down.shape[0]
    tm, tn = _tile(M, tm, 16), _tile(F, tn, 128)
    n_after = 0 if after is None else 1

    def body(*refs):
        dh_ref, w_ref, g_ref, u_ref, dg_ref, du_ref = refs[n_after:]
        w = w_ref[...]
        for rows in _row_chunks(tm):
            da = _dot(dh_ref[rows, :], w, NT)
            s, ds = _silu_and_grad(g_ref[rows, :].astype(F32))
            dg_ref[rows, :] = (da * u_ref[rows, :].astype(F32) * ds).astype(dg_ref.dtype)
            du_ref[rows, :] = (da * s).astype(du_ref.dtype)

    o_spec = pl.BlockSpec((tm, tn), lambda i, j: (i, j))
    return pl.pallas_call(
        body, name=name, grid=(M // tm, F // tn),
        in_specs=[_ANY] * n_after + [pl.BlockSpec((tm, D), lambda i, j: (i, 0)),
                                     pl.BlockSpec((tn, D), lambda i, j: (j, 0)), o_spec, o_spec],
        out_specs=[o_spec, o_spec], out_shape=[jax.ShapeDtypeStruct((M, F), BF16)] * 2,
        compiler_params=_params("parallel", "parallel"),
    )(*((after,) if n_after else ()), dh, w_down, gate, up)


def _adamw_update(w, g, m, v):
    nm = ADAM_B1 * m + (1.0 - ADAM_B1) * g
    nv = ADAM_B2 * v + (1.0 - ADAM_B2) * (g * g)
    m_hat = nm / (1.0 - ADAM_B1 ** ADAM_STEP)
    v_hat = nv / (1.0 - ADAM_B2 ** ADAM_STEP)
    return -ADAM_LR * (m_hat / (jnp.sqrt(v_hat) + ADAM_EPS) + ADAM_WD * w), nm, nv


def _adamw(w, g, m, v, *, name):
    shape = w.shape
    cols = shape[-1]
    rows = w.size // cols
    w2, g2, m2, v2 = (t.reshape(rows, cols) for t in (w, g, m, v))
    if rows % 8 == 0 or cols % 128 != 0:
        tr, tc = (_tile(rows, 256, 8) if rows % 8 == 0 else rows), cols
    else:
        tr, tc = rows, _tile(cols, 256, 128)

    def body(w_ref, g_ref, m_ref, v_ref, d_ref, nm_ref, nv_ref):
        d_ref[...], nm_ref[...], nv_ref[...] = _adamw_update(w_ref[...], g_ref[...], m_ref[...], v_ref[...])

    blk = pl.BlockSpec((tr, tc), lambda i, j: (i, j))
    outs = pl.pallas_call(
        body, name=name, grid=(rows // tr, cols // tc), in_specs=[blk] * 4, out_specs=[blk] * 3,
        out_shape=[jax.ShapeDtypeStruct((rows, cols), F32)] * 3, compiler_params=_params("parallel", "parallel"),
    )(w2, g2, m2, v2)
    return tuple(t.reshape(shape) for t in outs)


def _sum_slabs(x, *, name):
    _, R, C = x.shape
    sub = 16 if x.dtype == BF16 else 8
    if R % sub == 0:
        tr, tc = _tile(R, 128, sub), C
    else:
        tr, tc = R, _tile(C, 256, 128)

    def body(x_ref, o_ref):
        acc = x_ref[0].astype(F32)
        for s in range(1, N_DEV):
            acc = acc + x_ref[s].astype(F32)
        o_ref[...] = acc

    return pl.pallas_call(
        body, name=name, grid=(R // tr, C // tc),
        in_specs=[pl.BlockSpec((N_DEV, tr, tc), lambda i, j: (0, i, j))],
        out_specs=pl.BlockSpec((tr, tc), lambda i, j: (i, j)),
        out_shape=jax.ShapeDtypeStruct((R, C), F32), compiler_params=_params("parallel", "parallel"),
    )(x)


def _sum_adamw(x, w, m, v, *, name):
    _, R, C = x.shape
    if R % 16 == 0:
        tr, tc = _tile(R, 128, 16), C
    else:
        tr, tc = R, _tile(C, 256, 128)

    def body(x_ref, w_ref, m_ref, v_ref, g_ref, d_ref, nm_ref, nv_ref):
        g = x_ref[0].astype(F32)
        for s in range(1, N_DEV):
            g = g + x_ref[s].astype(F32)
        g_ref[...] = g
        d_ref[...], nm_ref[...], nv_ref[...] = _adamw_update(w_ref[...], g, m_ref[...], v_ref[...])

    blk = pl.BlockSpec((tr, tc), lambda i, j: (i, j))
    return pl.pallas_call(
        body, name=name, grid=(R // tr, C // tc),
        in_specs=[pl.BlockSpec((N_DEV, tr, tc), lambda i, j: (0, i, j)), blk, blk, blk], out_specs=[blk] * 4,
        out_shape=[jax.ShapeDtypeStruct((R, C), F32)] * 4, compiler_params=_params("parallel", "parallel"),
    )(x, w, m, v)


def _peers():
    x, y, c = lax.axis_index("x"), lax.axis_index("y"), lax.axis_index("c")
    me = 4 * x + 2 * y + c
    peers = []
    for k in range(1, N_DEV):
        px = 1 - x if k & 4 else x
        py = 1 - y if k & 2 else y
        pc = 1 - c if k & 1 else c
        peers.append(((px, py, pc), 4 * px + 2 * py + pc))
    return me, peers


_HBM = pl.BlockSpec(memory_space=pltpu.HBM)
_SEM = pl.BlockSpec(memory_space=pltpu.SEMAPHORE)
_EFFECT = pltpu.SideEffectType.DATAFLOW_SIDE_EFFECTING


PLAN_GATHER = tuple((k, "x", 0) for k in range(1, N_DEV))
PLAN_SCATTER = tuple((k, "xk", 0) for k in range(1, N_DEV))
PLAN_GATHER_CHIPS = tuple((k, "x", 0) for k in (1, 2, 4, 6))
PLAN_GATHER_PASS_ON = tuple((1, ("land", q), q) for q in (2, 4, 6))


def _plan_refs(plan, j, x_ref, land_ref, me, peers, receiving):
    k, source, r = plan[j]
    index_of = lambda q: me if q == 0 else peers[q - 1][1]
    pos, target = peers[k - 1]
    if source == "x":
        src = x_ref
    elif source == "xk":
        src = x_ref.at[target]
    else:
        src = land_ref.at[index_of(source[1])]
    return pos, src, land_ref.at[index_of(k ^ r) if receiving else index_of(r)]


def _exchange_start(x, *, plan, name, after=None, land=None, slab=None):
    n_after = 0 if after is None else 1
    n = len(plan)

    def body(*refs):
        x_ref, land_ref, send_sems, recv_sems, _, _, token = refs[n_after:]
        me, peers = _peers()
        for j in range(n):
            pos, src, dst = _plan_refs(plan, j, x_ref, land_ref, me, peers, receiving=False)
            pltpu.make_async_remote_copy(src_ref=src, dst_ref=dst, send_sem=send_sems.at[j], recv_sem=recv_sems.at[j],
                                         device_id=pos, device_id_type=pl.DeviceIdType.MESH).start()
        token[...] = jnp.zeros_like(token)

    if land is None:
        land = lax.empty((N_DEV,) + tuple(slab), x.dtype)
    return pl.pallas_call(
        body, name=name,
        out_shape=(pltpu.SemaphoreType.DMA((n,)), pltpu.SemaphoreType.DMA((n,)),
                   pltpu.HBM(x.shape, x.dtype), pltpu.HBM(land.shape, land.dtype), jax.ShapeDtypeStruct((8, 128), F32)),
        in_specs=[_ANY] * n_after + [_HBM, _HBM],
        out_specs=(_SEM, _SEM, _HBM, _HBM, pl.BlockSpec(memory_space=pltpu.VMEM)),
        input_output_aliases={n_after: 2, n_after + 1: 3},
        compiler_params=pltpu.CompilerParams(has_side_effects=_EFFECT),
    )(*((after,) if n_after else ()), pltpu.with_memory_space_constraint(x, pltpu.HBM),
      pltpu.with_memory_space_constraint(land, pltpu.HBM))


def _exchange_wait(handle, after, *, plan, name):
    send_sems, recv_sems, x_thru, land_thru, _ = handle
    afters = list(after) if isinstance(after, (list, tuple)) else [after]

    def body(x_ref, land_ref, send_sems, recv_sems, *rest):
        me, peers = _peers()
        for j in range(len(plan)):
            pos, src, dst = _plan_refs(plan, j, x_ref, land_ref, me, peers, receiving=True)
            cp = pltpu.make_async_remote_copy(src_ref=src, dst_ref=dst, send_sem=send_sems.at[j], recv_sem=recv_sems.at[j],
                                              device_id=pos, device_id_type=pl.DeviceIdType.MESH)
            cp.wait_send()
            cp.wait_recv()

    return pl.pallas_call(
        body, name=name,
        out_shape=(pltpu.HBM(x_thru.shape, x_thru.dtype), pltpu.HBM(land_thru.shape, land_thru.dtype)),
        in_specs=[_HBM, _HBM, _SEM, _SEM] + [_ANY] * len(afters), out_specs=(_HBM, _HBM),
        input_output_aliases={0: 0, 1: 1}, compiler_params=pltpu.CompilerParams(has_side_effects=_EFFECT),
    )(x_thru, land_thru, send_sems, recv_sems, *afters)


def _to_proj_rows(t):
    z = jnp.zeros((D_PROJ - C_SM - 2 * GDN_HEADS - GLA_RANK,) + t.shape[1:], t.dtype)
    return jnp.concatenate([t[R_Z:R_A], t[R_GR:R_LR], t[R_GQ:R_GR], t[:R_Z], t[R_A:R_GQ], t[R_LR:], z], axis=0)


def _from_proj_rows(t):
    ab = C_SM + 2 * GDN_HEADS
    return jnp.concatenate([t[C_QKV:C_SM], t[C_Z:C_GR], t[C_SM:ab], t[C_GQ:C_QKV], t[C_GR:C_GQ],
                            t[ab:ab + GLA_RANK]], axis=0)


def _local_step(x, target, meta, attn_nw, conv_w, a_log, dt_bias, gdn_nw, w2, b2, gla_nw, ffn_nw, final_nw,
                fetch, emit, start=None):
    head = jnp.concatenate([jnp.zeros((ROW_PAD, D_MODEL), F32), meta], axis=0)
    conv_w8 = jnp.concatenate([conv_w, jnp.zeros((8 - CONV_K, conv_w.shape[1]), F32)], axis=0)
    w2p = jnp.zeros((SM_W, GLA_QK), F32).at[2 * GDN_HEADS:2 * GDN_HEADS + GLA_RANK].set(w2)
    alog_p = jnp.zeros((1, SM_W), F32).at[:, :GDN_HEADS].set(a_log)
    dt_p = jnp.zeros((1, SM_W), F32).at[:, :GDN_HEADS].set(dt_bias)

    h0, n1 = _embed_norm(head, x, attn_nw, name="attn_norm", after=start)
    w_in_t = fetch("w_in_t", (n1, conv_w8, w2p, alog_p, dt_p))
    proj, sm = _in_proj(n1, w_in_t, name="in_proj")
    gb, la = _gates_fwd(sm, w2p, b2, alog_p, dt_p, name="gates")
    act = _prep_fwd(proj, conv_w8, name="gdn_prep")
    o_gdn, s_gdn, t_gdn = _gdn_fwd(act, gb, name="gdn_fwd")
    o_gla, s_gla = _gla_fwd(proj, la, name="gla_fwd")
    mixed = _mix_fwd(o_gdn, o_gla, proj, gdn_nw, gla_nw, name="mix")
    w_out = fetch("w_out", mixed)
    h1 = _matmul(mixed, w_out, mode="nn", add=h0, name="out_proj")
    n2 = _rmsnorm_fwd(h1, ffn_nw, name="ffn_norm")
    w_gate_t, w_up_t = fetch("w_gate_t", n2), fetch("w_up_t", n2)
    gate, up, hid = _swiglu_fwd(n2, w_gate_t, w_up_t, name="swiglu")
    w_down = fetch("w_down", hid)
    h2 = _matmul(hid, w_down, mode="nn", add=h1, name="ffn_down", tm=1376, tn=256)
    dh2, dh2_b, d_final_nw, loss = _loss_head(h2, final_nw, target, name="loss_head")

    wg = dict(mode="tn", out_dtype=BF16, tn=512)
    tok = emit("w_down", _matmul(hid, dh2_b, name="d_w_down", tm=704, **wg))
    d_gate, d_up = _swiglu_bwd(dh2_b, w_down, gate, up, name="d_swiglu", after=tok)
    tok = emit("w_gate_t", _matmul(d_gate, n2, name="d_w_gate", tm=704, **wg))
    tok = emit("w_up_t", _matmul(d_up, n2, name="d_w_up", tm=704, after=tok, **wg))
    d_n2 = _matmul_pair(d_gate, w_gate_t, d_up, w_up_t, name="d_n2", after=tok)
    dh1, dh1_b, d_ffn_nw = _rmsnorm_bwd(h1, ffn_nw, d_n2, dh2, name="d_ffn_norm")

    tok = emit("w_out", _matmul(mixed, dh1_b, name="d_w_out", tm=512, **wg))
    d_mixed = _matmul(dh1_b, w_out, mode="nt", name="d_mixed", after=tok)
    do_gdn, do_gla, d_proj, d_gdn_nw, d_gla_nw = _mix_bwd(o_gdn, o_gla, proj, gdn_nw, gla_nw, d_mixed, name="d_mix")
    d_proj, d_la = _gla_bwd(proj, la, do_gla, s_gla, d_proj, name="gla_bwd")
    dact, dgb_heads = _gdn_bwd(act, gb, do_gdn, s_gdn, t_gdn, name="gdn_bwd")
    d_proj, d_w2p, d_b2, d_alog, d_dt = _gates_bwd(sm, w2p, b2, alog_p, dt_p, dgb_heads, d_la, d_proj, name="d_gates")
    d_proj, d_conv_w8 = _prep_bwd(proj, conv_w8, dact, d_proj, name="d_gdn_prep")
    tok = emit("w_in_t", _matmul(d_proj, n1, name="d_w_in", tm=768, **wg))
    d_n1 = _matmul(d_proj, w_in_t, mode="nn", name="d_n1", tm=688, after=tok)
    grad_x, d_head, d_attn_nw = _embed_norm_bwd(h0, attn_nw, d_n1, dh1, name="d_attn_norm")

    return dict(
        loss=loss[0, 0], grad_x=grad_x, meta=d_head[ROW_PAD:HEAD_ROWS], attn_nw=d_attn_nw,
        conv_w=d_conv_w8[:CONV_K], a_log=d_alog[:, :GDN_HEADS], dt_bias=d_dt[:, :GDN_HEADS], gdn_nw=d_gdn_nw,
        w2=d_w2p[2 * GDN_HEADS:2 * GDN_HEADS + GLA_RANK], b2=d_b2, gla_nw=d_gla_nw, ffn_nw=d_ffn_nw,
        final_nw=d_final_nw)


SMALL_ROWS = 32


def kernel(x, meta_tokens, attn_norm_w, w_in, gdn_conv_w, gdn_a_log, gdn_dt_bias, gdn_norm_w, gla_gate_w2, gla_gate_b, gla_norm_w, w_out, ffn_norm_w, w_gate, w_up, w_down, final_norm_w, loss_target, m_meta_tokens, m_attn_norm_w, m_w_in, m_gdn_conv_w, m_gdn_a_log, m_gdn_dt_bias, m_gdn_norm_w, m_gla_gate_w2, m_gla_gate_b, m_gla_norm_w, m_w_out, m_ffn_norm_w, m_w_gate, m_w_up, m_w_down, m_final_norm_w, v_meta_tokens, v_attn_norm_w, v_w_in, v_gdn_conv_w, v_gdn_a_log, v_gdn_dt_bias, v_gdn_norm_w, v_gla_gate_w2, v_gla_gate_b, v_gla_norm_w, v_w_out, v_ffn_norm_w, v_w_gate, v_w_up, v_w_down, v_final_norm_w):
    me = 4 * lax.axis_index("x") + 2 * lax.axis_index("y") + lax.axis_index("c")

    n_conv = gdn_conv_w.shape[2]
    n_w2 = gla_gate_w2.shape[2]
    n_meta = meta_tokens.shape[1]
    small = jnp.zeros((40, n_conv), F32)
    small = small.at[0:N_META, :n_meta].set(meta_tokens)
    small = small.at[N_META:N_META + CONV_K, :].set(gdn_conv_w[0])
    small = small.at[24:24 + GLA_RANK, :n_w2].set(gla_gate_w2[0])
    small_h = _exchange_start(small, plan=PLAN_GATHER, slab=small.shape, name="gather_small_start")

    w_in_slab = w_in[0].T.astype(BF16)
    in_h = _exchange_start(w_in_slab, plan=PLAN_GATHER_CHIPS, slab=w_in_slab.shape, name="gather_w_in_start",
                           after=small_h[4])
    handles, tok = {}, in_h[4]
    for wname, slab in (("w_out", w_out[0]), ("w_gate_t", w_gate[0].T), ("w_up_t", w_up[0].T), ("w_down", w_down[0])):
        slab = slab.astype(BF16)
        handles[wname] = _exchange_start(slab, plan=PLAN_GATHER, slab=slab.shape, name="gather_" + wname + "_start", after=tok)
        tok = handles[wname][4]

    own, small_all = _exchange_wait(small_h, tok, plan=PLAN_GATHER, name="gather_small_wait")
    small_all = lax.dynamic_update_index_in_dim(small_all, own, me, 0)
    meta_f = small_all[:, 0:N_META, :n_meta].transpose(1, 0, 2).reshape(N_META, D_MODEL)
    conv_f = small_all[:, N_META:N_META + CONV_K, :].transpose(1, 0, 2).reshape(CONV_K, N_DEV * n_conv)
    w2_f = small_all[:, 24:24 + GLA_RANK, :n_w2].transpose(1, 0, 2).reshape(GLA_RANK, N_DEV * n_w2)

    def fetch(name, after):
        if name == "w_in_t":
            own, got = _exchange_wait(in_h, after, plan=PLAN_GATHER_CHIPS, name="gather_w_in_wait")
            pass_h = _exchange_start(own, plan=PLAN_GATHER_PASS_ON, land=got, name="pass_w_in_start")
            own, got = _exchange_wait(pass_h, pass_h[4], plan=PLAN_GATHER_PASS_ON, name="pass_w_in_wait")
            got = lax.dynamic_update_index_in_dim(got, own, me, 0)
            return _to_proj_rows(got.reshape(D_IN, D_MODEL))
        own, got = _exchange_wait(handles[name], after, plan=PLAN_GATHER, name="gather_" + name + "_wait")
        got = lax.dynamic_update_index_in_dim(got, own, me, 0)
        return got.reshape(N_DEV * got.shape[1], D_MODEL)

    sent = {}

    def emit(name, grad):
        if name == "w_in_t":
            grad = _from_proj_rows(grad)
        parts = grad.reshape(N_DEV, grad.shape[0] // N_DEV, D_MODEL)
        sent[name] = _exchange_start(parts, plan=PLAN_SCATTER, slab=parts.shape[1:], name="scatter_" + name + "_start")
        return sent[name][4]

    g = _local_step(x[0], loss_target[0], meta_f, attn_norm_w, conv_f, gdn_a_log, gdn_dt_bias, gdn_norm_w, w2_f,
                    gla_gate_b, gla_norm_w, ffn_norm_w, final_norm_w.reshape(1, D_MODEL), fetch, emit, start=tok)

    misc = jnp.concatenate([g["a_log"], g["dt_bias"], g["gdn_nw"], g["gla_nw"], g["b2"], g["loss"].reshape(1, 1)], axis=1)
    n_misc = misc.shape[1]
    misc = jnp.pad(misc, ((0, 0), (0, D_MODEL - n_misc)))
    rows = jnp.concatenate([g["attn_nw"], g["ffn_nw"], g["final_nw"], misc, g["meta"],
                            g["conv_w"].reshape(-1, D_MODEL), g["w2"].reshape(-1, D_MODEL)], axis=0)
    rows = jnp.pad(rows, ((0, SMALL_ROWS - rows.shape[0]), (0, 0)))
    rows_h = _exchange_start(rows, plan=PLAN_GATHER, slab=rows.shape, name="gather_small_grads_start")

    big = {}
    after = rows_h[4]
    for name, w, m, v, transposed in (("w_down", w_down, m_w_down, v_w_down, False), ("w_gate_t", w_gate, m_w_gate, v_w_gate, True),
                                      ("w_up_t", w_up, m_w_up, v_w_up, True), ("w_out", w_out, m_w_out, v_w_out, False),
                                      ("w_in_t", w_in, m_w_in, v_w_in, True)):
        own, got = _exchange_wait(sent[name], after, plan=PLAN_SCATTER, name="scatter_" + name + "_wait")
        got = lax.dynamic_update_index_in_dim(got, lax.dynamic_index_in_dim(own, me, 0, keepdims=False), me, 0)
        local = [t[0].T if transposed else t[0] for t in (w, m, v)]
        res = _sum_adamw(got, *local, name="adamw_" + name)
        big[name] = [t.T[None] if transposed else t[None] for t in res]
        after = res[0]

    own, got = _exchange_wait(rows_h, after, plan=PLAN_GATHER, name="gather_small_grads_wait")
    tot = _sum_slabs(lax.dynamic_update_index_in_dim(got, own, me, 0), name="sum_small_grads")
    grad_attn_nw, grad_ffn_nw, grad_final_nw = tot[0:1], tot[1:2], tot[2]
    grad_a_log = tot[3:4, 0:8]
    grad_dt = tot[3:4, 8:16]
    grad_gdn_nw = tot[3:4, 16:16 + GDN_DV]
    grad_gla_nw = tot[3:4, 144:144 + GLA_DV]
    grad_b2 = tot[3:4, 400:400 + GLA_QK]
    loss = tot[3, n_misc - 1]
    r0 = 4 + N_META
    grad_meta = lax.dynamic_slice(tot[4:r0], (0, me * n_meta), (N_META, n_meta))
    r1 = r0 + CONV_K * N_DEV * n_conv // D_MODEL
    grad_conv = lax.dynamic_slice(tot[r0:r1].reshape(CONV_K, N_DEV * n_conv), (0, me * n_conv), (CONV_K, n_conv))[None]
    r2 = r1 + GLA_RANK * N_DEV * n_w2 // D_MODEL
    grad_w2 = lax.dynamic_slice(tot[r1:r2].reshape(GLA_RANK, N_DEV * n_w2), (0, me * n_w2), (GLA_RANK, n_w2))[None]

    weights = [meta_tokens, attn_norm_w, w_in, gdn_conv_w, gdn_a_log, gdn_dt_bias, gdn_norm_w, gla_gate_w2,
               gla_gate_b, gla_norm_w, w_out, ffn_norm_w, w_gate, w_up, w_down, final_norm_w]
    grads = [grad_meta, grad_attn_nw, "w_in_t", grad_conv, grad_a_log, grad_dt, grad_gdn_nw, grad_w2,
             grad_b2, grad_gla_nw, "w_out", grad_ffn_nw, "w_gate_t", "w_up_t", "w_down", grad_final_nw]
    ms = [m_meta_tokens, m_attn_norm_w, m_w_in, m_gdn_conv_w, m_gdn_a_log, m_gdn_dt_bias, m_gdn_norm_w,
          m_gla_gate_w2, m_gla_gate_b, m_gla_norm_w, m_w_out, m_ffn_norm_w, m_w_gate, m_w_up, m_w_down, m_final_norm_w]
    vs = [v_meta_tokens, v_attn_norm_w, v_w_in, v_gdn_conv_w, v_gdn_a_log, v_gdn_dt_bias, v_gdn_norm_w,
          v_gla_gate_w2, v_gla_gate_b, v_gla_norm_w, v_w_out, v_ffn_norm_w, v_w_gate, v_w_up, v_w_down, v_final_norm_w]
    outs = [[], [], [], []]
    for idx, (w, gr, m, v) in enumerate(zip(weights, grads, ms, vs)):
        if isinstance(gr, str):
            res = big[gr]
        else:
            gr = gr.reshape(w.shape)
            res = (gr,) + _adamw(w, gr, m, v, name=f"adamw_{idx}")
        for lst, t in zip(outs, res):
            lst.append(t)
    return (loss, g["grad_x"][None], *outs[0], *outs[1], *outs[2], *outs[3])
```

```python
import functools

import jax
import jax.numpy as jnp
from jax import lax
from jax.experimental import pallas as pl
from jax.experimental.pallas import tpu as pltpu

F32 = jnp.float32
BF16 = jnp.bfloat16
_MXU_DTYPE = jnp.bfloat16

D_MODEL = 2048
N_META = 16
ROW_PAD = 48
HEAD_ROWS = ROW_PAD + N_META
CONV_K = 4
GDN_HEADS, GDN_DK, GDN_DV, GDN_CHUNK = 8, 128, 128, 64
GLA_HEADS, GLA_DK, GLA_DV, GLA_CHUNK = 4, 128, 256, 16
GLA_RANK = 16
GLA_GATE_NORMALIZER = 16.0
GDN_QK = GDN_HEADS * GDN_DK
GDN_V = GDN_HEADS * GDN_DV
GLA_QK = GLA_HEADS * GLA_DK
GLA_V = GLA_HEADS * GLA_DV
D_FF = 5632
D_IN = 7200
NORM_EPS = 1e-6
C_Z, C_GR, C_GQ, C_GK, C_GV, C_QKV, C_SM = 0, 1024, 2048, 2560, 3072, 4096, 7168
SM_W = 128
D_PROJ = 7680
R_Z, R_A, R_B, R_GQ, R_GK, R_GV, R_GR, R_LR = 3072, 4096, 4104, 4112, 4624, 5136, 6160, 7184

ADAM_LR, ADAM_B1, ADAM_B2, ADAM_EPS, ADAM_WD, ADAM_STEP = 0.001, 0.9, 0.999, 1e-08, 0.01, 10

N_DEV = 8
VMEM_LIMIT = 56 * 1024 * 1024

NN = (((1,), (0,)), ((), ()))
NT = (((1,), (1,)), ((), ()))
TN = (((0,), (0,)), ((), ()))


def _dot(a, b, dims=NN):
    return lax.dot_general(a.astype(_MXU_DTYPE), b.astype(_MXU_DTYPE), dims, preferred_element_type=F32)


def _running_sum(x, reverse=False):
    n = x.shape[0]
    row = lax.broadcasted_iota(jnp.int32, x.shape, 0)
    s = 1
    while s < n:
        if reverse:
            x = x + jnp.where(row < n - s, pltpu.roll(x, n - s, 0), 0.0)
        else:
            x = x + jnp.where(row >= s, pltpu.roll(x, s, 0), 0.0)
        s *= 2
    return x


def _dot3(a, b):
    ah = a.astype(BF16)
    al = (a - ah.astype(F32)).astype(BF16)
    bh = b.astype(BF16)
    bl = (b - bh.astype(F32)).astype(BF16)
    d = functools.partial(lax.dot_general, dimension_numbers=NN, preferred_element_type=F32)
    return d(ah, bh) + (d(ah, bl) + d(al, bh))


def _tile(n, target, mult=8):
    best = None
    for t in range(mult, min(n, target) + 1, mult):
        if n % t == 0:
            best = t
    return best if best is not None else n


def _params(*sem):
    return pltpu.CompilerParams(dimension_semantics=sem, vmem_limit_bytes=VMEM_LIMIT)


def _sigmoid(x):
    return 0.5 * jnp.tanh(0.5 * x) + 0.5


def _softplus(x):
    return jnp.maximum(x, 0.0) + jnp.log1p(jnp.exp(-jnp.abs(x)))


def _silu_and_grad(c):
    s = _sigmoid(c)
    return c * s, s * (1.0 + c * (1.0 - s))


_ANY = pl.BlockSpec(memory_space=pl.ANY)


def _matmul(a, b, *, mode, name, out_dtype=F32, add=None, after=None, tm=1376, tn=512):
    if mode == "tn":
        K, M = a.shape
        N = b.shape[1]
    else:
        M, K = a.shape
        N = b.shape[0] if mode == "nt" else b.shape[1]
    tm = _tile(M, tm, 128 if mode == "tn" else 16)
    tn = _tile(N, tn, 128)
    dims = {"nn": NN, "nt": NT, "tn": TN}[mode]
    n_after = 0 if after is None else 1

    def body(*refs):
        refs = refs[n_after:]
        r = _dot(refs[0][...], refs[1][...], dims)
        if add is not None:
            r = r + refs[2][...]
        refs[-1][...] = r.astype(out_dtype)

    a_spec = pl.BlockSpec((K, tm), lambda i, j: (0, i)) if mode == "tn" else pl.BlockSpec((tm, K), lambda i, j: (i, 0))
    b_spec = pl.BlockSpec((tn, K), lambda i, j: (j, 0)) if mode == "nt" else pl.BlockSpec((K, tn), lambda i, j: (0, j))
    o_spec = pl.BlockSpec((tm, tn), lambda i, j: (i, j))
    in_specs = [_ANY] * n_after + [a_spec, b_spec] + ([o_spec] if add is not None else [])
    args = ((after,) if n_after else ()) + (a, b) + ((add,) if add is not None else ())
    return pl.pallas_call(
        body, name=name, grid=(M // tm, N // tn), in_specs=in_specs, out_specs=o_spec,
        out_shape=jax.ShapeDtypeStruct((M, N), out_dtype), compiler_params=_params("parallel", "parallel"),
    )(*args)


def _in_proj(n, w_in_t, *, name, tm=1376, tn=512):
    M, K = n.shape
    N = w_in_t.shape[0]
    tm, tn = _tile(M, tm, 16), _tile(N, tn, 128)
    assert C_SM % tn == 0
    j_small = C_SM // tn

    def body(n_ref, w_ref, o_ref, sm_ref):
        r = _dot(n_ref[...], w_ref[...], NT)
        o_ref[...] = r.astype(o_ref.dtype)

        @pl.when(pl.program_id(1) == j_small)
        def _():
            sm_ref[...] = r[:, 0:SM_W]

    return pl.pallas_call(
        body, name=name, grid=(M // tm, N // tn),
        in_specs=[pl.BlockSpec((tm, K), lambda i, j: (i, 0)), pl.BlockSpec((tn, K), lambda i, j: (j, 0))],
        out_specs=[pl.BlockSpec((tm, tn), lambda i, j: (i, j)), pl.BlockSpec((tm, SM_W), lambda i, j: (i, 0))],
        out_shape=[jax.ShapeDtypeStruct((M, N), BF16), jax.ShapeDtypeStruct((M, SM_W), F32)],
        compiler_params=_params("parallel", "arbitrary"),
    )(n, w_in_t)


def _matmul_pair(a1, b1, a2, b2, *, name, after=None, tm=688, tn=256):
    M, K = a1.shape
    N = b1.shape[1]
    tm, tn = _tile(M, tm, 16), _tile(N, tn, 128)
    n_after = 0 if after is None else 1

    def body(*refs):
        a1_ref, b1_ref, a2_ref, b2_ref, o_ref = refs[n_after:]
        o_ref[...] = _dot(a1_ref[...], b1_ref[...]) + _dot(a2_ref[...], b2_ref[...])

    a_spec = pl.BlockSpec((tm, K), lambda i, j: (i, 0))
    b_spec = pl.BlockSpec((K, tn), lambda i, j: (0, j))
    return pl.pallas_call(
        body, name=name, grid=(M // tm, N // tn), in_specs=[_ANY] * n_after + [a_spec, b_spec, a_spec, b_spec],
        out_specs=pl.BlockSpec((tm, tn), lambda i, j: (i, j)), out_shape=jax.ShapeDtypeStruct((M, N), F32),
        compiler_params=_params("parallel", "parallel"),
    )(*((after,) if n_after else ()), a1, b1, a2, b2)


def _rmsnorm_fwd(h, w, *, name):
    M, D = h.shape
    tm = _tile(M, 688, 16)

    def body(h_ref, w_ref, n_ref):
        x = h_ref[...]
        r = lax.rsqrt(jnp.mean(x * x, axis=-1, keepdims=True) + NORM_EPS)
        n_ref[...] = (x * r * w_ref[...]).astype(n_ref.dtype)

    return pl.pallas_call(
        body, name=name, grid=(M // tm,),
        in_specs=[pl.BlockSpec((tm, D), lambda i: (i, 0)), pl.BlockSpec((1, D), lambda i: (0, 0))],
        out_specs=pl.BlockSpec((tm, D), lambda i: (i, 0)),
        out_shape=jax.ShapeDtypeStruct((M, D), BF16),
        compiler_params=_params("parallel"),
    )(h, w)


SEQ_BLOCK = HEAD_ROWS


def _seq_blocks_per_tile(rows):
    n = rows // SEQ_BLOCK
    return max(m for m in (1, 2, 3, 4) if n % m == 0)


def _seq_specs(m, D):
    return [pl.BlockSpec((SEQ_BLOCK, D), functools.partial(lambda i, k: (jnp.maximum(m * i + k - 1, 0), 0), k=k))
            for k in range(m)]


def _embed_norm(head, x, w, *, name, after=None):
    S, D = x.shape
    m = _seq_blocks_per_tile(S + HEAD_ROWS)
    n_after = 0 if after is None else 1

    def body(*refs):
        refs = refs[n_after:]
        head_ref, x_refs, w_ref, h_ref, n_ref = refs[0], refs[1:1 + m], refs[1 + m], refs[2 + m], refs[3 + m]
        i = pl.program_id(0)
        for k in range(m):
            blk = x_refs[k][...]
            if k == 0:
                blk = jnp.where(i == 0, head_ref[...], blk)
            rows = slice(k * SEQ_BLOCK, (k + 1) * SEQ_BLOCK)
            h_ref[rows, :] = blk
            r = lax.rsqrt(jnp.mean(blk * blk, axis=-1, keepdims=True) + NORM_EPS)
            n_ref[rows, :] = (blk * r * w_ref[...]).astype(n_ref.dtype)

    tile = pl.BlockSpec((m * SEQ_BLOCK, D), lambda i: (i, 0))
    return pl.pallas_call(
        body, name=name, grid=((S + HEAD_ROWS) // (m * SEQ_BLOCK),),
        in_specs=[_ANY] * n_after + [pl.BlockSpec((SEQ_BLOCK, D), lambda i: (0, 0))] + _seq_specs(m, D)
        + [pl.BlockSpec((1, D), lambda i: (0, 0))],
        out_specs=[tile, tile],
        out_shape=[jax.ShapeDtypeStruct((S + HEAD_ROWS, D), F32), jax.ShapeDtypeStruct((S + HEAD_ROWS, D), BF16)],
        compiler_params=_params("parallel"),
    )(*((after,) if n_after else ()), head, *([x] * m), w)


def _embed_norm_bwd(h, w, dn, dres, *, name):
    M, D = h.shape
    S = M - HEAD_ROWS
    m = _seq_blocks_per_tile(S)
    g = S // (m * SEQ_BLOCK)

    def one(x, dn_, dres_, w_):
        r = lax.rsqrt(jnp.mean(x * x, axis=-1, keepdims=True) + NORM_EPS)
        xhat = x * r
        dxhat = dn_ * w_
        dh = dres_ + r * (dxhat - xhat * jnp.mean(dxhat * xhat, axis=-1, keepdims=True))
        return dh, jnp.sum((dn_ * xhat).reshape(SEQ_BLOCK // 8, 8, D), axis=0)

    def body(*refs):
        w_ref = refs[0]
        groups = [refs[1 + a * (m + 1):1 + (a + 1) * (m + 1)] for a in range(3)]
        gx_ref, dhead_ref, dw_ref, acc_ref = refs[1 + 3 * (m + 1):]
        i = pl.program_id(0)
        w_ = w_ref[...]
        part = jnp.zeros((8, D), F32)
        for k in range(m):
            dh, p = one(*(grp[1 + k][...] for grp in groups), w_)
            gx_ref[k * SEQ_BLOCK:(k + 1) * SEQ_BLOCK, :] = dh
            part = part + p

        @pl.when(i == 0)
        def _():
            dh, p = one(*(grp[0][...] for grp in groups), w_)
            dhead_ref[...] = dh
            acc_ref[...] = part + p

        @pl.when(i > 0)
        def _():
            acc_ref[...] += part

        @pl.when(i == g - 1)
        def _():
            dw_ref[...] = jnp.sum(acc_ref[...], axis=0, keepdims=True)

    first = pl.BlockSpec((SEQ_BLOCK, D), lambda i: (0, 0))
    blocks = [pl.BlockSpec((SEQ_BLOCK, D), functools.partial(lambda i, k: (m * i + k + 1, 0), k=k)) for k in range(m)]
    vec = pl.BlockSpec((1, D), lambda i: (0, 0))
    return pl.pallas_call(
        body, name=name, grid=(g,), in_specs=[vec] + ([first] + blocks) * 3,
        out_specs=[pl.BlockSpec((m * SEQ_BLOCK, D), lambda i: (i, 0)), first, vec],
        out_shape=[jax.ShapeDtypeStruct((S, D), F32), jax.ShapeDtypeStruct((SEQ_BLOCK, D), F32),
                   jax.ShapeDtypeStruct((1, D), F32)],
        scratch_shapes=[pltpu.VMEM((8, D), F32)],
        compiler_params=_params("arbitrary"),
    )(w, *([h] * (m + 1)), *([dn] * (m + 1)), *([dres] * (m + 1)))


def _rmsnorm_bwd(h, w, dn, dres, *, name):
    M, D = h.shape
    tm = _tile(M, 344, 16)
    g = M // tm

    def body(h_ref, w_ref, dn_ref, dres_ref, dh_ref, dhb_ref, dw_ref, acc_ref):
        i = pl.program_id(0)
        x = h_ref[...]
        r = lax.rsqrt(jnp.mean(x * x, axis=-1, keepdims=True) + NORM_EPS)
        xhat = x * r
        dn_ = dn_ref[...]
        dxhat = dn_ * w_ref[...]
        dh = dres_ref[...] + r * (dxhat - xhat * jnp.mean(dxhat * xhat, axis=-1, keepdims=True))
        dh_ref[...] = dh
        dhb_ref[...] = dh.astype(dhb_ref.dtype)
        part = jnp.sum((dn_ * xhat).reshape(tm // 8, 8, D), axis=0)

        @pl.when(i == 0)
        def _():
            acc_ref[...] = part

        @pl.when(i > 0)
        def _():
            acc_ref[...] += part

        @pl.when(i == g - 1)
        def _():
            dw_ref[...] = jnp.sum(acc_ref[...], axis=0, keepdims=True)

    row = pl.BlockSpec((tm, D), lambda i: (i, 0))
    vec = pl.BlockSpec((1, D), lambda i: (0, 0))
    return pl.pallas_call(
        body, name=name, grid=(g,), in_specs=[row, vec, row, row],
        out_specs=[row, row, vec],
        out_shape=[jax.ShapeDtypeStruct((M, D), F32), jax.ShapeDtypeStruct((M, D), BF16),
                   jax.ShapeDtypeStruct((1, D), F32)],
        scratch_shapes=[pltpu.VMEM((8, D), F32)],
        compiler_params=_params("arbitrary"),
    )(h, w, dn, dres)


def _loss_head(h, w, target, *, name):
    M, D = h.shape
    m = _seq_blocks_per_tile(M)
    tm = m * SEQ_BLOCK
    g = M // tm

    def body(h_ref, w_ref, *rest):
        t_refs = rest[:m]
        dh_ref, dhb_ref, dw_ref, loss_ref, acc_ref, lacc_ref = rest[m:]
        i = pl.program_id(0)
        x = h_ref[...]
        row = i * tm + lax.broadcasted_iota(jnp.int32, (tm, 1), 0)
        live = row >= HEAD_ROWS
        r = lax.rsqrt(jnp.mean(x * x, axis=-1, keepdims=True) + NORM_EPS)
        xhat = x * r
        t = jnp.concatenate([t_ref[...] for t_ref in t_refs], axis=0)
        err = jnp.where(live, xhat * w_ref[...] - t, 0.0)
        dy = err * (1.0 / D)
        dxhat = dy * w_ref[...]
        dh = r * (dxhat - xhat * jnp.mean(dxhat * xhat, axis=-1, keepdims=True))
        dh_ref[...] = dh
        dhb_ref[...] = dh.astype(dhb_ref.dtype)
        part = jnp.sum((dy * xhat).reshape(tm // 8, 8, D), axis=0)
        lpart = jnp.sum((err * err).reshape(tm // 8, 8, D), axis=0)

        @pl.when(i == 0)
        def _():
            acc_ref[...] = part
            lacc_ref[...] = lpart

        @pl.when(i > 0)
        def _():
            acc_ref[...] += part
            lacc_ref[...] += lpart

        @pl.when(i == g - 1)
        def _():
            dw_ref[...] = jnp.sum(acc_ref[...], axis=0, keepdims=True)
            tot = jnp.sum(jnp.sum(lacc_ref[...], axis=0, keepdims=True), axis=1, keepdims=True)
            loss_ref[...] = jnp.broadcast_to(tot * (0.5 / D), (1, 128))

    row = pl.BlockSpec((tm, D), lambda i: (i, 0))
    vec = pl.BlockSpec((1, D), lambda i: (0, 0))
    return pl.pallas_call(
        body, name=name, grid=(g,), in_specs=[row, vec] + _seq_specs(m, D),
        out_specs=[row, row, vec, pl.BlockSpec((1, 128), lambda i: (0, 0))],
        out_shape=[jax.ShapeDtypeStruct((M, D), F32), jax.ShapeDtypeStruct((M, D), BF16),
                   jax.ShapeDtypeStruct((1, D), F32), jax.ShapeDtypeStruct((1, 128), F32)],
        scratch_shapes=[pltpu.VMEM((8, D), F32), pltpu.VMEM((8, D), F32)],
        compiler_params=_params("arbitrary"),
    )(h, w, *([target] * m))


def _gate_terms(sm, w2p, b2, alog_p, dt_p, row0):
    tm = sm.shape[0]
    lane = lax.broadcasted_iota(jnp.int32, (tm, SM_W), 1)
    live = (row0 + lax.broadcasted_iota(jnp.int32, (tm, 1), 0)) >= ROW_PAD
    pre = sm + dt_p
    neg_a = -jnp.exp(alog_p)
    g = neg_a * _softplus(pre)
    beta = _sigmoid(sm)
    z = _dot(sm, w2p) + b2
    return lane, live, pre, neg_a, g, beta, z


def _gates_fwd(sm, w2p, b2, alog_p, dt_p, *, name):
    M = sm.shape[0]
    tm = _tile(M, 688, 8)

    def body(sm_ref, w2_ref, b2_ref, al_ref, dt_ref, gb_ref, la_ref):
        row0 = pl.program_id(0) * tm
        lane, live, _, _, g, beta, z = _gate_terms(sm_ref[...].astype(F32), w2_ref[...], b2_ref[...], al_ref[...], dt_ref[...], row0)
        gb = jnp.where(lane < GDN_HEADS, g, jnp.where(lane < 2 * GDN_HEADS, beta, 0.0))
        gb_ref[...] = jnp.where(live, gb, 0.0)
        la = (jnp.minimum(z, 0.0) - jnp.log1p(jnp.exp(-jnp.abs(z)))) * (1.0 / GLA_GATE_NORMALIZER)
        la_ref[...] = jnp.where(live, la, 0.0)

    full = lambda s: pl.BlockSpec(s, lambda i: (0, 0))
    return pl.pallas_call(
        body, name=name, grid=(M // tm,),
        in_specs=[pl.BlockSpec((tm, SM_W), lambda i: (i, 0)), full((SM_W, GLA_QK)), full((1, GLA_QK)),
                  full((1, SM_W)), full((1, SM_W))],
        out_specs=[pl.BlockSpec((tm, SM_W), lambda i: (i, 0)), pl.BlockSpec((tm, GLA_QK), lambda i: (i, 0))],
        out_shape=[jax.ShapeDtypeStruct((M, SM_W), F32), jax.ShapeDtypeStruct((M, GLA_QK), F32)],
        compiler_params=_params("parallel"),
    )(sm, w2p, b2, alog_p, dt_p)


def _gates_bwd(sm, w2p, b2, alog_p, dt_p, dgb_heads, dla, d_proj, *, name):
    M = sm.shape[0]
    tm = _tile(M, 688, 8)
    g_ = M // tm

    tail_w = D_PROJ - C_SM

    def body(sm_ref, w2_ref, b2_ref, al_ref, dt_ref, dgb_ref, dla_ref, _,
             dsm_ref, dw2_ref, db2_ref, dal_ref, ddt_ref):
        i = pl.program_id(0)
        sm = sm_ref[...].astype(F32)
        lane, live, pre, neg_a, g, beta, z = _gate_terms(sm, w2_ref[...], b2_ref[...], al_ref[...], dt_ref[...], i * tm)
        dz = jnp.where(live, dla_ref[...] * (_sigmoid(-z) * (1.0 / GLA_GATE_NORMALIZER)), 0.0)
        dsm_lr = _dot(dz, w2_ref[...], NT)
        dgb = dgb_ref[0]
        for hh in range(1, GDN_HEADS):
            dgb = dgb + dgb_ref[hh]
        dgb = jnp.where(live, dgb, 0.0)
        da = dgb * neg_a * _sigmoid(pre)
        db = dgb * beta * (1.0 - beta)
        dsm = jnp.where(lane < GDN_HEADS, da, jnp.where(lane < 2 * GDN_HEADS, db, dsm_lr))
        dsm_ref[:, 0:SM_W] = dsm.astype(dsm_ref.dtype)
        if tail_w > SM_W:
            dsm_ref[:, SM_W:tail_w] = jnp.zeros((tm, tail_w - SM_W), dsm_ref.dtype)
        is_a = lane < GDN_HEADS
        dal = jnp.sum(jnp.where(is_a, dgb * g, 0.0), axis=0, keepdims=True)
        ddt = jnp.sum(jnp.where(is_a, da, 0.0), axis=0, keepdims=True)
        dw2 = _dot(sm, dz, TN)
        db2 = jnp.sum(dz, axis=0, keepdims=True)

        @pl.when(i == 0)
        def _():
            dw2_ref[...] = dw2
            db2_ref[...] = db2
            dal_ref[...] = dal
            ddt_ref[...] = ddt

        @pl.when(i > 0)
        def _():
            dw2_ref[...] += dw2
            db2_ref[...] += db2
            dal_ref[...] += dal
            ddt_ref[...] += ddt

    full = lambda s: pl.BlockSpec(s, lambda i: (0, 0))
    return pl.pallas_call(
        body, name=name, grid=(g_,),
        in_specs=[pl.BlockSpec((tm, SM_W), lambda i: (i, 0)), full((SM_W, GLA_QK)), full((1, GLA_QK)),
                  full((1, SM_W)), full((1, SM_W)),
                  pl.BlockSpec((GDN_HEADS, tm, SM_W), lambda i: (0, i, 0)),
                  pl.BlockSpec((tm, GLA_QK), lambda i: (i, 0)), _ANY],
        out_specs=[pl.BlockSpec((tm, tail_w), lambda i: (i, C_SM // tail_w)), full((SM_W, GLA_QK)), full((1, GLA_QK)),
                   full((1, SM_W)), full((1, SM_W))],
        out_shape=[jax.ShapeDtypeStruct(d_proj.shape, d_proj.dtype), jax.ShapeDtypeStruct((SM_W, GLA_QK), F32),
                   jax.ShapeDtypeStruct((1, GLA_QK), F32), jax.ShapeDtypeStruct((1, SM_W), F32),
                   jax.ShapeDtypeStruct((1, SM_W), F32)],
        input_output_aliases={7: 0},
        compiler_params=_params("arbitrary"),
    )(sm, w2p, b2, alog_p, dt_p, dgb_heads, dla, d_proj)


QKV_W = GDN_QK
N_QKV_GROUPS = 3
QKV_B0 = C_QKV // QKV_W
HALO = 16


def _conv_terms(x_ref, halo_ref, cw_ref, xs_ref, i, tm):
    xs_ref[HALO:HALO + tm, :] = x_ref[...].astype(F32)
    xs_ref[0:HALO, :] = jnp.where(i > 0, halo_ref[...].astype(F32), 0.0)
    cw = cw_ref[...]
    xs = xs_ref[...]
    taps = [(pltpu.roll(xs, CONV_K - 1 - t, 0) if t < CONV_K - 1 else xs)[HALO:HALO + tm, :] for t in range(CONV_K)]
    c = taps[0] * cw[0:1, :]
    for t in range(1, CONV_K):
        c = c + taps[t] * cw[t:t + 1, :]
    return c, taps


def _prep_fwd(proj, conv_w8, *, name):
    M = proj.shape[0]
    tm = _tile(M, 688, 16)

    def body(x_ref, halo_ref, cw_ref, o_ref, xs_ref):
        j, i = pl.program_id(0), pl.program_id(1)
        c, _ = _conv_terms(x_ref, halo_ref, cw_ref, xs_ref, i, tm)
        s, _ = _silu_and_grad(c)
        scale = jnp.where(j == 0, GDN_DK ** -0.5, 1.0)
        for hh in range(GDN_HEADS):
            cols = slice(hh * 128, (hh + 1) * 128)
            sh = s[:, cols]
            r = lax.rsqrt(jnp.sum(sh * sh, axis=-1, keepdims=True) + NORM_EPS)
            o_ref[:, cols] = jnp.where(j < 2, sh * (r * scale), sh)

    hb = tm // HALO
    return pl.pallas_call(
        body, name=name, grid=(N_QKV_GROUPS, M // tm),
        in_specs=[pl.BlockSpec((tm, QKV_W), lambda j, i: (i, QKV_B0 + j)),
                  pl.BlockSpec((HALO, QKV_W), lambda j, i: (jnp.maximum(i * hb - 1, 0), QKV_B0 + j)),
                  pl.BlockSpec((8, QKV_W), lambda j, i: (0, j))],
        out_specs=pl.BlockSpec((tm, QKV_W), lambda j, i: (i, j)),
        out_shape=jax.ShapeDtypeStruct((M, N_QKV_GROUPS * QKV_W), F32),
        scratch_shapes=[pltpu.VMEM((tm + HALO, QKV_W), F32)],
        compiler_params=_params("parallel", "arbitrary"),
    )(proj, proj, conv_w8)


def _prep_bwd(proj, conv_w8, dact, d_proj, *, name):
    M = proj.shape[0]
    tm = _tile(M, 688, 16)
    g_ = M // tm
    ext = tm + HALO

    def body(x_ref, prev_ref, next_ref, cw_ref, da_ref, dan_ref, _, o_ref, dcw_ref, xs_ref, das_ref, dcs_ref):
        j, i = pl.program_id(0), pl.program_id(1)
        not_last = i < g_ - 1
        xs_ref[0:HALO, :] = jnp.where(i > 0, prev_ref[...].astype(F32), 0.0)
        xs_ref[HALO:HALO + tm, :] = x_ref[...].astype(F32)
        xs_ref[HALO + tm:HALO + ext, :] = jnp.where(not_last, next_ref[...].astype(F32), 0.0)
        das_ref[0:tm, :] = da_ref[...]
        das_ref[tm:ext, :] = jnp.where(not_last, dan_ref[...], 0.0)
        cw = cw_ref[...]
        xs = xs_ref[...]
        taps = [(pltpu.roll(xs, CONV_K - 1 - t, 0) if t < CONV_K - 1 else xs)[HALO:HALO + ext, :] for t in range(CONV_K)]
        c = taps[0] * cw[0:1, :]
        for t in range(1, CONV_K):
            c = c + taps[t] * cw[t:t + 1, :]
        s, ds_dc = _silu_and_grad(c)
        scale = jnp.where(j == 0, GDN_DK ** -0.5, 1.0)
        for hh in range(GDN_HEADS):
            cols = slice(hh * 128, (hh + 1) * 128)
            sh = s[:, cols]
            r = lax.rsqrt(jnp.sum(sh * sh, axis=-1, keepdims=True) + NORM_EPS)
            da = das_ref[:, cols]
            y = sh * r
            dy = da * scale
            ds_norm = r * (dy - y * jnp.sum(dy * y, axis=-1, keepdims=True))
            dcs_ref[:, cols] = jnp.where(j < 2, ds_norm, da) * ds_dc[:, cols]
        dc = dcs_ref[...]
        acc = dc[0:tm, :] * cw[CONV_K - 1:CONV_K, :]
        for t in range(CONV_K - 1):
            acc = acc + pltpu.roll(dc, ext - (CONV_K - 1 - t), 0)[0:tm, :] * cw[t:t + 1, :]
        o_ref[...] = acc.astype(o_ref.dtype)
        r8 = lax.broadcasted_iota(jnp.int32, (8, QKV_W), 0)
        part = jnp.zeros((8, QKV_W), F32)
        for t in range(CONV_K):
            part = jnp.where(r8 == t, jnp.sum(dc[0:tm, :] * taps[t][0:tm, :], axis=0, keepdims=True), part)

        @pl.when(i == 0)
        def _():
            dcw_ref[...] = part

        @pl.when(i > 0)
        def _():
            dcw_ref[...] += part

    hb = tm // HALO
    last = M // HALO - 1
    prev_of = lambda i: jnp.maximum(i * hb - 1, 0)
    next_of = lambda i: jnp.minimum((i + 1) * hb, last)
    return pl.pallas_call(
        body, name=name, grid=(N_QKV_GROUPS, g_),
        in_specs=[pl.BlockSpec((tm, QKV_W), lambda j, i: (i, QKV_B0 + j)),
                  pl.BlockSpec((HALO, QKV_W), lambda j, i: (prev_of(i), QKV_B0 + j)),
                  pl.BlockSpec((HALO, QKV_W), lambda j, i: (next_of(i), QKV_B0 + j)),
                  pl.BlockSpec((8, QKV_W), lambda j, i: (0, j)),
                  pl.BlockSpec((tm, QKV_W), lambda j, i: (i, j)),
                  pl.BlockSpec((HALO, QKV_W), lambda j, i: (next_of(i), j)), _ANY],
        out_specs=[pl.BlockSpec((tm, QKV_W), lambda j, i: (i, QKV_B0 + j)), pl.BlockSpec((8, QKV_W), lambda j, i: (0, j))],
        out_shape=[jax.ShapeDtypeStruct(d_proj.shape, d_proj.dtype),
                   jax.ShapeDtypeStruct((8, N_QKV_GROUPS * QKV_W), F32)],
        input_output_aliases={6: 0},
        scratch_shapes=[pltpu.VMEM((HALO + ext, QKV_W), F32), pltpu.VMEM((ext, QKV_W), F32), pltpu.VMEM((ext, QKV_W), F32)],
        compiler_params=_params("parallel", "arbitrary"),
    )(proj, proj, proj, conv_w8, dact, dact, d_proj)


def _round_robin(gens):
    gens = list(gens)
    while gens:
        alive = []
        for gen in gens:
            try:
                next(gen)
                alive.append(gen)
            except StopIteration:
                pass
        gens = alive


def _unit_lower_inverse(a_low, eye):
    n = a_low.shape[0]
    ri = lax.broadcasted_iota(jnp.int32, (n, n), 0)
    ci = lax.broadcasted_iota(jnp.int32, (n, n), 1)
    same = lambda shift: (ri >> shift) == (ci >> shift)
    b = jnp.where(same(3), -a_low, 0.0)
    x = eye + b
    p2 = _dot3(b, b)
    yield
    x = x + _dot3(x, p2)
    p4 = _dot3(p2, p2)
    yield
    x = x + _dot3(x, p4)
    yield
    for shift in (3, 4, 5):
        between = jnp.where(same(shift + 1) & ~same(shift), a_low, 0.0)
        t = _dot3(between, x)
        yield
        x = x - _dot3(x, t)
        yield
    return x


class _GdnChunk:
    def build(self, q, k, v, gb, h, sum_on_mxu):
        C = GDN_CHUNK
        lane = lax.broadcasted_iota(jnp.int32, (C, SM_W), 1)
        g = jnp.sum(jnp.where(lane == h, gb, 0.0), axis=1, keepdims=True)
        self.beta = jnp.sum(jnp.where(lane == h + GDN_HEADS, gb, 0.0), axis=1, keepdims=True)
        ri = lax.broadcasted_iota(jnp.int32, (C, C), 0)
        ci = lax.broadcasted_iota(jnp.int32, (C, C), 1)
        self.causal = ri >= ci
        self.strict = ri > ci
        self.eye = (ri == ci).astype(F32)
        if sum_on_mxu:
            gcb = lax.dot_general(self.causal.astype(F32), jnp.broadcast_to(g, (C, SM_W)), NN,
                                  precision=lax.Precision.HIGHEST, preferred_element_type=F32)
        else:
            gcb = _running_sum(jnp.broadcast_to(g, (C, SM_W)))
        yield
        self.gcol = gcb[:, 0:1]
        grow = gcb.T[0:1, 0:C]
        self.decay = jnp.exp(jnp.where(self.causal, self.gcol - grow, -1e30))
        self.egc = jnp.exp(self.gcol)
        glast = gcb[C - 1:C, 0:1]
        self.elast = jnp.exp(glast - self.gcol)
        self.gl = jnp.exp(glast)
        self.q, self.k, self.v = q, k, v
        self.kb = k * self.beta
        m = _dot(self.kb, k, NT)
        n_ = _dot(q, k, NT)
        yield
        self.a_low = jnp.where(self.strict, m * self.decay, 0.0)
        self.p = n_ * self.decay
        self.qd = q * self.egc
        self.kd = k * self.elast
        self.bu = v * self.beta
        self.bw = self.kb * self.egc


GDN_HB = 8
GDN_HG = GDN_HEADS // GDN_HB


def _gdn_specs(n_of):
    C, W = GDN_CHUNK, 128 * GDN_HB
    q_spec = pl.BlockSpec((C, W), lambda g, n: (n_of(n), g))
    k_spec = pl.BlockSpec((C, W), lambda g, n: (n_of(n), g + GDN_HG))
    v_spec = pl.BlockSpec((C, W), lambda g, n: (n_of(n), g + 2 * GDN_HG))
    gb_spec = pl.BlockSpec((C, SM_W), lambda g, n: (n_of(n), 0))
    o_spec = pl.BlockSpec((C, W), lambda g, n: (n_of(n), g))
    s_spec = pl.BlockSpec((GDN_HB, None, GDN_DK, GDN_DV), lambda g, n: (g, n_of(n), 0, 0))
    t_spec = pl.BlockSpec((GDN_HB, None, C, C), lambda g, n: (g, n_of(n), 0, 0))
    return q_spec, k_spec, v_spec, gb_spec, o_spec, s_spec, t_spec


def _gdn_fwd(act, gb, *, name):
    M = act.shape[0]
    N = M // GDN_CHUNK

    def body(q_ref, k_ref, v_ref, gb_ref, o_ref, s_ref, t_ref, state):
        g, n = pl.program_id(0), pl.program_id(1)

        @pl.when(n == 0)
        def _():
            state[...] = jnp.zeros_like(state)

        gb_ = gb_ref[...]

        def head(hh):
            cols = slice(hh * 128, (hh + 1) * 128)
            c = _GdnChunk()
            yield from c.build(q_ref[:, cols], k_ref[:, cols], v_ref[:, cols], gb_, g * GDN_HB + hh, sum_on_mxu=True)
            tinv = yield from _unit_lower_inverse(c.a_low, c.eye)
            s = state[hh]
            s_ref[hh] = s
            t_ref[hh] = tinv
            u = _dot(tinv, c.bu)
            w = _dot(tinv, c.bw)
            yield
            vn = u - _dot(w, s)
            o1 = _dot(c.qd, s)
            yield
            o_ref[:, cols] = (o1 + _dot(c.p, vn)).astype(o_ref.dtype)
            state[hh] = c.gl * s + _dot(c.kd, vn, TN)

        _round_robin(head(hh) for hh in range(GDN_HB))

    q_spec, k_spec, v_spec, gb_spec, o_spec, s_spec, t_spec = _gdn_specs(lambda n: n)
    return pl.pallas_call(
        body, name=name, grid=(GDN_HG, N),
        in_specs=[q_spec, k_spec, v_spec, gb_spec], out_specs=[o_spec, s_spec, t_spec],
        out_shape=[jax.ShapeDtypeStruct((M, GDN_V), BF16),
                   jax.ShapeDtypeStruct((GDN_HEADS, N, GDN_DK, GDN_DV), F32),
                   jax.ShapeDtypeStruct((GDN_HEADS, N, GDN_CHUNK, GDN_CHUNK), F32)],
        scratch_shapes=[pltpu.VMEM((GDN_HB, GDN_DK, GDN_DV), F32)],
        compiler_params=_params("parallel", "arbitrary"),
    )(act, act, act, gb)


def _gdn_bwd(act, gb, do, s_all, t_all, *, name):
    M = act.shape[0]
    N = M // GDN_CHUNK
    C = GDN_CHUNK
    assert GDN_HG == 1

    def body(q_ref, k_ref, v_ref, gb_ref, do_ref, s_ref, t_ref, dact_ref, dgb_ref, dstate):
        g, n = pl.program_id(0), pl.program_id(1)

        @pl.when(n == 0)
        def _():
            dstate[...] = jnp.zeros_like(dstate)

        gb_ = gb_ref[...]
        last = lax.broadcasted_iota(jnp.int32, (C, 1), 0) == C - 1
        lane = lax.broadcasted_iota(jnp.int32, (C, SM_W), 1)
        def head(hh):
            cols = slice(hh * 128, (hh + 1) * 128)
            h = g * GDN_HB + hh
            c = _GdnChunk()
            yield from c.build(q_ref[:, cols], k_ref[:, cols], v_ref[:, cols], gb_, h, sum_on_mxu=False)
            tinv = t_ref[hh]
            tinv_t = tinv.T
            s = s_ref[hh]
            do_ = do_ref[:, cols]
            ds1 = dstate[hh]
            u = _dot(tinv, c.bu)
            w = _dot(tinv, c.bw)
            dqd = _dot(do_, s, NT)
            yield
            dvn0 = _dot(c.p, do_, TN) + _dot(c.kd, ds1)
            dst0 = _dot(c.qd, do_, TN) + c.gl * ds1
            yield
            vn = u - _dot(w, s)
            dvn = dvn0
            yield
            dp = jnp.where(c.causal, _dot(do_, vn, NT), 0.0)
            dstate[hh] = dst0 - _dot(w, dvn, TN)
            dkd = _dot(vn, ds1, NT)
            dw = -_dot(dvn, s, NT)
            dbu = _dot(tinv_t, dvn)
            dgl = jnp.sum(jnp.sum(s * ds1, axis=1, keepdims=True), axis=0, keepdims=True)
            yield
            dbw = _dot(tinv_t, dw)
            t1 = _dot(dbu, u, NT)
            yield
            da = jnp.where(c.strict, -(t1 + _dot(dbw, w, NT)), 0.0)
            dn_ = dp * c.decay
            dq0 = _dot(dn_, c.k)
            dk0 = _dot(dn_, c.q, TN)
            yield
            dm = da * c.decay
            e = da * c.a_low + dp * c.p
            dkb = _dot(dm, c.k) + dbw * c.egc
            dact_ref[:, GDN_QK + hh * 128:GDN_QK + (hh + 1) * 128] = (
                _dot(dm, c.kb, TN) + dk0 + dkb * c.beta + dkd * c.elast)
            dact_ref[:, cols] = dq0 + dqd * c.egc
            dact_ref[:, 2 * GDN_QK + hh * 128:2 * GDN_QK + (hh + 1) * 128] = dbu * c.beta
            dbeta = jnp.sum(dbu * c.v, axis=1, keepdims=True) + jnp.sum(dkb * c.k, axis=1, keepdims=True)
            t_kd = jnp.sum(dkd * c.kd, axis=1, keepdims=True)
            dgc = (jnp.sum(e, axis=1, keepdims=True) - jnp.sum(e.T, axis=1, keepdims=True)
                   + jnp.sum(dbw * c.bw, axis=1, keepdims=True) + jnp.sum(dqd * c.qd, axis=1, keepdims=True) - t_kd)
            dgc = dgc + jnp.where(last, jnp.sum(t_kd, axis=0, keepdims=True) + dgl * c.gl, 0.0)
            yield
            dg = _running_sum(jnp.broadcast_to(dgc, (C, SM_W)), reverse=True)
            dgb_ref[hh] = jnp.where(lane == h, dg, jnp.where(lane == h + GDN_HEADS, dbeta, 0.0))

        _round_robin(head(hh) for hh in range(GDN_HB))

    rev = lambda n: N - 1 - n
    q_spec, k_spec, v_spec, gb_spec, o_spec, s_spec, t_spec = _gdn_specs(rev)
    dgb_spec = pl.BlockSpec((GDN_HB, C, SM_W), lambda g, n: (g, rev(n), 0))
    return pl.pallas_call(
        body, name=name, grid=(GDN_HG, N),
        in_specs=[q_spec, k_spec, v_spec, gb_spec, o_spec, s_spec, t_spec],
        out_specs=[pl.BlockSpec((C, 2 * GDN_QK + GDN_V), lambda g, n: (rev(n), 0)), dgb_spec],
        out_shape=[jax.ShapeDtypeStruct((M, 2 * GDN_QK + GDN_V), F32),
                   jax.ShapeDtypeStruct((GDN_HEADS, M, SM_W), F32)],
        scratch_shapes=[pltpu.VMEM((GDN_HB, GDN_DK, GDN_DV), F32)],
        compiler_params=_params("parallel", "arbitrary"),
    )(act, act, act, gb, do, s_all, t_all)


GLA_STEP_ROWS = 64
GLA_SUB = GLA_STEP_ROWS // GLA_CHUNK


def _gla_cumsum(la):
    return _running_sum(la)


GLA_HALF = GLA_CHUNK // 2


def _gla_cross_factors(b):
    top = lax.broadcasted_iota(jnp.int32, b.shape, 0) < GLA_HALF
    bm = b[GLA_HALF - 1:GLA_HALF, :]
    late = jnp.where(top, 0.0, jnp.exp(jnp.minimum(b - bm, 0.0)))
    early = jnp.where(top, jnp.exp(jnp.minimum(bm - b, 0.0)), 0.0)
    return late, early


def _gla_half_decay(bh, ii):
    rj = lax.broadcasted_iota(jnp.int32, bh.shape, 0)
    return jnp.where(rj <= ii, jnp.exp(jnp.minimum(bh[ii:ii + 1, :] - bh, 0.0)), 0.0)


def _gla_scores_t(q, k, b):
    C, H = GLA_CHUNK, GLA_HALF
    lane = lax.broadcasted_iota(jnp.int32, (H, C), 1)
    halves = []
    for h0 in (0, H):
        qh, kh, bh = q[h0:h0 + H], k[h0:h0 + H], b[h0:h0 + H]
        sth = jnp.zeros((H, C), F32)
        for ii in range(H):
            si = jnp.sum(qh[ii:ii + 1, :] * kh * _gla_half_decay(bh, ii), axis=1, keepdims=True)
            sth = jnp.where(lane == h0 + ii, si, sth)
            if ii % 4 == 3:
                yield
        halves.append(sth)
    late, early = _gla_cross_factors(b)
    between = _dot(k * early, q * late, NT)
    yield
    return jnp.concatenate(halves, axis=0) + between


def _gla_specs(n_of):
    R = GLA_STEP_ROWS
    q_spec = pl.BlockSpec((R, GLA_QK), lambda n: (n_of(n), C_GQ // GLA_QK))
    k_spec = pl.BlockSpec((R, GLA_QK), lambda n: (n_of(n), C_GK // GLA_QK))
    v_spec = pl.BlockSpec((R, GLA_V), lambda n: (n_of(n), C_GV // GLA_V))
    la_spec = pl.BlockSpec((R, GLA_QK), lambda n: (n_of(n), 0))
    o_spec = pl.BlockSpec((R, GLA_V), lambda n: (n_of(n), 0))
    s_spec = pl.BlockSpec((GLA_HEADS, None, GLA_SUB, GLA_DV, GLA_DK), lambda n: (0, n_of(n), 0, 0, 0))
    return q_spec, k_spec, v_spec, la_spec, o_spec, s_spec


def _gla_fwd(proj, la, *, name):
    M = proj.shape[0]
    N = M // GLA_STEP_ROWS
    C = GLA_CHUNK

    def body(q_ref, k_ref, v_ref, la_ref, o_ref, s_ref, state):
        n = pl.program_id(0)

        @pl.when(n == 0)
        def _():
            state[...] = jnp.zeros_like(state)

        local = {}

        def within(hh, c):
            kc = slice(hh * GLA_DK, (hh + 1) * GLA_DK)
            vc = slice(hh * GLA_DV, (hh + 1) * GLA_DV)
            rows = slice(c * C, (c + 1) * C)
            q = q_ref[rows, kc].astype(F32) * (GLA_DK ** -0.5)
            k = k_ref[rows, kc].astype(F32)
            v = v_ref[rows, vc].astype(F32)
            b = _gla_cumsum(la_ref[rows, kc])
            yield
            blast = b[C - 1:C, :]
            sc_t = yield from _gla_scores_t(q, k, b)
            kv = _dot(v, k * jnp.exp(blast - b), TN)
            o2 = _dot(sc_t, v, TN)
            yield
            local[hh, c] = (q * jnp.exp(b), jnp.exp(blast), kv, o2)

        def across(hh):
            vc = slice(hh * GLA_DV, (hh + 1) * GLA_DV)
            st = state[hh]
            for c in range(GLA_SUB):
                qe, eblast, kv, o2 = local[hh, c]
                s_ref[hh, c] = st
                o1 = _dot(qe, st, NT)
                yield
                o_ref[c * C:(c + 1) * C, vc] = (o1 + o2).astype(o_ref.dtype)
                st = st * eblast + kv
            state[hh] = st

        _round_robin(within(hh, c) for c in range(GLA_SUB) for hh in range(GLA_HEADS))
        _round_robin(across(hh) for hh in range(GLA_HEADS))

    q_spec, k_spec, v_spec, la_spec, o_spec, s_spec = _gla_specs(lambda n: n)
    return pl.pallas_call(
        body, name=name, grid=(N,),
        in_specs=[q_spec, k_spec, v_spec, la_spec], out_specs=[o_spec, s_spec],
        out_shape=[jax.ShapeDtypeStruct((M, GLA_V), BF16),
                   jax.ShapeDtypeStruct((GLA_HEADS, N, GLA_SUB, GLA_DV, GLA_DK), F32)],
        scratch_shapes=[pltpu.VMEM((GLA_HEADS, GLA_DV, GLA_DK), F32)],
        compiler_params=_params("arbitrary"),
    )(proj, proj, proj, la)


def _gla_bwd(proj, la, do, s_all, d_proj, *, name):
    M = proj.shape[0]
    N = M // GLA_STEP_ROWS
    C = GLA_CHUNK
    qkv_w = 2 * GLA_QK + GLA_V
    assert C_GK == C_GQ + GLA_QK and C_GV == C_GK + GLA_QK and C_GQ % qkv_w == 0

    def body(q_ref, k_ref, v_ref, la_ref, do_ref, s_ref, _, dp_ref, dla_ref, dstate):
        n = pl.program_id(0)

        @pl.when(n == 0)
        def _():
            dstate[...] = jnp.zeros_like(dstate)

        H = GLA_HALF
        lane = lax.broadcasted_iota(jnp.int32, (C, C), 1)
        row = lax.broadcasted_iota(jnp.int32, (C, C), 0)
        ri = lax.broadcasted_iota(jnp.int32, (C, GLA_DK), 0)
        lane_h = lax.broadcasted_iota(jnp.int32, (H, C), 1)
        ri_h = lax.broadcasted_iota(jnp.int32, (H, GLA_DK), 0)
        cross = (row < H) & (lane >= H)
        def head(hh):
            kc = slice(hh * GLA_DK, (hh + 1) * GLA_DK)
            vc = slice(hh * GLA_DV, (hh + 1) * GLA_DV)
            ds1 = dstate[hh]
            for c in reversed(range(GLA_SUB)):
                rows = slice(c * C, (c + 1) * C)
                q = q_ref[rows, kc].astype(F32) * (GLA_DK ** -0.5)
                k = k_ref[rows, kc].astype(F32)
                v = v_ref[rows, vc].astype(F32)
                b = _gla_cumsum(la_ref[rows, kc])
                do_ = do_ref[rows, vc]
                st = s_ref[hh, c]
                dsc_t = _dot(v, do_, NT)
                dqe = _dot(do_, st)
                dke = _dot(v, ds1)
                yield
                blast = b[C - 1:C, :]
                eb = jnp.exp(b)
                elast = jnp.exp(blast - b)
                eblast = jnp.exp(blast)
                qe = q * eb
                ke = k * elast
                dv2 = _dot(ke, ds1, NT)
                ds_new = _dot(do_, qe, TN)
                deblast = jnp.sum(st * ds1, axis=0, keepdims=True)
                sc_halves, dq_halves, dk_halves = [], [], []
                for h0 in (0, H):
                    qh, kh, bh, dsch = q[h0:h0 + H], k[h0:h0 + H], b[h0:h0 + H], dsc_t[h0:h0 + H]
                    sch = jnp.zeros((H, C), F32)
                    dqh = jnp.zeros((H, GLA_DK), F32)
                    dkh = jnp.zeros((H, GLA_DK), F32)
                    for ii in range(H):
                        f = _gla_half_decay(bh, ii)
                        kf = kh * f
                        si = jnp.sum(qh[ii:ii + 1, :] * kf, axis=1, keepdims=True)
                        sch = jnp.where(lane_h == h0 + ii, si, sch)
                        dsi = jnp.sum(jnp.where(lane_h == h0 + ii, dsch, 0.0), axis=1, keepdims=True)
                        dqh = jnp.where(ri_h == ii, jnp.sum(dsi * kf, axis=0, keepdims=True), dqh)
                        dkh = dkh + (dsi * f) * qh[ii:ii + 1, :]
                        if ii % 4 == 3:
                            yield
                    sc_halves.append(sch)
                    dq_halves.append(dqh)
                    dk_halves.append(dkh)
                late, early = _gla_cross_factors(b)
                q_late, k_early = q * late, k * early
                dsc_x = jnp.where(cross, dsc_t, 0.0)
                sc_t = jnp.concatenate(sc_halves, axis=0) + _dot(k_early, q_late, NT)
                dq_sc = jnp.concatenate(dq_halves, axis=0) + _dot(dsc_x, k_early, TN) * late
                dk_sc = jnp.concatenate(dk_halves, axis=0) + _dot(dsc_x, q_late) * early
                yield
                dv1 = _dot(sc_t, do_)
                dp_ref[rows, kc] = ((dq_sc + dqe * eb) * (GLA_DK ** -0.5)).astype(dp_ref.dtype)
                dp_ref[rows, GLA_QK + hh * GLA_DK:GLA_QK + (hh + 1) * GLA_DK] = (dk_sc + dke * elast).astype(dp_ref.dtype)
                t_ke = dke * ke
                db = q * dq_sc - k * dk_sc + dqe * qe - t_ke
                db = db + jnp.where(ri == C - 1, jnp.sum(t_ke, axis=0, keepdims=True) + deblast * eblast, 0.0)
                dla = _running_sum(db, reverse=True)
                yield
                dp_ref[rows, 2 * GLA_QK + hh * GLA_DV:2 * GLA_QK + (hh + 1) * GLA_DV] = (dv1 + dv2).astype(dp_ref.dtype)
                dla_ref[rows, kc] = dla
                ds1 = ds1 * eblast + ds_new
            dstate[hh] = ds1

        _round_robin(head(hh) for hh in range(GLA_HEADS))

    rev = lambda n: N - 1 - n
    q_spec, k_spec, v_spec, la_spec, o_spec, s_spec = _gla_specs(rev)
    return pl.pallas_call(
        body, name=name, grid=(N,),
        in_specs=[q_spec, k_spec, v_spec, la_spec, o_spec, s_spec, _ANY],
        out_specs=[pl.BlockSpec((GLA_STEP_ROWS, qkv_w), lambda n: (rev(n), C_GQ // qkv_w)), la_spec],
        out_shape=[jax.ShapeDtypeStruct(d_proj.shape, d_proj.dtype), jax.ShapeDtypeStruct((M, GLA_QK), F32)],
        input_output_aliases={6: 0},
        scratch_shapes=[pltpu.VMEM((GLA_HEADS, GLA_DV, GLA_DK), F32)],
        compiler_params=_params("arbitrary"),
    )(proj, proj, proj, la, do, s_all, d_proj)


def _head_norm(o, wn):
    r = lax.rsqrt(jnp.mean(o * o, axis=-1, keepdims=True) + NORM_EPS)
    return o * r, r


def _mix_heads():
    heads = [(0, GDN_DV, hh * GDN_DV, hh * GDN_DV) for hh in range(GDN_HEADS)]
    heads += [(1, GLA_DV, GDN_V + hh * GLA_DV, hh * GLA_DV) for hh in range(GLA_HEADS)]
    return heads


def _mix_fwd(o_gdn, o_gla, proj, wn_gdn, wn_gla, *, name):
    M = proj.shape[0]
    tm = _tile(M, 344, 16)

    def body(og_ref, ol_ref, z_ref, r_ref, wg_ref, wl_ref, m_ref):
        srcs = ((og_ref, z_ref, wg_ref), (ol_ref, r_ref, wl_ref))
        for grp, width, mcol, col in _mix_heads():
            o_ref, gate_ref, w_ref = srcs[grp]
            xhat, _ = _head_norm(o_ref[:, col:col + width].astype(F32), None)
            gate, _ = _silu_and_grad(gate_ref[:, col:col + width].astype(F32))
            m_ref[:, mcol:mcol + width] = (xhat * w_ref[...] * gate).astype(m_ref.dtype)

    full = lambda s: pl.BlockSpec(s, lambda i: (0, 0))
    return pl.pallas_call(
        body, name=name, grid=(M // tm,),
        in_specs=[pl.BlockSpec((tm, GDN_V), lambda i: (i, 0)), pl.BlockSpec((tm, GLA_V), lambda i: (i, 0)),
                  pl.BlockSpec((tm, GDN_V), lambda i: (i, C_Z // GDN_V)),
                  pl.BlockSpec((tm, GLA_V), lambda i: (i, C_GR // GLA_V)),
                  full((1, GDN_DV)), full((1, GLA_DV))],
        out_specs=pl.BlockSpec((tm, D_MODEL), lambda i: (i, 0)),
        out_shape=jax.ShapeDtypeStruct((M, D_MODEL), BF16),
        compiler_params=_params("parallel"),
    )(o_gdn, o_gla, proj, proj, wn_gdn, wn_gla)


def _mix_bwd(o_gdn, o_gla, proj, wn_gdn, wn_gla, dmixed, *, name):
    M = proj.shape[0]
    tm = _tile(M, 344, 16)
    g_ = M // tm
    assert C_Z == 0 and C_GR == GDN_V

    def body(og_ref, ol_ref, z_ref, r_ref, wg_ref, wl_ref, dm_ref,
             dog_ref, dol_ref, dzr_ref, dwg_ref, dwl_ref):
        i = pl.program_id(0)
        srcs = ((og_ref, z_ref, wg_ref, dog_ref), (ol_ref, r_ref, wl_ref, dol_ref))
        dws = [jnp.zeros((1, GDN_DV), F32), jnp.zeros((1, GLA_DV), F32)]
        for grp, width, mcol, col in _mix_heads():
            o_ref, gate_ref, w_ref, do_ref = srcs[grp]
            cols = slice(col, col + width)
            xhat, r = _head_norm(o_ref[:, cols].astype(F32), None)
            gate, dgate_dc = _silu_and_grad(gate_ref[:, cols].astype(F32))
            dm = dm_ref[:, mcol:mcol + width]
            dzr_ref[:, mcol:mcol + width] = (dm * xhat * w_ref[...] * dgate_dc).astype(dzr_ref.dtype)
            dnorm = dm * gate
            dws[grp] = dws[grp] + jnp.sum(dnorm * xhat, axis=0, keepdims=True)
            dxhat = dnorm * w_ref[...]
            do_ref[:, cols] = r * (dxhat - xhat * jnp.mean(dxhat * xhat, axis=-1, keepdims=True))

        @pl.when(i == 0)
        def _():
            dwg_ref[...] = dws[0]
            dwl_ref[...] = dws[1]

        @pl.when(i > 0)
        def _():
            dwg_ref[...] += dws[0]
            dwl_ref[...] += dws[1]

    full = lambda s: pl.BlockSpec(s, lambda i: (0, 0))
    half = pl.BlockSpec((tm, GDN_V), lambda i: (i, 0))
    return pl.pallas_call(
        body, name=name, grid=(g_,),
        in_specs=[half, half, pl.BlockSpec((tm, GDN_V), lambda i: (i, C_Z // GDN_V)),
                  pl.BlockSpec((tm, GLA_V), lambda i: (i, C_GR // GLA_V)),
                  full((1, GDN_DV)), full((1, GLA_DV)), pl.BlockSpec((tm, D_MODEL), lambda i: (i, 0))],
        out_specs=[half, half, pl.BlockSpec((tm, GDN_V + GLA_V), lambda i: (i, 0)),
                   full((1, GDN_DV)), full((1, GLA_DV))],
        out_shape=[jax.ShapeDtypeStruct((M, GDN_V), F32), jax.ShapeDtypeStruct((M, GLA_V), F32),
                   jax.ShapeDtypeStruct((M, D_PROJ), BF16),
                   jax.ShapeDtypeStruct((1, GDN_DV), F32), jax.ShapeDtypeStruct((1, GLA_DV), F32)],
        compiler_params=_params("arbitrary"),
    )(o_gdn, o_gla, proj, proj, wn_gdn, wn_gla, dmixed)


def _row_chunks(tm, parts=2):
    if tm % (16 * parts):
        return [slice(0, tm)]
    return [slice(p * (tm // parts), (p + 1) * (tm // parts)) for p in range(parts)]


def _swiglu_fwd(n, w_gate_t, w_up_t, *, name, tm=1376, tn=512):
    M, D = n.shape
    F = w_gate_t.shape[0]
    tm, tn = _tile(M, tm, 16), _tile(F, tn, 128)

    def body(n_ref, wg_ref, wu_ref, g_ref, u_ref, a_ref):
        wg, wu = wg_ref[...], wu_ref[...]
        for rows in _row_chunks(tm):
            x = n_ref[rows, :]
            g = _dot(x, wg, NT)
            u = _dot(x, wu, NT)
            s, _ = _silu_and_grad(g)
            g_ref[rows, :] = g.astype(g_ref.dtype)
            u_ref[rows, :] = u.astype(u_ref.dtype)
            a_ref[rows, :] = (s * u).astype(a_ref.dtype)

    w_spec = pl.BlockSpec((tn, D), lambda i, j: (j, 0))
    o_spec = pl.BlockSpec((tm, tn), lambda i, j: (i, j))
    return pl.pallas_call(
        body, name=name, grid=(M // tm, F // tn),
        in_specs=[pl.BlockSpec((tm, D), lambda i, j: (i, 0)), w_spec, w_spec], out_specs=[o_spec] * 3,
        out_shape=[jax.ShapeDtypeStruct((M, F), BF16)] * 3, compiler_params=_params("parallel", "parallel"),
    )(n, w_gate_t, w_up_t)


def _swiglu_bwd(dh, w_down, gate, up, *, name, after=None, tm=1376, tn=512):
    M, D = dh.shape
    F = w_down.shape[0]
    tm, tn = _tile(M, tm, 16), _tile(F, tn, 128)
    n_after = 0 if after is None else 1

    def body(*refs):
        dh_ref, w_ref, g_ref, u_ref, dg_ref, du_ref = refs[n_after:]
        w = w_ref[...]
        for rows in _row_chunks(tm):
            da = _dot(dh_ref[rows, :], w, NT)
            s, ds = _silu_and_grad(g_ref[rows, :].astype(F32))
            dg_ref[rows, :] = (da * u_ref[rows, :].astype(F32) * ds).astype(dg_ref.dtype)
            du_ref[rows, :] = (da * s).astype(du_ref.dtype)

    o_spec = pl.BlockSpec((tm, tn), lambda i, j: (i, j))
    return pl.pallas_call(
        body, name=name, grid=(M // tm, F // tn),
        in_specs=[_ANY] * n_after + [pl.BlockSpec((tm, D), lambda i, j: (i, 0)),
                                     pl.BlockSpec((tn, D), lambda i, j: (j, 0)), o_spec, o_spec],
        out_specs=[o_spec, o_spec], out_shape=[jax.ShapeDtypeStruct((M, F), BF16)] * 2,
        compiler_params=_params("parallel", "parallel"),
    )(*((after,) if n_after else ()), dh, w_down, gate, up)


def _adamw_update(w, g, m, v):
    nm = ADAM_B1 * m + (1.0 - ADAM_B1) * g
    nv = ADAM_B2 * v + (1.0 - ADAM_B2) * (g * g)
    m_hat = nm / (1.0 - ADAM_B1 ** ADAM_STEP)
    v_hat = nv / (1.0 - ADAM_B2 ** ADAM_STEP)
    return -ADAM_LR * (m_hat / (jnp.sqrt(v_hat) + ADAM_EPS) + ADAM_WD * w), nm, nv


def _adamw(w, g, m, v, *, name):
    shape = w.shape
    cols = shape[-1]
    rows = w.size // cols
    w2, g2, m2, v2 = (t.reshape(rows, cols) for t in (w, g, m, v))
    if rows % 8 == 0 or cols % 128 != 0:
        tr, tc = (_tile(rows, 256, 8) if rows % 8 == 0 else rows), cols
    else:
        tr, tc = rows, _tile(cols, 256, 128)

    def body(w_ref, g_ref, m_ref, v_ref, d_ref, nm_ref, nv_ref):
        d_ref[...], nm_ref[...], nv_ref[...] = _adamw_update(w_ref[...], g_ref[...], m_ref[...], v_ref[...])

    blk = pl.BlockSpec((tr, tc), lambda i, j: (i, j))
    outs = pl.pallas_call(
        body, name=name, grid=(rows // tr, cols // tc), in_specs=[blk] * 4, out_specs=[blk] * 3,
        out_shape=[jax.ShapeDtypeStruct((rows, cols), F32)] * 3, compiler_params=_params("parallel", "parallel"),
    )(w2, g2, m2, v2)
    return tuple(t.reshape(shape) for t in outs)


def _sum_slabs(x, *, name):
    _, R, C = x.shape
    sub = 16 if x.dtype == BF16 else 8
    if R % sub == 0:
        tr, tc = _tile(R, 128, sub), C
    else:
        tr, tc = R, _tile(C, 256, 128)

    def body(x_ref, o_ref):
        acc = x_ref[0].astype(F32)
        for s in range(1, N_DEV):
            acc = acc + x_ref[s].astype(F32)
        o_ref[...] = acc

    return pl.pallas_call(
        body, name=name, grid=(R // tr, C // tc),
        in_specs=[pl.BlockSpec((N_DEV, tr, tc), lambda i, j: (0, i, j))],
        out_specs=pl.BlockSpec((tr, tc), lambda i, j: (i, j)),
        out_shape=jax.ShapeDtypeStruct((R, C), F32), compiler_params=_params("parallel", "parallel"),
    )(x)


def _sum_adamw(x, w, m, v, *, name):
    _, R, C = x.shape
    if R % 16 == 0:
        tr, tc = _tile(R, 128, 16), C
    else:
        tr, tc = R, _tile(C, 256, 128)

    def body(x_ref, w_ref, m_ref, v_ref, g_ref, d_ref, nm_ref, nv_ref):
        g = x_ref[0].astype(F32)
        for s in range(1, N_DEV):
            g = g + x_ref[s].astype(F32)
        g_ref[...] = g
        d_ref[...], nm_ref[...], nv_ref[...] = _adamw_update(w_ref[...], g, m_ref[...], v_ref[...])

    blk = pl.BlockSpec((tr, tc), lambda i, j: (i, j))
    return pl.pallas_call(
        body, name=name, grid=(R // tr, C // tc),
        in_specs=[pl.BlockSpec((N_DEV, tr, tc), lambda i, j: (0, i, j)), blk, blk, blk], out_specs=[blk] * 4,
        out_shape=[jax.ShapeDtypeStruct((R, C), F32)] * 4, compiler_params=_params("parallel", "parallel"),
    )(x, w, m, v)


def _peers():
    x, y, c = lax.axis_index("x"), lax.axis_index("y"), lax.axis_index("c")
    me = 4 * x + 2 * y + c
    peers = []
    for k in range(1, N_DEV):
        px = 1 - x if k & 4 else x
        py = 1 - y if k & 2 else y
        pc = 1 - c if k & 1 else c
        peers.append(((px, py, pc), 4 * px + 2 * py + pc))
    return me, peers


_HBM = pl.BlockSpec(memory_space=pltpu.HBM)
_SEM = pl.BlockSpec(memory_space=pltpu.SEMAPHORE)
_EFFECT = pltpu.SideEffectType.DATAFLOW_SIDE_EFFECTING


PLAN_GATHER = tuple((k, "x", 0) for k in range(1, N_DEV))
PLAN_SCATTER = tuple((k, "xk", 0) for k in range(1, N_DEV))
PLAN_GATHER_CHIPS = tuple((k, "x", 0) for k in (1, 2, 4, 6))
PLAN_GATHER_PASS_ON = tuple((1, ("land", q), q) for q in (2, 4, 6))


def _plan_refs(plan, j, x_ref, land_ref, me, peers, receiving):
    k, source, r = plan[j]
    index_of = lambda q: me if q == 0 else peers[q - 1][1]
    pos, target = peers[k - 1]
    if source == "x":
        src = x_ref
    elif source == "xk":
        src = x_ref.at[target]
    else:
        src = land_ref.at[index_of(source[1])]
    return pos, src, land_ref.at[index_of(k ^ r) if receiving else index_of(r)]


def _exchange_start(x, *, plan, name, after=None, land=None, slab=None):
    n_after = 0 if after is None else 1
    n = len(plan)

    def body(*refs):
        x_ref, land_ref, send_sems, recv_sems, _, _, token = refs[n_after:]
        me, peers = _peers()
        for j in range(n):
            pos, src, dst = _plan_refs(plan, j, x_ref, land_ref, me, peers, receiving=False)
            pltpu.make_async_remote_copy(src_ref=src, dst_ref=dst, send_sem=send_sems.at[j], recv_sem=recv_sems.at[j],
                                         device_id=pos, device_id_type=pl.DeviceIdType.MESH).start()
        token[...] = jnp.zeros_like(token)

    if land is None:
        land = lax.empty((N_DEV,) + tuple(slab), x.dtype)
    return pl.pallas_call(
        body, name=name,
        out_shape=(pltpu.SemaphoreType.DMA((n,)), pltpu.SemaphoreType.DMA((n,)),
                   pltpu.HBM(x.shape, x.dtype), pltpu.HBM(land.shape, land.dtype), jax.ShapeDtypeStruct((8, 128), F32)),
        in_specs=[_ANY] * n_after + [_HBM, _HBM],
        out_specs=(_SEM, _SEM, _HBM, _HBM, pl.BlockSpec(memory_space=pltpu.VMEM)),
        input_output_aliases={n_after: 2, n_after + 1: 3},
        compiler_params=pltpu.CompilerParams(has_side_effects=_EFFECT),
    )(*((after,) if n_after else ()), pltpu.with_memory_space_constraint(x, pltpu.HBM),
      pltpu.with_memory_space_constraint(land, pltpu.HBM))


def _exchange_start_many(items, *, name):
    n_items = len(items)
    lands = [lax.empty((N_DEV,) + tuple(slab), x.dtype) for x, _, slab in items]

    def body(*refs):
        ins, outs = refs[:2 * n_items], refs[2 * n_items:]
        me, peers = _peers()
        for i, (_, plan, _) in enumerate(items):
            x_ref, land_ref = ins[2 * i], ins[2 * i + 1]
            send_sems, recv_sems = outs[4 * i], outs[4 * i + 1]
            for j in range(len(plan)):
                pos, src, dst = _plan_refs(plan, j, x_ref, land_ref, me, peers, receiving=False)
                pltpu.make_async_remote_copy(src_ref=src, dst_ref=dst, send_sem=send_sems.at[j], recv_sem=recv_sems.at[j],
                                             device_id=pos, device_id_type=pl.DeviceIdType.MESH).start()
        outs[-1][...] = jnp.zeros_like(outs[-1])

    out_shape, out_specs, operands, aliases = [], [], [], {}
    for i, ((x, plan, _), land) in enumerate(zip(items, lands)):
        out_shape += [pltpu.SemaphoreType.DMA((len(plan),)), pltpu.SemaphoreType.DMA((len(plan),)),
                      pltpu.HBM(x.shape, x.dtype), pltpu.HBM(land.shape, land.dtype)]
        out_specs += [_SEM, _SEM, _HBM, _HBM]
        operands += [pltpu.with_memory_space_constraint(x, pltpu.HBM), pltpu.with_memory_space_constraint(land, pltpu.HBM)]
        aliases[2 * i], aliases[2 * i + 1] = 4 * i + 2, 4 * i + 3
    res = pl.pallas_call(
        body, name=name,
        out_shape=tuple(out_shape) + (jax.ShapeDtypeStruct((8, 128), F32),),
        in_specs=[_HBM] * (2 * n_items), out_specs=tuple(out_specs) + (pl.BlockSpec(memory_space=pltpu.VMEM),),
        input_output_aliases=aliases, compiler_params=pltpu.CompilerParams(has_side_effects=_EFFECT),
    )(*operands)
    return [tuple(res[4 * i:4 * i + 4]) + (res[-1],) for i in range(n_items)]


def _exchange_wait(handle, after, *, plan, name):
    send_sems, recv_sems, x_thru, land_thru, _ = handle
    afters = list(after) if isinstance(after, (list, tuple)) else [after]

    def body(x_ref, land_ref, send_sems, recv_sems, *rest):
        me, peers = _peers()
        for j in range(len(plan)):
            pos, src, dst = _plan_refs(plan, j, x_ref, land_ref, me, peers, receiving=True)
            cp = pltpu.make_async_remote_copy(src_ref=src, dst_ref=dst, send_sem=send_sems.at[j], recv_sem=recv_sems.at[j],
                                              device_id=pos, device_id_type=pl.DeviceIdType.MESH)
            cp.wait_send()
            cp.wait_recv()

    return pl.pallas_call(
        body, name=name,
        out_shape=(pltpu.HBM(x_thru.shape, x_thru.dtype), pltpu.HBM(land_thru.shape, land_thru.dtype)),
        in_specs=[_HBM, _HBM, _SEM, _SEM] + [_ANY] * len(afters), out_specs=(_HBM, _HBM),
        input_output_aliases={0: 0, 1: 1}, compiler_params=pltpu.CompilerParams(has_side_effects=_EFFECT),
    )(x_thru, land_thru, send_sems, recv_sems, *afters)


def _to_proj_rows(t):
    z = jnp.zeros((D_PROJ - C_SM - 2 * GDN_HEADS - GLA_RANK,) + t.shape[1:], t.dtype)
    return jnp.concatenate([t[R_Z:R_A], t[R_GR:R_LR], t[R_GQ:R_GR], t[:R_Z], t[R_A:R_GQ], t[R_LR:], z], axis=0)


def _from_proj_rows(t):
    ab = C_SM + 2 * GDN_HEADS
    return jnp.concatenate([t[C_QKV:C_SM], t[C_Z:C_GR], t[C_SM:ab], t[C_GQ:C_QKV], t[C_GR:C_GQ],
                            t[ab:ab + GLA_RANK]], axis=0)


def _local_step(x, target, meta, attn_nw, conv_w, a_log, dt_bias, gdn_nw, w2, b2, gla_nw, ffn_nw, final_nw,
                fetch, emit, start=None):
    head = jnp.concatenate([jnp.zeros((ROW_PAD, D_MODEL), F32), meta], axis=0)
    conv_w8 = jnp.concatenate([conv_w, jnp.zeros((8 - CONV_K, conv_w.shape[1]), F32)], axis=0)
    w2p = jnp.zeros((SM_W, GLA_QK), F32).at[2 * GDN_HEADS:2 * GDN_HEADS + GLA_RANK].set(w2)
    alog_p = jnp.zeros((1, SM_W), F32).at[:, :GDN_HEADS].set(a_log)
    dt_p = jnp.zeros((1, SM_W), F32).at[:, :GDN_HEADS].set(dt_bias)

    h0, n1 = _embed_norm(head, x, attn_nw, name="attn_norm", after=start)
    w_in_t = fetch("w_in_t", (n1, conv_w8, w2p, alog_p, dt_p))
    proj, sm = _in_proj(n1, w_in_t, name="in_proj")
    gb, la = _gates_fwd(sm, w2p, b2, alog_p, dt_p, name="gates")
    act = _prep_fwd(proj, conv_w8, name="gdn_prep")
    o_gdn, s_gdn, t_gdn = _gdn_fwd(act, gb, name="gdn_fwd")
    o_gla, s_gla = _gla_fwd(proj, la, name="gla_fwd")
    mixed = _mix_fwd(o_gdn, o_gla, proj, gdn_nw, gla_nw, name="mix")
    w_out = fetch("w_out", mixed)
    h1 = _matmul(mixed, w_out, mode="nn", add=h0, name="out_proj")
    n2 = _rmsnorm_fwd(h1, ffn_nw, name="ffn_norm")
    w_gate_t, w_up_t = fetch("w_gate_t", n2), fetch("w_up_t", n2)
    gate, up, hid = _swiglu_fwd(n2, w_gate_t, w_up_t, name="swiglu")
    w_down = fetch("w_down", hid)
    h2 = _matmul(hid, w_down, mode="nn", add=h1, name="ffn_down", tm=1376, tn=256)
    dh2, dh2_b, d_final_nw, loss = _loss_head(h2, final_nw, target, name="loss_head")

    wg = dict(mode="tn", out_dtype=BF16, tn=512)
    tok = emit("w_down", _matmul(hid, dh2_b, name="d_w_down", tm=704, **wg))
    d_gate, d_up = _swiglu_bwd(dh2_b, w_down, gate, up, name="d_swiglu", after=tok)
    tok = emit("w_gate_t", _matmul(d_gate, n2, name="d_w_gate", tm=704, **wg))
    tok = emit("w_up_t", _matmul(d_up, n2, name="d_w_up", tm=704, after=tok, **wg))
    d_n2 = _matmul_pair(d_gate, w_gate_t, d_up, w_up_t, name="d_n2", after=tok)
    dh1, dh1_b, d_ffn_nw = _rmsnorm_bwd(h1, ffn_nw, d_n2, dh2, name="d_ffn_norm")

    tok = emit("w_out", _matmul(mixed, dh1_b, name="d_w_out", tm=512, **wg))
    d_mixed = _matmul(dh1_b, w_out, mode="nt", name="d_mixed", after=tok)
    do_gdn, do_gla, d_proj, d_gdn_nw, d_gla_nw = _mix_bwd(o_gdn, o_gla, proj, gdn_nw, gla_nw, d_mixed, name="d_mix")
    d_proj, d_la = _gla_bwd(proj, la, do_gla, s_gla, d_proj, name="gla_bwd")
    dact, dgb_heads = _gdn_bwd(act, gb, do_gdn, s_gdn, t_gdn, name="gdn_bwd")
    d_proj, d_w2p, d_b2, d_alog, d_dt = _gates_bwd(sm, w2p, b2, alog_p, dt_p, dgb_heads, d_la, d_proj, name="d_gates")
    d_proj, d_conv_w8 = _prep_bwd(proj, conv_w8, dact, d_proj, name="d_gdn_prep")
    tok = emit("w_in_t", _matmul(d_proj, n1, name="d_w_in", tm=768, **wg))
    d_n1 = _matmul(d_proj, w_in_t, mode="nn", name="d_n1", tm=688, after=tok)
    grad_x, d_head, d_attn_nw = _embed_norm_bwd(h0, attn_nw, d_n1, dh1, name="d_attn_norm")

    return dict(
        loss=loss[0, 0], grad_x=grad_x, meta=d_head[ROW_PAD:HEAD_ROWS], attn_nw=d_attn_nw,
        conv_w=d_conv_w8[:CONV_K], a_log=d_alog[:, :GDN_HEADS], dt_bias=d_dt[:, :GDN_HEADS], gdn_nw=d_gdn_nw,
        w2=d_w2p[2 * GDN_HEADS:2 * GDN_HEADS + GLA_RANK], b2=d_b2, gla_nw=d_gla_nw, ffn_nw=d_ffn_nw,
        final_nw=d_final_nw)


SMALL_ROWS = 32


def kernel(x, meta_tokens, attn_norm_w, w_in, gdn_conv_w, gdn_a_log, gdn_dt_bias, gdn_norm_w, gla_gate_w2, gla_gate_b, gla_norm_w, w_out, ffn_norm_w, w_gate, w_up, w_down, final_norm_w, loss_target, m_meta_tokens, m_attn_norm_w, m_w_in, m_gdn_conv_w, m_gdn_a_log, m_gdn_dt_bias, m_gdn_norm_w, m_gla_gate_w2, m_gla_gate_b, m_gla_norm_w, m_w_out, m_ffn_norm_w, m_w_gate, m_w_up, m_w_down, m_final_norm_w, v_meta_tokens, v_attn_norm_w, v_w_in, v_gdn_conv_w, v_gdn_a_log, v_gdn_dt_bias, v_gdn_norm_w, v_gla_gate_w2, v_gla_gate_b, v_gla_norm_w, v_w_out, v_ffn_norm_w, v_w_gate, v_w_up, v_w_down, v_final_norm_w):
    me = 4 * lax.axis_index("x") + 2 * lax.axis_index("y") + lax.axis_index("c")

    n_conv = gdn_conv_w.shape[2]
    n_w2 = gla_gate_w2.shape[2]
    n_meta = meta_tokens.shape[1]
    small = jnp.zeros((40, n_conv), F32)
    small = small.at[0:N_META, :n_meta].set(meta_tokens)
    small = small.at[N_META:N_META + CONV_K, :].set(gdn_conv_w[0])
    small = small.at[24:24 + GLA_RANK, :n_w2].set(gla_gate_w2[0])

    w_in_slab = w_in[0].T.astype(BF16)
    items = [(small, PLAN_GATHER, small.shape), (w_in_slab, PLAN_GATHER_CHIPS, w_in_slab.shape)]
    wnames = ("w_out", "w_gate_t", "w_up_t", "w_down")
    for slab in (w_out[0], w_gate[0].T, w_up[0].T, w_down[0]):
        items.append((slab.astype(BF16), PLAN_GATHER, slab.shape))
    started = _exchange_start_many(items, name="gather_weights_start")
    small_h, in_h = started[0], started[1]
    handles = dict(zip(wnames, started[2:]))
    tok = small_h[4]

    own, small_all = _exchange_wait(small_h, tok, plan=PLAN_GATHER, name="gather_small_wait")
    small_all = lax.dynamic_update_index_in_dim(small_all, own, me, 0)
    meta_f = small_all[:, 0:N_META, :n_meta].transpose(1, 0, 2).reshape(N_META, D_MODEL)
    conv_f = small_all[:, N_META:N_META + CONV_K, :].transpose(1, 0, 2).reshape(CONV_K, N_DEV * n_conv)
    w2_f = small_all[:, 24:24 + GLA_RANK, :n_w2].transpose(1, 0, 2).reshape(GLA_RANK, N_DEV * n_w2)

    def fetch(name, after):
        if name == "w_in_t":
            own, got = _exchange_wait(in_h, after, plan=PLAN_GATHER_CHIPS, name="gather_w_in_wait")
            pass_h = _exchange_start(own, plan=PLAN_GATHER_PASS_ON, land=got, name="pass_w_in_start")
            own, got = _exchange_wait(pass_h, pass_h[4], plan=PLAN_GATHER_PASS_ON, name="pass_w_in_wait")
            got = lax.dynamic_update_index_in_dim(got, own, me, 0)
            return _to_proj_rows(got.reshape(D_IN, D_MODEL))
        own, got = _exchange_wait(handles[name], after, plan=PLAN_GATHER, name="gather_" + name + "_wait")
        got = lax.dynamic_update_index_in_dim(got, own, me, 0)
        return got.reshape(N_DEV * got.shape[1], D_MODEL)

    sent = {}

    def emit(name, grad):
        if name == "w_in_t":
            grad = _from_proj_rows(grad)
        parts = grad.reshape(N_DEV, grad.shape[0] // N_DEV, D_MODEL)
        sent[name] = _exchange_start(parts, plan=PLAN_SCATTER, slab=parts.shape[1:], name="scatter_" + name + "_start")
        return sent[name][4]

    g = _local_step(x[0], loss_target[0], meta_f, attn_norm_w, conv_f, gdn_a_log, gdn_dt_bias, gdn_norm_w, w2_f,
                    gla_gate_b, gla_norm_w, ffn_norm_w, final_norm_w.reshape(1, D_MODEL), fetch, emit, start=tok)

    misc = jnp.concatenate([g["a_log"], g["dt_bias"], g["gdn_nw"], g["gla_nw"], g["b2"], g["loss"].reshape(1, 1)], axis=1)
    n_misc = misc.shape[1]
    misc = jnp.pad(misc, ((0, 0), (0, D_MODEL - n_misc)))
    rows = jnp.concatenate([g["attn_nw"], g["ffn_nw"], g["final_nw"], misc, g["meta"],
                            g["conv_w"].reshape(-1, D_MODEL), g["w2"].reshape(-1, D_MODEL)], axis=0)
    rows = jnp.pad(rows, ((0, SMALL_ROWS - rows.shape[0]), (0, 0)))
    rows_h = _exchange_start(rows, plan=PLAN_GATHER, slab=rows.shape, name="gather_small_grads_start")

    big = {}
    after = rows_h[4]
    for name, w, m, v, transposed in (("w_down", w_down, m_w_down, v_w_down, False), ("w_gate_t", w_gate, m_w_gate, v_w_gate, True),
                                      ("w_up_t", w_up, m_w_up, v_w_up, True), ("w_out", w_out, m_w_out, v_w_out, False),
                                      ("w_in_t", w_in, m_w_in, v_w_in, True)):
        own, got = _exchange_wait(sent[name], after, plan=PLAN_SCATTER, name="scatter_" + name + "_wait")
        got = lax.dynamic_update_index_in_dim(got, lax.dynamic_index_in_dim(own, me, 0, keepdims=False), me, 0)
        local = [t[0].T if transposed else t[0] for t in (w, m, v)]
        res = _sum_adamw(got, *local, name="adamw_" + name)
        big[name] = [t.T[None] if transposed else t[None] for t in res]
        after = res[0]

    own, got = _exchange_wait(rows_h, after, plan=PLAN_GATHER, name="gather_small_grads_wait")
    tot = _sum_slabs(lax.dynamic_update_index_in_dim(got, own, me, 0), name="sum_small_grads")
    grad_attn_nw, grad_ffn_nw, grad_final_nw = tot[0:1], tot[1:2], tot[2]
    grad_a_log = tot[3:4, 0:8]
    grad_dt = tot[3:4, 8:16]
    grad_gdn_nw = tot[3:4, 16:16 + GDN_DV]
    grad_gla_nw = tot[3:4, 144:144 + GLA_DV]
    grad_b2 = tot[3:4, 400:400 + GLA_QK]
    loss = tot[3, n_misc - 1]
    r0 = 4 + N_META
    grad_meta = lax.dynamic_slice(tot[4:r0], (0, me * n_meta), (N_META, n_meta))
    r1 = r0 + CONV_K * N_DEV * n_conv // D_MODEL
    grad_conv = lax.dynamic_slice(tot[r0:r1].reshape(CONV_K, N_DEV * n_conv), (0, me * n_conv), (CONV_K, n_conv))[None]
    r2 = r1 + GLA_RANK * N_DEV * n_w2 // D_MODEL
    grad_w2 = lax.dynamic_slice(tot[r1:r2].reshape(GLA_RANK, N_DEV * n_w2), (0, me * n_w2), (GLA_RANK, n_w2))[None]

    weights = [meta_tokens, attn_norm_w, w_in, gdn_conv_w, gdn_a_log, gdn_dt_bias, gdn_norm_w, gla_gate_w2,
               gla_gate_b, gla_norm_w, w_out, ffn_norm_w, w_gate, w_up, w_down, final_norm_w]
    grads = [grad_meta, grad_attn_nw, "w_in_t", grad_conv, grad_a_log, grad_dt, grad_gdn_nw, grad_w2,
             grad_b2, grad_gla_nw, "w_out", grad_ffn_nw, "w_gate_t", "w_up_t", "w_down", grad_final_nw]
    ms = [m_meta_tokens, m_attn_norm_w, m_w_in, m_gdn_conv_w, m_gdn_a_log, m_gdn_dt_bias, m_gdn_norm_w,
          m_gla_gate_w2, m_gla_gate_b, m_gla_norm_w, m_w_out, m_ffn_norm_w, m_w_gate, m_w_up, m_w_down, m_final_norm_w]
    vs = [v_meta_tokens, v_attn_norm_w, v_w_in, v_gdn_conv_w, v_gdn_a_log, v_gdn_dt_bias, v_gdn_norm_w,
          v_gla_gate_w2, v_gla_gate_b, v_gla_norm_w, v_w_out, v_ffn_norm_w, v_w_gate, v_w_up, v_w_down, v_final_norm_w]
    outs = [[], [], [], []]
    for idx, (w, gr, m, v) in enumerate(zip(weights, grads, ms, vs)):
        if isinstance(gr, str):
            res = big[gr]
        else:
            gr = gr.reshape(w.shape)
            res = (gr,) + _adamw(w, gr, m, v, name=f"adamw_{idx}")
        for lst, t in zip(outs, res):
            lst.append(t)
    return (loss, g["grad_x"][None], *outs[0], *outs[1], *outs[2], *outs[3])
```

```python
import functools

import jax
import jax.numpy as jnp
from jax import lax
from jax.experimental import pallas as pl
from jax.experimental.pallas import tpu as pltpu

F32 = jnp.float32
BF16 = jnp.bfloat16
_MXU_DTYPE = jnp.bfloat16

D_MODEL = 2048
N_META = 16
ROW_PAD = 48
HEAD_ROWS = ROW_PAD + N_META
CONV_K = 4
GDN_HEADS, GDN_DK, GDN_DV, GDN_CHUNK = 8, 128, 128, 64
GLA_HEADS, GLA_DK, GLA_DV, GLA_CHUNK = 4, 128, 256, 16
GLA_RANK = 16
GLA_GATE_NORMALIZER = 16.0
GDN_QK = GDN_HEADS * GDN_DK
GDN_V = GDN_HEADS * GDN_DV
GLA_QK = GLA_HEADS * GLA_DK
GLA_V = GLA_HEADS * GLA_DV
D_FF = 5632
D_IN = 7200
NORM_EPS = 1e-6
C_Z, C_GR, C_GQ, C_GK, C_GV, C_QKV, C_SM = 0, 1024, 2048, 2560, 3072, 4096, 7168
SM_W = 128
D_PROJ = 7680
R_Z, R_A, R_B, R_GQ, R_GK, R_GV, R_GR, R_LR = 3072, 4096, 4104, 4112, 4624, 5136, 6160, 7184

ADAM_LR, ADAM_B1, ADAM_B2, ADAM_EPS, ADAM_WD, ADAM_STEP = 0.001, 0.9, 0.999, 1e-08, 0.01, 10

N_DEV = 8
VMEM_LIMIT = 56 * 1024 * 1024

NN = (((1,), (0,)), ((), ()))
NT = (((1,), (1,)), ((), ()))
TN = (((0,), (0,)), ((), ()))


def _dot(a, b, dims=NN):
    return lax.dot_general(a.astype(_MXU_DTYPE), b.astype(_MXU_DTYPE), dims, preferred_element_type=F32)


def _running_sum(x, reverse=False):
    n = x.shape[0]
    row = lax.broadcasted_iota(jnp.int32, x.shape, 0)
    s = 1
    while s < n:
        if reverse:
            x = x + jnp.where(row < n - s, pltpu.roll(x, n - s, 0), 0.0)
        else:
            x = x + jnp.where(row >= s, pltpu.roll(x, s, 0), 0.0)
        s *= 2
    return x


def _dot3(a, b):
    ah = a.astype(BF16)
    al = (a - ah.astype(F32)).astype(BF16)
    bh = b.astype(BF16)
    bl = (b - bh.astype(F32)).astype(BF16)
    d = functools.partial(lax.dot_general, dimension_numbers=NN, preferred_element_type=F32)
    return d(ah, bh) + (d(ah, bl) + d(al, bh))


def _tile(n, target, mult=8):
    best = None
    for t in range(mult, min(n, target) + 1, mult):
        if n % t == 0:
            best = t
    return best if best is not None else n


def _params(*sem):
    return pltpu.CompilerParams(dimension_semantics=sem, vmem_limit_bytes=VMEM_LIMIT)


def _sigmoid(x):
    return 0.5 * jnp.tanh(0.5 * x) + 0.5


def _softplus(x):
    return jnp.maximum(x, 0.0) + jnp.log1p(jnp.exp(-jnp.abs(x)))


def _silu_and_grad(c):
    s = _sigmoid(c)
    return c * s, s * (1.0 + c * (1.0 - s))


_ANY = pl.BlockSpec(memory_space=pl.ANY)


def _matmul(a, b, *, mode, name, out_dtype=F32, add=None, after=None, tm=1376, tn=512):
    if mode == "tn":
        K, M = a.shape
        N = b.shape[1]
    else:
        M, K = a.shape
        N = b.shape[0] if mode == "nt" else b.shape[1]
    tm = _tile(M, tm, 128 if mode == "tn" else 16)
    tn = _tile(N, tn, 128)
    dims = {"nn": NN, "nt": NT, "tn": TN}[mode]
    n_after = 0 if after is None else 1

    def body(*refs):
        refs = refs[n_after:]
        r = _dot(refs[0][...], refs[1][...], dims)
        if add is not None:
            r = r + refs[2][...]
        refs[-1][...] = r.astype(out_dtype)

    a_spec = pl.BlockSpec((K, tm), lambda i, j: (0, i)) if mode == "tn" else pl.BlockSpec((tm, K), lambda i, j: (i, 0))
    b_spec = pl.BlockSpec((tn, K), lambda i, j: (j, 0)) if mode == "nt" else pl.BlockSpec((K, tn), lambda i, j: (0, j))
    o_spec = pl.BlockSpec((tm, tn), lambda i, j: (i, j))
    in_specs = [_ANY] * n_after + [a_spec, b_spec] + ([o_spec] if add is not None else [])
    args = ((after,) if n_after else ()) + (a, b) + ((add,) if add is not None else ())
    return pl.pallas_call(
        body, name=name, grid=(M // tm, N // tn), in_specs=in_specs, out_specs=o_spec,
        out_shape=jax.ShapeDtypeStruct((M, N), out_dtype), compiler_params=_params("parallel", "parallel"),
    )(*args)


def _in_proj(n, w_in_t, *, name, tm=1376, tn=512):
    M, K = n.shape
    N = w_in_t.shape[0]
    tm, tn = _tile(M, tm, 16), _tile(N, tn, 128)
    assert C_SM % tn == 0
    j_small = C_SM // tn

    def body(n_ref, w_ref, o_ref, sm_ref):
        r = _dot(n_ref[...], w_ref[...], NT)
        o_ref[...] = r.astype(o_ref.dtype)

        @pl.when(pl.program_id(1) == j_small)
        def _():
            sm_ref[...] = r[:, 0:SM_W]

    return pl.pallas_call(
        body, name=name, grid=(M // tm, N // tn),
        in_specs=[pl.BlockSpec((tm, K), lambda i, j: (i, 0)), pl.BlockSpec((tn, K), lambda i, j: (j, 0))],
        out_specs=[pl.BlockSpec((tm, tn), lambda i, j: (i, j)), pl.BlockSpec((tm, SM_W), lambda i, j: (i, 0))],
        out_shape=[jax.ShapeDtypeStruct((M, N), BF16), jax.ShapeDtypeStruct((M, SM_W), F32)],
        compiler_params=_params("parallel", "arbitrary"),
    )(n, w_in_t)


def _matmul_pair(a1, b1, a2, b2, *, name, after=None, tm=688, tn=256):
    M, K = a1.shape
    N = b1.shape[1]
    tm, tn = _tile(M, tm, 16), _tile(N, tn, 128)
    n_after = 0 if after is None else 1

    def body(*refs):
        a1_ref, b1_ref, a2_ref, b2_ref, o_ref = refs[n_after:]
        o_ref[...] = _dot(a1_ref[...], b1_ref[...]) + _dot(a2_ref[...], b2_ref[...])

    a_spec = pl.BlockSpec((tm, K), lambda i, j: (i, 0))
    b_spec = pl.BlockSpec((K, tn), lambda i, j: (0, j))
    return pl.pallas_call(
        body, name=name, grid=(M // tm, N // tn), in_specs=[_ANY] * n_after + [a_spec, b_spec, a_spec, b_spec],
        out_specs=pl.BlockSpec((tm, tn), lambda i, j: (i, j)), out_shape=jax.ShapeDtypeStruct((M, N), F32),
        compiler_params=_params("parallel", "parallel"),
    )(*((after,) if n_after else ()), a1, b1, a2, b2)


def _rmsnorm_fwd(h, w, *, name):
    M, D = h.shape
    tm = _tile(M, 688, 16)

    def body(h_ref, w_ref, n_ref):
        x = h_ref[...]
        r = lax.rsqrt(jnp.mean(x * x, axis=-1, keepdims=True) + NORM_EPS)
        n_ref[...] = (x * r * w_ref[...]).astype(n_ref.dtype)

    return pl.pallas_call(
        body, name=name, grid=(M // tm,),
        in_specs=[pl.BlockSpec((tm, D), lambda i: (i, 0)), pl.BlockSpec((1, D), lambda i: (0, 0))],
        out_specs=pl.BlockSpec((tm, D), lambda i: (i, 0)),
        out_shape=jax.ShapeDtypeStruct((M, D), BF16),
        compiler_params=_params("parallel"),
    )(h, w)


SEQ_BLOCK = HEAD_ROWS


def _seq_blocks_per_tile(rows):
    n = rows // SEQ_BLOCK
    return max(m for m in (1, 2, 3, 4) if n % m == 0)


def _seq_specs(m, D):
    return [pl.BlockSpec((SEQ_BLOCK, D), functools.partial(lambda i, k: (jnp.maximum(m * i + k - 1, 0), 0), k=k))
            for k in range(m)]


def _embed_norm(head, x, w, *, name, after=None):
    S, D = x.shape
    m = _seq_blocks_per_tile(S + HEAD_ROWS)
    n_after = 0 if after is None else 1

    def body(*refs):
        refs = refs[n_after:]
        head_ref, x_refs, w_ref, h_ref, n_ref = refs[0], refs[1:1 + m], refs[1 + m], refs[2 + m], refs[3 + m]
        i = pl.program_id(0)
        for k in range(m):
            blk = x_refs[k][...]
            if k == 0:
                blk = jnp.where(i == 0, head_ref[...], blk)
            rows = slice(k * SEQ_BLOCK, (k + 1) * SEQ_BLOCK)
            h_ref[rows, :] = blk
            r = lax.rsqrt(jnp.mean(blk * blk, axis=-1, keepdims=True) + NORM_EPS)
            n_ref[rows, :] = (blk * r * w_ref[...]).astype(n_ref.dtype)

    tile = pl.BlockSpec((m * SEQ_BLOCK, D), lambda i: (i, 0))
    return pl.pallas_call(
        body, name=name, grid=((S + HEAD_ROWS) // (m * SEQ_BLOCK),),
        in_specs=[_ANY] * n_after + [pl.BlockSpec((SEQ_BLOCK, D), lambda i: (0, 0))] + _seq_specs(m, D)
        + [pl.BlockSpec((1, D), lambda i: (0, 0))],
        out_specs=[tile, tile],
        out_shape=[jax.ShapeDtypeStruct((S + HEAD_ROWS, D), F32), jax.ShapeDtypeStruct((S + HEAD_ROWS, D), BF16)],
        compiler_params=_params("parallel"),
    )(*((after,) if n_after else ()), head, *([x] * m), w)


def _embed_norm_bwd(h, w, dn, dres, *, name):
    M, D = h.shape
    S = M - HEAD_ROWS
    m = _seq_blocks_per_tile(S)
    g = S // (m * SEQ_BLOCK)

    def one(x, dn_, dres_, w_):
        r = lax.rsqrt(jnp.mean(x * x, axis=-1, keepdims=True) + NORM_EPS)
        xhat = x * r
        dxhat = dn_ * w_
        dh = dres_ + r * (dxhat - xhat * jnp.mean(dxhat * xhat, axis=-1, keepdims=True))
        return dh, jnp.sum((dn_ * xhat).reshape(SEQ_BLOCK // 8, 8, D), axis=0)

    def body(*refs):
        w_ref = refs[0]
        groups = [refs[1 + a * (m + 1):1 + (a + 1) * (m + 1)] for a in range(3)]
        gx_ref, dhead_ref, dw_ref, acc_ref = refs[1 + 3 * (m + 1):]
        i = pl.program_id(0)
        w_ = w_ref[...]
        part = jnp.zeros((8, D), F32)
        for k in range(m):
            dh, p = one(*(grp[1 + k][...] for grp in groups), w_)
            gx_ref[k * SEQ_BLOCK:(k + 1) * SEQ_BLOCK, :] = dh
            part = part + p

        @pl.when(i == 0)
        def _():
            dh, p = one(*(grp[0][...] for grp in groups), w_)
            dhead_ref[...] = dh
            acc_ref[...] = part + p

        @pl.when(i > 0)
        def _():
            acc_ref[...] += part

        @pl.when(i == g - 1)
        def _():
            dw_ref[...] = jnp.sum(acc_ref[...], axis=0, keepdims=True)

    first = pl.BlockSpec((SEQ_BLOCK, D), lambda i: (0, 0))
    blocks = [pl.BlockSpec((SEQ_BLOCK, D), functools.partial(lambda i, k: (m * i + k + 1, 0), k=k)) for k in range(m)]
    vec = pl.BlockSpec((1, D), lambda i: (0, 0))
    return pl.pallas_call(
        body, name=name, grid=(g,), in_specs=[vec] + ([first] + blocks) * 3,
        out_specs=[pl.BlockSpec((m * SEQ_BLOCK, D), lambda i: (i, 0)), first, vec],
        out_shape=[jax.ShapeDtypeStruct((S, D), F32), jax.ShapeDtypeStruct((SEQ_BLOCK, D), F32),
                   jax.ShapeDtypeStruct((1, D), F32)],
        scratch_shapes=[pltpu.VMEM((8, D), F32)],
        compiler_params=_params("arbitrary"),
    )(w, *([h] * (m + 1)), *([dn] * (m + 1)), *([dres] * (m + 1)))


def _rmsnorm_bwd(h, w, dn, dres, *, name):
    M, D = h.shape
    tm = _tile(M, 344, 16)
    g = M // tm

    def body(h_ref, w_ref, dn_ref, dres_ref, dh_ref, dhb_ref, dw_ref, acc_ref):
        i = pl.program_id(0)
        x = h_ref[...]
        r = lax.rsqrt(jnp.mean(x * x, axis=-1, keepdims=True) + NORM_EPS)
        xhat = x * r
        dn_ = dn_ref[...]
        dxhat = dn_ * w_ref[...]
        dh = dres_ref[...] + r * (dxhat - xhat * jnp.mean(dxhat * xhat, axis=-1, keepdims=True))
        dh_ref[...] = dh
        dhb_ref[...] = dh.astype(dhb_ref.dtype)
        part = jnp.sum((dn_ * xhat).reshape(tm // 8, 8, D), axis=0)

        @pl.when(i == 0)
        def _():
            acc_ref[...] = part

        @pl.when(i > 0)
        def _():
            acc_ref[...] += part

        @pl.when(i == g - 1)
        def _():
            dw_ref[...] = jnp.sum(acc_ref[...], axis=0, keepdims=True)

    row = pl.BlockSpec((tm, D), lambda i: (i, 0))
    vec = pl.BlockSpec((1, D), lambda i: (0, 0))
    return pl.pallas_call(
        body, name=name, grid=(g,), in_specs=[row, vec, row, row],
        out_specs=[row, row, vec],
        out_shape=[jax.ShapeDtypeStruct((M, D), F32), jax.ShapeDtypeStruct((M, D), BF16),
                   jax.ShapeDtypeStruct((1, D), F32)],
        scratch_shapes=[pltpu.VMEM((8, D), F32)],
        compiler_params=_params("arbitrary"),
    )(h, w, dn, dres)


def _loss_head(h, w, target, *, name):
    M, D = h.shape
    m = _seq_blocks_per_tile(M)
    tm = m * SEQ_BLOCK
    g = M // tm

    def body(h_ref, w_ref, *rest):
        t_refs = rest[:m]
        dh_ref, dhb_ref, dw_ref, loss_ref, acc_ref, lacc_ref = rest[m:]
        i = pl.program_id(0)
        x = h_ref[...]
        row = i * tm + lax.broadcasted_iota(jnp.int32, (tm, 1), 0)
        live = row >= HEAD_ROWS
        r = lax.rsqrt(jnp.mean(x * x, axis=-1, keepdims=True) + NORM_EPS)
        xhat = x * r
        t = jnp.concatenate([t_ref[...] for t_ref in t_refs], axis=0)
        err = jnp.where(live, xhat * w_ref[...] - t, 0.0)
        dy = err * (1.0 / D)
        dxhat = dy * w_ref[...]
        dh = r * (dxhat - xhat * jnp.mean(dxhat * xhat, axis=-1, keepdims=True))
        dh_ref[...] = dh
        dhb_ref[...] = dh.astype(dhb_ref.dtype)
        part = jnp.sum((dy * xhat).reshape(tm // 8, 8, D), axis=0)
        lpart = jnp.sum((err * err).reshape(tm // 8, 8, D), axis=0)

        @pl.when(i == 0)
        def _():
            acc_ref[...] = part
            lacc_ref[...] = lpart

        @pl.when(i > 0)
        def _():
            acc_ref[...] += part
            lacc_ref[...] += lpart

        @pl.when(i == g - 1)
        def _():
            dw_ref[...] = jnp.sum(acc_ref[...], axis=0, keepdims=True)
            tot = jnp.sum(jnp.sum(lacc_ref[...], axis=0, keepdims=True), axis=1, keepdims=True)
            loss_ref[...] = jnp.broadcast_to(tot * (0.5 / D), (1, 128))

    row = pl.BlockSpec((tm, D), lambda i: (i, 0))
    vec = pl.BlockSpec((1, D), lambda i: (0, 0))
    return pl.pallas_call(
        body, name=name, grid=(g,), in_specs=[row, vec] + _seq_specs(m, D),
        out_specs=[row, row, vec, pl.BlockSpec((1, 128), lambda i: (0, 0))],
        out_shape=[jax.ShapeDtypeStruct((M, D), F32), jax.ShapeDtypeStruct((M, D), BF16),
                   jax.ShapeDtypeStruct((1, D), F32), jax.ShapeDtypeStruct((1, 128), F32)],
        scratch_shapes=[pltpu.VMEM((8, D), F32), pltpu.VMEM((8, D), F32)],
        compiler_params=_params("arbitrary"),
    )(h, w, *([target] * m))


def _gate_terms(sm, w2p, b2, alog_p, dt_p, row0):
    tm = sm.shape[0]
    lane = lax.broadcasted_iota(jnp.int32, (tm, SM_W), 1)
    live = (row0 + lax.broadcasted_iota(jnp.int32, (tm, 1), 0)) >= ROW_PAD
    pre = sm + dt_p
    neg_a = -jnp.exp(alog_p)
    g = neg_a * _softplus(pre)
    beta = _sigmoid(sm)
    z = _dot(sm, w2p) + b2
    return lane, live, pre, neg_a, g, beta, z


def _gates_fwd(sm, w2p, b2, alog_p, dt_p, *, name):
    M = sm.shape[0]
    tm = _tile(M, 688, 8)

    def body(sm_ref, w2_ref, b2_ref, al_ref, dt_ref, gb_ref, la_ref):
        row0 = pl.program_id(0) * tm
        lane, live, _, _, g, beta, z = _gate_terms(sm_ref[...].astype(F32), w2_ref[...], b2_ref[...], al_ref[...], dt_ref[...], row0)
        gb = jnp.where(lane < GDN_HEADS, g, jnp.where(lane < 2 * GDN_HEADS, beta, 0.0))
        gb_ref[...] = jnp.where(live, gb, 0.0)
        la = (jnp.minimum(z, 0.0) - jnp.log1p(jnp.exp(-jnp.abs(z)))) * (1.0 / GLA_GATE_NORMALIZER)
        la_ref[...] = jnp.where(live, la, 0.0)

    full = lambda s: pl.BlockSpec(s, lambda i: (0, 0))
    return pl.pallas_call(
        body, name=name, grid=(M // tm,),
        in_specs=[pl.BlockSpec((tm, SM_W), lambda i: (i, 0)), full((SM_W, GLA_QK)), full((1, GLA_QK)),
                  full((1, SM_W)), full((1, SM_W))],
        out_specs=[pl.BlockSpec((tm, SM_W), lambda i: (i, 0)), pl.BlockSpec((tm, GLA_QK), lambda i: (i, 0))],
        out_shape=[jax.ShapeDtypeStruct((M, SM_W), F32), jax.ShapeDtypeStruct((M, GLA_QK), F32)],
        compiler_params=_params("parallel"),
    )(sm, w2p, b2, alog_p, dt_p)


def _gates_bwd(sm, w2p, b2, alog_p, dt_p, dgb_heads, dla, d_proj, *, name):
    M = sm.shape[0]
    tm = _tile(M, 688, 8)
    g_ = M // tm

    tail_w = D_PROJ - C_SM

    def body(sm_ref, w2_ref, b2_ref, al_ref, dt_ref, dgb_ref, dla_ref, _,
             dsm_ref, dw2_ref, db2_ref, dal_ref, ddt_ref):
        i = pl.program_id(0)
        sm = sm_ref[...].astype(F32)
        lane, live, pre, neg_a, g, beta, z = _gate_terms(sm, w2_ref[...], b2_ref[...], al_ref[...], dt_ref[...], i * tm)
        dz = jnp.where(live, dla_ref[...] * (_sigmoid(-z) * (1.0 / GLA_GATE_NORMALIZER)), 0.0)
        dsm_lr = _dot(dz, w2_ref[...], NT)
        dgb = dgb_ref[0]
        for hh in range(1, GDN_HEADS):
            dgb = dgb + dgb_ref[hh]
        dgb = jnp.where(live, dgb, 0.0)
        da = dgb * neg_a * _sigmoid(pre)
        db = dgb * beta * (1.0 - beta)
        dsm = jnp.where(lane < GDN_HEADS, da, jnp.where(lane < 2 * GDN_HEADS, db, dsm_lr))
        dsm_ref[:, 0:SM_W] = dsm.astype(dsm_ref.dtype)
        if tail_w > SM_W:
            dsm_ref[:, SM_W:tail_w] = jnp.zeros((tm, tail_w - SM_W), dsm_ref.dtype)
        is_a = lane < GDN_HEADS
        dal = jnp.sum(jnp.where(is_a, dgb * g, 0.0), axis=0, keepdims=True)
        ddt = jnp.sum(jnp.where(is_a, da, 0.0), axis=0, keepdims=True)
        dw2 = _dot(sm, dz, TN)
        db2 = jnp.sum(dz, axis=0, keepdims=True)

        @pl.when(i == 0)
        def _():
            dw2_ref[...] = dw2
            db2_ref[...] = db2
            dal_ref[...] = dal
            ddt_ref[...] = ddt

        @pl.when(i > 0)
        def _():
            dw2_ref[...] += dw2
            db2_ref[...] += db2
            dal_ref[...] += dal
            ddt_ref[...] += ddt

    full = lambda s: pl.BlockSpec(s, lambda i: (0, 0))
    return pl.pallas_call(
        body, name=name, grid=(g_,),
        in_specs=[pl.BlockSpec((tm, SM_W), lambda i: (i, 0)), full((SM_W, GLA_QK)), full((1, GLA_QK)),
                  full((1, SM_W)), full((1, SM_W)),
                  pl.BlockSpec((GDN_HEADS, tm, SM_W), lambda i: (0, i, 0)),
                  pl.BlockSpec((tm, GLA_QK), lambda i: (i, 0)), _ANY],
        out_specs=[pl.BlockSpec((tm, tail_w), lambda i: (i, C_SM // tail_w)), full((SM_W, GLA_QK)), full((1, GLA_QK)),
                   full((1, SM_W)), full((1, SM_W))],
        out_shape=[jax.ShapeDtypeStruct(d_proj.shape, d_proj.dtype), jax.ShapeDtypeStruct((SM_W, GLA_QK), F32),
                   jax.ShapeDtypeStruct((1, GLA_QK), F32), jax.ShapeDtypeStruct((1, SM_W), F32),
                   jax.ShapeDtypeStruct((1, SM_W), F32)],
        input_output_aliases={7: 0},
        compiler_params=_params("arbitrary"),
    )(sm, w2p, b2, alog_p, dt_p, dgb_heads, dla, d_proj)


QKV_W = GDN_QK
N_QKV_GROUPS = 3
QKV_B0 = C_QKV // QKV_W
HALO = 16


def _conv_terms(x_ref, halo_ref, cw_ref, xs_ref, i, tm):
    xs_ref[HALO:HALO + tm, :] = x_ref[...].astype(F32)
    xs_ref[0:HALO, :] = jnp.where(i > 0, halo_ref[...].astype(F32), 0.0)
    cw = cw_ref[...]
    xs = xs_ref[...]
    taps = [(pltpu.roll(xs, CONV_K - 1 - t, 0) if t < CONV_K - 1 else xs)[HALO:HALO + tm, :] for t in range(CONV_K)]
    c = taps[0] * cw[0:1, :]
    for t in range(1, CONV_K):
        c = c + taps[t] * cw[t:t + 1, :]
    return c, taps


def _prep_fwd(proj, conv_w8, *, name):
    M = proj.shape[0]
    tm = _tile(M, 688, 16)

    def body(x_ref, halo_ref, cw_ref, o_ref, xs_ref):
        j, i = pl.program_id(0), pl.program_id(1)
        c, _ = _conv_terms(x_ref, halo_ref, cw_ref, xs_ref, i, tm)
        s, _ = _silu_and_grad(c)
        scale = jnp.where(j == 0, GDN_DK ** -0.5, 1.0)
        for hh in range(GDN_HEADS):
            cols = slice(hh * 128, (hh + 1) * 128)
            sh = s[:, cols]
            r = lax.rsqrt(jnp.sum(sh * sh, axis=-1, keepdims=True) + NORM_EPS)
            o_ref[:, cols] = jnp.where(j < 2, sh * (r * scale), sh)

    hb = tm // HALO
    return pl.pallas_call(
        body, name=name, grid=(N_QKV_GROUPS, M // tm),
        in_specs=[pl.BlockSpec((tm, QKV_W), lambda j, i: (i, QKV_B0 + j)),
                  pl.BlockSpec((HALO, QKV_W), lambda j, i: (jnp.maximum(i * hb - 1, 0), QKV_B0 + j)),
                  pl.BlockSpec((8, QKV_W), lambda j, i: (0, j))],
        out_specs=pl.BlockSpec((tm, QKV_W), lambda j, i: (i, j)),
        out_shape=jax.ShapeDtypeStruct((M, N_QKV_GROUPS * QKV_W), F32),
        scratch_shapes=[pltpu.VMEM((tm + HALO, QKV_W), F32)],
        compiler_params=_params("parallel", "arbitrary"),
    )(proj, proj, conv_w8)


def _prep_bwd(proj, conv_w8, dact, d_proj, *, name):
    M = proj.shape[0]
    tm = _tile(M, 688, 16)
    g_ = M // tm
    ext = tm + HALO

    def body(x_ref, prev_ref, next_ref, cw_ref, da_ref, dan_ref, _, o_ref, dcw_ref, xs_ref, das_ref, dcs_ref):
        j, i = pl.program_id(0), pl.program_id(1)
        not_last = i < g_ - 1
        xs_ref[0:HALO, :] = jnp.where(i > 0, prev_ref[...].astype(F32), 0.0)
        xs_ref[HALO:HALO + tm, :] = x_ref[...].astype(F32)
        xs_ref[HALO + tm:HALO + ext, :] = jnp.where(not_last, next_ref[...].astype(F32), 0.0)
        das_ref[0:tm, :] = da_ref[...]
        das_ref[tm:ext, :] = jnp.where(not_last, dan_ref[...], 0.0)
        cw = cw_ref[...]
        xs = xs_ref[...]
        taps = [(pltpu.roll(xs, CONV_K - 1 - t, 0) if t < CONV_K - 1 else xs)[HALO:HALO + ext, :] for t in range(CONV_K)]
        c = taps[0] * cw[0:1, :]
        for t in range(1, CONV_K):
            c = c + taps[t] * cw[t:t + 1, :]
        s, ds_dc = _silu_and_grad(c)
        scale = jnp.where(j == 0, GDN_DK ** -0.5, 1.0)
        for hh in range(GDN_HEADS):
            cols = slice(hh * 128, (hh + 1) * 128)
            sh = s[:, cols]
            r = lax.rsqrt(jnp.sum(sh * sh, axis=-1, keepdims=True) + NORM_EPS)
            da = das_ref[:, cols]
            y = sh * r
            dy = da * scale
            ds_norm = r * (dy - y * jnp.sum(dy * y, axis=-1, keepdims=True))
            dcs_ref[:, cols] = jnp.where(j < 2, ds_norm, da) * ds_dc[:, cols]
        dc = dcs_ref[...]
        acc = dc[0:tm, :] * cw[CONV_K - 1:CONV_K, :]
        for t in range(CONV_K - 1):
            acc = acc + pltpu.roll(dc, ext - (CONV_K - 1 - t), 0)[0:tm, :] * cw[t:t + 1, :]
        o_ref[...] = acc.astype(o_ref.dtype)
        r8 = lax.broadcasted_iota(jnp.int32, (8, QKV_W), 0)
        part = jnp.zeros((8, QKV_W), F32)
        for t in range(CONV_K):
            part = jnp.where(r8 == t, jnp.sum(dc[0:tm, :] * taps[t][0:tm, :], axis=0, keepdims=True), part)

        @pl.when(i == 0)
        def _():
            dcw_ref[...] = part

        @pl.when(i > 0)
        def _():
            dcw_ref[...] += part

    hb = tm // HALO
    last = M // HALO - 1
    prev_of = lambda i: jnp.maximum(i * hb - 1, 0)
    next_of = lambda i: jnp.minimum((i + 1) * hb, last)
    return pl.pallas_call(
        body, name=name, grid=(N_QKV_GROUPS, g_),
        in_specs=[pl.BlockSpec((tm, QKV_W), lambda j, i: (i, QKV_B0 + j)),
                  pl.BlockSpec((HALO, QKV_W), lambda j, i: (prev_of(i), QKV_B0 + j)),
                  pl.BlockSpec((HALO, QKV_W), lambda j, i: (next_of(i), QKV_B0 + j)),
                  pl.BlockSpec((8, QKV_W), lambda j, i: (0, j)),
                  pl.BlockSpec((tm, QKV_W), lambda j, i: (i, j)),
                  pl.BlockSpec((HALO, QKV_W), lambda j, i: (next_of(i), j)), _ANY],
        out_specs=[pl.BlockSpec((tm, QKV_W), lambda j, i: (i, QKV_B0 + j)), pl.BlockSpec((8, QKV_W), lambda j, i: (0, j))],
        out_shape=[jax.ShapeDtypeStruct(d_proj.shape, d_proj.dtype),
                   jax.ShapeDtypeStruct((8, N_QKV_GROUPS * QKV_W), F32)],
        input_output_aliases={6: 0},
        scratch_shapes=[pltpu.VMEM((HALO + ext, QKV_W), F32), pltpu.VMEM((ext, QKV_W), F32), pltpu.VMEM((ext, QKV_W), F32)],
        compiler_params=_params("parallel", "arbitrary"),
    )(proj, proj, proj, conv_w8, dact, dact, d_proj)


def _round_robin(gens):
    gens = list(gens)
    while gens:
        alive = []
        for gen in gens:
            try:
                next(gen)
                alive.append(gen)
            except StopIteration:
                pass
        gens = alive


def _unit_lower_inverse(a_low, eye):
    n = a_low.shape[0]
    ri = lax.broadcasted_iota(jnp.int32, (n, n), 0)
    ci = lax.broadcasted_iota(jnp.int32, (n, n), 1)
    same = lambda shift: (ri >> shift) == (ci >> shift)
    b = jnp.where(same(3), -a_low, 0.0)
    x = eye + b
    p2 = _dot3(b, b)
    yield
    x = x + _dot3(x, p2)
    p4 = _dot3(p2, p2)
    yield
    x = x + _dot3(x, p4)
    yield
    for shift in (3, 4, 5):
        between = jnp.where(same(shift + 1) & ~same(shift), a_low, 0.0)
        t = _dot3(between, x)
        yield
        x = x - _dot3(x, t)
        yield
    return x


class _GdnChunk:
    def build(self, q, k, v, gb, h, sum_on_mxu):
        C = GDN_CHUNK
        lane = lax.broadcasted_iota(jnp.int32, (C, SM_W), 1)
        g = jnp.sum(jnp.where(lane == h, gb, 0.0), axis=1, keepdims=True)
        self.beta = jnp.sum(jnp.where(lane == h + GDN_HEADS, gb, 0.0), axis=1, keepdims=True)
        ri = lax.broadcasted_iota(jnp.int32, (C, C), 0)
        ci = lax.broadcasted_iota(jnp.int32, (C, C), 1)
        self.causal = ri >= ci
        self.strict = ri > ci
        self.eye = (ri == ci).astype(F32)
        if sum_on_mxu:
            gcb = lax.dot_general(self.causal.astype(F32), jnp.broadcast_to(g, (C, SM_W)), NN,
                                  precision=lax.Precision.HIGHEST, preferred_element_type=F32)
        else:
            gcb = _running_sum(jnp.broadcast_to(g, (C, SM_W)))
        yield
        self.gcol = gcb[:, 0:1]
        grow = gcb.T[0:1, 0:C]
        self.decay = jnp.exp(jnp.where(self.causal, self.gcol - grow, -1e30))
        self.egc = jnp.exp(self.gcol)
        glast = gcb[C - 1:C, 0:1]
        self.elast = jnp.exp(glast - self.gcol)
        self.gl = jnp.exp(glast)
        self.q, self.k, self.v = q, k, v
        self.kb = k * self.beta
        m = _dot(self.kb, k, NT)
        n_ = _dot(q, k, NT)
        yield
        self.a_low = jnp.where(self.strict, m * self.decay, 0.0)
        self.p = n_ * self.decay
        self.qd = q * self.egc
        self.kd = k * self.elast
        self.bu = v * self.beta
        self.bw = self.kb * self.egc


GDN_HB = 8
GDN_HG = GDN_HEADS // GDN_HB


def _gdn_specs(n_of):
    C, W = GDN_CHUNK, 128 * GDN_HB
    q_spec = pl.BlockSpec((C, W), lambda g, n: (n_of(n), g))
    k_spec = pl.BlockSpec((C, W), lambda g, n: (n_of(n), g + GDN_HG))
    v_spec = pl.BlockSpec((C, W), lambda g, n: (n_of(n), g + 2 * GDN_HG))
    gb_spec = pl.BlockSpec((C, SM_W), lambda g, n: (n_of(n), 0))
    o_spec = pl.BlockSpec((C, W), lambda g, n: (n_of(n), g))
    s_spec = pl.BlockSpec((GDN_HB, None, GDN_DK, GDN_DV), lambda g, n: (g, n_of(n), 0, 0))
    t_spec = pl.BlockSpec((GDN_HB, None, C, C), lambda g, n: (g, n_of(n), 0, 0))
    return q_spec, k_spec, v_spec, gb_spec, o_spec, s_spec, t_spec


def _gdn_fwd(act, gb, *, name):
    M = act.shape[0]
    N = M // GDN_CHUNK

    def body(q_ref, k_ref, v_ref, gb_ref, o_ref, s_ref, t_ref, state):
        g, n = pl.program_id(0), pl.program_id(1)

        @pl.when(n == 0)
        def _():
            state[...] = jnp.zeros_like(state)

        gb_ = gb_ref[...]

        def head(hh):
            cols = slice(hh * 128, (hh + 1) * 128)
            c = _GdnChunk()
            yield from c.build(q_ref[:, cols], k_ref[:, cols], v_ref[:, cols], gb_, g * GDN_HB + hh, sum_on_mxu=True)
            tinv = yield from _unit_lower_inverse(c.a_low, c.eye)
            s = state[hh]
            s_ref[hh] = s
            t_ref[hh] = tinv
            u = _dot(tinv, c.bu)
            w = _dot(tinv, c.bw)
            yield
            vn = u - _dot(w, s)
            o1 = _dot(c.qd, s)
            yield
            o_ref[:, cols] = (o1 + _dot(c.p, vn)).astype(o_ref.dtype)
            state[hh] = c.gl * s + _dot(c.kd, vn, TN)

        _round_robin(head(hh) for hh in range(GDN_HB))

    q_spec, k_spec, v_spec, gb_spec, o_spec, s_spec, t_spec = _gdn_specs(lambda n: n)
    return pl.pallas_call(
        body, name=name, grid=(GDN_HG, N),
        in_specs=[q_spec, k_spec, v_spec, gb_spec], out_specs=[o_spec, s_spec, t_spec],
        out_shape=[jax.ShapeDtypeStruct((M, GDN_V), BF16),
                   jax.ShapeDtypeStruct((GDN_HEADS, N, GDN_DK, GDN_DV), F32),
                   jax.ShapeDtypeStruct((GDN_HEADS, N, GDN_CHUNK, GDN_CHUNK), F32)],
        scratch_shapes=[pltpu.VMEM((GDN_HB, GDN_DK, GDN_DV), F32)],
        compiler_params=_params("parallel", "arbitrary"),
    )(act, act, act, gb)


def _gdn_bwd(act, gb, do, s_all, t_all, *, name):
    M = act.shape[0]
    N = M // GDN_CHUNK
    C = GDN_CHUNK
    assert GDN_HG == 1

    def body(q_ref, k_ref, v_ref, gb_ref, do_ref, s_ref, t_ref, dact_ref, dgb_ref, dstate):
        g, n = pl.program_id(0), pl.program_id(1)

        @pl.when(n == 0)
        def _():
            dstate[...] = jnp.zeros_like(dstate)

        gb_ = gb_ref[...]
        last = lax.broadcasted_iota(jnp.int32, (C, 1), 0) == C - 1
        lane = lax.broadcasted_iota(jnp.int32, (C, SM_W), 1)
        def head(hh):
            cols = slice(hh * 128, (hh + 1) * 128)
            h = g * GDN_HB + hh
            c = _GdnChunk()
            yield from c.build(q_ref[:, cols], k_ref[:, cols], v_ref[:, cols], gb_, h, sum_on_mxu=False)
            tinv = t_ref[hh]
            tinv_t = tinv.T
            s = s_ref[hh]
            do_ = do_ref[:, cols]
            ds1 = dstate[hh]
            u = _dot(tinv, c.bu)
            w = _dot(tinv, c.bw)
            dqd = _dot(do_, s, NT)
            yield
            dvn0 = _dot(c.p, do_, TN) + _dot(c.kd, ds1)
            dst0 = _dot(c.qd, do_, TN) + c.gl * ds1
            yield
            vn = u - _dot(w, s)
            dvn = dvn0
            yield
            dp = jnp.where(c.causal, _dot(do_, vn, NT), 0.0)
            dstate[hh] = dst0 - _dot(w, dvn, TN)
            dkd = _dot(vn, ds1, NT)
            dw = -_dot(dvn, s, NT)
            dbu = _dot(tinv_t, dvn)
            dgl = jnp.sum(jnp.sum(s * ds1, axis=1, keepdims=True), axis=0, keepdims=True)
            yield
            dbw = _dot(tinv_t, dw)
            t1 = _dot(dbu, u, NT)
            yield
            da = jnp.where(c.strict, -(t1 + _dot(dbw, w, NT)), 0.0)
            dn_ = dp * c.decay
            dq0 = _dot(dn_, c.k)
            dk0 = _dot(dn_, c.q, TN)
            yield
            dm = da * c.decay
            e = da * c.a_low + dp * c.p
            dkb = _dot(dm, c.k) + dbw * c.egc
            dact_ref[:, GDN_QK + hh * 128:GDN_QK + (hh + 1) * 128] = (
                _dot(dm, c.kb, TN) + dk0 + dkb * c.beta + dkd * c.elast)
            dact_ref[:, cols] = dq0 + dqd * c.egc
            dact_ref[:, 2 * GDN_QK + hh * 128:2 * GDN_QK + (hh + 1) * 128] = dbu * c.beta
            dbeta = jnp.sum(dbu * c.v, axis=1, keepdims=True) + jnp.sum(dkb * c.k, axis=1, keepdims=True)
            t_kd = jnp.sum(dkd * c.kd, axis=1, keepdims=True)
            dgc = (jnp.sum(e, axis=1, keepdims=True) - jnp.sum(e.T, axis=1, keepdims=True)
                   + jnp.sum(dbw * c.bw, axis=1, keepdims=True) + jnp.sum(dqd * c.qd, axis=1, keepdims=True) - t_kd)
            dgc = dgc + jnp.where(last, jnp.sum(t_kd, axis=0, keepdims=True) + dgl * c.gl, 0.0)
            yield
            dg = _running_sum(jnp.broadcast_to(dgc, (C, SM_W)), reverse=True)
            dgb_ref[hh] = jnp.where(lane == h, dg, jnp.where(lane == h + GDN_HEADS, dbeta, 0.0))

        _round_robin(head(hh) for hh in range(GDN_HB))

    rev = lambda n: N - 1 - n
    q_spec, k_spec, v_spec, gb_spec, o_spec, s_spec, t_spec = _gdn_specs(rev)
    dgb_spec = pl.BlockSpec((GDN_HB, C, SM_W), lambda g, n: (g, rev(n), 0))
    return pl.pallas_call(
        body, name=name, grid=(GDN_HG, N),
        in_specs=[q_spec, k_spec, v_spec, gb_spec, o_spec, s_spec, t_spec],
        out_specs=[pl.BlockSpec((C, 2 * GDN_QK + GDN_V), lambda g, n: (rev(n), 0)), dgb_spec],
        out_shape=[jax.ShapeDtypeStruct((M, 2 * GDN_QK + GDN_V), F32),
                   jax.ShapeDtypeStruct((GDN_HEADS, M, SM_W), F32)],
        scratch_shapes=[pltpu.VMEM((GDN_HB, GDN_DK, GDN_DV), F32)],
        compiler_params=_params("parallel", "arbitrary"),
    )(act, act, act, gb, do, s_all, t_all)


GLA_STEP_ROWS = 64
GLA_SUB = GLA_STEP_ROWS // GLA_CHUNK


def _gla_cumsum(la):
    return _running_sum(la)


GLA_HALF = GLA_CHUNK // 2


def _gla_cross_factors(b):
    top = lax.broadcasted_iota(jnp.int32, b.shape, 0) < GLA_HALF
    bm = b[GLA_HALF - 1:GLA_HALF, :]
    late = jnp.where(top, 0.0, jnp.exp(jnp.minimum(b - bm, 0.0)))
    early = jnp.where(top, jnp.exp(jnp.minimum(bm - b, 0.0)), 0.0)
    return late, early


def _gla_half_decay(bh, ii):
    rj = lax.broadcasted_iota(jnp.int32, bh.shape, 0)
    return jnp.where(rj <= ii, jnp.exp(jnp.minimum(bh[ii:ii + 1, :] - bh, 0.0)), 0.0)


def _gla_scores_t(q, k, b):
    C, H = GLA_CHUNK, GLA_HALF
    lane = lax.broadcasted_iota(jnp.int32, (H, C), 1)
    halves = []
    for h0 in (0, H):
        qh, kh, bh = q[h0:h0 + H], k[h0:h0 + H], b[h0:h0 + H]
        sth = jnp.zeros((H, C), F32)
        for ii in range(H):
            si = jnp.sum(qh[ii:ii + 1, :] * kh * _gla_half_decay(bh, ii), axis=1, keepdims=True)
            sth = jnp.where(lane == h0 + ii, si, sth)
            if ii % 4 == 3:
                yield
        halves.append(sth)
    late, early = _gla_cross_factors(b)
    between = _dot(k * early, q * late, NT)
    yield
    return jnp.concatenate(halves, axis=0) + between


def _gla_specs(n_of):
    R = GLA_STEP_ROWS
    q_spec = pl.BlockSpec((R, GLA_QK), lambda n: (n_of(n), C_GQ // GLA_QK))
    k_spec = pl.BlockSpec((R, GLA_QK), lambda n: (n_of(n), C_GK // GLA_QK))
    v_spec = pl.BlockSpec((R, GLA_V), lambda n: (n_of(n), C_GV // GLA_V))
    la_spec = pl.BlockSpec((R, GLA_QK), lambda n: (n_of(n), 0))
    o_spec = pl.BlockSpec((R, GLA_V), lambda n: (n_of(n), 0))
    s_spec = pl.BlockSpec((GLA_HEADS, None, GLA_SUB, GLA_DV, GLA_DK), lambda n: (0, n_of(n), 0, 0, 0))
    return q_spec, k_spec, v_spec, la_spec, o_spec, s_spec


def _gla_fwd(proj, la, *, name):
    M = proj.shape[0]
    N = M // GLA_STEP_ROWS
    C = GLA_CHUNK

    def body(q_ref, k_ref, v_ref, la_ref, o_ref, s_ref, state):
        n = pl.program_id(0)

        @pl.when(n == 0)
        def _():
            state[...] = jnp.zeros_like(state)

        local = {}

        def within(hh, c):
            kc = slice(hh * GLA_DK, (hh + 1) * GLA_DK)
            vc = slice(hh * GLA_DV, (hh + 1) * GLA_DV)
            rows = slice(c * C, (c + 1) * C)
            q = q_ref[rows, kc].astype(F32) * (GLA_DK ** -0.5)
            k = k_ref[rows, kc].astype(F32)
            v = v_ref[rows, vc].astype(F32)
            b = _gla_cumsum(la_ref[rows, kc])
            yield
            blast = b[C - 1:C, :]
            sc_t = yield from _gla_scores_t(q, k, b)
            kv = _dot(v, k * jnp.exp(blast - b), TN)
            o2 = _dot(sc_t, v, TN)
            yield
            local[hh, c] = (q * jnp.exp(b), jnp.exp(blast), kv, o2)

        def across(hh):
            vc = slice(hh * GLA_DV, (hh + 1) * GLA_DV)
            st = state[hh]
            for c in range(GLA_SUB):
                qe, eblast, kv, o2 = local[hh, c]
                s_ref[hh, c] = st
                o1 = _dot(qe, st, NT)
                yield
                o_ref[c * C:(c + 1) * C, vc] = (o1 + o2).astype(o_ref.dtype)
                st = st * eblast + kv
            state[hh] = st

        _round_robin(within(hh, c) for c in range(GLA_SUB) for hh in range(GLA_HEADS))
        _round_robin(across(hh) for hh in range(GLA_HEADS))

    q_spec, k_spec, v_spec, la_spec, o_spec, s_spec = _gla_specs(lambda n: n)
    return pl.pallas_call(
        body, name=name, grid=(N,),
        in_specs=[q_spec, k_spec, v_spec, la_spec], out_specs=[o_spec, s_spec],
        out_shape=[jax.ShapeDtypeStruct((M, GLA_V), BF16),
                   jax.ShapeDtypeStruct((GLA_HEADS, N, GLA_SUB, GLA_DV, GLA_DK), F32)],
        scratch_shapes=[pltpu.VMEM((GLA_HEADS, GLA_DV, GLA_DK), F32)],
        compiler_params=_params("arbitrary"),
    )(proj, proj, proj, la)


def _gla_bwd(proj, la, do, s_all, d_proj, *, name):
    M = proj.shape[0]
    N = M // GLA_STEP_ROWS
    C = GLA_CHUNK
    qkv_w = 2 * GLA_QK + GLA_V
    assert C_GK == C_GQ + GLA_QK and C_GV == C_GK + GLA_QK and C_GQ % qkv_w == 0

    def body(q_ref, k_ref, v_ref, la_ref, do_ref, s_ref, _, dp_ref, dla_ref, dstate):
        n = pl.program_id(0)

        @pl.when(n == 0)
        def _():
            dstate[...] = jnp.zeros_like(dstate)

        H = GLA_HALF
        lane = lax.broadcasted_iota(jnp.int32, (C, C), 1)
        row = lax.broadcasted_iota(jnp.int32, (C, C), 0)
        ri = lax.broadcasted_iota(jnp.int32, (C, GLA_DK), 0)
        lane_h = lax.broadcasted_iota(jnp.int32, (H, C), 1)
        ri_h = lax.broadcasted_iota(jnp.int32, (H, GLA_DK), 0)
        cross = (row < H) & (lane >= H)
        def head(hh):
            kc = slice(hh * GLA_DK, (hh + 1) * GLA_DK)
            vc = slice(hh * GLA_DV, (hh + 1) * GLA_DV)
            ds1 = dstate[hh]
            for c in reversed(range(GLA_SUB)):
                rows = slice(c * C, (c + 1) * C)
                q = q_ref[rows, kc].astype(F32) * (GLA_DK ** -0.5)
                k = k_ref[rows, kc].astype(F32)
                v = v_ref[rows, vc].astype(F32)
                b = _gla_cumsum(la_ref[rows, kc])
                do_ = do_ref[rows, vc]
                st = s_ref[hh, c]
                dsc_t = _dot(v, do_, NT)
                dqe = _dot(do_, st)
                dke = _dot(v, ds1)
                yield
                blast = b[C - 1:C, :]
                eb = jnp.exp(b)
                elast = jnp.exp(blast - b)
                eblast = jnp.exp(blast)
                qe = q * eb
                ke = k * elast
                dv2 = _dot(ke, ds1, NT)
                ds_new = _dot(do_, qe, TN)
                deblast = jnp.sum(st * ds1, axis=0, keepdims=True)
                sc_halves, dq_halves, dk_halves = [], [], []
                for h0 in (0, H):
                    qh, kh, bh, dsch = q[h0:h0 + H], k[h0:h0 + H], b[h0:h0 + H], dsc_t[h0:h0 + H]
                    sch = jnp.zeros((H, C), F32)
                    dqh = jnp.zeros((H, GLA_DK), F32)
                    dkh = jnp.zeros((H, GLA_DK), F32)
                    for ii in range(H):
                        f = _gla_half_decay(bh, ii)
                        kf = kh * f
                        si = jnp.sum(qh[ii:ii + 1, :] * kf, axis=1, keepdims=True)
                        sch = jnp.where(lane_h == h0 + ii, si, sch)
                        dsi = jnp.sum(jnp.where(lane_h == h0 + ii, dsch, 0.0), axis=1, keepdims=True)
                        dqh = jnp.where(ri_h == ii, jnp.sum(dsi * kf, axis=0, keepdims=True), dqh)
                        dkh = dkh + (dsi * f) * qh[ii:ii + 1, :]
                        if ii % 4 == 3:
                            yield
                    sc_halves.append(sch)
                    dq_halves.append(dqh)
                    dk_halves.append(dkh)
                late, early = _gla_cross_factors(b)
                q_late, k_early = q * late, k * early
                dsc_x = jnp.where(cross, dsc_t, 0.0)
                sc_t = jnp.concatenate(sc_halves, axis=0) + _dot(k_early, q_late, NT)
                dq_sc = jnp.concatenate(dq_halves, axis=0) + _dot(dsc_x, k_early, TN) * late
                dk_sc = jnp.concatenate(dk_halves, axis=0) + _dot(dsc_x, q_late) * early
                yield
                dv1 = _dot(sc_t, do_)
                dp_ref[rows, kc] = ((dq_sc + dqe * eb) * (GLA_DK ** -0.5)).astype(dp_ref.dtype)
                dp_ref[rows, GLA_QK + hh * GLA_DK:GLA_QK + (hh + 1) * GLA_DK] = (dk_sc + dke * elast).astype(dp_ref.dtype)
                t_ke = dke * ke
                db = q * dq_sc - k * dk_sc + dqe * qe - t_ke
                db = db + jnp.where(ri == C - 1, jnp.sum(t_ke, axis=0, keepdims=True) + deblast * eblast, 0.0)
                dla = _running_sum(db, reverse=True)
                yield
                dp_ref[rows, 2 * GLA_QK + hh * GLA_DV:2 * GLA_QK + (hh + 1) * GLA_DV] = (dv1 + dv2).astype(dp_ref.dtype)
                dla_ref[rows, kc] = dla
                ds1 = ds1 * eblast + ds_new
            dstate[hh] = ds1

        _round_robin(head(hh) for hh in range(GLA_HEADS))

    rev = lambda n: N - 1 - n
    q_spec, k_spec, v_spec, la_spec, o_spec, s_spec = _gla_specs(rev)
    return pl.pallas_call(
        body, name=name, grid=(N,),
        in_specs=[q_spec, k_spec, v_spec, la_spec, o_spec, s_spec, _ANY],
        out_specs=[pl.BlockSpec((GLA_STEP_ROWS, qkv_w), lambda n: (rev(n), C_GQ // qkv_w)), la_spec],
        out_shape=[jax.ShapeDtypeStruct(d_proj.shape, d_proj.dtype), jax.ShapeDtypeStruct((M, GLA_QK), F32)],
        input_output_aliases={6: 0},
        scratch_shapes=[pltpu.VMEM((GLA_HEADS, GLA_DV, GLA_DK), F32)],
        compiler_params=_params("arbitrary"),
    )(proj, proj, proj, la, do, s_all, d_proj)


def _head_norm(o, wn):
    r = lax.rsqrt(jnp.mean(o * o, axis=-1, keepdims=True) + NORM_EPS)
    return o * r, r


def _mix_heads():
    heads = [(0, GDN_DV, hh * GDN_DV, hh * GDN_DV) for hh in range(GDN_HEADS)]
    heads += [(1, GLA_DV, GDN_V + hh * GLA_DV, hh * GLA_DV) for hh in range(GLA_HEADS)]
    return heads


def _mix_fwd(o_gdn, o_gla, proj, wn_gdn, wn_gla, *, name):
    M = proj.shape[0]
    tm = _tile(M, 344, 16)

    def body(og_ref, ol_ref, z_ref, r_ref, wg_ref, wl_ref, m_ref):
        srcs = ((og_ref, z_ref, wg_ref), (ol_ref, r_ref, wl_ref))
        for grp, width, mcol, col in _mix_heads():
            o_ref, gate_ref, w_ref = srcs[grp]
            xhat, _ = _head_norm(o_ref[:, col:col + width].astype(F32), None)
            gate, _ = _silu_and_grad(gate_ref[:, col:col + width].astype(F32))
            m_ref[:, mcol:mcol + width] = (xhat * w_ref[...] * gate).astype(m_ref.dtype)

    full = lambda s: pl.BlockSpec(s, lambda i: (0, 0))
    return pl.pallas_call(
        body, name=name, grid=(M // tm,),
        in_specs=[pl.BlockSpec((tm, GDN_V), lambda i: (i, 0)), pl.BlockSpec((tm, GLA_V), lambda i: (i, 0)),
                  pl.BlockSpec((tm, GDN_V), lambda i: (i, C_Z // GDN_V)),
                  pl.BlockSpec((tm, GLA_V), lambda i: (i, C_GR // GLA_V)),
                  full((1, GDN_DV)), full((1, GLA_DV))],
        out_specs=pl.BlockSpec((tm, D_MODEL), lambda i: (i, 0)),
        out_shape=jax.ShapeDtypeStruct((M, D_MODEL), BF16),
        compiler_params=_params("parallel"),
    )(o_gdn, o_gla, proj, proj, wn_gdn, wn_gla)


def _mix_bwd(o_gdn, o_gla, proj, wn_gdn, wn_gla, dmixed, *, name):
    M = proj.shape[0]
    tm = _tile(M, 344, 16)
    g_ = M // tm
    assert C_Z == 0 and C_GR == GDN_V

    def body(og_ref, ol_ref, z_ref, r_ref, wg_ref, wl_ref, dm_ref,
             dog_ref, dol_ref, dzr_ref, dwg_ref, dwl_ref):
        i = pl.program_id(0)
        srcs = ((og_ref, z_ref, wg_ref, dog_ref), (ol_ref, r_ref, wl_ref, dol_ref))
        dws = [jnp.zeros((1, GDN_DV), F32), jnp.zeros((1, GLA_DV), F32)]
        for grp, width, mcol, col in _mix_heads():
            o_ref, gate_ref, w_ref, do_ref = srcs[grp]
            cols = slice(col, col + width)
            xhat, r = _head_norm(o_ref[:, cols].astype(F32), None)
            gate, dgate_dc = _silu_and_grad(gate_ref[:, cols].astype(F32))
            dm = dm_ref[:, mcol:mcol + width]
            dzr_ref[:, mcol:mcol + width] = (dm * xhat * w_ref[...] * dgate_dc).astype(dzr_ref.dtype)
            dnorm = dm * gate
            dws[grp] = dws[grp] + jnp.sum(dnorm * xhat, axis=0, keepdims=True)
            dxhat = dnorm * w_ref[...]
            do_ref[:, cols] = r * (dxhat - xhat * jnp.mean(dxhat * xhat, axis=-1, keepdims=True))

        @pl.when(i == 0)
        def _():
            dwg_ref[...] = dws[0]
            dwl_ref[...] = dws[1]

        @pl.when(i > 0)
        def _():
            dwg_ref[...] += dws[0]
            dwl_ref[...] += dws[1]

    full = lambda s: pl.BlockSpec(s, lambda i: (0, 0))
    half = pl.BlockSpec((tm, GDN_V), lambda i: (i, 0))
    return pl.pallas_call(
        body, name=name, grid=(g_,),
        in_specs=[half, half, pl.BlockSpec((tm, GDN_V), lambda i: (i, C_Z // GDN_V)),
                  pl.BlockSpec((tm, GLA_V), lambda i: (i, C_GR // GLA_V)),
                  full((1, GDN_DV)), full((1, GLA_DV)), pl.BlockSpec((tm, D_MODEL), lambda i: (i, 0))],
        out_specs=[half, half, pl.BlockSpec((tm, GDN_V + GLA_V), lambda i: (i, 0)),
                   full((1, GDN_DV)), full((1, GLA_DV))],
        out_shape=[jax.ShapeDtypeStruct((M, GDN_V), F32), jax.ShapeDtypeStruct((M, GLA_V), F32),
                   jax.ShapeDtypeStruct((M, D_PROJ), BF16),
                   jax.ShapeDtypeStruct((1, GDN_DV), F32), jax.ShapeDtypeStruct((1, GLA_DV), F32)],
        compiler_params=_params("arbitrary"),
    )(o_gdn, o_gla, proj, proj, wn_gdn, wn_gla, dmixed)


def _row_chunks(tm, parts=2):
    if tm % (16 * parts):
        return [slice(0, tm)]
    return [slice(p * (tm // parts), (p + 1) * (tm // parts)) for p in range(parts)]


def _swiglu_fwd(n, w_gate_t, w_up_t, *, name, tm=1376, tn=512):
    M, D = n.shape
    F = w_gate_t.shape[0]
    tm, tn = _tile(M, tm, 16), _tile(F, tn, 128)

    def body(n_ref, wg_ref, wu_ref, g_ref, u_ref, a_ref):
        wg, wu = wg_ref[...], wu_ref[...]
        for rows in _row_chunks(tm):
            x = n_ref[rows, :]
            g = _dot(x, wg, NT)
            u = _dot(x, wu, NT)
            s, _ = _silu_and_grad(g)
            g_ref[rows, :] = g.astype(g_ref.dtype)
            u_ref[rows, :] = u.astype(u_ref.dtype)
            a_ref[rows, :] = (s * u).astype(a_ref.dtype)

    w_spec = pl.BlockSpec((tn, D), lambda i, j: (j, 0))
    o_spec = pl.BlockSpec((tm, tn), lambda i, j: (i, j))
    return pl.pallas_call(
        body, name=name, grid=(M // tm, F // tn),
        in_specs=[pl.BlockSpec((tm, D), lambda i, j: (i, 0)), w_spec, w_spec], out_specs=[o_spec] * 3,
        out_shape=[jax.ShapeDtypeStruct((M, F), BF16)] * 3, compiler_params=_params("parallel", "parallel"),
    )(n, w_gate_t, w_up_t)


def _swiglu_bwd(dh, w_down, gate, up, *, name, after=None, tm=1376, tn=512):
    M, D = dh.shape
    F = w_down.shape[0]
    tm, tn = _tile(M, tm, 16), _tile(F, tn, 128)
    n_after = 0 if after is None else 1

    def body(*refs):
        dh_ref, w_ref, g_ref, u_ref, dg_ref, du_ref = refs[n_after:]
        w = w_ref[...]
        for rows in _row_chunks(tm):
            da = _dot(dh_ref[rows, :], w, NT)
            s, ds = _silu_and_grad(g_ref[rows, :].astype(F32))
            dg_ref[rows, :] = (da * u_ref[rows, :].astype(F32) * ds).astype(dg_ref.dtype)
            du_ref[rows, :] = (da * s).astype(du_ref.dtype)

    o_spec = pl.BlockSpec((tm, tn), lambda i, j: (i, j))
    return pl.pallas_call(
        body, name=name, grid=(M // tm, F // tn),
        in_specs=[_ANY] * n_after + [pl.BlockSpec((tm, D), lambda i, j: (i, 0)),
                                     pl.BlockSpec((tn, D), lambda i, j: (j, 0)), o_spec, o_spec],
        out_specs=[o_spec, o_spec], out_shape=[jax.ShapeDtypeStruct((M, F), BF16)] * 2,
        compiler_params=_params("parallel", "parallel"),
    )(*((after,) if n_after else ()), dh, w_down, gate, up)


def _adamw_update(w, g, m, v):
    nm = ADAM_B1 * m + (1.0 - ADAM_B1) * g
    nv = ADAM_B2 * v + (1.0 - ADAM_B2) * (g * g)
    m_hat = nm / (1.0 - ADAM_B1 ** ADAM_STEP)
    v_hat = nv / (1.0 - ADAM_B2 ** ADAM_STEP)
    return -ADAM_LR * (m_hat / (jnp.sqrt(v_hat) + ADAM_EPS) + ADAM_WD * w), nm, nv


def _adamw(w, g, m, v, *, name):
    shape = w.shape
    cols = shape[-1]
    rows = w.size // cols
    w2, g2, m2, v2 = (t.reshape(rows, cols) for t in (w, g, m, v))
    if rows % 8 == 0 or cols % 128 != 0:
        tr, tc = (_tile(rows, 256, 8) if rows % 8 == 0 else rows), cols
    else:
        tr, tc = rows, _tile(cols, 256, 128)

    def body(w_ref, g_ref, m_ref, v_ref, d_ref, nm_ref, nv_ref):
        d_ref[...], nm_ref[...], nv_ref[...] = _adamw_update(w_ref[...], g_ref[...], m_ref[...], v_ref[...])

    blk = pl.BlockSpec((tr, tc), lambda i, j: (i, j))
    outs = pl.pallas_call(
        body, name=name, grid=(rows // tr, cols // tc), in_specs=[blk] * 4, out_specs=[blk] * 3,
        out_shape=[jax.ShapeDtypeStruct((rows, cols), F32)] * 3, compiler_params=_params("parallel", "parallel"),
    )(w2, g2, m2, v2)
    return tuple(t.reshape(shape) for t in outs)


def _sum_slabs(x, *, name):
    _, R, C = x.shape
    sub = 16 if x.dtype == BF16 else 8
    if R % sub == 0:
        tr, tc = _tile(R, 128, sub), C
    else:
        tr, tc = R, _tile(C, 256, 128)

    def body(x_ref, o_ref):
        acc = x_ref[0].astype(F32)
        for s in range(1, N_DEV):
            acc = acc + x_ref[s].astype(F32)
        o_ref[...] = acc

    return pl.pallas_call(
        body, name=name, grid=(R // tr, C // tc),
        in_specs=[pl.BlockSpec((N_DEV, tr, tc), lambda i, j: (0, i, j))],
        out_specs=pl.BlockSpec((tr, tc), lambda i, j: (i, j)),
        out_shape=jax.ShapeDtypeStruct((R, C), F32), compiler_params=_params("parallel", "parallel"),
    )(x)


def _sum_adamw(x, w, m, v, *, name):
    _, R, C = x.shape
    if R % 16 == 0:
        tr, tc = _tile(R, 128, 16), C
    else:
        tr, tc = R, _tile(C, 256, 128)

    def body(x_ref, w_ref, m_ref, v_ref, g_ref, d_ref, nm_ref, nv_ref):
        g = x_ref[0].astype(F32)
        for s in range(1, N_DEV):
            g = g + x_ref[s].astype(F32)
        g_ref[...] = g
        d_ref[...], nm_ref[...], nv_ref[...] = _adamw_update(w_ref[...], g, m_ref[...], v_ref[...])

    blk = pl.BlockSpec((tr, tc), lambda i, j: (i, j))
    return pl.pallas_call(
        body, name=name, grid=(R // tr, C // tc),
        in_specs=[pl.BlockSpec((N_DEV, tr, tc), lambda i, j: (0, i, j)), blk, blk, blk], out_specs=[blk] * 4,
        out_shape=[jax.ShapeDtypeStruct((R, C), F32)] * 4, compiler_params=_params("parallel", "parallel"),
    )(x, w, m, v)


def _peers():
    x, y, c = lax.axis_index("x"), lax.axis_index("y"), lax.axis_index("c")
    me = 4 * x + 2 * y + c
    peers = []
    for k in range(1, N_DEV):
        px = 1 - x if k & 4 else x
        py = 1 - y if k & 2 else y
        pc = 1 - c if k & 1 else c
        peers.append(((px, py, pc), 4 * px + 2 * py + pc))
    return me, peers


_HBM = pl.BlockSpec(memory_space=pltpu.HBM)
_SEM = pl.BlockSpec(memory_space=pltpu.SEMAPHORE)
_EFFECT = pltpu.SideEffectType.DATAFLOW_SIDE_EFFECTING


PLAN_GATHER = tuple((k, "x", 0) for k in range(1, N_DEV))
PLAN_SCATTER = tuple((k, "xk", 0) for k in range(1, N_DEV))
PLAN_GATHER_CHIPS = tuple((k, "x", 0) for k in (1, 2, 4, 6))
PLAN_GATHER_PASS_ON = tuple((1, ("land", q), q) for q in (2, 4, 6))


def _plan_refs(plan, j, x_ref, land_ref, me, peers, receiving):
    k, source, r = plan[j]
    index_of = lambda q: me if q == 0 else peers[q - 1][1]
    pos, target = peers[k - 1]
    if source == "x":
        src = x_ref
    elif source == "xk":
        src = x_ref.at[target]
    else:
        src = land_ref.at[index_of(source[1])]
    return pos, src, land_ref.at[index_of(k ^ r) if receiving else index_of(r)]


def _exchange_start(x, *, plan, name, after=None, land=None, slab=None):
    n_after = 0 if after is None else 1
    n = len(plan)

    def body(*refs):
        x_ref, land_ref, send_sems, recv_sems, _, _, token = refs[n_after:]
        me, peers = _peers()
        for j in range(n):
            pos, src, dst = _plan_refs(plan, j, x_ref, land_ref, me, peers, receiving=False)
            pltpu.make_async_remote_copy(src_ref=src, dst_ref=dst, send_sem=send_sems.at[j], recv_sem=recv_sems.at[j],
                                         device_id=pos, device_id_type=pl.DeviceIdType.MESH).start()
        token[...] = jnp.zeros_like(token)

    if land is None:
        land = lax.empty((N_DEV,) + tuple(slab), x.dtype)
    return pl.pallas_call(
        body, name=name,
        out_shape=(pltpu.SemaphoreType.DMA((n,)), pltpu.SemaphoreType.DMA((n,)),
                   pltpu.HBM(x.shape, x.dtype), pltpu.HBM(land.shape, land.dtype), jax.ShapeDtypeStruct((8, 128), F32)),
        in_specs=[_ANY] * n_after + [_HBM, _HBM],
        out_specs=(_SEM, _SEM, _HBM, _HBM, pl.BlockSpec(memory_space=pltpu.VMEM)),
        input_output_aliases={n_after: 2, n_after + 1: 3},
        compiler_params=pltpu.CompilerParams(has_side_effects=_EFFECT),
    )(*((after,) if n_after else ()), pltpu.with_memory_space_constraint(x, pltpu.HBM),
      pltpu.with_memory_space_constraint(land, pltpu.HBM))


def _exchange_start_many(items, *, name):
    n_items = len(items)
    lands = [lax.empty((N_DEV,) + tuple(slab), x.dtype) for x, _, slab in items]

    def body(*refs):
        ins, outs = refs[:2 * n_items], refs[2 * n_items:]
        me, peers = _peers()
        for i, (_, plan, _) in enumerate(items):
            x_ref, land_ref = ins[2 * i], ins[2 * i + 1]
            send_sems, recv_sems = outs[4 * i], outs[4 * i + 1]
            for j in range(len(plan)):
                pos, src, dst = _plan_refs(plan, j, x_ref, land_ref, me, peers, receiving=False)
                pltpu.make_async_remote_copy(src_ref=src, dst_ref=dst, send_sem=send_sems.at[j], recv_sem=recv_sems.at[j],
                                             device_id=pos, device_id_type=pl.DeviceIdType.MESH).start()
        outs[-1][...] = jnp.zeros_like(outs[-1])

    out_shape, out_specs, operands, aliases = [], [], [], {}
    for i, ((x, plan, _), land) in enumerate(zip(items, lands)):
        out_shape += [pltpu.SemaphoreType.DMA((len(plan),)), pltpu.SemaphoreType.DMA((len(plan),)),
                      pltpu.HBM(x.shape, x.dtype), pltpu.HBM(land.shape, land.dtype)]
        out_specs += [_SEM, _SEM, _HBM, _HBM]
        operands += [pltpu.with_memory_space_constraint(x, pltpu.HBM), pltpu.with_memory_space_constraint(land, pltpu.HBM)]
        aliases[2 * i], aliases[2 * i + 1] = 4 * i + 2, 4 * i + 3
    res = pl.pallas_call(
        body, name=name,
        out_shape=tuple(out_shape) + (jax.ShapeDtypeStruct((8, 128), F32),),
        in_specs=[_HBM] * (2 * n_items), out_specs=tuple(out_specs) + (pl.BlockSpec(memory_space=pltpu.VMEM),),
        input_output_aliases=aliases, compiler_params=pltpu.CompilerParams(has_side_effects=_EFFECT),
    )(*operands)
    return [tuple(res[4 * i:4 * i + 4]) + (res[-1],) for i in range(n_items)]


def _exchange_wait(handle, after, *, plan, name):
    send_sems, recv_sems, x_thru, land_thru, _ = handle
    afters = list(after) if isinstance(after, (list, tuple)) else [after]

    def body(x_ref, land_ref, send_sems, recv_sems, *rest):
        me, peers = _peers()
        for j in range(len(plan)):
            pos, src, dst = _plan_refs(plan, j, x_ref, land_ref, me, peers, receiving=True)
            cp = pltpu.make_async_remote_copy(src_ref=src, dst_ref=dst, send_sem=send_sems.at[j], recv_sem=recv_sems.at[j],
                                              device_id=pos, device_id_type=pl.DeviceIdType.MESH)
            cp.wait_send()
            cp.wait_recv()

    return pl.pallas_call(
        body, name=name,
        out_shape=(pltpu.HBM(x_thru.shape, x_thru.dtype), pltpu.HBM(land_thru.shape, land_thru.dtype)),
        in_specs=[_HBM, _HBM, _SEM, _SEM] + [_ANY] * len(afters), out_specs=(_HBM, _HBM),
        input_output_aliases={0: 0, 1: 1}, compiler_params=pltpu.CompilerParams(has_side_effects=_EFFECT),
    )(x_thru, land_thru, send_sems, recv_sems, *afters)


def _wait_and_pass_on(handle, after, *, name):
    send_sems, recv_sems, x_thru, land_thru, _ = handle
    afters = list(after) if isinstance(after, (list, tuple)) else [after]
    first, second = PLAN_GATHER_CHIPS, PLAN_GATHER_PASS_ON

    def body(x_ref, land_ref, send_sems, recv_sems, *rest):
        send_next, recv_next, _, _, token = rest[len(afters):]
        me, peers = _peers()
        arrived = []
        for j in range(len(first)):
            pos, src, dst = _plan_refs(first, j, x_ref, land_ref, me, peers, receiving=True)
            cp = pltpu.make_async_remote_copy(src_ref=src, dst_ref=dst, send_sem=send_sems.at[j], recv_sem=recv_sems.at[j],
                                              device_id=pos, device_id_type=pl.DeviceIdType.MESH)
            cp.wait_recv()
            arrived.append(cp)
            for i in range(len(second)):
                if second[i][2] == first[j][0]:
                    pos, src, dst = _plan_refs(second, i, x_ref, land_ref, me, peers, receiving=False)
                    pltpu.make_async_remote_copy(src_ref=src, dst_ref=dst, send_sem=send_next.at[i], recv_sem=recv_next.at[i],
                                                 device_id=pos, device_id_type=pl.DeviceIdType.MESH).start()
        for cp in arrived:
            cp.wait_send()
        token[...] = jnp.zeros_like(token)

    n = len(second)
    return pl.pallas_call(
        body, name=name,
        out_shape=(pltpu.SemaphoreType.DMA((n,)), pltpu.SemaphoreType.DMA((n,)),
                   pltpu.HBM(x_thru.shape, x_thru.dtype), pltpu.HBM(land_thru.shape, land_thru.dtype),
                   jax.ShapeDtypeStruct((8, 128), F32)),
        in_specs=[_HBM, _HBM, _SEM, _SEM] + [_ANY] * len(afters),
        out_specs=(_SEM, _SEM, _HBM, _HBM, pl.BlockSpec(memory_space=pltpu.VMEM)),
        input_output_aliases={0: 2, 1: 3}, compiler_params=pltpu.CompilerParams(has_side_effects=_EFFECT),
    )(x_thru, land_thru, send_sems, recv_sems, *afters)


def _to_proj_rows(t):
    z = jnp.zeros((D_PROJ - C_SM - 2 * GDN_HEADS - GLA_RANK,) + t.shape[1:], t.dtype)
    return jnp.concatenate([t[R_Z:R_A], t[R_GR:R_LR], t[R_GQ:R_GR], t[:R_Z], t[R_A:R_GQ], t[R_LR:], z], axis=0)


def _from_proj_rows(t):
    ab = C_SM + 2 * GDN_HEADS
    return jnp.concatenate([t[C_QKV:C_SM], t[C_Z:C_GR], t[C_SM:ab], t[C_GQ:C_QKV], t[C_GR:C_GQ],
                            t[ab:ab + GLA_RANK]], axis=0)


def _local_step(x, target, meta, attn_nw, conv_w, a_log, dt_bias, gdn_nw, w2, b2, gla_nw, ffn_nw, final_nw,
                fetch, emit, start=None):
    head = jnp.concatenate([jnp.zeros((ROW_PAD, D_MODEL), F32), meta], axis=0)
    conv_w8 = jnp.concatenate([conv_w, jnp.zeros((8 - CONV_K, conv_w.shape[1]), F32)], axis=0)
    w2p = jnp.zeros((SM_W, GLA_QK), F32).at[2 * GDN_HEADS:2 * GDN_HEADS + GLA_RANK].set(w2)
    alog_p = jnp.zeros((1, SM_W), F32).at[:, :GDN_HEADS].set(a_log)
    dt_p = jnp.zeros((1, SM_W), F32).at[:, :GDN_HEADS].set(dt_bias)

    h0, n1 = _embed_norm(head, x, attn_nw, name="attn_norm", after=start)
    w_in_t = fetch("w_in_t", (n1, conv_w8, w2p, alog_p, dt_p))
    proj, sm = _in_proj(n1, w_in_t, name="in_proj")
    gb, la = _gates_fwd(sm, w2p, b2, alog_p, dt_p, name="gates")
    act = _prep_fwd(proj, conv_w8, name="gdn_prep")
    o_gdn, s_gdn, t_gdn = _gdn_fwd(act, gb, name="gdn_fwd")
    o_gla, s_gla = _gla_fwd(proj, la, name="gla_fwd")
    mixed = _mix_fwd(o_gdn, o_gla, proj, gdn_nw, gla_nw, name="mix")
    w_out = fetch("w_out", mixed)
    h1 = _matmul(mixed, w_out, mode="nn", add=h0, name="out_proj")
    n2 = _rmsnorm_fwd(h1, ffn_nw, name="ffn_norm")
    w_gate_t, w_up_t = fetch("w_gate_t", n2), fetch("w_up_t", n2)
    gate, up, hid = _swiglu_fwd(n2, w_gate_t, w_up_t, name="swiglu")
    w_down = fetch("w_down", hid)
    h2 = _matmul(hid, w_down, mode="nn", add=h1, name="ffn_down", tm=1376, tn=256)
    dh2, dh2_b, d_final_nw, loss = _loss_head(h2, final_nw, target, name="loss_head")

    wg = dict(mode="tn", out_dtype=BF16, tn=512)
    tok = emit("w_down", _matmul(hid, dh2_b, name="d_w_down", tm=704, **wg))
    d_gate, d_up = _swiglu_bwd(dh2_b, w_down, gate, up, name="d_swiglu", after=tok)
    tok = emit("w_gate_t", _matmul(d_gate, n2, name="d_w_gate", tm=704, **wg))
    tok = emit("w_up_t", _matmul(d_up, n2, name="d_w_up", tm=704, after=tok, **wg))
    d_n2 = _matmul_pair(d_gate, w_gate_t, d_up, w_up_t, name="d_n2", after=tok)
    dh1, dh1_b, d_ffn_nw = _rmsnorm_bwd(h1, ffn_nw, d_n2, dh2, name="d_ffn_norm")

    tok = emit("w_out", _matmul(mixed, dh1_b, name="d_w_out", tm=512, **wg))
    d_mixed = _matmul(dh1_b, w_out, mode="nt", name="d_mixed", after=tok)
    do_gdn, do_gla, d_proj, d_gdn_nw, d_gla_nw = _mix_bwd(o_gdn, o_gla, proj, gdn_nw, gla_nw, d_mixed, name="d_mix")
    d_proj, d_la = _gla_bwd(proj, la, do_gla, s_gla, d_proj, name="gla_bwd")
    dact, dgb_heads = _gdn_bwd(act, gb, do_gdn, s_gdn, t_gdn, name="gdn_bwd")
    d_proj, d_w2p, d_b2, d_alog, d_dt = _gates_bwd(sm, w2p, b2, alog_p, dt_p, dgb_heads, d_la, d_proj, name="d_gates")
    d_proj, d_conv_w8 = _prep_bwd(proj, conv_w8, dact, d_proj, name="d_gdn_prep")
    tok = emit("w_in_t", _matmul(d_proj, n1, name="d_w_in", tm=768, **wg))
    d_n1 = _matmul(d_proj, w_in_t, mode="nn", name="d_n1", tm=688, after=tok)
    grad_x, d_head, d_attn_nw = _embed_norm_bwd(h0, attn_nw, d_n1, dh1, name="d_attn_norm")

    return dict(
        loss=loss[0, 0], grad_x=grad_x, meta=d_head[ROW_PAD:HEAD_ROWS], attn_nw=d_attn_nw,
        conv_w=d_conv_w8[:CONV_K], a_log=d_alog[:, :GDN_HEADS], dt_bias=d_dt[:, :GDN_HEADS], gdn_nw=d_gdn_nw,
        w2=d_w2p[2 * GDN_HEADS:2 * GDN_HEADS + GLA_RANK], b2=d_b2, gla_nw=d_gla_nw, ffn_nw=d_ffn_nw,
        final_nw=d_final_nw)


SMALL_ROWS = 32


def kernel(x, meta_tokens, attn_norm_w, w_in, gdn_conv_w, gdn_a_log, gdn_dt_bias, gdn_norm_w, gla_gate_w2, gla_gate_b, gla_norm_w, w_out, ffn_norm_w, w_gate, w_up, w_down, final_norm_w, loss_target, m_meta_tokens, m_attn_norm_w, m_w_in, m_gdn_conv_w, m_gdn_a_log, m_gdn_dt_bias, m_gdn_norm_w, m_gla_gate_w2, m_gla_gate_b, m_gla_norm_w, m_w_out, m_ffn_norm_w, m_w_gate, m_w_up, m_w_down, m_final_norm_w, v_meta_tokens, v_attn_norm_w, v_w_in, v_gdn_conv_w, v_gdn_a_log, v_gdn_dt_bias, v_gdn_norm_w, v_gla_gate_w2, v_gla_gate_b, v_gla_norm_w, v_w_out, v_ffn_norm_w, v_w_gate, v_w_up, v_w_down, v_final_norm_w):
    me = 4 * lax.axis_index("x") + 2 * lax.axis_index("y") + lax.axis_index("c")

    n_conv = gdn_conv_w.shape[2]
    n_w2 = gla_gate_w2.shape[2]
    n_meta = meta_tokens.shape[1]
    small = jnp.zeros((40, n_conv), F32)
    small = small.at[0:N_META, :n_meta].set(meta_tokens)
    small = small.at[N_META:N_META + CONV_K, :].set(gdn_conv_w[0])
    small = small.at[24:24 + GLA_RANK, :n_w2].set(gla_gate_w2[0])

    w_in_slab = w_in[0].T.astype(BF16)
    items = [(small, PLAN_GATHER, small.shape), (w_in_slab, PLAN_GATHER_CHIPS, w_in_slab.shape)]
    wnames = ("w_out", "w_gate_t", "w_up_t", "w_down")
    for slab in (w_out[0], w_gate[0].T, w_up[0].T, w_down[0]):
        items.append((slab.astype(BF16), PLAN_GATHER, slab.shape))
    started = _exchange_start_many(items, name="gather_weights_start")
    small_h, in_h = started[0], started[1]
    handles = dict(zip(wnames, started[2:]))
    tok = small_h[4]

    own, small_all = _exchange_wait(small_h, tok, plan=PLAN_GATHER, name="gather_small_wait")
    small_all = lax.dynamic_update_index_in_dim(small_all, own, me, 0)
    meta_f = small_all[:, 0:N_META, :n_meta].transpose(1, 0, 2).reshape(N_META, D_MODEL)
    conv_f = small_all[:, N_META:N_META + CONV_K, :].transpose(1, 0, 2).reshape(CONV_K, N_DEV * n_conv)
    w2_f = small_all[:, 24:24 + GLA_RANK, :n_w2].transpose(1, 0, 2).reshape(GLA_RANK, N_DEV * n_w2)

    def fetch(name, after):
        if name == "w_in_t":
            pass_h = _wait_and_pass_on(in_h, after, name="gather_w_in_wait_pass_on")
            own, got = _exchange_wait(pass_h, pass_h[4], plan=PLAN_GATHER_PASS_ON, name="pass_w_in_wait")
            got = lax.dynamic_update_index_in_dim(got, own, me, 0)
            return _to_proj_rows(got.reshape(D_IN, D_MODEL))
        own, got = _exchange_wait(handles[name], after, plan=PLAN_GATHER, name="gather_" + name + "_wait")
        got = lax.dynamic_update_index_in_dim(got, own, me, 0)
        return got.reshape(N_DEV * got.shape[1], D_MODEL)

    sent = {}

    def emit(name, grad):
        if name == "w_in_t":
            grad = _from_proj_rows(grad)
        parts = grad.reshape(N_DEV, grad.shape[0] // N_DEV, D_MODEL)
        sent[name] = _exchange_start(parts, plan=PLAN_SCATTER, slab=parts.shape[1:], name="scatter_" + name + "_start")
        return sent[name][4]

    g = _local_step(x[0], loss_target[0], meta_f, attn_norm_w, conv_f, gdn_a_log, gdn_dt_bias, gdn_norm_w, w2_f,
                    gla_gate_b, gla_norm_w, ffn_norm_w, final_norm_w.reshape(1, D_MODEL), fetch, emit, start=tok)

    misc = jnp.concatenate([g["a_log"], g["dt_bias"], g["gdn_nw"], g["gla_nw"], g["b2"], g["loss"].reshape(1, 1)], axis=1)
    n_misc = misc.shape[1]
    misc = jnp.pad(misc, ((0, 0), (0, D_MODEL - n_misc)))
    rows = jnp.concatenate([g["attn_nw"], g["ffn_nw"], g["final_nw"], misc, g["meta"],
                            g["conv_w"].reshape(-1, D_MODEL), g["w2"].reshape(-1, D_MODEL)], axis=0)
    rows = jnp.pad(rows, ((0, SMALL_ROWS - rows.shape[0]), (0, 0)))
    rows_h = _exchange_start(rows, plan=PLAN_GATHER, slab=rows.shape, name="gather_small_grads_start")

    big = {}
    after = rows_h[4]
    for name, w, m, v, transposed in (("w_down", w_down, m_w_down, v_w_down, False), ("w_gate_t", w_gate, m_w_gate, v_w_gate, True),
                                      ("w_up_t", w_up, m_w_up, v_w_up, True), ("w_out", w_out, m_w_out, v_w_out, False),
                                      ("w_in_t", w_in, m_w_in, v_w_in, True)):
        own, got = _exchange_wait(sent[name], after, plan=PLAN_SCATTER, name="scatter_" + name + "_wait")
        got = lax.dynamic_update_index_in_dim(got, lax.dynamic_index_in_dim(own, me, 0, keepdims=False), me, 0)
        local = [t[0].T if transposed else t[0] for t in (w, m, v)]
        res = _sum_adamw(got, *local, name="adamw_" + name)
        big[name] = [t.T[None] if transposed else t[None] for t in res]
        after = res[0]

    own, got = _exchange_wait(rows_h, after, plan=PLAN_GATHER, name="gather_small_grads_wait")
    tot = _sum_slabs(lax.dynamic_update_index_in_dim(got, own, me, 0), name="sum_small_grads")
    grad_attn_nw, grad_ffn_nw, grad_final_nw = tot[0:1], tot[1:2], tot[2]
    grad_a_log = tot[3:4, 0:8]
    grad_dt = tot[3:4, 8:16]
    grad_gdn_nw = tot[3:4, 16:16 + GDN_DV]
    grad_gla_nw = tot[3:4, 144:144 + GLA_DV]
    grad_b2 = tot[3:4, 400:400 + GLA_QK]
    loss = tot[3, n_misc - 1]
    r0 = 4 + N_META
    grad_meta = lax.dynamic_slice(tot[4:r0], (0, me * n_meta), (N_META, n_meta))
    r1 = r0 + CONV_K * N_DEV * n_conv // D_MODEL
    grad_conv = lax.dynamic_slice(tot[r0:r1].reshape(CONV_K, N_DEV * n_conv), (0, me * n_conv), (CONV_K, n_conv))[None]
    r2 = r1 + GLA_RANK * N_DEV * n_w2 // D_MODEL
    grad_w2 = lax.dynamic_slice(tot[r1:r2].reshape(GLA_RANK, N_DEV * n_w2), (0, me * n_w2), (GLA_RANK, n_w2))[None]

    weights = [meta_tokens, attn_norm_w, w_in, gdn_conv_w, gdn_a_log, gdn_dt_bias, gdn_norm_w, gla_gate_w2,
               gla_gate_b, gla_norm_w, w_out, ffn_norm_w, w_gate, w_up, w_down, final_norm_w]
    grads = [grad_meta, grad_attn_nw, "w_in_t", grad_conv, grad_a_log, grad_dt, grad_gdn_nw, grad_w2,
             grad_b2, grad_gla_nw, "w_out", grad_ffn_nw, "w_gate_t", "w_up_t", "w_down", grad_final_nw]
    ms = [m_meta_tokens, m_attn_norm_w, m_w_in, m_gdn_conv_w, m_gdn_a_log, m_gdn_dt_bias, m_gdn_norm_w,
          m_gla_gate_w2, m_gla_gate_b, m_gla_norm_w, m_w_out, m_ffn_norm_w, m_w_gate, m_w_up, m_w_down, m_final_norm_w]
    vs = [v_meta_tokens, v_attn_norm_w, v_w_in, v_gdn_conv_w, v_gdn_a_log, v_gdn_dt_bias, v_gdn_norm_w,
          v_gla_gate_w2, v_gla_gate_b, v_gla_norm_w, v_w_out, v_ffn_norm_w, v_w_gate, v_w_up, v_w_down, v_final_norm_w]
    outs = [[], [], [], []]
    for idx, (w, gr, m, v) in enumerate(zip(weights, grads, ms, vs)):
        if isinstance(gr, str):
            res = big[gr]
        else:
            gr = gr.reshape(w.shape)
            res = (gr,) + _adamw(w, gr, m, v, name=f"adamw_{idx}")
        for lst, t in zip(outs, res):
            lst.append(t)
    return (loss, g["grad_x"][None], *outs[0], *outs[1], *outs[2], *outs[3])
```

```python
import functools

import jax
import jax.numpy as jnp
from jax import lax
from jax.experimental import pallas as pl
from jax.experimental.pallas import tpu as pltpu

F32 = jnp.float32
BF16 = jnp.bfloat16
_MXU_DTYPE = jnp.bfloat16

D_MODEL = 2048
N_META = 16
ROW_PAD = 48
HEAD_ROWS = ROW_PAD + N_META
CONV_K = 4
GDN_HEADS, GDN_DK, GDN_DV, GDN_CHUNK = 8, 128, 128, 64
GLA_HEADS, GLA_DK, GLA_DV, GLA_CHUNK = 4, 128, 256, 16
GLA_RANK = 16
GLA_GATE_NORMALIZER = 16.0
GDN_QK = GDN_HEADS * GDN_DK
GDN_V = GDN_HEADS * GDN_DV
GLA_QK = GLA_HEADS * GLA_DK
GLA_V = GLA_HEADS * GLA_DV
D_FF = 5632
D_IN = 7200
NORM_EPS = 1e-6
C_Z, C_GR, C_GQ, C_GK, C_GV, C_QKV, C_SM = 0, 1024, 2048, 2560, 3072, 4096, 7168
SM_W = 128
D_PROJ = 7680
R_Z, R_A, R_B, R_GQ, R_GK, R_GV, R_GR, R_LR = 3072, 4096, 4104, 4112, 4624, 5136, 6160, 7184

ADAM_LR, ADAM_B1, ADAM_B2, ADAM_EPS, ADAM_WD, ADAM_STEP = 0.001, 0.9, 0.999, 1e-08, 0.01, 10

N_DEV = 8
VMEM_LIMIT = 56 * 1024 * 1024

NN = (((1,), (0,)), ((), ()))
NT = (((1,), (1,)), ((), ()))
TN = (((0,), (0,)), ((), ()))


def _dot(a, b, dims=NN):
    return lax.dot_general(a.astype(_MXU_DTYPE), b.astype(_MXU_DTYPE), dims, preferred_element_type=F32)


def _running_sum(x, reverse=False):
    n = x.shape[0]
    row = lax.broadcasted_iota(jnp.int32, x.shape, 0)
    s = 1
    while s < n:
        if reverse:
            x = x + jnp.where(row < n - s, pltpu.roll(x, n - s, 0), 0.0)
        else:
            x = x + jnp.where(row >= s, pltpu.roll(x, s, 0), 0.0)
        s *= 2
    return x


def _dot3(a, b):
    ah = a.astype(BF16)
    al = (a - ah.astype(F32)).astype(BF16)
    bh = b.astype(BF16)
    bl = (b - bh.astype(F32)).astype(BF16)
    d = functools.partial(lax.dot_general, dimension_numbers=NN, preferred_element_type=F32)
    return d(ah, bh) + (d(ah, bl) + d(al, bh))


def _tile(n, target, mult=8):
    best = None
    for t in range(mult, min(n, target) + 1, mult):
        if n % t == 0:
            best = t
    return best if best is not None else n


def _params(*sem):
    return pltpu.CompilerParams(dimension_semantics=sem, vmem_limit_bytes=VMEM_LIMIT)


def _sigmoid(x):
    return 0.5 * jnp.tanh(0.5 * x) + 0.5


def _softplus(x):
    return jnp.maximum(x, 0.0) + jnp.log1p(jnp.exp(-jnp.abs(x)))


def _silu_and_grad(c):
    s = _sigmoid(c)
    return c * s, s * (1.0 + c * (1.0 - s))


_ANY = pl.BlockSpec(memory_space=pl.ANY)


def _matmul(a, b, *, mode, name, out_dtype=F32, add=None, after=None, tm=1376, tn=512):
    if mode == "tn":
        K, M = a.shape
        N = b.shape[1]
    else:
        M, K = a.shape
        N = b.shape[0] if mode == "nt" else b.shape[1]
    tm = _tile(M, tm, 128 if mode == "tn" else 16)
    tn = _tile(N, tn, 128)
    dims = {"nn": NN, "nt": NT, "tn": TN}[mode]
    n_after = 0 if after is None else 1

    def body(*refs):
        refs = refs[n_after:]
        r = _dot(refs[0][...], refs[1][...], dims)
        if add is not None:
            r = r + refs[2][...]
        refs[-1][...] = r.astype(out_dtype)

    a_spec = pl.BlockSpec((K, tm), lambda i, j: (0, i)) if mode == "tn" else pl.BlockSpec((tm, K), lambda i, j: (i, 0))
    b_spec = pl.BlockSpec((tn, K), lambda i, j: (j, 0)) if mode == "nt" else pl.BlockSpec((K, tn), lambda i, j: (0, j))
    o_spec = pl.BlockSpec((tm, tn), lambda i, j: (i, j))
    in_specs = [_ANY] * n_after + [a_spec, b_spec] + ([o_spec] if add is not None else [])
    args = ((after,) if n_after else ()) + (a, b) + ((add,) if add is not None else ())
    return pl.pallas_call(
        body, name=name, grid=(M // tm, N // tn), in_specs=in_specs, out_specs=o_spec,
        out_shape=jax.ShapeDtypeStruct((M, N), out_dtype), compiler_params=_params("parallel", "parallel"),
    )(*args)


def _in_proj(n, w_in_t, *, name, tm=1376, tn=512):
    M, K = n.shape
    N = w_in_t.shape[0]
    tm, tn = _tile(M, tm, 16), _tile(N, tn, 128)
    assert C_SM % tn == 0
    j_small = C_SM // tn

    def body(n_ref, w_ref, o_ref, sm_ref):
        r = _dot(n_ref[...], w_ref[...], NT)
        o_ref[...] = r.astype(o_ref.dtype)

        @pl.when(pl.program_id(1) == j_small)
        def _():
            sm_ref[...] = r[:, 0:SM_W]

    return pl.pallas_call(
        body, name=name, grid=(M // tm, N // tn),
        in_specs=[pl.BlockSpec((tm, K), lambda i, j: (i, 0)), pl.BlockSpec((tn, K), lambda i, j: (j, 0))],
        out_specs=[pl.BlockSpec((tm, tn), lambda i, j: (i, j)), pl.BlockSpec((tm, SM_W), lambda i, j: (i, 0))],
        out_shape=[jax.ShapeDtypeStruct((M, N), BF16), jax.ShapeDtypeStruct((M, SM_W), F32)],
        compiler_params=_params("parallel", "arbitrary"),
    )(n, w_in_t)


def _matmul_pair(a1, b1, a2, b2, *, name, after=None, tm=688, tn=256):
    M, K = a1.shape
    N = b1.shape[1]
    tm, tn = _tile(M, tm, 16), _tile(N, tn, 128)
    n_after = 0 if after is None else 1

    def body(*refs):
        a1_ref, b1_ref, a2_ref, b2_ref, o_ref = refs[n_after:]
        o_ref[...] = _dot(a1_ref[...], b1_ref[...]) + _dot(a2_ref[...], b2_ref[...])

    a_spec = pl.BlockSpec((tm, K), lambda i, j: (i, 0))
    b_spec = pl.BlockSpec((K, tn), lambda i, j: (0, j))
    return pl.pallas_call(
        body, name=name, grid=(M // tm, N // tn), in_specs=[_ANY] * n_after + [a_spec, b_spec, a_spec, b_spec],
        out_specs=pl.BlockSpec((tm, tn), lambda i, j: (i, j)), out_shape=jax.ShapeDtypeStruct((M, N), F32),
        compiler_params=_params("parallel", "parallel"),
    )(*((after,) if n_after else ()), a1, b1, a2, b2)


def _rmsnorm_fwd(h, w, *, name):
    M, D = h.shape
    tm = _tile(M, 688, 16)

    def body(h_ref, w_ref, n_ref):
        x = h_ref[...]
        r = lax.rsqrt(jnp.mean(x * x, axis=-1, keepdims=True) + NORM_EPS)
        n_ref[...] = (x * r * w_ref[...]).astype(n_ref.dtype)

    return pl.pallas_call(
        body, name=name, grid=(M // tm,),
        in_specs=[pl.BlockSpec((tm, D), lambda i: (i, 0)), pl.BlockSpec((1, D), lambda i: (0, 0))],
        out_specs=pl.BlockSpec((tm, D), lambda i: (i, 0)),
        out_shape=jax.ShapeDtypeStruct((M, D), BF16),
        compiler_params=_params("parallel"),
    )(h, w)


SEQ_BLOCK = HEAD_ROWS


def _seq_blocks_per_tile(rows):
    n = rows // SEQ_BLOCK
    return max(m for m in (1, 2, 3, 4) if n % m == 0)


def _seq_specs(m, D):
    return [pl.BlockSpec((SEQ_BLOCK, D), functools.partial(lambda i, k: (jnp.maximum(m * i + k - 1, 0), 0), k=k))
            for k in range(m)]


def _embed_norm(head, x, w, *, name, after=None):
    S, D = x.shape
    m = _seq_blocks_per_tile(S + HEAD_ROWS)
    n_after = 0 if after is None else 1

    def body(*refs):
        refs = refs[n_after:]
        head_ref, x_refs, w_ref, h_ref, n_ref = refs[0], refs[1:1 + m], refs[1 + m], refs[2 + m], refs[3 + m]
        i = pl.program_id(0)
        for k in range(m):
            blk = x_refs[k][...]
            if k == 0:
                blk = jnp.where(i == 0, head_ref[...], blk)
            rows = slice(k * SEQ_BLOCK, (k + 1) * SEQ_BLOCK)
            h_ref[rows, :] = blk
            r = lax.rsqrt(jnp.mean(blk * blk, axis=-1, keepdims=True) + NORM_EPS)
            n_ref[rows, :] = (blk * r * w_ref[...]).astype(n_ref.dtype)

    tile = pl.BlockSpec((m * SEQ_BLOCK, D), lambda i: (i, 0))
    return pl.pallas_call(
        body, name=name, grid=((S + HEAD_ROWS) // (m * SEQ_BLOCK),),
        in_specs=[_ANY] * n_after + [pl.BlockSpec((SEQ_BLOCK, D), lambda i: (0, 0))] + _seq_specs(m, D)
        + [pl.BlockSpec((1, D), lambda i: (0, 0))],
        out_specs=[tile, tile],
        out_shape=[jax.ShapeDtypeStruct((S + HEAD_ROWS, D), F32), jax.ShapeDtypeStruct((S + HEAD_ROWS, D), BF16)],
        compiler_params=_params("parallel"),
    )(*((after,) if n_after else ()), head, *([x] * m), w)


def _embed_norm_bwd(h, w, dn, dres, *, name):
    M, D = h.shape
    S = M - HEAD_ROWS
    m = _seq_blocks_per_tile(S)
    g = S // (m * SEQ_BLOCK)

    def one(x, dn_, dres_, w_):
        r = lax.rsqrt(jnp.mean(x * x, axis=-1, keepdims=True) + NORM_EPS)
        xhat = x * r
        dxhat = dn_ * w_
        dh = dres_ + r * (dxhat - xhat * jnp.mean(dxhat * xhat, axis=-1, keepdims=True))
        return dh, jnp.sum((dn_ * xhat).reshape(SEQ_BLOCK // 8, 8, D), axis=0)

    def body(*refs):
        w_ref = refs[0]
        groups = [refs[1 + a * (m + 1):1 + (a + 1) * (m + 1)] for a in range(3)]
        gx_ref, dhead_ref, dw_ref, acc_ref = refs[1 + 3 * (m + 1):]
        i = pl.program_id(0)
        w_ = w_ref[...]
        part = jnp.zeros((8, D), F32)
        for k in range(m):
            dh, p = one(*(grp[1 + k][...] for grp in groups), w_)
            gx_ref[k * SEQ_BLOCK:(k + 1) * SEQ_BLOCK, :] = dh
            part = part + p

        @pl.when(i == 0)
        def _():
            dh, p = one(*(grp[0][...] for grp in groups), w_)
            dhead_ref[...] = dh
            acc_ref[...] = part + p

        @pl.when(i > 0)
        def _():
            acc_ref[...] += part

        @pl.when(i == g - 1)
        def _():
            dw_ref[...] = jnp.sum(acc_ref[...], axis=0, keepdims=True)

    first = pl.BlockSpec((SEQ_BLOCK, D), lambda i: (0, 0))
    blocks = [pl.BlockSpec((SEQ_BLOCK, D), functools.partial(lambda i, k: (m * i + k + 1, 0), k=k)) for k in range(m)]
    vec = pl.BlockSpec((1, D), lambda i: (0, 0))
    return pl.pallas_call(
        body, name=name, grid=(g,), in_specs=[vec] + ([first] + blocks) * 3,
        out_specs=[pl.BlockSpec((m * SEQ_BLOCK, D), lambda i: (i, 0)), first, vec],
        out_shape=[jax.ShapeDtypeStruct((S, D), F32), jax.ShapeDtypeStruct((SEQ_BLOCK, D), F32),
                   jax.ShapeDtypeStruct((1, D), F32)],
        scratch_shapes=[pltpu.VMEM((8, D), F32)],
        compiler_params=_params("arbitrary"),
    )(w, *([h] * (m + 1)), *([dn] * (m + 1)), *([dres] * (m + 1)))


def _rmsnorm_bwd(h, w, dn, dres, *, name):
    M, D = h.shape
    tm = _tile(M, 344, 16)
    g = M // tm

    def body(h_ref, w_ref, dn_ref, dres_ref, dh_ref, dhb_ref, dw_ref, acc_ref):
        i = pl.program_id(0)
        x = h_ref[...]
        r = lax.rsqrt(jnp.mean(x * x, axis=-1, keepdims=True) + NORM_EPS)
        xhat = x * r
        dn_ = dn_ref[...]
        dxhat = dn_ * w_ref[...]
        dh = dres_ref[...] + r * (dxhat - xhat * jnp.mean(dxhat * xhat, axis=-1, keepdims=True))
        dh_ref[...] = dh
        dhb_ref[...] = dh.astype(dhb_ref.dtype)
        part = jnp.sum((dn_ * xhat).reshape(tm // 8, 8, D), axis=0)

        @pl.when(i == 0)
        def _():
            acc_ref[...] = part

        @pl.when(i > 0)
        def _():
            acc_ref[...] += part

        @pl.when(i == g - 1)
        def _():
            dw_ref[...] = jnp.sum(acc_ref[...], axis=0, keepdims=True)

    row = pl.BlockSpec((tm, D), lambda i: (i, 0))
    vec = pl.BlockSpec((1, D), lambda i: (0, 0))
    return pl.pallas_call(
        body, name=name, grid=(g,), in_specs=[row, vec, row, row],
        out_specs=[row, row, vec],
        out_shape=[jax.ShapeDtypeStruct((M, D), F32), jax.ShapeDtypeStruct((M, D), BF16),
                   jax.ShapeDtypeStruct((1, D), F32)],
        scratch_shapes=[pltpu.VMEM((8, D), F32)],
        compiler_params=_params("arbitrary"),
    )(h, w, dn, dres)


def _loss_head(h, w, target, *, name):
    M, D = h.shape
    m = _seq_blocks_per_tile(M)
    tm = m * SEQ_BLOCK
    g = M // tm

    def body(h_ref, w_ref, *rest):
        t_refs = rest[:m]
        dh_ref, dhb_ref, dw_ref, loss_ref, acc_ref, lacc_ref = rest[m:]
        i = pl.program_id(0)
        x = h_ref[...]
        row = i * tm + lax.broadcasted_iota(jnp.int32, (tm, 1), 0)
        live = row >= HEAD_ROWS
        r = lax.rsqrt(jnp.mean(x * x, axis=-1, keepdims=True) + NORM_EPS)
        xhat = x * r
        t = jnp.concatenate([t_ref[...] for t_ref in t_refs], axis=0)
        err = jnp.where(live, xhat * w_ref[...] - t, 0.0)
        dy = err * (1.0 / D)
        dxhat = dy * w_ref[...]
        dh = r * (dxhat - xhat * jnp.mean(dxhat * xhat, axis=-1, keepdims=True))
        dh_ref[...] = dh
        dhb_ref[...] = dh.astype(dhb_ref.dtype)
        part = jnp.sum((dy * xhat).reshape(tm // 8, 8, D), axis=0)
        lpart = jnp.sum((err * err).reshape(tm // 8, 8, D), axis=0)

        @pl.when(i == 0)
        def _():
            acc_ref[...] = part
            lacc_ref[...] = lpart

        @pl.when(i > 0)
        def _():
            acc_ref[...] += part
            lacc_ref[...] += lpart

        @pl.when(i == g - 1)
        def _():
            dw_ref[...] = jnp.sum(acc_ref[...], axis=0, keepdims=True)
            tot = jnp.sum(jnp.sum(lacc_ref[...], axis=0, keepdims=True), axis=1, keepdims=True)
            loss_ref[...] = jnp.broadcast_to(tot * (0.5 / D), (1, 128))

    row = pl.BlockSpec((tm, D), lambda i: (i, 0))
    vec = pl.BlockSpec((1, D), lambda i: (0, 0))
    return pl.pallas_call(
        body, name=name, grid=(g,), in_specs=[row, vec] + _seq_specs(m, D),
        out_specs=[row, row, vec, pl.BlockSpec((1, 128), lambda i: (0, 0))],
        out_shape=[jax.ShapeDtypeStruct((M, D), F32), jax.ShapeDtypeStruct((M, D), BF16),
                   jax.ShapeDtypeStruct((1, D), F32), jax.ShapeDtypeStruct((1, 128), F32)],
        scratch_shapes=[pltpu.VMEM((8, D), F32), pltpu.VMEM((8, D), F32)],
        compiler_params=_params("arbitrary"),
    )(h, w, *([target] * m))


def _gate_terms(sm, w2p, b2, alog_p, dt_p, row0):
    tm = sm.shape[0]
    lane = lax.broadcasted_iota(jnp.int32, (tm, SM_W), 1)
    live = (row0 + lax.broadcasted_iota(jnp.int32, (tm, 1), 0)) >= ROW_PAD
    pre = sm + dt_p
    neg_a = -jnp.exp(alog_p)
    g = neg_a * _softplus(pre)
    beta = _sigmoid(sm)
    z = _dot(sm, w2p) + b2
    return lane, live, pre, neg_a, g, beta, z


def _gates_fwd(sm, w2p, b2, alog_p, dt_p, *, name):
    M = sm.shape[0]
    tm = _tile(M, 688, 8)

    def body(sm_ref, w2_ref, b2_ref, al_ref, dt_ref, gb_ref, la_ref):
        row0 = pl.program_id(0) * tm
        lane, live, _, _, g, beta, z = _gate_terms(sm_ref[...].astype(F32), w2_ref[...], b2_ref[...], al_ref[...], dt_ref[...], row0)
        gb = jnp.where(lane < GDN_HEADS, g, jnp.where(lane < 2 * GDN_HEADS, beta, 0.0))
        gb_ref[...] = jnp.where(live, gb, 0.0)
        la = (jnp.minimum(z, 0.0) - jnp.log1p(jnp.exp(-jnp.abs(z)))) * (1.0 / GLA_GATE_NORMALIZER)
        la_ref[...] = jnp.where(live, la, 0.0)

    full = lambda s: pl.BlockSpec(s, lambda i: (0, 0))
    return pl.pallas_call(
        body, name=name, grid=(M // tm,),
        in_specs=[pl.BlockSpec((tm, SM_W), lambda i: (i, 0)), full((SM_W, GLA_QK)), full((1, GLA_QK)),
                  full((1, SM_W)), full((1, SM_W))],
        out_specs=[pl.BlockSpec((tm, SM_W), lambda i: (i, 0)), pl.BlockSpec((tm, GLA_QK), lambda i: (i, 0))],
        out_shape=[jax.ShapeDtypeStruct((M, SM_W), F32), jax.ShapeDtypeStruct((M, GLA_QK), F32)],
        compiler_params=_params("parallel"),
    )(sm, w2p, b2, alog_p, dt_p)


def _gates_bwd(sm, w2p, b2, alog_p, dt_p, dgb_heads, dla, d_proj, *, name):
    M = sm.shape[0]
    tm = _tile(M, 688, 8)
    g_ = M // tm

    tail_w = D_PROJ - C_SM

    def body(sm_ref, w2_ref, b2_ref, al_ref, dt_ref, dgb_ref, dla_ref, _,
             dsm_ref, dw2_ref, db2_ref, dal_ref, ddt_ref):
        i = pl.program_id(0)
        sm = sm_ref[...].astype(F32)
        lane, live, pre, neg_a, g, beta, z = _gate_terms(sm, w2_ref[...], b2_ref[...], al_ref[...], dt_ref[...], i * tm)
        dz = jnp.where(live, dla_ref[...] * (_sigmoid(-z) * (1.0 / GLA_GATE_NORMALIZER)), 0.0)
        dsm_lr = _dot(dz, w2_ref[...], NT)
        dgb = dgb_ref[0]
        for hh in range(1, GDN_HEADS):
            dgb = dgb + dgb_ref[hh]
        dgb = jnp.where(live, dgb, 0.0)
        da = dgb * neg_a * _sigmoid(pre)
        db = dgb * beta * (1.0 - beta)
        dsm = jnp.where(lane < GDN_HEADS, da, jnp.where(lane < 2 * GDN_HEADS, db, dsm_lr))
        dsm_ref[:, 0:SM_W] = dsm.astype(dsm_ref.dtype)
        if tail_w > SM_W:
            dsm_ref[:, SM_W:tail_w] = jnp.zeros((tm, tail_w - SM_W), dsm_ref.dtype)
        is_a = lane < GDN_HEADS
        dal = jnp.sum(jnp.where(is_a, dgb * g, 0.0), axis=0, keepdims=True)
        ddt = jnp.sum(jnp.where(is_a, da, 0.0), axis=0, keepdims=True)
        dw2 = _dot(sm, dz, TN)
        db2 = jnp.sum(dz, axis=0, keepdims=True)

        @pl.when(i == 0)
        def _():
            dw2_ref[...] = dw2
            db2_ref[...] = db2
            dal_ref[...] = dal
            ddt_ref[...] = ddt

        @pl.when(i > 0)
        def _():
            dw2_ref[...] += dw2
            db2_ref[...] += db2
            dal_ref[...] += dal
            ddt_ref[...] += ddt

    full = lambda s: pl.BlockSpec(s, lambda i: (0, 0))
    return pl.pallas_call(
        body, name=name, grid=(g_,),
        in_specs=[pl.BlockSpec((tm, SM_W), lambda i: (i, 0)), full((SM_W, GLA_QK)), full((1, GLA_QK)),
                  full((1, SM_W)), full((1, SM_W)),
                  pl.BlockSpec((GDN_HEADS, tm, SM_W), lambda i: (0, i, 0)),
                  pl.BlockSpec((tm, GLA_QK), lambda i: (i, 0)), _ANY],
        out_specs=[pl.BlockSpec((tm, tail_w), lambda i: (i, C_SM // tail_w)), full((SM_W, GLA_QK)), full((1, GLA_QK)),
                   full((1, SM_W)), full((1, SM_W))],
        out_shape=[jax.ShapeDtypeStruct(d_proj.shape, d_proj.dtype), jax.ShapeDtypeStruct((SM_W, GLA_QK), F32),
                   jax.ShapeDtypeStruct((1, GLA_QK), F32), jax.ShapeDtypeStruct((1, SM_W), F32),
                   jax.ShapeDtypeStruct((1, SM_W), F32)],
        input_output_aliases={7: 0},
        compiler_params=_params("arbitrary"),
    )(sm, w2p, b2, alog_p, dt_p, dgb_heads, dla, d_proj)


QKV_W = GDN_QK
N_QKV_GROUPS = 3
QKV_B0 = C_QKV // QKV_W
HALO = 16


def _conv_terms(x_ref, halo_ref, cw_ref, xs_ref, i, tm):
    xs_ref[HALO:HALO + tm, :] = x_ref[...].astype(F32)
    xs_ref[0:HALO, :] = jnp.where(i > 0, halo_ref[...].astype(F32), 0.0)
    cw = cw_ref[...]
    xs = xs_ref[...]
    taps = [(pltpu.roll(xs, CONV_K - 1 - t, 0) if t < CONV_K - 1 else xs)[HALO:HALO + tm, :] for t in range(CONV_K)]
    c = taps[0] * cw[0:1, :]
    for t in range(1, CONV_K):
        c = c + taps[t] * cw[t:t + 1, :]
    return c, taps


def _prep_fwd(proj, conv_w8, *, name):
    M = proj.shape[0]
    tm = _tile(M, 688, 16)

    def body(x_ref, halo_ref, cw_ref, o_ref, xs_ref):
        j, i = pl.program_id(0), pl.program_id(1)
        c, _ = _conv_terms(x_ref, halo_ref, cw_ref, xs_ref, i, tm)
        s, _ = _silu_and_grad(c)
        scale = jnp.where(j == 0, GDN_DK ** -0.5, 1.0)
        for hh in range(GDN_HEADS):
            cols = slice(hh * 128, (hh + 1) * 128)
            sh = s[:, cols]
            r = lax.rsqrt(jnp.sum(sh * sh, axis=-1, keepdims=True) + NORM_EPS)
            o_ref[:, cols] = jnp.where(j < 2, sh * (r * scale), sh)

    hb = tm // HALO
    return pl.pallas_call(
        body, name=name, grid=(N_QKV_GROUPS, M // tm),
        in_specs=[pl.BlockSpec((tm, QKV_W), lambda j, i: (i, QKV_B0 + j)),
                  pl.BlockSpec((HALO, QKV_W), lambda j, i: (jnp.maximum(i * hb - 1, 0), QKV_B0 + j)),
                  pl.BlockSpec((8, QKV_W), lambda j, i: (0, j))],
        out_specs=pl.BlockSpec((tm, QKV_W), lambda j, i: (i, j)),
        out_shape=jax.ShapeDtypeStruct((M, N_QKV_GROUPS * QKV_W), F32),
        scratch_shapes=[pltpu.VMEM((tm + HALO, QKV_W), F32)],
        compiler_params=_params("parallel", "arbitrary"),
    )(proj, proj, conv_w8)


def _prep_bwd(proj, conv_w8, dact, d_proj, *, name):
    M = proj.shape[0]
    tm = _tile(M, 688, 16)
    g_ = M // tm
    ext = tm + HALO

    def body(x_ref, prev_ref, next_ref, cw_ref, da_ref, dan_ref, _, o_ref, dcw_ref, xs_ref, das_ref, dcs_ref):
        j, i = pl.program_id(0), pl.program_id(1)
        not_last = i < g_ - 1
        xs_ref[0:HALO, :] = jnp.where(i > 0, prev_ref[...].astype(F32), 0.0)
        xs_ref[HALO:HALO + tm, :] = x_ref[...].astype(F32)
        xs_ref[HALO + tm:HALO + ext, :] = jnp.where(not_last, next_ref[...].astype(F32), 0.0)
        das_ref[0:tm, :] = da_ref[...]
        das_ref[tm:ext, :] = jnp.where(not_last, dan_ref[...], 0.0)
        cw = cw_ref[...]
        xs = xs_ref[...]
        taps = [(pltpu.roll(xs, CONV_K - 1 - t, 0) if t < CONV_K - 1 else xs)[HALO:HALO + ext, :] for t in range(CONV_K)]
        c = taps[0] * cw[0:1, :]
        for t in range(1, CONV_K):
            c = c + taps[t] * cw[t:t + 1, :]
        s, ds_dc = _silu_and_grad(c)
        scale = jnp.where(j == 0, GDN_DK ** -0.5, 1.0)
        for hh in range(GDN_HEADS):
            cols = slice(hh * 128, (hh + 1) * 128)
            sh = s[:, cols]
            r = lax.rsqrt(jnp.sum(sh * sh, axis=-1, keepdims=True) + NORM_EPS)
            da = das_ref[:, cols]
            y = sh * r
            dy = da * scale
            ds_norm = r * (dy - y * jnp.sum(dy * y, axis=-1, keepdims=True))
            dcs_ref[:, cols] = jnp.where(j < 2, ds_norm, da) * ds_dc[:, cols]
        dc = dcs_ref[...]
        acc = dc[0:tm, :] * cw[CONV_K - 1:CONV_K, :]
        for t in range(CONV_K - 1):
            acc = acc + pltpu.roll(dc, ext - (CONV_K - 1 - t), 0)[0:tm, :] * cw[t:t + 1, :]
        o_ref[...] = acc.astype(o_ref.dtype)
        r8 = lax.broadcasted_iota(jnp.int32, (8, QKV_W), 0)
        part = jnp.zeros((8, QKV_W), F32)
        for t in range(CONV_K):
            part = jnp.where(r8 == t, jnp.sum(dc[0:tm, :] * taps[t][0:tm, :], axis=0, keepdims=True), part)

        @pl.when(i == 0)
        def _():
            dcw_ref[...] = part

        @pl.when(i > 0)
        def _():
            dcw_ref[...] += part

    hb = tm // HALO
    last = M // HALO - 1
    prev_of = lambda i: jnp.maximum(i * hb - 1, 0)
    next_of = lambda i: jnp.minimum((i + 1) * hb, last)
    return pl.pallas_call(
        body, name=name, grid=(N_QKV_GROUPS, g_),
        in_specs=[pl.BlockSpec((tm, QKV_W), lambda j, i: (i, QKV_B0 + j)),
                  pl.BlockSpec((HALO, QKV_W), lambda j, i: (prev_of(i), QKV_B0 + j)),
                  pl.BlockSpec((HALO, QKV_W), lambda j, i: (next_of(i), QKV_B0 + j)),
                  pl.BlockSpec((8, QKV_W), lambda j, i: (0, j)),
                  pl.BlockSpec((tm, QKV_W), lambda j, i: (i, j)),
                  pl.BlockSpec((HALO, QKV_W), lambda j, i: (next_of(i), j)), _ANY],
        out_specs=[pl.BlockSpec((tm, QKV_W), lambda j, i: (i, QKV_B0 + j)), pl.BlockSpec((8, QKV_W), lambda j, i: (0, j))],
        out_shape=[jax.ShapeDtypeStruct(d_proj.shape, d_proj.dtype),
                   jax.ShapeDtypeStruct((8, N_QKV_GROUPS * QKV_W), F32)],
        input_output_aliases={6: 0},
        scratch_shapes=[pltpu.VMEM((HALO + ext, QKV_W), F32), pltpu.VMEM((ext, QKV_W), F32), pltpu.VMEM((ext, QKV_W), F32)],
        compiler_params=_params("parallel", "arbitrary"),
    )(proj, proj, proj, conv_w8, dact, dact, d_proj)


def _round_robin(gens):
    gens = list(gens)
    while gens:
        alive = []
        for gen in gens:
            try:
                next(gen)
                alive.append(gen)
            except StopIteration:
                pass
        gens = alive


def _unit_lower_inverse(a_low, eye):
    n = a_low.shape[0]
    ri = lax.broadcasted_iota(jnp.int32, (n, n), 0)
    ci = lax.broadcasted_iota(jnp.int32, (n, n), 1)
    same = lambda shift: (ri >> shift) == (ci >> shift)
    b = jnp.where(same(3), -a_low, 0.0)
    x = eye + b
    p2 = _dot3(b, b)
    yield
    x = x + _dot3(x, p2)
    p4 = _dot3(p2, p2)
    yield
    x = x + _dot3(x, p4)
    yield
    for shift in (3, 4, 5):
        between = jnp.where(same(shift + 1) & ~same(shift), a_low, 0.0)
        t = _dot3(between, x)
        yield
        x = x - _dot3(x, t)
        yield
    return x


class _GdnChunk:
    def build(self, q, k, v, gb, h, sum_on_mxu):
        C = GDN_CHUNK
        lane = lax.broadcasted_iota(jnp.int32, (C, SM_W), 1)
        g = jnp.sum(jnp.where(lane == h, gb, 0.0), axis=1, keepdims=True)
        self.beta = jnp.sum(jnp.where(lane == h + GDN_HEADS, gb, 0.0), axis=1, keepdims=True)
        ri = lax.broadcasted_iota(jnp.int32, (C, C), 0)
        ci = lax.broadcasted_iota(jnp.int32, (C, C), 1)
        self.causal = ri >= ci
        self.strict = ri > ci
        self.eye = (ri == ci).astype(F32)
        if sum_on_mxu:
            gcb = lax.dot_general(self.causal.astype(F32), jnp.broadcast_to(g, (C, SM_W)), NN,
                                  precision=lax.Precision.HIGHEST, preferred_element_type=F32)
        else:
            gcb = _running_sum(jnp.broadcast_to(g, (C, SM_W)))
        yield
        self.gcol = gcb[:, 0:1]
        grow = gcb.T[0:1, 0:C]
        self.decay = jnp.exp(jnp.where(self.causal, self.gcol - grow, -1e30))
        self.egc = jnp.exp(self.gcol)
        glast = gcb[C - 1:C, 0:1]
        self.elast = jnp.exp(glast - self.gcol)
        self.gl = jnp.exp(glast)
        self.q, self.k, self.v = q, k, v
        self.kb = k * self.beta
        m = _dot(self.kb, k, NT)
        n_ = _dot(q, k, NT)
        yield
        self.a_low = jnp.where(self.strict, m * self.decay, 0.0)
        self.p = n_ * self.decay
        self.qd = q * self.egc
        self.kd = k * self.elast
        self.bu = v * self.beta
        self.bw = self.kb * self.egc


GDN_HB = 8
GDN_HG = GDN_HEADS // GDN_HB


def _gdn_specs(n_of):
    C, W = GDN_CHUNK, 128 * GDN_HB
    q_spec = pl.BlockSpec((C, W), lambda g, n: (n_of(n), g))
    k_spec = pl.BlockSpec((C, W), lambda g, n: (n_of(n), g + GDN_HG))
    v_spec = pl.BlockSpec((C, W), lambda g, n: (n_of(n), g + 2 * GDN_HG))
    gb_spec = pl.BlockSpec((C, SM_W), lambda g, n: (n_of(n), 0))
    o_spec = pl.BlockSpec((C, W), lambda g, n: (n_of(n), g))
    s_spec = pl.BlockSpec((GDN_HB, None, GDN_DK, GDN_DV), lambda g, n: (g, n_of(n), 0, 0))
    t_spec = pl.BlockSpec((GDN_HB, None, C, C), lambda g, n: (g, n_of(n), 0, 0))
    return q_spec, k_spec, v_spec, gb_spec, o_spec, s_spec, t_spec


def _gdn_fwd(act, gb, *, name):
    M = act.shape[0]
    N = M // GDN_CHUNK

    def body(q_ref, k_ref, v_ref, gb_ref, o_ref, s_ref, t_ref, state):
        g, n = pl.program_id(0), pl.program_id(1)

        @pl.when(n == 0)
        def _():
            state[...] = jnp.zeros_like(state)

        gb_ = gb_ref[...]

        def head(hh):
            cols = slice(hh * 128, (hh + 1) * 128)
            c = _GdnChunk()
            yield from c.build(q_ref[:, cols], k_ref[:, cols], v_ref[:, cols], gb_, g * GDN_HB + hh, sum_on_mxu=True)
            tinv = yield from _unit_lower_inverse(c.a_low, c.eye)
            s = state[hh]
            s_ref[hh] = s
            t_ref[hh] = tinv
            u = _dot(tinv, c.bu)
            w = _dot(tinv, c.bw)
            yield
            vn = u - _dot(w, s)
            o1 = _dot(c.qd, s)
            yield
            o_ref[:, cols] = (o1 + _dot(c.p, vn)).astype(o_ref.dtype)
            state[hh] = c.gl * s + _dot(c.kd, vn, TN)

        _round_robin(head(hh) for hh in range(GDN_HB))

    q_spec, k_spec, v_spec, gb_spec, o_spec, s_spec, t_spec = _gdn_specs(lambda n: n)
    return pl.pallas_call(
        body, name=name, grid=(GDN_HG, N),
        in_specs=[q_spec, k_spec, v_spec, gb_spec], out_specs=[o_spec, s_spec, t_spec],
        out_shape=[jax.ShapeDtypeStruct((M, GDN_V), BF16),
                   jax.ShapeDtypeStruct((GDN_HEADS, N, GDN_DK, GDN_DV), F32),
                   jax.ShapeDtypeStruct((GDN_HEADS, N, GDN_CHUNK, GDN_CHUNK), F32)],
        scratch_shapes=[pltpu.VMEM((GDN_HB, GDN_DK, GDN_DV), F32)],
        compiler_params=_params("parallel", "arbitrary"),
    )(act, act, act, gb)


def _gdn_bwd(act, gb, do, s_all, t_all, *, name):
    M = act.shape[0]
    N = M // GDN_CHUNK
    C = GDN_CHUNK
    assert GDN_HG == 1

    def body(q_ref, k_ref, v_ref, gb_ref, do_ref, s_ref, t_ref, dact_ref, dgb_ref, dstate):
        g, n = pl.program_id(0), pl.program_id(1)

        @pl.when(n == 0)
        def _():
            dstate[...] = jnp.zeros_like(dstate)

        gb_ = gb_ref[...]
        last = lax.broadcasted_iota(jnp.int32, (C, 1), 0) == C - 1
        lane = lax.broadcasted_iota(jnp.int32, (C, SM_W), 1)
        def head(hh):
            cols = slice(hh * 128, (hh + 1) * 128)
            h = g * GDN_HB + hh
            c = _GdnChunk()
            yield from c.build(q_ref[:, cols], k_ref[:, cols], v_ref[:, cols], gb_, h, sum_on_mxu=False)
            tinv = t_ref[hh]
            tinv_t = tinv.T
            s = s_ref[hh]
            do_ = do_ref[:, cols]
            ds1 = dstate[hh]
            u = _dot(tinv, c.bu)
            w = _dot(tinv, c.bw)
            dqd = _dot(do_, s, NT)
            yield
            dvn0 = _dot(c.p, do_, TN) + _dot(c.kd, ds1)
            dst0 = _dot(c.qd, do_, TN) + c.gl * ds1
            yield
            vn = u - _dot(w, s)
            dvn = dvn0
            yield
            dp = jnp.where(c.causal, _dot(do_, vn, NT), 0.0)
            dstate[hh] = dst0 - _dot(w, dvn, TN)
            dkd = _dot(vn, ds1, NT)
            dw = -_dot(dvn, s, NT)
            dbu = _dot(tinv_t, dvn)
            dgl = jnp.sum(jnp.sum(s * ds1, axis=1, keepdims=True), axis=0, keepdims=True)
            yield
            dbw = _dot(tinv_t, dw)
            t1 = _dot(dbu, u, NT)
            yield
            da = jnp.where(c.strict, -(t1 + _dot(dbw, w, NT)), 0.0)
            dn_ = dp * c.decay
            dq0 = _dot(dn_, c.k)
            dk0 = _dot(dn_, c.q, TN)
            yield
            dm = da * c.decay
            e = da * c.a_low + dp * c.p
            dkb = _dot(dm, c.k) + dbw * c.egc
            dact_ref[:, GDN_QK + hh * 128:GDN_QK + (hh + 1) * 128] = (
                _dot(dm, c.kb, TN) + dk0 + dkb * c.beta + dkd * c.elast)
            dact_ref[:, cols] = dq0 + dqd * c.egc
            dact_ref[:, 2 * GDN_QK + hh * 128:2 * GDN_QK + (hh + 1) * 128] = dbu * c.beta
            dbeta = jnp.sum(dbu * c.v, axis=1, keepdims=True) + jnp.sum(dkb * c.k, axis=1, keepdims=True)
            t_kd = jnp.sum(dkd * c.kd, axis=1, keepdims=True)
            dgc = (jnp.sum(e, axis=1, keepdims=True) - jnp.sum(e.T, axis=1, keepdims=True)
                   + jnp.sum(dbw * c.bw, axis=1, keepdims=True) + jnp.sum(dqd * c.qd, axis=1, keepdims=True) - t_kd)
            dgc = dgc + jnp.where(last, jnp.sum(t_kd, axis=0, keepdims=True) + dgl * c.gl, 0.0)
            yield
            dg = _running_sum(jnp.broadcast_to(dgc, (C, SM_W)), reverse=True)
            dgb_ref[hh] = jnp.where(lane == h, dg, jnp.where(lane == h + GDN_HEADS, dbeta, 0.0))

        _round_robin(head(hh) for hh in range(GDN_HB))

    rev = lambda n: N - 1 - n
    q_spec, k_spec, v_spec, gb_spec, o_spec, s_spec, t_spec = _gdn_specs(rev)
    dgb_spec = pl.BlockSpec((GDN_HB, C, SM_W), lambda g, n: (g, rev(n), 0))
    return pl.pallas_call(
        body, name=name, grid=(GDN_HG, N),
        in_specs=[q_spec, k_spec, v_spec, gb_spec, o_spec, s_spec, t_spec],
        out_specs=[pl.BlockSpec((C, 2 * GDN_QK + GDN_V), lambda g, n: (rev(n), 0)), dgb_spec],
        out_shape=[jax.ShapeDtypeStruct((M, 2 * GDN_QK + GDN_V), F32),
                   jax.ShapeDtypeStruct((GDN_HEADS, M, SM_W), F32)],
        scratch_shapes=[pltpu.VMEM((GDN_HB, GDN_DK, GDN_DV), F32)],
        compiler_params=_params("parallel", "arbitrary"),
    )(act, act, act, gb, do, s_all, t_all)


GLA_STEP_ROWS = 64
GLA_SUB = GLA_STEP_ROWS // GLA_CHUNK


def _gla_cumsum(la):
    return _running_sum(la)


GLA_HALF = GLA_CHUNK // 2


def _gla_cross_factors(b):
    top = lax.broadcasted_iota(jnp.int32, b.shape, 0) < GLA_HALF
    bm = b[GLA_HALF - 1:GLA_HALF, :]
    late = jnp.where(top, 0.0, jnp.exp(jnp.minimum(b - bm, 0.0)))
    early = jnp.where(top, jnp.exp(jnp.minimum(bm - b, 0.0)), 0.0)
    return late, early


def _gla_half_decay(bh, ii):
    rj = lax.broadcasted_iota(jnp.int32, bh.shape, 0)
    return jnp.where(rj <= ii, jnp.exp(jnp.minimum(bh[ii:ii + 1, :] - bh, 0.0)), 0.0)


def _gla_scores_t(q, k, b):
    C, H = GLA_CHUNK, GLA_HALF
    lane = lax.broadcasted_iota(jnp.int32, (H, C), 1)
    halves = []
    for h0 in (0, H):
        qh, kh, bh = q[h0:h0 + H], k[h0:h0 + H], b[h0:h0 + H]
        sth = jnp.zeros((H, C), F32)
        for ii in range(H):
            si = jnp.sum(qh[ii:ii + 1, :] * kh * _gla_half_decay(bh, ii), axis=1, keepdims=True)
            sth = jnp.where(lane == h0 + ii, si, sth)
            if ii % 4 == 3:
                yield
        halves.append(sth)
    late, early = _gla_cross_factors(b)
    between = _dot(k * early, q * late, NT)
    yield
    return jnp.concatenate(halves, axis=0) + between


def _gla_specs(n_of):
    R = GLA_STEP_ROWS
    q_spec = pl.BlockSpec((R, GLA_QK), lambda n: (n_of(n), C_GQ // GLA_QK))
    k_spec = pl.BlockSpec((R, GLA_QK), lambda n: (n_of(n), C_GK // GLA_QK))
    v_spec = pl.BlockSpec((R, GLA_V), lambda n: (n_of(n), C_GV // GLA_V))
    la_spec = pl.BlockSpec((R, GLA_QK), lambda n: (n_of(n), 0))
    o_spec = pl.BlockSpec((R, GLA_V), lambda n: (n_of(n), 0))
    s_spec = pl.BlockSpec((GLA_HEADS, None, GLA_SUB, GLA_DV, GLA_DK), lambda n: (0, n_of(n), 0, 0, 0))
    return q_spec, k_spec, v_spec, la_spec, o_spec, s_spec


def _gla_fwd(proj, la, *, name):
    M = proj.shape[0]
    N = M // GLA_STEP_ROWS
    C = GLA_CHUNK

    def body(q_ref, k_ref, v_ref, la_ref, o_ref, s_ref, state):
        n = pl.program_id(0)

        @pl.when(n == 0)
        def _():
            state[...] = jnp.zeros_like(state)

        local = {}

        def within(hh, c):
            kc = slice(hh * GLA_DK, (hh + 1) * GLA_DK)
            vc = slice(hh * GLA_DV, (hh + 1) * GLA_DV)
            rows = slice(c * C, (c + 1) * C)
            q = q_ref[rows, kc].astype(F32) * (GLA_DK ** -0.5)
            k = k_ref[rows, kc].astype(F32)
            v = v_ref[rows, vc].astype(F32)
            b = _gla_cumsum(la_ref[rows, kc])
            yield
            blast = b[C - 1:C, :]
            sc_t = yield from _gla_scores_t(q, k, b)
            kv = _dot(v, k * jnp.exp(blast - b), TN)
            o2 = _dot(sc_t, v, TN)
            yield
            local[hh, c] = (q * jnp.exp(b), jnp.exp(blast), kv, o2)

        def across(hh):
            vc = slice(hh * GLA_DV, (hh + 1) * GLA_DV)
            st = state[hh]
            for c in range(GLA_SUB):
                qe, eblast, kv, o2 = local[hh, c]
                s_ref[hh, c] = st
                o1 = _dot(qe, st, NT)
                yield
                o_ref[c * C:(c + 1) * C, vc] = (o1 + o2).astype(o_ref.dtype)
                st = st * eblast + kv
            state[hh] = st

        _round_robin(within(hh, c) for c in range(GLA_SUB) for hh in range(GLA_HEADS))
        _round_robin(across(hh) for hh in range(GLA_HEADS))

    q_spec, k_spec, v_spec, la_spec, o_spec, s_spec = _gla_specs(lambda n: n)
    return pl.pallas_call(
        body, name=name, grid=(N,),
        in_specs=[q_spec, k_spec, v_spec, la_spec], out_specs=[o_spec, s_spec],
        out_shape=[jax.ShapeDtypeStruct((M, GLA_V), BF16),
                   jax.ShapeDtypeStruct((GLA_HEADS, N, GLA_SUB, GLA_DV, GLA_DK), F32)],
        scratch_shapes=[pltpu.VMEM((GLA_HEADS, GLA_DV, GLA_DK), F32)],
        compiler_params=_params("arbitrary"),
    )(proj, proj, proj, la)


def _gla_bwd(proj, la, do, s_all, d_proj, *, name):
    M = proj.shape[0]
    N = M // GLA_STEP_ROWS
    C = GLA_CHUNK
    qkv_w = 2 * GLA_QK + GLA_V
    assert C_GK == C_GQ + GLA_QK and C_GV == C_GK + GLA_QK and C_GQ % qkv_w == 0

    def body(q_ref, k_ref, v_ref, la_ref, do_ref, s_ref, _, dp_ref, dla_ref, dstate):
        n = pl.program_id(0)

        @pl.when(n == 0)
        def _():
            dstate[...] = jnp.zeros_like(dstate)

        H = GLA_HALF
        lane = lax.broadcasted_iota(jnp.int32, (C, C), 1)
        row = lax.broadcasted_iota(jnp.int32, (C, C), 0)
        ri = lax.broadcasted_iota(jnp.int32, (C, GLA_DK), 0)
        lane_h = lax.broadcasted_iota(jnp.int32, (H, C), 1)
        ri_h = lax.broadcasted_iota(jnp.int32, (H, GLA_DK), 0)
        cross = (row < H) & (lane >= H)
        def head(hh):
            kc = slice(hh * GLA_DK, (hh + 1) * GLA_DK)
            vc = slice(hh * GLA_DV, (hh + 1) * GLA_DV)
            ds1 = dstate[hh]
            for c in reversed(range(GLA_SUB)):
                rows = slice(c * C, (c + 1) * C)
                q = q_ref[rows, kc].astype(F32) * (GLA_DK ** -0.5)
                k = k_ref[rows, kc].astype(F32)
                v = v_ref[rows, vc].astype(F32)
                b = _gla_cumsum(la_ref[rows, kc])
                do_ = do_ref[rows, vc]
                st = s_ref[hh, c]
                dsc_t = _dot(v, do_, NT)
                dqe = _dot(do_, st)
                dke = _dot(v, ds1)
                yield
                blast = b[C - 1:C, :]
                eb = jnp.exp(b)
                elast = jnp.exp(blast - b)
                eblast = jnp.exp(blast)
                qe = q * eb
                ke = k * elast
                dv2 = _dot(ke, ds1, NT)
                ds_new = _dot(do_, qe, TN)
                deblast = jnp.sum(st * ds1, axis=0, keepdims=True)
                sc_halves, dq_halves, dk_halves = [], [], []
                for h0 in (0, H):
                    qh, kh, bh, dsch = q[h0:h0 + H], k[h0:h0 + H], b[h0:h0 + H], dsc_t[h0:h0 + H]
                    sch = jnp.zeros((H, C), F32)
                    dqh = jnp.zeros((H, GLA_DK), F32)
                    dkh = jnp.zeros((H, GLA_DK), F32)
                    for ii in range(H):
                        f = _gla_half_decay(bh, ii)
                        kf = kh * f
                        si = jnp.sum(qh[ii:ii + 1, :] * kf, axis=1, keepdims=True)
                        sch = jnp.where(lane_h == h0 + ii, si, sch)
                        dsi = jnp.sum(jnp.where(lane_h == h0 + ii, dsch, 0.0), axis=1, keepdims=True)
                        dqh = jnp.where(ri_h == ii, jnp.sum(dsi * kf, axis=0, keepdims=True), dqh)
                        dkh = dkh + (dsi * f) * qh[ii:ii + 1, :]
                        if ii % 4 == 3:
                            yield
                    sc_halves.append(sch)
                    dq_halves.append(dqh)
                    dk_halves.append(dkh)
                late, early = _gla_cross_factors(b)
                q_late, k_early = q * late, k * early
                dsc_x = jnp.where(cross, dsc_t, 0.0)
                sc_t = jnp.concatenate(sc_halves, axis=0) + _dot(k_early, q_late, NT)
                dq_sc = jnp.concatenate(dq_halves, axis=0) + _dot(dsc_x, k_early, TN) * late
                dk_sc = jnp.concatenate(dk_halves, axis=0) + _dot(dsc_x, q_late) * early
                yield
                dv1 = _dot(sc_t, do_)
                dp_ref[rows, kc] = ((dq_sc + dqe * eb) * (GLA_DK ** -0.5)).astype(dp_ref.dtype)
                dp_ref[rows, GLA_QK + hh * GLA_DK:GLA_QK + (hh + 1) * GLA_DK] = (dk_sc + dke * elast).astype(dp_ref.dtype)
                t_ke = dke * ke
                db = q * dq_sc - k * dk_sc + dqe * qe - t_ke
                db = db + jnp.where(ri == C - 1, jnp.sum(t_ke, axis=0, keepdims=True) + deblast * eblast, 0.0)
                dla = _running_sum(db, reverse=True)
                yield
                dp_ref[rows, 2 * GLA_QK + hh * GLA_DV:2 * GLA_QK + (hh + 1) * GLA_DV] = (dv1 + dv2).astype(dp_ref.dtype)
                dla_ref[rows, kc] = dla
                ds1 = ds1 * eblast + ds_new
            dstate[hh] = ds1

        _round_robin(head(hh) for hh in range(GLA_HEADS))

    rev = lambda n: N - 1 - n
    q_spec, k_spec, v_spec, la_spec, o_spec, s_spec = _gla_specs(rev)
    return pl.pallas_call(
        body, name=name, grid=(N,),
        in_specs=[q_spec, k_spec, v_spec, la_spec, o_spec, s_spec, _ANY],
        out_specs=[pl.BlockSpec((GLA_STEP_ROWS, qkv_w), lambda n: (rev(n), C_GQ // qkv_w)), la_spec],
        out_shape=[jax.ShapeDtypeStruct(d_proj.shape, d_proj.dtype), jax.ShapeDtypeStruct((M, GLA_QK), F32)],
        input_output_aliases={6: 0},
        scratch_shapes=[pltpu.VMEM((GLA_HEADS, GLA_DV, GLA_DK), F32)],
        compiler_params=_params("arbitrary"),
    )(proj, proj, proj, la, do, s_all, d_proj)


def _head_norm(o, wn):
    r = lax.rsqrt(jnp.mean(o * o, axis=-1, keepdims=True) + NORM_EPS)
    return o * r, r


def _mix_heads():
    heads = [(0, GDN_DV, hh * GDN_DV, hh * GDN_DV) for hh in range(GDN_HEADS)]
    heads += [(1, GLA_DV, GDN_V + hh * GLA_DV, hh * GLA_DV) for hh in range(GLA_HEADS)]
    return heads


def _mix_fwd(o_gdn, o_gla, proj, wn_gdn, wn_gla, *, name):
    M = proj.shape[0]
    tm = _tile(M, 344, 16)

    def body(og_ref, ol_ref, z_ref, r_ref, wg_ref, wl_ref, m_ref):
        srcs = ((og_ref, z_ref, wg_ref), (ol_ref, r_ref, wl_ref))
        for grp, width, mcol, col in _mix_heads():
            o_ref, gate_ref, w_ref = srcs[grp]
            xhat, _ = _head_norm(o_ref[:, col:col + width].astype(F32), None)
            gate, _ = _silu_and_grad(gate_ref[:, col:col + width].astype(F32))
            m_ref[:, mcol:mcol + width] = (xhat * w_ref[...] * gate).astype(m_ref.dtype)

    full = lambda s: pl.BlockSpec(s, lambda i: (0, 0))
    return pl.pallas_call(
        body, name=name, grid=(M // tm,),
        in_specs=[pl.BlockSpec((tm, GDN_V), lambda i: (i, 0)), pl.BlockSpec((tm, GLA_V), lambda i: (i, 0)),
                  pl.BlockSpec((tm, GDN_V), lambda i: (i, C_Z // GDN_V)),
                  pl.BlockSpec((tm, GLA_V), lambda i: (i, C_GR // GLA_V)),
                  full((1, GDN_DV)), full((1, GLA_DV))],
        out_specs=pl.BlockSpec((tm, D_MODEL), lambda i: (i, 0)),
        out_shape=jax.ShapeDtypeStruct((M, D_MODEL), BF16),
        compiler_params=_params("parallel"),
    )(o_gdn, o_gla, proj, proj, wn_gdn, wn_gla)


def _mix_bwd(o_gdn, o_gla, proj, wn_gdn, wn_gla, dmixed, *, name):
    M = proj.shape[0]
    tm = _tile(M, 344, 16)
    g_ = M // tm
    assert C_Z == 0 and C_GR == GDN_V

    def body(og_ref, ol_ref, z_ref, r_ref, wg_ref, wl_ref, dm_ref,
             dog_ref, dol_ref, dzr_ref, dwg_ref, dwl_ref):
        i = pl.program_id(0)
        srcs = ((og_ref, z_ref, wg_ref, dog_ref), (ol_ref, r_ref, wl_ref, dol_ref))
        dws = [jnp.zeros((1, GDN_DV), F32), jnp.zeros((1, GLA_DV), F32)]
        for grp, width, mcol, col in _mix_heads():
            o_ref, gate_ref, w_ref, do_ref = srcs[grp]
            cols = slice(col, col + width)
            xhat, r = _head_norm(o_ref[:, cols].astype(F32), None)
            gate, dgate_dc = _silu_and_grad(gate_ref[:, cols].astype(F32))
            dm = dm_ref[:, mcol:mcol + width]
            dzr_ref[:, mcol:mcol + width] = (dm * xhat * w_ref[...] * dgate_dc).astype(dzr_ref.dtype)
            dnorm = dm * gate
            dws[grp] = dws[grp] + jnp.sum(dnorm * xhat, axis=0, keepdims=True)
            dxhat = dnorm * w_ref[...]
            do_ref[:, cols] = r * (dxhat - xhat * jnp.mean(dxhat * xhat, axis=-1, keepdims=True))

        @pl.when(i == 0)
        def _():
            dwg_ref[...] = dws[0]
            dwl_ref[...] = dws[1]

        @pl.when(i > 0)
        def _():
            dwg_ref[...] += dws[0]
            dwl_ref[...] += dws[1]

    full = lambda s: pl.BlockSpec(s, lambda i: (0, 0))
    half = pl.BlockSpec((tm, GDN_V), lambda i: (i, 0))
    return pl.pallas_call(
        body, name=name, grid=(g_,),
        in_specs=[half, half, pl.BlockSpec((tm, GDN_V), lambda i: (i, C_Z // GDN_V)),
                  pl.BlockSpec((tm, GLA_V), lambda i: (i, C_GR // GLA_V)),
                  full((1, GDN_DV)), full((1, GLA_DV)), pl.BlockSpec((tm, D_MODEL), lambda i: (i, 0))],
        out_specs=[half, half, pl.BlockSpec((tm, GDN_V + GLA_V), lambda i: (i, 0)),
                   full((1, GDN_DV)), full((1, GLA_DV))],
        out_shape=[jax.ShapeDtypeStruct((M, GDN_V), F32), jax.ShapeDtypeStruct((M, GLA_V), F32),
                   jax.ShapeDtypeStruct((M, D_PROJ), BF16),
                   jax.ShapeDtypeStruct((1, GDN_DV), F32), jax.ShapeDtypeStruct((1, GLA_DV), F32)],
        compiler_params=_params("arbitrary"),
    )(o_gdn, o_gla, proj, proj, wn_gdn, wn_gla, dmixed)


def _row_chunks(tm, parts=2):
    if tm % (16 * parts):
        return [slice(0, tm)]
    return [slice(p * (tm // parts), (p + 1) * (tm // parts)) for p in range(parts)]


def _swiglu_fwd(n, w_gate_t, w_up_t, *, name, tm=1376, tn=512):
    M, D = n.shape
    F = w_gate_t.shape[0]
    tm, tn = _tile(M, tm, 16), _tile(F, tn, 128)

    def body(n_ref, wg_ref, wu_ref, g_ref, u_ref, a_ref):
        wg, wu = wg_ref[...], wu_ref[...]
        for rows in _row_chunks(tm):
            x = n_ref[rows, :]
            g = _dot(x, wg, NT)
            u = _dot(x, wu, NT)
            s, _ = _silu_and_grad(g)
            g_ref[rows, :] = g.astype(g_ref.dtype)
            u_ref[rows, :] = u.astype(u_ref.dtype)
            a_ref[rows, :] = (s * u).astype(a_ref.dtype)

    w_spec = pl.BlockSpec((tn, D), lambda i, j: (j, 0))
    o_spec = pl.BlockSpec((tm, tn), lambda i, j: (i, j))
    return pl.pallas_call(
        body, name=name, grid=(M // tm, F // tn),
        in_specs=[pl.BlockSpec((tm, D), lambda i, j: (i, 0)), w_spec, w_spec], out_specs=[o_spec] * 3,
        out_shape=[jax.ShapeDtypeStruct((M, F), BF16)] * 3, compiler_params=_params("parallel", "parallel"),
    )(n, w_gate_t, w_up_t)


def _swiglu_bwd(dh, w_down, gate, up, *, name, after=None, tm=1376, tn=512):
    M, D = dh.shape
    F = w_down.shape[0]
    tm, tn = _tile(M, tm, 16), _tile(F, tn, 128)
    n_after = 0 if after is None else 1

    def body(*refs):
        dh_ref, w_ref, g_ref, u_ref, dg_ref, du_ref = refs[n_after:]
        w = w_ref[...]
        for rows in _row_chunks(tm):
            da = _dot(dh_ref[rows, :], w, NT)
            s, ds = _silu_and_grad(g_ref[rows, :].astype(F32))
            dg_ref[rows, :] = (da * u_ref[rows, :].astype(F32) * ds).astype(dg_ref.dtype)
            du_ref[rows, :] = (da * s).astype(du_ref.dtype)

    o_spec = pl.BlockSpec((tm, tn), lambda i, j: (i, j))
    return pl.pallas_call(
        body, name=name, grid=(M // tm, F // tn),
        in_specs=[_ANY] * n_after + [pl.BlockSpec((tm, D), lambda i, j: (i, 0)),
                                     pl.BlockSpec((tn, D), lambda i, j: (j, 0)), o_spec, o_spec],
        out_specs=[o_spec, o_spec], out_shape=[jax.ShapeDtypeStruct((M, F), BF16)] * 2,
        compiler_params=_params("parallel", "parallel"),
    )(*((after,) if n_after else ()), dh, w_down, gate, up)


def _adamw_update(w, g, m, v):
    nm = ADAM_B1 * m + (1.0 - ADAM_B1) * g
    nv = ADAM_B2 * v + (1.0 - ADAM_B2) * (g * g)
    m_hat = nm / (1.0 - ADAM_B1 ** ADAM_STEP)
    v_hat = nv / (1.0 - ADAM_B2 ** ADAM_STEP)
    return -ADAM_LR * (m_hat / (jnp.sqrt(v_hat) + ADAM_EPS) + ADAM_WD * w), nm, nv


def _adamw(w, g, m, v, *, name):
    shape = w.shape
    cols = shape[-1]
    rows = w.size // cols
    w2, g2, m2, v2 = (t.reshape(rows, cols) for t in (w, g, m, v))
    if rows % 8 == 0 or cols % 128 != 0:
        tr, tc = (_tile(rows, 256, 8) if rows % 8 == 0 else rows), cols
    else:
        tr, tc = rows, _tile(cols, 256, 128)

    def body(w_ref, g_ref, m_ref, v_ref, d_ref, nm_ref, nv_ref):
        d_ref[...], nm_ref[...], nv_ref[...] = _adamw_update(w_ref[...], g_ref[...], m_ref[...], v_ref[...])

    blk = pl.BlockSpec((tr, tc), lambda i, j: (i, j))
    outs = pl.pallas_call(
        body, name=name, grid=(rows // tr, cols // tc), in_specs=[blk] * 4, out_specs=[blk] * 3,
        out_shape=[jax.ShapeDtypeStruct((rows, cols), F32)] * 3, compiler_params=_params("parallel", "parallel"),
    )(w2, g2, m2, v2)
    return tuple(t.reshape(shape) for t in outs)


def _sum_slabs(x, *, name):
    _, R, C = x.shape
    sub = 16 if x.dtype == BF16 else 8
    if R % sub == 0:
        tr, tc = _tile(R, 128, sub), C
    else:
        tr, tc = R, _tile(C, 256, 128)

    def body(x_ref, o_ref):
        acc = x_ref[0].astype(F32)
        for s in range(1, N_DEV):
            acc = acc + x_ref[s].astype(F32)
        o_ref[...] = acc

    return pl.pallas_call(
        body, name=name, grid=(R // tr, C // tc),
        in_specs=[pl.BlockSpec((N_DEV, tr, tc), lambda i, j: (0, i, j))],
        out_specs=pl.BlockSpec((tr, tc), lambda i, j: (i, j)),
        out_shape=jax.ShapeDtypeStruct((R, C), F32), compiler_params=_params("parallel", "parallel"),
    )(x)


def _sum_adamw(x, w, m, v, *, name):
    _, R, C = x.shape
    if R % 16 == 0:
        tr, tc = _tile(R, 128, 16), C
    else:
        tr, tc = R, _tile(C, 256, 128)

    def body(x_ref, w_ref, m_ref, v_ref, g_ref, d_ref, nm_ref, nv_ref):
        g = x_ref[0].astype(F32)
        for s in range(1, N_DEV):
            g = g + x_ref[s].astype(F32)
        g_ref[...] = g
        d_ref[...], nm_ref[...], nv_ref[...] = _adamw_update(w_ref[...], g, m_ref[...], v_ref[...])

    blk = pl.BlockSpec((tr, tc), lambda i, j: (i, j))
    return pl.pallas_call(
        body, name=name, grid=(R // tr, C // tc),
        in_specs=[pl.BlockSpec((N_DEV, tr, tc), lambda i, j: (0, i, j)), blk, blk, blk], out_specs=[blk] * 4,
        out_shape=[jax.ShapeDtypeStruct((R, C), F32)] * 4, compiler_params=_params("parallel", "parallel"),
    )(x, w, m, v)


def _peers():
    x, y, c = lax.axis_index("x"), lax.axis_index("y"), lax.axis_index("c")
    me = 4 * x + 2 * y + c
    peers = []
    for k in range(1, N_DEV):
        px = 1 - x if k & 4 else x
        py = 1 - y if k & 2 else y
        pc = 1 - c if k & 1 else c
        peers.append(((px, py, pc), 4 * px + 2 * py + pc))
    return me, peers


_HBM = pl.BlockSpec(memory_space=pltpu.HBM)
_SEM = pl.BlockSpec(memory_space=pltpu.SEMAPHORE)
_EFFECT = pltpu.SideEffectType.DATAFLOW_SIDE_EFFECTING


PLAN_GATHER = tuple((k, "x", 0) for k in range(0, N_DEV))
PLAN_SCATTER = tuple((k, "xk", 0) for k in range(1, N_DEV))
PLAN_GATHER_CHIPS = tuple((k, "x", 0) for k in (1, 2, 4, 6))
PLAN_GATHER_PASS_ON = tuple((1, ("land", q), q) for q in (2, 4, 6))


def _plan_refs(plan, j, x_ref, land_ref, me, peers, receiving):
    k, source, r = plan[j]
    index_of = lambda q: me if q == 0 else peers[q - 1][1]
    pos, target = ((lax.axis_index("x"), lax.axis_index("y"), lax.axis_index("c")), me) if k == 0 else peers[k - 1]
    if source == "x":
        src = x_ref
    elif source == "xk":
        src = x_ref.at[target]
    else:
        src = land_ref.at[index_of(source[1])]
    return pos, src, land_ref.at[index_of(k ^ r) if receiving else index_of(r)]


def _exchange_start(x, *, plan, name, after=None, land=None, slab=None):
    n_after = 0 if after is None else 1
    n = len(plan)

    def body(*refs):
        x_ref, land_ref, send_sems, recv_sems, _, _, token = refs[n_after:]
        me, peers = _peers()
        for j in range(n):
            pos, src, dst = _plan_refs(plan, j, x_ref, land_ref, me, peers, receiving=False)
            pltpu.make_async_remote_copy(src_ref=src, dst_ref=dst, send_sem=send_sems.at[j], recv_sem=recv_sems.at[j],
                                         device_id=pos, device_id_type=pl.DeviceIdType.MESH).start()
        token[...] = jnp.zeros_like(token)

    if land is None:
        land = lax.empty((N_DEV,) + tuple(slab), x.dtype)
    return pl.pallas_call(
        body, name=name,
        out_shape=(pltpu.SemaphoreType.DMA((n,)), pltpu.SemaphoreType.DMA((n,)),
                   pltpu.HBM(x.shape, x.dtype), pltpu.HBM(land.shape, land.dtype), jax.ShapeDtypeStruct((8, 128), F32)),
        in_specs=[_ANY] * n_after + [_HBM, _HBM],
        out_specs=(_SEM, _SEM, _HBM, _HBM, pl.BlockSpec(memory_space=pltpu.VMEM)),
        input_output_aliases={n_after: 2, n_after + 1: 3},
        compiler_params=pltpu.CompilerParams(has_side_effects=_EFFECT),
    )(*((after,) if n_after else ()), pltpu.with_memory_space_constraint(x, pltpu.HBM),
      pltpu.with_memory_space_constraint(land, pltpu.HBM))


def _exchange_start_many(items, *, name):
    n_items = len(items)
    lands = [lax.empty((N_DEV,) + tuple(slab), x.dtype) for x, _, slab in items]

    def body(*refs):
        ins, outs = refs[:2 * n_items], refs[2 * n_items:]
        me, peers = _peers()
        for i, (_, plan, _) in enumerate(items):
            x_ref, land_ref = ins[2 * i], ins[2 * i + 1]
            send_sems, recv_sems = outs[4 * i], outs[4 * i + 1]
            for j in range(len(plan)):
                pos, src, dst = _plan_refs(plan, j, x_ref, land_ref, me, peers, receiving=False)
                pltpu.make_async_remote_copy(src_ref=src, dst_ref=dst, send_sem=send_sems.at[j], recv_sem=recv_sems.at[j],
                                             device_id=pos, device_id_type=pl.DeviceIdType.MESH).start()
        outs[-1][...] = jnp.zeros_like(outs[-1])

    out_shape, out_specs, operands, aliases = [], [], [], {}
    for i, ((x, plan, _), land) in enumerate(zip(items, lands)):
        out_shape += [pltpu.SemaphoreType.DMA((len(plan),)), pltpu.SemaphoreType.DMA((len(plan),)),
                      pltpu.HBM(x.shape, x.dtype), pltpu.HBM(land.shape, land.dtype)]
        out_specs += [_SEM, _SEM, _HBM, _HBM]
        operands += [pltpu.with_memory_space_constraint(x, pltpu.HBM), pltpu.with_memory_space_constraint(land, pltpu.HBM)]
        aliases[2 * i], aliases[2 * i + 1] = 4 * i + 2, 4 * i + 3
    res = pl.pallas_call(
        body, name=name,
        out_shape=tuple(out_shape) + (jax.ShapeDtypeStruct((8, 128), F32),),
        in_specs=[_HBM] * (2 * n_items), out_specs=tuple(out_specs) + (pl.BlockSpec(memory_space=pltpu.VMEM),),
        input_output_aliases=aliases, compiler_params=pltpu.CompilerParams(has_side_effects=_EFFECT),
    )(*operands)
    return [tuple(res[4 * i:4 * i + 4]) + (res[-1],) for i in range(n_items)]


def _exchange_wait(handle, after, *, plan, name):
    send_sems, recv_sems, x_thru, land_thru, _ = handle
    afters = list(after) if isinstance(after, (list, tuple)) else [after]

    def body(x_ref, land_ref, send_sems, recv_sems, *rest):
        me, peers = _peers()
        for j in range(len(plan)):
            pos, src, dst = _plan_refs(plan, j, x_ref, land_ref, me, peers, receiving=True)
            cp = pltpu.make_async_remote_copy(src_ref=src, dst_ref=dst, send_sem=send_sems.at[j], recv_sem=recv_sems.at[j],
                                              device_id=pos, device_id_type=pl.DeviceIdType.MESH)
            cp.wait_send()
            cp.wait_recv()

    return pl.pallas_call(
        body, name=name,
        out_shape=(pltpu.HBM(x_thru.shape, x_thru.dtype), pltpu.HBM(land_thru.shape, land_thru.dtype)),
        in_specs=[_HBM, _HBM, _SEM, _SEM] + [_ANY] * len(afters), out_specs=(_HBM, _HBM),
        input_output_aliases={0: 0, 1: 1}, compiler_params=pltpu.CompilerParams(has_side_effects=_EFFECT),
    )(x_thru, land_thru, send_sems, recv_sems, *afters)


def _wait_and_pass_on(handle, after, *, name):
    send_sems, recv_sems, x_thru, land_thru, _ = handle
    afters = list(after) if isinstance(after, (list, tuple)) else [after]
    first, second = PLAN_GATHER_CHIPS, PLAN_GATHER_PASS_ON

    def body(x_ref, land_ref, send_sems, recv_sems, *rest):
        send_next, recv_next, _, _, token = rest[len(afters):]
        me, peers = _peers()
        arrived = []
        for j in range(len(first)):
            pos, src, dst = _plan_refs(first, j, x_ref, land_ref, me, peers, receiving=True)
            cp = pltpu.make_async_remote_copy(src_ref=src, dst_ref=dst, send_sem=send_sems.at[j], recv_sem=recv_sems.at[j],
                                              device_id=pos, device_id_type=pl.DeviceIdType.MESH)
            cp.wait_recv()
            arrived.append(cp)
            for i in range(len(second)):
                if second[i][2] == first[j][0]:
                    pos, src, dst = _plan_refs(second, i, x_ref, land_ref, me, peers, receiving=False)
                    pltpu.make_async_remote_copy(src_ref=src, dst_ref=dst, send_sem=send_next.at[i], recv_sem=recv_next.at[i],
                                                 device_id=pos, device_id_type=pl.DeviceIdType.MESH).start()
        for cp in arrived:
            cp.wait_send()
        token[...] = jnp.zeros_like(token)

    n = len(second)
    return pl.pallas_call(
        body, name=name,
        out_shape=(pltpu.SemaphoreType.DMA((n,)), pltpu.SemaphoreType.DMA((n,)),
                   pltpu.HBM(x_thru.shape, x_thru.dtype), pltpu.HBM(land_thru.shape, land_thru.dtype),
                   jax.ShapeDtypeStruct((8, 128), F32)),
        in_specs=[_HBM, _HBM, _SEM, _SEM] + [_ANY] * len(afters),
        out_specs=(_SEM, _SEM, _HBM, _HBM, pl.BlockSpec(memory_space=pltpu.VMEM)),
        input_output_aliases={0: 2, 1: 3}, compiler_params=pltpu.CompilerParams(has_side_effects=_EFFECT),
    )(x_thru, land_thru, send_sems, recv_sems, *afters)


def _to_proj_rows(t):
    z = jnp.zeros((D_PROJ - C_SM - 2 * GDN_HEADS - GLA_RANK,) + t.shape[1:], t.dtype)
    return jnp.concatenate([t[R_Z:R_A], t[R_GR:R_LR], t[R_GQ:R_GR], t[:R_Z], t[R_A:R_GQ], t[R_LR:], z], axis=0)


def _from_proj_rows(t):
    ab = C_SM + 2 * GDN_HEADS
    return jnp.concatenate([t[C_QKV:C_SM], t[C_Z:C_GR], t[C_SM:ab], t[C_GQ:C_QKV], t[C_GR:C_GQ],
                            t[ab:ab + GLA_RANK]], axis=0)


def _local_step(x, target, meta, attn_nw, conv_w, a_log, dt_bias, gdn_nw, w2, b2, gla_nw, ffn_nw, final_nw,
                fetch, emit, start=None):
    head = jnp.concatenate([jnp.zeros((ROW_PAD, D_MODEL), F32), meta], axis=0)
    conv_w8 = jnp.concatenate([conv_w, jnp.zeros((8 - CONV_K, conv_w.shape[1]), F32)], axis=0)
    w2p = jnp.zeros((SM_W, GLA_QK), F32).at[2 * GDN_HEADS:2 * GDN_HEADS + GLA_RANK].set(w2)
    alog_p = jnp.zeros((1, SM_W), F32).at[:, :GDN_HEADS].set(a_log)
    dt_p = jnp.zeros((1, SM_W), F32).at[:, :GDN_HEADS].set(dt_bias)

    h0, n1 = _embed_norm(head, x, attn_nw, name="attn_norm", after=start)
    w_in_t = fetch("w_in_t", (n1, conv_w8, w2p, alog_p, dt_p))
    proj, sm = _in_proj(n1, w_in_t, name="in_proj")
    gb, la = _gates_fwd(sm, w2p, b2, alog_p, dt_p, name="gates")
    act = _prep_fwd(proj, conv_w8, name="gdn_prep")
    o_gdn, s_gdn, t_gdn = _gdn_fwd(act, gb, name="gdn_fwd")
    o_gla, s_gla = _gla_fwd(proj, la, name="gla_fwd")
    mixed = _mix_fwd(o_gdn, o_gla, proj, gdn_nw, gla_nw, name="mix")
    w_out = fetch("w_out", mixed)
    h1 = _matmul(mixed, w_out, mode="nn", add=h0, name="out_proj")
    n2 = _rmsnorm_fwd(h1, ffn_nw, name="ffn_norm")
    w_gate_t, w_up_t = fetch("w_gate_t", n2), fetch("w_up_t", n2)
    gate, up, hid = _swiglu_fwd(n2, w_gate_t, w_up_t, name="swiglu")
    w_down = fetch("w_down", hid)
    h2 = _matmul(hid, w_down, mode="nn", add=h1, name="ffn_down", tm=1376, tn=256)
    dh2, dh2_b, d_final_nw, loss = _loss_head(h2, final_nw, target, name="loss_head")

    wg = dict(mode="tn", out_dtype=BF16, tn=512)
    tok = emit("w_down", _matmul(hid, dh2_b, name="d_w_down", tm=704, **wg))
    d_gate, d_up = _swiglu_bwd(dh2_b, w_down, gate, up, name="d_swiglu", after=tok)
    tok = emit("w_gate_t", _matmul(d_gate, n2, name="d_w_gate", tm=704, **wg))
    tok = emit("w_up_t", _matmul(d_up, n2, name="d_w_up", tm=704, after=tok, **wg))
    d_n2 = _matmul_pair(d_gate, w_gate_t, d_up, w_up_t, name="d_n2", after=tok)
    dh1, dh1_b, d_ffn_nw = _rmsnorm_bwd(h1, ffn_nw, d_n2, dh2, name="d_ffn_norm")

    tok = emit("w_out", _matmul(mixed, dh1_b, name="d_w_out", tm=512, **wg))
    d_mixed = _matmul(dh1_b, w_out, mode="nt", name="d_mixed", after=tok)
    do_gdn, do_gla, d_proj, d_gdn_nw, d_gla_nw = _mix_bwd(o_gdn, o_gla, proj, gdn_nw, gla_nw, d_mixed, name="d_mix")
    d_proj, d_la = _gla_bwd(proj, la, do_gla, s_gla, d_proj, name="gla_bwd")
    dact, dgb_heads = _gdn_bwd(act, gb, do_gdn, s_gdn, t_gdn, name="gdn_bwd")
    d_proj, d_w2p, d_b2, d_alog, d_dt = _gates_bwd(sm, w2p, b2, alog_p, dt_p, dgb_heads, d_la, d_proj, name="d_gates")
    d_proj, d_conv_w8 = _prep_bwd(proj, conv_w8, dact, d_proj, name="d_gdn_prep")
    tok = emit("w_in_t", _matmul(d_proj, n1, name="d_w_in", tm=768, **wg))
    d_n1 = _matmul(d_proj, w_in_t, mode="nn", name="d_n1", tm=688, after=tok)
    grad_x, d_head, d_attn_nw = _embed_norm_bwd(h0, attn_nw, d_n1, dh1, name="d_attn_norm")

    return dict(
        loss=loss[0, 0], grad_x=grad_x, meta=d_head[ROW_PAD:HEAD_ROWS], attn_nw=d_attn_nw,
        conv_w=d_conv_w8[:CONV_K], a_log=d_alog[:, :GDN_HEADS], dt_bias=d_dt[:, :GDN_HEADS], gdn_nw=d_gdn_nw,
        w2=d_w2p[2 * GDN_HEADS:2 * GDN_HEADS + GLA_RANK], b2=d_b2, gla_nw=d_gla_nw, ffn_nw=d_ffn_nw,
        final_nw=d_final_nw)


SMALL_ROWS = 32


def kernel(x, meta_tokens, attn_norm_w, w_in, gdn_conv_w, gdn_a_log, gdn_dt_bias, gdn_norm_w, gla_gate_w2, gla_gate_b, gla_norm_w, w_out, ffn_norm_w, w_gate, w_up, w_down, final_norm_w, loss_target, m_meta_tokens, m_attn_norm_w, m_w_in, m_gdn_conv_w, m_gdn_a_log, m_gdn_dt_bias, m_gdn_norm_w, m_gla_gate_w2, m_gla_gate_b, m_gla_norm_w, m_w_out, m_ffn_norm_w, m_w_gate, m_w_up, m_w_down, m_final_norm_w, v_meta_tokens, v_attn_norm_w, v_w_in, v_gdn_conv_w, v_gdn_a_log, v_gdn_dt_bias, v_gdn_norm_w, v_gla_gate_w2, v_gla_gate_b, v_gla_norm_w, v_w_out, v_ffn_norm_w, v_w_gate, v_w_up, v_w_down, v_final_norm_w):
    me = 4 * lax.axis_index("x") + 2 * lax.axis_index("y") + lax.axis_index("c")

    n_conv = gdn_conv_w.shape[2]
    n_w2 = gla_gate_w2.shape[2]
    n_meta = meta_tokens.shape[1]
    small = jnp.zeros((40, n_conv), F32)
    small = small.at[0:N_META, :n_meta].set(meta_tokens)
    small = small.at[N_META:N_META + CONV_K, :].set(gdn_conv_w[0])
    small = small.at[24:24 + GLA_RANK, :n_w2].set(gla_gate_w2[0])

    w_in_slab = w_in[0].T.astype(BF16)
    items = [(small, PLAN_GATHER, small.shape), (w_in_slab, PLAN_GATHER_CHIPS, w_in_slab.shape)]
    wnames = ("w_out", "w_gate_t", "w_up_t", "w_down")
    for slab in (w_out[0], w_gate[0].T, w_up[0].T, w_down[0]):
        items.append((slab.astype(BF16), PLAN_GATHER, slab.shape))
    started = _exchange_start_many(items, name="gather_weights_start")
    small_h, in_h = started[0], started[1]
    handles = dict(zip(wnames, started[2:]))
    tok = small_h[4]

    own, small_all = _exchange_wait(small_h, tok, plan=PLAN_GATHER, name="gather_small_wait")
    meta_f = small_all[:, 0:N_META, :n_meta].transpose(1, 0, 2).reshape(N_META, D_MODEL)
    conv_f = small_all[:, N_META:N_META + CONV_K, :].transpose(1, 0, 2).reshape(CONV_K, N_DEV * n_conv)
    w2_f = small_all[:, 24:24 + GLA_RANK, :n_w2].transpose(1, 0, 2).reshape(GLA_RANK, N_DEV * n_w2)

    def fetch(name, after):
        if name == "w_in_t":
            pass_h = _wait_and_pass_on(in_h, after, name="gather_w_in_wait_pass_on")
            own, got = _exchange_wait(pass_h, pass_h[4], plan=PLAN_GATHER_PASS_ON, name="pass_w_in_wait")
            got = lax.dynamic_update_index_in_dim(got, own, me, 0)
            return _to_proj_rows(got.reshape(D_IN, D_MODEL))
        own, got = _exchange_wait(handles[name], after, plan=PLAN_GATHER, name="gather_" + name + "_wait")
        return got.reshape(N_DEV * got.shape[1], D_MODEL)

    sent = {}

    def emit(name, grad):
        if name == "w_in_t":
            grad = _from_proj_rows(grad)
        parts = grad.reshape(N_DEV, grad.shape[0] // N_DEV, D_MODEL)
        sent[name] = _exchange_start(parts, plan=PLAN_SCATTER, slab=parts.shape[1:], name="scatter_" + name + "_start")
        return sent[name][4]

    g = _local_step(x[0], loss_target[0], meta_f, attn_norm_w, conv_f, gdn_a_log, gdn_dt_bias, gdn_norm_w, w2_f,
                    gla_gate_b, gla_norm_w, ffn_norm_w, final_norm_w.reshape(1, D_MODEL), fetch, emit, start=tok)

    misc = jnp.concatenate([g["a_log"], g["dt_bias"], g["gdn_nw"], g["gla_nw"], g["b2"], g["loss"].reshape(1, 1)], axis=1)
    n_misc = misc.shape[1]
    misc = jnp.pad(misc, ((0, 0), (0, D_MODEL - n_misc)))
    rows = jnp.concatenate([g["attn_nw"], g["ffn_nw"], g["final_nw"], misc, g["meta"],
                            g["conv_w"].reshape(-1, D_MODEL), g["w2"].reshape(-1, D_MODEL)], axis=0)
    rows = jnp.pad(rows, ((0, SMALL_ROWS - rows.shape[0]), (0, 0)))
    rows_h = _exchange_start(rows, plan=PLAN_GATHER, slab=rows.shape, name="gather_small_grads_start")

    big = {}
    after = rows_h[4]
    for name, w, m, v, transposed in (("w_down", w_down, m_w_down, v_w_down, False), ("w_gate_t", w_gate, m_w_gate, v_w_gate, True),
                                      ("w_up_t", w_up, m_w_up, v_w_up, True), ("w_out", w_out, m_w_out, v_w_out, False),
                                      ("w_in_t", w_in, m_w_in, v_w_in, True)):
        own, got = _exchange_wait(sent[name], after, plan=PLAN_SCATTER, name="scatter_" + name + "_wait")
        got = lax.dynamic_update_index_in_dim(got, lax.dynamic_index_in_dim(own, me, 0, keepdims=False), me, 0)
        local = [t[0].T if transposed else t[0] for t in (w, m, v)]
        res = _sum_adamw(got, *local, name="adamw_" + name)
        big[name] = [t.T[None] if transposed else t[None] for t in res]
        after = res[0]

    own, got = _exchange_wait(rows_h, after, plan=PLAN_GATHER, name="gather_small_grads_wait")
    tot = _sum_slabs(got, name="sum_small_grads")
    grad_attn_nw, grad_ffn_nw, grad_final_nw = tot[0:1], tot[1:2], tot[2]
    grad_a_log = tot[3:4, 0:8]
    grad_dt = tot[3:4, 8:16]
    grad_gdn_nw = tot[3:4, 16:16 + GDN_DV]
    grad_gla_nw = tot[3:4, 144:144 + GLA_DV]
    grad_b2 = tot[3:4, 400:400 + GLA_QK]
    loss = tot[3, n_misc - 1]
    r0 = 4 + N_META
    grad_meta = lax.dynamic_slice(tot[4:r0], (0, me * n_meta), (N_META, n_meta))
    r1 = r0 + CONV_K * N_DEV * n_conv // D_MODEL
    grad_conv = lax.dynamic_slice(tot[r0:r1].reshape(CONV_K, N_DEV * n_conv), (0, me * n_conv), (CONV_K, n_conv))[None]
    r2 = r1 + GLA_RANK * N_DEV * n_w2 // D_MODEL
    grad_w2 = lax.dynamic_slice(tot[r1:r2].reshape(GLA_RANK, N_DEV * n_w2), (0, me * n_w2), (GLA_RANK, n_w2))[None]

    weights = [meta_tokens, attn_norm_w, w_in, gdn_conv_w, gdn_a_log, gdn_dt_bias, gdn_norm_w, gla_gate_w2,
               gla_gate_b, gla_norm_w, w_out, ffn_norm_w, w_gate, w_up, w_down, final_norm_w]
    grads = [grad_meta, grad_attn_nw, "w_in_t", grad_conv, grad_a_log, grad_dt, grad_gdn_nw, grad_w2,
             grad_b2, grad_gla_nw, "w_out", grad_ffn_nw, "w_gate_t", "w_up_t", "w_down", grad_final_nw]
    ms = [m_meta_tokens, m_attn_norm_w, m_w_in, m_gdn_conv_w, m_gdn_a_log, m_gdn_dt_bias, m_gdn_norm_w,
          m_gla_gate_w2, m_gla_gate_b, m_gla_norm_w, m_w_out, m_ffn_norm_w, m_w_gate, m_w_up, m_w_down, m_final_norm_w]
    vs = [v_meta_tokens, v_attn_norm_w, v_w_in, v_gdn_conv_w, v_gdn_a_log, v_gdn_dt_bias, v_gdn_norm_w,
          v_gla_gate_w2, v_gla_gate_b, v_gla_norm_w, v_w_out, v_ffn_norm_w, v_w_gate, v_w_up, v_w_down, v_final_norm_w]
    outs = [[], [], [], []]
    for idx, (w, gr, m, v) in enumerate(zip(weights, grads, ms, vs)):
        if isinstance(gr, str):
            res = big[gr]
        else:
            gr = gr.reshape(w.shape)
            res = (gr,) + _adamw(w, gr, m, v, name=f"adamw_{idx}")
        for lst, t in zip(outs, res):
            lst.append(t)
    return (loss, g["grad_x"][None], *outs[0], *outs[1], *outs[2], *outs[3])
```

```python
import functools

import jax
import jax.numpy as jnp
from jax import lax
from jax.experimental import pallas as pl
from jax.experimental.pallas import tpu as pltpu

F32 = jnp.float32
BF16 = jnp.bfloat16
_MXU_DTYPE = jnp.bfloat16

D_MODEL = 2048
N_META = 16
ROW_PAD = 48
HEAD_ROWS = ROW_PAD + N_META
CONV_K = 4
GDN_HEADS, GDN_DK, GDN_DV, GDN_CHUNK = 8, 128, 128, 64
GLA_HEADS, GLA_DK, GLA_DV, GLA_CHUNK = 4, 128, 256, 16
GLA_RANK = 16
GLA_GATE_NORMALIZER = 16.0
GDN_QK = GDN_HEADS * GDN_DK
GDN_V = GDN_HEADS * GDN_DV
GLA_QK = GLA_HEADS * GLA_DK
GLA_V = GLA_HEADS * GLA_DV
D_FF = 5632
D_IN = 7200
NORM_EPS = 1e-6
C_Z, C_GR, C_GQ, C_GK, C_GV, C_QKV, C_SM = 0, 1024, 2048, 2560, 3072, 4096, 7168
SM_W = 128
D_PROJ = 7680
R_Z, R_A, R_B, R_GQ, R_GK, R_GV, R_GR, R_LR = 3072, 4096, 4104, 4112, 4624, 5136, 6160, 7184

ADAM_LR, ADAM_B1, ADAM_B2, ADAM_EPS, ADAM_WD, ADAM_STEP = 0.001, 0.9, 0.999, 1e-08, 0.01, 10

N_DEV = 8
VMEM_LIMIT = 56 * 1024 * 1024

NN = (((1,), (0,)), ((), ()))
NT = (((1,), (1,)), ((), ()))
TN = (((0,), (0,)), ((), ()))


def _dot(a, b, dims=NN):
    return lax.dot_general(a.astype(_MXU_DTYPE), b.astype(_MXU_DTYPE), dims, preferred_element_type=F32)


def _running_sum(x, reverse=False):
    n = x.shape[0]
    row = lax.broadcasted_iota(jnp.int32, x.shape, 0)
    s = 1
    while s < n:
        if reverse:
            x = x + jnp.where(row < n - s, pltpu.roll(x, n - s, 0), 0.0)
        else:
            x = x + jnp.where(row >= s, pltpu.roll(x, s, 0), 0.0)
        s *= 2
    return x


def _dot3(a, b):
    ah = a.astype(BF16)
    al = (a - ah.astype(F32)).astype(BF16)
    bh = b.astype(BF16)
    bl = (b - bh.astype(F32)).astype(BF16)
    d = functools.partial(lax.dot_general, dimension_numbers=NN, preferred_element_type=F32)
    return d(ah, bh) + (d(ah, bl) + d(al, bh))


def _tile(n, target, mult=8):
    best = None
    for t in range(mult, min(n, target) + 1, mult):
        if n % t == 0:
            best = t
    return best if best is not None else n


def _params(*sem):
    return pltpu.CompilerParams(dimension_semantics=sem, vmem_limit_bytes=VMEM_LIMIT)


def _sigmoid(x):
    return 0.5 * jnp.tanh(0.5 * x) + 0.5


def _softplus(x):
    return jnp.maximum(x, 0.0) + jnp.log1p(jnp.exp(-jnp.abs(x)))


def _silu_and_grad(c):
    s = _sigmoid(c)
    return c * s, s * (1.0 + c * (1.0 - s))


_ANY = pl.BlockSpec(memory_space=pl.ANY)


def _matmul(a, b, *, mode, name, out_dtype=F32, add=None, after=None, tm=1376, tn=512):
    if mode == "tn":
        K, M = a.shape
        N = b.shape[1]
    else:
        M, K = a.shape
        N = b.shape[0] if mode == "nt" else b.shape[1]
    tm = _tile(M, tm, 128 if mode == "tn" else 16)
    tn = _tile(N, tn, 128)
    dims = {"nn": NN, "nt": NT, "tn": TN}[mode]
    n_after = 0 if after is None else 1

    def body(*refs):
        refs = refs[n_after:]
        r = _dot(refs[0][...], refs[1][...], dims)
        if add is not None:
            r = r + refs[2][...]
        refs[-1][...] = r.astype(out_dtype)

    a_spec = pl.BlockSpec((K, tm), lambda i, j: (0, i)) if mode == "tn" else pl.BlockSpec((tm, K), lambda i, j: (i, 0))
    b_spec = pl.BlockSpec((tn, K), lambda i, j: (j, 0)) if mode == "nt" else pl.BlockSpec((K, tn), lambda i, j: (0, j))
    o_spec = pl.BlockSpec((tm, tn), lambda i, j: (i, j))
    in_specs = [_ANY] * n_after + [a_spec, b_spec] + ([o_spec] if add is not None else [])
    args = ((after,) if n_after else ()) + (a, b) + ((add,) if add is not None else ())
    return pl.pallas_call(
        body, name=name, grid=(M // tm, N // tn), in_specs=in_specs, out_specs=o_spec,
        out_shape=jax.ShapeDtypeStruct((M, N), out_dtype), compiler_params=_params("parallel", "parallel"),
    )(*args)


def _in_proj(n, w_in_t, *, name, tm=1376, tn=512):
    M, K = n.shape
    N = w_in_t.shape[0]
    tm, tn = _tile(M, tm, 16), _tile(N, tn, 128)
    assert C_SM % tn == 0
    j_small = C_SM // tn

    def body(n_ref, w_ref, o_ref, sm_ref):
        r = _dot(n_ref[...], w_ref[...], NT)
        o_ref[...] = r.astype(o_ref.dtype)

        @pl.when(pl.program_id(1) == j_small)
        def _():
            sm_ref[...] = r[:, 0:SM_W]

    return pl.pallas_call(
        body, name=name, grid=(M // tm, N // tn),
        in_specs=[pl.BlockSpec((tm, K), lambda i, j: (i, 0)), pl.BlockSpec((tn, K), lambda i, j: (j, 0))],
        out_specs=[pl.BlockSpec((tm, tn), lambda i, j: (i, j)), pl.BlockSpec((tm, SM_W), lambda i, j: (i, 0))],
        out_shape=[jax.ShapeDtypeStruct((M, N), BF16), jax.ShapeDtypeStruct((M, SM_W), F32)],
        compiler_params=_params("parallel", "arbitrary"),
    )(n, w_in_t)


def _matmul_pair(a1, b1, a2, b2, *, name, after=None, tm=688, tn=256):
    M, K = a1.shape
    N = b1.shape[1]
    tm, tn = _tile(M, tm, 16), _tile(N, tn, 128)
    n_after = 0 if after is None else 1

    def body(*refs):
        a1_ref, b1_ref, a2_ref, b2_ref, o_ref = refs[n_after:]
        o_ref[...] = _dot(a1_ref[...], b1_ref[...]) + _dot(a2_ref[...], b2_ref[...])

    a_spec = pl.BlockSpec((tm, K), lambda i, j: (i, 0))
    b_spec = pl.BlockSpec((K, tn), lambda i, j: (0, j))
    return pl.pallas_call(
        body, name=name, grid=(M // tm, N // tn), in_specs=[_ANY] * n_after + [a_spec, b_spec, a_spec, b_spec],
        out_specs=pl.BlockSpec((tm, tn), lambda i, j: (i, j)), out_shape=jax.ShapeDtypeStruct((M, N), F32),
        compiler_params=_params("parallel", "parallel"),
    )(*((after,) if n_after else ()), a1, b1, a2, b2)


def _rmsnorm_fwd(h, w, *, name):
    M, D = h.shape
    tm = _tile(M, 688, 16)

    def body(h_ref, w_ref, n_ref):
        x = h_ref[...]
        r = lax.rsqrt(jnp.mean(x * x, axis=-1, keepdims=True) + NORM_EPS)
        n_ref[...] = (x * r * w_ref[...]).astype(n_ref.dtype)

    return pl.pallas_call(
        body, name=name, grid=(M // tm,),
        in_specs=[pl.BlockSpec((tm, D), lambda i: (i, 0)), pl.BlockSpec((1, D), lambda i: (0, 0))],
        out_specs=pl.BlockSpec((tm, D), lambda i: (i, 0)),
        out_shape=jax.ShapeDtypeStruct((M, D), BF16),
        compiler_params=_params("parallel"),
    )(h, w)


SEQ_BLOCK = HEAD_ROWS


def _seq_blocks_per_tile(rows):
    n = rows // SEQ_BLOCK
    return max(m for m in (1, 2, 3, 4) if n % m == 0)


def _seq_specs(m, D):
    return [pl.BlockSpec((SEQ_BLOCK, D), functools.partial(lambda i, k: (jnp.maximum(m * i + k - 1, 0), 0), k=k))
            for k in range(m)]


def _embed_norm(head, x, w, *, name, after=None):
    S, D = x.shape
    m = _seq_blocks_per_tile(S + HEAD_ROWS)
    n_after = 0 if after is None else 1

    def body(*refs):
        refs = refs[n_after:]
        head_ref, x_refs, w_ref, h_ref, n_ref = refs[0], refs[1:1 + m], refs[1 + m], refs[2 + m], refs[3 + m]
        i = pl.program_id(0)
        for k in range(m):
            blk = x_refs[k][...]
            if k == 0:
                blk = jnp.where(i == 0, head_ref[...], blk)
            rows = slice(k * SEQ_BLOCK, (k + 1) * SEQ_BLOCK)
            h_ref[rows, :] = blk
            r = lax.rsqrt(jnp.mean(blk * blk, axis=-1, keepdims=True) + NORM_EPS)
            n_ref[rows, :] = (blk * r * w_ref[...]).astype(n_ref.dtype)

    tile = pl.BlockSpec((m * SEQ_BLOCK, D), lambda i: (i, 0))
    return pl.pallas_call(
        body, name=name, grid=((S + HEAD_ROWS) // (m * SEQ_BLOCK),),
        in_specs=[_ANY] * n_after + [pl.BlockSpec((SEQ_BLOCK, D), lambda i: (0, 0))] + _seq_specs(m, D)
        + [pl.BlockSpec((1, D), lambda i: (0, 0))],
        out_specs=[tile, tile],
        out_shape=[jax.ShapeDtypeStruct((S + HEAD_ROWS, D), F32), jax.ShapeDtypeStruct((S + HEAD_ROWS, D), BF16)],
        compiler_params=_params("parallel"),
    )(*((after,) if n_after else ()), head, *([x] * m), w)


def _embed_norm_bwd(h, w, dn, dres, *, name):
    M, D = h.shape
    S = M - HEAD_ROWS
    m = _seq_blocks_per_tile(S)
    g = S // (m * SEQ_BLOCK)

    def one(x, dn_, dres_, w_):
        r = lax.rsqrt(jnp.mean(x * x, axis=-1, keepdims=True) + NORM_EPS)
        xhat = x * r
        dxhat = dn_ * w_
        dh = dres_ + r * (dxhat - xhat * jnp.mean(dxhat * xhat, axis=-1, keepdims=True))
        return dh, jnp.sum((dn_ * xhat).reshape(SEQ_BLOCK // 8, 8, D), axis=0)

    def body(*refs):
        w_ref = refs[0]
        groups = [refs[1 + a * (m + 1):1 + (a + 1) * (m + 1)] for a in range(3)]
        gx_ref, dhead_ref, dw_ref, acc_ref = refs[1 + 3 * (m + 1):]
        i = pl.program_id(0)
        w_ = w_ref[...]
        part = jnp.zeros((8, D), F32)
        for k in range(m):
            dh, p = one(*(grp[1 + k][...] for grp in groups), w_)
            gx_ref[k * SEQ_BLOCK:(k + 1) * SEQ_BLOCK, :] = dh
            part = part + p

        @pl.when(i == 0)
        def _():
            dh, p = one(*(grp[0][...] for grp in groups), w_)
            dhead_ref[...] = dh
            acc_ref[...] = part + p

        @pl.when(i > 0)
        def _():
            acc_ref[...] += part

        @pl.when(i == g - 1)
        def _():
            dw_ref[...] = jnp.sum(acc_ref[...], axis=0, keepdims=True)

    first = pl.BlockSpec((SEQ_BLOCK, D), lambda i: (0, 0))
    blocks = [pl.BlockSpec((SEQ_BLOCK, D), functools.partial(lambda i, k: (m * i + k + 1, 0), k=k)) for k in range(m)]
    vec = pl.BlockSpec((1, D), lambda i: (0, 0))
    return pl.pallas_call(
        body, name=name, grid=(g,), in_specs=[vec] + ([first] + blocks) * 3,
        out_specs=[pl.BlockSpec((m * SEQ_BLOCK, D), lambda i: (i, 0)), first, vec],
        out_shape=[jax.ShapeDtypeStruct((S, D), F32), jax.ShapeDtypeStruct((SEQ_BLOCK, D), F32),
                   jax.ShapeDtypeStruct((1, D), F32)],
        scratch_shapes=[pltpu.VMEM((8, D), F32)],
        compiler_params=_params("arbitrary"),
    )(w, *([h] * (m + 1)), *([dn] * (m + 1)), *([dres] * (m + 1)))


def _rmsnorm_bwd(h, w, dn, dres, *, name):
    M, D = h.shape
    tm = _tile(M, 344, 16)
    g = M // tm

    def body(h_ref, w_ref, dn_ref, dres_ref, dh_ref, dhb_ref, dw_ref, acc_ref):
        i = pl.program_id(0)
        x = h_ref[...]
        r = lax.rsqrt(jnp.mean(x * x, axis=-1, keepdims=True) + NORM_EPS)
        xhat = x * r
        dn_ = dn_ref[...]
        dxhat = dn_ * w_ref[...]
        dh = dres_ref[...] + r * (dxhat - xhat * jnp.mean(dxhat * xhat, axis=-1, keepdims=True))
        dh_ref[...] = dh
        dhb_ref[...] = dh.astype(dhb_ref.dtype)
        part = jnp.sum((dn_ * xhat).reshape(tm // 8, 8, D), axis=0)

        @pl.when(i == 0)
        def _():
            acc_ref[...] = part

        @pl.when(i > 0)
        def _():
            acc_ref[...] += part

        @pl.when(i == g - 1)
        def _():
            dw_ref[...] = jnp.sum(acc_ref[...], axis=0, keepdims=True)

    row = pl.BlockSpec((tm, D), lambda i: (i, 0))
    vec = pl.BlockSpec((1, D), lambda i: (0, 0))
    return pl.pallas_call(
        body, name=name, grid=(g,), in_specs=[row, vec, row, row],
        out_specs=[row, row, vec],
        out_shape=[jax.ShapeDtypeStruct((M, D), F32), jax.ShapeDtypeStruct((M, D), BF16),
                   jax.ShapeDtypeStruct((1, D), F32)],
        scratch_shapes=[pltpu.VMEM((8, D), F32)],
        compiler_params=_params("arbitrary"),
    )(h, w, dn, dres)


def _loss_head(h, w, target, *, name):
    M, D = h.shape
    m = _seq_blocks_per_tile(M)
    tm = m * SEQ_BLOCK
    g = M // tm

    def body(h_ref, w_ref, *rest):
        t_refs = rest[:m]
        dh_ref, dhb_ref, dw_ref, loss_ref, acc_ref, lacc_ref = rest[m:]
        i = pl.program_id(0)
        x = h_ref[...]
        row = i * tm + lax.broadcasted_iota(jnp.int32, (tm, 1), 0)
        live = row >= HEAD_ROWS
        r = lax.rsqrt(jnp.mean(x * x, axis=-1, keepdims=True) + NORM_EPS)
        xhat = x * r
        t = jnp.concatenate([t_ref[...] for t_ref in t_refs], axis=0)
        err = jnp.where(live, xhat * w_ref[...] - t, 0.0)
        dy = err * (1.0 / D)
        dxhat = dy * w_ref[...]
        dh = r * (dxhat - xhat * jnp.mean(dxhat * xhat, axis=-1, keepdims=True))
        dh_ref[...] = dh
        dhb_ref[...] = dh.astype(dhb_ref.dtype)
        part = jnp.sum((dy * xhat).reshape(tm // 8, 8, D), axis=0)
        lpart = jnp.sum((err * err).reshape(tm // 8, 8, D), axis=0)

        @pl.when(i == 0)
        def _():
            acc_ref[...] = part
            lacc_ref[...] = lpart

        @pl.when(i > 0)
        def _():
            acc_ref[...] += part
            lacc_ref[...] += lpart

        @pl.when(i == g - 1)
        def _():
            dw_ref[...] = jnp.sum(acc_ref[...], axis=0, keepdims=True)
            tot = jnp.sum(jnp.sum(lacc_ref[...], axis=0, keepdims=True), axis=1, keepdims=True)
            loss_ref[...] = jnp.broadcast_to(tot * (0.5 / D), (1, 128))

    row = pl.BlockSpec((tm, D), lambda i: (i, 0))
    vec = pl.BlockSpec((1, D), lambda i: (0, 0))
    return pl.pallas_call(
        body, name=name, grid=(g,), in_specs=[row, vec] + _seq_specs(m, D),
        out_specs=[row, row, vec, pl.BlockSpec((1, 128), lambda i: (0, 0))],
        out_shape=[jax.ShapeDtypeStruct((M, D), F32), jax.ShapeDtypeStruct((M, D), BF16),
                   jax.ShapeDtypeStruct((1, D), F32), jax.ShapeDtypeStruct((1, 128), F32)],
        scratch_shapes=[pltpu.VMEM((8, D), F32), pltpu.VMEM((8, D), F32)],
        compiler_params=_params("arbitrary"),
    )(h, w, *([target] * m))


def _gate_terms(sm, w2p, b2, alog_p, dt_p, row0):
    tm = sm.shape[0]
    lane = lax.broadcasted_iota(jnp.int32, (tm, SM_W), 1)
    live = (row0 + lax.broadcasted_iota(jnp.int32, (tm, 1), 0)) >= ROW_PAD
    pre = sm + dt_p
    neg_a = -jnp.exp(alog_p)
    g = neg_a * _softplus(pre)
    beta = _sigmoid(sm)
    z = _dot(sm, w2p) + b2
    return lane, live, pre, neg_a, g, beta, z


def _gates_fwd(sm, w2p, b2, alog_p, dt_p, *, name):
    M = sm.shape[0]
    tm = _tile(M, 688, 8)

    def body(sm_ref, w2_ref, b2_ref, al_ref, dt_ref, gb_ref, la_ref):
        row0 = pl.program_id(0) * tm
        lane, live, _, _, g, beta, z = _gate_terms(sm_ref[...].astype(F32), w2_ref[...], b2_ref[...], al_ref[...], dt_ref[...], row0)
        gb = jnp.where(lane < GDN_HEADS, g, jnp.where(lane < 2 * GDN_HEADS, beta, 0.0))
        gb_ref[...] = jnp.where(live, gb, 0.0)
        la = (jnp.minimum(z, 0.0) - jnp.log1p(jnp.exp(-jnp.abs(z)))) * (1.0 / GLA_GATE_NORMALIZER)
        la_ref[...] = jnp.where(live, la, 0.0)

    full = lambda s: pl.BlockSpec(s, lambda i: (0, 0))
    return pl.pallas_call(
        body, name=name, grid=(M // tm,),
        in_specs=[pl.BlockSpec((tm, SM_W), lambda i: (i, 0)), full((SM_W, GLA_QK)), full((1, GLA_QK)),
                  full((1, SM_W)), full((1, SM_W))],
        out_specs=[pl.BlockSpec((tm, SM_W), lambda i: (i, 0)), pl.BlockSpec((tm, GLA_QK), lambda i: (i, 0))],
        out_shape=[jax.ShapeDtypeStruct((M, SM_W), F32), jax.ShapeDtypeStruct((M, GLA_QK), F32)],
        compiler_params=_params("parallel"),
    )(sm, w2p, b2, alog_p, dt_p)


def _gates_bwd(sm, w2p, b2, alog_p, dt_p, dgb_heads, dla, d_proj, *, name):
    M = sm.shape[0]
    tm = _tile(M, 688, 8)
    g_ = M // tm

    tail_w = D_PROJ - C_SM

    def body(sm_ref, w2_ref, b2_ref, al_ref, dt_ref, dgb_ref, dla_ref, _,
             dsm_ref, dw2_ref, db2_ref, dal_ref, ddt_ref):
        i = pl.program_id(0)
        sm = sm_ref[...].astype(F32)
        lane, live, pre, neg_a, g, beta, z = _gate_terms(sm, w2_ref[...], b2_ref[...], al_ref[...], dt_ref[...], i * tm)
        dz = jnp.where(live, dla_ref[...] * (_sigmoid(-z) * (1.0 / GLA_GATE_NORMALIZER)), 0.0)
        dsm_lr = _dot(dz, w2_ref[...], NT)
        dgb = dgb_ref[0]
        for hh in range(1, GDN_HEADS):
            dgb = dgb + dgb_ref[hh]
        dgb = jnp.where(live, dgb, 0.0)
        da = dgb * neg_a * _sigmoid(pre)
        db = dgb * beta * (1.0 - beta)
        dsm = jnp.where(lane < GDN_HEADS, da, jnp.where(lane < 2 * GDN_HEADS, db, dsm_lr))
        dsm_ref[:, 0:SM_W] = dsm.astype(dsm_ref.dtype)
        if tail_w > SM_W:
            dsm_ref[:, SM_W:tail_w] = jnp.zeros((tm, tail_w - SM_W), dsm_ref.dtype)
        is_a = lane < GDN_HEADS
        dal = jnp.sum(jnp.where(is_a, dgb * g, 0.0), axis=0, keepdims=True)
        ddt = jnp.sum(jnp.where(is_a, da, 0.0), axis=0, keepdims=True)
        dw2 = _dot(sm, dz, TN)
        db2 = jnp.sum(dz, axis=0, keepdims=True)

        @pl.when(i == 0)
        def _():
            dw2_ref[...] = dw2
            db2_ref[...] = db2
            dal_ref[...] = dal
            ddt_ref[...] = ddt

        @pl.when(i > 0)
        def _():
            dw2_ref[...] += dw2
            db2_ref[...] += db2
            dal_ref[...] += dal
            ddt_ref[...] += ddt

    full = lambda s: pl.BlockSpec(s, lambda i: (0, 0))
    return pl.pallas_call(
        body, name=name, grid=(g_,),
        in_specs=[pl.BlockSpec((tm, SM_W), lambda i: (i, 0)), full((SM_W, GLA_QK)), full((1, GLA_QK)),
                  full((1, SM_W)), full((1, SM_W)),
                  pl.BlockSpec((GDN_HEADS, tm, SM_W), lambda i: (0, i, 0)),
                  pl.BlockSpec((tm, GLA_QK), lambda i: (i, 0)), _ANY],
        out_specs=[pl.BlockSpec((tm, tail_w), lambda i: (i, C_SM // tail_w)), full((SM_W, GLA_QK)), full((1, GLA_QK)),
                   full((1, SM_W)), full((1, SM_W))],
        out_shape=[jax.ShapeDtypeStruct(d_proj.shape, d_proj.dtype), jax.ShapeDtypeStruct((SM_W, GLA_QK), F32),
                   jax.ShapeDtypeStruct((1, GLA_QK), F32), jax.ShapeDtypeStruct((1, SM_W), F32),
                   jax.ShapeDtypeStruct((1, SM_W), F32)],
        input_output_aliases={7: 0},
        compiler_params=_params("arbitrary"),
    )(sm, w2p, b2, alog_p, dt_p, dgb_heads, dla, d_proj)


QKV_W = GDN_QK
N_QKV_GROUPS = 3
QKV_B0 = C_QKV // QKV_W
HALO = 16


def _conv_terms(x_ref, halo_ref, cw_ref, xs_ref, i, tm):
    xs_ref[HALO:HALO + tm, :] = x_ref[...].astype(F32)
    xs_ref[0:HALO, :] = jnp.where(i > 0, halo_ref[...].astype(F32), 0.0)
    cw = cw_ref[...]
    xs = xs_ref[...]
    taps = [(pltpu.roll(xs, CONV_K - 1 - t, 0) if t < CONV_K - 1 else xs)[HALO:HALO + tm, :] for t in range(CONV_K)]
    c = taps[0] * cw[0:1, :]
    for t in range(1, CONV_K):
        c = c + taps[t] * cw[t:t + 1, :]
    return c, taps


def _prep_fwd(proj, conv_w8, *, name):
    M = proj.shape[0]
    tm = _tile(M, 688, 16)

    def body(x_ref, halo_ref, cw_ref, o_ref, xs_ref):
        j, i = pl.program_id(0), pl.program_id(1)
        c, _ = _conv_terms(x_ref, halo_ref, cw_ref, xs_ref, i, tm)
        s, _ = _silu_and_grad(c)
        scale = jnp.where(j == 0, GDN_DK ** -0.5, 1.0)
        for hh in range(GDN_HEADS):
            cols = slice(hh * 128, (hh + 1) * 128)
            sh = s[:, cols]
            r = lax.rsqrt(jnp.sum(sh * sh, axis=-1, keepdims=True) + NORM_EPS)
            o_ref[:, cols] = jnp.where(j < 2, sh * (r * scale), sh)

    hb = tm // HALO
    return pl.pallas_call(
        body, name=name, grid=(N_QKV_GROUPS, M // tm),
        in_specs=[pl.BlockSpec((tm, QKV_W), lambda j, i: (i, QKV_B0 + j)),
                  pl.BlockSpec((HALO, QKV_W), lambda j, i: (jnp.maximum(i * hb - 1, 0), QKV_B0 + j)),
                  pl.BlockSpec((8, QKV_W), lambda j, i: (0, j))],
        out_specs=pl.BlockSpec((tm, QKV_W), lambda j, i: (i, j)),
        out_shape=jax.ShapeDtypeStruct((M, N_QKV_GROUPS * QKV_W), F32),
        scratch_shapes=[pltpu.VMEM((tm + HALO, QKV_W), F32)],
        compiler_params=_params("parallel", "arbitrary"),
    )(proj, proj, conv_w8)


def _prep_bwd(proj, conv_w8, dact, d_proj, *, name):
    M = proj.shape[0]
    tm = _tile(M, 688, 16)
    g_ = M // tm
    ext = tm + HALO

    def body(x_ref, prev_ref, next_ref, cw_ref, da_ref, dan_ref, _, o_ref, dcw_ref, xs_ref, das_ref, dcs_ref):
        j, i = pl.program_id(0), pl.program_id(1)
        not_last = i < g_ - 1
        xs_ref[0:HALO, :] = jnp.where(i > 0, prev_ref[...].astype(F32), 0.0)
        xs_ref[HALO:HALO + tm, :] = x_ref[...].astype(F32)
        xs_ref[HALO + tm:HALO + ext, :] = jnp.where(not_last, next_ref[...].astype(F32), 0.0)
        das_ref[0:tm, :] = da_ref[...]
        das_ref[tm:ext, :] = jnp.where(not_last, dan_ref[...], 0.0)
        cw = cw_ref[...]
        xs = xs_ref[...]
        taps = [(pltpu.roll(xs, CONV_K - 1 - t, 0) if t < CONV_K - 1 else xs)[HALO:HALO + ext, :] for t in range(CONV_K)]
        c = taps[0] * cw[0:1, :]
        for t in range(1, CONV_K):
            c = c + taps[t] * cw[t:t + 1, :]
        s, ds_dc = _silu_and_grad(c)
        scale = jnp.where(j == 0, GDN_DK ** -0.5, 1.0)
        for hh in range(GDN_HEADS):
            cols = slice(hh * 128, (hh + 1) * 128)
            sh = s[:, cols]
            r = lax.rsqrt(jnp.sum(sh * sh, axis=-1, keepdims=True) + NORM_EPS)
            da = das_ref[:, cols]
            y = sh * r
            dy = da * scale
            ds_norm = r * (dy - y * jnp.sum(dy * y, axis=-1, keepdims=True))
            dcs_ref[:, cols] = jnp.where(j < 2, ds_norm, da) * ds_dc[:, cols]
        dc = dcs_ref[...]
        acc = dc[0:tm, :] * cw[CONV_K - 1:CONV_K, :]
        for t in range(CONV_K - 1):
            acc = acc + pltpu.roll(dc, ext - (CONV_K - 1 - t), 0)[0:tm, :] * cw[t:t + 1, :]
        o_ref[...] = acc.astype(o_ref.dtype)
        r8 = lax.broadcasted_iota(jnp.int32, (8, QKV_W), 0)
        part = jnp.zeros((8, QKV_W), F32)
        for t in range(CONV_K):
            part = jnp.where(r8 == t, jnp.sum(dc[0:tm, :] * taps[t][0:tm, :], axis=0, keepdims=True), part)

        @pl.when(i == 0)
        def _():
            dcw_ref[...] = part

        @pl.when(i > 0)
        def _():
            dcw_ref[...] += part

    hb = tm // HALO
    last = M // HALO - 1
    prev_of = lambda i: jnp.maximum(i * hb - 1, 0)
    next_of = lambda i: jnp.minimum((i + 1) * hb, last)
    return pl.pallas_call(
        body, name=name, grid=(N_QKV_GROUPS, g_),
        in_specs=[pl.BlockSpec((tm, QKV_W), lambda j, i: (i, QKV_B0 + j)),
                  pl.BlockSpec((HALO, QKV_W), lambda j, i: (prev_of(i), QKV_B0 + j)),
                  pl.BlockSpec((HALO, QKV_W), lambda j, i: (next_of(i), QKV_B0 + j)),
                  pl.BlockSpec((8, QKV_W), lambda j, i: (0, j)),
                  pl.BlockSpec((tm, QKV_W), lambda j, i: (i, j)),
                  pl.BlockSpec((HALO, QKV_W), lambda j, i: (next_of(i), j)), _ANY],
        out_specs=[pl.BlockSpec((tm, QKV_W), lambda j, i: (i, QKV_B0 + j)), pl.BlockSpec((8, QKV_W), lambda j, i: (0, j))],
        out_shape=[jax.ShapeDtypeStruct(d_proj.shape, d_proj.dtype),
                   jax.ShapeDtypeStruct((8, N_QKV_GROUPS * QKV_W), F32)],
        input_output_aliases={6: 0},
        scratch_shapes=[pltpu.VMEM((HALO + ext, QKV_W), F32), pltpu.VMEM((ext, QKV_W), F32), pltpu.VMEM((ext, QKV_W), F32)],
        compiler_params=_params("parallel", "arbitrary"),
    )(proj, proj, proj, conv_w8, dact, dact, d_proj)


def _round_robin(gens):
    gens = list(gens)
    while gens:
        alive = []
        for gen in gens:
            try:
                next(gen)
                alive.append(gen)
            except StopIteration:
                pass
        gens = alive


def _unit_lower_inverse(a_low, eye):
    n = a_low.shape[0]
    ri = lax.broadcasted_iota(jnp.int32, (n, n), 0)
    ci = lax.broadcasted_iota(jnp.int32, (n, n), 1)
    same = lambda shift: (ri >> shift) == (ci >> shift)
    b = jnp.where(same(3), -a_low, 0.0)
    x = eye + b
    p2 = _dot3(b, b)
    yield
    x = x + _dot3(x, p2)
    p4 = _dot3(p2, p2)
    yield
    x = x + _dot3(x, p4)
    yield
    for shift in (3, 4, 5):
        between = jnp.where(same(shift + 1) & ~same(shift), a_low, 0.0)
        t = _dot3(between, x)
        yield
        x = x - _dot3(x, t)
        yield
    return x


class _GdnChunk:
    def build(self, q, k, v, gb, h, sum_on_mxu):
        C = GDN_CHUNK
        lane = lax.broadcasted_iota(jnp.int32, (C, SM_W), 1)
        g = jnp.sum(jnp.where(lane == h, gb, 0.0), axis=1, keepdims=True)
        self.beta = jnp.sum(jnp.where(lane == h + GDN_HEADS, gb, 0.0), axis=1, keepdims=True)
        ri = lax.broadcasted_iota(jnp.int32, (C, C), 0)
        ci = lax.broadcasted_iota(jnp.int32, (C, C), 1)
        self.causal = ri >= ci
        self.strict = ri > ci
        self.eye = (ri == ci).astype(F32)
        if sum_on_mxu:
            gcb = lax.dot_general(self.causal.astype(F32), jnp.broadcast_to(g, (C, SM_W)), NN,
                                  precision=lax.Precision.HIGHEST, preferred_element_type=F32)
        else:
            gcb = _running_sum(jnp.broadcast_to(g, (C, SM_W)))
        yield
        self.gcol = gcb[:, 0:1]
        grow = gcb.T[0:1, 0:C]
        self.decay = jnp.exp(jnp.where(self.causal, self.gcol - grow, -1e30))
        self.egc = jnp.exp(self.gcol)
        glast = gcb[C - 1:C, 0:1]
        self.elast = jnp.exp(glast - self.gcol)
        self.gl = jnp.exp(glast)
        self.q, self.k, self.v = q, k, v
        self.kb = k * self.beta
        m = _dot(self.kb, k, NT)
        n_ = _dot(q, k, NT)
        yield
        self.a_low = jnp.where(self.strict, m * self.decay, 0.0)
        self.p = n_ * self.decay
        self.qd = q * self.egc
        self.kd = k * self.elast
        self.bu = v * self.beta
        self.bw = self.kb * self.egc


GDN_HB = 8
GDN_HG = GDN_HEADS // GDN_HB


def _gdn_specs(n_of):
    C, W = GDN_CHUNK, 128 * GDN_HB
    q_spec = pl.BlockSpec((C, W), lambda g, n: (n_of(n), g))
    k_spec = pl.BlockSpec((C, W), lambda g, n: (n_of(n), g + GDN_HG))
    v_spec = pl.BlockSpec((C, W), lambda g, n: (n_of(n), g + 2 * GDN_HG))
    gb_spec = pl.BlockSpec((C, SM_W), lambda g, n: (n_of(n), 0))
    o_spec = pl.BlockSpec((C, W), lambda g, n: (n_of(n), g))
    s_spec = pl.BlockSpec((GDN_HB, None, GDN_DK, GDN_DV), lambda g, n: (g, n_of(n), 0, 0))
    t_spec = pl.BlockSpec((GDN_HB, None, C, C), lambda g, n: (g, n_of(n), 0, 0))
    return q_spec, k_spec, v_spec, gb_spec, o_spec, s_spec, t_spec


def _gdn_fwd(act, gb, *, name):
    M = act.shape[0]
    N = M // GDN_CHUNK

    def body(q_ref, k_ref, v_ref, gb_ref, o_ref, s_ref, t_ref, state):
        g, n = pl.program_id(0), pl.program_id(1)

        @pl.when(n == 0)
        def _():
            state[...] = jnp.zeros_like(state)

        gb_ = gb_ref[...]

        def head(hh):
            cols = slice(hh * 128, (hh + 1) * 128)
            c = _GdnChunk()
            yield from c.build(q_ref[:, cols], k_ref[:, cols], v_ref[:, cols], gb_, g * GDN_HB + hh, sum_on_mxu=True)
            tinv = yield from _unit_lower_inverse(c.a_low, c.eye)
            s = state[hh]
            s_ref[hh] = s
            t_ref[hh] = tinv
            u = _dot(tinv, c.bu)
            w = _dot(tinv, c.bw)
            yield
            vn = u - _dot(w, s)
            o1 = _dot(c.qd, s)
            yield
            o_ref[:, cols] = (o1 + _dot(c.p, vn)).astype(o_ref.dtype)
            state[hh] = c.gl * s + _dot(c.kd, vn, TN)

        _round_robin(head(hh) for hh in range(GDN_HB))

    q_spec, k_spec, v_spec, gb_spec, o_spec, s_spec, t_spec = _gdn_specs(lambda n: n)
    return pl.pallas_call(
        body, name=name, grid=(GDN_HG, N),
        in_specs=[q_spec, k_spec, v_spec, gb_spec], out_specs=[o_spec, s_spec, t_spec],
        out_shape=[jax.ShapeDtypeStruct((M, GDN_V), BF16),
                   jax.ShapeDtypeStruct((GDN_HEADS, N, GDN_DK, GDN_DV), F32),
                   jax.ShapeDtypeStruct((GDN_HEADS, N, GDN_CHUNK, GDN_CHUNK), F32)],
        scratch_shapes=[pltpu.VMEM((GDN_HB, GDN_DK, GDN_DV), F32)],
        compiler_params=_params("parallel", "arbitrary"),
    )(act, act, act, gb)


def _gdn_bwd(act, gb, do, s_all, t_all, *, name):
    M = act.shape[0]
    N = M // GDN_CHUNK
    C = GDN_CHUNK
    assert GDN_HG == 1

    def body(q_ref, k_ref, v_ref, gb_ref, do_ref, s_ref, t_ref, dact_ref, dgb_ref, dstate):
        g, n = pl.program_id(0), pl.program_id(1)

        @pl.when(n == 0)
        def _():
            dstate[...] = jnp.zeros_like(dstate)

        gb_ = gb_ref[...]
        last = lax.broadcasted_iota(jnp.int32, (C, 1), 0) == C - 1
        lane = lax.broadcasted_iota(jnp.int32, (C, SM_W), 1)
        def head(hh):
            cols = slice(hh * 128, (hh + 1) * 128)
            h = g * GDN_HB + hh
            c = _GdnChunk()
            yield from c.build(q_ref[:, cols], k_ref[:, cols], v_ref[:, cols], gb_, h, sum_on_mxu=False)
            tinv = t_ref[hh]
            tinv_t = tinv.T
            s = s_ref[hh]
            do_ = do_ref[:, cols]
            ds1 = dstate[hh]
            u = _dot(tinv, c.bu)
            w = _dot(tinv, c.bw)
            dqd = _dot(do_, s, NT)
            yield
            dvn0 = _dot(c.p, do_, TN) + _dot(c.kd, ds1)
            dst0 = _dot(c.qd, do_, TN) + c.gl * ds1
            yield
            vn = u - _dot(w, s)
            dvn = dvn0
            yield
            dp = jnp.where(c.causal, _dot(do_, vn, NT), 0.0)
            dstate[hh] = dst0 - _dot(w, dvn, TN)
            dkd = _dot(vn, ds1, NT)
            dw = -_dot(dvn, s, NT)
            dbu = _dot(tinv_t, dvn)
            dgl = jnp.sum(jnp.sum(s * ds1, axis=1, keepdims=True), axis=0, keepdims=True)
            yield
            dbw = _dot(tinv_t, dw)
            t1 = _dot(dbu, u, NT)
            yield
            da = jnp.where(c.strict, -(t1 + _dot(dbw, w, NT)), 0.0)
            dn_ = dp * c.decay
            dq0 = _dot(dn_, c.k)
            dk0 = _dot(dn_, c.q, TN)
            yield
            dm = da * c.decay
            e = da * c.a_low + dp * c.p
            dkb = _dot(dm, c.k) + dbw * c.egc
            dact_ref[:, GDN_QK + hh * 128:GDN_QK + (hh + 1) * 128] = (
                _dot(dm, c.kb, TN) + dk0 + dkb * c.beta + dkd * c.elast)
            dact_ref[:, cols] = dq0 + dqd * c.egc
            dact_ref[:, 2 * GDN_QK + hh * 128:2 * GDN_QK + (hh + 1) * 128] = dbu * c.beta
            dbeta = jnp.sum(dbu * c.v, axis=1, keepdims=True) + jnp.sum(dkb * c.k, axis=1, keepdims=True)
            t_kd = jnp.sum(dkd * c.kd, axis=1, keepdims=True)
            dgc = (jnp.sum(e, axis=1, keepdims=True) - jnp.sum(e.T, axis=1, keepdims=True)
                   + jnp.sum(dbw * c.bw, axis=1, keepdims=True) + jnp.sum(dqd * c.qd, axis=1, keepdims=True) - t_kd)
            dgc = dgc + jnp.where(last, jnp.sum(t_kd, axis=0, keepdims=True) + dgl * c.gl, 0.0)
            yield
            dg = _running_sum(jnp.broadcast_to(dgc, (C, SM_W)), reverse=True)
            dgb_ref[hh] = jnp.where(lane == h, dg, jnp.where(lane == h + GDN_HEADS, dbeta, 0.0))

        _round_robin(head(hh) for hh in range(GDN_HB))

    rev = lambda n: N - 1 - n
    q_spec, k_spec, v_spec, gb_spec, o_spec, s_spec, t_spec = _gdn_specs(rev)
    dgb_spec = pl.BlockSpec((GDN_HB, C, SM_W), lambda g, n: (g, rev(n), 0))
    return pl.pallas_call(
        body, name=name, grid=(GDN_HG, N),
        in_specs=[q_spec, k_spec, v_spec, gb_spec, o_spec, s_spec, t_spec],
        out_specs=[pl.BlockSpec((C, 2 * GDN_QK + GDN_V), lambda g, n: (rev(n), 0)), dgb_spec],
        out_shape=[jax.ShapeDtypeStruct((M, 2 * GDN_QK + GDN_V), F32),
                   jax.ShapeDtypeStruct((GDN_HEADS, M, SM_W), F32)],
        scratch_shapes=[pltpu.VMEM((GDN_HB, GDN_DK, GDN_DV), F32)],
        compiler_params=_params("parallel", "arbitrary"),
    )(act, act, act, gb, do, s_all, t_all)


GLA_STEP_ROWS = 64
GLA_SUB = GLA_STEP_ROWS // GLA_CHUNK


def _gla_cumsum(la):
    return _running_sum(la)


GLA_HALF = GLA_CHUNK // 2


def _gla_cross_factors(b):
    top = lax.broadcasted_iota(jnp.int32, b.shape, 0) < GLA_HALF
    bm = b[GLA_HALF - 1:GLA_HALF, :]
    late = jnp.where(top, 0.0, jnp.exp(jnp.minimum(b - bm, 0.0)))
    early = jnp.where(top, jnp.exp(jnp.minimum(bm - b, 0.0)), 0.0)
    return late, early


def _gla_half_decay(bh, ii):
    rj = lax.broadcasted_iota(jnp.int32, bh.shape, 0)
    return jnp.where(rj <= ii, jnp.exp(jnp.minimum(bh[ii:ii + 1, :] - bh, 0.0)), 0.0)


def _gla_scores_t(q, k, b):
    C, H = GLA_CHUNK, GLA_HALF
    lane = lax.broadcasted_iota(jnp.int32, (H, C), 1)
    halves = []
    for h0 in (0, H):
        qh, kh, bh = q[h0:h0 + H], k[h0:h0 + H], b[h0:h0 + H]
        sth = jnp.zeros((H, C), F32)
        for ii in range(H):
            si = jnp.sum(qh[ii:ii + 1, :] * kh * _gla_half_decay(bh, ii), axis=1, keepdims=True)
            sth = jnp.where(lane == h0 + ii, si, sth)
            if ii % 4 == 3:
                yield
        halves.append(sth)
    late, early = _gla_cross_factors(b)
    between = _dot(k * early, q * late, NT)
    yield
    return jnp.concatenate(halves, axis=0) + between


def _gla_specs(n_of):
    R = GLA_STEP_ROWS
    q_spec = pl.BlockSpec((R, GLA_QK), lambda n: (n_of(n), C_GQ // GLA_QK))
    k_spec = pl.BlockSpec((R, GLA_QK), lambda n: (n_of(n), C_GK // GLA_QK))
    v_spec = pl.BlockSpec((R, GLA_V), lambda n: (n_of(n), C_GV // GLA_V))
    la_spec = pl.BlockSpec((R, GLA_QK), lambda n: (n_of(n), 0))
    o_spec = pl.BlockSpec((R, GLA_V), lambda n: (n_of(n), 0))
    s_spec = pl.BlockSpec((GLA_HEADS, None, GLA_SUB, GLA_DV, GLA_DK), lambda n: (0, n_of(n), 0, 0, 0))
    return q_spec, k_spec, v_spec, la_spec, o_spec, s_spec


def _gla_fwd(proj, la, *, name):
    M = proj.shape[0]
    N = M // GLA_STEP_ROWS
    C = GLA_CHUNK

    def body(q_ref, k_ref, v_ref, la_ref, o_ref, s_ref, state):
        n = pl.program_id(0)

        @pl.when(n == 0)
        def _():
            state[...] = jnp.zeros_like(state)

        local = {}

        def within(hh, c):
            kc = slice(hh * GLA_DK, (hh + 1) * GLA_DK)
            vc = slice(hh * GLA_DV, (hh + 1) * GLA_DV)
            rows = slice(c * C, (c + 1) * C)
            q = q_ref[rows, kc].astype(F32) * (GLA_DK ** -0.5)
            k = k_ref[rows, kc].astype(F32)
            v = v_ref[rows, vc].astype(F32)
            b = _gla_cumsum(la_ref[rows, kc])
            yield
            blast = b[C - 1:C, :]
            sc_t = yield from _gla_scores_t(q, k, b)
            kv = _dot(v, k * jnp.exp(blast - b), TN)
            o2 = _dot(sc_t, v, TN)
            yield
            local[hh, c] = (q * jnp.exp(b), jnp.exp(blast), kv, o2)

        def across(hh):
            vc = slice(hh * GLA_DV, (hh + 1) * GLA_DV)
            st = state[hh]
            for c in range(GLA_SUB):
                qe, eblast, kv, o2 = local[hh, c]
                s_ref[hh, c] = st
                o1 = _dot(qe, st, NT)
                yield
                o_ref[c * C:(c + 1) * C, vc] = (o1 + o2).astype(o_ref.dtype)
                st = st * eblast + kv
            state[hh] = st

        _round_robin(within(hh, c) for c in range(GLA_SUB) for hh in range(GLA_HEADS))
        _round_robin(across(hh) for hh in range(GLA_HEADS))

    q_spec, k_spec, v_spec, la_spec, o_spec, s_spec = _gla_specs(lambda n: n)
    return pl.pallas_call(
        body, name=name, grid=(N,),
        in_specs=[q_spec, k_spec, v_spec, la_spec], out_specs=[o_spec, s_spec],
        out_shape=[jax.ShapeDtypeStruct((M, GLA_V), BF16),
                   jax.ShapeDtypeStruct((GLA_HEADS, N, GLA_SUB, GLA_DV, GLA_DK), F32)],
        scratch_shapes=[pltpu.VMEM((GLA_HEADS, GLA_DV, GLA_DK), F32)],
        compiler_params=_params("arbitrary"),
    )(proj, proj, proj, la)


def _gla_bwd(proj, la, do, s_all, d_proj, *, name):
    M = proj.shape[0]
    N = M // GLA_STEP_ROWS
    C = GLA_CHUNK
    qkv_w = 2 * GLA_QK + GLA_V
    assert C_GK == C_GQ + GLA_QK and C_GV == C_GK + GLA_QK and C_GQ % qkv_w == 0

    def body(q_ref, k_ref, v_ref, la_ref, do_ref, s_ref, _, dp_ref, dla_ref, dstate):
        n = pl.program_id(0)

        @pl.when(n == 0)
        def _():
            dstate[...] = jnp.zeros_like(dstate)

        H = GLA_HALF
        lane = lax.broadcasted_iota(jnp.int32, (C, C), 1)
        row = lax.broadcasted_iota(jnp.int32, (C, C), 0)
        ri = lax.broadcasted_iota(jnp.int32, (C, GLA_DK), 0)
        lane_h = lax.broadcasted_iota(jnp.int32, (H, C), 1)
        ri_h = lax.broadcasted_iota(jnp.int32, (H, GLA_DK), 0)
        cross = (row < H) & (lane >= H)
        def head(hh):
            kc = slice(hh * GLA_DK, (hh + 1) * GLA_DK)
            vc = slice(hh * GLA_DV, (hh + 1) * GLA_DV)
            ds1 = dstate[hh]
            for c in reversed(range(GLA_SUB)):
                rows = slice(c * C, (c + 1) * C)
                q = q_ref[rows, kc].astype(F32) * (GLA_DK ** -0.5)
                k = k_ref[rows, kc].astype(F32)
                v = v_ref[rows, vc].astype(F32)
                b = _gla_cumsum(la_ref[rows, kc])
                do_ = do_ref[rows, vc]
                st = s_ref[hh, c]
                dsc_t = _dot(v, do_, NT)
                dqe = _dot(do_, st)
                dke = _dot(v, ds1)
                yield
                blast = b[C - 1:C, :]
                eb = jnp.exp(b)
                elast = jnp.exp(blast - b)
                eblast = jnp.exp(blast)
                qe = q * eb
                ke = k * elast
                dv2 = _dot(ke, ds1, NT)
                ds_new = _dot(do_, qe, TN)
                deblast = jnp.sum(st * ds1, axis=0, keepdims=True)
                sc_halves, dq_halves, dk_halves = [], [], []
                for h0 in (0, H):
                    qh, kh, bh, dsch = q[h0:h0 + H], k[h0:h0 + H], b[h0:h0 + H], dsc_t[h0:h0 + H]
                    sch = jnp.zeros((H, C), F32)
                    dqh = jnp.zeros((H, GLA_DK), F32)
                    dkh = jnp.zeros((H, GLA_DK), F32)
                    for ii in range(H):
                        f = _gla_half_decay(bh, ii)
                        kf = kh * f
                        si = jnp.sum(qh[ii:ii + 1, :] * kf, axis=1, keepdims=True)
                        sch = jnp.where(lane_h == h0 + ii, si, sch)
                        dsi = jnp.sum(jnp.where(lane_h == h0 + ii, dsch, 0.0), axis=1, keepdims=True)
                        dqh = jnp.where(ri_h == ii, jnp.sum(dsi * kf, axis=0, keepdims=True), dqh)
                        dkh = dkh + (dsi * f) * qh[ii:ii + 1, :]
                        if ii % 4 == 3:
                            yield
                    sc_halves.append(sch)
                    dq_halves.append(dqh)
                    dk_halves.append(dkh)
                late, early = _gla_cross_factors(b)
                q_late, k_early = q * late, k * early
                dsc_x = jnp.where(cross, dsc_t, 0.0)
                sc_t = jnp.concatenate(sc_halves, axis=0) + _dot(k_early, q_late, NT)
                dq_sc = jnp.concatenate(dq_halves, axis=0) + _dot(dsc_x, k_early, TN) * late
                dk_sc = jnp.concatenate(dk_halves, axis=0) + _dot(dsc_x, q_late) * early
                yield
                dv1 = _dot(sc_t, do_)
                dp_ref[rows, kc] = ((dq_sc + dqe * eb) * (GLA_DK ** -0.5)).astype(dp_ref.dtype)
                dp_ref[rows, GLA_QK + hh * GLA_DK:GLA_QK + (hh + 1) * GLA_DK] = (dk_sc + dke * elast).astype(dp_ref.dtype)
                t_ke = dke * ke
                db = q * dq_sc - k * dk_sc + dqe * qe - t_ke
                db = db + jnp.where(ri == C - 1, jnp.sum(t_ke, axis=0, keepdims=True) + deblast * eblast, 0.0)
                dla = _running_sum(db, reverse=True)
                yield
                dp_ref[rows, 2 * GLA_QK + hh * GLA_DV:2 * GLA_QK + (hh + 1) * GLA_DV] = (dv1 + dv2).astype(dp_ref.dtype)
                dla_ref[rows, kc] = dla
                ds1 = ds1 * eblast + ds_new
            dstate[hh] = ds1

        _round_robin(head(hh) for hh in range(GLA_HEADS))

    rev = lambda n: N - 1 - n
    q_spec, k_spec, v_spec, la_spec, o_spec, s_spec = _gla_specs(rev)
    return pl.pallas_call(
        body, name=name, grid=(N,),
        in_specs=[q_spec, k_spec, v_spec, la_spec, o_spec, s_spec, _ANY],
        out_specs=[pl.BlockSpec((GLA_STEP_ROWS, qkv_w), lambda n: (rev(n), C_GQ // qkv_w)), la_spec],
        out_shape=[jax.ShapeDtypeStruct(d_proj.shape, d_proj.dtype), jax.ShapeDtypeStruct((M, GLA_QK), F32)],
        input_output_aliases={6: 0},
        scratch_shapes=[pltpu.VMEM((GLA_HEADS, GLA_DV, GLA_DK), F32)],
        compiler_params=_params("arbitrary"),
    )(proj, proj, proj, la, do, s_all, d_proj)


def _head_norm(o, wn):
    r = lax.rsqrt(jnp.mean(o * o, axis=-1, keepdims=True) + NORM_EPS)
    return o * r, r


def _mix_heads():
    heads = [(0, GDN_DV, hh * GDN_DV, hh * GDN_DV) for hh in range(GDN_HEADS)]
    heads += [(1, GLA_DV, GDN_V + hh * GLA_DV, hh * GLA_DV) for hh in range(GLA_HEADS)]
    return heads


def _mix_fwd(o_gdn, o_gla, proj, wn_gdn, wn_gla, *, name):
    M = proj.shape[0]
    tm = _tile(M, 344, 16)

    def body(og_ref, ol_ref, z_ref, r_ref, wg_ref, wl_ref, m_ref):
        srcs = ((og_ref, z_ref, wg_ref), (ol_ref, r_ref, wl_ref))
        for grp, width, mcol, col in _mix_heads():
            o_ref, gate_ref, w_ref = srcs[grp]
            xhat, _ = _head_norm(o_ref[:, col:col + width].astype(F32), None)
            gate, _ = _silu_and_grad(gate_ref[:, col:col + width].astype(F32))
            m_ref[:, mcol:mcol + width] = (xhat * w_ref[...] * gate).astype(m_ref.dtype)

    full = lambda s: pl.BlockSpec(s, lambda i: (0, 0))
    return pl.pallas_call(
        body, name=name, grid=(M // tm,),
        in_specs=[pl.BlockSpec((tm, GDN_V), lambda i: (i, 0)), pl.BlockSpec((tm, GLA_V), lambda i: (i, 0)),
                  pl.BlockSpec((tm, GDN_V), lambda i: (i, C_Z // GDN_V)),
                  pl.BlockSpec((tm, GLA_V), lambda i: (i, C_GR // GLA_V)),
                  full((1, GDN_DV)), full((1, GLA_DV))],
        out_specs=pl.BlockSpec((tm, D_MODEL), lambda i: (i, 0)),
        out_shape=jax.ShapeDtypeStruct((M, D_MODEL), BF16),
        compiler_params=_params("parallel"),
    )(o_gdn, o_gla, proj, proj, wn_gdn, wn_gla)


def _mix_bwd(o_gdn, o_gla, proj, wn_gdn, wn_gla, dmixed, *, name):
    M = proj.shape[0]
    tm = _tile(M, 344, 16)
    g_ = M // tm
    assert C_Z == 0 and C_GR == GDN_V

    def body(og_ref, ol_ref, z_ref, r_ref, wg_ref, wl_ref, dm_ref,
             dog_ref, dol_ref, dzr_ref, dwg_ref, dwl_ref):
        i = pl.program_id(0)
        srcs = ((og_ref, z_ref, wg_ref, dog_ref), (ol_ref, r_ref, wl_ref, dol_ref))
        dws = [jnp.zeros((1, GDN_DV), F32), jnp.zeros((1, GLA_DV), F32)]
        for grp, width, mcol, col in _mix_heads():
            o_ref, gate_ref, w_ref, do_ref = srcs[grp]
            cols = slice(col, col + width)
            xhat, r = _head_norm(o_ref[:, cols].astype(F32), None)
            gate, dgate_dc = _silu_and_grad(gate_ref[:, cols].astype(F32))
            dm = dm_ref[:, mcol:mcol + width]
            dzr_ref[:, mcol:mcol + width] = (dm * xhat * w_ref[...] * dgate_dc).astype(dzr_ref.dtype)
            dnorm = dm * gate
            dws[grp] = dws[grp] + jnp.sum(dnorm * xhat, axis=0, keepdims=True)
            dxhat = dnorm * w_ref[...]
            do_ref[:, cols] = r * (dxhat - xhat * jnp.mean(dxhat * xhat, axis=-1, keepdims=True))

        @pl.when(i == 0)
        def _():
            dwg_ref[...] = dws[0]
            dwl_ref[...] = dws[1]

        @pl.when(i > 0)
        def _():
            dwg_ref[...] += dws[0]
            dwl_ref[...] += dws[1]

    full = lambda s: pl.BlockSpec(s, lambda i: (0, 0))
    half = pl.BlockSpec((tm, GDN_V), lambda i: (i, 0))
    return pl.pallas_call(
        body, name=name, grid=(g_,),
        in_specs=[half, half, pl.BlockSpec((tm, GDN_V), lambda i: (i, C_Z // GDN_V)),
                  pl.BlockSpec((tm, GLA_V), lambda i: (i, C_GR // GLA_V)),
                  full((1, GDN_DV)), full((1, GLA_DV)), pl.BlockSpec((tm, D_MODEL), lambda i: (i, 0))],
        out_specs=[half, half, pl.BlockSpec((tm, GDN_V + GLA_V), lambda i: (i, 0)),
                   full((1, GDN_DV)), full((1, GLA_DV))],
        out_shape=[jax.ShapeDtypeStruct((M, GDN_V), F32), jax.ShapeDtypeStruct((M, GLA_V), F32),
                   jax.ShapeDtypeStruct((M, D_PROJ), BF16),
                   jax.ShapeDtypeStruct((1, GDN_DV), F32), jax.ShapeDtypeStruct((1, GLA_DV), F32)],
        compiler_params=_params("arbitrary"),
    )(o_gdn, o_gla, proj, proj, wn_gdn, wn_gla, dmixed)


def _row_chunks(tm, parts=2):
    if tm % (16 * parts):
        return [slice(0, tm)]
    return [slice(p * (tm // parts), (p + 1) * (tm // parts)) for p in range(parts)]


def _swiglu_fwd(n, w_gate_t, w_up_t, *, name, tm=1376, tn=512):
    M, D = n.shape
    F = w_gate_t.shape[0]
    tm, tn = _tile(M, tm, 16), _tile(F, tn, 128)

    def body(n_ref, wg_ref, wu_ref, g_ref, u_ref, a_ref):
        wg, wu = wg_ref[...], wu_ref[...]
        for rows in _row_chunks(tm):
            x = n_ref[rows, :]
            g = _dot(x, wg, NT)
            u = _dot(x, wu, NT)
            s, _ = _silu_and_grad(g)
            g_ref[rows, :] = g.astype(g_ref.dtype)
            u_ref[rows, :] = u.astype(u_ref.dtype)
            a_ref[rows, :] = (s * u).astype(a_ref.dtype)

    w_spec = pl.BlockSpec((tn, D), lambda i, j: (j, 0))
    o_spec = pl.BlockSpec((tm, tn), lambda i, j: (i, j))
    return pl.pallas_call(
        body, name=name, grid=(M // tm, F // tn),
        in_specs=[pl.BlockSpec((tm, D), lambda i, j: (i, 0)), w_spec, w_spec], out_specs=[o_spec] * 3,
        out_shape=[jax.ShapeDtypeStruct((M, F), BF16)] * 3, compiler_params=_params("parallel", "parallel"),
    )(n, w_gate_t, w_up_t)


def _swiglu_bwd(dh, w_down, gate, up, *, name, after=None, tm=1376, tn=512):
    M, D = dh.shape
    F = w_down.shape[0]
    tm, tn = _tile(M, tm, 16), _tile(F, tn, 128)
    n_after = 0 if after is None else 1

    def body(*refs):
        dh_ref, w_ref, g_ref, u_ref, dg_ref, du_ref = refs[n_after:]
        w = w_ref[...]
        for rows in _row_chunks(tm):
            da = _dot(dh_ref[rows, :], w, NT)
            s, ds = _silu_and_grad(g_ref[rows, :].astype(F32))
            dg_ref[rows, :] = (da * u_ref[rows, :].astype(F32) * ds).astype(dg_ref.dtype)
            du_ref[rows, :] = (da * s).astype(du_ref.dtype)

    o_spec = pl.BlockSpec((tm, tn), lambda i, j: (i, j))
    return pl.pallas_call(
        body, name=name, grid=(M // tm, F // tn),
        in_specs=[_ANY] * n_after + [pl.BlockSpec((tm, D), lambda i, j: (i, 0)),
                                     pl.BlockSpec((tn, D), lambda i, j: (j, 0)), o_spec, o_spec],
        out_specs=[o_spec, o_spec], out_shape=[jax.ShapeDtypeStruct((M, F), BF16)] * 2,
        compiler_params=_params("parallel", "parallel"),
    )(*((after,) if n_after else ()), dh, w_down, gate, up)


def _adamw_update(w, g, m, v):
    nm = ADAM_B1 * m + (1.0 - ADAM_B1) * g
    nv = ADAM_B2 * v + (1.0 - ADAM_B2) * (g * g)
    m_hat = nm / (1.0 - ADAM_B1 ** ADAM_STEP)
    v_hat = nv / (1.0 - ADAM_B2 ** ADAM_STEP)
    return -ADAM_LR * (m_hat / (jnp.sqrt(v_hat) + ADAM_EPS) + ADAM_WD * w), nm, nv


def _adamw(w, g, m, v, *, name):
    shape = w.shape
    cols = shape[-1]
    rows = w.size // cols
    w2, g2, m2, v2 = (t.reshape(rows, cols) for t in (w, g, m, v))
    if rows % 8 == 0 or cols % 128 != 0:
        tr, tc = (_tile(rows, 256, 8) if rows % 8 == 0 else rows), cols
    else:
        tr, tc = rows, _tile(cols, 256, 128)

    def body(w_ref, g_ref, m_ref, v_ref, d_ref, nm_ref, nv_ref):
        d_ref[...], nm_ref[...], nv_ref[...] = _adamw_update(w_ref[...], g_ref[...], m_ref[...], v_ref[...])

    blk = pl.BlockSpec((tr, tc), lambda i, j: (i, j))
    outs = pl.pallas_call(
        body, name=name, grid=(rows // tr, cols // tc), in_specs=[blk] * 4, out_specs=[blk] * 3,
        out_shape=[jax.ShapeDtypeStruct((rows, cols), F32)] * 3, compiler_params=_params("parallel", "parallel"),
    )(w2, g2, m2, v2)
    return tuple(t.reshape(shape) for t in outs)


def _sum_slabs(x, *, name):
    _, R, C = x.shape
    sub = 16 if x.dtype == BF16 else 8
    if R % sub == 0:
        tr, tc = _tile(R, 128, sub), C
    else:
        tr, tc = R, _tile(C, 256, 128)

    def body(x_ref, o_ref):
        acc = x_ref[0].astype(F32)
        for s in range(1, N_DEV):
            acc = acc + x_ref[s].astype(F32)
        o_ref[...] = acc

    return pl.pallas_call(
        body, name=name, grid=(R // tr, C // tc),
        in_specs=[pl.BlockSpec((N_DEV, tr, tc), lambda i, j: (0, i, j))],
        out_specs=pl.BlockSpec((tr, tc), lambda i, j: (i, j)),
        out_shape=jax.ShapeDtypeStruct((R, C), F32), compiler_params=_params("parallel", "parallel"),
    )(x)


def _sum_adamw(x, w, m, v, *, name):
    _, R, C = x.shape
    if R % 16 == 0:
        tr, tc = _tile(R, 128, 16), C
    else:
        tr, tc = R, _tile(C, 256, 128)

    def body(x_ref, w_ref, m_ref, v_ref, g_ref, d_ref, nm_ref, nv_ref):
        g = x_ref[0].astype(F32)
        for s in range(1, N_DEV):
            g = g + x_ref[s].astype(F32)
        g_ref[...] = g
        d_ref[...], nm_ref[...], nv_ref[...] = _adamw_update(w_ref[...], g, m_ref[...], v_ref[...])

    blk = pl.BlockSpec((tr, tc), lambda i, j: (i, j))
    return pl.pallas_call(
        body, name=name, grid=(R // tr, C // tc),
        in_specs=[pl.BlockSpec((N_DEV, tr, tc), lambda i, j: (0, i, j)), blk, blk, blk], out_specs=[blk] * 4,
        out_shape=[jax.ShapeDtypeStruct((R, C), F32)] * 4, compiler_params=_params("parallel", "parallel"),
    )(x, w, m, v)


def _peers():
    x, y, c = lax.axis_index("x"), lax.axis_index("y"), lax.axis_index("c")
    me = 4 * x + 2 * y + c
    peers = []
    for k in range(1, N_DEV):
        px = 1 - x if k & 4 else x
        py = 1 - y if k & 2 else y
        pc = 1 - c if k & 1 else c
        peers.append(((px, py, pc), 4 * px + 2 * py + pc))
    return me, peers


_HBM = pl.BlockSpec(memory_space=pltpu.HBM)
_SEM = pl.BlockSpec(memory_space=pltpu.SEMAPHORE)
_EFFECT = pltpu.SideEffectType.DATAFLOW_SIDE_EFFECTING


PLAN_GATHER = tuple((k, "x", 0) for k in range(0, N_DEV))
PLAN_SCATTER = tuple((k, "xk", 0) for k in range(0, N_DEV))
PLAN_GATHER_CHIPS = tuple((k, "x", 0) for k in (1, 2, 4, 6))
PLAN_GATHER_PASS_ON = tuple((1, ("land", q), q) for q in (2, 4, 6))


def _plan_refs(plan, j, x_ref, land_ref, me, peers, receiving):
    k, source, r = plan[j]
    index_of = lambda q: me if q == 0 else peers[q - 1][1]
    pos, target = ((lax.axis_index("x"), lax.axis_index("y"), lax.axis_index("c")), me) if k == 0 else peers[k - 1]
    if source == "x":
        src = x_ref
    elif source == "xk":
        src = x_ref.at[target]
    else:
        src = land_ref.at[index_of(source[1])]
    return pos, src, land_ref.at[index_of(k ^ r) if receiving else index_of(r)]


def _exchange_start(x, *, plan, name, after=None, land=None, slab=None):
    n_after = 0 if after is None else 1
    n = len(plan)

    def body(*refs):
        x_ref, land_ref, send_sems, recv_sems, _, _, token = refs[n_after:]
        me, peers = _peers()
        for j in range(n):
            pos, src, dst = _plan_refs(plan, j, x_ref, land_ref, me, peers, receiving=False)
            pltpu.make_async_remote_copy(src_ref=src, dst_ref=dst, send_sem=send_sems.at[j], recv_sem=recv_sems.at[j],
                                         device_id=pos, device_id_type=pl.DeviceIdType.MESH).start()
        token[...] = jnp.zeros_like(token)

    if land is None:
        land = lax.empty((N_DEV,) + tuple(slab), x.dtype)
    return pl.pallas_call(
        body, name=name,
        out_shape=(pltpu.SemaphoreType.DMA((n,)), pltpu.SemaphoreType.DMA((n,)),
                   pltpu.HBM(x.shape, x.dtype), pltpu.HBM(land.shape, land.dtype), jax.ShapeDtypeStruct((8, 128), F32)),
        in_specs=[_ANY] * n_after + [_HBM, _HBM],
        out_specs=(_SEM, _SEM, _HBM, _HBM, pl.BlockSpec(memory_space=pltpu.VMEM)),
        input_output_aliases={n_after: 2, n_after + 1: 3},
        compiler_params=pltpu.CompilerParams(has_side_effects=_EFFECT),
    )(*((after,) if n_after else ()), pltpu.with_memory_space_constraint(x, pltpu.HBM),
      pltpu.with_memory_space_constraint(land, pltpu.HBM))


def _exchange_start_many(items, *, name):
    n_items = len(items)
    lands = [lax.empty((N_DEV,) + tuple(slab), x.dtype) for x, _, slab in items]

    def body(*refs):
        ins, outs = refs[:2 * n_items], refs[2 * n_items:]
        me, peers = _peers()
        for i, (_, plan, _) in enumerate(items):
            x_ref, land_ref = ins[2 * i], ins[2 * i + 1]
            send_sems, recv_sems = outs[4 * i], outs[4 * i + 1]
            for j in range(len(plan)):
                pos, src, dst = _plan_refs(plan, j, x_ref, land_ref, me, peers, receiving=False)
                pltpu.make_async_remote_copy(src_ref=src, dst_ref=dst, send_sem=send_sems.at[j], recv_sem=recv_sems.at[j],
                                             device_id=pos, device_id_type=pl.DeviceIdType.MESH).start()
        outs[-1][...] = jnp.zeros_like(outs[-1])

    out_shape, out_specs, operands, aliases = [], [], [], {}
    for i, ((x, plan, _), land) in enumerate(zip(items, lands)):
        out_shape += [pltpu.SemaphoreType.DMA((len(plan),)), pltpu.SemaphoreType.DMA((len(plan),)),
                      pltpu.HBM(x.shape, x.dtype), pltpu.HBM(land.shape, land.dtype)]
        out_specs += [_SEM, _SEM, _HBM, _HBM]
        operands += [pltpu.with_memory_space_constraint(x, pltpu.HBM), pltpu.with_memory_space_constraint(land, pltpu.HBM)]
        aliases[2 * i], aliases[2 * i + 1] = 4 * i + 2, 4 * i + 3
    res = pl.pallas_call(
        body, name=name,
        out_shape=tuple(out_shape) + (jax.ShapeDtypeStruct((8, 128), F32),),
        in_specs=[_HBM] * (2 * n_items), out_specs=tuple(out_specs) + (pl.BlockSpec(memory_space=pltpu.VMEM),),
        input_output_aliases=aliases, compiler_params=pltpu.CompilerParams(has_side_effects=_EFFECT),
    )(*operands)
    return [tuple(res[4 * i:4 * i + 4]) + (res[-1],) for i in range(n_items)]


def _exchange_wait(handle, after, *, plan, name):
    send_sems, recv_sems, x_thru, land_thru, _ = handle
    afters = list(after) if isinstance(after, (list, tuple)) else [after]

    def body(x_ref, land_ref, send_sems, recv_sems, *rest):
        me, peers = _peers()
        for j in range(len(plan)):
            pos, src, dst = _plan_refs(plan, j, x_ref, land_ref, me, peers, receiving=True)
            cp = pltpu.make_async_remote_copy(src_ref=src, dst_ref=dst, send_sem=send_sems.at[j], recv_sem=recv_sems.at[j],
                                              device_id=pos, device_id_type=pl.DeviceIdType.MESH)
            cp.wait_send()
            cp.wait_recv()

    return pl.pallas_call(
        body, name=name,
        out_shape=(pltpu.HBM(x_thru.shape, x_thru.dtype), pltpu.HBM(land_thru.shape, land_thru.dtype)),
        in_specs=[_HBM, _HBM, _SEM, _SEM] + [_ANY] * len(afters), out_specs=(_HBM, _HBM),
        input_output_aliases={0: 0, 1: 1}, compiler_params=pltpu.CompilerParams(has_side_effects=_EFFECT),
    )(x_thru, land_thru, send_sems, recv_sems, *afters)


def _wait_and_pass_on(handle, after, *, name):
    send_sems, recv_sems, x_thru, land_thru, _ = handle
    afters = list(after) if isinstance(after, (list, tuple)) else [after]
    first, second = PLAN_GATHER_CHIPS, PLAN_GATHER_PASS_ON

    def body(x_ref, land_ref, send_sems, recv_sems, *rest):
        send_next, recv_next, _, _, token = rest[len(afters):]
        me, peers = _peers()
        arrived = []
        for j in range(len(first)):
            pos, src, dst = _plan_refs(first, j, x_ref, land_ref, me, peers, receiving=True)
            cp = pltpu.make_async_remote_copy(src_ref=src, dst_ref=dst, send_sem=send_sems.at[j], recv_sem=recv_sems.at[j],
                                              device_id=pos, device_id_type=pl.DeviceIdType.MESH)
            cp.wait_recv()
            arrived.append(cp)
            for i in range(len(second)):
                if second[i][2] == first[j][0]:
                    pos, src, dst = _plan_refs(second, i, x_ref, land_ref, me, peers, receiving=False)
                    pltpu.make_async_remote_copy(src_ref=src, dst_ref=dst, send_sem=send_next.at[i], recv_sem=recv_next.at[i],
                                                 device_id=pos, device_id_type=pl.DeviceIdType.MESH).start()
        for cp in arrived:
            cp.wait_send()
        token[...] = jnp.zeros_like(token)

    n = len(second)
    return pl.pallas_call(
        body, name=name,
        out_shape=(pltpu.SemaphoreType.DMA((n,)), pltpu.SemaphoreType.DMA((n,)),
                   pltpu.HBM(x_thru.shape, x_thru.dtype), pltpu.HBM(land_thru.shape, land_thru.dtype),
                   jax.ShapeDtypeStruct((8, 128), F32)),
        in_specs=[_HBM, _HBM, _SEM, _SEM] + [_ANY] * len(afters),
        out_specs=(_SEM, _SEM, _HBM, _HBM, pl.BlockSpec(memory_space=pltpu.VMEM)),
        input_output_aliases={0: 2, 1: 3}, compiler_params=pltpu.CompilerParams(has_side_effects=_EFFECT),
    )(x_thru, land_thru, send_sems, recv_sems, *afters)


def _to_proj_rows(t):
    z = jnp.zeros((D_PROJ - C_SM - 2 * GDN_HEADS - GLA_RANK,) + t.shape[1:], t.dtype)
    return jnp.concatenate([t[R_Z:R_A], t[R_GR:R_LR], t[R_GQ:R_GR], t[:R_Z], t[R_A:R_GQ], t[R_LR:], z], axis=0)


def _from_proj_rows(t):
    ab = C_SM + 2 * GDN_HEADS
    return jnp.concatenate([t[C_QKV:C_SM], t[C_Z:C_GR], t[C_SM:ab], t[C_GQ:C_QKV], t[C_GR:C_GQ],
                            t[ab:ab + GLA_RANK]], axis=0)


def _local_step(x, target, meta, attn_nw, conv_w, a_log, dt_bias, gdn_nw, w2, b2, gla_nw, ffn_nw, final_nw,
                fetch, emit, start=None):
    head = jnp.concatenate([jnp.zeros((ROW_PAD, D_MODEL), F32), meta], axis=0)
    conv_w8 = jnp.concatenate([conv_w, jnp.zeros((8 - CONV_K, conv_w.shape[1]), F32)], axis=0)
    w2p = jnp.zeros((SM_W, GLA_QK), F32).at[2 * GDN_HEADS:2 * GDN_HEADS + GLA_RANK].set(w2)
    alog_p = jnp.zeros((1, SM_W), F32).at[:, :GDN_HEADS].set(a_log)
    dt_p = jnp.zeros((1, SM_W), F32).at[:, :GDN_HEADS].set(dt_bias)

    h0, n1 = _embed_norm(head, x, attn_nw, name="attn_norm", after=start)
    w_in_t = fetch("w_in_t", (n1, conv_w8, w2p, alog_p, dt_p))
    proj, sm = _in_proj(n1, w_in_t, name="in_proj")
    gb, la = _gates_fwd(sm, w2p, b2, alog_p, dt_p, name="gates")
    act = _prep_fwd(proj, conv_w8, name="gdn_prep")
    o_gdn, s_gdn, t_gdn = _gdn_fwd(act, gb, name="gdn_fwd")
    o_gla, s_gla = _gla_fwd(proj, la, name="gla_fwd")
    mixed = _mix_fwd(o_gdn, o_gla, proj, gdn_nw, gla_nw, name="mix")
    w_out = fetch("w_out", mixed)
    h1 = _matmul(mixed, w_out, mode="nn", add=h0, name="out_proj")
    n2 = _rmsnorm_fwd(h1, ffn_nw, name="ffn_norm")
    w_gate_t, w_up_t = fetch("w_gate_t", n2), fetch("w_up_t", n2)
    gate, up, hid = _swiglu_fwd(n2, w_gate_t, w_up_t, name="swiglu")
    w_down = fetch("w_down", hid)
    h2 = _matmul(hid, w_down, mode="nn", add=h1, name="ffn_down", tm=1376, tn=256)
    dh2, dh2_b, d_final_nw, loss = _loss_head(h2, final_nw, target, name="loss_head")

    wg = dict(mode="tn", out_dtype=BF16, tn=512)
    tok = emit("w_down", _matmul(hid, dh2_b, name="d_w_down", tm=704, **wg))
    d_gate, d_up = _swiglu_bwd(dh2_b, w_down, gate, up, name="d_swiglu", after=tok)
    tok = emit("w_gate_t", _matmul(d_gate, n2, name="d_w_gate", tm=704, **wg))
    tok = emit("w_up_t", _matmul(d_up, n2, name="d_w_up", tm=704, after=tok, **wg))
    d_n2 = _matmul_pair(d_gate, w_gate_t, d_up, w_up_t, name="d_n2", after=tok)
    dh1, dh1_b, d_ffn_nw = _rmsnorm_bwd(h1, ffn_nw, d_n2, dh2, name="d_ffn_norm")

    tok = emit("w_out", _matmul(mixed, dh1_b, name="d_w_out", tm=512, **wg))
    d_mixed = _matmul(dh1_b, w_out, mode="nt", name="d_mixed", after=tok)
    do_gdn, do_gla, d_proj, d_gdn_nw, d_gla_nw = _mix_bwd(o_gdn, o_gla, proj, gdn_nw, gla_nw, d_mixed, name="d_mix")
    d_proj, d_la = _gla_bwd(proj, la, do_gla, s_gla, d_proj, name="gla_bwd")
    dact, dgb_heads = _gdn_bwd(act, gb, do_gdn, s_gdn, t_gdn, name="gdn_bwd")
    d_proj, d_w2p, d_b2, d_alog, d_dt = _gates_bwd(sm, w2p, b2, alog_p, dt_p, dgb_heads, d_la, d_proj, name="d_gates")
    d_proj, d_conv_w8 = _prep_bwd(proj, conv_w8, dact, d_proj, name="d_gdn_prep")
    tok = emit("w_in_t", _matmul(d_proj, n1, name="d_w_in", tm=768, **wg))
    d_n1 = _matmul(d_proj, w_in_t, mode="nn", name="d_n1", tm=688, after=tok)
    grad_x, d_head, d_attn_nw = _embed_norm_bwd(h0, attn_nw, d_n1, dh1, name="d_attn_norm")

    return dict(
        loss=loss[0, 0], grad_x=grad_x, meta=d_head[ROW_PAD:HEAD_ROWS], attn_nw=d_attn_nw,
        conv_w=d_conv_w8[:CONV_K], a_log=d_alog[:, :GDN_HEADS], dt_bias=d_dt[:, :GDN_HEADS], gdn_nw=d_gdn_nw,
        w2=d_w2p[2 * GDN_HEADS:2 * GDN_HEADS + GLA_RANK], b2=d_b2, gla_nw=d_gla_nw, ffn_nw=d_ffn_nw,
        final_nw=d_final_nw)


SMALL_ROWS = 32


def kernel(x, meta_tokens, attn_norm_w, w_in, gdn_conv_w, gdn_a_log, gdn_dt_bias, gdn_norm_w, gla_gate_w2, gla_gate_b, gla_norm_w, w_out, ffn_norm_w, w_gate, w_up, w_down, final_norm_w, loss_target, m_meta_tokens, m_attn_norm_w, m_w_in, m_gdn_conv_w, m_gdn_a_log, m_gdn_dt_bias, m_gdn_norm_w, m_gla_gate_w2, m_gla_gate_b, m_gla_norm_w, m_w_out, m_ffn_norm_w, m_w_gate, m_w_up, m_w_down, m_final_norm_w, v_meta_tokens, v_attn_norm_w, v_w_in, v_gdn_conv_w, v_gdn_a_log, v_gdn_dt_bias, v_gdn_norm_w, v_gla_gate_w2, v_gla_gate_b, v_gla_norm_w, v_w_out, v_ffn_norm_w, v_w_gate, v_w_up, v_w_down, v_final_norm_w):
    me = 4 * lax.axis_index("x") + 2 * lax.axis_index("y") + lax.axis_index("c")

    n_conv = gdn_conv_w.shape[2]
    n_w2 = gla_gate_w2.shape[2]
    n_meta = meta_tokens.shape[1]
    small = jnp.zeros((40, n_conv), F32)
    small = small.at[0:N_META, :n_meta].set(meta_tokens)
    small = small.at[N_META:N_META + CONV_K, :].set(gdn_conv_w[0])
    small = small.at[24:24 + GLA_RANK, :n_w2].set(gla_gate_w2[0])

    w_in_slab = w_in[0].T.astype(BF16)
    items = [(small, PLAN_GATHER, small.shape), (w_in_slab, PLAN_GATHER_CHIPS, w_in_slab.shape)]
    wnames = ("w_out", "w_gate_t", "w_up_t", "w_down")
    for slab in (w_out[0], w_gate[0].T, w_up[0].T, w_down[0]):
        items.append((slab.astype(BF16), PLAN_GATHER, slab.shape))
    started = _exchange_start_many(items, name="gather_weights_start")
    small_h, in_h = started[0], started[1]
    handles = dict(zip(wnames, started[2:]))
    tok = small_h[4]

    own, small_all = _exchange_wait(small_h, tok, plan=PLAN_GATHER, name="gather_small_wait")
    meta_f = small_all[:, 0:N_META, :n_meta].transpose(1, 0, 2).reshape(N_META, D_MODEL)
    conv_f = small_all[:, N_META:N_META + CONV_K, :].transpose(1, 0, 2).reshape(CONV_K, N_DEV * n_conv)
    w2_f = small_all[:, 24:24 + GLA_RANK, :n_w2].transpose(1, 0, 2).reshape(GLA_RANK, N_DEV * n_w2)

    def fetch(name, after):
        if name == "w_in_t":
            pass_h = _wait_and_pass_on(in_h, after, name="gather_w_in_wait_pass_on")
            own, got = _exchange_wait(pass_h, pass_h[4], plan=PLAN_GATHER_PASS_ON, name="pass_w_in_wait")
            got = lax.dynamic_update_index_in_dim(got, own, me, 0)
            return _to_proj_rows(got.reshape(D_IN, D_MODEL))
        own, got = _exchange_wait(handles[name], after, plan=PLAN_GATHER, name="gather_" + name + "_wait")
        return got.reshape(N_DEV * got.shape[1], D_MODEL)

    sent = {}

    def emit(name, grad):
        if name == "w_in_t":
            grad = _from_proj_rows(grad)
        parts = grad.reshape(N_DEV, grad.shape[0] // N_DEV, D_MODEL)
        sent[name] = _exchange_start(parts, plan=PLAN_SCATTER, slab=parts.shape[1:], name="scatter_" + name + "_start")
        return sent[name][4]

    g = _local_step(x[0], loss_target[0], meta_f, attn_norm_w, conv_f, gdn_a_log, gdn_dt_bias, gdn_norm_w, w2_f,
                    gla_gate_b, gla_norm_w, ffn_norm_w, final_norm_w.reshape(1, D_MODEL), fetch, emit, start=tok)

    misc = jnp.concatenate([g["a_log"], g["dt_bias"], g["gdn_nw"], g["gla_nw"], g["b2"], g["loss"].reshape(1, 1)], axis=1)
    n_misc = misc.shape[1]
    misc = jnp.pad(misc, ((0, 0), (0, D_MODEL - n_misc)))
    rows = jnp.concatenate([g["attn_nw"], g["ffn_nw"], g["final_nw"], misc, g["meta"],
                            g["conv_w"].reshape(-1, D_MODEL), g["w2"].reshape(-1, D_MODEL)], axis=0)
    rows = jnp.pad(rows, ((0, SMALL_ROWS - rows.shape[0]), (0, 0)))
    rows_h = _exchange_start(rows, plan=PLAN_GATHER, slab=rows.shape, name="gather_small_grads_start")

    big = {}
    after = rows_h[4]
    for name, w, m, v, transposed in (("w_down", w_down, m_w_down, v_w_down, False), ("w_gate_t", w_gate, m_w_gate, v_w_gate, True),
                                      ("w_up_t", w_up, m_w_up, v_w_up, True), ("w_out", w_out, m_w_out, v_w_out, False),
                                      ("w_in_t", w_in, m_w_in, v_w_in, True)):
        own, got = _exchange_wait(sent[name], after, plan=PLAN_SCATTER, name="scatter_" + name + "_wait")
        local = [t[0].T if transposed else t[0] for t in (w, m, v)]
        res = _sum_adamw(got, *local, name="adamw_" + name)
        big[name] = [t.T[None] if transposed else t[None] for t in res]
        after = res[0]

    own, got = _exchange_wait(rows_h, after, plan=PLAN_GATHER, name="gather_small_grads_wait")
    tot = _sum_slabs(got, name="sum_small_grads")
    grad_attn_nw, grad_ffn_nw, grad_final_nw = tot[0:1], tot[1:2], tot[2]
    grad_a_log = tot[3:4, 0:8]
    grad_dt = tot[3:4, 8:16]
    grad_gdn_nw = tot[3:4, 16:16 + GDN_DV]
    grad_gla_nw = tot[3:4, 144:144 + GLA_DV]
    grad_b2 = tot[3:4, 400:400 + GLA_QK]
    loss = tot[3, n_misc - 1]
    r0 = 4 + N_META
    grad_meta = lax.dynamic_slice(tot[4:r0], (0, me * n_meta), (N_META, n_meta))
    r1 = r0 + CONV_K * N_DEV * n_conv // D_MODEL
    grad_conv = lax.dynamic_slice(tot[r0:r1].reshape(CONV_K, N_DEV * n_conv), (0, me * n_conv), (CONV_K, n_conv))[None]
    r2 = r1 + GLA_RANK * N_DEV * n_w2 // D_MODEL
    grad_w2 = lax.dynamic_slice(tot[r1:r2].reshape(GLA_RANK, N_DEV * n_w2), (0, me * n_w2), (GLA_RANK, n_w2))[None]

    weights = [meta_tokens, attn_norm_w, w_in, gdn_conv_w, gdn_a_log, gdn_dt_bias, gdn_norm_w, gla_gate_w2,
               gla_gate_b, gla_norm_w, w_out, ffn_norm_w, w_gate, w_up, w_down, final_norm_w]
    grads = [grad_meta, grad_attn_nw, "w_in_t", grad_conv, grad_a_log, grad_dt, grad_gdn_nw, grad_w2,
             grad_b2, grad_gla_nw, "w_out", grad_ffn_nw, "w_gate_t", "w_up_t", "w_down", grad_final_nw]
    ms = [m_meta_tokens, m_attn_norm_w, m_w_in, m_gdn_conv_w, m_gdn_a_log, m_gdn_dt_bias, m_gdn_norm_w,
          m_gla_gate_w2, m_gla_gate_b, m_gla_norm_w, m_w_out, m_ffn_norm_w, m_w_gate, m_w_up, m_w_down, m_final_norm_w]
    vs = [v_meta_tokens, v_attn_norm_w, v_w_in, v_gdn_conv_w, v_gdn_a_log, v_gdn_dt_bias, v_gdn_norm_w,
          v_gla_gate_w2, v_gla_gate_b, v_gla_norm_w, v_w_out, v_ffn_norm_w, v_w_gate, v_w_up, v_w_down, v_final_norm_w]
    outs = [[], [], [], []]
    for idx, (w, gr, m, v) in enumerate(zip(weights, grads, ms, vs)):
        if isinstance(gr, str):
            res = big[gr]
        else:
            gr = gr.reshape(w.shape)
            res = (gr,) + _adamw(w, gr, m, v, name=f"adamw_{idx}")
        for lst, t in zip(outs, res):
            lst.append(t)
    return (loss, g["grad_x"][None], *outs[0], *outs[1], *outs[2], *outs[3])
```
